```python
import jax, jax.numpy as jnp
from jax import lax
import numpy as np

D_MODEL = 1024
BATCH = 8
SEQ = 2048
DEPTH = 1

HG_HEADS = 8
HG_KEY = 128
HG_VAL = D_MODEL // HG_HEADS
HG_FDIM = HG_HEADS * HG_KEY
HG_VDIM = HG_HEADS * HG_VAL
HG_CHUNK = 64
ATT_GROUPS = ((128, 1), (512, 4), (2048, 16))
N_GROUPS = 3
ATT_HEADS = 8
ATT_HEAD_DIM = 64
ATT_DIM = ATT_HEADS * ATT_HEAD_DIM
ROPE_THETA = 10000.0
N_BRANCH = 2
NORM_EPS = 1e-6
IN_SIZES = (HG_FDIM, HG_FDIM, HG_VDIM, HG_VDIM, 3 * N_GROUPS * ATT_DIM, ATT_DIM, N_BRANCH * D_MODEL)
IN_COLS = HG_FDIM * 2 + HG_VDIM * 2 + 3 * N_GROUPS * ATT_DIM + ATT_DIM + N_BRANCH * D_MODEL
IN_SPLITS = (HG_FDIM, 2 * HG_FDIM, 2 * HG_FDIM + HG_VDIM, 2 * HG_FDIM + 2 * HG_VDIM,
             2 * HG_FDIM + 2 * HG_VDIM + 3 * N_GROUPS * ATT_DIM,
             2 * HG_FDIM + 2 * HG_VDIM + 3 * N_GROUPS * ATT_DIM + ATT_DIM)

kernel_name = "hybrid_hgrn2_dilated_attn_gated_merge"


def rmsnorm(x, w):
    xf = x.astype(jnp.float32)
    y = xf * lax.rsqrt(jnp.mean(xf * xf, axis=-1, keepdims=True) + NORM_EPS)
    return (y * w.astype(jnp.float32)).astype(x.dtype)


def rope_tables(positions, dim):
    inv_freq = ROPE_THETA ** (-jnp.arange(0, dim, 2, dtype=jnp.float32) / dim)
    ang = positions.astype(jnp.float32)[..., None] * inv_freq
    return jnp.cos(ang), jnp.sin(ang)


def apply_rope(t, cos, sin):
    c = cos[:, :, None, None, :]
    s = sin[:, :, None, None, :]
    t1, t2 = jnp.split(t, 2, axis=-1)
    return jnp.concatenate([t1 * c - t2 * s, t2 * c + t1 * s], axis=-1)


def hgrn2_chunked(q, k, logf, v):
    B, S, H, K = q.shape
    V = v.shape[-1]
    C = HG_CHUNK
    nC = S // C

    def chunks(t):
        return t.reshape(B, nC, C, H, t.shape[-1]).transpose(1, 0, 3, 2, 4)

    causal = jnp.tril(jnp.ones((C, C), dtype=bool))

    def step(state, xs):
        qc, kc, gc, vc = xs
        b = jnp.cumsum(gc, axis=2)
        diff = b[:, :, :, None, :] - b[:, :, None, :, :]
        decay = jnp.exp(jnp.where(causal[:, :, None], diff, -jnp.inf))
        scores = jnp.einsum('bhtk,bhtsk,bhsk->bhts', qc, decay, kc)
        o = (jnp.einsum('bhts,bhsv->bhtv', scores, vc)
             + jnp.einsum('bhtk,bhkv->bhtv', qc * jnp.exp(b), state))
        b_last = b[:, :, -1:, :]
        state = (jnp.exp(b_last)[:, :, 0, :, None] * state
                 + jnp.einsum('bhsk,bhsv->bhkv', kc * jnp.exp(b_last - b), vc))
        return state, o

    s0 = jnp.zeros((B, H, K, V), jnp.float32)
    _, o = lax.scan(step, s0, (chunks(q), chunks(k), chunks(logf), chunks(v)))
    return o.transpose(1, 0, 3, 2, 4).reshape(B, S, H, V)


def dilated_window_attention(q, k, v, window, dilation):
    B, S, H, E = q.shape
    n = window // dilation
    M = S // dilation
    nblk = -(-M // n)
    Mp = nblk * n

    def to_residue(t):
        t = t.reshape(B, M, dilation, H, E).transpose(0, 2, 3, 1, 4)
        return jnp.pad(t, ((0, 0), (0, 0), (0, 0), (0, Mp - M), (0, 0)))

    def key_blocks(t):
        t = jnp.pad(t, ((0, 0), (0, 0), (0, 0), (n, 0), (0, 0))).reshape(B, dilation, H, nblk + 1, n, E)
        return jnp.concatenate([t[:, :, :, :-1], t[:, :, :, 1:]], axis=-2)

    qb = to_residue(q).reshape(B, dilation, H, nblk, n, E)
    kb = key_blocks(to_residue(k))
    vb = key_blocks(to_residue(v))
    s = jnp.einsum('brhcqe,brhcke->brhcqk', qb, kb) * (E ** -0.5)
    i = jnp.arange(n)[:, None]
    j = jnp.arange(2 * n)[None, :]
    dist = i + n - j
    c = jnp.arange(nblk)[:, None, None]
    valid = (dist >= 0) & (dist <= n) & (c * n - n + j >= 0)
    s = jnp.where(valid, s, -jnp.inf)
    m = jnp.max(s, axis=-1)
    p = jnp.exp(s - m[..., None])
    l = jnp.sum(p, axis=-1)
    o = jnp.einsum('brhcqk,brhcke->brhcqe', p, vb)

    def back(t):
        rest = t.shape[5:]
        t = t.reshape(B, dilation, H, Mp, *rest)[:, :, :, :M]
        t = jnp.moveaxis(t, 3, 1)
        return t.reshape(B, S, H, *rest)

    return back(o), back(m), back(l)


def hybrid_layer(x, cos, sin, norm_w, w_in, lb, hgrn_norm_w, w_branch_a, w_branch_b, w_out):
    B, S, _ = x.shape
    h = rmsnorm(x, norm_w)
    z = h @ w_in
    hq, hf, hi, hg, aqkv, ag, gates = jnp.split(z, IN_SPLITS, axis=-1)

    f = lb + (1.0 - lb) * jax.nn.sigmoid(hf.astype(jnp.float32))
    shp = (B, S, HG_HEADS, HG_KEY)
    q_a = jax.nn.silu(hq.astype(jnp.float32)).reshape(shp)
    o_a = hgrn2_chunked(q_a, (1.0 - f).reshape(shp), jnp.log(f).reshape(shp),
                        hi.astype(jnp.float32).reshape(B, S, HG_HEADS, HG_VAL))
    o_a = rmsnorm(o_a, hgrn_norm_w) * jax.nn.silu(hg.astype(jnp.float32)).reshape(B, S, HG_HEADS, HG_VAL)
    y_a = o_a.reshape(B, S, HG_VDIM).astype(x.dtype) @ w_branch_a

    aqkv = aqkv.astype(jnp.float32).reshape(B, S, 3, N_GROUPS, ATT_HEADS, ATT_HEAD_DIM)
    q_b = apply_rope(aqkv[:, :, 0], cos, sin)
    k_b = apply_rope(aqkv[:, :, 1], cos, sin)
    v_b = aqkv[:, :, 2]
    outs, maxes, dens = [], [], []
    for g, (window, dilation) in enumerate(ATT_GROUPS):
        o_g, m_g, l_g = dilated_window_attention(q_b[:, :, g], k_b[:, :, g], v_b[:, :, g], window, dilation)
        outs.append(o_g)
        maxes.append(m_g)
        dens.append(l_g)
    ms = jnp.stack(maxes)
    wts = jnp.exp(ms - jnp.max(ms, axis=0))
    den = jnp.sum(wts * jnp.stack(dens), axis=0)
    num = jnp.sum(wts[..., None] * jnp.stack(outs), axis=0)
    o_b = (num / den[..., None]).reshape(B, S, ATT_DIM) * jax.nn.silu(ag.astype(jnp.float32))
    y_b = o_b.astype(x.dtype) @ w_branch_b

    g_a, g_b = jnp.split(jax.nn.sigmoid(gates), N_BRANCH, axis=-1)
    merged = g_a * y_a + g_b * y_b
    return x + merged @ w_out


def _fwd_setup_inputs(seed: int = 0) -> dict:
    key = jax.random.key(seed)
    ks = jax.random.split(key, 12)
    x = jax.random.normal(ks[0], (BATCH, SEQ, D_MODEL), jnp.float32)
    offsets = jax.random.randint(ks[1], (BATCH, 1), 0, 4096, dtype=jnp.int32)
    positions = offsets + jnp.arange(SEQ, dtype=jnp.int32)[None, :]
    norm_w = 1.0 + 0.02 * jax.random.normal(ks[2], (DEPTH, D_MODEL), jnp.float32)
    w_in = jax.random.normal(ks[3], (DEPTH, D_MODEL, IN_COLS), jnp.float32) * D_MODEL ** -0.5
    lb_logits = 0.5 * jax.random.normal(ks[4], (DEPTH + 1, HG_FDIM), jnp.float32)
    hgrn_norm_w = 1.0 + 0.02 * jax.random.normal(ks[5], (DEPTH, HG_VAL), jnp.float32)
    w_branch_a = jax.random.normal(ks[6], (DEPTH, HG_VDIM, D_MODEL), jnp.float32) * HG_VDIM ** -0.5
    w_branch_b = jax.random.normal(ks[7], (DEPTH, ATT_DIM, D_MODEL), jnp.float32) * ATT_DIM ** -0.5
    w_out = jax.random.normal(ks[8], (DEPTH, D_MODEL, D_MODEL), jnp.float32) * D_MODEL ** -0.5
    final_norm_w = 1.0 + 0.02 * jax.random.normal(ks[9], (D_MODEL,), jnp.float32)
    return {"x": x, "positions": positions, "norm_w": norm_w, "w_in": w_in, "lb_logits": lb_logits,
            "hgrn_norm_w": hgrn_norm_w, "w_branch_a": w_branch_a, "w_branch_b": w_branch_b,
            "w_out": w_out, "final_norm_w": final_norm_w}


def _fwd_reference(x, positions, norm_w, w_in, lb_logits, hgrn_norm_w, w_branch_a, w_branch_b, w_out, final_norm_w):
    cos, sin = rope_tables(positions, ATT_HEAD_DIM)
    lower_bounds = jnp.cumsum(jax.nn.softmax(lb_logits.astype(jnp.float32), axis=0), axis=0)
    for layer in range(DEPTH):
        x = hybrid_layer(x, cos, sin, norm_w[layer], w_in[layer], lower_bounds[layer], hgrn_norm_w[layer],
                         w_branch_a[layer], w_branch_b[layer], w_out[layer])
    return rmsnorm(x, final_norm_w)


import jax as _jax
import jax.numpy as _jnp

TWIN_FORMAT = 'train_step'
FWD_PARAMS = ['x', 'positions', 'norm_w', 'w_in', 'lb_logits', 'hgrn_norm_w', 'w_branch_a', 'w_branch_b', 'w_out', 'final_norm_w']
TWIN_WEIGHTS = ['norm_w', 'w_in', 'lb_logits', 'hgrn_norm_w', 'w_branch_a', 'w_branch_b', 'w_out', 'final_norm_w']
TWIN_DIFF_INPUT = 'x'
TWIN_INPUTS = ['x', 'positions', 'norm_w', 'w_in', 'lb_logits', 'hgrn_norm_w', 'w_branch_a', 'w_branch_b', 'w_out', 'final_norm_w', 'loss_target', 'm_norm_w', 'm_w_in', 'm_lb_logits', 'm_hgrn_norm_w', 'm_w_branch_a', 'm_w_branch_b', 'm_w_out', 'm_final_norm_w', 'v_norm_w', 'v_w_in', 'v_lb_logits', 'v_hgrn_norm_w', 'v_w_branch_a', 'v_w_branch_b', 'v_w_out', 'v_final_norm_w']
TWIN_OUTPUTS = ['loss', 'grad_x', 'grad_norm_w', 'grad_w_in', 'grad_lb_logits', 'grad_hgrn_norm_w', 'grad_w_branch_a', 'grad_w_branch_b', 'grad_w_out', 'grad_final_norm_w', 'delta_norm_w', 'delta_w_in', 'delta_lb_logits', 'delta_hgrn_norm_w', 'delta_w_branch_a', 'delta_w_branch_b', 'delta_w_out', 'delta_final_norm_w', 'new_m_norm_w', 'new_m_w_in', 'new_m_lb_logits', 'new_m_hgrn_norm_w', 'new_m_w_branch_a', 'new_m_w_branch_b', 'new_m_w_out', 'new_m_final_norm_w', 'new_v_norm_w', 'new_v_w_in', 'new_v_lb_logits', 'new_v_hgrn_norm_w', 'new_v_w_branch_a', 'new_v_w_branch_b', 'new_v_w_out', 'new_v_final_norm_w']
TWIN_LEAF_KINDS = {'loss': 'loss', 'grad_x': 'grad_x', 'grad_norm_w': 'grad_w', 'grad_w_in': 'grad_w', 'grad_lb_logits': 'grad_w', 'grad_hgrn_norm_w': 'grad_w', 'grad_w_branch_a': 'grad_w', 'grad_w_branch_b': 'grad_w', 'grad_w_out': 'grad_w', 'grad_final_norm_w': 'grad_w', 'delta_norm_w': 'delta_w', 'delta_w_in': 'delta_w', 'delta_lb_logits': 'delta_w', 'delta_hgrn_norm_w': 'delta_w', 'delta_w_branch_a': 'delta_w', 'delta_w_branch_b': 'delta_w', 'delta_w_out': 'delta_w', 'delta_final_norm_w': 'delta_w', 'new_m_norm_w': 'new_m', 'new_m_w_in': 'new_m', 'new_m_lb_logits': 'new_m', 'new_m_hgrn_norm_w': 'new_m', 'new_m_w_branch_a': 'new_m', 'new_m_w_branch_b': 'new_m', 'new_m_w_out': 'new_m', 'new_m_final_norm_w': 'new_m', 'new_v_norm_w': 'new_v', 'new_v_w_in': 'new_v', 'new_v_lb_logits': 'new_v', 'new_v_hgrn_norm_w': 'new_v', 'new_v_w_branch_a': 'new_v', 'new_v_w_branch_b': 'new_v', 'new_v_w_out': 'new_v', 'new_v_final_norm_w': 'new_v'}


def _forward(args):
    return _fwd_reference(*[args[k] for k in FWD_PARAMS])


def _output_shape():
    out = _jax.eval_shape(lambda: _forward(_fwd_setup_inputs(0)))
    return out.shape, out.dtype

N_MICROBATCH = 1
ADAM_LR = 0.001
ADAM_B1 = 0.9
ADAM_B2 = 0.999
ADAM_EPS = 1e-08
ADAM_WD = 0.01
ADAM_STEP = 10
PER_EXAMPLE_BATCH_AXIS = {'x': 0, 'positions': 0, 'loss_target': 0}
SHARED_INPUTS = []
_WEIGHT_DTYPES = {'norm_w': _jnp.float32, 'w_in': _jnp.float32, 'lb_logits': _jnp.float32, 'hgrn_norm_w': _jnp.float32, 'w_branch_a': _jnp.float32, 'w_branch_b': _jnp.float32, 'w_out': _jnp.float32, 'final_norm_w': _jnp.float32}
MOMENT_SCALE = {'norm_w': 5.835805e-02, 'w_in': 1.785727e-02, 'lb_logits': 3.329464e-03, 'hgrn_norm_w': 1.044816e-01, 'w_branch_a': 3.848277e-02, 'w_branch_b': 7.154914e-03, 'w_out': 3.912642e-02, 'final_norm_w': 1.599607e+01}


def _to_microbatches(a, axis):
    t = _jnp.moveaxis(a, axis, 0)
    t = t.reshape((N_MICROBATCH, t.shape[0] // N_MICROBATCH) + t.shape[1:])
    return _jnp.moveaxis(t, 1, axis + 1)


def setup_inputs(seed: int = 0) -> dict:
    inp = _fwd_setup_inputs(seed)
    key = _jax.random.fold_in(_jax.random.key(seed), 7919)
    shape, _ = _output_shape()
    out = dict(inp)
    out["loss_target"] = _jax.random.normal(_jax.random.fold_in(key, 0), shape, _jnp.float32)
    for i, name in enumerate(TWIN_WEIGHTS):
        w = inp[name].astype(_jnp.float32)
        if MOMENT_SCALE is None:
            s = _jnp.sqrt(_jnp.mean(_jnp.square(w)) + 1e-30)
        else:
            s = MOMENT_SCALE[name]
        km, kv = _jax.random.split(_jax.random.fold_in(key, i + 1))
        out[name] = w
        out["m_" + name] = s * _jax.random.normal(km, w.shape, _jnp.float32)
        out["v_" + name] = (s * s) * _jax.random.uniform(kv, w.shape, _jnp.float32, 0.5, 1.5)
    if N_MICROBATCH > 1:
        for name, axis in PER_EXAMPLE_BATCH_AXIS.items():
            out[name] = _to_microbatches(out[name], axis)
    return {'x': out['x'], 'positions': out['positions'], 'norm_w': out['norm_w'], 'w_in': out['w_in'], 'lb_logits': out['lb_logits'], 'hgrn_norm_w': out['hgrn_norm_w'], 'w_branch_a': out['w_branch_a'], 'w_branch_b': out['w_branch_b'], 'w_out': out['w_out'], 'final_norm_w': out['final_norm_w'], 'loss_target': out['loss_target'], 'm_norm_w': out['m_norm_w'], 'm_w_in': out['m_w_in'], 'm_lb_logits': out['m_lb_logits'], 'm_hgrn_norm_w': out['m_hgrn_norm_w'], 'm_w_branch_a': out['m_w_branch_a'], 'm_w_branch_b': out['m_w_branch_b'], 'm_w_out': out['m_w_out'], 'm_final_norm_w': out['m_final_norm_w'], 'v_norm_w': out['v_norm_w'], 'v_w_in': out['v_w_in'], 'v_lb_logits': out['v_lb_logits'], 'v_hgrn_norm_w': out['v_hgrn_norm_w'], 'v_w_branch_a': out['v_w_branch_a'], 'v_w_branch_b': out['v_w_branch_b'], 'v_w_out': out['v_w_out'], 'v_final_norm_w': out['v_final_norm_w']}


def _loss(weights, diff, rest, loss_target):
    with _jax.named_scope("forward"):
        args = {**rest, TWIN_DIFF_INPUT: diff, **{k: w.astype(_WEIGHT_DTYPES[k]) for k, w in weights.items()}}
        y = _forward(args)
    with _jax.named_scope("loss_head"):
        err = _jnp.square(y.astype(_jnp.float32) - loss_target)
        return 0.5 * _jnp.sum(_jnp.mean(err, axis=-1)) if err.ndim else 0.5 * err


def _adamw(w, g, m, v):
    m = ADAM_B1 * m + (1.0 - ADAM_B1) * g
    v = ADAM_B2 * v + (1.0 - ADAM_B2) * _jnp.square(g)
    m_hat = m / (1.0 - ADAM_B1 ** ADAM_STEP)
    v_hat = v / (1.0 - ADAM_B2 ** ADAM_STEP)
    delta = -ADAM_LR * (m_hat / (_jnp.sqrt(v_hat) + ADAM_EPS) + ADAM_WD * w)
    return delta, m, v


def reference(x, positions, norm_w, w_in, lb_logits, hgrn_norm_w, w_branch_a, w_branch_b, w_out, final_norm_w, loss_target, m_norm_w, m_w_in, m_lb_logits, m_hgrn_norm_w, m_w_branch_a, m_w_branch_b, m_w_out, m_final_norm_w, v_norm_w, v_w_in, v_lb_logits, v_hgrn_norm_w, v_w_branch_a, v_w_branch_b, v_w_out, v_final_norm_w):
    given = dict(x=x, positions=positions, norm_w=norm_w, w_in=w_in, lb_logits=lb_logits, hgrn_norm_w=hgrn_norm_w, w_branch_a=w_branch_a, w_branch_b=w_branch_b, w_out=w_out, final_norm_w=final_norm_w, loss_target=loss_target, m_norm_w=m_norm_w, m_w_in=m_w_in, m_lb_logits=m_lb_logits, m_hgrn_norm_w=m_hgrn_norm_w, m_w_branch_a=m_w_branch_a, m_w_branch_b=m_w_branch_b, m_w_out=m_w_out, m_final_norm_w=m_final_norm_w, v_norm_w=v_norm_w, v_w_in=v_w_in, v_lb_logits=v_lb_logits, v_hgrn_norm_w=v_hgrn_norm_w, v_w_branch_a=v_w_branch_a, v_w_branch_b=v_w_branch_b, v_w_out=v_w_out, v_final_norm_w=v_final_norm_w)
    weights = {n: given[n] for n in TWIN_WEIGHTS}
    shared = {n: given[n] for n in SHARED_INPUTS}
    per_example = {n: given[n] for n in ['x', 'positions']}
    grad_fn = _jax.value_and_grad(_loss, argnums=(0, 1))

    def one_microbatch(ex, loss_target):
        ex = dict(ex)
        diff = ex.pop(TWIN_DIFF_INPUT)
        return grad_fn(weights, diff, {**shared, **ex}, loss_target)

    if N_MICROBATCH == 1:
        loss, (grad_w, grad_x) = one_microbatch(per_example, given["loss_target"])
    else:
        def body(carry, xs):
            loss_sum, grad_sum = carry
            l_k, (gw_k, gx_k) = one_microbatch(xs[0], xs[1])
            with _jax.named_scope("update"):
                return (loss_sum + l_k, _jax.tree.map(_jnp.add, grad_sum, gw_k)), gx_k

        init = (_jnp.zeros((), _jnp.float32), _jax.tree.map(_jnp.zeros_like, weights))
        (loss, grad_w), grad_x = _jax.lax.scan(body, init, (per_example, given["loss_target"]))
    with _jax.named_scope("update"):
        delta_w, new_m, new_v = {}, {}, {}
        for n in TWIN_WEIGHTS:
            delta_w[n], new_m[n], new_v[n] = _adamw(weights[n], grad_w[n], given["m_" + n], given["v_" + n])
    return (loss, grad_x, *[grad_w[n] for n in TWIN_WEIGHTS], *[delta_w[n] for n in TWIN_WEIGHTS],
            *[new_m[n] for n in TWIN_WEIGHTS], *[new_v[n] for n in TWIN_WEIGHTS])
```

```python
import functools

import jax
import jax.numpy as jnp
from jax import lax
from jax.experimental import pallas as pl
from jax.experimental.pallas import tpu as pltpu

T = 2048
D = 1024
NIN = 11264
HEADS = 8
HK = 128
CH = 16
NCH = T // CH
ATT_GROUPS = ((128, 1), (512, 4), (2048, 16))
ATT_COL0 = 4096
AG_COL0 = 8704
GATE_COL0 = 9216
EPS = 1e-6
ROPE_THETA = 10000.0
LR, B1, B2, ADAM_EPS, WD, STEP = 0.001, 0.9, 0.999, 1e-08, 0.01, 10

F32 = jnp.float32
BF = jnp.bfloat16
VMEM_LIMIT = 56 * 1024 * 1024

_NN = (((1,), (0,)), ((), ()))
_NT = (((1,), (1,)), ((), ()))
_TN = (((0,), (0,)), ((), ()))


def _dot(a, b, dims=_NN):
    return lax.dot_general(a, b, dims, preferred_element_type=F32)


def _bdot(a, b, dims=_NN):
    return lax.dot_general(a.astype(BF), b.astype(BF), dims, preferred_element_type=F32)


def _sigmoid(x):
    return jax.nn.sigmoid(x)


def _params(sem=None):
    return pltpu.CompilerParams(dimension_semantics=sem, vmem_limit_bytes=VMEM_LIMIT)


def _matmul(a, b, *, ta=False, tb=False, out_dtype=F32, tm=512, tn=512, tk=None, name):
    m = a.shape[1] if ta else a.shape[0]
    kdim = a.shape[0] if ta else a.shape[1]
    n = b.shape[0] if tb else b.shape[1]
    tk = tk or kdim
    tm, tn = min(tm, m), min(tn, n)
    nm, nn, nk = m // tm, n // tn, kdim // tk
    dims = (((0 if ta else 1,), (1 if tb else 0,)), ((), ()))

    def body(a_ref, b_ref, o_ref, *scratch):
        prod = _bdot(a_ref[...], b_ref[...], dims)
        if nk == 1:
            o_ref[...] = prod.astype(out_dtype)
        else:
            acc = scratch[0]
            k = pl.program_id(2)

            @pl.when(k == 0)
            def _():
                acc[...] = prod

            @pl.when(k > 0)
            def _():
                acc[...] += prod

            @pl.when(k == nk - 1)
            def _():
                o_ref[...] = acc[...].astype(out_dtype)

    a_spec = pl.BlockSpec((tk, tm), lambda i, j, k: (k, i)) if ta else pl.BlockSpec((tm, tk), lambda i, j, k: (i, k))
    b_spec = pl.BlockSpec((tn, tk), lambda i, j, k: (j, k)) if tb else pl.BlockSpec((tk, tn), lambda i, j, k: (k, j))
    return pl.pallas_call(
        body, name=name, grid=(nm, nn, nk),
        in_specs=[a_spec, b_spec],
        out_specs=pl.BlockSpec((tm, tn), lambda i, j, k: (i, j)),
        out_shape=jax.ShapeDtypeStruct((m, n), out_dtype),
        scratch_shapes=[pltpu.VMEM((tm, tn), F32)] if nk > 1 else [],
        compiler_params=_params(("parallel", "parallel", "arbitrary")),
    )(a, b)


def _rmsnorm_fwd(x, w):
    tm = 256

    def body(x_ref, w_ref, h_ref):
        xv = x_ref[...]
        r = lax.rsqrt(jnp.mean(xv * xv, axis=-1, keepdims=True) + EPS)
        h_ref[...] = (xv * r * w_ref[...]).astype(BF)

    return pl.pallas_call(
        body, name="rmsnorm_fwd", grid=(T // tm,),
        in_specs=[pl.BlockSpec((tm, D), lambda i: (i, 0)), pl.BlockSpec((1, D), lambda i: (0, 0))],
        out_specs=pl.BlockSpec((tm, D), lambda i: (i, 0)),
        out_shape=jax.ShapeDtypeStruct((T, D), BF),
        compiler_params=_params(("parallel",)),
    )(x, w)


def _rmsnorm_bwd(x, dh, dout, w):
    tm = 256

    def body(x_ref, dh_ref, dout_ref, w_ref, gx_ref, gw_ref):
        @pl.when(pl.program_id(0) == 0)
        def _():
            gw_ref[...] = jnp.zeros_like(gw_ref)

        xv, dhv = x_ref[...], dh_ref[...]
        r = lax.rsqrt(jnp.mean(xv * xv, axis=-1, keepdims=True) + EPS)
        nrm = xv * r
        dn = dhv * w_ref[...]
        gw_ref[...] += jnp.sum(dhv * nrm, axis=0, keepdims=True)
        gx_ref[...] = dout_ref[...] + r * (dn - nrm * jnp.mean(dn * nrm, axis=-1, keepdims=True))

    row = pl.BlockSpec((tm, D), lambda i: (i, 0))
    vec = pl.BlockSpec((1, D), lambda i: (0, 0))
    return pl.pallas_call(
        body, name="rmsnorm_bwd", grid=(T // tm,),
        in_specs=[row, row, row, vec], out_specs=[row, vec],
        out_shape=[jax.ShapeDtypeStruct((T, D), F32), jax.ShapeDtypeStruct((1, D), F32)],
        compiler_params=_params(("arbitrary",)),
    )(x, dh, dout, w)


def _lower_bound(lbl):
    mx = jnp.max(lbl, axis=0, keepdims=True)
    e = jnp.exp(lbl - mx)
    return e[0:1] / jnp.sum(e, axis=0, keepdims=True)


def _cumsum_rows(g, rows):
    b = g
    sh = 1
    while sh < CH:
        b = b + jnp.where(rows >= sh, pltpu.roll(b, sh, axis=0), 0.0)
        sh *= 2
    return b


def _rev_cumsum_rows(g, rows):
    b = g
    sh = 1
    while sh < CH:
        b = b + jnp.where(rows < CH - sh, pltpu.roll(b, CH - sh, axis=0), 0.0)
        sh *= 2
    return b


def _hgrn_fwd(z, lbl, nw):
    def body(hq_ref, hf_ref, hi_ref, hg_ref, lbl_ref, nw_ref, oraw_ref, og_ref, sh_ref, st_ref):
        @pl.when(pl.program_id(0) == 0)
        def _():
            st_ref[...] = jnp.zeros_like(st_ref)

        lb_all = _lower_bound(lbl_ref[...])
        rows = lax.broadcasted_iota(jnp.int32, (CH, HK), 0)
        nwv = nw_ref[...]
        for h in range(HEADS):
            sl = slice(HK * h, HK * (h + 1))
            lb = lb_all[:, sl]
            hq, hf, v, hg = hq_ref[:, sl], hf_ref[:, sl], hi_ref[:, sl], hg_ref[:, sl]
            q = hq * _sigmoid(hq)
            f = lb + (1.0 - lb) * _sigmoid(hf)
            k = 1.0 - f
            b = _cumsum_rows(jnp.log(f), rows)
            st0 = st_ref[h]
            sh_ref[0, h] = st0
            o = _bdot(q * jnp.exp(b), st0, _NT)
            for s in range(CH):
                e_s = jnp.exp(jnp.where(rows >= s, b - b[s:s + 1], -jnp.inf))
                a = jnp.sum(q * e_s * k[s:s + 1], axis=1, keepdims=True)
                o = o + a * v[s:s + 1]
            bl = b[CH - 1:CH]
            st_ref[h] = st0 * jnp.exp(bl) + _bdot(v, k * jnp.exp(bl - b), _TN)
            oraw_ref[:, sl] = o
            nrm = o * lax.rsqrt(jnp.mean(o * o, axis=1, keepdims=True) + EPS)
            og_ref[:, sl] = (nrm * nwv * (hg * _sigmoid(hg))).astype(BF)

    zblk = lambda c: pl.BlockSpec((CH, D), lambda i, c=c: (i, c))
    return pl.pallas_call(
        body, name="hgrn_fwd", grid=(NCH,),
        in_specs=[zblk(0), zblk(1), zblk(2), zblk(3),
                  pl.BlockSpec((2, D), lambda i: (0, 0)), pl.BlockSpec((1, HK), lambda i: (0, 0))],
        out_specs=[pl.BlockSpec((CH, D), lambda i: (i, 0)), pl.BlockSpec((CH, D), lambda i: (i, 0)),
                   pl.BlockSpec((1, HEADS, HK, HK), lambda i: (i, 0, 0, 0))],
        out_shape=[jax.ShapeDtypeStruct((T, D), F32), jax.ShapeDtypeStruct((T, D), BF),
                   jax.ShapeDtypeStruct((NCH, HEADS, HK, HK), F32)],
        scratch_shapes=[pltpu.VMEM((HEADS, HK, HK), F32)],
        compiler_params=_params(("arbitrary",)),
    )(z, z, z, z, lbl, nw)


def _hgrn_bwd(z, lbl, nw, oraw, dog, shist):
    def body(hq_ref, hf_ref, hi_ref, hg_ref, lbl_ref, nw_ref, oraw_ref, dog_ref, sh_ref,
             dz_ref, dlb_ref, dnw_ref, dst_ref):
        @pl.when(pl.program_id(0) == 0)
        def _():
            dst_ref[...] = jnp.zeros_like(dst_ref)
            dlb_ref[...] = jnp.zeros_like(dlb_ref)
            dnw_ref[...] = jnp.zeros_like(dnw_ref)

        lb_all = _lower_bound(lbl_ref[...])
        rows = lax.broadcasted_iota(jnp.int32, (CH, HK), 0)
        rowc = lax.broadcasted_iota(jnp.int32, (CH, 1), 0)
        nwv = nw_ref[...]
        dnw = jnp.zeros((1, HK), F32)
        for h in range(HEADS):
            sl = slice(HK * h, HK * (h + 1))
            lb = lb_all[:, sl]
            hq, hf, v, hg = hq_ref[:, sl], hf_ref[:, sl], hi_ref[:, sl], hg_ref[:, sl]
            o, dg_out = oraw_ref[:, sl], dog_ref[:, sl]
            sg = _sigmoid(hg)
            sil = hg * sg
            r = lax.rsqrt(jnp.mean(o * o, axis=1, keepdims=True) + EPS)
            nrm = o * r
            d_hg = dg_out * (nrm * nwv) * (sg * (1.0 + hg * (1.0 - sg)))
            dn = dg_out * nwv * sil
            dnw = dnw + jnp.sum(dg_out * nrm * sil, axis=0, keepdims=True)
            do = r * (dn - nrm * jnp.mean(dn * nrm, axis=1, keepdims=True))
            sq = _sigmoid(hq)
            q = hq * sq
            sig = _sigmoid(hf)
            f = lb + (1.0 - lb) * sig
            k = 1.0 - f
            b = _cumsum_rows(jnp.log(f), rows)
            eb = jnp.exp(b)
            qe = q * eb
            bl = b[CH - 1:CH]
            ebl = jnp.exp(bl)
            kdec = jnp.exp(bl - b)
            ke = k * kdec
            st0 = sh_ref[0, h]
            dst1 = dst_ref[h]
            dqe = _bdot(do, st0)
            dst_ref[h] = dst1 * ebl + _bdot(do, qe, _TN)
            dq = dqe * eb
            db = dqe * qe
            dke = _bdot(v, dst1)
            dv = _bdot(ke, dst1, _NT)
            dk = dke * kdec
            rr = dke * ke
            db = db - rr
            db_last = jnp.sum(rr, axis=0, keepdims=True) + ebl * jnp.sum(dst1 * st0, axis=0, keepdims=True)
            for s in range(CH):
                one = (rowc == s).astype(F32)
                ks, vs = k[s:s + 1], v[s:s + 1]
                e_s = jnp.exp(jnp.where(rows >= s, b - b[s:s + 1], -jnp.inf))
                qes = q * e_s
                w = qes * ks
                a = jnp.sum(w, axis=1, keepdims=True)
                da = jnp.sum(do * vs, axis=1, keepdims=True)
                dv = dv + one * jnp.sum(a * do, axis=0, keepdims=True)
                dq = dq + da * e_s * ks
                dk = dk + one * jnp.sum(da * qes, axis=0, keepdims=True)
                u = da * w
                db = db + u - one * jnp.sum(u, axis=0, keepdims=True)
            db = db + (rowc == CH - 1).astype(F32) * db_last
            dgl = _rev_cumsum_rows(db, rows)
            df = dgl / f - dk
            dlb_ref[:, sl] += jnp.sum(df * (1.0 - sig), axis=0, keepdims=True)
            dz_ref[:, sl] = (dq * (sq * (1.0 + hq * (1.0 - sq)))).astype(BF)
            dz_ref[:, D + HK * h:D + HK * (h + 1)] = (df * (1.0 - lb) * sig * (1.0 - sig)).astype(BF)
            dz_ref[:, 2 * D + HK * h:2 * D + HK * (h + 1)] = dv.astype(BF)
            dz_ref[:, 3 * D + HK * h:3 * D + HK * (h + 1)] = d_hg.astype(BF)
        dnw_ref[...] += dnw

    rev = lambda i: NCH - 1 - i
    zblk = lambda c: pl.BlockSpec((CH, D), lambda i, c=c: (rev(i), c))
    return pl.pallas_call(
        body, name="hgrn_bwd", grid=(NCH,),
        in_specs=[zblk(0), zblk(1), zblk(2), zblk(3),
                  pl.BlockSpec((2, D), lambda i: (0, 0)), pl.BlockSpec((1, HK), lambda i: (0, 0)),
                  pl.BlockSpec((CH, D), lambda i: (rev(i), 0)), pl.BlockSpec((CH, D), lambda i: (rev(i), 0)),
                  pl.BlockSpec((1, HEADS, HK, HK), lambda i: (rev(i), 0, 0, 0))],
        out_specs=[pl.BlockSpec((CH, 4 * D), lambda i: (rev(i), 0)),
                   pl.BlockSpec((1, D), lambda i: (0, 0)), pl.BlockSpec((1, HK), lambda i: (0, 0))],
        out_shape=[jax.ShapeDtypeStruct((T, 4 * D), BF), jax.ShapeDtypeStruct((1, D), F32),
                   jax.ShapeDtypeStruct((1, HK), F32)],
        scratch_shapes=[pltpu.VMEM((HEADS, HK, HK), F32)],
        compiler_params=_params(("arbitrary",)),
    )(z, z, z, z, lbl, nw, oraw, dog, shist)


BLK = 128
NBLK = T // BLK


def _rot_half(x):
    lane = lax.broadcasted_iota(jnp.int32, x.shape, 1)
    return jnp.where((lane % 64) < 32, -pltpu.roll(x, 96, axis=1), pltpu.roll(x, 32, axis=1))


def _head_masks():
    lane = lax.broadcasted_iota(jnp.int32, (1, BLK), 1)
    return [(lane < 64).astype(F32), (lane >= 64).astype(F32)]


def _pieces(dil):
    m = T // dil
    out = []
    for r in range(dil):
        for j in range(m // BLK):
            start = r + dil * BLK * j
            rows = pl.ds(start, BLK, stride=dil) if dil > 1 else pl.ds(start, BLK)
            out.append((rows, r * m + BLK * j))
    return out


def _rope_tables(pos_ref, invf_ref, cos_ref, sin_ref):
    for c in range(T // 256):
        rows = pl.ds(256 * c, 256)
        ang = pos_ref[rows, :].astype(F32) * invf_ref[...]
        cos_ref[rows, :] = jnp.cos(ang)
        sin_ref[rows, :] = jnp.sin(ang)


def _block_scores(qh, kc, kp, has_prev):
    ii = lax.broadcasted_iota(jnp.int32, (BLK, BLK), 0)
    jj = lax.broadcasted_iota(jnp.int32, (BLK, BLK), 1)
    s_c = jnp.where(jj <= ii, _dot(qh, kc, _NT) * 0.125, -jnp.inf)
    s_p = jnp.where((jj >= ii) & has_prev, _dot(qh, kp, _NT) * 0.125, -jnp.inf)
    return s_c, s_p


def _attn_fwd(z, pos, invf):
    def body(q_ref, k_ref, v_ref, ag_ref, pos_ref, invf_ref, ob_ref, opre_ref, lse_ref,
             cos_ref, sin_ref, qr_ref, kr_ref, vr_ref, og_ref, lg_ref, otok_ref, ltok_ref):
        g = pl.program_id(1)
        masks = _head_masks()

        @pl.when(g == 0)
        def _():
            _rope_tables(pos_ref, invf_ref, cos_ref, sin_ref)

        def group(gi):
            dil = ATT_GROUPS[gi][1]
            nblk = (T // dil) // BLK
            for rows, dst in _pieces(dil):
                c, s = cos_ref[rows, :], sin_ref[rows, :]
                qv, kv = q_ref[rows, :], k_ref[rows, :]
                qr_ref[pl.ds(dst, BLK), :] = qv * c + _rot_half(qv) * s
                kr_ref[pl.ds(dst, BLK), :] = kv * c + _rot_half(kv) * s
                vr_ref[pl.ds(dst, BLK), :] = v_ref[rows, :]

            def blk(bi, carry):
                off = pl.multiple_of(bi * BLK, BLK)
                offp = pl.multiple_of(jnp.maximum(bi - 1, 0) * BLK, BLK)
                has_prev = (bi % nblk) != 0
                qb = qr_ref[pl.ds(off, BLK), :]
                kc, vc = kr_ref[pl.ds(off, BLK), :].astype(BF), vr_ref[pl.ds(off, BLK), :].astype(BF)
                kp, vp = kr_ref[pl.ds(offp, BLK), :].astype(BF), vr_ref[pl.ds(offp, BLK), :].astype(BF)
                o_acc = jnp.zeros((BLK, BLK), F32)
                l_acc = jnp.zeros((BLK, BLK), F32)
                for mh in masks:
                    s_c, s_p = _block_scores((qb * mh).astype(BF), kc, kp, has_prev)
                    mx = jnp.maximum(jnp.max(s_c, axis=1, keepdims=True), jnp.max(s_p, axis=1, keepdims=True))
                    p_c, p_p = jnp.exp(s_c - mx), jnp.exp(s_p - mx)
                    den = jnp.sum(p_c, axis=1, keepdims=True) + jnp.sum(p_p, axis=1, keepdims=True)
                    oh = _dot(p_c.astype(BF), vc) + _dot(p_p.astype(BF), vp)
                    o_acc = o_acc + (oh / den) * mh
                    l_acc = l_acc + (mx + jnp.log(den)) * mh
                og_ref[pl.ds(off, BLK), :] = o_acc
                lg_ref[pl.ds(off, BLK), :] = l_acc
                return carry

            lax.fori_loop(0, NBLK, blk, 0)
            for rows, src in _pieces(dil):
                otok_ref[gi, rows, :] = og_ref[pl.ds(src, BLK), :]
                ltok_ref[gi, rows, :] = lg_ref[pl.ds(src, BLK), :]

        for gi in range(3):
            pl.when(g == gi)(functools.partial(group, gi))

        @pl.when(g == 2)
        def _():
            for c in range(T // 256):
                rows = pl.ds(256 * c, 256)
                l0, l1, l2 = ltok_ref[0, rows, :], ltok_ref[1, rows, :], ltok_ref[2, rows, :]
                mx = jnp.maximum(jnp.maximum(l0, l1), l2)
                lse = mx + jnp.log(jnp.exp(l0 - mx) + jnp.exp(l1 - mx) + jnp.exp(l2 - mx))
                o = (jnp.exp(l0 - lse) * otok_ref[0, rows, :] + jnp.exp(l1 - lse) * otok_ref[1, rows, :]
                     + jnp.exp(l2 - lse) * otok_ref[2, rows, :])
                ag = ag_ref[rows, :]
                opre_ref[rows, :] = o
                lse_ref[rows, :] = lse
                ob_ref[rows, :] = (o * (ag * _sigmoid(ag))).astype(BF)

    c0 = ATT_COL0 // BLK
    zspec = lambda part: pl.BlockSpec((T, BLK), lambda p, g, part=part: (0, c0 + 12 * part + 4 * g + p))
    outspec = pl.BlockSpec((T, BLK), lambda p, g: (0, p))
    big = lambda: pltpu.VMEM((T, BLK), F32)
    return pl.pallas_call(
        body, name="attn_fwd", grid=(4, 3),
        in_specs=[zspec(0), zspec(1), zspec(2),
                  pl.BlockSpec((T, BLK), lambda p, g: (0, AG_COL0 // BLK + p)),
                  pl.BlockSpec((T, 1), lambda p, g: (0, 0)), pl.BlockSpec((1, BLK), lambda p, g: (0, 0))],
        out_specs=[outspec, outspec, outspec],
        out_shape=[jax.ShapeDtypeStruct((T, 512), BF), jax.ShapeDtypeStruct((T, 512), F32),
                   jax.ShapeDtypeStruct((T, 512), F32)],
        scratch_shapes=[big(), big(), big(), big(), big(), big(), big(),
                        pltpu.VMEM((3, T, BLK), F32), pltpu.VMEM((3, T, BLK), F32)],
        compiler_params=_params(("parallel", "arbitrary")),
    )(z, z, z, z, pos, invf)


def _attn_bwd(z, pos, invf, opre, lse, dob):
    def body(q_ref, k_ref, v_ref, ag_ref, pos_ref, invf_ref, o_ref, lse_ref, dob_ref,
             dq_ref, dk_ref, dv_ref, dag_ref,
             cos_ref, sin_ref, qr_ref, kr_ref, vr_ref, dor_ref, lr_ref, dr_ref, dqr_ref, dkr_ref, dvr_ref):
        g = pl.program_id(1)
        masks = _head_masks()

        @pl.when(g == 0)
        def _():
            _rope_tables(pos_ref, invf_ref, cos_ref, sin_ref)
            for c in range(T // 256):
                rows = pl.ds(256 * c, 256)
                ag = ag_ref[rows, :]
                sg = _sigmoid(ag)
                dag_ref[rows, :] = (dob_ref[rows, :] * o_ref[rows, :] * (sg * (1.0 + ag * (1.0 - sg)))).astype(BF)

        def group(gi):
            dil = ATT_GROUPS[gi][1]
            nblk = (T // dil) // BLK
            for rows, dst in _pieces(dil):
                drows = pl.ds(dst, BLK)
                c, s = cos_ref[rows, :], sin_ref[rows, :]
                qv, kv = q_ref[rows, :], k_ref[rows, :]
                qr_ref[drows, :] = qv * c + _rot_half(qv) * s
                kr_ref[drows, :] = kv * c + _rot_half(kv) * s
                vr_ref[drows, :] = v_ref[rows, :]
                ag = ag_ref[rows, :]
                do = dob_ref[rows, :] * (ag * _sigmoid(ag))
                prod = do * o_ref[rows, :]
                dor_ref[drows, :] = do
                lr_ref[drows, :] = lse_ref[rows, :]
                dr_ref[drows, :] = (jnp.sum(prod * masks[0], axis=1, keepdims=True) * masks[0]
                                    + jnp.sum(prod * masks[1], axis=1, keepdims=True) * masks[1])
            dkr_ref[...] = jnp.zeros_like(dkr_ref)
            dvr_ref[...] = jnp.zeros_like(dvr_ref)

            def blk(bi, carry):
                off = pl.multiple_of(bi * BLK, BLK)
                offp = pl.multiple_of(jnp.maximum(bi - 1, 0) * BLK, BLK)
                cur, prev = pl.ds(off, BLK), pl.ds(offp, BLK)
                has_prev = (bi % nblk) != 0
                qb, dob_b = qr_ref[cur, :], dor_ref[cur, :]
                lb, db = lr_ref[cur, :], dr_ref[cur, :]
                kc, vc = kr_ref[cur, :].astype(BF), vr_ref[cur, :].astype(BF)
                kp, vp = kr_ref[prev, :].astype(BF), vr_ref[prev, :].astype(BF)
                dq = jnp.zeros((BLK, BLK), F32)
                dkc, dkp, dvc, dvp = dq, dq, dq, dq
                for mh in masks:
                    qh, doh = (qb * mh).astype(BF), (dob_b * mh).astype(BF)
                    lh = jnp.sum(lb * mh, axis=1, keepdims=True) * (1.0 / 64)
                    dh = jnp.sum(db * mh, axis=1, keepdims=True) * (1.0 / 64)
                    s_c, s_p = _block_scores(qh, kc, kp, has_prev)
                    p_c, p_p = jnp.exp(s_c - lh), jnp.exp(s_p - lh)
                    ds_c = (p_c * (_dot(doh, vc, _NT) - dh) * 0.125).astype(BF)
                    ds_p = (p_p * (_dot(doh, vp, _NT) - dh) * 0.125).astype(BF)
                    dq = dq + (_dot(ds_c, kc) + _dot(ds_p, kp)) * mh
                    dkc = dkc + _dot(ds_c, qh, _TN)
                    dkp = dkp + _dot(ds_p, qh, _TN)
                    dvc = dvc + _dot(p_c.astype(BF), doh, _TN)
                    dvp = dvp + _dot(p_p.astype(BF), doh, _TN)
                dqr_ref[cur, :] = dq
                dkr_ref[prev, :] += dkp
                dvr_ref[prev, :] += dvp
                dkr_ref[cur, :] += dkc
                dvr_ref[cur, :] += dvc
                return carry

            lax.fori_loop(0, NBLK, blk, 0)
            for rows, src in _pieces(dil):
                srows = pl.ds(src, BLK)
                c, s = cos_ref[rows, :], sin_ref[rows, :]
                dq, dk = dqr_ref[srows, :], dkr_ref[srows, :]
                qr_ref[rows, :] = dq * c - _rot_half(dq * s)
                kr_ref[rows, :] = dk * c - _rot_half(dk * s)
                vr_ref[rows, :] = dvr_ref[srows, :]
            for c in range(T // 256):
                rows = pl.ds(256 * c, 256)
                dq_ref[rows, :] = qr_ref[rows, :].astype(BF)
                dk_ref[rows, :] = kr_ref[rows, :].astype(BF)
                dv_ref[rows, :] = vr_ref[rows, :].astype(BF)

        for gi in range(3):
            pl.when(g == gi)(functools.partial(group, gi))

    c0 = ATT_COL0 // BLK
    zspec = lambda part: pl.BlockSpec((T, BLK), lambda p, g, part=part: (0, c0 + 12 * part + 4 * g + p))
    pspec = pl.BlockSpec((T, BLK), lambda p, g: (0, p))
    gspec = pl.BlockSpec((T, BLK), lambda p, g: (0, 4 * g + p))
    big = lambda: pltpu.VMEM((T, BLK), F32)
    return pl.pallas_call(
        body, name="attn_bwd", grid=(4, 3),
        in_specs=[zspec(0), zspec(1), zspec(2),
                  pl.BlockSpec((T, BLK), lambda p, g: (0, AG_COL0 // BLK + p)),
                  pl.BlockSpec((T, 1), lambda p, g: (0, 0)), pl.BlockSpec((1, BLK), lambda p, g: (0, 0)),
                  pspec, pspec, pspec],
        out_specs=[gspec, gspec, gspec, pspec],
        out_shape=[jax.ShapeDtypeStruct((T, 1536), BF), jax.ShapeDtypeStruct((T, 1536), BF),
                   jax.ShapeDtypeStruct((T, 1536), BF), jax.ShapeDtypeStruct((T, 512), BF)],
        scratch_shapes=[big() for _ in range(11)],
        compiler_params=_params(("parallel", "arbitrary")),
    )(z, z, z, z, pos, invf, opre, lse, dob)


def _merge_fwd(ya, yb, z):
    tm = 256

    def body(ya_ref, yb_ref, ga_ref, gb_ref, m_ref):
        m_ref[...] = (_sigmoid(ga_ref[...]) * ya_ref[...] + _sigmoid(gb_ref[...]) * yb_ref[...]).astype(BF)

    row = pl.BlockSpec((tm, D), lambda i: (i, 0))
    return pl.pallas_call(
        body, name="merge_fwd", grid=(T // tm,),
        in_specs=[row, row, pl.BlockSpec((tm, D), lambda i: (i, GATE_COL0 // D)),
                  pl.BlockSpec((tm, D), lambda i: (i, GATE_COL0 // D + 1))],
        out_specs=row, out_shape=jax.ShapeDtypeStruct((T, D), BF),
        compiler_params=_params(("parallel",)),
    )(ya, yb, z, z)


def _out_loss(merged, w_out, x, tgt, wf):
    tm = 256

    def body(m_ref, w_ref, x_ref, t_ref, wf_ref, dout_ref, loss_ref, gwf_ref):
        @pl.when(pl.program_id(0) == 0)
        def _():
            loss_ref[...] = jnp.zeros_like(loss_ref)
            gwf_ref[...] = jnp.zeros_like(gwf_ref)

        out = x_ref[...] + _dot(m_ref[...], w_ref[...])
        r = lax.rsqrt(jnp.mean(out * out, axis=-1, keepdims=True) + EPS)
        yh = out * r
        wfv = wf_ref[...]
        err = yh * wfv - t_ref[...]
        loss_ref[...] += jnp.sum(err * err, axis=0, keepdims=True) * (0.5 / D)
        dy = err * (1.0 / D)
        gwf_ref[...] += jnp.sum(dy * yh, axis=0, keepdims=True)
        dyh = dy * wfv
        dout_ref[...] = r * (dyh - yh * jnp.mean(dyh * yh, axis=-1, keepdims=True))

    row = pl.BlockSpec((tm, D), lambda i: (i, 0))
    vec = pl.BlockSpec((1, D), lambda i: (0, 0))
    return pl.pallas_call(
        body, name="out_loss", grid=(T // tm,),
        in_specs=[row, pl.BlockSpec((D, D), lambda i: (0, 0)), row, row, vec],
        out_specs=[row, vec, vec],
        out_shape=[jax.ShapeDtypeStruct((T, D), F32), jax.ShapeDtypeStruct((1, D), F32),
                   jax.ShapeDtypeStruct((1, D), F32)],
        compiler_params=_params(("arbitrary",)),
    )(merged, w_out, x, tgt, wf)


def _merge_bwd(dm, ya, yb, z):
    tm = 256

    def body(dm_ref, ya_ref, yb_ref, ga_ref, gb_ref, dya_ref, dyb_ref, dg_ref):
        dmv = dm_ref[...]
        sa, sb = _sigmoid(ga_ref[...]), _sigmoid(gb_ref[...])
        dya_ref[...] = (sa * dmv).astype(BF)
        dyb_ref[...] = (sb * dmv).astype(BF)
        dg_ref[:, :D] = (dmv * ya_ref[...] * sa * (1.0 - sa)).astype(BF)
        dg_ref[:, D:] = (dmv * yb_ref[...] * sb * (1.0 - sb)).astype(BF)

    row = pl.BlockSpec((tm, D), lambda i: (i, 0))
    return pl.pallas_call(
        body, name="merge_bwd", grid=(T // tm,),
        in_specs=[row, row, row, pl.BlockSpec((tm, D), lambda i: (i, GATE_COL0 // D)),
                  pl.BlockSpec((tm, D), lambda i: (i, GATE_COL0 // D + 1))],
        out_specs=[row, row, pl.BlockSpec((tm, 2 * D), lambda i: (i, 0))],
        out_shape=[jax.ShapeDtypeStruct((T, D), BF), jax.ShapeDtypeStruct((T, D), BF),
                   jax.ShapeDtypeStruct((T, 2 * D), BF)],
        compiler_params=_params(("parallel",)),
    )(dm, ya, yb, z, z)


def _rope_inv_freq():
    inv = ROPE_THETA ** (-jnp.arange(0, 64, 2, dtype=F32) / 64)
    return jnp.tile(inv, 4).reshape(1, BLK)


def _local_step(x, pos, norm_w, lbl, hnw, wf, tgt, w_in, w_a, w_b, w_out):
    invf = _rope_inv_freq()
    h = _rmsnorm_fwd(x, norm_w)
    z = _matmul(h, w_in, tm=1024, tn=512, name="z_proj")
    oraw, og, shist = _hgrn_fwd(z, lbl, hnw)
    ob, opre, lse = _attn_fwd(z, pos, invf)
    ya = _matmul(og, w_a, tm=1024, tn=512, name="ya_proj")
    yb = _matmul(ob, w_b, tm=1024, tn=512, name="yb_proj")
    merged = _merge_fwd(ya, yb, z)
    dout, loss_vec, g_wf = _out_loss(merged, w_out, x, tgt, wf)

    dm = _matmul(dout, w_out, tb=True, tm=1024, tn=512, name="d_merged")
    g_wout = _matmul(merged, dout, ta=True, out_dtype=BF, tm=512, tn=1024, name="g_wout")
    dya, dyb, dgates = _merge_bwd(dm, ya, yb, z)
    dog = _matmul(dya, w_a, tb=True, tm=1024, tn=512, name="d_og")
    g_wa = _matmul(og, dya, ta=True, out_dtype=BF, tm=512, tn=1024, name="g_wa")
    dob = _matmul(dyb, w_b, tb=True, tm=1024, tn=512, name="d_ob")
    g_wb = _matmul(ob, dyb, ta=True, out_dtype=BF, tm=512, tn=1024, name="g_wb")
    dz_h, dlb, g_hnw = _hgrn_bwd(z, lbl, hnw, oraw, dog, shist)
    dq, dk, dv, dag = _attn_bwd(z, pos, invf, opre, lse, dob)
    dz = jnp.concatenate([dz_h, dq, dk, dv, dag, dgates], axis=1)
    g_win = _matmul(h, dz, ta=True, out_dtype=BF, tm=1024, tn=512, name="g_win")
    dh = _matmul(dz, w_in, tb=True, tm=1024, tn=1024, tk=1408, name="d_h")
    gx, g_nw = _rmsnorm_bwd(x, dh, dout, norm_w)
    return dict(loss_vec=loss_vec, gx=gx, g_nw=g_nw, dlb=dlb, g_hnw=g_hnw, g_wf=g_wf,
                g_win=g_win, g_wa=g_wa, g_wb=g_wb, g_wout=g_wout)


MESH = pl.DeviceIdType.MESH
HBM = pl.BlockSpec(memory_space=pl.ANY)
WEIGHT_AXES = (1, 0, 1, 0)


def _place():
    x, y, c = lax.axis_index("x"), lax.axis_index("y"), lax.axis_index("c")
    chips = [(1 - x, y), (x, 1 - y), (1 - x, 1 - y)]
    return x, y, c, chips


def _block_half(ref, shard_shape, axis, j, half):
    r, c = shard_shape
    hr = r // 2
    if axis == 0:
        return ref.at[pl.ds(pl.multiple_of(j * r + half * hr, 16), hr), :]
    return ref.at[pl.ds(pl.multiple_of(half * hr, 16), hr), pl.ds(pl.multiple_of(j * c, 128), c)]


def _all_gather(shards):
    n = len(shards)
    shapes = [s.shape for s in shards]

    def body(*refs):
        ins, outs = refs[:n], refs[n:2 * n]
        send1, recv1, send2, recv2, lsem = refs[2 * n:]
        x, y, c, chips = _place()
        me = 2 * x + y
        started, local = [], []
        for a in range(n):
            r, cc = shapes[a]
            ax = WEIGHT_AXES[a]
            mine = (outs[a].at[pl.ds(pl.multiple_of(me * r, 16), r), :] if ax == 0
                    else outs[a].at[:, pl.ds(pl.multiple_of(me * cc, 128), cc)])
            loc = pltpu.make_async_copy(ins[a], mine, lsem.at[a])
            loc.start()
            local.append(loc)
            src = ins[a].at[pl.ds(pl.multiple_of(c * (r // 2), 16), r // 2), :]
            for k, (px, py) in enumerate(chips):
                cp = pltpu.make_async_remote_copy(
                    src_ref=src, dst_ref=_block_half(outs[a], shapes[a], ax, me, c),
                    send_sem=send1.at[a, k], recv_sem=recv1.at[a, k], device_id=(px, py, c), device_id_type=MESH)
                cp.start()
                started.append(cp)
        for a in range(n):
            for k, (px, py) in enumerate(chips):
                reg = _block_half(outs[a], shapes[a], WEIGHT_AXES[a], 2 * px + py, c)
                pltpu.make_async_remote_copy(
                    src_ref=reg, dst_ref=reg, send_sem=send1.at[a, k], recv_sem=recv1.at[a, k],
                    device_id=(px, py, c), device_id_type=MESH).wait_recv()
                fw = pltpu.make_async_remote_copy(
                    src_ref=reg, dst_ref=reg, send_sem=send2.at[a, k], recv_sem=recv2.at[a, k],
                    device_id=(x, y, 1 - c), device_id_type=MESH)
                fw.start()
                started.append(fw)
        for a in range(n):
            for k, (px, py) in enumerate(chips):
                reg = _block_half(outs[a], shapes[a], WEIGHT_AXES[a], 2 * px + py, 1 - c)
                pltpu.make_async_remote_copy(
                    src_ref=reg, dst_ref=reg, send_sem=send2.at[a, k], recv_sem=recv2.at[a, k],
                    device_id=(x, y, 1 - c), device_id_type=MESH).wait_recv()
        for cp in started:
            cp.wait_send()
        for cp in local:
            cp.wait()

    full =[(4 * r, c) if ax == 0 else (r, 4 * c) for (r, c), ax in zip(shapes, WEIGHT_AXES)]
    return pl.pallas_call(
        body, name="weights_all_gather",
        in_specs=[HBM] * n, out_specs=[HBM] * n,
        out_shape=[jax.ShapeDtypeStruct(f, BF) for f in full],
        scratch_shapes=[pltpu.SemaphoreType.DMA((n, 3)), pltpu.SemaphoreType.DMA((n, 3)),
                        pltpu.SemaphoreType.DMA((n, 3)), pltpu.SemaphoreType.DMA((n, 3)),
                        pltpu.SemaphoreType.DMA((n,))],
    )(*shards)


def _as3d(g, shard_shape, axis):
    r, c = shard_shape
    return g.reshape(4, r, c) if axis == 0 else g.reshape(1, r, 4 * c)


def _half_rows(ref3, hr, half):
    return ref3.at[:, pl.ds(pl.multiple_of(half * hr, 16), hr), :]


def _rs_pair_exchange(g3s):
    n = len(g3s)

    def body(*refs):
        ins, outs = refs[:n], refs[n:2 * n]
        send, recv = refs[2 * n:]
        x, y, c, _ = _place()
        cps = []
        for a in range(n):
            hr = g3s[a].shape[1] // 2
            cp = pltpu.make_async_remote_copy(
                src_ref=_half_rows(ins[a], hr, 1 - c), dst_ref=outs[a],
                send_sem=send.at[a], recv_sem=recv.at[a], device_id=(x, y, 1 - c), device_id_type=MESH)
            cp.start()
            cps.append(cp)
        for cp in cps:
            cp.wait()

    return pl.pallas_call(
        body, name="grads_pair_exchange",
        in_specs=[HBM] * n, out_specs=[HBM] * n,
        out_shape=[jax.ShapeDtypeStruct((g.shape[0], g.shape[1] // 2, g.shape[2]), BF) for g in g3s],
        scratch_shapes=[pltpu.SemaphoreType.DMA((n,)), pltpu.SemaphoreType.DMA((n,))],
    )(*g3s)


def _pair_sum(g3, land, cidx, name):
    nb, r, w = g3.shape
    hr = r // 2
    tr = 64

    def body(c_ref, g_ref, l_ref, o_ref):
        o_ref[...] = (g_ref[...].astype(F32) + l_ref[...].astype(F32)).astype(BF)

    blk = (nb, tr, w)
    return pl.pallas_call(
        body, name=name,
        grid_spec=pltpu.PrefetchScalarGridSpec(
            num_scalar_prefetch=1, grid=(hr // tr,),
            in_specs=[pl.BlockSpec(blk, lambda i, c: (0, c[0] * (hr // tr) + i, 0)),
                      pl.BlockSpec(blk, lambda i, c: (0, i, 0))],
            out_specs=pl.BlockSpec(blk, lambda i, c: (0, i, 0))),
        out_shape=jax.ShapeDtypeStruct((nb, hr, w), BF),
        compiler_params=_params(("parallel",)),
    )(cidx, g3, land)


def _rs_chip_exchange(p3s, shapes):
    n = len(p3s)

    def body(*refs):
        ins, outs = refs[:n], refs[n:2 * n]
        send, recv = refs[2 * n:]
        x, y, c, chips = _place()
        cps = []
        for a in range(n):
            r, cc = shapes[a]
            for k, (px, py) in enumerate(chips):
                j = 2 * px + py
                src = ins[a].at[j] if WEIGHT_AXES[a] == 0 else ins[a].at[0, :, pl.ds(pl.multiple_of(j * cc, 128), cc)]
                cp = pltpu.make_async_remote_copy(
                    src_ref=src, dst_ref=outs[a].at[k], send_sem=send.at[a, k], recv_sem=recv.at[a, k],
                    device_id=(px, py, c), device_id_type=MESH)
                cp.start()
                cps.append(cp)
        for cp in cps:
            cp.wait()

    return pl.pallas_call(
        body, name="grads_chip_exchange",
        in_specs=[HBM] * n, out_specs=[HBM] * n,
        out_shape=[jax.ShapeDtypeStruct((3, r // 2, c), BF) for r, c in shapes],
        scratch_shapes=[pltpu.SemaphoreType.DMA((n, 3)), pltpu.SemaphoreType.DMA((n, 3))],
    )(*p3s)


def _chip_sum(p3, land, shard_shape, axis, meidx, name):
    r, c = shard_shape
    hr = r // 2
    tr = 64

    def body(me_ref, p_ref, l_ref, o_ref):
        acc = p_ref[...].astype(F32)
        for k in range(3):
            acc = acc + l_ref[k].astype(F32)
        o_ref[...] = acc

    own = (pl.BlockSpec((None, tr, c), lambda i, me: (me[0], i, 0)) if axis == 0
           else pl.BlockSpec((None, tr, c), lambda i, me: (0, i, me[0])))
    return pl.pallas_call(
        body, name=name,
        grid_spec=pltpu.PrefetchScalarGridSpec(
            num_scalar_prefetch=1, grid=(hr // tr,),
            in_specs=[own, pl.BlockSpec((3, tr, c), lambda i, me: (0, i, 0))],
            out_specs=pl.BlockSpec((tr, c), lambda i, me: (i, 0))),
        out_shape=jax.ShapeDtypeStruct((hr, c), F32),
        compiler_params=_params(("parallel",)),
    )(meidx, p3, land)


def _rs_pair_gather(halves):
    n = len(halves)

    def body(*refs):
        ins, outs = refs[:n], refs[n:2 * n]
        send, recv, lsem = refs[2 * n:]
        x, y, c, _ = _place()
        cps = []
        for a in range(n):
            hr = halves[a].shape[0]
            reg = outs[a].at[pl.ds(pl.multiple_of(c * hr, 8), hr), :]
            loc = pltpu.make_async_copy(ins[a], reg, lsem.at[a])
            loc.start()
            cp = pltpu.make_async_remote_copy(
                src_ref=ins[a], dst_ref=reg, send_sem=send.at[a], recv_sem=recv.at[a],
                device_id=(x, y, 1 - c), device_id_type=MESH)
            cp.start()
            cps += [loc, cp]
        for cp in cps:
            cp.wait()

    return pl.pallas_call(
        body, name="grads_pair_gather",
        in_specs=[HBM] * n, out_specs=[HBM] * n,
        out_shape=[jax.ShapeDtypeStruct((2 * h.shape[0], h.shape[1]), F32) for h in halves],
        scratch_shapes=[pltpu.SemaphoreType.DMA((n,)), pltpu.SemaphoreType.DMA((n,)), pltpu.SemaphoreType.DMA((n,))],
    )(*halves)


def _reduce_scatter(grads, shapes):
    x, y, c = lax.axis_index("x"), lax.axis_index("y"), lax.axis_index("c")
    cidx = jnp.reshape(c, (1,)).astype(jnp.int32)
    meidx = jnp.reshape(2 * x + y, (1,)).astype(jnp.int32)
    g3s = [_as3d(g, s, ax) for g, s, ax in zip(grads, shapes, WEIGHT_AXES)]
    lands = _rs_pair_exchange(g3s)
    p3s = [_pair_sum(g3, l, cidx, f"pair_sum_{a}") for a, (g3, l) in enumerate(zip(g3s, lands))]
    lands2 = _rs_chip_exchange(p3s, shapes)
    halves = [_chip_sum(p3, l2, s, ax, meidx, f"chip_sum_{a}")
              for a, (p3, l2, s, ax) in enumerate(zip(p3s, lands2, shapes, WEIGHT_AXES))]
    return _rs_pair_gather(halves)


NSMALL = 8


def _small_all_reduce(g_nw, dlb, g_hnw, g_wf, loss_vec):
    def body(nw_ref, lb_ref, hn_ref, wf_ref, ls_ref, out_ref, pack_ref, buf_ref, send, recv):
        x, y, c = lax.axis_index("x"), lax.axis_index("y"), lax.axis_index("c")
        me = 4 * x + 2 * y + c
        pack_ref[...] = jnp.zeros_like(pack_ref)
        pack_ref[0:1, :] = nw_ref[...]
        pack_ref[1:2, :] = lb_ref[...]
        pack_ref[2:3, 0:HK] = hn_ref[...]
        pack_ref[3:4, :] = wf_ref[...]
        pack_ref[4:5, :] = ls_ref[...]
        buf_ref[me] = pack_ref[...]
        cps = []
        for d in range(1, 8):
            dx, dy, dc = d >> 2, (d >> 1) & 1, d & 1
            peer = (1 - x if dx else x, 1 - y if dy else y, 1 - c if dc else c)
            cp = pltpu.make_async_remote_copy(
                src_ref=pack_ref, dst_ref=buf_ref.at[me], send_sem=send.at[d - 1], recv_sem=recv.at[d - 1],
                device_id=peer, device_id_type=MESH)
            cp.start()
            cps.append(cp)
        for d in range(1, 8):
            dx, dy, dc = d >> 2, (d >> 1) & 1, d & 1
            src = 4 * (1 - x if dx else x) + 2 * (1 - y if dy else y) + (1 - c if dc else c)
            pltpu.make_async_remote_copy(
                src_ref=pack_ref, dst_ref=buf_ref.at[src], send_sem=send.at[d - 1], recv_sem=recv.at[d - 1],
                device_id=(x, y, c), device_id_type=MESH).wait_recv()
        for cp in cps:
            cp.wait_send()
        acc = buf_ref[0]
        for i in range(1, 8):
            acc = acc + buf_ref[i]
        out_ref[...] = acc

    vm = pl.BlockSpec(memory_space=pltpu.VMEM)
    return pl.pallas_call(
        body, name="small_all_reduce",
        in_specs=[vm] * 5, out_specs=vm,
        out_shape=jax.ShapeDtypeStruct((NSMALL, D), F32),
        scratch_shapes=[pltpu.VMEM((NSMALL, D), F32), pltpu.VMEM((8, NSMALL, D), F32),
                        pltpu.SemaphoreType.DMA((7,)), pltpu.SemaphoreType.DMA((7,))],
    )(g_nw, dlb, g_hnw, g_wf, loss_vec)


def _adamw_math(w, g, m, v):
    m = B1 * m + (1.0 - B1) * g
    v = B2 * v + (1.0 - B2) * (g * g)
    m_hat = m / (1.0 - B1 ** STEP)
    v_hat = v / (1.0 - B2 ** STEP)
    return -LR * (m_hat / (jnp.sqrt(v_hat) + ADAM_EPS) + WD * w), m, v


def _adamw(w, g, m, v, name):
    r, c = w.shape
    tr = 64

    def body(w_ref, g_ref, m_ref, v_ref, d_ref, nm_ref, nv_ref):
        d_ref[...], nm_ref[...], nv_ref[...] = _adamw_math(w_ref[...], g_ref[...], m_ref[...], v_ref[...])

    blk = pl.BlockSpec((tr, c), lambda i: (i, 0))
    return pl.pallas_call(
        body, name=name, grid=(r // tr,), in_specs=[blk] * 4, out_specs=[blk] * 3,
        out_shape=[jax.ShapeDtypeStruct((r, c), F32)] * 3,
        compiler_params=_params(("parallel",)),
    )(w, g, m, v)


def _small_update(red, lbl, params):
    def body(red_ref, *refs):
        ins, outs = refs[:12], refs[12:]
        lb = _lower_bound(ins[3][...])
        dl0 = red_ref[1:2, :] * lb * (1.0 - lb)
        row = lax.broadcasted_iota(jnp.int32, (2, D), 0)
        grads = [red_ref[0:1, :], jnp.where(row == 0, dl0, -dl0), red_ref[2:3, 0:HK], red_ref[3:4, :]]
        for i, g in enumerate(grads):
            w, m, v = ins[3 * i][...], ins[3 * i + 1][...], ins[3 * i + 2][...]
            d, nm, nv = _adamw_math(w, g, m, v)
            outs[4 * i][...] = g
            outs[4 * i + 1][...] = d
            outs[4 * i + 2][...] = nm
            outs[4 * i + 3][...] = nv
        outs[16][...] = jnp.sum(red_ref[4:5, :], axis=1, keepdims=True)

    flat = [a for p in params for a in p]
    vm = pl.BlockSpec(memory_space=pltpu.VMEM)
    shapes = [jax.ShapeDtypeStruct(p[0].shape, F32) for p in params for _ in range(4)]
    return pl.pallas_call(
        body, name="small_update",
        in_specs=[vm] * 13, out_specs=[vm] * 17,
        out_shape=shapes + [jax.ShapeDtypeStruct((1, 1), F32)],
    )(red, *flat)


def kernel(x, positions, norm_w, w_in, lb_logits, hgrn_norm_w, w_branch_a, w_branch_b, w_out, final_norm_w, loss_target, m_norm_w, m_w_in, m_lb_logits, m_hgrn_norm_w, m_w_branch_a, m_w_branch_b, m_w_out, m_final_norm_w, v_norm_w, v_w_in, v_lb_logits, v_hgrn_norm_w, v_w_branch_a, v_w_branch_b, v_w_out, v_final_norm_w):
    big_w = [w_in[0], w_branch_a[0], w_branch_b[0], w_out[0]]
    big_m = [m_w_in[0], m_w_branch_a[0], m_w_branch_b[0], m_w_out[0]]
    big_v = [v_w_in[0], v_w_branch_a[0], v_w_branch_b[0], v_w_out[0]]
    shapes = [w.shape for w in big_w]
    wf = final_norm_w.reshape(1, D)

    full = _all_gather([w.astype(BF) for w in big_w])
    loc = _local_step(x[0], positions.reshape(T, 1), norm_w, lb_logits, hgrn_norm_w, wf, loss_target[0], *full)
    g_big = _reduce_scatter([loc["g_win"], loc["g_wa"], loc["g_wb"], loc["g_wout"]], shapes)
    red = _small_all_reduce(loc["g_nw"], loc["dlb"], loc["g_hnw"], loc["g_wf"], loc["loss_vec"])

    small = _small_update(red, lb_logits, [
        (norm_w, m_norm_w, v_norm_w), (lb_logits, m_lb_logits, v_lb_logits),
        (hgrn_norm_w, m_hgrn_norm_w, v_hgrn_norm_w),
        (wf, m_final_norm_w.reshape(1, D), v_final_norm_w.reshape(1, D))])
    loss = small[16].reshape(())
    sg, sd, sm, sv = ([small[4 * i + j] for i in range(4)] for j in range(4))
    for lst in (sg, sd, sm, sv):
        lst[3] = lst[3].reshape(D)
    upd = [_adamw(w, g, m, v, f"adamw_{a}") for a, (w, g, m, v) in enumerate(zip(big_w, g_big, big_m, big_v))]
    bg = [g[None] for g in g_big]
    bd, bm, bv = ([u[j][None] for u in upd] for j in range(3))

    def order(s, b):
        return [s[0], b[0], s[1], s[2], b[1], b[2], b[3], s[3]]

    return (loss, loc["gx"][None], *order(sg, bg), *order(sd, bd), *order(sm, bm), *order(sv, bv))
```

```python
import functools

import jax
import jax.numpy as jnp
from jax import lax
from jax.experimental import pallas as pl
from jax.experimental.pallas import tpu as pltpu

T = 2048
D = 1024
NIN = 11264
HEADS = 8
HK = 128
CH = 16
NCH = T // CH
ATT_GROUPS = ((128, 1), (512, 4), (2048, 16))
ATT_COL0 = 4096
AG_COL0 = 8704
GATE_COL0 = 9216
EPS = 1e-6
ROPE_THETA = 10000.0
LR, B1, B2, ADAM_EPS, WD, STEP = 0.001, 0.9, 0.999, 1e-08, 0.01, 10

F32 = jnp.float32
BF = jnp.bfloat16
VMEM_LIMIT = 56 * 1024 * 1024

_NN = (((1,), (0,)), ((), ()))
_NT = (((1,), (1,)), ((), ()))
_TN = (((0,), (0,)), ((), ()))


def _dot(a, b, dims=_NN):
    return lax.dot_general(a, b, dims, preferred_element_type=F32)


def _bdot(a, b, dims=_NN):
    return lax.dot_general(a.astype(BF), b.astype(BF), dims, preferred_element_type=F32)


def _sigmoid(x):
    return jax.nn.sigmoid(x)


def _params(sem=None):
    return pltpu.CompilerParams(dimension_semantics=sem, vmem_limit_bytes=VMEM_LIMIT)


def _matmul(a, b, *, ta=False, tb=False, out_dtype=F32, tm=512, tn=512, tk=None, name):
    m = a.shape[1] if ta else a.shape[0]
    kdim = a.shape[0] if ta else a.shape[1]
    n = b.shape[0] if tb else b.shape[1]
    tk = tk or kdim
    tm, tn = min(tm, m), min(tn, n)
    nm, nn, nk = m // tm, n // tn, kdim // tk
    dims = (((0 if ta else 1,), (1 if tb else 0,)), ((), ()))

    def body(a_ref, b_ref, o_ref, *scratch):
        prod = _bdot(a_ref[...], b_ref[...], dims)
        if nk == 1:
            o_ref[...] = prod.astype(out_dtype)
        else:
            acc = scratch[0]
            k = pl.program_id(2)

            @pl.when(k == 0)
            def _():
                acc[...] = prod

            @pl.when(k > 0)
            def _():
                acc[...] += prod

            @pl.when(k == nk - 1)
            def _():
                o_ref[...] = acc[...].astype(out_dtype)

    a_spec = pl.BlockSpec((tk, tm), lambda i, j, k: (k, i)) if ta else pl.BlockSpec((tm, tk), lambda i, j, k: (i, k))
    b_spec = pl.BlockSpec((tn, tk), lambda i, j, k: (j, k)) if tb else pl.BlockSpec((tk, tn), lambda i, j, k: (k, j))
    return pl.pallas_call(
        body, name=name, grid=(nm, nn, nk),
        in_specs=[a_spec, b_spec],
        out_specs=pl.BlockSpec((tm, tn), lambda i, j, k: (i, j)),
        out_shape=jax.ShapeDtypeStruct((m, n), out_dtype),
        scratch_shapes=[pltpu.VMEM((tm, tn), F32)] if nk > 1 else [],
        compiler_params=_params(("parallel", "parallel", "arbitrary")),
    )(a, b)


def _rmsnorm_fwd(x, w):
    tm = 256

    def body(x_ref, w_ref, h_ref):
        xv = x_ref[...]
        r = lax.rsqrt(jnp.mean(xv * xv, axis=-1, keepdims=True) + EPS)
        h_ref[...] = (xv * r * w_ref[...]).astype(BF)

    return pl.pallas_call(
        body, name="rmsnorm_fwd", grid=(T // tm,),
        in_specs=[pl.BlockSpec((tm, D), lambda i: (i, 0)), pl.BlockSpec((1, D), lambda i: (0, 0))],
        out_specs=pl.BlockSpec((tm, D), lambda i: (i, 0)),
        out_shape=jax.ShapeDtypeStruct((T, D), BF),
        compiler_params=_params(("parallel",)),
    )(x, w)


def _rmsnorm_bwd(x, dh, dout, w):
    tm = 256

    def body(x_ref, dh_ref, dout_ref, w_ref, gx_ref, gw_ref):
        @pl.when(pl.program_id(0) == 0)
        def _():
            gw_ref[...] = jnp.zeros_like(gw_ref)

        xv, dhv = x_ref[...], dh_ref[...]
        r = lax.rsqrt(jnp.mean(xv * xv, axis=-1, keepdims=True) + EPS)
        nrm = xv * r
        dn = dhv * w_ref[...]
        gw_ref[...] += jnp.sum(dhv * nrm, axis=0, keepdims=True)
        gx_ref[...] = dout_ref[...] + r * (dn - nrm * jnp.mean(dn * nrm, axis=-1, keepdims=True))

    row = pl.BlockSpec((tm, D), lambda i: (i, 0))
    vec = pl.BlockSpec((1, D), lambda i: (0, 0))
    return pl.pallas_call(
        body, name="rmsnorm_bwd", grid=(T // tm,),
        in_specs=[row, row, row, vec], out_specs=[row, vec],
        out_shape=[jax.ShapeDtypeStruct((T, D), F32), jax.ShapeDtypeStruct((1, D), F32)],
        compiler_params=_params(("arbitrary",)),
    )(x, dh, dout, w)


def _lower_bound(lbl):
    mx = jnp.max(lbl, axis=0, keepdims=True)
    e = jnp.exp(lbl - mx)
    return e[0:1] / jnp.sum(e, axis=0, keepdims=True)


def _cumsum_rows(g, rows):
    b = g
    sh = 1
    while sh < CH:
        b = b + jnp.where(rows >= sh, pltpu.roll(b, sh, axis=0), 0.0)
        sh *= 2
    return b


def _rev_cumsum_rows(g, rows):
    b = g
    sh = 1
    while sh < CH:
        b = b + jnp.where(rows < CH - sh, pltpu.roll(b, CH - sh, axis=0), 0.0)
        sh *= 2
    return b


def _hgrn_fwd(z, lbl, nw):
    def body(hq_ref, hf_ref, hi_ref, hg_ref, lbl_ref, nw_ref, oraw_ref, og_ref, sh_ref, st_ref):
        @pl.when(pl.program_id(0) == 0)
        def _():
            st_ref[...] = jnp.zeros_like(st_ref)

        lb_all = _lower_bound(lbl_ref[...])
        rows = lax.broadcasted_iota(jnp.int32, (CH, HK), 0)
        nwv = nw_ref[...]
        for h in range(HEADS):
            sl = slice(HK * h, HK * (h + 1))
            lb = lb_all[:, sl]
            hq, hf, v, hg = hq_ref[:, sl], hf_ref[:, sl], hi_ref[:, sl], hg_ref[:, sl]
            q = hq * _sigmoid(hq)
            f = lb + (1.0 - lb) * _sigmoid(hf)
            k = 1.0 - f
            b = _cumsum_rows(jnp.log(f), rows)
            st0 = st_ref[h]
            sh_ref[0, h] = st0
            o = _bdot(q * jnp.exp(b), st0, _NT)
            for s in range(CH):
                e_s = jnp.exp(jnp.where(rows >= s, b - b[s:s + 1], -jnp.inf))
                a = jnp.sum(q * e_s * k[s:s + 1], axis=1, keepdims=True)
                o = o + a * v[s:s + 1]
            bl = b[CH - 1:CH]
            st_ref[h] = st0 * jnp.exp(bl) + _bdot(v, k * jnp.exp(bl - b), _TN)
            oraw_ref[:, sl] = o
            nrm = o * lax.rsqrt(jnp.mean(o * o, axis=1, keepdims=True) + EPS)
            og_ref[:, sl] = (nrm * nwv * (hg * _sigmoid(hg))).astype(BF)

    zblk = lambda c: pl.BlockSpec((CH, D), lambda i, c=c: (i, c))
    return pl.pallas_call(
        body, name="hgrn_fwd", grid=(NCH,),
        in_specs=[zblk(0), zblk(1), zblk(2), zblk(3),
                  pl.BlockSpec((2, D), lambda i: (0, 0)), pl.BlockSpec((1, HK), lambda i: (0, 0))],
        out_specs=[pl.BlockSpec((CH, D), lambda i: (i, 0)), pl.BlockSpec((CH, D), lambda i: (i, 0)),
                   pl.BlockSpec((1, HEADS, HK, HK), lambda i: (i, 0, 0, 0))],
        out_shape=[jax.ShapeDtypeStruct((T, D), F32), jax.ShapeDtypeStruct((T, D), BF),
                   jax.ShapeDtypeStruct((NCH, HEADS, HK, HK), F32)],
        scratch_shapes=[pltpu.VMEM((HEADS, HK, HK), F32)],
        compiler_params=_params(("arbitrary",)),
    )(z, z, z, z, lbl, nw)


def _hgrn_bwd(z, lbl, nw, oraw, dog, shist):
    def body(hq_ref, hf_ref, hi_ref, hg_ref, lbl_ref, nw_ref, oraw_ref, dog_ref, sh_ref,
             dz_ref, dlb_ref, dnw_ref, dst_ref):
        @pl.when(pl.program_id(0) == 0)
        def _():
            dst_ref[...] = jnp.zeros_like(dst_ref)
            dlb_ref[...] = jnp.zeros_like(dlb_ref)
            dnw_ref[...] = jnp.zeros_like(dnw_ref)

        lb_all = _lower_bound(lbl_ref[...])
        rows = lax.broadcasted_iota(jnp.int32, (CH, HK), 0)
        rowc = lax.broadcasted_iota(jnp.int32, (CH, 1), 0)
        nwv = nw_ref[...]
        dnw = jnp.zeros((1, HK), F32)
        for h in range(HEADS):
            sl = slice(HK * h, HK * (h + 1))
            lb = lb_all[:, sl]
            hq, hf, v, hg = hq_ref[:, sl], hf_ref[:, sl], hi_ref[:, sl], hg_ref[:, sl]
            o, dg_out = oraw_ref[:, sl], dog_ref[:, sl]
            sg = _sigmoid(hg)
            sil = hg * sg
            r = lax.rsqrt(jnp.mean(o * o, axis=1, keepdims=True) + EPS)
            nrm = o * r
            d_hg = dg_out * (nrm * nwv) * (sg * (1.0 + hg * (1.0 - sg)))
            dn = dg_out * nwv * sil
            dnw = dnw + jnp.sum(dg_out * nrm * sil, axis=0, keepdims=True)
            do = r * (dn - nrm * jnp.mean(dn * nrm, axis=1, keepdims=True))
            sq = _sigmoid(hq)
            q = hq * sq
            sig = _sigmoid(hf)
            f = lb + (1.0 - lb) * sig
            k = 1.0 - f
            b = _cumsum_rows(jnp.log(f), rows)
            eb = jnp.exp(b)
            qe = q * eb
            bl = b[CH - 1:CH]
            ebl = jnp.exp(bl)
            kdec = jnp.exp(bl - b)
            ke = k * kdec
            st0 = sh_ref[0, h]
            dst1 = dst_ref[h]
            dqe = _bdot(do, st0)
            dst_ref[h] = dst1 * ebl + _bdot(do, qe, _TN)
            dq = dqe * eb
            db = dqe * qe
            dke = _bdot(v, dst1)
            dv = _bdot(ke, dst1, _NT)
            dk = dke * kdec
            rr = dke * ke
            db = db - rr
            db_last = jnp.sum(rr, axis=0, keepdims=True) + ebl * jnp.sum(dst1 * st0, axis=0, keepdims=True)
            for s in range(CH):
                one = (rowc == s).astype(F32)
                ks, vs = k[s:s + 1], v[s:s + 1]
                e_s = jnp.exp(jnp.where(rows >= s, b - b[s:s + 1], -jnp.inf))
                qes = q * e_s
                w = qes * ks
                a = jnp.sum(w, axis=1, keepdims=True)
                da = jnp.sum(do * vs, axis=1, keepdims=True)
                dv = dv + one * jnp.sum(a * do, axis=0, keepdims=True)
                dq = dq + da * e_s * ks
                dk = dk + one * jnp.sum(da * qes, axis=0, keepdims=True)
                u = da * w
                db = db + u - one * jnp.sum(u, axis=0, keepdims=True)
            db = db + (rowc == CH - 1).astype(F32) * db_last
            dgl = _rev_cumsum_rows(db, rows)
            df = dgl / f - dk
            dlb_ref[:, sl] += jnp.sum(df * (1.0 - sig), axis=0, keepdims=True)
            dz_ref[:, sl] = (dq * (sq * (1.0 + hq * (1.0 - sq)))).astype(BF)
            dz_ref[:, D + HK * h:D + HK * (h + 1)] = (df * (1.0 - lb) * sig * (1.0 - sig)).astype(BF)
            dz_ref[:, 2 * D + HK * h:2 * D + HK * (h + 1)] = dv.astype(BF)
            dz_ref[:, 3 * D + HK * h:3 * D + HK * (h + 1)] = d_hg.astype(BF)
        dnw_ref[...] += dnw

    rev = lambda i: NCH - 1 - i
    zblk = lambda c: pl.BlockSpec((CH, D), lambda i, c=c: (rev(i), c))
    return pl.pallas_call(
        body, name="hgrn_bwd", grid=(NCH,),
        in_specs=[zblk(0), zblk(1), zblk(2), zblk(3),
                  pl.BlockSpec((2, D), lambda i: (0, 0)), pl.BlockSpec((1, HK), lambda i: (0, 0)),
                  pl.BlockSpec((CH, D), lambda i: (rev(i), 0)), pl.BlockSpec((CH, D), lambda i: (rev(i), 0)),
                  pl.BlockSpec((1, HEADS, HK, HK), lambda i: (rev(i), 0, 0, 0))],
        out_specs=[pl.BlockSpec((CH, 4 * D), lambda i: (rev(i), 0)),
                   pl.BlockSpec((1, D), lambda i: (0, 0)), pl.BlockSpec((1, HK), lambda i: (0, 0))],
        out_shape=[jax.ShapeDtypeStruct((T, 4 * D), BF), jax.ShapeDtypeStruct((1, D), F32),
                   jax.ShapeDtypeStruct((1, HK), F32)],
        scratch_shapes=[pltpu.VMEM((HEADS, HK, HK), F32)],
        compiler_params=_params(("arbitrary",)),
    )(z, z, z, z, lbl, nw, oraw, dog, shist)


BLK = 128
NBLK = T // BLK


def _rot_half(x):
    lane = lax.broadcasted_iota(jnp.int32, x.shape, 1)
    return jnp.where((lane % 64) < 32, -pltpu.roll(x, 96, axis=1), pltpu.roll(x, 32, axis=1))


def _head_masks():
    lane = lax.broadcasted_iota(jnp.int32, (1, BLK), 1)
    return [(lane < 64).astype(F32), (lane >= 64).astype(F32)]


def _pieces(dil):
    m = T // dil
    out = []
    for r in range(dil):
        for j in range(m // BLK):
            start = r + dil * BLK * j
            rows = pl.ds(start, BLK, stride=dil) if dil > 1 else pl.ds(start, BLK)
            out.append((rows, r * m + BLK * j))
    return out


def _rope_tables(pos_ref, invf_ref, cos_ref, sin_ref):
    for c in range(T // 256):
        rows = pl.ds(256 * c, 256)
        ang = pos_ref[rows, :].astype(F32) * invf_ref[...]
        cos_ref[rows, :] = jnp.cos(ang)
        sin_ref[rows, :] = jnp.sin(ang)


def _block_scores(qh, kc, kp, has_prev):
    ii = lax.broadcasted_iota(jnp.int32, (BLK, BLK), 0)
    jj = lax.broadcasted_iota(jnp.int32, (BLK, BLK), 1)
    s_c = jnp.where(jj <= ii, _dot(qh, kc, _NT) * 0.125, -jnp.inf)
    s_p = jnp.where((jj >= ii) & has_prev, _dot(qh, kp, _NT) * 0.125, -jnp.inf)
    return s_c, s_p


def _attn_fwd(z, pos, invf):
    def body(q_ref, k_ref, v_ref, ag_ref, pos_ref, invf_ref, ob_ref, opre_ref, lse_ref,
             cos_ref, sin_ref, qr_ref, kr_ref, vr_ref, og_ref, lg_ref, otok_ref, ltok_ref):
        g = pl.program_id(1)
        masks = _head_masks()

        @pl.when(g == 0)
        def _():
            _rope_tables(pos_ref, invf_ref, cos_ref, sin_ref)

        def group(gi):
            dil = ATT_GROUPS[gi][1]
            nblk = (T // dil) // BLK
            for rows, dst in _pieces(dil):
                c, s = cos_ref[rows, :], sin_ref[rows, :]
                qv, kv = q_ref[rows, :], k_ref[rows, :]
                qr_ref[pl.ds(dst, BLK), :] = qv * c + _rot_half(qv) * s
                kr_ref[pl.ds(dst, BLK), :] = kv * c + _rot_half(kv) * s
                vr_ref[pl.ds(dst, BLK), :] = v_ref[rows, :]

            def blk(bi, carry):
                off = pl.multiple_of(bi * BLK, BLK)
                offp = pl.multiple_of(jnp.maximum(bi - 1, 0) * BLK, BLK)
                has_prev = (bi % nblk) != 0
                qb = qr_ref[pl.ds(off, BLK), :]
                kc, vc = kr_ref[pl.ds(off, BLK), :].astype(BF), vr_ref[pl.ds(off, BLK), :].astype(BF)
                kp, vp = kr_ref[pl.ds(offp, BLK), :].astype(BF), vr_ref[pl.ds(offp, BLK), :].astype(BF)
                o_acc = jnp.zeros((BLK, BLK), F32)
                l_acc = jnp.zeros((BLK, BLK), F32)
                for mh in masks:
                    s_c, s_p = _block_scores((qb * mh).astype(BF), kc, kp, has_prev)
                    mx = jnp.maximum(jnp.max(s_c, axis=1, keepdims=True), jnp.max(s_p, axis=1, keepdims=True))
                    p_c, p_p = jnp.exp(s_c - mx), jnp.exp(s_p - mx)
                    den = jnp.sum(p_c, axis=1, keepdims=True) + jnp.sum(p_p, axis=1, keepdims=True)
                    oh = _dot(p_c.astype(BF), vc) + _dot(p_p.astype(BF), vp)
                    o_acc = o_acc + (oh / den) * mh
                    l_acc = l_acc + (mx + jnp.log(den)) * mh
                og_ref[pl.ds(off, BLK), :] = o_acc
                lg_ref[pl.ds(off, BLK), :] = l_acc
                return carry

            lax.fori_loop(0, NBLK, blk, 0)
            for rows, src in _pieces(dil):
                otok_ref[gi, rows, :] = og_ref[pl.ds(src, BLK), :]
                ltok_ref[gi, rows, :] = lg_ref[pl.ds(src, BLK), :]

        for gi in range(3):
            pl.when(g == gi)(functools.partial(group, gi))

        @pl.when(g == 2)
        def _():
            for c in range(T // 256):
                rows = pl.ds(256 * c, 256)
                l0, l1, l2 = ltok_ref[0, rows, :], ltok_ref[1, rows, :], ltok_ref[2, rows, :]
                mx = jnp.maximum(jnp.maximum(l0, l1), l2)
                lse = mx + jnp.log(jnp.exp(l0 - mx) + jnp.exp(l1 - mx) + jnp.exp(l2 - mx))
                o = (jnp.exp(l0 - lse) * otok_ref[0, rows, :] + jnp.exp(l1 - lse) * otok_ref[1, rows, :]
                     + jnp.exp(l2 - lse) * otok_ref[2, rows, :])
                ag = ag_ref[rows, :]
                opre_ref[rows, :] = o
                lse_ref[rows, :] = lse
                ob_ref[rows, :] = (o * (ag * _sigmoid(ag))).astype(BF)

    c0 = ATT_COL0 // BLK
    zspec = lambda part: pl.BlockSpec((T, BLK), lambda p, g, part=part: (0, c0 + 12 * part + 4 * g + p))
    outspec = pl.BlockSpec((T, BLK), lambda p, g: (0, p))
    big = lambda: pltpu.VMEM((T, BLK), F32)
    return pl.pallas_call(
        body, name="attn_fwd", grid=(4, 3),
        in_specs=[zspec(0), zspec(1), zspec(2),
                  pl.BlockSpec((T, BLK), lambda p, g: (0, AG_COL0 // BLK + p)),
                  pl.BlockSpec((T, 1), lambda p, g: (0, 0)), pl.BlockSpec((1, BLK), lambda p, g: (0, 0))],
        out_specs=[outspec, outspec, outspec],
        out_shape=[jax.ShapeDtypeStruct((T, 512), BF), jax.ShapeDtypeStruct((T, 512), F32),
                   jax.ShapeDtypeStruct((T, 512), F32)],
        scratch_shapes=[big(), big(), big(), big(), big(), big(), big(),
                        pltpu.VMEM((3, T, BLK), F32), pltpu.VMEM((3, T, BLK), F32)],
        compiler_params=_params(("parallel", "arbitrary")),
    )(z, z, z, z, pos, invf)


def _attn_bwd(z, pos, invf, opre, lse, dob):
    def body(q_ref, k_ref, v_ref, ag_ref, pos_ref, invf_ref, o_ref, lse_ref, dob_ref,
             dq_ref, dk_ref, dv_ref, dag_ref,
             cos_ref, sin_ref, qr_ref, kr_ref, vr_ref, dor_ref, lr_ref, dr_ref, dqr_ref, dkr_ref, dvr_ref):
        g = pl.program_id(1)
        masks = _head_masks()

        @pl.when(g == 0)
        def _():
            _rope_tables(pos_ref, invf_ref, cos_ref, sin_ref)
            for c in range(T // 256):
                rows = pl.ds(256 * c, 256)
                ag = ag_ref[rows, :]
                sg = _sigmoid(ag)
                dag_ref[rows, :] = (dob_ref[rows, :] * o_ref[rows, :] * (sg * (1.0 + ag * (1.0 - sg)))).astype(BF)

        def group(gi):
            dil = ATT_GROUPS[gi][1]
            nblk = (T // dil) // BLK
            for rows, dst in _pieces(dil):
                drows = pl.ds(dst, BLK)
                c, s = cos_ref[rows, :], sin_ref[rows, :]
                qv, kv = q_ref[rows, :], k_ref[rows, :]
                qr_ref[drows, :] = qv * c + _rot_half(qv) * s
                kr_ref[drows, :] = kv * c + _rot_half(kv) * s
                vr_ref[drows, :] = v_ref[rows, :]
                ag = ag_ref[rows, :]
                do = dob_ref[rows, :] * (ag * _sigmoid(ag))
                prod = do * o_ref[rows, :]
                dor_ref[drows, :] = do
                lr_ref[drows, :] = lse_ref[rows, :]
                dr_ref[drows, :] = (jnp.sum(prod * masks[0], axis=1, keepdims=True) * masks[0]
                                    + jnp.sum(prod * masks[1], axis=1, keepdims=True) * masks[1])
            dkr_ref[...] = jnp.zeros_like(dkr_ref)
            dvr_ref[...] = jnp.zeros_like(dvr_ref)

            def blk(bi, carry):
                off = pl.multiple_of(bi * BLK, BLK)
                offp = pl.multiple_of(jnp.maximum(bi - 1, 0) * BLK, BLK)
                cur, prev = pl.ds(off, BLK), pl.ds(offp, BLK)
                has_prev = (bi % nblk) != 0
                qb, dob_b = qr_ref[cur, :], dor_ref[cur, :]
                lb, db = lr_ref[cur, :], dr_ref[cur, :]
                kc, vc = kr_ref[cur, :].astype(BF), vr_ref[cur, :].astype(BF)
                kp, vp = kr_ref[prev, :].astype(BF), vr_ref[prev, :].astype(BF)
                dq = jnp.zeros((BLK, BLK), F32)
                dkc, dkp, dvc, dvp = dq, dq, dq, dq
                for mh in masks:
                    qh, doh = (qb * mh).astype(BF), (dob_b * mh).astype(BF)
                    lh = jnp.sum(lb * mh, axis=1, keepdims=True) * (1.0 / 64)
                    dh = jnp.sum(db * mh, axis=1, keepdims=True) * (1.0 / 64)
                    s_c, s_p = _block_scores(qh, kc, kp, has_prev)
                    p_c, p_p = jnp.exp(s_c - lh), jnp.exp(s_p - lh)
                    ds_c = (p_c * (_dot(doh, vc, _NT) - dh) * 0.125).astype(BF)
                    ds_p = (p_p * (_dot(doh, vp, _NT) - dh) * 0.125).astype(BF)
                    dq = dq + (_dot(ds_c, kc) + _dot(ds_p, kp)) * mh
                    dkc = dkc + _dot(ds_c, qh, _TN)
                    dkp = dkp + _dot(ds_p, qh, _TN)
                    dvc = dvc + _dot(p_c.astype(BF), doh, _TN)
                    dvp = dvp + _dot(p_p.astype(BF), doh, _TN)
                dqr_ref[cur, :] = dq
                dkr_ref[prev, :] += dkp
                dvr_ref[prev, :] += dvp
                dkr_ref[cur, :] += dkc
                dvr_ref[cur, :] += dvc
                return carry

            lax.fori_loop(0, NBLK, blk, 0)
            for rows, src in _pieces(dil):
                srows = pl.ds(src, BLK)
                c, s = cos_ref[rows, :], sin_ref[rows, :]
                dq, dk = dqr_ref[srows, :], dkr_ref[srows, :]
                qr_ref[rows, :] = dq * c - _rot_half(dq * s)
                kr_ref[rows, :] = dk * c - _rot_half(dk * s)
                vr_ref[rows, :] = dvr_ref[srows, :]
            for c in range(T // 256):
                rows = pl.ds(256 * c, 256)
                dq_ref[rows, :] = qr_ref[rows, :].astype(BF)
                dk_ref[rows, :] = kr_ref[rows, :].astype(BF)
                dv_ref[rows, :] = vr_ref[rows, :].astype(BF)

        for gi in range(3):
            pl.when(g == gi)(functools.partial(group, gi))

    c0 = ATT_COL0 // BLK
    zspec = lambda part: pl.BlockSpec((T, BLK), lambda p, g, part=part: (0, c0 + 12 * part + 4 * g + p))
    pspec = pl.BlockSpec((T, BLK), lambda p, g: (0, p))
    gspec = pl.BlockSpec((T, BLK), lambda p, g: (0, 4 * g + p))
    big = lambda: pltpu.VMEM((T, BLK), F32)
    return pl.pallas_call(
        body, name="attn_bwd", grid=(4, 3),
        in_specs=[zspec(0), zspec(1), zspec(2),
                  pl.BlockSpec((T, BLK), lambda p, g: (0, AG_COL0 // BLK + p)),
                  pl.BlockSpec((T, 1), lambda p, g: (0, 0)), pl.BlockSpec((1, BLK), lambda p, g: (0, 0)),
                  pspec, pspec, pspec],
        out_specs=[gspec, gspec, gspec, pspec],
        out_shape=[jax.ShapeDtypeStruct((T, 1536), BF), jax.ShapeDtypeStruct((T, 1536), BF),
                   jax.ShapeDtypeStruct((T, 1536), BF), jax.ShapeDtypeStruct((T, 512), BF)],
        scratch_shapes=[big() for _ in range(11)],
        compiler_params=_params(("parallel", "arbitrary")),
    )(z, z, z, z, pos, invf, opre, lse, dob)


def _merge_fwd(ya, yb, z):
    tm = 256

    def body(ya_ref, yb_ref, ga_ref, gb_ref, m_ref):
        m_ref[...] = (_sigmoid(ga_ref[...]) * ya_ref[...] + _sigmoid(gb_ref[...]) * yb_ref[...]).astype(BF)

    row = pl.BlockSpec((tm, D), lambda i: (i, 0))
    return pl.pallas_call(
        body, name="merge_fwd", grid=(T // tm,),
        in_specs=[row, row, pl.BlockSpec((tm, D), lambda i: (i, GATE_COL0 // D)),
                  pl.BlockSpec((tm, D), lambda i: (i, GATE_COL0 // D + 1))],
        out_specs=row, out_shape=jax.ShapeDtypeStruct((T, D), BF),
        compiler_params=_params(("parallel",)),
    )(ya, yb, z, z)


def _out_loss(merged, w_out, x, tgt, wf):
    tm = 256

    def body(m_ref, w_ref, x_ref, t_ref, wf_ref, dout_ref, loss_ref, gwf_ref):
        @pl.when(pl.program_id(0) == 0)
        def _():
            loss_ref[...] = jnp.zeros_like(loss_ref)
            gwf_ref[...] = jnp.zeros_like(gwf_ref)

        out = x_ref[...] + _dot(m_ref[...], w_ref[...])
        r = lax.rsqrt(jnp.mean(out * out, axis=-1, keepdims=True) + EPS)
        yh = out * r
        wfv = wf_ref[...]
        err = yh * wfv - t_ref[...]
        loss_ref[...] += jnp.sum(err * err, axis=0, keepdims=True) * (0.5 / D)
        dy = err * (1.0 / D)
        gwf_ref[...] += jnp.sum(dy * yh, axis=0, keepdims=True)
        dyh = dy * wfv
        dout_ref[...] = r * (dyh - yh * jnp.mean(dyh * yh, axis=-1, keepdims=True))

    row = pl.BlockSpec((tm, D), lambda i: (i, 0))
    vec = pl.BlockSpec((1, D), lambda i: (0, 0))
    return pl.pallas_call(
        body, name="out_loss", grid=(T // tm,),
        in_specs=[row, pl.BlockSpec((D, D), lambda i: (0, 0)), row, row, vec],
        out_specs=[row, vec, vec],
        out_shape=[jax.ShapeDtypeStruct((T, D), F32), jax.ShapeDtypeStruct((1, D), F32),
                   jax.ShapeDtypeStruct((1, D), F32)],
        compiler_params=_params(("arbitrary",)),
    )(merged, w_out, x, tgt, wf)


def _merge_bwd(dm, ya, yb, z):
    tm = 256

    def body(dm_ref, ya_ref, yb_ref, ga_ref, gb_ref, dya_ref, dyb_ref, dg_ref):
        dmv = dm_ref[...]
        sa, sb = _sigmoid(ga_ref[...]), _sigmoid(gb_ref[...])
        dya_ref[...] = (sa * dmv).astype(BF)
        dyb_ref[...] = (sb * dmv).astype(BF)
        dg_ref[:, :D] = (dmv * ya_ref[...] * sa * (1.0 - sa)).astype(BF)
        dg_ref[:, D:] = (dmv * yb_ref[...] * sb * (1.0 - sb)).astype(BF)

    row = pl.BlockSpec((tm, D), lambda i: (i, 0))
    return pl.pallas_call(
        body, name="merge_bwd", grid=(T // tm,),
        in_specs=[row, row, row, pl.BlockSpec((tm, D), lambda i: (i, GATE_COL0 // D)),
                  pl.BlockSpec((tm, D), lambda i: (i, GATE_COL0 // D + 1))],
        out_specs=[row, row, pl.BlockSpec((tm, 2 * D), lambda i: (i, 0))],
        out_shape=[jax.ShapeDtypeStruct((T, D), BF), jax.ShapeDtypeStruct((T, D), BF),
                   jax.ShapeDtypeStruct((T, 2 * D), BF)],
        compiler_params=_params(("parallel",)),
    )(dm, ya, yb, z, z)


def _rope_inv_freq():
    inv = ROPE_THETA ** (-jnp.arange(0, 64, 2, dtype=F32) / 64)
    return jnp.tile(inv, 4).reshape(1, BLK)


def _local_step(x, pos, norm_w, lbl, hnw, wf, tgt, w_in, w_a, w_b, w_out):
    invf = _rope_inv_freq()
    h = _rmsnorm_fwd(x, norm_w)
    z = _matmul(h, w_in, tm=1024, tn=512, name="z_proj")
    oraw, og, shist = _hgrn_fwd(z, lbl, hnw)
    ob, opre, lse = _attn_fwd(z, pos, invf)
    ya = _matmul(og, w_a, tm=1024, tn=512, name="ya_proj")
    yb = _matmul(ob, w_b, tm=1024, tn=512, name="yb_proj")
    merged = _merge_fwd(ya, yb, z)
    dout, loss_vec, g_wf = _out_loss(merged, w_out, x, tgt, wf)

    dm = _matmul(dout, w_out, tb=True, tm=1024, tn=512, name="d_merged")
    g_wout = _matmul(merged, dout, ta=True, out_dtype=BF, tm=512, tn=1024, name="g_wout")
    dya, dyb, dgates = _merge_bwd(dm, ya, yb, z)
    dog = _matmul(dya, w_a, tb=True, tm=1024, tn=512, name="d_og")
    g_wa = _matmul(og, dya, ta=True, out_dtype=BF, tm=512, tn=1024, name="g_wa")
    dob = _matmul(dyb, w_b, tb=True, tm=1024, tn=512, name="d_ob")
    g_wb = _matmul(ob, dyb, ta=True, out_dtype=BF, tm=512, tn=1024, name="g_wb")
    dz_h, dlb, g_hnw = _hgrn_bwd(z, lbl, hnw, oraw, dog, shist)
    dq, dk, dv, dag = _attn_bwd(z, pos, invf, opre, lse, dob)
    dz = jnp.concatenate([dz_h, dq, dk, dv, dag, dgates], axis=1)
    g_win = _matmul(h, dz, ta=True, out_dtype=BF, tm=1024, tn=512, name="g_win")
    dh = _matmul(dz, w_in, tb=True, tm=1024, tn=1024, tk=1408, name="d_h")
    gx, g_nw = _rmsnorm_bwd(x, dh, dout, norm_w)
    return dict(loss_vec=loss_vec, gx=gx, g_nw=g_nw, dlb=dlb, g_hnw=g_hnw, g_wf=g_wf,
                g_win=g_win, g_wa=g_wa, g_wb=g_wb, g_wout=g_wout)


MESH = pl.DeviceIdType.MESH
HBM = pl.BlockSpec(memory_space=pl.ANY)
WEIGHT_AXES = (1, 0, 1, 0)


def _place():
    x, y, c = lax.axis_index("x"), lax.axis_index("y"), lax.axis_index("c")
    chips = [(1 - x, y), (x, 1 - y), (1 - x, 1 - y)]
    return x, y, c, chips


def _block_half(ref, shard_shape, axis, j, half):
    r, c = shard_shape
    hr = r // 2
    if axis == 0:
        return ref.at[pl.ds(pl.multiple_of(j * r + half * hr, 16), hr), :]
    return ref.at[pl.ds(pl.multiple_of(half * hr, 16), hr), pl.ds(pl.multiple_of(j * c, 128), c)]


def _all_gather(shards):
    n = len(shards)
    shapes = [s.shape for s in shards]

    def body(*refs):
        ins, outs = refs[:n], refs[n:2 * n]
        send1, recv1, send2, recv2, send0, recv0 = refs[2 * n:]
        x, y, c, chips = _place()
        me = 2 * x + y
        started, own = [], []
        for a in range(n):
            r, cc = shapes[a]
            ax = WEIGHT_AXES[a]
            mine = (outs[a].at[pl.ds(pl.multiple_of(me * r, 16), r), :] if ax == 0
                    else outs[a].at[:, pl.ds(pl.multiple_of(me * cc, 128), cc)])
            swap = pltpu.make_async_remote_copy(
                src_ref=ins[a], dst_ref=mine, send_sem=send0.at[a], recv_sem=recv0.at[a],
                device_id=(x, y, 1 - c), device_id_type=MESH)
            swap.start()
            own.append(swap)
            src = ins[a].at[pl.ds(pl.multiple_of(c * (r // 2), 16), r // 2), :]
            for k, (px, py) in enumerate(chips):
                cp = pltpu.make_async_remote_copy(
                    src_ref=src, dst_ref=_block_half(outs[a], shapes[a], ax, me, c),
                    send_sem=send1.at[a, k], recv_sem=recv1.at[a, k], device_id=(px, py, c), device_id_type=MESH)
                cp.start()
                started.append(cp)
        for a in range(n):
            for k, (px, py) in enumerate(chips):
                reg = _block_half(outs[a], shapes[a], WEIGHT_AXES[a], 2 * px + py, c)
                pltpu.make_async_remote_copy(
                    src_ref=reg, dst_ref=reg, send_sem=send1.at[a, k], recv_sem=recv1.at[a, k],
                    device_id=(px, py, c), device_id_type=MESH).wait_recv()
                fw = pltpu.make_async_remote_copy(
                    src_ref=reg, dst_ref=reg, send_sem=send2.at[a, k], recv_sem=recv2.at[a, k],
                    device_id=(x, y, 1 - c), device_id_type=MESH)
                fw.start()
                started.append(fw)
        for a in range(n):
            for k, (px, py) in enumerate(chips):
                reg = _block_half(outs[a], shapes[a], WEIGHT_AXES[a], 2 * px + py, 1 - c)
                pltpu.make_async_remote_copy(
                    src_ref=reg, dst_ref=reg, send_sem=send2.at[a, k], recv_sem=recv2.at[a, k],
                    device_id=(x, y, 1 - c), device_id_type=MESH).wait_recv()
        for cp in started:
            cp.wait_send()
        for cp in own:
            cp.wait()

    full = [(4 * r, c) if ax == 0 else (r, 4 * c) for (r, c), ax in zip(shapes, WEIGHT_AXES)]
    return pl.pallas_call(
        body, name="weights_all_gather",
        in_specs=[HBM] * n, out_specs=[HBM] * n,
        out_shape=[jax.ShapeDtypeStruct(f, BF) for f in full],
        scratch_shapes=[pltpu.SemaphoreType.DMA((n, 3)), pltpu.SemaphoreType.DMA((n, 3)),
                        pltpu.SemaphoreType.DMA((n, 3)), pltpu.SemaphoreType.DMA((n, 3)),
                        pltpu.SemaphoreType.DMA((n,)), pltpu.SemaphoreType.DMA((n,))],
    )(*shards)


def _as3d(g, shard_shape, axis):
    r, c = shard_shape
    return g.reshape(4, r, c) if axis == 0 else g.reshape(1, r, 4 * c)


def _half_rows(ref3, hr, half):
    return ref3.at[:, pl.ds(pl.multiple_of(half * hr, 16), hr), :]


def _rs_pair_exchange(g3s):
    n = len(g3s)

    def body(*refs):
        ins, outs = refs[:n], refs[n:2 * n]
        send, recv = refs[2 * n:]
        x, y, c, _ = _place()
        cps = []
        for a in range(n):
            hr = g3s[a].shape[1] // 2
            cp = pltpu.make_async_remote_copy(
                src_ref=_half_rows(ins[a], hr, 1 - c), dst_ref=outs[a],
                send_sem=send.at[a], recv_sem=recv.at[a], device_id=(x, y, 1 - c), device_id_type=MESH)
            cp.start()
            cps.append(cp)
        for cp in cps:
            cp.wait()

    return pl.pallas_call(
        body, name="grads_pair_exchange",
        in_specs=[HBM] * n, out_specs=[HBM] * n,
        out_shape=[jax.ShapeDtypeStruct((g.shape[0], g.shape[1] // 2, g.shape[2]), BF) for g in g3s],
        scratch_shapes=[pltpu.SemaphoreType.DMA((n,)), pltpu.SemaphoreType.DMA((n,))],
    )(*g3s)


def _pair_sum(g3, land, cidx, name):
    nb, r, w = g3.shape
    hr = r // 2
    tr = 64

    def body(c_ref, g_ref, l_ref, o_ref):
        o_ref[...] = (g_ref[...].astype(F32) + l_ref[...].astype(F32)).astype(BF)

    blk = (nb, tr, w)
    return pl.pallas_call(
        body, name=name,
        grid_spec=pltpu.PrefetchScalarGridSpec(
            num_scalar_prefetch=1, grid=(hr // tr,),
            in_specs=[pl.BlockSpec(blk, lambda i, c: (0, c[0] * (hr // tr) + i, 0)),
                      pl.BlockSpec(blk, lambda i, c: (0, i, 0))],
            out_specs=pl.BlockSpec(blk, lambda i, c: (0, i, 0))),
        out_shape=jax.ShapeDtypeStruct((nb, hr, w), BF),
        compiler_params=_params(("parallel",)),
    )(cidx, g3, land)


def _rs_chip_exchange(p3s, shapes):
    n = len(p3s)

    def body(*refs):
        ins, outs = refs[:n], refs[n:2 * n]
        send, recv = refs[2 * n:]
        x, y, c, chips = _place()
        cps = []
        for a in range(n):
            r, cc = shapes[a]
            for k, (px, py) in enumerate(chips):
                j = 2 * px + py
                src = ins[a].at[j] if WEIGHT_AXES[a] == 0 else ins[a].at[0, :, pl.ds(pl.multiple_of(j * cc, 128), cc)]
                cp = pltpu.make_async_remote_copy(
                    src_ref=src, dst_ref=outs[a].at[k], send_sem=send.at[a, k], recv_sem=recv.at[a, k],
                    device_id=(px, py, c), device_id_type=MESH)
                cp.start()
                cps.append(cp)
        for cp in cps:
            cp.wait()

    return pl.pallas_call(
        body, name="grads_chip_exchange",
        in_specs=[HBM] * n, out_specs=[HBM] * n,
        out_shape=[jax.ShapeDtypeStruct((3, r // 2, c), BF) for r, c in shapes],
        scratch_shapes=[pltpu.SemaphoreType.DMA((n, 3)), pltpu.SemaphoreType.DMA((n, 3))],
    )(*p3s)


def _chip_sum(p3, land, shard_shape, axis, idx, name):
    r, c = shard_shape
    hr = r // 2
    tr = 64
    nt = hr // tr

    def body(idx_ref, p_ref, l_ref, o_ref):
        acc = p_ref[...].astype(F32)
        for k in range(3):
            acc = acc + l_ref[k].astype(F32)
        o_ref[...] = acc

    own = (pl.BlockSpec((None, tr, c), lambda i, idx: (idx[0], i, 0)) if axis == 0
           else pl.BlockSpec((None, tr, c), lambda i, idx: (0, i, idx[0])))
    return pl.pallas_call(
        body, name=name,
        grid_spec=pltpu.PrefetchScalarGridSpec(
            num_scalar_prefetch=1, grid=(nt,),
            in_specs=[own, pl.BlockSpec((3, tr, c), lambda i, idx: (0, i, 0))],
            out_specs=pl.BlockSpec((tr, c), lambda i, idx: (idx[1] * nt + i, 0))),
        out_shape=jax.ShapeDtypeStruct((r, c), F32),
        compiler_params=_params(("parallel",)),
    )(idx, p3, land)


def _rs_pair_gather(fulls):
    n = len(fulls)

    def body(*refs):
        ins, outs = refs[:n], refs[n:2 * n]
        send, recv = refs[2 * n:]
        x, y, c, _ = _place()
        cps = []
        for a in range(n):
            hr = fulls[a].shape[0] // 2
            rows = pl.ds(pl.multiple_of(c * hr, 8), hr)
            cp = pltpu.make_async_remote_copy(
                src_ref=ins[a].at[rows, :], dst_ref=outs[a].at[rows, :], send_sem=send.at[a], recv_sem=recv.at[a],
                device_id=(x, y, 1 - c), device_id_type=MESH)
            cp.start()
            cps.append(cp)
        for a, cp in enumerate(cps):
            cp.wait_send()
            hr = fulls[a].shape[0] // 2
            other = pl.ds(pl.multiple_of((1 - c) * hr, 8), hr)
            pltpu.make_async_remote_copy(
                src_ref=ins[a].at[other, :], dst_ref=outs[a].at[other, :], send_sem=send.at[a], recv_sem=recv.at[a],
                device_id=(x, y, 1 - c), device_id_type=MESH).wait_recv()

    return pl.pallas_call(
        body, name="grads_pair_gather",
        in_specs=[HBM] * n, out_specs=[HBM] * n,
        out_shape=[jax.ShapeDtypeStruct(f.shape, F32) for f in fulls],
        input_output_aliases={a: a for a in range(n)},
        scratch_shapes=[pltpu.SemaphoreType.DMA((n,)), pltpu.SemaphoreType.DMA((n,))],
    )(*fulls)


def _reduce_scatter(grads, shapes):
    x, y, c = lax.axis_index("x"), lax.axis_index("y"), lax.axis_index("c")
    cidx = jnp.reshape(c, (1,)).astype(jnp.int32)
    idx = jnp.stack([2 * x + y, c]).astype(jnp.int32)
    g3s = [_as3d(g, s, ax) for g, s, ax in zip(grads, shapes, WEIGHT_AXES)]
    lands = _rs_pair_exchange(g3s)
    p3s = [_pair_sum(g3, l, cidx, f"pair_sum_{a}") for a, (g3, l) in enumerate(zip(g3s, lands))]
    lands2 = _rs_chip_exchange(p3s, shapes)
    fulls = [_chip_sum(p3, l2, s, ax, idx, f"chip_sum_{a}")
             for a, (p3, l2, s, ax) in enumerate(zip(p3s, lands2, shapes, WEIGHT_AXES))]
    return _rs_pair_gather(fulls)


NSMALL = 8


def _small_all_reduce(g_nw, dlb, g_hnw, g_wf, loss_vec):
    def body(nw_ref, lb_ref, hn_ref, wf_ref, ls_ref, out_ref, pack_ref, buf_ref, send, recv):
        x, y, c = lax.axis_index("x"), lax.axis_index("y"), lax.axis_index("c")
        me = 4 * x + 2 * y + c
        pack_ref[...] = jnp.zeros_like(pack_ref)
        pack_ref[0:1, :] = nw_ref[...]
        pack_ref[1:2, :] = lb_ref[...]
        pack_ref[2:3, 0:HK] = hn_ref[...]
        pack_ref[3:4, :] = wf_ref[...]
        pack_ref[4:5, :] = ls_ref[...]
        buf_ref[me] = pack_ref[...]
        cps = []
        for d in range(1, 8):
            dx, dy, dc = d >> 2, (d >> 1) & 1, d & 1
            peer = (1 - x if dx else x, 1 - y if dy else y, 1 - c if dc else c)
            cp = pltpu.make_async_remote_copy(
                src_ref=pack_ref, dst_ref=buf_ref.at[me], send_sem=send.at[d - 1], recv_sem=recv.at[d - 1],
                device_id=peer, device_id_type=MESH)
            cp.start()
            cps.append(cp)
        for d in range(1, 8):
            dx, dy, dc = d >> 2, (d >> 1) & 1, d & 1
            src = 4 * (1 - x if dx else x) + 2 * (1 - y if dy else y) + (1 - c if dc else c)
            pltpu.make_async_remote_copy(
                src_ref=pack_ref, dst_ref=buf_ref.at[src], send_sem=send.at[d - 1], recv_sem=recv.at[d - 1],
                device_id=(x, y, c), device_id_type=MESH).wait_recv()
        for cp in cps:
            cp.wait_send()
        acc = buf_ref[0]
        for i in range(1, 8):
            acc = acc + buf_ref[i]
        out_ref[...] = acc

    vm = pl.BlockSpec(memory_space=pltpu.VMEM)
    return pl.pallas_call(
        body, name="small_all_reduce",
        in_specs=[vm] * 5, out_specs=vm,
        out_shape=jax.ShapeDtypeStruct((NSMALL, D), F32),
        scratch_shapes=[pltpu.VMEM((NSMALL, D), F32), pltpu.VMEM((8, NSMALL, D), F32),
                        pltpu.SemaphoreType.DMA((7,)), pltpu.SemaphoreType.DMA((7,))],
    )(g_nw, dlb, g_hnw, g_wf, loss_vec)


def _adamw_math(w, g, m, v):
    m = B1 * m + (1.0 - B1) * g
    v = B2 * v + (1.0 - B2) * (g * g)
    m_hat = m / (1.0 - B1 ** STEP)
    v_hat = v / (1.0 - B2 ** STEP)
    return -LR * (m_hat / (jnp.sqrt(v_hat) + ADAM_EPS) + WD * w), m, v


def _adamw(w, g, m, v, name):
    r, c = w.shape
    tr = 64

    def body(w_ref, g_ref, m_ref, v_ref, d_ref, nm_ref, nv_ref):
        d_ref[...], nm_ref[...], nv_ref[...] = _adamw_math(w_ref[...], g_ref[...], m_ref[...], v_ref[...])

    blk = pl.BlockSpec((tr, c), lambda i: (i, 0))
    return pl.pallas_call(
        body, name=name, grid=(r // tr,), in_specs=[blk] * 4, out_specs=[blk] * 3,
        out_shape=[jax.ShapeDtypeStruct((r, c), F32)] * 3,
        compiler_params=_params(("parallel",)),
    )(w, g, m, v)


def _small_update(red, lbl, params):
    def body(red_ref, *refs):
        ins, outs = refs[:12], refs[12:]
        lb = _lower_bound(ins[3][...])
        dl0 = red_ref[1:2, :] * lb * (1.0 - lb)
        row = lax.broadcasted_iota(jnp.int32, (2, D), 0)
        grads = [red_ref[0:1, :], jnp.where(row == 0, dl0, -dl0), red_ref[2:3, 0:HK], red_ref[3:4, :]]
        for i, g in enumerate(grads):
            w, m, v = ins[3 * i][...], ins[3 * i + 1][...], ins[3 * i + 2][...]
            d, nm, nv = _adamw_math(w, g, m, v)
            outs[4 * i][...] = g
            outs[4 * i + 1][...] = d
            outs[4 * i + 2][...] = nm
            outs[4 * i + 3][...] = nv
        outs[16][...] = jnp.sum(red_ref[4:5, :], axis=1, keepdims=True)

    flat = [a for p in params for a in p]
    vm = pl.BlockSpec(memory_space=pltpu.VMEM)
    shapes = [jax.ShapeDtypeStruct(p[0].shape, F32) for p in params for _ in range(4)]
    return pl.pallas_call(
        body, name="small_update",
        in_specs=[vm] * 13, out_specs=[vm] * 17,
        out_shape=shapes + [jax.ShapeDtypeStruct((1, 1), F32)],
    )(red, *flat)


def kernel(x, positions, norm_w, w_in, lb_logits, hgrn_norm_w, w_branch_a, w_branch_b, w_out, final_norm_w, loss_target, m_norm_w, m_w_in, m_lb_logits, m_hgrn_norm_w, m_w_branch_a, m_w_branch_b, m_w_out, m_final_norm_w, v_norm_w, v_w_in, v_lb_logits, v_hgrn_norm_w, v_w_branch_a, v_w_branch_b, v_w_out, v_final_norm_w):
    big_w = [w_in[0], w_branch_a[0], w_branch_b[0], w_out[0]]
    big_m = [m_w_in[0], m_w_branch_a[0], m_w_branch_b[0], m_w_out[0]]
    big_v = [v_w_in[0], v_w_branch_a[0], v_w_branch_b[0], v_w_out[0]]
    shapes = [w.shape for w in big_w]
    wf = final_norm_w.reshape(1, D)

    full = _all_gather([w.astype(BF) for w in big_w])
    loc = _local_step(x[0], positions.reshape(T, 1), norm_w, lb_logits, hgrn_norm_w, wf, loss_target[0], *full)
    g_big = _reduce_scatter([loc["g_win"], loc["g_wa"], loc["g_wb"], loc["g_wout"]], shapes)
    red = _small_all_reduce(loc["g_nw"], loc["dlb"], loc["g_hnw"], loc["g_wf"], loc["loss_vec"])

    small = _small_update(red, lb_logits, [
        (norm_w, m_norm_w, v_norm_w), (lb_logits, m_lb_logits, v_lb_logits),
        (hgrn_norm_w, m_hgrn_norm_w, v_hgrn_norm_w),
        (wf, m_final_norm_w.reshape(1, D), v_final_norm_w.reshape(1, D))])
    loss = small[16].reshape(())
    sg, sd, sm, sv = ([small[4 * i + j] for i in range(4)] for j in range(4))
    for lst in (sg, sd, sm, sv):
        lst[3] = lst[3].reshape(D)
    upd = [_adamw(w, g, m, v, f"adamw_{a}") for a, (w, g, m, v) in enumerate(zip(big_w, g_big, big_m, big_v))]
    bg = [g[None] for g in g_big]
    bd, bm, bv = ([u[j][None] for u in upd] for j in range(3))

    def order(s, b):
        return [s[0], b[0], s[1], s[2], b[1], b[2], b[3], s[3]]

    return (loss, loc["gx"][None], *order(sg, bg), *order(sd, bd), *order(sm, bm), *order(sv, bv))
```

```python
import functools

import jax
import jax.numpy as jnp
from jax import lax
from jax.experimental import pallas as pl
from jax.experimental.pallas import tpu as pltpu

T = 2048
D = 1024
NIN = 11264
HEADS = 8
HK = 128
CH = 16
NCH = T // CH
ATT_GROUPS = ((128, 1), (512, 4), (2048, 16))
ATT_COL0 = 4096
AG_COL0 = 8704
GATE_COL0 = 9216
EPS = 1e-6
ROPE_THETA = 10000.0
LR, B1, B2, ADAM_EPS, WD, STEP = 0.001, 0.9, 0.999, 1e-08, 0.01, 10

F32 = jnp.float32
BF = jnp.bfloat16
VMEM_LIMIT = 56 * 1024 * 1024

_NN = (((1,), (0,)), ((), ()))
_NT = (((1,), (1,)), ((), ()))
_TN = (((0,), (0,)), ((), ()))


def _dot(a, b, dims=_NN):
    return lax.dot_general(a, b, dims, preferred_element_type=F32)


def _bdot(a, b, dims=_NN):
    return lax.dot_general(a.astype(BF), b.astype(BF), dims, preferred_element_type=F32)


def _sigmoid(x):
    return jax.nn.sigmoid(x)


def _params(sem=None):
    return pltpu.CompilerParams(dimension_semantics=sem, vmem_limit_bytes=VMEM_LIMIT)


def _matmul(a, b, *, ta=False, tb=False, out_dtype=F32, tm=512, tn=512, tk=None, name):
    m = a.shape[1] if ta else a.shape[0]
    kdim = a.shape[0] if ta else a.shape[1]
    n = b.shape[0] if tb else b.shape[1]
    tk = tk or kdim
    tm, tn = min(tm, m), min(tn, n)
    nm, nn, nk = m // tm, n // tn, kdim // tk
    dims = (((0 if ta else 1,), (1 if tb else 0,)), ((), ()))

    def body(a_ref, b_ref, o_ref, *scratch):
        prod = _bdot(a_ref[...], b_ref[...], dims)
        if nk == 1:
            o_ref[...] = prod.astype(out_dtype)
        else:
            acc = scratch[0]
            k = pl.program_id(2)

            @pl.when(k == 0)
            def _():
                acc[...] = prod

            @pl.when(k > 0)
            def _():
                acc[...] += prod

            @pl.when(k == nk - 1)
            def _():
                o_ref[...] = acc[...].astype(out_dtype)

    a_spec = pl.BlockSpec((tk, tm), lambda i, j, k: (k, i)) if ta else pl.BlockSpec((tm, tk), lambda i, j, k: (i, k))
    b_spec = pl.BlockSpec((tn, tk), lambda i, j, k: (j, k)) if tb else pl.BlockSpec((tk, tn), lambda i, j, k: (k, j))
    return pl.pallas_call(
        body, name=name, grid=(nm, nn, nk),
        in_specs=[a_spec, b_spec],
        out_specs=pl.BlockSpec((tm, tn), lambda i, j, k: (i, j)),
        out_shape=jax.ShapeDtypeStruct((m, n), out_dtype),
        scratch_shapes=[pltpu.VMEM((tm, tn), F32)] if nk > 1 else [],
        compiler_params=_params(("parallel", "parallel", "arbitrary")),
    )(a, b)


def _rmsnorm_fwd(x, w):
    tm = 256

    def body(x_ref, w_ref, h_ref):
        xv = x_ref[...]
        r = lax.rsqrt(jnp.mean(xv * xv, axis=-1, keepdims=True) + EPS)
        h_ref[...] = (xv * r * w_ref[...]).astype(BF)

    return pl.pallas_call(
        body, name="rmsnorm_fwd", grid=(T // tm,),
        in_specs=[pl.BlockSpec((tm, D), lambda i: (i, 0)), pl.BlockSpec((1, D), lambda i: (0, 0))],
        out_specs=pl.BlockSpec((tm, D), lambda i: (i, 0)),
        out_shape=jax.ShapeDtypeStruct((T, D), BF),
        compiler_params=_params(("parallel",)),
    )(x, w)


def _rmsnorm_bwd(x, dh, dout, w):
    tm = 256

    def body(x_ref, dh_ref, dout_ref, w_ref, gx_ref, gw_ref):
        @pl.when(pl.program_id(0) == 0)
        def _():
            gw_ref[...] = jnp.zeros_like(gw_ref)

        xv, dhv = x_ref[...], dh_ref[...]
        r = lax.rsqrt(jnp.mean(xv * xv, axis=-1, keepdims=True) + EPS)
        nrm = xv * r
        dn = dhv * w_ref[...]
        gw_ref[...] += jnp.sum(dhv * nrm, axis=0, keepdims=True)
        gx_ref[...] = dout_ref[...] + r * (dn - nrm * jnp.mean(dn * nrm, axis=-1, keepdims=True))

    row = pl.BlockSpec((tm, D), lambda i: (i, 0))
    vec = pl.BlockSpec((1, D), lambda i: (0, 0))
    return pl.pallas_call(
        body, name="rmsnorm_bwd", grid=(T // tm,),
        in_specs=[row, row, row, vec], out_specs=[row, vec],
        out_shape=[jax.ShapeDtypeStruct((T, D), F32), jax.ShapeDtypeStruct((1, D), F32)],
        compiler_params=_params(("arbitrary",)),
    )(x, dh, dout, w)


def _lower_bound(lbl):
    mx = jnp.max(lbl, axis=0, keepdims=True)
    e = jnp.exp(lbl - mx)
    return e[0:1] / jnp.sum(e, axis=0, keepdims=True)


def _cumsum_rows(g, rows):
    b = g
    sh = 1
    while sh < CH:
        b = b + jnp.where(rows >= sh, pltpu.roll(b, sh, axis=0), 0.0)
        sh *= 2
    return b


def _rev_cumsum_rows(g, rows):
    b = g
    sh = 1
    while sh < CH:
        b = b + jnp.where(rows < CH - sh, pltpu.roll(b, CH - sh, axis=0), 0.0)
        sh *= 2
    return b


def _hgrn_fwd(z, lbl, nw):
    def body(hq_ref, hf_ref, hi_ref, hg_ref, lbl_ref, nw_ref, oraw_ref, og_ref, sh_ref, st_ref):
        @pl.when(pl.program_id(0) == 0)
        def _():
            st_ref[...] = jnp.zeros_like(st_ref)

        lb_all = _lower_bound(lbl_ref[...])
        rows = lax.broadcasted_iota(jnp.int32, (CH, HK), 0)
        nwv = nw_ref[...]
        for h in range(HEADS):
            sl = slice(HK * h, HK * (h + 1))
            lb = lb_all[:, sl]
            hq, hf, v, hg = hq_ref[:, sl], hf_ref[:, sl], hi_ref[:, sl], hg_ref[:, sl]
            q = hq * _sigmoid(hq)
            f = lb + (1.0 - lb) * _sigmoid(hf)
            k = 1.0 - f
            b = _cumsum_rows(jnp.log(f), rows)
            st0 = st_ref[h]
            sh_ref[0, h] = st0
            o = _bdot(q * jnp.exp(b), st0, _NT)
            for s in range(CH):
                e_s = jnp.exp(jnp.where(rows >= s, b - b[s:s + 1], -jnp.inf))
                a = jnp.sum(q * e_s * k[s:s + 1], axis=1, keepdims=True)
                o = o + a * v[s:s + 1]
            bl = b[CH - 1:CH]
            st_ref[h] = st0 * jnp.exp(bl) + _bdot(v, k * jnp.exp(bl - b), _TN)
            oraw_ref[:, sl] = o
            nrm = o * lax.rsqrt(jnp.mean(o * o, axis=1, keepdims=True) + EPS)
            og_ref[:, sl] = (nrm * nwv * (hg * _sigmoid(hg))).astype(BF)

    zblk = lambda c: pl.BlockSpec((CH, D), lambda i, c=c: (i, c))
    return pl.pallas_call(
        body, name="hgrn_fwd", grid=(NCH,),
        in_specs=[zblk(0), zblk(1), zblk(2), zblk(3),
                  pl.BlockSpec((2, D), lambda i: (0, 0)), pl.BlockSpec((1, HK), lambda i: (0, 0))],
        out_specs=[pl.BlockSpec((CH, D), lambda i: (i, 0)), pl.BlockSpec((CH, D), lambda i: (i, 0)),
                   pl.BlockSpec((1, HEADS, HK, HK), lambda i: (i, 0, 0, 0))],
        out_shape=[jax.ShapeDtypeStruct((T, D), F32), jax.ShapeDtypeStruct((T, D), BF),
                   jax.ShapeDtypeStruct((NCH, HEADS, HK, HK), F32)],
        scratch_shapes=[pltpu.VMEM((HEADS, HK, HK), F32)],
        compiler_params=_params(("arbitrary",)),
    )(z, z, z, z, lbl, nw)


def _hgrn_bwd(z, lbl, nw, oraw, dog, shist):
    def body(hq_ref, hf_ref, hi_ref, hg_ref, lbl_ref, nw_ref, oraw_ref, dog_ref, sh_ref,
             dz_ref, dlb_ref, dnw_ref, dst_ref):
        @pl.when(pl.program_id(0) == 0)
        def _():
            dst_ref[...] = jnp.zeros_like(dst_ref)
            dlb_ref[...] = jnp.zeros_like(dlb_ref)
            dnw_ref[...] = jnp.zeros_like(dnw_ref)

        lb_all = _lower_bound(lbl_ref[...])
        rows = lax.broadcasted_iota(jnp.int32, (CH, HK), 0)
        rowc = lax.broadcasted_iota(jnp.int32, (CH, 1), 0)
        nwv = nw_ref[...]
        dnw = jnp.zeros((1, HK), F32)
        for h in range(HEADS):
            sl = slice(HK * h, HK * (h + 1))
            lb = lb_all[:, sl]
            hq, hf, v, hg = hq_ref[:, sl], hf_ref[:, sl], hi_ref[:, sl], hg_ref[:, sl]
            o, dg_out = oraw_ref[:, sl], dog_ref[:, sl]
            sg = _sigmoid(hg)
            sil = hg * sg
            r = lax.rsqrt(jnp.mean(o * o, axis=1, keepdims=True) + EPS)
            nrm = o * r
            d_hg = dg_out * (nrm * nwv) * (sg * (1.0 + hg * (1.0 - sg)))
            dn = dg_out * nwv * sil
            dnw = dnw + jnp.sum(dg_out * nrm * sil, axis=0, keepdims=True)
            do = r * (dn - nrm * jnp.mean(dn * nrm, axis=1, keepdims=True))
            sq = _sigmoid(hq)
            q = hq * sq
            sig = _sigmoid(hf)
            f = lb + (1.0 - lb) * sig
            k = 1.0 - f
            b = _cumsum_rows(jnp.log(f), rows)
            eb = jnp.exp(b)
            qe = q * eb
            bl = b[CH - 1:CH]
            ebl = jnp.exp(bl)
            kdec = jnp.exp(bl - b)
            ke = k * kdec
            st0 = sh_ref[0, h]
            dst1 = dst_ref[h]
            dqe = _bdot(do, st0)
            dst_ref[h] = dst1 * ebl + _bdot(do, qe, _TN)
            dq = dqe * eb
            db = dqe * qe
            dke = _bdot(v, dst1)
            dv = _bdot(ke, dst1, _NT)
            dk = dke * kdec
            rr = dke * ke
            db = db - rr
            db_last = jnp.sum(rr, axis=0, keepdims=True) + ebl * jnp.sum(dst1 * st0, axis=0, keepdims=True)
            for s in range(CH):
                one = (rowc == s).astype(F32)
                ks, vs = k[s:s + 1], v[s:s + 1]
                e_s = jnp.exp(jnp.where(rows >= s, b - b[s:s + 1], -jnp.inf))
                qes = q * e_s
                w = qes * ks
                a = jnp.sum(w, axis=1, keepdims=True)
                da = jnp.sum(do * vs, axis=1, keepdims=True)
                dv = dv + one * jnp.sum(a * do, axis=0, keepdims=True)
                dq = dq + da * e_s * ks
                dk = dk + one * jnp.sum(da * qes, axis=0, keepdims=True)
                u = da * w
                db = db + u - one * jnp.sum(u, axis=0, keepdims=True)
            db = db + (rowc == CH - 1).astype(F32) * db_last
            dgl = _rev_cumsum_rows(db, rows)
            df = dgl / f - dk
            dlb_ref[:, sl] += jnp.sum(df * (1.0 - sig), axis=0, keepdims=True)
            dz_ref[:, sl] = (dq * (sq * (1.0 + hq * (1.0 - sq)))).astype(BF)
            dz_ref[:, D + HK * h:D + HK * (h + 1)] = (df * (1.0 - lb) * sig * (1.0 - sig)).astype(BF)
            dz_ref[:, 2 * D + HK * h:2 * D + HK * (h + 1)] = dv.astype(BF)
            dz_ref[:, 3 * D + HK * h:3 * D + HK * (h + 1)] = d_hg.astype(BF)
        dnw_ref[...] += dnw

    rev = lambda i: NCH - 1 - i
    zblk = lambda c: pl.BlockSpec((CH, D), lambda i, c=c: (rev(i), c))
    return pl.pallas_call(
        body, name="hgrn_bwd", grid=(NCH,),
        in_specs=[zblk(0), zblk(1), zblk(2), zblk(3),
                  pl.BlockSpec((2, D), lambda i: (0, 0)), pl.BlockSpec((1, HK), lambda i: (0, 0)),
                  pl.BlockSpec((CH, D), lambda i: (rev(i), 0)), pl.BlockSpec((CH, D), lambda i: (rev(i), 0)),
                  pl.BlockSpec((1, HEADS, HK, HK), lambda i: (rev(i), 0, 0, 0))],
        out_specs=[pl.BlockSpec((CH, 4 * D), lambda i: (rev(i), 0)),
                   pl.BlockSpec((1, D), lambda i: (0, 0)), pl.BlockSpec((1, HK), lambda i: (0, 0))],
        out_shape=[jax.ShapeDtypeStruct((T, 4 * D), BF), jax.ShapeDtypeStruct((1, D), F32),
                   jax.ShapeDtypeStruct((1, HK), F32)],
        scratch_shapes=[pltpu.VMEM((HEADS, HK, HK), F32)],
        compiler_params=_params(("arbitrary",)),
    )(z, z, z, z, lbl, nw, oraw, dog, shist)


BLK = 128
NBLK = T // BLK
QK_SCALE = 0.125


def _head_masks():
    lane = lax.broadcasted_iota(jnp.int32, (1, BLK), 1)
    return [(lane < 64).astype(F32), (lane >= 64).astype(F32)]


def _pieces(dil):
    m = T // dil
    out = []
    for r in range(dil):
        for j in range(m // BLK):
            start = r + dil * BLK * j
            rows = pl.ds(start, BLK, stride=dil) if dil > 1 else pl.ds(start, BLK)
            out.append((rows, r * m + BLK * j))
    return out


def _rope_tables(pos_ref, invf_ref, cos_ref, sa_ref, sb_ref):
    lane = lax.broadcasted_iota(jnp.int32, (256, BLK), 1)
    first = (lane % 64) < 32
    for c in range(T // 256):
        rows = pl.ds(256 * c, 256)
        ang = pos_ref[rows, :].astype(F32) * invf_ref[...]
        s = jnp.sin(ang)
        cos_ref[rows, :] = jnp.cos(ang)
        sa_ref[rows, :] = jnp.where(first, -s, 0.0)
        sb_ref[rows, :] = jnp.where(first, 0.0, s)


def _rope(x, c, sa, sb):
    return x * c + pltpu.roll(x, 96, axis=1) * sa + pltpu.roll(x, 32, axis=1) * sb


def _rope_t(d, c, sa, sb):
    return d * c + pltpu.roll(d * sa, 32, axis=1) + pltpu.roll(d * sb, 96, axis=1)


def _window_bias(bias_ref):
    ii = lax.broadcasted_iota(jnp.int32, (2 * BLK, BLK), 0) % BLK
    jj = lax.broadcasted_iota(jnp.int32, (2 * BLK, BLK), 1)
    bias_ref[0] = jnp.where(jj <= ii, 0.0, -jnp.inf)
    bias_ref[1] = jnp.where(jj >= ii, 0.0, -jnp.inf)


def _stack_heads(x, masks):
    return jnp.concatenate([x * masks[0], x * masks[1]], axis=0).astype(BF)


def _attn_fwd(z, pos, invf):
    def body(q_ref, k_ref, v_ref, ag_ref, pos_ref, invf_ref, ob_ref, opre_ref, lse_ref,
             cos_ref, sa_ref, sb_ref, bias_ref, qr_ref, kr_ref, vr_ref, og_ref, lg_ref, otok_ref, ltok_ref):
        g = pl.program_id(1)
        masks = _head_masks()

        @pl.when(g == 0)
        def _():
            _rope_tables(pos_ref, invf_ref, cos_ref, sa_ref, sb_ref)
            _window_bias(bias_ref)

        def group(gi):
            dil = ATT_GROUPS[gi][1]
            nblk = (T // dil) // BLK
            for rows, dst in _pieces(dil):
                c, sa, sb = cos_ref[rows, :], sa_ref[rows, :], sb_ref[rows, :]
                qr_ref[pl.ds(dst, BLK), :] = _rope(q_ref[rows, :], c, sa, sb) * QK_SCALE
                kr_ref[pl.ds(dst, BLK), :] = _rope(k_ref[rows, :], c, sa, sb)
                vr_ref[pl.ds(dst, BLK), :] = v_ref[rows, :]

            def blk(bi, carry):
                cur = pl.ds(pl.multiple_of(bi * BLK, BLK), BLK)
                prev = pl.ds(pl.multiple_of(jnp.maximum(bi - 1, 0) * BLK, BLK), BLK)
                q2 = _stack_heads(qr_ref[cur, :], masks)
                kc, vc = kr_ref[cur, :].astype(BF), vr_ref[cur, :].astype(BF)
                s_c = _dot(q2, kc, _NT) + bias_ref[0]
                if nblk > 1:
                    kp, vp = kr_ref[prev, :].astype(BF), vr_ref[prev, :].astype(BF)
                    s_p = _dot(q2, kp, _NT) + (bias_ref[1] + jnp.where((bi % nblk) != 0, 0.0, -jnp.inf))
                    mx = jnp.max(jnp.maximum(s_c, s_p), axis=1, keepdims=True)
                    p_c, p_p = jnp.exp(s_c - mx), jnp.exp(s_p - mx)
                    den = jnp.sum(p_c + p_p, axis=1, keepdims=True)
                    oh = _dot(p_c.astype(BF), vc) + _dot(p_p.astype(BF), vp)
                else:
                    mx = jnp.max(s_c, axis=1, keepdims=True)
                    p_c = jnp.exp(s_c - mx)
                    den = jnp.sum(p_c, axis=1, keepdims=True)
                    oh = _dot(p_c.astype(BF), vc)
                on = oh / den
                lsev = jnp.broadcast_to(mx + jnp.log(den), (2 * BLK, BLK))
                og_ref[cur, :] = on[:BLK] * masks[0] + on[BLK:] * masks[1]
                lg_ref[0, cur, :] = lsev[:BLK]
                lg_ref[1, cur, :] = lsev[BLK:]
                return carry

            lax.fori_loop(0, NBLK, blk, 0)
            for rows, src in _pieces(dil):
                srows = pl.ds(src, BLK)
                otok_ref[gi, rows, :] = og_ref[srows, :]
                ltok_ref[gi, 0, rows, :] = lg_ref[0, srows, :]
                ltok_ref[gi, 1, rows, :] = lg_ref[1, srows, :]

        for gi in range(3):
            pl.when(g == gi)(functools.partial(group, gi))

        @pl.when(g == 2)
        def _():
            for c in range(T // BLK):
                rows = pl.ds(BLK * c, BLK)
                wts = []
                for hh in range(2):
                    l0, l1, l2 = ltok_ref[0, hh, rows, :], ltok_ref[1, hh, rows, :], ltok_ref[2, hh, rows, :]
                    mx = jnp.maximum(jnp.maximum(l0, l1), l2)
                    lse = mx + jnp.log(jnp.exp(l0 - mx) + jnp.exp(l1 - mx) + jnp.exp(l2 - mx))
                    lse_ref[rows, BLK * hh:BLK * (hh + 1)] = lse
                    wts.append([jnp.exp(l0 - lse), jnp.exp(l1 - lse), jnp.exp(l2 - lse)])
                o = sum((wts[0][gi] * masks[0] + wts[1][gi] * masks[1]) * otok_ref[gi, rows, :] for gi in range(3))
                ag = ag_ref[rows, :]
                opre_ref[rows, :] = o
                ob_ref[rows, :] = (o * (ag * _sigmoid(ag))).astype(BF)

    c0 = ATT_COL0 // BLK
    zspec = lambda part: pl.BlockSpec((T, BLK), lambda p, g, part=part: (0, c0 + 12 * part + 4 * g + p))
    outspec = pl.BlockSpec((T, BLK), lambda p, g: (0, p))
    big = lambda: pltpu.VMEM((T, BLK), F32)
    return pl.pallas_call(
        body, name="attn_fwd", grid=(4, 3),
        in_specs=[zspec(0), zspec(1), zspec(2),
                  pl.BlockSpec((T, BLK), lambda p, g: (0, AG_COL0 // BLK + p)),
                  pl.BlockSpec((T, 1), lambda p, g: (0, 0)), pl.BlockSpec((1, BLK), lambda p, g: (0, 0))],
        out_specs=[outspec, outspec, pl.BlockSpec((T, 2 * BLK), lambda p, g: (0, p))],
        out_shape=[jax.ShapeDtypeStruct((T, 512), BF), jax.ShapeDtypeStruct((T, 512), F32),
                   jax.ShapeDtypeStruct((T, 8 * BLK), F32)],
        scratch_shapes=[big(), big(), big(), pltpu.VMEM((2, 2 * BLK, BLK), F32), big(), big(), big(), big(),
                        pltpu.VMEM((2, T, BLK), F32), pltpu.VMEM((3, T, BLK), F32), pltpu.VMEM((3, 2, T, BLK), F32)],
        compiler_params=_params(("parallel", "arbitrary")),
    )(z, z, z, z, pos, invf)


def _attn_bwd(z, pos, invf, opre, lse, dob):
    def body(q_ref, k_ref, v_ref, ag_ref, pos_ref, invf_ref, o_ref, lse0_ref, lse1_ref, dob_ref,
             dq_ref, dk_ref, dv_ref, dag_ref,
             cos_ref, sa_ref, sb_ref, bias_ref, dtok_ref, qr_ref, kr_ref, vr_ref, dor_ref, lr_ref, dr_ref,
             dqr_ref, dkr_ref, dvr_ref):
        g = pl.program_id(1)
        masks = _head_masks()

        @pl.when(g == 0)
        def _():
            _rope_tables(pos_ref, invf_ref, cos_ref, sa_ref, sb_ref)
            _window_bias(bias_ref)
            for c in range(T // BLK):
                rows = pl.ds(BLK * c, BLK)
                ag, dob_v, o = ag_ref[rows, :], dob_ref[rows, :], o_ref[rows, :]
                sg = _sigmoid(ag)
                dag_ref[rows, :] = (dob_v * o * (sg * (1.0 + ag * (1.0 - sg)))).astype(BF)
                prod = dob_v * (ag * sg) * o
                for hh, mh in enumerate(masks):
                    dtok_ref[hh, rows, :] = jnp.broadcast_to(jnp.sum(prod * mh, axis=1, keepdims=True), (BLK, BLK))

        def group(gi):
            dil = ATT_GROUPS[gi][1]
            nblk = (T // dil) // BLK
            for rows, dst in _pieces(dil):
                drows = pl.ds(dst, BLK)
                c, sa, sb = cos_ref[rows, :], sa_ref[rows, :], sb_ref[rows, :]
                qr_ref[drows, :] = _rope(q_ref[rows, :], c, sa, sb) * QK_SCALE
                kr_ref[drows, :] = _rope(k_ref[rows, :], c, sa, sb)
                vr_ref[drows, :] = v_ref[rows, :]
                ag = ag_ref[rows, :]
                dor_ref[drows, :] = dob_ref[rows, :] * (ag * _sigmoid(ag))
                for hh, lse_ref in enumerate((lse0_ref, lse1_ref)):
                    lr_ref[hh, drows, :] = lse_ref[rows, :]
                    dr_ref[hh, drows, :] = dtok_ref[hh, rows, :]
            dkr_ref[...] = jnp.zeros_like(dkr_ref)
            dvr_ref[...] = jnp.zeros_like(dvr_ref)

            def blk(bi, carry):
                cur = pl.ds(pl.multiple_of(bi * BLK, BLK), BLK)
                prev = pl.ds(pl.multiple_of(jnp.maximum(bi - 1, 0) * BLK, BLK), BLK)
                q2, do2 = _stack_heads(qr_ref[cur, :], masks), _stack_heads(dor_ref[cur, :], masks)
                lh = jnp.concatenate([lr_ref[0, cur, :], lr_ref[1, cur, :]], axis=0)
                dh = jnp.concatenate([dr_ref[0, cur, :], dr_ref[1, cur, :]], axis=0)
                kc, vc = kr_ref[cur, :].astype(BF), vr_ref[cur, :].astype(BF)
                p_c = jnp.exp(_dot(q2, kc, _NT) + bias_ref[0] - lh)
                ds_c = (p_c * (_dot(do2, vc, _NT) - dh)).astype(BF)
                dq2 = _dot(ds_c, kc)
                dkr_ref[cur, :] += _dot(ds_c, q2, _TN)
                dvr_ref[cur, :] += _dot(p_c.astype(BF), do2, _TN)
                if nblk > 1:
                    kp, vp = kr_ref[prev, :].astype(BF), vr_ref[prev, :].astype(BF)
                    bias_p = bias_ref[1] + jnp.where((bi % nblk) != 0, 0.0, -jnp.inf)
                    p_p = jnp.exp(_dot(q2, kp, _NT) + bias_p - lh)
                    ds_p = (p_p * (_dot(do2, vp, _NT) - dh)).astype(BF)
                    dq2 = dq2 + _dot(ds_p, kp)
                    dkr_ref[prev, :] += _dot(ds_p, q2, _TN)
                    dvr_ref[prev, :] += _dot(p_p.astype(BF), do2, _TN)
                dqr_ref[cur, :] = dq2[:BLK] * masks[0] + dq2[BLK:] * masks[1]
                return carry

            lax.fori_loop(0, NBLK, blk, 0)
            for rows, src in _pieces(dil):
                srows = pl.ds(src, BLK)
                c, sa, sb = cos_ref[rows, :], sa_ref[rows, :], sb_ref[rows, :]
                qr_ref[rows, :] = _rope_t(dqr_ref[srows, :] * QK_SCALE, c, sa, sb)
                kr_ref[rows, :] = _rope_t(dkr_ref[srows, :], c, sa, sb)
                vr_ref[rows, :] = dvr_ref[srows, :]
            for c in range(T // 256):
                rows = pl.ds(256 * c, 256)
                dq_ref[rows, :] = qr_ref[rows, :].astype(BF)
                dk_ref[rows, :] = kr_ref[rows, :].astype(BF)
                dv_ref[rows, :] = vr_ref[rows, :].astype(BF)

        for gi in range(3):
            pl.when(g == gi)(functools.partial(group, gi))

    c0 = ATT_COL0 // BLK
    zspec = lambda part: pl.BlockSpec((T, BLK), lambda p, g, part=part: (0, c0 + 12 * part + 4 * g + p))
    pspec = pl.BlockSpec((T, BLK), lambda p, g: (0, p))
    gspec = pl.BlockSpec((T, BLK), lambda p, g: (0, 4 * g + p))
    big = lambda: pltpu.VMEM((T, BLK), F32)
    two = lambda: pltpu.VMEM((2, T, BLK), F32)
    return pl.pallas_call(
        body, name="attn_bwd", grid=(4, 3),
        in_specs=[zspec(0), zspec(1), zspec(2),
                  pl.BlockSpec((T, BLK), lambda p, g: (0, AG_COL0 // BLK + p)),
                  pl.BlockSpec((T, 1), lambda p, g: (0, 0)), pl.BlockSpec((1, BLK), lambda p, g: (0, 0)),
                  pspec, pl.BlockSpec((T, BLK), lambda p, g: (0, 2 * p)),
                  pl.BlockSpec((T, BLK), lambda p, g: (0, 2 * p + 1)), pspec],
        out_specs=[gspec, gspec, gspec, pspec],
        out_shape=[jax.ShapeDtypeStruct((T, 1536), BF), jax.ShapeDtypeStruct((T, 1536), BF),
                   jax.ShapeDtypeStruct((T, 1536), BF), jax.ShapeDtypeStruct((T, 512), BF)],
        scratch_shapes=[big(), big(), big(), pltpu.VMEM((2, 2 * BLK, BLK), F32), two(), big(), big(), big(), big(),
                        two(), two(), big(), big(), big()],
        compiler_params=_params(("parallel", "arbitrary")),
    )(z, z, z, z, pos, invf, opre, lse, lse, dob)


def _merge_fwd(ya, yb, z):
    tm = 256

    def body(ya_ref, yb_ref, ga_ref, gb_ref, m_ref):
        m_ref[...] = (_sigmoid(ga_ref[...]) * ya_ref[...] + _sigmoid(gb_ref[...]) * yb_ref[...]).astype(BF)

    row = pl.BlockSpec((tm, D), lambda i: (i, 0))
    return pl.pallas_call(
        body, name="merge_fwd", grid=(T // tm,),
        in_specs=[row, row, pl.BlockSpec((tm, D), lambda i: (i, GATE_COL0 // D)),
                  pl.BlockSpec((tm, D), lambda i: (i, GATE_COL0 // D + 1))],
        out_specs=row, out_shape=jax.ShapeDtypeStruct((T, D), BF),
        compiler_params=_params(("parallel",)),
    )(ya, yb, z, z)


def _out_loss(merged, w_out, x, tgt, wf):
    tm = 256

    def body(m_ref, w_ref, x_ref, t_ref, wf_ref, dout_ref, loss_ref, gwf_ref):
        @pl.when(pl.program_id(0) == 0)
        def _():
            loss_ref[...] = jnp.zeros_like(loss_ref)
            gwf_ref[...] = jnp.zeros_like(gwf_ref)

        out = x_ref[...] + _dot(m_ref[...], w_ref[...])
        r = lax.rsqrt(jnp.mean(out * out, axis=-1, keepdims=True) + EPS)
        yh = out * r
        wfv = wf_ref[...]
        err = yh * wfv - t_ref[...]
        loss_ref[...] += jnp.sum(err * err, axis=0, keepdims=True) * (0.5 / D)
        dy = err * (1.0 / D)
        gwf_ref[...] += jnp.sum(dy * yh, axis=0, keepdims=True)
        dyh = dy * wfv
        dout_ref[...] = r * (dyh - yh * jnp.mean(dyh * yh, axis=-1, keepdims=True))

    row = pl.BlockSpec((tm, D), lambda i: (i, 0))
    vec = pl.BlockSpec((1, D), lambda i: (0, 0))
    return pl.pallas_call(
        body, name="out_loss", grid=(T // tm,),
        in_specs=[row, pl.BlockSpec((D, D), lambda i: (0, 0)), row, row, vec],
        out_specs=[row, vec, vec],
        out_shape=[jax.ShapeDtypeStruct((T, D), F32), jax.ShapeDtypeStruct((1, D), F32),
                   jax.ShapeDtypeStruct((1, D), F32)],
        compiler_params=_params(("arbitrary",)),
    )(merged, w_out, x, tgt, wf)


def _merge_bwd(dm, ya, yb, z):
    tm = 256

    def body(dm_ref, ya_ref, yb_ref, ga_ref, gb_ref, dya_ref, dyb_ref, dg_ref):
        dmv = dm_ref[...]
        sa, sb = _sigmoid(ga_ref[...]), _sigmoid(gb_ref[...])
        dya_ref[...] = (sa * dmv).astype(BF)
        dyb_ref[...] = (sb * dmv).astype(BF)
        dg_ref[:, :D] = (dmv * ya_ref[...] * sa * (1.0 - sa)).astype(BF)
        dg_ref[:, D:] = (dmv * yb_ref[...] * sb * (1.0 - sb)).astype(BF)

    row = pl.BlockSpec((tm, D), lambda i: (i, 0))
    return pl.pallas_call(
        body, name="merge_bwd", grid=(T // tm,),
        in_specs=[row, row, row, pl.BlockSpec((tm, D), lambda i: (i, GATE_COL0 // D)),
                  pl.BlockSpec((tm, D), lambda i: (i, GATE_COL0 // D + 1))],
        out_specs=[row, row, pl.BlockSpec((tm, 2 * D), lambda i: (i, 0))],
        out_shape=[jax.ShapeDtypeStruct((T, D), BF), jax.ShapeDtypeStruct((T, D), BF),
                   jax.ShapeDtypeStruct((T, 2 * D), BF)],
        compiler_params=_params(("parallel",)),
    )(dm, ya, yb, z, z)


def _rope_inv_freq():
    inv = ROPE_THETA ** (-jnp.arange(0, 64, 2, dtype=F32) / 64)
    return jnp.tile(inv, 4).reshape(1, BLK)


def _local_step(x, pos, norm_w, lbl, hnw, wf, tgt, w_in, w_a, w_b, w_out):
    invf = _rope_inv_freq()
    h = _rmsnorm_fwd(x, norm_w)
    z = _matmul(h, w_in, tm=1024, tn=512, name="z_proj")
    oraw, og, shist = _hgrn_fwd(z, lbl, hnw)
    ob, opre, lse = _attn_fwd(z, pos, invf)
    ya = _matmul(og, w_a, tm=1024, tn=512, name="ya_proj")
    yb = _matmul(ob, w_b, tm=1024, tn=512, name="yb_proj")
    merged = _merge_fwd(ya, yb, z)
    dout, loss_vec, g_wf = _out_loss(merged, w_out, x, tgt, wf)

    dm = _matmul(dout, w_out, tb=True, tm=1024, tn=512, name="d_merged")
    g_wout = _matmul(merged, dout, ta=True, out_dtype=BF, tm=512, tn=1024, name="g_wout")
    dya, dyb, dgates = _merge_bwd(dm, ya, yb, z)
    dog = _matmul(dya, w_a, tb=True, tm=1024, tn=512, name="d_og")
    g_wa = _matmul(og, dya, ta=True, out_dtype=BF, tm=512, tn=1024, name="g_wa")
    dob = _matmul(dyb, w_b, tb=True, tm=1024, tn=512, name="d_ob")
    g_wb = _matmul(ob, dyb, ta=True, out_dtype=BF, tm=512, tn=1024, name="g_wb")
    dz_h, dlb, g_hnw = _hgrn_bwd(z, lbl, hnw, oraw, dog, shist)
    dq, dk, dv, dag = _attn_bwd(z, pos, invf, opre, lse, dob)
    dz = jnp.concatenate([dz_h, dq, dk, dv, dag, dgates], axis=1)
    g_win = _matmul(h, dz, ta=True, out_dtype=BF, tm=1024, tn=512, name="g_win")
    dh = _matmul(dz, w_in, tb=True, tm=1024, tn=1024, tk=1408, name="d_h")
    gx, g_nw = _rmsnorm_bwd(x, dh, dout, norm_w)
    return dict(loss_vec=loss_vec, gx=gx, g_nw=g_nw, dlb=dlb, g_hnw=g_hnw, g_wf=g_wf,
                g_win=g_win, g_wa=g_wa, g_wb=g_wb, g_wout=g_wout)


MESH = pl.DeviceIdType.MESH
HBM = pl.BlockSpec(memory_space=pl.ANY)
WEIGHT_AXES = (1, 0, 1, 0)


def _place():
    x, y, c = lax.axis_index("x"), lax.axis_index("y"), lax.axis_index("c")
    chips = [(1 - x, y), (x, 1 - y), (1 - x, 1 - y)]
    return x, y, c, chips


def _block_half(ref, shard_shape, axis, j, half):
    r, c = shard_shape
    hr = r // 2
    if axis == 0:
        return ref.at[pl.ds(pl.multiple_of(j * r + half * hr, 16), hr), :]
    return ref.at[pl.ds(pl.multiple_of(half * hr, 16), hr), pl.ds(pl.multiple_of(j * c, 128), c)]


def _all_gather(shards):
    n = len(shards)
    shapes = [s.shape for s in shards]

    def body(*refs):
        ins, outs = refs[:n], refs[n:2 * n]
        send1, recv1, send2, recv2, send0, recv0 = refs[2 * n:]
        x, y, c, chips = _place()
        me = 2 * x + y
        started, own = [], []
        for a in range(n):
            r, cc = shapes[a]
            ax = WEIGHT_AXES[a]
            mine = (outs[a].at[pl.ds(pl.multiple_of(me * r, 16), r), :] if ax == 0
                    else outs[a].at[:, pl.ds(pl.multiple_of(me * cc, 128), cc)])
            swap = pltpu.make_async_remote_copy(
                src_ref=ins[a], dst_ref=mine, send_sem=send0.at[a], recv_sem=recv0.at[a],
                device_id=(x, y, 1 - c), device_id_type=MESH)
            swap.start()
            own.append(swap)
            src = ins[a].at[pl.ds(pl.multiple_of(c * (r // 2), 16), r // 2), :]
            for k, (px, py) in enumerate(chips):
                cp = pltpu.make_async_remote_copy(
                    src_ref=src, dst_ref=_block_half(outs[a], shapes[a], ax, me, c),
                    send_sem=send1.at[a, k], recv_sem=recv1.at[a, k], device_id=(px, py, c), device_id_type=MESH)
                cp.start()
                started.append(cp)
        for a in range(n):
            for k, (px, py) in enumerate(chips):
                reg = _block_half(outs[a], shapes[a], WEIGHT_AXES[a], 2 * px + py, c)
                pltpu.make_async_remote_copy(
                    src_ref=reg, dst_ref=reg, send_sem=send1.at[a, k], recv_sem=recv1.at[a, k],
                    device_id=(px, py, c), device_id_type=MESH).wait_recv()
                fw = pltpu.make_async_remote_copy(
                    src_ref=reg, dst_ref=reg, send_sem=send2.at[a, k], recv_sem=recv2.at[a, k],
                    device_id=(x, y, 1 - c), device_id_type=MESH)
                fw.start()
                started.append(fw)
        for a in range(n):
            for k, (px, py) in enumerate(chips):
                reg = _block_half(outs[a], shapes[a], WEIGHT_AXES[a], 2 * px + py, 1 - c)
                pltpu.make_async_remote_copy(
                    src_ref=reg, dst_ref=reg, send_sem=send2.at[a, k], recv_sem=recv2.at[a, k],
                    device_id=(x, y, 1 - c), device_id_type=MESH).wait_recv()
        for cp in started:
            cp.wait_send()
        for cp in own:
            cp.wait()

    full = [(4 * r, c) if ax == 0 else (r, 4 * c) for (r, c), ax in zip(shapes, WEIGHT_AXES)]
    return pl.pallas_call(
        body, name="weights_all_gather",
        in_specs=[HBM] * n, out_specs=[HBM] * n,
        out_shape=[jax.ShapeDtypeStruct(f, BF) for f in full],
        scratch_shapes=[pltpu.SemaphoreType.DMA((n, 3)), pltpu.SemaphoreType.DMA((n, 3)),
                        pltpu.SemaphoreType.DMA((n, 3)), pltpu.SemaphoreType.DMA((n, 3)),
                        pltpu.SemaphoreType.DMA((n,)), pltpu.SemaphoreType.DMA((n,))],
    )(*shards)


def _as3d(g, shard_shape, axis):
    r, c = shard_shape
    return g.reshape(4, r, c) if axis == 0 else g.reshape(1, r, 4 * c)


def _half_rows(ref3, hr, half):
    return ref3.at[:, pl.ds(pl.multiple_of(half * hr, 16), hr), :]


def _rs_pair_exchange(g3s):
    n = len(g3s)

    def body(*refs):
        ins, outs = refs[:n], refs[n:2 * n]
        send, recv = refs[2 * n:]
        x, y, c, _ = _place()
        cps = []
        for a in range(n):
            hr = g3s[a].shape[1] // 2
            cp = pltpu.make_async_remote_copy(
                src_ref=_half_rows(ins[a], hr, 1 - c), dst_ref=outs[a],
                send_sem=send.at[a], recv_sem=recv.at[a], device_id=(x, y, 1 - c), device_id_type=MESH)
            cp.start()
            cps.append(cp)
        for cp in cps:
            cp.wait()

    return pl.pallas_call(
        body, name="grads_pair_exchange",
        in_specs=[HBM] * n, out_specs=[HBM] * n,
        out_shape=[jax.ShapeDtypeStruct((g.shape[0], g.shape[1] // 2, g.shape[2]), BF) for g in g3s],
        scratch_shapes=[pltpu.SemaphoreType.DMA((n,)), pltpu.SemaphoreType.DMA((n,))],
    )(*g3s)


def _pair_sum(g3, land, cidx, name):
    nb, r, w = g3.shape
    hr = r // 2
    tr = 64

    def body(c_ref, g_ref, l_ref, o_ref):
        o_ref[...] = (g_ref[...].astype(F32) + l_ref[...].astype(F32)).astype(BF)

    blk = (nb, tr, w)
    return pl.pallas_call(
        body, name=name,
        grid_spec=pltpu.PrefetchScalarGridSpec(
            num_scalar_prefetch=1, grid=(hr // tr,),
            in_specs=[pl.BlockSpec(blk, lambda i, c: (0, c[0] * (hr // tr) + i, 0)),
                      pl.BlockSpec(blk, lambda i, c: (0, i, 0))],
            out_specs=pl.BlockSpec(blk, lambda i, c: (0, i, 0))),
        out_shape=jax.ShapeDtypeStruct((nb, hr, w), BF),
        compiler_params=_params(("parallel",)),
    )(cidx, g3, land)


def _rs_chip_exchange(p3s, shapes):
    n = len(p3s)

    def body(*refs):
        ins, outs = refs[:n], refs[n:2 * n]
        send, recv = refs[2 * n:]
        x, y, c, chips = _place()
        cps = []
        for a in range(n):
            r, cc = shapes[a]
            for k, (px, py) in enumerate(chips):
                j = 2 * px + py
                src = ins[a].at[j] if WEIGHT_AXES[a] == 0 else ins[a].at[0, :, pl.ds(pl.multiple_of(j * cc, 128), cc)]
                cp = pltpu.make_async_remote_copy(
                    src_ref=src, dst_ref=outs[a].at[k], send_sem=send.at[a, k], recv_sem=recv.at[a, k],
                    device_id=(px, py, c), device_id_type=MESH)
                cp.start()
                cps.append(cp)
        for cp in cps:
            cp.wait()

    return pl.pallas_call(
        body, name="grads_chip_exchange",
        in_specs=[HBM] * n, out_specs=[HBM] * n,
        out_shape=[jax.ShapeDtypeStruct((3, r // 2, c), BF) for r, c in shapes],
        scratch_shapes=[pltpu.SemaphoreType.DMA((n, 3)), pltpu.SemaphoreType.DMA((n, 3))],
    )(*p3s)


def _chip_sum(p3, land, shard_shape, axis, idx, name):
    r, c = shard_shape
    hr = r // 2
    tr = 64
    nt = hr // tr

    def body(idx_ref, p_ref, l_ref, o_ref):
        acc = p_ref[...].astype(F32)
        for k in range(3):
            acc = acc + l_ref[k].astype(F32)
        o_ref[...] = acc

    own = (pl.BlockSpec((None, tr, c), lambda i, idx: (idx[0], i, 0)) if axis == 0
           else pl.BlockSpec((None, tr, c), lambda i, idx: (0, i, idx[0])))
    return pl.pallas_call(
        body, name=name,
        grid_spec=pltpu.PrefetchScalarGridSpec(
            num_scalar_prefetch=1, grid=(nt,),
            in_specs=[own, pl.BlockSpec((3, tr, c), lambda i, idx: (0, i, 0))],
            out_specs=pl.BlockSpec((tr, c), lambda i, idx: (idx[1] * nt + i, 0))),
        out_shape=jax.ShapeDtypeStruct((r, c), F32),
        compiler_params=_params(("parallel",)),
    )(idx, p3, land)


def _rs_pair_gather(fulls):
    n = len(fulls)

    def body(*refs):
        ins, outs = refs[:n], refs[n:2 * n]
        send, recv = refs[2 * n:]
        x, y, c, _ = _place()
        cps = []
        for a in range(n):
            hr = fulls[a].shape[0] // 2
            rows = pl.ds(pl.multiple_of(c * hr, 8), hr)
            cp = pltpu.make_async_remote_copy(
                src_ref=ins[a].at[rows, :], dst_ref=outs[a].at[rows, :], send_sem=send.at[a], recv_sem=recv.at[a],
                device_id=(x, y, 1 - c), device_id_type=MESH)
            cp.start()
            cps.append(cp)
        for a, cp in enumerate(cps):
            cp.wait_send()
            hr = fulls[a].shape[0] // 2
            other = pl.ds(pl.multiple_of((1 - c) * hr, 8), hr)
            pltpu.make_async_remote_copy(
                src_ref=ins[a].at[other, :], dst_ref=outs[a].at[other, :], send_sem=send.at[a], recv_sem=recv.at[a],
                device_id=(x, y, 1 - c), device_id_type=MESH).wait_recv()

    return pl.pallas_call(
        body, name="grads_pair_gather",
        in_specs=[HBM] * n, out_specs=[HBM] * n,
        out_shape=[jax.ShapeDtypeStruct(f.shape, F32) for f in fulls],
        input_output_aliases={a: a for a in range(n)},
        scratch_shapes=[pltpu.SemaphoreType.DMA((n,)), pltpu.SemaphoreType.DMA((n,))],
    )(*fulls)


def _reduce_scatter(grads, shapes):
    x, y, c = lax.axis_index("x"), lax.axis_index("y"), lax.axis_index("c")
    cidx = jnp.reshape(c, (1,)).astype(jnp.int32)
    idx = jnp.stack([2 * x + y, c]).astype(jnp.int32)
    g3s = [_as3d(g, s, ax) for g, s, ax in zip(grads, shapes, WEIGHT_AXES)]
    lands = _rs_pair_exchange(g3s)
    p3s = [_pair_sum(g3, l, cidx, f"pair_sum_{a}") for a, (g3, l) in enumerate(zip(g3s, lands))]
    lands2 = _rs_chip_exchange(p3s, shapes)
    fulls = [_chip_sum(p3, l2, s, ax, idx, f"chip_sum_{a}")
             for a, (p3, l2, s, ax) in enumerate(zip(p3s, lands2, shapes, WEIGHT_AXES))]
    return _rs_pair_gather(fulls)


NSMALL = 8


def _small_all_reduce(g_nw, dlb, g_hnw, g_wf, loss_vec):
    def body(nw_ref, lb_ref, hn_ref, wf_ref, ls_ref, out_ref, pack_ref, buf_ref, send, recv):
        x, y, c = lax.axis_index("x"), lax.axis_index("y"), lax.axis_index("c")
        me = 4 * x + 2 * y + c
        pack_ref[...] = jnp.zeros_like(pack_ref)
        pack_ref[0:1, :] = nw_ref[...]
        pack_ref[1:2, :] = lb_ref[...]
        pack_ref[2:3, 0:HK] = hn_ref[...]
        pack_ref[3:4, :] = wf_ref[...]
        pack_ref[4:5, :] = ls_ref[...]
        buf_ref[me] = pack_ref[...]
        cps = []
        for d in range(1, 8):
            dx, dy, dc = d >> 2, (d >> 1) & 1, d & 1
            peer = (1 - x if dx else x, 1 - y if dy else y, 1 - c if dc else c)
            cp = pltpu.make_async_remote_copy(
                src_ref=pack_ref, dst_ref=buf_ref.at[me], send_sem=send.at[d - 1], recv_sem=recv.at[d - 1],
                device_id=peer, device_id_type=MESH)
            cp.start()
            cps.append(cp)
        for d in range(1, 8):
            dx, dy, dc = d >> 2, (d >> 1) & 1, d & 1
            src = 4 * (1 - x if dx else x) + 2 * (1 - y if dy else y) + (1 - c if dc else c)
            pltpu.make_async_remote_copy(
                src_ref=pack_ref, dst_ref=buf_ref.at[src], send_sem=send.at[d - 1], recv_sem=recv.at[d - 1],
                device_id=(x, y, c), device_id_type=MESH).wait_recv()
        for cp in cps:
            cp.wait_send()
        acc = buf_ref[0]
        for i in range(1, 8):
            acc = acc + buf_ref[i]
        out_ref[...] = acc

    vm = pl.BlockSpec(memory_space=pltpu.VMEM)
    return pl.pallas_call(
        body, name="small_all_reduce",
        in_specs=[vm] * 5, out_specs=vm,
        out_shape=jax.ShapeDtypeStruct((NSMALL, D), F32),
        scratch_shapes=[pltpu.VMEM((NSMALL, D), F32), pltpu.VMEM((8, NSMALL, D), F32),
                        pltpu.SemaphoreType.DMA((7,)), pltpu.SemaphoreType.DMA((7,))],
    )(g_nw, dlb, g_hnw, g_wf, loss_vec)


def _adamw_math(w, g, m, v):
    m = B1 * m + (1.0 - B1) * g
    v = B2 * v + (1.0 - B2) * (g * g)
    m_hat = m / (1.0 - B1 ** STEP)
    v_hat = v / (1.0 - B2 ** STEP)
    return -LR * (m_hat / (jnp.sqrt(v_hat) + ADAM_EPS) + WD * w), m, v


def _adamw(w, g, m, v, name):
    r, c = w.shape
    tr = 64

    def body(w_ref, g_ref, m_ref, v_ref, d_ref, nm_ref, nv_ref):
        d_ref[...], nm_ref[...], nv_ref[...] = _adamw_math(w_ref[...], g_ref[...], m_ref[...], v_ref[...])

    blk = pl.BlockSpec((tr, c), lambda i: (i, 0))
    return pl.pallas_call(
        body, name=name, grid=(r // tr,), in_specs=[blk] * 4, out_specs=[blk] * 3,
        out_shape=[jax.ShapeDtypeStruct((r, c), F32)] * 3,
        compiler_params=_params(("parallel",)),
    )(w, g, m, v)


def _small_update(red, lbl, params):
    def body(red_ref, *refs):
        ins, outs = refs[:12], refs[12:]
        lb = _lower_bound(ins[3][...])
        dl0 = red_ref[1:2, :] * lb * (1.0 - lb)
        row = lax.broadcasted_iota(jnp.int32, (2, D), 0)
        grads = [red_ref[0:1, :], jnp.where(row == 0, dl0, -dl0), red_ref[2:3, 0:HK], red_ref[3:4, :]]
        for i, g in enumerate(grads):
            w, m, v = ins[3 * i][...], ins[3 * i + 1][...], ins[3 * i + 2][...]
            d, nm, nv = _adamw_math(w, g, m, v)
            outs[4 * i][...] = g
            outs[4 * i + 1][...] = d
            outs[4 * i + 2][...] = nm
            outs[4 * i + 3][...] = nv
        outs[16][...] = jnp.sum(red_ref[4:5, :], axis=1, keepdims=True)

    flat = [a for p in params for a in p]
    vm = pl.BlockSpec(memory_space=pltpu.VMEM)
    shapes = [jax.ShapeDtypeStruct(p[0].shape, F32) for p in params for _ in range(4)]
    return pl.pallas_call(
        body, name="small_update",
        in_specs=[vm] * 13, out_specs=[vm] * 17,
        out_shape=shapes + [jax.ShapeDtypeStruct((1, 1), F32)],
    )(red, *flat)


def kernel(x, positions, norm_w, w_in, lb_logits, hgrn_norm_w, w_branch_a, w_branch_b, w_out, final_norm_w, loss_target, m_norm_w, m_w_in, m_lb_logits, m_hgrn_norm_w, m_w_branch_a, m_w_branch_b, m_w_out, m_final_norm_w, v_norm_w, v_w_in, v_lb_logits, v_hgrn_norm_w, v_w_branch_a, v_w_branch_b, v_w_out, v_final_norm_w):
    big_w = [w_in[0], w_branch_a[0], w_branch_b[0], w_out[0]]
    big_m = [m_w_in[0], m_w_branch_a[0], m_w_branch_b[0], m_w_out[0]]
    big_v = [v_w_in[0], v_w_branch_a[0], v_w_branch_b[0], v_w_out[0]]
    shapes = [w.shape for w in big_w]
    wf = final_norm_w.reshape(1, D)

    full = _all_gather([w.astype(BF) for w in big_w])
    loc = _local_step(x[0], positions.reshape(T, 1), norm_w, lb_logits, hgrn_norm_w, wf, loss_target[0], *full)
    g_big = _reduce_scatter([loc["g_win"], loc["g_wa"], loc["g_wb"], loc["g_wout"]], shapes)
    red = _small_all_reduce(loc["g_nw"], loc["dlb"], loc["g_hnw"], loc["g_wf"], loc["loss_vec"])

    small = _small_update(red, lb_logits, [
        (norm_w, m_norm_w, v_norm_w), (lb_logits, m_lb_logits, v_lb_logits),
        (hgrn_norm_w, m_hgrn_norm_w, v_hgrn_norm_w),
        (wf, m_final_norm_w.reshape(1, D), v_final_norm_w.reshape(1, D))])
    loss = small[16].reshape(())
    sg, sd, sm, sv = ([small[4 * i + j] for i in range(4)] for j in range(4))
    for lst in (sg, sd, sm, sv):
        lst[3] = lst[3].reshape(D)
    upd = [_adamw(w, g, m, v, f"adamw_{a}") for a, (w, g, m, v) in enumerate(zip(big_w, g_big, big_m, big_v))]
    bg = [g[None] for g in g_big]
    bd, bm, bv = ([u[j][None] for u in upd] for j in range(3))

    def order(s, b):
        return [s[0], b[0], s[1], s[2], b[1], b[2], b[3], s[3]]

    return (loss, loc["gx"][None], *order(sg, bg), *order(sd, bd), *order(sm, bm), *order(sv, bv))
```

```python
import functools

import jax
import jax.numpy as jnp
from jax import lax
from jax.experimental import pallas as pl
from jax.experimental.pallas import tpu as pltpu

T = 2048
D = 1024
NIN = 11264
HEADS = 8
HK = 128
CH = 16
NCH = T // CH
ATT_GROUPS = ((128, 1), (512, 4), (2048, 16))
ATT_COL0 = 4096
AG_COL0 = 8704
GATE_COL0 = 9216
EPS = 1e-6
ROPE_THETA = 10000.0
LR, B1, B2, ADAM_EPS, WD, STEP = 0.001, 0.9, 0.999, 1e-08, 0.01, 10

F32 = jnp.float32
BF = jnp.bfloat16
VMEM_LIMIT = 56 * 1024 * 1024

_NN = (((1,), (0,)), ((), ()))
_NT = (((1,), (1,)), ((), ()))
_TN = (((0,), (0,)), ((), ()))


def _dot(a, b, dims=_NN):
    return lax.dot_general(a, b, dims, preferred_element_type=F32)


def _bdot(a, b, dims=_NN):
    return lax.dot_general(a.astype(BF), b.astype(BF), dims, preferred_element_type=F32)


def _sigmoid(x):
    return jax.nn.sigmoid(x)


def _params(sem=None):
    return pltpu.CompilerParams(dimension_semantics=sem, vmem_limit_bytes=VMEM_LIMIT)


def _matmul(a, b, *, ta=False, tb=False, out_dtype=F32, tm=512, tn=512, tk=None, name):
    m = a.shape[1] if ta else a.shape[0]
    kdim = a.shape[0] if ta else a.shape[1]
    n = b.shape[0] if tb else b.shape[1]
    tk = tk or kdim
    tm, tn = min(tm, m), min(tn, n)
    nm, nn, nk = m // tm, n // tn, kdim // tk
    dims = (((0 if ta else 1,), (1 if tb else 0,)), ((), ()))

    def body(a_ref, b_ref, o_ref, *scratch):
        prod = _bdot(a_ref[...], b_ref[...], dims)
        if nk == 1:
            o_ref[...] = prod.astype(out_dtype)
        else:
            acc = scratch[0]
            k = pl.program_id(2)

            @pl.when(k == 0)
            def _():
                acc[...] = prod

            @pl.when(k > 0)
            def _():
                acc[...] += prod

            @pl.when(k == nk - 1)
            def _():
                o_ref[...] = acc[...].astype(out_dtype)

    a_spec = pl.BlockSpec((tk, tm), lambda i, j, k: (k, i)) if ta else pl.BlockSpec((tm, tk), lambda i, j, k: (i, k))
    b_spec = pl.BlockSpec((tn, tk), lambda i, j, k: (j, k)) if tb else pl.BlockSpec((tk, tn), lambda i, j, k: (k, j))
    return pl.pallas_call(
        body, name=name, grid=(nm, nn, nk),
        in_specs=[a_spec, b_spec],
        out_specs=pl.BlockSpec((tm, tn), lambda i, j, k: (i, j)),
        out_shape=jax.ShapeDtypeStruct((m, n), out_dtype),
        scratch_shapes=[pltpu.VMEM((tm, tn), F32)] if nk > 1 else [],
        compiler_params=_params(("parallel", "parallel", "arbitrary")),
    )(a, b)


def _rmsnorm_fwd(x, w):
    tm = 256

    def body(x_ref, w_ref, h_ref):
        xv = x_ref[...]
        r = lax.rsqrt(jnp.mean(xv * xv, axis=-1, keepdims=True) + EPS)
        h_ref[...] = (xv * r * w_ref[...]).astype(BF)

    return pl.pallas_call(
        body, name="rmsnorm_fwd", grid=(T // tm,),
        in_specs=[pl.BlockSpec((tm, D), lambda i: (i, 0)), pl.BlockSpec((1, D), lambda i: (0, 0))],
        out_specs=pl.BlockSpec((tm, D), lambda i: (i, 0)),
        out_shape=jax.ShapeDtypeStruct((T, D), BF),
        compiler_params=_params(("parallel",)),
    )(x, w)


def _rmsnorm_bwd(x, dh, dout, w):
    tm = 256

    def body(x_ref, dh_ref, dout_ref, w_ref, gx_ref, gw_ref):
        @pl.when(pl.program_id(0) == 0)
        def _():
            gw_ref[...] = jnp.zeros_like(gw_ref)

        xv, dhv = x_ref[...], dh_ref[...]
        r = lax.rsqrt(jnp.mean(xv * xv, axis=-1, keepdims=True) + EPS)
        nrm = xv * r
        dn = dhv * w_ref[...]
        gw_ref[...] += jnp.sum(dhv * nrm, axis=0, keepdims=True)
        gx_ref[...] = dout_ref[...] + r * (dn - nrm * jnp.mean(dn * nrm, axis=-1, keepdims=True))

    row = pl.BlockSpec((tm, D), lambda i: (i, 0))
    vec = pl.BlockSpec((1, D), lambda i: (0, 0))
    return pl.pallas_call(
        body, name="rmsnorm_bwd", grid=(T // tm,),
        in_specs=[row, row, row, vec], out_specs=[row, vec],
        out_shape=[jax.ShapeDtypeStruct((T, D), F32), jax.ShapeDtypeStruct((1, D), F32)],
        compiler_params=_params(("arbitrary",)),
    )(x, dh, dout, w)


def _lower_bound(lbl):
    mx = jnp.max(lbl, axis=0, keepdims=True)
    e = jnp.exp(lbl - mx)
    return e[0:1] / jnp.sum(e, axis=0, keepdims=True)


def _cumsum_rows(g, rows):
    b = g
    sh = 1
    while sh < CH:
        b = b + jnp.where(rows >= sh, pltpu.roll(b, sh, axis=0), 0.0)
        sh *= 2
    return b


def _rev_cumsum_rows(g, rows):
    b = g
    sh = 1
    while sh < CH:
        b = b + jnp.where(rows < CH - sh, pltpu.roll(b, CH - sh, axis=0), 0.0)
        sh *= 2
    return b


SUB = CH // 2


def _direct_block(qb, kb, vb, bb, rows8):
    ob = jnp.zeros_like(qb)
    for s in range(SUB):
        e_s = jnp.exp(jnp.where(rows8 >= s, bb - bb[s:s + 1], -jnp.inf))
        ob = ob + jnp.sum(qb * e_s * kb[s:s + 1], axis=1, keepdims=True) * vb[s:s + 1]
    return ob


def _direct_block_bwd(qb, kb, vb, bb, dob, rows8, rowc8):
    dq = dk = dv = db = jnp.zeros_like(qb)
    for s in range(SUB):
        one = (rowc8 == s).astype(F32)
        ks, vs = kb[s:s + 1], vb[s:s + 1]
        e_s = jnp.exp(jnp.where(rows8 >= s, bb - bb[s:s + 1], -jnp.inf))
        qes = qb * e_s
        w = qes * ks
        a = jnp.sum(w, axis=1, keepdims=True)
        da = jnp.sum(dob * vs, axis=1, keepdims=True)
        dv = dv + one * jnp.sum(a * dob, axis=0, keepdims=True)
        dq = dq + da * e_s * ks
        dk = dk + one * jnp.sum(da * qes, axis=0, keepdims=True)
        u = da * w
        db = db + u - one * jnp.sum(u, axis=0, keepdims=True)
    return dq, dk, dv, db


def _cross_factors(q, k, b):
    ref = b[SUB - 1:SUB]
    e_hi, e_lo = jnp.exp(b[SUB:] - ref), jnp.exp(ref - b[:SUB])
    return q[SUB:] * e_hi, k[:SUB] * e_lo, e_hi, e_lo


def _intra_fwd(q, k, v, b, rows8):
    lo = _direct_block(q[:SUB], k[:SUB], v[:SUB], b[:SUB], rows8)
    hi = _direct_block(q[SUB:], k[SUB:], v[SUB:], b[SUB:], rows8)
    qe_hi, ke_lo, _, _ = _cross_factors(q, k, b)
    for s in range(SUB):
        hi = hi + jnp.sum(qe_hi * ke_lo[s:s + 1], axis=1, keepdims=True) * v[s:s + 1]
    return jnp.concatenate([lo, hi], axis=0)


def _intra_bwd(q, k, v, b, do, rows8, rowc8):
    dq_lo, dk_lo, dv_lo, db_lo = _direct_block_bwd(q[:SUB], k[:SUB], v[:SUB], b[:SUB], do[:SUB], rows8, rowc8)
    dq_hi, dk_hi, dv_hi, db_hi = _direct_block_bwd(q[SUB:], k[SUB:], v[SUB:], b[SUB:], do[SUB:], rows8, rowc8)
    qe_hi, ke_lo, e_hi, e_lo = _cross_factors(q, k, b)
    do_hi, v_lo = do[SUB:], v[:SUB]
    dqe = dke = jnp.zeros_like(qe_hi)
    for s in range(SUB):
        one = (rowc8 == s).astype(F32)
        a = jnp.sum(qe_hi * ke_lo[s:s + 1], axis=1, keepdims=True)
        da = jnp.sum(do_hi * v_lo[s:s + 1], axis=1, keepdims=True)
        dv_lo = dv_lo + one * jnp.sum(a * do_hi, axis=0, keepdims=True)
        dqe = dqe + da * ke_lo[s:s + 1]
        dke = dke + one * jnp.sum(da * qe_hi, axis=0, keepdims=True)
    u_hi, u_lo = dqe * qe_hi, dke * ke_lo
    d_ref = jnp.sum(u_lo, axis=0, keepdims=True) - jnp.sum(u_hi, axis=0, keepdims=True)
    db_lo = db_lo - u_lo + (rowc8 == SUB - 1).astype(F32) * d_ref
    cat = lambda lo, hi: jnp.concatenate([lo, hi], axis=0)
    return (cat(dq_lo, dq_hi + dqe * e_hi), cat(dk_lo + dke * e_lo, dk_hi), cat(dv_lo, dv_hi),
            cat(db_lo, db_hi + u_hi))


def _hgrn_fwd(z, lbl, nw):
    def body(hq_ref, hf_ref, hi_ref, hg_ref, lbl_ref, nw_ref, oraw_ref, og_ref, sh_ref, st_ref):
        @pl.when(pl.program_id(0) == 0)
        def _():
            st_ref[...] = jnp.zeros_like(st_ref)

        lb_all = _lower_bound(lbl_ref[...])
        rows = lax.broadcasted_iota(jnp.int32, (CH, HK), 0)
        rows8 = lax.broadcasted_iota(jnp.int32, (SUB, HK), 0)
        nwv = nw_ref[...]
        for h in range(HEADS):
            sl = slice(HK * h, HK * (h + 1))
            lb = lb_all[:, sl]
            hq, hf, v, hg = hq_ref[:, sl], hf_ref[:, sl], hi_ref[:, sl], hg_ref[:, sl]
            q = hq * _sigmoid(hq)
            f = lb + (1.0 - lb) * _sigmoid(hf)
            k = 1.0 - f
            b = _cumsum_rows(jnp.log(f), rows)
            sh_ref[0, h] = st_ref[h]
            o = _bdot(q * jnp.exp(b), st_ref[h], _NT) + _intra_fwd(q, k, v, b, rows8)
            bl = b[CH - 1:CH]
            st_ref[h] = st_ref[h] * jnp.exp(bl)
            st_ref[h] += _bdot(v, k * jnp.exp(bl - b), _TN)
            oraw_ref[:, sl] = o
            nrm = o * lax.rsqrt(jnp.mean(o * o, axis=1, keepdims=True) + EPS)
            og_ref[:, sl] = (nrm * nwv * (hg * _sigmoid(hg))).astype(BF)

    zblk = lambda c: pl.BlockSpec((CH, D), lambda i, c=c: (i, c))
    return pl.pallas_call(
        body, name="hgrn_fwd", grid=(NCH,),
        in_specs=[zblk(0), zblk(1), zblk(2), zblk(3),
                  pl.BlockSpec((2, D), lambda i: (0, 0)), pl.BlockSpec((1, HK), lambda i: (0, 0))],
        out_specs=[pl.BlockSpec((CH, D), lambda i: (i, 0)), pl.BlockSpec((CH, D), lambda i: (i, 0)),
                   pl.BlockSpec((1, HEADS, HK, HK), lambda i: (i, 0, 0, 0))],
        out_shape=[jax.ShapeDtypeStruct((T, D), F32), jax.ShapeDtypeStruct((T, D), BF),
                   jax.ShapeDtypeStruct((NCH, HEADS, HK, HK), F32)],
        scratch_shapes=[pltpu.VMEM((HEADS, HK, HK), F32)],
        compiler_params=_params(("arbitrary",)),
    )(z, z, z, z, lbl, nw)


def _hgrn_bwd(z, lbl, nw, oraw, dog, shist):
    def body(hq_ref, hf_ref, hi_ref, hg_ref, lbl_ref, nw_ref, oraw_ref, dog_ref, sh_ref,
             dz_ref, dlb_ref, dnw_ref, dst_ref):
        @pl.when(pl.program_id(0) == 0)
        def _():
            dst_ref[...] = jnp.zeros_like(dst_ref)
            dlb_ref[...] = jnp.zeros_like(dlb_ref)
            dnw_ref[...] = jnp.zeros_like(dnw_ref)

        lb_all = _lower_bound(lbl_ref[...])
        rows = lax.broadcasted_iota(jnp.int32, (CH, HK), 0)
        rowc = lax.broadcasted_iota(jnp.int32, (CH, 1), 0)
        rows8 = lax.broadcasted_iota(jnp.int32, (SUB, HK), 0)
        rowc8 = lax.broadcasted_iota(jnp.int32, (SUB, 1), 0)
        nwv = nw_ref[...]
        dnw = jnp.zeros((1, HK), F32)
        for h in range(HEADS):
            sl = slice(HK * h, HK * (h + 1))
            lb = lb_all[:, sl]
            hq, hf, v, hg = hq_ref[:, sl], hf_ref[:, sl], hi_ref[:, sl], hg_ref[:, sl]
            o, dg_out = oraw_ref[:, sl], dog_ref[:, sl]
            sg = _sigmoid(hg)
            sil = hg * sg
            r = lax.rsqrt(jnp.mean(o * o, axis=1, keepdims=True) + EPS)
            nrm = o * r
            d_hg = dg_out * (nrm * nwv) * (sg * (1.0 + hg * (1.0 - sg)))
            dn = dg_out * nwv * sil
            dnw = dnw + jnp.sum(dg_out * nrm * sil, axis=0, keepdims=True)
            do = r * (dn - nrm * jnp.mean(dn * nrm, axis=1, keepdims=True))
            sq = _sigmoid(hq)
            q = hq * sq
            sig = _sigmoid(hf)
            f = lb + (1.0 - lb) * sig
            k = 1.0 - f
            b = _cumsum_rows(jnp.log(f), rows)
            eb = jnp.exp(b)
            qe = q * eb
            bl = b[CH - 1:CH]
            ebl = jnp.exp(bl)
            kdec = jnp.exp(bl - b)
            ke = k * kdec
            dqe = _bdot(do, sh_ref[0, h])
            dq = dqe * eb
            db = dqe * qe
            dke = _bdot(v, dst_ref[h])
            dv = _bdot(ke, dst_ref[h], _NT)
            dk = dke * kdec
            rr = dke * ke
            db = db - rr
            db_last = (jnp.sum(rr, axis=0, keepdims=True)
                       + ebl * jnp.sum(dst_ref[h] * sh_ref[0, h], axis=0, keepdims=True))
            dst_ref[h] = dst_ref[h] * ebl
            dst_ref[h] += _bdot(do, qe, _TN)
            dq_i, dk_i, dv_i, db_i = _intra_bwd(q, k, v, b, do, rows8, rowc8)
            dq, dk, dv = dq + dq_i, dk + dk_i, dv + dv_i
            db = db + db_i + (rowc == CH - 1).astype(F32) * db_last
            dgl = _rev_cumsum_rows(db, rows)
            df = dgl / f - dk
            dlb_ref[:, sl] += jnp.sum(df * (1.0 - sig), axis=0, keepdims=True)
            dz_ref[:, sl] = (dq * (sq * (1.0 + hq * (1.0 - sq)))).astype(BF)
            dz_ref[:, D + HK * h:D + HK * (h + 1)] = (df * (1.0 - lb) * sig * (1.0 - sig)).astype(BF)
            dz_ref[:, 2 * D + HK * h:2 * D + HK * (h + 1)] = dv.astype(BF)
            dz_ref[:, 3 * D + HK * h:3 * D + HK * (h + 1)] = d_hg.astype(BF)
        dnw_ref[...] += dnw

    rev = lambda i: NCH - 1 - i
    zblk = lambda c: pl.BlockSpec((CH, D), lambda i, c=c: (rev(i), c))
    return pl.pallas_call(
        body, name="hgrn_bwd", grid=(NCH,),
        in_specs=[zblk(0), zblk(1), zblk(2), zblk(3),
                  pl.BlockSpec((2, D), lambda i: (0, 0)), pl.BlockSpec((1, HK), lambda i: (0, 0)),
                  pl.BlockSpec((CH, D), lambda i: (rev(i), 0)), pl.BlockSpec((CH, D), lambda i: (rev(i), 0)),
                  pl.BlockSpec((1, HEADS, HK, HK), lambda i: (rev(i), 0, 0, 0))],
        out_specs=[pl.BlockSpec((CH, 4 * D), lambda i: (rev(i), 0)),
                   pl.BlockSpec((1, D), lambda i: (0, 0)), pl.BlockSpec((1, HK), lambda i: (0, 0))],
        out_shape=[jax.ShapeDtypeStruct((T, 4 * D), BF), jax.ShapeDtypeStruct((1, D), F32),
                   jax.ShapeDtypeStruct((1, HK), F32)],
        scratch_shapes=[pltpu.VMEM((HEADS, HK, HK), F32)],
        compiler_params=_params(("arbitrary",)),
    )(z, z, z, z, lbl, nw, oraw, dog, shist)


BLK = 128
NBLK = T // BLK
QK_SCALE = 0.125


def _head_masks():
    lane = lax.broadcasted_iota(jnp.int32, (1, BLK), 1)
    return [(lane < 64).astype(F32), (lane >= 64).astype(F32)]


def _pieces(dil):
    m = T // dil
    out = []
    for r in range(dil):
        for j in range(m // BLK):
            start = r + dil * BLK * j
            rows = pl.ds(start, BLK, stride=dil) if dil > 1 else pl.ds(start, BLK)
            out.append((rows, r * m + BLK * j))
    return out


def _rope_tables(pos_ref, invf_ref, cos_ref, sa_ref, sb_ref):
    lane = lax.broadcasted_iota(jnp.int32, (256, BLK), 1)
    first = (lane % 64) < 32
    for c in range(T // 256):
        rows = pl.ds(256 * c, 256)
        ang = pos_ref[rows, :].astype(F32) * invf_ref[...]
        s = jnp.sin(ang)
        cos_ref[rows, :] = jnp.cos(ang)
        sa_ref[rows, :] = jnp.where(first, -s, 0.0)
        sb_ref[rows, :] = jnp.where(first, 0.0, s)


def _rope(x, c, sa, sb):
    return x * c + pltpu.roll(x, 96, axis=1) * sa + pltpu.roll(x, 32, axis=1) * sb


def _rope_t(d, c, sa, sb):
    return d * c + pltpu.roll(d * sa, 32, axis=1) + pltpu.roll(d * sb, 96, axis=1)


def _window_bias(bias_ref):
    ii = lax.broadcasted_iota(jnp.int32, (2 * BLK, BLK), 0) % BLK
    jj = lax.broadcasted_iota(jnp.int32, (2 * BLK, BLK), 1)
    bias_ref[0] = jnp.where(jj <= ii, 0.0, -jnp.inf)
    bias_ref[1] = jnp.where(jj >= ii, 0.0, -jnp.inf)


def _stack_heads(x, masks):
    return jnp.concatenate([x * masks[0], x * masks[1]], axis=0).astype(BF)


def _attn_fwd(z, pos, invf):
    def body(q_ref, k_ref, v_ref, ag_ref, pos_ref, invf_ref, ob_ref, opre_ref, lse_ref,
             cos_ref, sa_ref, sb_ref, bias_ref, qr_ref, kr_ref, vr_ref, og_ref, lg_ref, otok_ref, ltok_ref):
        g = pl.program_id(1)
        masks = _head_masks()

        @pl.when(g == 0)
        def _():
            _rope_tables(pos_ref, invf_ref, cos_ref, sa_ref, sb_ref)
            _window_bias(bias_ref)

        def group(gi):
            dil = ATT_GROUPS[gi][1]
            nblk = (T // dil) // BLK
            for rows, dst in _pieces(dil):
                c, sa, sb = cos_ref[rows, :], sa_ref[rows, :], sb_ref[rows, :]
                qr_ref[pl.ds(dst, BLK), :] = _rope(q_ref[rows, :], c, sa, sb) * QK_SCALE
                kr_ref[pl.ds(dst, BLK), :] = _rope(k_ref[rows, :], c, sa, sb)
                vr_ref[pl.ds(dst, BLK), :] = v_ref[rows, :]

            def blk(bi, carry):
                cur = pl.ds(pl.multiple_of(bi * BLK, BLK), BLK)
                prev = pl.ds(pl.multiple_of(jnp.maximum(bi - 1, 0) * BLK, BLK), BLK)
                q2 = _stack_heads(qr_ref[cur, :], masks)
                kc, vc = kr_ref[cur, :].astype(BF), vr_ref[cur, :].astype(BF)
                s_c = _dot(q2, kc, _NT) + bias_ref[0]
                if nblk > 1:
                    kp, vp = kr_ref[prev, :].astype(BF), vr_ref[prev, :].astype(BF)
                    s_p = _dot(q2, kp, _NT) + (bias_ref[1] + jnp.where((bi % nblk) != 0, 0.0, -jnp.inf))
                    mx = jnp.max(jnp.maximum(s_c, s_p), axis=1, keepdims=True)
                    p_c, p_p = jnp.exp(s_c - mx), jnp.exp(s_p - mx)
                    den = jnp.sum(p_c + p_p, axis=1, keepdims=True)
                    oh = _dot(p_c.astype(BF), vc) + _dot(p_p.astype(BF), vp)
                else:
                    mx = jnp.max(s_c, axis=1, keepdims=True)
                    p_c = jnp.exp(s_c - mx)
                    den = jnp.sum(p_c, axis=1, keepdims=True)
                    oh = _dot(p_c.astype(BF), vc)
                on = oh / den
                lsev = jnp.broadcast_to(mx + jnp.log(den), (2 * BLK, BLK))
                og_ref[cur, :] = on[:BLK] * masks[0] + on[BLK:] * masks[1]
                lg_ref[0, cur, :] = lsev[:BLK]
                lg_ref[1, cur, :] = lsev[BLK:]
                return carry

            lax.fori_loop(0, NBLK, blk, 0)
            for rows, src in _pieces(dil):
                srows = pl.ds(src, BLK)
                otok_ref[gi, rows, :] = og_ref[srows, :]
                ltok_ref[gi, 0, rows, :] = lg_ref[0, srows, :]
                ltok_ref[gi, 1, rows, :] = lg_ref[1, srows, :]

        for gi in range(3):
            pl.when(g == gi)(functools.partial(group, gi))

        @pl.when(g == 2)
        def _():
            for c in range(T // BLK):
                rows = pl.ds(BLK * c, BLK)
                wts = []
                for hh in range(2):
                    l0, l1, l2 = ltok_ref[0, hh, rows, :], ltok_ref[1, hh, rows, :], ltok_ref[2, hh, rows, :]
                    mx = jnp.maximum(jnp.maximum(l0, l1), l2)
                    lse = mx + jnp.log(jnp.exp(l0 - mx) + jnp.exp(l1 - mx) + jnp.exp(l2 - mx))
                    lse_ref[rows, BLK * hh:BLK * (hh + 1)] = lse
                    wts.append([jnp.exp(l0 - lse), jnp.exp(l1 - lse), jnp.exp(l2 - lse)])
                o = sum((wts[0][gi] * masks[0] + wts[1][gi] * masks[1]) * otok_ref[gi, rows, :] for gi in range(3))
                ag = ag_ref[rows, :]
                opre_ref[rows, :] = o
                ob_ref[rows, :] = (o * (ag * _sigmoid(ag))).astype(BF)

    c0 = ATT_COL0 // BLK
    zspec = lambda part: pl.BlockSpec((T, BLK), lambda p, g, part=part: (0, c0 + 12 * part + 4 * g + p))
    outspec = pl.BlockSpec((T, BLK), lambda p, g: (0, p))
    big = lambda: pltpu.VMEM((T, BLK), F32)
    return pl.pallas_call(
        body, name="attn_fwd", grid=(4, 3),
        in_specs=[zspec(0), zspec(1), zspec(2),
                  pl.BlockSpec((T, BLK), lambda p, g: (0, AG_COL0 // BLK + p)),
                  pl.BlockSpec((T, 1), lambda p, g: (0, 0)), pl.BlockSpec((1, BLK), lambda p, g: (0, 0))],
        out_specs=[outspec, outspec, pl.BlockSpec((T, 2 * BLK), lambda p, g: (0, p))],
        out_shape=[jax.ShapeDtypeStruct((T, 512), BF), jax.ShapeDtypeStruct((T, 512), F32),
                   jax.ShapeDtypeStruct((T, 8 * BLK), F32)],
        scratch_shapes=[big(), big(), big(), pltpu.VMEM((2, 2 * BLK, BLK), F32), big(), big(), big(), big(),
                        pltpu.VMEM((2, T, BLK), F32), pltpu.VMEM((3, T, BLK), F32), pltpu.VMEM((3, 2, T, BLK), F32)],
        compiler_params=_params(("parallel", "arbitrary")),
    )(z, z, z, z, pos, invf)


def _attn_bwd(z, pos, invf, opre, lse, dob):
    def body(q_ref, k_ref, v_ref, ag_ref, pos_ref, invf_ref, o_ref, lse0_ref, lse1_ref, dob_ref,
             dq_ref, dk_ref, dv_ref, dag_ref,
             cos_ref, sa_ref, sb_ref, bias_ref, dtok_ref, qr_ref, kr_ref, vr_ref, dor_ref, lr_ref, dr_ref,
             dqr_ref, dkr_ref, dvr_ref):
        g = pl.program_id(1)
        masks = _head_masks()

        @pl.when(g == 0)
        def _():
            _rope_tables(pos_ref, invf_ref, cos_ref, sa_ref, sb_ref)
            _window_bias(bias_ref)
            for c in range(T // BLK):
                rows = pl.ds(BLK * c, BLK)
                ag, dob_v, o = ag_ref[rows, :], dob_ref[rows, :], o_ref[rows, :]
                sg = _sigmoid(ag)
                dag_ref[rows, :] = (dob_v * o * (sg * (1.0 + ag * (1.0 - sg)))).astype(BF)
                prod = dob_v * (ag * sg) * o
                for hh, mh in enumerate(masks):
                    dtok_ref[hh, rows, :] = jnp.broadcast_to(jnp.sum(prod * mh, axis=1, keepdims=True), (BLK, BLK))

        def group(gi):
            dil = ATT_GROUPS[gi][1]
            nblk = (T // dil) // BLK
            for rows, dst in _pieces(dil):
                drows = pl.ds(dst, BLK)
                c, sa, sb = cos_ref[rows, :], sa_ref[rows, :], sb_ref[rows, :]
                qr_ref[drows, :] = _rope(q_ref[rows, :], c, sa, sb) * QK_SCALE
                kr_ref[drows, :] = _rope(k_ref[rows, :], c, sa, sb)
                vr_ref[drows, :] = v_ref[rows, :]
                ag = ag_ref[rows, :]
                dor_ref[drows, :] = dob_ref[rows, :] * (ag * _sigmoid(ag))
                for hh, lse_ref in enumerate((lse0_ref, lse1_ref)):
                    lr_ref[hh, drows, :] = lse_ref[rows, :]
                    dr_ref[hh, drows, :] = dtok_ref[hh, rows, :]
            dkr_ref[...] = jnp.zeros_like(dkr_ref)
            dvr_ref[...] = jnp.zeros_like(dvr_ref)

            def blk(bi, carry):
                cur = pl.ds(pl.multiple_of(bi * BLK, BLK), BLK)
                prev = pl.ds(pl.multiple_of(jnp.maximum(bi - 1, 0) * BLK, BLK), BLK)
                q2, do2 = _stack_heads(qr_ref[cur, :], masks), _stack_heads(dor_ref[cur, :], masks)
                lh = jnp.concatenate([lr_ref[0, cur, :], lr_ref[1, cur, :]], axis=0)
                dh = jnp.concatenate([dr_ref[0, cur, :], dr_ref[1, cur, :]], axis=0)
                kc, vc = kr_ref[cur, :].astype(BF), vr_ref[cur, :].astype(BF)
                p_c = jnp.exp(_dot(q2, kc, _NT) + bias_ref[0] - lh)
                ds_c = (p_c * (_dot(do2, vc, _NT) - dh)).astype(BF)
                dq2 = _dot(ds_c, kc)
                dkr_ref[cur, :] += _dot(ds_c, q2, _TN)
                dvr_ref[cur, :] += _dot(p_c.astype(BF), do2, _TN)
                if nblk > 1:
                    kp, vp = kr_ref[prev, :].astype(BF), vr_ref[prev, :].astype(BF)
                    bias_p = bias_ref[1] + jnp.where((bi % nblk) != 0, 0.0, -jnp.inf)
                    p_p = jnp.exp(_dot(q2, kp, _NT) + bias_p - lh)
                    ds_p = (p_p * (_dot(do2, vp, _NT) - dh)).astype(BF)
                    dq2 = dq2 + _dot(ds_p, kp)
                    dkr_ref[prev, :] += _dot(ds_p, q2, _TN)
                    dvr_ref[prev, :] += _dot(p_p.astype(BF), do2, _TN)
                dqr_ref[cur, :] = dq2[:BLK] * masks[0] + dq2[BLK:] * masks[1]
                return carry

            lax.fori_loop(0, NBLK, blk, 0)
            for rows, src in _pieces(dil):
                srows = pl.ds(src, BLK)
                c, sa, sb = cos_ref[rows, :], sa_ref[rows, :], sb_ref[rows, :]
                qr_ref[rows, :] = _rope_t(dqr_ref[srows, :] * QK_SCALE, c, sa, sb)
                kr_ref[rows, :] = _rope_t(dkr_ref[srows, :], c, sa, sb)
                vr_ref[rows, :] = dvr_ref[srows, :]
            for c in range(T // 256):
                rows = pl.ds(256 * c, 256)
                dq_ref[rows, :] = qr_ref[rows, :].astype(BF)
                dk_ref[rows, :] = kr_ref[rows, :].astype(BF)
                dv_ref[rows, :] = vr_ref[rows, :].astype(BF)

        for gi in range(3):
            pl.when(g == gi)(functools.partial(group, gi))

    c0 = ATT_COL0 // BLK
    zspec = lambda part: pl.BlockSpec((T, BLK), lambda p, g, part=part: (0, c0 + 12 * part + 4 * g + p))
    pspec = pl.BlockSpec((T, BLK), lambda p, g: (0, p))
    gspec = pl.BlockSpec((T, BLK), lambda p, g: (0, 4 * g + p))
    big = lambda: pltpu.VMEM((T, BLK), F32)
    two = lambda: pltpu.VMEM((2, T, BLK), F32)
    return pl.pallas_call(
        body, name="attn_bwd", grid=(4, 3),
        in_specs=[zspec(0), zspec(1), zspec(2),
                  pl.BlockSpec((T, BLK), lambda p, g: (0, AG_COL0 // BLK + p)),
                  pl.BlockSpec((T, 1), lambda p, g: (0, 0)), pl.BlockSpec((1, BLK), lambda p, g: (0, 0)),
                  pspec, pl.BlockSpec((T, BLK), lambda p, g: (0, 2 * p)),
                  pl.BlockSpec((T, BLK), lambda p, g: (0, 2 * p + 1)), pspec],
        out_specs=[gspec, gspec, gspec, pspec],
        out_shape=[jax.ShapeDtypeStruct((T, 1536), BF), jax.ShapeDtypeStruct((T, 1536), BF),
                   jax.ShapeDtypeStruct((T, 1536), BF), jax.ShapeDtypeStruct((T, 512), BF)],
        scratch_shapes=[big(), big(), big(), pltpu.VMEM((2, 2 * BLK, BLK), F32), two(), big(), big(), big(), big(),
                        two(), two(), big(), big(), big()],
        compiler_params=_params(("parallel", "arbitrary")),
    )(z, z, z, z, pos, invf, opre, lse, lse, dob)


def _merge_fwd(ya, yb, z):
    tm = 256

    def body(ya_ref, yb_ref, ga_ref, gb_ref, m_ref):
        m_ref[...] = (_sigmoid(ga_ref[...]) * ya_ref[...] + _sigmoid(gb_ref[...]) * yb_ref[...]).astype(BF)

    row = pl.BlockSpec((tm, D), lambda i: (i, 0))
    return pl.pallas_call(
        body, name="merge_fwd", grid=(T // tm,),
        in_specs=[row, row, pl.BlockSpec((tm, D), lambda i: (i, GATE_COL0 // D)),
                  pl.BlockSpec((tm, D), lambda i: (i, GATE_COL0 // D + 1))],
        out_specs=row, out_shape=jax.ShapeDtypeStruct((T, D), BF),
        compiler_params=_params(("parallel",)),
    )(ya, yb, z, z)


def _out_loss(merged, w_out, x, tgt, wf):
    tm = 256

    def body(m_ref, w_ref, x_ref, t_ref, wf_ref, dout_ref, loss_ref, gwf_ref):
        @pl.when(pl.program_id(0) == 0)
        def _():
            loss_ref[...] = jnp.zeros_like(loss_ref)
            gwf_ref[...] = jnp.zeros_like(gwf_ref)

        out = x_ref[...] + _dot(m_ref[...], w_ref[...])
        r = lax.rsqrt(jnp.mean(out * out, axis=-1, keepdims=True) + EPS)
        yh = out * r
        wfv = wf_ref[...]
        err = yh * wfv - t_ref[...]
        loss_ref[...] += jnp.sum(err * err, axis=0, keepdims=True) * (0.5 / D)
        dy = err * (1.0 / D)
        gwf_ref[...] += jnp.sum(dy * yh, axis=0, keepdims=True)
        dyh = dy * wfv
        dout_ref[...] = r * (dyh - yh * jnp.mean(dyh * yh, axis=-1, keepdims=True))

    row = pl.BlockSpec((tm, D), lambda i: (i, 0))
    vec = pl.BlockSpec((1, D), lambda i: (0, 0))
    return pl.pallas_call(
        body, name="out_loss", grid=(T // tm,),
        in_specs=[row, pl.BlockSpec((D, D), lambda i: (0, 0)), row, row, vec],
        out_specs=[row, vec, vec],
        out_shape=[jax.ShapeDtypeStruct((T, D), F32), jax.ShapeDtypeStruct((1, D), F32),
                   jax.ShapeDtypeStruct((1, D), F32)],
        compiler_params=_params(("arbitrary",)),
    )(merged, w_out, x, tgt, wf)


def _merge_bwd(dm, ya, yb, z):
    tm = 256

    def body(dm_ref, ya_ref, yb_ref, ga_ref, gb_ref, dya_ref, dyb_ref, dg_ref):
        dmv = dm_ref[...]
        sa, sb = _sigmoid(ga_ref[...]), _sigmoid(gb_ref[...])
        dya_ref[...] = (sa * dmv).astype(BF)
        dyb_ref[...] = (sb * dmv).astype(BF)
        dg_ref[:, :D] = (dmv * ya_ref[...] * sa * (1.0 - sa)).astype(BF)
        dg_ref[:, D:] = (dmv * yb_ref[...] * sb * (1.0 - sb)).astype(BF)

    row = pl.BlockSpec((tm, D), lambda i: (i, 0))
    return pl.pallas_call(
        body, name="merge_bwd", grid=(T // tm,),
        in_specs=[row, row, row, pl.BlockSpec((tm, D), lambda i: (i, GATE_COL0 // D)),
                  pl.BlockSpec((tm, D), lambda i: (i, GATE_COL0 // D + 1))],
        out_specs=[row, row, pl.BlockSpec((tm, 2 * D), lambda i: (i, 0))],
        out_shape=[jax.ShapeDtypeStruct((T, D), BF), jax.ShapeDtypeStruct((T, D), BF),
                   jax.ShapeDtypeStruct((T, 2 * D), BF)],
        compiler_params=_params(("parallel",)),
    )(dm, ya, yb, z, z)


def _rope_inv_freq():
    inv = ROPE_THETA ** (-jnp.arange(0, 64, 2, dtype=F32) / 64)
    return jnp.tile(inv, 4).reshape(1, BLK)


def _local_step(x, pos, norm_w, lbl, hnw, wf, tgt, w_in, w_a, w_b, w_out):
    invf = _rope_inv_freq()
    h = _rmsnorm_fwd(x, norm_w)
    z = _matmul(h, w_in, tm=1024, tn=512, name="z_proj")
    oraw, og, shist = _hgrn_fwd(z, lbl, hnw)
    ob, opre, lse = _attn_fwd(z, pos, invf)
    ya = _matmul(og, w_a, tm=1024, tn=512, name="ya_proj")
    yb = _matmul(ob, w_b, tm=1024, tn=512, name="yb_proj")
    merged = _merge_fwd(ya, yb, z)
    dout, loss_vec, g_wf = _out_loss(merged, w_out, x, tgt, wf)

    dm = _matmul(dout, w_out, tb=True, tm=1024, tn=512, name="d_merged")
    g_wout = _matmul(merged, dout, ta=True, out_dtype=BF, tm=512, tn=1024, name="g_wout")
    dya, dyb, dgates = _merge_bwd(dm, ya, yb, z)
    dog = _matmul(dya, w_a, tb=True, tm=1024, tn=512, name="d_og")
    g_wa = _matmul(og, dya, ta=True, out_dtype=BF, tm=512, tn=1024, name="g_wa")
    dob = _matmul(dyb, w_b, tb=True, tm=1024, tn=512, name="d_ob")
    g_wb = _matmul(ob, dyb, ta=True, out_dtype=BF, tm=512, tn=1024, name="g_wb")
    dz_h, dlb, g_hnw = _hgrn_bwd(z, lbl, hnw, oraw, dog, shist)
    dq, dk, dv, dag = _attn_bwd(z, pos, invf, opre, lse, dob)
    dz = jnp.concatenate([dz_h, dq, dk, dv, dag, dgates], axis=1)
    g_win = _matmul(h, dz, ta=True, out_dtype=BF, tm=1024, tn=512, name="g_win")
    dh = _matmul(dz, w_in, tb=True, tm=1024, tn=1024, tk=1408, name="d_h")
    gx, g_nw = _rmsnorm_bwd(x, dh, dout, norm_w)
    return dict(loss_vec=loss_vec, gx=gx, g_nw=g_nw, dlb=dlb, g_hnw=g_hnw, g_wf=g_wf,
                g_win=g_win, g_wa=g_wa, g_wb=g_wb, g_wout=g_wout)


MESH = pl.DeviceIdType.MESH
HBM = pl.BlockSpec(memory_space=pl.ANY)
WEIGHT_AXES = (1, 0, 1, 0)


def _place():
    x, y, c = lax.axis_index("x"), lax.axis_index("y"), lax.axis_index("c")
    chips = [(1 - x, y), (x, 1 - y), (1 - x, 1 - y)]
    return x, y, c, chips


def _block_half(ref, shard_shape, axis, j, half):
    r, c = shard_shape
    hr = r // 2
    if axis == 0:
        return ref.at[pl.ds(pl.multiple_of(j * r + half * hr, 16), hr), :]
    return ref.at[pl.ds(pl.multiple_of(half * hr, 16), hr), pl.ds(pl.multiple_of(j * c, 128), c)]


def _all_gather(shards):
    n = len(shards)
    shapes = [s.shape for s in shards]

    def body(*refs):
        ins, outs = refs[:n], refs[n:2 * n]
        send1, recv1, send2, recv2, send0, recv0 = refs[2 * n:]
        x, y, c, chips = _place()
        me = 2 * x + y
        started, own = [], []
        for a in range(n):
            r, cc = shapes[a]
            ax = WEIGHT_AXES[a]
            mine = (outs[a].at[pl.ds(pl.multiple_of(me * r, 16), r), :] if ax == 0
                    else outs[a].at[:, pl.ds(pl.multiple_of(me * cc, 128), cc)])
            swap = pltpu.make_async_remote_copy(
                src_ref=ins[a], dst_ref=mine, send_sem=send0.at[a], recv_sem=recv0.at[a],
                device_id=(x, y, 1 - c), device_id_type=MESH)
            swap.start()
            own.append(swap)
            src = ins[a].at[pl.ds(pl.multiple_of(c * (r // 2), 16), r // 2), :]
            for k, (px, py) in enumerate(chips):
                cp = pltpu.make_async_remote_copy(
                    src_ref=src, dst_ref=_block_half(outs[a], shapes[a], ax, me, c),
                    send_sem=send1.at[a, k], recv_sem=recv1.at[a, k], device_id=(px, py, c), device_id_type=MESH)
                cp.start()
                started.append(cp)
        for a in range(n):
            for k, (px, py) in enumerate(chips):
                reg = _block_half(outs[a], shapes[a], WEIGHT_AXES[a], 2 * px + py, c)
                pltpu.make_async_remote_copy(
                    src_ref=reg, dst_ref=reg, send_sem=send1.at[a, k], recv_sem=recv1.at[a, k],
                    device_id=(px, py, c), device_id_type=MESH).wait_recv()
                fw = pltpu.make_async_remote_copy(
                    src_ref=reg, dst_ref=reg, send_sem=send2.at[a, k], recv_sem=recv2.at[a, k],
                    device_id=(x, y, 1 - c), device_id_type=MESH)
                fw.start()
                started.append(fw)
        for a in range(n):
            for k, (px, py) in enumerate(chips):
                reg = _block_half(outs[a], shapes[a], WEIGHT_AXES[a], 2 * px + py, 1 - c)
                pltpu.make_async_remote_copy(
                    src_ref=reg, dst_ref=reg, send_sem=send2.at[a, k], recv_sem=recv2.at[a, k],
                    device_id=(x, y, 1 - c), device_id_type=MESH).wait_recv()
        for cp in started:
            cp.wait_send()
        for cp in own:
            cp.wait()

    full = [(4 * r, c) if ax == 0 else (r, 4 * c) for (r, c), ax in zip(shapes, WEIGHT_AXES)]
    return pl.pallas_call(
        body, name="weights_all_gather",
        in_specs=[HBM] * n, out_specs=[HBM] * n,
        out_shape=[jax.ShapeDtypeStruct(f, BF) for f in full],
        scratch_shapes=[pltpu.SemaphoreType.DMA((n, 3)), pltpu.SemaphoreType.DMA((n, 3)),
                        pltpu.SemaphoreType.DMA((n, 3)), pltpu.SemaphoreType.DMA((n, 3)),
                        pltpu.SemaphoreType.DMA((n,)), pltpu.SemaphoreType.DMA((n,))],
    )(*shards)


def _as3d(g, shard_shape, axis):
    r, c = shard_shape
    return g.reshape(4, r, c) if axis == 0 else g.reshape(1, r, 4 * c)


def _half_rows(ref3, hr, half):
    return ref3.at[:, pl.ds(pl.multiple_of(half * hr, 16), hr), :]


def _rs_pair_exchange(g3s):
    n = len(g3s)

    def body(*refs):
        ins, outs = refs[:n], refs[n:2 * n]
        send, recv = refs[2 * n:]
        x, y, c, _ = _place()
        cps = []
        for a in range(n):
            hr = g3s[a].shape[1] // 2
            cp = pltpu.make_async_remote_copy(
                src_ref=_half_rows(ins[a], hr, 1 - c), dst_ref=outs[a],
                send_sem=send.at[a], recv_sem=recv.at[a], device_id=(x, y, 1 - c), device_id_type=MESH)
            cp.start()
            cps.append(cp)
        for cp in cps:
            cp.wait()

    return pl.pallas_call(
        body, name="grads_pair_exchange",
        in_specs=[HBM] * n, out_specs=[HBM] * n,
        out_shape=[jax.ShapeDtypeStruct((g.shape[0], g.shape[1] // 2, g.shape[2]), BF) for g in g3s],
        scratch_shapes=[pltpu.SemaphoreType.DMA((n,)), pltpu.SemaphoreType.DMA((n,))],
    )(*g3s)


def _pair_sum(g3, land, cidx, name):
    nb, r, w = g3.shape
    hr = r // 2
    tr = 64

    def body(c_ref, g_ref, l_ref, o_ref):
        o_ref[...] = (g_ref[...].astype(F32) + l_ref[...].astype(F32)).astype(BF)

    blk = (nb, tr, w)
    return pl.pallas_call(
        body, name=name,
        grid_spec=pltpu.PrefetchScalarGridSpec(
            num_scalar_prefetch=1, grid=(hr // tr,),
            in_specs=[pl.BlockSpec(blk, lambda i, c: (0, c[0] * (hr // tr) + i, 0)),
                      pl.BlockSpec(blk, lambda i, c: (0, i, 0))],
            out_specs=pl.BlockSpec(blk, lambda i, c: (0, i, 0))),
        out_shape=jax.ShapeDtypeStruct((nb, hr, w), BF),
        compiler_params=_params(("parallel",)),
    )(cidx, g3, land)


def _rs_chip_exchange(p3s, shapes):
    n = len(p3s)

    def body(*refs):
        ins, outs = refs[:n], refs[n:2 * n]
        send, recv = refs[2 * n:]
        x, y, c, chips = _place()
        cps = []
        for a in range(n):
            r, cc = shapes[a]
            for k, (px, py) in enumerate(chips):
                j = 2 * px + py
                src = ins[a].at[j] if WEIGHT_AXES[a] == 0 else ins[a].at[0, :, pl.ds(pl.multiple_of(j * cc, 128), cc)]
                cp = pltpu.make_async_remote_copy(
                    src_ref=src, dst_ref=outs[a].at[k], send_sem=send.at[a, k], recv_sem=recv.at[a, k],
                    device_id=(px, py, c), device_id_type=MESH)
                cp.start()
                cps.append(cp)
        for cp in cps:
            cp.wait()

    return pl.pallas_call(
        body, name="grads_chip_exchange",
        in_specs=[HBM] * n, out_specs=[HBM] * n,
        out_shape=[jax.ShapeDtypeStruct((3, r // 2, c), BF) for r, c in shapes],
        scratch_shapes=[pltpu.SemaphoreType.DMA((n, 3)), pltpu.SemaphoreType.DMA((n, 3))],
    )(*p3s)


def _chip_sum(p3, land, shard_shape, axis, idx, name):
    r, c = shard_shape
    hr = r // 2
    tr = 64
    nt = hr // tr

    def body(idx_ref, p_ref, l_ref, o_ref):
        acc = p_ref[...].astype(F32)
        for k in range(3):
            acc = acc + l_ref[k].astype(F32)
        o_ref[...] = acc

    own = (pl.BlockSpec((None, tr, c), lambda i, idx: (idx[0], i, 0)) if axis == 0
           else pl.BlockSpec((None, tr, c), lambda i, idx: (0, i, idx[0])))
    return pl.pallas_call(
        body, name=name,
        grid_spec=pltpu.PrefetchScalarGridSpec(
            num_scalar_prefetch=1, grid=(nt,),
            in_specs=[own, pl.BlockSpec((3, tr, c), lambda i, idx: (0, i, 0))],
            out_specs=pl.BlockSpec((tr, c), lambda i, idx: (idx[1] * nt + i, 0))),
        out_shape=jax.ShapeDtypeStruct((r, c), F32),
        compiler_params=_params(("parallel",)),
    )(idx, p3, land)


def _rs_pair_gather(fulls):
    n = len(fulls)

    def body(*refs):
        ins, outs = refs[:n], refs[n:2 * n]
        send, recv = refs[2 * n:]
        x, y, c, _ = _place()
        cps = []
        for a in range(n):
            hr = fulls[a].shape[0] // 2
            rows = pl.ds(pl.multiple_of(c * hr, 8), hr)
            cp = pltpu.make_async_remote_copy(
                src_ref=ins[a].at[rows, :], dst_ref=outs[a].at[rows, :], send_sem=send.at[a], recv_sem=recv.at[a],
                device_id=(x, y, 1 - c), device_id_type=MESH)
            cp.start()
            cps.append(cp)
        for a, cp in enumerate(cps):
            cp.wait_send()
            hr = fulls[a].shape[0] // 2
            other = pl.ds(pl.multiple_of((1 - c) * hr, 8), hr)
            pltpu.make_async_remote_copy(
                src_ref=ins[a].at[other, :], dst_ref=outs[a].at[other, :], send_sem=send.at[a], recv_sem=recv.at[a],
                device_id=(x, y, 1 - c), device_id_type=MESH).wait_recv()

    return pl.pallas_call(
        body, name="grads_pair_gather",
        in_specs=[HBM] * n, out_specs=[HBM] * n,
        out_shape=[jax.ShapeDtypeStruct(f.shape, F32) for f in fulls],
        input_output_aliases={a: a for a in range(n)},
        scratch_shapes=[pltpu.SemaphoreType.DMA((n,)), pltpu.SemaphoreType.DMA((n,))],
    )(*fulls)


def _reduce_scatter(grads, shapes):
    x, y, c = lax.axis_index("x"), lax.axis_index("y"), lax.axis_index("c")
    cidx = jnp.reshape(c, (1,)).astype(jnp.int32)
    idx = jnp.stack([2 * x + y, c]).astype(jnp.int32)
    g3s = [_as3d(g, s, ax) for g, s, ax in zip(grads, shapes, WEIGHT_AXES)]
    lands = _rs_pair_exchange(g3s)
    p3s = [_pair_sum(g3, l, cidx, f"pair_sum_{a}") for a, (g3, l) in enumerate(zip(g3s, lands))]
    lands2 = _rs_chip_exchange(p3s, shapes)
    fulls = [_chip_sum(p3, l2, s, ax, idx, f"chip_sum_{a}")
             for a, (p3, l2, s, ax) in enumerate(zip(p3s, lands2, shapes, WEIGHT_AXES))]
    return _rs_pair_gather(fulls)


NSMALL = 8


def _small_all_reduce(g_nw, dlb, g_hnw, g_wf, loss_vec):
    def body(nw_ref, lb_ref, hn_ref, wf_ref, ls_ref, out_ref, pack_ref, buf_ref, send, recv):
        x, y, c = lax.axis_index("x"), lax.axis_index("y"), lax.axis_index("c")
        me = 4 * x + 2 * y + c
        pack_ref[...] = jnp.zeros_like(pack_ref)
        pack_ref[0:1, :] = nw_ref[...]
        pack_ref[1:2, :] = lb_ref[...]
        pack_ref[2:3, 0:HK] = hn_ref[...]
        pack_ref[3:4, :] = wf_ref[...]
        pack_ref[4:5, :] = ls_ref[...]
        buf_ref[me] = pack_ref[...]
        cps = []
        for d in range(1, 8):
            dx, dy, dc = d >> 2, (d >> 1) & 1, d & 1
            peer = (1 - x if dx else x, 1 - y if dy else y, 1 - c if dc else c)
            cp = pltpu.make_async_remote_copy(
                src_ref=pack_ref, dst_ref=buf_ref.at[me], send_sem=send.at[d - 1], recv_sem=recv.at[d - 1],
                device_id=peer, device_id_type=MESH)
            cp.start()
            cps.append(cp)
        for d in range(1, 8):
            dx, dy, dc = d >> 2, (d >> 1) & 1, d & 1
            src = 4 * (1 - x if dx else x) + 2 * (1 - y if dy else y) + (1 - c if dc else c)
            pltpu.make_async_remote_copy(
                src_ref=pack_ref, dst_ref=buf_ref.at[src], send_sem=send.at[d - 1], recv_sem=recv.at[d - 1],
                device_id=(x, y, c), device_id_type=MESH).wait_recv()
        for cp in cps:
            cp.wait_send()
        acc = buf_ref[0]
        for i in range(1, 8):
            acc = acc + buf_ref[i]
        out_ref[...] = acc

    vm = pl.BlockSpec(memory_space=pltpu.VMEM)
    return pl.pallas_call(
        body, name="small_all_reduce",
        in_specs=[vm] * 5, out_specs=vm,
        out_shape=jax.ShapeDtypeStruct((NSMALL, D), F32),
        scratch_shapes=[pltpu.VMEM((NSMALL, D), F32), pltpu.VMEM((8, NSMALL, D), F32),
                        pltpu.SemaphoreType.DMA((7,)), pltpu.SemaphoreType.DMA((7,))],
    )(g_nw, dlb, g_hnw, g_wf, loss_vec)


def _adamw_math(w, g, m, v):
    m = B1 * m + (1.0 - B1) * g
    v = B2 * v + (1.0 - B2) * (g * g)
    m_hat = m / (1.0 - B1 ** STEP)
    v_hat = v / (1.0 - B2 ** STEP)
    return -LR * (m_hat / (jnp.sqrt(v_hat) + ADAM_EPS) + WD * w), m, v


def _adamw(w, g, m, v, name):
    r, c = w.shape
    tr = 64

    def body(w_ref, g_ref, m_ref, v_ref, d_ref, nm_ref, nv_ref):
        d_ref[...], nm_ref[...], nv_ref[...] = _adamw_math(w_ref[...], g_ref[...], m_ref[...], v_ref[...])

    blk = pl.BlockSpec((tr, c), lambda i: (i, 0))
    return pl.pallas_call(
        body, name=name, grid=(r // tr,), in_specs=[blk] * 4, out_specs=[blk] * 3,
        out_shape=[jax.ShapeDtypeStruct((r, c), F32)] * 3,
        compiler_params=_params(("parallel",)),
    )(w, g, m, v)


def _small_update(red, lbl, params):
    def body(red_ref, *refs):
        ins, outs = refs[:12], refs[12:]
        lb = _lower_bound(ins[3][...])
        dl0 = red_ref[1:2, :] * lb * (1.0 - lb)
        row = lax.broadcasted_iota(jnp.int32, (2, D), 0)
        grads = [red_ref[0:1, :], jnp.where(row == 0, dl0, -dl0), red_ref[2:3, 0:HK], red_ref[3:4, :]]
        for i, g in enumerate(grads):
            w, m, v = ins[3 * i][...], ins[3 * i + 1][...], ins[3 * i + 2][...]
            d, nm, nv = _adamw_math(w, g, m, v)
            outs[4 * i][...] = g
            outs[4 * i + 1][...] = d
            outs[4 * i + 2][...] = nm
            outs[4 * i + 3][...] = nv
        outs[16][...] = jnp.sum(red_ref[4:5, :], axis=1, keepdims=True)

    flat = [a for p in params for a in p]
    vm = pl.BlockSpec(memory_space=pltpu.VMEM)
    shapes = [jax.ShapeDtypeStruct(p[0].shape, F32) for p in params for _ in range(4)]
    return pl.pallas_call(
        body, name="small_update",
        in_specs=[vm] * 13, out_specs=[vm] * 17,
        out_shape=shapes + [jax.ShapeDtypeStruct((1, 1), F32)],
    )(red, *flat)


def kernel(x, positions, norm_w, w_in, lb_logits, hgrn_norm_w, w_branch_a, w_branch_b, w_out, final_norm_w, loss_target, m_norm_w, m_w_in, m_lb_logits, m_hgrn_norm_w, m_w_branch_a, m_w_branch_b, m_w_out, m_final_norm_w, v_norm_w, v_w_in, v_lb_logits, v_hgrn_norm_w, v_w_branch_a, v_w_branch_b, v_w_out, v_final_norm_w):
    big_w = [w_in[0], w_branch_a[0], w_branch_b[0], w_out[0]]
    big_m = [m_w_in[0], m_w_branch_a[0], m_w_branch_b[0], m_w_out[0]]
    big_v = [v_w_in[0], v_w_branch_a[0], v_w_branch_b[0], v_w_out[0]]
    shapes = [w.shape for w in big_w]
    wf = final_norm_w.reshape(1, D)

    full = _all_gather([w.astype(BF) for w in big_w])
    loc = _local_step(x[0], positions.reshape(T, 1), norm_w, lb_logits, hgrn_norm_w, wf, loss_target[0], *full)
    g_big = _reduce_scatter([loc["g_win"], loc["g_wa"], loc["g_wb"], loc["g_wout"]], shapes)
    red = _small_all_reduce(loc["g_nw"], loc["dlb"], loc["g_hnw"], loc["g_wf"], loc["loss_vec"])

    small = _small_update(red, lb_logits, [
        (norm_w, m_norm_w, v_norm_w), (lb_logits, m_lb_logits, v_lb_logits),
        (hgrn_norm_w, m_hgrn_norm_w, v_hgrn_norm_w),
        (wf, m_final_norm_w.reshape(1, D), v_final_norm_w.reshape(1, D))])
    loss = small[16].reshape(())
    sg, sd, sm, sv = ([small[4 * i + j] for i in range(4)] for j in range(4))
    for lst in (sg, sd, sm, sv):
        lst[3] = lst[3].reshape(D)
    upd = [_adamw(w, g, m, v, f"adamw_{a}") for a, (w, g, m, v) in enumerate(zip(big_w, g_big, big_m, big_v))]
    bg = [g[None] for g in g_big]
    bd, bm, bv = ([u[j][None] for u in upd] for j in range(3))

    def order(s, b):
        return [s[0], b[0], s[1], s[2], b[1], b[2], b[3], s[3]]

    return (loss, loc["gx"][None], *order(sg, bg), *order(sd, bd), *order(sm, bm), *order(sv, bv))
```

```python
import functools

import jax
import jax.numpy as jnp
from jax import lax
from jax.experimental import pallas as pl
from jax.experimental.pallas import tpu as pltpu

T = 2048
D = 1024
NIN = 11264
HEADS = 8
HK = 128
CH = 16
NCH = T // CH
HSTEP = 2
ATT_GROUPS = ((128, 1), (512, 4), (2048, 16))
ATT_COL0 = 4096
AG_COL0 = 8704
GATE_COL0 = 9216
EPS = 1e-6
ROPE_THETA = 10000.0
LR, B1, B2, ADAM_EPS, WD, STEP = 0.001, 0.9, 0.999, 1e-08, 0.01, 10

F32 = jnp.float32
BF = jnp.bfloat16
VMEM_LIMIT = 56 * 1024 * 1024

_NN = (((1,), (0,)), ((), ()))
_NT = (((1,), (1,)), ((), ()))
_TN = (((0,), (0,)), ((), ()))


def _dot(a, b, dims=_NN):
    return lax.dot_general(a, b, dims, preferred_element_type=F32)


def _bdot(a, b, dims=_NN):
    return lax.dot_general(a.astype(BF), b.astype(BF), dims, preferred_element_type=F32)


def _sigmoid(x):
    return jax.nn.sigmoid(x)


def _params(sem=None):
    return pltpu.CompilerParams(dimension_semantics=sem, vmem_limit_bytes=VMEM_LIMIT)


def _matmul(a, b, *, ta=False, tb=False, out_dtype=F32, tm=512, tn=512, tk=None, name):
    m = a.shape[1] if ta else a.shape[0]
    kdim = a.shape[0] if ta else a.shape[1]
    n = b.shape[0] if tb else b.shape[1]
    tk = tk or kdim
    tm, tn = min(tm, m), min(tn, n)
    nm, nn, nk = m // tm, n // tn, kdim // tk
    dims = (((0 if ta else 1,), (1 if tb else 0,)), ((), ()))

    def body(a_ref, b_ref, o_ref, *scratch):
        prod = _bdot(a_ref[...], b_ref[...], dims)
        if nk == 1:
            o_ref[...] = prod.astype(out_dtype)
        else:
            acc = scratch[0]
            k = pl.program_id(2)

            @pl.when(k == 0)
            def _():
                acc[...] = prod

            @pl.when(k > 0)
            def _():
                acc[...] += prod

            @pl.when(k == nk - 1)
            def _():
                o_ref[...] = acc[...].astype(out_dtype)

    a_spec = pl.BlockSpec((tk, tm), lambda i, j, k: (k, i)) if ta else pl.BlockSpec((tm, tk), lambda i, j, k: (i, k))
    b_spec = pl.BlockSpec((tn, tk), lambda i, j, k: (j, k)) if tb else pl.BlockSpec((tk, tn), lambda i, j, k: (k, j))
    return pl.pallas_call(
        body, name=name, grid=(nm, nn, nk),
        in_specs=[a_spec, b_spec],
        out_specs=pl.BlockSpec((tm, tn), lambda i, j, k: (i, j)),
        out_shape=jax.ShapeDtypeStruct((m, n), out_dtype),
        scratch_shapes=[pltpu.VMEM((tm, tn), F32)] if nk > 1 else [],
        compiler_params=_params(("parallel", "parallel", "arbitrary")),
    )(a, b)


DZ_TILE = 512


def _part_offsets(parts):
    counts = [p.shape[1] // DZ_TILE for p in parts]
    offs = [sum(counts[:i]) for i in range(len(parts))]
    return counts, offs


def _part_spec(rows, cnt, off, tile_axis):
    def index(*g):
        return (0 if rows is None else g[0], jnp.clip(g[tile_axis] - off, 0, cnt - 1))
    return index


def _grad_w_in(h, parts):
    counts, offs = _part_offsets(parts)
    n = len(parts)

    def body(h_ref, *refs):
        o_ref = refs[n]
        j = pl.program_id(0)
        for p_ref, cnt, off in zip(refs[:n], counts, offs):
            @pl.when((j >= off) & (j < off + cnt))
            def _(p_ref=p_ref):
                o_ref[...] = _bdot(h_ref[...], p_ref[...], _TN).astype(BF)

    return pl.pallas_call(
        body, name="g_win", grid=(sum(counts),),
        in_specs=[pl.BlockSpec((T, D), lambda j: (0, 0))] +
                 [pl.BlockSpec((T, DZ_TILE), _part_spec(None, c, o, 0)) for c, o in zip(counts, offs)],
        out_specs=pl.BlockSpec((D, DZ_TILE), lambda j: (0, j)),
        out_shape=jax.ShapeDtypeStruct((D, NIN), BF),
        compiler_params=_params(("parallel",)),
    )(h, *parts)


def _grad_h(parts, w_in):
    counts, offs = _part_offsets(parts)
    n = len(parts)
    tm = 1024
    nk = sum(counts)

    def body(*refs):
        w_ref, o_ref, acc = refs[n], refs[n + 1], refs[n + 2]
        k = pl.program_id(1)

        @pl.when(k == 0)
        def _():
            acc[...] = jnp.zeros_like(acc)

        for p_ref, cnt, off in zip(refs[:n], counts, offs):
            @pl.when((k >= off) & (k < off + cnt))
            def _(p_ref=p_ref):
                acc[...] += _bdot(p_ref[...], w_ref[...], _NT)

        @pl.when(k == nk - 1)
        def _():
            o_ref[...] = acc[...]

    return pl.pallas_call(
        body, name="d_h", grid=(T // tm, nk),
        in_specs=[pl.BlockSpec((tm, DZ_TILE), _part_spec(0, c, o, 1)) for c, o in zip(counts, offs)] +
                 [pl.BlockSpec((D, DZ_TILE), lambda i, k: (0, k))],
        out_specs=pl.BlockSpec((tm, D), lambda i, k: (i, 0)),
        out_shape=jax.ShapeDtypeStruct((T, D), F32),
        scratch_shapes=[pltpu.VMEM((tm, D), F32)],
        compiler_params=_params(("parallel", "arbitrary")),
    )(*parts, w_in)


def _rmsnorm_fwd(x, w):
    tm = 256

    def body(x_ref, w_ref, h_ref):
        xv = x_ref[...]
        r = lax.rsqrt(jnp.mean(xv * xv, axis=-1, keepdims=True) + EPS)
        h_ref[...] = (xv * r * w_ref[...]).astype(BF)

    return pl.pallas_call(
        body, name="rmsnorm_fwd", grid=(T // tm,),
        in_specs=[pl.BlockSpec((tm, D), lambda i: (i, 0)), pl.BlockSpec((1, D), lambda i: (0, 0))],
        out_specs=pl.BlockSpec((tm, D), lambda i: (i, 0)),
        out_shape=jax.ShapeDtypeStruct((T, D), BF),
        compiler_params=_params(("parallel",)),
    )(x, w)


def _rmsnorm_bwd(x, dh, dout, w):
    tm = 256

    def body(x_ref, dh_ref, dout_ref, w_ref, gx_ref, gw_ref):
        @pl.when(pl.program_id(0) == 0)
        def _():
            gw_ref[...] = jnp.zeros_like(gw_ref)

        xv, dhv = x_ref[...], dh_ref[...]
        r = lax.rsqrt(jnp.mean(xv * xv, axis=-1, keepdims=True) + EPS)
        nrm = xv * r
        dn = dhv * w_ref[...]
        gw_ref[...] += jnp.sum(dhv * nrm, axis=0, keepdims=True)
        gx_ref[...] = dout_ref[...] + r * (dn - nrm * jnp.mean(dn * nrm, axis=-1, keepdims=True))

    row = pl.BlockSpec((tm, D), lambda i: (i, 0))
    vec = pl.BlockSpec((1, D), lambda i: (0, 0))
    return pl.pallas_call(
        body, name="rmsnorm_bwd", grid=(T // tm,),
        in_specs=[row, row, row, vec], out_specs=[row, vec],
        out_shape=[jax.ShapeDtypeStruct((T, D), F32), jax.ShapeDtypeStruct((1, D), F32)],
        compiler_params=_params(("arbitrary",)),
    )(x, dh, dout, w)


def _lower_bound(lbl):
    mx = jnp.max(lbl, axis=0, keepdims=True)
    e = jnp.exp(lbl - mx)
    return e[0:1] / jnp.sum(e, axis=0, keepdims=True)


def _cumsum_rows(g, rows):
    b = g
    sh = 1
    while sh < CH:
        b = b + jnp.where(rows >= sh, pltpu.roll(b, sh, axis=0), 0.0)
        sh *= 2
    return b


def _rev_cumsum_rows(g, rows):
    b = g
    sh = 1
    while sh < CH:
        b = b + jnp.where(rows < CH - sh, pltpu.roll(b, CH - sh, axis=0), 0.0)
        sh *= 2
    return b


SUB = CH // 2


def _direct_block(qb, kb, vb, bb, rows8):
    ob = jnp.zeros_like(qb)
    for s in range(SUB):
        e_s = jnp.exp(jnp.where(rows8 >= s, bb - bb[s:s + 1], -jnp.inf))
        ob = ob + jnp.sum(qb * e_s * kb[s:s + 1], axis=1, keepdims=True) * vb[s:s + 1]
    return ob


def _direct_block_bwd(qb, kb, vb, bb, dob, rows8, rowc8):
    dq = dk = dv = db = jnp.zeros_like(qb)
    for s in range(SUB):
        one = (rowc8 == s).astype(F32)
        ks, vs = kb[s:s + 1], vb[s:s + 1]
        e_s = jnp.exp(jnp.where(rows8 >= s, bb - bb[s:s + 1], -jnp.inf))
        qes = qb * e_s
        w = qes * ks
        a = jnp.sum(w, axis=1, keepdims=True)
        da = jnp.sum(dob * vs, axis=1, keepdims=True)
        dv = dv + one * jnp.sum(a * dob, axis=0, keepdims=True)
        dq = dq + da * e_s * ks
        dk = dk + one * jnp.sum(da * qes, axis=0, keepdims=True)
        u = da * w
        db = db + u - one * jnp.sum(u, axis=0, keepdims=True)
    return dq, dk, dv, db


def _cross_factors(q, k, b):
    ref = b[SUB - 1:SUB]
    e_hi, e_lo = jnp.exp(b[SUB:] - ref), jnp.exp(ref - b[:SUB])
    return q[SUB:] * e_hi, k[:SUB] * e_lo, e_hi, e_lo


def _intra_fwd(q, k, v, b, rows8):
    lo = _direct_block(q[:SUB], k[:SUB], v[:SUB], b[:SUB], rows8)
    hi = _direct_block(q[SUB:], k[SUB:], v[SUB:], b[SUB:], rows8)
    qe_hi, ke_lo, _, _ = _cross_factors(q, k, b)
    for s in range(SUB):
        hi = hi + jnp.sum(qe_hi * ke_lo[s:s + 1], axis=1, keepdims=True) * v[s:s + 1]
    return jnp.concatenate([lo, hi], axis=0)


def _intra_bwd(q, k, v, b, do, rows8, rowc8):
    dq_lo, dk_lo, dv_lo, db_lo = _direct_block_bwd(q[:SUB], k[:SUB], v[:SUB], b[:SUB], do[:SUB], rows8, rowc8)
    dq_hi, dk_hi, dv_hi, db_hi = _direct_block_bwd(q[SUB:], k[SUB:], v[SUB:], b[SUB:], do[SUB:], rows8, rowc8)
    qe_hi, ke_lo, e_hi, e_lo = _cross_factors(q, k, b)
    do_hi, v_lo = do[SUB:], v[:SUB]
    dqe = dke = jnp.zeros_like(qe_hi)
    for s in range(SUB):
        one = (rowc8 == s).astype(F32)
        a = jnp.sum(qe_hi * ke_lo[s:s + 1], axis=1, keepdims=True)
        da = jnp.sum(do_hi * v_lo[s:s + 1], axis=1, keepdims=True)
        dv_lo = dv_lo + one * jnp.sum(a * do_hi, axis=0, keepdims=True)
        dqe = dqe + da * ke_lo[s:s + 1]
        dke = dke + one * jnp.sum(da * qe_hi, axis=0, keepdims=True)
    u_hi, u_lo = dqe * qe_hi, dke * ke_lo
    d_ref = jnp.sum(u_lo, axis=0, keepdims=True) - jnp.sum(u_hi, axis=0, keepdims=True)
    db_lo = db_lo - u_lo + (rowc8 == SUB - 1).astype(F32) * d_ref
    cat = lambda lo, hi: jnp.concatenate([lo, hi], axis=0)
    return (cat(dq_lo, dq_hi + dqe * e_hi), cat(dk_lo + dke * e_lo, dk_hi), cat(dv_lo, dv_hi),
            cat(db_lo, db_hi + u_hi))


def _hgrn_fwd(z, lbl, nw):
    def body(hq_ref, hf_ref, hi_ref, hg_ref, lbl_ref, nw_ref, oraw_ref, og_ref, sh_ref, st_ref):
        @pl.when(pl.program_id(0) == 0)
        def _():
            st_ref[...] = jnp.zeros_like(st_ref)

        lb_all = _lower_bound(lbl_ref[...])
        rows = lax.broadcasted_iota(jnp.int32, (CH, HK), 0)
        rows8 = lax.broadcasted_iota(jnp.int32, (SUB, HK), 0)
        nwv = nw_ref[...]
        for cc, h in [(cc, h) for cc in range(HSTEP) for h in range(HEADS)]:
            rs = slice(CH * cc, CH * (cc + 1))
            sl = slice(HK * h, HK * (h + 1))
            lb = lb_all[:, sl]
            hq, hf, v, hg = hq_ref[rs, sl], hf_ref[rs, sl], hi_ref[rs, sl], hg_ref[rs, sl]
            q = hq * _sigmoid(hq)
            f = lb + (1.0 - lb) * _sigmoid(hf)
            k = 1.0 - f
            b = _cumsum_rows(jnp.log(f), rows)
            sh_ref[cc, h] = st_ref[h]
            o = _bdot(q * jnp.exp(b), st_ref[h], _NT) + _intra_fwd(q, k, v, b, rows8)
            bl = b[CH - 1:CH]
            st_ref[h] = st_ref[h] * jnp.exp(bl)
            st_ref[h] += _bdot(v, k * jnp.exp(bl - b), _TN)
            oraw_ref[rs, sl] = o
            nrm = o * lax.rsqrt(jnp.mean(o * o, axis=1, keepdims=True) + EPS)
            og_ref[rs, sl] = (nrm * nwv * (hg * _sigmoid(hg))).astype(BF)

    zblk = lambda c: pl.BlockSpec((CH * HSTEP, D), lambda i, c=c: (i, c))
    return pl.pallas_call(
        body, name="hgrn_fwd", grid=(NCH // HSTEP,),
        in_specs=[zblk(0), zblk(1), zblk(2), zblk(3),
                  pl.BlockSpec((2, D), lambda i: (0, 0)), pl.BlockSpec((1, HK), lambda i: (0, 0))],
        out_specs=[zblk(0), zblk(0),
                   pl.BlockSpec((HSTEP, HEADS, HK, HK), lambda i: (i, 0, 0, 0))],
        out_shape=[jax.ShapeDtypeStruct((T, D), F32), jax.ShapeDtypeStruct((T, D), BF),
                   jax.ShapeDtypeStruct((NCH, HEADS, HK, HK), F32)],
        scratch_shapes=[pltpu.VMEM((HEADS, HK, HK), F32)],
        compiler_params=_params(("arbitrary",)),
    )(z, z, z, z, lbl, nw)


def _hgrn_bwd(z, lbl, nw, oraw, dog, shist):
    hstep = 1

    def body(hq_ref, hf_ref, hi_ref, hg_ref, lbl_ref, nw_ref, oraw_ref, dog_ref, sh_ref,
             dz_ref, dlb_ref, dnw_ref, dst_ref):
        @pl.when(pl.program_id(0) == 0)
        def _():
            dst_ref[...] = jnp.zeros_like(dst_ref)
            dlb_ref[...] = jnp.zeros_like(dlb_ref)
            dnw_ref[...] = jnp.zeros_like(dnw_ref)

        lb_all = _lower_bound(lbl_ref[...])
        rows = lax.broadcasted_iota(jnp.int32, (CH, HK), 0)
        rowc = lax.broadcasted_iota(jnp.int32, (CH, 1), 0)
        rows8 = lax.broadcasted_iota(jnp.int32, (SUB, HK), 0)
        rowc8 = lax.broadcasted_iota(jnp.int32, (SUB, 1), 0)
        nwv = nw_ref[...]
        dnw = jnp.zeros((1, HK), F32)
        for cc, h in [(cc, h) for cc in reversed(range(hstep)) for h in range(HEADS)]:
            rs = slice(CH * cc, CH * (cc + 1))
            sl = slice(HK * h, HK * (h + 1))
            lb = lb_all[:, sl]
            hq, hf, v, hg = hq_ref[rs, sl], hf_ref[rs, sl], hi_ref[rs, sl], hg_ref[rs, sl]
            o, dg_out = oraw_ref[rs, sl], dog_ref[rs, sl]
            sg = _sigmoid(hg)
            sil = hg * sg
            r = lax.rsqrt(jnp.mean(o * o, axis=1, keepdims=True) + EPS)
            nrm = o * r
            d_hg = dg_out * (nrm * nwv) * (sg * (1.0 + hg * (1.0 - sg)))
            dn = dg_out * nwv * sil
            dnw = dnw + jnp.sum(dg_out * nrm * sil, axis=0, keepdims=True)
            do = r * (dn - nrm * jnp.mean(dn * nrm, axis=1, keepdims=True))
            sq = _sigmoid(hq)
            q = hq * sq
            sig = _sigmoid(hf)
            f = lb + (1.0 - lb) * sig
            k = 1.0 - f
            b = _cumsum_rows(jnp.log(f), rows)
            eb = jnp.exp(b)
            qe = q * eb
            bl = b[CH - 1:CH]
            ebl = jnp.exp(bl)
            kdec = jnp.exp(bl - b)
            ke = k * kdec
            dqe = _bdot(do, sh_ref[cc, h])
            dq = dqe * eb
            db = dqe * qe
            dke = _bdot(v, dst_ref[h])
            dv = _bdot(ke, dst_ref[h], _NT)
            dk = dke * kdec
            rr = dke * ke
            db = db - rr
            db_last = (jnp.sum(rr, axis=0, keepdims=True)
                       + ebl * jnp.sum(dst_ref[h] * sh_ref[cc, h], axis=0, keepdims=True))
            dst_ref[h] = dst_ref[h] * ebl
            dst_ref[h] += _bdot(do, qe, _TN)
            dq_i, dk_i, dv_i, db_i = _intra_bwd(q, k, v, b, do, rows8, rowc8)
            dq, dk, dv = dq + dq_i, dk + dk_i, dv + dv_i
            db = db + db_i + (rowc == CH - 1).astype(F32) * db_last
            dgl = _rev_cumsum_rows(db, rows)
            df = dgl / f - dk
            dlb_ref[:, sl] += jnp.sum(df * (1.0 - sig), axis=0, keepdims=True)
            dz_ref[rs, sl] = (dq * (sq * (1.0 + hq * (1.0 - sq)))).astype(BF)
            dz_ref[rs, D + HK * h:D + HK * (h + 1)] = (df * (1.0 - lb) * sig * (1.0 - sig)).astype(BF)
            dz_ref[rs, 2 * D + HK * h:2 * D + HK * (h + 1)] = dv.astype(BF)
            dz_ref[rs, 3 * D + HK * h:3 * D + HK * (h + 1)] = d_hg.astype(BF)
        dnw_ref[...] += dnw

    rev = lambda i: NCH // hstep - 1 - i
    zblk = lambda c: pl.BlockSpec((CH * hstep, D), lambda i, c=c: (rev(i), c))
    return pl.pallas_call(
        body, name="hgrn_bwd", grid=(NCH // hstep,),
        in_specs=[zblk(0), zblk(1), zblk(2), zblk(3),
                  pl.BlockSpec((2, D), lambda i: (0, 0)), pl.BlockSpec((1, HK), lambda i: (0, 0)),
                  zblk(0), zblk(0),
                  pl.BlockSpec((hstep, HEADS, HK, HK), lambda i: (rev(i), 0, 0, 0))],
        out_specs=[pl.BlockSpec((CH * hstep, 4 * D), lambda i: (rev(i), 0)),
                   pl.BlockSpec((1, D), lambda i: (0, 0)), pl.BlockSpec((1, HK), lambda i: (0, 0))],
        out_shape=[jax.ShapeDtypeStruct((T, 4 * D), BF), jax.ShapeDtypeStruct((1, D), F32),
                   jax.ShapeDtypeStruct((1, HK), F32)],
        scratch_shapes=[pltpu.VMEM((HEADS, HK, HK), F32)],
        compiler_params=_params(("arbitrary",)),
    )(z, z, z, z, lbl, nw, oraw, dog, shist)


BLK = 128
NBLK = T // BLK
QK_SCALE = 0.125


def _head_masks():
    lane = lax.broadcasted_iota(jnp.int32, (1, BLK), 1)
    return [(lane < 64).astype(F32), (lane >= 64).astype(F32)]


def _pieces(dil):
    m = T // dil
    out = []
    for r in range(dil):
        for j in range(m // BLK):
            start = r + dil * BLK * j
            rows = pl.ds(start, BLK, stride=dil) if dil > 1 else pl.ds(start, BLK)
            out.append((rows, r * m + BLK * j))
    return out


def _rope_tables(pos_ref, invf_ref, cos_ref, sa_ref, sb_ref):
    lane = lax.broadcasted_iota(jnp.int32, (256, BLK), 1)
    first = (lane % 64) < 32
    for c in range(T // 256):
        rows = pl.ds(256 * c, 256)
        ang = pos_ref[rows, :].astype(F32) * invf_ref[...]
        s = jnp.sin(ang)
        cos_ref[rows, :] = jnp.cos(ang)
        sa_ref[rows, :] = jnp.where(first, -s, 0.0)
        sb_ref[rows, :] = jnp.where(first, 0.0, s)


def _rope(x, c, sa, sb):
    return x * c + pltpu.roll(x, 96, axis=1) * sa + pltpu.roll(x, 32, axis=1) * sb


def _rope_t(d, c, sa, sb):
    return d * c + pltpu.roll(d * sa, 32, axis=1) + pltpu.roll(d * sb, 96, axis=1)


def _window_bias(bias_ref):
    ii = lax.broadcasted_iota(jnp.int32, (2 * BLK, BLK), 0) % BLK
    jj = lax.broadcasted_iota(jnp.int32, (2 * BLK, BLK), 1)
    bias_ref[0] = jnp.where(jj <= ii, 0.0, -jnp.inf)
    bias_ref[1] = jnp.where(jj >= ii, 0.0, -jnp.inf)


def _blocks(bi):
    if isinstance(bi, int):
        return pl.ds(bi * BLK, BLK), pl.ds(max(bi - 1, 0) * BLK, BLK)
    return (pl.ds(pl.multiple_of(bi * BLK, BLK), BLK),
            pl.ds(pl.multiple_of(jnp.maximum(bi - 1, 0) * BLK, BLK), BLK))


def _stack_heads(x, masks):
    return jnp.concatenate([x * masks[0], x * masks[1]], axis=0).astype(BF)


def _attn_fwd(z, pos, invf):
    def body(q_ref, k_ref, v_ref, ag_ref, pos_ref, invf_ref, ob_ref, opre_ref, lse_ref,
             cos_ref, sa_ref, sb_ref, bias_ref, qr_ref, kr_ref, vr_ref, og_ref, lg_ref, otok_ref, ltok_ref, sc_ref):
        g = pl.program_id(1)
        masks = _head_masks()

        @pl.when(g == 0)
        def _():
            _rope_tables(pos_ref, invf_ref, cos_ref, sa_ref, sb_ref)
            _window_bias(bias_ref)

        def group(gi):
            dil = ATT_GROUPS[gi][1]
            nblk = (T // dil) // BLK
            for rows, dst in _pieces(dil):
                c, sa, sb = cos_ref[rows, :], sa_ref[rows, :], sb_ref[rows, :]
                qr_ref[pl.ds(dst, BLK), :] = _rope(q_ref[rows, :], c, sa, sb) * QK_SCALE
                kr_ref[pl.ds(dst, BLK), :] = _rope(k_ref[rows, :], c, sa, sb)
                vr_ref[pl.ds(dst, BLK), :] = v_ref[rows, :]

            def scores(bi, slot):
                cur, prev = _blocks(bi)
                q2 = _stack_heads(qr_ref[cur, :], masks)
                sc_ref[slot, 0] = _dot(q2, kr_ref[cur, :].astype(BF), _NT) + bias_ref[0]
                if nblk > 1:
                    sc_ref[slot, 1] = (_dot(q2, kr_ref[prev, :].astype(BF), _NT)
                                       + (bias_ref[1] + jnp.where((bi % nblk) != 0, 0.0, -jnp.inf)))

            def finish(bi, slot):
                cur, prev = _blocks(bi)
                s_c, vc = sc_ref[slot, 0], vr_ref[cur, :].astype(BF)
                if nblk > 1:
                    s_p, vp = sc_ref[slot, 1], vr_ref[prev, :].astype(BF)
                    mx = jnp.max(jnp.maximum(s_c, s_p), axis=1, keepdims=True)
                    p_c, p_p = jnp.exp(s_c - mx), jnp.exp(s_p - mx)
                    den = jnp.sum(p_c + p_p, axis=1, keepdims=True)
                    oh = _dot(p_c.astype(BF), vc) + _dot(p_p.astype(BF), vp)
                else:
                    mx = jnp.max(s_c, axis=1, keepdims=True)
                    p_c = jnp.exp(s_c - mx)
                    den = jnp.sum(p_c, axis=1, keepdims=True)
                    oh = _dot(p_c.astype(BF), vc)
                on = oh / den
                lsev = jnp.broadcast_to(mx + jnp.log(den), (2 * BLK, BLK))
                og_ref[cur, :] = on[:BLK] * masks[0] + on[BLK:] * masks[1]
                lg_ref[0, cur, :] = lsev[:BLK]
                lg_ref[1, cur, :] = lsev[BLK:]

            def pair(j, carry):
                finish(2 * j, 0)
                scores(2 * j + 1, 1)
                finish(2 * j + 1, 1)
                scores(jnp.minimum(2 * j + 2, NBLK - 1), 0)
                return carry

            scores(0, 0)
            lax.fori_loop(0, NBLK // 2, pair, 0)
            for rows, src in _pieces(dil):
                srows = pl.ds(src, BLK)
                otok_ref[gi, rows, :] = og_ref[srows, :]
                ltok_ref[gi, 0, rows, :] = lg_ref[0, srows, :]
                ltok_ref[gi, 1, rows, :] = lg_ref[1, srows, :]

        for gi in range(3):
            pl.when(g == gi)(functools.partial(group, gi))

        @pl.when(g == 2)
        def _():
            for c in range(T // BLK):
                rows = pl.ds(BLK * c, BLK)
                wts = []
                for hh in range(2):
                    l0, l1, l2 = ltok_ref[0, hh, rows, :], ltok_ref[1, hh, rows, :], ltok_ref[2, hh, rows, :]
                    mx = jnp.maximum(jnp.maximum(l0, l1), l2)
                    lse = mx + jnp.log(jnp.exp(l0 - mx) + jnp.exp(l1 - mx) + jnp.exp(l2 - mx))
                    lse_ref[rows, BLK * hh:BLK * (hh + 1)] = lse
                    wts.append([jnp.exp(l0 - lse), jnp.exp(l1 - lse), jnp.exp(l2 - lse)])
                o = sum((wts[0][gi] * masks[0] + wts[1][gi] * masks[1]) * otok_ref[gi, rows, :] for gi in range(3))
                ag = ag_ref[rows, :]
                opre_ref[rows, :] = o
                ob_ref[rows, :] = (o * (ag * _sigmoid(ag))).astype(BF)

    c0 = ATT_COL0 // BLK
    zspec = lambda part: pl.BlockSpec((T, BLK), lambda p, g, part=part: (0, c0 + 12 * part + 4 * g + p))
    outspec = pl.BlockSpec((T, BLK), lambda p, g: (0, p))
    big = lambda: pltpu.VMEM((T, BLK), F32)
    return pl.pallas_call(
        body, name="attn_fwd", grid=(4, 3),
        in_specs=[zspec(0), zspec(1), zspec(2),
                  pl.BlockSpec((T, BLK), lambda p, g: (0, AG_COL0 // BLK + p)),
                  pl.BlockSpec((T, 1), lambda p, g: (0, 0)), pl.BlockSpec((1, BLK), lambda p, g: (0, 0))],
        out_specs=[outspec, outspec, pl.BlockSpec((T, 2 * BLK), lambda p, g: (0, p))],
        out_shape=[jax.ShapeDtypeStruct((T, 512), BF), jax.ShapeDtypeStruct((T, 512), F32),
                   jax.ShapeDtypeStruct((T, 8 * BLK), F32)],
        scratch_shapes=[big(), big(), big(), pltpu.VMEM((2, 2 * BLK, BLK), F32), big(), big(), big(), big(),
                        pltpu.VMEM((2, T, BLK), F32), pltpu.VMEM((3, T, BLK), F32), pltpu.VMEM((3, 2, T, BLK), F32),
                        pltpu.VMEM((2, 2, 2 * BLK, BLK), F32)],
        compiler_params=_params(("parallel", "arbitrary")),
    )(z, z, z, z, pos, invf)


def _attn_bwd(z, pos, invf, opre, lse, dob):
    def body(q_ref, k_ref, v_ref, ag_ref, pos_ref, invf_ref, o_ref, lse0_ref, lse1_ref, dob_ref,
             dq_ref, dk_ref, dv_ref, dag_ref,
             cos_ref, sa_ref, sb_ref, bias_ref, dtok_ref, qr_ref, kr_ref, vr_ref, dor_ref, lr_ref, dr_ref,
             dqr_ref, dkr_ref, dvr_ref, pd_ref):
        g = pl.program_id(1)
        masks = _head_masks()

        @pl.when(g == 0)
        def _():
            _rope_tables(pos_ref, invf_ref, cos_ref, sa_ref, sb_ref)
            _window_bias(bias_ref)
            for c in range(T // BLK):
                rows = pl.ds(BLK * c, BLK)
                ag, dob_v, o = ag_ref[rows, :], dob_ref[rows, :], o_ref[rows, :]
                sg = _sigmoid(ag)
                dag_ref[rows, :] = (dob_v * o * (sg * (1.0 + ag * (1.0 - sg)))).astype(BF)
                prod = dob_v * (ag * sg) * o
                for hh, mh in enumerate(masks):
                    dtok_ref[hh, rows, :] = jnp.broadcast_to(jnp.sum(prod * mh, axis=1, keepdims=True), (BLK, BLK))

        def group(gi):
            dil = ATT_GROUPS[gi][1]
            nblk = (T // dil) // BLK
            for rows, dst in _pieces(dil):
                drows = pl.ds(dst, BLK)
                c, sa, sb = cos_ref[rows, :], sa_ref[rows, :], sb_ref[rows, :]
                qr_ref[drows, :] = _rope(q_ref[rows, :], c, sa, sb) * QK_SCALE
                kr_ref[drows, :] = _rope(k_ref[rows, :], c, sa, sb)
                vr_ref[drows, :] = v_ref[rows, :]
                ag = ag_ref[rows, :]
                dor_ref[drows, :] = dob_ref[rows, :] * (ag * _sigmoid(ag))
                for hh, lse_ref in enumerate((lse0_ref, lse1_ref)):
                    lr_ref[hh, drows, :] = lse_ref[rows, :]
                    dr_ref[hh, drows, :] = dtok_ref[hh, rows, :]
            dkr_ref[...] = jnp.zeros_like(dkr_ref)
            dvr_ref[...] = jnp.zeros_like(dvr_ref)

            def probs(bi, slot):
                cur, prev = _blocks(bi)
                q2, do2 = _stack_heads(qr_ref[cur, :], masks), _stack_heads(dor_ref[cur, :], masks)
                lh = jnp.concatenate([lr_ref[0, cur, :], lr_ref[1, cur, :]], axis=0)
                dh = jnp.concatenate([dr_ref[0, cur, :], dr_ref[1, cur, :]], axis=0)
                p_c = jnp.exp(_dot(q2, kr_ref[cur, :].astype(BF), _NT) + bias_ref[0] - lh)
                pd_ref[slot, 0] = p_c.astype(BF)
                pd_ref[slot, 1] = (p_c * (_dot(do2, vr_ref[cur, :].astype(BF), _NT) - dh)).astype(BF)
                if nblk > 1:
                    bias_p = bias_ref[1] + jnp.where((bi % nblk) != 0, 0.0, -jnp.inf)
                    p_p = jnp.exp(_dot(q2, kr_ref[prev, :].astype(BF), _NT) + bias_p - lh)
                    pd_ref[slot, 2] = p_p.astype(BF)
                    pd_ref[slot, 3] = (p_p * (_dot(do2, vr_ref[prev, :].astype(BF), _NT) - dh)).astype(BF)

            def grads(bi, slot):
                cur, prev = _blocks(bi)
                q2, do2 = _stack_heads(qr_ref[cur, :], masks), _stack_heads(dor_ref[cur, :], masks)
                p_c, ds_c = pd_ref[slot, 0], pd_ref[slot, 1]
                dq2 = _dot(ds_c, kr_ref[cur, :].astype(BF))
                dkr_ref[cur, :] += _dot(ds_c, q2, _TN)
                dvr_ref[cur, :] += _dot(p_c, do2, _TN)
                if nblk > 1:
                    p_p, ds_p = pd_ref[slot, 2], pd_ref[slot, 3]
                    dq2 = dq2 + _dot(ds_p, kr_ref[prev, :].astype(BF))
                    dkr_ref[prev, :] += _dot(ds_p, q2, _TN)
                    dvr_ref[prev, :] += _dot(p_p, do2, _TN)
                dqr_ref[cur, :] = dq2[:BLK] * masks[0] + dq2[BLK:] * masks[1]

            def pair(j, carry):
                grads(2 * j, 0)
                probs(2 * j + 1, 1)
                grads(2 * j + 1, 1)
                probs(jnp.minimum(2 * j + 2, NBLK - 1), 0)
                return carry

            probs(0, 0)
            lax.fori_loop(0, NBLK // 2, pair, 0)
            for rows, src in _pieces(dil):
                srows = pl.ds(src, BLK)
                c, sa, sb = cos_ref[rows, :], sa_ref[rows, :], sb_ref[rows, :]
                qr_ref[rows, :] = _rope_t(dqr_ref[srows, :] * QK_SCALE, c, sa, sb)
                kr_ref[rows, :] = _rope_t(dkr_ref[srows, :], c, sa, sb)
                vr_ref[rows, :] = dvr_ref[srows, :]
            for c in range(T // 256):
                rows = pl.ds(256 * c, 256)
                dq_ref[rows, :] = qr_ref[rows, :].astype(BF)
                dk_ref[rows, :] = kr_ref[rows, :].astype(BF)
                dv_ref[rows, :] = vr_ref[rows, :].astype(BF)

        for gi in range(3):
            pl.when(g == gi)(functools.partial(group, gi))

    c0 = ATT_COL0 // BLK
    zspec = lambda part: pl.BlockSpec((T, BLK), lambda p, g, part=part: (0, c0 + 12 * part + 4 * g + p))
    pspec = pl.BlockSpec((T, BLK), lambda p, g: (0, p))
    gspec = pl.BlockSpec((T, BLK), lambda p, g: (0, 4 * g + p))
    big = lambda: pltpu.VMEM((T, BLK), F32)
    two = lambda: pltpu.VMEM((2, T, BLK), F32)
    return pl.pallas_call(
        body, name="attn_bwd", grid=(4, 3),
        in_specs=[zspec(0), zspec(1), zspec(2),
                  pl.BlockSpec((T, BLK), lambda p, g: (0, AG_COL0 // BLK + p)),
                  pl.BlockSpec((T, 1), lambda p, g: (0, 0)), pl.BlockSpec((1, BLK), lambda p, g: (0, 0)),
                  pspec, pl.BlockSpec((T, BLK), lambda p, g: (0, 2 * p)),
                  pl.BlockSpec((T, BLK), lambda p, g: (0, 2 * p + 1)), pspec],
        out_specs=[gspec, gspec, gspec, pspec],
        out_shape=[jax.ShapeDtypeStruct((T, 1536), BF), jax.ShapeDtypeStruct((T, 1536), BF),
                   jax.ShapeDtypeStruct((T, 1536), BF), jax.ShapeDtypeStruct((T, 512), BF)],
        scratch_shapes=[big(), big(), big(), pltpu.VMEM((2, 2 * BLK, BLK), F32), two(), big(), big(), big(), big(),
                        two(), two(), big(), big(), big(), pltpu.VMEM((2, 4, 2 * BLK, BLK), BF)],
        compiler_params=_params(("parallel", "arbitrary")),
    )(z, z, z, z, pos, invf, opre, lse, lse, dob)


def _merge_fwd(ya, yb, z):
    tm = 256

    def body(ya_ref, yb_ref, ga_ref, gb_ref, m_ref):
        m_ref[...] = (_sigmoid(ga_ref[...]) * ya_ref[...] + _sigmoid(gb_ref[...]) * yb_ref[...]).astype(BF)

    row = pl.BlockSpec((tm, D), lambda i: (i, 0))
    return pl.pallas_call(
        body, name="merge_fwd", grid=(T // tm,),
        in_specs=[row, row, pl.BlockSpec((tm, D), lambda i: (i, GATE_COL0 // D)),
                  pl.BlockSpec((tm, D), lambda i: (i, GATE_COL0 // D + 1))],
        out_specs=row, out_shape=jax.ShapeDtypeStruct((T, D), BF),
        compiler_params=_params(("parallel",)),
    )(ya, yb, z, z)


def _out_loss(merged, w_out, x, tgt, wf):
    tm = 256

    def body(m_ref, w_ref, x_ref, t_ref, wf_ref, dout_ref, loss_ref, gwf_ref):
        @pl.when(pl.program_id(0) == 0)
        def _():
            loss_ref[...] = jnp.zeros_like(loss_ref)
            gwf_ref[...] = jnp.zeros_like(gwf_ref)

        out = x_ref[...] + _dot(m_ref[...], w_ref[...])
        r = lax.rsqrt(jnp.mean(out * out, axis=-1, keepdims=True) + EPS)
        yh = out * r
        wfv = wf_ref[...]
        err = yh * wfv - t_ref[...]
        loss_ref[...] += jnp.sum(err * err, axis=0, keepdims=True) * (0.5 / D)
        dy = err * (1.0 / D)
        gwf_ref[...] += jnp.sum(dy * yh, axis=0, keepdims=True)
        dyh = dy * wfv
        dout_ref[...] = r * (dyh - yh * jnp.mean(dyh * yh, axis=-1, keepdims=True))

    row = pl.BlockSpec((tm, D), lambda i: (i, 0))
    vec = pl.BlockSpec((1, D), lambda i: (0, 0))
    return pl.pallas_call(
        body, name="out_loss", grid=(T // tm,),
        in_specs=[row, pl.BlockSpec((D, D), lambda i: (0, 0)), row, row, vec],
        out_specs=[row, vec, vec],
        out_shape=[jax.ShapeDtypeStruct((T, D), F32), jax.ShapeDtypeStruct((1, D), F32),
                   jax.ShapeDtypeStruct((1, D), F32)],
        compiler_params=_params(("arbitrary",)),
    )(merged, w_out, x, tgt, wf)


def _merge_bwd(dm, ya, yb, z):
    tm = 256

    def body(dm_ref, ya_ref, yb_ref, ga_ref, gb_ref, dya_ref, dyb_ref, dg_ref):
        dmv = dm_ref[...]
        sa, sb = _sigmoid(ga_ref[...]), _sigmoid(gb_ref[...])
        dya_ref[...] = (sa * dmv).astype(BF)
        dyb_ref[...] = (sb * dmv).astype(BF)
        dg_ref[:, :D] = (dmv * ya_ref[...] * sa * (1.0 - sa)).astype(BF)
        dg_ref[:, D:] = (dmv * yb_ref[...] * sb * (1.0 - sb)).astype(BF)

    row = pl.BlockSpec((tm, D), lambda i: (i, 0))
    return pl.pallas_call(
        body, name="merge_bwd", grid=(T // tm,),
        in_specs=[row, row, row, pl.BlockSpec((tm, D), lambda i: (i, GATE_COL0 // D)),
                  pl.BlockSpec((tm, D), lambda i: (i, GATE_COL0 // D + 1))],
        out_specs=[row, row, pl.BlockSpec((tm, 2 * D), lambda i: (i, 0))],
        out_shape=[jax.ShapeDtypeStruct((T, D), BF), jax.ShapeDtypeStruct((T, D), BF),
                   jax.ShapeDtypeStruct((T, 2 * D), BF)],
        compiler_params=_params(("parallel",)),
    )(dm, ya, yb, z, z)


def _rope_inv_freq():
    inv = ROPE_THETA ** (-jnp.arange(0, 64, 2, dtype=F32) / 64)
    return jnp.tile(inv, 4).reshape(1, BLK)


def _local_step(x, pos, norm_w, lbl, hnw, wf, tgt, w_in, w_a, w_b, w_out):
    invf = _rope_inv_freq()
    h = _rmsnorm_fwd(x, norm_w)
    z = _matmul(h, w_in, tm=1024, tn=512, name="z_proj")
    oraw, og, shist = _hgrn_fwd(z, lbl, hnw)
    ob, opre, lse = _attn_fwd(z, pos, invf)
    ya = _matmul(og, w_a, tm=1024, tn=512, name="ya_proj")
    yb = _matmul(ob, w_b, tm=1024, tn=512, name="yb_proj")
    merged = _merge_fwd(ya, yb, z)
    dout, loss_vec, g_wf = _out_loss(merged, w_out, x, tgt, wf)

    dm = _matmul(dout, w_out, tb=True, tm=1024, tn=512, name="d_merged")
    g_wout = _matmul(merged, dout, ta=True, out_dtype=BF, tm=512, tn=1024, name="g_wout")
    dya, dyb, dgates = _merge_bwd(dm, ya, yb, z)
    dog = _matmul(dya, w_a, tb=True, tm=1024, tn=512, name="d_og")
    g_wa = _matmul(og, dya, ta=True, out_dtype=BF, tm=512, tn=1024, name="g_wa")
    dob = _matmul(dyb, w_b, tb=True, tm=1024, tn=512, name="d_ob")
    g_wb = _matmul(ob, dyb, ta=True, out_dtype=BF, tm=512, tn=1024, name="g_wb")
    dz_h, dlb, g_hnw = _hgrn_bwd(z, lbl, hnw, oraw, dog, shist)
    dq, dk, dv, dag = _attn_bwd(z, pos, invf, opre, lse, dob)
    dz_parts = [dz_h, dq, dk, dv, dag, dgates]
    g_win = _grad_w_in(h, dz_parts)
    dh = _grad_h(dz_parts, w_in)
    gx, g_nw = _rmsnorm_bwd(x, dh, dout, norm_w)
    return dict(loss_vec=loss_vec, gx=gx, g_nw=g_nw, dlb=dlb, g_hnw=g_hnw, g_wf=g_wf,
                g_win=g_win, g_wa=g_wa, g_wb=g_wb, g_wout=g_wout)


MESH = pl.DeviceIdType.MESH
HBM = pl.BlockSpec(memory_space=pl.ANY)
WEIGHT_AXES = (1, 0, 1, 0)


def _place():
    x, y, c = lax.axis_index("x"), lax.axis_index("y"), lax.axis_index("c")
    chips = [(1 - x, y), (x, 1 - y), (1 - x, 1 - y)]
    return x, y, c, chips


def _block_half(ref, shard_shape, axis, j, half):
    r, c = shard_shape
    hr = r // 2
    if axis == 0:
        return ref.at[pl.ds(pl.multiple_of(j * r + half * hr, 16), hr), :]
    return ref.at[pl.ds(pl.multiple_of(half * hr, 16), hr), pl.ds(pl.multiple_of(j * c, 128), c)]


def _all_gather(shards):
    n = len(shards)
    shapes = [s.shape for s in shards]

    def body(*refs):
        ins, outs = refs[:n], refs[n:2 * n]
        send1, recv1, send2, recv2, send0, recv0 = refs[2 * n:]
        x, y, c, chips = _place()
        me = 2 * x + y
        started, own = [], []
        for a in range(n):
            r, cc = shapes[a]
            ax = WEIGHT_AXES[a]
            mine = (outs[a].at[pl.ds(pl.multiple_of(me * r, 16), r), :] if ax == 0
                    else outs[a].at[:, pl.ds(pl.multiple_of(me * cc, 128), cc)])
            swap = pltpu.make_async_remote_copy(
                src_ref=ins[a], dst_ref=mine, send_sem=send0.at[a], recv_sem=recv0.at[a],
                device_id=(x, y, 1 - c), device_id_type=MESH)
            swap.start()
            own.append(swap)
            src = ins[a].at[pl.ds(pl.multiple_of(c * (r // 2), 16), r // 2), :]
            for k, (px, py) in enumerate(chips):
                cp = pltpu.make_async_remote_copy(
                    src_ref=src, dst_ref=_block_half(outs[a], shapes[a], ax, me, c),
                    send_sem=send1.at[a, k], recv_sem=recv1.at[a, k], device_id=(px, py, c), device_id_type=MESH)
                cp.start()
                started.append(cp)
        for a in range(n):
            for k, (px, py) in enumerate(chips):
                reg = _block_half(outs[a], shapes[a], WEIGHT_AXES[a], 2 * px + py, c)
                pltpu.make_async_remote_copy(
                    src_ref=reg, dst_ref=reg, send_sem=send1.at[a, k], recv_sem=recv1.at[a, k],
                    device_id=(px, py, c), device_id_type=MESH).wait_recv()
                fw = pltpu.make_async_remote_copy(
                    src_ref=reg, dst_ref=reg, send_sem=send2.at[a, k], recv_sem=recv2.at[a, k],
                    device_id=(x, y, 1 - c), device_id_type=MESH)
                fw.start()
                started.append(fw)
        for a in range(n):
            for k, (px, py) in enumerate(chips):
                reg = _block_half(outs[a], shapes[a], WEIGHT_AXES[a], 2 * px + py, 1 - c)
                pltpu.make_async_remote_copy(
                    src_ref=reg, dst_ref=reg, send_sem=send2.at[a, k], recv_sem=recv2.at[a, k],
                    device_id=(x, y, 1 - c), device_id_type=MESH).wait_recv()
        for cp in started:
            cp.wait_send()
        for cp in own:
            cp.wait()

    full = [(4 * r, c) if ax == 0 else (r, 4 * c) for (r, c), ax in zip(shapes, WEIGHT_AXES)]
    return pl.pallas_call(
        body, name="weights_all_gather",
        in_specs=[HBM] * n, out_specs=[HBM] * n,
        out_shape=[jax.ShapeDtypeStruct(f, BF) for f in full],
        scratch_shapes=[pltpu.SemaphoreType.DMA((n, 3)), pltpu.SemaphoreType.DMA((n, 3)),
                        pltpu.SemaphoreType.DMA((n, 3)), pltpu.SemaphoreType.DMA((n, 3)),
                        pltpu.SemaphoreType.DMA((n,)), pltpu.SemaphoreType.DMA((n,))],
    )(*shards)


def _as3d(g, shard_shape, axis):
    r, c = shard_shape
    return g.reshape(4, r, c) if axis == 0 else g.reshape(1, r, 4 * c)


def _half_rows(ref3, hr, half):
    return ref3.at[:, pl.ds(pl.multiple_of(half * hr, 16), hr), :]


def _rs_pair_exchange(g3s):
    n = len(g3s)

    def body(*refs):
        ins, outs = refs[:n], refs[n:2 * n]
        send, recv = refs[2 * n:]
        x, y, c, _ = _place()
        cps = []
        for a in range(n):
            hr = g3s[a].shape[1] // 2
            cp = pltpu.make_async_remote_copy(
                src_ref=_half_rows(ins[a], hr, 1 - c), dst_ref=outs[a],
                send_sem=send.at[a], recv_sem=recv.at[a], device_id=(x, y, 1 - c), device_id_type=MESH)
            cp.start()
            cps.append(cp)
        for cp in cps:
            cp.wait()

    return pl.pallas_call(
        body, name="grads_pair_exchange",
        in_specs=[HBM] * n, out_specs=[HBM] * n,
        out_shape=[jax.ShapeDtypeStruct((g.shape[0], g.shape[1] // 2, g.shape[2]), BF) for g in g3s],
        scratch_shapes=[pltpu.SemaphoreType.DMA((n,)), pltpu.SemaphoreType.DMA((n,))],
    )(*g3s)


def _pair_sum(g3, land, cidx, name):
    nb, r, w = g3.shape
    hr = r // 2
    tr = 64

    def body(c_ref, g_ref, l_ref, o_ref):
        o_ref[...] = (g_ref[...].astype(F32) + l_ref[...].astype(F32)).astype(BF)

    blk = (nb, tr, w)
    return pl.pallas_call(
        body, name=name,
        grid_spec=pltpu.PrefetchScalarGridSpec(
            num_scalar_prefetch=1, grid=(hr // tr,),
            in_specs=[pl.BlockSpec(blk, lambda i, c: (0, c[0] * (hr // tr) + i, 0)),
                      pl.BlockSpec(blk, lambda i, c: (0, i, 0))],
            out_specs=pl.BlockSpec(blk, lambda i, c: (0, i, 0))),
        out_shape=jax.ShapeDtypeStruct((nb, hr, w), BF),
        compiler_params=_params(("parallel",)),
    )(cidx, g3, land)


def _rs_chip_exchange(p3s, shapes):
    n = len(p3s)

    def body(*refs):
        ins, outs = refs[:n], refs[n:2 * n]
        send, recv = refs[2 * n:]
        x, y, c, chips = _place()
        cps = []
        for a in range(n):
            r, cc = shapes[a]
            for k, (px, py) in enumerate(chips):
                j = 2 * px + py
                src = ins[a].at[j] if WEIGHT_AXES[a] == 0 else ins[a].at[0, :, pl.ds(pl.multiple_of(j * cc, 128), cc)]
                cp = pltpu.make_async_remote_copy(
                    src_ref=src, dst_ref=outs[a].at[k], send_sem=send.at[a, k], recv_sem=recv.at[a, k],
                    device_id=(px, py, c), device_id_type=MESH)
                cp.start()
                cps.append(cp)
        for cp in cps:
            cp.wait()

    return pl.pallas_call(
        body, name="grads_chip_exchange",
        in_specs=[HBM] * n, out_specs=[HBM] * n,
        out_shape=[jax.ShapeDtypeStruct((3, r // 2, c), BF) for r, c in shapes],
        scratch_shapes=[pltpu.SemaphoreType.DMA((n, 3)), pltpu.SemaphoreType.DMA((n, 3))],
    )(*p3s)


def _chip_sum(p3, land, shard_shape, axis, idx, name):
    r, c = shard_shape
    hr = r // 2
    tr = 64
    nt = hr // tr

    def body(idx_ref, p_ref, l_ref, o_ref):
        acc = p_ref[...].astype(F32)
        for k in range(3):
            acc = acc + l_ref[k].astype(F32)
        o_ref[...] = acc

    own = (pl.BlockSpec((None, tr, c), lambda i, idx: (idx[0], i, 0)) if axis == 0
           else pl.BlockSpec((None, tr, c), lambda i, idx: (0, i, idx[0])))
    return pl.pallas_call(
        body, name=name,
        grid_spec=pltpu.PrefetchScalarGridSpec(
            num_scalar_prefetch=1, grid=(nt,),
            in_specs=[own, pl.BlockSpec((3, tr, c), lambda i, idx: (0, i, 0))],
            out_specs=pl.BlockSpec((tr, c), lambda i, idx: (idx[1] * nt + i, 0))),
        out_shape=jax.ShapeDtypeStruct((r, c), F32),
        compiler_params=_params(("parallel",)),
    )(idx, p3, land)


def _rs_pair_gather(fulls):
    n = len(fulls)

    def body(*refs):
        ins, outs = refs[:n], refs[n:2 * n]
        send, recv = refs[2 * n:]
        x, y, c, _ = _place()
        cps = []
        for a in range(n):
            hr = fulls[a].shape[0] // 2
            rows = pl.ds(pl.multiple_of(c * hr, 8), hr)
            cp = pltpu.make_async_remote_copy(
                src_ref=ins[a].at[rows, :], dst_ref=outs[a].at[rows, :], send_sem=send.at[a], recv_sem=recv.at[a],
                device_id=(x, y, 1 - c), device_id_type=MESH)
            cp.start()
            cps.append(cp)
        for a, cp in enumerate(cps):
            cp.wait_send()
            hr = fulls[a].shape[0] // 2
            other = pl.ds(pl.multiple_of((1 - c) * hr, 8), hr)
            pltpu.make_async_remote_copy(
                src_ref=ins[a].at[other, :], dst_ref=outs[a].at[other, :], send_sem=send.at[a], recv_sem=recv.at[a],
                device_id=(x, y, 1 - c), device_id_type=MESH).wait_recv()

    return pl.pallas_call(
        body, name="grads_pair_gather",
        in_specs=[HBM] * n, out_specs=[HBM] * n,
        out_shape=[jax.ShapeDtypeStruct(f.shape, F32) for f in fulls],
        input_output_aliases={a: a for a in range(n)},
        scratch_shapes=[pltpu.SemaphoreType.DMA((n,)), pltpu.SemaphoreType.DMA((n,))],
    )(*fulls)


def _reduce_scatter(grads, shapes):
    x, y, c = lax.axis_index("x"), lax.axis_index("y"), lax.axis_index("c")
    cidx = jnp.reshape(c, (1,)).astype(jnp.int32)
    idx = jnp.stack([2 * x + y, c]).astype(jnp.int32)
    g3s = [_as3d(g, s, ax) for g, s, ax in zip(grads, shapes, WEIGHT_AXES)]
    lands = _rs_pair_exchange(g3s)
    p3s = [_pair_sum(g3, l, cidx, f"pair_sum_{a}") for a, (g3, l) in enumerate(zip(g3s, lands))]
    lands2 = _rs_chip_exchange(p3s, shapes)
    fulls = [_chip_sum(p3, l2, s, ax, idx, f"chip_sum_{a}")
             for a, (p3, l2, s, ax) in enumerate(zip(p3s, lands2, shapes, WEIGHT_AXES))]
    return _rs_pair_gather(fulls)


NSMALL = 8


def _small_all_reduce(g_nw, dlb, g_hnw, g_wf, loss_vec):
    def body(nw_ref, lb_ref, hn_ref, wf_ref, ls_ref, out_ref, pack_ref, buf_ref, send, recv):
        x, y, c = lax.axis_index("x"), lax.axis_index("y"), lax.axis_index("c")
        me = 4 * x + 2 * y + c
        pack_ref[...] = jnp.zeros_like(pack_ref)
        pack_ref[0:1, :] = nw_ref[...]
        pack_ref[1:2, :] = lb_ref[...]
        pack_ref[2:3, 0:HK] = hn_ref[...]
        pack_ref[3:4, :] = wf_ref[...]
        pack_ref[4:5, :] = ls_ref[...]
        buf_ref[me] = pack_ref[...]
        cps = []
        for d in range(1, 8):
            dx, dy, dc = d >> 2, (d >> 1) & 1, d & 1
            peer = (1 - x if dx else x, 1 - y if dy else y, 1 - c if dc else c)
            cp = pltpu.make_async_remote_copy(
                src_ref=pack_ref, dst_ref=buf_ref.at[me], send_sem=send.at[d - 1], recv_sem=recv.at[d - 1],
                device_id=peer, device_id_type=MESH)
            cp.start()
            cps.append(cp)
        for d in range(1, 8):
            dx, dy, dc = d >> 2, (d >> 1) & 1, d & 1
            src = 4 * (1 - x if dx else x) + 2 * (1 - y if dy else y) + (1 - c if dc else c)
            pltpu.make_async_remote_copy(
                src_ref=pack_ref, dst_ref=buf_ref.at[src], send_sem=send.at[d - 1], recv_sem=recv.at[d - 1],
                device_id=(x, y, c), device_id_type=MESH).wait_recv()
        for cp in cps:
            cp.wait_send()
        acc = buf_ref[0]
        for i in range(1, 8):
            acc = acc + buf_ref[i]
        out_ref[...] = acc

    vm = pl.BlockSpec(memory_space=pltpu.VMEM)
    return pl.pallas_call(
        body, name="small_all_reduce",
        in_specs=[vm] * 5, out_specs=vm,
        out_shape=jax.ShapeDtypeStruct((NSMALL, D), F32),
        scratch_shapes=[pltpu.VMEM((NSMALL, D), F32), pltpu.VMEM((8, NSMALL, D), F32),
                        pltpu.SemaphoreType.DMA((7,)), pltpu.SemaphoreType.DMA((7,))],
    )(g_nw, dlb, g_hnw, g_wf, loss_vec)


def _adamw_math(w, g, m, v):
    m = B1 * m + (1.0 - B1) * g
    v = B2 * v + (1.0 - B2) * (g * g)
    m_hat = m / (1.0 - B1 ** STEP)
    v_hat = v / (1.0 - B2 ** STEP)
    return -LR * (m_hat / (jnp.sqrt(v_hat) + ADAM_EPS) + WD * w), m, v


def _adamw(w, g, m, v, name):
    r, c = w.shape
    tr = 64

    def body(w_ref, g_ref, m_ref, v_ref, d_ref, nm_ref, nv_ref):
        d_ref[...], nm_ref[...], nv_ref[...] = _adamw_math(w_ref[...], g_ref[...], m_ref[...], v_ref[...])

    blk = pl.BlockSpec((tr, c), lambda i: (i, 0))
    return pl.pallas_call(
        body, name=name, grid=(r // tr,), in_specs=[blk] * 4, out_specs=[blk] * 3,
        out_shape=[jax.ShapeDtypeStruct((r, c), F32)] * 3,
        compiler_params=_params(("parallel",)),
    )(w, g, m, v)


def _small_update(red, lbl, params):
    def body(red_ref, *refs):
        ins, outs = refs[:12], refs[12:]
        lb = _lower_bound(ins[3][...])
        dl0 = red_ref[1:2, :] * lb * (1.0 - lb)
        row = lax.broadcasted_iota(jnp.int32, (2, D), 0)
        grads = [red_ref[0:1, :], jnp.where(row == 0, dl0, -dl0), red_ref[2:3, 0:HK], red_ref[3:4, :]]
        for i, g in enumerate(grads):
            w, m, v = ins[3 * i][...], ins[3 * i + 1][...], ins[3 * i + 2][...]
            d, nm, nv = _adamw_math(w, g, m, v)
            outs[4 * i][...] = g
            outs[4 * i + 1][...] = d
            outs[4 * i + 2][...] = nm
            outs[4 * i + 3][...] = nv
        outs[16][...] = jnp.sum(red_ref[4:5, :], axis=1, keepdims=True)

    flat = [a for p in params for a in p]
    vm = pl.BlockSpec(memory_space=pltpu.VMEM)
    shapes = [jax.ShapeDtypeStruct(p[0].shape, F32) for p in params for _ in range(4)]
    return pl.pallas_call(
        body, name="small_update",
        in_specs=[vm] * 13, out_specs=[vm] * 17,
        out_shape=shapes + [jax.ShapeDtypeStruct((1, 1), F32)],
    )(red, *flat)


def kernel(x, positions, norm_w, w_in, lb_logits, hgrn_norm_w, w_branch_a, w_branch_b, w_out, final_norm_w, loss_target, m_norm_w, m_w_in, m_lb_logits, m_hgrn_norm_w, m_w_branch_a, m_w_branch_b, m_w_out, m_final_norm_w, v_norm_w, v_w_in, v_lb_logits, v_hgrn_norm_w, v_w_branch_a, v_w_branch_b, v_w_out, v_final_norm_w):
    big_w = [w_in[0], w_branch_a[0], w_branch_b[0], w_out[0]]
    big_m = [m_w_in[0], m_w_branch_a[0], m_w_branch_b[0], m_w_out[0]]
    big_v = [v_w_in[0], v_w_branch_a[0], v_w_branch_b[0], v_w_out[0]]
    shapes = [w.shape for w in big_w]
    wf = final_norm_w.reshape(1, D)

    full = _all_gather([w.astype(BF) for w in big_w])
    loc = _local_step(x[0], positions.reshape(T, 1), norm_w, lb_logits, hgrn_norm_w, wf, loss_target[0], *full)
    g_big = _reduce_scatter([loc["g_win"], loc["g_wa"], loc["g_wb"], loc["g_wout"]], shapes)
    red = _small_all_reduce(loc["g_nw"], loc["dlb"], loc["g_hnw"], loc["g_wf"], loc["loss_vec"])

    small = _small_update(red, lb_logits, [
        (norm_w, m_norm_w, v_norm_w), (lb_logits, m_lb_logits, v_lb_logits),
        (hgrn_norm_w, m_hgrn_norm_w, v_hgrn_norm_w),
        (wf, m_final_norm_w.reshape(1, D), v_final_norm_w.reshape(1, D))])
    loss = small[16].reshape(())
    sg, sd, sm, sv = ([small[4 * i + j] for i in range(4)] for j in range(4))
    for lst in (sg, sd, sm, sv):
        lst[3] = lst[3].reshape(D)
    upd = [_adamw(w, g, m, v, f"adamw_{a}") for a, (w, g, m, v) in enumerate(zip(big_w, g_big, big_m, big_v))]
    bg = [g[None] for g in g_big]
    bd, bm, bv = ([u[j][None] for u in upd] for j in range(3))

    def order(s, b):
        return [s[0], b[0], s[1], s[2], b[1], b[2], b[3], s[3]]

    return (loss, loc["gx"][None], *order(sg, bg), *order(sd, bd), *order(sm, bm), *order(sv, bv))
```

```python
import functools

import jax
import jax.numpy as jnp
from jax import lax
from jax.experimental import pallas as pl
from jax.experimental.pallas import tpu as pltpu

T = 2048
D = 1024
NIN = 11264
HEADS = 8
HK = 128
CH = 16
NCH = T // CH
HSTEP = 2
ATT_GROUPS = ((128, 1), (512, 4), (2048, 16))
ATT_COL0 = 4096
AG_COL0 = 8704
GATE_COL0 = 9216
EPS = 1e-6
ROPE_THETA = 10000.0
LR, B1, B2, ADAM_EPS, WD, STEP = 0.001, 0.9, 0.999, 1e-08, 0.01, 10

F32 = jnp.float32
BF = jnp.bfloat16
VMEM_LIMIT = 56 * 1024 * 1024

_NN = (((1,), (0,)), ((), ()))
_NT = (((1,), (1,)), ((), ()))
_TN = (((0,), (0,)), ((), ()))


def _dot(a, b, dims=_NN):
    return lax.dot_general(a, b, dims, preferred_element_type=F32)


def _bdot(a, b, dims=_NN):
    return lax.dot_general(a.astype(BF), b.astype(BF), dims, preferred_element_type=F32)


def _sigmoid(x):
    return jax.nn.sigmoid(x)


def _params(sem=None):
    return pltpu.CompilerParams(dimension_semantics=sem, vmem_limit_bytes=VMEM_LIMIT)


def _matmul(a, b, *, ta=False, tb=False, out_dtype=F32, tm=512, tn=512, tk=None, name):
    m = a.shape[1] if ta else a.shape[0]
    kdim = a.shape[0] if ta else a.shape[1]
    n = b.shape[0] if tb else b.shape[1]
    tk = tk or kdim
    tm, tn = min(tm, m), min(tn, n)
    nm, nn, nk = m // tm, n // tn, kdim // tk
    dims = (((0 if ta else 1,), (1 if tb else 0,)), ((), ()))

    def body(a_ref, b_ref, o_ref, *scratch):
        prod = _bdot(a_ref[...], b_ref[...], dims)
        if nk == 1:
            o_ref[...] = prod.astype(out_dtype)
        else:
            acc = scratch[0]
            k = pl.program_id(2)

            @pl.when(k == 0)
            def _():
                acc[...] = prod

            @pl.when(k > 0)
            def _():
                acc[...] += prod

            @pl.when(k == nk - 1)
            def _():
                o_ref[...] = acc[...].astype(out_dtype)

    a_spec = pl.BlockSpec((tk, tm), lambda i, j, k: (k, i)) if ta else pl.BlockSpec((tm, tk), lambda i, j, k: (i, k))
    b_spec = pl.BlockSpec((tn, tk), lambda i, j, k: (j, k)) if tb else pl.BlockSpec((tk, tn), lambda i, j, k: (k, j))
    return pl.pallas_call(
        body, name=name, grid=(nm, nn, nk),
        in_specs=[a_spec, b_spec],
        out_specs=pl.BlockSpec((tm, tn), lambda i, j, k: (i, j)),
        out_shape=jax.ShapeDtypeStruct((m, n), out_dtype),
        scratch_shapes=[pltpu.VMEM((tm, tn), F32)] if nk > 1 else [],
        compiler_params=_params(("parallel", "parallel", "arbitrary")),
    )(a, b)


DZ_TILE = 512


def _part_offsets(parts):
    counts = [p.shape[1] // DZ_TILE for p in parts]
    offs = [sum(counts[:i]) for i in range(len(parts))]
    return counts, offs


def _part_spec(rows, cnt, off, tile_axis):
    def index(*g):
        return (0 if rows is None else g[0], jnp.clip(g[tile_axis] - off, 0, cnt - 1))
    return index


def _grad_w_in(h, parts):
    counts, offs = _part_offsets(parts)
    n = len(parts)

    def body(h_ref, *refs):
        o_ref = refs[n]
        j = pl.program_id(0)
        for p_ref, cnt, off in zip(refs[:n], counts, offs):
            @pl.when((j >= off) & (j < off + cnt))
            def _(p_ref=p_ref):
                o_ref[...] = _bdot(h_ref[...], p_ref[...], _TN).astype(BF)

    return pl.pallas_call(
        body, name="g_win", grid=(sum(counts),),
        in_specs=[pl.BlockSpec((T, D), lambda j: (0, 0))] +
                 [pl.BlockSpec((T, DZ_TILE), _part_spec(None, c, o, 0)) for c, o in zip(counts, offs)],
        out_specs=pl.BlockSpec((D, DZ_TILE), lambda j: (0, j)),
        out_shape=jax.ShapeDtypeStruct((D, NIN), BF),
        compiler_params=_params(("parallel",)),
    )(h, *parts)


def _grad_h(parts, w_in, p3s, shapes):
    counts, offs = _part_offsets(parts)
    n, m = len(parts), len(p3s)
    tm = 1024
    nm, nk = T // tm, sum(counts)
    side_specs, land_shapes, sems = _chip_exchange_io(shapes) if m else ([], [], [])

    def body(*refs):
        w_ref = refs[n]
        p3_refs = refs[n + 1:n + 1 + m]
        o_ref = refs[n + 1 + m]
        land_refs = refs[n + 2 + m:n + 2 + 2 * m]
        acc = refs[n + 2 + 2 * m]
        i, k = pl.program_id(0), pl.program_id(1)

        def exchange():
            return _chip_exchange_copies(p3_refs, land_refs, *refs[n + 3 + 2 * m:], shapes) if m else []

        @pl.when((i == 0) & (k == 0))
        def _():
            for cp in exchange():
                cp.start()

        @pl.when(k == 0)
        def _():
            acc[...] = jnp.zeros_like(acc)

        for p_ref, cnt, off in zip(refs[:n], counts, offs):
            @pl.when((k >= off) & (k < off + cnt))
            def _(p_ref=p_ref):
                acc[...] += _bdot(p_ref[...], w_ref[...], _NT)

        @pl.when(k == nk - 1)
        def _():
            o_ref[...] = acc[...]

        @pl.when((i == nm - 1) & (k == nk - 1))
        def _():
            for cp in exchange():
                cp.wait()

    out = pl.pallas_call(
        body, name="d_h", grid=(nm, nk),
        in_specs=[pl.BlockSpec((tm, DZ_TILE), _part_spec(0, c, o, 1)) for c, o in zip(counts, offs)] +
                 [pl.BlockSpec((D, DZ_TILE), lambda i, k: (0, k))] + side_specs,
        out_specs=[pl.BlockSpec((tm, D), lambda i, k: (i, 0))] + side_specs,
        out_shape=[jax.ShapeDtypeStruct((T, D), F32)] + land_shapes,
        scratch_shapes=[pltpu.VMEM((tm, D), F32)] + sems,
        compiler_params=_params(("arbitrary", "arbitrary")),
    )(*parts, w_in, *p3s)
    return out[0], out[1:]


def _rmsnorm_fwd(x, w):
    tm = 256

    def body(x_ref, w_ref, h_ref):
        xv = x_ref[...]
        r = lax.rsqrt(jnp.mean(xv * xv, axis=-1, keepdims=True) + EPS)
        h_ref[...] = (xv * r * w_ref[...]).astype(BF)

    return pl.pallas_call(
        body, name="rmsnorm_fwd", grid=(T // tm,),
        in_specs=[pl.BlockSpec((tm, D), lambda i: (i, 0)), pl.BlockSpec((1, D), lambda i: (0, 0))],
        out_specs=pl.BlockSpec((tm, D), lambda i: (i, 0)),
        out_shape=jax.ShapeDtypeStruct((T, D), BF),
        compiler_params=_params(("parallel",)),
    )(x, w)


def _rmsnorm_bwd(x, dh, dout, w):
    tm = 256

    def body(x_ref, dh_ref, dout_ref, w_ref, gx_ref, gw_ref):
        @pl.when(pl.program_id(0) == 0)
        def _():
            gw_ref[...] = jnp.zeros_like(gw_ref)

        xv, dhv = x_ref[...], dh_ref[...]
        r = lax.rsqrt(jnp.mean(xv * xv, axis=-1, keepdims=True) + EPS)
        nrm = xv * r
        dn = dhv * w_ref[...]
        gw_ref[...] += jnp.sum(dhv * nrm, axis=0, keepdims=True)
        gx_ref[...] = dout_ref[...] + r * (dn - nrm * jnp.mean(dn * nrm, axis=-1, keepdims=True))

    row = pl.BlockSpec((tm, D), lambda i: (i, 0))
    vec = pl.BlockSpec((1, D), lambda i: (0, 0))
    return pl.pallas_call(
        body, name="rmsnorm_bwd", grid=(T // tm,),
        in_specs=[row, row, row, vec], out_specs=[row, vec],
        out_shape=[jax.ShapeDtypeStruct((T, D), F32), jax.ShapeDtypeStruct((1, D), F32)],
        compiler_params=_params(("arbitrary",)),
    )(x, dh, dout, w)


def _lower_bound(lbl):
    mx = jnp.max(lbl, axis=0, keepdims=True)
    e = jnp.exp(lbl - mx)
    return e[0:1] / jnp.sum(e, axis=0, keepdims=True)


def _cumsum_rows(g, rows):
    b = g
    sh = 1
    while sh < CH:
        b = b + jnp.where(rows >= sh, pltpu.roll(b, sh, axis=0), 0.0)
        sh *= 2
    return b


def _rev_cumsum_rows(g, rows):
    b = g
    sh = 1
    while sh < CH:
        b = b + jnp.where(rows < CH - sh, pltpu.roll(b, CH - sh, axis=0), 0.0)
        sh *= 2
    return b


SUB = CH // 2


def _direct_block(qb, kb, vb, bb, rows8):
    ob = jnp.zeros_like(qb)
    for s in range(SUB):
        e_s = jnp.exp(jnp.where(rows8 >= s, bb - bb[s:s + 1], -jnp.inf))
        ob = ob + jnp.sum(qb * e_s * kb[s:s + 1], axis=1, keepdims=True) * vb[s:s + 1]
    return ob


def _direct_block_bwd(qb, kb, vb, bb, dob, rows8, rowc8):
    dq = dk = dv = db = jnp.zeros_like(qb)
    for s in range(SUB):
        one = (rowc8 == s).astype(F32)
        ks, vs = kb[s:s + 1], vb[s:s + 1]
        e_s = jnp.exp(jnp.where(rows8 >= s, bb - bb[s:s + 1], -jnp.inf))
        qes = qb * e_s
        w = qes * ks
        a = jnp.sum(w, axis=1, keepdims=True)
        da = jnp.sum(dob * vs, axis=1, keepdims=True)
        dv = dv + one * jnp.sum(a * dob, axis=0, keepdims=True)
        dq = dq + da * e_s * ks
        dk = dk + one * jnp.sum(da * qes, axis=0, keepdims=True)
        u = da * w
        db = db + u - one * jnp.sum(u, axis=0, keepdims=True)
    return dq, dk, dv, db


def _cross_factors(q, k, b):
    ref = b[SUB - 1:SUB]
    e_hi, e_lo = jnp.exp(b[SUB:] - ref), jnp.exp(ref - b[:SUB])
    return q[SUB:] * e_hi, k[:SUB] * e_lo, e_hi, e_lo


def _intra_fwd(q, k, v, b, rows8):
    lo = _direct_block(q[:SUB], k[:SUB], v[:SUB], b[:SUB], rows8)
    hi = _direct_block(q[SUB:], k[SUB:], v[SUB:], b[SUB:], rows8)
    qe_hi, ke_lo, _, _ = _cross_factors(q, k, b)
    for s in range(SUB):
        hi = hi + jnp.sum(qe_hi * ke_lo[s:s + 1], axis=1, keepdims=True) * v[s:s + 1]
    return jnp.concatenate([lo, hi], axis=0)


def _intra_bwd(q, k, v, b, do, rows8, rowc8):
    dq_lo, dk_lo, dv_lo, db_lo = _direct_block_bwd(q[:SUB], k[:SUB], v[:SUB], b[:SUB], do[:SUB], rows8, rowc8)
    dq_hi, dk_hi, dv_hi, db_hi = _direct_block_bwd(q[SUB:], k[SUB:], v[SUB:], b[SUB:], do[SUB:], rows8, rowc8)
    qe_hi, ke_lo, e_hi, e_lo = _cross_factors(q, k, b)
    do_hi, v_lo = do[SUB:], v[:SUB]
    dqe = dke = jnp.zeros_like(qe_hi)
    for s in range(SUB):
        one = (rowc8 == s).astype(F32)
        a = jnp.sum(qe_hi * ke_lo[s:s + 1], axis=1, keepdims=True)
        da = jnp.sum(do_hi * v_lo[s:s + 1], axis=1, keepdims=True)
        dv_lo = dv_lo + one * jnp.sum(a * do_hi, axis=0, keepdims=True)
        dqe = dqe + da * ke_lo[s:s + 1]
        dke = dke + one * jnp.sum(da * qe_hi, axis=0, keepdims=True)
    u_hi, u_lo = dqe * qe_hi, dke * ke_lo
    d_ref = jnp.sum(u_lo, axis=0, keepdims=True) - jnp.sum(u_hi, axis=0, keepdims=True)
    db_lo = db_lo - u_lo + (rowc8 == SUB - 1).astype(F32) * d_ref
    cat = lambda lo, hi: jnp.concatenate([lo, hi], axis=0)
    return (cat(dq_lo, dq_hi + dqe * e_hi), cat(dk_lo + dke * e_lo, dk_hi), cat(dv_lo, dv_hi),
            cat(db_lo, db_hi + u_hi))


def _hgrn_fwd(z, lbl, nw):
    def body(hq_ref, hf_ref, hi_ref, hg_ref, lbl_ref, nw_ref, oraw_ref, og_ref, sh_ref, st_ref):
        @pl.when(pl.program_id(0) == 0)
        def _():
            st_ref[...] = jnp.zeros_like(st_ref)

        lb_all = _lower_bound(lbl_ref[...])
        rows = lax.broadcasted_iota(jnp.int32, (CH, HK), 0)
        rows8 = lax.broadcasted_iota(jnp.int32, (SUB, HK), 0)
        nwv = nw_ref[...]
        for cc, h in [(cc, h) for cc in range(HSTEP) for h in range(HEADS)]:
            rs = slice(CH * cc, CH * (cc + 1))
            sl = slice(HK * h, HK * (h + 1))
            lb = lb_all[:, sl]
            hq, hf, v, hg = hq_ref[rs, sl], hf_ref[rs, sl], hi_ref[rs, sl], hg_ref[rs, sl]
            q = hq * _sigmoid(hq)
            f = lb + (1.0 - lb) * _sigmoid(hf)
            k = 1.0 - f
            b = _cumsum_rows(jnp.log(f), rows)
            sh_ref[cc, h] = st_ref[h]
            o = _bdot(q * jnp.exp(b), st_ref[h], _NT) + _intra_fwd(q, k, v, b, rows8)
            bl = b[CH - 1:CH]
            st_ref[h] = st_ref[h] * jnp.exp(bl)
            st_ref[h] += _bdot(v, k * jnp.exp(bl - b), _TN)
            oraw_ref[rs, sl] = o
            nrm = o * lax.rsqrt(jnp.mean(o * o, axis=1, keepdims=True) + EPS)
            og_ref[rs, sl] = (nrm * nwv * (hg * _sigmoid(hg))).astype(BF)

    zblk = lambda c: pl.BlockSpec((CH * HSTEP, D), lambda i, c=c: (i, c))
    return pl.pallas_call(
        body, name="hgrn_fwd", grid=(NCH // HSTEP,),
        in_specs=[zblk(0), zblk(1), zblk(2), zblk(3),
                  pl.BlockSpec((2, D), lambda i: (0, 0)), pl.BlockSpec((1, HK), lambda i: (0, 0))],
        out_specs=[zblk(0), zblk(0),
                   pl.BlockSpec((HSTEP, HEADS, HK, HK), lambda i: (i, 0, 0, 0))],
        out_shape=[jax.ShapeDtypeStruct((T, D), F32), jax.ShapeDtypeStruct((T, D), BF),
                   jax.ShapeDtypeStruct((NCH, HEADS, HK, HK), F32)],
        scratch_shapes=[pltpu.VMEM((HEADS, HK, HK), F32)],
        compiler_params=_params(("arbitrary",)),
    )(z, z, z, z, lbl, nw)


def _hgrn_bwd(z, lbl, nw, oraw, dog, shist):
    hstep = 1

    def body(hq_ref, hf_ref, hi_ref, hg_ref, lbl_ref, nw_ref, oraw_ref, dog_ref, sh_ref,
             dz_ref, dlb_ref, dnw_ref, dst_ref):
        @pl.when(pl.program_id(0) == 0)
        def _():
            dst_ref[...] = jnp.zeros_like(dst_ref)
            dlb_ref[...] = jnp.zeros_like(dlb_ref)
            dnw_ref[...] = jnp.zeros_like(dnw_ref)

        lb_all = _lower_bound(lbl_ref[...])
        rows = lax.broadcasted_iota(jnp.int32, (CH, HK), 0)
        rowc = lax.broadcasted_iota(jnp.int32, (CH, 1), 0)
        rows8 = lax.broadcasted_iota(jnp.int32, (SUB, HK), 0)
        rowc8 = lax.broadcasted_iota(jnp.int32, (SUB, 1), 0)
        nwv = nw_ref[...]
        dnw = jnp.zeros((1, HK), F32)
        for cc, h in [(cc, h) for cc in reversed(range(hstep)) for h in range(HEADS)]:
            rs = slice(CH * cc, CH * (cc + 1))
            sl = slice(HK * h, HK * (h + 1))
            lb = lb_all[:, sl]
            hq, hf, v, hg = hq_ref[rs, sl], hf_ref[rs, sl], hi_ref[rs, sl], hg_ref[rs, sl]
            o, dg_out = oraw_ref[rs, sl], dog_ref[rs, sl]
            sg = _sigmoid(hg)
            sil = hg * sg
            r = lax.rsqrt(jnp.mean(o * o, axis=1, keepdims=True) + EPS)
            nrm = o * r
            d_hg = dg_out * (nrm * nwv) * (sg * (1.0 + hg * (1.0 - sg)))
            dn = dg_out * nwv * sil
            dnw = dnw + jnp.sum(dg_out * nrm * sil, axis=0, keepdims=True)
            do = r * (dn - nrm * jnp.mean(dn * nrm, axis=1, keepdims=True))
            sq = _sigmoid(hq)
            q = hq * sq
            sig = _sigmoid(hf)
            f = lb + (1.0 - lb) * sig
            k = 1.0 - f
            b = _cumsum_rows(jnp.log(f), rows)
            eb = jnp.exp(b)
            qe = q * eb
            bl = b[CH - 1:CH]
            ebl = jnp.exp(bl)
            kdec = jnp.exp(bl - b)
            ke = k * kdec
            dqe = _bdot(do, sh_ref[cc, h])
            dq = dqe * eb
            db = dqe * qe
            dke = _bdot(v, dst_ref[h])
            dv = _bdot(ke, dst_ref[h], _NT)
            dk = dke * kdec
            rr = dke * ke
            db = db - rr
            db_last = (jnp.sum(rr, axis=0, keepdims=True)
                       + ebl * jnp.sum(dst_ref[h] * sh_ref[cc, h], axis=0, keepdims=True))
            dst_ref[h] = dst_ref[h] * ebl
            dst_ref[h] += _bdot(do, qe, _TN)
            dq_i, dk_i, dv_i, db_i = _intra_bwd(q, k, v, b, do, rows8, rowc8)
            dq, dk, dv = dq + dq_i, dk + dk_i, dv + dv_i
            db = db + db_i + (rowc == CH - 1).astype(F32) * db_last
            dgl = _rev_cumsum_rows(db, rows)
            df = dgl / f - dk
            dlb_ref[:, sl] += jnp.sum(df * (1.0 - sig), axis=0, keepdims=True)
            dz_ref[rs, sl] = (dq * (sq * (1.0 + hq * (1.0 - sq)))).astype(BF)
            dz_ref[rs, D + HK * h:D + HK * (h + 1)] = (df * (1.0 - lb) * sig * (1.0 - sig)).astype(BF)
            dz_ref[rs, 2 * D + HK * h:2 * D + HK * (h + 1)] = dv.astype(BF)
            dz_ref[rs, 3 * D + HK * h:3 * D + HK * (h + 1)] = d_hg.astype(BF)
        dnw_ref[...] += dnw

    rev = lambda i: NCH // hstep - 1 - i
    zblk = lambda c: pl.BlockSpec((CH * hstep, D), lambda i, c=c: (rev(i), c))
    return pl.pallas_call(
        body, name="hgrn_bwd", grid=(NCH // hstep,),
        in_specs=[zblk(0), zblk(1), zblk(2), zblk(3),
                  pl.BlockSpec((2, D), lambda i: (0, 0)), pl.BlockSpec((1, HK), lambda i: (0, 0)),
                  zblk(0), zblk(0),
                  pl.BlockSpec((hstep, HEADS, HK, HK), lambda i: (rev(i), 0, 0, 0))],
        out_specs=[pl.BlockSpec((CH * hstep, 4 * D), lambda i: (rev(i), 0)),
                   pl.BlockSpec((1, D), lambda i: (0, 0)), pl.BlockSpec((1, HK), lambda i: (0, 0))],
        out_shape=[jax.ShapeDtypeStruct((T, 4 * D), BF), jax.ShapeDtypeStruct((1, D), F32),
                   jax.ShapeDtypeStruct((1, HK), F32)],
        scratch_shapes=[pltpu.VMEM((HEADS, HK, HK), F32)],
        compiler_params=_params(("arbitrary",)),
    )(z, z, z, z, lbl, nw, oraw, dog, shist)


BLK = 128
NBLK = T // BLK
QK_SCALE = 0.125


def _head_masks():
    lane = lax.broadcasted_iota(jnp.int32, (1, BLK), 1)
    return [(lane < 64).astype(F32), (lane >= 64).astype(F32)]


def _pieces(dil):
    m = T // dil
    out = []
    for r in range(dil):
        for j in range(m // BLK):
            start = r + dil * BLK * j
            rows = pl.ds(start, BLK, stride=dil) if dil > 1 else pl.ds(start, BLK)
            out.append((rows, r * m + BLK * j))
    return out


def _rope_tables(pos_ref, invf_ref, cos_ref, sa_ref, sb_ref):
    lane = lax.broadcasted_iota(jnp.int32, (256, BLK), 1)
    first = (lane % 64) < 32
    for c in range(T // 256):
        rows = pl.ds(256 * c, 256)
        ang = pos_ref[rows, :].astype(F32) * invf_ref[...]
        s = jnp.sin(ang)
        cos_ref[rows, :] = jnp.cos(ang)
        sa_ref[rows, :] = jnp.where(first, -s, 0.0)
        sb_ref[rows, :] = jnp.where(first, 0.0, s)


def _rope(x, c, sa, sb):
    return x * c + pltpu.roll(x, 96, axis=1) * sa + pltpu.roll(x, 32, axis=1) * sb


def _rope_t(d, c, sa, sb):
    return d * c + pltpu.roll(d * sa, 32, axis=1) + pltpu.roll(d * sb, 96, axis=1)


def _window_bias(bias_ref):
    ii = lax.broadcasted_iota(jnp.int32, (2 * BLK, BLK), 0) % BLK
    jj = lax.broadcasted_iota(jnp.int32, (2 * BLK, BLK), 1)
    bias_ref[0] = jnp.where(jj <= ii, 0.0, -jnp.inf)
    bias_ref[1] = jnp.where(jj >= ii, 0.0, -jnp.inf)


def _blocks(bi):
    if isinstance(bi, int):
        return pl.ds(bi * BLK, BLK), pl.ds(max(bi - 1, 0) * BLK, BLK)
    return (pl.ds(pl.multiple_of(bi * BLK, BLK), BLK),
            pl.ds(pl.multiple_of(jnp.maximum(bi - 1, 0) * BLK, BLK), BLK))


def _stack_heads(x, masks):
    return jnp.concatenate([x * masks[0], x * masks[1]], axis=0).astype(BF)


def _attn_fwd(z, pos, invf):
    def body(q_ref, k_ref, v_ref, ag_ref, pos_ref, invf_ref, ob_ref, opre_ref, lse_ref,
             cos_ref, sa_ref, sb_ref, bias_ref, qr_ref, kr_ref, vr_ref, og_ref, lg_ref, otok_ref, ltok_ref, sc_ref):
        g = pl.program_id(1)
        masks = _head_masks()

        @pl.when(g == 0)
        def _():
            _rope_tables(pos_ref, invf_ref, cos_ref, sa_ref, sb_ref)
            _window_bias(bias_ref)

        def group(gi):
            dil = ATT_GROUPS[gi][1]
            nblk = (T // dil) // BLK
            for rows, dst in _pieces(dil):
                c, sa, sb = cos_ref[rows, :], sa_ref[rows, :], sb_ref[rows, :]
                qr_ref[pl.ds(dst, BLK), :] = _rope(q_ref[rows, :], c, sa, sb) * QK_SCALE
                kr_ref[pl.ds(dst, BLK), :] = _rope(k_ref[rows, :], c, sa, sb)
                vr_ref[pl.ds(dst, BLK), :] = v_ref[rows, :]

            def scores(bi, slot):
                cur, prev = _blocks(bi)
                q2 = _stack_heads(qr_ref[cur, :], masks)
                sc_ref[slot, 0] = _dot(q2, kr_ref[cur, :].astype(BF), _NT) + bias_ref[0]
                if nblk > 1:
                    sc_ref[slot, 1] = (_dot(q2, kr_ref[prev, :].astype(BF), _NT)
                                       + (bias_ref[1] + jnp.where((bi % nblk) != 0, 0.0, -jnp.inf)))

            def finish(bi, slot):
                cur, prev = _blocks(bi)
                s_c, vc = sc_ref[slot, 0], vr_ref[cur, :].astype(BF)
                if nblk > 1:
                    s_p, vp = sc_ref[slot, 1], vr_ref[prev, :].astype(BF)
                    mx = jnp.max(jnp.maximum(s_c, s_p), axis=1, keepdims=True)
                    p_c, p_p = jnp.exp(s_c - mx), jnp.exp(s_p - mx)
                    den = jnp.sum(p_c + p_p, axis=1, keepdims=True)
                    oh = _dot(p_c.astype(BF), vc) + _dot(p_p.astype(BF), vp)
                else:
                    mx = jnp.max(s_c, axis=1, keepdims=True)
                    p_c = jnp.exp(s_c - mx)
                    den = jnp.sum(p_c, axis=1, keepdims=True)
                    oh = _dot(p_c.astype(BF), vc)
                on = oh / den
                lsev = jnp.broadcast_to(mx + jnp.log(den), (2 * BLK, BLK))
                og_ref[cur, :] = on[:BLK] * masks[0] + on[BLK:] * masks[1]
                lg_ref[0, cur, :] = lsev[:BLK]
                lg_ref[1, cur, :] = lsev[BLK:]

            def pair(j, carry):
                finish(2 * j, 0)
                scores(2 * j + 1, 1)
                finish(2 * j + 1, 1)
                scores(jnp.minimum(2 * j + 2, NBLK - 1), 0)
                return carry

            scores(0, 0)
            lax.fori_loop(0, NBLK // 2, pair, 0)
            for rows, src in _pieces(dil):
                srows = pl.ds(src, BLK)
                otok_ref[gi, rows, :] = og_ref[srows, :]
                ltok_ref[gi, 0, rows, :] = lg_ref[0, srows, :]
                ltok_ref[gi, 1, rows, :] = lg_ref[1, srows, :]

        for gi in range(3):
            pl.when(g == gi)(functools.partial(group, gi))

        @pl.when(g == 2)
        def _():
            for c in range(T // BLK):
                rows = pl.ds(BLK * c, BLK)
                wts = []
                for hh in range(2):
                    l0, l1, l2 = ltok_ref[0, hh, rows, :], ltok_ref[1, hh, rows, :], ltok_ref[2, hh, rows, :]
                    mx = jnp.maximum(jnp.maximum(l0, l1), l2)
                    lse = mx + jnp.log(jnp.exp(l0 - mx) + jnp.exp(l1 - mx) + jnp.exp(l2 - mx))
                    lse_ref[rows, BLK * hh:BLK * (hh + 1)] = lse
                    wts.append([jnp.exp(l0 - lse), jnp.exp(l1 - lse), jnp.exp(l2 - lse)])
                o = sum((wts[0][gi] * masks[0] + wts[1][gi] * masks[1]) * otok_ref[gi, rows, :] for gi in range(3))
                ag = ag_ref[rows, :]
                opre_ref[rows, :] = o
                ob_ref[rows, :] = (o * (ag * _sigmoid(ag))).astype(BF)

    c0 = ATT_COL0 // BLK
    zspec = lambda part: pl.BlockSpec((T, BLK), lambda p, g, part=part: (0, c0 + 12 * part + 4 * g + p))
    outspec = pl.BlockSpec((T, BLK), lambda p, g: (0, p))
    big = lambda: pltpu.VMEM((T, BLK), F32)
    return pl.pallas_call(
        body, name="attn_fwd", grid=(4, 3),
        in_specs=[zspec(0), zspec(1), zspec(2),
                  pl.BlockSpec((T, BLK), lambda p, g: (0, AG_COL0 // BLK + p)),
                  pl.BlockSpec((T, 1), lambda p, g: (0, 0)), pl.BlockSpec((1, BLK), lambda p, g: (0, 0))],
        out_specs=[outspec, outspec, pl.BlockSpec((T, 2 * BLK), lambda p, g: (0, p))],
        out_shape=[jax.ShapeDtypeStruct((T, 512), BF), jax.ShapeDtypeStruct((T, 512), F32),
                   jax.ShapeDtypeStruct((T, 8 * BLK), F32)],
        scratch_shapes=[big(), big(), big(), pltpu.VMEM((2, 2 * BLK, BLK), F32), big(), big(), big(), big(),
                        pltpu.VMEM((2, T, BLK), F32), pltpu.VMEM((3, T, BLK), F32), pltpu.VMEM((3, 2, T, BLK), F32),
                        pltpu.VMEM((2, 2, 2 * BLK, BLK), F32)],
        compiler_params=_params(("parallel", "arbitrary")),
    )(z, z, z, z, pos, invf)


def _attn_bwd(z, pos, invf, opre, lse, dob):
    def body(q_ref, k_ref, v_ref, ag_ref, pos_ref, invf_ref, o_ref, lse0_ref, lse1_ref, dob_ref,
             dq_ref, dk_ref, dv_ref, dag_ref,
             cos_ref, sa_ref, sb_ref, bias_ref, dtok_ref, qr_ref, kr_ref, vr_ref, dor_ref, lr_ref, dr_ref,
             dqr_ref, dkr_ref, dvr_ref, pd_ref):
        g = pl.program_id(1)
        masks = _head_masks()

        @pl.when(g == 0)
        def _():
            _rope_tables(pos_ref, invf_ref, cos_ref, sa_ref, sb_ref)
            _window_bias(bias_ref)
            for c in range(T // BLK):
                rows = pl.ds(BLK * c, BLK)
                ag, dob_v, o = ag_ref[rows, :], dob_ref[rows, :], o_ref[rows, :]
                sg = _sigmoid(ag)
                dag_ref[rows, :] = (dob_v * o * (sg * (1.0 + ag * (1.0 - sg)))).astype(BF)
                prod = dob_v * (ag * sg) * o
                for hh, mh in enumerate(masks):
                    dtok_ref[hh, rows, :] = jnp.broadcast_to(jnp.sum(prod * mh, axis=1, keepdims=True), (BLK, BLK))

        def group(gi):
            dil = ATT_GROUPS[gi][1]
            nblk = (T // dil) // BLK
            for rows, dst in _pieces(dil):
                drows = pl.ds(dst, BLK)
                c, sa, sb = cos_ref[rows, :], sa_ref[rows, :], sb_ref[rows, :]
                qr_ref[drows, :] = _rope(q_ref[rows, :], c, sa, sb) * QK_SCALE
                kr_ref[drows, :] = _rope(k_ref[rows, :], c, sa, sb)
                vr_ref[drows, :] = v_ref[rows, :]
                ag = ag_ref[rows, :]
                dor_ref[drows, :] = dob_ref[rows, :] * (ag * _sigmoid(ag))
                for hh, lse_ref in enumerate((lse0_ref, lse1_ref)):
                    lr_ref[hh, drows, :] = lse_ref[rows, :]
                    dr_ref[hh, drows, :] = dtok_ref[hh, rows, :]
            dkr_ref[...] = jnp.zeros_like(dkr_ref)
            dvr_ref[...] = jnp.zeros_like(dvr_ref)

            def probs(bi, slot):
                cur, prev = _blocks(bi)
                q2, do2 = _stack_heads(qr_ref[cur, :], masks), _stack_heads(dor_ref[cur, :], masks)
                lh = jnp.concatenate([lr_ref[0, cur, :], lr_ref[1, cur, :]], axis=0)
                dh = jnp.concatenate([dr_ref[0, cur, :], dr_ref[1, cur, :]], axis=0)
                p_c = jnp.exp(_dot(q2, kr_ref[cur, :].astype(BF), _NT) + bias_ref[0] - lh)
                pd_ref[slot, 0] = p_c.astype(BF)
                pd_ref[slot, 1] = (p_c * (_dot(do2, vr_ref[cur, :].astype(BF), _NT) - dh)).astype(BF)
                if nblk > 1:
                    bias_p = bias_ref[1] + jnp.where((bi % nblk) != 0, 0.0, -jnp.inf)
                    p_p = jnp.exp(_dot(q2, kr_ref[prev, :].astype(BF), _NT) + bias_p - lh)
                    pd_ref[slot, 2] = p_p.astype(BF)
                    pd_ref[slot, 3] = (p_p * (_dot(do2, vr_ref[prev, :].astype(BF), _NT) - dh)).astype(BF)

            def grads(bi, slot):
                cur, prev = _blocks(bi)
                q2, do2 = _stack_heads(qr_ref[cur, :], masks), _stack_heads(dor_ref[cur, :], masks)
                p_c, ds_c = pd_ref[slot, 0], pd_ref[slot, 1]
                dq2 = _dot(ds_c, kr_ref[cur, :].astype(BF))
                dkr_ref[cur, :] += _dot(ds_c, q2, _TN)
                dvr_ref[cur, :] += _dot(p_c, do2, _TN)
                if nblk > 1:
                    p_p, ds_p = pd_ref[slot, 2], pd_ref[slot, 3]
                    dq2 = dq2 + _dot(ds_p, kr_ref[prev, :].astype(BF))
                    dkr_ref[prev, :] += _dot(ds_p, q2, _TN)
                    dvr_ref[prev, :] += _dot(p_p, do2, _TN)
                dqr_ref[cur, :] = dq2[:BLK] * masks[0] + dq2[BLK:] * masks[1]

            def pair(j, carry):
                grads(2 * j, 0)
                probs(2 * j + 1, 1)
                grads(2 * j + 1, 1)
                probs(jnp.minimum(2 * j + 2, NBLK - 1), 0)
                return carry

            probs(0, 0)
            lax.fori_loop(0, NBLK // 2, pair, 0)
            for rows, src in _pieces(dil):
                srows = pl.ds(src, BLK)
                c, sa, sb = cos_ref[rows, :], sa_ref[rows, :], sb_ref[rows, :]
                qr_ref[rows, :] = _rope_t(dqr_ref[srows, :] * QK_SCALE, c, sa, sb)
                kr_ref[rows, :] = _rope_t(dkr_ref[srows, :], c, sa, sb)
                vr_ref[rows, :] = dvr_ref[srows, :]
            for c in range(T // 256):
                rows = pl.ds(256 * c, 256)
                dq_ref[rows, :] = qr_ref[rows, :].astype(BF)
                dk_ref[rows, :] = kr_ref[rows, :].astype(BF)
                dv_ref[rows, :] = vr_ref[rows, :].astype(BF)

        for gi in range(3):
            pl.when(g == gi)(functools.partial(group, gi))

    c0 = ATT_COL0 // BLK
    zspec = lambda part: pl.BlockSpec((T, BLK), lambda p, g, part=part: (0, c0 + 12 * part + 4 * g + p))
    pspec = pl.BlockSpec((T, BLK), lambda p, g: (0, p))
    gspec = pl.BlockSpec((T, BLK), lambda p, g: (0, 4 * g + p))
    big = lambda: pltpu.VMEM((T, BLK), F32)
    two = lambda: pltpu.VMEM((2, T, BLK), F32)
    return pl.pallas_call(
        body, name="attn_bwd", grid=(4, 3),
        in_specs=[zspec(0), zspec(1), zspec(2),
                  pl.BlockSpec((T, BLK), lambda p, g: (0, AG_COL0 // BLK + p)),
                  pl.BlockSpec((T, 1), lambda p, g: (0, 0)), pl.BlockSpec((1, BLK), lambda p, g: (0, 0)),
                  pspec, pl.BlockSpec((T, BLK), lambda p, g: (0, 2 * p)),
                  pl.BlockSpec((T, BLK), lambda p, g: (0, 2 * p + 1)), pspec],
        out_specs=[gspec, gspec, gspec, pspec],
        out_shape=[jax.ShapeDtypeStruct((T, 1536), BF), jax.ShapeDtypeStruct((T, 1536), BF),
                   jax.ShapeDtypeStruct((T, 1536), BF), jax.ShapeDtypeStruct((T, 512), BF)],
        scratch_shapes=[big(), big(), big(), pltpu.VMEM((2, 2 * BLK, BLK), F32), two(), big(), big(), big(), big(),
                        two(), two(), big(), big(), big(), pltpu.VMEM((2, 4, 2 * BLK, BLK), BF)],
        compiler_params=_params(("parallel", "arbitrary")),
    )(z, z, z, z, pos, invf, opre, lse, lse, dob)


def _merge_fwd(ya, yb, z):
    tm = 256

    def body(ya_ref, yb_ref, ga_ref, gb_ref, m_ref):
        m_ref[...] = (_sigmoid(ga_ref[...]) * ya_ref[...] + _sigmoid(gb_ref[...]) * yb_ref[...]).astype(BF)

    row = pl.BlockSpec((tm, D), lambda i: (i, 0))
    return pl.pallas_call(
        body, name="merge_fwd", grid=(T // tm,),
        in_specs=[row, row, pl.BlockSpec((tm, D), lambda i: (i, GATE_COL0 // D)),
                  pl.BlockSpec((tm, D), lambda i: (i, GATE_COL0 // D + 1))],
        out_specs=row, out_shape=jax.ShapeDtypeStruct((T, D), BF),
        compiler_params=_params(("parallel",)),
    )(ya, yb, z, z)


def _out_loss(merged, w_out, x, tgt, wf):
    tm = 256

    def body(m_ref, w_ref, x_ref, t_ref, wf_ref, dout_ref, loss_ref, gwf_ref):
        @pl.when(pl.program_id(0) == 0)
        def _():
            loss_ref[...] = jnp.zeros_like(loss_ref)
            gwf_ref[...] = jnp.zeros_like(gwf_ref)

        out = x_ref[...] + _dot(m_ref[...], w_ref[...])
        r = lax.rsqrt(jnp.mean(out * out, axis=-1, keepdims=True) + EPS)
        yh = out * r
        wfv = wf_ref[...]
        err = yh * wfv - t_ref[...]
        loss_ref[...] += jnp.sum(err * err, axis=0, keepdims=True) * (0.5 / D)
        dy = err * (1.0 / D)
        gwf_ref[...] += jnp.sum(dy * yh, axis=0, keepdims=True)
        dyh = dy * wfv
        dout_ref[...] = r * (dyh - yh * jnp.mean(dyh * yh, axis=-1, keepdims=True))

    row = pl.BlockSpec((tm, D), lambda i: (i, 0))
    vec = pl.BlockSpec((1, D), lambda i: (0, 0))
    return pl.pallas_call(
        body, name="out_loss", grid=(T // tm,),
        in_specs=[row, pl.BlockSpec((D, D), lambda i: (0, 0)), row, row, vec],
        out_specs=[row, vec, vec],
        out_shape=[jax.ShapeDtypeStruct((T, D), F32), jax.ShapeDtypeStruct((1, D), F32),
                   jax.ShapeDtypeStruct((1, D), F32)],
        compiler_params=_params(("arbitrary",)),
    )(merged, w_out, x, tgt, wf)


def _merge_bwd(dm, ya, yb, z):
    tm = 256

    def body(dm_ref, ya_ref, yb_ref, ga_ref, gb_ref, dya_ref, dyb_ref, dg_ref):
        dmv = dm_ref[...]
        sa, sb = _sigmoid(ga_ref[...]), _sigmoid(gb_ref[...])
        dya_ref[...] = (sa * dmv).astype(BF)
        dyb_ref[...] = (sb * dmv).astype(BF)
        dg_ref[:, :D] = (dmv * ya_ref[...] * sa * (1.0 - sa)).astype(BF)
        dg_ref[:, D:] = (dmv * yb_ref[...] * sb * (1.0 - sb)).astype(BF)

    row = pl.BlockSpec((tm, D), lambda i: (i, 0))
    return pl.pallas_call(
        body, name="merge_bwd", grid=(T // tm,),
        in_specs=[row, row, row, pl.BlockSpec((tm, D), lambda i: (i, GATE_COL0 // D)),
                  pl.BlockSpec((tm, D), lambda i: (i, GATE_COL0 // D + 1))],
        out_specs=[row, row, pl.BlockSpec((tm, 2 * D), lambda i: (i, 0))],
        out_shape=[jax.ShapeDtypeStruct((T, D), BF), jax.ShapeDtypeStruct((T, D), BF),
                   jax.ShapeDtypeStruct((T, 2 * D), BF)],
        compiler_params=_params(("parallel",)),
    )(dm, ya, yb, z, z)


def _rope_inv_freq():
    inv = ROPE_THETA ** (-jnp.arange(0, 64, 2, dtype=F32) / 64)
    return jnp.tile(inv, 4).reshape(1, BLK)


def _local_step(x, pos, norm_w, lbl, hnw, wf, tgt, w_in, w_a, w_b, w_out, shard_shapes=()):
    invf = _rope_inv_freq()
    h = _rmsnorm_fwd(x, norm_w)
    z = _matmul(h, w_in, tm=1024, tn=512, name="z_proj")
    oraw, og, shist = _hgrn_fwd(z, lbl, hnw)
    ob, opre, lse = _attn_fwd(z, pos, invf)
    ya = _matmul(og, w_a, tm=1024, tn=512, name="ya_proj")
    yb = _matmul(ob, w_b, tm=1024, tn=512, name="yb_proj")
    merged = _merge_fwd(ya, yb, z)
    dout, loss_vec, g_wf = _out_loss(merged, w_out, x, tgt, wf)

    dm = _matmul(dout, w_out, tb=True, tm=1024, tn=512, name="d_merged")
    g_wout = _matmul(merged, dout, ta=True, out_dtype=BF, tm=512, tn=1024, name="g_wout")
    dya, dyb, dgates = _merge_bwd(dm, ya, yb, z)
    dog = _matmul(dya, w_a, tb=True, tm=1024, tn=512, name="d_og")
    g_wa = _matmul(og, dya, ta=True, out_dtype=BF, tm=512, tn=1024, name="g_wa")
    dob = _matmul(dyb, w_b, tb=True, tm=1024, tn=512, name="d_ob")
    g_wb = _matmul(ob, dyb, ta=True, out_dtype=BF, tm=512, tn=1024, name="g_wb")
    dz_h, dlb, g_hnw = _hgrn_bwd(z, lbl, hnw, oraw, dog, shist)
    dq, dk, dv, dag = _attn_bwd(z, pos, invf, opre, lse, dob)
    dz_parts = [dz_h, dq, dk, dv, dag, dgates]
    grads = [_grad_w_in(h, dz_parts), g_wa, g_wb, g_wout]
    p3s = _rs_partials(grads, shard_shapes) if shard_shapes else []
    dh, landed = _grad_h(dz_parts, w_in, p3s, shard_shapes)
    gx, g_nw = _rmsnorm_bwd(x, dh, dout, norm_w)
    g_big = _rs_finish(p3s, landed, shard_shapes) if shard_shapes else grads
    return dict(loss_vec=loss_vec, gx=gx, g_nw=g_nw, dlb=dlb, g_hnw=g_hnw, g_wf=g_wf,
                g_win=g_big[0], g_wa=g_big[1], g_wb=g_big[2], g_wout=g_big[3])


MESH = pl.DeviceIdType.MESH
HBM = pl.BlockSpec(memory_space=pl.ANY)
WEIGHT_AXES = (1, 0, 1, 0)


def _place():
    x, y, c = lax.axis_index("x"), lax.axis_index("y"), lax.axis_index("c")
    chips = [(1 - x, y), (x, 1 - y), (1 - x, 1 - y)]
    return x, y, c, chips


def _block_half(ref, shard_shape, axis, j, half):
    r, c = shard_shape
    hr = r // 2
    if axis == 0:
        return ref.at[pl.ds(pl.multiple_of(j * r + half * hr, 16), hr), :]
    return ref.at[pl.ds(pl.multiple_of(half * hr, 16), hr), pl.ds(pl.multiple_of(j * c, 128), c)]


def _all_gather(shards):
    n = len(shards)
    shapes = [s.shape for s in shards]

    def body(*refs):
        ins, outs = refs[:n], refs[n:2 * n]
        send1, recv1, send2, recv2, send0, recv0 = refs[2 * n:]
        x, y, c, chips = _place()
        me = 2 * x + y
        started, own = [], []
        for a in range(n):
            r, cc = shapes[a]
            ax = WEIGHT_AXES[a]
            mine = (outs[a].at[pl.ds(pl.multiple_of(me * r, 16), r), :] if ax == 0
                    else outs[a].at[:, pl.ds(pl.multiple_of(me * cc, 128), cc)])
            swap = pltpu.make_async_remote_copy(
                src_ref=ins[a], dst_ref=mine, send_sem=send0.at[a], recv_sem=recv0.at[a],
                device_id=(x, y, 1 - c), device_id_type=MESH)
            swap.start()
            own.append(swap)
            src = ins[a].at[pl.ds(pl.multiple_of(c * (r // 2), 16), r // 2), :]
            for k, (px, py) in enumerate(chips):
                cp = pltpu.make_async_remote_copy(
                    src_ref=src, dst_ref=_block_half(outs[a], shapes[a], ax, me, c),
                    send_sem=send1.at[a, k], recv_sem=recv1.at[a, k], device_id=(px, py, c), device_id_type=MESH)
                cp.start()
                started.append(cp)
        for a in range(n):
            for k, (px, py) in enumerate(chips):
                reg = _block_half(outs[a], shapes[a], WEIGHT_AXES[a], 2 * px + py, c)
                pltpu.make_async_remote_copy(
                    src_ref=reg, dst_ref=reg, send_sem=send1.at[a, k], recv_sem=recv1.at[a, k],
                    device_id=(px, py, c), device_id_type=MESH).wait_recv()
                fw = pltpu.make_async_remote_copy(
                    src_ref=reg, dst_ref=reg, send_sem=send2.at[a, k], recv_sem=recv2.at[a, k],
                    device_id=(x, y, 1 - c), device_id_type=MESH)
                fw.start()
                started.append(fw)
        for a in range(n):
            for k, (px, py) in enumerate(chips):
                reg = _block_half(outs[a], shapes[a], WEIGHT_AXES[a], 2 * px + py, 1 - c)
                pltpu.make_async_remote_copy(
                    src_ref=reg, dst_ref=reg, send_sem=send2.at[a, k], recv_sem=recv2.at[a, k],
                    device_id=(x, y, 1 - c), device_id_type=MESH).wait_recv()
        for cp in started:
            cp.wait_send()
        for cp in own:
            cp.wait()

    full = [(4 * r, c) if ax == 0 else (r, 4 * c) for (r, c), ax in zip(shapes, WEIGHT_AXES)]
    return pl.pallas_call(
        body, name="weights_all_gather",
        in_specs=[HBM] * n, out_specs=[HBM] * n,
        out_shape=[jax.ShapeDtypeStruct(f, BF) for f in full],
        scratch_shapes=[pltpu.SemaphoreType.DMA((n, 3)), pltpu.SemaphoreType.DMA((n, 3)),
                        pltpu.SemaphoreType.DMA((n, 3)), pltpu.SemaphoreType.DMA((n, 3)),
                        pltpu.SemaphoreType.DMA((n,)), pltpu.SemaphoreType.DMA((n,))],
    )(*shards)


def _as3d(g, shard_shape, axis):
    r, c = shard_shape
    return g.reshape(4, r, c) if axis == 0 else g.reshape(1, r, 4 * c)


def _half_rows(ref3, hr, half):
    return ref3.at[:, pl.ds(pl.multiple_of(half * hr, 16), hr), :]


def _rs_pair_exchange(g3s):
    n = len(g3s)

    def body(*refs):
        ins, outs = refs[:n], refs[n:2 * n]
        send, recv = refs[2 * n:]
        x, y, c, _ = _place()
        cps = []
        for a in range(n):
            hr = g3s[a].shape[1] // 2
            cp = pltpu.make_async_remote_copy(
                src_ref=_half_rows(ins[a], hr, 1 - c), dst_ref=outs[a],
                send_sem=send.at[a], recv_sem=recv.at[a], device_id=(x, y, 1 - c), device_id_type=MESH)
            cp.start()
            cps.append(cp)
        for cp in cps:
            cp.wait()

    return pl.pallas_call(
        body, name="grads_pair_exchange",
        in_specs=[HBM] * n, out_specs=[HBM] * n,
        out_shape=[jax.ShapeDtypeStruct((g.shape[0], g.shape[1] // 2, g.shape[2]), BF) for g in g3s],
        scratch_shapes=[pltpu.SemaphoreType.DMA((n,)), pltpu.SemaphoreType.DMA((n,))],
    )(*g3s)


def _pair_sum(g3, land, cidx, name):
    nb, r, w = g3.shape
    hr = r // 2
    tr = 64

    def body(c_ref, g_ref, l_ref, o_ref):
        o_ref[...] = (g_ref[...].astype(F32) + l_ref[...].astype(F32)).astype(BF)

    blk = (nb, tr, w)
    return pl.pallas_call(
        body, name=name,
        grid_spec=pltpu.PrefetchScalarGridSpec(
            num_scalar_prefetch=1, grid=(hr // tr,),
            in_specs=[pl.BlockSpec(blk, lambda i, c: (0, c[0] * (hr // tr) + i, 0)),
                      pl.BlockSpec(blk, lambda i, c: (0, i, 0))],
            out_specs=pl.BlockSpec(blk, lambda i, c: (0, i, 0))),
        out_shape=jax.ShapeDtypeStruct((nb, hr, w), BF),
        compiler_params=_params(("parallel",)),
    )(cidx, g3, land)


def _chip_exchange_copies(ins, outs, send, recv, shapes):
    x, y, c, chips = _place()
    cps = []
    for a in range(len(ins)):
        r, cc = shapes[a]
        for k, (px, py) in enumerate(chips):
            j = 2 * px + py
            src = ins[a].at[j] if WEIGHT_AXES[a] == 0 else ins[a].at[0, :, pl.ds(pl.multiple_of(j * cc, 128), cc)]
            cps.append(pltpu.make_async_remote_copy(
                src_ref=src, dst_ref=outs[a].at[k], send_sem=send.at[a, k], recv_sem=recv.at[a, k],
                device_id=(px, py, c), device_id_type=MESH))
    return cps


def _chip_exchange_io(shapes):
    n = len(shapes)
    return ([HBM] * n, [jax.ShapeDtypeStruct((3, r // 2, c), BF) for r, c in shapes],
            [pltpu.SemaphoreType.DMA((n, 3)), pltpu.SemaphoreType.DMA((n, 3))])


def _chip_sum(p3, land, shard_shape, axis, idx, name):
    r, c = shard_shape
    hr = r // 2
    tr = 64
    nt = hr // tr

    def body(idx_ref, p_ref, l_ref, o_ref):
        acc = p_ref[...].astype(F32)
        for k in range(3):
            acc = acc + l_ref[k].astype(F32)
        o_ref[...] = acc

    own = (pl.BlockSpec((None, tr, c), lambda i, idx: (idx[0], i, 0)) if axis == 0
           else pl.BlockSpec((None, tr, c), lambda i, idx: (0, i, idx[0])))
    return pl.pallas_call(
        body, name=name,
        grid_spec=pltpu.PrefetchScalarGridSpec(
            num_scalar_prefetch=1, grid=(nt,),
            in_specs=[own, pl.BlockSpec((3, tr, c), lambda i, idx: (0, i, 0))],
            out_specs=pl.BlockSpec((tr, c), lambda i, idx: (idx[1] * nt + i, 0))),
        out_shape=jax.ShapeDtypeStruct((r, c), F32),
        compiler_params=_params(("parallel",)),
    )(idx, p3, land)


def _rs_pair_gather(fulls):
    n = len(fulls)

    def body(*refs):
        ins, outs = refs[:n], refs[n:2 * n]
        send, recv = refs[2 * n:]
        x, y, c, _ = _place()
        cps = []
        for a in range(n):
            hr = fulls[a].shape[0] // 2
            rows = pl.ds(pl.multiple_of(c * hr, 8), hr)
            cp = pltpu.make_async_remote_copy(
                src_ref=ins[a].at[rows, :], dst_ref=outs[a].at[rows, :], send_sem=send.at[a], recv_sem=recv.at[a],
                device_id=(x, y, 1 - c), device_id_type=MESH)
            cp.start()
            cps.append(cp)
        for a, cp in enumerate(cps):
            cp.wait_send()
            hr = fulls[a].shape[0] // 2
            other = pl.ds(pl.multiple_of((1 - c) * hr, 8), hr)
            pltpu.make_async_remote_copy(
                src_ref=ins[a].at[other, :], dst_ref=outs[a].at[other, :], send_sem=send.at[a], recv_sem=recv.at[a],
                device_id=(x, y, 1 - c), device_id_type=MESH).wait_recv()

    return pl.pallas_call(
        body, name="grads_pair_gather",
        in_specs=[HBM] * n, out_specs=[HBM] * n,
        out_shape=[jax.ShapeDtypeStruct(f.shape, F32) for f in fulls],
        input_output_aliases={a: a for a in range(n)},
        scratch_shapes=[pltpu.SemaphoreType.DMA((n,)), pltpu.SemaphoreType.DMA((n,))],
    )(*fulls)


def _rs_partials(grads, shapes):
    cidx = jnp.reshape(lax.axis_index("c"), (1,)).astype(jnp.int32)
    g3s = [_as3d(g, s, ax) for g, s, ax in zip(grads, shapes, WEIGHT_AXES)]
    lands = _rs_pair_exchange(g3s)
    return [_pair_sum(g3, l, cidx, f"pair_sum_{a}") for a, (g3, l) in enumerate(zip(g3s, lands))]


def _rs_finish(p3s, landed, shapes):
    x, y, c = lax.axis_index("x"), lax.axis_index("y"), lax.axis_index("c")
    idx = jnp.stack([2 * x + y, c]).astype(jnp.int32)
    fulls = [_chip_sum(p3, l2, s, ax, idx, f"chip_sum_{a}")
             for a, (p3, l2, s, ax) in enumerate(zip(p3s, landed, shapes, WEIGHT_AXES))]
    return _rs_pair_gather(fulls)


NSMALL = 8


def _small_all_reduce(g_nw, dlb, g_hnw, g_wf, loss_vec):
    def body(nw_ref, lb_ref, hn_ref, wf_ref, ls_ref, out_ref, pack_ref, buf_ref, send, recv):
        x, y, c = lax.axis_index("x"), lax.axis_index("y"), lax.axis_index("c")
        me = 4 * x + 2 * y + c
        pack_ref[...] = jnp.zeros_like(pack_ref)
        pack_ref[0:1, :] = nw_ref[...]
        pack_ref[1:2, :] = lb_ref[...]
        pack_ref[2:3, 0:HK] = hn_ref[...]
        pack_ref[3:4, :] = wf_ref[...]
        pack_ref[4:5, :] = ls_ref[...]
        buf_ref[me] = pack_ref[...]
        cps = []
        for d in range(1, 8):
            dx, dy, dc = d >> 2, (d >> 1) & 1, d & 1
            peer = (1 - x if dx else x, 1 - y if dy else y, 1 - c if dc else c)
            cp = pltpu.make_async_remote_copy(
                src_ref=pack_ref, dst_ref=buf_ref.at[me], send_sem=send.at[d - 1], recv_sem=recv.at[d - 1],
                device_id=peer, device_id_type=MESH)
            cp.start()
            cps.append(cp)
        for d in range(1, 8):
            dx, dy, dc = d >> 2, (d >> 1) & 1, d & 1
            src = 4 * (1 - x if dx else x) + 2 * (1 - y if dy else y) + (1 - c if dc else c)
            pltpu.make_async_remote_copy(
                src_ref=pack_ref, dst_ref=buf_ref.at[src], send_sem=send.at[d - 1], recv_sem=recv.at[d - 1],
                device_id=(x, y, c), device_id_type=MESH).wait_recv()
        for cp in cps:
            cp.wait_send()
        acc = buf_ref[0]
        for i in range(1, 8):
            acc = acc + buf_ref[i]
        out_ref[...] = acc

    vm = pl.BlockSpec(memory_space=pltpu.VMEM)
    return pl.pallas_call(
        body, name="small_all_reduce",
        in_specs=[vm] * 5, out_specs=vm,
        out_shape=jax.ShapeDtypeStruct((NSMALL, D), F32),
        scratch_shapes=[pltpu.VMEM((NSMALL, D), F32), pltpu.VMEM((8, NSMALL, D), F32),
                        pltpu.SemaphoreType.DMA((7,)), pltpu.SemaphoreType.DMA((7,))],
    )(g_nw, dlb, g_hnw, g_wf, loss_vec)


def _adamw_math(w, g, m, v):
    m = B1 * m + (1.0 - B1) * g
    v = B2 * v + (1.0 - B2) * (g * g)
    m_hat = m / (1.0 - B1 ** STEP)
    v_hat = v / (1.0 - B2 ** STEP)
    return -LR * (m_hat / (jnp.sqrt(v_hat) + ADAM_EPS) + WD * w), m, v


def _adamw(w, g, m, v, name):
    r, c = w.shape
    tr = 64

    def body(w_ref, g_ref, m_ref, v_ref, d_ref, nm_ref, nv_ref):
        d_ref[...], nm_ref[...], nv_ref[...] = _adamw_math(w_ref[...], g_ref[...], m_ref[...], v_ref[...])

    blk = pl.BlockSpec((tr, c), lambda i: (i, 0))
    return pl.pallas_call(
        body, name=name, grid=(r // tr,), in_specs=[blk] * 4, out_specs=[blk] * 3,
        out_shape=[jax.ShapeDtypeStruct((r, c), F32)] * 3,
        compiler_params=_params(("parallel",)),
    )(w, g, m, v)


def _small_update(red, lbl, params):
    def body(red_ref, *refs):
        ins, outs = refs[:12], refs[12:]
        lb = _lower_bound(ins[3][...])
        dl0 = red_ref[1:2, :] * lb * (1.0 - lb)
        row = lax.broadcasted_iota(jnp.int32, (2, D), 0)
        grads = [red_ref[0:1, :], jnp.where(row == 0, dl0, -dl0), red_ref[2:3, 0:HK], red_ref[3:4, :]]
        for i, g in enumerate(grads):
            w, m, v = ins[3 * i][...], ins[3 * i + 1][...], ins[3 * i + 2][...]
            d, nm, nv = _adamw_math(w, g, m, v)
            outs[4 * i][...] = g
            outs[4 * i + 1][...] = d
            outs[4 * i + 2][...] = nm
            outs[4 * i + 3][...] = nv
        outs[16][...] = jnp.sum(red_ref[4:5, :], axis=1, keepdims=True)

    flat = [a for p in params for a in p]
    vm = pl.BlockSpec(memory_space=pltpu.VMEM)
    shapes = [jax.ShapeDtypeStruct(p[0].shape, F32) for p in params for _ in range(4)]
    return pl.pallas_call(
        body, name="small_update",
        in_specs=[vm] * 13, out_specs=[vm] * 17,
        out_shape=shapes + [jax.ShapeDtypeStruct((1, 1), F32)],
    )(red, *flat)


def kernel(x, positions, norm_w, w_in, lb_logits, hgrn_norm_w, w_branch_a, w_branch_b, w_out, final_norm_w, loss_target, m_norm_w, m_w_in, m_lb_logits, m_hgrn_norm_w, m_w_branch_a, m_w_branch_b, m_w_out, m_final_norm_w, v_norm_w, v_w_in, v_lb_logits, v_hgrn_norm_w, v_w_branch_a, v_w_branch_b, v_w_out, v_final_norm_w):
    big_w = [w_in[0], w_branch_a[0], w_branch_b[0], w_out[0]]
    big_m = [m_w_in[0], m_w_branch_a[0], m_w_branch_b[0], m_w_out[0]]
    big_v = [v_w_in[0], v_w_branch_a[0], v_w_branch_b[0], v_w_out[0]]
    shapes = [w.shape for w in big_w]
    wf = final_norm_w.reshape(1, D)

    full = _all_gather([w.astype(BF) for w in big_w])
    loc = _local_step(x[0], positions.reshape(T, 1), norm_w, lb_logits, hgrn_norm_w, wf, loss_target[0], *full,
                      shard_shapes=shapes)
    g_big = [loc["g_win"], loc["g_wa"], loc["g_wb"], loc["g_wout"]]
    red = _small_all_reduce(loc["g_nw"], loc["dlb"], loc["g_hnw"], loc["g_wf"], loc["loss_vec"])

    small = _small_update(red, lb_logits, [
        (norm_w, m_norm_w, v_norm_w), (lb_logits, m_lb_logits, v_lb_logits),
        (hgrn_norm_w, m_hgrn_norm_w, v_hgrn_norm_w),
        (wf, m_final_norm_w.reshape(1, D), v_final_norm_w.reshape(1, D))])
    loss = small[16].reshape(())
    sg, sd, sm, sv = ([small[4 * i + j] for i in range(4)] for j in range(4))
    for lst in (sg, sd, sm, sv):
        lst[3] = lst[3].reshape(D)
    upd = [_adamw(w, g, m, v, f"adamw_{a}") for a, (w, g, m, v) in enumerate(zip(big_w, g_big, big_m, big_v))]
    bg = [g[None] for g in g_big]
    bd, bm, bv = ([u[j][None] for u in upd] for j in range(3))

    def order(s, b):
        return [s[0], b[0], s[1], s[2], b[1], b[2], b[3], s[3]]

    return (loss, loc["gx"][None], *order(sg, bg), *order(sd, bd), *order(sm, bm), *order(sv, bv))
```

```python
import functools

import jax
import jax.numpy as jnp
from jax import lax
from jax.experimental import pallas as pl
from jax.experimental.pallas import tpu as pltpu

T = 2048
D = 1024
NIN = 11264
HEADS = 8
HK = 128
CH = 16
NCH = T // CH
HSTEP = 2
ATT_GROUPS = ((128, 1), (512, 4), (2048, 16))
ATT_COL0 = 4096
AG_COL0 = 8704
GATE_COL0 = 9216
EPS = 1e-6
ROPE_THETA = 10000.0
LR, B1, B2, ADAM_EPS, WD, STEP = 0.001, 0.9, 0.999, 1e-08, 0.01, 10

F32 = jnp.float32
BF = jnp.bfloat16
VMEM_LIMIT = 56 * 1024 * 1024

_NN = (((1,), (0,)), ((), ()))
_NT = (((1,), (1,)), ((), ()))
_TN = (((0,), (0,)), ((), ()))


def _dot(a, b, dims=_NN):
    return lax.dot_general(a, b, dims, preferred_element_type=F32)


def _bdot(a, b, dims=_NN):
    return lax.dot_general(a.astype(BF), b.astype(BF), dims, preferred_element_type=F32)


def _sigmoid(x):
    return jax.nn.sigmoid(x)


def _params(sem=None):
    return pltpu.CompilerParams(dimension_semantics=sem, vmem_limit_bytes=VMEM_LIMIT)


def _matmul(a, b, *, ta=False, tb=False, out_dtype=F32, tm=512, tn=512, tk=None, name, side=None):
    m = a.shape[1] if ta else a.shape[0]
    kdim = a.shape[0] if ta else a.shape[1]
    n = b.shape[0] if tb else b.shape[1]
    tk = tk or kdim
    tm, tn = min(tm, m), min(tn, n)
    nm, nn, nk = m // tm, n // tn, kdim // tk
    dims = (((0 if ta else 1,), (1 if tb else 0,)), ((), ()))
    s_arrays, s_in_specs, s_shapes, s_out_specs, s_sems = _side_io(side)
    na, no = len(s_arrays), len(s_shapes)
    nacc = 1 if nk > 1 else 0

    def body(*refs):
        a_ref, b_ref = refs[:2]
        s_ins, o_ref, s_outs = refs[2:2 + na], refs[2 + na], refs[3 + na:3 + na + no]
        scratch = refs[3 + na + no:]
        s_sem_refs = scratch[nacc:]
        i, j, k = pl.program_id(0), pl.program_id(1), pl.program_id(2)
        if side is not None:
            @pl.when((i == 0) & (j == 0) & (k == 0))
            def _():
                side.first(s_ins, s_outs, s_sem_refs)

        prod = _bdot(a_ref[...], b_ref[...], dims)
        if nk == 1:
            o_ref[...] = prod.astype(out_dtype)
        else:
            acc = scratch[0]

            @pl.when(k == 0)
            def _():
                acc[...] = prod

            @pl.when(k > 0)
            def _():
                acc[...] += prod

            @pl.when(k == nk - 1)
            def _():
                o_ref[...] = acc[...].astype(out_dtype)

        if side is not None:
            @pl.when((i == nm - 1) & (j == nn - 1) & (k == nk - 1))
            def _():
                side.last(s_ins, s_outs, s_sem_refs)

    a_spec = pl.BlockSpec((tk, tm), lambda i, j, k: (k, i)) if ta else pl.BlockSpec((tm, tk), lambda i, j, k: (i, k))
    b_spec = pl.BlockSpec((tn, tk), lambda i, j, k: (j, k)) if tb else pl.BlockSpec((tk, tn), lambda i, j, k: (k, j))
    sem = ("parallel", "parallel", "arbitrary") if side is None else ("arbitrary",) * 3
    out = pl.pallas_call(
        body, name=name, grid=(nm, nn, nk),
        in_specs=[a_spec, b_spec] + s_in_specs,
        out_specs=[pl.BlockSpec((tm, tn), lambda i, j, k: (i, j))] + s_out_specs,
        out_shape=[jax.ShapeDtypeStruct((m, n), out_dtype)] + s_shapes,
        scratch_shapes=([pltpu.VMEM((tm, tn), F32)] if nk > 1 else []) + s_sems,
        compiler_params=_params(sem),
    )(a, b, *s_arrays)
    return out[0] if side is None else (out[0], out[1:])


DZ_TILE = 512


def _part_offsets(parts):
    counts = [p.shape[1] // DZ_TILE for p in parts]
    offs = [sum(counts[:i]) for i in range(len(parts))]
    return counts, offs


def _part_spec(rows, cnt, off, tile_axis):
    def index(*g):
        return (0 if rows is None else g[0], jnp.clip(g[tile_axis] - off, 0, cnt - 1))
    return index


def _grad_w_in(h, parts):
    counts, offs = _part_offsets(parts)
    n = len(parts)

    def body(h_ref, *refs):
        o_ref = refs[n]
        j = pl.program_id(0)
        for p_ref, cnt, off in zip(refs[:n], counts, offs):
            @pl.when((j >= off) & (j < off + cnt))
            def _(p_ref=p_ref):
                o_ref[...] = _bdot(h_ref[...], p_ref[...], _TN).astype(BF)

    return pl.pallas_call(
        body, name="g_win", grid=(sum(counts),),
        in_specs=[pl.BlockSpec((T, D), lambda j: (0, 0))] +
                 [pl.BlockSpec((T, DZ_TILE), _part_spec(None, c, o, 0)) for c, o in zip(counts, offs)],
        out_specs=pl.BlockSpec((D, DZ_TILE), lambda j: (0, j)),
        out_shape=jax.ShapeDtypeStruct((D, NIN), BF),
        compiler_params=_params(("parallel",)),
    )(h, *parts)


def _side_io(side):
    if side is None:
        return [], [], [], [], []
    return (side.arrays, [HBM] * len(side.arrays), side.out_shapes, [HBM] * len(side.out_shapes), side.sems)


def _grad_x(parts, w_in, x, dout, norm_w, side=None):
    counts, offs = _part_offsets(parts)
    n = len(parts)
    tm = 1024
    nm, nk = T // tm, sum(counts)
    s_arrays, s_in_specs, s_shapes, s_out_specs, s_sems = _side_io(side)
    na, no = len(s_arrays), len(s_shapes)

    def body(*refs):
        w_ref, x_ref, dout_ref, nw_ref = refs[n:n + 4]
        s_ins = refs[n + 4:n + 4 + na]
        gx_ref, gw_ref = refs[n + 4 + na:n + 6 + na]
        s_outs = refs[n + 6 + na:n + 6 + na + no]
        acc = refs[n + 6 + na + no]
        s_sem_refs = refs[n + 7 + na + no:]
        i, k = pl.program_id(0), pl.program_id(1)

        @pl.when((i == 0) & (k == 0))
        def _():
            gw_ref[...] = jnp.zeros_like(gw_ref)
            if side is not None:
                side.first(s_ins, s_outs, s_sem_refs)

        @pl.when(k == 0)
        def _():
            acc[...] = jnp.zeros_like(acc)

        for p_ref, cnt, off in zip(refs[:n], counts, offs):
            @pl.when((k >= off) & (k < off + cnt))
            def _(p_ref=p_ref):
                acc[...] += _bdot(p_ref[...], w_ref[...], _NT)

        @pl.when(k == nk - 1)
        def _():
            gw = jnp.zeros((1, D), F32)
            for c in range(tm // BLK):
                rows = pl.ds(BLK * c, BLK)
                xv, dhv = x_ref[rows, :], acc[rows, :]
                r = lax.rsqrt(jnp.mean(xv * xv, axis=-1, keepdims=True) + EPS)
                nrm = xv * r
                dn = dhv * nw_ref[...]
                gw = gw + jnp.sum(dhv * nrm, axis=0, keepdims=True)
                gx_ref[rows, :] = dout_ref[rows, :] + r * (dn - nrm * jnp.mean(dn * nrm, axis=-1, keepdims=True))
            gw_ref[...] += gw

        if side is not None:
            @pl.when((i == nm - 1) & (k == nk - 1))
            def _():
                side.last(s_ins, s_outs, s_sem_refs)

    row = pl.BlockSpec((tm, D), lambda i, k: (i, 0))
    vec = pl.BlockSpec((1, D), lambda i, k: (0, 0))
    out = pl.pallas_call(
        body, name="grad_x", grid=(nm, nk),
        in_specs=[pl.BlockSpec((tm, DZ_TILE), _part_spec(0, c, o, 1)) for c, o in zip(counts, offs)] +
                 [pl.BlockSpec((D, DZ_TILE), lambda i, k: (0, k)), row, row, vec] + s_in_specs,
        out_specs=[row, vec] + s_out_specs,
        out_shape=[jax.ShapeDtypeStruct((T, D), F32), jax.ShapeDtypeStruct((1, D), F32)] + s_shapes,
        scratch_shapes=[pltpu.VMEM((tm, D), F32)] + s_sems,
        compiler_params=_params(("arbitrary", "arbitrary")),
    )(*parts, w_in, x, dout, norm_w, *s_arrays)
    return out[0], out[1], out[2:]


def _rmsnorm_fwd(x, w):
    tm = 256

    def body(x_ref, w_ref, h_ref):
        xv = x_ref[...]
        r = lax.rsqrt(jnp.mean(xv * xv, axis=-1, keepdims=True) + EPS)
        h_ref[...] = (xv * r * w_ref[...]).astype(BF)

    return pl.pallas_call(
        body, name="rmsnorm_fwd", grid=(T // tm,),
        in_specs=[pl.BlockSpec((tm, D), lambda i: (i, 0)), pl.BlockSpec((1, D), lambda i: (0, 0))],
        out_specs=pl.BlockSpec((tm, D), lambda i: (i, 0)),
        out_shape=jax.ShapeDtypeStruct((T, D), BF),
        compiler_params=_params(("parallel",)),
    )(x, w)


def _lower_bound(lbl):
    mx = jnp.max(lbl, axis=0, keepdims=True)
    e = jnp.exp(lbl - mx)
    return e[0:1] / jnp.sum(e, axis=0, keepdims=True)


def _cumsum_rows(g, rows):
    b = g
    sh = 1
    while sh < CH:
        b = b + jnp.where(rows >= sh, pltpu.roll(b, sh, axis=0), 0.0)
        sh *= 2
    return b


def _rev_cumsum_rows(g, rows):
    b = g
    sh = 1
    while sh < CH:
        b = b + jnp.where(rows < CH - sh, pltpu.roll(b, CH - sh, axis=0), 0.0)
        sh *= 2
    return b


SUB = CH // 2


def _direct_block(qb, kb, vb, bb, rows8):
    ob = jnp.zeros_like(qb)
    for s in range(SUB):
        e_s = jnp.exp(jnp.where(rows8 >= s, bb - bb[s:s + 1], -jnp.inf))
        ob = ob + jnp.sum(qb * e_s * kb[s:s + 1], axis=1, keepdims=True) * vb[s:s + 1]
    return ob


def _direct_block_bwd(qb, kb, vb, bb, dob, rows8, rowc8):
    dq = dk = dv = db = jnp.zeros_like(qb)
    for s in range(SUB):
        one = (rowc8 == s).astype(F32)
        ks, vs = kb[s:s + 1], vb[s:s + 1]
        e_s = jnp.exp(jnp.where(rows8 >= s, bb - bb[s:s + 1], -jnp.inf))
        qes = qb * e_s
        w = qes * ks
        a = jnp.sum(w, axis=1, keepdims=True)
        da = jnp.sum(dob * vs, axis=1, keepdims=True)
        dv = dv + one * jnp.sum(a * dob, axis=0, keepdims=True)
        dq = dq + da * e_s * ks
        dk = dk + one * jnp.sum(da * qes, axis=0, keepdims=True)
        u = da * w
        db = db + u - one * jnp.sum(u, axis=0, keepdims=True)
    return dq, dk, dv, db


def _cross_factors(q, k, b):
    ref = b[SUB - 1:SUB]
    e_hi, e_lo = jnp.exp(b[SUB:] - ref), jnp.exp(ref - b[:SUB])
    return q[SUB:] * e_hi, k[:SUB] * e_lo, e_hi, e_lo


def _intra_fwd(q, k, v, b, rows8):
    lo = _direct_block(q[:SUB], k[:SUB], v[:SUB], b[:SUB], rows8)
    hi = _direct_block(q[SUB:], k[SUB:], v[SUB:], b[SUB:], rows8)
    qe_hi, ke_lo, _, _ = _cross_factors(q, k, b)
    for s in range(SUB):
        hi = hi + jnp.sum(qe_hi * ke_lo[s:s + 1], axis=1, keepdims=True) * v[s:s + 1]
    return jnp.concatenate([lo, hi], axis=0)


def _intra_bwd(q, k, v, b, do, rows8, rowc8):
    dq_lo, dk_lo, dv_lo, db_lo = _direct_block_bwd(q[:SUB], k[:SUB], v[:SUB], b[:SUB], do[:SUB], rows8, rowc8)
    dq_hi, dk_hi, dv_hi, db_hi = _direct_block_bwd(q[SUB:], k[SUB:], v[SUB:], b[SUB:], do[SUB:], rows8, rowc8)
    qe_hi, ke_lo, e_hi, e_lo = _cross_factors(q, k, b)
    do_hi, v_lo = do[SUB:], v[:SUB]
    dqe = dke = jnp.zeros_like(qe_hi)
    for s in range(SUB):
        one = (rowc8 == s).astype(F32)
        a = jnp.sum(qe_hi * ke_lo[s:s + 1], axis=1, keepdims=True)
        da = jnp.sum(do_hi * v_lo[s:s + 1], axis=1, keepdims=True)
        dv_lo = dv_lo + one * jnp.sum(a * do_hi, axis=0, keepdims=True)
        dqe = dqe + da * ke_lo[s:s + 1]
        dke = dke + one * jnp.sum(da * qe_hi, axis=0, keepdims=True)
    u_hi, u_lo = dqe * qe_hi, dke * ke_lo
    d_ref = jnp.sum(u_lo, axis=0, keepdims=True) - jnp.sum(u_hi, axis=0, keepdims=True)
    db_lo = db_lo - u_lo + (rowc8 == SUB - 1).astype(F32) * d_ref
    cat = lambda lo, hi: jnp.concatenate([lo, hi], axis=0)
    return (cat(dq_lo, dq_hi + dqe * e_hi), cat(dk_lo + dke * e_lo, dk_hi), cat(dv_lo, dv_hi),
            cat(db_lo, db_hi + u_hi))


def _hgrn_fwd(z, lbl, nw):
    def body(hq_ref, hf_ref, hi_ref, hg_ref, lbl_ref, nw_ref, oraw_ref, og_ref, sh_ref, st_ref):
        @pl.when(pl.program_id(0) == 0)
        def _():
            st_ref[...] = jnp.zeros_like(st_ref)

        lb_all = _lower_bound(lbl_ref[...])
        rows = lax.broadcasted_iota(jnp.int32, (CH, HK), 0)
        rows8 = lax.broadcasted_iota(jnp.int32, (SUB, HK), 0)
        nwv = nw_ref[...]
        for cc, h in [(cc, h) for cc in range(HSTEP) for h in range(HEADS)]:
            rs = slice(CH * cc, CH * (cc + 1))
            sl = slice(HK * h, HK * (h + 1))
            lb = lb_all[:, sl]
            hq, hf, v, hg = hq_ref[rs, sl], hf_ref[rs, sl], hi_ref[rs, sl], hg_ref[rs, sl]
            q = hq * _sigmoid(hq)
            f = lb + (1.0 - lb) * _sigmoid(hf)
            k = 1.0 - f
            b = _cumsum_rows(jnp.log(f), rows)
            sh_ref[cc, h] = st_ref[h]
            o = _bdot(q * jnp.exp(b), st_ref[h], _NT) + _intra_fwd(q, k, v, b, rows8)
            bl = b[CH - 1:CH]
            st_ref[h] = st_ref[h] * jnp.exp(bl)
            st_ref[h] += _bdot(v, k * jnp.exp(bl - b), _TN)
            oraw_ref[rs, sl] = o
            nrm = o * lax.rsqrt(jnp.mean(o * o, axis=1, keepdims=True) + EPS)
            og_ref[rs, sl] = (nrm * nwv * (hg * _sigmoid(hg))).astype(BF)

    zblk = lambda c: pl.BlockSpec((CH * HSTEP, D), lambda i, c=c: (i, c))
    return pl.pallas_call(
        body, name="hgrn_fwd", grid=(NCH // HSTEP,),
        in_specs=[zblk(0), zblk(1), zblk(2), zblk(3),
                  pl.BlockSpec((2, D), lambda i: (0, 0)), pl.BlockSpec((1, HK), lambda i: (0, 0))],
        out_specs=[zblk(0), zblk(0),
                   pl.BlockSpec((HSTEP, HEADS, HK, HK), lambda i: (i, 0, 0, 0))],
        out_shape=[jax.ShapeDtypeStruct((T, D), F32), jax.ShapeDtypeStruct((T, D), BF),
                   jax.ShapeDtypeStruct((NCH, HEADS, HK, HK), F32)],
        scratch_shapes=[pltpu.VMEM((HEADS, HK, HK), F32)],
        compiler_params=_params(("arbitrary",)),
    )(z, z, z, z, lbl, nw)


def _hgrn_bwd(z, lbl, nw, oraw, dog, shist, side=None):
    hstep = 1
    s_arrays, s_in_specs, s_shapes, s_out_specs, s_sems = _side_io(side)
    na, no = len(s_arrays), len(s_shapes)

    def body(*refs):
        hq_ref, hf_ref, hi_ref, hg_ref, lbl_ref, nw_ref, oraw_ref, dog_ref, sh_ref = refs[:9]
        s_ins = refs[9:9 + na]
        dz_ref, dlb_ref, dnw_ref = refs[9 + na:12 + na]
        s_outs = refs[12 + na:12 + na + no]
        dst_ref = refs[12 + na + no]
        s_sem_refs = refs[13 + na + no:]

        @pl.when(pl.program_id(0) == 0)
        def _():
            dst_ref[...] = jnp.zeros_like(dst_ref)
            dlb_ref[...] = jnp.zeros_like(dlb_ref)
            dnw_ref[...] = jnp.zeros_like(dnw_ref)
            if side is not None:
                side.first(s_ins, s_outs, s_sem_refs)

        lb_all = _lower_bound(lbl_ref[...])
        rows = lax.broadcasted_iota(jnp.int32, (CH, HK), 0)
        rowc = lax.broadcasted_iota(jnp.int32, (CH, 1), 0)
        rows8 = lax.broadcasted_iota(jnp.int32, (SUB, HK), 0)
        rowc8 = lax.broadcasted_iota(jnp.int32, (SUB, 1), 0)
        nwv = nw_ref[...]
        dnw = jnp.zeros((1, HK), F32)
        for cc, h in [(cc, h) for cc in reversed(range(hstep)) for h in range(HEADS)]:
            rs = slice(CH * cc, CH * (cc + 1))
            sl = slice(HK * h, HK * (h + 1))
            lb = lb_all[:, sl]
            hq, hf, v, hg = hq_ref[rs, sl], hf_ref[rs, sl], hi_ref[rs, sl], hg_ref[rs, sl]
            o, dg_out = oraw_ref[rs, sl], dog_ref[rs, sl]
            sg = _sigmoid(hg)
            sil = hg * sg
            r = lax.rsqrt(jnp.mean(o * o, axis=1, keepdims=True) + EPS)
            nrm = o * r
            d_hg = dg_out * (nrm * nwv) * (sg * (1.0 + hg * (1.0 - sg)))
            dn = dg_out * nwv * sil
            dnw = dnw + jnp.sum(dg_out * nrm * sil, axis=0, keepdims=True)
            do = r * (dn - nrm * jnp.mean(dn * nrm, axis=1, keepdims=True))
            sq = _sigmoid(hq)
            q = hq * sq
            sig = _sigmoid(hf)
            f = lb + (1.0 - lb) * sig
            k = 1.0 - f
            b = _cumsum_rows(jnp.log(f), rows)
            eb = jnp.exp(b)
            qe = q * eb
            bl = b[CH - 1:CH]
            ebl = jnp.exp(bl)
            kdec = jnp.exp(bl - b)
            ke = k * kdec
            dqe = _bdot(do, sh_ref[cc, h])
            dq = dqe * eb
            db = dqe * qe
            dke = _bdot(v, dst_ref[h])
            dv = _bdot(ke, dst_ref[h], _NT)
            dk = dke * kdec
            rr = dke * ke
            db = db - rr
            db_last = (jnp.sum(rr, axis=0, keepdims=True)
                       + ebl * jnp.sum(dst_ref[h] * sh_ref[cc, h], axis=0, keepdims=True))
            dst_ref[h] = dst_ref[h] * ebl
            dst_ref[h] += _bdot(do, qe, _TN)
            dq_i, dk_i, dv_i, db_i = _intra_bwd(q, k, v, b, do, rows8, rowc8)
            dq, dk, dv = dq + dq_i, dk + dk_i, dv + dv_i
            db = db + db_i + (rowc == CH - 1).astype(F32) * db_last
            dgl = _rev_cumsum_rows(db, rows)
            df = dgl / f - dk
            dlb_ref[:, sl] += jnp.sum(df * (1.0 - sig), axis=0, keepdims=True)
            dz_ref[rs, sl] = (dq * (sq * (1.0 + hq * (1.0 - sq)))).astype(BF)
            dz_ref[rs, D + HK * h:D + HK * (h + 1)] = (df * (1.0 - lb) * sig * (1.0 - sig)).astype(BF)
            dz_ref[rs, 2 * D + HK * h:2 * D + HK * (h + 1)] = dv.astype(BF)
            dz_ref[rs, 3 * D + HK * h:3 * D + HK * (h + 1)] = d_hg.astype(BF)
        dnw_ref[...] += dnw
        if side is not None:
            @pl.when(pl.program_id(0) == NCH // hstep - 1)
            def _():
                side.last(s_ins, s_outs, s_sem_refs)

    rev = lambda i: NCH // hstep - 1 - i
    zblk = lambda c: pl.BlockSpec((CH * hstep, D), lambda i, c=c: (rev(i), c))
    out = pl.pallas_call(
        body, name="hgrn_bwd", grid=(NCH // hstep,),
        in_specs=[zblk(0), zblk(1), zblk(2), zblk(3),
                  pl.BlockSpec((2, D), lambda i: (0, 0)), pl.BlockSpec((1, HK), lambda i: (0, 0)),
                  zblk(0), zblk(0),
                  pl.BlockSpec((hstep, HEADS, HK, HK), lambda i: (rev(i), 0, 0, 0))] + s_in_specs,
        out_specs=[pl.BlockSpec((CH * hstep, 4 * D), lambda i: (rev(i), 0)),
                   pl.BlockSpec((1, D), lambda i: (0, 0)), pl.BlockSpec((1, HK), lambda i: (0, 0))] + s_out_specs,
        out_shape=[jax.ShapeDtypeStruct((T, 4 * D), BF), jax.ShapeDtypeStruct((1, D), F32),
                   jax.ShapeDtypeStruct((1, HK), F32)] + s_shapes,
        scratch_shapes=[pltpu.VMEM((HEADS, HK, HK), F32)] + s_sems,
        compiler_params=_params(("arbitrary",)),
    )(z, z, z, z, lbl, nw, oraw, dog, shist, *s_arrays)
    return out[0], out[1], out[2], out[3:]


BLK = 128
NBLK = T // BLK
QK_SCALE = 0.125


def _head_masks():
    lane = lax.broadcasted_iota(jnp.int32, (1, BLK), 1)
    return [(lane < 64).astype(F32), (lane >= 64).astype(F32)]


def _pieces(dil):
    m = T // dil
    out = []
    for r in range(dil):
        for j in range(m // BLK):
            start = r + dil * BLK * j
            rows = pl.ds(start, BLK, stride=dil) if dil > 1 else pl.ds(start, BLK)
            out.append((rows, r * m + BLK * j))
    return out


def _rope_tables(pos_ref, invf_ref, cos_ref, sa_ref, sb_ref):
    lane = lax.broadcasted_iota(jnp.int32, (256, BLK), 1)
    first = (lane % 64) < 32
    for c in range(T // 256):
        rows = pl.ds(256 * c, 256)
        ang = pos_ref[rows, :].astype(F32) * invf_ref[...]
        s = jnp.sin(ang)
        cos_ref[rows, :] = jnp.cos(ang)
        sa_ref[rows, :] = jnp.where(first, -s, 0.0)
        sb_ref[rows, :] = jnp.where(first, 0.0, s)


def _rope(x, c, sa, sb):
    return x * c + pltpu.roll(x, 96, axis=1) * sa + pltpu.roll(x, 32, axis=1) * sb


def _rope_t(d, c, sa, sb):
    return d * c + pltpu.roll(d * sa, 32, axis=1) + pltpu.roll(d * sb, 96, axis=1)


def _window_bias(bias_ref):
    ii = lax.broadcasted_iota(jnp.int32, (2 * BLK, BLK), 0) % BLK
    jj = lax.broadcasted_iota(jnp.int32, (2 * BLK, BLK), 1)
    bias_ref[0] = jnp.where(jj <= ii, 0.0, -jnp.inf)
    bias_ref[1] = jnp.where(jj >= ii, 0.0, -jnp.inf)


def _blocks(bi):
    if isinstance(bi, int):
        return pl.ds(bi * BLK, BLK), pl.ds(max(bi - 1, 0) * BLK, BLK)
    return (pl.ds(pl.multiple_of(bi * BLK, BLK), BLK),
            pl.ds(pl.multiple_of(jnp.maximum(bi - 1, 0) * BLK, BLK), BLK))


def _stack_heads(x, masks):
    return jnp.concatenate([x * masks[0], x * masks[1]], axis=0).astype(BF)


def _attn_fwd(z, pos, invf):
    def body(q_ref, k_ref, v_ref, ag_ref, pos_ref, invf_ref, ob_ref, opre_ref, lse_ref,
             cos_ref, sa_ref, sb_ref, bias_ref, qr_ref, kr_ref, vr_ref, og_ref, lg_ref, otok_ref, ltok_ref, sc_ref):
        g = pl.program_id(1)
        masks = _head_masks()

        @pl.when(g == 0)
        def _():
            _rope_tables(pos_ref, invf_ref, cos_ref, sa_ref, sb_ref)
            _window_bias(bias_ref)

        def group(gi):
            dil = ATT_GROUPS[gi][1]
            nblk = (T // dil) // BLK
            for rows, dst in _pieces(dil):
                c, sa, sb = cos_ref[rows, :], sa_ref[rows, :], sb_ref[rows, :]
                qr_ref[pl.ds(dst, BLK), :] = _rope(q_ref[rows, :], c, sa, sb) * QK_SCALE
                kr_ref[pl.ds(dst, BLK), :] = _rope(k_ref[rows, :], c, sa, sb)
                vr_ref[pl.ds(dst, BLK), :] = v_ref[rows, :]

            def scores(bi, slot):
                cur, prev = _blocks(bi)
                q2 = _stack_heads(qr_ref[cur, :], masks)
                sc_ref[slot, 0] = _dot(q2, kr_ref[cur, :].astype(BF), _NT) + bias_ref[0]
                if nblk > 1:
                    sc_ref[slot, 1] = (_dot(q2, kr_ref[prev, :].astype(BF), _NT)
                                       + (bias_ref[1] + jnp.where((bi % nblk) != 0, 0.0, -jnp.inf)))

            def finish(bi, slot):
                cur, prev = _blocks(bi)
                s_c, vc = sc_ref[slot, 0], vr_ref[cur, :].astype(BF)
                if nblk > 1:
                    s_p, vp = sc_ref[slot, 1], vr_ref[prev, :].astype(BF)
                    mx = jnp.max(jnp.maximum(s_c, s_p), axis=1, keepdims=True)
                    p_c, p_p = jnp.exp(s_c - mx), jnp.exp(s_p - mx)
                    den = jnp.sum(p_c + p_p, axis=1, keepdims=True)
                    oh = _dot(p_c.astype(BF), vc) + _dot(p_p.astype(BF), vp)
                else:
                    mx = jnp.max(s_c, axis=1, keepdims=True)
                    p_c = jnp.exp(s_c - mx)
                    den = jnp.sum(p_c, axis=1, keepdims=True)
                    oh = _dot(p_c.astype(BF), vc)
                on = oh / den
                lsev = jnp.broadcast_to(mx + jnp.log(den), (2 * BLK, BLK))
                og_ref[cur, :] = on[:BLK] * masks[0] + on[BLK:] * masks[1]
                lg_ref[0, cur, :] = lsev[:BLK]
                lg_ref[1, cur, :] = lsev[BLK:]

            def pair(j, carry):
                finish(2 * j, 0)
                scores(2 * j + 1, 1)
                finish(2 * j + 1, 1)
                scores(jnp.minimum(2 * j + 2, NBLK - 1), 0)
                return carry

            scores(0, 0)
            lax.fori_loop(0, NBLK // 2, pair, 0)
            for rows, src in _pieces(dil):
                srows = pl.ds(src, BLK)
                otok_ref[gi, rows, :] = og_ref[srows, :]
                ltok_ref[gi, 0, rows, :] = lg_ref[0, srows, :]
                ltok_ref[gi, 1, rows, :] = lg_ref[1, srows, :]

        for gi in range(3):
            pl.when(g == gi)(functools.partial(group, gi))

        @pl.when(g == 2)
        def _():
            for c in range(T // BLK):
                rows = pl.ds(BLK * c, BLK)
                wts = []
                for hh in range(2):
                    l0, l1, l2 = ltok_ref[0, hh, rows, :], ltok_ref[1, hh, rows, :], ltok_ref[2, hh, rows, :]
                    mx = jnp.maximum(jnp.maximum(l0, l1), l2)
                    lse = mx + jnp.log(jnp.exp(l0 - mx) + jnp.exp(l1 - mx) + jnp.exp(l2 - mx))
                    lse_ref[rows, BLK * hh:BLK * (hh + 1)] = lse
                    wts.append([jnp.exp(l0 - lse), jnp.exp(l1 - lse), jnp.exp(l2 - lse)])
                o = sum((wts[0][gi] * masks[0] + wts[1][gi] * masks[1]) * otok_ref[gi, rows, :] for gi in range(3))
                ag = ag_ref[rows, :]
                opre_ref[rows, :] = o
                ob_ref[rows, :] = (o * (ag * _sigmoid(ag))).astype(BF)

    c0 = ATT_COL0 // BLK
    zspec = lambda part: pl.BlockSpec((T, BLK), lambda p, g, part=part: (0, c0 + 12 * part + 4 * g + p))
    outspec = pl.BlockSpec((T, BLK), lambda p, g: (0, p))
    big = lambda: pltpu.VMEM((T, BLK), F32)
    return pl.pallas_call(
        body, name="attn_fwd", grid=(4, 3),
        in_specs=[zspec(0), zspec(1), zspec(2),
                  pl.BlockSpec((T, BLK), lambda p, g: (0, AG_COL0 // BLK + p)),
                  pl.BlockSpec((T, 1), lambda p, g: (0, 0)), pl.BlockSpec((1, BLK), lambda p, g: (0, 0))],
        out_specs=[outspec, outspec, pl.BlockSpec((T, 2 * BLK), lambda p, g: (0, p))],
        out_shape=[jax.ShapeDtypeStruct((T, 512), BF), jax.ShapeDtypeStruct((T, 512), F32),
                   jax.ShapeDtypeStruct((T, 8 * BLK), F32)],
        scratch_shapes=[big(), big(), big(), pltpu.VMEM((2, 2 * BLK, BLK), F32), big(), big(), big(), big(),
                        pltpu.VMEM((2, T, BLK), F32), pltpu.VMEM((3, T, BLK), F32), pltpu.VMEM((3, 2, T, BLK), F32),
                        pltpu.VMEM((2, 2, 2 * BLK, BLK), F32)],
        compiler_params=_params(("parallel", "arbitrary")),
    )(z, z, z, z, pos, invf)


def _attn_bwd(z, pos, invf, opre, lse, dob):
    def body(q_ref, k_ref, v_ref, ag_ref, pos_ref, invf_ref, o_ref, lse0_ref, lse1_ref, dob_ref,
             dq_ref, dk_ref, dv_ref, dag_ref,
             cos_ref, sa_ref, sb_ref, bias_ref, dtok_ref, qr_ref, kr_ref, vr_ref, dor_ref, lr_ref, dr_ref,
             dqr_ref, dkr_ref, dvr_ref, pd_ref):
        g = pl.program_id(1)
        masks = _head_masks()

        @pl.when(g == 0)
        def _():
            _rope_tables(pos_ref, invf_ref, cos_ref, sa_ref, sb_ref)
            _window_bias(bias_ref)
            for c in range(T // BLK):
                rows = pl.ds(BLK * c, BLK)
                ag, dob_v, o = ag_ref[rows, :], dob_ref[rows, :], o_ref[rows, :]
                sg = _sigmoid(ag)
                dag_ref[rows, :] = (dob_v * o * (sg * (1.0 + ag * (1.0 - sg)))).astype(BF)
                prod = dob_v * (ag * sg) * o
                for hh, mh in enumerate(masks):
                    dtok_ref[hh, rows, :] = jnp.broadcast_to(jnp.sum(prod * mh, axis=1, keepdims=True), (BLK, BLK))

        def group(gi):
            dil = ATT_GROUPS[gi][1]
            nblk = (T // dil) // BLK
            for rows, dst in _pieces(dil):
                drows = pl.ds(dst, BLK)
                c, sa, sb = cos_ref[rows, :], sa_ref[rows, :], sb_ref[rows, :]
                qr_ref[drows, :] = _rope(q_ref[rows, :], c, sa, sb) * QK_SCALE
                kr_ref[drows, :] = _rope(k_ref[rows, :], c, sa, sb)
                vr_ref[drows, :] = v_ref[rows, :]
                ag = ag_ref[rows, :]
                dor_ref[drows, :] = dob_ref[rows, :] * (ag * _sigmoid(ag))
                for hh, lse_ref in enumerate((lse0_ref, lse1_ref)):
                    lr_ref[hh, drows, :] = lse_ref[rows, :]
                    dr_ref[hh, drows, :] = dtok_ref[hh, rows, :]
            dkr_ref[...] = jnp.zeros_like(dkr_ref)
            dvr_ref[...] = jnp.zeros_like(dvr_ref)

            def probs(bi, slot):
                cur, prev = _blocks(bi)
                q2, do2 = _stack_heads(qr_ref[cur, :], masks), _stack_heads(dor_ref[cur, :], masks)
                lh = jnp.concatenate([lr_ref[0, cur, :], lr_ref[1, cur, :]], axis=0)
                dh = jnp.concatenate([dr_ref[0, cur, :], dr_ref[1, cur, :]], axis=0)
                p_c = jnp.exp(_dot(q2, kr_ref[cur, :].astype(BF), _NT) + bias_ref[0] - lh)
                pd_ref[slot, 0] = p_c.astype(BF)
                pd_ref[slot, 1] = (p_c * (_dot(do2, vr_ref[cur, :].astype(BF), _NT) - dh)).astype(BF)
                if nblk > 1:
                    bias_p = bias_ref[1] + jnp.where((bi % nblk) != 0, 0.0, -jnp.inf)
                    p_p = jnp.exp(_dot(q2, kr_ref[prev, :].astype(BF), _NT) + bias_p - lh)
                    pd_ref[slot, 2] = p_p.astype(BF)
                    pd_ref[slot, 3] = (p_p * (_dot(do2, vr_ref[prev, :].astype(BF), _NT) - dh)).astype(BF)

            def grads(bi, slot):
                cur, prev = _blocks(bi)
                q2, do2 = _stack_heads(qr_ref[cur, :], masks), _stack_heads(dor_ref[cur, :], masks)
                p_c, ds_c = pd_ref[slot, 0], pd_ref[slot, 1]
                dq2 = _dot(ds_c, kr_ref[cur, :].astype(BF))
                dkr_ref[cur, :] += _dot(ds_c, q2, _TN)
                dvr_ref[cur, :] += _dot(p_c, do2, _TN)
                if nblk > 1:
                    p_p, ds_p = pd_ref[slot, 2], pd_ref[slot, 3]
                    dq2 = dq2 + _dot(ds_p, kr_ref[prev, :].astype(BF))
                    dkr_ref[prev, :] += _dot(ds_p, q2, _TN)
                    dvr_ref[prev, :] += _dot(p_p, do2, _TN)
                dqr_ref[cur, :] = dq2[:BLK] * masks[0] + dq2[BLK:] * masks[1]

            def pair(j, carry):
                grads(2 * j, 0)
                probs(2 * j + 1, 1)
                grads(2 * j + 1, 1)
                probs(jnp.minimum(2 * j + 2, NBLK - 1), 0)
                return carry

            probs(0, 0)
            lax.fori_loop(0, NBLK // 2, pair, 0)
            for rows, src in _pieces(dil):
                srows = pl.ds(src, BLK)
                c, sa, sb = cos_ref[rows, :], sa_ref[rows, :], sb_ref[rows, :]
                qr_ref[rows, :] = _rope_t(dqr_ref[srows, :] * QK_SCALE, c, sa, sb)
                kr_ref[rows, :] = _rope_t(dkr_ref[srows, :], c, sa, sb)
                vr_ref[rows, :] = dvr_ref[srows, :]
            for c in range(T // 256):
                rows = pl.ds(256 * c, 256)
                dq_ref[rows, :] = qr_ref[rows, :].astype(BF)
                dk_ref[rows, :] = kr_ref[rows, :].astype(BF)
                dv_ref[rows, :] = vr_ref[rows, :].astype(BF)

        for gi in range(3):
            pl.when(g == gi)(functools.partial(group, gi))

    c0 = ATT_COL0 // BLK
    zspec = lambda part: pl.BlockSpec((T, BLK), lambda p, g, part=part: (0, c0 + 12 * part + 4 * g + p))
    pspec = pl.BlockSpec((T, BLK), lambda p, g: (0, p))
    gspec = pl.BlockSpec((T, BLK), lambda p, g: (0, 4 * g + p))
    big = lambda: pltpu.VMEM((T, BLK), F32)
    two = lambda: pltpu.VMEM((2, T, BLK), F32)
    return pl.pallas_call(
        body, name="attn_bwd", grid=(4, 3),
        in_specs=[zspec(0), zspec(1), zspec(2),
                  pl.BlockSpec((T, BLK), lambda p, g: (0, AG_COL0 // BLK + p)),
                  pl.BlockSpec((T, 1), lambda p, g: (0, 0)), pl.BlockSpec((1, BLK), lambda p, g: (0, 0)),
                  pspec, pl.BlockSpec((T, BLK), lambda p, g: (0, 2 * p)),
                  pl.BlockSpec((T, BLK), lambda p, g: (0, 2 * p + 1)), pspec],
        out_specs=[gspec, gspec, gspec, pspec],
        out_shape=[jax.ShapeDtypeStruct((T, 1536), BF), jax.ShapeDtypeStruct((T, 1536), BF),
                   jax.ShapeDtypeStruct((T, 1536), BF), jax.ShapeDtypeStruct((T, 512), BF)],
        scratch_shapes=[big(), big(), big(), pltpu.VMEM((2, 2 * BLK, BLK), F32), two(), big(), big(), big(), big(),
                        two(), two(), big(), big(), big(), pltpu.VMEM((2, 4, 2 * BLK, BLK), BF)],
        compiler_params=_params(("parallel", "arbitrary")),
    )(z, z, z, z, pos, invf, opre, lse, lse, dob)


def _merge_fwd(ya, yb, z):
    tm = 256

    def body(ya_ref, yb_ref, ga_ref, gb_ref, m_ref):
        m_ref[...] = (_sigmoid(ga_ref[...]) * ya_ref[...] + _sigmoid(gb_ref[...]) * yb_ref[...]).astype(BF)

    row = pl.BlockSpec((tm, D), lambda i: (i, 0))
    return pl.pallas_call(
        body, name="merge_fwd", grid=(T // tm,),
        in_specs=[row, row, pl.BlockSpec((tm, D), lambda i: (i, GATE_COL0 // D)),
                  pl.BlockSpec((tm, D), lambda i: (i, GATE_COL0 // D + 1))],
        out_specs=row, out_shape=jax.ShapeDtypeStruct((T, D), BF),
        compiler_params=_params(("parallel",)),
    )(ya, yb, z, z)


def _out_loss(merged, w_out, x, tgt, wf):
    tm = 256

    def body(m_ref, w_ref, x_ref, t_ref, wf_ref, dout_ref, loss_ref, gwf_ref):
        @pl.when(pl.program_id(0) == 0)
        def _():
            loss_ref[...] = jnp.zeros_like(loss_ref)
            gwf_ref[...] = jnp.zeros_like(gwf_ref)

        out = x_ref[...] + _dot(m_ref[...], w_ref[...])
        r = lax.rsqrt(jnp.mean(out * out, axis=-1, keepdims=True) + EPS)
        yh = out * r
        wfv = wf_ref[...]
        err = yh * wfv - t_ref[...]
        loss_ref[...] += jnp.sum(err * err, axis=0, keepdims=True) * (0.5 / D)
        dy = err * (1.0 / D)
        gwf_ref[...] += jnp.sum(dy * yh, axis=0, keepdims=True)
        dyh = dy * wfv
        dout_ref[...] = r * (dyh - yh * jnp.mean(dyh * yh, axis=-1, keepdims=True))

    row = pl.BlockSpec((tm, D), lambda i: (i, 0))
    vec = pl.BlockSpec((1, D), lambda i: (0, 0))
    return pl.pallas_call(
        body, name="out_loss", grid=(T // tm,),
        in_specs=[row, pl.BlockSpec((D, D), lambda i: (0, 0)), row, row, vec],
        out_specs=[row, vec, vec],
        out_shape=[jax.ShapeDtypeStruct((T, D), F32), jax.ShapeDtypeStruct((1, D), F32),
                   jax.ShapeDtypeStruct((1, D), F32)],
        compiler_params=_params(("arbitrary",)),
    )(merged, w_out, x, tgt, wf)


def _merge_bwd(dm, ya, yb, z):
    tm = 256

    def body(dm_ref, ya_ref, yb_ref, ga_ref, gb_ref, dya_ref, dyb_ref, dg_ref):
        dmv = dm_ref[...]
        sa, sb = _sigmoid(ga_ref[...]), _sigmoid(gb_ref[...])
        dya_ref[...] = (sa * dmv).astype(BF)
        dyb_ref[...] = (sb * dmv).astype(BF)
        dg_ref[:, :D] = (dmv * ya_ref[...] * sa * (1.0 - sa)).astype(BF)
        dg_ref[:, D:] = (dmv * yb_ref[...] * sb * (1.0 - sb)).astype(BF)

    row = pl.BlockSpec((tm, D), lambda i: (i, 0))
    return pl.pallas_call(
        body, name="merge_bwd", grid=(T // tm,),
        in_specs=[row, row, row, pl.BlockSpec((tm, D), lambda i: (i, GATE_COL0 // D)),
                  pl.BlockSpec((tm, D), lambda i: (i, GATE_COL0 // D + 1))],
        out_specs=[row, row, pl.BlockSpec((tm, 2 * D), lambda i: (i, 0))],
        out_shape=[jax.ShapeDtypeStruct((T, D), BF), jax.ShapeDtypeStruct((T, D), BF),
                   jax.ShapeDtypeStruct((T, 2 * D), BF)],
        compiler_params=_params(("parallel",)),
    )(dm, ya, yb, z, z)


def _rope_inv_freq():
    inv = ROPE_THETA ** (-jnp.arange(0, 64, 2, dtype=F32) / 64)
    return jnp.tile(inv, 4).reshape(1, BLK)


def _local_step(x, pos, norm_w, lbl, hnw, wf, tgt, w_in, w_a, w_b, w_out, shard_shapes=()):
    invf = _rope_inv_freq()
    h = _rmsnorm_fwd(x, norm_w)
    if shard_shapes:
        z, (w_a, w_b, w_out) = _matmul(h, w_in, tm=1024, tn=512, name="z_proj",
                                       side=_gather_side([w_a, w_b, w_out], WEIGHT_AXES[1:]))
    else:
        z = _matmul(h, w_in, tm=1024, tn=512, name="z_proj")
    oraw, og, shist = _hgrn_fwd(z, lbl, hnw)
    ob, opre, lse = _attn_fwd(z, pos, invf)
    ya = _matmul(og, w_a, tm=1024, tn=512, name="ya_proj")
    yb = _matmul(ob, w_b, tm=1024, tn=512, name="yb_proj")
    merged = _merge_fwd(ya, yb, z)
    dout, loss_vec, g_wf = _out_loss(merged, w_out, x, tgt, wf)

    dm = _matmul(dout, w_out, tb=True, tm=1024, tn=512, name="d_merged")
    g_wout = _matmul(merged, dout, ta=True, out_dtype=BF, tm=512, tn=1024, name="g_wout")
    dya, dyb, dgates = _merge_bwd(dm, ya, yb, z)
    dog = _matmul(dya, w_a, tb=True, tm=1024, tn=512, name="d_og")
    g_wa = _matmul(og, dya, ta=True, out_dtype=BF, tm=512, tn=1024, name="g_wa")
    dob = _matmul(dyb, w_b, tb=True, tm=1024, tn=512, name="d_ob")
    g_wb = _matmul(ob, dyb, ta=True, out_dtype=BF, tm=512, tn=1024, name="g_wb")
    small = [g_wa, g_wb, g_wout]
    side_s = side_w = None
    if shard_shapes:
        p3_s = _rs_partials(small, shard_shapes[1:], WEIGHT_AXES[1:], "small")
        side_s = _chip_exchange_side(p3_s, shard_shapes[1:], WEIGHT_AXES[1:])
    dz_h, dlb, g_hnw, land_s = _hgrn_bwd(z, lbl, hnw, oraw, dog, shist, side=side_s)
    dq, dk, dv, dag = _attn_bwd(z, pos, invf, opre, lse, dob)
    dz_parts = [dz_h, dq, dk, dv, dag, dgates]
    g_big = [_grad_w_in(h, dz_parts)] + small
    if shard_shapes:
        p3_w = _rs_partials(g_big[:1], shard_shapes[:1], WEIGHT_AXES[:1], "w_in")
        side_w = _chip_exchange_side(p3_w, shard_shapes[:1], WEIGHT_AXES[:1])
    gx, g_nw, land_w = _grad_x(dz_parts, w_in, x, dout, norm_w, side=side_w)
    if shard_shapes:
        g_big = _rs_finish(p3_w + p3_s, list(land_w) + list(land_s), shard_shapes, WEIGHT_AXES)
    return dict(loss_vec=loss_vec, gx=gx, g_nw=g_nw, dlb=dlb, g_hnw=g_hnw, g_wf=g_wf,
                g_win=g_big[0], g_wa=g_big[1], g_wb=g_big[2], g_wout=g_big[3])


MESH = pl.DeviceIdType.MESH
HBM = pl.BlockSpec(memory_space=pl.ANY)
WEIGHT_AXES = (1, 0, 1, 0)


def _place():
    x, y, c = lax.axis_index("x"), lax.axis_index("y"), lax.axis_index("c")
    chips = [(1 - x, y), (x, 1 - y), (1 - x, 1 - y)]
    return x, y, c, chips


def _block_half(ref, shard_shape, axis, j, half):
    r, c = shard_shape
    hr = r // 2
    if axis == 0:
        return ref.at[pl.ds(pl.multiple_of(j * r + half * hr, 16), hr), :]
    return ref.at[pl.ds(pl.multiple_of(half * hr, 16), hr), pl.ds(pl.multiple_of(j * c, 128), c)]


class _Side:
    def __init__(self, arrays, out_shapes, sems, first, last):
        self.arrays, self.out_shapes, self.sems, self.first, self.last = arrays, out_shapes, sems, first, last


def _gather_side(shards, axes):
    n = len(shards)
    shapes = [s.shape for s in shards]

    def copies(ins, outs, sems):
        send1, recv1, send2, recv2, send0, recv0 = sems
        x, y, c, chips = _place()
        me = 2 * x + y
        own, sends, landed, fwds, fwd_landed = [], [], [], [], []
        for a in range(n):
            r, cc = shapes[a]
            mine = (outs[a].at[pl.ds(pl.multiple_of(me * r, 16), r), :] if axes[a] == 0
                    else outs[a].at[:, pl.ds(pl.multiple_of(me * cc, 128), cc)])
            own.append(pltpu.make_async_remote_copy(
                src_ref=ins[a], dst_ref=mine, send_sem=send0.at[a], recv_sem=recv0.at[a],
                device_id=(x, y, 1 - c), device_id_type=MESH))
            src = ins[a].at[pl.ds(pl.multiple_of(c * (r // 2), 16), r // 2), :]
            for k, (px, py) in enumerate(chips):
                sends.append(pltpu.make_async_remote_copy(
                    src_ref=src, dst_ref=_block_half(outs[a], shapes[a], axes[a], me, c),
                    send_sem=send1.at[a, k], recv_sem=recv1.at[a, k], device_id=(px, py, c), device_id_type=MESH))
                reg = _block_half(outs[a], shapes[a], axes[a], 2 * px + py, c)
                landed.append(pltpu.make_async_remote_copy(
                    src_ref=reg, dst_ref=reg, send_sem=send1.at[a, k], recv_sem=recv1.at[a, k],
                    device_id=(px, py, c), device_id_type=MESH))
                fwds.append(pltpu.make_async_remote_copy(
                    src_ref=reg, dst_ref=reg, send_sem=send2.at[a, k], recv_sem=recv2.at[a, k],
                    device_id=(x, y, 1 - c), device_id_type=MESH))
                oth = _block_half(outs[a], shapes[a], axes[a], 2 * px + py, 1 - c)
                fwd_landed.append(pltpu.make_async_remote_copy(
                    src_ref=oth, dst_ref=oth, send_sem=send2.at[a, k], recv_sem=recv2.at[a, k],
                    device_id=(x, y, 1 - c), device_id_type=MESH))
        return own, sends, landed, fwds, fwd_landed

    def first(ins, outs, sems):
        own, sends, _, _, _ = copies(ins, outs, sems)
        for cp in own + sends:
            cp.start()

    def last(ins, outs, sems):
        own, sends, landed, fwds, fwd_landed = copies(ins, outs, sems)
        for arrived, fw in zip(landed, fwds):
            arrived.wait_recv()
            fw.start()
        for cp in fwd_landed:
            cp.wait_recv()
        for cp in sends + fwds:
            cp.wait_send()
        for cp in own:
            cp.wait()

    full = [(4 * r, c) if ax == 0 else (r, 4 * c) for (r, c), ax in zip(shapes, axes)]
    sems = [pltpu.SemaphoreType.DMA((n, 3)), pltpu.SemaphoreType.DMA((n, 3)),
            pltpu.SemaphoreType.DMA((n, 3)), pltpu.SemaphoreType.DMA((n, 3)),
            pltpu.SemaphoreType.DMA((n,)), pltpu.SemaphoreType.DMA((n,))]
    return _Side(list(shards), [jax.ShapeDtypeStruct(f, BF) for f in full], sems, first, last)


def _run_side(side, name):
    na, no = len(side.arrays), len(side.out_shapes)

    def body(*refs):
        ins, outs, sems = refs[:na], refs[na:na + no], refs[na + no:]
        side.first(ins, outs, sems)
        side.last(ins, outs, sems)

    return pl.pallas_call(
        body, name=name, in_specs=[HBM] * na, out_specs=[HBM] * no,
        out_shape=side.out_shapes, scratch_shapes=side.sems,
    )(*side.arrays)


def _as3d(g, shard_shape, axis):
    r, c = shard_shape
    return g.reshape(4, r, c) if axis == 0 else g.reshape(1, r, 4 * c)


def _half_rows(ref3, hr, half):
    return ref3.at[:, pl.ds(pl.multiple_of(half * hr, 16), hr), :]


def _rs_pair_exchange(g3s, name):
    n = len(g3s)

    def body(*refs):
        ins, outs = refs[:n], refs[n:2 * n]
        send, recv = refs[2 * n:]
        x, y, c, _ = _place()
        cps = []
        for a in range(n):
            hr = g3s[a].shape[1] // 2
            cp = pltpu.make_async_remote_copy(
                src_ref=_half_rows(ins[a], hr, 1 - c), dst_ref=outs[a],
                send_sem=send.at[a], recv_sem=recv.at[a], device_id=(x, y, 1 - c), device_id_type=MESH)
            cp.start()
            cps.append(cp)
        for cp in cps:
            cp.wait()

    return pl.pallas_call(
        body, name=name,
        in_specs=[HBM] * n, out_specs=[HBM] * n,
        out_shape=[jax.ShapeDtypeStruct((g.shape[0], g.shape[1] // 2, g.shape[2]), BF) for g in g3s],
        scratch_shapes=[pltpu.SemaphoreType.DMA((n,)), pltpu.SemaphoreType.DMA((n,))],
    )(*g3s)


def _pair_sum(g3, land, cidx, name):
    nb, r, w = g3.shape
    hr = r // 2
    tr = 64

    def body(c_ref, g_ref, l_ref, o_ref):
        o_ref[...] = (g_ref[...].astype(F32) + l_ref[...].astype(F32)).astype(BF)

    blk = (nb, tr, w)
    return pl.pallas_call(
        body, name=name,
        grid_spec=pltpu.PrefetchScalarGridSpec(
            num_scalar_prefetch=1, grid=(hr // tr,),
            in_specs=[pl.BlockSpec(blk, lambda i, c: (0, c[0] * (hr // tr) + i, 0)),
                      pl.BlockSpec(blk, lambda i, c: (0, i, 0))],
            out_specs=pl.BlockSpec(blk, lambda i, c: (0, i, 0))),
        out_shape=jax.ShapeDtypeStruct((nb, hr, w), BF),
        compiler_params=_params(("parallel",)),
    )(cidx, g3, land)


def _chip_exchange_side(p3s, shapes, axes):
    n = len(p3s)

    def copies(ins, outs, sems):
        send, recv = sems
        x, y, c, chips = _place()
        cps = []
        for a in range(n):
            r, cc = shapes[a]
            for k, (px, py) in enumerate(chips):
                j = 2 * px + py
                src = ins[a].at[j] if axes[a] == 0 else ins[a].at[0, :, pl.ds(pl.multiple_of(j * cc, 128), cc)]
                cps.append(pltpu.make_async_remote_copy(
                    src_ref=src, dst_ref=outs[a].at[k], send_sem=send.at[a, k], recv_sem=recv.at[a, k],
                    device_id=(px, py, c), device_id_type=MESH))
        return cps

    def first(ins, outs, sems):
        for cp in copies(ins, outs, sems):
            cp.start()

    def last(ins, outs, sems):
        for cp in copies(ins, outs, sems):
            cp.wait()

    return _Side(list(p3s), [jax.ShapeDtypeStruct((3, r // 2, c), BF) for r, c in shapes],
                 [pltpu.SemaphoreType.DMA((n, 3)), pltpu.SemaphoreType.DMA((n, 3))], first, last)


def _chip_sum(p3, land, shard_shape, axis, idx, name):
    r, c = shard_shape
    hr = r // 2
    tr = 64
    nt = hr // tr

    def body(idx_ref, p_ref, l_ref, o_ref):
        acc = p_ref[...].astype(F32)
        for k in range(3):
            acc = acc + l_ref[k].astype(F32)
        o_ref[...] = acc

    own = (pl.BlockSpec((None, tr, c), lambda i, idx: (idx[0], i, 0)) if axis == 0
           else pl.BlockSpec((None, tr, c), lambda i, idx: (0, i, idx[0])))
    return pl.pallas_call(
        body, name=name,
        grid_spec=pltpu.PrefetchScalarGridSpec(
            num_scalar_prefetch=1, grid=(nt,),
            in_specs=[own, pl.BlockSpec((3, tr, c), lambda i, idx: (0, i, 0))],
            out_specs=pl.BlockSpec((tr, c), lambda i, idx: (idx[1] * nt + i, 0))),
        out_shape=jax.ShapeDtypeStruct((r, c), F32),
        compiler_params=_params(("parallel",)),
    )(idx, p3, land)


def _rs_pair_gather(fulls):
    n = len(fulls)

    def body(*refs):
        ins, outs = refs[:n], refs[n:2 * n]
        send, recv = refs[2 * n:]
        x, y, c, _ = _place()
        cps = []
        for a in range(n):
            hr = fulls[a].shape[0] // 2
            rows = pl.ds(pl.multiple_of(c * hr, 8), hr)
            cp = pltpu.make_async_remote_copy(
                src_ref=ins[a].at[rows, :], dst_ref=outs[a].at[rows, :], send_sem=send.at[a], recv_sem=recv.at[a],
                device_id=(x, y, 1 - c), device_id_type=MESH)
            cp.start()
            cps.append(cp)
        for a, cp in enumerate(cps):
            cp.wait_send()
            hr = fulls[a].shape[0] // 2
            other = pl.ds(pl.multiple_of((1 - c) * hr, 8), hr)
            pltpu.make_async_remote_copy(
                src_ref=ins[a].at[other, :], dst_ref=outs[a].at[other, :], send_sem=send.at[a], recv_sem=recv.at[a],
                device_id=(x, y, 1 - c), device_id_type=MESH).wait_recv()

    return pl.pallas_call(
        body, name="grads_pair_gather",
        in_specs=[HBM] * n, out_specs=[HBM] * n,
        out_shape=[jax.ShapeDtypeStruct(f.shape, F32) for f in fulls],
        input_output_aliases={a: a for a in range(n)},
        scratch_shapes=[pltpu.SemaphoreType.DMA((n,)), pltpu.SemaphoreType.DMA((n,))],
    )(*fulls)


def _rs_partials(grads, shapes, axes, tag):
    cidx = jnp.reshape(lax.axis_index("c"), (1,)).astype(jnp.int32)
    g3s = [_as3d(g, s, ax) for g, s, ax in zip(grads, shapes, axes)]
    lands = _rs_pair_exchange(g3s, f"grads_pair_exchange_{tag}")
    return [_pair_sum(g3, l, cidx, f"pair_sum_{tag}_{a}") for a, (g3, l) in enumerate(zip(g3s, lands))]


def _rs_finish(p3s, landed, shapes, axes):
    x, y, c = lax.axis_index("x"), lax.axis_index("y"), lax.axis_index("c")
    idx = jnp.stack([2 * x + y, c]).astype(jnp.int32)
    fulls = [_chip_sum(p3, l2, s, ax, idx, f"chip_sum_{a}")
             for a, (p3, l2, s, ax) in enumerate(zip(p3s, landed, shapes, axes))]
    return _rs_pair_gather(fulls)


NSMALL = 8


def _small_all_reduce(g_nw, dlb, g_hnw, g_wf, loss_vec):
    def body(nw_ref, lb_ref, hn_ref, wf_ref, ls_ref, out_ref, pack_ref, buf_ref, send, recv):
        x, y, c = lax.axis_index("x"), lax.axis_index("y"), lax.axis_index("c")
        me = 4 * x + 2 * y + c
        pack_ref[...] = jnp.zeros_like(pack_ref)
        pack_ref[0:1, :] = nw_ref[...]
        pack_ref[1:2, :] = lb_ref[...]
        pack_ref[2:3, 0:HK] = hn_ref[...]
        pack_ref[3:4, :] = wf_ref[...]
        pack_ref[4:5, :] = ls_ref[...]
        buf_ref[me] = pack_ref[...]
        cps = []
        for d in range(1, 8):
            dx, dy, dc = d >> 2, (d >> 1) & 1, d & 1
            peer = (1 - x if dx else x, 1 - y if dy else y, 1 - c if dc else c)
            cp = pltpu.make_async_remote_copy(
                src_ref=pack_ref, dst_ref=buf_ref.at[me], send_sem=send.at[d - 1], recv_sem=recv.at[d - 1],
                device_id=peer, device_id_type=MESH)
            cp.start()
            cps.append(cp)
        for d in range(1, 8):
            dx, dy, dc = d >> 2, (d >> 1) & 1, d & 1
            src = 4 * (1 - x if dx else x) + 2 * (1 - y if dy else y) + (1 - c if dc else c)
            pltpu.make_async_remote_copy(
                src_ref=pack_ref, dst_ref=buf_ref.at[src], send_sem=send.at[d - 1], recv_sem=recv.at[d - 1],
                device_id=(x, y, c), device_id_type=MESH).wait_recv()
        for cp in cps:
            cp.wait_send()
        acc = buf_ref[0]
        for i in range(1, 8):
            acc = acc + buf_ref[i]
        out_ref[...] = acc

    vm = pl.BlockSpec(memory_space=pltpu.VMEM)
    return pl.pallas_call(
        body, name="small_all_reduce",
        in_specs=[vm] * 5, out_specs=vm,
        out_shape=jax.ShapeDtypeStruct((NSMALL, D), F32),
        scratch_shapes=[pltpu.VMEM((NSMALL, D), F32), pltpu.VMEM((8, NSMALL, D), F32),
                        pltpu.SemaphoreType.DMA((7,)), pltpu.SemaphoreType.DMA((7,))],
    )(g_nw, dlb, g_hnw, g_wf, loss_vec)


def _adamw_math(w, g, m, v):
    m = B1 * m + (1.0 - B1) * g
    v = B2 * v + (1.0 - B2) * (g * g)
    m_hat = m / (1.0 - B1 ** STEP)
    v_hat = v / (1.0 - B2 ** STEP)
    return -LR * (m_hat / (jnp.sqrt(v_hat) + ADAM_EPS) + WD * w), m, v


def _adamw(w, g, m, v, name):
    r, c = w.shape
    tr = 64

    def body(w_ref, g_ref, m_ref, v_ref, d_ref, nm_ref, nv_ref):
        d_ref[...], nm_ref[...], nv_ref[...] = _adamw_math(w_ref[...], g_ref[...], m_ref[...], v_ref[...])

    blk = pl.BlockSpec((tr, c), lambda i: (i, 0))
    return pl.pallas_call(
        body, name=name, grid=(r // tr,), in_specs=[blk] * 4, out_specs=[blk] * 3,
        out_shape=[jax.ShapeDtypeStruct((r, c), F32)] * 3,
        compiler_params=_params(("parallel",)),
    )(w, g, m, v)


def _small_update(red, lbl, params):
    def body(red_ref, *refs):
        ins, outs = refs[:12], refs[12:]
        lb = _lower_bound(ins[3][...])
        dl0 = red_ref[1:2, :] * lb * (1.0 - lb)
        row = lax.broadcasted_iota(jnp.int32, (2, D), 0)
        grads = [red_ref[0:1, :], jnp.where(row == 0, dl0, -dl0), red_ref[2:3, 0:HK], red_ref[3:4, :]]
        for i, g in enumerate(grads):
            w, m, v = ins[3 * i][...], ins[3 * i + 1][...], ins[3 * i + 2][...]
            d, nm, nv = _adamw_math(w, g, m, v)
            outs[4 * i][...] = g
            outs[4 * i + 1][...] = d
            outs[4 * i + 2][...] = nm
            outs[4 * i + 3][...] = nv
        outs[16][...] = jnp.sum(red_ref[4:5, :], axis=1, keepdims=True)

    flat = [a for p in params for a in p]
    vm = pl.BlockSpec(memory_space=pltpu.VMEM)
    shapes = [jax.ShapeDtypeStruct(p[0].shape, F32) for p in params for _ in range(4)]
    return pl.pallas_call(
        body, name="small_update",
        in_specs=[vm] * 13, out_specs=[vm] * 17,
        out_shape=shapes + [jax.ShapeDtypeStruct((1, 1), F32)],
    )(red, *flat)


def kernel(x, positions, norm_w, w_in, lb_logits, hgrn_norm_w, w_branch_a, w_branch_b, w_out, final_norm_w, loss_target, m_norm_w, m_w_in, m_lb_logits, m_hgrn_norm_w, m_w_branch_a, m_w_branch_b, m_w_out, m_final_norm_w, v_norm_w, v_w_in, v_lb_logits, v_hgrn_norm_w, v_w_branch_a, v_w_branch_b, v_w_out, v_final_norm_w):
    big_w = [w_in[0], w_branch_a[0], w_branch_b[0], w_out[0]]
    big_m = [m_w_in[0], m_w_branch_a[0], m_w_branch_b[0], m_w_out[0]]
    big_v = [v_w_in[0], v_w_branch_a[0], v_w_branch_b[0], v_w_out[0]]
    shapes = [w.shape for w in big_w]
    wf = final_norm_w.reshape(1, D)

    shards = [w.astype(BF) for w in big_w]
    w_in_full, = _run_side(_gather_side(shards[:1], WEIGHT_AXES[:1]), "w_in_all_gather")
    loc = _local_step(x[0], positions.reshape(T, 1), norm_w, lb_logits, hgrn_norm_w, wf, loss_target[0],
                      w_in_full, *shards[1:], shard_shapes=shapes)
    g_big = [loc["g_win"], loc["g_wa"], loc["g_wb"], loc["g_wout"]]
    red = _small_all_reduce(loc["g_nw"], loc["dlb"], loc["g_hnw"], loc["g_wf"], loc["loss_vec"])

    small = _small_update(red, lb_logits, [
        (norm_w, m_norm_w, v_norm_w), (lb_logits, m_lb_logits, v_lb_logits),
        (hgrn_norm_w, m_hgrn_norm_w, v_hgrn_norm_w),
        (wf, m_final_norm_w.reshape(1, D), v_final_norm_w.reshape(1, D))])
    loss = small[16].reshape(())
    sg, sd, sm, sv = ([small[4 * i + j] for i in range(4)] for j in range(4))
    for lst in (sg, sd, sm, sv):
        lst[3] = lst[3].reshape(D)
    upd = [_adamw(w, g, m, v, f"adamw_{a}") for a, (w, g, m, v) in enumerate(zip(big_w, g_big, big_m, big_v))]
    bg = [g[None] for g in g_big]
    bd, bm, bv = ([u[j][None] for u in upd] for j in range(3))

    def order(s, b):
        return [s[0], b[0], s[1], s[2], b[1], b[2], b[3], s[3]]

    return (loss, loc["gx"][None], *order(sg, bg), *order(sd, bd), *order(sm, bm), *order(sv, bv))
```

```python
import functools

import jax
import jax.numpy as jnp
from jax import lax
from jax.experimental import pallas as pl
from jax.experimental.pallas import tpu as pltpu

T = 2048
D = 1024
NIN = 11264
HEADS = 8
HK = 128
CH = 16
NCH = T // CH
HSTEP = 2
ATT_GROUPS = ((128, 1), (512, 4), (2048, 16))
ATT_COL0 = 4096
AG_COL0 = 8704
GATE_COL0 = 9216
EPS = 1e-6
ROPE_THETA = 10000.0
LR, B1, B2, ADAM_EPS, WD, STEP = 0.001, 0.9, 0.999, 1e-08, 0.01, 10

F32 = jnp.float32
BF = jnp.bfloat16
VMEM_LIMIT = 56 * 1024 * 1024

_NN = (((1,), (0,)), ((), ()))
_NT = (((1,), (1,)), ((), ()))
_TN = (((0,), (0,)), ((), ()))


def _dot(a, b, dims=_NN):
    return lax.dot_general(a, b, dims, preferred_element_type=F32)


def _bdot(a, b, dims=_NN):
    return lax.dot_general(a.astype(BF), b.astype(BF), dims, preferred_element_type=F32)


def _sigmoid(x):
    return jax.nn.sigmoid(x)


def _params(sem=None):
    return pltpu.CompilerParams(dimension_semantics=sem, vmem_limit_bytes=VMEM_LIMIT)


def _matmul(a, b, *, ta=False, tb=False, out_dtype=F32, tm=512, tn=512, tk=None, name, side=None):
    m = a.shape[1] if ta else a.shape[0]
    kdim = a.shape[0] if ta else a.shape[1]
    n = b.shape[0] if tb else b.shape[1]
    tk = tk or kdim
    tm, tn = min(tm, m), min(tn, n)
    nm, nn, nk = m // tm, n // tn, kdim // tk
    dims = (((0 if ta else 1,), (1 if tb else 0,)), ((), ()))
    s_arrays, s_in_specs, s_shapes, s_out_specs, s_sems = _side_io(side)
    na, no = len(s_arrays), len(s_shapes)
    nacc = 1 if nk > 1 else 0

    def body(*refs):
        a_ref, b_ref = refs[:2]
        s_ins, o_ref, s_outs = refs[2:2 + na], refs[2 + na], refs[3 + na:3 + na + no]
        scratch = refs[3 + na + no:]
        s_sem_refs = scratch[nacc:]
        i, j, k = pl.program_id(0), pl.program_id(1), pl.program_id(2)
        if side is not None:
            @pl.when((i == 0) & (j == 0) & (k == 0))
            def _():
                side.first(s_ins, s_outs, s_sem_refs)

        prod = _bdot(a_ref[...], b_ref[...], dims)
        if nk == 1:
            o_ref[...] = prod.astype(out_dtype)
        else:
            acc = scratch[0]

            @pl.when(k == 0)
            def _():
                acc[...] = prod

            @pl.when(k > 0)
            def _():
                acc[...] += prod

            @pl.when(k == nk - 1)
            def _():
                o_ref[...] = acc[...].astype(out_dtype)

        if side is not None:
            @pl.when((i == nm - 1) & (j == nn - 1) & (k == nk - 1))
            def _():
                side.last(s_ins, s_outs, s_sem_refs)

    a_spec = pl.BlockSpec((tk, tm), lambda i, j, k: (k, i)) if ta else pl.BlockSpec((tm, tk), lambda i, j, k: (i, k))
    b_spec = pl.BlockSpec((tn, tk), lambda i, j, k: (j, k)) if tb else pl.BlockSpec((tk, tn), lambda i, j, k: (k, j))
    sem = ("parallel", "parallel", "arbitrary") if side is None else ("arbitrary",) * 3
    out = pl.pallas_call(
        body, name=name, grid=(nm, nn, nk),
        in_specs=[a_spec, b_spec] + s_in_specs,
        out_specs=[pl.BlockSpec((tm, tn), lambda i, j, k: (i, j))] + s_out_specs,
        out_shape=[jax.ShapeDtypeStruct((m, n), out_dtype)] + s_shapes,
        scratch_shapes=([pltpu.VMEM((tm, tn), F32)] if nk > 1 else []) + s_sems,
        compiler_params=_params(sem),
    )(a, b, *s_arrays)
    return out[0] if side is None else (out[0], out[1:])


DZ_TILE = 512


def _part_offsets(parts):
    counts = [p.shape[1] // DZ_TILE for p in parts]
    offs = [sum(counts[:i]) for i in range(len(parts))]
    return counts, offs


def _part_spec(rows, cnt, off, tile_axis):
    def index(*g):
        return (0 if rows is None else g[0], jnp.clip(g[tile_axis] - off, 0, cnt - 1))
    return index


def _grad_w_in(h, parts):
    counts, offs = _part_offsets(parts)
    n = len(parts)

    def body(h_ref, *refs):
        o_ref = refs[n]
        j = pl.program_id(0)
        for p_ref, cnt, off in zip(refs[:n], counts, offs):
            @pl.when((j >= off) & (j < off + cnt))
            def _(p_ref=p_ref):
                o_ref[...] = _bdot(h_ref[...], p_ref[...], _TN).astype(BF)

    return pl.pallas_call(
        body, name="g_win", grid=(sum(counts),),
        in_specs=[pl.BlockSpec((T, D), lambda j: (0, 0))] +
                 [pl.BlockSpec((T, DZ_TILE), _part_spec(None, c, o, 0)) for c, o in zip(counts, offs)],
        out_specs=pl.BlockSpec((D, DZ_TILE), lambda j: (0, j)),
        out_shape=jax.ShapeDtypeStruct((D, NIN), BF),
        compiler_params=_params(("parallel",)),
    )(h, *parts)


def _side_io(side):
    if side is None:
        return [], [], [], [], []
    return (side.arrays, [HBM] * len(side.arrays), side.out_shapes, [HBM] * len(side.out_shapes), side.sems)


def _grad_x(parts, w_in, x, dout, norm_w, side=None):
    counts, offs = _part_offsets(parts)
    n = len(parts)
    tm = 1024
    nm, nk = T // tm, sum(counts)
    s_arrays, s_in_specs, s_shapes, s_out_specs, s_sems = _side_io(side)
    na, no = len(s_arrays), len(s_shapes)

    def body(*refs):
        w_ref, x_ref, dout_ref, nw_ref = refs[n:n + 4]
        s_ins = refs[n + 4:n + 4 + na]
        gx_ref, gw_ref = refs[n + 4 + na:n + 6 + na]
        s_outs = refs[n + 6 + na:n + 6 + na + no]
        acc = refs[n + 6 + na + no]
        s_sem_refs = refs[n + 7 + na + no:]
        i, k = pl.program_id(0), pl.program_id(1)

        @pl.when((i == 0) & (k == 0))
        def _():
            gw_ref[...] = jnp.zeros_like(gw_ref)
            if side is not None:
                side.first(s_ins, s_outs, s_sem_refs)

        @pl.when(k == 0)
        def _():
            acc[...] = jnp.zeros_like(acc)

        for p_ref, cnt, off in zip(refs[:n], counts, offs):
            @pl.when((k >= off) & (k < off + cnt))
            def _(p_ref=p_ref):
                acc[...] += _bdot(p_ref[...], w_ref[...], _NT)

        @pl.when(k == nk - 1)
        def _():
            gw = jnp.zeros((1, D), F32)
            for c in range(tm // BLK):
                rows = pl.ds(BLK * c, BLK)
                xv, dhv = x_ref[rows, :], acc[rows, :]
                r = lax.rsqrt(jnp.mean(xv * xv, axis=-1, keepdims=True) + EPS)
                nrm = xv * r
                dn = dhv * nw_ref[...]
                gw = gw + jnp.sum(dhv * nrm, axis=0, keepdims=True)
                gx_ref[rows, :] = dout_ref[rows, :] + r * (dn - nrm * jnp.mean(dn * nrm, axis=-1, keepdims=True))
            gw_ref[...] += gw

        if side is not None:
            @pl.when((i == nm - 1) & (k == nk - 1))
            def _():
                side.last(s_ins, s_outs, s_sem_refs)

    row = pl.BlockSpec((tm, D), lambda i, k: (i, 0))
    vec = pl.BlockSpec((1, D), lambda i, k: (0, 0))
    out = pl.pallas_call(
        body, name="grad_x", grid=(nm, nk),
        in_specs=[pl.BlockSpec((tm, DZ_TILE), _part_spec(0, c, o, 1)) for c, o in zip(counts, offs)] +
                 [pl.BlockSpec((D, DZ_TILE), lambda i, k: (0, k)), row, row, vec] + s_in_specs,
        out_specs=[row, vec] + s_out_specs,
        out_shape=[jax.ShapeDtypeStruct((T, D), F32), jax.ShapeDtypeStruct((1, D), F32)] + s_shapes,
        scratch_shapes=[pltpu.VMEM((tm, D), F32)] + s_sems,
        compiler_params=_params(("arbitrary", "arbitrary")),
    )(*parts, w_in, x, dout, norm_w, *s_arrays)
    return out[0], out[1], out[2:]


def _rmsnorm_fwd(x, w):
    tm = 256

    def body(x_ref, w_ref, h_ref):
        xv = x_ref[...]
        r = lax.rsqrt(jnp.mean(xv * xv, axis=-1, keepdims=True) + EPS)
        h_ref[...] = (xv * r * w_ref[...]).astype(BF)

    return pl.pallas_call(
        body, name="rmsnorm_fwd", grid=(T // tm,),
        in_specs=[pl.BlockSpec((tm, D), lambda i: (i, 0)), pl.BlockSpec((1, D), lambda i: (0, 0))],
        out_specs=pl.BlockSpec((tm, D), lambda i: (i, 0)),
        out_shape=jax.ShapeDtypeStruct((T, D), BF),
        compiler_params=_params(("parallel",)),
    )(x, w)


def _lower_bound(lbl):
    mx = jnp.max(lbl, axis=0, keepdims=True)
    e = jnp.exp(lbl - mx)
    return e[0:1] / jnp.sum(e, axis=0, keepdims=True)


def _cumsum_rows(g, rows):
    b = g
    sh = 1
    while sh < CH:
        b = b + jnp.where(rows >= sh, pltpu.roll(b, sh, axis=0), 0.0)
        sh *= 2
    return b


def _rev_cumsum_rows(g, rows):
    b = g
    sh = 1
    while sh < CH:
        b = b + jnp.where(rows < CH - sh, pltpu.roll(b, CH - sh, axis=0), 0.0)
        sh *= 2
    return b


SUB = CH // 2


def _direct_block(qb, kb, vb, bb, rows8):
    ob = jnp.zeros_like(qb)
    for s in range(SUB):
        e_s = jnp.exp(jnp.where(rows8 >= s, bb - bb[s:s + 1], -jnp.inf))
        ob = ob + jnp.sum(qb * e_s * kb[s:s + 1], axis=1, keepdims=True) * vb[s:s + 1]
    return ob


def _direct_block_bwd(qb, kb, vb, bb, dob, rows8, rowc8):
    dq = dk = dv = db = jnp.zeros_like(qb)
    for s in range(SUB):
        one = (rowc8 == s).astype(F32)
        ks, vs = kb[s:s + 1], vb[s:s + 1]
        e_s = jnp.exp(jnp.where(rows8 >= s, bb - bb[s:s + 1], -jnp.inf))
        qes = qb * e_s
        w = qes * ks
        a = jnp.sum(w, axis=1, keepdims=True)
        da = jnp.sum(dob * vs, axis=1, keepdims=True)
        dv = dv + one * jnp.sum(a * dob, axis=0, keepdims=True)
        dq = dq + da * e_s * ks
        dk = dk + one * jnp.sum(da * qes, axis=0, keepdims=True)
        u = da * w
        db = db + u - one * jnp.sum(u, axis=0, keepdims=True)
    return dq, dk, dv, db


def _cross_factors(q, k, b):
    ref = b[SUB - 1:SUB]
    e_hi, e_lo = jnp.exp(b[SUB:] - ref), jnp.exp(ref - b[:SUB])
    return q[SUB:] * e_hi, k[:SUB] * e_lo, e_hi, e_lo


def _intra_fwd(q, k, v, b, rows8):
    lo = _direct_block(q[:SUB], k[:SUB], v[:SUB], b[:SUB], rows8)
    hi = _direct_block(q[SUB:], k[SUB:], v[SUB:], b[SUB:], rows8)
    qe_hi, ke_lo, _, _ = _cross_factors(q, k, b)
    for s in range(SUB):
        hi = hi + jnp.sum(qe_hi * ke_lo[s:s + 1], axis=1, keepdims=True) * v[s:s + 1]
    return jnp.concatenate([lo, hi], axis=0)


def _intra_bwd(q, k, v, b, do, rows8, rowc8):
    dq_lo, dk_lo, dv_lo, db_lo = _direct_block_bwd(q[:SUB], k[:SUB], v[:SUB], b[:SUB], do[:SUB], rows8, rowc8)
    dq_hi, dk_hi, dv_hi, db_hi = _direct_block_bwd(q[SUB:], k[SUB:], v[SUB:], b[SUB:], do[SUB:], rows8, rowc8)
    qe_hi, ke_lo, e_hi, e_lo = _cross_factors(q, k, b)
    do_hi, v_lo = do[SUB:], v[:SUB]
    dqe = dke = jnp.zeros_like(qe_hi)
    for s in range(SUB):
        one = (rowc8 == s).astype(F32)
        a = jnp.sum(qe_hi * ke_lo[s:s + 1], axis=1, keepdims=True)
        da = jnp.sum(do_hi * v_lo[s:s + 1], axis=1, keepdims=True)
        dv_lo = dv_lo + one * jnp.sum(a * do_hi, axis=0, keepdims=True)
        dqe = dqe + da * ke_lo[s:s + 1]
        dke = dke + one * jnp.sum(da * qe_hi, axis=0, keepdims=True)
    u_hi, u_lo = dqe * qe_hi, dke * ke_lo
    d_ref = jnp.sum(u_lo, axis=0, keepdims=True) - jnp.sum(u_hi, axis=0, keepdims=True)
    db_lo = db_lo - u_lo + (rowc8 == SUB - 1).astype(F32) * d_ref
    cat = lambda lo, hi: jnp.concatenate([lo, hi], axis=0)
    return (cat(dq_lo, dq_hi + dqe * e_hi), cat(dk_lo + dke * e_lo, dk_hi), cat(dv_lo, dv_hi),
            cat(db_lo, db_hi + u_hi))


def _hgrn_fwd(z, lbl, nw):
    def body(hq_ref, hf_ref, hi_ref, hg_ref, lbl_ref, nw_ref, oraw_ref, og_ref, sh_ref, st_ref):
        @pl.when(pl.program_id(0) == 0)
        def _():
            st_ref[...] = jnp.zeros_like(st_ref)

        lb_all = _lower_bound(lbl_ref[...])
        rows = lax.broadcasted_iota(jnp.int32, (CH, HK), 0)
        rows8 = lax.broadcasted_iota(jnp.int32, (SUB, HK), 0)
        nwv = nw_ref[...]
        for cc, h in [(cc, h) for cc in range(HSTEP) for h in range(HEADS)]:
            rs = slice(CH * cc, CH * (cc + 1))
            sl = slice(HK * h, HK * (h + 1))
            lb = lb_all[:, sl]
            hq, hf, v, hg = hq_ref[rs, sl], hf_ref[rs, sl], hi_ref[rs, sl], hg_ref[rs, sl]
            q = hq * _sigmoid(hq)
            f = lb + (1.0 - lb) * _sigmoid(hf)
            k = 1.0 - f
            b = _cumsum_rows(jnp.log(f), rows)
            sh_ref[cc, h] = st_ref[h]
            o = _bdot(q * jnp.exp(b), st_ref[h], _NT) + _intra_fwd(q, k, v, b, rows8)
            bl = b[CH - 1:CH]
            st_ref[h] = st_ref[h] * jnp.exp(bl)
            st_ref[h] += _bdot(v, k * jnp.exp(bl - b), _TN)
            oraw_ref[rs, sl] = o
            nrm = o * lax.rsqrt(jnp.mean(o * o, axis=1, keepdims=True) + EPS)
            og_ref[rs, sl] = (nrm * nwv * (hg * _sigmoid(hg))).astype(BF)

    zblk = lambda c: pl.BlockSpec((CH * HSTEP, D), lambda i, c=c: (i, c))
    return pl.pallas_call(
        body, name="hgrn_fwd", grid=(NCH // HSTEP,),
        in_specs=[zblk(0), zblk(1), zblk(2), zblk(3),
                  pl.BlockSpec((2, D), lambda i: (0, 0)), pl.BlockSpec((1, HK), lambda i: (0, 0))],
        out_specs=[zblk(0), zblk(0),
                   pl.BlockSpec((HSTEP, HEADS, HK, HK), lambda i: (i, 0, 0, 0))],
        out_shape=[jax.ShapeDtypeStruct((T, D), F32), jax.ShapeDtypeStruct((T, D), BF),
                   jax.ShapeDtypeStruct((NCH, HEADS, HK, HK), F32)],
        scratch_shapes=[pltpu.VMEM((HEADS, HK, HK), F32)],
        compiler_params=_params(("arbitrary",)),
    )(z, z, z, z, lbl, nw)


def _hgrn_bwd(z, lbl, nw, oraw, dog, shist, side=None):
    hstep = 1
    s_arrays, s_in_specs, s_shapes, s_out_specs, s_sems = _side_io(side)
    na, no = len(s_arrays), len(s_shapes)

    def body(*refs):
        hq_ref, hf_ref, hi_ref, hg_ref, lbl_ref, nw_ref, oraw_ref, dog_ref, sh_ref = refs[:9]
        s_ins = refs[9:9 + na]
        dz_ref, dlb_ref, dnw_ref = refs[9 + na:12 + na]
        s_outs = refs[12 + na:12 + na + no]
        dst_ref = refs[12 + na + no]
        s_sem_refs = refs[13 + na + no:]

        @pl.when(pl.program_id(0) == 0)
        def _():
            dst_ref[...] = jnp.zeros_like(dst_ref)
            dlb_ref[...] = jnp.zeros_like(dlb_ref)
            dnw_ref[...] = jnp.zeros_like(dnw_ref)
            if side is not None:
                side.first(s_ins, s_outs, s_sem_refs)

        lb_all = _lower_bound(lbl_ref[...])
        rows = lax.broadcasted_iota(jnp.int32, (CH, HK), 0)
        rowc = lax.broadcasted_iota(jnp.int32, (CH, 1), 0)
        rows8 = lax.broadcasted_iota(jnp.int32, (SUB, HK), 0)
        rowc8 = lax.broadcasted_iota(jnp.int32, (SUB, 1), 0)
        nwv = nw_ref[...]
        dnw = jnp.zeros((1, HK), F32)
        for cc, h in [(cc, h) for cc in reversed(range(hstep)) for h in range(HEADS)]:
            rs = slice(CH * cc, CH * (cc + 1))
            sl = slice(HK * h, HK * (h + 1))
            lb = lb_all[:, sl]
            hq, hf, v, hg = hq_ref[rs, sl], hf_ref[rs, sl], hi_ref[rs, sl], hg_ref[rs, sl]
            o, dg_out = oraw_ref[rs, sl], dog_ref[rs, sl]
            sg = _sigmoid(hg)
            sil = hg * sg
            r = lax.rsqrt(jnp.mean(o * o, axis=1, keepdims=True) + EPS)
            nrm = o * r
            d_hg = dg_out * (nrm * nwv) * (sg * (1.0 + hg * (1.0 - sg)))
            dn = dg_out * nwv * sil
            dnw = dnw + jnp.sum(dg_out * nrm * sil, axis=0, keepdims=True)
            do = r * (dn - nrm * jnp.mean(dn * nrm, axis=1, keepdims=True))
            sq = _sigmoid(hq)
            q = hq * sq
            sig = _sigmoid(hf)
            f = lb + (1.0 - lb) * sig
            k = 1.0 - f
            b = _cumsum_rows(jnp.log(f), rows)
            eb = jnp.exp(b)
            qe = q * eb
            bl = b[CH - 1:CH]
            ebl = jnp.exp(bl)
            kdec = jnp.exp(bl - b)
            ke = k * kdec
            dqe = _bdot(do, sh_ref[cc, h])
            dq = dqe * eb
            db = dqe * qe
            dke = _bdot(v, dst_ref[h])
            dv = _bdot(ke, dst_ref[h], _NT)
            dk = dke * kdec
            rr = dke * ke
            db = db - rr
            db_last = (jnp.sum(rr, axis=0, keepdims=True)
                       + ebl * jnp.sum(dst_ref[h] * sh_ref[cc, h], axis=0, keepdims=True))
            dst_ref[h] = dst_ref[h] * ebl
            dst_ref[h] += _bdot(do, qe, _TN)
            dq_i, dk_i, dv_i, db_i = _intra_bwd(q, k, v, b, do, rows8, rowc8)
            dq, dk, dv = dq + dq_i, dk + dk_i, dv + dv_i
            db = db + db_i + (rowc == CH - 1).astype(F32) * db_last
            dgl = _rev_cumsum_rows(db, rows)
            df = dgl / f - dk
            dlb_ref[:, sl] += jnp.sum(df * (1.0 - sig), axis=0, keepdims=True)
            dz_ref[rs, sl] = (dq * (sq * (1.0 + hq * (1.0 - sq)))).astype(BF)
            dz_ref[rs, D + HK * h:D + HK * (h + 1)] = (df * (1.0 - lb) * sig * (1.0 - sig)).astype(BF)
            dz_ref[rs, 2 * D + HK * h:2 * D + HK * (h + 1)] = dv.astype(BF)
            dz_ref[rs, 3 * D + HK * h:3 * D + HK * (h + 1)] = d_hg.astype(BF)
        dnw_ref[...] += dnw
        if side is not None:
            @pl.when(pl.program_id(0) == NCH // hstep - 1)
            def _():
                side.last(s_ins, s_outs, s_sem_refs)

    rev = lambda i: NCH // hstep - 1 - i
    zblk = lambda c: pl.BlockSpec((CH * hstep, D), lambda i, c=c: (rev(i), c))
    out = pl.pallas_call(
        body, name="hgrn_bwd", grid=(NCH // hstep,),
        in_specs=[zblk(0), zblk(1), zblk(2), zblk(3),
                  pl.BlockSpec((2, D), lambda i: (0, 0)), pl.BlockSpec((1, HK), lambda i: (0, 0)),
                  zblk(0), zblk(0),
                  pl.BlockSpec((hstep, HEADS, HK, HK), lambda i: (rev(i), 0, 0, 0))] + s_in_specs,
        out_specs=[pl.BlockSpec((CH * hstep, 4 * D), lambda i: (rev(i), 0)),
                   pl.BlockSpec((1, D), lambda i: (0, 0)), pl.BlockSpec((1, HK), lambda i: (0, 0))] + s_out_specs,
        out_shape=[jax.ShapeDtypeStruct((T, 4 * D), BF), jax.ShapeDtypeStruct((1, D), F32),
                   jax.ShapeDtypeStruct((1, HK), F32)] + s_shapes,
        scratch_shapes=[pltpu.VMEM((HEADS, HK, HK), F32)] + s_sems,
        compiler_params=_params(("arbitrary",)),
    )(z, z, z, z, lbl, nw, oraw, dog, shist, *s_arrays)
    return out[0], out[1], out[2], out[3:]


BLK = 128
NBLK = T // BLK
QK_SCALE = 0.125


def _head_masks():
    lane = lax.broadcasted_iota(jnp.int32, (1, BLK), 1)
    return [(lane < 64).astype(F32), (lane >= 64).astype(F32)]


def _pieces(dil):
    m = T // dil
    out = []
    for r in range(dil):
        for j in range(m // BLK):
            start = r + dil * BLK * j
            rows = pl.ds(start, BLK, stride=dil) if dil > 1 else pl.ds(start, BLK)
            out.append((rows, r * m + BLK * j))
    return out


def _rope_tables(pos_ref, invf_ref, cos_ref, sa_ref, sb_ref):
    lane = lax.broadcasted_iota(jnp.int32, (256, BLK), 1)
    first = (lane % 64) < 32
    for c in range(T // 256):
        rows = pl.ds(256 * c, 256)
        ang = pos_ref[rows, :].astype(F32) * invf_ref[...]
        s = jnp.sin(ang)
        cos_ref[rows, :] = jnp.cos(ang)
        sa_ref[rows, :] = jnp.where(first, -s, 0.0)
        sb_ref[rows, :] = jnp.where(first, 0.0, s)


def _rope(x, c, sa, sb):
    return x * c + pltpu.roll(x, 96, axis=1) * sa + pltpu.roll(x, 32, axis=1) * sb


def _rope_t(d, c, sa, sb):
    return d * c + pltpu.roll(d * sa, 32, axis=1) + pltpu.roll(d * sb, 96, axis=1)


def _window_bias(bias_ref):
    ii = lax.broadcasted_iota(jnp.int32, (2 * BLK, BLK), 0) % BLK
    jj = lax.broadcasted_iota(jnp.int32, (2 * BLK, BLK), 1)
    bias_ref[0] = jnp.where(jj <= ii, 0.0, -jnp.inf)
    bias_ref[1] = jnp.where(jj >= ii, 0.0, -jnp.inf)


def _blocks(bi):
    if isinstance(bi, int):
        return pl.ds(bi * BLK, BLK), pl.ds(max(bi - 1, 0) * BLK, BLK)
    return (pl.ds(pl.multiple_of(bi * BLK, BLK), BLK),
            pl.ds(pl.multiple_of(jnp.maximum(bi - 1, 0) * BLK, BLK), BLK))


def _stack_heads(x, masks):
    return jnp.concatenate([x * masks[0], x * masks[1]], axis=0).astype(BF)


def _attn_fwd(z, pos, invf):
    def body(q_ref, k_ref, v_ref, ag_ref, pos_ref, invf_ref, ob_ref, opre_ref, lse_ref,
             cos_ref, sa_ref, sb_ref, bias_ref, qr_ref, kr_ref, vr_ref, og_ref, lg_ref, otok_ref, ltok_ref, sc_ref):
        g = pl.program_id(1)
        masks = _head_masks()

        @pl.when(g == 0)
        def _():
            _rope_tables(pos_ref, invf_ref, cos_ref, sa_ref, sb_ref)
            _window_bias(bias_ref)

        def group(gi):
            dil = ATT_GROUPS[gi][1]
            nblk = (T // dil) // BLK
            for rows, dst in _pieces(dil):
                c, sa, sb = cos_ref[rows, :], sa_ref[rows, :], sb_ref[rows, :]
                qr_ref[pl.ds(dst, BLK), :] = _rope(q_ref[rows, :], c, sa, sb) * QK_SCALE
                kr_ref[pl.ds(dst, BLK), :] = _rope(k_ref[rows, :], c, sa, sb)
                vr_ref[pl.ds(dst, BLK), :] = v_ref[rows, :]

            def scores(bi, slot):
                cur, prev = _blocks(bi)
                q2 = _stack_heads(qr_ref[cur, :], masks)
                sc_ref[slot, 0] = _dot(q2, kr_ref[cur, :].astype(BF), _NT) + bias_ref[0]
                if nblk > 1:
                    sc_ref[slot, 1] = (_dot(q2, kr_ref[prev, :].astype(BF), _NT)
                                       + (bias_ref[1] + jnp.where((bi % nblk) != 0, 0.0, -jnp.inf)))

            def finish(bi, slot):
                cur, prev = _blocks(bi)
                s_c, vc = sc_ref[slot, 0], vr_ref[cur, :].astype(BF)
                if nblk > 1:
                    s_p, vp = sc_ref[slot, 1], vr_ref[prev, :].astype(BF)
                    mx = jnp.max(jnp.maximum(s_c, s_p), axis=1, keepdims=True)
                    p_c, p_p = jnp.exp(s_c - mx), jnp.exp(s_p - mx)
                    den = jnp.sum(p_c + p_p, axis=1, keepdims=True)
                    oh = _dot(p_c.astype(BF), vc) + _dot(p_p.astype(BF), vp)
                else:
                    mx = jnp.max(s_c, axis=1, keepdims=True)
                    p_c = jnp.exp(s_c - mx)
                    den = jnp.sum(p_c, axis=1, keepdims=True)
                    oh = _dot(p_c.astype(BF), vc)
                on = oh / den
                lsev = jnp.broadcast_to(mx + jnp.log(den), (2 * BLK, BLK))
                og_ref[cur, :] = on[:BLK] * masks[0] + on[BLK:] * masks[1]
                lg_ref[0, cur, :] = lsev[:BLK]
                lg_ref[1, cur, :] = lsev[BLK:]

            def pair(j, carry):
                finish(2 * j, 0)
                scores(2 * j + 1, 1)
                finish(2 * j + 1, 1)
                scores(jnp.minimum(2 * j + 2, NBLK - 1), 0)
                return carry

            scores(0, 0)
            lax.fori_loop(0, NBLK // 2, pair, 0)
            for rows, src in _pieces(dil):
                srows = pl.ds(src, BLK)
                otok_ref[gi, rows, :] = og_ref[srows, :]
                ltok_ref[gi, 0, rows, :] = lg_ref[0, srows, :]
                ltok_ref[gi, 1, rows, :] = lg_ref[1, srows, :]

        for gi in range(3):
            pl.when(g == gi)(functools.partial(group, gi))

        @pl.when(g == 2)
        def _():
            for c in range(T // BLK):
                rows = pl.ds(BLK * c, BLK)
                wts = []
                for hh in range(2):
                    l0, l1, l2 = ltok_ref[0, hh, rows, :], ltok_ref[1, hh, rows, :], ltok_ref[2, hh, rows, :]
                    mx = jnp.maximum(jnp.maximum(l0, l1), l2)
                    lse = mx + jnp.log(jnp.exp(l0 - mx) + jnp.exp(l1 - mx) + jnp.exp(l2 - mx))
                    lse_ref[rows, BLK * hh:BLK * (hh + 1)] = lse
                    wts.append([jnp.exp(l0 - lse), jnp.exp(l1 - lse), jnp.exp(l2 - lse)])
                o = sum((wts[0][gi] * masks[0] + wts[1][gi] * masks[1]) * otok_ref[gi, rows, :] for gi in range(3))
                ag = ag_ref[rows, :]
                opre_ref[rows, :] = o
                ob_ref[rows, :] = (o * (ag * _sigmoid(ag))).astype(BF)

    c0 = ATT_COL0 // BLK
    zspec = lambda part: pl.BlockSpec((T, BLK), lambda p, g, part=part: (0, c0 + 12 * part + 4 * g + p))
    outspec = pl.BlockSpec((T, BLK), lambda p, g: (0, p))
    big = lambda: pltpu.VMEM((T, BLK), F32)
    return pl.pallas_call(
        body, name="attn_fwd", grid=(4, 3),
        in_specs=[zspec(0), zspec(1), zspec(2),
                  pl.BlockSpec((T, BLK), lambda p, g: (0, AG_COL0 // BLK + p)),
                  pl.BlockSpec((T, 1), lambda p, g: (0, 0)), pl.BlockSpec((1, BLK), lambda p, g: (0, 0))],
        out_specs=[outspec, outspec, pl.BlockSpec((T, 2 * BLK), lambda p, g: (0, p))],
        out_shape=[jax.ShapeDtypeStruct((T, 512), BF), jax.ShapeDtypeStruct((T, 512), F32),
                   jax.ShapeDtypeStruct((T, 8 * BLK), F32)],
        scratch_shapes=[big(), big(), big(), pltpu.VMEM((2, 2 * BLK, BLK), F32), big(), big(), big(), big(),
                        pltpu.VMEM((2, T, BLK), F32), pltpu.VMEM((3, T, BLK), F32), pltpu.VMEM((3, 2, T, BLK), F32),
                        pltpu.VMEM((2, 2, 2 * BLK, BLK), F32)],
        compiler_params=_params(("parallel", "arbitrary")),
    )(z, z, z, z, pos, invf)


def _attn_bwd(z, pos, invf, opre, lse, dob):
    def body(q_ref, k_ref, v_ref, ag_ref, pos_ref, invf_ref, o_ref, lse0_ref, lse1_ref, dob_ref,
             dq_ref, dk_ref, dv_ref, dag_ref,
             cos_ref, sa_ref, sb_ref, bias_ref, dtok_ref, qr_ref, kr_ref, vr_ref, dor_ref, lr_ref, dr_ref,
             dqr_ref, dkr_ref, dvr_ref, pd_ref):
        g = pl.program_id(1)
        masks = _head_masks()

        @pl.when(g == 0)
        def _():
            _rope_tables(pos_ref, invf_ref, cos_ref, sa_ref, sb_ref)
            _window_bias(bias_ref)
            for c in range(T // BLK):
                rows = pl.ds(BLK * c, BLK)
                ag, dob_v, o = ag_ref[rows, :], dob_ref[rows, :], o_ref[rows, :]
                sg = _sigmoid(ag)
                dag_ref[rows, :] = (dob_v * o * (sg * (1.0 + ag * (1.0 - sg)))).astype(BF)
                prod = dob_v * (ag * sg) * o
                for hh, mh in enumerate(masks):
                    dtok_ref[hh, rows, :] = jnp.broadcast_to(jnp.sum(prod * mh, axis=1, keepdims=True), (BLK, BLK))

        def group(gi):
            dil = ATT_GROUPS[gi][1]
            nblk = (T // dil) // BLK
            for rows, dst in _pieces(dil):
                drows = pl.ds(dst, BLK)
                c, sa, sb = cos_ref[rows, :], sa_ref[rows, :], sb_ref[rows, :]
                qr_ref[drows, :] = _rope(q_ref[rows, :], c, sa, sb) * QK_SCALE
                kr_ref[drows, :] = _rope(k_ref[rows, :], c, sa, sb)
                vr_ref[drows, :] = v_ref[rows, :]
                ag = ag_ref[rows, :]
                dor_ref[drows, :] = dob_ref[rows, :] * (ag * _sigmoid(ag))
                for hh, lse_ref in enumerate((lse0_ref, lse1_ref)):
                    lr_ref[hh, drows, :] = lse_ref[rows, :]
                    dr_ref[hh, drows, :] = dtok_ref[hh, rows, :]
            dkr_ref[...] = jnp.zeros_like(dkr_ref)
            dvr_ref[...] = jnp.zeros_like(dvr_ref)

            def probs(bi, slot):
                cur, prev = _blocks(bi)
                q2, do2 = _stack_heads(qr_ref[cur, :], masks), _stack_heads(dor_ref[cur, :], masks)
                lh = jnp.concatenate([lr_ref[0, cur, :], lr_ref[1, cur, :]], axis=0)
                dh = jnp.concatenate([dr_ref[0, cur, :], dr_ref[1, cur, :]], axis=0)
                p_c = jnp.exp(_dot(q2, kr_ref[cur, :].astype(BF), _NT) + bias_ref[0] - lh)
                pd_ref[slot, 0] = p_c.astype(BF)
                pd_ref[slot, 1] = (p_c * (_dot(do2, vr_ref[cur, :].astype(BF), _NT) - dh)).astype(BF)
                if nblk > 1:
                    bias_p = bias_ref[1] + jnp.where((bi % nblk) != 0, 0.0, -jnp.inf)
                    p_p = jnp.exp(_dot(q2, kr_ref[prev, :].astype(BF), _NT) + bias_p - lh)
                    pd_ref[slot, 2] = p_p.astype(BF)
                    pd_ref[slot, 3] = (p_p * (_dot(do2, vr_ref[prev, :].astype(BF), _NT) - dh)).astype(BF)

            def grads(bi, slot):
                cur, prev = _blocks(bi)
                q2, do2 = _stack_heads(qr_ref[cur, :], masks), _stack_heads(dor_ref[cur, :], masks)
                p_c, ds_c = pd_ref[slot, 0], pd_ref[slot, 1]
                dq2 = _dot(ds_c, kr_ref[cur, :].astype(BF))
                dkr_ref[cur, :] += _dot(ds_c, q2, _TN)
                dvr_ref[cur, :] += _dot(p_c, do2, _TN)
                if nblk > 1:
                    p_p, ds_p = pd_ref[slot, 2], pd_ref[slot, 3]
                    dq2 = dq2 + _dot(ds_p, kr_ref[prev, :].astype(BF))
                    dkr_ref[prev, :] += _dot(ds_p, q2, _TN)
                    dvr_ref[prev, :] += _dot(p_p, do2, _TN)
                dqr_ref[cur, :] = dq2[:BLK] * masks[0] + dq2[BLK:] * masks[1]

            def pair(j, carry):
                grads(2 * j, 0)
                probs(2 * j + 1, 1)
                grads(2 * j + 1, 1)
                probs(jnp.minimum(2 * j + 2, NBLK - 1), 0)
                return carry

            probs(0, 0)
            lax.fori_loop(0, NBLK // 2, pair, 0)
            for rows, src in _pieces(dil):
                srows = pl.ds(src, BLK)
                c, sa, sb = cos_ref[rows, :], sa_ref[rows, :], sb_ref[rows, :]
                qr_ref[rows, :] = _rope_t(dqr_ref[srows, :] * QK_SCALE, c, sa, sb)
                kr_ref[rows, :] = _rope_t(dkr_ref[srows, :], c, sa, sb)
                vr_ref[rows, :] = dvr_ref[srows, :]
            for c in range(T // 256):
                rows = pl.ds(256 * c, 256)
                dq_ref[rows, :] = qr_ref[rows, :].astype(BF)
                dk_ref[rows, :] = kr_ref[rows, :].astype(BF)
                dv_ref[rows, :] = vr_ref[rows, :].astype(BF)

        for gi in range(3):
            pl.when(g == gi)(functools.partial(group, gi))

    c0 = ATT_COL0 // BLK
    zspec = lambda part: pl.BlockSpec((T, BLK), lambda p, g, part=part: (0, c0 + 12 * part + 4 * g + p))
    pspec = pl.BlockSpec((T, BLK), lambda p, g: (0, p))
    gspec = pl.BlockSpec((T, BLK), lambda p, g: (0, 4 * g + p))
    big = lambda: pltpu.VMEM((T, BLK), F32)
    two = lambda: pltpu.VMEM((2, T, BLK), F32)
    return pl.pallas_call(
        body, name="attn_bwd", grid=(4, 3),
        in_specs=[zspec(0), zspec(1), zspec(2),
                  pl.BlockSpec((T, BLK), lambda p, g: (0, AG_COL0 // BLK + p)),
                  pl.BlockSpec((T, 1), lambda p, g: (0, 0)), pl.BlockSpec((1, BLK), lambda p, g: (0, 0)),
                  pspec, pl.BlockSpec((T, BLK), lambda p, g: (0, 2 * p)),
                  pl.BlockSpec((T, BLK), lambda p, g: (0, 2 * p + 1)), pspec],
        out_specs=[gspec, gspec, gspec, pspec],
        out_shape=[jax.ShapeDtypeStruct((T, 1536), BF), jax.ShapeDtypeStruct((T, 1536), BF),
                   jax.ShapeDtypeStruct((T, 1536), BF), jax.ShapeDtypeStruct((T, 512), BF)],
        scratch_shapes=[big(), big(), big(), pltpu.VMEM((2, 2 * BLK, BLK), F32), two(), big(), big(), big(), big(),
                        two(), two(), big(), big(), big(), pltpu.VMEM((2, 4, 2 * BLK, BLK), BF)],
        compiler_params=_params(("parallel", "arbitrary")),
    )(z, z, z, z, pos, invf, opre, lse, lse, dob)


def _merge_fwd(ya, yb, z):
    tm = 256

    def body(ya_ref, yb_ref, ga_ref, gb_ref, m_ref):
        m_ref[...] = (_sigmoid(ga_ref[...]) * ya_ref[...] + _sigmoid(gb_ref[...]) * yb_ref[...]).astype(BF)

    row = pl.BlockSpec((tm, D), lambda i: (i, 0))
    return pl.pallas_call(
        body, name="merge_fwd", grid=(T // tm,),
        in_specs=[row, row, pl.BlockSpec((tm, D), lambda i: (i, GATE_COL0 // D)),
                  pl.BlockSpec((tm, D), lambda i: (i, GATE_COL0 // D + 1))],
        out_specs=row, out_shape=jax.ShapeDtypeStruct((T, D), BF),
        compiler_params=_params(("parallel",)),
    )(ya, yb, z, z)


def _out_loss(merged, w_out, x, tgt, wf):
    tm = 256

    def body(m_ref, w_ref, x_ref, t_ref, wf_ref, dout_ref, loss_ref, gwf_ref):
        @pl.when(pl.program_id(0) == 0)
        def _():
            loss_ref[...] = jnp.zeros_like(loss_ref)
            gwf_ref[...] = jnp.zeros_like(gwf_ref)

        out = x_ref[...] + _dot(m_ref[...], w_ref[...])
        r = lax.rsqrt(jnp.mean(out * out, axis=-1, keepdims=True) + EPS)
        yh = out * r
        wfv = wf_ref[...]
        err = yh * wfv - t_ref[...]
        loss_ref[...] += jnp.sum(err * err, axis=0, keepdims=True) * (0.5 / D)
        dy = err * (1.0 / D)
        gwf_ref[...] += jnp.sum(dy * yh, axis=0, keepdims=True)
        dyh = dy * wfv
        dout_ref[...] = r * (dyh - yh * jnp.mean(dyh * yh, axis=-1, keepdims=True))

    row = pl.BlockSpec((tm, D), lambda i: (i, 0))
    vec = pl.BlockSpec((1, D), lambda i: (0, 0))
    return pl.pallas_call(
        body, name="out_loss", grid=(T // tm,),
        in_specs=[row, pl.BlockSpec((D, D), lambda i: (0, 0)), row, row, vec],
        out_specs=[row, vec, vec],
        out_shape=[jax.ShapeDtypeStruct((T, D), F32), jax.ShapeDtypeStruct((1, D), F32),
                   jax.ShapeDtypeStruct((1, D), F32)],
        compiler_params=_params(("arbitrary",)),
    )(merged, w_out, x, tgt, wf)


def _merge_bwd(dm, ya, yb, z):
    tm = 256

    def body(dm_ref, ya_ref, yb_ref, ga_ref, gb_ref, dya_ref, dyb_ref, dg_ref):
        dmv = dm_ref[...]
        sa, sb = _sigmoid(ga_ref[...]), _sigmoid(gb_ref[...])
        dya_ref[...] = (sa * dmv).astype(BF)
        dyb_ref[...] = (sb * dmv).astype(BF)
        dg_ref[:, :D] = (dmv * ya_ref[...] * sa * (1.0 - sa)).astype(BF)
        dg_ref[:, D:] = (dmv * yb_ref[...] * sb * (1.0 - sb)).astype(BF)

    row = pl.BlockSpec((tm, D), lambda i: (i, 0))
    return pl.pallas_call(
        body, name="merge_bwd", grid=(T // tm,),
        in_specs=[row, row, row, pl.BlockSpec((tm, D), lambda i: (i, GATE_COL0 // D)),
                  pl.BlockSpec((tm, D), lambda i: (i, GATE_COL0 // D + 1))],
        out_specs=[row, row, pl.BlockSpec((tm, 2 * D), lambda i: (i, 0))],
        out_shape=[jax.ShapeDtypeStruct((T, D), BF), jax.ShapeDtypeStruct((T, D), BF),
                   jax.ShapeDtypeStruct((T, 2 * D), BF)],
        compiler_params=_params(("parallel",)),
    )(dm, ya, yb, z, z)


def _rope_inv_freq():
    inv = ROPE_THETA ** (-jnp.arange(0, 64, 2, dtype=F32) / 64)
    return jnp.tile(inv, 4).reshape(1, BLK)


def _local_step(x, pos, norm_w, lbl, hnw, wf, tgt, w_in, w_a, w_b, w_out, shard_shapes=()):
    invf = _rope_inv_freq()
    h = _rmsnorm_fwd(x, norm_w)
    if shard_shapes:
        z, (w_a, w_b, w_out) = _matmul(h, w_in, tm=1024, tn=512, name="z_proj",
                                       side=_gather_side([w_a, w_b, w_out], WEIGHT_AXES[1:]))
    else:
        z = _matmul(h, w_in, tm=1024, tn=512, name="z_proj")
    oraw, og, shist = _hgrn_fwd(z, lbl, hnw)
    ob, opre, lse = _attn_fwd(z, pos, invf)
    ya = _matmul(og, w_a, tm=1024, tn=512, name="ya_proj")
    yb = _matmul(ob, w_b, tm=1024, tn=512, name="yb_proj")
    merged = _merge_fwd(ya, yb, z)
    dout, loss_vec, g_wf = _out_loss(merged, w_out, x, tgt, wf)

    dm = _matmul(dout, w_out, tb=True, tm=1024, tn=512, name="d_merged")
    g_wout = _matmul(merged, dout, ta=True, out_dtype=BF, tm=512, tn=1024, name="g_wout")
    dya, dyb, dgates = _merge_bwd(dm, ya, yb, z)
    dog = _matmul(dya, w_a, tb=True, tm=1024, tn=512, name="d_og")
    g_wa = _matmul(og, dya, ta=True, out_dtype=BF, tm=512, tn=1024, name="g_wa")
    dob = _matmul(dyb, w_b, tb=True, tm=1024, tn=512, name="d_ob")
    g_wb = _matmul(ob, dyb, ta=True, out_dtype=BF, tm=512, tn=1024, name="g_wb")
    small = [g_wa, g_wb, g_wout]
    side_s = side_w = None
    if shard_shapes:
        p3_s = _rs_partials(small, shard_shapes[1:], WEIGHT_AXES[1:], "small")
        side_s = _chip_exchange_side(p3_s, shard_shapes[1:], WEIGHT_AXES[1:])
    dz_h, dlb, g_hnw, land_s = _hgrn_bwd(z, lbl, hnw, oraw, dog, shist, side=side_s)
    dq, dk, dv, dag = _attn_bwd(z, pos, invf, opre, lse, dob)
    dz_parts = [dz_h, dq, dk, dv, dag, dgates]
    g_big = [_grad_w_in(h, dz_parts)] + small
    if shard_shapes:
        p3_w = _rs_partials(g_big[:1], shard_shapes[:1], WEIGHT_AXES[:1], "w_in")
        side_w = _chip_exchange_side(p3_w, shard_shapes[:1], WEIGHT_AXES[:1])
    gx, g_nw, land_w = _grad_x(dz_parts, w_in, x, dout, norm_w, side=side_w)
    if shard_shapes:
        g_big = _rs_finish(p3_w + p3_s, list(land_w) + list(land_s), shard_shapes, WEIGHT_AXES)
    return dict(loss_vec=loss_vec, gx=gx, g_nw=g_nw, dlb=dlb, g_hnw=g_hnw, g_wf=g_wf,
                g_win=g_big[0], g_wa=g_big[1], g_wb=g_big[2], g_wout=g_big[3])


MESH = pl.DeviceIdType.MESH
HBM = pl.BlockSpec(memory_space=pl.ANY)
WEIGHT_AXES = (1, 0, 1, 0)


def _place():
    x, y, c = lax.axis_index("x"), lax.axis_index("y"), lax.axis_index("c")
    chips = [(1 - x, y), (x, 1 - y), (1 - x, 1 - y)]
    return x, y, c, chips


def _block_half(ref, shard_shape, axis, j, half):
    r, c = shard_shape
    hr = r // 2
    if axis == 0:
        return ref.at[pl.ds(pl.multiple_of(j * r + half * hr, 16), hr), :]
    return ref.at[pl.ds(pl.multiple_of(half * hr, 16), hr), pl.ds(pl.multiple_of(j * c, 128), c)]


class _Side:
    def __init__(self, arrays, out_shapes, sems, first, last):
        self.arrays, self.out_shapes, self.sems, self.first, self.last = arrays, out_shapes, sems, first, last


def _gather_side(shards, axes):
    n = len(shards)
    shapes = [s.shape for s in shards]

    def copies(ins, outs, sems):
        send1, recv1, send2, recv2, send0, recv0 = sems
        x, y, c, chips = _place()
        me = 2 * x + y
        sib = (x, y, 1 - c)
        near = ((1 - c) * (1 - x) + c * x, (1 - c) * y + c * (1 - y))
        far = ((1 - c) * x + c * (1 - x), (1 - c) * (1 - y) + c * y)
        out = []
        for a in range(n):
            r, cc = shapes[a]
            mine = (outs[a].at[pl.ds(pl.multiple_of(me * r, 16), r), :] if axes[a] == 0
                    else outs[a].at[:, pl.ds(pl.multiple_of(me * cc, 128), cc)])
            own = pltpu.make_async_remote_copy(
                src_ref=ins[a], dst_ref=mine, send_sem=send0.at[a], recv_sem=recv0.at[a],
                device_id=sib, device_id_type=MESH)
            src = ins[a].at[pl.ds(pl.multiple_of(c * (r // 2), 16), r // 2), :]
            sends = [pltpu.make_async_remote_copy(
                src_ref=src, dst_ref=_block_half(outs[a], shapes[a], axes[a], me, c),
                send_sem=send1.at[a, k], recv_sem=recv1.at[a, k], device_id=(*chips[k], c), device_id_type=MESH)
                for k in range(2)]

            def region(chip, half):
                return _block_half(outs[a], shapes[a], axes[a], 2 * chip[0] + chip[1], half)

            def arrival(chip, k):
                reg = region(chip, c)
                return pltpu.make_async_remote_copy(
                    src_ref=reg, dst_ref=reg, send_sem=send1.at[a, k], recv_sem=recv1.at[a, k],
                    device_id=(*chip, c), device_id_type=MESH)

            def to_sibling(chip, k):
                reg = region(chip, c)
                return pltpu.make_async_remote_copy(
                    src_ref=reg, dst_ref=reg, send_sem=send2.at[a, k], recv_sem=recv2.at[a, k],
                    device_id=sib, device_id_type=MESH)

            def from_sibling(chip, k):
                reg = region(chip, 1 - c)
                return pltpu.make_async_remote_copy(
                    src_ref=reg, dst_ref=reg, send_sem=send2.at[a, k], recv_sem=recv2.at[a, k],
                    device_id=sib, device_id_type=MESH)

            relay = pltpu.make_async_remote_copy(
                src_ref=region(near, c), dst_ref=region(near, c), send_sem=send1.at[a, 2], recv_sem=recv1.at[a, 2],
                device_id=(*far, c), device_id_type=MESH)
            hops = [(arrival(near, c), to_sibling(near, c)), (arrival(far, 1 - c), to_sibling(far, 1 - c)),
                    (arrival(chips[2], 2), to_sibling(chips[2], 2))]
            back = [from_sibling(chips[k], k) for k in range(3)]
            out.append((own, sends, relay, hops, back))
        return out

    def first(ins, outs, sems):
        for own, sends, _, _, _ in copies(ins, outs, sems):
            own.start()
            for cp in sends:
                cp.start()

    def last(ins, outs, sems):
        per_array = copies(ins, outs, sems)
        for step in range(3):
            for _, _, relay, hops, _ in per_array:
                arrived, onward = hops[step]
                arrived.wait_recv()
                if step == 0:
                    relay.start()
                onward.start()
        for own, sends, relay, hops, back in per_array:
            for cp in back:
                cp.wait_recv()
            for cp in sends + [relay] + [onward for _, onward in hops]:
                cp.wait_send()
            own.wait()

    full = [(4 * r, c) if ax == 0 else (r, 4 * c) for (r, c), ax in zip(shapes, axes)]
    sems = [pltpu.SemaphoreType.DMA((n, 3)), pltpu.SemaphoreType.DMA((n, 3)),
            pltpu.SemaphoreType.DMA((n, 3)), pltpu.SemaphoreType.DMA((n, 3)),
            pltpu.SemaphoreType.DMA((n,)), pltpu.SemaphoreType.DMA((n,))]
    return _Side(list(shards), [jax.ShapeDtypeStruct(f, BF) for f in full], sems, first, last)


def _run_side(side, name):
    na, no = len(side.arrays), len(side.out_shapes)

    def body(*refs):
        ins, outs, sems = refs[:na], refs[na:na + no], refs[na + no:]
        side.first(ins, outs, sems)
        side.last(ins, outs, sems)

    return pl.pallas_call(
        body, name=name, in_specs=[HBM] * na, out_specs=[HBM] * no,
        out_shape=side.out_shapes, scratch_shapes=side.sems,
    )(*side.arrays)


def _as3d(g, shard_shape, axis):
    r, c = shard_shape
    return g.reshape(4, r, c) if axis == 0 else g.reshape(1, r, 4 * c)


def _half_rows(ref3, hr, half):
    return ref3.at[:, pl.ds(pl.multiple_of(half * hr, 16), hr), :]


def _rs_pair_exchange(g3s, name):
    n = len(g3s)

    def body(*refs):
        ins, outs = refs[:n], refs[n:2 * n]
        send, recv = refs[2 * n:]
        x, y, c, _ = _place()
        cps = []
        for a in range(n):
            hr = g3s[a].shape[1] // 2
            cp = pltpu.make_async_remote_copy(
                src_ref=_half_rows(ins[a], hr, 1 - c), dst_ref=outs[a],
                send_sem=send.at[a], recv_sem=recv.at[a], device_id=(x, y, 1 - c), device_id_type=MESH)
            cp.start()
            cps.append(cp)
        for cp in cps:
            cp.wait()

    return pl.pallas_call(
        body, name=name,
        in_specs=[HBM] * n, out_specs=[HBM] * n,
        out_shape=[jax.ShapeDtypeStruct((g.shape[0], g.shape[1] // 2, g.shape[2]), BF) for g in g3s],
        scratch_shapes=[pltpu.SemaphoreType.DMA((n,)), pltpu.SemaphoreType.DMA((n,))],
    )(*g3s)


def _pair_sum(g3, land, cidx, name):
    nb, r, w = g3.shape
    hr = r // 2
    tr = 64

    def body(c_ref, g_ref, l_ref, o_ref):
        o_ref[...] = (g_ref[...].astype(F32) + l_ref[...].astype(F32)).astype(BF)

    blk = (nb, tr, w)
    return pl.pallas_call(
        body, name=name,
        grid_spec=pltpu.PrefetchScalarGridSpec(
            num_scalar_prefetch=1, grid=(hr // tr,),
            in_specs=[pl.BlockSpec(blk, lambda i, c: (0, c[0] * (hr // tr) + i, 0)),
                      pl.BlockSpec(blk, lambda i, c: (0, i, 0))],
            out_specs=pl.BlockSpec(blk, lambda i, c: (0, i, 0))),
        out_shape=jax.ShapeDtypeStruct((nb, hr, w), BF),
        compiler_params=_params(("parallel",)),
    )(cidx, g3, land)


def _chip_exchange_side(p3s, shapes, axes):
    n = len(p3s)

    def copies(ins, outs, sems):
        send, recv = sems
        x, y, c, chips = _place()
        cps = []
        for a in range(n):
            r, cc = shapes[a]
            for k, (px, py) in enumerate(chips):
                j = 2 * px + py
                src = ins[a].at[j] if axes[a] == 0 else ins[a].at[0, :, pl.ds(pl.multiple_of(j * cc, 128), cc)]
                cps.append(pltpu.make_async_remote_copy(
                    src_ref=src, dst_ref=outs[a].at[k], send_sem=send.at[a, k], recv_sem=recv.at[a, k],
                    device_id=(px, py, c), device_id_type=MESH))
        return cps

    def first(ins, outs, sems):
        for cp in copies(ins, outs, sems):
            cp.start()

    def last(ins, outs, sems):
        for cp in copies(ins, outs, sems):
            cp.wait()

    return _Side(list(p3s), [jax.ShapeDtypeStruct((3, r // 2, c), BF) for r, c in shapes],
                 [pltpu.SemaphoreType.DMA((n, 3)), pltpu.SemaphoreType.DMA((n, 3))], first, last)


def _chip_sum(p3, land, shard_shape, axis, idx, name):
    r, c = shard_shape
    hr = r // 2
    tr = 64
    nt = hr // tr

    def body(idx_ref, p_ref, l_ref, o_ref):
        acc = p_ref[...].astype(F32)
        for k in range(3):
            acc = acc + l_ref[k].astype(F32)
        o_ref[...] = acc

    own = (pl.BlockSpec((None, tr, c), lambda i, idx: (idx[0], i, 0)) if axis == 0
           else pl.BlockSpec((None, tr, c), lambda i, idx: (0, i, idx[0])))
    return pl.pallas_call(
        body, name=name,
        grid_spec=pltpu.PrefetchScalarGridSpec(
            num_scalar_prefetch=1, grid=(nt,),
            in_specs=[own, pl.BlockSpec((3, tr, c), lambda i, idx: (0, i, 0))],
            out_specs=pl.BlockSpec((tr, c), lambda i, idx: (idx[1] * nt + i, 0))),
        out_shape=jax.ShapeDtypeStruct((r, c), F32),
        compiler_params=_params(("parallel",)),
    )(idx, p3, land)


def _rs_pair_gather(fulls):
    n = len(fulls)

    def body(*refs):
        ins, outs = refs[:n], refs[n:2 * n]
        send, recv = refs[2 * n:]
        x, y, c, _ = _place()
        cps = []
        for a in range(n):
            hr = fulls[a].shape[0] // 2
            rows = pl.ds(pl.multiple_of(c * hr, 8), hr)
            cp = pltpu.make_async_remote_copy(
                src_ref=ins[a].at[rows, :], dst_ref=outs[a].at[rows, :], send_sem=send.at[a], recv_sem=recv.at[a],
                device_id=(x, y, 1 - c), device_id_type=MESH)
            cp.start()
            cps.append(cp)
        for a, cp in enumerate(cps):
            cp.wait_send()
            hr = fulls[a].shape[0] // 2
            other = pl.ds(pl.multiple_of((1 - c) * hr, 8), hr)
            pltpu.make_async_remote_copy(
                src_ref=ins[a].at[other, :], dst_ref=outs[a].at[other, :], send_sem=send.at[a], recv_sem=recv.at[a],
                device_id=(x, y, 1 - c), device_id_type=MESH).wait_recv()

    return pl.pallas_call(
        body, name="grads_pair_gather",
        in_specs=[HBM] * n, out_specs=[HBM] * n,
        out_shape=[jax.ShapeDtypeStruct(f.shape, F32) for f in fulls],
        input_output_aliases={a: a for a in range(n)},
        scratch_shapes=[pltpu.SemaphoreType.DMA((n,)), pltpu.SemaphoreType.DMA((n,))],
    )(*fulls)


def _rs_partials(grads, shapes, axes, tag):
    cidx = jnp.reshape(lax.axis_index("c"), (1,)).astype(jnp.int32)
    g3s = [_as3d(g, s, ax) for g, s, ax in zip(grads, shapes, axes)]
    lands = _rs_pair_exchange(g3s, f"grads_pair_exchange_{tag}")
    return [_pair_sum(g3, l, cidx, f"pair_sum_{tag}_{a}") for a, (g3, l) in enumerate(zip(g3s, lands))]


def _rs_finish(p3s, landed, shapes, axes):
    x, y, c = lax.axis_index("x"), lax.axis_index("y"), lax.axis_index("c")
    idx = jnp.stack([2 * x + y, c]).astype(jnp.int32)
    fulls = [_chip_sum(p3, l2, s, ax, idx, f"chip_sum_{a}")
             for a, (p3, l2, s, ax) in enumerate(zip(p3s, landed, shapes, axes))]
    return _rs_pair_gather(fulls)


NSMALL = 8


def _small_all_reduce(g_nw, dlb, g_hnw, g_wf, loss_vec):
    def body(nw_ref, lb_ref, hn_ref, wf_ref, ls_ref, out_ref, pack_ref, buf_ref, send, recv):
        x, y, c = lax.axis_index("x"), lax.axis_index("y"), lax.axis_index("c")
        me = 4 * x + 2 * y + c
        pack_ref[...] = jnp.zeros_like(pack_ref)
        pack_ref[0:1, :] = nw_ref[...]
        pack_ref[1:2, :] = lb_ref[...]
        pack_ref[2:3, 0:HK] = hn_ref[...]
        pack_ref[3:4, :] = wf_ref[...]
        pack_ref[4:5, :] = ls_ref[...]
        buf_ref[me] = pack_ref[...]
        cps = []
        for d in range(1, 8):
            dx, dy, dc = d >> 2, (d >> 1) & 1, d & 1
            peer = (1 - x if dx else x, 1 - y if dy else y, 1 - c if dc else c)
            cp = pltpu.make_async_remote_copy(
                src_ref=pack_ref, dst_ref=buf_ref.at[me], send_sem=send.at[d - 1], recv_sem=recv.at[d - 1],
                device_id=peer, device_id_type=MESH)
            cp.start()
            cps.append(cp)
        for d in range(1, 8):
            dx, dy, dc = d >> 2, (d >> 1) & 1, d & 1
            src = 4 * (1 - x if dx else x) + 2 * (1 - y if dy else y) + (1 - c if dc else c)
            pltpu.make_async_remote_copy(
                src_ref=pack_ref, dst_ref=buf_ref.at[src], send_sem=send.at[d - 1], recv_sem=recv.at[d - 1],
                device_id=(x, y, c), device_id_type=MESH).wait_recv()
        for cp in cps:
            cp.wait_send()
        acc = buf_ref[0]
        for i in range(1, 8):
            acc = acc + buf_ref[i]
        out_ref[...] = acc

    vm = pl.BlockSpec(memory_space=pltpu.VMEM)
    return pl.pallas_call(
        body, name="small_all_reduce",
        in_specs=[vm] * 5, out_specs=vm,
        out_shape=jax.ShapeDtypeStruct((NSMALL, D), F32),
        scratch_shapes=[pltpu.VMEM((NSMALL, D), F32), pltpu.VMEM((8, NSMALL, D), F32),
                        pltpu.SemaphoreType.DMA((7,)), pltpu.SemaphoreType.DMA((7,))],
    )(g_nw, dlb, g_hnw, g_wf, loss_vec)


def _adamw_math(w, g, m, v):
    m = B1 * m + (1.0 - B1) * g
    v = B2 * v + (1.0 - B2) * (g * g)
    m_hat = m / (1.0 - B1 ** STEP)
    v_hat = v / (1.0 - B2 ** STEP)
    return -LR * (m_hat / (jnp.sqrt(v_hat) + ADAM_EPS) + WD * w), m, v


def _adamw(w, g, m, v, name):
    r, c = w.shape
    tr = 64

    def body(w_ref, g_ref, m_ref, v_ref, d_ref, nm_ref, nv_ref):
        d_ref[...], nm_ref[...], nv_ref[...] = _adamw_math(w_ref[...], g_ref[...], m_ref[...], v_ref[...])

    blk = pl.BlockSpec((tr, c), lambda i: (i, 0))
    return pl.pallas_call(
        body, name=name, grid=(r // tr,), in_specs=[blk] * 4, out_specs=[blk] * 3,
        out_shape=[jax.ShapeDtypeStruct((r, c), F32)] * 3,
        compiler_params=_params(("parallel",)),
    )(w, g, m, v)


def _small_update(red, lbl, params):
    def body(red_ref, *refs):
        ins, outs = refs[:12], refs[12:]
        lb = _lower_bound(ins[3][...])
        dl0 = red_ref[1:2, :] * lb * (1.0 - lb)
        row = lax.broadcasted_iota(jnp.int32, (2, D), 0)
        grads = [red_ref[0:1, :], jnp.where(row == 0, dl0, -dl0), red_ref[2:3, 0:HK], red_ref[3:4, :]]
        for i, g in enumerate(grads):
            w, m, v = ins[3 * i][...], ins[3 * i + 1][...], ins[3 * i + 2][...]
            d, nm, nv = _adamw_math(w, g, m, v)
            outs[4 * i][...] = g
            outs[4 * i + 1][...] = d
            outs[4 * i + 2][...] = nm
            outs[4 * i + 3][...] = nv
        outs[16][...] = jnp.sum(red_ref[4:5, :], axis=1, keepdims=True)

    flat = [a for p in params for a in p]
    vm = pl.BlockSpec(memory_space=pltpu.VMEM)
    shapes = [jax.ShapeDtypeStruct(p[0].shape, F32) for p in params for _ in range(4)]
    return pl.pallas_call(
        body, name="small_update",
        in_specs=[vm] * 13, out_specs=[vm] * 17,
        out_shape=shapes + [jax.ShapeDtypeStruct((1, 1), F32)],
    )(red, *flat)


def kernel(x, positions, norm_w, w_in, lb_logits, hgrn_norm_w, w_branch_a, w_branch_b, w_out, final_norm_w, loss_target, m_norm_w, m_w_in, m_lb_logits, m_hgrn_norm_w, m_w_branch_a, m_w_branch_b, m_w_out, m_final_norm_w, v_norm_w, v_w_in, v_lb_logits, v_hgrn_norm_w, v_w_branch_a, v_w_branch_b, v_w_out, v_final_norm_w):
    big_w = [w_in[0], w_branch_a[0], w_branch_b[0], w_out[0]]
    big_m = [m_w_in[0], m_w_branch_a[0], m_w_branch_b[0], m_w_out[0]]
    big_v = [v_w_in[0], v_w_branch_a[0], v_w_branch_b[0], v_w_out[0]]
    shapes = [w.shape for w in big_w]
    wf = final_norm_w.reshape(1, D)

    shards = [w.astype(BF) for w in big_w]
    w_in_full, = _run_side(_gather_side(shards[:1], WEIGHT_AXES[:1]), "w_in_all_gather")
    loc = _local_step(x[0], positions.reshape(T, 1), norm_w, lb_logits, hgrn_norm_w, wf, loss_target[0],
                      w_in_full, *shards[1:], shard_shapes=shapes)
    g_big = [loc["g_win"], loc["g_wa"], loc["g_wb"], loc["g_wout"]]
    red = _small_all_reduce(loc["g_nw"], loc["dlb"], loc["g_hnw"], loc["g_wf"], loc["loss_vec"])

    small = _small_update(red, lb_logits, [
        (norm_w, m_norm_w, v_norm_w), (lb_logits, m_lb_logits, v_lb_logits),
        (hgrn_norm_w, m_hgrn_norm_w, v_hgrn_norm_w),
        (wf, m_final_norm_w.reshape(1, D), v_final_norm_w.reshape(1, D))])
    loss = small[16].reshape(())
    sg, sd, sm, sv = ([small[4 * i + j] for i in range(4)] for j in range(4))
    for lst in (sg, sd, sm, sv):
        lst[3] = lst[3].reshape(D)
    upd = [_adamw(w, g, m, v, f"adamw_{a}") for a, (w, g, m, v) in enumerate(zip(big_w, g_big, big_m, big_v))]
    bg = [g[None] for g in g_big]
    bd, bm, bv = ([u[j][None] for u in upd] for j in range(3))

    def order(s, b):
        return [s[0], b[0], s[1], s[2], b[1], b[2], b[3], s[3]]

    return (loss, loc["gx"][None], *order(sg, bg), *order(sd, bd), *order(sm, bm), *order(sv, bv))
```

```python
import functools

import jax
import jax.numpy as jnp
from jax import lax
from jax.experimental import pallas as pl
from jax.experimental.pallas import tpu as pltpu

T = 2048
D = 1024
NIN = 11264
HEADS = 8
HK = 128
CH = 16
NCH = T // CH
HSTEP = 2
ATT_GROUPS = ((128, 1), (512, 4), (2048, 16))
ATT_COL0 = 4096
AG_COL0 = 8704
GATE_COL0 = 9216
EPS = 1e-6
ROPE_THETA = 10000.0
LR, B1, B2, ADAM_EPS, WD, STEP = 0.001, 0.9, 0.999, 1e-08, 0.01, 10

F32 = jnp.float32
BF = jnp.bfloat16
VMEM_LIMIT = 56 * 1024 * 1024

_NN = (((1,), (0,)), ((), ()))
_NT = (((1,), (1,)), ((), ()))
_TN = (((0,), (0,)), ((), ()))


def _dot(a, b, dims=_NN):
    return lax.dot_general(a, b, dims, preferred_element_type=F32)


def _bdot(a, b, dims=_NN):
    return lax.dot_general(a.astype(BF), b.astype(BF), dims, preferred_element_type=F32)


def _sigmoid(x):
    return jax.nn.sigmoid(x)


def _params(sem=None):
    return pltpu.CompilerParams(dimension_semantics=sem, vmem_limit_bytes=VMEM_LIMIT)


def _matmul(a, b, *, ta=False, tb=False, out_dtype=F32, tm=512, tn=512, tk=None, name, side=None):
    m = a.shape[1] if ta else a.shape[0]
    kdim = a.shape[0] if ta else a.shape[1]
    n = b.shape[0] if tb else b.shape[1]
    tk = tk or kdim
    tm, tn = min(tm, m), min(tn, n)
    nm, nn, nk = m // tm, n // tn, kdim // tk
    dims = (((0 if ta else 1,), (1 if tb else 0,)), ((), ()))
    s_arrays, s_in_specs, s_shapes, s_out_specs, s_sems = _side_io(side)
    na, no = len(s_arrays), len(s_shapes)
    nacc = 1 if nk > 1 else 0

    def body(*refs):
        a_ref, b_ref = refs[:2]
        s_ins, o_ref, s_outs = refs[2:2 + na], refs[2 + na], refs[3 + na:3 + na + no]
        scratch = refs[3 + na + no:]
        s_sem_refs = scratch[nacc:]
        i, j, k = pl.program_id(0), pl.program_id(1), pl.program_id(2)
        if side is not None:
            @pl.when((i == 0) & (j == 0) & (k == 0))
            def _():
                side.first(s_ins, s_outs, s_sem_refs)

        prod = _bdot(a_ref[...], b_ref[...], dims)
        if nk == 1:
            o_ref[...] = prod.astype(out_dtype)
        else:
            acc = scratch[0]

            @pl.when(k == 0)
            def _():
                acc[...] = prod

            @pl.when(k > 0)
            def _():
                acc[...] += prod

            @pl.when(k == nk - 1)
            def _():
                o_ref[...] = acc[...].astype(out_dtype)

        if side is not None:
            @pl.when((i == nm - 1) & (j == nn - 1) & (k == nk - 1))
            def _():
                side.last(s_ins, s_outs, s_sem_refs)

    a_spec = pl.BlockSpec((tk, tm), lambda i, j, k: (k, i)) if ta else pl.BlockSpec((tm, tk), lambda i, j, k: (i, k))
    b_spec = pl.BlockSpec((tn, tk), lambda i, j, k: (j, k)) if tb else pl.BlockSpec((tk, tn), lambda i, j, k: (k, j))
    sem = ("parallel", "parallel", "arbitrary") if side is None else ("arbitrary",) * 3
    out = pl.pallas_call(
        body, name=name, grid=(nm, nn, nk),
        in_specs=[a_spec, b_spec] + s_in_specs,
        out_specs=[pl.BlockSpec((tm, tn), lambda i, j, k: (i, j))] + s_out_specs,
        out_shape=[jax.ShapeDtypeStruct((m, n), out_dtype)] + s_shapes,
        scratch_shapes=([pltpu.VMEM((tm, tn), F32)] if nk > 1 else []) + s_sems,
        compiler_params=_params(sem),
    )(a, b, *s_arrays)
    return out[0] if side is None else (out[0], out[1:])


DZ_TILE = 512


def _part_offsets(parts):
    counts = [p.shape[1] // DZ_TILE for p in parts]
    offs = [sum(counts[:i]) for i in range(len(parts))]
    return counts, offs


def _part_spec(rows, cnt, off, tile_axis):
    def index(*g):
        return (0 if rows is None else g[0], jnp.clip(g[tile_axis] - off, 0, cnt - 1))
    return index


def _grad_w_in(h, parts):
    counts, offs = _part_offsets(parts)
    n = len(parts)

    def body(h_ref, *refs):
        o_ref = refs[n]
        j = pl.program_id(0)
        for p_ref, cnt, off in zip(refs[:n], counts, offs):
            @pl.when((j >= off) & (j < off + cnt))
            def _(p_ref=p_ref):
                o_ref[...] = _bdot(h_ref[...], p_ref[...], _TN).astype(BF)

    return pl.pallas_call(
        body, name="g_win", grid=(sum(counts),),
        in_specs=[pl.BlockSpec((T, D), lambda j: (0, 0))] +
                 [pl.BlockSpec((T, DZ_TILE), _part_spec(None, c, o, 0)) for c, o in zip(counts, offs)],
        out_specs=pl.BlockSpec((D, DZ_TILE), lambda j: (0, j)),
        out_shape=jax.ShapeDtypeStruct((D, NIN), BF),
        compiler_params=_params(("parallel",)),
    )(h, *parts)


def _side_io(side):
    if side is None:
        return [], [], [], [], []
    return (side.arrays, [HBM] * len(side.arrays), side.out_shapes, [HBM] * len(side.out_shapes), side.sems)


def _grad_x(parts, w_in, x, dout, norm_w, side=None):
    counts, offs = _part_offsets(parts)
    n = len(parts)
    tm = 1024
    nm, nk = T // tm, sum(counts)
    s_arrays, s_in_specs, s_shapes, s_out_specs, s_sems = _side_io(side)
    na, no = len(s_arrays), len(s_shapes)

    def body(*refs):
        w_ref, x_ref, dout_ref, nw_ref = refs[n:n + 4]
        s_ins = refs[n + 4:n + 4 + na]
        gx_ref, gw_ref = refs[n + 4 + na:n + 6 + na]
        s_outs = refs[n + 6 + na:n + 6 + na + no]
        acc = refs[n + 6 + na + no]
        s_sem_refs = refs[n + 7 + na + no:]
        i, k = pl.program_id(0), pl.program_id(1)

        @pl.when((i == 0) & (k == 0))
        def _():
            gw_ref[...] = jnp.zeros_like(gw_ref)
            if side is not None:
                side.first(s_ins, s_outs, s_sem_refs)

        @pl.when(k == 0)
        def _():
            acc[...] = jnp.zeros_like(acc)

        for p_ref, cnt, off in zip(refs[:n], counts, offs):
            @pl.when((k >= off) & (k < off + cnt))
            def _(p_ref=p_ref):
                acc[...] += _bdot(p_ref[...], w_ref[...], _NT)

        @pl.when(k == nk - 1)
        def _():
            gw = jnp.zeros((1, D), F32)
            for c in range(tm // BLK):
                rows = pl.ds(BLK * c, BLK)
                xv, dhv = x_ref[rows, :], acc[rows, :]
                r = lax.rsqrt(jnp.mean(xv * xv, axis=-1, keepdims=True) + EPS)
                nrm = xv * r
                dn = dhv * nw_ref[...]
                gw = gw + jnp.sum(dhv * nrm, axis=0, keepdims=True)
                gx_ref[rows, :] = dout_ref[rows, :] + r * (dn - nrm * jnp.mean(dn * nrm, axis=-1, keepdims=True))
            gw_ref[...] += gw

        if side is not None:
            @pl.when((i == nm - 1) & (k == nk - 1))
            def _():
                side.last(s_ins, s_outs, s_sem_refs)

    row = pl.BlockSpec((tm, D), lambda i, k: (i, 0))
    vec = pl.BlockSpec((1, D), lambda i, k: (0, 0))
    out = pl.pallas_call(
        body, name="grad_x", grid=(nm, nk),
        in_specs=[pl.BlockSpec((tm, DZ_TILE), _part_spec(0, c, o, 1)) for c, o in zip(counts, offs)] +
                 [pl.BlockSpec((D, DZ_TILE), lambda i, k: (0, k)), row, row, vec] + s_in_specs,
        out_specs=[row, vec] + s_out_specs,
        out_shape=[jax.ShapeDtypeStruct((T, D), F32), jax.ShapeDtypeStruct((1, D), F32)] + s_shapes,
        scratch_shapes=[pltpu.VMEM((tm, D), F32)] + s_sems,
        compiler_params=_params(("arbitrary", "arbitrary")),
    )(*parts, w_in, x, dout, norm_w, *s_arrays)
    return out[0], out[1], out[2:]


def _rmsnorm_fwd(x, w):
    tm = 256

    def body(x_ref, w_ref, h_ref):
        xv = x_ref[...]
        r = lax.rsqrt(jnp.mean(xv * xv, axis=-1, keepdims=True) + EPS)
        h_ref[...] = (xv * r * w_ref[...]).astype(BF)

    return pl.pallas_call(
        body, name="rmsnorm_fwd", grid=(T // tm,),
        in_specs=[pl.BlockSpec((tm, D), lambda i: (i, 0)), pl.BlockSpec((1, D), lambda i: (0, 0))],
        out_specs=pl.BlockSpec((tm, D), lambda i: (i, 0)),
        out_shape=jax.ShapeDtypeStruct((T, D), BF),
        compiler_params=_params(("parallel",)),
    )(x, w)


def _lower_bound(lbl):
    mx = jnp.max(lbl, axis=0, keepdims=True)
    e = jnp.exp(lbl - mx)
    return e[0:1] / jnp.sum(e, axis=0, keepdims=True)


def _cumsum_rows(g, rows):
    b = g
    sh = 1
    while sh < CH:
        b = b + jnp.where(rows >= sh, pltpu.roll(b, sh, axis=0), 0.0)
        sh *= 2
    return b


def _rev_cumsum_rows(g, rows):
    b = g
    sh = 1
    while sh < CH:
        b = b + jnp.where(rows < CH - sh, pltpu.roll(b, CH - sh, axis=0), 0.0)
        sh *= 2
    return b


SUB = CH // 2


def _direct_block(qb, kb, vb, bb, rows8):
    ob = jnp.zeros_like(qb)
    for s in range(SUB):
        e_s = jnp.exp(jnp.where(rows8 >= s, bb - bb[s:s + 1], -jnp.inf))
        ob = ob + jnp.sum(qb * e_s * kb[s:s + 1], axis=1, keepdims=True) * vb[s:s + 1]
    return ob


def _direct_block_bwd(qb, kb, vb, bb, dob, rows8, rowc8):
    dq = dk = dv = db = jnp.zeros_like(qb)
    for s in range(SUB):
        one = (rowc8 == s).astype(F32)
        ks, vs = kb[s:s + 1], vb[s:s + 1]
        e_s = jnp.exp(jnp.where(rows8 >= s, bb - bb[s:s + 1], -jnp.inf))
        qes = qb * e_s
        w = qes * ks
        a = jnp.sum(w, axis=1, keepdims=True)
        da = jnp.sum(dob * vs, axis=1, keepdims=True)
        dv = dv + one * jnp.sum(a * dob, axis=0, keepdims=True)
        dq = dq + da * e_s * ks
        dk = dk + one * jnp.sum(da * qes, axis=0, keepdims=True)
        u = da * w
        db = db + u - one * jnp.sum(u, axis=0, keepdims=True)
    return dq, dk, dv, db


def _cross_factors(q, k, b):
    ref = b[SUB - 1:SUB]
    e_hi, e_lo = jnp.exp(b[SUB:] - ref), jnp.exp(ref - b[:SUB])
    return q[SUB:] * e_hi, k[:SUB] * e_lo, e_hi, e_lo


def _intra_fwd(q, k, v, b, rows8):
    lo = _direct_block(q[:SUB], k[:SUB], v[:SUB], b[:SUB], rows8)
    hi = _direct_block(q[SUB:], k[SUB:], v[SUB:], b[SUB:], rows8)
    qe_hi, ke_lo, _, _ = _cross_factors(q, k, b)
    for s in range(SUB):
        hi = hi + jnp.sum(qe_hi * ke_lo[s:s + 1], axis=1, keepdims=True) * v[s:s + 1]
    return jnp.concatenate([lo, hi], axis=0)


def _intra_bwd(q, k, v, b, do, rows8, rowc8):
    dq_lo, dk_lo, dv_lo, db_lo = _direct_block_bwd(q[:SUB], k[:SUB], v[:SUB], b[:SUB], do[:SUB], rows8, rowc8)
    dq_hi, dk_hi, dv_hi, db_hi = _direct_block_bwd(q[SUB:], k[SUB:], v[SUB:], b[SUB:], do[SUB:], rows8, rowc8)
    qe_hi, ke_lo, e_hi, e_lo = _cross_factors(q, k, b)
    do_hi, v_lo = do[SUB:], v[:SUB]
    dqe = dke = jnp.zeros_like(qe_hi)
    for s in range(SUB):
        one = (rowc8 == s).astype(F32)
        a = jnp.sum(qe_hi * ke_lo[s:s + 1], axis=1, keepdims=True)
        da = jnp.sum(do_hi * v_lo[s:s + 1], axis=1, keepdims=True)
        dv_lo = dv_lo + one * jnp.sum(a * do_hi, axis=0, keepdims=True)
        dqe = dqe + da * ke_lo[s:s + 1]
        dke = dke + one * jnp.sum(da * qe_hi, axis=0, keepdims=True)
    u_hi, u_lo = dqe * qe_hi, dke * ke_lo
    d_ref = jnp.sum(u_lo, axis=0, keepdims=True) - jnp.sum(u_hi, axis=0, keepdims=True)
    db_lo = db_lo - u_lo + (rowc8 == SUB - 1).astype(F32) * d_ref
    cat = lambda lo, hi: jnp.concatenate([lo, hi], axis=0)
    return (cat(dq_lo, dq_hi + dqe * e_hi), cat(dk_lo + dke * e_lo, dk_hi), cat(dv_lo, dv_hi),
            cat(db_lo, db_hi + u_hi))


def _hgrn_fwd(z, lbl, nw):
    def body(hq_ref, hf_ref, hi_ref, hg_ref, lbl_ref, nw_ref, oraw_ref, og_ref, sh_ref, st_ref):
        @pl.when(pl.program_id(0) == 0)
        def _():
            st_ref[...] = jnp.zeros_like(st_ref)

        lb_all = _lower_bound(lbl_ref[...])
        rows = lax.broadcasted_iota(jnp.int32, (CH, HK), 0)
        rows8 = lax.broadcasted_iota(jnp.int32, (SUB, HK), 0)
        nwv = nw_ref[...]
        for cc, h in [(cc, h) for cc in range(HSTEP) for h in range(HEADS)]:
            rs = slice(CH * cc, CH * (cc + 1))
            sl = slice(HK * h, HK * (h + 1))
            lb = lb_all[:, sl]
            hq, hf, v, hg = hq_ref[rs, sl], hf_ref[rs, sl], hi_ref[rs, sl], hg_ref[rs, sl]
            q = hq * _sigmoid(hq)
            f = lb + (1.0 - lb) * _sigmoid(hf)
            k = 1.0 - f
            b = _cumsum_rows(jnp.log(f), rows)
            sh_ref[cc, h] = st_ref[h]
            o = _bdot(q * jnp.exp(b), st_ref[h], _NT) + _intra_fwd(q, k, v, b, rows8)
            bl = b[CH - 1:CH]
            st_ref[h] = st_ref[h] * jnp.exp(bl)
            st_ref[h] += _bdot(v, k * jnp.exp(bl - b), _TN)
            oraw_ref[rs, sl] = o
            nrm = o * lax.rsqrt(jnp.mean(o * o, axis=1, keepdims=True) + EPS)
            og_ref[rs, sl] = (nrm * nwv * (hg * _sigmoid(hg))).astype(BF)

    zblk = lambda c: pl.BlockSpec((CH * HSTEP, D), lambda i, c=c: (i, c))
    return pl.pallas_call(
        body, name="hgrn_fwd", grid=(NCH // HSTEP,),
        in_specs=[zblk(0), zblk(1), zblk(2), zblk(3),
                  pl.BlockSpec((2, D), lambda i: (0, 0)), pl.BlockSpec((1, HK), lambda i: (0, 0))],
        out_specs=[zblk(0), zblk(0),
                   pl.BlockSpec((HSTEP, HEADS, HK, HK), lambda i: (i, 0, 0, 0))],
        out_shape=[jax.ShapeDtypeStruct((T, D), F32), jax.ShapeDtypeStruct((T, D), BF),
                   jax.ShapeDtypeStruct((NCH, HEADS, HK, HK), F32)],
        scratch_shapes=[pltpu.VMEM((HEADS, HK, HK), F32)],
        compiler_params=_params(("arbitrary",)),
    )(z, z, z, z, lbl, nw)


def _hgrn_bwd(z, lbl, nw, oraw, dog, shist, side=None):
    hstep = 1
    s_arrays, s_in_specs, s_shapes, s_out_specs, s_sems = _side_io(side)
    na, no = len(s_arrays), len(s_shapes)

    def body(*refs):
        hq_ref, hf_ref, hi_ref, hg_ref, lbl_ref, nw_ref, oraw_ref, dog_ref, sh_ref = refs[:9]
        s_ins = refs[9:9 + na]
        dz_ref, dlb_ref, dnw_ref = refs[9 + na:12 + na]
        s_outs = refs[12 + na:12 + na + no]
        dst_ref = refs[12 + na + no]
        s_sem_refs = refs[13 + na + no:]

        @pl.when(pl.program_id(0) == 0)
        def _():
            dst_ref[...] = jnp.zeros_like(dst_ref)
            dlb_ref[...] = jnp.zeros_like(dlb_ref)
            dnw_ref[...] = jnp.zeros_like(dnw_ref)
            if side is not None:
                side.first(s_ins, s_outs, s_sem_refs)

        lb_all = _lower_bound(lbl_ref[...])
        rows = lax.broadcasted_iota(jnp.int32, (CH, HK), 0)
        rowc = lax.broadcasted_iota(jnp.int32, (CH, 1), 0)
        rows8 = lax.broadcasted_iota(jnp.int32, (SUB, HK), 0)
        rowc8 = lax.broadcasted_iota(jnp.int32, (SUB, 1), 0)
        nwv = nw_ref[...]
        dnw = jnp.zeros((1, HK), F32)
        for cc, h in [(cc, h) for cc in reversed(range(hstep)) for h in range(HEADS)]:
            rs = slice(CH * cc, CH * (cc + 1))
            sl = slice(HK * h, HK * (h + 1))
            lb = lb_all[:, sl]
            hq, hf, v, hg = hq_ref[rs, sl], hf_ref[rs, sl], hi_ref[rs, sl], hg_ref[rs, sl]
            o, dg_out = oraw_ref[rs, sl], dog_ref[rs, sl]
            sg = _sigmoid(hg)
            sil = hg * sg
            r = lax.rsqrt(jnp.mean(o * o, axis=1, keepdims=True) + EPS)
            nrm = o * r
            d_hg = dg_out * (nrm * nwv) * (sg * (1.0 + hg * (1.0 - sg)))
            dn = dg_out * nwv * sil
            dnw = dnw + jnp.sum(dg_out * nrm * sil, axis=0, keepdims=True)
            do = r * (dn - nrm * jnp.mean(dn * nrm, axis=1, keepdims=True))
            sq = _sigmoid(hq)
            q = hq * sq
            sig = _sigmoid(hf)
            f = lb + (1.0 - lb) * sig
            k = 1.0 - f
            b = _cumsum_rows(jnp.log(f), rows)
            eb = jnp.exp(b)
            qe = q * eb
            bl = b[CH - 1:CH]
            ebl = jnp.exp(bl)
            kdec = jnp.exp(bl - b)
            ke = k * kdec
            dqe = _bdot(do, sh_ref[cc, h])
            dq = dqe * eb
            db = dqe * qe
            dke = _bdot(v, dst_ref[h])
            dv = _bdot(ke, dst_ref[h], _NT)
            dk = dke * kdec
            rr = dke * ke
            db = db - rr
            db_last = (jnp.sum(rr, axis=0, keepdims=True)
                       + ebl * jnp.sum(dst_ref[h] * sh_ref[cc, h], axis=0, keepdims=True))
            dst_ref[h] = dst_ref[h] * ebl
            dst_ref[h] += _bdot(do, qe, _TN)
            dq_i, dk_i, dv_i, db_i = _intra_bwd(q, k, v, b, do, rows8, rowc8)
            dq, dk, dv = dq + dq_i, dk + dk_i, dv + dv_i
            db = db + db_i + (rowc == CH - 1).astype(F32) * db_last
            dgl = _rev_cumsum_rows(db, rows)
            df = dgl / f - dk
            dlb_ref[:, sl] += jnp.sum(df * (1.0 - sig), axis=0, keepdims=True)
            dz_ref[rs, sl] = (dq * (sq * (1.0 + hq * (1.0 - sq)))).astype(BF)
            dz_ref[rs, D + HK * h:D + HK * (h + 1)] = (df * (1.0 - lb) * sig * (1.0 - sig)).astype(BF)
            dz_ref[rs, 2 * D + HK * h:2 * D + HK * (h + 1)] = dv.astype(BF)
            dz_ref[rs, 3 * D + HK * h:3 * D + HK * (h + 1)] = d_hg.astype(BF)
        dnw_ref[...] += dnw
        if side is not None:
            @pl.when(pl.program_id(0) == NCH // hstep - 1)
            def _():
                side.last(s_ins, s_outs, s_sem_refs)

    rev = lambda i: NCH // hstep - 1 - i
    zblk = lambda c: pl.BlockSpec((CH * hstep, D), lambda i, c=c: (rev(i), c))
    out = pl.pallas_call(
        body, name="hgrn_bwd", grid=(NCH // hstep,),
        in_specs=[zblk(0), zblk(1), zblk(2), zblk(3),
                  pl.BlockSpec((2, D), lambda i: (0, 0)), pl.BlockSpec((1, HK), lambda i: (0, 0)),
                  zblk(0), zblk(0),
                  pl.BlockSpec((hstep, HEADS, HK, HK), lambda i: (rev(i), 0, 0, 0))] + s_in_specs,
        out_specs=[pl.BlockSpec((CH * hstep, 4 * D), lambda i: (rev(i), 0)),
                   pl.BlockSpec((1, D), lambda i: (0, 0)), pl.BlockSpec((1, HK), lambda i: (0, 0))] + s_out_specs,
        out_shape=[jax.ShapeDtypeStruct((T, 4 * D), BF), jax.ShapeDtypeStruct((1, D), F32),
                   jax.ShapeDtypeStruct((1, HK), F32)] + s_shapes,
        scratch_shapes=[pltpu.VMEM((HEADS, HK, HK), F32)] + s_sems,
        compiler_params=_params(("arbitrary",)),
    )(z, z, z, z, lbl, nw, oraw, dog, shist, *s_arrays)
    return out[0], out[1], out[2], out[3:]


BLK = 128
NBLK = T // BLK
QK_SCALE = 0.125


def _head_masks():
    lane = lax.broadcasted_iota(jnp.int32, (1, BLK), 1)
    return [(lane < 64).astype(F32), (lane >= 64).astype(F32)]


def _pieces(dil):
    m = T // dil
    out = []
    for r in range(dil):
        for j in range(m // BLK):
            start = r + dil * BLK * j
            rows = pl.ds(start, BLK, stride=dil) if dil > 1 else pl.ds(start, BLK)
            out.append((rows, r * m + BLK * j))
    return out


def _rope_tables(pos, invf):
    tm = 256

    def body(pos_ref, invf_ref, cos_ref, sa_ref, sb_ref):
        first = (lax.broadcasted_iota(jnp.int32, (tm, BLK), 1) % 64) < 32
        ang = pos_ref[...].astype(F32) * invf_ref[...]
        s = jnp.sin(ang)
        cos_ref[...] = jnp.cos(ang)
        sa_ref[...] = jnp.where(first, -s, 0.0)
        sb_ref[...] = jnp.where(first, 0.0, s)

    row = pl.BlockSpec((tm, BLK), lambda i: (i, 0))
    return pl.pallas_call(
        body, name="rope_tables", grid=(T // tm,),
        in_specs=[pl.BlockSpec((tm, 1), lambda i: (i, 0)), pl.BlockSpec((1, BLK), lambda i: (0, 0))],
        out_specs=[row, row, row], out_shape=[jax.ShapeDtypeStruct((T, BLK), F32)] * 3,
        compiler_params=_params(("parallel",)),
    )(pos, invf)


def _rope(x, c, sa, sb):
    return x * c + pltpu.roll(x, 96, axis=1) * sa + pltpu.roll(x, 32, axis=1) * sb


def _rope_t(d, c, sa, sb):
    return d * c + pltpu.roll(d * sa, 32, axis=1) + pltpu.roll(d * sb, 96, axis=1)


def _window_bias(bias_ref):
    ii = lax.broadcasted_iota(jnp.int32, (2 * BLK, BLK), 0) % BLK
    jj = lax.broadcasted_iota(jnp.int32, (2 * BLK, BLK), 1)
    bias_ref[0] = jnp.where(jj <= ii, 0.0, -jnp.inf)
    bias_ref[1] = jnp.where(jj >= ii, 0.0, -jnp.inf)


def _blocks(bi):
    if isinstance(bi, int):
        return pl.ds(bi * BLK, BLK), pl.ds(max(bi - 1, 0) * BLK, BLK)
    return (pl.ds(pl.multiple_of(bi * BLK, BLK), BLK),
            pl.ds(pl.multiple_of(jnp.maximum(bi - 1, 0) * BLK, BLK), BLK))


def _stack_heads(x, masks):
    return jnp.concatenate([x * masks[0], x * masks[1]], axis=0).astype(BF)


def _attn_fwd(z, cos, sa, sb):
    def body(q_ref, k_ref, v_ref, ag_ref, cos_ref, sa_ref, sb_ref, ob_ref, opre_ref, lse_ref,
             bias_ref, qr_ref, kr_ref, vr_ref, og_ref, lg_ref, otok_ref, ltok_ref, sc_ref):
        g = pl.program_id(1)
        masks = _head_masks()

        @pl.when(g == 0)
        def _():
            _window_bias(bias_ref)

        def group(gi):
            dil = ATT_GROUPS[gi][1]
            nblk = (T // dil) // BLK
            for rows, dst in _pieces(dil):
                c, sa, sb = cos_ref[rows, :], sa_ref[rows, :], sb_ref[rows, :]
                qr_ref[pl.ds(dst, BLK), :] = _rope(q_ref[rows, :], c, sa, sb) * QK_SCALE
                kr_ref[pl.ds(dst, BLK), :] = _rope(k_ref[rows, :], c, sa, sb)
                vr_ref[pl.ds(dst, BLK), :] = v_ref[rows, :]

            def scores(bi, slot):
                cur, prev = _blocks(bi)
                q2 = _stack_heads(qr_ref[cur, :], masks)
                sc_ref[slot, 0] = _dot(q2, kr_ref[cur, :].astype(BF), _NT) + bias_ref[0]
                if nblk > 1:
                    sc_ref[slot, 1] = (_dot(q2, kr_ref[prev, :].astype(BF), _NT)
                                       + (bias_ref[1] + jnp.where((bi % nblk) != 0, 0.0, -jnp.inf)))

            def finish(bi, slot):
                cur, prev = _blocks(bi)
                s_c, vc = sc_ref[slot, 0], vr_ref[cur, :].astype(BF)
                if nblk > 1:
                    s_p, vp = sc_ref[slot, 1], vr_ref[prev, :].astype(BF)
                    mx = jnp.max(jnp.maximum(s_c, s_p), axis=1, keepdims=True)
                    p_c, p_p = jnp.exp(s_c - mx), jnp.exp(s_p - mx)
                    den = jnp.sum(p_c + p_p, axis=1, keepdims=True)
                    oh = _dot(p_c.astype(BF), vc) + _dot(p_p.astype(BF), vp)
                else:
                    mx = jnp.max(s_c, axis=1, keepdims=True)
                    p_c = jnp.exp(s_c - mx)
                    den = jnp.sum(p_c, axis=1, keepdims=True)
                    oh = _dot(p_c.astype(BF), vc)
                on = oh / den
                lsev = jnp.broadcast_to(mx + jnp.log(den), (2 * BLK, BLK))
                og_ref[cur, :] = on[:BLK] * masks[0] + on[BLK:] * masks[1]
                lg_ref[0, cur, :] = lsev[:BLK]
                lg_ref[1, cur, :] = lsev[BLK:]

            def pair(j, carry):
                finish(2 * j, 0)
                scores(2 * j + 1, 1)
                finish(2 * j + 1, 1)
                scores(jnp.minimum(2 * j + 2, NBLK - 1), 0)
                return carry

            scores(0, 0)
            lax.fori_loop(0, NBLK // 2, pair, 0)
            for rows, src in _pieces(dil):
                srows = pl.ds(src, BLK)
                otok_ref[gi, rows, :] = og_ref[srows, :]
                ltok_ref[gi, 0, rows, :] = lg_ref[0, srows, :]
                ltok_ref[gi, 1, rows, :] = lg_ref[1, srows, :]

        for gi in range(3):
            pl.when(g == gi)(functools.partial(group, gi))

        @pl.when(g == 2)
        def _():
            for c in range(T // BLK):
                rows = pl.ds(BLK * c, BLK)
                wts = []
                for hh in range(2):
                    l0, l1, l2 = ltok_ref[0, hh, rows, :], ltok_ref[1, hh, rows, :], ltok_ref[2, hh, rows, :]
                    mx = jnp.maximum(jnp.maximum(l0, l1), l2)
                    e0, e1, e2 = jnp.exp(l0 - mx), jnp.exp(l1 - mx), jnp.exp(l2 - mx)
                    tot = e0 + e1 + e2
                    lse_ref[rows, BLK * hh:BLK * (hh + 1)] = mx + jnp.log(tot)
                    inv = 1.0 / tot
                    wts.append([e0 * inv, e1 * inv, e2 * inv])
                o = sum((wts[0][gi] * masks[0] + wts[1][gi] * masks[1]) * otok_ref[gi, rows, :] for gi in range(3))
                ag = ag_ref[rows, :]
                opre_ref[rows, :] = o
                ob_ref[rows, :] = (o * (ag * _sigmoid(ag))).astype(BF)

    c0 = ATT_COL0 // BLK
    zspec = lambda part: pl.BlockSpec((T, BLK), lambda p, g, part=part: (0, c0 + 12 * part + 4 * g + p))
    outspec = pl.BlockSpec((T, BLK), lambda p, g: (0, p))
    table = pl.BlockSpec((T, BLK), lambda p, g: (0, 0))
    big = lambda: pltpu.VMEM((T, BLK), F32)
    return pl.pallas_call(
        body, name="attn_fwd", grid=(4, 3),
        in_specs=[zspec(0), zspec(1), zspec(2),
                  pl.BlockSpec((T, BLK), lambda p, g: (0, AG_COL0 // BLK + p)), table, table, table],
        out_specs=[outspec, outspec, pl.BlockSpec((T, 2 * BLK), lambda p, g: (0, p))],
        out_shape=[jax.ShapeDtypeStruct((T, 512), BF), jax.ShapeDtypeStruct((T, 512), F32),
                   jax.ShapeDtypeStruct((T, 8 * BLK), F32)],
        scratch_shapes=[pltpu.VMEM((2, 2 * BLK, BLK), F32), big(), big(), big(), big(),
                        pltpu.VMEM((2, T, BLK), F32), pltpu.VMEM((3, T, BLK), F32), pltpu.VMEM((3, 2, T, BLK), F32),
                        pltpu.VMEM((2, 2, 2 * BLK, BLK), F32)],
        compiler_params=_params(("parallel", "arbitrary")),
    )(z, z, z, z, cos, sa, sb)


def _attn_bwd(z, cos, sa, sb, opre, lse, dob):
    def body(q_ref, k_ref, v_ref, ag_ref, cos_ref, sa_ref, sb_ref, o_ref, lse0_ref, lse1_ref, dob_ref,
             dq_ref, dk_ref, dv_ref, dag_ref,
             bias_ref, dtok_ref, qr_ref, kr_ref, vr_ref, dor_ref, lr_ref, dr_ref,
             dqr_ref, dkr_ref, dvr_ref, pd_ref):
        g = pl.program_id(1)
        masks = _head_masks()

        @pl.when(g == 0)
        def _():
            _window_bias(bias_ref)
            for c in range(T // BLK):
                rows = pl.ds(BLK * c, BLK)
                ag, dob_v, o = ag_ref[rows, :], dob_ref[rows, :], o_ref[rows, :]
                sg = _sigmoid(ag)
                dag_ref[rows, :] = (dob_v * o * (sg * (1.0 + ag * (1.0 - sg)))).astype(BF)
                prod = dob_v * (ag * sg) * o
                for hh, mh in enumerate(masks):
                    dtok_ref[hh, rows, :] = jnp.broadcast_to(jnp.sum(prod * mh, axis=1, keepdims=True), (BLK, BLK))

        def group(gi):
            dil = ATT_GROUPS[gi][1]
            nblk = (T // dil) // BLK
            for rows, dst in _pieces(dil):
                drows = pl.ds(dst, BLK)
                c, sa, sb = cos_ref[rows, :], sa_ref[rows, :], sb_ref[rows, :]
                qr_ref[drows, :] = _rope(q_ref[rows, :], c, sa, sb) * QK_SCALE
                kr_ref[drows, :] = _rope(k_ref[rows, :], c, sa, sb)
                vr_ref[drows, :] = v_ref[rows, :]
                ag = ag_ref[rows, :]
                dor_ref[drows, :] = dob_ref[rows, :] * (ag * _sigmoid(ag))
                for hh, lse_ref in enumerate((lse0_ref, lse1_ref)):
                    lr_ref[hh, drows, :] = lse_ref[rows, :]
                    dr_ref[hh, drows, :] = dtok_ref[hh, rows, :]
            dkr_ref[...] = jnp.zeros_like(dkr_ref)
            dvr_ref[...] = jnp.zeros_like(dvr_ref)

            def probs(bi, slot):
                cur, prev = _blocks(bi)
                q2, do2 = _stack_heads(qr_ref[cur, :], masks), _stack_heads(dor_ref[cur, :], masks)
                lh = jnp.concatenate([lr_ref[0, cur, :], lr_ref[1, cur, :]], axis=0)
                dh = jnp.concatenate([dr_ref[0, cur, :], dr_ref[1, cur, :]], axis=0)
                p_c = jnp.exp(_dot(q2, kr_ref[cur, :].astype(BF), _NT) + bias_ref[0] - lh)
                pd_ref[slot, 0] = p_c.astype(BF)
                pd_ref[slot, 1] = (p_c * (_dot(do2, vr_ref[cur, :].astype(BF), _NT) - dh)).astype(BF)
                if nblk > 1:
                    bias_p = bias_ref[1] + jnp.where((bi % nblk) != 0, 0.0, -jnp.inf)
                    p_p = jnp.exp(_dot(q2, kr_ref[prev, :].astype(BF), _NT) + bias_p - lh)
                    pd_ref[slot, 2] = p_p.astype(BF)
                    pd_ref[slot, 3] = (p_p * (_dot(do2, vr_ref[prev, :].astype(BF), _NT) - dh)).astype(BF)

            def grads(bi, slot):
                cur, prev = _blocks(bi)
                q2, do2 = _stack_heads(qr_ref[cur, :], masks), _stack_heads(dor_ref[cur, :], masks)
                p_c, ds_c = pd_ref[slot, 0], pd_ref[slot, 1]
                dq2 = _dot(ds_c, kr_ref[cur, :].astype(BF))
                dkr_ref[cur, :] += _dot(ds_c, q2, _TN)
                dvr_ref[cur, :] += _dot(p_c, do2, _TN)
                if nblk > 1:
                    p_p, ds_p = pd_ref[slot, 2], pd_ref[slot, 3]
                    dq2 = dq2 + _dot(ds_p, kr_ref[prev, :].astype(BF))
                    dkr_ref[prev, :] += _dot(ds_p, q2, _TN)
                    dvr_ref[prev, :] += _dot(p_p, do2, _TN)
                dqr_ref[cur, :] = dq2[:BLK] * masks[0] + dq2[BLK:] * masks[1]

            def pair(j, carry):
                grads(2 * j, 0)
                probs(2 * j + 1, 1)
                grads(2 * j + 1, 1)
                probs(jnp.minimum(2 * j + 2, NBLK - 1), 0)
                return carry

            probs(0, 0)
            lax.fori_loop(0, NBLK // 2, pair, 0)
            for rows, src in _pieces(dil):
                srows = pl.ds(src, BLK)
                c, sa, sb = cos_ref[rows, :], sa_ref[rows, :], sb_ref[rows, :]
                qr_ref[rows, :] = _rope_t(dqr_ref[srows, :] * QK_SCALE, c, sa, sb)
                kr_ref[rows, :] = _rope_t(dkr_ref[srows, :], c, sa, sb)
                vr_ref[rows, :] = dvr_ref[srows, :]
            for c in range(T // 256):
                rows = pl.ds(256 * c, 256)
                dq_ref[rows, :] = qr_ref[rows, :].astype(BF)
                dk_ref[rows, :] = kr_ref[rows, :].astype(BF)
                dv_ref[rows, :] = vr_ref[rows, :].astype(BF)

        for gi in range(3):
            pl.when(g == gi)(functools.partial(group, gi))

    c0 = ATT_COL0 // BLK
    zspec = lambda part: pl.BlockSpec((T, BLK), lambda p, g, part=part: (0, c0 + 12 * part + 4 * g + p))
    pspec = pl.BlockSpec((T, BLK), lambda p, g: (0, p))
    gspec = pl.BlockSpec((T, BLK), lambda p, g: (0, 4 * g + p))
    table = pl.BlockSpec((T, BLK), lambda p, g: (0, 0))
    big = lambda: pltpu.VMEM((T, BLK), F32)
    two = lambda: pltpu.VMEM((2, T, BLK), F32)
    return pl.pallas_call(
        body, name="attn_bwd", grid=(4, 3),
        in_specs=[zspec(0), zspec(1), zspec(2),
                  pl.BlockSpec((T, BLK), lambda p, g: (0, AG_COL0 // BLK + p)), table, table, table,
                  pspec, pl.BlockSpec((T, BLK), lambda p, g: (0, 2 * p)),
                  pl.BlockSpec((T, BLK), lambda p, g: (0, 2 * p + 1)), pspec],
        out_specs=[gspec, gspec, gspec, pspec],
        out_shape=[jax.ShapeDtypeStruct((T, 1536), BF), jax.ShapeDtypeStruct((T, 1536), BF),
                   jax.ShapeDtypeStruct((T, 1536), BF), jax.ShapeDtypeStruct((T, 512), BF)],
        scratch_shapes=[pltpu.VMEM((2, 2 * BLK, BLK), F32), two(), big(), big(), big(), big(),
                        two(), two(), big(), big(), big(), pltpu.VMEM((2, 4, 2 * BLK, BLK), BF)],
        compiler_params=_params(("parallel", "arbitrary")),
    )(z, z, z, z, cos, sa, sb, opre, lse, lse, dob)


def _merge_fwd(ya, yb, z):
    tm = 256

    def body(ya_ref, yb_ref, ga_ref, gb_ref, m_ref):
        m_ref[...] = (_sigmoid(ga_ref[...]) * ya_ref[...] + _sigmoid(gb_ref[...]) * yb_ref[...]).astype(BF)

    row = pl.BlockSpec((tm, D), lambda i: (i, 0))
    return pl.pallas_call(
        body, name="merge_fwd", grid=(T // tm,),
        in_specs=[row, row, pl.BlockSpec((tm, D), lambda i: (i, GATE_COL0 // D)),
                  pl.BlockSpec((tm, D), lambda i: (i, GATE_COL0 // D + 1))],
        out_specs=row, out_shape=jax.ShapeDtypeStruct((T, D), BF),
        compiler_params=_params(("parallel",)),
    )(ya, yb, z, z)


def _out_loss(merged, w_out, x, tgt, wf):
    tm = 256

    def body(m_ref, w_ref, x_ref, t_ref, wf_ref, dout_ref, loss_ref, gwf_ref):
        @pl.when(pl.program_id(0) == 0)
        def _():
            loss_ref[...] = jnp.zeros_like(loss_ref)
            gwf_ref[...] = jnp.zeros_like(gwf_ref)

        out = x_ref[...] + _dot(m_ref[...], w_ref[...])
        r = lax.rsqrt(jnp.mean(out * out, axis=-1, keepdims=True) + EPS)
        yh = out * r
        wfv = wf_ref[...]
        err = yh * wfv - t_ref[...]
        loss_ref[...] += jnp.sum(err * err, axis=0, keepdims=True) * (0.5 / D)
        dy = err * (1.0 / D)
        gwf_ref[...] += jnp.sum(dy * yh, axis=0, keepdims=True)
        dyh = dy * wfv
        dout_ref[...] = r * (dyh - yh * jnp.mean(dyh * yh, axis=-1, keepdims=True))

    row = pl.BlockSpec((tm, D), lambda i: (i, 0))
    vec = pl.BlockSpec((1, D), lambda i: (0, 0))
    return pl.pallas_call(
        body, name="out_loss", grid=(T // tm,),
        in_specs=[row, pl.BlockSpec((D, D), lambda i: (0, 0)), row, row, vec],
        out_specs=[row, vec, vec],
        out_shape=[jax.ShapeDtypeStruct((T, D), F32), jax.ShapeDtypeStruct((1, D), F32),
                   jax.ShapeDtypeStruct((1, D), F32)],
        compiler_params=_params(("arbitrary",)),
    )(merged, w_out, x, tgt, wf)


def _merge_bwd(dm, ya, yb, z):
    tm = 256

    def body(dm_ref, ya_ref, yb_ref, ga_ref, gb_ref, dya_ref, dyb_ref, dg_ref):
        dmv = dm_ref[...]
        sa, sb = _sigmoid(ga_ref[...]), _sigmoid(gb_ref[...])
        dya_ref[...] = (sa * dmv).astype(BF)
        dyb_ref[...] = (sb * dmv).astype(BF)
        dg_ref[:, :D] = (dmv * ya_ref[...] * sa * (1.0 - sa)).astype(BF)
        dg_ref[:, D:] = (dmv * yb_ref[...] * sb * (1.0 - sb)).astype(BF)

    row = pl.BlockSpec((tm, D), lambda i: (i, 0))
    return pl.pallas_call(
        body, name="merge_bwd", grid=(T // tm,),
        in_specs=[row, row, row, pl.BlockSpec((tm, D), lambda i: (i, GATE_COL0 // D)),
                  pl.BlockSpec((tm, D), lambda i: (i, GATE_COL0 // D + 1))],
        out_specs=[row, row, pl.BlockSpec((tm, 2 * D), lambda i: (i, 0))],
        out_shape=[jax.ShapeDtypeStruct((T, D), BF), jax.ShapeDtypeStruct((T, D), BF),
                   jax.ShapeDtypeStruct((T, 2 * D), BF)],
        compiler_params=_params(("parallel",)),
    )(dm, ya, yb, z, z)


def _rope_inv_freq():
    inv = ROPE_THETA ** (-jnp.arange(0, 64, 2, dtype=F32) / 64)
    return jnp.tile(inv, 4).reshape(1, BLK)


def _local_step(x, pos, norm_w, lbl, hnw, wf, tgt, w_in, w_a, w_b, w_out, shard_shapes=()):
    invf = _rope_inv_freq()
    h = _rmsnorm_fwd(x, norm_w)
    if shard_shapes:
        z, (w_a, w_b, w_out) = _matmul(h, w_in, tm=T, tn=512, name="z_proj",
                                       side=_gather_side([w_a, w_b, w_out], WEIGHT_AXES[1:]))
    else:
        z = _matmul(h, w_in, tm=T, tn=512, name="z_proj")
    oraw, og, shist = _hgrn_fwd(z, lbl, hnw)
    cos, sa, sb = _rope_tables(pos, invf)
    ob, opre, lse = _attn_fwd(z, cos, sa, sb)
    ya = _matmul(og, w_a, tm=1024, tn=512, name="ya_proj")
    yb = _matmul(ob, w_b, tm=1024, tn=512, name="yb_proj")
    merged = _merge_fwd(ya, yb, z)
    dout, loss_vec, g_wf = _out_loss(merged, w_out, x, tgt, wf)

    dm = _matmul(dout, w_out, tb=True, tm=1024, tn=512, name="d_merged")
    g_wout = _matmul(merged, dout, ta=True, out_dtype=BF, tm=512, tn=1024, name="g_wout")
    dya, dyb, dgates = _merge_bwd(dm, ya, yb, z)
    dog = _matmul(dya, w_a, tb=True, tm=1024, tn=512, name="d_og")
    g_wa = _matmul(og, dya, ta=True, out_dtype=BF, tm=512, tn=1024, name="g_wa")
    dob = _matmul(dyb, w_b, tb=True, tm=1024, tn=512, name="d_ob")
    g_wb = _matmul(ob, dyb, ta=True, out_dtype=BF, tm=512, tn=1024, name="g_wb")
    small = [g_wa, g_wb, g_wout]
    side_s = side_w = None
    if shard_shapes:
        p3_s = _rs_partials(small, shard_shapes[1:], WEIGHT_AXES[1:], "small")
        side_s = _chip_exchange_side(p3_s, shard_shapes[1:], WEIGHT_AXES[1:])
    dz_h, dlb, g_hnw, land_s = _hgrn_bwd(z, lbl, hnw, oraw, dog, shist, side=side_s)
    dq, dk, dv, dag = _attn_bwd(z, cos, sa, sb, opre, lse, dob)
    dz_parts = [dz_h, dq, dk, dv, dag, dgates]
    g_big = [_grad_w_in(h, dz_parts)] + small
    if shard_shapes:
        p3_w = _rs_partials(g_big[:1], shard_shapes[:1], WEIGHT_AXES[:1], "w_in")
        side_w = _chip_exchange_side(p3_w, shard_shapes[:1], WEIGHT_AXES[:1])
    gx, g_nw, land_w = _grad_x(dz_parts, w_in, x, dout, norm_w, side=side_w)
    if shard_shapes:
        g_big = _rs_finish(p3_w + p3_s, list(land_w) + list(land_s), shard_shapes, WEIGHT_AXES)
    return dict(loss_vec=loss_vec, gx=gx, g_nw=g_nw, dlb=dlb, g_hnw=g_hnw, g_wf=g_wf,
                g_win=g_big[0], g_wa=g_big[1], g_wb=g_big[2], g_wout=g_big[3])


MESH = pl.DeviceIdType.MESH
HBM = pl.BlockSpec(memory_space=pl.ANY)
WEIGHT_AXES = (1, 0, 1, 0)


def _place():
    x, y, c = lax.axis_index("x"), lax.axis_index("y"), lax.axis_index("c")
    chips = [(1 - x, y), (x, 1 - y), (1 - x, 1 - y)]
    return x, y, c, chips


def _block_half(ref, shard_shape, axis, j, half):
    r, c = shard_shape
    hr = r // 2
    if axis == 0:
        return ref.at[pl.ds(pl.multiple_of(j * r + half * hr, 16), hr), :]
    return ref.at[pl.ds(pl.multiple_of(half * hr, 16), hr), pl.ds(pl.multiple_of(j * c, 128), c)]


class _Side:
    def __init__(self, arrays, out_shapes, sems, first, last):
        self.arrays, self.out_shapes, self.sems, self.first, self.last = arrays, out_shapes, sems, first, last


def _gather_side(shards, axes):
    n = len(shards)
    shapes = [s.shape for s in shards]

    def copies(ins, outs, sems):
        send1, recv1, send2, recv2, send0, recv0 = sems
        x, y, c, chips = _place()
        me = 2 * x + y
        sib = (x, y, 1 - c)
        near = ((1 - c) * (1 - x) + c * x, (1 - c) * y + c * (1 - y))
        far = ((1 - c) * x + c * (1 - x), (1 - c) * (1 - y) + c * y)
        out = []
        for a in range(n):
            r, cc = shapes[a]
            mine = (outs[a].at[pl.ds(pl.multiple_of(me * r, 16), r), :] if axes[a] == 0
                    else outs[a].at[:, pl.ds(pl.multiple_of(me * cc, 128), cc)])
            own = pltpu.make_async_remote_copy(
                src_ref=ins[a], dst_ref=mine, send_sem=send0.at[a], recv_sem=recv0.at[a],
                device_id=sib, device_id_type=MESH)
            src = ins[a].at[pl.ds(pl.multiple_of(c * (r // 2), 16), r // 2), :]
            sends = [pltpu.make_async_remote_copy(
                src_ref=src, dst_ref=_block_half(outs[a], shapes[a], axes[a], me, c),
                send_sem=send1.at[a, k], recv_sem=recv1.at[a, k], device_id=(*chips[k], c), device_id_type=MESH)
                for k in range(2)]

            def region(chip, half):
                return _block_half(outs[a], shapes[a], axes[a], 2 * chip[0] + chip[1], half)

            def arrival(chip, k):
                reg = region(chip, c)
                return pltpu.make_async_remote_copy(
                    src_ref=reg, dst_ref=reg, send_sem=send1.at[a, k], recv_sem=recv1.at[a, k],
                    device_id=(*chip, c), device_id_type=MESH)

            def to_sibling(chip, k):
                reg = region(chip, c)
                return pltpu.make_async_remote_copy(
                    src_ref=reg, dst_ref=reg, send_sem=send2.at[a, k], recv_sem=recv2.at[a, k],
                    device_id=sib, device_id_type=MESH)

            def from_sibling(chip, k):
                reg = region(chip, 1 - c)
                return pltpu.make_async_remote_copy(
                    src_ref=reg, dst_ref=reg, send_sem=send2.at[a, k], recv_sem=recv2.at[a, k],
                    device_id=sib, device_id_type=MESH)

            relay = pltpu.make_async_remote_copy(
                src_ref=region(near, c), dst_ref=region(near, c), send_sem=send1.at[a, 2], recv_sem=recv1.at[a, 2],
                device_id=(*far, c), device_id_type=MESH)
            hops = [(arrival(near, c), to_sibling(near, c)), (arrival(far, 1 - c), to_sibling(far, 1 - c)),
                    (arrival(chips[2], 2), to_sibling(chips[2], 2))]
            back = [from_sibling(chips[k], k) for k in range(3)]
            out.append((own, sends, relay, hops, back))
        return out

    def first(ins, outs, sems):
        for own, sends, _, _, _ in copies(ins, outs, sems):
            own.start()
            for cp in sends:
                cp.start()

    def last(ins, outs, sems):
        per_array = copies(ins, outs, sems)
        for step in range(3):
            for _, _, relay, hops, _ in per_array:
                arrived, onward = hops[step]
                arrived.wait_recv()
                if step == 0:
                    relay.start()
                onward.start()
        for own, sends, relay, hops, back in per_array:
            for cp in back:
                cp.wait_recv()
            for cp in sends + [relay] + [onward for _, onward in hops]:
                cp.wait_send()
            own.wait()

    full = [(4 * r, c) if ax == 0 else (r, 4 * c) for (r, c), ax in zip(shapes, axes)]
    sems = [pltpu.SemaphoreType.DMA((n, 3)), pltpu.SemaphoreType.DMA((n, 3)),
            pltpu.SemaphoreType.DMA((n, 3)), pltpu.SemaphoreType.DMA((n, 3)),
            pltpu.SemaphoreType.DMA((n,)), pltpu.SemaphoreType.DMA((n,))]
    return _Side(list(shards), [jax.ShapeDtypeStruct(f, BF) for f in full], sems, first, last)


def _run_side(side, name):
    na, no = len(side.arrays), len(side.out_shapes)

    def body(*refs):
        ins, outs, sems = refs[:na], refs[na:na + no], refs[na + no:]
        side.first(ins, outs, sems)
        side.last(ins, outs, sems)

    return pl.pallas_call(
        body, name=name, in_specs=[HBM] * na, out_specs=[HBM] * no,
        out_shape=side.out_shapes, scratch_shapes=side.sems,
    )(*side.arrays)


def _as3d(g, shard_shape, axis):
    r, c = shard_shape
    return g.reshape(4, r, c) if axis == 0 else g.reshape(1, r, 4 * c)


def _half_rows(ref3, hr, half):
    return ref3.at[:, pl.ds(pl.multiple_of(half * hr, 16), hr), :]


def _rs_pair_exchange(g3s, name):
    n = len(g3s)

    def body(*refs):
        ins, outs = refs[:n], refs[n:2 * n]
        send, recv = refs[2 * n:]
        x, y, c, _ = _place()
        cps = []
        for a in range(n):
            hr = g3s[a].shape[1] // 2
            cp = pltpu.make_async_remote_copy(
                src_ref=_half_rows(ins[a], hr, 1 - c), dst_ref=outs[a],
                send_sem=send.at[a], recv_sem=recv.at[a], device_id=(x, y, 1 - c), device_id_type=MESH)
            cp.start()
            cps.append(cp)
        for cp in cps:
            cp.wait()

    return pl.pallas_call(
        body, name=name,
        in_specs=[HBM] * n, out_specs=[HBM] * n,
        out_shape=[jax.ShapeDtypeStruct((g.shape[0], g.shape[1] // 2, g.shape[2]), BF) for g in g3s],
        scratch_shapes=[pltpu.SemaphoreType.DMA((n,)), pltpu.SemaphoreType.DMA((n,))],
    )(*g3s)


def _pair_sum(g3, land, cidx, name):
    nb, r, w = g3.shape
    hr = r // 2
    tr = 64

    def body(c_ref, g_ref, l_ref, o_ref):
        o_ref[...] = (g_ref[...].astype(F32) + l_ref[...].astype(F32)).astype(BF)

    blk = (nb, tr, w)
    return pl.pallas_call(
        body, name=name,
        grid_spec=pltpu.PrefetchScalarGridSpec(
            num_scalar_prefetch=1, grid=(hr // tr,),
            in_specs=[pl.BlockSpec(blk, lambda i, c: (0, c[0] * (hr // tr) + i, 0)),
                      pl.BlockSpec(blk, lambda i, c: (0, i, 0))],
            out_specs=pl.BlockSpec(blk, lambda i, c: (0, i, 0))),
        out_shape=jax.ShapeDtypeStruct((nb, hr, w), BF),
        compiler_params=_params(("parallel",)),
    )(cidx, g3, land)


def _chip_exchange_side(p3s, shapes, axes):
    n = len(p3s)

    def copies(ins, outs, sems):
        send, recv = sems
        x, y, c, chips = _place()
        cps = []
        for a in range(n):
            r, cc = shapes[a]
            for k, (px, py) in enumerate(chips):
                j = 2 * px + py
                src = ins[a].at[j] if axes[a] == 0 else ins[a].at[0, :, pl.ds(pl.multiple_of(j * cc, 128), cc)]
                cps.append(pltpu.make_async_remote_copy(
                    src_ref=src, dst_ref=outs[a].at[k], send_sem=send.at[a, k], recv_sem=recv.at[a, k],
                    device_id=(px, py, c), device_id_type=MESH))
        return cps

    def first(ins, outs, sems):
        for cp in copies(ins, outs, sems):
            cp.start()

    def last(ins, outs, sems):
        for cp in copies(ins, outs, sems):
            cp.wait()

    return _Side(list(p3s), [jax.ShapeDtypeStruct((3, r // 2, c), BF) for r, c in shapes],
                 [pltpu.SemaphoreType.DMA((n, 3)), pltpu.SemaphoreType.DMA((n, 3))], first, last)


def _chip_sum(p3, land, shard_shape, axis, idx, name):
    r, c = shard_shape
    hr = r // 2
    tr = 64
    nt = hr // tr

    def body(idx_ref, p_ref, l_ref, o_ref):
        acc = p_ref[...].astype(F32)
        for k in range(3):
            acc = acc + l_ref[k].astype(F32)
        o_ref[...] = acc

    own = (pl.BlockSpec((None, tr, c), lambda i, idx: (idx[0], i, 0)) if axis == 0
           else pl.BlockSpec((None, tr, c), lambda i, idx: (0, i, idx[0])))
    return pl.pallas_call(
        body, name=name,
        grid_spec=pltpu.PrefetchScalarGridSpec(
            num_scalar_prefetch=1, grid=(nt,),
            in_specs=[own, pl.BlockSpec((3, tr, c), lambda i, idx: (0, i, 0))],
            out_specs=pl.BlockSpec((tr, c), lambda i, idx: (idx[1] * nt + i, 0))),
        out_shape=jax.ShapeDtypeStruct((r, c), F32),
        compiler_params=_params(("parallel",)),
    )(idx, p3, land)


def _rs_pair_gather(fulls):
    n = len(fulls)

    def body(*refs):
        ins, outs = refs[:n], refs[n:2 * n]
        send, recv = refs[2 * n:]
        x, y, c, _ = _place()
        cps = []
        for a in range(n):
            hr = fulls[a].shape[0] // 2
            rows = pl.ds(pl.multiple_of(c * hr, 8), hr)
            cp = pltpu.make_async_remote_copy(
                src_ref=ins[a].at[rows, :], dst_ref=outs[a].at[rows, :], send_sem=send.at[a], recv_sem=recv.at[a],
                device_id=(x, y, 1 - c), device_id_type=MESH)
            cp.start()
            cps.append(cp)
        for a, cp in enumerate(cps):
            cp.wait_send()
            hr = fulls[a].shape[0] // 2
            other = pl.ds(pl.multiple_of((1 - c) * hr, 8), hr)
            pltpu.make_async_remote_copy(
                src_ref=ins[a].at[other, :], dst_ref=outs[a].at[other, :], send_sem=send.at[a], recv_sem=recv.at[a],
                device_id=(x, y, 1 - c), device_id_type=MESH).wait_recv()

    return pl.pallas_call(
        body, name="grads_pair_gather",
        in_specs=[HBM] * n, out_specs=[HBM] * n,
        out_shape=[jax.ShapeDtypeStruct(f.shape, F32) for f in fulls],
        input_output_aliases={a: a for a in range(n)},
        scratch_shapes=[pltpu.SemaphoreType.DMA((n,)), pltpu.SemaphoreType.DMA((n,))],
    )(*fulls)


def _rs_partials(grads, shapes, axes, tag):
    cidx = jnp.reshape(lax.axis_index("c"), (1,)).astype(jnp.int32)
    g3s = [_as3d(g, s, ax) for g, s, ax in zip(grads, shapes, axes)]
    lands = _rs_pair_exchange(g3s, f"grads_pair_exchange_{tag}")
    return [_pair_sum(g3, l, cidx, f"pair_sum_{tag}_{a}") for a, (g3, l) in enumerate(zip(g3s, lands))]


def _rs_finish(p3s, landed, shapes, axes):
    x, y, c = lax.axis_index("x"), lax.axis_index("y"), lax.axis_index("c")
    idx = jnp.stack([2 * x + y, c]).astype(jnp.int32)
    fulls = [_chip_sum(p3, l2, s, ax, idx, f"chip_sum_{a}")
             for a, (p3, l2, s, ax) in enumerate(zip(p3s, landed, shapes, axes))]
    return _rs_pair_gather(fulls)


NSMALL = 8


def _small_all_reduce(g_nw, dlb, g_hnw, g_wf, loss_vec):
    def body(nw_ref, lb_ref, hn_ref, wf_ref, ls_ref, out_ref, pack_ref, buf_ref, send, recv):
        x, y, c = lax.axis_index("x"), lax.axis_index("y"), lax.axis_index("c")
        me = 4 * x + 2 * y + c
        pack_ref[...] = jnp.zeros_like(pack_ref)
        pack_ref[0:1, :] = nw_ref[...]
        pack_ref[1:2, :] = lb_ref[...]
        pack_ref[2:3, 0:HK] = hn_ref[...]
        pack_ref[3:4, :] = wf_ref[...]
        pack_ref[4:5, :] = ls_ref[...]
        buf_ref[me] = pack_ref[...]
        cps = []
        for d in range(1, 8):
            dx, dy, dc = d >> 2, (d >> 1) & 1, d & 1
            peer = (1 - x if dx else x, 1 - y if dy else y, 1 - c if dc else c)
            cp = pltpu.make_async_remote_copy(
                src_ref=pack_ref, dst_ref=buf_ref.at[me], send_sem=send.at[d - 1], recv_sem=recv.at[d - 1],
                device_id=peer, device_id_type=MESH)
            cp.start()
            cps.append(cp)
        for d in range(1, 8):
            dx, dy, dc = d >> 2, (d >> 1) & 1, d & 1
            src = 4 * (1 - x if dx else x) + 2 * (1 - y if dy else y) + (1 - c if dc else c)
            pltpu.make_async_remote_copy(
                src_ref=pack_ref, dst_ref=buf_ref.at[src], send_sem=send.at[d - 1], recv_sem=recv.at[d - 1],
                device_id=(x, y, c), device_id_type=MESH).wait_recv()
        for cp in cps:
            cp.wait_send()
        acc = buf_ref[0]
        for i in range(1, 8):
            acc = acc + buf_ref[i]
        out_ref[...] = acc

    vm = pl.BlockSpec(memory_space=pltpu.VMEM)
    return pl.pallas_call(
        body, name="small_all_reduce",
        in_specs=[vm] * 5, out_specs=vm,
        out_shape=jax.ShapeDtypeStruct((NSMALL, D), F32),
        scratch_shapes=[pltpu.VMEM((NSMALL, D), F32), pltpu.VMEM((8, NSMALL, D), F32),
                        pltpu.SemaphoreType.DMA((7,)), pltpu.SemaphoreType.DMA((7,))],
    )(g_nw, dlb, g_hnw, g_wf, loss_vec)


def _adamw_math(w, g, m, v):
    m = B1 * m + (1.0 - B1) * g
    v = B2 * v + (1.0 - B2) * (g * g)
    m_hat = m / (1.0 - B1 ** STEP)
    v_hat = v / (1.0 - B2 ** STEP)
    return -LR * (m_hat / (jnp.sqrt(v_hat) + ADAM_EPS) + WD * w), m, v


def _adamw(w, g, m, v, name):
    r, c = w.shape
    tr = 64

    def body(w_ref, g_ref, m_ref, v_ref, d_ref, nm_ref, nv_ref):
        d_ref[...], nm_ref[...], nv_ref[...] = _adamw_math(w_ref[...], g_ref[...], m_ref[...], v_ref[...])

    blk = pl.BlockSpec((tr, c), lambda i: (i, 0))
    return pl.pallas_call(
        body, name=name, grid=(r // tr,), in_specs=[blk] * 4, out_specs=[blk] * 3,
        out_shape=[jax.ShapeDtypeStruct((r, c), F32)] * 3,
        compiler_params=_params(("parallel",)),
    )(w, g, m, v)


def _small_update(red, lbl, params):
    def body(red_ref, *refs):
        ins, outs = refs[:12], refs[12:]
        lb = _lower_bound(ins[3][...])
        dl0 = red_ref[1:2, :] * lb * (1.0 - lb)
        row = lax.broadcasted_iota(jnp.int32, (2, D), 0)
        grads = [red_ref[0:1, :], jnp.where(row == 0, dl0, -dl0), red_ref[2:3, 0:HK], red_ref[3:4, :]]
        for i, g in enumerate(grads):
            w, m, v = ins[3 * i][...], ins[3 * i + 1][...], ins[3 * i + 2][...]
            d, nm, nv = _adamw_math(w, g, m, v)
            outs[4 * i][...] = g
            outs[4 * i + 1][...] = d
            outs[4 * i + 2][...] = nm
            outs[4 * i + 3][...] = nv
        outs[16][...] = jnp.sum(red_ref[4:5, :], axis=1, keepdims=True)

    flat = [a for p in params for a in p]
    vm = pl.BlockSpec(memory_space=pltpu.VMEM)
    shapes = [jax.ShapeDtypeStruct(p[0].shape, F32) for p in params for _ in range(4)]
    return pl.pallas_call(
        body, name="small_update",
        in_specs=[vm] * 13, out_specs=[vm] * 17,
        out_shape=shapes + [jax.ShapeDtypeStruct((1, 1), F32)],
    )(red, *flat)


def kernel(x, positions, norm_w, w_in, lb_logits, hgrn_norm_w, w_branch_a, w_branch_b, w_out, final_norm_w, loss_target, m_norm_w, m_w_in, m_lb_logits, m_hgrn_norm_w, m_w_branch_a, m_w_branch_b, m_w_out, m_final_norm_w, v_norm_w, v_w_in, v_lb_logits, v_hgrn_norm_w, v_w_branch_a, v_w_branch_b, v_w_out, v_final_norm_w):
    big_w = [w_in[0], w_branch_a[0], w_branch_b[0], w_out[0]]
    big_m = [m_w_in[0], m_w_branch_a[0], m_w_branch_b[0], m_w_out[0]]
    big_v = [v_w_in[0], v_w_branch_a[0], v_w_branch_b[0], v_w_out[0]]
    shapes = [w.shape for w in big_w]
    wf = final_norm_w.reshape(1, D)

    shards = [w.astype(BF) for w in big_w]
    w_in_full, = _run_side(_gather_side(shards[:1], WEIGHT_AXES[:1]), "w_in_all_gather")
    loc = _local_step(x[0], positions.reshape(T, 1), norm_w, lb_logits, hgrn_norm_w, wf, loss_target[0],
                      w_in_full, *shards[1:], shard_shapes=shapes)
    g_big = [loc["g_win"], loc["g_wa"], loc["g_wb"], loc["g_wout"]]
    red = _small_all_reduce(loc["g_nw"], loc["dlb"], loc["g_hnw"], loc["g_wf"], loc["loss_vec"])

    small = _small_update(red, lb_logits, [
        (norm_w, m_norm_w, v_norm_w), (lb_logits, m_lb_logits, v_lb_logits),
        (hgrn_norm_w, m_hgrn_norm_w, v_hgrn_norm_w),
        (wf, m_final_norm_w.reshape(1, D), v_final_norm_w.reshape(1, D))])
    loss = small[16].reshape(())
    sg, sd, sm, sv = ([small[4 * i + j] for i in range(4)] for j in range(4))
    for lst in (sg, sd, sm, sv):
        lst[3] = lst[3].reshape(D)
    upd = [_adamw(w, g, m, v, f"adamw_{a}") for a, (w, g, m, v) in enumerate(zip(big_w, g_big, big_m, big_v))]
    bg = [g[None] for g in g_big]
    bd, bm, bv = ([u[j][None] for u in upd] for j in range(3))

    def order(s, b):
        return [s[0], b[0], s[1], s[2], b[1], b[2], b[3], s[3]]

    return (loss, loc["gx"][None], *order(sg, bg), *order(sd, bd), *order(sm, bm), *order(sv, bv))
```

```python
import functools

import jax
import jax.numpy as jnp
from jax import lax
from jax.experimental import pallas as pl
from jax.experimental.pallas import tpu as pltpu

T = 2048
D = 1024
NIN = 11264
HEADS = 8
HK = 128
CH = 16
NCH = T // CH
HSTEP = 2
ATT_GROUPS = ((128, 1), (512, 4), (2048, 16))
ATT_COL0 = 4096
AG_COL0 = 8704
GATE_COL0 = 9216
EPS = 1e-6
ROPE_THETA = 10000.0
LR, B1, B2, ADAM_EPS, WD, STEP = 0.001, 0.9, 0.999, 1e-08, 0.01, 10

F32 = jnp.float32
BF = jnp.bfloat16
VMEM_LIMIT = 56 * 1024 * 1024

_NN = (((1,), (0,)), ((), ()))
_NT = (((1,), (1,)), ((), ()))
_TN = (((0,), (0,)), ((), ()))


def _dot(a, b, dims=_NN):
    return lax.dot_general(a, b, dims, preferred_element_type=F32)


def _bdot(a, b, dims=_NN):
    return lax.dot_general(a.astype(BF), b.astype(BF), dims, preferred_element_type=F32)


def _sigmoid(x):
    return jax.nn.sigmoid(x)


def _params(sem=None):
    return pltpu.CompilerParams(dimension_semantics=sem, vmem_limit_bytes=VMEM_LIMIT)


def _matmul(a, b, *, ta=False, tb=False, out_dtype=F32, tm=512, tn=512, tk=None, name, side=None):
    m = a.shape[1] if ta else a.shape[0]
    kdim = a.shape[0] if ta else a.shape[1]
    n = b.shape[0] if tb else b.shape[1]
    tk = tk or kdim
    tm, tn = min(tm, m), min(tn, n)
    nm, nn, nk = m // tm, n // tn, kdim // tk
    dims = (((0 if ta else 1,), (1 if tb else 0,)), ((), ()))
    s_arrays, s_in_specs, s_shapes, s_out_specs, s_sems = _side_io(side)
    na, no = len(s_arrays), len(s_shapes)
    nacc = 1 if nk > 1 else 0

    def body(*refs):
        a_ref, b_ref = refs[:2]
        s_ins, o_ref, s_outs = refs[2:2 + na], refs[2 + na], refs[3 + na:3 + na + no]
        scratch = refs[3 + na + no:]
        s_sem_refs = scratch[nacc:]
        i, j, k = pl.program_id(0), pl.program_id(1), pl.program_id(2)
        if side is not None:
            @pl.when((i == 0) & (j == 0) & (k == 0))
            def _():
                side.first(s_ins, s_outs, s_sem_refs)

        prod = _bdot(a_ref[...], b_ref[...], dims)
        if nk == 1:
            o_ref[...] = prod.astype(out_dtype)
        else:
            acc = scratch[0]

            @pl.when(k == 0)
            def _():
                acc[...] = prod

            @pl.when(k > 0)
            def _():
                acc[...] += prod

            @pl.when(k == nk - 1)
            def _():
                o_ref[...] = acc[...].astype(out_dtype)

        if side is not None:
            @pl.when((i == nm - 1) & (j == nn - 1) & (k == nk - 1))
            def _():
                side.last(s_ins, s_outs, s_sem_refs)

    a_spec = pl.BlockSpec((tk, tm), lambda i, j, k: (k, i)) if ta else pl.BlockSpec((tm, tk), lambda i, j, k: (i, k))
    b_spec = pl.BlockSpec((tn, tk), lambda i, j, k: (j, k)) if tb else pl.BlockSpec((tk, tn), lambda i, j, k: (k, j))
    sem = ("parallel", "parallel", "arbitrary") if side is None else ("arbitrary",) * 3
    out = pl.pallas_call(
        body, name=name, grid=(nm, nn, nk),
        in_specs=[a_spec, b_spec] + s_in_specs,
        out_specs=[pl.BlockSpec((tm, tn), lambda i, j, k: (i, j))] + s_out_specs,
        out_shape=[jax.ShapeDtypeStruct((m, n), out_dtype)] + s_shapes,
        scratch_shapes=([pltpu.VMEM((tm, tn), F32)] if nk > 1 else []) + s_sems,
        compiler_params=_params(sem),
    )(a, b, *s_arrays)
    return out[0] if side is None else (out[0], out[1:])


DZ_TILE = 512


def _part_offsets(parts):
    counts = [p.shape[1] // DZ_TILE for p in parts]
    offs = [sum(counts[:i]) for i in range(len(parts))]
    return counts, offs


def _part_spec(rows, cnt, off, tile_axis):
    def index(*g):
        return (0 if rows is None else g[0], jnp.clip(g[tile_axis] - off, 0, cnt - 1))
    return index


def _grad_w_in(h, parts):
    counts, offs = _part_offsets(parts)
    n = len(parts)

    def body(h_ref, *refs):
        o_ref = refs[n]
        j = pl.program_id(0)
        for p_ref, cnt, off in zip(refs[:n], counts, offs):
            @pl.when((j >= off) & (j < off + cnt))
            def _(p_ref=p_ref):
                o_ref[...] = _bdot(h_ref[...], p_ref[...], _TN).astype(BF)

    return pl.pallas_call(
        body, name="g_win", grid=(sum(counts),),
        in_specs=[pl.BlockSpec((T, D), lambda j: (0, 0))] +
                 [pl.BlockSpec((T, DZ_TILE), _part_spec(None, c, o, 0)) for c, o in zip(counts, offs)],
        out_specs=pl.BlockSpec((D, DZ_TILE), lambda j: (0, j)),
        out_shape=jax.ShapeDtypeStruct((D, NIN), BF),
        compiler_params=_params(("parallel",)),
    )(h, *parts)


def _side_io(side):
    if side is None:
        return [], [], [], [], []
    return (side.arrays, [HBM] * len(side.arrays), side.out_shapes, [HBM] * len(side.out_shapes), side.sems)


def _grad_x(parts, w_in, x, dout, norm_w, side=None):
    counts, offs = _part_offsets(parts)
    n = len(parts)
    tm = 1024
    nm, nk = T // tm, sum(counts)
    s_arrays, s_in_specs, s_shapes, s_out_specs, s_sems = _side_io(side)
    na, no = len(s_arrays), len(s_shapes)

    def body(*refs):
        w_ref, x_ref, dout_ref, nw_ref = refs[n:n + 4]
        s_ins = refs[n + 4:n + 4 + na]
        gx_ref, gw_ref = refs[n + 4 + na:n + 6 + na]
        s_outs = refs[n + 6 + na:n + 6 + na + no]
        acc = refs[n + 6 + na + no]
        s_sem_refs = refs[n + 7 + na + no:]
        i, k = pl.program_id(0), pl.program_id(1)

        @pl.when((i == 0) & (k == 0))
        def _():
            gw_ref[...] = jnp.zeros_like(gw_ref)
            if side is not None:
                side.first(s_ins, s_outs, s_sem_refs)

        @pl.when(k == 0)
        def _():
            acc[...] = jnp.zeros_like(acc)

        for p_ref, cnt, off in zip(refs[:n], counts, offs):
            @pl.when((k >= off) & (k < off + cnt))
            def _(p_ref=p_ref):
                acc[...] += _bdot(p_ref[...], w_ref[...], _NT)

        @pl.when(k == nk - 1)
        def _():
            gw = jnp.zeros((1, D), F32)
            for c in range(tm // BLK):
                rows = pl.ds(BLK * c, BLK)
                xv, dhv = x_ref[rows, :], acc[rows, :]
                r = lax.rsqrt(jnp.mean(xv * xv, axis=-1, keepdims=True) + EPS)
                nrm = xv * r
                dn = dhv * nw_ref[...]
                gw = gw + jnp.sum(dhv * nrm, axis=0, keepdims=True)
                gx_ref[rows, :] = dout_ref[rows, :] + r * (dn - nrm * jnp.mean(dn * nrm, axis=-1, keepdims=True))
            gw_ref[...] += gw

        if side is not None:
            @pl.when((i == nm - 1) & (k == nk - 1))
            def _():
                side.last(s_ins, s_outs, s_sem_refs)

    row = pl.BlockSpec((tm, D), lambda i, k: (i, 0))
    vec = pl.BlockSpec((1, D), lambda i, k: (0, 0))
    out = pl.pallas_call(
        body, name="grad_x", grid=(nm, nk),
        in_specs=[pl.BlockSpec((tm, DZ_TILE), _part_spec(0, c, o, 1)) for c, o in zip(counts, offs)] +
                 [pl.BlockSpec((D, DZ_TILE), lambda i, k: (0, k)), row, row, vec] + s_in_specs,
        out_specs=[row, vec] + s_out_specs,
        out_shape=[jax.ShapeDtypeStruct((T, D), F32), jax.ShapeDtypeStruct((1, D), F32)] + s_shapes,
        scratch_shapes=[pltpu.VMEM((tm, D), F32)] + s_sems,
        compiler_params=_params(("arbitrary", "arbitrary")),
    )(*parts, w_in, x, dout, norm_w, *s_arrays)
    return out[0], out[1], out[2:]


def _rmsnorm_fwd(x, w):
    tm = 256

    def body(x_ref, w_ref, h_ref):
        xv = x_ref[...]
        r = lax.rsqrt(jnp.mean(xv * xv, axis=-1, keepdims=True) + EPS)
        h_ref[...] = (xv * r * w_ref[...]).astype(BF)

    return pl.pallas_call(
        body, name="rmsnorm_fwd", grid=(T // tm,),
        in_specs=[pl.BlockSpec((tm, D), lambda i: (i, 0)), pl.BlockSpec((1, D), lambda i: (0, 0))],
        out_specs=pl.BlockSpec((tm, D), lambda i: (i, 0)),
        out_shape=jax.ShapeDtypeStruct((T, D), BF),
        compiler_params=_params(("parallel",)),
    )(x, w)


def _lower_bound(lbl):
    mx = jnp.max(lbl, axis=0, keepdims=True)
    e = jnp.exp(lbl - mx)
    return e[0:1] / jnp.sum(e, axis=0, keepdims=True)


def _cumsum_rows(g, rows):
    b = g
    sh = 1
    while sh < CH:
        b = b + jnp.where(rows >= sh, pltpu.roll(b, sh, axis=0), 0.0)
        sh *= 2
    return b


def _rev_cumsum_rows(g, rows):
    b = g
    sh = 1
    while sh < CH:
        b = b + jnp.where(rows < CH - sh, pltpu.roll(b, CH - sh, axis=0), 0.0)
        sh *= 2
    return b


SUB = CH // 2


def _direct_block(qb, kb, vb, bb, rows8):
    ob = jnp.zeros_like(qb)
    for s in range(SUB):
        e_s = jnp.exp(jnp.where(rows8 >= s, bb - bb[s:s + 1], -jnp.inf))
        ob = ob + jnp.sum(qb * e_s * kb[s:s + 1], axis=1, keepdims=True) * vb[s:s + 1]
    return ob


def _direct_block_bwd(qb, kb, vb, bb, dob, rows8, rowc8):
    dq = dk = dv = db = jnp.zeros_like(qb)
    for s in range(SUB):
        one = (rowc8 == s).astype(F32)
        ks, vs = kb[s:s + 1], vb[s:s + 1]
        e_s = jnp.exp(jnp.where(rows8 >= s, bb - bb[s:s + 1], -jnp.inf))
        qes = qb * e_s
        w = qes * ks
        a = jnp.sum(w, axis=1, keepdims=True)
        da = jnp.sum(dob * vs, axis=1, keepdims=True)
        dv = dv + one * jnp.sum(a * dob, axis=0, keepdims=True)
        dq = dq + da * e_s * ks
        dk = dk + one * jnp.sum(da * qes, axis=0, keepdims=True)
        u = da * w
        db = db + u - one * jnp.sum(u, axis=0, keepdims=True)
    return dq, dk, dv, db


def _cross_factors(q, k, b):
    ref = b[SUB - 1:SUB]
    e_hi, e_lo = jnp.exp(b[SUB:] - ref), jnp.exp(ref - b[:SUB])
    return q[SUB:] * e_hi, k[:SUB] * e_lo, e_hi, e_lo


def _intra_fwd(q, k, v, b, rows8):
    lo = _direct_block(q[:SUB], k[:SUB], v[:SUB], b[:SUB], rows8)
    hi = _direct_block(q[SUB:], k[SUB:], v[SUB:], b[SUB:], rows8)
    qe_hi, ke_lo, _, _ = _cross_factors(q, k, b)
    for s in range(SUB):
        hi = hi + jnp.sum(qe_hi * ke_lo[s:s + 1], axis=1, keepdims=True) * v[s:s + 1]
    return jnp.concatenate([lo, hi], axis=0)


def _intra_bwd(q, k, v, b, do, rows8, rowc8):
    dq_lo, dk_lo, dv_lo, db_lo = _direct_block_bwd(q[:SUB], k[:SUB], v[:SUB], b[:SUB], do[:SUB], rows8, rowc8)
    dq_hi, dk_hi, dv_hi, db_hi = _direct_block_bwd(q[SUB:], k[SUB:], v[SUB:], b[SUB:], do[SUB:], rows8, rowc8)
    qe_hi, ke_lo, e_hi, e_lo = _cross_factors(q, k, b)
    do_hi, v_lo = do[SUB:], v[:SUB]
    dqe = dke = jnp.zeros_like(qe_hi)
    for s in range(SUB):
        one = (rowc8 == s).astype(F32)
        a = jnp.sum(qe_hi * ke_lo[s:s + 1], axis=1, keepdims=True)
        da = jnp.sum(do_hi * v_lo[s:s + 1], axis=1, keepdims=True)
        dv_lo = dv_lo + one * jnp.sum(a * do_hi, axis=0, keepdims=True)
        dqe = dqe + da * ke_lo[s:s + 1]
        dke = dke + one * jnp.sum(da * qe_hi, axis=0, keepdims=True)
    u_hi, u_lo = dqe * qe_hi, dke * ke_lo
    d_ref = jnp.sum(u_lo, axis=0, keepdims=True) - jnp.sum(u_hi, axis=0, keepdims=True)
    db_lo = db_lo - u_lo + (rowc8 == SUB - 1).astype(F32) * d_ref
    cat = lambda lo, hi: jnp.concatenate([lo, hi], axis=0)
    return (cat(dq_lo, dq_hi + dqe * e_hi), cat(dk_lo + dke * e_lo, dk_hi), cat(dv_lo, dv_hi),
            cat(db_lo, db_hi + u_hi))


def _hgrn_fwd(z, lbl, nw):
    def body(hq_ref, hf_ref, hi_ref, hg_ref, lbl_ref, nw_ref, oraw_ref, og_ref, sh_ref, st_ref):
        @pl.when(pl.program_id(0) == 0)
        def _():
            st_ref[...] = jnp.zeros_like(st_ref)

        lb_all = _lower_bound(lbl_ref[...])
        rows = lax.broadcasted_iota(jnp.int32, (CH, HK), 0)
        rows8 = lax.broadcasted_iota(jnp.int32, (SUB, HK), 0)
        nwv = nw_ref[...]
        for cc, h in [(cc, h) for cc in range(HSTEP) for h in range(HEADS)]:
            rs = slice(CH * cc, CH * (cc + 1))
            sl = slice(HK * h, HK * (h + 1))
            lb = lb_all[:, sl]
            hq, hf, v, hg = hq_ref[rs, sl], hf_ref[rs, sl], hi_ref[rs, sl], hg_ref[rs, sl]
            q = hq * _sigmoid(hq)
            f = lb + (1.0 - lb) * _sigmoid(hf)
            k = 1.0 - f
            b = _cumsum_rows(jnp.log(f), rows)
            sh_ref[cc, h] = st_ref[h]
            o = _bdot(q * jnp.exp(b), st_ref[h], _NT) + _intra_fwd(q, k, v, b, rows8)
            bl = b[CH - 1:CH]
            st_ref[h] = st_ref[h] * jnp.exp(bl)
            st_ref[h] += _bdot(v, k * jnp.exp(bl - b), _TN)
            oraw_ref[rs, sl] = o
            nrm = o * lax.rsqrt(jnp.mean(o * o, axis=1, keepdims=True) + EPS)
            og_ref[rs, sl] = (nrm * nwv * (hg * _sigmoid(hg))).astype(BF)

    zblk = lambda c: pl.BlockSpec((CH * HSTEP, D), lambda i, c=c: (i, c))
    return pl.pallas_call(
        body, name="hgrn_fwd", grid=(NCH // HSTEP,),
        in_specs=[zblk(0), zblk(1), zblk(2), zblk(3),
                  pl.BlockSpec((2, D), lambda i: (0, 0)), pl.BlockSpec((1, HK), lambda i: (0, 0))],
        out_specs=[zblk(0), zblk(0),
                   pl.BlockSpec((HSTEP, HEADS, HK, HK), lambda i: (i, 0, 0, 0))],
        out_shape=[jax.ShapeDtypeStruct((T, D), F32), jax.ShapeDtypeStruct((T, D), BF),
                   jax.ShapeDtypeStruct((NCH, HEADS, HK, HK), F32)],
        scratch_shapes=[pltpu.VMEM((HEADS, HK, HK), F32)],
        compiler_params=_params(("arbitrary",)),
    )(z, z, z, z, lbl, nw)


def _hgrn_bwd(z, lbl, nw, oraw, dog, shist, side=None):
    hstep = 1
    s_arrays, s_in_specs, s_shapes, s_out_specs, s_sems = _side_io(side)
    na, no = len(s_arrays), len(s_shapes)

    def body(*refs):
        hq_ref, hf_ref, hi_ref, hg_ref, lbl_ref, nw_ref, oraw_ref, dog_ref, sh_ref = refs[:9]
        s_ins = refs[9:9 + na]
        dz_ref, dlb_ref, dnw_ref = refs[9 + na:12 + na]
        s_outs = refs[12 + na:12 + na + no]
        dst_ref = refs[12 + na + no]
        s_sem_refs = refs[13 + na + no:]

        @pl.when(pl.program_id(0) == 0)
        def _():
            dst_ref[...] = jnp.zeros_like(dst_ref)
            dlb_ref[...] = jnp.zeros_like(dlb_ref)
            dnw_ref[...] = jnp.zeros_like(dnw_ref)
            if side is not None:
                side.first(s_ins, s_outs, s_sem_refs)

        lb_all = _lower_bound(lbl_ref[...])
        rows = lax.broadcasted_iota(jnp.int32, (CH, HK), 0)
        rowc = lax.broadcasted_iota(jnp.int32, (CH, 1), 0)
        rows8 = lax.broadcasted_iota(jnp.int32, (SUB, HK), 0)
        rowc8 = lax.broadcasted_iota(jnp.int32, (SUB, 1), 0)
        nwv = nw_ref[...]
        dnw = jnp.zeros((1, HK), F32)
        for cc, h in [(cc, h) for cc in reversed(range(hstep)) for h in range(HEADS)]:
            rs = slice(CH * cc, CH * (cc + 1))
            sl = slice(HK * h, HK * (h + 1))
            lb = lb_all[:, sl]
            hq, hf, v, hg = hq_ref[rs, sl], hf_ref[rs, sl], hi_ref[rs, sl], hg_ref[rs, sl]
            o, dg_out = oraw_ref[rs, sl], dog_ref[rs, sl]
            sg = _sigmoid(hg)
            sil = hg * sg
            r = lax.rsqrt(jnp.mean(o * o, axis=1, keepdims=True) + EPS)
            nrm = o * r
            d_hg = dg_out * (nrm * nwv) * (sg * (1.0 + hg * (1.0 - sg)))
            dn = dg_out * nwv * sil
            dnw = dnw + jnp.sum(dg_out * nrm * sil, axis=0, keepdims=True)
            do = r * (dn - nrm * jnp.mean(dn * nrm, axis=1, keepdims=True))
            sq = _sigmoid(hq)
            q = hq * sq
            sig = _sigmoid(hf)
            f = lb + (1.0 - lb) * sig
            k = 1.0 - f
            b = _cumsum_rows(jnp.log(f), rows)
            eb = jnp.exp(b)
            qe = q * eb
            bl = b[CH - 1:CH]
            ebl = jnp.exp(bl)
            kdec = jnp.exp(bl - b)
            ke = k * kdec
            dqe = _bdot(do, sh_ref[cc, h])
            dq = dqe * eb
            db = dqe * qe
            dke = _bdot(v, dst_ref[h])
            dv = _bdot(ke, dst_ref[h], _NT)
            dk = dke * kdec
            rr = dke * ke
            db = db - rr
            db_last = (jnp.sum(rr, axis=0, keepdims=True)
                       + ebl * jnp.sum(dst_ref[h] * sh_ref[cc, h], axis=0, keepdims=True))
            dst_ref[h] = dst_ref[h] * ebl
            dst_ref[h] += _bdot(do, qe, _TN)
            dq_i, dk_i, dv_i, db_i = _intra_bwd(q, k, v, b, do, rows8, rowc8)
            dq, dk, dv = dq + dq_i, dk + dk_i, dv + dv_i
            db = db + db_i + (rowc == CH - 1).astype(F32) * db_last
            dgl = _rev_cumsum_rows(db, rows)
            df = dgl / f - dk
            dlb_ref[:, sl] += jnp.sum(df * (1.0 - sig), axis=0, keepdims=True)
            dz_ref[rs, sl] = (dq * (sq * (1.0 + hq * (1.0 - sq)))).astype(BF)
            dz_ref[rs, D + HK * h:D + HK * (h + 1)] = (df * (1.0 - lb) * sig * (1.0 - sig)).astype(BF)
            dz_ref[rs, 2 * D + HK * h:2 * D + HK * (h + 1)] = dv.astype(BF)
            dz_ref[rs, 3 * D + HK * h:3 * D + HK * (h + 1)] = d_hg.astype(BF)
        dnw_ref[...] += dnw
        if side is not None:
            @pl.when(pl.program_id(0) == NCH // hstep - 1)
            def _():
                side.last(s_ins, s_outs, s_sem_refs)

    rev = lambda i: NCH // hstep - 1 - i
    zblk = lambda c: pl.BlockSpec((CH * hstep, D), lambda i, c=c: (rev(i), c))
    out = pl.pallas_call(
        body, name="hgrn_bwd", grid=(NCH // hstep,),
        in_specs=[zblk(0), zblk(1), zblk(2), zblk(3),
                  pl.BlockSpec((2, D), lambda i: (0, 0)), pl.BlockSpec((1, HK), lambda i: (0, 0)),
                  zblk(0), zblk(0),
                  pl.BlockSpec((hstep, HEADS, HK, HK), lambda i: (rev(i), 0, 0, 0))] + s_in_specs,
        out_specs=[pl.BlockSpec((CH * hstep, 4 * D), lambda i: (rev(i), 0)),
                   pl.BlockSpec((1, D), lambda i: (0, 0)), pl.BlockSpec((1, HK), lambda i: (0, 0))] + s_out_specs,
        out_shape=[jax.ShapeDtypeStruct((T, 4 * D), BF), jax.ShapeDtypeStruct((1, D), F32),
                   jax.ShapeDtypeStruct((1, HK), F32)] + s_shapes,
        scratch_shapes=[pltpu.VMEM((HEADS, HK, HK), F32)] + s_sems,
        compiler_params=_params(("arbitrary",)),
    )(z, z, z, z, lbl, nw, oraw, dog, shist, *s_arrays)
    return out[0], out[1], out[2], out[3:]


BLK = 128
NBLK = T // BLK
QK_SCALE = 0.125


def _head_masks():
    lane = lax.broadcasted_iota(jnp.int32, (1, BLK), 1)
    return [(lane < 64).astype(F32), (lane >= 64).astype(F32)]


def _pieces(dil):
    m = T // dil
    out = []
    for r in range(dil):
        for j in range(m // BLK):
            start = r + dil * BLK * j
            rows = pl.ds(start, BLK, stride=dil) if dil > 1 else pl.ds(start, BLK)
            out.append((rows, r * m + BLK * j))
    return out


def _rope_tables(pos, invf):
    tm = 256

    def body(pos_ref, invf_ref, cos_ref, sa_ref, sb_ref):
        first = (lax.broadcasted_iota(jnp.int32, (tm, BLK), 1) % 64) < 32
        ang = pos_ref[...].astype(F32) * invf_ref[...]
        s = jnp.sin(ang)
        cos_ref[...] = jnp.cos(ang)
        sa_ref[...] = jnp.where(first, -s, 0.0)
        sb_ref[...] = jnp.where(first, 0.0, s)

    row = pl.BlockSpec((tm, BLK), lambda i: (i, 0))
    return pl.pallas_call(
        body, name="rope_tables", grid=(T // tm,),
        in_specs=[pl.BlockSpec((tm, 1), lambda i: (i, 0)), pl.BlockSpec((1, BLK), lambda i: (0, 0))],
        out_specs=[row, row, row], out_shape=[jax.ShapeDtypeStruct((T, BLK), F32)] * 3,
        compiler_params=_params(("parallel",)),
    )(pos, invf)


def _rope(x, c, sa, sb):
    return x * c + pltpu.roll(x, 96, axis=1) * sa + pltpu.roll(x, 32, axis=1) * sb


def _rope_t(d, c, sa, sb):
    return d * c + pltpu.roll(d * sa, 32, axis=1) + pltpu.roll(d * sb, 96, axis=1)


def _rope_and_regroup(dil, q_ref, k_ref, v_ref, tables, stage_q, stage_k, qr_ref, kr_ref, vr_ref):
    cos_ref, sa_ref, sb_ref = tables
    to_q, to_k = (qr_ref, kr_ref) if dil == 1 else (stage_q, stage_k)
    for c in range(T // BLK):
        rows = pl.ds(BLK * c, BLK)
        cs, sa, sb = cos_ref[rows, :], sa_ref[rows, :], sb_ref[rows, :]
        to_q[rows, :] = _rope(q_ref[rows, :], cs, sa, sb) * QK_SCALE
        to_k[rows, :] = _rope(k_ref[rows, :], cs, sa, sb)
    for rows, dst in _pieces(dil):
        drows = pl.ds(dst, BLK)
        if dil > 1:
            qr_ref[drows, :] = stage_q[rows, :]
            kr_ref[drows, :] = stage_k[rows, :]
        vr_ref[drows, :] = v_ref[rows, :]


def _window_bias(bias_ref):
    ii = lax.broadcasted_iota(jnp.int32, (2 * BLK, BLK), 0) % BLK
    jj = lax.broadcasted_iota(jnp.int32, (2 * BLK, BLK), 1)
    bias_ref[0] = jnp.where(jj <= ii, 0.0, -jnp.inf)
    bias_ref[1] = jnp.where(jj >= ii, 0.0, -jnp.inf)


def _blocks(bi):
    if isinstance(bi, int):
        return pl.ds(bi * BLK, BLK), pl.ds(max(bi - 1, 0) * BLK, BLK)
    return (pl.ds(pl.multiple_of(bi * BLK, BLK), BLK),
            pl.ds(pl.multiple_of(jnp.maximum(bi - 1, 0) * BLK, BLK), BLK))


def _stack_heads(x, masks):
    return jnp.concatenate([x * masks[0], x * masks[1]], axis=0).astype(BF)


def _attn_fwd(z, cos, sa, sb):
    def body(q_ref, k_ref, v_ref, ag_ref, cos_ref, sa_ref, sb_ref, ob_ref, opre_ref, lse_ref,
             bias_ref, qr_ref, kr_ref, vr_ref, og_ref, lg_ref, otok_ref, ltok_ref, sc_ref):
        g = pl.program_id(1)
        masks = _head_masks()

        @pl.when(g == 0)
        def _():
            _window_bias(bias_ref)

        def group(gi):
            dil = ATT_GROUPS[gi][1]
            nblk = (T // dil) // BLK
            _rope_and_regroup(dil, q_ref, k_ref, v_ref, (cos_ref, sa_ref, sb_ref), lg_ref.at[0], lg_ref.at[1],
                              qr_ref, kr_ref, vr_ref)

            def scores(bi, slot):
                cur, prev = _blocks(bi)
                q2 = _stack_heads(qr_ref[cur, :], masks)
                sc_ref[slot, 0] = _dot(q2, kr_ref[cur, :].astype(BF), _NT) + bias_ref[0]
                if nblk > 1:
                    sc_ref[slot, 1] = (_dot(q2, kr_ref[prev, :].astype(BF), _NT)
                                       + (bias_ref[1] + jnp.where((bi % nblk) != 0, 0.0, -jnp.inf)))

            def finish(bi, slot):
                cur, prev = _blocks(bi)
                s_c, vc = sc_ref[slot, 0], vr_ref[cur, :].astype(BF)
                if nblk > 1:
                    s_p, vp = sc_ref[slot, 1], vr_ref[prev, :].astype(BF)
                    mx = jnp.max(jnp.maximum(s_c, s_p), axis=1, keepdims=True)
                    p_c, p_p = jnp.exp(s_c - mx), jnp.exp(s_p - mx)
                    den = jnp.sum(p_c + p_p, axis=1, keepdims=True)
                    oh = _dot(p_c.astype(BF), vc) + _dot(p_p.astype(BF), vp)
                else:
                    mx = jnp.max(s_c, axis=1, keepdims=True)
                    p_c = jnp.exp(s_c - mx)
                    den = jnp.sum(p_c, axis=1, keepdims=True)
                    oh = _dot(p_c.astype(BF), vc)
                on = oh / den
                lsev = jnp.broadcast_to(mx + jnp.log(den), (2 * BLK, BLK))
                og_ref[cur, :] = on[:BLK] * masks[0] + on[BLK:] * masks[1]
                lg_ref[0, cur, :] = lsev[:BLK]
                lg_ref[1, cur, :] = lsev[BLK:]

            def pair(j, carry):
                finish(2 * j, 0)
                scores(2 * j + 1, 1)
                finish(2 * j + 1, 1)
                scores(jnp.minimum(2 * j + 2, NBLK - 1), 0)
                return carry

            scores(0, 0)
            lax.fori_loop(0, NBLK // 2, pair, 0)
            for rows, src in _pieces(dil):
                srows = pl.ds(src, BLK)
                otok_ref[gi, rows, :] = og_ref[srows, :]
                ltok_ref[gi, 0, rows, :] = lg_ref[0, srows, :]
                ltok_ref[gi, 1, rows, :] = lg_ref[1, srows, :]

        for gi in range(3):
            pl.when(g == gi)(functools.partial(group, gi))

        @pl.when(g == 2)
        def _():
            for c in range(T // BLK):
                rows = pl.ds(BLK * c, BLK)
                wts = []
                for hh in range(2):
                    l0, l1, l2 = ltok_ref[0, hh, rows, :], ltok_ref[1, hh, rows, :], ltok_ref[2, hh, rows, :]
                    mx = jnp.maximum(jnp.maximum(l0, l1), l2)
                    e0, e1, e2 = jnp.exp(l0 - mx), jnp.exp(l1 - mx), jnp.exp(l2 - mx)
                    tot = e0 + e1 + e2
                    lse_ref[rows, BLK * hh:BLK * (hh + 1)] = mx + jnp.log(tot)
                    inv = 1.0 / tot
                    wts.append([e0 * inv, e1 * inv, e2 * inv])
                o = sum((wts[0][gi] * masks[0] + wts[1][gi] * masks[1]) * otok_ref[gi, rows, :] for gi in range(3))
                ag = ag_ref[rows, :]
                opre_ref[rows, :] = o
                ob_ref[rows, :] = (o * (ag * _sigmoid(ag))).astype(BF)

    c0 = ATT_COL0 // BLK
    zspec = lambda part: pl.BlockSpec((T, BLK), lambda p, g, part=part: (0, c0 + 12 * part + 4 * g + p))
    outspec = pl.BlockSpec((T, BLK), lambda p, g: (0, p))
    table = pl.BlockSpec((T, BLK), lambda p, g: (0, 0))
    big = lambda: pltpu.VMEM((T, BLK), F32)
    return pl.pallas_call(
        body, name="attn_fwd", grid=(4, 3),
        in_specs=[zspec(0), zspec(1), zspec(2),
                  pl.BlockSpec((T, BLK), lambda p, g: (0, AG_COL0 // BLK + p)), table, table, table],
        out_specs=[outspec, outspec, pl.BlockSpec((T, 2 * BLK), lambda p, g: (0, p))],
        out_shape=[jax.ShapeDtypeStruct((T, 512), BF), jax.ShapeDtypeStruct((T, 512), F32),
                   jax.ShapeDtypeStruct((T, 8 * BLK), F32)],
        scratch_shapes=[pltpu.VMEM((2, 2 * BLK, BLK), F32), big(), big(), big(), big(),
                        pltpu.VMEM((2, T, BLK), F32), pltpu.VMEM((3, T, BLK), F32), pltpu.VMEM((3, 2, T, BLK), F32),
                        pltpu.VMEM((2, 2, 2 * BLK, BLK), F32)],
        compiler_params=_params(("parallel", "arbitrary")),
    )(z, z, z, z, cos, sa, sb)


def _attn_bwd(z, cos, sa, sb, opre, lse, dob):
    def body(q_ref, k_ref, v_ref, ag_ref, cos_ref, sa_ref, sb_ref, o_ref, lse0_ref, lse1_ref, dob_ref,
             dq_ref, dk_ref, dv_ref, dag_ref,
             bias_ref, dtok_ref, qr_ref, kr_ref, vr_ref, dor_ref, lr_ref, dr_ref,
             dqr_ref, dkr_ref, dvr_ref, pd_ref, dotok_ref):
        g = pl.program_id(1)
        masks = _head_masks()

        @pl.when(g == 0)
        def _():
            _window_bias(bias_ref)
            for c in range(T // BLK):
                rows = pl.ds(BLK * c, BLK)
                ag, dob_v, o = ag_ref[rows, :], dob_ref[rows, :], o_ref[rows, :]
                sg = _sigmoid(ag)
                dag_ref[rows, :] = (dob_v * o * (sg * (1.0 + ag * (1.0 - sg)))).astype(BF)
                do = dob_v * (ag * sg)
                dotok_ref[rows, :] = do
                prod = do * o
                for hh, mh in enumerate(masks):
                    dtok_ref[hh, rows, :] = jnp.broadcast_to(jnp.sum(prod * mh, axis=1, keepdims=True), (BLK, BLK))

        def group(gi):
            dil = ATT_GROUPS[gi][1]
            nblk = (T // dil) // BLK
            _rope_and_regroup(dil, q_ref, k_ref, v_ref, (cos_ref, sa_ref, sb_ref), dqr_ref, dvr_ref,
                              qr_ref, kr_ref, vr_ref)
            for rows, dst in _pieces(dil):
                drows = pl.ds(dst, BLK)
                dor_ref[drows, :] = dotok_ref[rows, :]
                for hh, lse_ref in enumerate((lse0_ref, lse1_ref)):
                    lr_ref[hh, drows, :] = lse_ref[rows, :]
                    dr_ref[hh, drows, :] = dtok_ref[hh, rows, :]
            dkr_ref[...] = jnp.zeros_like(dkr_ref)
            dvr_ref[...] = jnp.zeros_like(dvr_ref)

            def probs(bi, slot):
                cur, prev = _blocks(bi)
                q2, do2 = _stack_heads(qr_ref[cur, :], masks), _stack_heads(dor_ref[cur, :], masks)
                lh = jnp.concatenate([lr_ref[0, cur, :], lr_ref[1, cur, :]], axis=0)
                dh = jnp.concatenate([dr_ref[0, cur, :], dr_ref[1, cur, :]], axis=0)
                p_c = jnp.exp(_dot(q2, kr_ref[cur, :].astype(BF), _NT) + bias_ref[0] - lh)
                pd_ref[slot, 0] = p_c.astype(BF)
                pd_ref[slot, 1] = (p_c * (_dot(do2, vr_ref[cur, :].astype(BF), _NT) - dh)).astype(BF)
                if nblk > 1:
                    bias_p = bias_ref[1] + jnp.where((bi % nblk) != 0, 0.0, -jnp.inf)
                    p_p = jnp.exp(_dot(q2, kr_ref[prev, :].astype(BF), _NT) + bias_p - lh)
                    pd_ref[slot, 2] = p_p.astype(BF)
                    pd_ref[slot, 3] = (p_p * (_dot(do2, vr_ref[prev, :].astype(BF), _NT) - dh)).astype(BF)

            def grads(bi, slot):
                cur, prev = _blocks(bi)
                q2, do2 = _stack_heads(qr_ref[cur, :], masks), _stack_heads(dor_ref[cur, :], masks)
                p_c, ds_c = pd_ref[slot, 0], pd_ref[slot, 1]
                dq2 = _dot(ds_c, kr_ref[cur, :].astype(BF))
                dkr_ref[cur, :] += _dot(ds_c, q2, _TN)
                dvr_ref[cur, :] += _dot(p_c, do2, _TN)
                if nblk > 1:
                    p_p, ds_p = pd_ref[slot, 2], pd_ref[slot, 3]
                    dq2 = dq2 + _dot(ds_p, kr_ref[prev, :].astype(BF))
                    dkr_ref[prev, :] += _dot(ds_p, q2, _TN)
                    dvr_ref[prev, :] += _dot(p_p, do2, _TN)
                dqr_ref[cur, :] = dq2[:BLK] * masks[0] + dq2[BLK:] * masks[1]

            def pair(j, carry):
                grads(2 * j, 0)
                probs(2 * j + 1, 1)
                grads(2 * j + 1, 1)
                probs(jnp.minimum(2 * j + 2, NBLK - 1), 0)
                return carry

            probs(0, 0)
            lax.fori_loop(0, NBLK // 2, pair, 0)
            if dil > 1:
                for rows, src in _pieces(dil):
                    srows = pl.ds(src, BLK)
                    qr_ref[rows, :] = dqr_ref[srows, :]
                    kr_ref[rows, :] = dkr_ref[srows, :]
                    vr_ref[rows, :] = dvr_ref[srows, :]
            tq, tk, tv = (qr_ref, kr_ref, vr_ref) if dil > 1 else (dqr_ref, dkr_ref, dvr_ref)
            for c in range(T // BLK):
                rows = pl.ds(BLK * c, BLK)
                cs, sa, sb = cos_ref[rows, :], sa_ref[rows, :], sb_ref[rows, :]
                dq_ref[rows, :] = _rope_t(tq[rows, :] * QK_SCALE, cs, sa, sb).astype(BF)
                dk_ref[rows, :] = _rope_t(tk[rows, :], cs, sa, sb).astype(BF)
                dv_ref[rows, :] = tv[rows, :].astype(BF)

        for gi in range(3):
            pl.when(g == gi)(functools.partial(group, gi))

    c0 = ATT_COL0 // BLK
    zspec = lambda part: pl.BlockSpec((T, BLK), lambda p, g, part=part: (0, c0 + 12 * part + 4 * g + p))
    pspec = pl.BlockSpec((T, BLK), lambda p, g: (0, p))
    gspec = pl.BlockSpec((T, BLK), lambda p, g: (0, 4 * g + p))
    table = pl.BlockSpec((T, BLK), lambda p, g: (0, 0))
    big = lambda: pltpu.VMEM((T, BLK), F32)
    two = lambda: pltpu.VMEM((2, T, BLK), F32)
    return pl.pallas_call(
        body, name="attn_bwd", grid=(4, 3),
        in_specs=[zspec(0), zspec(1), zspec(2),
                  pl.BlockSpec((T, BLK), lambda p, g: (0, AG_COL0 // BLK + p)), table, table, table,
                  pspec, pl.BlockSpec((T, BLK), lambda p, g: (0, 2 * p)),
                  pl.BlockSpec((T, BLK), lambda p, g: (0, 2 * p + 1)), pspec],
        out_specs=[gspec, gspec, gspec, pspec],
        out_shape=[jax.ShapeDtypeStruct((T, 1536), BF), jax.ShapeDtypeStruct((T, 1536), BF),
                   jax.ShapeDtypeStruct((T, 1536), BF), jax.ShapeDtypeStruct((T, 512), BF)],
        scratch_shapes=[pltpu.VMEM((2, 2 * BLK, BLK), F32), two(), big(), big(), big(), big(),
                        two(), two(), big(), big(), big(), pltpu.VMEM((2, 4, 2 * BLK, BLK), BF), big()],
        compiler_params=_params(("parallel", "arbitrary")),
    )(z, z, z, z, cos, sa, sb, opre, lse, lse, dob)


def _merge_fwd(ya, yb, z):
    tm = 256

    def body(ya_ref, yb_ref, ga_ref, gb_ref, m_ref):
        m_ref[...] = (_sigmoid(ga_ref[...]) * ya_ref[...] + _sigmoid(gb_ref[...]) * yb_ref[...]).astype(BF)

    row = pl.BlockSpec((tm, D), lambda i: (i, 0))
    return pl.pallas_call(
        body, name="merge_fwd", grid=(T // tm,),
        in_specs=[row, row, pl.BlockSpec((tm, D), lambda i: (i, GATE_COL0 // D)),
                  pl.BlockSpec((tm, D), lambda i: (i, GATE_COL0 // D + 1))],
        out_specs=row, out_shape=jax.ShapeDtypeStruct((T, D), BF),
        compiler_params=_params(("parallel",)),
    )(ya, yb, z, z)


def _out_loss(merged, w_out, x, tgt, wf):
    tm = 256

    def body(m_ref, w_ref, x_ref, t_ref, wf_ref, dout_ref, loss_ref, gwf_ref):
        @pl.when(pl.program_id(0) == 0)
        def _():
            loss_ref[...] = jnp.zeros_like(loss_ref)
            gwf_ref[...] = jnp.zeros_like(gwf_ref)

        out = x_ref[...] + _dot(m_ref[...], w_ref[...])
        r = lax.rsqrt(jnp.mean(out * out, axis=-1, keepdims=True) + EPS)
        yh = out * r
        wfv = wf_ref[...]
        err = yh * wfv - t_ref[...]
        loss_ref[...] += jnp.sum(err * err, axis=0, keepdims=True) * (0.5 / D)
        dy = err * (1.0 / D)
        gwf_ref[...] += jnp.sum(dy * yh, axis=0, keepdims=True)
        dyh = dy * wfv
        dout_ref[...] = r * (dyh - yh * jnp.mean(dyh * yh, axis=-1, keepdims=True))

    row = pl.BlockSpec((tm, D), lambda i: (i, 0))
    vec = pl.BlockSpec((1, D), lambda i: (0, 0))
    return pl.pallas_call(
        body, name="out_loss", grid=(T // tm,),
        in_specs=[row, pl.BlockSpec((D, D), lambda i: (0, 0)), row, row, vec],
        out_specs=[row, vec, vec],
        out_shape=[jax.ShapeDtypeStruct((T, D), F32), jax.ShapeDtypeStruct((1, D), F32),
                   jax.ShapeDtypeStruct((1, D), F32)],
        compiler_params=_params(("arbitrary",)),
    )(merged, w_out, x, tgt, wf)


def _merge_bwd(dm, ya, yb, z):
    tm = 256

    def body(dm_ref, ya_ref, yb_ref, ga_ref, gb_ref, dya_ref, dyb_ref, dg_ref):
        dmv = dm_ref[...]
        sa, sb = _sigmoid(ga_ref[...]), _sigmoid(gb_ref[...])
        dya_ref[...] = (sa * dmv).astype(BF)
        dyb_ref[...] = (sb * dmv).astype(BF)
        dg_ref[:, :D] = (dmv * ya_ref[...] * sa * (1.0 - sa)).astype(BF)
        dg_ref[:, D:] = (dmv * yb_ref[...] * sb * (1.0 - sb)).astype(BF)

    row = pl.BlockSpec((tm, D), lambda i: (i, 0))
    return pl.pallas_call(
        body, name="merge_bwd", grid=(T // tm,),
        in_specs=[row, row, row, pl.BlockSpec((tm, D), lambda i: (i, GATE_COL0 // D)),
                  pl.BlockSpec((tm, D), lambda i: (i, GATE_COL0 // D + 1))],
        out_specs=[row, row, pl.BlockSpec((tm, 2 * D), lambda i: (i, 0))],
        out_shape=[jax.ShapeDtypeStruct((T, D), BF), jax.ShapeDtypeStruct((T, D), BF),
                   jax.ShapeDtypeStruct((T, 2 * D), BF)],
        compiler_params=_params(("parallel",)),
    )(dm, ya, yb, z, z)


def _rope_inv_freq():
    inv = ROPE_THETA ** (-jnp.arange(0, 64, 2, dtype=F32) / 64)
    return jnp.tile(inv, 4).reshape(1, BLK)


def _local_step(x, pos, norm_w, lbl, hnw, wf, tgt, w_in, w_a, w_b, w_out, shard_shapes=()):
    invf = _rope_inv_freq()
    h = _rmsnorm_fwd(x, norm_w)
    if shard_shapes:
        z, (w_a, w_b, w_out) = _matmul(h, w_in, tm=T, tn=512, name="z_proj",
                                       side=_gather_side([w_a, w_b, w_out], WEIGHT_AXES[1:]))
    else:
        z = _matmul(h, w_in, tm=T, tn=512, name="z_proj")
    oraw, og, shist = _hgrn_fwd(z, lbl, hnw)
    cos, sa, sb = _rope_tables(pos, invf)
    ob, opre, lse = _attn_fwd(z, cos, sa, sb)
    ya = _matmul(og, w_a, tm=1024, tn=512, name="ya_proj")
    yb = _matmul(ob, w_b, tm=1024, tn=512, name="yb_proj")
    merged = _merge_fwd(ya, yb, z)
    dout, loss_vec, g_wf = _out_loss(merged, w_out, x, tgt, wf)

    dm = _matmul(dout, w_out, tb=True, tm=1024, tn=512, name="d_merged")
    g_wout = _matmul(merged, dout, ta=True, out_dtype=BF, tm=512, tn=1024, name="g_wout")
    dya, dyb, dgates = _merge_bwd(dm, ya, yb, z)
    dog = _matmul(dya, w_a, tb=True, tm=1024, tn=512, name="d_og")
    g_wa = _matmul(og, dya, ta=True, out_dtype=BF, tm=512, tn=1024, name="g_wa")
    dob = _matmul(dyb, w_b, tb=True, tm=1024, tn=512, name="d_ob")
    g_wb = _matmul(ob, dyb, ta=True, out_dtype=BF, tm=512, tn=1024, name="g_wb")
    small = [g_wa, g_wb, g_wout]
    side_s = side_w = None
    if shard_shapes:
        p3_s = _rs_partials(small, shard_shapes[1:], WEIGHT_AXES[1:], "small")
        side_s = _chip_exchange_side(p3_s, shard_shapes[1:], WEIGHT_AXES[1:])
    dz_h, dlb, g_hnw, land_s = _hgrn_bwd(z, lbl, hnw, oraw, dog, shist, side=side_s)
    dq, dk, dv, dag = _attn_bwd(z, cos, sa, sb, opre, lse, dob)
    dz_parts = [dz_h, dq, dk, dv, dag, dgates]
    g_big = [_grad_w_in(h, dz_parts)] + small
    if shard_shapes:
        p3_w = _rs_partials(g_big[:1], shard_shapes[:1], WEIGHT_AXES[:1], "w_in")
        side_w = _chip_exchange_side(p3_w, shard_shapes[:1], WEIGHT_AXES[:1])
    gx, g_nw, land_w = _grad_x(dz_parts, w_in, x, dout, norm_w, side=side_w)
    if shard_shapes:
        g_big = _rs_finish(p3_w + p3_s, list(land_w) + list(land_s), shard_shapes, WEIGHT_AXES)
    return dict(loss_vec=loss_vec, gx=gx, g_nw=g_nw, dlb=dlb, g_hnw=g_hnw, g_wf=g_wf,
                g_win=g_big[0], g_wa=g_big[1], g_wb=g_big[2], g_wout=g_big[3])


MESH = pl.DeviceIdType.MESH
HBM = pl.BlockSpec(memory_space=pl.ANY)
WEIGHT_AXES = (1, 0, 1, 0)


def _place():
    x, y, c = lax.axis_index("x"), lax.axis_index("y"), lax.axis_index("c")
    chips = [(1 - x, y), (x, 1 - y), (1 - x, 1 - y)]
    return x, y, c, chips


def _block_half(ref, shard_shape, axis, j, half):
    r, c = shard_shape
    hr = r // 2
    if axis == 0:
        return ref.at[pl.ds(pl.multiple_of(j * r + half * hr, 16), hr), :]
    return ref.at[pl.ds(pl.multiple_of(half * hr, 16), hr), pl.ds(pl.multiple_of(j * c, 128), c)]


class _Side:
    def __init__(self, arrays, out_shapes, sems, first, last):
        self.arrays, self.out_shapes, self.sems, self.first, self.last = arrays, out_shapes, sems, first, last


def _gather_side(shards, axes):
    n = len(shards)
    shapes = [s.shape for s in shards]

    def copies(ins, outs, sems):
        send1, recv1, send2, recv2, send0, recv0 = sems
        x, y, c, chips = _place()
        me = 2 * x + y
        sib = (x, y, 1 - c)
        near = ((1 - c) * (1 - x) + c * x, (1 - c) * y + c * (1 - y))
        far = ((1 - c) * x + c * (1 - x), (1 - c) * (1 - y) + c * y)
        out = []
        for a in range(n):
            r, cc = shapes[a]
            mine = (outs[a].at[pl.ds(pl.multiple_of(me * r, 16), r), :] if axes[a] == 0
                    else outs[a].at[:, pl.ds(pl.multiple_of(me * cc, 128), cc)])
            own = pltpu.make_async_remote_copy(
                src_ref=ins[a], dst_ref=mine, send_sem=send0.at[a], recv_sem=recv0.at[a],
                device_id=sib, device_id_type=MESH)
            src = ins[a].at[pl.ds(pl.multiple_of(c * (r // 2), 16), r // 2), :]
            sends = [pltpu.make_async_remote_copy(
                src_ref=src, dst_ref=_block_half(outs[a], shapes[a], axes[a], me, c),
                send_sem=send1.at[a, k], recv_sem=recv1.at[a, k], device_id=(*chips[k], c), device_id_type=MESH)
                for k in range(2)]

            def region(chip, half):
                return _block_half(outs[a], shapes[a], axes[a], 2 * chip[0] + chip[1], half)

            def arrival(chip, k):
                reg = region(chip, c)
                return pltpu.make_async_remote_copy(
                    src_ref=reg, dst_ref=reg, send_sem=send1.at[a, k], recv_sem=recv1.at[a, k],
                    device_id=(*chip, c), device_id_type=MESH)

            def to_sibling(chip, k):
                reg = region(chip, c)
                return pltpu.make_async_remote_copy(
                    src_ref=reg, dst_ref=reg, send_sem=send2.at[a, k], recv_sem=recv2.at[a, k],
                    device_id=sib, device_id_type=MESH)

            def from_sibling(chip, k):
                reg = region(chip, 1 - c)
                return pltpu.make_async_remote_copy(
                    src_ref=reg, dst_ref=reg, send_sem=send2.at[a, k], recv_sem=recv2.at[a, k],
                    device_id=sib, device_id_type=MESH)

            relay = pltpu.make_async_remote_copy(
                src_ref=region(near, c), dst_ref=region(near, c), send_sem=send1.at[a, 2], recv_sem=recv1.at[a, 2],
                device_id=(*far, c), device_id_type=MESH)
            hops = [(arrival(near, c), to_sibling(near, c)), (arrival(far, 1 - c), to_sibling(far, 1 - c)),
                    (arrival(chips[2], 2), to_sibling(chips[2], 2))]
            back = [from_sibling(chips[k], k) for k in range(3)]
            out.append((own, sends, relay, hops, back))
        return out

    def first(ins, outs, sems):
        for own, sends, _, _, _ in copies(ins, outs, sems):
            own.start()
            for cp in sends:
                cp.start()

    def last(ins, outs, sems):
        per_array = copies(ins, outs, sems)
        for step in range(3):
            for _, _, relay, hops, _ in per_array:
                arrived, onward = hops[step]
                arrived.wait_recv()
                if step == 0:
                    relay.start()
                onward.start()
        for own, sends, relay, hops, back in per_array:
            for cp in back:
                cp.wait_recv()
            for cp in sends + [relay] + [onward for _, onward in hops]:
                cp.wait_send()
            own.wait()

    full = [(4 * r, c) if ax == 0 else (r, 4 * c) for (r, c), ax in zip(shapes, axes)]
    sems = [pltpu.SemaphoreType.DMA((n, 3)), pltpu.SemaphoreType.DMA((n, 3)),
            pltpu.SemaphoreType.DMA((n, 3)), pltpu.SemaphoreType.DMA((n, 3)),
            pltpu.SemaphoreType.DMA((n,)), pltpu.SemaphoreType.DMA((n,))]
    return _Side(list(shards), [jax.ShapeDtypeStruct(f, BF) for f in full], sems, first, last)


def _run_side(side, name):
    na, no = len(side.arrays), len(side.out_shapes)

    def body(*refs):
        ins, outs, sems = refs[:na], refs[na:na + no], refs[na + no:]
        side.first(ins, outs, sems)
        side.last(ins, outs, sems)

    return pl.pallas_call(
        body, name=name, in_specs=[HBM] * na, out_specs=[HBM] * no,
        out_shape=side.out_shapes, scratch_shapes=side.sems,
    )(*side.arrays)


def _as3d(g, shard_shape, axis):
    r, c = shard_shape
    return g.reshape(4, r, c) if axis == 0 else g.reshape(1, r, 4 * c)


def _half_rows(ref3, hr, half):
    return ref3.at[:, pl.ds(pl.multiple_of(half * hr, 16), hr), :]


def _rs_pair_exchange(g3s, name):
    n = len(g3s)

    def body(*refs):
        ins, outs = refs[:n], refs[n:2 * n]
        send, recv = refs[2 * n:]
        x, y, c, _ = _place()
        cps = []
        for a in range(n):
            hr = g3s[a].shape[1] // 2
            cp = pltpu.make_async_remote_copy(
                src_ref=_half_rows(ins[a], hr, 1 - c), dst_ref=outs[a],
                send_sem=send.at[a], recv_sem=recv.at[a], device_id=(x, y, 1 - c), device_id_type=MESH)
            cp.start()
            cps.append(cp)
        for cp in cps:
            cp.wait()

    return pl.pallas_call(
        body, name=name,
        in_specs=[HBM] * n, out_specs=[HBM] * n,
        out_shape=[jax.ShapeDtypeStruct((g.shape[0], g.shape[1] // 2, g.shape[2]), BF) for g in g3s],
        scratch_shapes=[pltpu.SemaphoreType.DMA((n,)), pltpu.SemaphoreType.DMA((n,))],
    )(*g3s)


def _pair_sum(g3, land, cidx, name):
    nb, r, w = g3.shape
    hr = r // 2
    tr = 64

    def body(c_ref, g_ref, l_ref, o_ref):
        o_ref[...] = (g_ref[...].astype(F32) + l_ref[...].astype(F32)).astype(BF)

    blk = (nb, tr, w)
    return pl.pallas_call(
        body, name=name,
        grid_spec=pltpu.PrefetchScalarGridSpec(
            num_scalar_prefetch=1, grid=(hr // tr,),
            in_specs=[pl.BlockSpec(blk, lambda i, c: (0, c[0] * (hr // tr) + i, 0)),
                      pl.BlockSpec(blk, lambda i, c: (0, i, 0))],
            out_specs=pl.BlockSpec(blk, lambda i, c: (0, i, 0))),
        out_shape=jax.ShapeDtypeStruct((nb, hr, w), BF),
        compiler_params=_params(("parallel",)),
    )(cidx, g3, land)


def _chip_exchange_side(p3s, shapes, axes):
    n = len(p3s)

    def copies(ins, outs, sems):
        send, recv = sems
        x, y, c, chips = _place()
        cps = []
        for a in range(n):
            r, cc = shapes[a]
            for k, (px, py) in enumerate(chips):
                j = 2 * px + py
                src = ins[a].at[j] if axes[a] == 0 else ins[a].at[0, :, pl.ds(pl.multiple_of(j * cc, 128), cc)]
                cps.append(pltpu.make_async_remote_copy(
                    src_ref=src, dst_ref=outs[a].at[k], send_sem=send.at[a, k], recv_sem=recv.at[a, k],
                    device_id=(px, py, c), device_id_type=MESH))
        return cps

    def first(ins, outs, sems):
        for cp in copies(ins, outs, sems):
            cp.start()

    def last(ins, outs, sems):
        for cp in copies(ins, outs, sems):
            cp.wait()

    return _Side(list(p3s), [jax.ShapeDtypeStruct((3, r // 2, c), BF) for r, c in shapes],
                 [pltpu.SemaphoreType.DMA((n, 3)), pltpu.SemaphoreType.DMA((n, 3))], first, last)


def _chip_sum(p3, land, shard_shape, axis, idx, name):
    r, c = shard_shape
    hr = r // 2
    tr = 64
    nt = hr // tr

    def body(idx_ref, p_ref, l_ref, o_ref):
        acc = p_ref[...].astype(F32)
        for k in range(3):
            acc = acc + l_ref[k].astype(F32)
        o_ref[...] = acc

    own = (pl.BlockSpec((None, tr, c), lambda i, idx: (idx[0], i, 0)) if axis == 0
           else pl.BlockSpec((None, tr, c), lambda i, idx: (0, i, idx[0])))
    return pl.pallas_call(
        body, name=name,
        grid_spec=pltpu.PrefetchScalarGridSpec(
            num_scalar_prefetch=1, grid=(nt,),
            in_specs=[own, pl.BlockSpec((3, tr, c), lambda i, idx: (0, i, 0))],
            out_specs=pl.BlockSpec((tr, c), lambda i, idx: (idx[1] * nt + i, 0))),
        out_shape=jax.ShapeDtypeStruct((r, c), F32),
        compiler_params=_params(("parallel",)),
    )(idx, p3, land)


def _rs_pair_gather(fulls):
    n = len(fulls)

    def body(*refs):
        ins, outs = refs[:n], refs[n:2 * n]
        send, recv = refs[2 * n:]
        x, y, c, _ = _place()
        cps = []
        for a in range(n):
            hr = fulls[a].shape[0] // 2
            rows = pl.ds(pl.multiple_of(c * hr, 8), hr)
            cp = pltpu.make_async_remote_copy(
                src_ref=ins[a].at[rows, :], dst_ref=outs[a].at[rows, :], send_sem=send.at[a], recv_sem=recv.at[a],
                device_id=(x, y, 1 - c), device_id_type=MESH)
            cp.start()
            cps.append(cp)
        for a, cp in enumerate(cps):
            cp.wait_send()
            hr = fulls[a].shape[0] // 2
            other = pl.ds(pl.multiple_of((1 - c) * hr, 8), hr)
            pltpu.make_async_remote_copy(
                src_ref=ins[a].at[other, :], dst_ref=outs[a].at[other, :], send_sem=send.at[a], recv_sem=recv.at[a],
                device_id=(x, y, 1 - c), device_id_type=MESH).wait_recv()

    return pl.pallas_call(
        body, name="grads_pair_gather",
        in_specs=[HBM] * n, out_specs=[HBM] * n,
        out_shape=[jax.ShapeDtypeStruct(f.shape, F32) for f in fulls],
        input_output_aliases={a: a for a in range(n)},
        scratch_shapes=[pltpu.SemaphoreType.DMA((n,)), pltpu.SemaphoreType.DMA((n,))],
    )(*fulls)


def _rs_partials(grads, shapes, axes, tag):
    cidx = jnp.reshape(lax.axis_index("c"), (1,)).astype(jnp.int32)
    g3s = [_as3d(g, s, ax) for g, s, ax in zip(grads, shapes, axes)]
    lands = _rs_pair_exchange(g3s, f"grads_pair_exchange_{tag}")
    return [_pair_sum(g3, l, cidx, f"pair_sum_{tag}_{a}") for a, (g3, l) in enumerate(zip(g3s, lands))]


def _rs_finish(p3s, landed, shapes, axes):
    x, y, c = lax.axis_index("x"), lax.axis_index("y"), lax.axis_index("c")
    idx = jnp.stack([2 * x + y, c]).astype(jnp.int32)
    fulls = [_chip_sum(p3, l2, s, ax, idx, f"chip_sum_{a}")
             for a, (p3, l2, s, ax) in enumerate(zip(p3s, landed, shapes, axes))]
    return _rs_pair_gather(fulls)


NSMALL = 8


def _small_all_reduce(g_nw, dlb, g_hnw, g_wf, loss_vec):
    def body(nw_ref, lb_ref, hn_ref, wf_ref, ls_ref, out_ref, pack_ref, buf_ref, send, recv):
        x, y, c = lax.axis_index("x"), lax.axis_index("y"), lax.axis_index("c")
        me = 4 * x + 2 * y + c
        pack_ref[...] = jnp.zeros_like(pack_ref)
        pack_ref[0:1, :] = nw_ref[...]
        pack_ref[1:2, :] = lb_ref[...]
        pack_ref[2:3, 0:HK] = hn_ref[...]
        pack_ref[3:4, :] = wf_ref[...]
        pack_ref[4:5, :] = ls_ref[...]
        buf_ref[me] = pack_ref[...]
        cps = []
        for d in range(1, 8):
            dx, dy, dc = d >> 2, (d >> 1) & 1, d & 1
            peer = (1 - x if dx else x, 1 - y if dy else y, 1 - c if dc else c)
            cp = pltpu.make_async_remote_copy(
                src_ref=pack_ref, dst_ref=buf_ref.at[me], send_sem=send.at[d - 1], recv_sem=recv.at[d - 1],
                device_id=peer, device_id_type=MESH)
            cp.start()
            cps.append(cp)
        for d in range(1, 8):
            dx, dy, dc = d >> 2, (d >> 1) & 1, d & 1
            src = 4 * (1 - x if dx else x) + 2 * (1 - y if dy else y) + (1 - c if dc else c)
            pltpu.make_async_remote_copy(
                src_ref=pack_ref, dst_ref=buf_ref.at[src], send_sem=send.at[d - 1], recv_sem=recv.at[d - 1],
                device_id=(x, y, c), device_id_type=MESH).wait_recv()
        for cp in cps:
            cp.wait_send()
        acc = buf_ref[0]
        for i in range(1, 8):
            acc = acc + buf_ref[i]
        out_ref[...] = acc

    vm = pl.BlockSpec(memory_space=pltpu.VMEM)
    return pl.pallas_call(
        body, name="small_all_reduce",
        in_specs=[vm] * 5, out_specs=vm,
        out_shape=jax.ShapeDtypeStruct((NSMALL, D), F32),
        scratch_shapes=[pltpu.VMEM((NSMALL, D), F32), pltpu.VMEM((8, NSMALL, D), F32),
                        pltpu.SemaphoreType.DMA((7,)), pltpu.SemaphoreType.DMA((7,))],
    )(g_nw, dlb, g_hnw, g_wf, loss_vec)


def _adamw_math(w, g, m, v):
    m = B1 * m + (1.0 - B1) * g
    v = B2 * v + (1.0 - B2) * (g * g)
    m_hat = m / (1.0 - B1 ** STEP)
    v_hat = v / (1.0 - B2 ** STEP)
    return -LR * (m_hat / (jnp.sqrt(v_hat) + ADAM_EPS) + WD * w), m, v


def _adamw(w, g, m, v, name):
    r, c = w.shape
    tr = 64

    def body(w_ref, g_ref, m_ref, v_ref, d_ref, nm_ref, nv_ref):
        d_ref[...], nm_ref[...], nv_ref[...] = _adamw_math(w_ref[...], g_ref[...], m_ref[...], v_ref[...])

    blk = pl.BlockSpec((tr, c), lambda i: (i, 0))
    return pl.pallas_call(
        body, name=name, grid=(r // tr,), in_specs=[blk] * 4, out_specs=[blk] * 3,
        out_shape=[jax.ShapeDtypeStruct((r, c), F32)] * 3,
        compiler_params=_params(("parallel",)),
    )(w, g, m, v)


def _small_update(red, lbl, params):
    def body(red_ref, *refs):
        ins, outs = refs[:12], refs[12:]
        lb = _lower_bound(ins[3][...])
        dl0 = red_ref[1:2, :] * lb * (1.0 - lb)
        row = lax.broadcasted_iota(jnp.int32, (2, D), 0)
        grads = [red_ref[0:1, :], jnp.where(row == 0, dl0, -dl0), red_ref[2:3, 0:HK], red_ref[3:4, :]]
        for i, g in enumerate(grads):
            w, m, v = ins[3 * i][...], ins[3 * i + 1][...], ins[3 * i + 2][...]
            d, nm, nv = _adamw_math(w, g, m, v)
            outs[4 * i][...] = g
            outs[4 * i + 1][...] = d
            outs[4 * i + 2][...] = nm
            outs[4 * i + 3][...] = nv
        outs[16][...] = jnp.sum(red_ref[4:5, :], axis=1, keepdims=True)

    flat = [a for p in params for a in p]
    vm = pl.BlockSpec(memory_space=pltpu.VMEM)
    shapes = [jax.ShapeDtypeStruct(p[0].shape, F32) for p in params for _ in range(4)]
    return pl.pallas_call(
        body, name="small_update",
        in_specs=[vm] * 13, out_specs=[vm] * 17,
        out_shape=shapes + [jax.ShapeDtypeStruct((1, 1), F32)],
    )(red, *flat)


def kernel(x, positions, norm_w, w_in, lb_logits, hgrn_norm_w, w_branch_a, w_branch_b, w_out, final_norm_w, loss_target, m_norm_w, m_w_in, m_lb_logits, m_hgrn_norm_w, m_w_branch_a, m_w_branch_b, m_w_out, m_final_norm_w, v_norm_w, v_w_in, v_lb_logits, v_hgrn_norm_w, v_w_branch_a, v_w_branch_b, v_w_out, v_final_norm_w):
    big_w = [w_in[0], w_branch_a[0], w_branch_b[0], w_out[0]]
    big_m = [m_w_in[0], m_w_branch_a[0], m_w_branch_b[0], m_w_out[0]]
    big_v = [v_w_in[0], v_w_branch_a[0], v_w_branch_b[0], v_w_out[0]]
    shapes = [w.shape for w in big_w]
    wf = final_norm_w.reshape(1, D)

    shards = [w.astype(BF) for w in big_w]
    w_in_full, = _run_side(_gather_side(shards[:1], WEIGHT_AXES[:1]), "w_in_all_gather")
    loc = _local_step(x[0], positions.reshape(T, 1), norm_w, lb_logits, hgrn_norm_w, wf, loss_target[0],
                      w_in_full, *shards[1:], shard_shapes=shapes)
    g_big = [loc["g_win"], loc["g_wa"], loc["g_wb"], loc["g_wout"]]
    red = _small_all_reduce(loc["g_nw"], loc["dlb"], loc["g_hnw"], loc["g_wf"], loc["loss_vec"])

    small = _small_update(red, lb_logits, [
        (norm_w, m_norm_w, v_norm_w), (lb_logits, m_lb_logits, v_lb_logits),
        (hgrn_norm_w, m_hgrn_norm_w, v_hgrn_norm_w),
        (wf, m_final_norm_w.reshape(1, D), v_final_norm_w.reshape(1, D))])
    loss = small[16].reshape(())
    sg, sd, sm, sv = ([small[4 * i + j] for i in range(4)] for j in range(4))
    for lst in (sg, sd, sm, sv):
        lst[3] = lst[3].reshape(D)
    upd = [_adamw(w, g, m, v, f"adamw_{a}") for a, (w, g, m, v) in enumerate(zip(big_w, g_big, big_m, big_v))]
    bg = [g[None] for g in g_big]
    bd, bm, bv = ([u[j][None] for u in upd] for j in range(3))

    def order(s, b):
        return [s[0], b[0], s[1], s[2], b[1], b[2], b[3], s[3]]

    return (loss, loc["gx"][None], *order(sg, bg), *order(sd, bd), *order(sm, bm), *order(sv, bv))
```

```python
import functools

import jax
import jax.numpy as jnp
from jax import lax
from jax.experimental import pallas as pl
from jax.experimental.pallas import tpu as pltpu

T = 2048
D = 1024
NIN = 11264
HEADS = 8
HK = 128
CH = 16
NCH = T // CH
HSTEP = 2
ATT_GROUPS = ((128, 1), (512, 4), (2048, 16))
ATT_COL0 = 4096
AG_COL0 = 8704
GATE_COL0 = 9216
EPS = 1e-6
ROPE_THETA = 10000.0
LR, B1, B2, ADAM_EPS, WD, STEP = 0.001, 0.9, 0.999, 1e-08, 0.01, 10

F32 = jnp.float32
BF = jnp.bfloat16
VMEM_LIMIT = 56 * 1024 * 1024

_NN = (((1,), (0,)), ((), ()))
_NT = (((1,), (1,)), ((), ()))
_TN = (((0,), (0,)), ((), ()))


def _dot(a, b, dims=_NN):
    return lax.dot_general(a, b, dims, preferred_element_type=F32)


def _bdot(a, b, dims=_NN):
    return lax.dot_general(a.astype(BF), b.astype(BF), dims, preferred_element_type=F32)


def _sigmoid(x):
    return jax.nn.sigmoid(x)


def _params(sem=None):
    return pltpu.CompilerParams(dimension_semantics=sem, vmem_limit_bytes=VMEM_LIMIT)


def _matmul(a, b, *, ta=False, tb=False, out_dtype=F32, tm=512, tn=512, tk=None, name, side=None):
    m = a.shape[1] if ta else a.shape[0]
    kdim = a.shape[0] if ta else a.shape[1]
    n = b.shape[0] if tb else b.shape[1]
    tk = tk or kdim
    tm, tn = min(tm, m), min(tn, n)
    nm, nn, nk = m // tm, n // tn, kdim // tk
    dims = (((0 if ta else 1,), (1 if tb else 0,)), ((), ()))
    s_arrays, s_in_specs, s_shapes, s_out_specs, s_sems = _side_io(side)
    na, no = len(s_arrays), len(s_shapes)
    nacc = 1 if nk > 1 else 0

    def body(*refs):
        a_ref, b_ref = refs[:2]
        s_ins, o_ref, s_outs = refs[2:2 + na], refs[2 + na], refs[3 + na:3 + na + no]
        scratch = refs[3 + na + no:]
        s_sem_refs = scratch[nacc:]
        i, j, k = pl.program_id(0), pl.program_id(1), pl.program_id(2)
        if side is not None:
            @pl.when((i == 0) & (j == 0) & (k == 0))
            def _():
                side.first(s_ins, s_outs, s_sem_refs)

        prod = _bdot(a_ref[...], b_ref[...], dims)
        if nk == 1:
            o_ref[...] = prod.astype(out_dtype)
        else:
            acc = scratch[0]

            @pl.when(k == 0)
            def _():
                acc[...] = prod

            @pl.when(k > 0)
            def _():
                acc[...] += prod

            @pl.when(k == nk - 1)
            def _():
                o_ref[...] = acc[...].astype(out_dtype)

        if side is not None:
            @pl.when((i == nm - 1) & (j == nn - 1) & (k == nk - 1))
            def _():
                side.last(s_ins, s_outs, s_sem_refs)

    a_spec = pl.BlockSpec((tk, tm), lambda i, j, k: (k, i)) if ta else pl.BlockSpec((tm, tk), lambda i, j, k: (i, k))
    b_spec = pl.BlockSpec((tn, tk), lambda i, j, k: (j, k)) if tb else pl.BlockSpec((tk, tn), lambda i, j, k: (k, j))
    sem = ("parallel", "parallel", "arbitrary") if side is None else ("arbitrary",) * 3
    out = pl.pallas_call(
        body, name=name, grid=(nm, nn, nk),
        in_specs=[a_spec, b_spec] + s_in_specs,
        out_specs=[pl.BlockSpec((tm, tn), lambda i, j, k: (i, j))] + s_out_specs,
        out_shape=[jax.ShapeDtypeStruct((m, n), out_dtype)] + s_shapes,
        scratch_shapes=([pltpu.VMEM((tm, tn), F32)] if nk > 1 else []) + s_sems,
        compiler_params=_params(sem),
    )(a, b, *s_arrays)
    return out[0] if side is None else (out[0], out[1:])


DZ_TILE = 512


def _part_offsets(parts):
    counts = [p.shape[1] // DZ_TILE for p in parts]
    offs = [sum(counts[:i]) for i in range(len(parts))]
    return counts, offs


def _part_spec(rows, cnt, off, tile_axis):
    def index(*g):
        return (0 if rows is None else g[0], jnp.clip(g[tile_axis] - off, 0, cnt - 1))
    return index


def _grad_w_in(h, parts):
    counts, offs = _part_offsets(parts)
    n = len(parts)

    def body(h_ref, *refs):
        o_ref = refs[n]
        j = pl.program_id(0)
        for p_ref, cnt, off in zip(refs[:n], counts, offs):
            @pl.when((j >= off) & (j < off + cnt))
            def _(p_ref=p_ref):
                o_ref[...] = _bdot(h_ref[...], p_ref[...], _TN).astype(BF)

    return pl.pallas_call(
        body, name="g_win", grid=(sum(counts),),
        in_specs=[pl.BlockSpec((T, D), lambda j: (0, 0))] +
                 [pl.BlockSpec((T, DZ_TILE), _part_spec(None, c, o, 0)) for c, o in zip(counts, offs)],
        out_specs=pl.BlockSpec((D, DZ_TILE), lambda j: (0, j)),
        out_shape=jax.ShapeDtypeStruct((D, NIN), BF),
        compiler_params=_params(("parallel",)),
    )(h, *parts)


def _grad_w_in_half(h, parts, half_idx, side=None):
    counts, offs = _part_offsets(parts)
    n = len(parts)
    nj = sum(counts)
    s_arrays, s_in_specs, s_shapes, s_out_specs, s_sems = _side_io(side)
    na, no = len(s_arrays), len(s_shapes)

    def body(idx_ref, h_ref, *refs):
        s_ins, o_ref, s_outs, s_sem_refs = refs[n:n + na], refs[n + na], refs[n + na + 1:n + na + 1 + no], refs[n + na + 1 + no:]
        j = pl.program_id(0)
        if side is not None:
            @pl.when(j == 0)
            def _():
                side.first(s_ins, s_outs, s_sem_refs)

        for p_ref, cnt, off in zip(refs[:n], counts, offs):
            @pl.when((j >= off) & (j < off + cnt))
            def _(p_ref=p_ref):
                o_ref[...] = _bdot(h_ref[...], p_ref[...], _TN).astype(BF)

        if side is not None:
            @pl.when(j == nj - 1)
            def _():
                side.last(s_ins, s_outs, s_sem_refs)

    def part_spec(cnt, off):
        return pl.BlockSpec((T, DZ_TILE), lambda j, idx: (0, jnp.clip(j - off, 0, cnt - 1)))

    out = pl.pallas_call(
        body, name="g_win_half" if side is None else "g_win_half_carrying",
        grid_spec=pltpu.PrefetchScalarGridSpec(
            num_scalar_prefetch=1, grid=(nj,),
            in_specs=[pl.BlockSpec((T, D // 2), lambda j, idx: (0, idx[0]))] +
                     [part_spec(c, o) for c, o in zip(counts, offs)] + s_in_specs,
            out_specs=[pl.BlockSpec((D // 2, DZ_TILE), lambda j, idx: (0, j))] + s_out_specs,
            scratch_shapes=s_sems),
        out_shape=[jax.ShapeDtypeStruct((D // 2, NIN), BF)] + s_shapes,
        compiler_params=_params(("parallel",) if side is None else ("arbitrary",)),
    )(half_idx, h, *parts, *s_arrays)
    return out[0] if side is None else (out[0], out[1:])


def _side_io(side):
    if side is None:
        return [], [], [], [], []
    return (side.arrays, [HBM] * len(side.arrays), side.out_shapes, [HBM] * len(side.out_shapes), side.sems)


def _grad_x(parts, w_in, x, dout, norm_w, side=None):
    counts, offs = _part_offsets(parts)
    n = len(parts)
    tm = 1024
    nm, nk = T // tm, sum(counts)
    s_arrays, s_in_specs, s_shapes, s_out_specs, s_sems = _side_io(side)
    na, no = len(s_arrays), len(s_shapes)

    def body(*refs):
        w_ref, x_ref, dout_ref, nw_ref = refs[n:n + 4]
        s_ins = refs[n + 4:n + 4 + na]
        gx_ref, gw_ref = refs[n + 4 + na:n + 6 + na]
        s_outs = refs[n + 6 + na:n + 6 + na + no]
        acc = refs[n + 6 + na + no]
        s_sem_refs = refs[n + 7 + na + no:]
        i, k = pl.program_id(0), pl.program_id(1)

        @pl.when((i == 0) & (k == 0))
        def _():
            gw_ref[...] = jnp.zeros_like(gw_ref)
            if side is not None:
                side.first(s_ins, s_outs, s_sem_refs)

        @pl.when(k == 0)
        def _():
            acc[...] = jnp.zeros_like(acc)

        for p_ref, cnt, off in zip(refs[:n], counts, offs):
            @pl.when((k >= off) & (k < off + cnt))
            def _(p_ref=p_ref):
                acc[...] += _bdot(p_ref[...], w_ref[...], _NT)

        @pl.when(k == nk - 1)
        def _():
            gw = jnp.zeros((1, D), F32)
            for c in range(tm // BLK):
                rows = pl.ds(BLK * c, BLK)
                xv, dhv = x_ref[rows, :], acc[rows, :]
                r = lax.rsqrt(jnp.mean(xv * xv, axis=-1, keepdims=True) + EPS)
                nrm = xv * r
                dn = dhv * nw_ref[...]
                gw = gw + jnp.sum(dhv * nrm, axis=0, keepdims=True)
                gx_ref[rows, :] = dout_ref[rows, :] + r * (dn - nrm * jnp.mean(dn * nrm, axis=-1, keepdims=True))
            gw_ref[...] += gw

        if side is not None:
            @pl.when((i == nm - 1) & (k == nk - 1))
            def _():
                side.last(s_ins, s_outs, s_sem_refs)

    row = pl.BlockSpec((tm, D), lambda i, k: (i, 0))
    vec = pl.BlockSpec((1, D), lambda i, k: (0, 0))
    out = pl.pallas_call(
        body, name="grad_x", grid=(nm, nk),
        in_specs=[pl.BlockSpec((tm, DZ_TILE), _part_spec(0, c, o, 1)) for c, o in zip(counts, offs)] +
                 [pl.BlockSpec((D, DZ_TILE), lambda i, k: (0, k)), row, row, vec] + s_in_specs,
        out_specs=[row, vec] + s_out_specs,
        out_shape=[jax.ShapeDtypeStruct((T, D), F32), jax.ShapeDtypeStruct((1, D), F32)] + s_shapes,
        scratch_shapes=[pltpu.VMEM((tm, D), F32)] + s_sems,
        compiler_params=_params(("arbitrary", "arbitrary")),
    )(*parts, w_in, x, dout, norm_w, *s_arrays)
    return out[0], out[1], out[2:]


def _rmsnorm_fwd(x, w, side=None):
    tm = 256
    nm = T // tm
    s_arrays, s_in_specs, s_shapes, s_out_specs, s_sems = _side_io(side)
    na, no = len(s_arrays), len(s_shapes)

    def body(*refs):
        x_ref, w_ref = refs[:2]
        s_ins, h_ref, s_outs, s_sem_refs = refs[2:2 + na], refs[2 + na], refs[3 + na:3 + na + no], refs[3 + na + no:]
        if side is not None:
            @pl.when(pl.program_id(0) == 0)
            def _():
                side.first(s_ins, s_outs, s_sem_refs)

        xv = x_ref[...]
        r = lax.rsqrt(jnp.mean(xv * xv, axis=-1, keepdims=True) + EPS)
        h_ref[...] = (xv * r * w_ref[...]).astype(BF)
        if side is not None:
            @pl.when(pl.program_id(0) == nm - 1)
            def _():
                side.last(s_ins, s_outs, s_sem_refs)

    out = pl.pallas_call(
        body, name="rmsnorm_fwd", grid=(nm,),
        in_specs=[pl.BlockSpec((tm, D), lambda i: (i, 0)), pl.BlockSpec((1, D), lambda i: (0, 0))] + s_in_specs,
        out_specs=[pl.BlockSpec((tm, D), lambda i: (i, 0))] + s_out_specs,
        out_shape=[jax.ShapeDtypeStruct((T, D), BF)] + s_shapes,
        scratch_shapes=s_sems,
        compiler_params=_params(("parallel",) if side is None else ("arbitrary",)),
    )(x, w, *s_arrays)
    return out[0] if side is None else (out[0], out[1:])


def _lower_bound(lbl):
    mx = jnp.max(lbl, axis=0, keepdims=True)
    e = jnp.exp(lbl - mx)
    return e[0:1] / jnp.sum(e, axis=0, keepdims=True)


def _cumsum_rows(g, rows):
    b = g
    sh = 1
    while sh < CH:
        b = b + jnp.where(rows >= sh, pltpu.roll(b, sh, axis=0), 0.0)
        sh *= 2
    return b


def _rev_cumsum_rows(g, rows):
    b = g
    sh = 1
    while sh < CH:
        b = b + jnp.where(rows < CH - sh, pltpu.roll(b, CH - sh, axis=0), 0.0)
        sh *= 2
    return b


SUB = CH // 2


def _direct_block(qb, kb, vb, bb, rows8):
    ob = jnp.zeros_like(qb)
    for s in range(SUB):
        e_s = jnp.exp(jnp.where(rows8 >= s, bb - bb[s:s + 1], -jnp.inf))
        ob = ob + jnp.sum(qb * e_s * kb[s:s + 1], axis=1, keepdims=True) * vb[s:s + 1]
    return ob


def _direct_block_bwd(qb, kb, vb, bb, dob, rows8, rowc8):
    dq = dk = dv = db = jnp.zeros_like(qb)
    for s in range(SUB):
        one = (rowc8 == s).astype(F32)
        ks, vs = kb[s:s + 1], vb[s:s + 1]
        e_s = jnp.exp(jnp.where(rows8 >= s, bb - bb[s:s + 1], -jnp.inf))
        qes = qb * e_s
        w = qes * ks
        a = jnp.sum(w, axis=1, keepdims=True)
        da = jnp.sum(dob * vs, axis=1, keepdims=True)
        dv = dv + one * jnp.sum(a * dob, axis=0, keepdims=True)
        dq = dq + da * e_s * ks
        dk = dk + one * jnp.sum(da * qes, axis=0, keepdims=True)
        u = da * w
        db = db + u - one * jnp.sum(u, axis=0, keepdims=True)
    return dq, dk, dv, db


def _cross_factors(q, k, b):
    ref = b[SUB - 1:SUB]
    e_hi, e_lo = jnp.exp(b[SUB:] - ref), jnp.exp(ref - b[:SUB])
    return q[SUB:] * e_hi, k[:SUB] * e_lo, e_hi, e_lo


def _intra_fwd(q, k, v, b, rows8):
    lo = _direct_block(q[:SUB], k[:SUB], v[:SUB], b[:SUB], rows8)
    hi = _direct_block(q[SUB:], k[SUB:], v[SUB:], b[SUB:], rows8)
    qe_hi, ke_lo, _, _ = _cross_factors(q, k, b)
    for s in range(SUB):
        hi = hi + jnp.sum(qe_hi * ke_lo[s:s + 1], axis=1, keepdims=True) * v[s:s + 1]
    return jnp.concatenate([lo, hi], axis=0)


def _intra_bwd(q, k, v, b, do, rows8, rowc8):
    dq_lo, dk_lo, dv_lo, db_lo = _direct_block_bwd(q[:SUB], k[:SUB], v[:SUB], b[:SUB], do[:SUB], rows8, rowc8)
    dq_hi, dk_hi, dv_hi, db_hi = _direct_block_bwd(q[SUB:], k[SUB:], v[SUB:], b[SUB:], do[SUB:], rows8, rowc8)
    qe_hi, ke_lo, e_hi, e_lo = _cross_factors(q, k, b)
    do_hi, v_lo = do[SUB:], v[:SUB]
    dqe = dke = jnp.zeros_like(qe_hi)
    for s in range(SUB):
        one = (rowc8 == s).astype(F32)
        a = jnp.sum(qe_hi * ke_lo[s:s + 1], axis=1, keepdims=True)
        da = jnp.sum(do_hi * v_lo[s:s + 1], axis=1, keepdims=True)
        dv_lo = dv_lo + one * jnp.sum(a * do_hi, axis=0, keepdims=True)
        dqe = dqe + da * ke_lo[s:s + 1]
        dke = dke + one * jnp.sum(da * qe_hi, axis=0, keepdims=True)
    u_hi, u_lo = dqe * qe_hi, dke * ke_lo
    d_ref = jnp.sum(u_lo, axis=0, keepdims=True) - jnp.sum(u_hi, axis=0, keepdims=True)
    db_lo = db_lo - u_lo + (rowc8 == SUB - 1).astype(F32) * d_ref
    cat = lambda lo, hi: jnp.concatenate([lo, hi], axis=0)
    return (cat(dq_lo, dq_hi + dqe * e_hi), cat(dk_lo + dke * e_lo, dk_hi), cat(dv_lo, dv_hi),
            cat(db_lo, db_hi + u_hi))


def _hgrn_fwd(z, lbl, nw):
    def body(hq_ref, hf_ref, hi_ref, hg_ref, lbl_ref, nw_ref, oraw_ref, og_ref, sh_ref, st_ref):
        @pl.when(pl.program_id(0) == 0)
        def _():
            st_ref[...] = jnp.zeros_like(st_ref)

        lb_all = _lower_bound(lbl_ref[...])
        rows = lax.broadcasted_iota(jnp.int32, (CH, HK), 0)
        rows8 = lax.broadcasted_iota(jnp.int32, (SUB, HK), 0)
        nwv = nw_ref[...]
        for cc, h in [(cc, h) for cc in range(HSTEP) for h in range(HEADS)]:
            rs = slice(CH * cc, CH * (cc + 1))
            sl = slice(HK * h, HK * (h + 1))
            lb = lb_all[:, sl]
            hq, hf, v, hg = hq_ref[rs, sl], hf_ref[rs, sl], hi_ref[rs, sl], hg_ref[rs, sl]
            q = hq * _sigmoid(hq)
            f = lb + (1.0 - lb) * _sigmoid(hf)
            k = 1.0 - f
            b = _cumsum_rows(jnp.log(f), rows)
            sh_ref[cc, h] = st_ref[h]
            o = _bdot(q * jnp.exp(b), st_ref[h], _NT) + _intra_fwd(q, k, v, b, rows8)
            bl = b[CH - 1:CH]
            st_ref[h] = st_ref[h] * jnp.exp(bl)
            st_ref[h] += _bdot(v, k * jnp.exp(bl - b), _TN)
            oraw_ref[rs, sl] = o
            nrm = o * lax.rsqrt(jnp.mean(o * o, axis=1, keepdims=True) + EPS)
            og_ref[rs, sl] = (nrm * nwv * (hg * _sigmoid(hg))).astype(BF)

    zblk = lambda c: pl.BlockSpec((CH * HSTEP, D), lambda i, c=c: (i, c))
    return pl.pallas_call(
        body, name="hgrn_fwd", grid=(NCH // HSTEP,),
        in_specs=[zblk(0), zblk(1), zblk(2), zblk(3),
                  pl.BlockSpec((2, D), lambda i: (0, 0)), pl.BlockSpec((1, HK), lambda i: (0, 0))],
        out_specs=[zblk(0), zblk(0),
                   pl.BlockSpec((HSTEP, HEADS, HK, HK), lambda i: (i, 0, 0, 0))],
        out_shape=[jax.ShapeDtypeStruct((T, D), F32), jax.ShapeDtypeStruct((T, D), BF),
                   jax.ShapeDtypeStruct((NCH, HEADS, HK, HK), F32)],
        scratch_shapes=[pltpu.VMEM((HEADS, HK, HK), F32)],
        compiler_params=_params(("arbitrary",)),
    )(z, z, z, z, lbl, nw)


def _hgrn_bwd(z, lbl, nw, oraw, dog, shist, side=None):
    hstep = 1
    s_arrays, s_in_specs, s_shapes, s_out_specs, s_sems = _side_io(side)
    na, no = len(s_arrays), len(s_shapes)

    def body(*refs):
        hq_ref, hf_ref, hi_ref, hg_ref, lbl_ref, nw_ref, oraw_ref, dog_ref, sh_ref = refs[:9]
        s_ins = refs[9:9 + na]
        dz_ref, dlb_ref, dnw_ref = refs[9 + na:12 + na]
        s_outs = refs[12 + na:12 + na + no]
        dst_ref = refs[12 + na + no]
        s_sem_refs = refs[13 + na + no:]

        @pl.when(pl.program_id(0) == 0)
        def _():
            dst_ref[...] = jnp.zeros_like(dst_ref)
            dlb_ref[...] = jnp.zeros_like(dlb_ref)
            dnw_ref[...] = jnp.zeros_like(dnw_ref)
            if side is not None:
                side.first(s_ins, s_outs, s_sem_refs)

        lb_all = _lower_bound(lbl_ref[...])
        rows = lax.broadcasted_iota(jnp.int32, (CH, HK), 0)
        rowc = lax.broadcasted_iota(jnp.int32, (CH, 1), 0)
        rows8 = lax.broadcasted_iota(jnp.int32, (SUB, HK), 0)
        rowc8 = lax.broadcasted_iota(jnp.int32, (SUB, 1), 0)
        nwv = nw_ref[...]
        dnw = jnp.zeros((1, HK), F32)
        for cc, h in [(cc, h) for cc in reversed(range(hstep)) for h in range(HEADS)]:
            rs = slice(CH * cc, CH * (cc + 1))
            sl = slice(HK * h, HK * (h + 1))
            lb = lb_all[:, sl]
            hq, hf, v, hg = hq_ref[rs, sl], hf_ref[rs, sl], hi_ref[rs, sl], hg_ref[rs, sl]
            o, dg_out = oraw_ref[rs, sl], dog_ref[rs, sl]
            sg = _sigmoid(hg)
            sil = hg * sg
            r = lax.rsqrt(jnp.mean(o * o, axis=1, keepdims=True) + EPS)
            nrm = o * r
            d_hg = dg_out * (nrm * nwv) * (sg * (1.0 + hg * (1.0 - sg)))
            dn = dg_out * nwv * sil
            dnw = dnw + jnp.sum(dg_out * nrm * sil, axis=0, keepdims=True)
            do = r * (dn - nrm * jnp.mean(dn * nrm, axis=1, keepdims=True))
            sq = _sigmoid(hq)
            q = hq * sq
            sig = _sigmoid(hf)
            f = lb + (1.0 - lb) * sig
            k = 1.0 - f
            b = _cumsum_rows(jnp.log(f), rows)
            eb = jnp.exp(b)
            qe = q * eb
            bl = b[CH - 1:CH]
            ebl = jnp.exp(bl)
            kdec = jnp.exp(bl - b)
            ke = k * kdec
            dqe = _bdot(do, sh_ref[cc, h])
            dq = dqe * eb
            db = dqe * qe
            dke = _bdot(v, dst_ref[h])
            dv = _bdot(ke, dst_ref[h], _NT)
            dk = dke * kdec
            rr = dke * ke
            db = db - rr
            db_last = (jnp.sum(rr, axis=0, keepdims=True)
                       + ebl * jnp.sum(dst_ref[h] * sh_ref[cc, h], axis=0, keepdims=True))
            dst_ref[h] = dst_ref[h] * ebl
            dst_ref[h] += _bdot(do, qe, _TN)
            dq_i, dk_i, dv_i, db_i = _intra_bwd(q, k, v, b, do, rows8, rowc8)
            dq, dk, dv = dq + dq_i, dk + dk_i, dv + dv_i
            db = db + db_i + (rowc == CH - 1).astype(F32) * db_last
            dgl = _rev_cumsum_rows(db, rows)
            df = dgl / f - dk
            dlb_ref[:, sl] += jnp.sum(df * (1.0 - sig), axis=0, keepdims=True)
            dz_ref[rs, sl] = (dq * (sq * (1.0 + hq * (1.0 - sq)))).astype(BF)
            dz_ref[rs, D + HK * h:D + HK * (h + 1)] = (df * (1.0 - lb) * sig * (1.0 - sig)).astype(BF)
            dz_ref[rs, 2 * D + HK * h:2 * D + HK * (h + 1)] = dv.astype(BF)
            dz_ref[rs, 3 * D + HK * h:3 * D + HK * (h + 1)] = d_hg.astype(BF)
        dnw_ref[...] += dnw
        if side is not None:
            @pl.when(pl.program_id(0) == NCH // hstep - 1)
            def _():
                side.last(s_ins, s_outs, s_sem_refs)

    rev = lambda i: NCH // hstep - 1 - i
    zblk = lambda c: pl.BlockSpec((CH * hstep, D), lambda i, c=c: (rev(i), c))
    out = pl.pallas_call(
        body, name="hgrn_bwd", grid=(NCH // hstep,),
        in_specs=[zblk(0), zblk(1), zblk(2), zblk(3),
                  pl.BlockSpec((2, D), lambda i: (0, 0)), pl.BlockSpec((1, HK), lambda i: (0, 0)),
                  zblk(0), zblk(0),
                  pl.BlockSpec((hstep, HEADS, HK, HK), lambda i: (rev(i), 0, 0, 0))] + s_in_specs,
        out_specs=[pl.BlockSpec((CH * hstep, 4 * D), lambda i: (rev(i), 0)),
                   pl.BlockSpec((1, D), lambda i: (0, 0)), pl.BlockSpec((1, HK), lambda i: (0, 0))] + s_out_specs,
        out_shape=[jax.ShapeDtypeStruct((T, 4 * D), BF), jax.ShapeDtypeStruct((1, D), F32),
                   jax.ShapeDtypeStruct((1, HK), F32)] + s_shapes,
        scratch_shapes=[pltpu.VMEM((HEADS, HK, HK), F32)] + s_sems,
        compiler_params=_params(("arbitrary",)),
    )(z, z, z, z, lbl, nw, oraw, dog, shist, *s_arrays)
    return out[0], out[1], out[2], out[3:]


BLK = 128
NBLK = T // BLK
QK_SCALE = 0.125


def _head_masks():
    lane = lax.broadcasted_iota(jnp.int32, (1, BLK), 1)
    return [(lane < 64).astype(F32), (lane >= 64).astype(F32)]


def _pieces(dil):
    m = T // dil
    out = []
    for r in range(dil):
        for j in range(m // BLK):
            start = r + dil * BLK * j
            rows = pl.ds(start, BLK, stride=dil) if dil > 1 else pl.ds(start, BLK)
            out.append((rows, r * m + BLK * j))
    return out


def _rope_tables(pos, invf):
    tm = 256

    def body(pos_ref, invf_ref, cos_ref, sa_ref, sb_ref):
        first = (lax.broadcasted_iota(jnp.int32, (tm, BLK), 1) % 64) < 32
        ang = pos_ref[...].astype(F32) * invf_ref[...]
        s = jnp.sin(ang)
        cos_ref[...] = jnp.cos(ang)
        sa_ref[...] = jnp.where(first, -s, 0.0)
        sb_ref[...] = jnp.where(first, 0.0, s)

    row = pl.BlockSpec((tm, BLK), lambda i: (i, 0))
    return pl.pallas_call(
        body, name="rope_tables", grid=(T // tm,),
        in_specs=[pl.BlockSpec((tm, 1), lambda i: (i, 0)), pl.BlockSpec((1, BLK), lambda i: (0, 0))],
        out_specs=[row, row, row], out_shape=[jax.ShapeDtypeStruct((T, BLK), F32)] * 3,
        compiler_params=_params(("parallel",)),
    )(pos, invf)


def _rope(x, c, sa, sb):
    return x * c + pltpu.roll(x, 96, axis=1) * sa + pltpu.roll(x, 32, axis=1) * sb


def _rope_t(d, c, sa, sb):
    return d * c + pltpu.roll(d * sa, 32, axis=1) + pltpu.roll(d * sb, 96, axis=1)


def _rope_and_regroup(dil, q_ref, k_ref, v_ref, tables, stage_q, stage_k, qr_ref, kr_ref, vr_ref):
    cos_ref, sa_ref, sb_ref = tables
    to_q, to_k = (qr_ref, kr_ref) if dil == 1 else (stage_q, stage_k)
    for c in range(T // BLK):
        rows = pl.ds(BLK * c, BLK)
        cs, sa, sb = cos_ref[rows, :], sa_ref[rows, :], sb_ref[rows, :]
        to_q[rows, :] = _rope(q_ref[rows, :], cs, sa, sb) * QK_SCALE
        to_k[rows, :] = _rope(k_ref[rows, :], cs, sa, sb)
    for rows, dst in _pieces(dil):
        drows = pl.ds(dst, BLK)
        if dil > 1:
            qr_ref[drows, :] = stage_q[rows, :]
            kr_ref[drows, :] = stage_k[rows, :]
        vr_ref[drows, :] = v_ref[rows, :]


def _window_bias(bias_ref):
    ii = lax.broadcasted_iota(jnp.int32, (2 * BLK, BLK), 0) % BLK
    jj = lax.broadcasted_iota(jnp.int32, (2 * BLK, BLK), 1)
    bias_ref[0] = jnp.where(jj <= ii, 0.0, -jnp.inf)
    bias_ref[1] = jnp.where(jj >= ii, 0.0, -jnp.inf)


def _blocks(bi):
    if isinstance(bi, int):
        return pl.ds(bi * BLK, BLK), pl.ds(max(bi - 1, 0) * BLK, BLK)
    return (pl.ds(pl.multiple_of(bi * BLK, BLK), BLK),
            pl.ds(pl.multiple_of(jnp.maximum(bi - 1, 0) * BLK, BLK), BLK))


def _stack_heads(x, masks):
    return jnp.concatenate([x * masks[0], x * masks[1]], axis=0).astype(BF)


def _attn_fwd(z, cos, sa, sb):
    def body(q_ref, k_ref, v_ref, ag_ref, cos_ref, sa_ref, sb_ref, ob_ref, opre_ref, lse_ref,
             bias_ref, qr_ref, kr_ref, vr_ref, og_ref, lg_ref, otok_ref, ltok_ref, sc_ref):
        g = pl.program_id(1)
        masks = _head_masks()

        @pl.when(g == 0)
        def _():
            _window_bias(bias_ref)

        def group(gi):
            dil = ATT_GROUPS[gi][1]
            nblk = (T // dil) // BLK
            _rope_and_regroup(dil, q_ref, k_ref, v_ref, (cos_ref, sa_ref, sb_ref), lg_ref.at[0], lg_ref.at[1],
                              qr_ref, kr_ref, vr_ref)

            def scores(bi, slot):
                cur, prev = _blocks(bi)
                q2 = _stack_heads(qr_ref[cur, :], masks)
                sc_ref[slot, 0] = _dot(q2, kr_ref[cur, :].astype(BF), _NT) + bias_ref[0]
                if nblk > 1:
                    sc_ref[slot, 1] = (_dot(q2, kr_ref[prev, :].astype(BF), _NT)
                                       + (bias_ref[1] + jnp.where((bi % nblk) != 0, 0.0, -jnp.inf)))

            def finish(bi, slot):
                cur, prev = _blocks(bi)
                s_c, vc = sc_ref[slot, 0], vr_ref[cur, :].astype(BF)
                if nblk > 1:
                    s_p, vp = sc_ref[slot, 1], vr_ref[prev, :].astype(BF)
                    mx = jnp.max(jnp.maximum(s_c, s_p), axis=1, keepdims=True)
                    p_c, p_p = jnp.exp(s_c - mx), jnp.exp(s_p - mx)
                    den = jnp.sum(p_c + p_p, axis=1, keepdims=True)
                    oh = _dot(p_c.astype(BF), vc) + _dot(p_p.astype(BF), vp)
                else:
                    mx = jnp.max(s_c, axis=1, keepdims=True)
                    p_c = jnp.exp(s_c - mx)
                    den = jnp.sum(p_c, axis=1, keepdims=True)
                    oh = _dot(p_c.astype(BF), vc)
                on = oh / den
                lsev = jnp.broadcast_to(mx + jnp.log(den), (2 * BLK, BLK))
                og_ref[cur, :] = on[:BLK] * masks[0] + on[BLK:] * masks[1]
                lg_ref[0, cur, :] = lsev[:BLK]
                lg_ref[1, cur, :] = lsev[BLK:]

            def pair(j, carry):
                finish(2 * j, 0)
                scores(2 * j + 1, 1)
                finish(2 * j + 1, 1)
                scores(jnp.minimum(2 * j + 2, NBLK - 1), 0)
                return carry

            scores(0, 0)
            lax.fori_loop(0, NBLK // 2, pair, 0)
            for rows, src in _pieces(dil):
                srows = pl.ds(src, BLK)
                otok_ref[gi, rows, :] = og_ref[srows, :]
                ltok_ref[gi, 0, rows, :] = lg_ref[0, srows, :]
                ltok_ref[gi, 1, rows, :] = lg_ref[1, srows, :]

        for gi in range(3):
            pl.when(g == gi)(functools.partial(group, gi))

        @pl.when(g == 2)
        def _():
            for c in range(T // BLK):
                rows = pl.ds(BLK * c, BLK)
                wts = []
                for hh in range(2):
                    l0, l1, l2 = ltok_ref[0, hh, rows, :], ltok_ref[1, hh, rows, :], ltok_ref[2, hh, rows, :]
                    mx = jnp.maximum(jnp.maximum(l0, l1), l2)
                    e0, e1, e2 = jnp.exp(l0 - mx), jnp.exp(l1 - mx), jnp.exp(l2 - mx)
                    tot = e0 + e1 + e2
                    lse_ref[rows, BLK * hh:BLK * (hh + 1)] = mx + jnp.log(tot)
                    inv = 1.0 / tot
                    wts.append([e0 * inv, e1 * inv, e2 * inv])
                o = sum((wts[0][gi] * masks[0] + wts[1][gi] * masks[1]) * otok_ref[gi, rows, :] for gi in range(3))
                ag = ag_ref[rows, :]
                opre_ref[rows, :] = o
                ob_ref[rows, :] = (o * (ag * _sigmoid(ag))).astype(BF)

    c0 = ATT_COL0 // BLK
    zspec = lambda part: pl.BlockSpec((T, BLK), lambda p, g, part=part: (0, c0 + 12 * part + 4 * g + p))
    outspec = pl.BlockSpec((T, BLK), lambda p, g: (0, p))
    table = pl.BlockSpec((T, BLK), lambda p, g: (0, 0))
    big = lambda: pltpu.VMEM((T, BLK), F32)
    return pl.pallas_call(
        body, name="attn_fwd", grid=(4, 3),
        in_specs=[zspec(0), zspec(1), zspec(2),
                  pl.BlockSpec((T, BLK), lambda p, g: (0, AG_COL0 // BLK + p)), table, table, table],
        out_specs=[outspec, outspec, pl.BlockSpec((T, 2 * BLK), lambda p, g: (0, p))],
        out_shape=[jax.ShapeDtypeStruct((T, 512), BF), jax.ShapeDtypeStruct((T, 512), F32),
                   jax.ShapeDtypeStruct((T, 8 * BLK), F32)],
        scratch_shapes=[pltpu.VMEM((2, 2 * BLK, BLK), F32), big(), big(), big(), big(),
                        pltpu.VMEM((2, T, BLK), F32), pltpu.VMEM((3, T, BLK), F32), pltpu.VMEM((3, 2, T, BLK), F32),
                        pltpu.VMEM((2, 2, 2 * BLK, BLK), F32)],
        compiler_params=_params(("parallel", "arbitrary")),
    )(z, z, z, z, cos, sa, sb)


def _attn_bwd(z, cos, sa, sb, opre, lse, dob):
    def body(q_ref, k_ref, v_ref, ag_ref, cos_ref, sa_ref, sb_ref, o_ref, lse0_ref, lse1_ref, dob_ref,
             dq_ref, dk_ref, dv_ref, dag_ref,
             bias_ref, dtok_ref, qr_ref, kr_ref, vr_ref, dor_ref, lr_ref, dr_ref,
             dqr_ref, dkr_ref, dvr_ref, pd_ref, dotok_ref):
        g = pl.program_id(1)
        masks = _head_masks()

        @pl.when(g == 0)
        def _():
            _window_bias(bias_ref)
            for c in range(T // BLK):
                rows = pl.ds(BLK * c, BLK)
                ag, dob_v, o = ag_ref[rows, :], dob_ref[rows, :], o_ref[rows, :]
                sg = _sigmoid(ag)
                dag_ref[rows, :] = (dob_v * o * (sg * (1.0 + ag * (1.0 - sg)))).astype(BF)
                do = dob_v * (ag * sg)
                dotok_ref[rows, :] = do
                prod = do * o
                for hh, mh in enumerate(masks):
                    dtok_ref[hh, rows, :] = jnp.broadcast_to(jnp.sum(prod * mh, axis=1, keepdims=True), (BLK, BLK))

        def group(gi):
            dil = ATT_GROUPS[gi][1]
            nblk = (T // dil) // BLK
            _rope_and_regroup(dil, q_ref, k_ref, v_ref, (cos_ref, sa_ref, sb_ref), dqr_ref, dvr_ref,
                              qr_ref, kr_ref, vr_ref)
            for rows, dst in _pieces(dil):
                drows = pl.ds(dst, BLK)
                dor_ref[drows, :] = dotok_ref[rows, :]
                for hh, lse_ref in enumerate((lse0_ref, lse1_ref)):
                    lr_ref[hh, drows, :] = lse_ref[rows, :]
                    dr_ref[hh, drows, :] = dtok_ref[hh, rows, :]
            dkr_ref[...] = jnp.zeros_like(dkr_ref)
            dvr_ref[...] = jnp.zeros_like(dvr_ref)

            def probs(bi, slot):
                cur, prev = _blocks(bi)
                q2, do2 = _stack_heads(qr_ref[cur, :], masks), _stack_heads(dor_ref[cur, :], masks)
                lh = jnp.concatenate([lr_ref[0, cur, :], lr_ref[1, cur, :]], axis=0)
                dh = jnp.concatenate([dr_ref[0, cur, :], dr_ref[1, cur, :]], axis=0)
                p_c = jnp.exp(_dot(q2, kr_ref[cur, :].astype(BF), _NT) + bias_ref[0] - lh)
                pd_ref[slot, 0] = p_c.astype(BF)
                pd_ref[slot, 1] = (p_c * (_dot(do2, vr_ref[cur, :].astype(BF), _NT) - dh)).astype(BF)
                if nblk > 1:
                    bias_p = bias_ref[1] + jnp.where((bi % nblk) != 0, 0.0, -jnp.inf)
                    p_p = jnp.exp(_dot(q2, kr_ref[prev, :].astype(BF), _NT) + bias_p - lh)
                    pd_ref[slot, 2] = p_p.astype(BF)
                    pd_ref[slot, 3] = (p_p * (_dot(do2, vr_ref[prev, :].astype(BF), _NT) - dh)).astype(BF)

            def grads(bi, slot):
                cur, prev = _blocks(bi)
                q2, do2 = _stack_heads(qr_ref[cur, :], masks), _stack_heads(dor_ref[cur, :], masks)
                p_c, ds_c = pd_ref[slot, 0], pd_ref[slot, 1]
                dq2 = _dot(ds_c, kr_ref[cur, :].astype(BF))
                dkr_ref[cur, :] += _dot(ds_c, q2, _TN)
                dvr_ref[cur, :] += _dot(p_c, do2, _TN)
                if nblk > 1:
                    p_p, ds_p = pd_ref[slot, 2], pd_ref[slot, 3]
                    dq2 = dq2 + _dot(ds_p, kr_ref[prev, :].astype(BF))
                    dkr_ref[prev, :] += _dot(ds_p, q2, _TN)
                    dvr_ref[prev, :] += _dot(p_p, do2, _TN)
                dqr_ref[cur, :] = dq2[:BLK] * masks[0] + dq2[BLK:] * masks[1]

            def pair(j, carry):
                grads(2 * j, 0)
                probs(2 * j + 1, 1)
                grads(2 * j + 1, 1)
                probs(jnp.minimum(2 * j + 2, NBLK - 1), 0)
                return carry

            probs(0, 0)
            lax.fori_loop(0, NBLK // 2, pair, 0)
            if dil > 1:
                for rows, src in _pieces(dil):
                    srows = pl.ds(src, BLK)
                    qr_ref[rows, :] = dqr_ref[srows, :]
                    kr_ref[rows, :] = dkr_ref[srows, :]
                    vr_ref[rows, :] = dvr_ref[srows, :]
            tq, tk, tv = (qr_ref, kr_ref, vr_ref) if dil > 1 else (dqr_ref, dkr_ref, dvr_ref)
            for c in range(T // BLK):
                rows = pl.ds(BLK * c, BLK)
                cs, sa, sb = cos_ref[rows, :], sa_ref[rows, :], sb_ref[rows, :]
                dq_ref[rows, :] = _rope_t(tq[rows, :] * QK_SCALE, cs, sa, sb).astype(BF)
                dk_ref[rows, :] = _rope_t(tk[rows, :], cs, sa, sb).astype(BF)
                dv_ref[rows, :] = tv[rows, :].astype(BF)

        for gi in range(3):
            pl.when(g == gi)(functools.partial(group, gi))

    c0 = ATT_COL0 // BLK
    zspec = lambda part: pl.BlockSpec((T, BLK), lambda p, g, part=part: (0, c0 + 12 * part + 4 * g + p))
    pspec = pl.BlockSpec((T, BLK), lambda p, g: (0, p))
    gspec = pl.BlockSpec((T, BLK), lambda p, g: (0, 4 * g + p))
    table = pl.BlockSpec((T, BLK), lambda p, g: (0, 0))
    big = lambda: pltpu.VMEM((T, BLK), F32)
    two = lambda: pltpu.VMEM((2, T, BLK), F32)
    return pl.pallas_call(
        body, name="attn_bwd", grid=(4, 3),
        in_specs=[zspec(0), zspec(1), zspec(2),
                  pl.BlockSpec((T, BLK), lambda p, g: (0, AG_COL0 // BLK + p)), table, table, table,
                  pspec, pl.BlockSpec((T, BLK), lambda p, g: (0, 2 * p)),
                  pl.BlockSpec((T, BLK), lambda p, g: (0, 2 * p + 1)), pspec],
        out_specs=[gspec, gspec, gspec, pspec],
        out_shape=[jax.ShapeDtypeStruct((T, 1536), BF), jax.ShapeDtypeStruct((T, 1536), BF),
                   jax.ShapeDtypeStruct((T, 1536), BF), jax.ShapeDtypeStruct((T, 512), BF)],
        scratch_shapes=[pltpu.VMEM((2, 2 * BLK, BLK), F32), two(), big(), big(), big(), big(),
                        two(), two(), big(), big(), big(), pltpu.VMEM((2, 4, 2 * BLK, BLK), BF), big()],
        compiler_params=_params(("parallel", "arbitrary")),
    )(z, z, z, z, cos, sa, sb, opre, lse, lse, dob)


def _merge_fwd(ya, yb, z):
    tm = 256

    def body(ya_ref, yb_ref, ga_ref, gb_ref, m_ref):
        m_ref[...] = (_sigmoid(ga_ref[...]) * ya_ref[...] + _sigmoid(gb_ref[...]) * yb_ref[...]).astype(BF)

    row = pl.BlockSpec((tm, D), lambda i: (i, 0))
    return pl.pallas_call(
        body, name="merge_fwd", grid=(T // tm,),
        in_specs=[row, row, pl.BlockSpec((tm, D), lambda i: (i, GATE_COL0 // D)),
                  pl.BlockSpec((tm, D), lambda i: (i, GATE_COL0 // D + 1))],
        out_specs=row, out_shape=jax.ShapeDtypeStruct((T, D), BF),
        compiler_params=_params(("parallel",)),
    )(ya, yb, z, z)


def _out_loss(merged, w_out, x, tgt, wf):
    tm = 256

    def body(m_ref, w_ref, x_ref, t_ref, wf_ref, dout_ref, loss_ref, gwf_ref):
        @pl.when(pl.program_id(0) == 0)
        def _():
            loss_ref[...] = jnp.zeros_like(loss_ref)
            gwf_ref[...] = jnp.zeros_like(gwf_ref)

        out = x_ref[...] + _dot(m_ref[...], w_ref[...])
        r = lax.rsqrt(jnp.mean(out * out, axis=-1, keepdims=True) + EPS)
        yh = out * r
        wfv = wf_ref[...]
        err = yh * wfv - t_ref[...]
        loss_ref[...] += jnp.sum(err * err, axis=0, keepdims=True) * (0.5 / D)
        dy = err * (1.0 / D)
        gwf_ref[...] += jnp.sum(dy * yh, axis=0, keepdims=True)
        dyh = dy * wfv
        dout_ref[...] = r * (dyh - yh * jnp.mean(dyh * yh, axis=-1, keepdims=True))

    row = pl.BlockSpec((tm, D), lambda i: (i, 0))
    vec = pl.BlockSpec((1, D), lambda i: (0, 0))
    return pl.pallas_call(
        body, name="out_loss", grid=(T // tm,),
        in_specs=[row, pl.BlockSpec((D, D), lambda i: (0, 0)), row, row, vec],
        out_specs=[row, vec, vec],
        out_shape=[jax.ShapeDtypeStruct((T, D), F32), jax.ShapeDtypeStruct((1, D), F32),
                   jax.ShapeDtypeStruct((1, D), F32)],
        compiler_params=_params(("arbitrary",)),
    )(merged, w_out, x, tgt, wf)


def _merge_bwd(dm, ya, yb, z):
    tm = 256

    def body(dm_ref, ya_ref, yb_ref, ga_ref, gb_ref, dya_ref, dyb_ref, dg_ref):
        dmv = dm_ref[...]
        sa, sb = _sigmoid(ga_ref[...]), _sigmoid(gb_ref[...])
        dya_ref[...] = (sa * dmv).astype(BF)
        dyb_ref[...] = (sb * dmv).astype(BF)
        dg_ref[:, :D] = (dmv * ya_ref[...] * sa * (1.0 - sa)).astype(BF)
        dg_ref[:, D:] = (dmv * yb_ref[...] * sb * (1.0 - sb)).astype(BF)

    row = pl.BlockSpec((tm, D), lambda i: (i, 0))
    return pl.pallas_call(
        body, name="merge_bwd", grid=(T // tm,),
        in_specs=[row, row, row, pl.BlockSpec((tm, D), lambda i: (i, GATE_COL0 // D)),
                  pl.BlockSpec((tm, D), lambda i: (i, GATE_COL0 // D + 1))],
        out_specs=[row, row, pl.BlockSpec((tm, 2 * D), lambda i: (i, 0))],
        out_shape=[jax.ShapeDtypeStruct((T, D), BF), jax.ShapeDtypeStruct((T, D), BF),
                   jax.ShapeDtypeStruct((T, 2 * D), BF)],
        compiler_params=_params(("parallel",)),
    )(dm, ya, yb, z, z)


def _rope_inv_freq():
    inv = ROPE_THETA ** (-jnp.arange(0, 64, 2, dtype=F32) / 64)
    return jnp.tile(inv, 4).reshape(1, BLK)


def _local_step(x, pos, norm_w, lbl, hnw, wf, tgt, w_in, w_a, w_b, w_out, shard_shapes=()):
    invf = _rope_inv_freq()
    if shard_shapes:
        h, (w_in,) = _rmsnorm_fwd(x, norm_w, side=_gather_side([w_in], WEIGHT_AXES[:1]))
        z, (w_a, w_b, w_out) = _matmul(h, w_in, tm=T, tn=512, name="z_proj",
                                       side=_gather_side([w_a, w_b, w_out], WEIGHT_AXES[1:]))
    else:
        h = _rmsnorm_fwd(x, norm_w)
        z = _matmul(h, w_in, tm=T, tn=512, name="z_proj")
    oraw, og, shist = _hgrn_fwd(z, lbl, hnw)
    cos, sa, sb = _rope_tables(pos, invf)
    ob, opre, lse = _attn_fwd(z, cos, sa, sb)
    ya = _matmul(og, w_a, tm=1024, tn=512, name="ya_proj")
    yb = _matmul(ob, w_b, tm=1024, tn=512, name="yb_proj")
    merged = _merge_fwd(ya, yb, z)
    dout, loss_vec, g_wf = _out_loss(merged, w_out, x, tgt, wf)

    dm = _matmul(dout, w_out, tb=True, tm=1024, tn=512, name="d_merged")
    g_wout = _matmul(merged, dout, ta=True, out_dtype=BF, tm=512, tn=1024, name="g_wout")
    dya, dyb, dgates = _merge_bwd(dm, ya, yb, z)
    dog = _matmul(dya, w_a, tb=True, tm=1024, tn=512, name="d_og")
    g_wa = _matmul(og, dya, ta=True, out_dtype=BF, tm=512, tn=1024, name="g_wa")
    dob = _matmul(dyb, w_b, tb=True, tm=1024, tn=512, name="d_ob")
    g_wb = _matmul(ob, dyb, ta=True, out_dtype=BF, tm=512, tn=1024, name="g_wb")
    small = [g_wa, g_wb, g_wout]
    side_s = side_w = None
    if shard_shapes:
        p3_s = _rs_partials(small, shard_shapes[1:], WEIGHT_AXES[1:], "small")
        side_s = _chip_exchange_side(p3_s, shard_shapes[1:], WEIGHT_AXES[1:])
    dz_h, dlb, g_hnw, land_s = _hgrn_bwd(z, lbl, hnw, oraw, dog, shist, side=side_s)
    dq, dk, dv, dag = _attn_bwd(z, cos, sa, sb, opre, lse, dob)
    dz_parts = [dz_h, dq, dk, dv, dag, dgates]
    if shard_shapes:
        c = lax.axis_index("c")
        half = lambda i: jnp.reshape(i, (1,)).astype(jnp.int32)
        g_send = _grad_w_in_half(h, dz_parts, half(1 - c))
        g_keep, (g_sib,) = _grad_w_in_half(h, dz_parts, half(c), side=_sibling_send_side(g_send))
        p3_w = [_add_bf16(g_keep, g_sib, "pair_sum_w_in").reshape(1, D // 2, NIN)]
        side_w = _chip_exchange_side(p3_w, shard_shapes[:1], WEIGHT_AXES[:1])
    else:
        g_big = [_grad_w_in(h, dz_parts)] + small
    gx, g_nw, land_w = _grad_x(dz_parts, w_in, x, dout, norm_w, side=side_w)
    if shard_shapes:
        g_big = _rs_finish(p3_w + p3_s, list(land_w) + list(land_s), shard_shapes, WEIGHT_AXES)
    return dict(loss_vec=loss_vec, gx=gx, g_nw=g_nw, dlb=dlb, g_hnw=g_hnw, g_wf=g_wf,
                g_win=g_big[0], g_wa=g_big[1], g_wb=g_big[2], g_wout=g_big[3])


MESH = pl.DeviceIdType.MESH
HBM = pl.BlockSpec(memory_space=pl.ANY)
WEIGHT_AXES = (1, 0, 1, 0)


def _place():
    x, y, c = lax.axis_index("x"), lax.axis_index("y"), lax.axis_index("c")
    chips = [(1 - x, y), (x, 1 - y), (1 - x, 1 - y)]
    return x, y, c, chips


def _block_half(ref, shard_shape, axis, j, half):
    r, c = shard_shape
    hr = r // 2
    if axis == 0:
        return ref.at[pl.ds(pl.multiple_of(j * r + half * hr, 16), hr), :]
    return ref.at[pl.ds(pl.multiple_of(half * hr, 16), hr), pl.ds(pl.multiple_of(j * c, 128), c)]


class _Side:
    def __init__(self, arrays, out_shapes, sems, first, last):
        self.arrays, self.out_shapes, self.sems, self.first, self.last = arrays, out_shapes, sems, first, last


def _gather_side(shards, axes):
    n = len(shards)
    shapes = [s.shape for s in shards]

    def copies(ins, outs, sems):
        send1, recv1, send2, recv2, send0, recv0 = sems
        x, y, c, chips = _place()
        me = 2 * x + y
        sib = (x, y, 1 - c)
        near = ((1 - c) * (1 - x) + c * x, (1 - c) * y + c * (1 - y))
        far = ((1 - c) * x + c * (1 - x), (1 - c) * (1 - y) + c * y)
        out = []
        for a in range(n):
            r, cc = shapes[a]
            mine = (outs[a].at[pl.ds(pl.multiple_of(me * r, 16), r), :] if axes[a] == 0
                    else outs[a].at[:, pl.ds(pl.multiple_of(me * cc, 128), cc)])
            own = pltpu.make_async_remote_copy(
                src_ref=ins[a], dst_ref=mine, send_sem=send0.at[a], recv_sem=recv0.at[a],
                device_id=sib, device_id_type=MESH)
            src = ins[a].at[pl.ds(pl.multiple_of(c * (r // 2), 16), r // 2), :]
            sends = [pltpu.make_async_remote_copy(
                src_ref=src, dst_ref=_block_half(outs[a], shapes[a], axes[a], me, c),
                send_sem=send1.at[a, k], recv_sem=recv1.at[a, k], device_id=(*chips[k], c), device_id_type=MESH)
                for k in range(2)]

            def region(chip, half):
                return _block_half(outs[a], shapes[a], axes[a], 2 * chip[0] + chip[1], half)

            def arrival(chip, k):
                reg = region(chip, c)
                return pltpu.make_async_remote_copy(
                    src_ref=reg, dst_ref=reg, send_sem=send1.at[a, k], recv_sem=recv1.at[a, k],
                    device_id=(*chip, c), device_id_type=MESH)

            def to_sibling(chip, k):
                reg = region(chip, c)
                return pltpu.make_async_remote_copy(
                    src_ref=reg, dst_ref=reg, send_sem=send2.at[a, k], recv_sem=recv2.at[a, k],
                    device_id=sib, device_id_type=MESH)

            def from_sibling(chip, k):
                reg = region(chip, 1 - c)
                return pltpu.make_async_remote_copy(
                    src_ref=reg, dst_ref=reg, send_sem=send2.at[a, k], recv_sem=recv2.at[a, k],
                    device_id=sib, device_id_type=MESH)

            relay = pltpu.make_async_remote_copy(
                src_ref=region(near, c), dst_ref=region(near, c), send_sem=send1.at[a, 2], recv_sem=recv1.at[a, 2],
                device_id=(*far, c), device_id_type=MESH)
            hops = [(arrival(near, c), to_sibling(near, c)), (arrival(far, 1 - c), to_sibling(far, 1 - c)),
                    (arrival(chips[2], 2), to_sibling(chips[2], 2))]
            back = [from_sibling(chips[k], k) for k in range(3)]
            out.append((own, sends, relay, hops, back))
        return out

    def first(ins, outs, sems):
        for own, sends, _, _, _ in copies(ins, outs, sems):
            own.start()
            for cp in sends:
                cp.start()

    def last(ins, outs, sems):
        per_array = copies(ins, outs, sems)
        for step in range(3):
            for _, _, relay, hops, _ in per_array:
                arrived, onward = hops[step]
                arrived.wait_recv()
                if step == 0:
                    relay.start()
                onward.start()
        for own, sends, relay, hops, back in per_array:
            for cp in back:
                cp.wait_recv()
            for cp in sends + [relay] + [onward for _, onward in hops]:
                cp.wait_send()
            own.wait()

    full = [(4 * r, c) if ax == 0 else (r, 4 * c) for (r, c), ax in zip(shapes, axes)]
    sems = [pltpu.SemaphoreType.DMA((n, 3)), pltpu.SemaphoreType.DMA((n, 3)),
            pltpu.SemaphoreType.DMA((n, 3)), pltpu.SemaphoreType.DMA((n, 3)),
            pltpu.SemaphoreType.DMA((n,)), pltpu.SemaphoreType.DMA((n,))]
    return _Side(list(shards), [jax.ShapeDtypeStruct(f, BF) for f in full], sems, first, last)


def _as3d(g, shard_shape, axis):
    r, c = shard_shape
    return g.reshape(4, r, c) if axis == 0 else g.reshape(1, r, 4 * c)


def _half_rows(ref3, hr, half):
    return ref3.at[:, pl.ds(pl.multiple_of(half * hr, 16), hr), :]


def _rs_pair_exchange(g3s, name):
    n = len(g3s)

    def body(*refs):
        ins, outs = refs[:n], refs[n:2 * n]
        send, recv = refs[2 * n:]
        x, y, c, _ = _place()
        cps = []
        for a in range(n):
            hr = g3s[a].shape[1] // 2
            cp = pltpu.make_async_remote_copy(
                src_ref=_half_rows(ins[a], hr, 1 - c), dst_ref=outs[a],
                send_sem=send.at[a], recv_sem=recv.at[a], device_id=(x, y, 1 - c), device_id_type=MESH)
            cp.start()
            cps.append(cp)
        for cp in cps:
            cp.wait()

    return pl.pallas_call(
        body, name=name,
        in_specs=[HBM] * n, out_specs=[HBM] * n,
        out_shape=[jax.ShapeDtypeStruct((g.shape[0], g.shape[1] // 2, g.shape[2]), BF) for g in g3s],
        scratch_shapes=[pltpu.SemaphoreType.DMA((n,)), pltpu.SemaphoreType.DMA((n,))],
    )(*g3s)


def _pair_sum(g3, land, cidx, name):
    nb, r, w = g3.shape
    hr = r // 2
    tr = 64

    def body(c_ref, g_ref, l_ref, o_ref):
        o_ref[...] = (g_ref[...].astype(F32) + l_ref[...].astype(F32)).astype(BF)

    blk = (nb, tr, w)
    return pl.pallas_call(
        body, name=name,
        grid_spec=pltpu.PrefetchScalarGridSpec(
            num_scalar_prefetch=1, grid=(hr // tr,),
            in_specs=[pl.BlockSpec(blk, lambda i, c: (0, c[0] * (hr // tr) + i, 0)),
                      pl.BlockSpec(blk, lambda i, c: (0, i, 0))],
            out_specs=pl.BlockSpec(blk, lambda i, c: (0, i, 0))),
        out_shape=jax.ShapeDtypeStruct((nb, hr, w), BF),
        compiler_params=_params(("parallel",)),
    )(cidx, g3, land)


def _chip_exchange_side(p3s, shapes, axes):
    n = len(p3s)

    def copies(ins, outs, sems):
        send, recv = sems
        x, y, c, chips = _place()
        cps = []
        for a in range(n):
            r, cc = shapes[a]
            for k, (px, py) in enumerate(chips):
                j = 2 * px + py
                src = ins[a].at[j] if axes[a] == 0 else ins[a].at[0, :, pl.ds(pl.multiple_of(j * cc, 128), cc)]
                cps.append(pltpu.make_async_remote_copy(
                    src_ref=src, dst_ref=outs[a].at[k], send_sem=send.at[a, k], recv_sem=recv.at[a, k],
                    device_id=(px, py, c), device_id_type=MESH))
        return cps

    def first(ins, outs, sems):
        for cp in copies(ins, outs, sems):
            cp.start()

    def last(ins, outs, sems):
        for cp in copies(ins, outs, sems):
            cp.wait()

    return _Side(list(p3s), [jax.ShapeDtypeStruct((3, r // 2, c), BF) for r, c in shapes],
                 [pltpu.SemaphoreType.DMA((n, 3)), pltpu.SemaphoreType.DMA((n, 3))], first, last)


def _chip_sum(p3, land, shard_shape, axis, idx, name):
    r, c = shard_shape
    hr = r // 2
    tr = 64
    nt = hr // tr

    def body(idx_ref, p_ref, l_ref, o_ref):
        acc = p_ref[...].astype(F32)
        for k in range(3):
            acc = acc + l_ref[k].astype(F32)
        o_ref[...] = acc

    own = (pl.BlockSpec((None, tr, c), lambda i, idx: (idx[0], i, 0)) if axis == 0
           else pl.BlockSpec((None, tr, c), lambda i, idx: (0, i, idx[0])))
    return pl.pallas_call(
        body, name=name,
        grid_spec=pltpu.PrefetchScalarGridSpec(
            num_scalar_prefetch=1, grid=(nt,),
            in_specs=[own, pl.BlockSpec((3, tr, c), lambda i, idx: (0, i, 0))],
            out_specs=pl.BlockSpec((tr, c), lambda i, idx: (idx[1] * nt + i, 0))),
        out_shape=jax.ShapeDtypeStruct((r, c), F32),
        compiler_params=_params(("parallel",)),
    )(idx, p3, land)


def _rs_pair_gather(fulls):
    n = len(fulls)

    def body(*refs):
        ins, outs = refs[:n], refs[n:2 * n]
        send, recv = refs[2 * n:]
        x, y, c, _ = _place()
        cps = []
        for a in range(n):
            hr = fulls[a].shape[0] // 2
            rows = pl.ds(pl.multiple_of(c * hr, 8), hr)
            cp = pltpu.make_async_remote_copy(
                src_ref=ins[a].at[rows, :], dst_ref=outs[a].at[rows, :], send_sem=send.at[a], recv_sem=recv.at[a],
                device_id=(x, y, 1 - c), device_id_type=MESH)
            cp.start()
            cps.append(cp)
        for a, cp in enumerate(cps):
            cp.wait_send()
            hr = fulls[a].shape[0] // 2
            other = pl.ds(pl.multiple_of((1 - c) * hr, 8), hr)
            pltpu.make_async_remote_copy(
                src_ref=ins[a].at[other, :], dst_ref=outs[a].at[other, :], send_sem=send.at[a], recv_sem=recv.at[a],
                device_id=(x, y, 1 - c), device_id_type=MESH).wait_recv()

    return pl.pallas_call(
        body, name="grads_pair_gather",
        in_specs=[HBM] * n, out_specs=[HBM] * n,
        out_shape=[jax.ShapeDtypeStruct(f.shape, F32) for f in fulls],
        input_output_aliases={a: a for a in range(n)},
        scratch_shapes=[pltpu.SemaphoreType.DMA((n,)), pltpu.SemaphoreType.DMA((n,))],
    )(*fulls)


def _sibling_send_side(arr):
    def copy(ins, outs, sems):
        x, y, c, _ = _place()
        return pltpu.make_async_remote_copy(
            src_ref=ins[0], dst_ref=outs[0], send_sem=sems[0].at[0], recv_sem=sems[1].at[0],
            device_id=(x, y, 1 - c), device_id_type=MESH)

    return _Side([arr], [jax.ShapeDtypeStruct(arr.shape, arr.dtype)],
                 [pltpu.SemaphoreType.DMA((1,)), pltpu.SemaphoreType.DMA((1,))],
                 lambda ins, outs, sems: copy(ins, outs, sems).start(),
                 lambda ins, outs, sems: copy(ins, outs, sems).wait())


def _add_bf16(a, b, name):
    r, c = a.shape
    tr = 64

    def body(a_ref, b_ref, o_ref):
        o_ref[...] = (a_ref[...].astype(F32) + b_ref[...].astype(F32)).astype(BF)

    blk = pl.BlockSpec((tr, c), lambda i: (i, 0))
    return pl.pallas_call(
        body, name=name, grid=(r // tr,), in_specs=[blk, blk], out_specs=blk,
        out_shape=jax.ShapeDtypeStruct((r, c), BF), compiler_params=_params(("parallel",)),
    )(a, b)


def _rs_partials(grads, shapes, axes, tag):
    cidx = jnp.reshape(lax.axis_index("c"), (1,)).astype(jnp.int32)
    g3s = [_as3d(g, s, ax) for g, s, ax in zip(grads, shapes, axes)]
    lands = _rs_pair_exchange(g3s, f"grads_pair_exchange_{tag}")
    return [_pair_sum(g3, l, cidx, f"pair_sum_{tag}_{a}") for a, (g3, l) in enumerate(zip(g3s, lands))]


def _rs_finish(p3s, landed, shapes, axes):
    x, y, c = lax.axis_index("x"), lax.axis_index("y"), lax.axis_index("c")
    idx = jnp.stack([2 * x + y, c]).astype(jnp.int32)
    fulls = [_chip_sum(p3, l2, s, ax, idx, f"chip_sum_{a}")
             for a, (p3, l2, s, ax) in enumerate(zip(p3s, landed, shapes, axes))]
    return _rs_pair_gather(fulls)


NSMALL = 8


def _small_all_reduce(g_nw, dlb, g_hnw, g_wf, loss_vec):
    def body(nw_ref, lb_ref, hn_ref, wf_ref, ls_ref, out_ref, pack_ref, buf_ref, send, recv):
        x, y, c = lax.axis_index("x"), lax.axis_index("y"), lax.axis_index("c")
        me = 4 * x + 2 * y + c
        pack_ref[...] = jnp.zeros_like(pack_ref)
        pack_ref[0:1, :] = nw_ref[...]
        pack_ref[1:2, :] = lb_ref[...]
        pack_ref[2:3, 0:HK] = hn_ref[...]
        pack_ref[3:4, :] = wf_ref[...]
        pack_ref[4:5, :] = ls_ref[...]
        buf_ref[me] = pack_ref[...]
        cps = []
        for d in range(1, 8):
            dx, dy, dc = d >> 2, (d >> 1) & 1, d & 1
            peer = (1 - x if dx else x, 1 - y if dy else y, 1 - c if dc else c)
            cp = pltpu.make_async_remote_copy(
                src_ref=pack_ref, dst_ref=buf_ref.at[me], send_sem=send.at[d - 1], recv_sem=recv.at[d - 1],
                device_id=peer, device_id_type=MESH)
            cp.start()
            cps.append(cp)
        for d in range(1, 8):
            dx, dy, dc = d >> 2, (d >> 1) & 1, d & 1
            src = 4 * (1 - x if dx else x) + 2 * (1 - y if dy else y) + (1 - c if dc else c)
            pltpu.make_async_remote_copy(
                src_ref=pack_ref, dst_ref=buf_ref.at[src], send_sem=send.at[d - 1], recv_sem=recv.at[d - 1],
                device_id=(x, y, c), device_id_type=MESH).wait_recv()
        for cp in cps:
            cp.wait_send()
        acc = buf_ref[0]
        for i in range(1, 8):
            acc = acc + buf_ref[i]
        out_ref[...] = acc

    vm = pl.BlockSpec(memory_space=pltpu.VMEM)
    return pl.pallas_call(
        body, name="small_all_reduce",
        in_specs=[vm] * 5, out_specs=vm,
        out_shape=jax.ShapeDtypeStruct((NSMALL, D), F32),
        scratch_shapes=[pltpu.VMEM((NSMALL, D), F32), pltpu.VMEM((8, NSMALL, D), F32),
                        pltpu.SemaphoreType.DMA((7,)), pltpu.SemaphoreType.DMA((7,))],
    )(g_nw, dlb, g_hnw, g_wf, loss_vec)


def _adamw_math(w, g, m, v):
    m = B1 * m + (1.0 - B1) * g
    v = B2 * v + (1.0 - B2) * (g * g)
    m_hat = m / (1.0 - B1 ** STEP)
    v_hat = v / (1.0 - B2 ** STEP)
    return -LR * (m_hat / (jnp.sqrt(v_hat) + ADAM_EPS) + WD * w), m, v


def _adamw(w, g, m, v, name):
    r, c = w.shape
    tr = 64

    def body(w_ref, g_ref, m_ref, v_ref, d_ref, nm_ref, nv_ref, go_ref):
        g = g_ref[...]
        d_ref[...], nm_ref[...], nv_ref[...] = _adamw_math(w_ref[...], g, m_ref[...], v_ref[...])
        go_ref[...] = g

    blk = pl.BlockSpec((tr, c), lambda i: (i, 0))
    return pl.pallas_call(
        body, name=name, grid=(r // tr,), in_specs=[blk] * 4, out_specs=[blk] * 4,
        out_shape=[jax.ShapeDtypeStruct((r, c), F32)] * 4,
        compiler_params=_params(("parallel",)),
    )(w, g, m, v)


def _small_update(red, lbl, params):
    def body(red_ref, *refs):
        ins, outs = refs[:12], refs[12:]
        lb = _lower_bound(ins[3][...])
        dl0 = red_ref[1:2, :] * lb * (1.0 - lb)
        row = lax.broadcasted_iota(jnp.int32, (2, D), 0)
        grads = [red_ref[0:1, :], jnp.where(row == 0, dl0, -dl0), red_ref[2:3, 0:HK], red_ref[3:4, :]]
        for i, g in enumerate(grads):
            w, m, v = ins[3 * i][...], ins[3 * i + 1][...], ins[3 * i + 2][...]
            d, nm, nv = _adamw_math(w, g, m, v)
            outs[4 * i][...] = g
            outs[4 * i + 1][...] = d
            outs[4 * i + 2][...] = nm
            outs[4 * i + 3][...] = nv
        outs[16][...] = jnp.sum(red_ref[4:5, :], axis=1, keepdims=True)

    flat = [a for p in params for a in p]
    vm = pl.BlockSpec(memory_space=pltpu.VMEM)
    shapes = [jax.ShapeDtypeStruct(p[0].shape, F32) for p in params for _ in range(4)]
    return pl.pallas_call(
        body, name="small_update",
        in_specs=[vm] * 13, out_specs=[vm] * 17,
        out_shape=shapes + [jax.ShapeDtypeStruct((1, 1), F32)],
    )(red, *flat)


def kernel(x, positions, norm_w, w_in, lb_logits, hgrn_norm_w, w_branch_a, w_branch_b, w_out, final_norm_w, loss_target, m_norm_w, m_w_in, m_lb_logits, m_hgrn_norm_w, m_w_branch_a, m_w_branch_b, m_w_out, m_final_norm_w, v_norm_w, v_w_in, v_lb_logits, v_hgrn_norm_w, v_w_branch_a, v_w_branch_b, v_w_out, v_final_norm_w):
    big_w = [w_in[0], w_branch_a[0], w_branch_b[0], w_out[0]]
    big_m = [m_w_in[0], m_w_branch_a[0], m_w_branch_b[0], m_w_out[0]]
    big_v = [v_w_in[0], v_w_branch_a[0], v_w_branch_b[0], v_w_out[0]]
    shapes = [w.shape for w in big_w]
    wf = final_norm_w.reshape(1, D)

    shards = [w.astype(BF) for w in big_w]
    loc = _local_step(x[0], positions.reshape(T, 1), norm_w, lb_logits, hgrn_norm_w, wf, loss_target[0],
                      *shards, shard_shapes=shapes)
    g_big = [loc["g_win"], loc["g_wa"], loc["g_wb"], loc["g_wout"]]
    red = _small_all_reduce(loc["g_nw"], loc["dlb"], loc["g_hnw"], loc["g_wf"], loc["loss_vec"])

    small = _small_update(red, lb_logits, [
        (norm_w, m_norm_w, v_norm_w), (lb_logits, m_lb_logits, v_lb_logits),
        (hgrn_norm_w, m_hgrn_norm_w, v_hgrn_norm_w),
        (wf, m_final_norm_w.reshape(1, D), v_final_norm_w.reshape(1, D))])
    loss = small[16].reshape(())
    sg, sd, sm, sv = ([small[4 * i + j] for i in range(4)] for j in range(4))
    for lst in (sg, sd, sm, sv):
        lst[3] = lst[3].reshape(D)
    upd = [_adamw(w, g, m, v, f"adamw_{a}") for a, (w, g, m, v) in enumerate(zip(big_w, g_big, big_m, big_v))]
    bd, bm, bv, bg = ([u[j][None] for u in upd] for j in range(4))

    def order(s, b):
        return [s[0], b[0], s[1], s[2], b[1], b[2], b[3], s[3]]

    return (loss, loc["gx"][None], *order(sg, bg), *order(sd, bd), *order(sm, bm), *order(sv, bv))
```

```python
import functools

import jax
import jax.numpy as jnp
from jax import lax
from jax.experimental import pallas as pl
from jax.experimental.pallas import tpu as pltpu

T = 2048
D = 1024
NIN = 11264
HEADS = 8
HK = 128
CH = 16
NCH = T // CH
HSTEP = 2
ATT_GROUPS = ((128, 1), (512, 4), (2048, 16))
ATT_COL0 = 4096
AG_COL0 = 8704
GATE_COL0 = 9216
EPS = 1e-6
ROPE_THETA = 10000.0
LR, B1, B2, ADAM_EPS, WD, STEP = 0.001, 0.9, 0.999, 1e-08, 0.01, 10

F32 = jnp.float32
BF = jnp.bfloat16
VMEM_LIMIT = 56 * 1024 * 1024

_NN = (((1,), (0,)), ((), ()))
_NT = (((1,), (1,)), ((), ()))
_TN = (((0,), (0,)), ((), ()))


def _dot(a, b, dims=_NN):
    return lax.dot_general(a, b, dims, preferred_element_type=F32)


def _bdot(a, b, dims=_NN):
    return lax.dot_general(a.astype(BF), b.astype(BF), dims, preferred_element_type=F32)


def _sigmoid(x):
    return jax.nn.sigmoid(x)


def _params(sem=None):
    return pltpu.CompilerParams(dimension_semantics=sem, vmem_limit_bytes=VMEM_LIMIT)


def _matmul(a, b, *, ta=False, tb=False, out_dtype=F32, tm=512, tn=512, tk=None, name, side=None):
    m = a.shape[1] if ta else a.shape[0]
    kdim = a.shape[0] if ta else a.shape[1]
    n = b.shape[0] if tb else b.shape[1]
    tk = tk or kdim
    tm, tn = min(tm, m), min(tn, n)
    nm, nn, nk = m // tm, n // tn, kdim // tk
    dims = (((0 if ta else 1,), (1 if tb else 0,)), ((), ()))
    s_arrays, s_in_specs, s_shapes, s_out_specs, s_sems = _side_io(side)
    na, no = len(s_arrays), len(s_shapes)
    nacc = 1 if nk > 1 else 0

    def body(*refs):
        a_ref, b_ref = refs[:2]
        s_ins, o_ref, s_outs = refs[2:2 + na], refs[2 + na], refs[3 + na:3 + na + no]
        scratch = refs[3 + na + no:]
        s_sem_refs = scratch[nacc:]
        i, j, k = pl.program_id(0), pl.program_id(1), pl.program_id(2)
        if side is not None:
            @pl.when((i == 0) & (j == 0) & (k == 0))
            def _():
                side.first(s_ins, s_outs, s_sem_refs)

        prod = _bdot(a_ref[...], b_ref[...], dims)
        if nk == 1:
            o_ref[...] = prod.astype(out_dtype)
        else:
            acc = scratch[0]

            @pl.when(k == 0)
            def _():
                acc[...] = prod

            @pl.when(k > 0)
            def _():
                acc[...] += prod

            @pl.when(k == nk - 1)
            def _():
                o_ref[...] = acc[...].astype(out_dtype)

        if side is not None:
            @pl.when((i == nm - 1) & (j == nn - 1) & (k == nk - 1))
            def _():
                side.last(s_ins, s_outs, s_sem_refs)

    a_spec = pl.BlockSpec((tk, tm), lambda i, j, k: (k, i)) if ta else pl.BlockSpec((tm, tk), lambda i, j, k: (i, k))
    b_spec = pl.BlockSpec((tn, tk), lambda i, j, k: (j, k)) if tb else pl.BlockSpec((tk, tn), lambda i, j, k: (k, j))
    sem = ("parallel", "parallel", "arbitrary") if side is None else ("arbitrary",) * 3
    out = pl.pallas_call(
        body, name=name, grid=(nm, nn, nk),
        in_specs=[a_spec, b_spec] + s_in_specs,
        out_specs=[pl.BlockSpec((tm, tn), lambda i, j, k: (i, j))] + s_out_specs,
        out_shape=[jax.ShapeDtypeStruct((m, n), out_dtype)] + s_shapes,
        scratch_shapes=([pltpu.VMEM((tm, tn), F32)] if nk > 1 else []) + s_sems,
        compiler_params=_params(sem),
    )(a, b, *s_arrays)
    return out[0] if side is None else (out[0], out[1:])


DZ_TILE = 512


def _part_offsets(parts):
    counts = [p.shape[1] // DZ_TILE for p in parts]
    offs = [sum(counts[:i]) for i in range(len(parts))]
    return counts, offs


def _part_spec(rows, cnt, off, tile_axis):
    def index(*g):
        return (0 if rows is None else g[0], jnp.clip(g[tile_axis] - off, 0, cnt - 1))
    return index


def _grad_w_in(h, parts):
    counts, offs = _part_offsets(parts)
    n = len(parts)

    def body(h_ref, *refs):
        o_ref = refs[n]
        j = pl.program_id(0)
        for p_ref, cnt, off in zip(refs[:n], counts, offs):
            @pl.when((j >= off) & (j < off + cnt))
            def _(p_ref=p_ref):
                o_ref[...] = _bdot(h_ref[...], p_ref[...], _TN).astype(BF)

    return pl.pallas_call(
        body, name="g_win", grid=(sum(counts),),
        in_specs=[pl.BlockSpec((T, D), lambda j: (0, 0))] +
                 [pl.BlockSpec((T, DZ_TILE), _part_spec(None, c, o, 0)) for c, o in zip(counts, offs)],
        out_specs=pl.BlockSpec((D, DZ_TILE), lambda j: (0, j)),
        out_shape=jax.ShapeDtypeStruct((D, NIN), BF),
        compiler_params=_params(("parallel",)),
    )(h, *parts)


def _grad_w_in_half(h, parts, half_idx, side=None):
    counts, offs = _part_offsets(parts)
    n = len(parts)
    nj = sum(counts)
    s_arrays, s_in_specs, s_shapes, s_out_specs, s_sems = _side_io(side)
    na, no = len(s_arrays), len(s_shapes)

    def body(idx_ref, h_ref, *refs):
        s_ins, o_ref, s_outs, s_sem_refs = refs[n:n + na], refs[n + na], refs[n + na + 1:n + na + 1 + no], refs[n + na + 1 + no:]
        j = pl.program_id(0)
        if side is not None:
            @pl.when(j == 0)
            def _():
                side.first(s_ins, s_outs, s_sem_refs)

        for p_ref, cnt, off in zip(refs[:n], counts, offs):
            @pl.when((j >= off) & (j < off + cnt))
            def _(p_ref=p_ref):
                o_ref[...] = _bdot(h_ref[...], p_ref[...], _TN).astype(BF)

        if side is not None:
            @pl.when(j == nj - 1)
            def _():
                side.last(s_ins, s_outs, s_sem_refs)

    def part_spec(cnt, off):
        return pl.BlockSpec((T, DZ_TILE), lambda j, idx: (0, jnp.clip(j - off, 0, cnt - 1)))

    out = pl.pallas_call(
        body, name="g_win_half" if side is None else "g_win_half_carrying",
        grid_spec=pltpu.PrefetchScalarGridSpec(
            num_scalar_prefetch=1, grid=(nj,),
            in_specs=[pl.BlockSpec((T, D // 2), lambda j, idx: (0, idx[0]))] +
                     [part_spec(c, o) for c, o in zip(counts, offs)] + s_in_specs,
            out_specs=[pl.BlockSpec((D // 2, DZ_TILE), lambda j, idx: (0, j))] + s_out_specs,
            scratch_shapes=s_sems),
        out_shape=[jax.ShapeDtypeStruct((D // 2, NIN), BF)] + s_shapes,
        compiler_params=_params(("parallel",) if side is None else ("arbitrary",)),
    )(half_idx, h, *parts, *s_arrays)
    return out[0] if side is None else (out[0], out[1:])


def _side_io(side):
    if side is None:
        return [], [], [], [], []
    return (side.arrays, [HBM] * len(side.arrays), side.out_shapes, [HBM] * len(side.out_shapes), side.sems)


def _grad_x(parts, w_in, x, dout, norm_w, side=None):
    counts, offs = _part_offsets(parts)
    n = len(parts)
    tm = 1024
    nm, nk = T // tm, sum(counts)
    s_arrays, s_in_specs, s_shapes, s_out_specs, s_sems = _side_io(side)
    na, no = len(s_arrays), len(s_shapes)

    def body(*refs):
        w_ref, x_ref, dout_ref, nw_ref = refs[n:n + 4]
        s_ins = refs[n + 4:n + 4 + na]
        gx_ref, gw_ref = refs[n + 4 + na:n + 6 + na]
        s_outs = refs[n + 6 + na:n + 6 + na + no]
        acc = refs[n + 6 + na + no]
        s_sem_refs = refs[n + 7 + na + no:]
        i, k = pl.program_id(0), pl.program_id(1)

        @pl.when((i == 0) & (k == 0))
        def _():
            gw_ref[...] = jnp.zeros_like(gw_ref)
            if side is not None:
                side.first(s_ins, s_outs, s_sem_refs)

        @pl.when(k == 0)
        def _():
            acc[...] = jnp.zeros_like(acc)

        for p_ref, cnt, off in zip(refs[:n], counts, offs):
            @pl.when((k >= off) & (k < off + cnt))
            def _(p_ref=p_ref):
                acc[...] += _bdot(p_ref[...], w_ref[...], _NT)

        @pl.when(k == nk - 1)
        def _():
            gw = jnp.zeros((1, D), F32)
            for c in range(tm // BLK):
                rows = pl.ds(BLK * c, BLK)
                xv, dhv = x_ref[rows, :], acc[rows, :]
                r = lax.rsqrt(jnp.mean(xv * xv, axis=-1, keepdims=True) + EPS)
                nrm = xv * r
                dn = dhv * nw_ref[...]
                gw = gw + jnp.sum(dhv * nrm, axis=0, keepdims=True)
                gx_ref[rows, :] = dout_ref[rows, :] + r * (dn - nrm * jnp.mean(dn * nrm, axis=-1, keepdims=True))
            gw_ref[...] += gw

        if side is not None:
            @pl.when((i == nm - 1) & (k == nk - 1))
            def _():
                side.last(s_ins, s_outs, s_sem_refs)

    row = pl.BlockSpec((tm, D), lambda i, k: (i, 0))
    vec = pl.BlockSpec((1, D), lambda i, k: (0, 0))
    out = pl.pallas_call(
        body, name="grad_x", grid=(nm, nk),
        in_specs=[pl.BlockSpec((tm, DZ_TILE), _part_spec(0, c, o, 1)) for c, o in zip(counts, offs)] +
                 [pl.BlockSpec((D, DZ_TILE), lambda i, k: (0, k)), row, row, vec] + s_in_specs,
        out_specs=[row, vec] + s_out_specs,
        out_shape=[jax.ShapeDtypeStruct((T, D), F32), jax.ShapeDtypeStruct((1, D), F32)] + s_shapes,
        scratch_shapes=[pltpu.VMEM((tm, D), F32)] + s_sems,
        compiler_params=_params(("arbitrary", "arbitrary")),
    )(*parts, w_in, x, dout, norm_w, *s_arrays)
    return out[0], out[1], out[2:]


def _norm_and_rope_tables(x, w, pos, invf, side=None):
    tm = 256
    nm = T // tm
    s_arrays, s_in_specs, s_shapes, s_out_specs, s_sems = _side_io(side)
    na, no = len(s_arrays), len(s_shapes)

    def body(*refs):
        x_ref, w_ref, pos_ref, invf_ref = refs[:4]
        s_ins = refs[4:4 + na]
        h_ref, cos_ref, sa_ref, sb_ref = refs[4 + na:8 + na]
        s_outs, s_sem_refs = refs[8 + na:8 + na + no], refs[8 + na + no:]
        if side is not None:
            @pl.when(pl.program_id(0) == 0)
            def _():
                side.first(s_ins, s_outs, s_sem_refs)

        xv = x_ref[...]
        r = lax.rsqrt(jnp.mean(xv * xv, axis=-1, keepdims=True) + EPS)
        h_ref[...] = (xv * r * w_ref[...]).astype(BF)
        first = (lax.broadcasted_iota(jnp.int32, (tm, 128), 1) % 64) < 32
        ang = pos_ref[...].astype(F32) * invf_ref[...]
        s = jnp.sin(ang)
        cos_ref[...] = jnp.cos(ang)
        sa_ref[...] = jnp.where(first, -s, 0.0)
        sb_ref[...] = jnp.where(first, 0.0, s)
        if side is not None:
            @pl.when(pl.program_id(0) == nm - 1)
            def _():
                side.last(s_ins, s_outs, s_sem_refs)

    tab = pl.BlockSpec((tm, 128), lambda i: (i, 0))
    out = pl.pallas_call(
        body, name="norm_and_rope_tables", grid=(nm,),
        in_specs=[pl.BlockSpec((tm, D), lambda i: (i, 0)), pl.BlockSpec((1, D), lambda i: (0, 0)),
                  pl.BlockSpec((tm, 1), lambda i: (i, 0)), pl.BlockSpec((1, 128), lambda i: (0, 0))] + s_in_specs,
        out_specs=[pl.BlockSpec((tm, D), lambda i: (i, 0)), tab, tab, tab] + s_out_specs,
        out_shape=[jax.ShapeDtypeStruct((T, D), BF)] + [jax.ShapeDtypeStruct((T, 128), F32)] * 3 + s_shapes,
        scratch_shapes=s_sems,
        compiler_params=_params(("parallel",) if side is None else ("arbitrary",)),
    )(x, w, pos, invf, *s_arrays)
    return out[0], out[1], out[2], out[3], out[4:]


def _lower_bound(lbl):
    mx = jnp.max(lbl, axis=0, keepdims=True)
    e = jnp.exp(lbl - mx)
    return e[0:1] / jnp.sum(e, axis=0, keepdims=True)


def _cumsum_rows(g, rows):
    b = g
    sh = 1
    while sh < CH:
        b = b + jnp.where(rows >= sh, pltpu.roll(b, sh, axis=0), 0.0)
        sh *= 2
    return b


def _rev_cumsum_rows(g, rows):
    b = g
    sh = 1
    while sh < CH:
        b = b + jnp.where(rows < CH - sh, pltpu.roll(b, CH - sh, axis=0), 0.0)
        sh *= 2
    return b


SUB = CH // 2


def _direct_block(qb, kb, vb, bb, rows8):
    ob = jnp.zeros_like(qb)
    for s in range(SUB):
        e_s = jnp.exp(jnp.where(rows8 >= s, bb - bb[s:s + 1], -jnp.inf))
        ob = ob + jnp.sum(qb * e_s * kb[s:s + 1], axis=1, keepdims=True) * vb[s:s + 1]
    return ob


def _direct_block_bwd(qb, kb, vb, bb, dob, rows8, rowc8):
    dq = dk = dv = db = jnp.zeros_like(qb)
    for s in range(SUB):
        one = (rowc8 == s).astype(F32)
        ks, vs = kb[s:s + 1], vb[s:s + 1]
        e_s = jnp.exp(jnp.where(rows8 >= s, bb - bb[s:s + 1], -jnp.inf))
        qes = qb * e_s
        w = qes * ks
        a = jnp.sum(w, axis=1, keepdims=True)
        da = jnp.sum(dob * vs, axis=1, keepdims=True)
        dv = dv + one * jnp.sum(a * dob, axis=0, keepdims=True)
        dq = dq + da * e_s * ks
        dk = dk + one * jnp.sum(da * qes, axis=0, keepdims=True)
        u = da * w
        db = db + u - one * jnp.sum(u, axis=0, keepdims=True)
    return dq, dk, dv, db


def _cross_factors(q, k, b):
    ref = b[SUB - 1:SUB]
    e_hi, e_lo = jnp.exp(b[SUB:] - ref), jnp.exp(ref - b[:SUB])
    return q[SUB:] * e_hi, k[:SUB] * e_lo, e_hi, e_lo


def _intra_fwd(q, k, v, b, rows8):
    lo = _direct_block(q[:SUB], k[:SUB], v[:SUB], b[:SUB], rows8)
    hi = _direct_block(q[SUB:], k[SUB:], v[SUB:], b[SUB:], rows8)
    qe_hi, ke_lo, _, _ = _cross_factors(q, k, b)
    for s in range(SUB):
        hi = hi + jnp.sum(qe_hi * ke_lo[s:s + 1], axis=1, keepdims=True) * v[s:s + 1]
    return jnp.concatenate([lo, hi], axis=0)


def _intra_bwd(q, k, v, b, do, rows8, rowc8):
    dq_lo, dk_lo, dv_lo, db_lo = _direct_block_bwd(q[:SUB], k[:SUB], v[:SUB], b[:SUB], do[:SUB], rows8, rowc8)
    dq_hi, dk_hi, dv_hi, db_hi = _direct_block_bwd(q[SUB:], k[SUB:], v[SUB:], b[SUB:], do[SUB:], rows8, rowc8)
    qe_hi, ke_lo, e_hi, e_lo = _cross_factors(q, k, b)
    do_hi, v_lo = do[SUB:], v[:SUB]
    dqe = dke = jnp.zeros_like(qe_hi)
    for s in range(SUB):
        one = (rowc8 == s).astype(F32)
        a = jnp.sum(qe_hi * ke_lo[s:s + 1], axis=1, keepdims=True)
        da = jnp.sum(do_hi * v_lo[s:s + 1], axis=1, keepdims=True)
        dv_lo = dv_lo + one * jnp.sum(a * do_hi, axis=0, keepdims=True)
        dqe = dqe + da * ke_lo[s:s + 1]
        dke = dke + one * jnp.sum(da * qe_hi, axis=0, keepdims=True)
    u_hi, u_lo = dqe * qe_hi, dke * ke_lo
    d_ref = jnp.sum(u_lo, axis=0, keepdims=True) - jnp.sum(u_hi, axis=0, keepdims=True)
    db_lo = db_lo - u_lo + (rowc8 == SUB - 1).astype(F32) * d_ref
    cat = lambda lo, hi: jnp.concatenate([lo, hi], axis=0)
    return (cat(dq_lo, dq_hi + dqe * e_hi), cat(dk_lo + dke * e_lo, dk_hi), cat(dv_lo, dv_hi),
            cat(db_lo, db_hi + u_hi))


def _hgrn_fwd(z, lbl, nw):
    def body(hq_ref, hf_ref, hi_ref, hg_ref, lbl_ref, nw_ref, oraw_ref, og_ref, sh_ref, st_ref):
        @pl.when(pl.program_id(0) == 0)
        def _():
            st_ref[...] = jnp.zeros_like(st_ref)

        lb_all = _lower_bound(lbl_ref[...])
        rows = lax.broadcasted_iota(jnp.int32, (CH, HK), 0)
        rows8 = lax.broadcasted_iota(jnp.int32, (SUB, HK), 0)
        nwv = nw_ref[...]
        for cc, h in [(cc, h) for cc in range(HSTEP) for h in range(HEADS)]:
            rs = slice(CH * cc, CH * (cc + 1))
            sl = slice(HK * h, HK * (h + 1))
            lb = lb_all[:, sl]
            hq, hf, v, hg = hq_ref[rs, sl], hf_ref[rs, sl], hi_ref[rs, sl], hg_ref[rs, sl]
            q = hq * _sigmoid(hq)
            f = lb + (1.0 - lb) * _sigmoid(hf)
            k = 1.0 - f
            b = _cumsum_rows(jnp.log(f), rows)
            sh_ref[cc, h] = st_ref[h]
            o = _bdot(q * jnp.exp(b), st_ref[h], _NT) + _intra_fwd(q, k, v, b, rows8)
            bl = b[CH - 1:CH]
            st_ref[h] = st_ref[h] * jnp.exp(bl)
            st_ref[h] += _bdot(v, k * jnp.exp(bl - b), _TN)
            oraw_ref[rs, sl] = o
            nrm = o * lax.rsqrt(jnp.mean(o * o, axis=1, keepdims=True) + EPS)
            og_ref[rs, sl] = (nrm * nwv * (hg * _sigmoid(hg))).astype(BF)

    zblk = lambda c: pl.BlockSpec((CH * HSTEP, D), lambda i, c=c: (i, c))
    return pl.pallas_call(
        body, name="hgrn_fwd", grid=(NCH // HSTEP,),
        in_specs=[zblk(0), zblk(1), zblk(2), zblk(3),
                  pl.BlockSpec((2, D), lambda i: (0, 0)), pl.BlockSpec((1, HK), lambda i: (0, 0))],
        out_specs=[zblk(0), zblk(0),
                   pl.BlockSpec((HSTEP, HEADS, HK, HK), lambda i: (i, 0, 0, 0))],
        out_shape=[jax.ShapeDtypeStruct((T, D), F32), jax.ShapeDtypeStruct((T, D), BF),
                   jax.ShapeDtypeStruct((NCH, HEADS, HK, HK), F32)],
        scratch_shapes=[pltpu.VMEM((HEADS, HK, HK), F32)],
        compiler_params=_params(("arbitrary",)),
    )(z, z, z, z, lbl, nw)


def _hgrn_bwd(z, lbl, nw, oraw, dog, shist, side=None):
    hstep = 1
    s_arrays, s_in_specs, s_shapes, s_out_specs, s_sems = _side_io(side)
    na, no = len(s_arrays), len(s_shapes)

    def body(*refs):
        hq_ref, hf_ref, hi_ref, hg_ref, lbl_ref, nw_ref, oraw_ref, dog_ref, sh_ref = refs[:9]
        s_ins = refs[9:9 + na]
        dz_ref, dlb_ref, dnw_ref = refs[9 + na:12 + na]
        s_outs = refs[12 + na:12 + na + no]
        dst_ref = refs[12 + na + no]
        s_sem_refs = refs[13 + na + no:]

        @pl.when(pl.program_id(0) == 0)
        def _():
            dst_ref[...] = jnp.zeros_like(dst_ref)
            dlb_ref[...] = jnp.zeros_like(dlb_ref)
            dnw_ref[...] = jnp.zeros_like(dnw_ref)
            if side is not None:
                side.first(s_ins, s_outs, s_sem_refs)

        lb_all = _lower_bound(lbl_ref[...])
        rows = lax.broadcasted_iota(jnp.int32, (CH, HK), 0)
        rowc = lax.broadcasted_iota(jnp.int32, (CH, 1), 0)
        rows8 = lax.broadcasted_iota(jnp.int32, (SUB, HK), 0)
        rowc8 = lax.broadcasted_iota(jnp.int32, (SUB, 1), 0)
        nwv = nw_ref[...]
        dnw = jnp.zeros((1, HK), F32)
        for cc, h in [(cc, h) for cc in reversed(range(hstep)) for h in range(HEADS)]:
            rs = slice(CH * cc, CH * (cc + 1))
            sl = slice(HK * h, HK * (h + 1))
            lb = lb_all[:, sl]
            hq, hf, v, hg = hq_ref[rs, sl], hf_ref[rs, sl], hi_ref[rs, sl], hg_ref[rs, sl]
            o, dg_out = oraw_ref[rs, sl], dog_ref[rs, sl]
            sg = _sigmoid(hg)
            sil = hg * sg
            r = lax.rsqrt(jnp.mean(o * o, axis=1, keepdims=True) + EPS)
            nrm = o * r
            d_hg = dg_out * (nrm * nwv) * (sg * (1.0 + hg * (1.0 - sg)))
            dn = dg_out * nwv * sil
            dnw = dnw + jnp.sum(dg_out * nrm * sil, axis=0, keepdims=True)
            do = r * (dn - nrm * jnp.mean(dn * nrm, axis=1, keepdims=True))
            sq = _sigmoid(hq)
            q = hq * sq
            sig = _sigmoid(hf)
            f = lb + (1.0 - lb) * sig
            k = 1.0 - f
            b = _cumsum_rows(jnp.log(f), rows)
            eb = jnp.exp(b)
            qe = q * eb
            bl = b[CH - 1:CH]
            ebl = jnp.exp(bl)
            kdec = jnp.exp(bl - b)
            ke = k * kdec
            dqe = _bdot(do, sh_ref[cc, h])
            dq = dqe * eb
            db = dqe * qe
            dke = _bdot(v, dst_ref[h])
            dv = _bdot(ke, dst_ref[h], _NT)
            dk = dke * kdec
            rr = dke * ke
            db = db - rr
            db_last = (jnp.sum(rr, axis=0, keepdims=True)
                       + ebl * jnp.sum(dst_ref[h] * sh_ref[cc, h], axis=0, keepdims=True))
            dst_ref[h] = dst_ref[h] * ebl
            dst_ref[h] += _bdot(do, qe, _TN)
            dq_i, dk_i, dv_i, db_i = _intra_bwd(q, k, v, b, do, rows8, rowc8)
            dq, dk, dv = dq + dq_i, dk + dk_i, dv + dv_i
            db = db + db_i + (rowc == CH - 1).astype(F32) * db_last
            dgl = _rev_cumsum_rows(db, rows)
            df = dgl / f - dk
            dlb_ref[:, sl] += jnp.sum(df * (1.0 - sig), axis=0, keepdims=True)
            dz_ref[rs, sl] = (dq * (sq * (1.0 + hq * (1.0 - sq)))).astype(BF)
            dz_ref[rs, D + HK * h:D + HK * (h + 1)] = (df * (1.0 - lb) * sig * (1.0 - sig)).astype(BF)
            dz_ref[rs, 2 * D + HK * h:2 * D + HK * (h + 1)] = dv.astype(BF)
            dz_ref[rs, 3 * D + HK * h:3 * D + HK * (h + 1)] = d_hg.astype(BF)
        dnw_ref[...] += dnw
        if side is not None:
            @pl.when(pl.program_id(0) == NCH // hstep - 1)
            def _():
                side.last(s_ins, s_outs, s_sem_refs)

    rev = lambda i: NCH // hstep - 1 - i
    zblk = lambda c: pl.BlockSpec((CH * hstep, D), lambda i, c=c: (rev(i), c))
    out = pl.pallas_call(
        body, name="hgrn_bwd", grid=(NCH // hstep,),
        in_specs=[zblk(0), zblk(1), zblk(2), zblk(3),
                  pl.BlockSpec((2, D), lambda i: (0, 0)), pl.BlockSpec((1, HK), lambda i: (0, 0)),
                  zblk(0), zblk(0),
                  pl.BlockSpec((hstep, HEADS, HK, HK), lambda i: (rev(i), 0, 0, 0))] + s_in_specs,
        out_specs=[pl.BlockSpec((CH * hstep, 4 * D), lambda i: (rev(i), 0)),
                   pl.BlockSpec((1, D), lambda i: (0, 0)), pl.BlockSpec((1, HK), lambda i: (0, 0))] + s_out_specs,
        out_shape=[jax.ShapeDtypeStruct((T, 4 * D), BF), jax.ShapeDtypeStruct((1, D), F32),
                   jax.ShapeDtypeStruct((1, HK), F32)] + s_shapes,
        scratch_shapes=[pltpu.VMEM((HEADS, HK, HK), F32)] + s_sems,
        compiler_params=_params(("arbitrary",)),
    )(z, z, z, z, lbl, nw, oraw, dog, shist, *s_arrays)
    return out[0], out[1], out[2], out[3:]


BLK = 128
NBLK = T // BLK
QK_SCALE = 0.125


def _head_masks():
    lane = lax.broadcasted_iota(jnp.int32, (1, BLK), 1)
    return [(lane < 64).astype(F32), (lane >= 64).astype(F32)]


def _pieces(dil):
    m = T // dil
    out = []
    for r in range(dil):
        for j in range(m // BLK):
            start = r + dil * BLK * j
            rows = pl.ds(start, BLK, stride=dil) if dil > 1 else pl.ds(start, BLK)
            out.append((rows, r * m + BLK * j))
    return out


def _rope(x, c, sa, sb):
    return x * c + pltpu.roll(x, 96, axis=1) * sa + pltpu.roll(x, 32, axis=1) * sb


def _rope_t(d, c, sa, sb):
    return d * c + pltpu.roll(d * sa, 32, axis=1) + pltpu.roll(d * sb, 96, axis=1)


def _rope_and_regroup(dil, q_ref, k_ref, v_ref, tables, stage_q, stage_k, qr_ref, kr_ref, vr_ref):
    cos_ref, sa_ref, sb_ref = tables
    to_q, to_k = (qr_ref, kr_ref) if dil == 1 else (stage_q, stage_k)
    for c in range(T // BLK):
        rows = pl.ds(BLK * c, BLK)
        cs, sa, sb = cos_ref[rows, :], sa_ref[rows, :], sb_ref[rows, :]
        to_q[rows, :] = _rope(q_ref[rows, :], cs, sa, sb) * QK_SCALE
        to_k[rows, :] = _rope(k_ref[rows, :], cs, sa, sb)
    for rows, dst in _pieces(dil):
        drows = pl.ds(dst, BLK)
        if dil > 1:
            qr_ref[drows, :] = stage_q[rows, :]
            kr_ref[drows, :] = stage_k[rows, :]
        vr_ref[drows, :] = v_ref[rows, :]


def _window_bias(bias_ref):
    ii = lax.broadcasted_iota(jnp.int32, (2 * BLK, BLK), 0) % BLK
    jj = lax.broadcasted_iota(jnp.int32, (2 * BLK, BLK), 1)
    bias_ref[0] = jnp.where(jj <= ii, 0.0, -jnp.inf)
    bias_ref[1] = jnp.where(jj >= ii, 0.0, -jnp.inf)


def _blocks(bi):
    if isinstance(bi, int):
        return pl.ds(bi * BLK, BLK), pl.ds(max(bi - 1, 0) * BLK, BLK)
    return (pl.ds(pl.multiple_of(bi * BLK, BLK), BLK),
            pl.ds(pl.multiple_of(jnp.maximum(bi - 1, 0) * BLK, BLK), BLK))


def _stack_heads(x, masks):
    return jnp.concatenate([x * masks[0], x * masks[1]], axis=0).astype(BF)


def _attn_fwd(z, cos, sa, sb):
    def body(q_ref, k_ref, v_ref, ag_ref, cos_ref, sa_ref, sb_ref, ob_ref, opre_ref, lse_ref,
             bias_ref, qr_ref, kr_ref, vr_ref, og_ref, lg_ref, otok_ref, ltok_ref, sc_ref):
        g = pl.program_id(1)
        masks = _head_masks()

        @pl.when(g == 0)
        def _():
            _window_bias(bias_ref)

        def group(gi):
            dil = ATT_GROUPS[gi][1]
            nblk = (T // dil) // BLK
            _rope_and_regroup(dil, q_ref, k_ref, v_ref, (cos_ref, sa_ref, sb_ref), lg_ref.at[0], lg_ref.at[1],
                              qr_ref, kr_ref, vr_ref)

            def scores(bi, slot):
                cur, prev = _blocks(bi)
                q2 = _stack_heads(qr_ref[cur, :], masks)
                sc_ref[slot, 0] = _dot(q2, kr_ref[cur, :].astype(BF), _NT) + bias_ref[0]
                if nblk > 1:
                    sc_ref[slot, 1] = (_dot(q2, kr_ref[prev, :].astype(BF), _NT)
                                       + (bias_ref[1] + jnp.where((bi % nblk) != 0, 0.0, -jnp.inf)))

            def finish(bi, slot):
                cur, prev = _blocks(bi)
                s_c, vc = sc_ref[slot, 0], vr_ref[cur, :].astype(BF)
                if nblk > 1:
                    s_p, vp = sc_ref[slot, 1], vr_ref[prev, :].astype(BF)
                    mx = jnp.max(jnp.maximum(s_c, s_p), axis=1, keepdims=True)
                    p_c, p_p = jnp.exp(s_c - mx), jnp.exp(s_p - mx)
                    den = jnp.sum(p_c + p_p, axis=1, keepdims=True)
                    oh = _dot(p_c.astype(BF), vc) + _dot(p_p.astype(BF), vp)
                else:
                    mx = jnp.max(s_c, axis=1, keepdims=True)
                    p_c = jnp.exp(s_c - mx)
                    den = jnp.sum(p_c, axis=1, keepdims=True)
                    oh = _dot(p_c.astype(BF), vc)
                on = oh / den
                lsev = jnp.broadcast_to(mx + jnp.log(den), (2 * BLK, BLK))
                og_ref[cur, :] = on[:BLK] * masks[0] + on[BLK:] * masks[1]
                lg_ref[0, cur, :] = lsev[:BLK]
                lg_ref[1, cur, :] = lsev[BLK:]

            def pair(j, carry):
                finish(2 * j, 0)
                scores(2 * j + 1, 1)
                finish(2 * j + 1, 1)
                scores(jnp.minimum(2 * j + 2, NBLK - 1), 0)
                return carry

            scores(0, 0)
            lax.fori_loop(0, NBLK // 2, pair, 0)
            for rows, src in _pieces(dil):
                srows = pl.ds(src, BLK)
                otok_ref[gi, rows, :] = og_ref[srows, :]
                ltok_ref[gi, 0, rows, :] = lg_ref[0, srows, :]
                ltok_ref[gi, 1, rows, :] = lg_ref[1, srows, :]

        for gi in range(3):
            pl.when(g == gi)(functools.partial(group, gi))

        @pl.when(g == 2)
        def _():
            for c in range(T // BLK):
                rows = pl.ds(BLK * c, BLK)
                wts = []
                for hh in range(2):
                    l0, l1, l2 = ltok_ref[0, hh, rows, :], ltok_ref[1, hh, rows, :], ltok_ref[2, hh, rows, :]
                    mx = jnp.maximum(jnp.maximum(l0, l1), l2)
                    e0, e1, e2 = jnp.exp(l0 - mx), jnp.exp(l1 - mx), jnp.exp(l2 - mx)
                    tot = e0 + e1 + e2
                    lse_ref[rows, BLK * hh:BLK * (hh + 1)] = mx + jnp.log(tot)
                    inv = 1.0 / tot
                    wts.append([e0 * inv, e1 * inv, e2 * inv])
                o = sum((wts[0][gi] * masks[0] + wts[1][gi] * masks[1]) * otok_ref[gi, rows, :] for gi in range(3))
                ag = ag_ref[rows, :]
                opre_ref[rows, :] = o
                ob_ref[rows, :] = (o * (ag * _sigmoid(ag))).astype(BF)

    c0 = ATT_COL0 // BLK
    zspec = lambda part: pl.BlockSpec((T, BLK), lambda p, g, part=part: (0, c0 + 12 * part + 4 * g + p))
    outspec = pl.BlockSpec((T, BLK), lambda p, g: (0, p))
    table = pl.BlockSpec((T, BLK), lambda p, g: (0, 0))
    big = lambda: pltpu.VMEM((T, BLK), F32)
    return pl.pallas_call(
        body, name="attn_fwd", grid=(4, 3),
        in_specs=[zspec(0), zspec(1), zspec(2),
                  pl.BlockSpec((T, BLK), lambda p, g: (0, AG_COL0 // BLK + p)), table, table, table],
        out_specs=[outspec, outspec, pl.BlockSpec((T, 2 * BLK), lambda p, g: (0, p))],
        out_shape=[jax.ShapeDtypeStruct((T, 512), BF), jax.ShapeDtypeStruct((T, 512), F32),
                   jax.ShapeDtypeStruct((T, 8 * BLK), F32)],
        scratch_shapes=[pltpu.VMEM((2, 2 * BLK, BLK), F32), big(), big(), big(), big(),
                        pltpu.VMEM((2, T, BLK), F32), pltpu.VMEM((3, T, BLK), F32), pltpu.VMEM((3, 2, T, BLK), F32),
                        pltpu.VMEM((2, 2, 2 * BLK, BLK), F32)],
        compiler_params=_params(("parallel", "arbitrary")),
    )(z, z, z, z, cos, sa, sb)


def _attn_bwd(z, cos, sa, sb, opre, lse, dob):
    def body(q_ref, k_ref, v_ref, ag_ref, cos_ref, sa_ref, sb_ref, o_ref, lse0_ref, lse1_ref, dob_ref,
             dq_ref, dk_ref, dv_ref, dag_ref,
             bias_ref, dtok_ref, qr_ref, kr_ref, vr_ref, dor_ref, lr_ref, dr_ref,
             dqr_ref, dkr_ref, dvr_ref, pd_ref, dotok_ref):
        g = pl.program_id(1)
        masks = _head_masks()

        @pl.when(g == 0)
        def _():
            _window_bias(bias_ref)
            for c in range(T // BLK):
                rows = pl.ds(BLK * c, BLK)
                ag, dob_v, o = ag_ref[rows, :], dob_ref[rows, :], o_ref[rows, :]
                sg = _sigmoid(ag)
                dag_ref[rows, :] = (dob_v * o * (sg * (1.0 + ag * (1.0 - sg)))).astype(BF)
                do = dob_v * (ag * sg)
                dotok_ref[rows, :] = do
                prod = do * o
                for hh, mh in enumerate(masks):
                    dtok_ref[hh, rows, :] = jnp.broadcast_to(jnp.sum(prod * mh, axis=1, keepdims=True), (BLK, BLK))

        def group(gi):
            dil = ATT_GROUPS[gi][1]
            nblk = (T // dil) // BLK
            _rope_and_regroup(dil, q_ref, k_ref, v_ref, (cos_ref, sa_ref, sb_ref), dqr_ref, dvr_ref,
                              qr_ref, kr_ref, vr_ref)
            for rows, dst in _pieces(dil):
                drows = pl.ds(dst, BLK)
                dor_ref[drows, :] = dotok_ref[rows, :]
                for hh, lse_ref in enumerate((lse0_ref, lse1_ref)):
                    lr_ref[hh, drows, :] = lse_ref[rows, :]
                    dr_ref[hh, drows, :] = dtok_ref[hh, rows, :]
            dkr_ref[...] = jnp.zeros_like(dkr_ref)
            dvr_ref[...] = jnp.zeros_like(dvr_ref)

            def probs(bi, slot):
                cur, prev = _blocks(bi)
                q2, do2 = _stack_heads(qr_ref[cur, :], masks), _stack_heads(dor_ref[cur, :], masks)
                lh = jnp.concatenate([lr_ref[0, cur, :], lr_ref[1, cur, :]], axis=0)
                dh = jnp.concatenate([dr_ref[0, cur, :], dr_ref[1, cur, :]], axis=0)
                p_c = jnp.exp(_dot(q2, kr_ref[cur, :].astype(BF), _NT) + bias_ref[0] - lh)
                pd_ref[slot, 0] = p_c.astype(BF)
                pd_ref[slot, 1] = (p_c * (_dot(do2, vr_ref[cur, :].astype(BF), _NT) - dh)).astype(BF)
                if nblk > 1:
                    bias_p = bias_ref[1] + jnp.where((bi % nblk) != 0, 0.0, -jnp.inf)
                    p_p = jnp.exp(_dot(q2, kr_ref[prev, :].astype(BF), _NT) + bias_p - lh)
                    pd_ref[slot, 2] = p_p.astype(BF)
                    pd_ref[slot, 3] = (p_p * (_dot(do2, vr_ref[prev, :].astype(BF), _NT) - dh)).astype(BF)

            def grads(bi, slot):
                cur, prev = _blocks(bi)
                q2, do2 = _stack_heads(qr_ref[cur, :], masks), _stack_heads(dor_ref[cur, :], masks)
                p_c, ds_c = pd_ref[slot, 0], pd_ref[slot, 1]
                dq2 = _dot(ds_c, kr_ref[cur, :].astype(BF))
                dkr_ref[cur, :] += _dot(ds_c, q2, _TN)
                dvr_ref[cur, :] += _dot(p_c, do2, _TN)
                if nblk > 1:
                    p_p, ds_p = pd_ref[slot, 2], pd_ref[slot, 3]
                    dq2 = dq2 + _dot(ds_p, kr_ref[prev, :].astype(BF))
                    dkr_ref[prev, :] += _dot(ds_p, q2, _TN)
                    dvr_ref[prev, :] += _dot(p_p, do2, _TN)
                dqr_ref[cur, :] = dq2[:BLK] * masks[0] + dq2[BLK:] * masks[1]

            def pair(j, carry):
                grads(2 * j, 0)
                probs(2 * j + 1, 1)
                grads(2 * j + 1, 1)
                probs(jnp.minimum(2 * j + 2, NBLK - 1), 0)
                return carry

            probs(0, 0)
            lax.fori_loop(0, NBLK // 2, pair, 0)
            if dil > 1:
                for rows, src in _pieces(dil):
                    srows = pl.ds(src, BLK)
                    qr_ref[rows, :] = dqr_ref[srows, :]
                    kr_ref[rows, :] = dkr_ref[srows, :]
                    vr_ref[rows, :] = dvr_ref[srows, :]
            tq, tk, tv = (qr_ref, kr_ref, vr_ref) if dil > 1 else (dqr_ref, dkr_ref, dvr_ref)
            for c in range(T // BLK):
                rows = pl.ds(BLK * c, BLK)
                cs, sa, sb = cos_ref[rows, :], sa_ref[rows, :], sb_ref[rows, :]
                dq_ref[rows, :] = _rope_t(tq[rows, :] * QK_SCALE, cs, sa, sb).astype(BF)
                dk_ref[rows, :] = _rope_t(tk[rows, :], cs, sa, sb).astype(BF)
                dv_ref[rows, :] = tv[rows, :].astype(BF)

        for gi in range(3):
            pl.when(g == gi)(functools.partial(group, gi))

    c0 = ATT_COL0 // BLK
    zspec = lambda part: pl.BlockSpec((T, BLK), lambda p, g, part=part: (0, c0 + 12 * part + 4 * g + p))
    pspec = pl.BlockSpec((T, BLK), lambda p, g: (0, p))
    gspec = pl.BlockSpec((T, BLK), lambda p, g: (0, 4 * g + p))
    table = pl.BlockSpec((T, BLK), lambda p, g: (0, 0))
    big = lambda: pltpu.VMEM((T, BLK), F32)
    two = lambda: pltpu.VMEM((2, T, BLK), F32)
    return pl.pallas_call(
        body, name="attn_bwd", grid=(4, 3),
        in_specs=[zspec(0), zspec(1), zspec(2),
                  pl.BlockSpec((T, BLK), lambda p, g: (0, AG_COL0 // BLK + p)), table, table, table,
                  pspec, pl.BlockSpec((T, BLK), lambda p, g: (0, 2 * p)),
                  pl.BlockSpec((T, BLK), lambda p, g: (0, 2 * p + 1)), pspec],
        out_specs=[gspec, gspec, gspec, pspec],
        out_shape=[jax.ShapeDtypeStruct((T, 1536), BF), jax.ShapeDtypeStruct((T, 1536), BF),
                   jax.ShapeDtypeStruct((T, 1536), BF), jax.ShapeDtypeStruct((T, 512), BF)],
        scratch_shapes=[pltpu.VMEM((2, 2 * BLK, BLK), F32), two(), big(), big(), big(), big(),
                        two(), two(), big(), big(), big(), pltpu.VMEM((2, 4, 2 * BLK, BLK), BF), big()],
        compiler_params=_params(("parallel", "arbitrary")),
    )(z, z, z, z, cos, sa, sb, opre, lse, lse, dob)


def _merge_fwd(ya, yb, z):
    tm = 256

    def body(ya_ref, yb_ref, ga_ref, gb_ref, m_ref):
        m_ref[...] = (_sigmoid(ga_ref[...]) * ya_ref[...] + _sigmoid(gb_ref[...]) * yb_ref[...]).astype(BF)

    row = pl.BlockSpec((tm, D), lambda i: (i, 0))
    return pl.pallas_call(
        body, name="merge_fwd", grid=(T // tm,),
        in_specs=[row, row, pl.BlockSpec((tm, D), lambda i: (i, GATE_COL0 // D)),
                  pl.BlockSpec((tm, D), lambda i: (i, GATE_COL0 // D + 1))],
        out_specs=row, out_shape=jax.ShapeDtypeStruct((T, D), BF),
        compiler_params=_params(("parallel",)),
    )(ya, yb, z, z)


def _out_loss(merged, w_out, x, tgt, wf):
    tm = 256

    def body(m_ref, w_ref, x_ref, t_ref, wf_ref, dout_ref, loss_ref, gwf_ref):
        @pl.when(pl.program_id(0) == 0)
        def _():
            loss_ref[...] = jnp.zeros_like(loss_ref)
            gwf_ref[...] = jnp.zeros_like(gwf_ref)

        out = x_ref[...] + _dot(m_ref[...], w_ref[...])
        r = lax.rsqrt(jnp.mean(out * out, axis=-1, keepdims=True) + EPS)
        yh = out * r
        wfv = wf_ref[...]
        err = yh * wfv - t_ref[...]
        loss_ref[...] += jnp.sum(err * err, axis=0, keepdims=True) * (0.5 / D)
        dy = err * (1.0 / D)
        gwf_ref[...] += jnp.sum(dy * yh, axis=0, keepdims=True)
        dyh = dy * wfv
        dout_ref[...] = r * (dyh - yh * jnp.mean(dyh * yh, axis=-1, keepdims=True))

    row = pl.BlockSpec((tm, D), lambda i: (i, 0))
    vec = pl.BlockSpec((1, D), lambda i: (0, 0))
    return pl.pallas_call(
        body, name="out_loss", grid=(T // tm,),
        in_specs=[row, pl.BlockSpec((D, D), lambda i: (0, 0)), row, row, vec],
        out_specs=[row, vec, vec],
        out_shape=[jax.ShapeDtypeStruct((T, D), F32), jax.ShapeDtypeStruct((1, D), F32),
                   jax.ShapeDtypeStruct((1, D), F32)],
        compiler_params=_params(("arbitrary",)),
    )(merged, w_out, x, tgt, wf)


def _merge_bwd(dm, ya, yb, z):
    tm = 256

    def body(dm_ref, ya_ref, yb_ref, ga_ref, gb_ref, dya_ref, dyb_ref, dg_ref):
        dmv = dm_ref[...]
        sa, sb = _sigmoid(ga_ref[...]), _sigmoid(gb_ref[...])
        dya_ref[...] = (sa * dmv).astype(BF)
        dyb_ref[...] = (sb * dmv).astype(BF)
        dg_ref[:, :D] = (dmv * ya_ref[...] * sa * (1.0 - sa)).astype(BF)
        dg_ref[:, D:] = (dmv * yb_ref[...] * sb * (1.0 - sb)).astype(BF)

    row = pl.BlockSpec((tm, D), lambda i: (i, 0))
    return pl.pallas_call(
        body, name="merge_bwd", grid=(T // tm,),
        in_specs=[row, row, row, pl.BlockSpec((tm, D), lambda i: (i, GATE_COL0 // D)),
                  pl.BlockSpec((tm, D), lambda i: (i, GATE_COL0 // D + 1))],
        out_specs=[row, row, pl.BlockSpec((tm, 2 * D), lambda i: (i, 0))],
        out_shape=[jax.ShapeDtypeStruct((T, D), BF), jax.ShapeDtypeStruct((T, D), BF),
                   jax.ShapeDtypeStruct((T, 2 * D), BF)],
        compiler_params=_params(("parallel",)),
    )(dm, ya, yb, z, z)


def _rope_inv_freq():
    inv = ROPE_THETA ** (-jnp.arange(0, 64, 2, dtype=F32) / 64)
    return jnp.tile(inv, 4).reshape(1, BLK)


def _local_step(x, pos, norm_w, lbl, hnw, wf, tgt, w_in, w_a, w_b, w_out, shard_shapes=()):
    invf = _rope_inv_freq()
    if shard_shapes:
        h, cos, sa, sb, (w_in,) = _norm_and_rope_tables(x, norm_w, pos, invf,
                                                      side=_gather_side([w_in], WEIGHT_AXES[:1]))
        z, (w_a, w_b, w_out) = _matmul(h, w_in, tm=T, tn=512, name="z_proj",
                                       side=_gather_side([w_a, w_b, w_out], WEIGHT_AXES[1:]))
    else:
        h, cos, sa, sb, _ = _norm_and_rope_tables(x, norm_w, pos, invf)
        z = _matmul(h, w_in, tm=T, tn=512, name="z_proj")
    oraw, og, shist = _hgrn_fwd(z, lbl, hnw)
    ob, opre, lse = _attn_fwd(z, cos, sa, sb)
    ya = _matmul(og, w_a, tm=1024, tn=512, name="ya_proj")
    yb = _matmul(ob, w_b, tm=1024, tn=512, name="yb_proj")
    merged = _merge_fwd(ya, yb, z)
    dout, loss_vec, g_wf = _out_loss(merged, w_out, x, tgt, wf)

    dm = _matmul(dout, w_out, tb=True, tm=1024, tn=512, name="d_merged")
    g_wout = _matmul(merged, dout, ta=True, out_dtype=BF, tm=512, tn=1024, name="g_wout")
    dya, dyb, dgates = _merge_bwd(dm, ya, yb, z)
    dog = _matmul(dya, w_a, tb=True, tm=1024, tn=512, name="d_og")
    g_wa = _matmul(og, dya, ta=True, out_dtype=BF, tm=512, tn=1024, name="g_wa")
    dob = _matmul(dyb, w_b, tb=True, tm=1024, tn=512, name="d_ob")
    g_wb = _matmul(ob, dyb, ta=True, out_dtype=BF, tm=512, tn=1024, name="g_wb")
    small = [g_wa, g_wb, g_wout]
    side_s = side_w = None
    if shard_shapes:
        p3_s = _rs_partials(small, shard_shapes[1:], WEIGHT_AXES[1:], "small")
        side_s = _chip_exchange_side(p3_s, shard_shapes[1:], WEIGHT_AXES[1:])
    dz_h, dlb, g_hnw, land_s = _hgrn_bwd(z, lbl, hnw, oraw, dog, shist, side=side_s)
    dq, dk, dv, dag = _attn_bwd(z, cos, sa, sb, opre, lse, dob)
    dz_parts = [dz_h, dq, dk, dv, dag, dgates]
    if shard_shapes:
        c = lax.axis_index("c")
        half = lambda i: jnp.reshape(i, (1,)).astype(jnp.int32)
        g_send = _grad_w_in_half(h, dz_parts, half(1 - c))
        g_keep, (g_sib,) = _grad_w_in_half(h, dz_parts, half(c), side=_sibling_send_side(g_send))
        p3_w = [_add_bf16(g_keep, g_sib, "pair_sum_w_in").reshape(1, D // 2, NIN)]
        side_w = _chip_exchange_side(p3_w, shard_shapes[:1], WEIGHT_AXES[:1])
    else:
        g_big = [_grad_w_in(h, dz_parts)] + small
    gx, g_nw, land_w = _grad_x(dz_parts, w_in, x, dout, norm_w, side=side_w)
    if shard_shapes:
        g_big = _rs_finish(p3_w + p3_s, list(land_w) + list(land_s), shard_shapes, WEIGHT_AXES)
    return dict(loss_vec=loss_vec, gx=gx, g_nw=g_nw, dlb=dlb, g_hnw=g_hnw, g_wf=g_wf,
                g_win=g_big[0], g_wa=g_big[1], g_wb=g_big[2], g_wout=g_big[3])


MESH = pl.DeviceIdType.MESH
HBM = pl.BlockSpec(memory_space=pl.ANY)
WEIGHT_AXES = (1, 0, 1, 0)


def _place():
    x, y, c = lax.axis_index("x"), lax.axis_index("y"), lax.axis_index("c")
    chips = [(1 - x, y), (x, 1 - y), (1 - x, 1 - y)]
    return x, y, c, chips


def _block_half(ref, shard_shape, axis, j, half):
    r, c = shard_shape
    hr = r // 2
    if axis == 0:
        return ref.at[pl.ds(pl.multiple_of(j * r + half * hr, 16), hr), :]
    return ref.at[pl.ds(pl.multiple_of(half * hr, 16), hr), pl.ds(pl.multiple_of(j * c, 128), c)]


class _Side:
    def __init__(self, arrays, out_shapes, sems, first, last):
        self.arrays, self.out_shapes, self.sems, self.first, self.last = arrays, out_shapes, sems, first, last


def _gather_side(shards, axes):
    n = len(shards)
    shapes = [s.shape for s in shards]

    def copies(ins, outs, sems):
        send1, recv1, send2, recv2, send0, recv0 = sems
        x, y, c, chips = _place()
        me = 2 * x + y
        sib = (x, y, 1 - c)
        near = ((1 - c) * (1 - x) + c * x, (1 - c) * y + c * (1 - y))
        far = ((1 - c) * x + c * (1 - x), (1 - c) * (1 - y) + c * y)
        out = []
        for a in range(n):
            r, cc = shapes[a]
            mine = (outs[a].at[pl.ds(pl.multiple_of(me * r, 16), r), :] if axes[a] == 0
                    else outs[a].at[:, pl.ds(pl.multiple_of(me * cc, 128), cc)])
            own = pltpu.make_async_remote_copy(
                src_ref=ins[a], dst_ref=mine, send_sem=send0.at[a], recv_sem=recv0.at[a],
                device_id=sib, device_id_type=MESH)
            src = ins[a].at[pl.ds(pl.multiple_of(c * (r // 2), 16), r // 2), :]
            sends = [pltpu.make_async_remote_copy(
                src_ref=src, dst_ref=_block_half(outs[a], shapes[a], axes[a], me, c),
                send_sem=send1.at[a, k], recv_sem=recv1.at[a, k], device_id=(*chips[k], c), device_id_type=MESH)
                for k in range(2)]

            def region(chip, half):
                return _block_half(outs[a], shapes[a], axes[a], 2 * chip[0] + chip[1], half)

            def arrival(chip, k):
                reg = region(chip, c)
                return pltpu.make_async_remote_copy(
                    src_ref=reg, dst_ref=reg, send_sem=send1.at[a, k], recv_sem=recv1.at[a, k],
                    device_id=(*chip, c), device_id_type=MESH)

            def to_sibling(chip, k):
                reg = region(chip, c)
                return pltpu.make_async_remote_copy(
                    src_ref=reg, dst_ref=reg, send_sem=send2.at[a, k], recv_sem=recv2.at[a, k],
                    device_id=sib, device_id_type=MESH)

            def from_sibling(chip, k):
                reg = region(chip, 1 - c)
                return pltpu.make_async_remote_copy(
                    src_ref=reg, dst_ref=reg, send_sem=send2.at[a, k], recv_sem=recv2.at[a, k],
                    device_id=sib, device_id_type=MESH)

            relay = pltpu.make_async_remote_copy(
                src_ref=region(near, c), dst_ref=region(near, c), send_sem=send1.at[a, 2], recv_sem=recv1.at[a, 2],
                device_id=(*far, c), device_id_type=MESH)
            hops = [(arrival(near, c), to_sibling(near, c)), (arrival(far, 1 - c), to_sibling(far, 1 - c)),
                    (arrival(chips[2], 2), to_sibling(chips[2], 2))]
            back = [from_sibling(chips[k], k) for k in range(3)]
            out.append((own, sends, relay, hops, back))
        return out

    def first(ins, outs, sems):
        for own, sends, _, _, _ in copies(ins, outs, sems):
            own.start()
            for cp in sends:
                cp.start()

    def last(ins, outs, sems):
        per_array = copies(ins, outs, sems)
        for step in range(3):
            for _, _, relay, hops, _ in per_array:
                arrived, onward = hops[step]
                arrived.wait_recv()
                if step == 0:
                    relay.start()
                onward.start()
        for own, sends, relay, hops, back in per_array:
            for cp in back:
                cp.wait_recv()
            for cp in sends + [relay] + [onward for _, onward in hops]:
                cp.wait_send()
            own.wait()

    full = [(4 * r, c) if ax == 0 else (r, 4 * c) for (r, c), ax in zip(shapes, axes)]
    sems = [pltpu.SemaphoreType.DMA((n, 3)), pltpu.SemaphoreType.DMA((n, 3)),
            pltpu.SemaphoreType.DMA((n, 3)), pltpu.SemaphoreType.DMA((n, 3)),
            pltpu.SemaphoreType.DMA((n,)), pltpu.SemaphoreType.DMA((n,))]
    return _Side(list(shards), [jax.ShapeDtypeStruct(f, BF) for f in full], sems, first, last)


def _as3d(g, shard_shape, axis):
    r, c = shard_shape
    return g.reshape(4, r, c) if axis == 0 else g.reshape(1, r, 4 * c)


def _half_rows(ref3, hr, half):
    return ref3.at[:, pl.ds(pl.multiple_of(half * hr, 16), hr), :]


def _rs_pair_exchange(g3s, name):
    n = len(g3s)

    def body(*refs):
        ins, outs = refs[:n], refs[n:2 * n]
        send, recv = refs[2 * n:]
        x, y, c, _ = _place()
        cps = []
        for a in range(n):
            hr = g3s[a].shape[1] // 2
            cp = pltpu.make_async_remote_copy(
                src_ref=_half_rows(ins[a], hr, 1 - c), dst_ref=outs[a],
                send_sem=send.at[a], recv_sem=recv.at[a], device_id=(x, y, 1 - c), device_id_type=MESH)
            cp.start()
            cps.append(cp)
        for cp in cps:
            cp.wait()

    return pl.pallas_call(
        body, name=name,
        in_specs=[HBM] * n, out_specs=[HBM] * n,
        out_shape=[jax.ShapeDtypeStruct((g.shape[0], g.shape[1] // 2, g.shape[2]), BF) for g in g3s],
        scratch_shapes=[pltpu.SemaphoreType.DMA((n,)), pltpu.SemaphoreType.DMA((n,))],
    )(*g3s)


def _pair_sum(g3, land, cidx, name):
    nb, r, w = g3.shape
    hr = r // 2
    tr = 64

    def body(c_ref, g_ref, l_ref, o_ref):
        o_ref[...] = (g_ref[...].astype(F32) + l_ref[...].astype(F32)).astype(BF)

    blk = (nb, tr, w)
    return pl.pallas_call(
        body, name=name,
        grid_spec=pltpu.PrefetchScalarGridSpec(
            num_scalar_prefetch=1, grid=(hr // tr,),
            in_specs=[pl.BlockSpec(blk, lambda i, c: (0, c[0] * (hr // tr) + i, 0)),
                      pl.BlockSpec(blk, lambda i, c: (0, i, 0))],
            out_specs=pl.BlockSpec(blk, lambda i, c: (0, i, 0))),
        out_shape=jax.ShapeDtypeStruct((nb, hr, w), BF),
        compiler_params=_params(("parallel",)),
    )(cidx, g3, land)


def _chip_exchange_side(p3s, shapes, axes):
    n = len(p3s)

    def copies(ins, outs, sems):
        send, recv = sems
        x, y, c, chips = _place()
        cps = []
        for a in range(n):
            r, cc = shapes[a]
            for k, (px, py) in enumerate(chips):
                j = 2 * px + py
                src = ins[a].at[j] if axes[a] == 0 else ins[a].at[0, :, pl.ds(pl.multiple_of(j * cc, 128), cc)]
                cps.append(pltpu.make_async_remote_copy(
                    src_ref=src, dst_ref=outs[a].at[k], send_sem=send.at[a, k], recv_sem=recv.at[a, k],
                    device_id=(px, py, c), device_id_type=MESH))
        return cps

    def first(ins, outs, sems):
        for cp in copies(ins, outs, sems):
            cp.start()

    def last(ins, outs, sems):
        for cp in copies(ins, outs, sems):
            cp.wait()

    return _Side(list(p3s), [jax.ShapeDtypeStruct((3, r // 2, c), BF) for r, c in shapes],
                 [pltpu.SemaphoreType.DMA((n, 3)), pltpu.SemaphoreType.DMA((n, 3))], first, last)


def _chip_sum(p3, land, shard_shape, axis, idx, name):
    r, c = shard_shape
    hr = r // 2
    tr = 64
    nt = hr // tr

    def body(idx_ref, p_ref, l_ref, o_ref):
        acc = p_ref[...].astype(F32)
        for k in range(3):
            acc = acc + l_ref[k].astype(F32)
        o_ref[...] = acc

    own = (pl.BlockSpec((None, tr, c), lambda i, idx: (idx[0], i, 0)) if axis == 0
           else pl.BlockSpec((None, tr, c), lambda i, idx: (0, i, idx[0])))
    return pl.pallas_call(
        body, name=name,
        grid_spec=pltpu.PrefetchScalarGridSpec(
            num_scalar_prefetch=1, grid=(nt,),
            in_specs=[own, pl.BlockSpec((3, tr, c), lambda i, idx: (0, i, 0))],
            out_specs=pl.BlockSpec((tr, c), lambda i, idx: (idx[1] * nt + i, 0))),
        out_shape=jax.ShapeDtypeStruct((r, c), F32),
        compiler_params=_params(("parallel",)),
    )(idx, p3, land)


def _rs_pair_gather(fulls):
    n = len(fulls)

    def body(*refs):
        ins, outs = refs[:n], refs[n:2 * n]
        send, recv = refs[2 * n:]
        x, y, c, _ = _place()
        cps = []
        for a in range(n):
            hr = fulls[a].shape[0] // 2
            rows = pl.ds(pl.multiple_of(c * hr, 8), hr)
            cp = pltpu.make_async_remote_copy(
                src_ref=ins[a].at[rows, :], dst_ref=outs[a].at[rows, :], send_sem=send.at[a], recv_sem=recv.at[a],
                device_id=(x, y, 1 - c), device_id_type=MESH)
            cp.start()
            cps.append(cp)
        for a, cp in enumerate(cps):
            cp.wait_send()
            hr = fulls[a].shape[0] // 2
            other = pl.ds(pl.multiple_of((1 - c) * hr, 8), hr)
            pltpu.make_async_remote_copy(
                src_ref=ins[a].at[other, :], dst_ref=outs[a].at[other, :], send_sem=send.at[a], recv_sem=recv.at[a],
                device_id=(x, y, 1 - c), device_id_type=MESH).wait_recv()

    return pl.pallas_call(
        body, name="grads_pair_gather",
        in_specs=[HBM] * n, out_specs=[HBM] * n,
        out_shape=[jax.ShapeDtypeStruct(f.shape, F32) for f in fulls],
        input_output_aliases={a: a for a in range(n)},
        scratch_shapes=[pltpu.SemaphoreType.DMA((n,)), pltpu.SemaphoreType.DMA((n,))],
    )(*fulls)


def _sibling_send_side(arr):
    def copy(ins, outs, sems):
        x, y, c, _ = _place()
        return pltpu.make_async_remote_copy(
            src_ref=ins[0], dst_ref=outs[0], send_sem=sems[0].at[0], recv_sem=sems[1].at[0],
            device_id=(x, y, 1 - c), device_id_type=MESH)

    return _Side([arr], [jax.ShapeDtypeStruct(arr.shape, arr.dtype)],
                 [pltpu.SemaphoreType.DMA((1,)), pltpu.SemaphoreType.DMA((1,))],
                 lambda ins, outs, sems: copy(ins, outs, sems).start(),
                 lambda ins, outs, sems: copy(ins, outs, sems).wait())


def _add_bf16(a, b, name):
    r, c = a.shape
    tr = 64

    def body(a_ref, b_ref, o_ref):
        o_ref[...] = (a_ref[...].astype(F32) + b_ref[...].astype(F32)).astype(BF)

    blk = pl.BlockSpec((tr, c), lambda i: (i, 0))
    return pl.pallas_call(
        body, name=name, grid=(r // tr,), in_specs=[blk, blk], out_specs=blk,
        out_shape=jax.ShapeDtypeStruct((r, c), BF), compiler_params=_params(("parallel",)),
    )(a, b)


def _rs_partials(grads, shapes, axes, tag):
    cidx = jnp.reshape(lax.axis_index("c"), (1,)).astype(jnp.int32)
    g3s = [_as3d(g, s, ax) for g, s, ax in zip(grads, shapes, axes)]
    lands = _rs_pair_exchange(g3s, f"grads_pair_exchange_{tag}")
    return [_pair_sum(g3, l, cidx, f"pair_sum_{tag}_{a}") for a, (g3, l) in enumerate(zip(g3s, lands))]


def _rs_finish(p3s, landed, shapes, axes):
    x, y, c = lax.axis_index("x"), lax.axis_index("y"), lax.axis_index("c")
    idx = jnp.stack([2 * x + y, c]).astype(jnp.int32)
    fulls = [_chip_sum(p3, l2, s, ax, idx, f"chip_sum_{a}")
             for a, (p3, l2, s, ax) in enumerate(zip(p3s, landed, shapes, axes))]
    return _rs_pair_gather(fulls)


NSMALL = 8


def _small_all_reduce(g_nw, dlb, g_hnw, g_wf, loss_vec):
    def body(nw_ref, lb_ref, hn_ref, wf_ref, ls_ref, out_ref, pack_ref, buf_ref, send, recv):
        x, y, c = lax.axis_index("x"), lax.axis_index("y"), lax.axis_index("c")
        me = 4 * x + 2 * y + c
        pack_ref[...] = jnp.zeros_like(pack_ref)
        pack_ref[0:1, :] = nw_ref[...]
        pack_ref[1:2, :] = lb_ref[...]
        pack_ref[2:3, 0:HK] = hn_ref[...]
        pack_ref[3:4, :] = wf_ref[...]
        pack_ref[4:5, :] = ls_ref[...]
        buf_ref[me] = pack_ref[...]
        cps = []
        for d in range(1, 8):
            dx, dy, dc = d >> 2, (d >> 1) & 1, d & 1
            peer = (1 - x if dx else x, 1 - y if dy else y, 1 - c if dc else c)
            cp = pltpu.make_async_remote_copy(
                src_ref=pack_ref, dst_ref=buf_ref.at[me], send_sem=send.at[d - 1], recv_sem=recv.at[d - 1],
                device_id=peer, device_id_type=MESH)
            cp.start()
            cps.append(cp)
        for d in range(1, 8):
            dx, dy, dc = d >> 2, (d >> 1) & 1, d & 1
            src = 4 * (1 - x if dx else x) + 2 * (1 - y if dy else y) + (1 - c if dc else c)
            pltpu.make_async_remote_copy(
                src_ref=pack_ref, dst_ref=buf_ref.at[src], send_sem=send.at[d - 1], recv_sem=recv.at[d - 1],
                device_id=(x, y, c), device_id_type=MESH).wait_recv()
        for cp in cps:
            cp.wait_send()
        acc = buf_ref[0]
        for i in range(1, 8):
            acc = acc + buf_ref[i]
        out_ref[...] = acc

    vm = pl.BlockSpec(memory_space=pltpu.VMEM)
    return pl.pallas_call(
        body, name="small_all_reduce",
        in_specs=[vm] * 5, out_specs=vm,
        out_shape=jax.ShapeDtypeStruct((NSMALL, D), F32),
        scratch_shapes=[pltpu.VMEM((NSMALL, D), F32), pltpu.VMEM((8, NSMALL, D), F32),
                        pltpu.SemaphoreType.DMA((7,)), pltpu.SemaphoreType.DMA((7,))],
    )(g_nw, dlb, g_hnw, g_wf, loss_vec)


def _adamw_math(w, g, m, v):
    m = B1 * m + (1.0 - B1) * g
    v = B2 * v + (1.0 - B2) * (g * g)
    m_hat = m / (1.0 - B1 ** STEP)
    v_hat = v / (1.0 - B2 ** STEP)
    return -LR * (m_hat / (jnp.sqrt(v_hat) + ADAM_EPS) + WD * w), m, v


def _adamw(w, g, m, v, name):
    r, c = w.shape
    tr = 64

    def body(w_ref, g_ref, m_ref, v_ref, d_ref, nm_ref, nv_ref, go_ref):
        g = g_ref[...]
        d_ref[...], nm_ref[...], nv_ref[...] = _adamw_math(w_ref[...], g, m_ref[...], v_ref[...])
        go_ref[...] = g

    blk = pl.BlockSpec((tr, c), lambda i: (i, 0))
    return pl.pallas_call(
        body, name=name, grid=(r // tr,), in_specs=[blk] * 4, out_specs=[blk] * 4,
        out_shape=[jax.ShapeDtypeStruct((r, c), F32)] * 4,
        compiler_params=_params(("parallel",)),
    )(w, g, m, v)


def _small_update(red, lbl, params):
    def body(red_ref, *refs):
        ins, outs = refs[:12], refs[12:]
        lb = _lower_bound(ins[3][...])
        dl0 = red_ref[1:2, :] * lb * (1.0 - lb)
        row = lax.broadcasted_iota(jnp.int32, (2, D), 0)
        grads = [red_ref[0:1, :], jnp.where(row == 0, dl0, -dl0), red_ref[2:3, 0:HK], red_ref[3:4, :]]
        for i, g in enumerate(grads):
            w, m, v = ins[3 * i][...], ins[3 * i + 1][...], ins[3 * i + 2][...]
            d, nm, nv = _adamw_math(w, g, m, v)
            outs[4 * i][...] = g
            outs[4 * i + 1][...] = d
            outs[4 * i + 2][...] = nm
            outs[4 * i + 3][...] = nv
        outs[16][...] = jnp.sum(red_ref[4:5, :], axis=1, keepdims=True)

    flat = [a for p in params for a in p]
    vm = pl.BlockSpec(memory_space=pltpu.VMEM)
    shapes = [jax.ShapeDtypeStruct(p[0].shape, F32) for p in params for _ in range(4)]
    return pl.pallas_call(
        body, name="small_update",
        in_specs=[vm] * 13, out_specs=[vm] * 17,
        out_shape=shapes + [jax.ShapeDtypeStruct((1, 1), F32)],
    )(red, *flat)


def kernel(x, positions, norm_w, w_in, lb_logits, hgrn_norm_w, w_branch_a, w_branch_b, w_out, final_norm_w, loss_target, m_norm_w, m_w_in, m_lb_logits, m_hgrn_norm_w, m_w_branch_a, m_w_branch_b, m_w_out, m_final_norm_w, v_norm_w, v_w_in, v_lb_logits, v_hgrn_norm_w, v_w_branch_a, v_w_branch_b, v_w_out, v_final_norm_w):
    big_w = [w_in[0], w_branch_a[0], w_branch_b[0], w_out[0]]
    big_m = [m_w_in[0], m_w_branch_a[0], m_w_branch_b[0], m_w_out[0]]
    big_v = [v_w_in[0], v_w_branch_a[0], v_w_branch_b[0], v_w_out[0]]
    shapes = [w.shape for w in big_w]
    wf = final_norm_w.reshape(1, D)

    shards = [w.astype(BF) for w in big_w]
    loc = _local_step(x[0], positions.reshape(T, 1), norm_w, lb_logits, hgrn_norm_w, wf, loss_target[0],
                      *shards, shard_shapes=shapes)
    g_big = [loc["g_win"], loc["g_wa"], loc["g_wb"], loc["g_wout"]]
    red = _small_all_reduce(loc["g_nw"], loc["dlb"], loc["g_hnw"], loc["g_wf"], loc["loss_vec"])

    small = _small_update(red, lb_logits, [
        (norm_w, m_norm_w, v_norm_w), (lb_logits, m_lb_logits, v_lb_logits),
        (hgrn_norm_w, m_hgrn_norm_w, v_hgrn_norm_w),
        (wf, m_final_norm_w.reshape(1, D), v_final_norm_w.reshape(1, D))])
    loss = small[16].reshape(())
    sg, sd, sm, sv = ([small[4 * i + j] for i in range(4)] for j in range(4))
    for lst in (sg, sd, sm, sv):
        lst[3] = lst[3].reshape(D)
    upd = [_adamw(w, g, m, v, f"adamw_{a}") for a, (w, g, m, v) in enumerate(zip(big_w, g_big, big_m, big_v))]
    bd, bm, bv, bg = ([u[j][None] for u in upd] for j in range(4))

    def order(s, b):
        return [s[0], b[0], s[1], s[2], b[1], b[2], b[3], s[3]]

    return (loss, loc["gx"][None], *order(sg, bg), *order(sd, bd), *order(sm, bm), *order(sv, bv))
```

```python
import functools

import jax
import jax.numpy as jnp
from jax import lax
from jax.experimental import pallas as pl
from jax.experimental.pallas import tpu as pltpu

T = 2048
D = 1024
NIN = 11264
HEADS = 8
HK = 128
CH = 16
NCH = T // CH
HSTEP = 2
ATT_GROUPS = ((128, 1), (512, 4), (2048, 16))
ATT_COL0 = 4096
AG_COL0 = 8704
GATE_COL0 = 9216
EPS = 1e-6
ROPE_THETA = 10000.0
LR, B1, B2, ADAM_EPS, WD, STEP = 0.001, 0.9, 0.999, 1e-08, 0.01, 10

F32 = jnp.float32
BF = jnp.bfloat16
VMEM_LIMIT = 56 * 1024 * 1024

_NN = (((1,), (0,)), ((), ()))
_NT = (((1,), (1,)), ((), ()))
_TN = (((0,), (0,)), ((), ()))


def _dot(a, b, dims=_NN):
    return lax.dot_general(a, b, dims, preferred_element_type=F32)


def _bdot(a, b, dims=_NN):
    return lax.dot_general(a.astype(BF), b.astype(BF), dims, preferred_element_type=F32)


def _sigmoid(x):
    return jax.nn.sigmoid(x)


def _params(sem=None):
    return pltpu.CompilerParams(dimension_semantics=sem, vmem_limit_bytes=VMEM_LIMIT)


def _matmul(a, b, *, ta=False, tb=False, out_dtype=F32, tm=512, tn=512, tk=None, name, side=None):
    m = a.shape[1] if ta else a.shape[0]
    kdim = a.shape[0] if ta else a.shape[1]
    n = b.shape[0] if tb else b.shape[1]
    tk = tk or kdim
    tm, tn = min(tm, m), min(tn, n)
    nm, nn, nk = m // tm, n // tn, kdim // tk
    dims = (((0 if ta else 1,), (1 if tb else 0,)), ((), ()))
    s_arrays, s_in_specs, s_shapes, s_out_specs, s_sems = _side_io(side)
    na, no = len(s_arrays), len(s_shapes)
    nacc = 1 if nk > 1 else 0

    def body(*refs):
        a_ref, b_ref = refs[:2]
        s_ins, o_ref, s_outs = refs[2:2 + na], refs[2 + na], refs[3 + na:3 + na + no]
        scratch = refs[3 + na + no:]
        s_sem_refs = scratch[nacc:]
        i, j, k = pl.program_id(0), pl.program_id(1), pl.program_id(2)
        if side is not None:
            @pl.when((i == 0) & (j == 0) & (k == 0))
            def _():
                side.first(s_ins, s_outs, s_sem_refs)

        prod = _bdot(a_ref[...], b_ref[...], dims)
        if nk == 1:
            o_ref[...] = prod.astype(out_dtype)
        else:
            acc = scratch[0]

            @pl.when(k == 0)
            def _():
                acc[...] = prod

            @pl.when(k > 0)
            def _():
                acc[...] += prod

            @pl.when(k == nk - 1)
            def _():
                o_ref[...] = acc[...].astype(out_dtype)

        if side is not None:
            @pl.when((i == nm - 1) & (j == nn - 1) & (k == nk - 1))
            def _():
                side.last(s_ins, s_outs, s_sem_refs)

    a_spec = pl.BlockSpec((tk, tm), lambda i, j, k: (k, i)) if ta else pl.BlockSpec((tm, tk), lambda i, j, k: (i, k))
    b_spec = pl.BlockSpec((tn, tk), lambda i, j, k: (j, k)) if tb else pl.BlockSpec((tk, tn), lambda i, j, k: (k, j))
    sem = ("parallel", "parallel", "arbitrary") if side is None else ("arbitrary",) * 3
    out = pl.pallas_call(
        body, name=name, grid=(nm, nn, nk),
        in_specs=[a_spec, b_spec] + s_in_specs,
        out_specs=[pl.BlockSpec((tm, tn), lambda i, j, k: (i, j))] + s_out_specs,
        out_shape=[jax.ShapeDtypeStruct((m, n), out_dtype)] + s_shapes,
        scratch_shapes=([pltpu.VMEM((tm, tn), F32)] if nk > 1 else []) + s_sems,
        compiler_params=_params(sem),
    )(a, b, *s_arrays)
    return out[0] if side is None else (out[0], out[1:])


DZ_TILE = 512


def _part_offsets(parts):
    counts = [p.shape[1] // DZ_TILE for p in parts]
    offs = [sum(counts[:i]) for i in range(len(parts))]
    return counts, offs


def _part_spec(rows, cnt, off, tile_axis):
    def index(*g):
        return (0 if rows is None else g[0], jnp.clip(g[tile_axis] - off, 0, cnt - 1))
    return index


def _grad_w_in(h, parts):
    counts, offs = _part_offsets(parts)
    n = len(parts)

    def body(h_ref, *refs):
        o_ref = refs[n]
        j = pl.program_id(0)
        for p_ref, cnt, off in zip(refs[:n], counts, offs):
            @pl.when((j >= off) & (j < off + cnt))
            def _(p_ref=p_ref):
                o_ref[...] = _bdot(h_ref[...], p_ref[...], _TN).astype(BF)

    return pl.pallas_call(
        body, name="g_win", grid=(sum(counts),),
        in_specs=[pl.BlockSpec((T, D), lambda j: (0, 0))] +
                 [pl.BlockSpec((T, DZ_TILE), _part_spec(None, c, o, 0)) for c, o in zip(counts, offs)],
        out_specs=pl.BlockSpec((D, DZ_TILE), lambda j: (0, j)),
        out_shape=jax.ShapeDtypeStruct((D, NIN), BF),
        compiler_params=_params(("parallel",)),
    )(h, *parts)


def _grad_w_in_half(h, parts, half_idx, side=None):
    counts, offs = _part_offsets(parts)
    n = len(parts)
    nj = sum(counts)
    s_arrays, s_in_specs, s_shapes, s_out_specs, s_sems = _side_io(side)
    na, no = len(s_arrays), len(s_shapes)

    def body(idx_ref, h_ref, *refs):
        s_ins, o_ref, s_outs, s_sem_refs = refs[n:n + na], refs[n + na], refs[n + na + 1:n + na + 1 + no], refs[n + na + 1 + no:]
        j = pl.program_id(0)
        if side is not None:
            @pl.when(j == 0)
            def _():
                side.first(s_ins, s_outs, s_sem_refs)

        for p_ref, cnt, off in zip(refs[:n], counts, offs):
            @pl.when((j >= off) & (j < off + cnt))
            def _(p_ref=p_ref):
                o_ref[...] = _bdot(h_ref[...], p_ref[...], _TN).astype(BF)

        if side is not None:
            @pl.when(j == nj - 1)
            def _():
                side.last(s_ins, s_outs, s_sem_refs)

    def part_spec(cnt, off):
        return pl.BlockSpec((T, DZ_TILE), lambda j, idx: (0, jnp.clip(j - off, 0, cnt - 1)))

    out = pl.pallas_call(
        body, name="g_win_half" if side is None else "g_win_half_carrying",
        grid_spec=pltpu.PrefetchScalarGridSpec(
            num_scalar_prefetch=1, grid=(nj,),
            in_specs=[pl.BlockSpec((T, D // 2), lambda j, idx: (0, idx[0]))] +
                     [part_spec(c, o) for c, o in zip(counts, offs)] + s_in_specs,
            out_specs=[pl.BlockSpec((D // 2, DZ_TILE), lambda j, idx: (0, j))] + s_out_specs,
            scratch_shapes=s_sems),
        out_shape=[jax.ShapeDtypeStruct((D // 2, NIN), BF)] + s_shapes,
        compiler_params=_params(("parallel",) if side is None else ("arbitrary",)),
    )(half_idx, h, *parts, *s_arrays)
    return out[0] if side is None else (out[0], out[1:])


def _side_io(side):
    if side is None:
        return [], [], [], [], []
    return (side.arrays, [HBM] * len(side.arrays), side.out_shapes, [HBM] * len(side.out_shapes), side.sems)


def _grad_x(parts, w_in, x, dout, norm_w, side=None):
    counts, offs = _part_offsets(parts)
    n = len(parts)
    tm = 1024
    nm, nk = T // tm, sum(counts)
    s_arrays, s_in_specs, s_shapes, s_out_specs, s_sems = _side_io(side)
    na, no = len(s_arrays), len(s_shapes)

    def body(*refs):
        w_ref, x_ref, dout_ref, nw_ref = refs[n:n + 4]
        s_ins = refs[n + 4:n + 4 + na]
        gx_ref, gw_ref = refs[n + 4 + na:n + 6 + na]
        s_outs = refs[n + 6 + na:n + 6 + na + no]
        acc = refs[n + 6 + na + no]
        s_sem_refs = refs[n + 7 + na + no:]
        i, k = pl.program_id(0), pl.program_id(1)

        @pl.when((i == 0) & (k == 0))
        def _():
            gw_ref[...] = jnp.zeros_like(gw_ref)
            if side is not None:
                side.first(s_ins, s_outs, s_sem_refs)

        @pl.when(k == 0)
        def _():
            acc[...] = jnp.zeros_like(acc)

        for p_ref, cnt, off in zip(refs[:n], counts, offs):
            @pl.when((k >= off) & (k < off + cnt))
            def _(p_ref=p_ref):
                acc[...] += _bdot(p_ref[...], w_ref[...], _NT)

        @pl.when(k == nk - 1)
        def _():
            gw = jnp.zeros((1, D), F32)
            for c in range(tm // BLK):
                rows = pl.ds(BLK * c, BLK)
                xv, dhv = x_ref[rows, :], acc[rows, :]
                r = lax.rsqrt(jnp.mean(xv * xv, axis=-1, keepdims=True) + EPS)
                nrm = xv * r
                dn = dhv * nw_ref[...]
                gw = gw + jnp.sum(dhv * nrm, axis=0, keepdims=True)
                gx_ref[rows, :] = dout_ref[rows, :] + r * (dn - nrm * jnp.mean(dn * nrm, axis=-1, keepdims=True))
            gw_ref[...] += gw

        if side is not None:
            @pl.when((i == nm - 1) & (k == nk - 1))
            def _():
                side.last(s_ins, s_outs, s_sem_refs)

    row = pl.BlockSpec((tm, D), lambda i, k: (i, 0))
    vec = pl.BlockSpec((1, D), lambda i, k: (0, 0))
    out = pl.pallas_call(
        body, name="grad_x", grid=(nm, nk),
        in_specs=[pl.BlockSpec((tm, DZ_TILE), _part_spec(0, c, o, 1)) for c, o in zip(counts, offs)] +
                 [pl.BlockSpec((D, DZ_TILE), lambda i, k: (0, k)), row, row, vec] + s_in_specs,
        out_specs=[row, vec] + s_out_specs,
        out_shape=[jax.ShapeDtypeStruct((T, D), F32), jax.ShapeDtypeStruct((1, D), F32)] + s_shapes,
        scratch_shapes=[pltpu.VMEM((tm, D), F32)] + s_sems,
        compiler_params=_params(("arbitrary", "arbitrary")),
    )(*parts, w_in, x, dout, norm_w, *s_arrays)
    return out[0], out[1], out[2:]


def _norm_and_rope_tables(x, w, pos, invf, side=None, own=None):
    tm = 256
    nm = T // tm
    s_arrays, s_in_specs, s_shapes, s_out_specs, s_sems = _side_io(side)
    na, no = len(s_arrays), len(s_shapes)
    nz = 0 if own is None else 1
    wsh, blk = own if own is not None else (None, jnp.zeros((1,), jnp.int32))

    def body(blk_ref, *refs):
        x_ref, w_ref, pos_ref, invf_ref = refs[:4]
        s_ins = refs[4 + nz:4 + nz + na]
        h_ref, cos_ref, sa_ref, sb_ref = refs[4 + nz + na:8 + nz + na]
        s_outs = refs[8 + 2 * nz + na:8 + 2 * nz + na + no]
        s_sem_refs = refs[8 + 2 * nz + na + no:]
        if side is not None:
            @pl.when(pl.program_id(0) == 0)
            def _():
                side.first(s_ins, s_outs, s_sem_refs)

        xv = x_ref[...]
        r = lax.rsqrt(jnp.mean(xv * xv, axis=-1, keepdims=True) + EPS)
        h = (xv * r * w_ref[...]).astype(BF)
        h_ref[...] = h
        if own is not None:
            refs[8 + nz + na][...] = _dot(h, refs[4][...])
        first = (lax.broadcasted_iota(jnp.int32, (tm, 128), 1) % 64) < 32
        ang = pos_ref[...].astype(F32) * invf_ref[...]
        s = jnp.sin(ang)
        cos_ref[...] = jnp.cos(ang)
        sa_ref[...] = jnp.where(first, -s, 0.0)
        sb_ref[...] = jnp.where(first, 0.0, s)
        if side is not None:
            @pl.when(pl.program_id(0) == nm - 1)
            def _():
                side.last(s_ins, s_outs, s_sem_refs)

    tab = pl.BlockSpec((tm, 128), lambda i, b: (i, 0))
    own_in = [] if own is None else [pl.BlockSpec(wsh.shape, lambda i, b: (0, 0))]
    own_out = [] if own is None else [pl.BlockSpec((tm, wsh.shape[1]), lambda i, b: (i, b[0]))]
    own_shape = [] if own is None else [jax.ShapeDtypeStruct((T, NIN), F32)]
    out = pl.pallas_call(
        body, name="norm_and_rope_tables",
        grid_spec=pltpu.PrefetchScalarGridSpec(
            num_scalar_prefetch=1, grid=(nm,),
            in_specs=[pl.BlockSpec((tm, D), lambda i, b: (i, 0)), pl.BlockSpec((1, D), lambda i, b: (0, 0)),
                      pl.BlockSpec((tm, 1), lambda i, b: (i, 0)), pl.BlockSpec((1, 128), lambda i, b: (0, 0))]
                     + own_in + s_in_specs,
            out_specs=[pl.BlockSpec((tm, D), lambda i, b: (i, 0)), tab, tab, tab] + own_out + s_out_specs,
            scratch_shapes=s_sems),
        out_shape=[jax.ShapeDtypeStruct((T, D), BF)] + [jax.ShapeDtypeStruct((T, 128), F32)] * 3 + own_shape + s_shapes,
        compiler_params=_params(("parallel",) if side is None else ("arbitrary",)),
    )(blk, x, w, pos, invf, *([] if own is None else [wsh]), *s_arrays)
    return out[0], out[1], out[2], out[3], (out[4] if own is not None else None), out[4 + nz:]


def _z_rest(h, w_in, z_own, blk, side):
    tm, tn = 1024, NIN // 8
    s_arrays, s_in_specs, s_shapes, s_out_specs, s_sems = _side_io(side)
    na, no = len(s_arrays), len(s_shapes)
    nm, ns = T // tm, 6

    def col(i, s, b):
        return (0, ((b[0] + 1 + s // 2) % 4) * 2 + s % 2)

    def body(blk_ref, h_ref, w_ref, zin_ref, *refs):
        s_ins, o_ref, s_outs, s_sem_refs = refs[:na], refs[na], refs[na + 1:na + 1 + no], refs[na + 1 + no:]
        i, s = pl.program_id(0), pl.program_id(1)

        @pl.when((i == 0) & (s == 0))
        def _():
            side.first(s_ins, s_outs, s_sem_refs)

        o_ref[...] = _dot(h_ref[...], w_ref[...])

        @pl.when((i == nm - 1) & (s == ns - 1))
        def _():
            side.last(s_ins, s_outs, s_sem_refs)

    out = pl.pallas_call(
        body, name="z_proj",
        grid_spec=pltpu.PrefetchScalarGridSpec(
            num_scalar_prefetch=1, grid=(nm, ns),
            in_specs=[pl.BlockSpec((tm, D), lambda i, s, b: (i, 0)), pl.BlockSpec((D, tn), col), HBM] + s_in_specs,
            out_specs=[pl.BlockSpec((tm, tn), lambda i, s, b: (i, col(i, s, b)[1]))] + s_out_specs,
            scratch_shapes=s_sems),
        out_shape=[jax.ShapeDtypeStruct((T, NIN), F32)] + s_shapes,
        input_output_aliases={3: 0},
        compiler_params=_params(("arbitrary", "arbitrary")),
    )(blk, h, w_in, z_own, *s_arrays)
    return out[0], out[1:]


def _lower_bound(lbl):
    mx = jnp.max(lbl, axis=0, keepdims=True)
    e = jnp.exp(lbl - mx)
    return e[0:1] / jnp.sum(e, axis=0, keepdims=True)


def _cumsum_rows(g, rows):
    b = g
    sh = 1
    while sh < CH:
        b = b + jnp.where(rows >= sh, pltpu.roll(b, sh, axis=0), 0.0)
        sh *= 2
    return b


def _rev_cumsum_rows(g, rows):
    b = g
    sh = 1
    while sh < CH:
        b = b + jnp.where(rows < CH - sh, pltpu.roll(b, CH - sh, axis=0), 0.0)
        sh *= 2
    return b


SUB = CH // 2


def _direct_block(qb, kb, vb, bb, rows8):
    ob = jnp.zeros_like(qb)
    for s in range(SUB):
        e_s = jnp.exp(jnp.where(rows8 >= s, bb - bb[s:s + 1], -jnp.inf))
        ob = ob + jnp.sum(qb * e_s * kb[s:s + 1], axis=1, keepdims=True) * vb[s:s + 1]
    return ob


def _direct_block_bwd(qb, kb, vb, bb, dob, rows8, rowc8):
    dq = dk = dv = db = jnp.zeros_like(qb)
    for s in range(SUB):
        one = (rowc8 == s).astype(F32)
        ks, vs = kb[s:s + 1], vb[s:s + 1]
        e_s = jnp.exp(jnp.where(rows8 >= s, bb - bb[s:s + 1], -jnp.inf))
        qes = qb * e_s
        w = qes * ks
        a = jnp.sum(w, axis=1, keepdims=True)
        da = jnp.sum(dob * vs, axis=1, keepdims=True)
        dv = dv + one * jnp.sum(a * dob, axis=0, keepdims=True)
        dq = dq + da * e_s * ks
        dk = dk + one * jnp.sum(da * qes, axis=0, keepdims=True)
        u = da * w
        db = db + u - one * jnp.sum(u, axis=0, keepdims=True)
    return dq, dk, dv, db


def _cross_factors(q, k, b):
    ref = b[SUB - 1:SUB]
    e_hi, e_lo = jnp.exp(b[SUB:] - ref), jnp.exp(ref - b[:SUB])
    return q[SUB:] * e_hi, k[:SUB] * e_lo, e_hi, e_lo


def _intra_fwd(q, k, v, b, rows8):
    lo = _direct_block(q[:SUB], k[:SUB], v[:SUB], b[:SUB], rows8)
    hi = _direct_block(q[SUB:], k[SUB:], v[SUB:], b[SUB:], rows8)
    qe_hi, ke_lo, _, _ = _cross_factors(q, k, b)
    for s in range(SUB):
        hi = hi + jnp.sum(qe_hi * ke_lo[s:s + 1], axis=1, keepdims=True) * v[s:s + 1]
    return jnp.concatenate([lo, hi], axis=0)


def _intra_bwd(q, k, v, b, do, rows8, rowc8):
    dq_lo, dk_lo, dv_lo, db_lo = _direct_block_bwd(q[:SUB], k[:SUB], v[:SUB], b[:SUB], do[:SUB], rows8, rowc8)
    dq_hi, dk_hi, dv_hi, db_hi = _direct_block_bwd(q[SUB:], k[SUB:], v[SUB:], b[SUB:], do[SUB:], rows8, rowc8)
    qe_hi, ke_lo, e_hi, e_lo = _cross_factors(q, k, b)
    do_hi, v_lo = do[SUB:], v[:SUB]
    dqe = dke = jnp.zeros_like(qe_hi)
    for s in range(SUB):
        one = (rowc8 == s).astype(F32)
        a = jnp.sum(qe_hi * ke_lo[s:s + 1], axis=1, keepdims=True)
        da = jnp.sum(do_hi * v_lo[s:s + 1], axis=1, keepdims=True)
        dv_lo = dv_lo + one * jnp.sum(a * do_hi, axis=0, keepdims=True)
        dqe = dqe + da * ke_lo[s:s + 1]
        dke = dke + one * jnp.sum(da * qe_hi, axis=0, keepdims=True)
    u_hi, u_lo = dqe * qe_hi, dke * ke_lo
    d_ref = jnp.sum(u_lo, axis=0, keepdims=True) - jnp.sum(u_hi, axis=0, keepdims=True)
    db_lo = db_lo - u_lo + (rowc8 == SUB - 1).astype(F32) * d_ref
    cat = lambda lo, hi: jnp.concatenate([lo, hi], axis=0)
    return (cat(dq_lo, dq_hi + dqe * e_hi), cat(dk_lo + dke * e_lo, dk_hi), cat(dv_lo, dv_hi),
            cat(db_lo, db_hi + u_hi))


def _hgrn_fwd(z, lbl, nw):
    def body(hq_ref, hf_ref, hi_ref, hg_ref, lbl_ref, nw_ref, oraw_ref, og_ref, sh_ref, st_ref):
        @pl.when(pl.program_id(0) == 0)
        def _():
            st_ref[...] = jnp.zeros_like(st_ref)

        lb_all = _lower_bound(lbl_ref[...])
        rows = lax.broadcasted_iota(jnp.int32, (CH, HK), 0)
        rows8 = lax.broadcasted_iota(jnp.int32, (SUB, HK), 0)
        nwv = nw_ref[...]
        for cc, h in [(cc, h) for cc in range(HSTEP) for h in range(HEADS)]:
            rs = slice(CH * cc, CH * (cc + 1))
            sl = slice(HK * h, HK * (h + 1))
            lb = lb_all[:, sl]
            hq, hf, v, hg = hq_ref[rs, sl], hf_ref[rs, sl], hi_ref[rs, sl], hg_ref[rs, sl]
            q = hq * _sigmoid(hq)
            f = lb + (1.0 - lb) * _sigmoid(hf)
            k = 1.0 - f
            b = _cumsum_rows(jnp.log(f), rows)
            sh_ref[cc, h] = st_ref[h]
            o = _bdot(q * jnp.exp(b), st_ref[h], _NT) + _intra_fwd(q, k, v, b, rows8)
            bl = b[CH - 1:CH]
            st_ref[h] = st_ref[h] * jnp.exp(bl)
            st_ref[h] += _bdot(v, k * jnp.exp(bl - b), _TN)
            oraw_ref[rs, sl] = o
            nrm = o * lax.rsqrt(jnp.mean(o * o, axis=1, keepdims=True) + EPS)
            og_ref[rs, sl] = (nrm * nwv * (hg * _sigmoid(hg))).astype(BF)

    zblk = lambda c: pl.BlockSpec((CH * HSTEP, D), lambda i, c=c: (i, c))
    return pl.pallas_call(
        body, name="hgrn_fwd", grid=(NCH // HSTEP,),
        in_specs=[zblk(0), zblk(1), zblk(2), zblk(3),
                  pl.BlockSpec((2, D), lambda i: (0, 0)), pl.BlockSpec((1, HK), lambda i: (0, 0))],
        out_specs=[zblk(0), zblk(0),
                   pl.BlockSpec((HSTEP, HEADS, HK, HK), lambda i: (i, 0, 0, 0))],
        out_shape=[jax.ShapeDtypeStruct((T, D), F32), jax.ShapeDtypeStruct((T, D), BF),
                   jax.ShapeDtypeStruct((NCH, HEADS, HK, HK), F32)],
        scratch_shapes=[pltpu.VMEM((HEADS, HK, HK), F32)],
        compiler_params=_params(("arbitrary",)),
    )(z, z, z, z, lbl, nw)


def _hgrn_bwd(z, lbl, nw, oraw, dog, shist, side=None):
    hstep = 1
    s_arrays, s_in_specs, s_shapes, s_out_specs, s_sems = _side_io(side)
    na, no = len(s_arrays), len(s_shapes)

    def body(*refs):
        hq_ref, hf_ref, hi_ref, hg_ref, lbl_ref, nw_ref, oraw_ref, dog_ref, sh_ref = refs[:9]
        s_ins = refs[9:9 + na]
        dz_ref, dlb_ref, dnw_ref = refs[9 + na:12 + na]
        s_outs = refs[12 + na:12 + na + no]
        dst_ref = refs[12 + na + no]
        s_sem_refs = refs[13 + na + no:]

        @pl.when(pl.program_id(0) == 0)
        def _():
            dst_ref[...] = jnp.zeros_like(dst_ref)
            dlb_ref[...] = jnp.zeros_like(dlb_ref)
            dnw_ref[...] = jnp.zeros_like(dnw_ref)
            if side is not None:
                side.first(s_ins, s_outs, s_sem_refs)

        lb_all = _lower_bound(lbl_ref[...])
        rows = lax.broadcasted_iota(jnp.int32, (CH, HK), 0)
        rowc = lax.broadcasted_iota(jnp.int32, (CH, 1), 0)
        rows8 = lax.broadcasted_iota(jnp.int32, (SUB, HK), 0)
        rowc8 = lax.broadcasted_iota(jnp.int32, (SUB, 1), 0)
        nwv = nw_ref[...]
        dnw = jnp.zeros((1, HK), F32)
        for cc, h in [(cc, h) for cc in reversed(range(hstep)) for h in range(HEADS)]:
            rs = slice(CH * cc, CH * (cc + 1))
            sl = slice(HK * h, HK * (h + 1))
            lb = lb_all[:, sl]
            hq, hf, v, hg = hq_ref[rs, sl], hf_ref[rs, sl], hi_ref[rs, sl], hg_ref[rs, sl]
            o, dg_out = oraw_ref[rs, sl], dog_ref[rs, sl]
            sg = _sigmoid(hg)
            sil = hg * sg
            r = lax.rsqrt(jnp.mean(o * o, axis=1, keepdims=True) + EPS)
            nrm = o * r
            d_hg = dg_out * (nrm * nwv) * (sg * (1.0 + hg * (1.0 - sg)))
            dn = dg_out * nwv * sil
            dnw = dnw + jnp.sum(dg_out * nrm * sil, axis=0, keepdims=True)
            do = r * (dn - nrm * jnp.mean(dn * nrm, axis=1, keepdims=True))
            sq = _sigmoid(hq)
            q = hq * sq
            sig = _sigmoid(hf)
            f = lb + (1.0 - lb) * sig
            k = 1.0 - f
            b = _cumsum_rows(jnp.log(f), rows)
            eb = jnp.exp(b)
            qe = q * eb
            bl = b[CH - 1:CH]
            ebl = jnp.exp(bl)
            kdec = jnp.exp(bl - b)
            ke = k * kdec
            dqe = _bdot(do, sh_ref[cc, h])
            dq = dqe * eb
            db = dqe * qe
            dke = _bdot(v, dst_ref[h])
            dv = _bdot(ke, dst_ref[h], _NT)
            dk = dke * kdec
            rr = dke * ke
            db = db - rr
            db_last = (jnp.sum(rr, axis=0, keepdims=True)
                       + ebl * jnp.sum(dst_ref[h] * sh_ref[cc, h], axis=0, keepdims=True))
            dst_ref[h] = dst_ref[h] * ebl
            dst_ref[h] += _bdot(do, qe, _TN)
            dq_i, dk_i, dv_i, db_i = _intra_bwd(q, k, v, b, do, rows8, rowc8)
            dq, dk, dv = dq + dq_i, dk + dk_i, dv + dv_i
            db = db + db_i + (rowc == CH - 1).astype(F32) * db_last
            dgl = _rev_cumsum_rows(db, rows)
            df = dgl / f - dk
            dlb_ref[:, sl] += jnp.sum(df * (1.0 - sig), axis=0, keepdims=True)
            dz_ref[rs, sl] = (dq * (sq * (1.0 + hq * (1.0 - sq)))).astype(BF)
            dz_ref[rs, D + HK * h:D + HK * (h + 1)] = (df * (1.0 - lb) * sig * (1.0 - sig)).astype(BF)
            dz_ref[rs, 2 * D + HK * h:2 * D + HK * (h + 1)] = dv.astype(BF)
            dz_ref[rs, 3 * D + HK * h:3 * D + HK * (h + 1)] = d_hg.astype(BF)
        dnw_ref[...] += dnw
        if side is not None:
            @pl.when(pl.program_id(0) == NCH // hstep - 1)
            def _():
                side.last(s_ins, s_outs, s_sem_refs)

    rev = lambda i: NCH // hstep - 1 - i
    zblk = lambda c: pl.BlockSpec((CH * hstep, D), lambda i, c=c: (rev(i), c))
    out = pl.pallas_call(
        body, name="hgrn_bwd", grid=(NCH // hstep,),
        in_specs=[zblk(0), zblk(1), zblk(2), zblk(3),
                  pl.BlockSpec((2, D), lambda i: (0, 0)), pl.BlockSpec((1, HK), lambda i: (0, 0)),
                  zblk(0), zblk(0),
                  pl.BlockSpec((hstep, HEADS, HK, HK), lambda i: (rev(i), 0, 0, 0))] + s_in_specs,
        out_specs=[pl.BlockSpec((CH * hstep, 4 * D), lambda i: (rev(i), 0)),
                   pl.BlockSpec((1, D), lambda i: (0, 0)), pl.BlockSpec((1, HK), lambda i: (0, 0))] + s_out_specs,
        out_shape=[jax.ShapeDtypeStruct((T, 4 * D), BF), jax.ShapeDtypeStruct((1, D), F32),
                   jax.ShapeDtypeStruct((1, HK), F32)] + s_shapes,
        scratch_shapes=[pltpu.VMEM((HEADS, HK, HK), F32)] + s_sems,
        compiler_params=_params(("arbitrary",)),
    )(z, z, z, z, lbl, nw, oraw, dog, shist, *s_arrays)
    return out[0], out[1], out[2], out[3:]


BLK = 128
NBLK = T // BLK
QK_SCALE = 0.125


def _head_masks():
    lane = lax.broadcasted_iota(jnp.int32, (1, BLK), 1)
    return [(lane < 64).astype(F32), (lane >= 64).astype(F32)]


def _pieces(dil):
    m = T // dil
    out = []
    for r in range(dil):
        for j in range(m // BLK):
            start = r + dil * BLK * j
            rows = pl.ds(start, BLK, stride=dil) if dil > 1 else pl.ds(start, BLK)
            out.append((rows, r * m + BLK * j))
    return out


def _rope(x, c, sa, sb):
    return x * c + pltpu.roll(x, 96, axis=1) * sa + pltpu.roll(x, 32, axis=1) * sb


def _rope_t(d, c, sa, sb):
    return d * c + pltpu.roll(d * sa, 32, axis=1) + pltpu.roll(d * sb, 96, axis=1)


def _rope_and_regroup(dil, q_ref, k_ref, v_ref, tables, stage_q, stage_k, qr_ref, kr_ref, vr_ref):
    cos_ref, sa_ref, sb_ref = tables
    to_q, to_k = (qr_ref, kr_ref) if dil == 1 else (stage_q, stage_k)
    for c in range(T // BLK):
        rows = pl.ds(BLK * c, BLK)
        cs, sa, sb = cos_ref[rows, :], sa_ref[rows, :], sb_ref[rows, :]
        to_q[rows, :] = _rope(q_ref[rows, :], cs, sa, sb) * QK_SCALE
        to_k[rows, :] = _rope(k_ref[rows, :], cs, sa, sb)
    for rows, dst in _pieces(dil):
        drows = pl.ds(dst, BLK)
        if dil > 1:
            qr_ref[drows, :] = stage_q[rows, :]
            kr_ref[drows, :] = stage_k[rows, :]
        vr_ref[drows, :] = v_ref[rows, :]


def _window_bias(bias_ref):
    ii = lax.broadcasted_iota(jnp.int32, (2 * BLK, BLK), 0) % BLK
    jj = lax.broadcasted_iota(jnp.int32, (2 * BLK, BLK), 1)
    bias_ref[0] = jnp.where(jj <= ii, 0.0, -jnp.inf)
    bias_ref[1] = jnp.where(jj >= ii, 0.0, -jnp.inf)


def _blocks(bi):
    if isinstance(bi, int):
        return pl.ds(bi * BLK, BLK), pl.ds(max(bi - 1, 0) * BLK, BLK)
    return (pl.ds(pl.multiple_of(bi * BLK, BLK), BLK),
            pl.ds(pl.multiple_of(jnp.maximum(bi - 1, 0) * BLK, BLK), BLK))


def _stack_heads(x, masks):
    return jnp.concatenate([x * masks[0], x * masks[1]], axis=0).astype(BF)


def _attn_fwd(z, cos, sa, sb):
    def body(q_ref, k_ref, v_ref, ag_ref, cos_ref, sa_ref, sb_ref, ob_ref, opre_ref, lse_ref,
             bias_ref, qr_ref, kr_ref, vr_ref, og_ref, lg_ref, otok_ref, ltok_ref, sc_ref):
        g = pl.program_id(1)
        masks = _head_masks()

        @pl.when(g == 0)
        def _():
            _window_bias(bias_ref)

        def group(gi):
            dil = ATT_GROUPS[gi][1]
            nblk = (T // dil) // BLK
            _rope_and_regroup(dil, q_ref, k_ref, v_ref, (cos_ref, sa_ref, sb_ref), lg_ref.at[0], lg_ref.at[1],
                              qr_ref, kr_ref, vr_ref)

            def scores(bi, slot):
                cur, prev = _blocks(bi)
                q2 = _stack_heads(qr_ref[cur, :], masks)
                sc_ref[slot, 0] = _dot(q2, kr_ref[cur, :].astype(BF), _NT) + bias_ref[0]
                if nblk > 1:
                    sc_ref[slot, 1] = (_dot(q2, kr_ref[prev, :].astype(BF), _NT)
                                       + (bias_ref[1] + jnp.where((bi % nblk) != 0, 0.0, -jnp.inf)))

            def finish(bi, slot):
                cur, prev = _blocks(bi)
                s_c, vc = sc_ref[slot, 0], vr_ref[cur, :].astype(BF)
                if nblk > 1:
                    s_p, vp = sc_ref[slot, 1], vr_ref[prev, :].astype(BF)
                    mx = jnp.max(jnp.maximum(s_c, s_p), axis=1, keepdims=True)
                    p_c, p_p = jnp.exp(s_c - mx), jnp.exp(s_p - mx)
                    den = jnp.sum(p_c + p_p, axis=1, keepdims=True)
                    oh = _dot(p_c.astype(BF), vc) + _dot(p_p.astype(BF), vp)
                else:
                    mx = jnp.max(s_c, axis=1, keepdims=True)
                    p_c = jnp.exp(s_c - mx)
                    den = jnp.sum(p_c, axis=1, keepdims=True)
                    oh = _dot(p_c.astype(BF), vc)
                on = oh / den
                lsev = jnp.broadcast_to(mx + jnp.log(den), (2 * BLK, BLK))
                og_ref[cur, :] = on[:BLK] * masks[0] + on[BLK:] * masks[1]
                lg_ref[0, cur, :] = lsev[:BLK]
                lg_ref[1, cur, :] = lsev[BLK:]

            def pair(j, carry):
                finish(2 * j, 0)
                scores(2 * j + 1, 1)
                finish(2 * j + 1, 1)
                scores(jnp.minimum(2 * j + 2, NBLK - 1), 0)
                return carry

            scores(0, 0)
            lax.fori_loop(0, NBLK // 2, pair, 0)
            for rows, src in _pieces(dil):
                srows = pl.ds(src, BLK)
                otok_ref[gi, rows, :] = og_ref[srows, :]
                ltok_ref[gi, 0, rows, :] = lg_ref[0, srows, :]
                ltok_ref[gi, 1, rows, :] = lg_ref[1, srows, :]

        for gi in range(3):
            pl.when(g == gi)(functools.partial(group, gi))

        @pl.when(g == 2)
        def _():
            for c in range(T // BLK):
                rows = pl.ds(BLK * c, BLK)
                wts = []
                for hh in range(2):
                    l0, l1, l2 = ltok_ref[0, hh, rows, :], ltok_ref[1, hh, rows, :], ltok_ref[2, hh, rows, :]
                    mx = jnp.maximum(jnp.maximum(l0, l1), l2)
                    e0, e1, e2 = jnp.exp(l0 - mx), jnp.exp(l1 - mx), jnp.exp(l2 - mx)
                    tot = e0 + e1 + e2
                    lse_ref[rows, BLK * hh:BLK * (hh + 1)] = mx + jnp.log(tot)
                    inv = 1.0 / tot
                    wts.append([e0 * inv, e1 * inv, e2 * inv])
                o = sum((wts[0][gi] * masks[0] + wts[1][gi] * masks[1]) * otok_ref[gi, rows, :] for gi in range(3))
                ag = ag_ref[rows, :]
                opre_ref[rows, :] = o
                ob_ref[rows, :] = (o * (ag * _sigmoid(ag))).astype(BF)

    c0 = ATT_COL0 // BLK
    zspec = lambda part: pl.BlockSpec((T, BLK), lambda p, g, part=part: (0, c0 + 12 * part + 4 * g + p))
    outspec = pl.BlockSpec((T, BLK), lambda p, g: (0, p))
    table = pl.BlockSpec((T, BLK), lambda p, g: (0, 0))
    big = lambda: pltpu.VMEM((T, BLK), F32)
    return pl.pallas_call(
        body, name="attn_fwd", grid=(4, 3),
        in_specs=[zspec(0), zspec(1), zspec(2),
                  pl.BlockSpec((T, BLK), lambda p, g: (0, AG_COL0 // BLK + p)), table, table, table],
        out_specs=[outspec, outspec, pl.BlockSpec((T, 2 * BLK), lambda p, g: (0, p))],
        out_shape=[jax.ShapeDtypeStruct((T, 512), BF), jax.ShapeDtypeStruct((T, 512), F32),
                   jax.ShapeDtypeStruct((T, 8 * BLK), F32)],
        scratch_shapes=[pltpu.VMEM((2, 2 * BLK, BLK), F32), big(), big(), big(), big(),
                        pltpu.VMEM((2, T, BLK), F32), pltpu.VMEM((3, T, BLK), F32), pltpu.VMEM((3, 2, T, BLK), F32),
                        pltpu.VMEM((2, 2, 2 * BLK, BLK), F32)],
        compiler_params=_params(("parallel", "arbitrary")),
    )(z, z, z, z, cos, sa, sb)


def _attn_bwd(z, cos, sa, sb, opre, lse, dob):
    def body(q_ref, k_ref, v_ref, ag_ref, cos_ref, sa_ref, sb_ref, o_ref, lse0_ref, lse1_ref, dob_ref,
             dq_ref, dk_ref, dv_ref, dag_ref,
             bias_ref, dtok_ref, qr_ref, kr_ref, vr_ref, dor_ref, lr_ref, dr_ref,
             dqr_ref, dkr_ref, dvr_ref, pd_ref, dotok_ref):
        g = pl.program_id(1)
        masks = _head_masks()

        @pl.when(g == 0)
        def _():
            _window_bias(bias_ref)
            for c in range(T // BLK):
                rows = pl.ds(BLK * c, BLK)
                ag, dob_v, o = ag_ref[rows, :], dob_ref[rows, :], o_ref[rows, :]
                sg = _sigmoid(ag)
                dag_ref[rows, :] = (dob_v * o * (sg * (1.0 + ag * (1.0 - sg)))).astype(BF)
                do = dob_v * (ag * sg)
                dotok_ref[rows, :] = do
                prod = do * o
                for hh, mh in enumerate(masks):
                    dtok_ref[hh, rows, :] = jnp.broadcast_to(jnp.sum(prod * mh, axis=1, keepdims=True), (BLK, BLK))

        def group(gi):
            dil = ATT_GROUPS[gi][1]
            nblk = (T // dil) // BLK
            _rope_and_regroup(dil, q_ref, k_ref, v_ref, (cos_ref, sa_ref, sb_ref), dqr_ref, dvr_ref,
                              qr_ref, kr_ref, vr_ref)
            for rows, dst in _pieces(dil):
                drows = pl.ds(dst, BLK)
                dor_ref[drows, :] = dotok_ref[rows, :]
                for hh, lse_ref in enumerate((lse0_ref, lse1_ref)):
                    lr_ref[hh, drows, :] = lse_ref[rows, :]
                    dr_ref[hh, drows, :] = dtok_ref[hh, rows, :]
            dkr_ref[...] = jnp.zeros_like(dkr_ref)
            dvr_ref[...] = jnp.zeros_like(dvr_ref)

            def probs(bi, slot):
                cur, prev = _blocks(bi)
                q2, do2 = _stack_heads(qr_ref[cur, :], masks), _stack_heads(dor_ref[cur, :], masks)
                lh = jnp.concatenate([lr_ref[0, cur, :], lr_ref[1, cur, :]], axis=0)
                dh = jnp.concatenate([dr_ref[0, cur, :], dr_ref[1, cur, :]], axis=0)
                p_c = jnp.exp(_dot(q2, kr_ref[cur, :].astype(BF), _NT) + bias_ref[0] - lh)
                pd_ref[slot, 0] = p_c.astype(BF)
                pd_ref[slot, 1] = (p_c * (_dot(do2, vr_ref[cur, :].astype(BF), _NT) - dh)).astype(BF)
                if nblk > 1:
                    bias_p = bias_ref[1] + jnp.where((bi % nblk) != 0, 0.0, -jnp.inf)
                    p_p = jnp.exp(_dot(q2, kr_ref[prev, :].astype(BF), _NT) + bias_p - lh)
                    pd_ref[slot, 2] = p_p.astype(BF)
                    pd_ref[slot, 3] = (p_p * (_dot(do2, vr_ref[prev, :].astype(BF), _NT) - dh)).astype(BF)

            def grads(bi, slot):
                cur, prev = _blocks(bi)
                q2, do2 = _stack_heads(qr_ref[cur, :], masks), _stack_heads(dor_ref[cur, :], masks)
                p_c, ds_c = pd_ref[slot, 0], pd_ref[slot, 1]
                dq2 = _dot(ds_c, kr_ref[cur, :].astype(BF))
                dkr_ref[cur, :] += _dot(ds_c, q2, _TN)
                dvr_ref[cur, :] += _dot(p_c, do2, _TN)
                if nblk > 1:
                    p_p, ds_p = pd_ref[slot, 2], pd_ref[slot, 3]
                    dq2 = dq2 + _dot(ds_p, kr_ref[prev, :].astype(BF))
                    dkr_ref[prev, :] += _dot(ds_p, q2, _TN)
                    dvr_ref[prev, :] += _dot(p_p, do2, _TN)
                dqr_ref[cur, :] = dq2[:BLK] * masks[0] + dq2[BLK:] * masks[1]

            def pair(j, carry):
                grads(2 * j, 0)
                probs(2 * j + 1, 1)
                grads(2 * j + 1, 1)
                probs(jnp.minimum(2 * j + 2, NBLK - 1), 0)
                return carry

            probs(0, 0)
            lax.fori_loop(0, NBLK // 2, pair, 0)
            if dil > 1:
                for rows, src in _pieces(dil):
                    srows = pl.ds(src, BLK)
                    qr_ref[rows, :] = dqr_ref[srows, :]
                    kr_ref[rows, :] = dkr_ref[srows, :]
                    vr_ref[rows, :] = dvr_ref[srows, :]
            tq, tk, tv = (qr_ref, kr_ref, vr_ref) if dil > 1 else (dqr_ref, dkr_ref, dvr_ref)
            for c in range(T // BLK):
                rows = pl.ds(BLK * c, BLK)
                cs, sa, sb = cos_ref[rows, :], sa_ref[rows, :], sb_ref[rows, :]
                dq_ref[rows, :] = _rope_t(tq[rows, :] * QK_SCALE, cs, sa, sb).astype(BF)
                dk_ref[rows, :] = _rope_t(tk[rows, :], cs, sa, sb).astype(BF)
                dv_ref[rows, :] = tv[rows, :].astype(BF)

        for gi in range(3):
            pl.when(g == gi)(functools.partial(group, gi))

    c0 = ATT_COL0 // BLK
    zspec = lambda part: pl.BlockSpec((T, BLK), lambda p, g, part=part: (0, c0 + 12 * part + 4 * g + p))
    pspec = pl.BlockSpec((T, BLK), lambda p, g: (0, p))
    gspec = pl.BlockSpec((T, BLK), lambda p, g: (0, 4 * g + p))
    table = pl.BlockSpec((T, BLK), lambda p, g: (0, 0))
    big = lambda: pltpu.VMEM((T, BLK), F32)
    two = lambda: pltpu.VMEM((2, T, BLK), F32)
    return pl.pallas_call(
        body, name="attn_bwd", grid=(4, 3),
        in_specs=[zspec(0), zspec(1), zspec(2),
                  pl.BlockSpec((T, BLK), lambda p, g: (0, AG_COL0 // BLK + p)), table, table, table,
                  pspec, pl.BlockSpec((T, BLK), lambda p, g: (0, 2 * p)),
                  pl.BlockSpec((T, BLK), lambda p, g: (0, 2 * p + 1)), pspec],
        out_specs=[gspec, gspec, gspec, pspec],
        out_shape=[jax.ShapeDtypeStruct((T, 1536), BF), jax.ShapeDtypeStruct((T, 1536), BF),
                   jax.ShapeDtypeStruct((T, 1536), BF), jax.ShapeDtypeStruct((T, 512), BF)],
        scratch_shapes=[pltpu.VMEM((2, 2 * BLK, BLK), F32), two(), big(), big(), big(), big(),
                        two(), two(), big(), big(), big(), pltpu.VMEM((2, 4, 2 * BLK, BLK), BF), big()],
        compiler_params=_params(("parallel", "arbitrary")),
    )(z, z, z, z, cos, sa, sb, opre, lse, lse, dob)


def _merge_fwd(ya, yb, z):
    tm = 256

    def body(ya_ref, yb_ref, ga_ref, gb_ref, m_ref):
        m_ref[...] = (_sigmoid(ga_ref[...]) * ya_ref[...] + _sigmoid(gb_ref[...]) * yb_ref[...]).astype(BF)

    row = pl.BlockSpec((tm, D), lambda i: (i, 0))
    return pl.pallas_call(
        body, name="merge_fwd", grid=(T // tm,),
        in_specs=[row, row, pl.BlockSpec((tm, D), lambda i: (i, GATE_COL0 // D)),
                  pl.BlockSpec((tm, D), lambda i: (i, GATE_COL0 // D + 1))],
        out_specs=row, out_shape=jax.ShapeDtypeStruct((T, D), BF),
        compiler_params=_params(("parallel",)),
    )(ya, yb, z, z)


def _out_loss(merged, w_out, x, tgt, wf):
    tm = 256

    def body(m_ref, w_ref, x_ref, t_ref, wf_ref, dout_ref, loss_ref, gwf_ref):
        @pl.when(pl.program_id(0) == 0)
        def _():
            loss_ref[...] = jnp.zeros_like(loss_ref)
            gwf_ref[...] = jnp.zeros_like(gwf_ref)

        out = x_ref[...] + _dot(m_ref[...], w_ref[...])
        r = lax.rsqrt(jnp.mean(out * out, axis=-1, keepdims=True) + EPS)
        yh = out * r
        wfv = wf_ref[...]
        err = yh * wfv - t_ref[...]
        loss_ref[...] += jnp.sum(err * err, axis=0, keepdims=True) * (0.5 / D)
        dy = err * (1.0 / D)
        gwf_ref[...] += jnp.sum(dy * yh, axis=0, keepdims=True)
        dyh = dy * wfv
        dout_ref[...] = r * (dyh - yh * jnp.mean(dyh * yh, axis=-1, keepdims=True))

    row = pl.BlockSpec((tm, D), lambda i: (i, 0))
    vec = pl.BlockSpec((1, D), lambda i: (0, 0))
    return pl.pallas_call(
        body, name="out_loss", grid=(T // tm,),
        in_specs=[row, pl.BlockSpec((D, D), lambda i: (0, 0)), row, row, vec],
        out_specs=[row, vec, vec],
        out_shape=[jax.ShapeDtypeStruct((T, D), F32), jax.ShapeDtypeStruct((1, D), F32),
                   jax.ShapeDtypeStruct((1, D), F32)],
        compiler_params=_params(("arbitrary",)),
    )(merged, w_out, x, tgt, wf)


def _merge_bwd(dm, ya, yb, z):
    tm = 256

    def body(dm_ref, ya_ref, yb_ref, ga_ref, gb_ref, dya_ref, dyb_ref, dg_ref):
        dmv = dm_ref[...]
        sa, sb = _sigmoid(ga_ref[...]), _sigmoid(gb_ref[...])
        dya_ref[...] = (sa * dmv).astype(BF)
        dyb_ref[...] = (sb * dmv).astype(BF)
        dg_ref[:, :D] = (dmv * ya_ref[...] * sa * (1.0 - sa)).astype(BF)
        dg_ref[:, D:] = (dmv * yb_ref[...] * sb * (1.0 - sb)).astype(BF)

    row = pl.BlockSpec((tm, D), lambda i: (i, 0))
    return pl.pallas_call(
        body, name="merge_bwd", grid=(T // tm,),
        in_specs=[row, row, row, pl.BlockSpec((tm, D), lambda i: (i, GATE_COL0 // D)),
                  pl.BlockSpec((tm, D), lambda i: (i, GATE_COL0 // D + 1))],
        out_specs=[row, row, pl.BlockSpec((tm, 2 * D), lambda i: (i, 0))],
        out_shape=[jax.ShapeDtypeStruct((T, D), BF), jax.ShapeDtypeStruct((T, D), BF),
                   jax.ShapeDtypeStruct((T, 2 * D), BF)],
        compiler_params=_params(("parallel",)),
    )(dm, ya, yb, z, z)


def _rope_inv_freq():
    inv = ROPE_THETA ** (-jnp.arange(0, 64, 2, dtype=F32) / 64)
    return jnp.tile(inv, 4).reshape(1, BLK)


def _local_step(x, pos, norm_w, lbl, hnw, wf, tgt, w_in, w_a, w_b, w_out, shard_shapes=()):
    invf = _rope_inv_freq()
    if shard_shapes:
        blk = jnp.reshape(2 * lax.axis_index("x") + lax.axis_index("y"), (1,)).astype(jnp.int32)
        h, cos, sa, sb, z_own, (w_in,) = _norm_and_rope_tables(
            x, norm_w, pos, invf, side=_gather_side([w_in], WEIGHT_AXES[:1]), own=(w_in, blk))
        z, (w_a, w_b, w_out) = _z_rest(h, w_in, z_own, blk, side=_gather_side([w_a, w_b, w_out], WEIGHT_AXES[1:]))
    else:
        h, cos, sa, sb, _, _ = _norm_and_rope_tables(x, norm_w, pos, invf)
        z = _matmul(h, w_in, tm=T, tn=512, name="z_proj")
    oraw, og, shist = _hgrn_fwd(z, lbl, hnw)
    ob, opre, lse = _attn_fwd(z, cos, sa, sb)
    ya = _matmul(og, w_a, tm=1024, tn=512, name="ya_proj")
    yb = _matmul(ob, w_b, tm=1024, tn=512, name="yb_proj")
    merged = _merge_fwd(ya, yb, z)
    dout, loss_vec, g_wf = _out_loss(merged, w_out, x, tgt, wf)

    dm = _matmul(dout, w_out, tb=True, tm=1024, tn=512, name="d_merged")
    g_wout = _matmul(merged, dout, ta=True, out_dtype=BF, tm=512, tn=1024, name="g_wout")
    dya, dyb, dgates = _merge_bwd(dm, ya, yb, z)
    dog = _matmul(dya, w_a, tb=True, tm=1024, tn=512, name="d_og")
    g_wa = _matmul(og, dya, ta=True, out_dtype=BF, tm=512, tn=1024, name="g_wa")
    dob = _matmul(dyb, w_b, tb=True, tm=1024, tn=512, name="d_ob")
    g_wb = _matmul(ob, dyb, ta=True, out_dtype=BF, tm=512, tn=1024, name="g_wb")
    small = [g_wa, g_wb, g_wout]
    side_s = side_w = None
    if shard_shapes:
        p3_s = _rs_partials(small, shard_shapes[1:], WEIGHT_AXES[1:], "small")
        side_s = _chip_exchange_side(p3_s, shard_shapes[1:], WEIGHT_AXES[1:])
    dz_h, dlb, g_hnw, land_s = _hgrn_bwd(z, lbl, hnw, oraw, dog, shist, side=side_s)
    dq, dk, dv, dag = _attn_bwd(z, cos, sa, sb, opre, lse, dob)
    dz_parts = [dz_h, dq, dk, dv, dag, dgates]
    if shard_shapes:
        c = lax.axis_index("c")
        half = lambda i: jnp.reshape(i, (1,)).astype(jnp.int32)
        g_send = _grad_w_in_half(h, dz_parts, half(1 - c))
        g_keep, (g_sib,) = _grad_w_in_half(h, dz_parts, half(c), side=_sibling_send_side(g_send))
        p3_w = [_add_bf16(g_keep, g_sib, "pair_sum_w_in").reshape(1, D // 2, NIN)]
        side_w = _chip_exchange_side(p3_w, shard_shapes[:1], WEIGHT_AXES[:1])
    else:
        g_big = [_grad_w_in(h, dz_parts)] + small
    gx, g_nw, land_w = _grad_x(dz_parts, w_in, x, dout, norm_w, side=side_w)
    if shard_shapes:
        g_big = _rs_finish(p3_w + p3_s, list(land_w) + list(land_s), shard_shapes, WEIGHT_AXES)
    return dict(loss_vec=loss_vec, gx=gx, g_nw=g_nw, dlb=dlb, g_hnw=g_hnw, g_wf=g_wf,
                g_win=g_big[0], g_wa=g_big[1], g_wb=g_big[2], g_wout=g_big[3])


MESH = pl.DeviceIdType.MESH
HBM = pl.BlockSpec(memory_space=pl.ANY)
WEIGHT_AXES = (1, 0, 1, 0)


def _place():
    x, y, c = lax.axis_index("x"), lax.axis_index("y"), lax.axis_index("c")
    chips = [(1 - x, y), (x, 1 - y), (1 - x, 1 - y)]
    return x, y, c, chips


def _block_half(ref, shard_shape, axis, j, half):
    r, c = shard_shape
    hr = r // 2
    if axis == 0:
        return ref.at[pl.ds(pl.multiple_of(j * r + half * hr, 16), hr), :]
    return ref.at[pl.ds(pl.multiple_of(half * hr, 16), hr), pl.ds(pl.multiple_of(j * c, 128), c)]


class _Side:
    def __init__(self, arrays, out_shapes, sems, first, last):
        self.arrays, self.out_shapes, self.sems, self.first, self.last = arrays, out_shapes, sems, first, last


def _gather_side(shards, axes):
    n = len(shards)
    shapes = [s.shape for s in shards]

    def copies(ins, outs, sems):
        send1, recv1, send2, recv2, send0, recv0 = sems
        x, y, c, chips = _place()
        me = 2 * x + y
        sib = (x, y, 1 - c)
        near = ((1 - c) * (1 - x) + c * x, (1 - c) * y + c * (1 - y))
        far = ((1 - c) * x + c * (1 - x), (1 - c) * (1 - y) + c * y)
        out = []
        for a in range(n):
            r, cc = shapes[a]
            mine = (outs[a].at[pl.ds(pl.multiple_of(me * r, 16), r), :] if axes[a] == 0
                    else outs[a].at[:, pl.ds(pl.multiple_of(me * cc, 128), cc)])
            own = pltpu.make_async_remote_copy(
                src_ref=ins[a], dst_ref=mine, send_sem=send0.at[a], recv_sem=recv0.at[a],
                device_id=sib, device_id_type=MESH)
            src = ins[a].at[pl.ds(pl.multiple_of(c * (r // 2), 16), r // 2), :]
            sends = [pltpu.make_async_remote_copy(
                src_ref=src, dst_ref=_block_half(outs[a], shapes[a], axes[a], me, c),
                send_sem=send1.at[a, k], recv_sem=recv1.at[a, k], device_id=(*chips[k], c), device_id_type=MESH)
                for k in range(2)]

            def region(chip, half):
                return _block_half(outs[a], shapes[a], axes[a], 2 * chip[0] + chip[1], half)

            def arrival(chip, k):
                reg = region(chip, c)
                return pltpu.make_async_remote_copy(
                    src_ref=reg, dst_ref=reg, send_sem=send1.at[a, k], recv_sem=recv1.at[a, k],
                    device_id=(*chip, c), device_id_type=MESH)

            def to_sibling(chip, k):
                reg = region(chip, c)
                return pltpu.make_async_remote_copy(
                    src_ref=reg, dst_ref=reg, send_sem=send2.at[a, k], recv_sem=recv2.at[a, k],
                    device_id=sib, device_id_type=MESH)

            def from_sibling(chip, k):
                reg = region(chip, 1 - c)
                return pltpu.make_async_remote_copy(
                    src_ref=reg, dst_ref=reg, send_sem=send2.at[a, k], recv_sem=recv2.at[a, k],
                    device_id=sib, device_id_type=MESH)

            relay = pltpu.make_async_remote_copy(
                src_ref=region(near, c), dst_ref=region(near, c), send_sem=send1.at[a, 2], recv_sem=recv1.at[a, 2],
                device_id=(*far, c), device_id_type=MESH)
            hops = [(arrival(near, c), to_sibling(near, c)), (arrival(far, 1 - c), to_sibling(far, 1 - c)),
                    (arrival(chips[2], 2), to_sibling(chips[2], 2))]
            back = [from_sibling(chips[k], k) for k in range(3)]
            out.append((own, sends, relay, hops, back))
        return out

    def first(ins, outs, sems):
        for own, sends, _, _, _ in copies(ins, outs, sems):
            own.start()
            for cp in sends:
                cp.start()

    def last(ins, outs, sems):
        per_array = copies(ins, outs, sems)
        for step in range(3):
            for _, _, relay, hops, _ in per_array:
                arrived, onward = hops[step]
                arrived.wait_recv()
                if step == 0:
                    relay.start()
                onward.start()
        for own, sends, relay, hops, back in per_array:
            for cp in back:
                cp.wait_recv()
            for cp in sends + [relay] + [onward for _, onward in hops]:
                cp.wait_send()
            own.wait()

    full = [(4 * r, c) if ax == 0 else (r, 4 * c) for (r, c), ax in zip(shapes, axes)]
    sems = [pltpu.SemaphoreType.DMA((n, 3)), pltpu.SemaphoreType.DMA((n, 3)),
            pltpu.SemaphoreType.DMA((n, 3)), pltpu.SemaphoreType.DMA((n, 3)),
            pltpu.SemaphoreType.DMA((n,)), pltpu.SemaphoreType.DMA((n,))]
    return _Side(list(shards), [jax.ShapeDtypeStruct(f, BF) for f in full], sems, first, last)


def _as3d(g, shard_shape, axis):
    r, c = shard_shape
    return g.reshape(4, r, c) if axis == 0 else g.reshape(1, r, 4 * c)


def _half_rows(ref3, hr, half):
    return ref3.at[:, pl.ds(pl.multiple_of(half * hr, 16), hr), :]


def _rs_pair_exchange(g3s, name):
    n = len(g3s)

    def body(*refs):
        ins, outs = refs[:n], refs[n:2 * n]
        send, recv = refs[2 * n:]
        x, y, c, _ = _place()
        cps = []
        for a in range(n):
            hr = g3s[a].shape[1] // 2
            cp = pltpu.make_async_remote_copy(
                src_ref=_half_rows(ins[a], hr, 1 - c), dst_ref=outs[a],
                send_sem=send.at[a], recv_sem=recv.at[a], device_id=(x, y, 1 - c), device_id_type=MESH)
            cp.start()
            cps.append(cp)
        for cp in cps:
            cp.wait()

    return pl.pallas_call(
        body, name=name,
        in_specs=[HBM] * n, out_specs=[HBM] * n,
        out_shape=[jax.ShapeDtypeStruct((g.shape[0], g.shape[1] // 2, g.shape[2]), BF) for g in g3s],
        scratch_shapes=[pltpu.SemaphoreType.DMA((n,)), pltpu.SemaphoreType.DMA((n,))],
    )(*g3s)


def _pair_sum(g3, land, cidx, name):
    nb, r, w = g3.shape
    hr = r // 2
    tr = 64

    def body(c_ref, g_ref, l_ref, o_ref):
        o_ref[...] = (g_ref[...].astype(F32) + l_ref[...].astype(F32)).astype(BF)

    blk = (nb, tr, w)
    return pl.pallas_call(
        body, name=name,
        grid_spec=pltpu.PrefetchScalarGridSpec(
            num_scalar_prefetch=1, grid=(hr // tr,),
            in_specs=[pl.BlockSpec(blk, lambda i, c: (0, c[0] * (hr // tr) + i, 0)),
                      pl.BlockSpec(blk, lambda i, c: (0, i, 0))],
            out_specs=pl.BlockSpec(blk, lambda i, c: (0, i, 0))),
        out_shape=jax.ShapeDtypeStruct((nb, hr, w), BF),
        compiler_params=_params(("parallel",)),
    )(cidx, g3, land)


def _chip_exchange_side(p3s, shapes, axes):
    n = len(p3s)

    def copies(ins, outs, sems):
        send, recv = sems
        x, y, c, chips = _place()
        cps = []
        for a in range(n):
            r, cc = shapes[a]
            for k, (px, py) in enumerate(chips):
                j = 2 * px + py
                src = ins[a].at[j] if axes[a] == 0 else ins[a].at[0, :, pl.ds(pl.multiple_of(j * cc, 128), cc)]
                cps.append(pltpu.make_async_remote_copy(
                    src_ref=src, dst_ref=outs[a].at[k], send_sem=send.at[a, k], recv_sem=recv.at[a, k],
                    device_id=(px, py, c), device_id_type=MESH))
        return cps

    def first(ins, outs, sems):
        for cp in copies(ins, outs, sems):
            cp.start()

    def last(ins, outs, sems):
        for cp in copies(ins, outs, sems):
            cp.wait()

    return _Side(list(p3s), [jax.ShapeDtypeStruct((3, r // 2, c), BF) for r, c in shapes],
                 [pltpu.SemaphoreType.DMA((n, 3)), pltpu.SemaphoreType.DMA((n, 3))], first, last)


def _chip_sum(p3, land, shard_shape, axis, idx, name):
    r, c = shard_shape
    hr = r // 2
    tr = 64
    nt = hr // tr

    def body(idx_ref, p_ref, l_ref, o_ref):
        acc = p_ref[...].astype(F32)
        for k in range(3):
            acc = acc + l_ref[k].astype(F32)
        o_ref[...] = acc

    own = (pl.BlockSpec((None, tr, c), lambda i, idx: (idx[0], i, 0)) if axis == 0
           else pl.BlockSpec((None, tr, c), lambda i, idx: (0, i, idx[0])))
    return pl.pallas_call(
        body, name=name,
        grid_spec=pltpu.PrefetchScalarGridSpec(
            num_scalar_prefetch=1, grid=(nt,),
            in_specs=[own, pl.BlockSpec((3, tr, c), lambda i, idx: (0, i, 0))],
            out_specs=pl.BlockSpec((tr, c), lambda i, idx: (idx[1] * nt + i, 0))),
        out_shape=jax.ShapeDtypeStruct((r, c), F32),
        compiler_params=_params(("parallel",)),
    )(idx, p3, land)


def _rs_pair_gather(fulls):
    n = len(fulls)

    def body(*refs):
        ins, outs = refs[:n], refs[n:2 * n]
        send, recv = refs[2 * n:]
        x, y, c, _ = _place()
        cps = []
        for a in range(n):
            hr = fulls[a].shape[0] // 2
            rows = pl.ds(pl.multiple_of(c * hr, 8), hr)
            cp = pltpu.make_async_remote_copy(
                src_ref=ins[a].at[rows, :], dst_ref=outs[a].at[rows, :], send_sem=send.at[a], recv_sem=recv.at[a],
                device_id=(x, y, 1 - c), device_id_type=MESH)
            cp.start()
            cps.append(cp)
        for a, cp in enumerate(cps):
            cp.wait_send()
            hr = fulls[a].shape[0] // 2
            other = pl.ds(pl.multiple_of((1 - c) * hr, 8), hr)
            pltpu.make_async_remote_copy(
                src_ref=ins[a].at[other, :], dst_ref=outs[a].at[other, :], send_sem=send.at[a], recv_sem=recv.at[a],
                device_id=(x, y, 1 - c), device_id_type=MESH).wait_recv()

    return pl.pallas_call(
        body, name="grads_pair_gather",
        in_specs=[HBM] * n, out_specs=[HBM] * n,
        out_shape=[jax.ShapeDtypeStruct(f.shape, F32) for f in fulls],
        input_output_aliases={a: a for a in range(n)},
        scratch_shapes=[pltpu.SemaphoreType.DMA((n,)), pltpu.SemaphoreType.DMA((n,))],
    )(*fulls)


def _sibling_send_side(arr):
    def copy(ins, outs, sems):
        x, y, c, _ = _place()
        return pltpu.make_async_remote_copy(
            src_ref=ins[0], dst_ref=outs[0], send_sem=sems[0].at[0], recv_sem=sems[1].at[0],
            device_id=(x, y, 1 - c), device_id_type=MESH)

    return _Side([arr], [jax.ShapeDtypeStruct(arr.shape, arr.dtype)],
                 [pltpu.SemaphoreType.DMA((1,)), pltpu.SemaphoreType.DMA((1,))],
                 lambda ins, outs, sems: copy(ins, outs, sems).start(),
                 lambda ins, outs, sems: copy(ins, outs, sems).wait())


def _add_bf16(a, b, name):
    r, c = a.shape
    tr = 64

    def body(a_ref, b_ref, o_ref):
        o_ref[...] = (a_ref[...].astype(F32) + b_ref[...].astype(F32)).astype(BF)

    blk = pl.BlockSpec((tr, c), lambda i: (i, 0))
    return pl.pallas_call(
        body, name=name, grid=(r // tr,), in_specs=[blk, blk], out_specs=blk,
        out_shape=jax.ShapeDtypeStruct((r, c), BF), compiler_params=_params(("parallel",)),
    )(a, b)


def _rs_partials(grads, shapes, axes, tag):
    cidx = jnp.reshape(lax.axis_index("c"), (1,)).astype(jnp.int32)
    g3s = [_as3d(g, s, ax) for g, s, ax in zip(grads, shapes, axes)]
    lands = _rs_pair_exchange(g3s, f"grads_pair_exchange_{tag}")
    return [_pair_sum(g3, l, cidx, f"pair_sum_{tag}_{a}") for a, (g3, l) in enumerate(zip(g3s, lands))]


def _rs_finish(p3s, landed, shapes, axes):
    x, y, c = lax.axis_index("x"), lax.axis_index("y"), lax.axis_index("c")
    idx = jnp.stack([2 * x + y, c]).astype(jnp.int32)
    fulls = [_chip_sum(p3, l2, s, ax, idx, f"chip_sum_{a}")
             for a, (p3, l2, s, ax) in enumerate(zip(p3s, landed, shapes, axes))]
    return _rs_pair_gather(fulls)


NSMALL = 8


def _small_all_reduce(g_nw, dlb, g_hnw, g_wf, loss_vec):
    def body(nw_ref, lb_ref, hn_ref, wf_ref, ls_ref, out_ref, pack_ref, buf_ref, send, recv):
        x, y, c = lax.axis_index("x"), lax.axis_index("y"), lax.axis_index("c")
        me = 4 * x + 2 * y + c
        pack_ref[...] = jnp.zeros_like(pack_ref)
        pack_ref[0:1, :] = nw_ref[...]
        pack_ref[1:2, :] = lb_ref[...]
        pack_ref[2:3, 0:HK] = hn_ref[...]
        pack_ref[3:4, :] = wf_ref[...]
        pack_ref[4:5, :] = ls_ref[...]
        buf_ref[me] = pack_ref[...]
        cps = []
        for d in range(1, 8):
            dx, dy, dc = d >> 2, (d >> 1) & 1, d & 1
            peer = (1 - x if dx else x, 1 - y if dy else y, 1 - c if dc else c)
            cp = pltpu.make_async_remote_copy(
                src_ref=pack_ref, dst_ref=buf_ref.at[me], send_sem=send.at[d - 1], recv_sem=recv.at[d - 1],
                device_id=peer, device_id_type=MESH)
            cp.start()
            cps.append(cp)
        for d in range(1, 8):
            dx, dy, dc = d >> 2, (d >> 1) & 1, d & 1
            src = 4 * (1 - x if dx else x) + 2 * (1 - y if dy else y) + (1 - c if dc else c)
            pltpu.make_async_remote_copy(
                src_ref=pack_ref, dst_ref=buf_ref.at[src], send_sem=send.at[d - 1], recv_sem=recv.at[d - 1],
                device_id=(x, y, c), device_id_type=MESH).wait_recv()
        for cp in cps:
            cp.wait_send()
        acc = buf_ref[0]
        for i in range(1, 8):
            acc = acc + buf_ref[i]
        out_ref[...] = acc

    vm = pl.BlockSpec(memory_space=pltpu.VMEM)
    return pl.pallas_call(
        body, name="small_all_reduce",
        in_specs=[vm] * 5, out_specs=vm,
        out_shape=jax.ShapeDtypeStruct((NSMALL, D), F32),
        scratch_shapes=[pltpu.VMEM((NSMALL, D), F32), pltpu.VMEM((8, NSMALL, D), F32),
                        pltpu.SemaphoreType.DMA((7,)), pltpu.SemaphoreType.DMA((7,))],
    )(g_nw, dlb, g_hnw, g_wf, loss_vec)


def _adamw_math(w, g, m, v):
    m = B1 * m + (1.0 - B1) * g
    v = B2 * v + (1.0 - B2) * (g * g)
    m_hat = m / (1.0 - B1 ** STEP)
    v_hat = v / (1.0 - B2 ** STEP)
    return -LR * (m_hat / (jnp.sqrt(v_hat) + ADAM_EPS) + WD * w), m, v


def _adamw(w, g, m, v, name):
    r, c = w.shape
    tr = 64

    def body(w_ref, g_ref, m_ref, v_ref, d_ref, nm_ref, nv_ref, go_ref):
        g = g_ref[...]
        d_ref[...], nm_ref[...], nv_ref[...] = _adamw_math(w_ref[...], g, m_ref[...], v_ref[...])
        go_ref[...] = g

    blk = pl.BlockSpec((tr, c), lambda i: (i, 0))
    return pl.pallas_call(
        body, name=name, grid=(r // tr,), in_specs=[blk] * 4, out_specs=[blk] * 4,
        out_shape=[jax.ShapeDtypeStruct((r, c), F32)] * 4,
        compiler_params=_params(("parallel",)),
    )(w, g, m, v)


def _small_update(red, lbl, params):
    def body(red_ref, *refs):
        ins, outs = refs[:12], refs[12:]
        lb = _lower_bound(ins[3][...])
        dl0 = red_ref[1:2, :] * lb * (1.0 - lb)
        row = lax.broadcasted_iota(jnp.int32, (2, D), 0)
        grads = [red_ref[0:1, :], jnp.where(row == 0, dl0, -dl0), red_ref[2:3, 0:HK], red_ref[3:4, :]]
        for i, g in enumerate(grads):
            w, m, v = ins[3 * i][...], ins[3 * i + 1][...], ins[3 * i + 2][...]
            d, nm, nv = _adamw_math(w, g, m, v)
            outs[4 * i][...] = g
            outs[4 * i + 1][...] = d
            outs[4 * i + 2][...] = nm
            outs[4 * i + 3][...] = nv
        outs[16][...] = jnp.sum(red_ref[4:5, :], axis=1, keepdims=True)

    flat = [a for p in params for a in p]
    vm = pl.BlockSpec(memory_space=pltpu.VMEM)
    shapes = [jax.ShapeDtypeStruct(p[0].shape, F32) for p in params for _ in range(4)]
    return pl.pallas_call(
        body, name="small_update",
        in_specs=[vm] * 13, out_specs=[vm] * 17,
        out_shape=shapes + [jax.ShapeDtypeStruct((1, 1), F32)],
    )(red, *flat)


def kernel(x, positions, norm_w, w_in, lb_logits, hgrn_norm_w, w_branch_a, w_branch_b, w_out, final_norm_w, loss_target, m_norm_w, m_w_in, m_lb_logits, m_hgrn_norm_w, m_w_branch_a, m_w_branch_b, m_w_out, m_final_norm_w, v_norm_w, v_w_in, v_lb_logits, v_hgrn_norm_w, v_w_branch_a, v_w_branch_b, v_w_out, v_final_norm_w):
    big_w = [w_in[0], w_branch_a[0], w_branch_b[0], w_out[0]]
    big_m = [m_w_in[0], m_w_branch_a[0], m_w_branch_b[0], m_w_out[0]]
    big_v = [v_w_in[0], v_w_branch_a[0], v_w_branch_b[0], v_w_out[0]]
    shapes = [w.shape for w in big_w]
    wf = final_norm_w.reshape(1, D)

    shards = [w.astype(BF) for w in big_w]
    loc = _local_step(x[0], positions.reshape(T, 1), norm_w, lb_logits, hgrn_norm_w, wf, loss_target[0],
                      *shards, shard_shapes=shapes)
    g_big = [loc["g_win"], loc["g_wa"], loc["g_wb"], loc["g_wout"]]
    red = _small_all_reduce(loc["g_nw"], loc["dlb"], loc["g_hnw"], loc["g_wf"], loc["loss_vec"])

    small = _small_update(red, lb_logits, [
        (norm_w, m_norm_w, v_norm_w), (lb_logits, m_lb_logits, v_lb_logits),
        (hgrn_norm_w, m_hgrn_norm_w, v_hgrn_norm_w),
        (wf, m_final_norm_w.reshape(1, D), v_final_norm_w.reshape(1, D))])
    loss = small[16].reshape(())
    sg, sd, sm, sv = ([small[4 * i + j] for i in range(4)] for j in range(4))
    for lst in (sg, sd, sm, sv):
        lst[3] = lst[3].reshape(D)
    upd = [_adamw(w, g, m, v, f"adamw_{a}") for a, (w, g, m, v) in enumerate(zip(big_w, g_big, big_m, big_v))]
    bd, bm, bv, bg = ([u[j][None] for u in upd] for j in range(4))

    def order(s, b):
        return [s[0], b[0], s[1], s[2], b[1], b[2], b[3], s[3]]

    return (loss, loc["gx"][None], *order(sg, bg), *order(sd, bd), *order(sm, bm), *order(sv, bv))
```

```python
import functools

import jax
import jax.numpy as jnp
from jax import lax
from jax.experimental import pallas as pl
from jax.experimental.pallas import tpu as pltpu

T = 2048
D = 1024
NIN = 11264
HEADS = 8
HK = 128
CH = 16
NCH = T // CH
HSTEP = 2
ATT_GROUPS = ((128, 1), (512, 4), (2048, 16))
ATT_COL0 = 4096
AG_COL0 = 8704
GATE_COL0 = 9216
EPS = 1e-6
ROPE_THETA = 10000.0
LR, B1, B2, ADAM_EPS, WD, STEP = 0.001, 0.9, 0.999, 1e-08, 0.01, 10

F32 = jnp.float32
BF = jnp.bfloat16
VMEM_LIMIT = 56 * 1024 * 1024

_NN = (((1,), (0,)), ((), ()))
_NT = (((1,), (1,)), ((), ()))
_TN = (((0,), (0,)), ((), ()))


def _dot(a, b, dims=_NN):
    return lax.dot_general(a, b, dims, preferred_element_type=F32)


def _bdot(a, b, dims=_NN):
    return lax.dot_general(a.astype(BF), b.astype(BF), dims, preferred_element_type=F32)


def _sigmoid(x):
    return jax.nn.sigmoid(x)


def _params(sem=None):
    return pltpu.CompilerParams(dimension_semantics=sem, vmem_limit_bytes=VMEM_LIMIT)


def _matmul(a, b, *, ta=False, tb=False, out_dtype=F32, tm=512, tn=512, tk=None, name, side=None):
    m = a.shape[1] if ta else a.shape[0]
    kdim = a.shape[0] if ta else a.shape[1]
    n = b.shape[0] if tb else b.shape[1]
    tk = tk or kdim
    tm, tn = min(tm, m), min(tn, n)
    nm, nn, nk = m // tm, n // tn, kdim // tk
    dims = (((0 if ta else 1,), (1 if tb else 0,)), ((), ()))
    s_arrays, s_in_specs, s_shapes, s_out_specs, s_sems = _side_io(side)
    na, no = len(s_arrays), len(s_shapes)
    nacc = 1 if nk > 1 else 0

    def body(*refs):
        a_ref, b_ref = refs[:2]
        s_ins, o_ref, s_outs = refs[2:2 + na], refs[2 + na], refs[3 + na:3 + na + no]
        scratch = refs[3 + na + no:]
        s_sem_refs = scratch[nacc:]
        i, j, k = pl.program_id(0), pl.program_id(1), pl.program_id(2)
        if side is not None:
            @pl.when((i == 0) & (j == 0) & (k == 0))
            def _():
                side.first(s_ins, s_outs, s_sem_refs)

        prod = _bdot(a_ref[...], b_ref[...], dims)
        if nk == 1:
            o_ref[...] = prod.astype(out_dtype)
        else:
            acc = scratch[0]

            @pl.when(k == 0)
            def _():
                acc[...] = prod

            @pl.when(k > 0)
            def _():
                acc[...] += prod

            @pl.when(k == nk - 1)
            def _():
                o_ref[...] = acc[...].astype(out_dtype)

        if side is not None:
            @pl.when((i == nm - 1) & (j == nn - 1) & (k == nk - 1))
            def _():
                side.last(s_ins, s_outs, s_sem_refs)

    a_spec = pl.BlockSpec((tk, tm), lambda i, j, k: (k, i)) if ta else pl.BlockSpec((tm, tk), lambda i, j, k: (i, k))
    b_spec = pl.BlockSpec((tn, tk), lambda i, j, k: (j, k)) if tb else pl.BlockSpec((tk, tn), lambda i, j, k: (k, j))
    sem = ("parallel", "parallel", "arbitrary") if side is None else ("arbitrary",) * 3
    out = pl.pallas_call(
        body, name=name, grid=(nm, nn, nk),
        in_specs=[a_spec, b_spec] + s_in_specs,
        out_specs=[pl.BlockSpec((tm, tn), lambda i, j, k: (i, j))] + s_out_specs,
        out_shape=[jax.ShapeDtypeStruct((m, n), out_dtype)] + s_shapes,
        scratch_shapes=([pltpu.VMEM((tm, tn), F32)] if nk > 1 else []) + s_sems,
        compiler_params=_params(sem),
    )(a, b, *s_arrays)
    return out[0] if side is None else (out[0], out[1:])


DZ_TILE = 512


def _part_offsets(parts):
    counts = [p.shape[1] // DZ_TILE for p in parts]
    offs = [sum(counts[:i]) for i in range(len(parts))]
    return counts, offs


def _part_spec(rows, cnt, off, tile_axis):
    def index(*g):
        return (0 if rows is None else g[0], jnp.clip(g[tile_axis] - off, 0, cnt - 1))
    return index


def _grad_w_in(h, parts):
    counts, offs = _part_offsets(parts)
    n = len(parts)

    def body(h_ref, *refs):
        o_ref = refs[n]
        j = pl.program_id(0)
        for p_ref, cnt, off in zip(refs[:n], counts, offs):
            @pl.when((j >= off) & (j < off + cnt))
            def _(p_ref=p_ref):
                o_ref[...] = _bdot(h_ref[...], p_ref[...], _TN).astype(BF)

    return pl.pallas_call(
        body, name="g_win", grid=(sum(counts),),
        in_specs=[pl.BlockSpec((T, D), lambda j: (0, 0))] +
                 [pl.BlockSpec((T, DZ_TILE), _part_spec(None, c, o, 0)) for c, o in zip(counts, offs)],
        out_specs=pl.BlockSpec((D, DZ_TILE), lambda j: (0, j)),
        out_shape=jax.ShapeDtypeStruct((D, NIN), BF),
        compiler_params=_params(("parallel",)),
    )(h, *parts)


def _grad_w_in_half(h, parts, half_idx, side=None):
    counts, offs = _part_offsets(parts)
    n = len(parts)
    nj = sum(counts)
    s_arrays, s_in_specs, s_shapes, s_out_specs, s_sems = _side_io(side)
    na, no = len(s_arrays), len(s_shapes)

    def body(idx_ref, h_ref, *refs):
        s_ins, o_ref, s_outs, s_sem_refs = refs[n:n + na], refs[n + na], refs[n + na + 1:n + na + 1 + no], refs[n + na + 1 + no:]
        j = pl.program_id(0)
        if side is not None:
            @pl.when(j == 0)
            def _():
                side.first(s_ins, s_outs, s_sem_refs)

        for p_ref, cnt, off in zip(refs[:n], counts, offs):
            @pl.when((j >= off) & (j < off + cnt))
            def _(p_ref=p_ref):
                o_ref[...] = _bdot(h_ref[...], p_ref[...], _TN).astype(BF)

        if side is not None:
            @pl.when(j == nj - 1)
            def _():
                side.last(s_ins, s_outs, s_sem_refs)

    def part_spec(cnt, off):
        return pl.BlockSpec((T, DZ_TILE), lambda j, idx: (0, jnp.clip(j - off, 0, cnt - 1)))

    out = pl.pallas_call(
        body, name="g_win_half" if side is None else "g_win_half_carrying",
        grid_spec=pltpu.PrefetchScalarGridSpec(
            num_scalar_prefetch=1, grid=(nj,),
            in_specs=[pl.BlockSpec((T, D // 2), lambda j, idx: (0, idx[0]))] +
                     [part_spec(c, o) for c, o in zip(counts, offs)] + s_in_specs,
            out_specs=[pl.BlockSpec((D // 2, DZ_TILE), lambda j, idx: (0, j))] + s_out_specs,
            scratch_shapes=s_sems),
        out_shape=[jax.ShapeDtypeStruct((D // 2, NIN), BF)] + s_shapes,
        compiler_params=_params(("parallel",) if side is None else ("arbitrary",)),
    )(half_idx, h, *parts, *s_arrays)
    return out[0] if side is None else (out[0], out[1:])


def _side_io(side):
    if side is None:
        return [], [], [], [], []
    return (side.arrays, [HBM] * len(side.arrays), side.out_shapes, [HBM] * len(side.out_shapes), side.sems)


def _grad_x(parts, w_in, x, dout, norm_w, side=None):
    counts, offs = _part_offsets(parts)
    n = len(parts)
    tm = 1024
    nm, nk = T // tm, sum(counts)
    s_arrays, s_in_specs, s_shapes, s_out_specs, s_sems = _side_io(side)
    na, no = len(s_arrays), len(s_shapes)

    def body(*refs):
        w_ref, x_ref, dout_ref, nw_ref = refs[n:n + 4]
        s_ins = refs[n + 4:n + 4 + na]
        gx_ref, gw_ref = refs[n + 4 + na:n + 6 + na]
        s_outs = refs[n + 6 + na:n + 6 + na + no]
        acc = refs[n + 6 + na + no]
        s_sem_refs = refs[n + 7 + na + no:]
        i, k = pl.program_id(0), pl.program_id(1)

        @pl.when((i == 0) & (k == 0))
        def _():
            gw_ref[...] = jnp.zeros_like(gw_ref)
            if side is not None:
                side.first(s_ins, s_outs, s_sem_refs)

        @pl.when(k == 0)
        def _():
            acc[...] = jnp.zeros_like(acc)

        for p_ref, cnt, off in zip(refs[:n], counts, offs):
            @pl.when((k >= off) & (k < off + cnt))
            def _(p_ref=p_ref):
                acc[...] += _bdot(p_ref[...], w_ref[...], _NT)

        @pl.when(k == nk - 1)
        def _():
            gw = jnp.zeros((1, D), F32)
            for c in range(tm // BLK):
                rows = pl.ds(BLK * c, BLK)
                xv, dhv = x_ref[rows, :], acc[rows, :]
                r = lax.rsqrt(jnp.mean(xv * xv, axis=-1, keepdims=True) + EPS)
                nrm = xv * r
                dn = dhv * nw_ref[...]
                gw = gw + jnp.sum(dhv * nrm, axis=0, keepdims=True)
                gx_ref[rows, :] = dout_ref[rows, :] + r * (dn - nrm * jnp.mean(dn * nrm, axis=-1, keepdims=True))
            gw_ref[...] += gw

        if side is not None:
            @pl.when((i == nm - 1) & (k == nk - 1))
            def _():
                side.last(s_ins, s_outs, s_sem_refs)

    row = pl.BlockSpec((tm, D), lambda i, k: (i, 0))
    vec = pl.BlockSpec((1, D), lambda i, k: (0, 0))
    out = pl.pallas_call(
        body, name="grad_x", grid=(nm, nk),
        in_specs=[pl.BlockSpec((tm, DZ_TILE), _part_spec(0, c, o, 1)) for c, o in zip(counts, offs)] +
                 [pl.BlockSpec((D, DZ_TILE), lambda i, k: (0, k)), row, row, vec] + s_in_specs,
        out_specs=[row, vec] + s_out_specs,
        out_shape=[jax.ShapeDtypeStruct((T, D), F32), jax.ShapeDtypeStruct((1, D), F32)] + s_shapes,
        scratch_shapes=[pltpu.VMEM((tm, D), F32)] + s_sems,
        compiler_params=_params(("arbitrary", "arbitrary")),
    )(*parts, w_in, x, dout, norm_w, *s_arrays)
    return out[0], out[1], out[2:]


def _norm_and_rope_tables(x, w, pos, invf, side=None, own=None):
    tm = 256
    nm = T // tm
    s_arrays, s_in_specs, s_shapes, s_out_specs, s_sems = _side_io(side)
    na, no = len(s_arrays), len(s_shapes)
    nz = 0 if own is None else 1
    wsh, blk = own if own is not None else (None, jnp.zeros((1,), jnp.int32))

    def body(blk_ref, *refs):
        x_ref, w_ref, pos_ref, invf_ref = refs[:4]
        s_ins = refs[4 + nz:4 + nz + na]
        h_ref, cos_ref, sa_ref, sb_ref = refs[4 + nz + na:8 + nz + na]
        s_outs = refs[8 + 2 * nz + na:8 + 2 * nz + na + no]
        s_sem_refs = refs[8 + 2 * nz + na + no:]
        if side is not None:
            @pl.when(pl.program_id(0) == 0)
            def _():
                side.first(s_ins, s_outs, s_sem_refs)

        xv = x_ref[...]
        r = lax.rsqrt(jnp.mean(xv * xv, axis=-1, keepdims=True) + EPS)
        h = (xv * r * w_ref[...]).astype(BF)
        h_ref[...] = h
        if own is not None:
            refs[8 + nz + na][...] = _dot(h, refs[4][...])
        first = (lax.broadcasted_iota(jnp.int32, (tm, 128), 1) % 64) < 32
        ang = pos_ref[...].astype(F32) * invf_ref[...]
        s = jnp.sin(ang)
        cos_ref[...] = jnp.cos(ang)
        sa_ref[...] = jnp.where(first, -s, 0.0)
        sb_ref[...] = jnp.where(first, 0.0, s)
        if side is not None:
            @pl.when(pl.program_id(0) == nm - 1)
            def _():
                side.last(s_ins, s_outs, s_sem_refs)

    tab = pl.BlockSpec((tm, 128), lambda i, b: (i, 0))
    own_in = [] if own is None else [pl.BlockSpec(wsh.shape, lambda i, b: (0, 0))]
    own_out = [] if own is None else [pl.BlockSpec((tm, wsh.shape[1]), lambda i, b: (i, b[0]))]
    own_shape = [] if own is None else [jax.ShapeDtypeStruct((T, NIN), F32)]
    out = pl.pallas_call(
        body, name="norm_and_rope_tables",
        grid_spec=pltpu.PrefetchScalarGridSpec(
            num_scalar_prefetch=1, grid=(nm,),
            in_specs=[pl.BlockSpec((tm, D), lambda i, b: (i, 0)), pl.BlockSpec((1, D), lambda i, b: (0, 0)),
                      pl.BlockSpec((tm, 1), lambda i, b: (i, 0)), pl.BlockSpec((1, 128), lambda i, b: (0, 0))]
                     + own_in + s_in_specs,
            out_specs=[pl.BlockSpec((tm, D), lambda i, b: (i, 0)), tab, tab, tab] + own_out + s_out_specs,
            scratch_shapes=s_sems),
        out_shape=[jax.ShapeDtypeStruct((T, D), BF)] + [jax.ShapeDtypeStruct((T, 128), F32)] * 3 + own_shape + s_shapes,
        compiler_params=_params(("parallel",) if side is None else ("arbitrary",)),
    )(blk, x, w, pos, invf, *([] if own is None else [wsh]), *s_arrays)
    return out[0], out[1], out[2], out[3], (out[4] if own is not None else None), out[4 + nz:]


def _z_rest(h, w_in, z_own, blk, side):
    tm, tn = 1024, NIN // 8
    s_arrays, s_in_specs, s_shapes, s_out_specs, s_sems = _side_io(side)
    na, no = len(s_arrays), len(s_shapes)
    nm, ns = T // tm, 6

    def col(i, s, b):
        return (0, ((b[0] + 1 + s // 2) % 4) * 2 + s % 2)

    def body(blk_ref, h_ref, w_ref, zin_ref, *refs):
        s_ins, o_ref, s_outs, s_sem_refs = refs[:na], refs[na], refs[na + 1:na + 1 + no], refs[na + 1 + no:]
        i, s = pl.program_id(0), pl.program_id(1)

        @pl.when((i == 0) & (s == 0))
        def _():
            side.first(s_ins, s_outs, s_sem_refs)

        o_ref[...] = _dot(h_ref[...], w_ref[...])

        @pl.when((i == nm - 1) & (s == ns - 1))
        def _():
            side.last(s_ins, s_outs, s_sem_refs)

    out = pl.pallas_call(
        body, name="z_proj",
        grid_spec=pltpu.PrefetchScalarGridSpec(
            num_scalar_prefetch=1, grid=(nm, ns),
            in_specs=[pl.BlockSpec((tm, D), lambda i, s, b: (i, 0)), pl.BlockSpec((D, tn), col), HBM] + s_in_specs,
            out_specs=[pl.BlockSpec((tm, tn), lambda i, s, b: (i, col(i, s, b)[1]))] + s_out_specs,
            scratch_shapes=s_sems),
        out_shape=[jax.ShapeDtypeStruct((T, NIN), F32)] + s_shapes,
        input_output_aliases={3: 0},
        compiler_params=_params(("arbitrary", "arbitrary")),
    )(blk, h, w_in, z_own, *s_arrays)
    return out[0], out[1:]


def _lower_bound(lbl):
    mx = jnp.max(lbl, axis=0, keepdims=True)
    e = jnp.exp(lbl - mx)
    return e[0:1] / jnp.sum(e, axis=0, keepdims=True)


def _cumsum_rows(g, rows):
    b = g
    sh = 1
    while sh < CH:
        b = b + jnp.where(rows >= sh, pltpu.roll(b, sh, axis=0), 0.0)
        sh *= 2
    return b


def _rev_cumsum_rows(g, rows):
    b = g
    sh = 1
    while sh < CH:
        b = b + jnp.where(rows < CH - sh, pltpu.roll(b, CH - sh, axis=0), 0.0)
        sh *= 2
    return b


SUB = CH // 2


def _direct_block(qb, kb, vb, bb, rows8):
    ob = jnp.zeros_like(qb)
    for s in range(SUB):
        e_s = jnp.exp(jnp.where(rows8 >= s, bb - bb[s:s + 1], -jnp.inf))
        ob = ob + jnp.sum(qb * e_s * kb[s:s + 1], axis=1, keepdims=True) * vb[s:s + 1]
    return ob


def _direct_block_bwd(qb, kb, vb, bb, dob, rows8, rowc8):
    dq = dk = dv = db = jnp.zeros_like(qb)
    for s in range(SUB):
        one = (rowc8 == s).astype(F32)
        ks, vs = kb[s:s + 1], vb[s:s + 1]
        e_s = jnp.exp(jnp.where(rows8 >= s, bb - bb[s:s + 1], -jnp.inf))
        qes = qb * e_s
        w = qes * ks
        a = jnp.sum(w, axis=1, keepdims=True)
        da = jnp.sum(dob * vs, axis=1, keepdims=True)
        dv = dv + one * jnp.sum(a * dob, axis=0, keepdims=True)
        dq = dq + da * e_s * ks
        dk = dk + one * jnp.sum(da * qes, axis=0, keepdims=True)
        u = da * w
        db = db + u - one * jnp.sum(u, axis=0, keepdims=True)
    return dq, dk, dv, db


def _cross_factors(q, k, b):
    ref = b[SUB - 1:SUB]
    e_hi, e_lo = jnp.exp(b[SUB:] - ref), jnp.exp(ref - b[:SUB])
    return q[SUB:] * e_hi, k[:SUB] * e_lo, e_hi, e_lo


def _intra_fwd(q, k, v, b, rows8):
    lo = _direct_block(q[:SUB], k[:SUB], v[:SUB], b[:SUB], rows8)
    hi = _direct_block(q[SUB:], k[SUB:], v[SUB:], b[SUB:], rows8)
    qe_hi, ke_lo, _, _ = _cross_factors(q, k, b)
    for s in range(SUB):
        hi = hi + jnp.sum(qe_hi * ke_lo[s:s + 1], axis=1, keepdims=True) * v[s:s + 1]
    return jnp.concatenate([lo, hi], axis=0)


def _intra_bwd(q, k, v, b, do, rows8, rowc8):
    dq_lo, dk_lo, dv_lo, db_lo = _direct_block_bwd(q[:SUB], k[:SUB], v[:SUB], b[:SUB], do[:SUB], rows8, rowc8)
    dq_hi, dk_hi, dv_hi, db_hi = _direct_block_bwd(q[SUB:], k[SUB:], v[SUB:], b[SUB:], do[SUB:], rows8, rowc8)
    qe_hi, ke_lo, e_hi, e_lo = _cross_factors(q, k, b)
    do_hi, v_lo = do[SUB:], v[:SUB]
    dqe = dke = jnp.zeros_like(qe_hi)
    for s in range(SUB):
        one = (rowc8 == s).astype(F32)
        a = jnp.sum(qe_hi * ke_lo[s:s + 1], axis=1, keepdims=True)
        da = jnp.sum(do_hi * v_lo[s:s + 1], axis=1, keepdims=True)
        dv_lo = dv_lo + one * jnp.sum(a * do_hi, axis=0, keepdims=True)
        dqe = dqe + da * ke_lo[s:s + 1]
        dke = dke + one * jnp.sum(da * qe_hi, axis=0, keepdims=True)
    u_hi, u_lo = dqe * qe_hi, dke * ke_lo
    d_ref = jnp.sum(u_lo, axis=0, keepdims=True) - jnp.sum(u_hi, axis=0, keepdims=True)
    db_lo = db_lo - u_lo + (rowc8 == SUB - 1).astype(F32) * d_ref
    cat = lambda lo, hi: jnp.concatenate([lo, hi], axis=0)
    return (cat(dq_lo, dq_hi + dqe * e_hi), cat(dk_lo + dke * e_lo, dk_hi), cat(dv_lo, dv_hi),
            cat(db_lo, db_hi + u_hi))


def _hgrn_fwd(z, lbl, nw):
    def body(hq_ref, hf_ref, hi_ref, hg_ref, lbl_ref, nw_ref, oraw_ref, og_ref, sh_ref, st_ref):
        @pl.when(pl.program_id(0) == 0)
        def _():
            st_ref[...] = jnp.zeros_like(st_ref)

        lb_all = _lower_bound(lbl_ref[...])
        rows = lax.broadcasted_iota(jnp.int32, (CH, HK), 0)
        rows8 = lax.broadcasted_iota(jnp.int32, (SUB, HK), 0)
        nwv = nw_ref[...]
        for cc, h in [(cc, h) for cc in range(HSTEP) for h in range(HEADS)]:
            rs = slice(CH * cc, CH * (cc + 1))
            sl = slice(HK * h, HK * (h + 1))
            lb = lb_all[:, sl]
            hq, hf, v, hg = hq_ref[rs, sl], hf_ref[rs, sl], hi_ref[rs, sl], hg_ref[rs, sl]
            q = hq * _sigmoid(hq)
            f = lb + (1.0 - lb) * _sigmoid(hf)
            k = 1.0 - f
            b = _cumsum_rows(jnp.log(f), rows)
            sh_ref[cc, h] = st_ref[h]
            o = _bdot(q * jnp.exp(b), st_ref[h], _NT) + _intra_fwd(q, k, v, b, rows8)
            bl = b[CH - 1:CH]
            st_ref[h] = st_ref[h] * jnp.exp(bl)
            st_ref[h] += _bdot(v, k * jnp.exp(bl - b), _TN)
            oraw_ref[rs, sl] = o
            nrm = o * lax.rsqrt(jnp.mean(o * o, axis=1, keepdims=True) + EPS)
            og_ref[rs, sl] = (nrm * nwv * (hg * _sigmoid(hg))).astype(BF)

    zblk = lambda c: pl.BlockSpec((CH * HSTEP, D), lambda i, c=c: (i, c))
    return pl.pallas_call(
        body, name="hgrn_fwd", grid=(NCH // HSTEP,),
        in_specs=[zblk(0), zblk(1), zblk(2), zblk(3),
                  pl.BlockSpec((2, D), lambda i: (0, 0)), pl.BlockSpec((1, HK), lambda i: (0, 0))],
        out_specs=[zblk(0), zblk(0),
                   pl.BlockSpec((HSTEP, HEADS, HK, HK), lambda i: (i, 0, 0, 0))],
        out_shape=[jax.ShapeDtypeStruct((T, D), F32), jax.ShapeDtypeStruct((T, D), BF),
                   jax.ShapeDtypeStruct((NCH, HEADS, HK, HK), F32)],
        scratch_shapes=[pltpu.VMEM((HEADS, HK, HK), F32)],
        compiler_params=_params(("arbitrary",)),
    )(z, z, z, z, lbl, nw)


def _hgrn_bwd(z, lbl, nw, oraw, dog, shist, side=None):
    hstep = 1
    s_arrays, s_in_specs, s_shapes, s_out_specs, s_sems = _side_io(side)
    na, no = len(s_arrays), len(s_shapes)

    def body(*refs):
        hq_ref, hf_ref, hi_ref, hg_ref, lbl_ref, nw_ref, oraw_ref, dog_ref, sh_ref = refs[:9]
        s_ins = refs[9:9 + na]
        dz_ref, dlb_ref, dnw_ref = refs[9 + na:12 + na]
        s_outs = refs[12 + na:12 + na + no]
        dst_ref = refs[12 + na + no]
        s_sem_refs = refs[13 + na + no:]

        @pl.when(pl.program_id(0) == 0)
        def _():
            dst_ref[...] = jnp.zeros_like(dst_ref)
            dlb_ref[...] = jnp.zeros_like(dlb_ref)
            dnw_ref[...] = jnp.zeros_like(dnw_ref)
            if side is not None:
                side.first(s_ins, s_outs, s_sem_refs)

        lb_all = _lower_bound(lbl_ref[...])
        rows = lax.broadcasted_iota(jnp.int32, (CH, HK), 0)
        rowc = lax.broadcasted_iota(jnp.int32, (CH, 1), 0)
        rows8 = lax.broadcasted_iota(jnp.int32, (SUB, HK), 0)
        rowc8 = lax.broadcasted_iota(jnp.int32, (SUB, 1), 0)
        nwv = nw_ref[...]
        dnw = jnp.zeros((1, HK), F32)
        for cc, h in [(cc, h) for cc in reversed(range(hstep)) for h in range(HEADS)]:
            rs = slice(CH * cc, CH * (cc + 1))
            sl = slice(HK * h, HK * (h + 1))
            lb = lb_all[:, sl]
            hq, hf, v, hg = hq_ref[rs, sl], hf_ref[rs, sl], hi_ref[rs, sl], hg_ref[rs, sl]
            o, dg_out = oraw_ref[rs, sl], dog_ref[rs, sl]
            sg = _sigmoid(hg)
            sil = hg * sg
            r = lax.rsqrt(jnp.mean(o * o, axis=1, keepdims=True) + EPS)
            nrm = o * r
            d_hg = dg_out * (nrm * nwv) * (sg * (1.0 + hg * (1.0 - sg)))
            dn = dg_out * nwv * sil
            dnw = dnw + jnp.sum(dg_out * nrm * sil, axis=0, keepdims=True)
            do = r * (dn - nrm * jnp.mean(dn * nrm, axis=1, keepdims=True))
            sq = _sigmoid(hq)
            q = hq * sq
            sig = _sigmoid(hf)
            f = lb + (1.0 - lb) * sig
            k = 1.0 - f
            b = _cumsum_rows(jnp.log(f), rows)
            eb = jnp.exp(b)
            qe = q * eb
            bl = b[CH - 1:CH]
            ebl = jnp.exp(bl)
            kdec = jnp.exp(bl - b)
            ke = k * kdec
            dqe = _bdot(do, sh_ref[cc, h])
            dq = dqe * eb
            db = dqe * qe
            dke = _bdot(v, dst_ref[h])
            dv = _bdot(ke, dst_ref[h], _NT)
            dk = dke * kdec
            rr = dke * ke
            db = db - rr
            db_last = (jnp.sum(rr, axis=0, keepdims=True)
                       + ebl * jnp.sum(dst_ref[h] * sh_ref[cc, h], axis=0, keepdims=True))
            dst_ref[h] = dst_ref[h] * ebl
            dst_ref[h] += _bdot(do, qe, _TN)
            dq_i, dk_i, dv_i, db_i = _intra_bwd(q, k, v, b, do, rows8, rowc8)
            dq, dk, dv = dq + dq_i, dk + dk_i, dv + dv_i
            db = db + db_i + (rowc == CH - 1).astype(F32) * db_last
            dgl = _rev_cumsum_rows(db, rows)
            df = dgl / f - dk
            dlb_ref[:, sl] += jnp.sum(df * (1.0 - sig), axis=0, keepdims=True)
            dz_ref[rs, sl] = (dq * (sq * (1.0 + hq * (1.0 - sq)))).astype(BF)
            dz_ref[rs, D + HK * h:D + HK * (h + 1)] = (df * (1.0 - lb) * sig * (1.0 - sig)).astype(BF)
            dz_ref[rs, 2 * D + HK * h:2 * D + HK * (h + 1)] = dv.astype(BF)
            dz_ref[rs, 3 * D + HK * h:3 * D + HK * (h + 1)] = d_hg.astype(BF)
        dnw_ref[...] += dnw
        if side is not None:
            @pl.when(pl.program_id(0) == NCH // hstep - 1)
            def _():
                side.last(s_ins, s_outs, s_sem_refs)

    rev = lambda i: NCH // hstep - 1 - i
    zblk = lambda c: pl.BlockSpec((CH * hstep, D), lambda i, c=c: (rev(i), c))
    out = pl.pallas_call(
        body, name="hgrn_bwd", grid=(NCH // hstep,),
        in_specs=[zblk(0), zblk(1), zblk(2), zblk(3),
                  pl.BlockSpec((2, D), lambda i: (0, 0)), pl.BlockSpec((1, HK), lambda i: (0, 0)),
                  zblk(0), zblk(0),
                  pl.BlockSpec((hstep, HEADS, HK, HK), lambda i: (rev(i), 0, 0, 0))] + s_in_specs,
        out_specs=[pl.BlockSpec((CH * hstep, 4 * D), lambda i: (rev(i), 0)),
                   pl.BlockSpec((1, D), lambda i: (0, 0)), pl.BlockSpec((1, HK), lambda i: (0, 0))] + s_out_specs,
        out_shape=[jax.ShapeDtypeStruct((T, 4 * D), BF), jax.ShapeDtypeStruct((1, D), F32),
                   jax.ShapeDtypeStruct((1, HK), F32)] + s_shapes,
        scratch_shapes=[pltpu.VMEM((HEADS, HK, HK), F32)] + s_sems,
        compiler_params=_params(("arbitrary",)),
    )(z, z, z, z, lbl, nw, oraw, dog, shist, *s_arrays)
    return out[0], out[1], out[2], out[3:]


BLK = 128
NBLK = T // BLK
QK_SCALE = 0.125


def _head_masks():
    lane = lax.broadcasted_iota(jnp.int32, (1, BLK), 1)
    return [(lane < 64).astype(F32), (lane >= 64).astype(F32)]


def _pieces(dil):
    m = T // dil
    out = []
    for r in range(dil):
        for j in range(m // BLK):
            start = r + dil * BLK * j
            rows = pl.ds(start, BLK, stride=dil) if dil > 1 else pl.ds(start, BLK)
            out.append((rows, r * m + BLK * j))
    return out


def _rope(x, c, sa, sb):
    return x * c + pltpu.roll(x, 96, axis=1) * sa + pltpu.roll(x, 32, axis=1) * sb


def _rope_t(d, c, sa, sb):
    return d * c + pltpu.roll(d * sa, 32, axis=1) + pltpu.roll(d * sb, 96, axis=1)


def _rope_and_regroup(dil, q_ref, k_ref, v_ref, tables, stage_q, stage_k, qr_ref, kr_ref, vr_ref):
    cos_ref, sa_ref, sb_ref = tables
    to_q, to_k = (qr_ref, kr_ref) if dil == 1 else (stage_q, stage_k)
    for c in range(T // BLK):
        rows = pl.ds(BLK * c, BLK)
        cs, sa, sb = cos_ref[rows, :], sa_ref[rows, :], sb_ref[rows, :]
        to_q[rows, :] = (_rope(q_ref[rows, :], cs, sa, sb) * QK_SCALE).astype(to_q.dtype)
        to_k[rows, :] = _rope(k_ref[rows, :], cs, sa, sb).astype(to_k.dtype)
    for rows, dst in _pieces(dil):
        drows = pl.ds(dst, BLK)
        if dil > 1:
            qr_ref[drows, :] = stage_q[rows, :].astype(qr_ref.dtype)
            kr_ref[drows, :] = stage_k[rows, :].astype(kr_ref.dtype)
        vr_ref[drows, :] = v_ref[rows, :].astype(vr_ref.dtype)


def _window_bias(bias_ref):
    ii = lax.broadcasted_iota(jnp.int32, (2 * BLK, BLK), 0) % BLK
    jj = lax.broadcasted_iota(jnp.int32, (2 * BLK, BLK), 1)
    bias_ref[0] = jnp.where(jj <= ii, 0.0, -jnp.inf)
    bias_ref[1] = jnp.where(jj >= ii, 0.0, -jnp.inf)


def _blocks(bi):
    if isinstance(bi, int):
        return pl.ds(bi * BLK, BLK), pl.ds(max(bi - 1, 0) * BLK, BLK)
    return (pl.ds(pl.multiple_of(bi * BLK, BLK), BLK),
            pl.ds(pl.multiple_of(jnp.maximum(bi - 1, 0) * BLK, BLK), BLK))


def _stack_heads(x, masks):
    return jnp.concatenate([x * masks[0].astype(x.dtype), x * masks[1].astype(x.dtype)], axis=0).astype(BF)


def _attn_fwd(z, cos, sa, sb):
    def body(q_ref, k_ref, v_ref, ag_ref, cos_ref, sa_ref, sb_ref, ob_ref, opre_ref, lse_ref, qr_ref, kr_ref, vr_ref,
             bias_ref, og_ref, lg_ref, otok_ref, ltok_ref, sc_ref):
        g = pl.program_id(1)
        masks = _head_masks()

        @pl.when(g == 0)
        def _():
            _window_bias(bias_ref)

        def group(gi):
            dil = ATT_GROUPS[gi][1]
            nblk = (T // dil) // BLK
            _rope_and_regroup(dil, q_ref, k_ref, v_ref, (cos_ref, sa_ref, sb_ref), lg_ref.at[0], lg_ref.at[1],
                              qr_ref, kr_ref, vr_ref)

            def scores(bi, slot):
                cur, prev = _blocks(bi)
                q2 = _stack_heads(qr_ref[cur, :], masks)
                sc_ref[slot, 0] = _dot(q2, kr_ref[cur, :], _NT) + bias_ref[0]
                if nblk > 1:
                    sc_ref[slot, 1] = (_dot(q2, kr_ref[prev, :], _NT)
                                       + (bias_ref[1] + jnp.where((bi % nblk) != 0, 0.0, -jnp.inf)))

            def finish(bi, slot):
                cur, prev = _blocks(bi)
                s_c, vc = sc_ref[slot, 0], vr_ref[cur, :]
                if nblk > 1:
                    s_p, vp = sc_ref[slot, 1], vr_ref[prev, :]
                    mx = jnp.max(jnp.maximum(s_c, s_p), axis=1, keepdims=True)
                    p_c, p_p = jnp.exp(s_c - mx), jnp.exp(s_p - mx)
                    den = jnp.sum(p_c + p_p, axis=1, keepdims=True)
                    oh = _dot(p_c.astype(BF), vc) + _dot(p_p.astype(BF), vp)
                else:
                    mx = jnp.max(s_c, axis=1, keepdims=True)
                    p_c = jnp.exp(s_c - mx)
                    den = jnp.sum(p_c, axis=1, keepdims=True)
                    oh = _dot(p_c.astype(BF), vc)
                on = oh / den
                lsev = jnp.broadcast_to(mx + jnp.log(den), (2 * BLK, BLK))
                og_ref[cur, :] = on[:BLK] * masks[0] + on[BLK:] * masks[1]
                lg_ref[0, cur, :] = lsev[:BLK]
                lg_ref[1, cur, :] = lsev[BLK:]

            def pair(j, carry):
                finish(2 * j, 0)
                scores(2 * j + 1, 1)
                finish(2 * j + 1, 1)
                scores(jnp.minimum(2 * j + 2, NBLK - 1), 0)
                return carry

            scores(0, 0)
            lax.fori_loop(0, NBLK // 2, pair, 0)
            for rows, src in _pieces(dil):
                srows = pl.ds(src, BLK)
                otok_ref[gi, rows, :] = og_ref[srows, :]
                ltok_ref[gi, 0, rows, :] = lg_ref[0, srows, :]
                ltok_ref[gi, 1, rows, :] = lg_ref[1, srows, :]

        for gi in range(3):
            pl.when(g == gi)(functools.partial(group, gi))

        @pl.when(g == 2)
        def _():
            for c in range(T // BLK):
                rows = pl.ds(BLK * c, BLK)
                wts = []
                for hh in range(2):
                    l0, l1, l2 = ltok_ref[0, hh, rows, :], ltok_ref[1, hh, rows, :], ltok_ref[2, hh, rows, :]
                    mx = jnp.maximum(jnp.maximum(l0, l1), l2)
                    e0, e1, e2 = jnp.exp(l0 - mx), jnp.exp(l1 - mx), jnp.exp(l2 - mx)
                    tot = e0 + e1 + e2
                    lse_ref[rows, BLK * hh:BLK * (hh + 1)] = mx + jnp.log(tot)
                    inv = 1.0 / tot
                    wts.append([e0 * inv, e1 * inv, e2 * inv])
                o = sum((wts[0][gi] * masks[0] + wts[1][gi] * masks[1]) * otok_ref[gi, rows, :] for gi in range(3))
                ag = ag_ref[rows, :]
                opre_ref[rows, :] = o
                ob_ref[rows, :] = (o * (ag * _sigmoid(ag))).astype(BF)

    c0 = ATT_COL0 // BLK
    zspec = lambda part: pl.BlockSpec((T, BLK), lambda p, g, part=part: (0, c0 + 12 * part + 4 * g + p))
    outspec = pl.BlockSpec((T, BLK), lambda p, g: (0, p))
    table = pl.BlockSpec((T, BLK), lambda p, g: (0, 0))
    regrouped = pl.BlockSpec((None, T, BLK), lambda p, g: (g, 0, p))
    big = lambda: pltpu.VMEM((T, BLK), F32)
    return pl.pallas_call(
        body, name="attn_fwd", grid=(4, 3),
        in_specs=[zspec(0), zspec(1), zspec(2),
                  pl.BlockSpec((T, BLK), lambda p, g: (0, AG_COL0 // BLK + p)), table, table, table],
        out_specs=[outspec, outspec, pl.BlockSpec((T, 2 * BLK), lambda p, g: (0, p)), regrouped, regrouped, regrouped],
        out_shape=[jax.ShapeDtypeStruct((T, 512), BF), jax.ShapeDtypeStruct((T, 512), F32),
                   jax.ShapeDtypeStruct((T, 8 * BLK), F32)] + [jax.ShapeDtypeStruct((3, T, 512), BF)] * 3,
        scratch_shapes=[pltpu.VMEM((2, 2 * BLK, BLK), F32), big(),
                        pltpu.VMEM((2, T, BLK), F32), pltpu.VMEM((3, T, BLK), F32), pltpu.VMEM((3, 2, T, BLK), F32),
                        pltpu.VMEM((2, 2, 2 * BLK, BLK), F32)],
        compiler_params=_params(("parallel", "arbitrary")),
    )(z, z, z, z, cos, sa, sb)


def _attn_bwd(z, qs, ks, vs, cos, sa, sb, opre, lse, dob):
    def body(qs_ref, ks_ref, vs_ref, ag_ref, cos_ref, sa_ref, sb_ref, o_ref, lse0_ref, lse1_ref, dob_ref,
             dq_ref, dk_ref, dv_ref, dag_ref,
             bias_ref, dtok_ref, qr_ref, kr_ref, vr_ref, dor_ref, lr_ref, dr_ref,
             dqr_ref, dkr_ref, dvr_ref, pd_ref, dotok_ref):
        g = pl.program_id(1)
        masks = _head_masks()

        @pl.when(g == 0)
        def _():
            _window_bias(bias_ref)
            for c in range(T // BLK):
                rows = pl.ds(BLK * c, BLK)
                ag, dob_v, o = ag_ref[rows, :], dob_ref[rows, :], o_ref[rows, :]
                sg = _sigmoid(ag)
                dag_ref[rows, :] = (dob_v * o * (sg * (1.0 + ag * (1.0 - sg)))).astype(BF)
                do = dob_v * (ag * sg)
                dotok_ref[rows, :] = do
                prod = do * o
                for hh, mh in enumerate(masks):
                    dtok_ref[hh, rows, :] = jnp.broadcast_to(jnp.sum(prod * mh, axis=1, keepdims=True), (BLK, BLK))

        def group(gi):
            dil = ATT_GROUPS[gi][1]
            nblk = (T // dil) // BLK
            for rows, dst in _pieces(dil):
                drows = pl.ds(dst, BLK)
                dor_ref[drows, :] = dotok_ref[rows, :]
                for hh, lse_ref in enumerate((lse0_ref, lse1_ref)):
                    lr_ref[hh, drows, :] = lse_ref[rows, :]
                    dr_ref[hh, drows, :] = dtok_ref[hh, rows, :]
            dkr_ref[...] = jnp.zeros_like(dkr_ref)
            dvr_ref[...] = jnp.zeros_like(dvr_ref)

            def probs(bi, slot):
                cur, prev = _blocks(bi)
                q2, do2 = _stack_heads(qs_ref[cur, :], masks), _stack_heads(dor_ref[cur, :], masks)
                lh = jnp.concatenate([lr_ref[0, cur, :], lr_ref[1, cur, :]], axis=0)
                dh = jnp.concatenate([dr_ref[0, cur, :], dr_ref[1, cur, :]], axis=0)
                p_c = jnp.exp(_dot(q2, ks_ref[cur, :], _NT) + bias_ref[0] - lh)
                pd_ref[slot, 0] = p_c.astype(BF)
                pd_ref[slot, 1] = (p_c * (_dot(do2, vs_ref[cur, :], _NT) - dh)).astype(BF)
                if nblk > 1:
                    bias_p = bias_ref[1] + jnp.where((bi % nblk) != 0, 0.0, -jnp.inf)
                    p_p = jnp.exp(_dot(q2, ks_ref[prev, :], _NT) + bias_p - lh)
                    pd_ref[slot, 2] = p_p.astype(BF)
                    pd_ref[slot, 3] = (p_p * (_dot(do2, vs_ref[prev, :], _NT) - dh)).astype(BF)

            def grads(bi, slot):
                cur, prev = _blocks(bi)
                q2, do2 = _stack_heads(qs_ref[cur, :], masks), _stack_heads(dor_ref[cur, :], masks)
                p_c, ds_c = pd_ref[slot, 0], pd_ref[slot, 1]
                dq2 = _dot(ds_c, ks_ref[cur, :])
                dkr_ref[cur, :] += _dot(ds_c, q2, _TN)
                dvr_ref[cur, :] += _dot(p_c, do2, _TN)
                if nblk > 1:
                    p_p, ds_p = pd_ref[slot, 2], pd_ref[slot, 3]
                    dq2 = dq2 + _dot(ds_p, ks_ref[prev, :])
                    dkr_ref[prev, :] += _dot(ds_p, q2, _TN)
                    dvr_ref[prev, :] += _dot(p_p, do2, _TN)
                dqr_ref[cur, :] = dq2[:BLK] * masks[0] + dq2[BLK:] * masks[1]

            def pair(j, carry):
                grads(2 * j, 0)
                probs(2 * j + 1, 1)
                grads(2 * j + 1, 1)
                probs(jnp.minimum(2 * j + 2, NBLK - 1), 0)
                return carry

            probs(0, 0)
            lax.fori_loop(0, NBLK // 2, pair, 0)
            if dil > 1:
                for rows, src in _pieces(dil):
                    srows = pl.ds(src, BLK)
                    qr_ref[rows, :] = dqr_ref[srows, :]
                    kr_ref[rows, :] = dkr_ref[srows, :]
                    vr_ref[rows, :] = dvr_ref[srows, :]
            tq, tk, tv = (qr_ref, kr_ref, vr_ref) if dil > 1 else (dqr_ref, dkr_ref, dvr_ref)
            for c in range(T // BLK):
                rows = pl.ds(BLK * c, BLK)
                cs, sa, sb = cos_ref[rows, :], sa_ref[rows, :], sb_ref[rows, :]
                dq_ref[rows, :] = _rope_t(tq[rows, :] * QK_SCALE, cs, sa, sb).astype(BF)
                dk_ref[rows, :] = _rope_t(tk[rows, :], cs, sa, sb).astype(BF)
                dv_ref[rows, :] = tv[rows, :].astype(BF)

        for gi in range(3):
            pl.when(g == gi)(functools.partial(group, gi))

    regrouped = pl.BlockSpec((None, T, BLK), lambda p, g: (g, 0, p))
    pspec = pl.BlockSpec((T, BLK), lambda p, g: (0, p))
    gspec = pl.BlockSpec((T, BLK), lambda p, g: (0, 4 * g + p))
    table = pl.BlockSpec((T, BLK), lambda p, g: (0, 0))
    big = lambda: pltpu.VMEM((T, BLK), F32)
    two = lambda: pltpu.VMEM((2, T, BLK), F32)
    return pl.pallas_call(
        body, name="attn_bwd", grid=(4, 3),
        in_specs=[regrouped, regrouped, regrouped,
                  pl.BlockSpec((T, BLK), lambda p, g: (0, AG_COL0 // BLK + p)), table, table, table,
                  pspec, pl.BlockSpec((T, BLK), lambda p, g: (0, 2 * p)),
                  pl.BlockSpec((T, BLK), lambda p, g: (0, 2 * p + 1)), pspec],
        out_specs=[gspec, gspec, gspec, pspec],
        out_shape=[jax.ShapeDtypeStruct((T, 1536), BF), jax.ShapeDtypeStruct((T, 1536), BF),
                   jax.ShapeDtypeStruct((T, 1536), BF), jax.ShapeDtypeStruct((T, 512), BF)],
        scratch_shapes=[pltpu.VMEM((2, 2 * BLK, BLK), F32), two(), big(), big(), big(), big(),
                        two(), two(), big(), big(), big(), pltpu.VMEM((2, 4, 2 * BLK, BLK), BF), big()],
        compiler_params=_params(("parallel", "arbitrary")),
    )(qs, ks, vs, z, cos, sa, sb, opre, lse, lse, dob)


def _merge_fwd(ya, yb, z):
    tm = 256

    def body(ya_ref, yb_ref, ga_ref, gb_ref, m_ref):
        m_ref[...] = (_sigmoid(ga_ref[...]) * ya_ref[...] + _sigmoid(gb_ref[...]) * yb_ref[...]).astype(BF)

    row = pl.BlockSpec((tm, D), lambda i: (i, 0))
    return pl.pallas_call(
        body, name="merge_fwd", grid=(T // tm,),
        in_specs=[row, row, pl.BlockSpec((tm, D), lambda i: (i, GATE_COL0 // D)),
                  pl.BlockSpec((tm, D), lambda i: (i, GATE_COL0 // D + 1))],
        out_specs=row, out_shape=jax.ShapeDtypeStruct((T, D), BF),
        compiler_params=_params(("parallel",)),
    )(ya, yb, z, z)


def _out_loss(merged, w_out, x, tgt, wf):
    tm = 256

    def body(m_ref, w_ref, x_ref, t_ref, wf_ref, dout_ref, loss_ref, gwf_ref):
        @pl.when(pl.program_id(0) == 0)
        def _():
            loss_ref[...] = jnp.zeros_like(loss_ref)
            gwf_ref[...] = jnp.zeros_like(gwf_ref)

        out = x_ref[...] + _dot(m_ref[...], w_ref[...])
        r = lax.rsqrt(jnp.mean(out * out, axis=-1, keepdims=True) + EPS)
        yh = out * r
        wfv = wf_ref[...]
        err = yh * wfv - t_ref[...]
        loss_ref[...] += jnp.sum(err * err, axis=0, keepdims=True) * (0.5 / D)
        dy = err * (1.0 / D)
        gwf_ref[...] += jnp.sum(dy * yh, axis=0, keepdims=True)
        dyh = dy * wfv
        dout_ref[...] = r * (dyh - yh * jnp.mean(dyh * yh, axis=-1, keepdims=True))

    row = pl.BlockSpec((tm, D), lambda i: (i, 0))
    vec = pl.BlockSpec((1, D), lambda i: (0, 0))
    return pl.pallas_call(
        body, name="out_loss", grid=(T // tm,),
        in_specs=[row, pl.BlockSpec((D, D), lambda i: (0, 0)), row, row, vec],
        out_specs=[row, vec, vec],
        out_shape=[jax.ShapeDtypeStruct((T, D), F32), jax.ShapeDtypeStruct((1, D), F32),
                   jax.ShapeDtypeStruct((1, D), F32)],
        compiler_params=_params(("arbitrary",)),
    )(merged, w_out, x, tgt, wf)


def _merge_bwd(dm, ya, yb, z):
    tm = 256

    def body(dm_ref, ya_ref, yb_ref, ga_ref, gb_ref, dya_ref, dyb_ref, dg_ref):
        dmv = dm_ref[...]
        sa, sb = _sigmoid(ga_ref[...]), _sigmoid(gb_ref[...])
        dya_ref[...] = (sa * dmv).astype(BF)
        dyb_ref[...] = (sb * dmv).astype(BF)
        dg_ref[:, :D] = (dmv * ya_ref[...] * sa * (1.0 - sa)).astype(BF)
        dg_ref[:, D:] = (dmv * yb_ref[...] * sb * (1.0 - sb)).astype(BF)

    row = pl.BlockSpec((tm, D), lambda i: (i, 0))
    return pl.pallas_call(
        body, name="merge_bwd", grid=(T // tm,),
        in_specs=[row, row, row, pl.BlockSpec((tm, D), lambda i: (i, GATE_COL0 // D)),
                  pl.BlockSpec((tm, D), lambda i: (i, GATE_COL0 // D + 1))],
        out_specs=[row, row, pl.BlockSpec((tm, 2 * D), lambda i: (i, 0))],
        out_shape=[jax.ShapeDtypeStruct((T, D), BF), jax.ShapeDtypeStruct((T, D), BF),
                   jax.ShapeDtypeStruct((T, 2 * D), BF)],
        compiler_params=_params(("parallel",)),
    )(dm, ya, yb, z, z)


def _rope_inv_freq():
    inv = ROPE_THETA ** (-jnp.arange(0, 64, 2, dtype=F32) / 64)
    return jnp.tile(inv, 4).reshape(1, BLK)


def _local_step(x, pos, norm_w, lbl, hnw, wf, tgt, w_in, w_a, w_b, w_out, shard_shapes=()):
    invf = _rope_inv_freq()
    if shard_shapes:
        blk = jnp.reshape(2 * lax.axis_index("x") + lax.axis_index("y"), (1,)).astype(jnp.int32)
        h, cos, sa, sb, z_own, (w_in,) = _norm_and_rope_tables(
            x, norm_w, pos, invf, side=_gather_side([w_in], WEIGHT_AXES[:1]), own=(w_in, blk))
        z, (w_a, w_b, w_out) = _z_rest(h, w_in, z_own, blk, side=_gather_side([w_a, w_b, w_out], WEIGHT_AXES[1:]))
    else:
        h, cos, sa, sb, _, _ = _norm_and_rope_tables(x, norm_w, pos, invf)
        z = _matmul(h, w_in, tm=T, tn=512, name="z_proj")
    oraw, og, shist = _hgrn_fwd(z, lbl, hnw)
    ob, opre, lse, qs, ks, vs = _attn_fwd(z, cos, sa, sb)
    ya = _matmul(og, w_a, tm=1024, tn=512, name="ya_proj")
    yb = _matmul(ob, w_b, tm=1024, tn=512, name="yb_proj")
    merged = _merge_fwd(ya, yb, z)
    dout, loss_vec, g_wf = _out_loss(merged, w_out, x, tgt, wf)

    dm = _matmul(dout, w_out, tb=True, tm=1024, tn=512, name="d_merged")
    g_wout = _matmul(merged, dout, ta=True, out_dtype=BF, tm=512, tn=1024, name="g_wout")
    dya, dyb, dgates = _merge_bwd(dm, ya, yb, z)
    dog = _matmul(dya, w_a, tb=True, tm=1024, tn=512, name="d_og")
    g_wa = _matmul(og, dya, ta=True, out_dtype=BF, tm=512, tn=1024, name="g_wa")
    dob = _matmul(dyb, w_b, tb=True, tm=1024, tn=512, name="d_ob")
    g_wb = _matmul(ob, dyb, ta=True, out_dtype=BF, tm=512, tn=1024, name="g_wb")
    small = [g_wa, g_wb, g_wout]
    side_s = side_w = None
    if shard_shapes:
        p3_s = _rs_partials(small, shard_shapes[1:], WEIGHT_AXES[1:], "small")
        side_s = _chip_exchange_side(p3_s, shard_shapes[1:], WEIGHT_AXES[1:])
    dz_h, dlb, g_hnw, land_s = _hgrn_bwd(z, lbl, hnw, oraw, dog, shist, side=side_s)
    dq, dk, dv, dag = _attn_bwd(z, qs, ks, vs, cos, sa, sb, opre, lse, dob)
    dz_parts = [dz_h, dq, dk, dv, dag, dgates]
    if shard_shapes:
        c = lax.axis_index("c")
        half = lambda i: jnp.reshape(i, (1,)).astype(jnp.int32)
        g_send = _grad_w_in_half(h, dz_parts, half(1 - c))
        g_keep, (g_sib,) = _grad_w_in_half(h, dz_parts, half(c), side=_sibling_send_side(g_send))
        p3_w = [_add_bf16(g_keep, g_sib, "pair_sum_w_in").reshape(1, D // 2, NIN)]
        side_w = _chip_exchange_side(p3_w, shard_shapes[:1], WEIGHT_AXES[:1])
    else:
        g_big = [_grad_w_in(h, dz_parts)] + small
    gx, g_nw, land_w = _grad_x(dz_parts, w_in, x, dout, norm_w, side=side_w)
    if shard_shapes:
        g_big = _rs_finish(p3_w + p3_s, list(land_w) + list(land_s), shard_shapes, WEIGHT_AXES)
    return dict(loss_vec=loss_vec, gx=gx, g_nw=g_nw, dlb=dlb, g_hnw=g_hnw, g_wf=g_wf,
                g_win=g_big[0], g_wa=g_big[1], g_wb=g_big[2], g_wout=g_big[3])


MESH = pl.DeviceIdType.MESH
HBM = pl.BlockSpec(memory_space=pl.ANY)
WEIGHT_AXES = (1, 0, 1, 0)


def _place():
    x, y, c = lax.axis_index("x"), lax.axis_index("y"), lax.axis_index("c")
    chips = [(1 - x, y), (x, 1 - y), (1 - x, 1 - y)]
    return x, y, c, chips


def _block_half(ref, shard_shape, axis, j, half):
    r, c = shard_shape
    hr = r // 2
    if axis == 0:
        return ref.at[pl.ds(pl.multiple_of(j * r + half * hr, 16), hr), :]
    return ref.at[pl.ds(pl.multiple_of(half * hr, 16), hr), pl.ds(pl.multiple_of(j * c, 128), c)]


class _Side:
    def __init__(self, arrays, out_shapes, sems, first, last):
        self.arrays, self.out_shapes, self.sems, self.first, self.last = arrays, out_shapes, sems, first, last


def _gather_side(shards, axes):
    n = len(shards)
    shapes = [s.shape for s in shards]

    def copies(ins, outs, sems):
        send1, recv1, send2, recv2, send0, recv0 = sems
        x, y, c, chips = _place()
        me = 2 * x + y
        sib = (x, y, 1 - c)
        near = ((1 - c) * (1 - x) + c * x, (1 - c) * y + c * (1 - y))
        far = ((1 - c) * x + c * (1 - x), (1 - c) * (1 - y) + c * y)
        out = []
        for a in range(n):
            r, cc = shapes[a]
            mine = (outs[a].at[pl.ds(pl.multiple_of(me * r, 16), r), :] if axes[a] == 0
                    else outs[a].at[:, pl.ds(pl.multiple_of(me * cc, 128), cc)])
            own = pltpu.make_async_remote_copy(
                src_ref=ins[a], dst_ref=mine, send_sem=send0.at[a], recv_sem=recv0.at[a],
                device_id=sib, device_id_type=MESH)
            src = ins[a].at[pl.ds(pl.multiple_of(c * (r // 2), 16), r // 2), :]
            sends = [pltpu.make_async_remote_copy(
                src_ref=src, dst_ref=_block_half(outs[a], shapes[a], axes[a], me, c),
                send_sem=send1.at[a, k], recv_sem=recv1.at[a, k], device_id=(*chips[k], c), device_id_type=MESH)
                for k in range(2)]

            def region(chip, half):
                return _block_half(outs[a], shapes[a], axes[a], 2 * chip[0] + chip[1], half)

            def arrival(chip, k):
                reg = region(chip, c)
                return pltpu.make_async_remote_copy(
                    src_ref=reg, dst_ref=reg, send_sem=send1.at[a, k], recv_sem=recv1.at[a, k],
                    device_id=(*chip, c), device_id_type=MESH)

            def to_sibling(chip, k):
                reg = region(chip, c)
                return pltpu.make_async_remote_copy(
                    src_ref=reg, dst_ref=reg, send_sem=send2.at[a, k], recv_sem=recv2.at[a, k],
                    device_id=sib, device_id_type=MESH)

            def from_sibling(chip, k):
                reg = region(chip, 1 - c)
                return pltpu.make_async_remote_copy(
                    src_ref=reg, dst_ref=reg, send_sem=send2.at[a, k], recv_sem=recv2.at[a, k],
                    device_id=sib, device_id_type=MESH)

            relay = pltpu.make_async_remote_copy(
                src_ref=region(near, c), dst_ref=region(near, c), send_sem=send1.at[a, 2], recv_sem=recv1.at[a, 2],
                device_id=(*far, c), device_id_type=MESH)
            hops = [(arrival(near, c), to_sibling(near, c)), (arrival(far, 1 - c), to_sibling(far, 1 - c)),
                    (arrival(chips[2], 2), to_sibling(chips[2], 2))]
            back = [from_sibling(chips[k], k) for k in range(3)]
            out.append((own, sends, relay, hops, back))
        return out

    def first(ins, outs, sems):
        for own, sends, _, _, _ in copies(ins, outs, sems):
            own.start()
            for cp in sends:
                cp.start()

    def last(ins, outs, sems):
        per_array = copies(ins, outs, sems)
        for step in range(3):
            for _, _, relay, hops, _ in per_array:
                arrived, onward = hops[step]
                arrived.wait_recv()
                if step == 0:
                    relay.start()
                onward.start()
        for own, sends, relay, hops, back in per_array:
            for cp in back:
                cp.wait_recv()
            for cp in sends + [relay] + [onward for _, onward in hops]:
                cp.wait_send()
            own.wait()

    full = [(4 * r, c) if ax == 0 else (r, 4 * c) for (r, c), ax in zip(shapes, axes)]
    sems = [pltpu.SemaphoreType.DMA((n, 3)), pltpu.SemaphoreType.DMA((n, 3)),
            pltpu.SemaphoreType.DMA((n, 3)), pltpu.SemaphoreType.DMA((n, 3)),
            pltpu.SemaphoreType.DMA((n,)), pltpu.SemaphoreType.DMA((n,))]
    return _Side(list(shards), [jax.ShapeDtypeStruct(f, BF) for f in full], sems, first, last)


def _as3d(g, shard_shape, axis):
    r, c = shard_shape
    return g.reshape(4, r, c) if axis == 0 else g.reshape(1, r, 4 * c)


def _half_rows(ref3, hr, half):
    return ref3.at[:, pl.ds(pl.multiple_of(half * hr, 16), hr), :]


def _rs_pair_exchange(g3s, name):
    n = len(g3s)

    def body(*refs):
        ins, outs = refs[:n], refs[n:2 * n]
        send, recv = refs[2 * n:]
        x, y, c, _ = _place()
        cps = []
        for a in range(n):
            hr = g3s[a].shape[1] // 2
            cp = pltpu.make_async_remote_copy(
                src_ref=_half_rows(ins[a], hr, 1 - c), dst_ref=outs[a],
                send_sem=send.at[a], recv_sem=recv.at[a], device_id=(x, y, 1 - c), device_id_type=MESH)
            cp.start()
            cps.append(cp)
        for cp in cps:
            cp.wait()

    return pl.pallas_call(
        body, name=name,
        in_specs=[HBM] * n, out_specs=[HBM] * n,
        out_shape=[jax.ShapeDtypeStruct((g.shape[0], g.shape[1] // 2, g.shape[2]), BF) for g in g3s],
        scratch_shapes=[pltpu.SemaphoreType.DMA((n,)), pltpu.SemaphoreType.DMA((n,))],
    )(*g3s)


def _pair_sum(g3, land, cidx, name):
    nb, r, w = g3.shape
    hr = r // 2
    tr = 64

    def body(c_ref, g_ref, l_ref, o_ref):
        o_ref[...] = (g_ref[...].astype(F32) + l_ref[...].astype(F32)).astype(BF)

    blk = (nb, tr, w)
    return pl.pallas_call(
        body, name=name,
        grid_spec=pltpu.PrefetchScalarGridSpec(
            num_scalar_prefetch=1, grid=(hr // tr,),
            in_specs=[pl.BlockSpec(blk, lambda i, c: (0, c[0] * (hr // tr) + i, 0)),
                      pl.BlockSpec(blk, lambda i, c: (0, i, 0))],
            out_specs=pl.BlockSpec(blk, lambda i, c: (0, i, 0))),
        out_shape=jax.ShapeDtypeStruct((nb, hr, w), BF),
        compiler_params=_params(("parallel",)),
    )(cidx, g3, land)


def _chip_exchange_side(p3s, shapes, axes):
    n = len(p3s)

    def copies(ins, outs, sems):
        send, recv = sems
        x, y, c, chips = _place()
        cps = []
        for a in range(n):
            r, cc = shapes[a]
            for k, (px, py) in enumerate(chips):
                j = 2 * px + py
                src = ins[a].at[j] if axes[a] == 0 else ins[a].at[0, :, pl.ds(pl.multiple_of(j * cc, 128), cc)]
                cps.append(pltpu.make_async_remote_copy(
                    src_ref=src, dst_ref=outs[a].at[k], send_sem=send.at[a, k], recv_sem=recv.at[a, k],
                    device_id=(px, py, c), device_id_type=MESH))
        return cps

    def first(ins, outs, sems):
        for cp in copies(ins, outs, sems):
            cp.start()

    def last(ins, outs, sems):
        for cp in copies(ins, outs, sems):
            cp.wait()

    return _Side(list(p3s), [jax.ShapeDtypeStruct((3, r // 2, c), BF) for r, c in shapes],
                 [pltpu.SemaphoreType.DMA((n, 3)), pltpu.SemaphoreType.DMA((n, 3))], first, last)


def _chip_sum(p3, land, shard_shape, axis, idx, name):
    r, c = shard_shape
    hr = r // 2
    tr = 64
    nt = hr // tr

    def body(idx_ref, p_ref, l_ref, o_ref):
        acc = p_ref[...].astype(F32)
        for k in range(3):
            acc = acc + l_ref[k].astype(F32)
        o_ref[...] = acc

    own = (pl.BlockSpec((None, tr, c), lambda i, idx: (idx[0], i, 0)) if axis == 0
           else pl.BlockSpec((None, tr, c), lambda i, idx: (0, i, idx[0])))
    return pl.pallas_call(
        body, name=name,
        grid_spec=pltpu.PrefetchScalarGridSpec(
            num_scalar_prefetch=1, grid=(nt,),
            in_specs=[own, pl.BlockSpec((3, tr, c), lambda i, idx: (0, i, 0))],
            out_specs=pl.BlockSpec((tr, c), lambda i, idx: (idx[1] * nt + i, 0))),
        out_shape=jax.ShapeDtypeStruct((r, c), F32),
        compiler_params=_params(("parallel",)),
    )(idx, p3, land)


def _rs_pair_gather(fulls):
    n = len(fulls)

    def body(*refs):
        ins, outs = refs[:n], refs[n:2 * n]
        send, recv = refs[2 * n:]
        x, y, c, _ = _place()
        cps = []
        for a in range(n):
            hr = fulls[a].shape[0] // 2
            rows = pl.ds(pl.multiple_of(c * hr, 8), hr)
            cp = pltpu.make_async_remote_copy(
                src_ref=ins[a].at[rows, :], dst_ref=outs[a].at[rows, :], send_sem=send.at[a], recv_sem=recv.at[a],
                device_id=(x, y, 1 - c), device_id_type=MESH)
            cp.start()
            cps.append(cp)
        for a, cp in enumerate(cps):
            cp.wait_send()
            hr = fulls[a].shape[0] // 2
            other = pl.ds(pl.multiple_of((1 - c) * hr, 8), hr)
            pltpu.make_async_remote_copy(
                src_ref=ins[a].at[other, :], dst_ref=outs[a].at[other, :], send_sem=send.at[a], recv_sem=recv.at[a],
                device_id=(x, y, 1 - c), device_id_type=MESH).wait_recv()

    return pl.pallas_call(
        body, name="grads_pair_gather",
        in_specs=[HBM] * n, out_specs=[HBM] * n,
        out_shape=[jax.ShapeDtypeStruct(f.shape, F32) for f in fulls],
        input_output_aliases={a: a for a in range(n)},
        scratch_shapes=[pltpu.SemaphoreType.DMA((n,)), pltpu.SemaphoreType.DMA((n,))],
    )(*fulls)


def _sibling_send_side(arr):
    def copy(ins, outs, sems):
        x, y, c, _ = _place()
        return pltpu.make_async_remote_copy(
            src_ref=ins[0], dst_ref=outs[0], send_sem=sems[0].at[0], recv_sem=sems[1].at[0],
            device_id=(x, y, 1 - c), device_id_type=MESH)

    return _Side([arr], [jax.ShapeDtypeStruct(arr.shape, arr.dtype)],
                 [pltpu.SemaphoreType.DMA((1,)), pltpu.SemaphoreType.DMA((1,))],
                 lambda ins, outs, sems: copy(ins, outs, sems).start(),
                 lambda ins, outs, sems: copy(ins, outs, sems).wait())


def _add_bf16(a, b, name):
    r, c = a.shape
    tr = 64

    def body(a_ref, b_ref, o_ref):
        o_ref[...] = (a_ref[...].astype(F32) + b_ref[...].astype(F32)).astype(BF)

    blk = pl.BlockSpec((tr, c), lambda i: (i, 0))
    return pl.pallas_call(
        body, name=name, grid=(r // tr,), in_specs=[blk, blk], out_specs=blk,
        out_shape=jax.ShapeDtypeStruct((r, c), BF), compiler_params=_params(("parallel",)),
    )(a, b)


def _rs_partials(grads, shapes, axes, tag):
    cidx = jnp.reshape(lax.axis_index("c"), (1,)).astype(jnp.int32)
    g3s = [_as3d(g, s, ax) for g, s, ax in zip(grads, shapes, axes)]
    lands = _rs_pair_exchange(g3s, f"grads_pair_exchange_{tag}")
    return [_pair_sum(g3, l, cidx, f"pair_sum_{tag}_{a}") for a, (g3, l) in enumerate(zip(g3s, lands))]


def _rs_finish(p3s, landed, shapes, axes):
    x, y, c = lax.axis_index("x"), lax.axis_index("y"), lax.axis_index("c")
    idx = jnp.stack([2 * x + y, c]).astype(jnp.int32)
    fulls = [_chip_sum(p3, l2, s, ax, idx, f"chip_sum_{a}")
             for a, (p3, l2, s, ax) in enumerate(zip(p3s, landed, shapes, axes))]
    return _rs_pair_gather(fulls)


NSMALL = 8


def _small_all_reduce(g_nw, dlb, g_hnw, g_wf, loss_vec):
    def body(nw_ref, lb_ref, hn_ref, wf_ref, ls_ref, out_ref, pack_ref, buf_ref, send, recv):
        x, y, c = lax.axis_index("x"), lax.axis_index("y"), lax.axis_index("c")
        me = 4 * x + 2 * y + c
        pack_ref[...] = jnp.zeros_like(pack_ref)
        pack_ref[0:1, :] = nw_ref[...]
        pack_ref[1:2, :] = lb_ref[...]
        pack_ref[2:3, 0:HK] = hn_ref[...]
        pack_ref[3:4, :] = wf_ref[...]
        pack_ref[4:5, :] = ls_ref[...]
        buf_ref[me] = pack_ref[...]
        cps = []
        for d in range(1, 8):
            dx, dy, dc = d >> 2, (d >> 1) & 1, d & 1
            peer = (1 - x if dx else x, 1 - y if dy else y, 1 - c if dc else c)
            cp = pltpu.make_async_remote_copy(
                src_ref=pack_ref, dst_ref=buf_ref.at[me], send_sem=send.at[d - 1], recv_sem=recv.at[d - 1],
                device_id=peer, device_id_type=MESH)
            cp.start()
            cps.append(cp)
        for d in range(1, 8):
            dx, dy, dc = d >> 2, (d >> 1) & 1, d & 1
            src = 4 * (1 - x if dx else x) + 2 * (1 - y if dy else y) + (1 - c if dc else c)
            pltpu.make_async_remote_copy(
                src_ref=pack_ref, dst_ref=buf_ref.at[src], send_sem=send.at[d - 1], recv_sem=recv.at[d - 1],
                device_id=(x, y, c), device_id_type=MESH).wait_recv()
        for cp in cps:
            cp.wait_send()
        acc = buf_ref[0]
        for i in range(1, 8):
            acc = acc + buf_ref[i]
        out_ref[...] = acc

    vm = pl.BlockSpec(memory_space=pltpu.VMEM)
    return pl.pallas_call(
        body, name="small_all_reduce",
        in_specs=[vm] * 5, out_specs=vm,
        out_shape=jax.ShapeDtypeStruct((NSMALL, D), F32),
        scratch_shapes=[pltpu.VMEM((NSMALL, D), F32), pltpu.VMEM((8, NSMALL, D), F32),
                        pltpu.SemaphoreType.DMA((7,)), pltpu.SemaphoreType.DMA((7,))],
    )(g_nw, dlb, g_hnw, g_wf, loss_vec)


def _adamw_math(w, g, m, v):
    m = B1 * m + (1.0 - B1) * g
    v = B2 * v + (1.0 - B2) * (g * g)
    m_hat = m / (1.0 - B1 ** STEP)
    v_hat = v / (1.0 - B2 ** STEP)
    return -LR * (m_hat / (jnp.sqrt(v_hat) + ADAM_EPS) + WD * w), m, v


def _adamw(w, g, m, v, name):
    r, c = w.shape
    tr = 64

    def body(w_ref, g_ref, m_ref, v_ref, d_ref, nm_ref, nv_ref, go_ref):
        g = g_ref[...]
        d_ref[...], nm_ref[...], nv_ref[...] = _adamw_math(w_ref[...], g, m_ref[...], v_ref[...])
        go_ref[...] = g

    blk = pl.BlockSpec((tr, c), lambda i: (i, 0))
    return pl.pallas_call(
        body, name=name, grid=(r // tr,), in_specs=[blk] * 4, out_specs=[blk] * 4,
        out_shape=[jax.ShapeDtypeStruct((r, c), F32)] * 4,
        compiler_params=_params(("parallel",)),
    )(w, g, m, v)


def _small_update(red, lbl, params):
    def body(red_ref, *refs):
        ins, outs = refs[:12], refs[12:]
        lb = _lower_bound(ins[3][...])
        dl0 = red_ref[1:2, :] * lb * (1.0 - lb)
        row = lax.broadcasted_iota(jnp.int32, (2, D), 0)
        grads = [red_ref[0:1, :], jnp.where(row == 0, dl0, -dl0), red_ref[2:3, 0:HK], red_ref[3:4, :]]
        for i, g in enumerate(grads):
            w, m, v = ins[3 * i][...], ins[3 * i + 1][...], ins[3 * i + 2][...]
            d, nm, nv = _adamw_math(w, g, m, v)
            outs[4 * i][...] = g
            outs[4 * i + 1][...] = d
            outs[4 * i + 2][...] = nm
            outs[4 * i + 3][...] = nv
        outs[16][...] = jnp.sum(red_ref[4:5, :], axis=1, keepdims=True)

    flat = [a for p in params for a in p]
    vm = pl.BlockSpec(memory_space=pltpu.VMEM)
    shapes = [jax.ShapeDtypeStruct(p[0].shape, F32) for p in params for _ in range(4)]
    return pl.pallas_call(
        body, name="small_update",
        in_specs=[vm] * 13, out_specs=[vm] * 17,
        out_shape=shapes + [jax.ShapeDtypeStruct((1, 1), F32)],
    )(red, *flat)


def kernel(x, positions, norm_w, w_in, lb_logits, hgrn_norm_w, w_branch_a, w_branch_b, w_out, final_norm_w, loss_target, m_norm_w, m_w_in, m_lb_logits, m_hgrn_norm_w, m_w_branch_a, m_w_branch_b, m_w_out, m_final_norm_w, v_norm_w, v_w_in, v_lb_logits, v_hgrn_norm_w, v_w_branch_a, v_w_branch_b, v_w_out, v_final_norm_w):
    big_w = [w_in[0], w_branch_a[0], w_branch_b[0], w_out[0]]
    big_m = [m_w_in[0], m_w_branch_a[0], m_w_branch_b[0], m_w_out[0]]
    big_v = [v_w_in[0], v_w_branch_a[0], v_w_branch_b[0], v_w_out[0]]
    shapes = [w.shape for w in big_w]
    wf = final_norm_w.reshape(1, D)

    shards = [w.astype(BF) for w in big_w]
    loc = _local_step(x[0], positions.reshape(T, 1), norm_w, lb_logits, hgrn_norm_w, wf, loss_target[0],
                      *shards, shard_shapes=shapes)
    g_big = [loc["g_win"], loc["g_wa"], loc["g_wb"], loc["g_wout"]]
    red = _small_all_reduce(loc["g_nw"], loc["dlb"], loc["g_hnw"], loc["g_wf"], loc["loss_vec"])

    small = _small_update(red, lb_logits, [
        (norm_w, m_norm_w, v_norm_w), (lb_logits, m_lb_logits, v_lb_logits),
        (hgrn_norm_w, m_hgrn_norm_w, v_hgrn_norm_w),
        (wf, m_final_norm_w.reshape(1, D), v_final_norm_w.reshape(1, D))])
    loss = small[16].reshape(())
    sg, sd, sm, sv = ([small[4 * i + j] for i in range(4)] for j in range(4))
    for lst in (sg, sd, sm, sv):
        lst[3] = lst[3].reshape(D)
    upd = [_adamw(w, g, m, v, f"adamw_{a}") for a, (w, g, m, v) in enumerate(zip(big_w, g_big, big_m, big_v))]
    bd, bm, bv, bg = ([u[j][None] for u in upd] for j in range(4))

    def order(s, b):
        return [s[0], b[0], s[1], s[2], b[1], b[2], b[3], s[3]]

    return (loss, loc["gx"][None], *order(sg, bg), *order(sd, bd), *order(sm, bm), *order(sv, bv))
```

```python
import functools

import jax
import jax.numpy as jnp
from jax import lax
from jax.experimental import pallas as pl
from jax.experimental.pallas import tpu as pltpu

T = 2048
D = 1024
NIN = 11264
HEADS = 8
HK = 128
CH = 16
NCH = T // CH
HSTEP = 2
ATT_GROUPS = ((128, 1), (512, 4), (2048, 16))
ATT_COL0 = 4096
AG_COL0 = 8704
GATE_COL0 = 9216
EPS = 1e-6
ROPE_THETA = 10000.0
LR, B1, B2, ADAM_EPS, WD, STEP = 0.001, 0.9, 0.999, 1e-08, 0.01, 10

F32 = jnp.float32
BF = jnp.bfloat16
VMEM_LIMIT = 56 * 1024 * 1024

_NN = (((1,), (0,)), ((), ()))
_NT = (((1,), (1,)), ((), ()))
_TN = (((0,), (0,)), ((), ()))


def _dot(a, b, dims=_NN):
    return lax.dot_general(a, b, dims, preferred_element_type=F32)


def _bdot(a, b, dims=_NN):
    return lax.dot_general(a.astype(BF), b.astype(BF), dims, preferred_element_type=F32)


def _sigmoid(x):
    return jax.nn.sigmoid(x)


def _params(sem=None):
    return pltpu.CompilerParams(dimension_semantics=sem, vmem_limit_bytes=VMEM_LIMIT)


def _matmul(a, b, *, ta=False, tb=False, out_dtype=F32, tm=512, tn=512, tk=None, name, side=None):
    m = a.shape[1] if ta else a.shape[0]
    kdim = a.shape[0] if ta else a.shape[1]
    n = b.shape[0] if tb else b.shape[1]
    tk = tk or kdim
    tm, tn = min(tm, m), min(tn, n)
    nm, nn, nk = m // tm, n // tn, kdim // tk
    dims = (((0 if ta else 1,), (1 if tb else 0,)), ((), ()))
    s_arrays, s_in_specs, s_shapes, s_out_specs, s_sems = _side_io(side)
    na, no = len(s_arrays), len(s_shapes)
    nacc = 1 if nk > 1 else 0

    def body(*refs):
        a_ref, b_ref = refs[:2]
        s_ins, o_ref, s_outs = refs[2:2 + na], refs[2 + na], refs[3 + na:3 + na + no]
        scratch = refs[3 + na + no:]
        s_sem_refs = scratch[nacc:]
        i, j, k = pl.program_id(0), pl.program_id(1), pl.program_id(2)
        if side is not None:
            @pl.when((i == 0) & (j == 0) & (k == 0))
            def _():
                side.first(s_ins, s_outs, s_sem_refs)

        prod = _bdot(a_ref[...], b_ref[...], dims)
        if nk == 1:
            o_ref[...] = prod.astype(out_dtype)
        else:
            acc = scratch[0]

            @pl.when(k == 0)
            def _():
                acc[...] = prod

            @pl.when(k > 0)
            def _():
                acc[...] += prod

            @pl.when(k == nk - 1)
            def _():
                o_ref[...] = acc[...].astype(out_dtype)

        if side is not None:
            @pl.when((i == nm - 1) & (j == nn - 1) & (k == nk - 1))
            def _():
                side.last(s_ins, s_outs, s_sem_refs)

    a_spec = pl.BlockSpec((tk, tm), lambda i, j, k: (k, i)) if ta else pl.BlockSpec((tm, tk), lambda i, j, k: (i, k))
    b_spec = pl.BlockSpec((tn, tk), lambda i, j, k: (j, k)) if tb else pl.BlockSpec((tk, tn), lambda i, j, k: (k, j))
    sem = ("parallel", "parallel", "arbitrary") if side is None else ("arbitrary",) * 3
    out = pl.pallas_call(
        body, name=name, grid=(nm, nn, nk),
        in_specs=[a_spec, b_spec] + s_in_specs,
        out_specs=[pl.BlockSpec((tm, tn), lambda i, j, k: (i, j))] + s_out_specs,
        out_shape=[jax.ShapeDtypeStruct((m, n), out_dtype)] + s_shapes,
        scratch_shapes=([pltpu.VMEM((tm, tn), F32)] if nk > 1 else []) + s_sems,
        compiler_params=_params(sem),
    )(a, b, *s_arrays)
    return out[0] if side is None else (out[0], out[1:])


DZ_TILE = 512


def _part_offsets(parts):
    counts = [p.shape[1] // DZ_TILE for p in parts]
    offs = [sum(counts[:i]) for i in range(len(parts))]
    return counts, offs


def _part_spec(rows, cnt, off, tile_axis):
    def index(*g):
        return (0 if rows is None else g[0], jnp.clip(g[tile_axis] - off, 0, cnt - 1))
    return index


def _grad_w_in(h, parts):
    counts, offs = _part_offsets(parts)
    n = len(parts)

    def body(h_ref, *refs):
        o_ref = refs[n]
        j = pl.program_id(0)
        for p_ref, cnt, off in zip(refs[:n], counts, offs):
            @pl.when((j >= off) & (j < off + cnt))
            def _(p_ref=p_ref):
                o_ref[...] = _bdot(h_ref[...], p_ref[...], _TN).astype(BF)

    return pl.pallas_call(
        body, name="g_win", grid=(sum(counts),),
        in_specs=[pl.BlockSpec((T, D), lambda j: (0, 0))] +
                 [pl.BlockSpec((T, DZ_TILE), _part_spec(None, c, o, 0)) for c, o in zip(counts, offs)],
        out_specs=pl.BlockSpec((D, DZ_TILE), lambda j: (0, j)),
        out_shape=jax.ShapeDtypeStruct((D, NIN), BF),
        compiler_params=_params(("parallel",)),
    )(h, *parts)


def _grad_w_in_half(h, parts, half_idx, side=None):
    counts, offs = _part_offsets(parts)
    n = len(parts)
    nj = sum(counts)
    s_arrays, s_in_specs, s_shapes, s_out_specs, s_sems = _side_io(side)
    na, no = len(s_arrays), len(s_shapes)

    def body(idx_ref, h_ref, *refs):
        s_ins, o_ref, s_outs, s_sem_refs = refs[n:n + na], refs[n + na], refs[n + na + 1:n + na + 1 + no], refs[n + na + 1 + no:]
        j = pl.program_id(0)
        if side is not None:
            @pl.when(j == 0)
            def _():
                side.first(s_ins, s_outs, s_sem_refs)

        for p_ref, cnt, off in zip(refs[:n], counts, offs):
            @pl.when((j >= off) & (j < off + cnt))
            def _(p_ref=p_ref):
                o_ref[...] = _bdot(h_ref[...], p_ref[...], _TN).astype(BF)

        if side is not None:
            @pl.when(j == nj - 1)
            def _():
                side.last(s_ins, s_outs, s_sem_refs)

    def part_spec(cnt, off):
        return pl.BlockSpec((T, DZ_TILE), lambda j, idx: (0, jnp.clip(j - off, 0, cnt - 1)))

    out = pl.pallas_call(
        body, name="g_win_half" if side is None else "g_win_half_carrying",
        grid_spec=pltpu.PrefetchScalarGridSpec(
            num_scalar_prefetch=1, grid=(nj,),
            in_specs=[pl.BlockSpec((T, D // 2), lambda j, idx: (0, idx[0]))] +
                     [part_spec(c, o) for c, o in zip(counts, offs)] + s_in_specs,
            out_specs=[pl.BlockSpec((D // 2, DZ_TILE), lambda j, idx: (0, j))] + s_out_specs,
            scratch_shapes=s_sems),
        out_shape=[jax.ShapeDtypeStruct((D // 2, NIN), BF)] + s_shapes,
        compiler_params=_params(("parallel",) if side is None else ("arbitrary",)),
    )(half_idx, h, *parts, *s_arrays)
    return out[0] if side is None else (out[0], out[1:])


def _side_io(side):
    if side is None:
        return [], [], [], [], []
    return (side.arrays, [HBM] * len(side.arrays), side.out_shapes, [HBM] * len(side.out_shapes), side.sems)


def _grad_x(parts, w_in, x, dout, norm_w, side=None):
    counts, offs = _part_offsets(parts)
    n = len(parts)
    tm = 1024
    nm, nk = T // tm, sum(counts)
    s_arrays, s_in_specs, s_shapes, s_out_specs, s_sems = _side_io(side)
    na, no = len(s_arrays), len(s_shapes)

    def body(*refs):
        w_ref, x_ref, dout_ref, nw_ref = refs[n:n + 4]
        s_ins = refs[n + 4:n + 4 + na]
        gx_ref, gw_ref = refs[n + 4 + na:n + 6 + na]
        s_outs = refs[n + 6 + na:n + 6 + na + no]
        acc = refs[n + 6 + na + no]
        s_sem_refs = refs[n + 7 + na + no:]
        i, k = pl.program_id(0), pl.program_id(1)

        @pl.when((i == 0) & (k == 0))
        def _():
            gw_ref[...] = jnp.zeros_like(gw_ref)
            if side is not None:
                side.first(s_ins, s_outs, s_sem_refs)

        @pl.when(k == 0)
        def _():
            acc[...] = jnp.zeros_like(acc)

        for p_ref, cnt, off in zip(refs[:n], counts, offs):
            @pl.when((k >= off) & (k < off + cnt))
            def _(p_ref=p_ref):
                acc[...] += _bdot(p_ref[...], w_ref[...], _NT)

        @pl.when(k == nk - 1)
        def _():
            gw = jnp.zeros((1, D), F32)
            for c in range(tm // BLK):
                rows = pl.ds(BLK * c, BLK)
                xv, dhv = x_ref[rows, :], acc[rows, :]
                r = lax.rsqrt(jnp.mean(xv * xv, axis=-1, keepdims=True) + EPS)
                nrm = xv * r
                dn = dhv * nw_ref[...]
                gw = gw + jnp.sum(dhv * nrm, axis=0, keepdims=True)
                gx_ref[rows, :] = dout_ref[rows, :] + r * (dn - nrm * jnp.mean(dn * nrm, axis=-1, keepdims=True))
            gw_ref[...] += gw

        if side is not None:
            @pl.when((i == nm - 1) & (k == nk - 1))
            def _():
                side.last(s_ins, s_outs, s_sem_refs)

    row = pl.BlockSpec((tm, D), lambda i, k: (i, 0))
    vec = pl.BlockSpec((1, D), lambda i, k: (0, 0))
    out = pl.pallas_call(
        body, name="grad_x", grid=(nm, nk),
        in_specs=[pl.BlockSpec((tm, DZ_TILE), _part_spec(0, c, o, 1)) for c, o in zip(counts, offs)] +
                 [pl.BlockSpec((D, DZ_TILE), lambda i, k: (0, k)), row, row, vec] + s_in_specs,
        out_specs=[row, vec] + s_out_specs,
        out_shape=[jax.ShapeDtypeStruct((T, D), F32), jax.ShapeDtypeStruct((1, D), F32)] + s_shapes,
        scratch_shapes=[pltpu.VMEM((tm, D), F32)] + s_sems,
        compiler_params=_params(("arbitrary", "arbitrary")),
    )(*parts, w_in, x, dout, norm_w, *s_arrays)
    return out[0], out[1], out[2:]


def _norm_and_rope_tables(x, w, pos, invf, side=None, own=None):
    tm = 256
    nm = T // tm
    s_arrays, s_in_specs, s_shapes, s_out_specs, s_sems = _side_io(side)
    na, no = len(s_arrays), len(s_shapes)
    nz = 0 if own is None else 1
    wsh, blk = own if own is not None else (None, jnp.zeros((1,), jnp.int32))

    def body(blk_ref, *refs):
        x_ref, w_ref, pos_ref, invf_ref = refs[:4]
        s_ins = refs[4 + nz:4 + nz + na]
        h_ref, cos_ref, sa_ref, sb_ref = refs[4 + nz + na:8 + nz + na]
        s_outs = refs[8 + 2 * nz + na:8 + 2 * nz + na + no]
        s_sem_refs = refs[8 + 2 * nz + na + no:]
        if side is not None:
            @pl.when(pl.program_id(0) == 0)
            def _():
                side.first(s_ins, s_outs, s_sem_refs)

        xv = x_ref[...]
        r = lax.rsqrt(jnp.mean(xv * xv, axis=-1, keepdims=True) + EPS)
        h = (xv * r * w_ref[...]).astype(BF)
        h_ref[...] = h
        if own is not None:
            refs[8 + nz + na][...] = _dot(h, refs[4][...])
        first = (lax.broadcasted_iota(jnp.int32, (tm, 128), 1) % 64) < 32
        ang = pos_ref[...].astype(F32) * invf_ref[...]
        s = jnp.sin(ang)
        cos_ref[...] = jnp.cos(ang)
        sa_ref[...] = jnp.where(first, -s, 0.0)
        sb_ref[...] = jnp.where(first, 0.0, s)
        if side is not None:
            @pl.when(pl.program_id(0) == nm - 1)
            def _():
                side.last(s_ins, s_outs, s_sem_refs)

    tab = pl.BlockSpec((tm, 128), lambda i, b: (i, 0))
    own_in = [] if own is None else [pl.BlockSpec(wsh.shape, lambda i, b: (0, 0))]
    own_out = [] if own is None else [pl.BlockSpec((tm, wsh.shape[1]), lambda i, b: (i, b[0]))]
    own_shape = [] if own is None else [jax.ShapeDtypeStruct((T, NIN), F32)]
    out = pl.pallas_call(
        body, name="norm_and_rope_tables",
        grid_spec=pltpu.PrefetchScalarGridSpec(
            num_scalar_prefetch=1, grid=(nm,),
            in_specs=[pl.BlockSpec((tm, D), lambda i, b: (i, 0)), pl.BlockSpec((1, D), lambda i, b: (0, 0)),
                      pl.BlockSpec((tm, 1), lambda i, b: (i, 0)), pl.BlockSpec((1, 128), lambda i, b: (0, 0))]
                     + own_in + s_in_specs,
            out_specs=[pl.BlockSpec((tm, D), lambda i, b: (i, 0)), tab, tab, tab] + own_out + s_out_specs,
            scratch_shapes=s_sems),
        out_shape=[jax.ShapeDtypeStruct((T, D), BF)] + [jax.ShapeDtypeStruct((T, 128), F32)] * 3 + own_shape + s_shapes,
        compiler_params=_params(("parallel",) if side is None else ("arbitrary",)),
    )(blk, x, w, pos, invf, *([] if own is None else [wsh]), *s_arrays)
    return out[0], out[1], out[2], out[3], (out[4] if own is not None else None), out[4 + nz:]


def _z_rest(h, w_in, z_own, blk, side):
    tm, tn = 1024, NIN // 8
    s_arrays, s_in_specs, s_shapes, s_out_specs, s_sems = _side_io(side)
    na, no = len(s_arrays), len(s_shapes)
    nm, ns = T // tm, 6

    def col(i, s, b):
        return (0, ((b[0] + 1 + s // 2) % 4) * 2 + s % 2)

    def body(blk_ref, h_ref, w_ref, zin_ref, *refs):
        s_ins, o_ref, s_outs, s_sem_refs = refs[:na], refs[na], refs[na + 1:na + 1 + no], refs[na + 1 + no:]
        i, s = pl.program_id(0), pl.program_id(1)

        @pl.when((i == 0) & (s == 0))
        def _():
            side.first(s_ins, s_outs, s_sem_refs)

        o_ref[...] = _dot(h_ref[...], w_ref[...])

        @pl.when((i == nm - 1) & (s == ns - 1))
        def _():
            side.last(s_ins, s_outs, s_sem_refs)

    out = pl.pallas_call(
        body, name="z_proj",
        grid_spec=pltpu.PrefetchScalarGridSpec(
            num_scalar_prefetch=1, grid=(nm, ns),
            in_specs=[pl.BlockSpec((tm, D), lambda i, s, b: (i, 0)), pl.BlockSpec((D, tn), col), HBM] + s_in_specs,
            out_specs=[pl.BlockSpec((tm, tn), lambda i, s, b: (i, col(i, s, b)[1]))] + s_out_specs,
            scratch_shapes=s_sems),
        out_shape=[jax.ShapeDtypeStruct((T, NIN), F32)] + s_shapes,
        input_output_aliases={3: 0},
        compiler_params=_params(("arbitrary", "arbitrary")),
    )(blk, h, w_in, z_own, *s_arrays)
    return out[0], out[1:]


def _lower_bound(lbl):
    mx = jnp.max(lbl, axis=0, keepdims=True)
    e = jnp.exp(lbl - mx)
    return e[0:1] / jnp.sum(e, axis=0, keepdims=True)


def _cumsum_rows(g, rows):
    b = g
    sh = 1
    while sh < CH:
        b = b + jnp.where(rows >= sh, pltpu.roll(b, sh, axis=0), 0.0)
        sh *= 2
    return b


def _rev_cumsum_rows(g, rows):
    b = g
    sh = 1
    while sh < CH:
        b = b + jnp.where(rows < CH - sh, pltpu.roll(b, CH - sh, axis=0), 0.0)
        sh *= 2
    return b


SUB = CH // 2


def _direct_block(qb, kb, vb, bb, rows8):
    ob = jnp.zeros_like(qb)
    for s in range(SUB):
        e_s = jnp.exp(jnp.where(rows8 >= s, bb - bb[s:s + 1], -jnp.inf))
        ob = ob + jnp.sum(qb * e_s * kb[s:s + 1], axis=1, keepdims=True) * vb[s:s + 1]
    return ob


def _direct_block_bwd(qb, kb, vb, bb, dob, rows8, rowc8):
    dq = dk = dv = db = jnp.zeros_like(qb)
    for s in range(SUB):
        one = (rowc8 == s).astype(F32)
        ks, vs = kb[s:s + 1], vb[s:s + 1]
        e_s = jnp.exp(jnp.where(rows8 >= s, bb - bb[s:s + 1], -jnp.inf))
        qes = qb * e_s
        w = qes * ks
        a = jnp.sum(w, axis=1, keepdims=True)
        da = jnp.sum(dob * vs, axis=1, keepdims=True)
        dv = dv + one * jnp.sum(a * dob, axis=0, keepdims=True)
        dq = dq + da * e_s * ks
        dk = dk + one * jnp.sum(da * qes, axis=0, keepdims=True)
        u = da * w
        db = db + u - one * jnp.sum(u, axis=0, keepdims=True)
    return dq, dk, dv, db


def _cross_factors(q, k, b):
    ref = b[SUB - 1:SUB]
    e_hi, e_lo = jnp.exp(b[SUB:] - ref), jnp.exp(ref - b[:SUB])
    return q[SUB:] * e_hi, k[:SUB] * e_lo, e_hi, e_lo


def _intra_fwd(q, k, v, b, rows8):
    lo = _direct_block(q[:SUB], k[:SUB], v[:SUB], b[:SUB], rows8)
    hi = _direct_block(q[SUB:], k[SUB:], v[SUB:], b[SUB:], rows8)
    qe_hi, ke_lo, _, _ = _cross_factors(q, k, b)
    for s in range(SUB):
        hi = hi + jnp.sum(qe_hi * ke_lo[s:s + 1], axis=1, keepdims=True) * v[s:s + 1]
    return jnp.concatenate([lo, hi], axis=0)


def _intra_bwd(q, k, v, b, do, rows8, rowc8):
    dq_lo, dk_lo, dv_lo, db_lo = _direct_block_bwd(q[:SUB], k[:SUB], v[:SUB], b[:SUB], do[:SUB], rows8, rowc8)
    dq_hi, dk_hi, dv_hi, db_hi = _direct_block_bwd(q[SUB:], k[SUB:], v[SUB:], b[SUB:], do[SUB:], rows8, rowc8)
    qe_hi, ke_lo, e_hi, e_lo = _cross_factors(q, k, b)
    do_hi, v_lo = do[SUB:], v[:SUB]
    dqe = dke = jnp.zeros_like(qe_hi)
    for s in range(SUB):
        one = (rowc8 == s).astype(F32)
        a = jnp.sum(qe_hi * ke_lo[s:s + 1], axis=1, keepdims=True)
        da = jnp.sum(do_hi * v_lo[s:s + 1], axis=1, keepdims=True)
        dv_lo = dv_lo + one * jnp.sum(a * do_hi, axis=0, keepdims=True)
        dqe = dqe + da * ke_lo[s:s + 1]
        dke = dke + one * jnp.sum(da * qe_hi, axis=0, keepdims=True)
    u_hi, u_lo = dqe * qe_hi, dke * ke_lo
    d_ref = jnp.sum(u_lo, axis=0, keepdims=True) - jnp.sum(u_hi, axis=0, keepdims=True)
    db_lo = db_lo - u_lo + (rowc8 == SUB - 1).astype(F32) * d_ref
    cat = lambda lo, hi: jnp.concatenate([lo, hi], axis=0)
    return (cat(dq_lo, dq_hi + dqe * e_hi), cat(dk_lo + dke * e_lo, dk_hi), cat(dv_lo, dv_hi),
            cat(db_lo, db_hi + u_hi))


def _hgrn_fwd(z, lbl, nw):
    def body(hq_ref, hf_ref, hi_ref, hg_ref, lbl_ref, nw_ref, oraw_ref, og_ref, sh_ref, st_ref):
        @pl.when(pl.program_id(0) == 0)
        def _():
            st_ref[...] = jnp.zeros_like(st_ref)

        lb_all = _lower_bound(lbl_ref[...])
        rows = lax.broadcasted_iota(jnp.int32, (CH, HK), 0)
        rows8 = lax.broadcasted_iota(jnp.int32, (SUB, HK), 0)
        nwv = nw_ref[...]
        for cc, h in [(cc, h) for cc in range(HSTEP) for h in range(HEADS)]:
            rs = slice(CH * cc, CH * (cc + 1))
            sl = slice(HK * h, HK * (h + 1))
            lb = lb_all[:, sl]
            hq, hf, v, hg = hq_ref[rs, sl], hf_ref[rs, sl], hi_ref[rs, sl], hg_ref[rs, sl]
            q = hq * _sigmoid(hq)
            f = lb + (1.0 - lb) * _sigmoid(hf)
            k = 1.0 - f
            b = _cumsum_rows(jnp.log(f), rows)
            sh_ref[cc, h] = st_ref[h]
            o = _bdot(q * jnp.exp(b), st_ref[h], _NT) + _intra_fwd(q, k, v, b, rows8)
            bl = b[CH - 1:CH]
            st_ref[h] = st_ref[h] * jnp.exp(bl)
            st_ref[h] += _bdot(v, k * jnp.exp(bl - b), _TN)
            oraw_ref[rs, sl] = o
            nrm = o * lax.rsqrt(jnp.mean(o * o, axis=1, keepdims=True) + EPS)
            og_ref[rs, sl] = (nrm * nwv * (hg * _sigmoid(hg))).astype(BF)

    zblk = lambda c: pl.BlockSpec((CH * HSTEP, D), lambda i, c=c: (i, c))
    return pl.pallas_call(
        body, name="hgrn_fwd", grid=(NCH // HSTEP,),
        in_specs=[zblk(0), zblk(1), zblk(2), zblk(3),
                  pl.BlockSpec((2, D), lambda i: (0, 0)), pl.BlockSpec((1, HK), lambda i: (0, 0))],
        out_specs=[zblk(0), zblk(0),
                   pl.BlockSpec((HSTEP, HEADS, HK, HK), lambda i: (i, 0, 0, 0))],
        out_shape=[jax.ShapeDtypeStruct((T, D), F32), jax.ShapeDtypeStruct((T, D), BF),
                   jax.ShapeDtypeStruct((NCH, HEADS, HK, HK), F32)],
        scratch_shapes=[pltpu.VMEM((HEADS, HK, HK), F32)],
        compiler_params=_params(("arbitrary",)),
    )(z, z, z, z, lbl, nw)


def _hgrn_bwd(z, lbl, nw, oraw, dog, shist, side=None):
    hstep = 1
    s_arrays, s_in_specs, s_shapes, s_out_specs, s_sems = _side_io(side)
    na, no = len(s_arrays), len(s_shapes)

    def body(*refs):
        hq_ref, hf_ref, hi_ref, hg_ref, lbl_ref, nw_ref, oraw_ref, dog_ref, sh_ref = refs[:9]
        s_ins = refs[9:9 + na]
        dz_ref, dlb_ref, dnw_ref = refs[9 + na:12 + na]
        s_outs = refs[12 + na:12 + na + no]
        dst_ref = refs[12 + na + no]
        s_sem_refs = refs[13 + na + no:]

        @pl.when(pl.program_id(0) == 0)
        def _():
            dst_ref[...] = jnp.zeros_like(dst_ref)
            dlb_ref[...] = jnp.zeros_like(dlb_ref)
            dnw_ref[...] = jnp.zeros_like(dnw_ref)
            if side is not None:
                side.first(s_ins, s_outs, s_sem_refs)

        lb_all = _lower_bound(lbl_ref[...])
        rows = lax.broadcasted_iota(jnp.int32, (CH, HK), 0)
        rowc = lax.broadcasted_iota(jnp.int32, (CH, 1), 0)
        rows8 = lax.broadcasted_iota(jnp.int32, (SUB, HK), 0)
        rowc8 = lax.broadcasted_iota(jnp.int32, (SUB, 1), 0)
        nwv = nw_ref[...]
        dnw = jnp.zeros((1, HK), F32)
        for cc, h in [(cc, h) for cc in reversed(range(hstep)) for h in range(HEADS)]:
            rs = slice(CH * cc, CH * (cc + 1))
            sl = slice(HK * h, HK * (h + 1))
            lb = lb_all[:, sl]
            hq, hf, v, hg = hq_ref[rs, sl], hf_ref[rs, sl], hi_ref[rs, sl], hg_ref[rs, sl]
            o, dg_out = oraw_ref[rs, sl], dog_ref[rs, sl]
            sg = _sigmoid(hg)
            sil = hg * sg
            r = lax.rsqrt(jnp.mean(o * o, axis=1, keepdims=True) + EPS)
            nrm = o * r
            d_hg = dg_out * (nrm * nwv) * (sg * (1.0 + hg * (1.0 - sg)))
            dn = dg_out * nwv * sil
            dnw = dnw + jnp.sum(dg_out * nrm * sil, axis=0, keepdims=True)
            do = r * (dn - nrm * jnp.mean(dn * nrm, axis=1, keepdims=True))
            sq = _sigmoid(hq)
            q = hq * sq
            sig = _sigmoid(hf)
            f = lb + (1.0 - lb) * sig
            k = 1.0 - f
            b = _cumsum_rows(jnp.log(f), rows)
            eb = jnp.exp(b)
            qe = q * eb
            bl = b[CH - 1:CH]
            ebl = jnp.exp(bl)
            kdec = jnp.exp(bl - b)
            ke = k * kdec
            dqe = _bdot(do, sh_ref[cc, h])
            dq = dqe * eb
            db = dqe * qe
            dke = _bdot(v, dst_ref[h])
            dv = _bdot(ke, dst_ref[h], _NT)
            dk = dke * kdec
            rr = dke * ke
            db = db - rr
            db_last = (jnp.sum(rr, axis=0, keepdims=True)
                       + ebl * jnp.sum(dst_ref[h] * sh_ref[cc, h], axis=0, keepdims=True))
            dst_ref[h] = dst_ref[h] * ebl
            dst_ref[h] += _bdot(do, qe, _TN)
            dq_i, dk_i, dv_i, db_i = _intra_bwd(q, k, v, b, do, rows8, rowc8)
            dq, dk, dv = dq + dq_i, dk + dk_i, dv + dv_i
            db = db + db_i + (rowc == CH - 1).astype(F32) * db_last
            dgl = _rev_cumsum_rows(db, rows)
            df = dgl / f - dk
            dlb_ref[:, sl] += jnp.sum(df * (1.0 - sig), axis=0, keepdims=True)
            dz_ref[rs, sl] = (dq * (sq * (1.0 + hq * (1.0 - sq)))).astype(BF)
            dz_ref[rs, D + HK * h:D + HK * (h + 1)] = (df * (1.0 - lb) * sig * (1.0 - sig)).astype(BF)
            dz_ref[rs, 2 * D + HK * h:2 * D + HK * (h + 1)] = dv.astype(BF)
            dz_ref[rs, 3 * D + HK * h:3 * D + HK * (h + 1)] = d_hg.astype(BF)
        dnw_ref[...] += dnw
        if side is not None:
            @pl.when(pl.program_id(0) == NCH // hstep - 1)
            def _():
                side.last(s_ins, s_outs, s_sem_refs)

    rev = lambda i: NCH // hstep - 1 - i
    zblk = lambda c: pl.BlockSpec((CH * hstep, D), lambda i, c=c: (rev(i), c))
    out = pl.pallas_call(
        body, name="hgrn_bwd", grid=(NCH // hstep,),
        in_specs=[zblk(0), zblk(1), zblk(2), zblk(3),
                  pl.BlockSpec((2, D), lambda i: (0, 0)), pl.BlockSpec((1, HK), lambda i: (0, 0)),
                  zblk(0), zblk(0),
                  pl.BlockSpec((hstep, HEADS, HK, HK), lambda i: (rev(i), 0, 0, 0))] + s_in_specs,
        out_specs=[pl.BlockSpec((CH * hstep, 4 * D), lambda i: (rev(i), 0)),
                   pl.BlockSpec((1, D), lambda i: (0, 0)), pl.BlockSpec((1, HK), lambda i: (0, 0))] + s_out_specs,
        out_shape=[jax.ShapeDtypeStruct((T, 4 * D), BF), jax.ShapeDtypeStruct((1, D), F32),
                   jax.ShapeDtypeStruct((1, HK), F32)] + s_shapes,
        scratch_shapes=[pltpu.VMEM((HEADS, HK, HK), F32)] + s_sems,
        compiler_params=_params(("arbitrary",)),
    )(z, z, z, z, lbl, nw, oraw, dog, shist, *s_arrays)
    return out[0], out[1], out[2], out[3:]


BLK = 128
NBLK = T // BLK
QK_SCALE = 0.125


def _head_masks():
    lane = lax.broadcasted_iota(jnp.int32, (1, BLK), 1)
    return [(lane < 64).astype(F32), (lane >= 64).astype(F32)]


def _pieces(dil):
    m = T // dil
    out = []
    for r in range(dil):
        for j in range(m // BLK):
            start = r + dil * BLK * j
            rows = pl.ds(start, BLK, stride=dil) if dil > 1 else pl.ds(start, BLK)
            out.append((rows, r * m + BLK * j))
    return out


def _rope(x, c, sa, sb):
    return x * c + pltpu.roll(x, 96, axis=1) * sa + pltpu.roll(x, 32, axis=1) * sb


def _rope_t(d, c, sa, sb):
    return d * c + pltpu.roll(d * sa, 32, axis=1) + pltpu.roll(d * sb, 96, axis=1)


def _rope_and_regroup(dil, q_ref, k_ref, v_ref, tables, stage_q, stage_k, qr_ref, kr_ref, vr_ref):
    cos_ref, sa_ref, sb_ref = tables
    to_q, to_k = (qr_ref, kr_ref) if dil == 1 else (stage_q, stage_k)
    for c in range(T // BLK):
        rows = pl.ds(BLK * c, BLK)
        cs, sa, sb = cos_ref[rows, :], sa_ref[rows, :], sb_ref[rows, :]
        to_q[rows, :] = (_rope(q_ref[rows, :], cs, sa, sb) * QK_SCALE).astype(to_q.dtype)
        to_k[rows, :] = _rope(k_ref[rows, :], cs, sa, sb).astype(to_k.dtype)
    for rows, dst in _pieces(dil):
        drows = pl.ds(dst, BLK)
        if dil > 1:
            qr_ref[drows, :] = stage_q[rows, :].astype(qr_ref.dtype)
            kr_ref[drows, :] = stage_k[rows, :].astype(kr_ref.dtype)
        vr_ref[drows, :] = v_ref[rows, :].astype(vr_ref.dtype)


def _window_bias(bias_ref):
    ii = lax.broadcasted_iota(jnp.int32, (2 * BLK, BLK), 0) % BLK
    jj = lax.broadcasted_iota(jnp.int32, (2 * BLK, BLK), 1)
    bias_ref[0] = jnp.where(jj <= ii, 0.0, -jnp.inf)
    bias_ref[1] = jnp.where(jj >= ii, 0.0, -jnp.inf)


def _blocks(bi):
    if isinstance(bi, int):
        return pl.ds(bi * BLK, BLK), pl.ds(max(bi - 1, 0) * BLK, BLK)
    return (pl.ds(pl.multiple_of(bi * BLK, BLK), BLK),
            pl.ds(pl.multiple_of(jnp.maximum(bi - 1, 0) * BLK, BLK), BLK))


def _stack_heads(x, masks):
    return jnp.concatenate([x * masks[0].astype(x.dtype), x * masks[1].astype(x.dtype)], axis=0).astype(BF)


def _attn_fwd(z, cos, sa, sb):
    def body(q_ref, k_ref, v_ref, ag_ref, cos_ref, sa_ref, sb_ref, ob_ref, opre_ref, lse_ref, qr_ref, kr_ref, vr_ref,
             bias_ref, og_ref, lg_ref, otok_ref, ltok_ref, sc_ref):
        g = pl.program_id(1)
        masks = _head_masks()

        @pl.when(g == 0)
        def _():
            _window_bias(bias_ref)

        def group(gi):
            dil = ATT_GROUPS[gi][1]
            nblk = (T // dil) // BLK
            _rope_and_regroup(dil, q_ref, k_ref, v_ref, (cos_ref, sa_ref, sb_ref), lg_ref.at[0], lg_ref.at[1],
                              qr_ref, kr_ref, vr_ref)

            def scores(bi, slot):
                cur, prev = _blocks(bi)
                q2 = _stack_heads(qr_ref[cur, :], masks)
                sc_ref[slot, 0] = _dot(q2, kr_ref[cur, :], _NT) + bias_ref[0]
                if nblk > 1:
                    sc_ref[slot, 1] = (_dot(q2, kr_ref[prev, :], _NT)
                                       + (bias_ref[1] + jnp.where((bi % nblk) != 0, 0.0, -jnp.inf)))

            def finish(bi, slot):
                cur, prev = _blocks(bi)
                s_c, vc = sc_ref[slot, 0], vr_ref[cur, :]
                if nblk > 1:
                    s_p, vp = sc_ref[slot, 1], vr_ref[prev, :]
                    mx = jnp.max(jnp.maximum(s_c, s_p), axis=1, keepdims=True)
                    p_c, p_p = jnp.exp(s_c - mx), jnp.exp(s_p - mx)
                    den = jnp.sum(p_c + p_p, axis=1, keepdims=True)
                    oh = _dot(p_c.astype(BF), vc) + _dot(p_p.astype(BF), vp)
                else:
                    mx = jnp.max(s_c, axis=1, keepdims=True)
                    p_c = jnp.exp(s_c - mx)
                    den = jnp.sum(p_c, axis=1, keepdims=True)
                    oh = _dot(p_c.astype(BF), vc)
                on = oh / den
                lsev = jnp.broadcast_to(mx + jnp.log(den), (2 * BLK, BLK))
                og_ref[cur, :] = on[:BLK] * masks[0] + on[BLK:] * masks[1]
                lg_ref[0, cur, :] = lsev[:BLK]
                lg_ref[1, cur, :] = lsev[BLK:]

            def pair(j, carry):
                finish(2 * j, 0)
                scores(2 * j + 1, 1)
                finish(2 * j + 1, 1)
                scores(jnp.minimum(2 * j + 2, NBLK - 1), 0)
                return carry

            scores(0, 0)
            lax.fori_loop(0, NBLK // 2, pair, 0)
            for rows, src in _pieces(dil):
                srows = pl.ds(src, BLK)
                otok_ref[gi, rows, :] = og_ref[srows, :]
                ltok_ref[gi, 0, rows, :] = lg_ref[0, srows, :]
                ltok_ref[gi, 1, rows, :] = lg_ref[1, srows, :]

        for gi in range(3):
            pl.when(g == gi)(functools.partial(group, gi))

        @pl.when(g == 2)
        def _():
            for c in range(T // BLK):
                rows = pl.ds(BLK * c, BLK)
                wts = []
                for hh in range(2):
                    l0, l1, l2 = ltok_ref[0, hh, rows, :], ltok_ref[1, hh, rows, :], ltok_ref[2, hh, rows, :]
                    mx = jnp.maximum(jnp.maximum(l0, l1), l2)
                    e0, e1, e2 = jnp.exp(l0 - mx), jnp.exp(l1 - mx), jnp.exp(l2 - mx)
                    tot = e0 + e1 + e2
                    lse_ref[rows, BLK * hh:BLK * (hh + 1)] = mx + jnp.log(tot)
                    inv = 1.0 / tot
                    wts.append([e0 * inv, e1 * inv, e2 * inv])
                o = sum((wts[0][gi] * masks[0] + wts[1][gi] * masks[1]) * otok_ref[gi, rows, :] for gi in range(3))
                ag = ag_ref[rows, :]
                opre_ref[rows, :] = o
                ob_ref[rows, :] = (o * (ag * _sigmoid(ag))).astype(BF)

    c0 = ATT_COL0 // BLK
    zspec = lambda part: pl.BlockSpec((T, BLK), lambda p, g, part=part: (0, c0 + 12 * part + 4 * g + p))
    outspec = pl.BlockSpec((T, BLK), lambda p, g: (0, p))
    table = pl.BlockSpec((T, BLK), lambda p, g: (0, 0))
    regrouped = pl.BlockSpec((None, T, BLK), lambda p, g: (g, 0, p))
    big = lambda: pltpu.VMEM((T, BLK), F32)
    return pl.pallas_call(
        body, name="attn_fwd", grid=(4, 3),
        in_specs=[zspec(0), zspec(1), zspec(2),
                  pl.BlockSpec((T, BLK), lambda p, g: (0, AG_COL0 // BLK + p)), table, table, table],
        out_specs=[outspec, outspec, pl.BlockSpec((T, 2 * BLK), lambda p, g: (0, p)), regrouped, regrouped, regrouped],
        out_shape=[jax.ShapeDtypeStruct((T, 512), BF), jax.ShapeDtypeStruct((T, 512), F32),
                   jax.ShapeDtypeStruct((T, 8 * BLK), F32)] + [jax.ShapeDtypeStruct((3, T, 512), BF)] * 3,
        scratch_shapes=[pltpu.VMEM((2, 2 * BLK, BLK), F32), big(),
                        pltpu.VMEM((2, T, BLK), F32), pltpu.VMEM((3, T, BLK), F32), pltpu.VMEM((3, 2, T, BLK), F32),
                        pltpu.VMEM((2, 2, 2 * BLK, BLK), F32)],
        compiler_params=_params(("parallel", "arbitrary")),
    )(z, z, z, z, cos, sa, sb)


def _attn_bwd(z, qs, ks, vs, cos, sa, sb, opre, lse, dob):
    def body(qs_ref, ks_ref, vs_ref, ag_ref, cos_ref, sa_ref, sb_ref, o_ref, lse0_ref, lse1_ref, dob_ref,
             dq_ref, dk_ref, dv_ref, dag_ref,
             bias_ref, dtok_ref, qr_ref, kr_ref, vr_ref, dor_ref, lr_ref, dr_ref,
             dqr_ref, dkr_ref, dvr_ref, pd_ref, dotok_ref):
        g = pl.program_id(1)
        masks = _head_masks()

        @pl.when(g == 0)
        def _():
            _window_bias(bias_ref)
            for c in range(T // BLK):
                rows = pl.ds(BLK * c, BLK)
                ag, dob_v, o = ag_ref[rows, :], dob_ref[rows, :], o_ref[rows, :]
                sg = _sigmoid(ag)
                dag_ref[rows, :] = (dob_v * o * (sg * (1.0 + ag * (1.0 - sg)))).astype(BF)
                do = dob_v * (ag * sg)
                dotok_ref[rows, :] = do
                prod = do * o
                for hh, mh in enumerate(masks):
                    dtok_ref[hh, rows, :] = jnp.broadcast_to(jnp.sum(prod * mh, axis=1, keepdims=True), (BLK, BLK))

        def group(gi):
            dil = ATT_GROUPS[gi][1]
            nblk = (T // dil) // BLK
            for rows, dst in _pieces(dil):
                drows = pl.ds(dst, BLK)
                dor_ref[drows, :] = dotok_ref[rows, :]
                for hh, lse_ref in enumerate((lse0_ref, lse1_ref)):
                    lr_ref[hh, drows, :] = lse_ref[rows, :]
                    dr_ref[hh, drows, :] = dtok_ref[hh, rows, :]
            dkr_ref[...] = jnp.zeros_like(dkr_ref)
            dvr_ref[...] = jnp.zeros_like(dvr_ref)

            def probs(bi, slot):
                cur, prev = _blocks(bi)
                q2, do2 = _stack_heads(qs_ref[cur, :], masks), _stack_heads(dor_ref[cur, :], masks)
                lh = jnp.concatenate([lr_ref[0, cur, :], lr_ref[1, cur, :]], axis=0)
                dh = jnp.concatenate([dr_ref[0, cur, :], dr_ref[1, cur, :]], axis=0)
                p_c = jnp.exp(_dot(q2, ks_ref[cur, :], _NT) + bias_ref[0] - lh)
                pd_ref[slot, 0] = p_c.astype(BF)
                pd_ref[slot, 1] = (p_c * (_dot(do2, vs_ref[cur, :], _NT) - dh)).astype(BF)
                if nblk > 1:
                    bias_p = bias_ref[1] + jnp.where((bi % nblk) != 0, 0.0, -jnp.inf)
                    p_p = jnp.exp(_dot(q2, ks_ref[prev, :], _NT) + bias_p - lh)
                    pd_ref[slot, 2] = p_p.astype(BF)
                    pd_ref[slot, 3] = (p_p * (_dot(do2, vs_ref[prev, :], _NT) - dh)).astype(BF)

            def grads(bi, slot):
                cur, prev = _blocks(bi)
                q2, do2 = _stack_heads(qs_ref[cur, :], masks), _stack_heads(dor_ref[cur, :], masks)
                p_c, ds_c = pd_ref[slot, 0], pd_ref[slot, 1]
                dq2 = _dot(ds_c, ks_ref[cur, :])
                dkr_ref[cur, :] += _dot(ds_c, q2, _TN)
                dvr_ref[cur, :] += _dot(p_c, do2, _TN)
                if nblk > 1:
                    p_p, ds_p = pd_ref[slot, 2], pd_ref[slot, 3]
                    dq2 = dq2 + _dot(ds_p, ks_ref[prev, :])
                    dkr_ref[prev, :] += _dot(ds_p, q2, _TN)
                    dvr_ref[prev, :] += _dot(p_p, do2, _TN)
                dqr_ref[cur, :] = dq2[:BLK] * masks[0] + dq2[BLK:] * masks[1]

            def pair(j, carry):
                grads(2 * j, 0)
                probs(2 * j + 1, 1)
                grads(2 * j + 1, 1)
                probs(jnp.minimum(2 * j + 2, NBLK - 1), 0)
                return carry

            probs(0, 0)
            lax.fori_loop(0, NBLK // 2, pair, 0)
            if dil > 1:
                for rows, src in _pieces(dil):
                    srows = pl.ds(src, BLK)
                    qr_ref[rows, :] = dqr_ref[srows, :]
                    kr_ref[rows, :] = dkr_ref[srows, :]
                    vr_ref[rows, :] = dvr_ref[srows, :]
            tq, tk, tv = (qr_ref, kr_ref, vr_ref) if dil > 1 else (dqr_ref, dkr_ref, dvr_ref)
            for c in range(T // BLK):
                rows = pl.ds(BLK * c, BLK)
                cs, sa, sb = cos_ref[rows, :], sa_ref[rows, :], sb_ref[rows, :]
                dq_ref[rows, :] = _rope_t(tq[rows, :] * QK_SCALE, cs, sa, sb).astype(BF)
                dk_ref[rows, :] = _rope_t(tk[rows, :], cs, sa, sb).astype(BF)
                dv_ref[rows, :] = tv[rows, :].astype(BF)

        for gi in range(3):
            pl.when(g == gi)(functools.partial(group, gi))

    regrouped = pl.BlockSpec((None, T, BLK), lambda p, g: (g, 0, p))
    pspec = pl.BlockSpec((T, BLK), lambda p, g: (0, p))
    gspec = pl.BlockSpec((T, BLK), lambda p, g: (0, 4 * g + p))
    table = pl.BlockSpec((T, BLK), lambda p, g: (0, 0))
    big = lambda: pltpu.VMEM((T, BLK), F32)
    two = lambda: pltpu.VMEM((2, T, BLK), F32)
    return pl.pallas_call(
        body, name="attn_bwd", grid=(4, 3),
        in_specs=[regrouped, regrouped, regrouped,
                  pl.BlockSpec((T, BLK), lambda p, g: (0, AG_COL0 // BLK + p)), table, table, table,
                  pspec, pl.BlockSpec((T, BLK), lambda p, g: (0, 2 * p)),
                  pl.BlockSpec((T, BLK), lambda p, g: (0, 2 * p + 1)), pspec],
        out_specs=[gspec, gspec, gspec, pspec],
        out_shape=[jax.ShapeDtypeStruct((T, 1536), BF), jax.ShapeDtypeStruct((T, 1536), BF),
                   jax.ShapeDtypeStruct((T, 1536), BF), jax.ShapeDtypeStruct((T, 512), BF)],
        scratch_shapes=[pltpu.VMEM((2, 2 * BLK, BLK), F32), two(), big(), big(), big(), big(),
                        two(), two(), big(), big(), big(), pltpu.VMEM((2, 4, 2 * BLK, BLK), BF), big()],
        compiler_params=_params(("parallel", "arbitrary")),
    )(qs, ks, vs, z, cos, sa, sb, opre, lse, lse, dob)


def _merge_out_loss(og, ob, z, w_a, w_b, w_out, x, tgt, wf):
    tm = 256

    def body(og_ref, ob_ref, ga_ref, gb_ref, wa_ref, wb_ref, wo_ref, x_ref, t_ref, wf_ref,
             ya_ref, yb_ref, m_ref, dout_ref, loss_ref, gwf_ref):
        @pl.when(pl.program_id(0) == 0)
        def _():
            loss_ref[...] = jnp.zeros_like(loss_ref)
            gwf_ref[...] = jnp.zeros_like(gwf_ref)

        ya, yb = _dot(og_ref[...], wa_ref[...]), _dot(ob_ref[...], wb_ref[...])
        ya_ref[...] = ya
        yb_ref[...] = yb
        m = (_sigmoid(ga_ref[...]) * ya + _sigmoid(gb_ref[...]) * yb).astype(BF)
        m_ref[...] = m
        out = x_ref[...] + _dot(m, wo_ref[...])
        r = lax.rsqrt(jnp.mean(out * out, axis=-1, keepdims=True) + EPS)
        yh = out * r
        wfv = wf_ref[...]
        err = yh * wfv - t_ref[...]
        loss_ref[...] += jnp.sum(err * err, axis=0, keepdims=True) * (0.5 / D)
        dy = err * (1.0 / D)
        gwf_ref[...] += jnp.sum(dy * yh, axis=0, keepdims=True)
        dyh = dy * wfv
        dout_ref[...] = r * (dyh - yh * jnp.mean(dyh * yh, axis=-1, keepdims=True))

    row = pl.BlockSpec((tm, D), lambda i: (i, 0))
    vec = pl.BlockSpec((1, D), lambda i: (0, 0))
    whole = lambda w: pl.BlockSpec(w.shape, lambda i: (0, 0))
    return pl.pallas_call(
        body, name="merge_out_loss", grid=(T // tm,),
        in_specs=[row, pl.BlockSpec((tm, ob.shape[1]), lambda i: (i, 0)),
                  pl.BlockSpec((tm, D), lambda i: (i, GATE_COL0 // D)),
                  pl.BlockSpec((tm, D), lambda i: (i, GATE_COL0 // D + 1)),
                  whole(w_a), whole(w_b), whole(w_out), row, row, vec],
        out_specs=[row, row, row, row, vec, vec],
        out_shape=[jax.ShapeDtypeStruct((T, D), F32), jax.ShapeDtypeStruct((T, D), F32),
                   jax.ShapeDtypeStruct((T, D), BF), jax.ShapeDtypeStruct((T, D), F32),
                   jax.ShapeDtypeStruct((1, D), F32), jax.ShapeDtypeStruct((1, D), F32)],
        compiler_params=_params(("arbitrary",)),
    )(og, ob, z, z, w_a, w_b, w_out, x, tgt, wf)


def _merge_proj_bwd(dout, ya, yb, z, w_a, w_b, w_out):
    tm = 256

    def body(dout_ref, ya_ref, yb_ref, ga_ref, gb_ref, wa_ref, wb_ref, wo_ref,
             dya_ref, dyb_ref, dg_ref, dog_ref, dob_ref):
        dmv = _dot(dout_ref[...].astype(BF), wo_ref[...], _NT)
        sa, sb = _sigmoid(ga_ref[...]), _sigmoid(gb_ref[...])
        dya, dyb = (sa * dmv).astype(BF), (sb * dmv).astype(BF)
        dya_ref[...] = dya
        dyb_ref[...] = dyb
        dg_ref[:, :D] = (dmv * ya_ref[...] * sa * (1.0 - sa)).astype(BF)
        dg_ref[:, D:] = (dmv * yb_ref[...] * sb * (1.0 - sb)).astype(BF)
        dog_ref[...] = _dot(dya, wa_ref[...], _NT)
        dob_ref[...] = _dot(dyb, wb_ref[...], _NT)

    row = pl.BlockSpec((tm, D), lambda i: (i, 0))
    whole = lambda w: pl.BlockSpec(w.shape, lambda i: (0, 0))
    nb = w_b.shape[0]
    return pl.pallas_call(
        body, name="merge_proj_bwd", grid=(T // tm,),
        in_specs=[row, row, row, pl.BlockSpec((tm, D), lambda i: (i, GATE_COL0 // D)),
                  pl.BlockSpec((tm, D), lambda i: (i, GATE_COL0 // D + 1)), whole(w_a), whole(w_b), whole(w_out)],
        out_specs=[row, row, pl.BlockSpec((tm, 2 * D), lambda i: (i, 0)), row,
                   pl.BlockSpec((tm, nb), lambda i: (i, 0))],
        out_shape=[jax.ShapeDtypeStruct((T, D), BF), jax.ShapeDtypeStruct((T, D), BF),
                   jax.ShapeDtypeStruct((T, 2 * D), BF), jax.ShapeDtypeStruct((T, D), F32),
                   jax.ShapeDtypeStruct((T, nb), F32)],
        compiler_params=_params(("parallel",)),
    )(dout, ya, yb, z, z, w_a, w_b, w_out)


def _rope_inv_freq():
    inv = ROPE_THETA ** (-jnp.arange(0, 64, 2, dtype=F32) / 64)
    return jnp.tile(inv, 4).reshape(1, BLK)


def _local_step(x, pos, norm_w, lbl, hnw, wf, tgt, w_in, w_a, w_b, w_out, shard_shapes=()):
    invf = _rope_inv_freq()
    if shard_shapes:
        blk = jnp.reshape(2 * lax.axis_index("x") + lax.axis_index("y"), (1,)).astype(jnp.int32)
        h, cos, sa, sb, z_own, (w_in,) = _norm_and_rope_tables(
            x, norm_w, pos, invf, side=_gather_side([w_in], WEIGHT_AXES[:1]), own=(w_in, blk))
        z, (w_a, w_b, w_out) = _z_rest(h, w_in, z_own, blk, side=_gather_side([w_a, w_b, w_out], WEIGHT_AXES[1:]))
    else:
        h, cos, sa, sb, _, _ = _norm_and_rope_tables(x, norm_w, pos, invf)
        z = _matmul(h, w_in, tm=T, tn=512, name="z_proj")
    oraw, og, shist = _hgrn_fwd(z, lbl, hnw)
    ob, opre, lse, qs, ks, vs = _attn_fwd(z, cos, sa, sb)
    ya, yb, merged, dout, loss_vec, g_wf = _merge_out_loss(og, ob, z, w_a, w_b, w_out, x, tgt, wf)

    dya, dyb, dgates, dog, dob = _merge_proj_bwd(dout, ya, yb, z, w_a, w_b, w_out)
    g_wout = _matmul(merged, dout, ta=True, out_dtype=BF, tm=512, tn=1024, name="g_wout")
    g_wa = _matmul(og, dya, ta=True, out_dtype=BF, tm=512, tn=1024, name="g_wa")
    g_wb = _matmul(ob, dyb, ta=True, out_dtype=BF, tm=512, tn=1024, name="g_wb")
    small = [g_wa, g_wb, g_wout]
    side_s = side_w = None
    if shard_shapes:
        p3_s = _rs_partials(small, shard_shapes[1:], WEIGHT_AXES[1:], "small")
        side_s = _chip_exchange_side(p3_s, shard_shapes[1:], WEIGHT_AXES[1:])
    dz_h, dlb, g_hnw, land_s = _hgrn_bwd(z, lbl, hnw, oraw, dog, shist, side=side_s)
    dq, dk, dv, dag = _attn_bwd(z, qs, ks, vs, cos, sa, sb, opre, lse, dob)
    dz_parts = [dz_h, dq, dk, dv, dag, dgates]
    if shard_shapes:
        c = lax.axis_index("c")
        half = lambda i: jnp.reshape(i, (1,)).astype(jnp.int32)
        g_send = _grad_w_in_half(h, dz_parts, half(1 - c))
        g_keep, (g_sib,) = _grad_w_in_half(h, dz_parts, half(c), side=_sibling_send_side(g_send))
        p3_w = [_add_bf16(g_keep, g_sib, "pair_sum_w_in").reshape(1, D // 2, NIN)]
        side_w = _chip_exchange_side(p3_w, shard_shapes[:1], WEIGHT_AXES[:1])
    else:
        g_big = [_grad_w_in(h, dz_parts)] + small
    gx, g_nw, land_w = _grad_x(dz_parts, w_in, x, dout, norm_w, side=side_w)
    if shard_shapes:
        g_big = _rs_finish(p3_w + p3_s, list(land_w) + list(land_s), shard_shapes, WEIGHT_AXES)
    return dict(loss_vec=loss_vec, gx=gx, g_nw=g_nw, dlb=dlb, g_hnw=g_hnw, g_wf=g_wf,
                g_win=g_big[0], g_wa=g_big[1], g_wb=g_big[2], g_wout=g_big[3])


MESH = pl.DeviceIdType.MESH
HBM = pl.BlockSpec(memory_space=pl.ANY)
WEIGHT_AXES = (1, 0, 1, 0)


def _place():
    x, y, c = lax.axis_index("x"), lax.axis_index("y"), lax.axis_index("c")
    chips = [(1 - x, y), (x, 1 - y), (1 - x, 1 - y)]
    return x, y, c, chips


def _block_half(ref, shard_shape, axis, j, half):
    r, c = shard_shape
    hr = r // 2
    if axis == 0:
        return ref.at[pl.ds(pl.multiple_of(j * r + half * hr, 16), hr), :]
    return ref.at[pl.ds(pl.multiple_of(half * hr, 16), hr), pl.ds(pl.multiple_of(j * c, 128), c)]


class _Side:
    def __init__(self, arrays, out_shapes, sems, first, last):
        self.arrays, self.out_shapes, self.sems, self.first, self.last = arrays, out_shapes, sems, first, last


def _gather_side(shards, axes):
    n = len(shards)
    shapes = [s.shape for s in shards]

    def copies(ins, outs, sems):
        send1, recv1, send2, recv2, send0, recv0 = sems
        x, y, c, chips = _place()
        me = 2 * x + y
        sib = (x, y, 1 - c)
        near = ((1 - c) * (1 - x) + c * x, (1 - c) * y + c * (1 - y))
        far = ((1 - c) * x + c * (1 - x), (1 - c) * (1 - y) + c * y)
        out = []
        for a in range(n):
            r, cc = shapes[a]
            mine = (outs[a].at[pl.ds(pl.multiple_of(me * r, 16), r), :] if axes[a] == 0
                    else outs[a].at[:, pl.ds(pl.multiple_of(me * cc, 128), cc)])
            own = pltpu.make_async_remote_copy(
                src_ref=ins[a], dst_ref=mine, send_sem=send0.at[a], recv_sem=recv0.at[a],
                device_id=sib, device_id_type=MESH)
            src = ins[a].at[pl.ds(pl.multiple_of(c * (r // 2), 16), r // 2), :]
            sends = [pltpu.make_async_remote_copy(
                src_ref=src, dst_ref=_block_half(outs[a], shapes[a], axes[a], me, c),
                send_sem=send1.at[a, k], recv_sem=recv1.at[a, k], device_id=(*chips[k], c), device_id_type=MESH)
                for k in range(2)]

            def region(chip, half):
                return _block_half(outs[a], shapes[a], axes[a], 2 * chip[0] + chip[1], half)

            def arrival(chip, k):
                reg = region(chip, c)
                return pltpu.make_async_remote_copy(
                    src_ref=reg, dst_ref=reg, send_sem=send1.at[a, k], recv_sem=recv1.at[a, k],
                    device_id=(*chip, c), device_id_type=MESH)

            def to_sibling(chip, k):
                reg = region(chip, c)
                return pltpu.make_async_remote_copy(
                    src_ref=reg, dst_ref=reg, send_sem=send2.at[a, k], recv_sem=recv2.at[a, k],
                    device_id=sib, device_id_type=MESH)

            def from_sibling(chip, k):
                reg = region(chip, 1 - c)
                return pltpu.make_async_remote_copy(
                    src_ref=reg, dst_ref=reg, send_sem=send2.at[a, k], recv_sem=recv2.at[a, k],
                    device_id=sib, device_id_type=MESH)

            relay = pltpu.make_async_remote_copy(
                src_ref=region(near, c), dst_ref=region(near, c), send_sem=send1.at[a, 2], recv_sem=recv1.at[a, 2],
                device_id=(*far, c), device_id_type=MESH)
            hops = [(arrival(near, c), to_sibling(near, c)), (arrival(far, 1 - c), to_sibling(far, 1 - c)),
                    (arrival(chips[2], 2), to_sibling(chips[2], 2))]
            back = [from_sibling(chips[k], k) for k in range(3)]
            out.append((own, sends, relay, hops, back))
        return out

    def first(ins, outs, sems):
        for own, sends, _, _, _ in copies(ins, outs, sems):
            own.start()
            for cp in sends:
                cp.start()

    def last(ins, outs, sems):
        per_array = copies(ins, outs, sems)
        for step in range(3):
            for _, _, relay, hops, _ in per_array:
                arrived, onward = hops[step]
                arrived.wait_recv()
                if step == 0:
                    relay.start()
                onward.start()
        for own, sends, relay, hops, back in per_array:
            for cp in back:
                cp.wait_recv()
            for cp in sends + [relay] + [onward for _, onward in hops]:
                cp.wait_send()
            own.wait()

    full = [(4 * r, c) if ax == 0 else (r, 4 * c) for (r, c), ax in zip(shapes, axes)]
    sems = [pltpu.SemaphoreType.DMA((n, 3)), pltpu.SemaphoreType.DMA((n, 3)),
            pltpu.SemaphoreType.DMA((n, 3)), pltpu.SemaphoreType.DMA((n, 3)),
            pltpu.SemaphoreType.DMA((n,)), pltpu.SemaphoreType.DMA((n,))]
    return _Side(list(shards), [jax.ShapeDtypeStruct(f, BF) for f in full], sems, first, last)


def _as3d(g, shard_shape, axis):
    r, c = shard_shape
    return g.reshape(4, r, c) if axis == 0 else g.reshape(1, r, 4 * c)


def _half_rows(ref3, hr, half):
    return ref3.at[:, pl.ds(pl.multiple_of(half * hr, 16), hr), :]


def _rs_pair_exchange(g3s, name):
    n = len(g3s)

    def body(*refs):
        ins, outs = refs[:n], refs[n:2 * n]
        send, recv = refs[2 * n:]
        x, y, c, _ = _place()
        cps = []
        for a in range(n):
            hr = g3s[a].shape[1] // 2
            cp = pltpu.make_async_remote_copy(
                src_ref=_half_rows(ins[a], hr, 1 - c), dst_ref=outs[a],
                send_sem=send.at[a], recv_sem=recv.at[a], device_id=(x, y, 1 - c), device_id_type=MESH)
            cp.start()
            cps.append(cp)
        for cp in cps:
            cp.wait()

    return pl.pallas_call(
        body, name=name,
        in_specs=[HBM] * n, out_specs=[HBM] * n,
        out_shape=[jax.ShapeDtypeStruct((g.shape[0], g.shape[1] // 2, g.shape[2]), BF) for g in g3s],
        scratch_shapes=[pltpu.SemaphoreType.DMA((n,)), pltpu.SemaphoreType.DMA((n,))],
    )(*g3s)


def _pair_sum(g3, land, cidx, name):
    nb, r, w = g3.shape
    hr = r // 2
    tr = 64

    def body(c_ref, g_ref, l_ref, o_ref):
        o_ref[...] = (g_ref[...].astype(F32) + l_ref[...].astype(F32)).astype(BF)

    blk = (nb, tr, w)
    return pl.pallas_call(
        body, name=name,
        grid_spec=pltpu.PrefetchScalarGridSpec(
            num_scalar_prefetch=1, grid=(hr // tr,),
            in_specs=[pl.BlockSpec(blk, lambda i, c: (0, c[0] * (hr // tr) + i, 0)),
                      pl.BlockSpec(blk, lambda i, c: (0, i, 0))],
            out_specs=pl.BlockSpec(blk, lambda i, c: (0, i, 0))),
        out_shape=jax.ShapeDtypeStruct((nb, hr, w), BF),
        compiler_params=_params(("parallel",)),
    )(cidx, g3, land)


def _chip_exchange_side(p3s, shapes, axes):
    n = len(p3s)

    def copies(ins, outs, sems):
        send, recv = sems
        x, y, c, chips = _place()
        cps = []
        for a in range(n):
            r, cc = shapes[a]
            for k, (px, py) in enumerate(chips):
                j = 2 * px + py
                src = ins[a].at[j] if axes[a] == 0 else ins[a].at[0, :, pl.ds(pl.multiple_of(j * cc, 128), cc)]
                cps.append(pltpu.make_async_remote_copy(
                    src_ref=src, dst_ref=outs[a].at[k], send_sem=send.at[a, k], recv_sem=recv.at[a, k],
                    device_id=(px, py, c), device_id_type=MESH))
        return cps

    def first(ins, outs, sems):
        for cp in copies(ins, outs, sems):
            cp.start()

    def last(ins, outs, sems):
        for cp in copies(ins, outs, sems):
            cp.wait()

    return _Side(list(p3s), [jax.ShapeDtypeStruct((3, r // 2, c), BF) for r, c in shapes],
                 [pltpu.SemaphoreType.DMA((n, 3)), pltpu.SemaphoreType.DMA((n, 3))], first, last)


def _chip_sum(p3, land, shard_shape, axis, idx, name):
    r, c = shard_shape
    hr = r // 2
    tr = 64
    nt = hr // tr

    def body(idx_ref, p_ref, l_ref, o_ref):
        acc = p_ref[...].astype(F32)
        for k in range(3):
            acc = acc + l_ref[k].astype(F32)
        o_ref[...] = acc

    own = (pl.BlockSpec((None, tr, c), lambda i, idx: (idx[0], i, 0)) if axis == 0
           else pl.BlockSpec((None, tr, c), lambda i, idx: (0, i, idx[0])))
    return pl.pallas_call(
        body, name=name,
        grid_spec=pltpu.PrefetchScalarGridSpec(
            num_scalar_prefetch=1, grid=(nt,),
            in_specs=[own, pl.BlockSpec((3, tr, c), lambda i, idx: (0, i, 0))],
            out_specs=pl.BlockSpec((tr, c), lambda i, idx: (idx[1] * nt + i, 0))),
        out_shape=jax.ShapeDtypeStruct((r, c), F32),
        compiler_params=_params(("parallel",)),
    )(idx, p3, land)


def _rs_pair_gather(fulls):
    n = len(fulls)

    def body(*refs):
        ins, outs = refs[:n], refs[n:2 * n]
        send, recv = refs[2 * n:]
        x, y, c, _ = _place()
        cps = []
        for a in range(n):
            hr = fulls[a].shape[0] // 2
            rows = pl.ds(pl.multiple_of(c * hr, 8), hr)
            cp = pltpu.make_async_remote_copy(
                src_ref=ins[a].at[rows, :], dst_ref=outs[a].at[rows, :], send_sem=send.at[a], recv_sem=recv.at[a],
                device_id=(x, y, 1 - c), device_id_type=MESH)
            cp.start()
            cps.append(cp)
        for a, cp in enumerate(cps):
            cp.wait_send()
            hr = fulls[a].shape[0] // 2
            other = pl.ds(pl.multiple_of((1 - c) * hr, 8), hr)
            pltpu.make_async_remote_copy(
                src_ref=ins[a].at[other, :], dst_ref=outs[a].at[other, :], send_sem=send.at[a], recv_sem=recv.at[a],
                device_id=(x, y, 1 - c), device_id_type=MESH).wait_recv()

    return pl.pallas_call(
        body, name="grads_pair_gather",
        in_specs=[HBM] * n, out_specs=[HBM] * n,
        out_shape=[jax.ShapeDtypeStruct(f.shape, F32) for f in fulls],
        input_output_aliases={a: a for a in range(n)},
        scratch_shapes=[pltpu.SemaphoreType.DMA((n,)), pltpu.SemaphoreType.DMA((n,))],
    )(*fulls)


def _sibling_send_side(arr):
    def copy(ins, outs, sems):
        x, y, c, _ = _place()
        return pltpu.make_async_remote_copy(
            src_ref=ins[0], dst_ref=outs[0], send_sem=sems[0].at[0], recv_sem=sems[1].at[0],
            device_id=(x, y, 1 - c), device_id_type=MESH)

    return _Side([arr], [jax.ShapeDtypeStruct(arr.shape, arr.dtype)],
                 [pltpu.SemaphoreType.DMA((1,)), pltpu.SemaphoreType.DMA((1,))],
                 lambda ins, outs, sems: copy(ins, outs, sems).start(),
                 lambda ins, outs, sems: copy(ins, outs, sems).wait())


def _add_bf16(a, b, name):
    r, c = a.shape
    tr = 64

    def body(a_ref, b_ref, o_ref):
        o_ref[...] = (a_ref[...].astype(F32) + b_ref[...].astype(F32)).astype(BF)

    blk = pl.BlockSpec((tr, c), lambda i: (i, 0))
    return pl.pallas_call(
        body, name=name, grid=(r // tr,), in_specs=[blk, blk], out_specs=blk,
        out_shape=jax.ShapeDtypeStruct((r, c), BF), compiler_params=_params(("parallel",)),
    )(a, b)


def _rs_partials(grads, shapes, axes, tag):
    cidx = jnp.reshape(lax.axis_index("c"), (1,)).astype(jnp.int32)
    g3s = [_as3d(g, s, ax) for g, s, ax in zip(grads, shapes, axes)]
    lands = _rs_pair_exchange(g3s, f"grads_pair_exchange_{tag}")
    return [_pair_sum(g3, l, cidx, f"pair_sum_{tag}_{a}") for a, (g3, l) in enumerate(zip(g3s, lands))]


def _rs_finish(p3s, landed, shapes, axes):
    x, y, c = lax.axis_index("x"), lax.axis_index("y"), lax.axis_index("c")
    idx = jnp.stack([2 * x + y, c]).astype(jnp.int32)
    fulls = [_chip_sum(p3, l2, s, ax, idx, f"chip_sum_{a}")
             for a, (p3, l2, s, ax) in enumerate(zip(p3s, landed, shapes, axes))]
    return _rs_pair_gather(fulls)


NSMALL = 8


def _small_all_reduce(g_nw, dlb, g_hnw, g_wf, loss_vec):
    def body(nw_ref, lb_ref, hn_ref, wf_ref, ls_ref, out_ref, pack_ref, buf_ref, send, recv):
        x, y, c = lax.axis_index("x"), lax.axis_index("y"), lax.axis_index("c")
        me = 4 * x + 2 * y + c
        pack_ref[...] = jnp.zeros_like(pack_ref)
        pack_ref[0:1, :] = nw_ref[...]
        pack_ref[1:2, :] = lb_ref[...]
        pack_ref[2:3, 0:HK] = hn_ref[...]
        pack_ref[3:4, :] = wf_ref[...]
        pack_ref[4:5, :] = ls_ref[...]
        buf_ref[me] = pack_ref[...]
        cps = []
        for d in range(1, 8):
            dx, dy, dc = d >> 2, (d >> 1) & 1, d & 1
            peer = (1 - x if dx else x, 1 - y if dy else y, 1 - c if dc else c)
            cp = pltpu.make_async_remote_copy(
                src_ref=pack_ref, dst_ref=buf_ref.at[me], send_sem=send.at[d - 1], recv_sem=recv.at[d - 1],
                device_id=peer, device_id_type=MESH)
            cp.start()
            cps.append(cp)
        for d in range(1, 8):
            dx, dy, dc = d >> 2, (d >> 1) & 1, d & 1
            src = 4 * (1 - x if dx else x) + 2 * (1 - y if dy else y) + (1 - c if dc else c)
            pltpu.make_async_remote_copy(
                src_ref=pack_ref, dst_ref=buf_ref.at[src], send_sem=send.at[d - 1], recv_sem=recv.at[d - 1],
                device_id=(x, y, c), device_id_type=MESH).wait_recv()
        for cp in cps:
            cp.wait_send()
        acc = buf_ref[0]
        for i in range(1, 8):
            acc = acc + buf_ref[i]
        out_ref[...] = acc

    vm = pl.BlockSpec(memory_space=pltpu.VMEM)
    return pl.pallas_call(
        body, name="small_all_reduce",
        in_specs=[vm] * 5, out_specs=vm,
        out_shape=jax.ShapeDtypeStruct((NSMALL, D), F32),
        scratch_shapes=[pltpu.VMEM((NSMALL, D), F32), pltpu.VMEM((8, NSMALL, D), F32),
                        pltpu.SemaphoreType.DMA((7,)), pltpu.SemaphoreType.DMA((7,))],
    )(g_nw, dlb, g_hnw, g_wf, loss_vec)


def _adamw_math(w, g, m, v):
    m = B1 * m + (1.0 - B1) * g
    v = B2 * v + (1.0 - B2) * (g * g)
    m_hat = m / (1.0 - B1 ** STEP)
    v_hat = v / (1.0 - B2 ** STEP)
    return -LR * (m_hat / (jnp.sqrt(v_hat) + ADAM_EPS) + WD * w), m, v


def _adamw(w, g, m, v, name):
    r, c = w.shape
    tr = 64

    def body(w_ref, g_ref, m_ref, v_ref, d_ref, nm_ref, nv_ref, go_ref):
        g = g_ref[...]
        d_ref[...], nm_ref[...], nv_ref[...] = _adamw_math(w_ref[...], g, m_ref[...], v_ref[...])
        go_ref[...] = g

    blk = pl.BlockSpec((tr, c), lambda i: (i, 0))
    return pl.pallas_call(
        body, name=name, grid=(r // tr,), in_specs=[blk] * 4, out_specs=[blk] * 4,
        out_shape=[jax.ShapeDtypeStruct((r, c), F32)] * 4,
        compiler_params=_params(("parallel",)),
    )(w, g, m, v)


def _small_update(red, lbl, params):
    def body(red_ref, *refs):
        ins, outs = refs[:12], refs[12:]
        lb = _lower_bound(ins[3][...])
        dl0 = red_ref[1:2, :] * lb * (1.0 - lb)
        row = lax.broadcasted_iota(jnp.int32, (2, D), 0)
        grads = [red_ref[0:1, :], jnp.where(row == 0, dl0, -dl0), red_ref[2:3, 0:HK], red_ref[3:4, :]]
        for i, g in enumerate(grads):
            w, m, v = ins[3 * i][...], ins[3 * i + 1][...], ins[3 * i + 2][...]
            d, nm, nv = _adamw_math(w, g, m, v)
            outs[4 * i][...] = g
            outs[4 * i + 1][...] = d
            outs[4 * i + 2][...] = nm
            outs[4 * i + 3][...] = nv
        outs[16][...] = jnp.sum(red_ref[4:5, :], axis=1, keepdims=True)

    flat = [a for p in params for a in p]
    vm = pl.BlockSpec(memory_space=pltpu.VMEM)
    shapes = [jax.ShapeDtypeStruct(p[0].shape, F32) for p in params for _ in range(4)]
    return pl.pallas_call(
        body, name="small_update",
        in_specs=[vm] * 13, out_specs=[vm] * 17,
        out_shape=shapes + [jax.ShapeDtypeStruct((1, 1), F32)],
    )(red, *flat)


def kernel(x, positions, norm_w, w_in, lb_logits, hgrn_norm_w, w_branch_a, w_branch_b, w_out, final_norm_w, loss_target, m_norm_w, m_w_in, m_lb_logits, m_hgrn_norm_w, m_w_branch_a, m_w_branch_b, m_w_out, m_final_norm_w, v_norm_w, v_w_in, v_lb_logits, v_hgrn_norm_w, v_w_branch_a, v_w_branch_b, v_w_out, v_final_norm_w):
    big_w = [w_in[0], w_branch_a[0], w_branch_b[0], w_out[0]]
    big_m = [m_w_in[0], m_w_branch_a[0], m_w_branch_b[0], m_w_out[0]]
    big_v = [v_w_in[0], v_w_branch_a[0], v_w_branch_b[0], v_w_out[0]]
    shapes = [w.shape for w in big_w]
    wf = final_norm_w.reshape(1, D)

    shards = [w.astype(BF) for w in big_w]
    loc = _local_step(x[0], positions.reshape(T, 1), norm_w, lb_logits, hgrn_norm_w, wf, loss_target[0],
                      *shards, shard_shapes=shapes)
    g_big = [loc["g_win"], loc["g_wa"], loc["g_wb"], loc["g_wout"]]
    red = _small_all_reduce(loc["g_nw"], loc["dlb"], loc["g_hnw"], loc["g_wf"], loc["loss_vec"])

    small = _small_update(red, lb_logits, [
        (norm_w, m_norm_w, v_norm_w), (lb_logits, m_lb_logits, v_lb_logits),
        (hgrn_norm_w, m_hgrn_norm_w, v_hgrn_norm_w),
        (wf, m_final_norm_w.reshape(1, D), v_final_norm_w.reshape(1, D))])
    loss = small[16].reshape(())
    sg, sd, sm, sv = ([small[4 * i + j] for i in range(4)] for j in range(4))
    for lst in (sg, sd, sm, sv):
        lst[3] = lst[3].reshape(D)
    upd = [_adamw(w, g, m, v, f"adamw_{a}") for a, (w, g, m, v) in enumerate(zip(big_w, g_big, big_m, big_v))]
    bd, bm, bv, bg = ([u[j][None] for u in upd] for j in range(4))

    def order(s, b):
        return [s[0], b[0], s[1], s[2], b[1], b[2], b[3], s[3]]

    return (loss, loc["gx"][None], *order(sg, bg), *order(sd, bd), *order(sm, bm), *order(sv, bv))
```

```python
import functools

import jax
import jax.numpy as jnp
from jax import lax
from jax.experimental import pallas as pl
from jax.experimental.pallas import tpu as pltpu

T = 2048
D = 1024
NIN = 11264
HEADS = 8
HK = 128
CH = 16
NCH = T // CH
HSTEP = 2
ATT_GROUPS = ((128, 1), (512, 4), (2048, 16))
ATT_COL0 = 4096
AG_COL0 = 8704
GATE_COL0 = 9216
EPS = 1e-6
ROPE_THETA = 10000.0
LR, B1, B2, ADAM_EPS, WD, STEP = 0.001, 0.9, 0.999, 1e-08, 0.01, 10

F32 = jnp.float32
BF = jnp.bfloat16
VMEM_LIMIT = 56 * 1024 * 1024

_NN = (((1,), (0,)), ((), ()))
_NT = (((1,), (1,)), ((), ()))
_TN = (((0,), (0,)), ((), ()))


def _dot(a, b, dims=_NN):
    return lax.dot_general(a, b, dims, preferred_element_type=F32)


def _bdot(a, b, dims=_NN):
    return lax.dot_general(a.astype(BF), b.astype(BF), dims, preferred_element_type=F32)


def _sigmoid(x):
    return jax.nn.sigmoid(x)


def _params(sem=None):
    return pltpu.CompilerParams(dimension_semantics=sem, vmem_limit_bytes=VMEM_LIMIT)


def _matmul(a, b, *, ta=False, tb=False, out_dtype=F32, tm=512, tn=512, tk=None, name, side=None):
    m = a.shape[1] if ta else a.shape[0]
    kdim = a.shape[0] if ta else a.shape[1]
    n = b.shape[0] if tb else b.shape[1]
    tk = tk or kdim
    tm, tn = min(tm, m), min(tn, n)
    nm, nn, nk = m // tm, n // tn, kdim // tk
    dims = (((0 if ta else 1,), (1 if tb else 0,)), ((), ()))
    s_arrays, s_in_specs, s_shapes, s_out_specs, s_sems = _side_io(side)
    na, no = len(s_arrays), len(s_shapes)
    nacc = 1 if nk > 1 else 0

    def body(*refs):
        a_ref, b_ref = refs[:2]
        s_ins, o_ref, s_outs = refs[2:2 + na], refs[2 + na], refs[3 + na:3 + na + no]
        scratch = refs[3 + na + no:]
        s_sem_refs = scratch[nacc:]
        i, j, k = pl.program_id(0), pl.program_id(1), pl.program_id(2)
        if side is not None:
            @pl.when((i == 0) & (j == 0) & (k == 0))
            def _():
                side.first(s_ins, s_outs, s_sem_refs)

        prod = _bdot(a_ref[...], b_ref[...], dims)
        if nk == 1:
            o_ref[...] = prod.astype(out_dtype)
        else:
            acc = scratch[0]

            @pl.when(k == 0)
            def _():
                acc[...] = prod

            @pl.when(k > 0)
            def _():
                acc[...] += prod

            @pl.when(k == nk - 1)
            def _():
                o_ref[...] = acc[...].astype(out_dtype)

        if side is not None:
            @pl.when((i == nm - 1) & (j == nn - 1) & (k == nk - 1))
            def _():
                side.last(s_ins, s_outs, s_sem_refs)

    a_spec = pl.BlockSpec((tk, tm), lambda i, j, k: (k, i)) if ta else pl.BlockSpec((tm, tk), lambda i, j, k: (i, k))
    b_spec = pl.BlockSpec((tn, tk), lambda i, j, k: (j, k)) if tb else pl.BlockSpec((tk, tn), lambda i, j, k: (k, j))
    sem = ("parallel", "parallel", "arbitrary") if side is None else ("arbitrary",) * 3
    out = pl.pallas_call(
        body, name=name, grid=(nm, nn, nk),
        in_specs=[a_spec, b_spec] + s_in_specs,
        out_specs=[pl.BlockSpec((tm, tn), lambda i, j, k: (i, j))] + s_out_specs,
        out_shape=[jax.ShapeDtypeStruct((m, n), out_dtype)] + s_shapes,
        scratch_shapes=([pltpu.VMEM((tm, tn), F32)] if nk > 1 else []) + s_sems,
        compiler_params=_params(sem),
    )(a, b, *s_arrays)
    return out[0] if side is None else (out[0], out[1:])


DZ_TILE = 512


def _part_offsets(parts):
    counts = [p.shape[1] // DZ_TILE for p in parts]
    offs = [sum(counts[:i]) for i in range(len(parts))]
    return counts, offs


def _part_spec(rows, cnt, off, tile_axis):
    def index(*g):
        return (0 if rows is None else g[0], jnp.clip(g[tile_axis] - off, 0, cnt - 1))
    return index


def _grad_w_in(h, parts):
    counts, offs = _part_offsets(parts)
    n = len(parts)

    def body(h_ref, *refs):
        o_ref = refs[n]
        j = pl.program_id(0)
        for p_ref, cnt, off in zip(refs[:n], counts, offs):
            @pl.when((j >= off) & (j < off + cnt))
            def _(p_ref=p_ref):
                o_ref[...] = _bdot(h_ref[...], p_ref[...], _TN).astype(BF)

    return pl.pallas_call(
        body, name="g_win", grid=(sum(counts),),
        in_specs=[pl.BlockSpec((T, D), lambda j: (0, 0))] +
                 [pl.BlockSpec((T, DZ_TILE), _part_spec(None, c, o, 0)) for c, o in zip(counts, offs)],
        out_specs=pl.BlockSpec((D, DZ_TILE), lambda j: (0, j)),
        out_shape=jax.ShapeDtypeStruct((D, NIN), BF),
        compiler_params=_params(("parallel",)),
    )(h, *parts)


def _grad_w_in_half(h, parts, half_idx, side=None):
    counts, offs = _part_offsets(parts)
    n = len(parts)
    nj = sum(counts)
    s_arrays, s_in_specs, s_shapes, s_out_specs, s_sems = _side_io(side)
    na, no = len(s_arrays), len(s_shapes)

    def body(idx_ref, h_ref, *refs):
        s_ins, o_ref, s_outs, s_sem_refs = refs[n:n + na], refs[n + na], refs[n + na + 1:n + na + 1 + no], refs[n + na + 1 + no:]
        j = pl.program_id(0)
        if side is not None:
            @pl.when(j == 0)
            def _():
                side.first(s_ins, s_outs, s_sem_refs)

        for p_ref, cnt, off in zip(refs[:n], counts, offs):
            @pl.when((j >= off) & (j < off + cnt))
            def _(p_ref=p_ref):
                o_ref[...] = _bdot(h_ref[...], p_ref[...], _TN).astype(BF)

        if side is not None:
            @pl.when(j == nj - 1)
            def _():
                side.last(s_ins, s_outs, s_sem_refs)

    def part_spec(cnt, off):
        return pl.BlockSpec((T, DZ_TILE), lambda j, idx: (0, jnp.clip(j - off, 0, cnt - 1)))

    out = pl.pallas_call(
        body, name="g_win_half" if side is None else "g_win_half_carrying",
        grid_spec=pltpu.PrefetchScalarGridSpec(
            num_scalar_prefetch=1, grid=(nj,),
            in_specs=[pl.BlockSpec((T, D // 2), lambda j, idx: (0, idx[0]))] +
                     [part_spec(c, o) for c, o in zip(counts, offs)] + s_in_specs,
            out_specs=[pl.BlockSpec((D // 2, DZ_TILE), lambda j, idx: (0, j))] + s_out_specs,
            scratch_shapes=s_sems),
        out_shape=[jax.ShapeDtypeStruct((D // 2, NIN), BF)] + s_shapes,
        compiler_params=_params(("parallel",) if side is None else ("arbitrary",)),
    )(half_idx, h, *parts, *s_arrays)
    return out[0] if side is None else (out[0], out[1:])


def _side_io(side):
    if side is None:
        return [], [], [], [], []
    return (side.arrays, [HBM] * len(side.arrays), side.out_shapes, [HBM] * len(side.out_shapes), side.sems)


def _grad_x(parts, w_in, x, dout, norm_w, side=None):
    counts, offs = _part_offsets(parts)
    n = len(parts)
    tm = 1024
    nm, nk = T // tm, sum(counts)
    s_arrays, s_in_specs, s_shapes, s_out_specs, s_sems = _side_io(side)
    na, no = len(s_arrays), len(s_shapes)

    def body(*refs):
        w_ref, x_ref, dout_ref, nw_ref = refs[n:n + 4]
        s_ins = refs[n + 4:n + 4 + na]
        gx_ref, gw_ref = refs[n + 4 + na:n + 6 + na]
        s_outs = refs[n + 6 + na:n + 6 + na + no]
        acc = refs[n + 6 + na + no]
        s_sem_refs = refs[n + 7 + na + no:]
        i, k = pl.program_id(0), pl.program_id(1)

        @pl.when((i == 0) & (k == 0))
        def _():
            gw_ref[...] = jnp.zeros_like(gw_ref)
            if side is not None:
                side.first(s_ins, s_outs, s_sem_refs)

        @pl.when(k == 0)
        def _():
            acc[...] = jnp.zeros_like(acc)

        for p_ref, cnt, off in zip(refs[:n], counts, offs):
            @pl.when((k >= off) & (k < off + cnt))
            def _(p_ref=p_ref):
                acc[...] += _bdot(p_ref[...], w_ref[...], _NT)

        @pl.when(k == nk - 1)
        def _():
            gw = jnp.zeros((1, D), F32)
            for c in range(tm // BLK):
                rows = pl.ds(BLK * c, BLK)
                xv, dhv = x_ref[rows, :], acc[rows, :]
                r = lax.rsqrt(jnp.mean(xv * xv, axis=-1, keepdims=True) + EPS)
                nrm = xv * r
                dn = dhv * nw_ref[...]
                gw = gw + jnp.sum(dhv * nrm, axis=0, keepdims=True)
                gx_ref[rows, :] = dout_ref[rows, :] + r * (dn - nrm * jnp.mean(dn * nrm, axis=-1, keepdims=True))
            gw_ref[...] += gw

        if side is not None:
            @pl.when((i == nm - 1) & (k == nk - 1))
            def _():
                side.last(s_ins, s_outs, s_sem_refs)

    row = pl.BlockSpec((tm, D), lambda i, k: (i, 0))
    vec = pl.BlockSpec((1, D), lambda i, k: (0, 0))
    out = pl.pallas_call(
        body, name="grad_x", grid=(nm, nk),
        in_specs=[pl.BlockSpec((tm, DZ_TILE), _part_spec(0, c, o, 1)) for c, o in zip(counts, offs)] +
                 [pl.BlockSpec((D, DZ_TILE), lambda i, k: (0, k)), row, row, vec] + s_in_specs,
        out_specs=[row, vec] + s_out_specs,
        out_shape=[jax.ShapeDtypeStruct((T, D), F32), jax.ShapeDtypeStruct((1, D), F32)] + s_shapes,
        scratch_shapes=[pltpu.VMEM((tm, D), F32)] + s_sems,
        compiler_params=_params(("arbitrary", "arbitrary")),
    )(*parts, w_in, x, dout, norm_w, *s_arrays)
    return out[0], out[1], out[2:]


def _norm_and_rope_tables(x, w, pos, invf, side=None, own=None):
    tm = 256
    nm = T // tm
    s_arrays, s_in_specs, s_shapes, s_out_specs, s_sems = _side_io(side)
    na, no = len(s_arrays), len(s_shapes)
    nz = 0 if own is None else 1
    wsh, blk = own if own is not None else (None, jnp.zeros((1,), jnp.int32))

    def body(blk_ref, *refs):
        x_ref, w_ref, pos_ref, invf_ref = refs[:4]
        s_ins = refs[4 + nz:4 + nz + na]
        h_ref, cos_ref, sa_ref, sb_ref = refs[4 + nz + na:8 + nz + na]
        s_outs = refs[8 + 2 * nz + na:8 + 2 * nz + na + no]
        s_sem_refs = refs[8 + 2 * nz + na + no:]
        if side is not None:
            @pl.when(pl.program_id(0) == 0)
            def _():
                side.first(s_ins, s_outs, s_sem_refs)

        xv = x_ref[...]
        r = lax.rsqrt(jnp.mean(xv * xv, axis=-1, keepdims=True) + EPS)
        h = (xv * r * w_ref[...]).astype(BF)
        h_ref[...] = h
        if own is not None:
            refs[8 + nz + na][...] = _dot(h, refs[4][...])
        first = (lax.broadcasted_iota(jnp.int32, (tm, 128), 1) % 64) < 32
        ang = pos_ref[...].astype(F32) * invf_ref[...]
        s = jnp.sin(ang)
        cos_ref[...] = jnp.cos(ang)
        sa_ref[...] = jnp.where(first, -s, 0.0)
        sb_ref[...] = jnp.where(first, 0.0, s)
        if side is not None:
            @pl.when(pl.program_id(0) == nm - 1)
            def _():
                side.last(s_ins, s_outs, s_sem_refs)

    tab = pl.BlockSpec((tm, 128), lambda i, b: (i, 0))
    own_in = [] if own is None else [pl.BlockSpec(wsh.shape, lambda i, b: (0, 0))]
    own_out = [] if own is None else [pl.BlockSpec((tm, wsh.shape[1]), lambda i, b: (i, b[0]))]
    own_shape = [] if own is None else [jax.ShapeDtypeStruct((T, NIN), F32)]
    out = pl.pallas_call(
        body, name="norm_and_rope_tables",
        grid_spec=pltpu.PrefetchScalarGridSpec(
            num_scalar_prefetch=1, grid=(nm,),
            in_specs=[pl.BlockSpec((tm, D), lambda i, b: (i, 0)), pl.BlockSpec((1, D), lambda i, b: (0, 0)),
                      pl.BlockSpec((tm, 1), lambda i, b: (i, 0)), pl.BlockSpec((1, 128), lambda i, b: (0, 0))]
                     + own_in + s_in_specs,
            out_specs=[pl.BlockSpec((tm, D), lambda i, b: (i, 0)), tab, tab, tab] + own_out + s_out_specs,
            scratch_shapes=s_sems),
        out_shape=[jax.ShapeDtypeStruct((T, D), BF)] + [jax.ShapeDtypeStruct((T, 128), F32)] * 3 + own_shape + s_shapes,
        compiler_params=_params(("parallel",) if side is None else ("arbitrary",)),
    )(blk, x, w, pos, invf, *([] if own is None else [wsh]), *s_arrays)
    return out[0], out[1], out[2], out[3], (out[4] if own is not None else None), out[4 + nz:]


def _z_blocks(h, w, z, idx, nb, side, name, fill=None):
    tm, tn = 1024, NIN // 8
    s_arrays, s_in_specs, s_shapes, s_out_specs, s_sems = _side_io(side)
    na, no = len(s_arrays), len(s_shapes)
    nm, ns = T // tm, 2 * nb
    nf = 0 if fill is None else 1

    def col(first, i, s, b):
        return (0, b[first + s // 2] * 2 + s % 2)

    def body(idx_ref, h_ref, w_ref, zin_ref, *refs):
        s_ins = refs[nf:nf + na]
        o_ref = refs[nf + na]
        s_outs = refs[nf + na + 1 + nf:nf + na + 1 + nf + no]
        s_sem_refs = refs[nf + na + 1 + nf + no + nf:]
        i, s = pl.program_id(0), pl.program_id(1)

        @pl.when((i == 0) & (s == 0))
        def _():
            side.first(s_ins, s_outs, s_sem_refs)

        if fill is not None:
            tile = pl.ds(pl.multiple_of((idx_ref[0] * 2 + s) * tn, 128), tn)
            store = pltpu.make_async_copy(w_ref, refs[nf + na + 1].at[:, tile], refs[nf + na + 1 + nf + no].at[0])
            pl.when(i == 0)(store.start)
        o_ref[...] = _dot(h_ref[...], w_ref[...])
        if fill is not None:
            pl.when(i == 0)(store.wait)

        @pl.when((i == nm - 1) & (s == ns - 1))
        def _():
            side.last(s_ins, s_outs, s_sem_refs)

    fills = [] if fill is None else [fill]
    out = pl.pallas_call(
        body, name=name,
        grid_spec=pltpu.PrefetchScalarGridSpec(
            num_scalar_prefetch=1, grid=(nm, ns),
            in_specs=[pl.BlockSpec((tm, D), lambda i, s, b: (i, 0)), pl.BlockSpec((D, tn), functools.partial(col, nb)),
                      HBM] + [HBM] * nf + s_in_specs,
            out_specs=[pl.BlockSpec((tm, tn), lambda i, s, b: (i, col(0, i, s, b)[1]))] + [HBM] * nf + s_out_specs,
            scratch_shapes=[pltpu.SemaphoreType.DMA((1,))] * nf + s_sems),
        out_shape=[jax.ShapeDtypeStruct((T, NIN), F32)] + [jax.ShapeDtypeStruct(f.shape, f.dtype) for f in fills]
                  + s_shapes,
        input_output_aliases={3: 0, **({4: 1} if fill is not None else {})},
        compiler_params=_params(("arbitrary", "arbitrary")),
    )(idx, h, w, z, *fills, *s_arrays)
    return (out[0], *out[1:1 + nf], out[1 + nf:])


def _lower_bound(lbl):
    mx = jnp.max(lbl, axis=0, keepdims=True)
    e = jnp.exp(lbl - mx)
    return e[0:1] / jnp.sum(e, axis=0, keepdims=True)


def _cumsum_rows(g, rows):
    b = g
    sh = 1
    while sh < CH:
        b = b + jnp.where(rows >= sh, pltpu.roll(b, sh, axis=0), 0.0)
        sh *= 2
    return b


def _rev_cumsum_rows(g, rows):
    b = g
    sh = 1
    while sh < CH:
        b = b + jnp.where(rows < CH - sh, pltpu.roll(b, CH - sh, axis=0), 0.0)
        sh *= 2
    return b


SUB = CH // 2


def _direct_block(qb, kb, vb, bb, rows8):
    ob = jnp.zeros_like(qb)
    for s in range(SUB):
        e_s = jnp.exp(jnp.where(rows8 >= s, bb - bb[s:s + 1], -jnp.inf))
        ob = ob + jnp.sum(qb * e_s * kb[s:s + 1], axis=1, keepdims=True) * vb[s:s + 1]
    return ob


def _direct_block_bwd(qb, kb, vb, bb, dob, rows8, rowc8):
    dq = dk = dv = db = jnp.zeros_like(qb)
    for s in range(SUB):
        one = (rowc8 == s).astype(F32)
        ks, vs = kb[s:s + 1], vb[s:s + 1]
        e_s = jnp.exp(jnp.where(rows8 >= s, bb - bb[s:s + 1], -jnp.inf))
        qes = qb * e_s
        w = qes * ks
        a = jnp.sum(w, axis=1, keepdims=True)
        da = jnp.sum(dob * vs, axis=1, keepdims=True)
        dv = dv + one * jnp.sum(a * dob, axis=0, keepdims=True)
        dq = dq + da * e_s * ks
        dk = dk + one * jnp.sum(da * qes, axis=0, keepdims=True)
        u = da * w
        db = db + u - one * jnp.sum(u, axis=0, keepdims=True)
    return dq, dk, dv, db


def _cross_factors(q, k, b):
    ref = b[SUB - 1:SUB]
    e_hi, e_lo = jnp.exp(b[SUB:] - ref), jnp.exp(ref - b[:SUB])
    return q[SUB:] * e_hi, k[:SUB] * e_lo, e_hi, e_lo


def _intra_fwd(q, k, v, b, rows8):
    lo = _direct_block(q[:SUB], k[:SUB], v[:SUB], b[:SUB], rows8)
    hi = _direct_block(q[SUB:], k[SUB:], v[SUB:], b[SUB:], rows8)
    qe_hi, ke_lo, _, _ = _cross_factors(q, k, b)
    for s in range(SUB):
        hi = hi + jnp.sum(qe_hi * ke_lo[s:s + 1], axis=1, keepdims=True) * v[s:s + 1]
    return jnp.concatenate([lo, hi], axis=0)


def _intra_bwd(q, k, v, b, do, rows8, rowc8):
    dq_lo, dk_lo, dv_lo, db_lo = _direct_block_bwd(q[:SUB], k[:SUB], v[:SUB], b[:SUB], do[:SUB], rows8, rowc8)
    dq_hi, dk_hi, dv_hi, db_hi = _direct_block_bwd(q[SUB:], k[SUB:], v[SUB:], b[SUB:], do[SUB:], rows8, rowc8)
    qe_hi, ke_lo, e_hi, e_lo = _cross_factors(q, k, b)
    do_hi, v_lo = do[SUB:], v[:SUB]
    dqe = dke = jnp.zeros_like(qe_hi)
    for s in range(SUB):
        one = (rowc8 == s).astype(F32)
        a = jnp.sum(qe_hi * ke_lo[s:s + 1], axis=1, keepdims=True)
        da = jnp.sum(do_hi * v_lo[s:s + 1], axis=1, keepdims=True)
        dv_lo = dv_lo + one * jnp.sum(a * do_hi, axis=0, keepdims=True)
        dqe = dqe + da * ke_lo[s:s + 1]
        dke = dke + one * jnp.sum(da * qe_hi, axis=0, keepdims=True)
    u_hi, u_lo = dqe * qe_hi, dke * ke_lo
    d_ref = jnp.sum(u_lo, axis=0, keepdims=True) - jnp.sum(u_hi, axis=0, keepdims=True)
    db_lo = db_lo - u_lo + (rowc8 == SUB - 1).astype(F32) * d_ref
    cat = lambda lo, hi: jnp.concatenate([lo, hi], axis=0)
    return (cat(dq_lo, dq_hi + dqe * e_hi), cat(dk_lo + dke * e_lo, dk_hi), cat(dv_lo, dv_hi),
            cat(db_lo, db_hi + u_hi))


def _hgrn_fwd(z, lbl, nw):
    def body(hq_ref, hf_ref, hi_ref, hg_ref, lbl_ref, nw_ref, oraw_ref, og_ref, sh_ref, st_ref):
        @pl.when(pl.program_id(0) == 0)
        def _():
            st_ref[...] = jnp.zeros_like(st_ref)

        lb_all = _lower_bound(lbl_ref[...])
        rows = lax.broadcasted_iota(jnp.int32, (CH, HK), 0)
        rows8 = lax.broadcasted_iota(jnp.int32, (SUB, HK), 0)
        nwv = nw_ref[...]
        for cc, h in [(cc, h) for cc in range(HSTEP) for h in range(HEADS)]:
            rs = slice(CH * cc, CH * (cc + 1))
            sl = slice(HK * h, HK * (h + 1))
            lb = lb_all[:, sl]
            hq, hf, v, hg = hq_ref[rs, sl], hf_ref[rs, sl], hi_ref[rs, sl], hg_ref[rs, sl]
            q = hq * _sigmoid(hq)
            f = lb + (1.0 - lb) * _sigmoid(hf)
            k = 1.0 - f
            b = _cumsum_rows(jnp.log(f), rows)
            sh_ref[cc, h] = st_ref[h]
            o = _bdot(q * jnp.exp(b), st_ref[h], _NT) + _intra_fwd(q, k, v, b, rows8)
            bl = b[CH - 1:CH]
            st_ref[h] = st_ref[h] * jnp.exp(bl)
            st_ref[h] += _bdot(v, k * jnp.exp(bl - b), _TN)
            oraw_ref[rs, sl] = o
            nrm = o * lax.rsqrt(jnp.mean(o * o, axis=1, keepdims=True) + EPS)
            og_ref[rs, sl] = (nrm * nwv * (hg * _sigmoid(hg))).astype(BF)

    zblk = lambda c: pl.BlockSpec((CH * HSTEP, D), lambda i, c=c: (i, c))
    return pl.pallas_call(
        body, name="hgrn_fwd", grid=(NCH // HSTEP,),
        in_specs=[zblk(0), zblk(1), zblk(2), zblk(3),
                  pl.BlockSpec((2, D), lambda i: (0, 0)), pl.BlockSpec((1, HK), lambda i: (0, 0))],
        out_specs=[zblk(0), zblk(0),
                   pl.BlockSpec((HSTEP, HEADS, HK, HK), lambda i: (i, 0, 0, 0))],
        out_shape=[jax.ShapeDtypeStruct((T, D), F32), jax.ShapeDtypeStruct((T, D), BF),
                   jax.ShapeDtypeStruct((NCH, HEADS, HK, HK), F32)],
        scratch_shapes=[pltpu.VMEM((HEADS, HK, HK), F32)],
        compiler_params=_params(("arbitrary",)),
    )(z, z, z, z, lbl, nw)


def _hgrn_bwd(z, lbl, nw, oraw, dog, shist, side=None):
    hstep = 1
    s_arrays, s_in_specs, s_shapes, s_out_specs, s_sems = _side_io(side)
    na, no = len(s_arrays), len(s_shapes)

    def body(*refs):
        hq_ref, hf_ref, hi_ref, hg_ref, lbl_ref, nw_ref, oraw_ref, dog_ref, sh_ref = refs[:9]
        s_ins = refs[9:9 + na]
        dz_ref, dlb_ref, dnw_ref = refs[9 + na:12 + na]
        s_outs = refs[12 + na:12 + na + no]
        dst_ref = refs[12 + na + no]
        s_sem_refs = refs[13 + na + no:]

        @pl.when(pl.program_id(0) == 0)
        def _():
            dst_ref[...] = jnp.zeros_like(dst_ref)
            dlb_ref[...] = jnp.zeros_like(dlb_ref)
            dnw_ref[...] = jnp.zeros_like(dnw_ref)
            if side is not None:
                side.first(s_ins, s_outs, s_sem_refs)

        lb_all = _lower_bound(lbl_ref[...])
        rows = lax.broadcasted_iota(jnp.int32, (CH, HK), 0)
        rowc = lax.broadcasted_iota(jnp.int32, (CH, 1), 0)
        rows8 = lax.broadcasted_iota(jnp.int32, (SUB, HK), 0)
        rowc8 = lax.broadcasted_iota(jnp.int32, (SUB, 1), 0)
        nwv = nw_ref[...]
        dnw = jnp.zeros((1, HK), F32)
        for cc, h in [(cc, h) for cc in reversed(range(hstep)) for h in range(HEADS)]:
            rs = slice(CH * cc, CH * (cc + 1))
            sl = slice(HK * h, HK * (h + 1))
            lb = lb_all[:, sl]
            hq, hf, v, hg = hq_ref[rs, sl], hf_ref[rs, sl], hi_ref[rs, sl], hg_ref[rs, sl]
            o, dg_out = oraw_ref[rs, sl], dog_ref[rs, sl]
            sg = _sigmoid(hg)
            sil = hg * sg
            r = lax.rsqrt(jnp.mean(o * o, axis=1, keepdims=True) + EPS)
            nrm = o * r
            d_hg = dg_out * (nrm * nwv) * (sg * (1.0 + hg * (1.0 - sg)))
            dn = dg_out * nwv * sil
            dnw = dnw + jnp.sum(dg_out * nrm * sil, axis=0, keepdims=True)
            do = r * (dn - nrm * jnp.mean(dn * nrm, axis=1, keepdims=True))
            sq = _sigmoid(hq)
            q = hq * sq
            sig = _sigmoid(hf)
            f = lb + (1.0 - lb) * sig
            k = 1.0 - f
            b = _cumsum_rows(jnp.log(f), rows)
            eb = jnp.exp(b)
            qe = q * eb
            bl = b[CH - 1:CH]
            ebl = jnp.exp(bl)
            kdec = jnp.exp(bl - b)
            ke = k * kdec
            dqe = _bdot(do, sh_ref[cc, h])
            dq = dqe * eb
            db = dqe * qe
            dke = _bdot(v, dst_ref[h])
            dv = _bdot(ke, dst_ref[h], _NT)
            dk = dke * kdec
            rr = dke * ke
            db = db - rr
            db_last = (jnp.sum(rr, axis=0, keepdims=True)
                       + ebl * jnp.sum(dst_ref[h] * sh_ref[cc, h], axis=0, keepdims=True))
            dst_ref[h] = dst_ref[h] * ebl
            dst_ref[h] += _bdot(do, qe, _TN)
            dq_i, dk_i, dv_i, db_i = _intra_bwd(q, k, v, b, do, rows8, rowc8)
            dq, dk, dv = dq + dq_i, dk + dk_i, dv + dv_i
            db = db + db_i + (rowc == CH - 1).astype(F32) * db_last
            dgl = _rev_cumsum_rows(db, rows)
            df = dgl / f - dk
            dlb_ref[:, sl] += jnp.sum(df * (1.0 - sig), axis=0, keepdims=True)
            dz_ref[rs, sl] = (dq * (sq * (1.0 + hq * (1.0 - sq)))).astype(BF)
            dz_ref[rs, D + HK * h:D + HK * (h + 1)] = (df * (1.0 - lb) * sig * (1.0 - sig)).astype(BF)
            dz_ref[rs, 2 * D + HK * h:2 * D + HK * (h + 1)] = dv.astype(BF)
            dz_ref[rs, 3 * D + HK * h:3 * D + HK * (h + 1)] = d_hg.astype(BF)
        dnw_ref[...] += dnw
        if side is not None:
            @pl.when(pl.program_id(0) == NCH // hstep - 1)
            def _():
                side.last(s_ins, s_outs, s_sem_refs)

    rev = lambda i: NCH // hstep - 1 - i
    zblk = lambda c: pl.BlockSpec((CH * hstep, D), lambda i, c=c: (rev(i), c))
    out = pl.pallas_call(
        body, name="hgrn_bwd", grid=(NCH // hstep,),
        in_specs=[zblk(0), zblk(1), zblk(2), zblk(3),
                  pl.BlockSpec((2, D), lambda i: (0, 0)), pl.BlockSpec((1, HK), lambda i: (0, 0)),
                  zblk(0), zblk(0),
                  pl.BlockSpec((hstep, HEADS, HK, HK), lambda i: (rev(i), 0, 0, 0))] + s_in_specs,
        out_specs=[pl.BlockSpec((CH * hstep, 4 * D), lambda i: (rev(i), 0)),
                   pl.BlockSpec((1, D), lambda i: (0, 0)), pl.BlockSpec((1, HK), lambda i: (0, 0))] + s_out_specs,
        out_shape=[jax.ShapeDtypeStruct((T, 4 * D), BF), jax.ShapeDtypeStruct((1, D), F32),
                   jax.ShapeDtypeStruct((1, HK), F32)] + s_shapes,
        scratch_shapes=[pltpu.VMEM((HEADS, HK, HK), F32)] + s_sems,
        compiler_params=_params(("arbitrary",)),
    )(z, z, z, z, lbl, nw, oraw, dog, shist, *s_arrays)
    return out[0], out[1], out[2], out[3:]


BLK = 128
NBLK = T // BLK
QK_SCALE = 0.125


def _head_masks():
    lane = lax.broadcasted_iota(jnp.int32, (1, BLK), 1)
    return [(lane < 64).astype(F32), (lane >= 64).astype(F32)]


def _pieces(dil):
    m = T // dil
    out = []
    for r in range(dil):
        for j in range(m // BLK):
            start = r + dil * BLK * j
            rows = pl.ds(start, BLK, stride=dil) if dil > 1 else pl.ds(start, BLK)
            out.append((rows, r * m + BLK * j))
    return out


def _rope(x, c, sa, sb):
    return x * c + pltpu.roll(x, 96, axis=1) * sa + pltpu.roll(x, 32, axis=1) * sb


def _rope_t(d, c, sa, sb):
    return d * c + pltpu.roll(d * sa, 32, axis=1) + pltpu.roll(d * sb, 96, axis=1)


def _rope_and_regroup(dil, q_ref, k_ref, v_ref, tables, stage_q, stage_k, qr_ref, kr_ref, vr_ref):
    cos_ref, sa_ref, sb_ref = tables
    to_q, to_k = (qr_ref, kr_ref) if dil == 1 else (stage_q, stage_k)
    for c in range(T // BLK):
        rows = pl.ds(BLK * c, BLK)
        cs, sa, sb = cos_ref[rows, :], sa_ref[rows, :], sb_ref[rows, :]
        to_q[rows, :] = (_rope(q_ref[rows, :], cs, sa, sb) * QK_SCALE).astype(to_q.dtype)
        to_k[rows, :] = _rope(k_ref[rows, :], cs, sa, sb).astype(to_k.dtype)
    for rows, dst in _pieces(dil):
        drows = pl.ds(dst, BLK)
        if dil > 1:
            qr_ref[drows, :] = stage_q[rows, :].astype(qr_ref.dtype)
            kr_ref[drows, :] = stage_k[rows, :].astype(kr_ref.dtype)
        vr_ref[drows, :] = v_ref[rows, :].astype(vr_ref.dtype)


def _window_bias(bias_ref):
    ii = lax.broadcasted_iota(jnp.int32, (2 * BLK, BLK), 0) % BLK
    jj = lax.broadcasted_iota(jnp.int32, (2 * BLK, BLK), 1)
    bias_ref[0] = jnp.where(jj <= ii, 0.0, -jnp.inf)
    bias_ref[1] = jnp.where(jj >= ii, 0.0, -jnp.inf)


def _blocks(bi):
    if isinstance(bi, int):
        return pl.ds(bi * BLK, BLK), pl.ds(max(bi - 1, 0) * BLK, BLK)
    return (pl.ds(pl.multiple_of(bi * BLK, BLK), BLK),
            pl.ds(pl.multiple_of(jnp.maximum(bi - 1, 0) * BLK, BLK), BLK))


def _stack_heads(x, masks):
    return jnp.concatenate([x * masks[0].astype(x.dtype), x * masks[1].astype(x.dtype)], axis=0).astype(BF)


def _attn_fwd(z, cos, sa, sb):
    def body(q_ref, k_ref, v_ref, ag_ref, cos_ref, sa_ref, sb_ref, ob_ref, opre_ref, lse_ref, qr_ref, kr_ref, vr_ref,
             bias_ref, og_ref, lg_ref, otok_ref, ltok_ref, sc_ref):
        g = pl.program_id(1)
        masks = _head_masks()

        @pl.when(g == 0)
        def _():
            _window_bias(bias_ref)

        def group(gi):
            dil = ATT_GROUPS[gi][1]
            nblk = (T // dil) // BLK
            _rope_and_regroup(dil, q_ref, k_ref, v_ref, (cos_ref, sa_ref, sb_ref), lg_ref.at[0], lg_ref.at[1],
                              qr_ref, kr_ref, vr_ref)

            def scores(bi, slot):
                cur, prev = _blocks(bi)
                q2 = _stack_heads(qr_ref[cur, :], masks)
                sc_ref[slot, 0] = _dot(q2, kr_ref[cur, :], _NT) + bias_ref[0]
                if nblk > 1:
                    sc_ref[slot, 1] = (_dot(q2, kr_ref[prev, :], _NT)
                                       + (bias_ref[1] + jnp.where((bi % nblk) != 0, 0.0, -jnp.inf)))

            def finish(bi, slot):
                cur, prev = _blocks(bi)
                s_c, vc = sc_ref[slot, 0], vr_ref[cur, :]
                if nblk > 1:
                    s_p, vp = sc_ref[slot, 1], vr_ref[prev, :]
                    mx = jnp.max(jnp.maximum(s_c, s_p), axis=1, keepdims=True)
                    p_c, p_p = jnp.exp(s_c - mx), jnp.exp(s_p - mx)
                    den = jnp.sum(p_c + p_p, axis=1, keepdims=True)
                    oh = _dot(p_c.astype(BF), vc) + _dot(p_p.astype(BF), vp)
                else:
                    mx = jnp.max(s_c, axis=1, keepdims=True)
                    p_c = jnp.exp(s_c - mx)
                    den = jnp.sum(p_c, axis=1, keepdims=True)
                    oh = _dot(p_c.astype(BF), vc)
                on = oh / den
                lsev = jnp.broadcast_to(mx + jnp.log(den), (2 * BLK, BLK))
                og_ref[cur, :] = on[:BLK] * masks[0] + on[BLK:] * masks[1]
                lg_ref[0, cur, :] = lsev[:BLK]
                lg_ref[1, cur, :] = lsev[BLK:]

            def pair(j, carry):
                finish(2 * j, 0)
                scores(2 * j + 1, 1)
                finish(2 * j + 1, 1)
                scores(jnp.minimum(2 * j + 2, NBLK - 1), 0)
                return carry

            scores(0, 0)
            lax.fori_loop(0, NBLK // 2, pair, 0)
            for rows, src in _pieces(dil):
                srows = pl.ds(src, BLK)
                otok_ref[gi, rows, :] = og_ref[srows, :]
                ltok_ref[gi, 0, rows, :] = lg_ref[0, srows, :]
                ltok_ref[gi, 1, rows, :] = lg_ref[1, srows, :]

        for gi in range(3):
            pl.when(g == gi)(functools.partial(group, gi))

        @pl.when(g == 2)
        def _():
            for c in range(T // BLK):
                rows = pl.ds(BLK * c, BLK)
                wts = []
                for hh in range(2):
                    l0, l1, l2 = ltok_ref[0, hh, rows, :], ltok_ref[1, hh, rows, :], ltok_ref[2, hh, rows, :]
                    mx = jnp.maximum(jnp.maximum(l0, l1), l2)
                    e0, e1, e2 = jnp.exp(l0 - mx), jnp.exp(l1 - mx), jnp.exp(l2 - mx)
                    tot = e0 + e1 + e2
                    lse_ref[rows, BLK * hh:BLK * (hh + 1)] = mx + jnp.log(tot)
                    inv = 1.0 / tot
                    wts.append([e0 * inv, e1 * inv, e2 * inv])
                o = sum((wts[0][gi] * masks[0] + wts[1][gi] * masks[1]) * otok_ref[gi, rows, :] for gi in range(3))
                ag = ag_ref[rows, :]
                opre_ref[rows, :] = o
                ob_ref[rows, :] = (o * (ag * _sigmoid(ag))).astype(BF)

    c0 = ATT_COL0 // BLK
    zspec = lambda part: pl.BlockSpec((T, BLK), lambda p, g, part=part: (0, c0 + 12 * part + 4 * g + p))
    outspec = pl.BlockSpec((T, BLK), lambda p, g: (0, p))
    table = pl.BlockSpec((T, BLK), lambda p, g: (0, 0))
    regrouped = pl.BlockSpec((None, T, BLK), lambda p, g: (g, 0, p))
    big = lambda: pltpu.VMEM((T, BLK), F32)
    return pl.pallas_call(
        body, name="attn_fwd", grid=(4, 3),
        in_specs=[zspec(0), zspec(1), zspec(2),
                  pl.BlockSpec((T, BLK), lambda p, g: (0, AG_COL0 // BLK + p)), table, table, table],
        out_specs=[outspec, outspec, pl.BlockSpec((T, 2 * BLK), lambda p, g: (0, p)), regrouped, regrouped, regrouped],
        out_shape=[jax.ShapeDtypeStruct((T, 512), BF), jax.ShapeDtypeStruct((T, 512), F32),
                   jax.ShapeDtypeStruct((T, 8 * BLK), F32)] + [jax.ShapeDtypeStruct((3, T, 512), BF)] * 3,
        scratch_shapes=[pltpu.VMEM((2, 2 * BLK, BLK), F32), big(),
                        pltpu.VMEM((2, T, BLK), F32), pltpu.VMEM((3, T, BLK), F32), pltpu.VMEM((3, 2, T, BLK), F32),
                        pltpu.VMEM((2, 2, 2 * BLK, BLK), F32)],
        compiler_params=_params(("parallel", "arbitrary")),
    )(z, z, z, z, cos, sa, sb)


def _attn_bwd(z, qs, ks, vs, cos, sa, sb, opre, lse, dob):
    def body(qs_ref, ks_ref, vs_ref, ag_ref, cos_ref, sa_ref, sb_ref, o_ref, lse0_ref, lse1_ref, dob_ref,
             dq_ref, dk_ref, dv_ref, dag_ref,
             bias_ref, dtok_ref, qr_ref, kr_ref, vr_ref, dor_ref, lr_ref, dr_ref,
             dqr_ref, dkr_ref, dvr_ref, pd_ref, dotok_ref):
        g = pl.program_id(1)
        masks = _head_masks()

        @pl.when(g == 0)
        def _():
            _window_bias(bias_ref)
            for c in range(T // BLK):
                rows = pl.ds(BLK * c, BLK)
                ag, dob_v, o = ag_ref[rows, :], dob_ref[rows, :], o_ref[rows, :]
                sg = _sigmoid(ag)
                dag_ref[rows, :] = (dob_v * o * (sg * (1.0 + ag * (1.0 - sg)))).astype(BF)
                do = dob_v * (ag * sg)
                dotok_ref[rows, :] = do
                prod = do * o
                for hh, mh in enumerate(masks):
                    dtok_ref[hh, rows, :] = jnp.broadcast_to(jnp.sum(prod * mh, axis=1, keepdims=True), (BLK, BLK))

        def group(gi):
            dil = ATT_GROUPS[gi][1]
            nblk = (T // dil) // BLK
            for rows, dst in _pieces(dil):
                drows = pl.ds(dst, BLK)
                dor_ref[drows, :] = dotok_ref[rows, :]
                for hh, lse_ref in enumerate((lse0_ref, lse1_ref)):
                    lr_ref[hh, drows, :] = lse_ref[rows, :]
                    dr_ref[hh, drows, :] = dtok_ref[hh, rows, :]
            dkr_ref[...] = jnp.zeros_like(dkr_ref)
            dvr_ref[...] = jnp.zeros_like(dvr_ref)

            def probs(bi, slot):
                cur, prev = _blocks(bi)
                q2, do2 = _stack_heads(qs_ref[cur, :], masks), _stack_heads(dor_ref[cur, :], masks)
                lh = jnp.concatenate([lr_ref[0, cur, :], lr_ref[1, cur, :]], axis=0)
                dh = jnp.concatenate([dr_ref[0, cur, :], dr_ref[1, cur, :]], axis=0)
                p_c = jnp.exp(_dot(q2, ks_ref[cur, :], _NT) + bias_ref[0] - lh)
                pd_ref[slot, 0] = p_c.astype(BF)
                pd_ref[slot, 1] = (p_c * (_dot(do2, vs_ref[cur, :], _NT) - dh)).astype(BF)
                if nblk > 1:
                    bias_p = bias_ref[1] + jnp.where((bi % nblk) != 0, 0.0, -jnp.inf)
                    p_p = jnp.exp(_dot(q2, ks_ref[prev, :], _NT) + bias_p - lh)
                    pd_ref[slot, 2] = p_p.astype(BF)
                    pd_ref[slot, 3] = (p_p * (_dot(do2, vs_ref[prev, :], _NT) - dh)).astype(BF)

            def grads(bi, slot):
                cur, prev = _blocks(bi)
                q2, do2 = _stack_heads(qs_ref[cur, :], masks), _stack_heads(dor_ref[cur, :], masks)
                p_c, ds_c = pd_ref[slot, 0], pd_ref[slot, 1]
                dq2 = _dot(ds_c, ks_ref[cur, :])
                dkr_ref[cur, :] += _dot(ds_c, q2, _TN)
                dvr_ref[cur, :] += _dot(p_c, do2, _TN)
                if nblk > 1:
                    p_p, ds_p = pd_ref[slot, 2], pd_ref[slot, 3]
                    dq2 = dq2 + _dot(ds_p, ks_ref[prev, :])
                    dkr_ref[prev, :] += _dot(ds_p, q2, _TN)
                    dvr_ref[prev, :] += _dot(p_p, do2, _TN)
                dqr_ref[cur, :] = dq2[:BLK] * masks[0] + dq2[BLK:] * masks[1]

            def pair(j, carry):
                grads(2 * j, 0)
                probs(2 * j + 1, 1)
                grads(2 * j + 1, 1)
                probs(jnp.minimum(2 * j + 2, NBLK - 1), 0)
                return carry

            probs(0, 0)
            lax.fori_loop(0, NBLK // 2, pair, 0)
            if dil > 1:
                for rows, src in _pieces(dil):
                    srows = pl.ds(src, BLK)
                    qr_ref[rows, :] = dqr_ref[srows, :]
                    kr_ref[rows, :] = dkr_ref[srows, :]
                    vr_ref[rows, :] = dvr_ref[srows, :]
            tq, tk, tv = (qr_ref, kr_ref, vr_ref) if dil > 1 else (dqr_ref, dkr_ref, dvr_ref)
            for c in range(T // BLK):
                rows = pl.ds(BLK * c, BLK)
                cs, sa, sb = cos_ref[rows, :], sa_ref[rows, :], sb_ref[rows, :]
                dq_ref[rows, :] = _rope_t(tq[rows, :] * QK_SCALE, cs, sa, sb).astype(BF)
                dk_ref[rows, :] = _rope_t(tk[rows, :], cs, sa, sb).astype(BF)
                dv_ref[rows, :] = tv[rows, :].astype(BF)

        for gi in range(3):
            pl.when(g == gi)(functools.partial(group, gi))

    regrouped = pl.BlockSpec((None, T, BLK), lambda p, g: (g, 0, p))
    pspec = pl.BlockSpec((T, BLK), lambda p, g: (0, p))
    gspec = pl.BlockSpec((T, BLK), lambda p, g: (0, 4 * g + p))
    table = pl.BlockSpec((T, BLK), lambda p, g: (0, 0))
    big = lambda: pltpu.VMEM((T, BLK), F32)
    two = lambda: pltpu.VMEM((2, T, BLK), F32)
    return pl.pallas_call(
        body, name="attn_bwd", grid=(4, 3),
        in_specs=[regrouped, regrouped, regrouped,
                  pl.BlockSpec((T, BLK), lambda p, g: (0, AG_COL0 // BLK + p)), table, table, table,
                  pspec, pl.BlockSpec((T, BLK), lambda p, g: (0, 2 * p)),
                  pl.BlockSpec((T, BLK), lambda p, g: (0, 2 * p + 1)), pspec],
        out_specs=[gspec, gspec, gspec, pspec],
        out_shape=[jax.ShapeDtypeStruct((T, 1536), BF), jax.ShapeDtypeStruct((T, 1536), BF),
                   jax.ShapeDtypeStruct((T, 1536), BF), jax.ShapeDtypeStruct((T, 512), BF)],
        scratch_shapes=[pltpu.VMEM((2, 2 * BLK, BLK), F32), two(), big(), big(), big(), big(),
                        two(), two(), big(), big(), big(), pltpu.VMEM((2, 4, 2 * BLK, BLK), BF), big()],
        compiler_params=_params(("parallel", "arbitrary")),
    )(qs, ks, vs, z, cos, sa, sb, opre, lse, lse, dob)


def _merge_out_loss(og, ob, z, w_a, w_b, w_out, x, tgt, wf):
    tm = 256

    def body(og_ref, ob_ref, ga_ref, gb_ref, wa_ref, wb_ref, wo_ref, x_ref, t_ref, wf_ref,
             ya_ref, yb_ref, m_ref, dout_ref, loss_ref, gwf_ref):
        @pl.when(pl.program_id(0) == 0)
        def _():
            loss_ref[...] = jnp.zeros_like(loss_ref)
            gwf_ref[...] = jnp.zeros_like(gwf_ref)

        ya, yb = _dot(og_ref[...], wa_ref[...]), _dot(ob_ref[...], wb_ref[...])
        ya_ref[...] = ya
        yb_ref[...] = yb
        m = (_sigmoid(ga_ref[...]) * ya + _sigmoid(gb_ref[...]) * yb).astype(BF)
        m_ref[...] = m
        out = x_ref[...] + _dot(m, wo_ref[...])
        r = lax.rsqrt(jnp.mean(out * out, axis=-1, keepdims=True) + EPS)
        yh = out * r
        wfv = wf_ref[...]
        err = yh * wfv - t_ref[...]
        loss_ref[...] += jnp.sum(err * err, axis=0, keepdims=True) * (0.5 / D)
        dy = err * (1.0 / D)
        gwf_ref[...] += jnp.sum(dy * yh, axis=0, keepdims=True)
        dyh = dy * wfv
        dout_ref[...] = r * (dyh - yh * jnp.mean(dyh * yh, axis=-1, keepdims=True))

    row = pl.BlockSpec((tm, D), lambda i: (i, 0))
    vec = pl.BlockSpec((1, D), lambda i: (0, 0))
    whole = lambda w: pl.BlockSpec(w.shape, lambda i: (0, 0))
    return pl.pallas_call(
        body, name="merge_out_loss", grid=(T // tm,),
        in_specs=[row, pl.BlockSpec((tm, ob.shape[1]), lambda i: (i, 0)),
                  pl.BlockSpec((tm, D), lambda i: (i, GATE_COL0 // D)),
                  pl.BlockSpec((tm, D), lambda i: (i, GATE_COL0 // D + 1)),
                  whole(w_a), whole(w_b), whole(w_out), row, row, vec],
        out_specs=[row, row, row, row, vec, vec],
        out_shape=[jax.ShapeDtypeStruct((T, D), F32), jax.ShapeDtypeStruct((T, D), F32),
                   jax.ShapeDtypeStruct((T, D), BF), jax.ShapeDtypeStruct((T, D), F32),
                   jax.ShapeDtypeStruct((1, D), F32), jax.ShapeDtypeStruct((1, D), F32)],
        compiler_params=_params(("arbitrary",)),
    )(og, ob, z, z, w_a, w_b, w_out, x, tgt, wf)


def _merge_proj_bwd(dout, ya, yb, z, w_a, w_b, w_out):
    tm = 256

    def body(dout_ref, ya_ref, yb_ref, ga_ref, gb_ref, wa_ref, wb_ref, wo_ref,
             dya_ref, dyb_ref, dg_ref, dog_ref, dob_ref):
        dmv = _dot(dout_ref[...].astype(BF), wo_ref[...], _NT)
        sa, sb = _sigmoid(ga_ref[...]), _sigmoid(gb_ref[...])
        dya, dyb = (sa * dmv).astype(BF), (sb * dmv).astype(BF)
        dya_ref[...] = dya
        dyb_ref[...] = dyb
        dg_ref[:, :D] = (dmv * ya_ref[...] * sa * (1.0 - sa)).astype(BF)
        dg_ref[:, D:] = (dmv * yb_ref[...] * sb * (1.0 - sb)).astype(BF)
        dog_ref[...] = _dot(dya, wa_ref[...], _NT)
        dob_ref[...] = _dot(dyb, wb_ref[...], _NT)

    row = pl.BlockSpec((tm, D), lambda i: (i, 0))
    whole = lambda w: pl.BlockSpec(w.shape, lambda i: (0, 0))
    nb = w_b.shape[0]
    return pl.pallas_call(
        body, name="merge_proj_bwd", grid=(T // tm,),
        in_specs=[row, row, row, pl.BlockSpec((tm, D), lambda i: (i, GATE_COL0 // D)),
                  pl.BlockSpec((tm, D), lambda i: (i, GATE_COL0 // D + 1)), whole(w_a), whole(w_b), whole(w_out)],
        out_specs=[row, row, pl.BlockSpec((tm, 2 * D), lambda i: (i, 0)), row,
                   pl.BlockSpec((tm, nb), lambda i: (i, 0))],
        out_shape=[jax.ShapeDtypeStruct((T, D), BF), jax.ShapeDtypeStruct((T, D), BF),
                   jax.ShapeDtypeStruct((T, 2 * D), BF), jax.ShapeDtypeStruct((T, D), F32),
                   jax.ShapeDtypeStruct((T, nb), F32)],
        compiler_params=_params(("parallel",)),
    )(dout, ya, yb, z, z, w_a, w_b, w_out)


def _rope_inv_freq():
    inv = ROPE_THETA ** (-jnp.arange(0, 64, 2, dtype=F32) / 64)
    return jnp.tile(inv, 4).reshape(1, BLK)


def _local_step(x, pos, norm_w, lbl, hnw, wf, tgt, w_in, w_a, w_b, w_out, shard_shapes=()):
    invf = _rope_inv_freq()
    if shard_shapes:
        blk = jnp.reshape(2 * lax.axis_index("x") + lax.axis_index("y"), (1,)).astype(jnp.int32)
        h, cos, sa, sb, z_own, (w_near,) = _norm_and_rope_tables(
            x, norm_w, pos, invf, side=_gather_near_side(w_in, WEIGHT_AXES[0]), own=(w_in, blk))
        near = jnp.concatenate([blk ^ 2, blk ^ 1])
        z, (w_diag,) = _z_blocks(h, w_near, z_own, jnp.concatenate([near, near]), 2, name="z_proj_near",
                                 side=_gather_diag_side(w_near, w_in.shape, WEIGHT_AXES[0]))
        z, w_in, (w_a, w_b, w_out) = _z_blocks(
            h, w_diag, z, jnp.concatenate([blk ^ 3, jnp.zeros_like(blk)]), 1, name="z_proj_diag", fill=w_near,
            side=_gather_side([w_a, w_b, w_out], WEIGHT_AXES[1:]))
    else:
        h, cos, sa, sb, _, _ = _norm_and_rope_tables(x, norm_w, pos, invf)
        z = _matmul(h, w_in, tm=T, tn=512, name="z_proj")
    oraw, og, shist = _hgrn_fwd(z, lbl, hnw)
    ob, opre, lse, qs, ks, vs = _attn_fwd(z, cos, sa, sb)
    ya, yb, merged, dout, loss_vec, g_wf = _merge_out_loss(og, ob, z, w_a, w_b, w_out, x, tgt, wf)

    dya, dyb, dgates, dog, dob = _merge_proj_bwd(dout, ya, yb, z, w_a, w_b, w_out)
    g_wout = _matmul(merged, dout, ta=True, out_dtype=BF, tm=512, tn=1024, name="g_wout")
    g_wa = _matmul(og, dya, ta=True, out_dtype=BF, tm=512, tn=1024, name="g_wa")
    g_wb = _matmul(ob, dyb, ta=True, out_dtype=BF, tm=512, tn=1024, name="g_wb")
    small = [g_wa, g_wb, g_wout]
    side_s = side_w = None
    if shard_shapes:
        p3_s = _rs_partials(small, shard_shapes[1:], WEIGHT_AXES[1:], "small")
        side_s = _chip_exchange_side(p3_s, shard_shapes[1:], WEIGHT_AXES[1:])
    dz_h, dlb, g_hnw, land_s = _hgrn_bwd(z, lbl, hnw, oraw, dog, shist, side=side_s)
    dq, dk, dv, dag = _attn_bwd(z, qs, ks, vs, cos, sa, sb, opre, lse, dob)
    dz_parts = [dz_h, dq, dk, dv, dag, dgates]
    if shard_shapes:
        c = lax.axis_index("c")
        half = lambda i: jnp.reshape(i, (1,)).astype(jnp.int32)
        g_send = _grad_w_in_half(h, dz_parts, half(1 - c))
        g_keep, (g_sib,) = _grad_w_in_half(h, dz_parts, half(c), side=_sibling_send_side(g_send))
        p3_w = [_add_bf16(g_keep, g_sib, "pair_sum_w_in").reshape(1, D // 2, NIN)]
        side_w = _chip_exchange_side(p3_w, shard_shapes[:1], WEIGHT_AXES[:1])
    else:
        g_big = [_grad_w_in(h, dz_parts)] + small
    gx, g_nw, land_w = _grad_x(dz_parts, w_in, x, dout, norm_w, side=side_w)
    if shard_shapes:
        g_big = _rs_finish(p3_w + p3_s, list(land_w) + list(land_s), shard_shapes, WEIGHT_AXES)
    return dict(loss_vec=loss_vec, gx=gx, g_nw=g_nw, dlb=dlb, g_hnw=g_hnw, g_wf=g_wf,
                g_win=g_big[0], g_wa=g_big[1], g_wb=g_big[2], g_wout=g_big[3])


MESH = pl.DeviceIdType.MESH
HBM = pl.BlockSpec(memory_space=pl.ANY)
WEIGHT_AXES = (1, 0, 1, 0)


def _place():
    x, y, c = lax.axis_index("x"), lax.axis_index("y"), lax.axis_index("c")
    chips = [(1 - x, y), (x, 1 - y), (1 - x, 1 - y)]
    return x, y, c, chips


def _block_half(ref, shard_shape, axis, j, half):
    r, c = shard_shape
    hr = r // 2
    if axis == 0:
        return ref.at[pl.ds(pl.multiple_of(j * r + half * hr, 16), hr), :]
    return ref.at[pl.ds(pl.multiple_of(half * hr, 16), hr), pl.ds(pl.multiple_of(j * c, 128), c)]


class _Side:
    def __init__(self, arrays, out_shapes, sems, first, last):
        self.arrays, self.out_shapes, self.sems, self.first, self.last = arrays, out_shapes, sems, first, last


def _gather_side(shards, axes):
    n = len(shards)
    shapes = [s.shape for s in shards]

    def copies(ins, outs, sems):
        send1, recv1, send2, recv2, send0, recv0 = sems
        x, y, c, chips = _place()
        me = 2 * x + y
        sib = (x, y, 1 - c)
        near = ((1 - c) * (1 - x) + c * x, (1 - c) * y + c * (1 - y))
        far = ((1 - c) * x + c * (1 - x), (1 - c) * (1 - y) + c * y)
        out = []
        for a in range(n):
            r, cc = shapes[a]
            mine = (outs[a].at[pl.ds(pl.multiple_of(me * r, 16), r), :] if axes[a] == 0
                    else outs[a].at[:, pl.ds(pl.multiple_of(me * cc, 128), cc)])
            own = pltpu.make_async_remote_copy(
                src_ref=ins[a], dst_ref=mine, send_sem=send0.at[a], recv_sem=recv0.at[a],
                device_id=sib, device_id_type=MESH)
            src = ins[a].at[pl.ds(pl.multiple_of(c * (r // 2), 16), r // 2), :]
            sends = [pltpu.make_async_remote_copy(
                src_ref=src, dst_ref=_block_half(outs[a], shapes[a], axes[a], me, c),
                send_sem=send1.at[a, k], recv_sem=recv1.at[a, k], device_id=(*chips[k], c), device_id_type=MESH)
                for k in range(2)]

            def region(chip, half):
                return _block_half(outs[a], shapes[a], axes[a], 2 * chip[0] + chip[1], half)

            def arrival(chip, k):
                reg = region(chip, c)
                return pltpu.make_async_remote_copy(
                    src_ref=reg, dst_ref=reg, send_sem=send1.at[a, k], recv_sem=recv1.at[a, k],
                    device_id=(*chip, c), device_id_type=MESH)

            def to_sibling(chip, k):
                reg = region(chip, c)
                return pltpu.make_async_remote_copy(
                    src_ref=reg, dst_ref=reg, send_sem=send2.at[a, k], recv_sem=recv2.at[a, k],
                    device_id=sib, device_id_type=MESH)

            def from_sibling(chip, k):
                reg = region(chip, 1 - c)
                return pltpu.make_async_remote_copy(
                    src_ref=reg, dst_ref=reg, send_sem=send2.at[a, k], recv_sem=recv2.at[a, k],
                    device_id=sib, device_id_type=MESH)

            relay = pltpu.make_async_remote_copy(
                src_ref=region(near, c), dst_ref=region(near, c), send_sem=send1.at[a, 2], recv_sem=recv1.at[a, 2],
                device_id=(*far, c), device_id_type=MESH)
            hops = [(arrival(near, c), to_sibling(near, c)), (arrival(far, 1 - c), to_sibling(far, 1 - c)),
                    (arrival(chips[2], 2), to_sibling(chips[2], 2))]
            back = [from_sibling(chips[k], k) for k in range(3)]
            out.append((own, sends, relay, hops, back))
        return out

    def first(ins, outs, sems):
        for own, sends, _, _, _ in copies(ins, outs, sems):
            own.start()
            for cp in sends:
                cp.start()

    def last(ins, outs, sems):
        per_array = copies(ins, outs, sems)
        for step in range(3):
            for _, _, relay, hops, _ in per_array:
                arrived, onward = hops[step]
                arrived.wait_recv()
                if step == 0:
                    relay.start()
                onward.start()
        for own, sends, relay, hops, back in per_array:
            for cp in back:
                cp.wait_recv()
            for cp in sends + [relay] + [onward for _, onward in hops]:
                cp.wait_send()
            own.wait()

    full = [(4 * r, c) if ax == 0 else (r, 4 * c) for (r, c), ax in zip(shapes, axes)]
    sems = [pltpu.SemaphoreType.DMA((n, 3)), pltpu.SemaphoreType.DMA((n, 3)),
            pltpu.SemaphoreType.DMA((n, 3)), pltpu.SemaphoreType.DMA((n, 3)),
            pltpu.SemaphoreType.DMA((n,)), pltpu.SemaphoreType.DMA((n,))]
    return _Side(list(shards), [jax.ShapeDtypeStruct(f, BF) for f in full], sems, first, last)


def _gather_near_side(shard, axis):
    shape = shard.shape
    r, cc = shape

    def copies(ins, outs, sems):
        send1, recv1, send2, recv2, send0, recv0 = sems
        x, y, c, chips = _place()
        me = 2 * x + y
        sib = (x, y, 1 - c)
        mine = (outs[0].at[pl.ds(pl.multiple_of(me * r, 16), r), :] if axis == 0
                else outs[0].at[:, pl.ds(pl.multiple_of(me * cc, 128), cc)])
        own = pltpu.make_async_remote_copy(
            src_ref=ins[0], dst_ref=mine, send_sem=send0.at[0], recv_sem=recv0.at[0],
            device_id=sib, device_id_type=MESH)
        src = ins[0].at[pl.ds(pl.multiple_of(c * (r // 2), 16), r // 2), :]

        def region(k, half):
            return _block_half(outs[0], shape, axis, 2 * chips[k][0] + chips[k][1], half)

        def moves(k):
            return [pltpu.make_async_remote_copy(
                        src_ref=s, dst_ref=d, send_sem=ss.at[k], recv_sem=rs.at[k], device_id=dev,
                        device_id_type=MESH)
                    for s, d, ss, rs, dev in (
                        (src, _block_half(outs[0], shape, axis, me, c), send1, recv1, (*chips[k], c)),
                        (region(k, c), region(k, c), send1, recv1, (*chips[k], c)),
                        (region(k, c), region(k, c), send2, recv2, sib),
                        (region(k, 1 - c), region(k, 1 - c), send2, recv2, sib))]

        return own, [moves(k) for k in range(2)]

    def first(ins, outs, sems):
        own, per_chip = copies(ins, outs, sems)
        own.start()
        for send, _, _, _ in per_chip:
            send.start()

    def last(ins, outs, sems):
        own, per_chip = copies(ins, outs, sems)
        for _, arrived, onward, _ in per_chip:
            arrived.wait_recv()
            onward.start()
        for send, _, onward, back in per_chip:
            back.wait_recv()
            send.wait_send()
            onward.wait_send()
        own.wait()

    full = (4 * r, cc) if axis == 0 else (r, 4 * cc)
    sems = [pltpu.SemaphoreType.DMA((2,))] * 4 + [pltpu.SemaphoreType.DMA((1,))] * 2
    return _Side([shard], [jax.ShapeDtypeStruct(full, BF)], sems, first, last)


def _gather_diag_side(gathered, shape, axis):
    r, cc = shape

    def copies(ins, outs, sems):
        send1, recv1, send2, recv2 = sems
        x, y, c, _ = _place()
        sib = (x, y, 1 - c)
        near = ((1 - c) * (1 - x) + c * x, (1 - c) * y + c * (1 - y))
        far = ((1 - c) * x + c * (1 - x), (1 - c) * (1 - y) + c * y)

        def half(i):
            return outs[0].at[pl.ds(pl.multiple_of(i * (r // 2), 16), r // 2), :]

        def move(s, d, ss, rs, dev):
            return pltpu.make_async_remote_copy(
                src_ref=s, dst_ref=d, send_sem=ss.at[0], recv_sem=rs.at[0], device_id=dev, device_id_type=MESH)

        relay = move(_block_half(ins[0], shape, axis, 2 * near[0] + near[1], c), half(c), send1, recv1, (*far, c))
        arrived = move(half(c), half(c), send1, recv1, (*far, c))
        onward = move(half(c), half(c), send2, recv2, sib)
        back = move(half(1 - c), half(1 - c), send2, recv2, sib)
        return relay, arrived, onward, back

    def first(ins, outs, sems):
        copies(ins, outs, sems)[0].start()

    def last(ins, outs, sems):
        relay, arrived, onward, back = copies(ins, outs, sems)
        arrived.wait_recv()
        onward.start()
        back.wait_recv()
        relay.wait_send()
        onward.wait_send()

    return _Side([gathered], [jax.ShapeDtypeStruct(shape, BF)], [pltpu.SemaphoreType.DMA((1,))] * 4, first, last)


def _as3d(g, shard_shape, axis):
    r, c = shard_shape
    return g.reshape(4, r, c) if axis == 0 else g.reshape(1, r, 4 * c)


def _half_rows(ref3, hr, half):
    return ref3.at[:, pl.ds(pl.multiple_of(half * hr, 16), hr), :]


def _rs_pair_exchange(g3s, name):
    n = len(g3s)

    def body(*refs):
        ins, outs = refs[:n], refs[n:2 * n]
        send, recv = refs[2 * n:]
        x, y, c, _ = _place()
        cps = []
        for a in range(n):
            hr = g3s[a].shape[1] // 2
            cp = pltpu.make_async_remote_copy(
                src_ref=_half_rows(ins[a], hr, 1 - c), dst_ref=outs[a],
                send_sem=send.at[a], recv_sem=recv.at[a], device_id=(x, y, 1 - c), device_id_type=MESH)
            cp.start()
            cps.append(cp)
        for cp in cps:
            cp.wait()

    return pl.pallas_call(
        body, name=name,
        in_specs=[HBM] * n, out_specs=[HBM] * n,
        out_shape=[jax.ShapeDtypeStruct((g.shape[0], g.shape[1] // 2, g.shape[2]), BF) for g in g3s],
        scratch_shapes=[pltpu.SemaphoreType.DMA((n,)), pltpu.SemaphoreType.DMA((n,))],
    )(*g3s)


def _pair_sum(g3, land, cidx, name):
    nb, r, w = g3.shape
    hr = r // 2
    tr = 64

    def body(c_ref, g_ref, l_ref, o_ref):
        o_ref[...] = (g_ref[...].astype(F32) + l_ref[...].astype(F32)).astype(BF)

    blk = (nb, tr, w)
    return pl.pallas_call(
        body, name=name,
        grid_spec=pltpu.PrefetchScalarGridSpec(
            num_scalar_prefetch=1, grid=(hr // tr,),
            in_specs=[pl.BlockSpec(blk, lambda i, c: (0, c[0] * (hr // tr) + i, 0)),
                      pl.BlockSpec(blk, lambda i, c: (0, i, 0))],
            out_specs=pl.BlockSpec(blk, lambda i, c: (0, i, 0))),
        out_shape=jax.ShapeDtypeStruct((nb, hr, w), BF),
        compiler_params=_params(("parallel",)),
    )(cidx, g3, land)


def _chip_exchange_side(p3s, shapes, axes):
    n = len(p3s)

    def copies(ins, outs, sems):
        send, recv = sems
        x, y, c, chips = _place()
        cps = []
        for a in range(n):
            r, cc = shapes[a]
            for k, (px, py) in enumerate(chips):
                j = 2 * px + py
                src = ins[a].at[j] if axes[a] == 0 else ins[a].at[0, :, pl.ds(pl.multiple_of(j * cc, 128), cc)]
                cps.append(pltpu.make_async_remote_copy(
                    src_ref=src, dst_ref=outs[a].at[k], send_sem=send.at[a, k], recv_sem=recv.at[a, k],
                    device_id=(px, py, c), device_id_type=MESH))
        return cps

    def first(ins, outs, sems):
        for cp in copies(ins, outs, sems):
            cp.start()

    def last(ins, outs, sems):
        for cp in copies(ins, outs, sems):
            cp.wait()

    return _Side(list(p3s), [jax.ShapeDtypeStruct((3, r // 2, c), BF) for r, c in shapes],
                 [pltpu.SemaphoreType.DMA((n, 3)), pltpu.SemaphoreType.DMA((n, 3))], first, last)


def _chip_sum(p3, land, shard_shape, axis, idx, name):
    r, c = shard_shape
    hr = r // 2
    tr = 64
    nt = hr // tr

    def body(idx_ref, p_ref, l_ref, o_ref):
        acc = p_ref[...].astype(F32)
        for k in range(3):
            acc = acc + l_ref[k].astype(F32)
        o_ref[...] = acc

    own = (pl.BlockSpec((None, tr, c), lambda i, idx: (idx[0], i, 0)) if axis == 0
           else pl.BlockSpec((None, tr, c), lambda i, idx: (0, i, idx[0])))
    return pl.pallas_call(
        body, name=name,
        grid_spec=pltpu.PrefetchScalarGridSpec(
            num_scalar_prefetch=1, grid=(nt,),
            in_specs=[own, pl.BlockSpec((3, tr, c), lambda i, idx: (0, i, 0))],
            out_specs=pl.BlockSpec((tr, c), lambda i, idx: (idx[1] * nt + i, 0))),
        out_shape=jax.ShapeDtypeStruct((r, c), F32),
        compiler_params=_params(("parallel",)),
    )(idx, p3, land)


def _rs_pair_gather(fulls):
    n = len(fulls)

    def body(*refs):
        ins, outs = refs[:n], refs[n:2 * n]
        send, recv = refs[2 * n:]
        x, y, c, _ = _place()
        cps = []
        for a in range(n):
            hr = fulls[a].shape[0] // 2
            rows = pl.ds(pl.multiple_of(c * hr, 8), hr)
            cp = pltpu.make_async_remote_copy(
                src_ref=ins[a].at[rows, :], dst_ref=outs[a].at[rows, :], send_sem=send.at[a], recv_sem=recv.at[a],
                device_id=(x, y, 1 - c), device_id_type=MESH)
            cp.start()
            cps.append(cp)
        for a, cp in enumerate(cps):
            cp.wait_send()
            hr = fulls[a].shape[0] // 2
            other = pl.ds(pl.multiple_of((1 - c) * hr, 8), hr)
            pltpu.make_async_remote_copy(
                src_ref=ins[a].at[other, :], dst_ref=outs[a].at[other, :], send_sem=send.at[a], recv_sem=recv.at[a],
                device_id=(x, y, 1 - c), device_id_type=MESH).wait_recv()

    return pl.pallas_call(
        body, name="grads_pair_gather",
        in_specs=[HBM] * n, out_specs=[HBM] * n,
        out_shape=[jax.ShapeDtypeStruct(f.shape, F32) for f in fulls],
        input_output_aliases={a: a for a in range(n)},
        scratch_shapes=[pltpu.SemaphoreType.DMA((n,)), pltpu.SemaphoreType.DMA((n,))],
    )(*fulls)


def _sibling_send_side(arr):
    def copy(ins, outs, sems):
        x, y, c, _ = _place()
        return pltpu.make_async_remote_copy(
            src_ref=ins[0], dst_ref=outs[0], send_sem=sems[0].at[0], recv_sem=sems[1].at[0],
            device_id=(x, y, 1 - c), device_id_type=MESH)

    return _Side([arr], [jax.ShapeDtypeStruct(arr.shape, arr.dtype)],
                 [pltpu.SemaphoreType.DMA((1,)), pltpu.SemaphoreType.DMA((1,))],
                 lambda ins, outs, sems: copy(ins, outs, sems).start(),
                 lambda ins, outs, sems: copy(ins, outs, sems).wait())


def _add_bf16(a, b, name):
    r, c = a.shape
    tr = 64

    def body(a_ref, b_ref, o_ref):
        o_ref[...] = (a_ref[...].astype(F32) + b_ref[...].astype(F32)).astype(BF)

    blk = pl.BlockSpec((tr, c), lambda i: (i, 0))
    return pl.pallas_call(
        body, name=name, grid=(r // tr,), in_specs=[blk, blk], out_specs=blk,
        out_shape=jax.ShapeDtypeStruct((r, c), BF), compiler_params=_params(("parallel",)),
    )(a, b)


def _rs_partials(grads, shapes, axes, tag):
    cidx = jnp.reshape(lax.axis_index("c"), (1,)).astype(jnp.int32)
    g3s = [_as3d(g, s, ax) for g, s, ax in zip(grads, shapes, axes)]
    lands = _rs_pair_exchange(g3s, f"grads_pair_exchange_{tag}")
    return [_pair_sum(g3, l, cidx, f"pair_sum_{tag}_{a}") for a, (g3, l) in enumerate(zip(g3s, lands))]


def _rs_finish(p3s, landed, shapes, axes):
    x, y, c = lax.axis_index("x"), lax.axis_index("y"), lax.axis_index("c")
    idx = jnp.stack([2 * x + y, c]).astype(jnp.int32)
    fulls = [_chip_sum(p3, l2, s, ax, idx, f"chip_sum_{a}")
             for a, (p3, l2, s, ax) in enumerate(zip(p3s, landed, shapes, axes))]
    return _rs_pair_gather(fulls)


NSMALL = 8


def _small_all_reduce(g_nw, dlb, g_hnw, g_wf, loss_vec):
    def body(nw_ref, lb_ref, hn_ref, wf_ref, ls_ref, out_ref, pack_ref, buf_ref, send, recv):
        x, y, c = lax.axis_index("x"), lax.axis_index("y"), lax.axis_index("c")
        me = 4 * x + 2 * y + c
        pack_ref[...] = jnp.zeros_like(pack_ref)
        pack_ref[0:1, :] = nw_ref[...]
        pack_ref[1:2, :] = lb_ref[...]
        pack_ref[2:3, 0:HK] = hn_ref[...]
        pack_ref[3:4, :] = wf_ref[...]
        pack_ref[4:5, :] = ls_ref[...]
        buf_ref[me] = pack_ref[...]
        cps = []
        for d in range(1, 8):
            dx, dy, dc = d >> 2, (d >> 1) & 1, d & 1
            peer = (1 - x if dx else x, 1 - y if dy else y, 1 - c if dc else c)
            cp = pltpu.make_async_remote_copy(
                src_ref=pack_ref, dst_ref=buf_ref.at[me], send_sem=send.at[d - 1], recv_sem=recv.at[d - 1],
                device_id=peer, device_id_type=MESH)
            cp.start()
            cps.append(cp)
        for d in range(1, 8):
            dx, dy, dc = d >> 2, (d >> 1) & 1, d & 1
            src = 4 * (1 - x if dx else x) + 2 * (1 - y if dy else y) + (1 - c if dc else c)
            pltpu.make_async_remote_copy(
                src_ref=pack_ref, dst_ref=buf_ref.at[src], send_sem=send.at[d - 1], recv_sem=recv.at[d - 1],
                device_id=(x, y, c), device_id_type=MESH).wait_recv()
        for cp in cps:
            cp.wait_send()
        acc = buf_ref[0]
        for i in range(1, 8):
            acc = acc + buf_ref[i]
        out_ref[...] = acc

    vm = pl.BlockSpec(memory_space=pltpu.VMEM)
    return pl.pallas_call(
        body, name="small_all_reduce",
        in_specs=[vm] * 5, out_specs=vm,
        out_shape=jax.ShapeDtypeStruct((NSMALL, D), F32),
        scratch_shapes=[pltpu.VMEM((NSMALL, D), F32), pltpu.VMEM((8, NSMALL, D), F32),
                        pltpu.SemaphoreType.DMA((7,)), pltpu.SemaphoreType.DMA((7,))],
    )(g_nw, dlb, g_hnw, g_wf, loss_vec)


def _adamw_math(w, g, m, v):
    m = B1 * m + (1.0 - B1) * g
    v = B2 * v + (1.0 - B2) * (g * g)
    m_hat = m / (1.0 - B1 ** STEP)
    v_hat = v / (1.0 - B2 ** STEP)
    return -LR * (m_hat / (jnp.sqrt(v_hat) + ADAM_EPS) + WD * w), m, v


def _adamw(w, g, m, v, name):
    r, c = w.shape
    tr = 64

    def body(w_ref, g_ref, m_ref, v_ref, d_ref, nm_ref, nv_ref, go_ref):
        g = g_ref[...]
        d_ref[...], nm_ref[...], nv_ref[...] = _adamw_math(w_ref[...], g, m_ref[...], v_ref[...])
        go_ref[...] = g

    blk = pl.BlockSpec((tr, c), lambda i: (i, 0))
    return pl.pallas_call(
        body, name=name, grid=(r // tr,), in_specs=[blk] * 4, out_specs=[blk] * 4,
        out_shape=[jax.ShapeDtypeStruct((r, c), F32)] * 4,
        compiler_params=_params(("parallel",)),
    )(w, g, m, v)


def _small_update(red, lbl, params):
    def body(red_ref, *refs):
        ins, outs = refs[:12], refs[12:]
        lb = _lower_bound(ins[3][...])
        dl0 = red_ref[1:2, :] * lb * (1.0 - lb)
        row = lax.broadcasted_iota(jnp.int32, (2, D), 0)
        grads = [red_ref[0:1, :], jnp.where(row == 0, dl0, -dl0), red_ref[2:3, 0:HK], red_ref[3:4, :]]
        for i, g in enumerate(grads):
            w, m, v = ins[3 * i][...], ins[3 * i + 1][...], ins[3 * i + 2][...]
            d, nm, nv = _adamw_math(w, g, m, v)
            outs[4 * i][...] = g
            outs[4 * i + 1][...] = d
            outs[4 * i + 2][...] = nm
            outs[4 * i + 3][...] = nv
        outs[16][...] = jnp.sum(red_ref[4:5, :], axis=1, keepdims=True)

    flat = [a for p in params for a in p]
    vm = pl.BlockSpec(memory_space=pltpu.VMEM)
    shapes = [jax.ShapeDtypeStruct(p[0].shape, F32) for p in params for _ in range(4)]
    return pl.pallas_call(
        body, name="small_update",
        in_specs=[vm] * 13, out_specs=[vm] * 17,
        out_shape=shapes + [jax.ShapeDtypeStruct((1, 1), F32)],
    )(red, *flat)


def kernel(x, positions, norm_w, w_in, lb_logits, hgrn_norm_w, w_branch_a, w_branch_b, w_out, final_norm_w, loss_target, m_norm_w, m_w_in, m_lb_logits, m_hgrn_norm_w, m_w_branch_a, m_w_branch_b, m_w_out, m_final_norm_w, v_norm_w, v_w_in, v_lb_logits, v_hgrn_norm_w, v_w_branch_a, v_w_branch_b, v_w_out, v_final_norm_w):
    big_w = [w_in[0], w_branch_a[0], w_branch_b[0], w_out[0]]
    big_m = [m_w_in[0], m_w_branch_a[0], m_w_branch_b[0], m_w_out[0]]
    big_v = [v_w_in[0], v_w_branch_a[0], v_w_branch_b[0], v_w_out[0]]
    shapes = [w.shape for w in big_w]
    wf = final_norm_w.reshape(1, D)

    shards = [w.astype(BF) for w in big_w]
    loc = _local_step(x[0], positions.reshape(T, 1), norm_w, lb_logits, hgrn_norm_w, wf, loss_target[0],
                      *shards, shard_shapes=shapes)
    g_big = [loc["g_win"], loc["g_wa"], loc["g_wb"], loc["g_wout"]]
    red = _small_all_reduce(loc["g_nw"], loc["dlb"], loc["g_hnw"], loc["g_wf"], loc["loss_vec"])

    small = _small_update(red, lb_logits, [
        (norm_w, m_norm_w, v_norm_w), (lb_logits, m_lb_logits, v_lb_logits),
        (hgrn_norm_w, m_hgrn_norm_w, v_hgrn_norm_w),
        (wf, m_final_norm_w.reshape(1, D), v_final_norm_w.reshape(1, D))])
    loss = small[16].reshape(())
    sg, sd, sm, sv = ([small[4 * i + j] for i in range(4)] for j in range(4))
    for lst in (sg, sd, sm, sv):
        lst[3] = lst[3].reshape(D)
    upd = [_adamw(w, g, m, v, f"adamw_{a}") for a, (w, g, m, v) in enumerate(zip(big_w, g_big, big_m, big_v))]
    bd, bm, bv, bg = ([u[j][None] for u in upd] for j in range(4))

    def order(s, b):
        return [s[0], b[0], s[1], s[2], b[1], b[2], b[3], s[3]]

    return (loss, loc["gx"][None], *order(sg, bg), *order(sd, bd), *order(sm, bm), *order(sv, bv))
```

```python
import functools

import jax
import jax.numpy as jnp
from jax import lax
from jax.experimental import pallas as pl
from jax.experimental.pallas import tpu as pltpu

T = 2048
D = 1024
NIN = 11264
HEADS = 8
HK = 128
CH = 16
NCH = T // CH
HSTEP = 2
ATT_GROUPS = ((128, 1), (512, 4), (2048, 16))
ATT_COL0 = 4096
AG_COL0 = 8704
GATE_COL0 = 9216
EPS = 1e-6
ROPE_THETA = 10000.0
LR, B1, B2, ADAM_EPS, WD, STEP = 0.001, 0.9, 0.999, 1e-08, 0.01, 10

F32 = jnp.float32
BF = jnp.bfloat16
VMEM_LIMIT = 56 * 1024 * 1024

_NN = (((1,), (0,)), ((), ()))
_NT = (((1,), (1,)), ((), ()))
_TN = (((0,), (0,)), ((), ()))


def _dot(a, b, dims=_NN):
    return lax.dot_general(a, b, dims, preferred_element_type=F32)


def _bdot(a, b, dims=_NN):
    return lax.dot_general(a.astype(BF), b.astype(BF), dims, preferred_element_type=F32)


def _sigmoid(x):
    return jax.nn.sigmoid(x)


def _params(sem=None):
    return pltpu.CompilerParams(dimension_semantics=sem, vmem_limit_bytes=VMEM_LIMIT)


def _matmul(a, b, *, ta=False, tb=False, out_dtype=F32, tm=512, tn=512, tk=None, name, side=None):
    m = a.shape[1] if ta else a.shape[0]
    kdim = a.shape[0] if ta else a.shape[1]
    n = b.shape[0] if tb else b.shape[1]
    tk = tk or kdim
    tm, tn = min(tm, m), min(tn, n)
    nm, nn, nk = m // tm, n // tn, kdim // tk
    dims = (((0 if ta else 1,), (1 if tb else 0,)), ((), ()))
    s_arrays, s_in_specs, s_shapes, s_out_specs, s_sems = _side_io(side)
    na, no = len(s_arrays), len(s_shapes)
    nacc = 1 if nk > 1 else 0

    def body(*refs):
        a_ref, b_ref = refs[:2]
        s_ins, o_ref, s_outs = refs[2:2 + na], refs[2 + na], refs[3 + na:3 + na + no]
        scratch = refs[3 + na + no:]
        s_sem_refs = scratch[nacc:]
        i, j, k = pl.program_id(0), pl.program_id(1), pl.program_id(2)
        if side is not None:
            @pl.when((i == 0) & (j == 0) & (k == 0))
            def _():
                side.first(s_ins, s_outs, s_sem_refs)

        prod = _bdot(a_ref[...], b_ref[...], dims)
        if nk == 1:
            o_ref[...] = prod.astype(out_dtype)
        else:
            acc = scratch[0]

            @pl.when(k == 0)
            def _():
                acc[...] = prod

            @pl.when(k > 0)
            def _():
                acc[...] += prod

            @pl.when(k == nk - 1)
            def _():
                o_ref[...] = acc[...].astype(out_dtype)

        if side is not None:
            @pl.when((i == nm - 1) & (j == nn - 1) & (k == nk - 1))
            def _():
                side.last(s_ins, s_outs, s_sem_refs)

    a_spec = pl.BlockSpec((tk, tm), lambda i, j, k: (k, i)) if ta else pl.BlockSpec((tm, tk), lambda i, j, k: (i, k))
    b_spec = pl.BlockSpec((tn, tk), lambda i, j, k: (j, k)) if tb else pl.BlockSpec((tk, tn), lambda i, j, k: (k, j))
    sem = ("parallel", "parallel", "arbitrary") if side is None else ("arbitrary",) * 3
    out = pl.pallas_call(
        body, name=name, grid=(nm, nn, nk),
        in_specs=[a_spec, b_spec] + s_in_specs,
        out_specs=[pl.BlockSpec((tm, tn), lambda i, j, k: (i, j))] + s_out_specs,
        out_shape=[jax.ShapeDtypeStruct((m, n), out_dtype)] + s_shapes,
        scratch_shapes=([pltpu.VMEM((tm, tn), F32)] if nk > 1 else []) + s_sems,
        compiler_params=_params(sem),
    )(a, b, *s_arrays)
    return out[0] if side is None else (out[0], out[1:])


DZ_TILE = 512


def _part_offsets(parts):
    counts = [p.shape[1] // DZ_TILE for p in parts]
    offs = [sum(counts[:i]) for i in range(len(parts))]
    return counts, offs


def _part_spec(rows, cnt, off, tile_axis):
    def index(*g):
        return (0 if rows is None else g[0], jnp.clip(g[tile_axis] - off, 0, cnt - 1))
    return index


def _grad_w_in(h, parts):
    counts, offs = _part_offsets(parts)
    n = len(parts)

    def body(h_ref, *refs):
        o_ref = refs[n]
        j = pl.program_id(0)
        for p_ref, cnt, off in zip(refs[:n], counts, offs):
            @pl.when((j >= off) & (j < off + cnt))
            def _(p_ref=p_ref):
                o_ref[...] = _bdot(h_ref[...], p_ref[...], _TN).astype(BF)

    return pl.pallas_call(
        body, name="g_win", grid=(sum(counts),),
        in_specs=[pl.BlockSpec((T, D), lambda j: (0, 0))] +
                 [pl.BlockSpec((T, DZ_TILE), _part_spec(None, c, o, 0)) for c, o in zip(counts, offs)],
        out_specs=pl.BlockSpec((D, DZ_TILE), lambda j: (0, j)),
        out_shape=jax.ShapeDtypeStruct((D, NIN), BF),
        compiler_params=_params(("parallel",)),
    )(h, *parts)


def _grad_w_in_half(h, parts, half_idx, side=None):
    counts, offs = _part_offsets(parts)
    n = len(parts)
    nj = sum(counts)
    s_arrays, s_in_specs, s_shapes, s_out_specs, s_sems = _side_io(side)
    na, no = len(s_arrays), len(s_shapes)

    def body(idx_ref, h_ref, *refs):
        s_ins, o_ref, s_outs, s_sem_refs = refs[n:n + na], refs[n + na], refs[n + na + 1:n + na + 1 + no], refs[n + na + 1 + no:]
        j = pl.program_id(0)
        if side is not None:
            @pl.when(j == 0)
            def _():
                side.first(s_ins, s_outs, s_sem_refs)

        for p_ref, cnt, off in zip(refs[:n], counts, offs):
            @pl.when((j >= off) & (j < off + cnt))
            def _(p_ref=p_ref):
                o_ref[...] = _bdot(h_ref[...], p_ref[...], _TN).astype(BF)

        if side is not None:
            @pl.when(j == nj - 1)
            def _():
                side.last(s_ins, s_outs, s_sem_refs)

    def part_spec(cnt, off):
        return pl.BlockSpec((T, DZ_TILE), lambda j, idx: (0, jnp.clip(j - off, 0, cnt - 1)))

    out = pl.pallas_call(
        body, name="g_win_half" if side is None else "g_win_half_carrying",
        grid_spec=pltpu.PrefetchScalarGridSpec(
            num_scalar_prefetch=1, grid=(nj,),
            in_specs=[pl.BlockSpec((T, D // 2), lambda j, idx: (0, idx[0]))] +
                     [part_spec(c, o) for c, o in zip(counts, offs)] + s_in_specs,
            out_specs=[pl.BlockSpec((D // 2, DZ_TILE), lambda j, idx: (0, j))] + s_out_specs,
            scratch_shapes=s_sems),
        out_shape=[jax.ShapeDtypeStruct((D // 2, NIN), BF)] + s_shapes,
        compiler_params=_params(("parallel",) if side is None else ("arbitrary",)),
    )(half_idx, h, *parts, *s_arrays)
    return out[0] if side is None else (out[0], out[1:])


def _side_io(side):
    if side is None:
        return [], [], [], [], []
    return (side.arrays, [HBM] * len(side.arrays), side.out_shapes, [HBM] * len(side.out_shapes), side.sems)


def _grad_x(parts, w_in, x, dout, norm_w, side=None):
    counts, offs = _part_offsets(parts)
    n = len(parts)
    tm = 1024
    nm, nk = T // tm, sum(counts)
    s_arrays, s_in_specs, s_shapes, s_out_specs, s_sems = _side_io(side)
    na, no = len(s_arrays), len(s_shapes)

    def body(*refs):
        w_ref, x_ref, dout_ref, nw_ref = refs[n:n + 4]
        s_ins = refs[n + 4:n + 4 + na]
        gx_ref, gw_ref = refs[n + 4 + na:n + 6 + na]
        s_outs = refs[n + 6 + na:n + 6 + na + no]
        acc = refs[n + 6 + na + no]
        s_sem_refs = refs[n + 7 + na + no:]
        i, k = pl.program_id(0), pl.program_id(1)

        @pl.when((i == 0) & (k == 0))
        def _():
            gw_ref[...] = jnp.zeros_like(gw_ref)
            if side is not None:
                side.first(s_ins, s_outs, s_sem_refs)

        @pl.when(k == 0)
        def _():
            acc[...] = jnp.zeros_like(acc)

        for p_ref, cnt, off in zip(refs[:n], counts, offs):
            @pl.when((k >= off) & (k < off + cnt))
            def _(p_ref=p_ref):
                acc[...] += _bdot(p_ref[...], w_ref[...], _NT)

        @pl.when(k == nk - 1)
        def _():
            gw = jnp.zeros((1, D), F32)
            for c in range(tm // BLK):
                rows = pl.ds(BLK * c, BLK)
                xv, dhv = x_ref[rows, :], acc[rows, :]
                r = lax.rsqrt(jnp.mean(xv * xv, axis=-1, keepdims=True) + EPS)
                nrm = xv * r
                dn = dhv * nw_ref[...]
                gw = gw + jnp.sum(dhv * nrm, axis=0, keepdims=True)
                gx_ref[rows, :] = dout_ref[rows, :] + r * (dn - nrm * jnp.mean(dn * nrm, axis=-1, keepdims=True))
            gw_ref[...] += gw

        if side is not None:
            @pl.when((i == nm - 1) & (k == nk - 1))
            def _():
                side.last(s_ins, s_outs, s_sem_refs)

    row = pl.BlockSpec((tm, D), lambda i, k: (i, 0))
    vec = pl.BlockSpec((1, D), lambda i, k: (0, 0))
    out = pl.pallas_call(
        body, name="grad_x", grid=(nm, nk),
        in_specs=[pl.BlockSpec((tm, DZ_TILE), _part_spec(0, c, o, 1)) for c, o in zip(counts, offs)] +
                 [pl.BlockSpec((D, DZ_TILE), lambda i, k: (0, k)), row, row, vec] + s_in_specs,
        out_specs=[row, vec] + s_out_specs,
        out_shape=[jax.ShapeDtypeStruct((T, D), F32), jax.ShapeDtypeStruct((1, D), F32)] + s_shapes,
        scratch_shapes=[pltpu.VMEM((tm, D), F32)] + s_sems,
        compiler_params=_params(("arbitrary", "arbitrary")),
    )(*parts, w_in, x, dout, norm_w, *s_arrays)
    return out[0], out[1], out[2:]


def _norm_and_rope_tables(x, w, pos, invf, side=None, own=None):
    tm = 256
    nm = T // tm
    s_arrays, s_in_specs, s_shapes, s_out_specs, s_sems = _side_io(side)
    na, no = len(s_arrays), len(s_shapes)
    nz = 0 if own is None else 1
    wsh, blk = own if own is not None else (None, jnp.zeros((1,), jnp.int32))

    def body(blk_ref, *refs):
        x_ref, w_ref, pos_ref, invf_ref = refs[:4]
        s_ins = refs[4 + nz:4 + nz + na]
        h_ref, cos_ref, sa_ref, sb_ref = refs[4 + nz + na:8 + nz + na]
        s_outs = refs[8 + 2 * nz + na:8 + 2 * nz + na + no]
        s_sem_refs = refs[8 + 2 * nz + na + no:]
        if side is not None:
            @pl.when(pl.program_id(0) == 0)
            def _():
                side.first(s_ins, s_outs, s_sem_refs)

        xv = x_ref[...]
        r = lax.rsqrt(jnp.mean(xv * xv, axis=-1, keepdims=True) + EPS)
        h = (xv * r * w_ref[...]).astype(BF)
        h_ref[...] = h
        if own is not None:
            refs[8 + nz + na][...] = _dot(h, refs[4][...])
        first = (lax.broadcasted_iota(jnp.int32, (tm, 128), 1) % 64) < 32
        ang = pos_ref[...].astype(F32) * invf_ref[...]
        s = jnp.sin(ang)
        cos_ref[...] = jnp.cos(ang)
        sa_ref[...] = jnp.where(first, -s, 0.0)
        sb_ref[...] = jnp.where(first, 0.0, s)
        if side is not None:
            @pl.when(pl.program_id(0) == nm - 1)
            def _():
                side.last(s_ins, s_outs, s_sem_refs)

    tab = pl.BlockSpec((tm, 128), lambda i, b: (i, 0))
    own_in = [] if own is None else [pl.BlockSpec(wsh.shape, lambda i, b: (0, 0))]
    own_out = [] if own is None else [pl.BlockSpec((tm, wsh.shape[1]), lambda i, b: (i, b[0]))]
    own_shape = [] if own is None else [jax.ShapeDtypeStruct((T, NIN), F32)]
    out = pl.pallas_call(
        body, name="norm_and_rope_tables",
        grid_spec=pltpu.PrefetchScalarGridSpec(
            num_scalar_prefetch=1, grid=(nm,),
            in_specs=[pl.BlockSpec((tm, D), lambda i, b: (i, 0)), pl.BlockSpec((1, D), lambda i, b: (0, 0)),
                      pl.BlockSpec((tm, 1), lambda i, b: (i, 0)), pl.BlockSpec((1, 128), lambda i, b: (0, 0))]
                     + own_in + s_in_specs,
            out_specs=[pl.BlockSpec((tm, D), lambda i, b: (i, 0)), tab, tab, tab] + own_out + s_out_specs,
            scratch_shapes=s_sems),
        out_shape=[jax.ShapeDtypeStruct((T, D), BF)] + [jax.ShapeDtypeStruct((T, 128), F32)] * 3 + own_shape + s_shapes,
        compiler_params=_params(("parallel",) if side is None else ("arbitrary",)),
    )(blk, x, w, pos, invf, *([] if own is None else [wsh]), *s_arrays)
    return out[0], out[1], out[2], out[3], (out[4] if own is not None else None), out[4 + nz:]


def _z_blocks(h, w, z, idx, nb, side, name, fill=None):
    tm, tn = 1024, NIN // 8
    s_arrays, s_in_specs, s_shapes, s_out_specs, s_sems = _side_io(side)
    na, no = len(s_arrays), len(s_shapes)
    nm, ns = T // tm, 2 * nb
    nf = 0 if fill is None else 1

    def col(first, i, s, b):
        return (0, b[first + s // 2] * 2 + s % 2)

    def body(idx_ref, h_ref, w_ref, zin_ref, *refs):
        s_ins = refs[nf:nf + na]
        o_ref = refs[nf + na]
        s_outs = refs[nf + na + 1 + nf:nf + na + 1 + nf + no]
        s_sem_refs = refs[nf + na + 1 + nf + no + nf:]
        i, s = pl.program_id(0), pl.program_id(1)

        if side is not None:
            @pl.when((i == 0) & (s == 0))
            def _():
                side.first(s_ins, s_outs, s_sem_refs)

        if fill is not None:
            tile = pl.ds(pl.multiple_of((idx_ref[0] * 2 + s) * tn, 128), tn)
            store = pltpu.make_async_copy(w_ref, refs[nf + na + 1].at[:, tile], refs[nf + na + 1 + nf + no].at[0])
            pl.when(i == 0)(store.start)
        o_ref[...] = _dot(h_ref[...], w_ref[...])
        if fill is not None:
            pl.when(i == 0)(store.wait)

        if side is not None:
            @pl.when((i == nm - 1) & (s == ns - 1))
            def _():
                side.last(s_ins, s_outs, s_sem_refs)

    fills = [] if fill is None else [fill]
    out = pl.pallas_call(
        body, name=name,
        grid_spec=pltpu.PrefetchScalarGridSpec(
            num_scalar_prefetch=1, grid=(nm, ns),
            in_specs=[pl.BlockSpec((tm, D), lambda i, s, b: (i, 0)), pl.BlockSpec((D, tn), functools.partial(col, nb)),
                      HBM] + [HBM] * nf + s_in_specs,
            out_specs=[pl.BlockSpec((tm, tn), lambda i, s, b: (i, col(0, i, s, b)[1]))] + [HBM] * nf + s_out_specs,
            scratch_shapes=[pltpu.SemaphoreType.DMA((1,))] * nf + s_sems),
        out_shape=[jax.ShapeDtypeStruct((T, NIN), F32)] + [jax.ShapeDtypeStruct(f.shape, f.dtype) for f in fills]
                  + s_shapes,
        input_output_aliases={3: 0, **({4: 1} if fill is not None else {})},
        compiler_params=_params(("arbitrary", "arbitrary")),
    )(idx, h, w, z, *fills, *s_arrays)
    return (out[0], *out[1:1 + nf], out[1 + nf:])


def _lower_bound(lbl):
    mx = jnp.max(lbl, axis=0, keepdims=True)
    e = jnp.exp(lbl - mx)
    return e[0:1] / jnp.sum(e, axis=0, keepdims=True)


def _cumsum_rows(g, rows):
    b = g
    sh = 1
    while sh < CH:
        b = b + jnp.where(rows >= sh, pltpu.roll(b, sh, axis=0), 0.0)
        sh *= 2
    return b


def _rev_cumsum_rows(g, rows):
    b = g
    sh = 1
    while sh < CH:
        b = b + jnp.where(rows < CH - sh, pltpu.roll(b, CH - sh, axis=0), 0.0)
        sh *= 2
    return b


SUB = CH // 2


def _direct_block(qb, kb, vb, bb, rows8):
    ob = jnp.zeros_like(qb)
    for s in range(SUB):
        e_s = jnp.exp(jnp.where(rows8 >= s, bb - bb[s:s + 1], -jnp.inf))
        ob = ob + jnp.sum(qb * e_s * kb[s:s + 1], axis=1, keepdims=True) * vb[s:s + 1]
    return ob


def _direct_block_bwd(qb, kb, vb, bb, dob, rows8, rowc8):
    dq = dk = dv = db = jnp.zeros_like(qb)
    for s in range(SUB):
        one = (rowc8 == s).astype(F32)
        ks, vs = kb[s:s + 1], vb[s:s + 1]
        e_s = jnp.exp(jnp.where(rows8 >= s, bb - bb[s:s + 1], -jnp.inf))
        qes = qb * e_s
        w = qes * ks
        a = jnp.sum(w, axis=1, keepdims=True)
        da = jnp.sum(dob * vs, axis=1, keepdims=True)
        dv = dv + one * jnp.sum(a * dob, axis=0, keepdims=True)
        dq = dq + da * e_s * ks
        dk = dk + one * jnp.sum(da * qes, axis=0, keepdims=True)
        u = da * w
        db = db + u - one * jnp.sum(u, axis=0, keepdims=True)
    return dq, dk, dv, db


def _cross_factors(q, k, b):
    ref = b[SUB - 1:SUB]
    e_hi, e_lo = jnp.exp(b[SUB:] - ref), jnp.exp(ref - b[:SUB])
    return q[SUB:] * e_hi, k[:SUB] * e_lo, e_hi, e_lo


def _intra_fwd(q, k, v, b, rows8):
    lo = _direct_block(q[:SUB], k[:SUB], v[:SUB], b[:SUB], rows8)
    hi = _direct_block(q[SUB:], k[SUB:], v[SUB:], b[SUB:], rows8)
    qe_hi, ke_lo, _, _ = _cross_factors(q, k, b)
    for s in range(SUB):
        hi = hi + jnp.sum(qe_hi * ke_lo[s:s + 1], axis=1, keepdims=True) * v[s:s + 1]
    return jnp.concatenate([lo, hi], axis=0)


def _intra_bwd(q, k, v, b, do, rows8, rowc8):
    dq_lo, dk_lo, dv_lo, db_lo = _direct_block_bwd(q[:SUB], k[:SUB], v[:SUB], b[:SUB], do[:SUB], rows8, rowc8)
    dq_hi, dk_hi, dv_hi, db_hi = _direct_block_bwd(q[SUB:], k[SUB:], v[SUB:], b[SUB:], do[SUB:], rows8, rowc8)
    qe_hi, ke_lo, e_hi, e_lo = _cross_factors(q, k, b)
    do_hi, v_lo = do[SUB:], v[:SUB]
    dqe = dke = jnp.zeros_like(qe_hi)
    for s in range(SUB):
        one = (rowc8 == s).astype(F32)
        a = jnp.sum(qe_hi * ke_lo[s:s + 1], axis=1, keepdims=True)
        da = jnp.sum(do_hi * v_lo[s:s + 1], axis=1, keepdims=True)
        dv_lo = dv_lo + one * jnp.sum(a * do_hi, axis=0, keepdims=True)
        dqe = dqe + da * ke_lo[s:s + 1]
        dke = dke + one * jnp.sum(da * qe_hi, axis=0, keepdims=True)
    u_hi, u_lo = dqe * qe_hi, dke * ke_lo
    d_ref = jnp.sum(u_lo, axis=0, keepdims=True) - jnp.sum(u_hi, axis=0, keepdims=True)
    db_lo = db_lo - u_lo + (rowc8 == SUB - 1).astype(F32) * d_ref
    cat = lambda lo, hi: jnp.concatenate([lo, hi], axis=0)
    return (cat(dq_lo, dq_hi + dqe * e_hi), cat(dk_lo + dke * e_lo, dk_hi), cat(dv_lo, dv_hi),
            cat(db_lo, db_hi + u_hi))


def _hgrn_fwd(z, lbl, nw, side=None):
    s_arrays, s_in_specs, s_shapes, s_out_specs, s_sems = _side_io(side)
    na, no = len(s_arrays), len(s_shapes)
    nsteps = NCH // HSTEP

    def body(hq_ref, hf_ref, hi_ref, hg_ref, lbl_ref, nw_ref, *refs):
        s_ins, (oraw_ref, og_ref, sh_ref) = refs[:na], refs[na:na + 3]
        s_outs, st_ref, s_sem_refs = refs[na + 3:na + 3 + no], refs[na + 3 + no], refs[na + 4 + no:]

        @pl.when(pl.program_id(0) == 0)
        def _():
            st_ref[...] = jnp.zeros_like(st_ref)
            if side is not None:
                side.first(s_ins, s_outs, s_sem_refs)

        lb_all = _lower_bound(lbl_ref[...])
        rows = lax.broadcasted_iota(jnp.int32, (CH, HK), 0)
        rows8 = lax.broadcasted_iota(jnp.int32, (SUB, HK), 0)
        nwv = nw_ref[...]
        for cc, h in [(cc, h) for cc in range(HSTEP) for h in range(HEADS)]:
            rs = slice(CH * cc, CH * (cc + 1))
            sl = slice(HK * h, HK * (h + 1))
            lb = lb_all[:, sl]
            hq, hf, v, hg = hq_ref[rs, sl], hf_ref[rs, sl], hi_ref[rs, sl], hg_ref[rs, sl]
            q = hq * _sigmoid(hq)
            f = lb + (1.0 - lb) * _sigmoid(hf)
            k = 1.0 - f
            b = _cumsum_rows(jnp.log(f), rows)
            sh_ref[cc, h] = st_ref[h]
            o = _bdot(q * jnp.exp(b), st_ref[h], _NT) + _intra_fwd(q, k, v, b, rows8)
            bl = b[CH - 1:CH]
            st_ref[h] = st_ref[h] * jnp.exp(bl)
            st_ref[h] += _bdot(v, k * jnp.exp(bl - b), _TN)
            oraw_ref[rs, sl] = o
            nrm = o * lax.rsqrt(jnp.mean(o * o, axis=1, keepdims=True) + EPS)
            og_ref[rs, sl] = (nrm * nwv * (hg * _sigmoid(hg))).astype(BF)

        if side is not None:
            @pl.when(pl.program_id(0) == nsteps - 1)
            def _():
                side.last(s_ins, s_outs, s_sem_refs)

    zblk = lambda c: pl.BlockSpec((CH * HSTEP, D), lambda i, c=c: (i, c))
    out = pl.pallas_call(
        body, name="hgrn_fwd", grid=(nsteps,),
        in_specs=[zblk(0), zblk(1), zblk(2), zblk(3),
                  pl.BlockSpec((2, D), lambda i: (0, 0)), pl.BlockSpec((1, HK), lambda i: (0, 0))] + s_in_specs,
        out_specs=[zblk(0), zblk(0),
                   pl.BlockSpec((HSTEP, HEADS, HK, HK), lambda i: (i, 0, 0, 0))] + s_out_specs,
        out_shape=[jax.ShapeDtypeStruct((T, D), F32), jax.ShapeDtypeStruct((T, D), BF),
                   jax.ShapeDtypeStruct((NCH, HEADS, HK, HK), F32)] + s_shapes,
        scratch_shapes=[pltpu.VMEM((HEADS, HK, HK), F32)] + s_sems,
        compiler_params=_params(("arbitrary",)),
    )(z, z, z, z, lbl, nw, *s_arrays)
    return out[0], out[1], out[2], out[3:]


def _hgrn_bwd(z, lbl, nw, oraw, dog, shist, side=None):
    hstep = 1
    s_arrays, s_in_specs, s_shapes, s_out_specs, s_sems = _side_io(side)
    na, no = len(s_arrays), len(s_shapes)

    def body(*refs):
        hq_ref, hf_ref, hi_ref, hg_ref, lbl_ref, nw_ref, oraw_ref, dog_ref, sh_ref = refs[:9]
        s_ins = refs[9:9 + na]
        dz_ref, dlb_ref, dnw_ref = refs[9 + na:12 + na]
        s_outs = refs[12 + na:12 + na + no]
        dst_ref = refs[12 + na + no]
        s_sem_refs = refs[13 + na + no:]

        @pl.when(pl.program_id(0) == 0)
        def _():
            dst_ref[...] = jnp.zeros_like(dst_ref)
            dlb_ref[...] = jnp.zeros_like(dlb_ref)
            dnw_ref[...] = jnp.zeros_like(dnw_ref)
            if side is not None:
                side.first(s_ins, s_outs, s_sem_refs)

        lb_all = _lower_bound(lbl_ref[...])
        rows = lax.broadcasted_iota(jnp.int32, (CH, HK), 0)
        rowc = lax.broadcasted_iota(jnp.int32, (CH, 1), 0)
        rows8 = lax.broadcasted_iota(jnp.int32, (SUB, HK), 0)
        rowc8 = lax.broadcasted_iota(jnp.int32, (SUB, 1), 0)
        nwv = nw_ref[...]
        dnw = jnp.zeros((1, HK), F32)
        for cc, h in [(cc, h) for cc in reversed(range(hstep)) for h in range(HEADS)]:
            rs = slice(CH * cc, CH * (cc + 1))
            sl = slice(HK * h, HK * (h + 1))
            lb = lb_all[:, sl]
            hq, hf, v, hg = hq_ref[rs, sl], hf_ref[rs, sl], hi_ref[rs, sl], hg_ref[rs, sl]
            o, dg_out = oraw_ref[rs, sl], dog_ref[rs, sl]
            sg = _sigmoid(hg)
            sil = hg * sg
            r = lax.rsqrt(jnp.mean(o * o, axis=1, keepdims=True) + EPS)
            nrm = o * r
            d_hg = dg_out * (nrm * nwv) * (sg * (1.0 + hg * (1.0 - sg)))
            dn = dg_out * nwv * sil
            dnw = dnw + jnp.sum(dg_out * nrm * sil, axis=0, keepdims=True)
            do = r * (dn - nrm * jnp.mean(dn * nrm, axis=1, keepdims=True))
            sq = _sigmoid(hq)
            q = hq * sq
            sig = _sigmoid(hf)
            f = lb + (1.0 - lb) * sig
            k = 1.0 - f
            b = _cumsum_rows(jnp.log(f), rows)
            eb = jnp.exp(b)
            qe = q * eb
            bl = b[CH - 1:CH]
            ebl = jnp.exp(bl)
            kdec = jnp.exp(bl - b)
            ke = k * kdec
            dqe = _bdot(do, sh_ref[cc, h])
            dq = dqe * eb
            db = dqe * qe
            dke = _bdot(v, dst_ref[h])
            dv = _bdot(ke, dst_ref[h], _NT)
            dk = dke * kdec
            rr = dke * ke
            db = db - rr
            db_last = (jnp.sum(rr, axis=0, keepdims=True)
                       + ebl * jnp.sum(dst_ref[h] * sh_ref[cc, h], axis=0, keepdims=True))
            dst_ref[h] = dst_ref[h] * ebl
            dst_ref[h] += _bdot(do, qe, _TN)
            dq_i, dk_i, dv_i, db_i = _intra_bwd(q, k, v, b, do, rows8, rowc8)
            dq, dk, dv = dq + dq_i, dk + dk_i, dv + dv_i
            db = db + db_i + (rowc == CH - 1).astype(F32) * db_last
            dgl = _rev_cumsum_rows(db, rows)
            df = dgl / f - dk
            dlb_ref[:, sl] += jnp.sum(df * (1.0 - sig), axis=0, keepdims=True)
            dz_ref[rs, sl] = (dq * (sq * (1.0 + hq * (1.0 - sq)))).astype(BF)
            dz_ref[rs, D + HK * h:D + HK * (h + 1)] = (df * (1.0 - lb) * sig * (1.0 - sig)).astype(BF)
            dz_ref[rs, 2 * D + HK * h:2 * D + HK * (h + 1)] = dv.astype(BF)
            dz_ref[rs, 3 * D + HK * h:3 * D + HK * (h + 1)] = d_hg.astype(BF)
        dnw_ref[...] += dnw
        if side is not None:
            @pl.when(pl.program_id(0) == NCH // hstep - 1)
            def _():
                side.last(s_ins, s_outs, s_sem_refs)

    rev = lambda i: NCH // hstep - 1 - i
    zblk = lambda c: pl.BlockSpec((CH * hstep, D), lambda i, c=c: (rev(i), c))
    out = pl.pallas_call(
        body, name="hgrn_bwd", grid=(NCH // hstep,),
        in_specs=[zblk(0), zblk(1), zblk(2), zblk(3),
                  pl.BlockSpec((2, D), lambda i: (0, 0)), pl.BlockSpec((1, HK), lambda i: (0, 0)),
                  zblk(0), zblk(0),
                  pl.BlockSpec((hstep, HEADS, HK, HK), lambda i: (rev(i), 0, 0, 0))] + s_in_specs,
        out_specs=[pl.BlockSpec((CH * hstep, 4 * D), lambda i: (rev(i), 0)),
                   pl.BlockSpec((1, D), lambda i: (0, 0)), pl.BlockSpec((1, HK), lambda i: (0, 0))] + s_out_specs,
        out_shape=[jax.ShapeDtypeStruct((T, 4 * D), BF), jax.ShapeDtypeStruct((1, D), F32),
                   jax.ShapeDtypeStruct((1, HK), F32)] + s_shapes,
        scratch_shapes=[pltpu.VMEM((HEADS, HK, HK), F32)] + s_sems,
        compiler_params=_params(("arbitrary",)),
    )(z, z, z, z, lbl, nw, oraw, dog, shist, *s_arrays)
    return out[0], out[1], out[2], out[3:]


BLK = 128
NBLK = T // BLK
QK_SCALE = 0.125


def _head_masks():
    lane = lax.broadcasted_iota(jnp.int32, (1, BLK), 1)
    return [(lane < 64).astype(F32), (lane >= 64).astype(F32)]


def _pieces(dil):
    m = T // dil
    out = []
    for r in range(dil):
        for j in range(m // BLK):
            start = r + dil * BLK * j
            rows = pl.ds(start, BLK, stride=dil) if dil > 1 else pl.ds(start, BLK)
            out.append((rows, r * m + BLK * j))
    return out


def _rope(x, c, sa, sb):
    return x * c + pltpu.roll(x, 96, axis=1) * sa + pltpu.roll(x, 32, axis=1) * sb


def _rope_t(d, c, sa, sb):
    return d * c + pltpu.roll(d * sa, 32, axis=1) + pltpu.roll(d * sb, 96, axis=1)


def _rope_and_regroup(dil, q_ref, k_ref, v_ref, tables, stage_q, stage_k, qr_ref, kr_ref, vr_ref):
    cos_ref, sa_ref, sb_ref = tables
    to_q, to_k = (qr_ref, kr_ref) if dil == 1 else (stage_q, stage_k)
    for c in range(T // BLK):
        rows = pl.ds(BLK * c, BLK)
        cs, sa, sb = cos_ref[rows, :], sa_ref[rows, :], sb_ref[rows, :]
        to_q[rows, :] = (_rope(q_ref[rows, :], cs, sa, sb) * QK_SCALE).astype(to_q.dtype)
        to_k[rows, :] = _rope(k_ref[rows, :], cs, sa, sb).astype(to_k.dtype)
    for rows, dst in _pieces(dil):
        drows = pl.ds(dst, BLK)
        if dil > 1:
            qr_ref[drows, :] = stage_q[rows, :].astype(qr_ref.dtype)
            kr_ref[drows, :] = stage_k[rows, :].astype(kr_ref.dtype)
        vr_ref[drows, :] = v_ref[rows, :].astype(vr_ref.dtype)


def _window_bias(bias_ref):
    ii = lax.broadcasted_iota(jnp.int32, (2 * BLK, BLK), 0) % BLK
    jj = lax.broadcasted_iota(jnp.int32, (2 * BLK, BLK), 1)
    bias_ref[0] = jnp.where(jj <= ii, 0.0, -jnp.inf)
    bias_ref[1] = jnp.where(jj >= ii, 0.0, -jnp.inf)


def _blocks(bi):
    if isinstance(bi, int):
        return pl.ds(bi * BLK, BLK), pl.ds(max(bi - 1, 0) * BLK, BLK)
    return (pl.ds(pl.multiple_of(bi * BLK, BLK), BLK),
            pl.ds(pl.multiple_of(jnp.maximum(bi - 1, 0) * BLK, BLK), BLK))


def _stack_heads(x, masks):
    return jnp.concatenate([x * masks[0].astype(x.dtype), x * masks[1].astype(x.dtype)], axis=0).astype(BF)


def _attn_fwd(z, cos, sa, sb):
    def body(q_ref, k_ref, v_ref, ag_ref, cos_ref, sa_ref, sb_ref, ob_ref, opre_ref, lse_ref, qr_ref, kr_ref, vr_ref,
             bias_ref, og_ref, lg_ref, otok_ref, ltok_ref, sc_ref):
        g = pl.program_id(1)
        masks = _head_masks()

        @pl.when(g == 0)
        def _():
            _window_bias(bias_ref)

        def group(gi):
            dil = ATT_GROUPS[gi][1]
            nblk = (T // dil) // BLK
            _rope_and_regroup(dil, q_ref, k_ref, v_ref, (cos_ref, sa_ref, sb_ref), lg_ref.at[0], lg_ref.at[1],
                              qr_ref, kr_ref, vr_ref)

            def scores(bi, slot):
                cur, prev = _blocks(bi)
                q2 = _stack_heads(qr_ref[cur, :], masks)
                sc_ref[slot, 0] = _dot(q2, kr_ref[cur, :], _NT) + bias_ref[0]
                if nblk > 1:
                    sc_ref[slot, 1] = (_dot(q2, kr_ref[prev, :], _NT)
                                       + (bias_ref[1] + jnp.where((bi % nblk) != 0, 0.0, -jnp.inf)))

            def finish(bi, slot):
                cur, prev = _blocks(bi)
                s_c, vc = sc_ref[slot, 0], vr_ref[cur, :]
                if nblk > 1:
                    s_p, vp = sc_ref[slot, 1], vr_ref[prev, :]
                    mx = jnp.max(jnp.maximum(s_c, s_p), axis=1, keepdims=True)
                    p_c, p_p = jnp.exp(s_c - mx), jnp.exp(s_p - mx)
                    den = jnp.sum(p_c + p_p, axis=1, keepdims=True)
                    oh = _dot(p_c.astype(BF), vc) + _dot(p_p.astype(BF), vp)
                else:
                    mx = jnp.max(s_c, axis=1, keepdims=True)
                    p_c = jnp.exp(s_c - mx)
                    den = jnp.sum(p_c, axis=1, keepdims=True)
                    oh = _dot(p_c.astype(BF), vc)
                on = oh / den
                lsev = jnp.broadcast_to(mx + jnp.log(den), (2 * BLK, BLK))
                og_ref[cur, :] = on[:BLK] * masks[0] + on[BLK:] * masks[1]
                lg_ref[0, cur, :] = lsev[:BLK]
                lg_ref[1, cur, :] = lsev[BLK:]

            def pair(j, carry):
                finish(2 * j, 0)
                scores(2 * j + 1, 1)
                finish(2 * j + 1, 1)
                scores(jnp.minimum(2 * j + 2, NBLK - 1), 0)
                return carry

            scores(0, 0)
            lax.fori_loop(0, NBLK // 2, pair, 0)
            for rows, src in _pieces(dil):
                srows = pl.ds(src, BLK)
                otok_ref[gi, rows, :] = og_ref[srows, :]
                ltok_ref[gi, 0, rows, :] = lg_ref[0, srows, :]
                ltok_ref[gi, 1, rows, :] = lg_ref[1, srows, :]

        for gi in range(3):
            pl.when(g == gi)(functools.partial(group, gi))

        @pl.when(g == 2)
        def _():
            for c in range(T // BLK):
                rows = pl.ds(BLK * c, BLK)
                wts = []
                for hh in range(2):
                    l0, l1, l2 = ltok_ref[0, hh, rows, :], ltok_ref[1, hh, rows, :], ltok_ref[2, hh, rows, :]
                    mx = jnp.maximum(jnp.maximum(l0, l1), l2)
                    e0, e1, e2 = jnp.exp(l0 - mx), jnp.exp(l1 - mx), jnp.exp(l2 - mx)
                    tot = e0 + e1 + e2
                    lse_ref[rows, BLK * hh:BLK * (hh + 1)] = mx + jnp.log(tot)
                    inv = 1.0 / tot
                    wts.append([e0 * inv, e1 * inv, e2 * inv])
                o = sum((wts[0][gi] * masks[0] + wts[1][gi] * masks[1]) * otok_ref[gi, rows, :] for gi in range(3))
                ag = ag_ref[rows, :]
                opre_ref[rows, :] = o
                ob_ref[rows, :] = (o * (ag * _sigmoid(ag))).astype(BF)

    c0 = ATT_COL0 // BLK
    zspec = lambda part: pl.BlockSpec((T, BLK), lambda p, g, part=part: (0, c0 + 12 * part + 4 * g + p))
    outspec = pl.BlockSpec((T, BLK), lambda p, g: (0, p))
    table = pl.BlockSpec((T, BLK), lambda p, g: (0, 0))
    regrouped = pl.BlockSpec((None, T, BLK), lambda p, g: (g, 0, p))
    big = lambda: pltpu.VMEM((T, BLK), F32)
    return pl.pallas_call(
        body, name="attn_fwd", grid=(4, 3),
        in_specs=[zspec(0), zspec(1), zspec(2),
                  pl.BlockSpec((T, BLK), lambda p, g: (0, AG_COL0 // BLK + p)), table, table, table],
        out_specs=[outspec, outspec, pl.BlockSpec((T, 2 * BLK), lambda p, g: (0, p)), regrouped, regrouped, regrouped],
        out_shape=[jax.ShapeDtypeStruct((T, 512), BF), jax.ShapeDtypeStruct((T, 512), F32),
                   jax.ShapeDtypeStruct((T, 8 * BLK), F32)] + [jax.ShapeDtypeStruct((3, T, 512), BF)] * 3,
        scratch_shapes=[pltpu.VMEM((2, 2 * BLK, BLK), F32), big(),
                        pltpu.VMEM((2, T, BLK), F32), pltpu.VMEM((3, T, BLK), F32), pltpu.VMEM((3, 2, T, BLK), F32),
                        pltpu.VMEM((2, 2, 2 * BLK, BLK), F32)],
        compiler_params=_params(("parallel", "arbitrary")),
    )(z, z, z, z, cos, sa, sb)


def _attn_bwd(z, qs, ks, vs, cos, sa, sb, opre, lse, dob):
    def body(qs_ref, ks_ref, vs_ref, ag_ref, cos_ref, sa_ref, sb_ref, o_ref, lse0_ref, lse1_ref, dob_ref,
             dq_ref, dk_ref, dv_ref, dag_ref,
             bias_ref, dtok_ref, qr_ref, kr_ref, vr_ref, dor_ref, lr_ref, dr_ref,
             dqr_ref, dkr_ref, dvr_ref, pd_ref, dotok_ref):
        g = pl.program_id(1)
        masks = _head_masks()

        @pl.when(g == 0)
        def _():
            _window_bias(bias_ref)
            for c in range(T // BLK):
                rows = pl.ds(BLK * c, BLK)
                ag, dob_v, o = ag_ref[rows, :], dob_ref[rows, :], o_ref[rows, :]
                sg = _sigmoid(ag)
                dag_ref[rows, :] = (dob_v * o * (sg * (1.0 + ag * (1.0 - sg)))).astype(BF)
                do = dob_v * (ag * sg)
                dotok_ref[rows, :] = do
                prod = do * o
                for hh, mh in enumerate(masks):
                    dtok_ref[hh, rows, :] = jnp.broadcast_to(jnp.sum(prod * mh, axis=1, keepdims=True), (BLK, BLK))

        def group(gi):
            dil = ATT_GROUPS[gi][1]
            nblk = (T // dil) // BLK
            for rows, dst in _pieces(dil):
                drows = pl.ds(dst, BLK)
                dor_ref[drows, :] = dotok_ref[rows, :]
                for hh, lse_ref in enumerate((lse0_ref, lse1_ref)):
                    lr_ref[hh, drows, :] = lse_ref[rows, :]
                    dr_ref[hh, drows, :] = dtok_ref[hh, rows, :]
            dkr_ref[...] = jnp.zeros_like(dkr_ref)
            dvr_ref[...] = jnp.zeros_like(dvr_ref)

            def probs(bi, slot):
                cur, prev = _blocks(bi)
                q2, do2 = _stack_heads(qs_ref[cur, :], masks), _stack_heads(dor_ref[cur, :], masks)
                lh = jnp.concatenate([lr_ref[0, cur, :], lr_ref[1, cur, :]], axis=0)
                dh = jnp.concatenate([dr_ref[0, cur, :], dr_ref[1, cur, :]], axis=0)
                p_c = jnp.exp(_dot(q2, ks_ref[cur, :], _NT) + bias_ref[0] - lh)
                pd_ref[slot, 0] = p_c.astype(BF)
                pd_ref[slot, 1] = (p_c * (_dot(do2, vs_ref[cur, :], _NT) - dh)).astype(BF)
                if nblk > 1:
                    bias_p = bias_ref[1] + jnp.where((bi % nblk) != 0, 0.0, -jnp.inf)
                    p_p = jnp.exp(_dot(q2, ks_ref[prev, :], _NT) + bias_p - lh)
                    pd_ref[slot, 2] = p_p.astype(BF)
                    pd_ref[slot, 3] = (p_p * (_dot(do2, vs_ref[prev, :], _NT) - dh)).astype(BF)

            def grads(bi, slot):
                cur, prev = _blocks(bi)
                q2, do2 = _stack_heads(qs_ref[cur, :], masks), _stack_heads(dor_ref[cur, :], masks)
                p_c, ds_c = pd_ref[slot, 0], pd_ref[slot, 1]
                dq2 = _dot(ds_c, ks_ref[cur, :])
                dkr_ref[cur, :] += _dot(ds_c, q2, _TN)
                dvr_ref[cur, :] += _dot(p_c, do2, _TN)
                if nblk > 1:
                    p_p, ds_p = pd_ref[slot, 2], pd_ref[slot, 3]
                    dq2 = dq2 + _dot(ds_p, ks_ref[prev, :])
                    dkr_ref[prev, :] += _dot(ds_p, q2, _TN)
                    dvr_ref[prev, :] += _dot(p_p, do2, _TN)
                dqr_ref[cur, :] = dq2[:BLK] * masks[0] + dq2[BLK:] * masks[1]

            def pair(j, carry):
                grads(2 * j, 0)
                probs(2 * j + 1, 1)
                grads(2 * j + 1, 1)
                probs(jnp.minimum(2 * j + 2, NBLK - 1), 0)
                return carry

            probs(0, 0)
            lax.fori_loop(0, NBLK // 2, pair, 0)
            if dil > 1:
                for rows, src in _pieces(dil):
                    srows = pl.ds(src, BLK)
                    qr_ref[rows, :] = dqr_ref[srows, :]
                    kr_ref[rows, :] = dkr_ref[srows, :]
                    vr_ref[rows, :] = dvr_ref[srows, :]
            tq, tk, tv = (qr_ref, kr_ref, vr_ref) if dil > 1 else (dqr_ref, dkr_ref, dvr_ref)
            for c in range(T // BLK):
                rows = pl.ds(BLK * c, BLK)
                cs, sa, sb = cos_ref[rows, :], sa_ref[rows, :], sb_ref[rows, :]
                dq_ref[rows, :] = _rope_t(tq[rows, :] * QK_SCALE, cs, sa, sb).astype(BF)
                dk_ref[rows, :] = _rope_t(tk[rows, :], cs, sa, sb).astype(BF)
                dv_ref[rows, :] = tv[rows, :].astype(BF)

        for gi in range(3):
            pl.when(g == gi)(functools.partial(group, gi))

    regrouped = pl.BlockSpec((None, T, BLK), lambda p, g: (g, 0, p))
    pspec = pl.BlockSpec((T, BLK), lambda p, g: (0, p))
    gspec = pl.BlockSpec((T, BLK), lambda p, g: (0, 4 * g + p))
    table = pl.BlockSpec((T, BLK), lambda p, g: (0, 0))
    big = lambda: pltpu.VMEM((T, BLK), F32)
    two = lambda: pltpu.VMEM((2, T, BLK), F32)
    return pl.pallas_call(
        body, name="attn_bwd", grid=(4, 3),
        in_specs=[regrouped, regrouped, regrouped,
                  pl.BlockSpec((T, BLK), lambda p, g: (0, AG_COL0 // BLK + p)), table, table, table,
                  pspec, pl.BlockSpec((T, BLK), lambda p, g: (0, 2 * p)),
                  pl.BlockSpec((T, BLK), lambda p, g: (0, 2 * p + 1)), pspec],
        out_specs=[gspec, gspec, gspec, pspec],
        out_shape=[jax.ShapeDtypeStruct((T, 1536), BF), jax.ShapeDtypeStruct((T, 1536), BF),
                   jax.ShapeDtypeStruct((T, 1536), BF), jax.ShapeDtypeStruct((T, 512), BF)],
        scratch_shapes=[pltpu.VMEM((2, 2 * BLK, BLK), F32), two(), big(), big(), big(), big(),
                        two(), two(), big(), big(), big(), pltpu.VMEM((2, 4, 2 * BLK, BLK), BF), big()],
        compiler_params=_params(("parallel", "arbitrary")),
    )(qs, ks, vs, z, cos, sa, sb, opre, lse, lse, dob)


def _merge_out_loss(og, ob, z, w_a, w_b, w_out, x, tgt, wf):
    tm = 256

    def body(og_ref, ob_ref, ga_ref, gb_ref, wa_ref, wb_ref, wo_ref, x_ref, t_ref, wf_ref,
             ya_ref, yb_ref, m_ref, dout_ref, loss_ref, gwf_ref):
        @pl.when(pl.program_id(0) == 0)
        def _():
            loss_ref[...] = jnp.zeros_like(loss_ref)
            gwf_ref[...] = jnp.zeros_like(gwf_ref)

        ya, yb = _dot(og_ref[...], wa_ref[...]), _dot(ob_ref[...], wb_ref[...])
        ya_ref[...] = ya
        yb_ref[...] = yb
        m = (_sigmoid(ga_ref[...]) * ya + _sigmoid(gb_ref[...]) * yb).astype(BF)
        m_ref[...] = m
        out = x_ref[...] + _dot(m, wo_ref[...])
        r = lax.rsqrt(jnp.mean(out * out, axis=-1, keepdims=True) + EPS)
        yh = out * r
        wfv = wf_ref[...]
        err = yh * wfv - t_ref[...]
        loss_ref[...] += jnp.sum(err * err, axis=0, keepdims=True) * (0.5 / D)
        dy = err * (1.0 / D)
        gwf_ref[...] += jnp.sum(dy * yh, axis=0, keepdims=True)
        dyh = dy * wfv
        dout_ref[...] = r * (dyh - yh * jnp.mean(dyh * yh, axis=-1, keepdims=True))

    row = pl.BlockSpec((tm, D), lambda i: (i, 0))
    vec = pl.BlockSpec((1, D), lambda i: (0, 0))
    whole = lambda w: pl.BlockSpec(w.shape, lambda i: (0, 0))
    return pl.pallas_call(
        body, name="merge_out_loss", grid=(T // tm,),
        in_specs=[row, pl.BlockSpec((tm, ob.shape[1]), lambda i: (i, 0)),
                  pl.BlockSpec((tm, D), lambda i: (i, GATE_COL0 // D)),
                  pl.BlockSpec((tm, D), lambda i: (i, GATE_COL0 // D + 1)),
                  whole(w_a), whole(w_b), whole(w_out), row, row, vec],
        out_specs=[row, row, row, row, vec, vec],
        out_shape=[jax.ShapeDtypeStruct((T, D), F32), jax.ShapeDtypeStruct((T, D), F32),
                   jax.ShapeDtypeStruct((T, D), BF), jax.ShapeDtypeStruct((T, D), F32),
                   jax.ShapeDtypeStruct((1, D), F32), jax.ShapeDtypeStruct((1, D), F32)],
        compiler_params=_params(("arbitrary",)),
    )(og, ob, z, z, w_a, w_b, w_out, x, tgt, wf)


def _merge_proj_bwd(dout, ya, yb, z, w_a, w_b, w_out):
    tm = 256

    def body(dout_ref, ya_ref, yb_ref, ga_ref, gb_ref, wa_ref, wb_ref, wo_ref,
             dya_ref, dyb_ref, dg_ref, dog_ref, dob_ref):
        dmv = _dot(dout_ref[...].astype(BF), wo_ref[...], _NT)
        sa, sb = _sigmoid(ga_ref[...]), _sigmoid(gb_ref[...])
        dya, dyb = (sa * dmv).astype(BF), (sb * dmv).astype(BF)
        dya_ref[...] = dya
        dyb_ref[...] = dyb
        dg_ref[:, :D] = (dmv * ya_ref[...] * sa * (1.0 - sa)).astype(BF)
        dg_ref[:, D:] = (dmv * yb_ref[...] * sb * (1.0 - sb)).astype(BF)
        dog_ref[...] = _dot(dya, wa_ref[...], _NT)
        dob_ref[...] = _dot(dyb, wb_ref[...], _NT)

    row = pl.BlockSpec((tm, D), lambda i: (i, 0))
    whole = lambda w: pl.BlockSpec(w.shape, lambda i: (0, 0))
    nb = w_b.shape[0]
    return pl.pallas_call(
        body, name="merge_proj_bwd", grid=(T // tm,),
        in_specs=[row, row, row, pl.BlockSpec((tm, D), lambda i: (i, GATE_COL0 // D)),
                  pl.BlockSpec((tm, D), lambda i: (i, GATE_COL0 // D + 1)), whole(w_a), whole(w_b), whole(w_out)],
        out_specs=[row, row, pl.BlockSpec((tm, 2 * D), lambda i: (i, 0)), row,
                   pl.BlockSpec((tm, nb), lambda i: (i, 0))],
        out_shape=[jax.ShapeDtypeStruct((T, D), BF), jax.ShapeDtypeStruct((T, D), BF),
                   jax.ShapeDtypeStruct((T, 2 * D), BF), jax.ShapeDtypeStruct((T, D), F32),
                   jax.ShapeDtypeStruct((T, nb), F32)],
        compiler_params=_params(("parallel",)),
    )(dout, ya, yb, z, z, w_a, w_b, w_out)


def _rope_inv_freq():
    inv = ROPE_THETA ** (-jnp.arange(0, 64, 2, dtype=F32) / 64)
    return jnp.tile(inv, 4).reshape(1, BLK)


def _local_step(x, pos, norm_w, lbl, hnw, wf, tgt, w_in, w_a, w_b, w_out, shard_shapes=()):
    invf = _rope_inv_freq()
    if shard_shapes:
        blk = jnp.reshape(2 * lax.axis_index("x") + lax.axis_index("y"), (1,)).astype(jnp.int32)
        h, cos, sa, sb, z_own, (w_near,) = _norm_and_rope_tables(
            x, norm_w, pos, invf, side=_gather_near_side(w_in, WEIGHT_AXES[0]), own=(w_in, blk))
        near = jnp.concatenate([blk ^ 2, blk ^ 1])
        z, (w_diag,) = _z_blocks(h, w_near, z_own, jnp.concatenate([near, near]), 2, name="z_proj_near",
                                 side=_gather_diag_side(w_near, w_in.shape, WEIGHT_AXES[0]))
        z, w_in, _ = _z_blocks(h, w_diag, z, jnp.concatenate([blk ^ 3, jnp.zeros_like(blk)]), 1, name="z_proj_diag",
                               fill=w_near, side=None)
        oraw, og, shist, (w_a, w_b, w_out) = _hgrn_fwd(
            z, lbl, hnw, side=_gather_side([w_a, w_b, w_out], WEIGHT_AXES[1:]))
    else:
        h, cos, sa, sb, _, _ = _norm_and_rope_tables(x, norm_w, pos, invf)
        z = _matmul(h, w_in, tm=T, tn=512, name="z_proj")
        oraw, og, shist, _ = _hgrn_fwd(z, lbl, hnw)
    ob, opre, lse, qs, ks, vs = _attn_fwd(z, cos, sa, sb)
    ya, yb, merged, dout, loss_vec, g_wf = _merge_out_loss(og, ob, z, w_a, w_b, w_out, x, tgt, wf)

    dya, dyb, dgates, dog, dob = _merge_proj_bwd(dout, ya, yb, z, w_a, w_b, w_out)
    g_wout = _matmul(merged, dout, ta=True, out_dtype=BF, tm=512, tn=1024, name="g_wout")
    g_wa = _matmul(og, dya, ta=True, out_dtype=BF, tm=512, tn=1024, name="g_wa")
    g_wb = _matmul(ob, dyb, ta=True, out_dtype=BF, tm=512, tn=1024, name="g_wb")
    small = [g_wa, g_wb, g_wout]
    side_s = side_w = None
    if shard_shapes:
        p3_s = _rs_partials(small, shard_shapes[1:], WEIGHT_AXES[1:], "small")
        side_s = _chip_exchange_side(p3_s, shard_shapes[1:], WEIGHT_AXES[1:])
    dz_h, dlb, g_hnw, land_s = _hgrn_bwd(z, lbl, hnw, oraw, dog, shist, side=side_s)
    dq, dk, dv, dag = _attn_bwd(z, qs, ks, vs, cos, sa, sb, opre, lse, dob)
    dz_parts = [dz_h, dq, dk, dv, dag, dgates]
    if shard_shapes:
        c = lax.axis_index("c")
        half = lambda i: jnp.reshape(i, (1,)).astype(jnp.int32)
        g_send = _grad_w_in_half(h, dz_parts, half(1 - c))
        g_keep, (g_sib,) = _grad_w_in_half(h, dz_parts, half(c), side=_sibling_send_side(g_send))
        p3_w = [_add_bf16(g_keep, g_sib, "pair_sum_w_in").reshape(1, D // 2, NIN)]
        side_w = _chip_exchange_side(p3_w, shard_shapes[:1], WEIGHT_AXES[:1])
    else:
        g_big = [_grad_w_in(h, dz_parts)] + small
    gx, g_nw, land_w = _grad_x(dz_parts, w_in, x, dout, norm_w, side=side_w)
    if shard_shapes:
        g_big = _rs_finish(p3_w + p3_s, list(land_w) + list(land_s), shard_shapes, WEIGHT_AXES)
    return dict(loss_vec=loss_vec, gx=gx, g_nw=g_nw, dlb=dlb, g_hnw=g_hnw, g_wf=g_wf,
                g_win=g_big[0], g_wa=g_big[1], g_wb=g_big[2], g_wout=g_big[3])


MESH = pl.DeviceIdType.MESH
HBM = pl.BlockSpec(memory_space=pl.ANY)
WEIGHT_AXES = (1, 0, 1, 0)


def _place():
    x, y, c = lax.axis_index("x"), lax.axis_index("y"), lax.axis_index("c")
    chips = [(1 - x, y), (x, 1 - y), (1 - x, 1 - y)]
    return x, y, c, chips


def _block_half(ref, shard_shape, axis, j, half):
    r, c = shard_shape
    hr = r // 2
    if axis == 0:
        return ref.at[pl.ds(pl.multiple_of(j * r + half * hr, 16), hr), :]
    return ref.at[pl.ds(pl.multiple_of(half * hr, 16), hr), pl.ds(pl.multiple_of(j * c, 128), c)]


class _Side:
    def __init__(self, arrays, out_shapes, sems, first, last):
        self.arrays, self.out_shapes, self.sems, self.first, self.last = arrays, out_shapes, sems, first, last


def _gather_side(shards, axes):
    n = len(shards)
    shapes = [s.shape for s in shards]

    def copies(ins, outs, sems):
        send1, recv1, send2, recv2, send0, recv0 = sems
        x, y, c, chips = _place()
        me = 2 * x + y
        sib = (x, y, 1 - c)
        near = ((1 - c) * (1 - x) + c * x, (1 - c) * y + c * (1 - y))
        far = ((1 - c) * x + c * (1 - x), (1 - c) * (1 - y) + c * y)
        out = []
        for a in range(n):
            r, cc = shapes[a]
            mine = (outs[a].at[pl.ds(pl.multiple_of(me * r, 16), r), :] if axes[a] == 0
                    else outs[a].at[:, pl.ds(pl.multiple_of(me * cc, 128), cc)])
            own = pltpu.make_async_remote_copy(
                src_ref=ins[a], dst_ref=mine, send_sem=send0.at[a], recv_sem=recv0.at[a],
                device_id=sib, device_id_type=MESH)
            src = ins[a].at[pl.ds(pl.multiple_of(c * (r // 2), 16), r // 2), :]
            sends = [pltpu.make_async_remote_copy(
                src_ref=src, dst_ref=_block_half(outs[a], shapes[a], axes[a], me, c),
                send_sem=send1.at[a, k], recv_sem=recv1.at[a, k], device_id=(*chips[k], c), device_id_type=MESH)
                for k in range(2)]

            def region(chip, half):
                return _block_half(outs[a], shapes[a], axes[a], 2 * chip[0] + chip[1], half)

            def arrival(chip, k):
                reg = region(chip, c)
                return pltpu.make_async_remote_copy(
                    src_ref=reg, dst_ref=reg, send_sem=send1.at[a, k], recv_sem=recv1.at[a, k],
                    device_id=(*chip, c), device_id_type=MESH)

            def to_sibling(chip, k):
                reg = region(chip, c)
                return pltpu.make_async_remote_copy(
                    src_ref=reg, dst_ref=reg, send_sem=send2.at[a, k], recv_sem=recv2.at[a, k],
                    device_id=sib, device_id_type=MESH)

            def from_sibling(chip, k):
                reg = region(chip, 1 - c)
                return pltpu.make_async_remote_copy(
                    src_ref=reg, dst_ref=reg, send_sem=send2.at[a, k], recv_sem=recv2.at[a, k],
                    device_id=sib, device_id_type=MESH)

            relay = pltpu.make_async_remote_copy(
                src_ref=region(near, c), dst_ref=region(near, c), send_sem=send1.at[a, 2], recv_sem=recv1.at[a, 2],
                device_id=(*far, c), device_id_type=MESH)
            hops = [(arrival(near, c), to_sibling(near, c)), (arrival(far, 1 - c), to_sibling(far, 1 - c)),
                    (arrival(chips[2], 2), to_sibling(chips[2], 2))]
            back = [from_sibling(chips[k], k) for k in range(3)]
            out.append((own, sends, relay, hops, back))
        return out

    def first(ins, outs, sems):
        for own, sends, _, _, _ in copies(ins, outs, sems):
            own.start()
            for cp in sends:
                cp.start()

    def last(ins, outs, sems):
        per_array = copies(ins, outs, sems)
        for step in range(3):
            for _, _, relay, hops, _ in per_array:
                arrived, onward = hops[step]
                arrived.wait_recv()
                if step == 0:
                    relay.start()
                onward.start()
        for own, sends, relay, hops, back in per_array:
            for cp in back:
                cp.wait_recv()
            for cp in sends + [relay] + [onward for _, onward in hops]:
                cp.wait_send()
            own.wait()

    full = [(4 * r, c) if ax == 0 else (r, 4 * c) for (r, c), ax in zip(shapes, axes)]
    sems = [pltpu.SemaphoreType.DMA((n, 3)), pltpu.SemaphoreType.DMA((n, 3)),
            pltpu.SemaphoreType.DMA((n, 3)), pltpu.SemaphoreType.DMA((n, 3)),
            pltpu.SemaphoreType.DMA((n,)), pltpu.SemaphoreType.DMA((n,))]
    return _Side(list(shards), [jax.ShapeDtypeStruct(f, BF) for f in full], sems, first, last)


def _gather_near_side(shard, axis):
    shape = shard.shape
    r, cc = shape

    def copies(ins, outs, sems):
        send1, recv1, send2, recv2, send0, recv0 = sems
        x, y, c, chips = _place()
        me = 2 * x + y
        sib = (x, y, 1 - c)
        mine = (outs[0].at[pl.ds(pl.multiple_of(me * r, 16), r), :] if axis == 0
                else outs[0].at[:, pl.ds(pl.multiple_of(me * cc, 128), cc)])
        own = pltpu.make_async_remote_copy(
            src_ref=ins[0], dst_ref=mine, send_sem=send0.at[0], recv_sem=recv0.at[0],
            device_id=sib, device_id_type=MESH)
        src = ins[0].at[pl.ds(pl.multiple_of(c * (r // 2), 16), r // 2), :]

        def region(k, half):
            return _block_half(outs[0], shape, axis, 2 * chips[k][0] + chips[k][1], half)

        def moves(k):
            return [pltpu.make_async_remote_copy(
                        src_ref=s, dst_ref=d, send_sem=ss.at[k], recv_sem=rs.at[k], device_id=dev,
                        device_id_type=MESH)
                    for s, d, ss, rs, dev in (
                        (src, _block_half(outs[0], shape, axis, me, c), send1, recv1, (*chips[k], c)),
                        (region(k, c), region(k, c), send1, recv1, (*chips[k], c)),
                        (region(k, c), region(k, c), send2, recv2, sib),
                        (region(k, 1 - c), region(k, 1 - c), send2, recv2, sib))]

        return own, [moves(k) for k in range(2)]

    def first(ins, outs, sems):
        own, per_chip = copies(ins, outs, sems)
        own.start()
        for send, _, _, _ in per_chip:
            send.start()

    def last(ins, outs, sems):
        own, per_chip = copies(ins, outs, sems)
        for _, arrived, onward, _ in per_chip:
            arrived.wait_recv()
            onward.start()
        for send, _, onward, back in per_chip:
            back.wait_recv()
            send.wait_send()
            onward.wait_send()
        own.wait()

    full = (4 * r, cc) if axis == 0 else (r, 4 * cc)
    sems = [pltpu.SemaphoreType.DMA((2,))] * 4 + [pltpu.SemaphoreType.DMA((1,))] * 2
    return _Side([shard], [jax.ShapeDtypeStruct(full, BF)], sems, first, last)


def _gather_diag_side(gathered, shape, axis):
    r, cc = shape

    def copies(ins, outs, sems):
        send1, recv1, send2, recv2 = sems
        x, y, c, _ = _place()
        sib = (x, y, 1 - c)
        near = ((1 - c) * (1 - x) + c * x, (1 - c) * y + c * (1 - y))
        far = ((1 - c) * x + c * (1 - x), (1 - c) * (1 - y) + c * y)

        def half(i):
            return outs[0].at[pl.ds(pl.multiple_of(i * (r // 2), 16), r // 2), :]

        def move(s, d, ss, rs, dev):
            return pltpu.make_async_remote_copy(
                src_ref=s, dst_ref=d, send_sem=ss.at[0], recv_sem=rs.at[0], device_id=dev, device_id_type=MESH)

        relay = move(_block_half(ins[0], shape, axis, 2 * near[0] + near[1], c), half(c), send1, recv1, (*far, c))
        arrived = move(half(c), half(c), send1, recv1, (*far, c))
        onward = move(half(c), half(c), send2, recv2, sib)
        back = move(half(1 - c), half(1 - c), send2, recv2, sib)
        return relay, arrived, onward, back

    def first(ins, outs, sems):
        copies(ins, outs, sems)[0].start()

    def last(ins, outs, sems):
        relay, arrived, onward, back = copies(ins, outs, sems)
        arrived.wait_recv()
        onward.start()
        back.wait_recv()
        relay.wait_send()
        onward.wait_send()

    return _Side([gathered], [jax.ShapeDtypeStruct(shape, BF)], [pltpu.SemaphoreType.DMA((1,))] * 4, first, last)


def _as3d(g, shard_shape, axis):
    r, c = shard_shape
    return g.reshape(4, r, c) if axis == 0 else g.reshape(1, r, 4 * c)


def _half_rows(ref3, hr, half):
    return ref3.at[:, pl.ds(pl.multiple_of(half * hr, 16), hr), :]


def _rs_pair_exchange(g3s, name):
    n = len(g3s)

    def body(*refs):
        ins, outs = refs[:n], refs[n:2 * n]
        send, recv = refs[2 * n:]
        x, y, c, _ = _place()
        cps = []
        for a in range(n):
            hr = g3s[a].shape[1] // 2
            cp = pltpu.make_async_remote_copy(
                src_ref=_half_rows(ins[a], hr, 1 - c), dst_ref=outs[a],
                send_sem=send.at[a], recv_sem=recv.at[a], device_id=(x, y, 1 - c), device_id_type=MESH)
            cp.start()
            cps.append(cp)
        for cp in cps:
            cp.wait()

    return pl.pallas_call(
        body, name=name,
        in_specs=[HBM] * n, out_specs=[HBM] * n,
        out_shape=[jax.ShapeDtypeStruct((g.shape[0], g.shape[1] // 2, g.shape[2]), BF) for g in g3s],
        scratch_shapes=[pltpu.SemaphoreType.DMA((n,)), pltpu.SemaphoreType.DMA((n,))],
    )(*g3s)


def _pair_sum(g3, land, cidx, name):
    nb, r, w = g3.shape
    hr = r // 2
    tr = 64

    def body(c_ref, g_ref, l_ref, o_ref):
        o_ref[...] = (g_ref[...].astype(F32) + l_ref[...].astype(F32)).astype(BF)

    blk = (nb, tr, w)
    return pl.pallas_call(
        body, name=name,
        grid_spec=pltpu.PrefetchScalarGridSpec(
            num_scalar_prefetch=1, grid=(hr // tr,),
            in_specs=[pl.BlockSpec(blk, lambda i, c: (0, c[0] * (hr // tr) + i, 0)),
                      pl.BlockSpec(blk, lambda i, c: (0, i, 0))],
            out_specs=pl.BlockSpec(blk, lambda i, c: (0, i, 0))),
        out_shape=jax.ShapeDtypeStruct((nb, hr, w), BF),
        compiler_params=_params(("parallel",)),
    )(cidx, g3, land)


def _chip_exchange_side(p3s, shapes, axes):
    n = len(p3s)

    def copies(ins, outs, sems):
        send, recv = sems
        x, y, c, chips = _place()
        cps = []
        for a in range(n):
            r, cc = shapes[a]
            for k, (px, py) in enumerate(chips):
                j = 2 * px + py
                src = ins[a].at[j] if axes[a] == 0 else ins[a].at[0, :, pl.ds(pl.multiple_of(j * cc, 128), cc)]
                cps.append(pltpu.make_async_remote_copy(
                    src_ref=src, dst_ref=outs[a].at[k], send_sem=send.at[a, k], recv_sem=recv.at[a, k],
                    device_id=(px, py, c), device_id_type=MESH))
        return cps

    def first(ins, outs, sems):
        for cp in copies(ins, outs, sems):
            cp.start()

    def last(ins, outs, sems):
        for cp in copies(ins, outs, sems):
            cp.wait()

    return _Side(list(p3s), [jax.ShapeDtypeStruct((3, r // 2, c), BF) for r, c in shapes],
                 [pltpu.SemaphoreType.DMA((n, 3)), pltpu.SemaphoreType.DMA((n, 3))], first, last)


def _chip_sum(p3, land, shard_shape, axis, idx, name):
    r, c = shard_shape
    hr = r // 2
    tr = 64
    nt = hr // tr

    def body(idx_ref, p_ref, l_ref, o_ref):
        acc = p_ref[...].astype(F32)
        for k in range(3):
            acc = acc + l_ref[k].astype(F32)
        o_ref[...] = acc

    own = (pl.BlockSpec((None, tr, c), lambda i, idx: (idx[0], i, 0)) if axis == 0
           else pl.BlockSpec((None, tr, c), lambda i, idx: (0, i, idx[0])))
    return pl.pallas_call(
        body, name=name,
        grid_spec=pltpu.PrefetchScalarGridSpec(
            num_scalar_prefetch=1, grid=(nt,),
            in_specs=[own, pl.BlockSpec((3, tr, c), lambda i, idx: (0, i, 0))],
            out_specs=pl.BlockSpec((tr, c), lambda i, idx: (idx[1] * nt + i, 0))),
        out_shape=jax.ShapeDtypeStruct((r, c), F32),
        compiler_params=_params(("parallel",)),
    )(idx, p3, land)


def _rs_pair_gather(fulls):
    n = len(fulls)

    def body(*refs):
        ins, outs = refs[:n], refs[n:2 * n]
        send, recv = refs[2 * n:]
        x, y, c, _ = _place()
        cps = []
        for a in range(n):
            hr = fulls[a].shape[0] // 2
            rows = pl.ds(pl.multiple_of(c * hr, 8), hr)
            cp = pltpu.make_async_remote_copy(
                src_ref=ins[a].at[rows, :], dst_ref=outs[a].at[rows, :], send_sem=send.at[a], recv_sem=recv.at[a],
                device_id=(x, y, 1 - c), device_id_type=MESH)
            cp.start()
            cps.append(cp)
        for a, cp in enumerate(cps):
            cp.wait_send()
            hr = fulls[a].shape[0] // 2
            other = pl.ds(pl.multiple_of((1 - c) * hr, 8), hr)
            pltpu.make_async_remote_copy(
                src_ref=ins[a].at[other, :], dst_ref=outs[a].at[other, :], send_sem=send.at[a], recv_sem=recv.at[a],
                device_id=(x, y, 1 - c), device_id_type=MESH).wait_recv()

    return pl.pallas_call(
        body, name="grads_pair_gather",
        in_specs=[HBM] * n, out_specs=[HBM] * n,
        out_shape=[jax.ShapeDtypeStruct(f.shape, F32) for f in fulls],
        input_output_aliases={a: a for a in range(n)},
        scratch_shapes=[pltpu.SemaphoreType.DMA((n,)), pltpu.SemaphoreType.DMA((n,))],
    )(*fulls)


def _sibling_send_side(arr):
    def copy(ins, outs, sems):
        x, y, c, _ = _place()
        return pltpu.make_async_remote_copy(
            src_ref=ins[0], dst_ref=outs[0], send_sem=sems[0].at[0], recv_sem=sems[1].at[0],
            device_id=(x, y, 1 - c), device_id_type=MESH)

    return _Side([arr], [jax.ShapeDtypeStruct(arr.shape, arr.dtype)],
                 [pltpu.SemaphoreType.DMA((1,)), pltpu.SemaphoreType.DMA((1,))],
                 lambda ins, outs, sems: copy(ins, outs, sems).start(),
                 lambda ins, outs, sems: copy(ins, outs, sems).wait())


def _add_bf16(a, b, name):
    r, c = a.shape
    tr = 64

    def body(a_ref, b_ref, o_ref):
        o_ref[...] = (a_ref[...].astype(F32) + b_ref[...].astype(F32)).astype(BF)

    blk = pl.BlockSpec((tr, c), lambda i: (i, 0))
    return pl.pallas_call(
        body, name=name, grid=(r // tr,), in_specs=[blk, blk], out_specs=blk,
        out_shape=jax.ShapeDtypeStruct((r, c), BF), compiler_params=_params(("parallel",)),
    )(a, b)


def _rs_partials(grads, shapes, axes, tag):
    cidx = jnp.reshape(lax.axis_index("c"), (1,)).astype(jnp.int32)
    g3s = [_as3d(g, s, ax) for g, s, ax in zip(grads, shapes, axes)]
    lands = _rs_pair_exchange(g3s, f"grads_pair_exchange_{tag}")
    return [_pair_sum(g3, l, cidx, f"pair_sum_{tag}_{a}") for a, (g3, l) in enumerate(zip(g3s, lands))]


def _rs_finish(p3s, landed, shapes, axes):
    x, y, c = lax.axis_index("x"), lax.axis_index("y"), lax.axis_index("c")
    idx = jnp.stack([2 * x + y, c]).astype(jnp.int32)
    fulls = [_chip_sum(p3, l2, s, ax, idx, f"chip_sum_{a}")
             for a, (p3, l2, s, ax) in enumerate(zip(p3s, landed, shapes, axes))]
    return _rs_pair_gather(fulls)


NSMALL = 8


def _small_all_reduce(g_nw, dlb, g_hnw, g_wf, loss_vec):
    def body(nw_ref, lb_ref, hn_ref, wf_ref, ls_ref, out_ref, pack_ref, buf_ref, send, recv):
        x, y, c = lax.axis_index("x"), lax.axis_index("y"), lax.axis_index("c")
        me = 4 * x + 2 * y + c
        pack_ref[...] = jnp.zeros_like(pack_ref)
        pack_ref[0:1, :] = nw_ref[...]
        pack_ref[1:2, :] = lb_ref[...]
        pack_ref[2:3, 0:HK] = hn_ref[...]
        pack_ref[3:4, :] = wf_ref[...]
        pack_ref[4:5, :] = ls_ref[...]
        buf_ref[me] = pack_ref[...]
        cps = []
        for d in range(1, 8):
            dx, dy, dc = d >> 2, (d >> 1) & 1, d & 1
            peer = (1 - x if dx else x, 1 - y if dy else y, 1 - c if dc else c)
            cp = pltpu.make_async_remote_copy(
                src_ref=pack_ref, dst_ref=buf_ref.at[me], send_sem=send.at[d - 1], recv_sem=recv.at[d - 1],
                device_id=peer, device_id_type=MESH)
            cp.start()
            cps.append(cp)
        for d in range(1, 8):
            dx, dy, dc = d >> 2, (d >> 1) & 1, d & 1
            src = 4 * (1 - x if dx else x) + 2 * (1 - y if dy else y) + (1 - c if dc else c)
            pltpu.make_async_remote_copy(
                src_ref=pack_ref, dst_ref=buf_ref.at[src], send_sem=send.at[d - 1], recv_sem=recv.at[d - 1],
                device_id=(x, y, c), device_id_type=MESH).wait_recv()
        for cp in cps:
            cp.wait_send()
        acc = buf_ref[0]
        for i in range(1, 8):
            acc = acc + buf_ref[i]
        out_ref[...] = acc

    vm = pl.BlockSpec(memory_space=pltpu.VMEM)
    return pl.pallas_call(
        body, name="small_all_reduce",
        in_specs=[vm] * 5, out_specs=vm,
        out_shape=jax.ShapeDtypeStruct((NSMALL, D), F32),
        scratch_shapes=[pltpu.VMEM((NSMALL, D), F32), pltpu.VMEM((8, NSMALL, D), F32),
                        pltpu.SemaphoreType.DMA((7,)), pltpu.SemaphoreType.DMA((7,))],
    )(g_nw, dlb, g_hnw, g_wf, loss_vec)


def _adamw_math(w, g, m, v):
    m = B1 * m + (1.0 - B1) * g
    v = B2 * v + (1.0 - B2) * (g * g)
    m_hat = m / (1.0 - B1 ** STEP)
    v_hat = v / (1.0 - B2 ** STEP)
    return -LR * (m_hat / (jnp.sqrt(v_hat) + ADAM_EPS) + WD * w), m, v


def _adamw(w, g, m, v, name):
    r, c = w.shape
    tr = 64

    def body(w_ref, g_ref, m_ref, v_ref, d_ref, nm_ref, nv_ref, go_ref):
        g = g_ref[...]
        d_ref[...], nm_ref[...], nv_ref[...] = _adamw_math(w_ref[...], g, m_ref[...], v_ref[...])
        go_ref[...] = g

    blk = pl.BlockSpec((tr, c), lambda i: (i, 0))
    return pl.pallas_call(
        body, name=name, grid=(r // tr,), in_specs=[blk] * 4, out_specs=[blk] * 4,
        out_shape=[jax.ShapeDtypeStruct((r, c), F32)] * 4,
        compiler_params=_params(("parallel",)),
    )(w, g, m, v)


def _small_update(red, lbl, params):
    def body(red_ref, *refs):
        ins, outs = refs[:12], refs[12:]
        lb = _lower_bound(ins[3][...])
        dl0 = red_ref[1:2, :] * lb * (1.0 - lb)
        row = lax.broadcasted_iota(jnp.int32, (2, D), 0)
        grads = [red_ref[0:1, :], jnp.where(row == 0, dl0, -dl0), red_ref[2:3, 0:HK], red_ref[3:4, :]]
        for i, g in enumerate(grads):
            w, m, v = ins[3 * i][...], ins[3 * i + 1][...], ins[3 * i + 2][...]
            d, nm, nv = _adamw_math(w, g, m, v)
            outs[4 * i][...] = g
            outs[4 * i + 1][...] = d
            outs[4 * i + 2][...] = nm
            outs[4 * i + 3][...] = nv
        outs[16][...] = jnp.sum(red_ref[4:5, :], axis=1, keepdims=True)

    flat = [a for p in params for a in p]
    vm = pl.BlockSpec(memory_space=pltpu.VMEM)
    shapes = [jax.ShapeDtypeStruct(p[0].shape, F32) for p in params for _ in range(4)]
    return pl.pallas_call(
        body, name="small_update",
        in_specs=[vm] * 13, out_specs=[vm] * 17,
        out_shape=shapes + [jax.ShapeDtypeStruct((1, 1), F32)],
    )(red, *flat)


def kernel(x, positions, norm_w, w_in, lb_logits, hgrn_norm_w, w_branch_a, w_branch_b, w_out, final_norm_w, loss_target, m_norm_w, m_w_in, m_lb_logits, m_hgrn_norm_w, m_w_branch_a, m_w_branch_b, m_w_out, m_final_norm_w, v_norm_w, v_w_in, v_lb_logits, v_hgrn_norm_w, v_w_branch_a, v_w_branch_b, v_w_out, v_final_norm_w):
    big_w = [w_in[0], w_branch_a[0], w_branch_b[0], w_out[0]]
    big_m = [m_w_in[0], m_w_branch_a[0], m_w_branch_b[0], m_w_out[0]]
    big_v = [v_w_in[0], v_w_branch_a[0], v_w_branch_b[0], v_w_out[0]]
    shapes = [w.shape for w in big_w]
    wf = final_norm_w.reshape(1, D)

    shards = [w.astype(BF) for w in big_w]
    loc = _local_step(x[0], positions.reshape(T, 1), norm_w, lb_logits, hgrn_norm_w, wf, loss_target[0],
                      *shards, shard_shapes=shapes)
    g_big = [loc["g_win"], loc["g_wa"], loc["g_wb"], loc["g_wout"]]
    red = _small_all_reduce(loc["g_nw"], loc["dlb"], loc["g_hnw"], loc["g_wf"], loc["loss_vec"])

    small = _small_update(red, lb_logits, [
        (norm_w, m_norm_w, v_norm_w), (lb_logits, m_lb_logits, v_lb_logits),
        (hgrn_norm_w, m_hgrn_norm_w, v_hgrn_norm_w),
        (wf, m_final_norm_w.reshape(1, D), v_final_norm_w.reshape(1, D))])
    loss = small[16].reshape(())
    sg, sd, sm, sv = ([small[4 * i + j] for i in range(4)] for j in range(4))
    for lst in (sg, sd, sm, sv):
        lst[3] = lst[3].reshape(D)
    upd = [_adamw(w, g, m, v, f"adamw_{a}") for a, (w, g, m, v) in enumerate(zip(big_w, g_big, big_m, big_v))]
    bd, bm, bv, bg = ([u[j][None] for u in upd] for j in range(4))

    def order(s, b):
        return [s[0], b[0], s[1], s[2], b[1], b[2], b[3], s[3]]

    return (loss, loc["gx"][None], *order(sg, bg), *order(sd, bd), *order(sm, bm), *order(sv, bv))
```

```python
import functools

import jax
import jax.numpy as jnp
from jax import lax
from jax.experimental import pallas as pl
from jax.experimental.pallas import tpu as pltpu

T = 2048
D = 1024
NIN = 11264
HEADS = 8
HK = 128
CH = 16
NCH = T // CH
HSTEP = 2
ATT_GROUPS = ((128, 1), (512, 4), (2048, 16))
ATT_COL0 = 4096
AG_COL0 = 8704
GATE_COL0 = 9216
EPS = 1e-6
ROPE_THETA = 10000.0
LR, B1, B2, ADAM_EPS, WD, STEP = 0.001, 0.9, 0.999, 1e-08, 0.01, 10

F32 = jnp.float32
BF = jnp.bfloat16
VMEM_LIMIT = 56 * 1024 * 1024

_NN = (((1,), (0,)), ((), ()))
_NT = (((1,), (1,)), ((), ()))
_TN = (((0,), (0,)), ((), ()))


def _dot(a, b, dims=_NN):
    return lax.dot_general(a, b, dims, preferred_element_type=F32)


def _bdot(a, b, dims=_NN):
    return lax.dot_general(a.astype(BF), b.astype(BF), dims, preferred_element_type=F32)


def _sigmoid(x):
    return jax.nn.sigmoid(x)


def _params(sem=None):
    return pltpu.CompilerParams(dimension_semantics=sem, vmem_limit_bytes=VMEM_LIMIT)


def _matmul(a, b, *, ta=False, tb=False, out_dtype=F32, tm=512, tn=512, tk=None, name, side=None):
    m = a.shape[1] if ta else a.shape[0]
    kdim = a.shape[0] if ta else a.shape[1]
    n = b.shape[0] if tb else b.shape[1]
    tk = tk or kdim
    tm, tn = min(tm, m), min(tn, n)
    nm, nn, nk = m // tm, n // tn, kdim // tk
    dims = (((0 if ta else 1,), (1 if tb else 0,)), ((), ()))
    s_arrays, s_in_specs, s_shapes, s_out_specs, s_sems = _side_io(side)
    na, no = len(s_arrays), len(s_shapes)
    nacc = 1 if nk > 1 else 0

    def body(*refs):
        a_ref, b_ref = refs[:2]
        s_ins, o_ref, s_outs = refs[2:2 + na], refs[2 + na], refs[3 + na:3 + na + no]
        scratch = refs[3 + na + no:]
        s_sem_refs = scratch[nacc:]
        i, j, k = pl.program_id(0), pl.program_id(1), pl.program_id(2)
        if side is not None:
            @pl.when((i == 0) & (j == 0) & (k == 0))
            def _():
                side.first(s_ins, s_outs, s_sem_refs)

        prod = _bdot(a_ref[...], b_ref[...], dims)
        if nk == 1:
            o_ref[...] = prod.astype(out_dtype)
        else:
            acc = scratch[0]

            @pl.when(k == 0)
            def _():
                acc[...] = prod

            @pl.when(k > 0)
            def _():
                acc[...] += prod

            @pl.when(k == nk - 1)
            def _():
                o_ref[...] = acc[...].astype(out_dtype)

        if side is not None:
            @pl.when((i == nm - 1) & (j == nn - 1) & (k == nk - 1))
            def _():
                side.last(s_ins, s_outs, s_sem_refs)

    a_spec = pl.BlockSpec((tk, tm), lambda i, j, k: (k, i)) if ta else pl.BlockSpec((tm, tk), lambda i, j, k: (i, k))
    b_spec = pl.BlockSpec((tn, tk), lambda i, j, k: (j, k)) if tb else pl.BlockSpec((tk, tn), lambda i, j, k: (k, j))
    sem = ("parallel", "parallel", "arbitrary") if side is None else ("arbitrary",) * 3
    out = pl.pallas_call(
        body, name=name, grid=(nm, nn, nk),
        in_specs=[a_spec, b_spec] + s_in_specs,
        out_specs=[pl.BlockSpec((tm, tn), lambda i, j, k: (i, j))] + s_out_specs,
        out_shape=[jax.ShapeDtypeStruct((m, n), out_dtype)] + s_shapes,
        scratch_shapes=([pltpu.VMEM((tm, tn), F32)] if nk > 1 else []) + s_sems,
        compiler_params=_params(sem),
    )(a, b, *s_arrays)
    return out[0] if side is None else (out[0], out[1:])


DZ_TILE = 512


def _part_offsets(parts):
    counts = [p.shape[1] // DZ_TILE for p in parts]
    offs = [sum(counts[:i]) for i in range(len(parts))]
    return counts, offs


def _part_spec(rows, cnt, off, tile_axis):
    def index(*g):
        return (0 if rows is None else g[0], jnp.clip(g[tile_axis] - off, 0, cnt - 1))
    return index


def _grad_w_in(h, parts):
    counts, offs = _part_offsets(parts)
    n = len(parts)

    def body(h_ref, *refs):
        o_ref = refs[n]
        j = pl.program_id(0)
        for p_ref, cnt, off in zip(refs[:n], counts, offs):
            @pl.when((j >= off) & (j < off + cnt))
            def _(p_ref=p_ref):
                o_ref[...] = _bdot(h_ref[...], p_ref[...], _TN).astype(BF)

    return pl.pallas_call(
        body, name="g_win", grid=(sum(counts),),
        in_specs=[pl.BlockSpec((T, D), lambda j: (0, 0))] +
                 [pl.BlockSpec((T, DZ_TILE), _part_spec(None, c, o, 0)) for c, o in zip(counts, offs)],
        out_specs=pl.BlockSpec((D, DZ_TILE), lambda j: (0, j)),
        out_shape=jax.ShapeDtypeStruct((D, NIN), BF),
        compiler_params=_params(("parallel",)),
    )(h, *parts)


def _grad_w_in_half(h, parts, half_idx, side=None):
    counts, offs = _part_offsets(parts)
    n = len(parts)
    nj = sum(counts)
    s_arrays, s_in_specs, s_shapes, s_out_specs, s_sems = _side_io(side)
    na, no = len(s_arrays), len(s_shapes)

    def body(idx_ref, h_ref, *refs):
        s_ins, o_ref, s_outs, s_sem_refs = refs[n:n + na], refs[n + na], refs[n + na + 1:n + na + 1 + no], refs[n + na + 1 + no:]
        j = pl.program_id(0)
        if side is not None:
            @pl.when(j == 0)
            def _():
                side.first(s_ins, s_outs, s_sem_refs)

        for p_ref, cnt, off in zip(refs[:n], counts, offs):
            @pl.when((j >= off) & (j < off + cnt))
            def _(p_ref=p_ref):
                o_ref[...] = _bdot(h_ref[...], p_ref[...], _TN).astype(BF)

        if side is not None:
            @pl.when(j == nj - 1)
            def _():
                side.last(s_ins, s_outs, s_sem_refs)

    def part_spec(cnt, off):
        return pl.BlockSpec((T, DZ_TILE), lambda j, idx: (0, jnp.clip(j - off, 0, cnt - 1)))

    out = pl.pallas_call(
        body, name="g_win_half" if side is None else "g_win_half_carrying",
        grid_spec=pltpu.PrefetchScalarGridSpec(
            num_scalar_prefetch=1, grid=(nj,),
            in_specs=[pl.BlockSpec((T, D // 2), lambda j, idx: (0, idx[0]))] +
                     [part_spec(c, o) for c, o in zip(counts, offs)] + s_in_specs,
            out_specs=[pl.BlockSpec((D // 2, DZ_TILE), lambda j, idx: (0, j))] + s_out_specs,
            scratch_shapes=s_sems),
        out_shape=[jax.ShapeDtypeStruct((D // 2, NIN), BF)] + s_shapes,
        compiler_params=_params(("parallel",) if side is None else ("arbitrary",)),
    )(half_idx, h, *parts, *s_arrays)
    return out[0] if side is None else (out[0], out[1:])


def _side_io(side):
    if side is None:
        return [], [], [], [], []
    return (side.arrays, [HBM] * len(side.arrays), side.out_shapes, [HBM] * len(side.out_shapes), side.sems)


def _grad_x(parts, w_in, x, dout, norm_w, side=None):
    counts, offs = _part_offsets(parts)
    n = len(parts)
    tm = 1024
    nm, nk = T // tm, sum(counts)
    s_arrays, s_in_specs, s_shapes, s_out_specs, s_sems = _side_io(side)
    na, no = len(s_arrays), len(s_shapes)

    def body(*refs):
        w_ref, x_ref, dout_ref, nw_ref = refs[n:n + 4]
        s_ins = refs[n + 4:n + 4 + na]
        gx_ref, gw_ref = refs[n + 4 + na:n + 6 + na]
        s_outs = refs[n + 6 + na:n + 6 + na + no]
        acc = refs[n + 6 + na + no]
        s_sem_refs = refs[n + 7 + na + no:]
        i, k = pl.program_id(0), pl.program_id(1)

        @pl.when((i == 0) & (k == 0))
        def _():
            gw_ref[...] = jnp.zeros_like(gw_ref)
            if side is not None:
                side.first(s_ins, s_outs, s_sem_refs)

        @pl.when(k == 0)
        def _():
            acc[...] = jnp.zeros_like(acc)

        for p_ref, cnt, off in zip(refs[:n], counts, offs):
            @pl.when((k >= off) & (k < off + cnt))
            def _(p_ref=p_ref):
                acc[...] += _bdot(p_ref[...], w_ref[...], _NT)

        @pl.when(k == nk - 1)
        def _():
            gw = jnp.zeros((1, D), F32)
            for c in range(tm // BLK):
                rows = pl.ds(BLK * c, BLK)
                xv, dhv = x_ref[rows, :], acc[rows, :]
                r = lax.rsqrt(jnp.mean(xv * xv, axis=-1, keepdims=True) + EPS)
                nrm = xv * r
                dn = dhv * nw_ref[...]
                gw = gw + jnp.sum(dhv * nrm, axis=0, keepdims=True)
                gx_ref[rows, :] = dout_ref[rows, :] + r * (dn - nrm * jnp.mean(dn * nrm, axis=-1, keepdims=True))
            gw_ref[...] += gw

        if side is not None:
            @pl.when((i == nm - 1) & (k == nk - 1))
            def _():
                side.last(s_ins, s_outs, s_sem_refs)

    row = pl.BlockSpec((tm, D), lambda i, k: (i, 0))
    vec = pl.BlockSpec((1, D), lambda i, k: (0, 0))
    out = pl.pallas_call(
        body, name="grad_x", grid=(nm, nk),
        in_specs=[pl.BlockSpec((tm, DZ_TILE), _part_spec(0, c, o, 1)) for c, o in zip(counts, offs)] +
                 [pl.BlockSpec((D, DZ_TILE), lambda i, k: (0, k)), row, row, vec] + s_in_specs,
        out_specs=[row, vec] + s_out_specs,
        out_shape=[jax.ShapeDtypeStruct((T, D), F32), jax.ShapeDtypeStruct((1, D), F32)] + s_shapes,
        scratch_shapes=[pltpu.VMEM((tm, D), F32)] + s_sems,
        compiler_params=_params(("arbitrary", "arbitrary")),
    )(*parts, w_in, x, dout, norm_w, *s_arrays)
    return out[0], out[1], out[2:]


def _norm_and_rope_tables(x, w, pos, invf, side=None, own=None):
    tm = 256
    nm = T // tm
    s_arrays, s_in_specs, s_shapes, s_out_specs, s_sems = _side_io(side)
    na, no = len(s_arrays), len(s_shapes)
    nz = 0 if own is None else 1
    wsh, blk = own if own is not None else (None, jnp.zeros((1,), jnp.int32))

    def body(blk_ref, *refs):
        x_ref, w_ref, pos_ref, invf_ref = refs[:4]
        s_ins = refs[4 + nz:4 + nz + na]
        h_ref, cos_ref, sa_ref, sb_ref = refs[4 + nz + na:8 + nz + na]
        s_outs = refs[8 + 2 * nz + na:8 + 2 * nz + na + no]
        s_sem_refs = refs[8 + 2 * nz + na + no:]
        if side is not None:
            @pl.when(pl.program_id(0) == 0)
            def _():
                side.first(s_ins, s_outs, s_sem_refs)

        xv = x_ref[...]
        r = lax.rsqrt(jnp.mean(xv * xv, axis=-1, keepdims=True) + EPS)
        h = (xv * r * w_ref[...]).astype(BF)
        h_ref[...] = h
        if own is not None:
            refs[8 + nz + na][...] = _dot(h, refs[4][...])
        first = (lax.broadcasted_iota(jnp.int32, (tm, 128), 1) % 64) < 32
        ang = pos_ref[...].astype(F32) * invf_ref[...]
        s = jnp.sin(ang)
        cos_ref[...] = jnp.cos(ang)
        sa_ref[...] = jnp.where(first, -s, 0.0)
        sb_ref[...] = jnp.where(first, 0.0, s)
        if side is not None:
            @pl.when(pl.program_id(0) == nm - 1)
            def _():
                side.last(s_ins, s_outs, s_sem_refs)

    tab = pl.BlockSpec((tm, 128), lambda i, b: (i, 0))
    own_in = [] if own is None else [pl.BlockSpec(wsh.shape, lambda i, b: (0, 0))]
    own_out = [] if own is None else [pl.BlockSpec((tm, wsh.shape[1]), lambda i, b: (i, b[0]))]
    own_shape = [] if own is None else [jax.ShapeDtypeStruct((T, NIN), F32)]
    out = pl.pallas_call(
        body, name="norm_and_rope_tables",
        grid_spec=pltpu.PrefetchScalarGridSpec(
            num_scalar_prefetch=1, grid=(nm,),
            in_specs=[pl.BlockSpec((tm, D), lambda i, b: (i, 0)), pl.BlockSpec((1, D), lambda i, b: (0, 0)),
                      pl.BlockSpec((tm, 1), lambda i, b: (i, 0)), pl.BlockSpec((1, 128), lambda i, b: (0, 0))]
                     + own_in + s_in_specs,
            out_specs=[pl.BlockSpec((tm, D), lambda i, b: (i, 0)), tab, tab, tab] + own_out + s_out_specs,
            scratch_shapes=s_sems),
        out_shape=[jax.ShapeDtypeStruct((T, D), BF)] + [jax.ShapeDtypeStruct((T, 128), F32)] * 3 + own_shape + s_shapes,
        compiler_params=_params(("parallel",) if side is None else ("arbitrary",)),
    )(blk, x, w, pos, invf, *([] if own is None else [wsh]), *s_arrays)
    return out[0], out[1], out[2], out[3], (out[4] if own is not None else None), out[4 + nz:]


def _z_blocks(h, w, z, idx, nb, side, name, fill=None):
    tm, tn = 1024, NIN // 8
    s_arrays, s_in_specs, s_shapes, s_out_specs, s_sems = _side_io(side)
    na, no = len(s_arrays), len(s_shapes)
    nm, ns = T // tm, 2 * nb
    nf = 0 if fill is None else 1

    def col(first, i, s, b):
        return (0, b[first + s // 2] * 2 + s % 2)

    def body(idx_ref, h_ref, w_ref, zin_ref, *refs):
        s_ins = refs[nf:nf + na]
        o_ref = refs[nf + na]
        s_outs = refs[nf + na + 1 + nf:nf + na + 1 + nf + no]
        s_sem_refs = refs[nf + na + 1 + nf + no + nf:]
        i, s = pl.program_id(0), pl.program_id(1)

        if side is not None:
            @pl.when((i == 0) & (s == 0))
            def _():
                side.first(s_ins, s_outs, s_sem_refs)

        if fill is not None:
            tile = pl.ds(pl.multiple_of((idx_ref[0] * 2 + s) * tn, 128), tn)
            store = pltpu.make_async_copy(w_ref, refs[nf + na + 1].at[:, tile], refs[nf + na + 1 + nf + no].at[0])
            pl.when(i == 0)(store.start)
        o_ref[...] = _dot(h_ref[...], w_ref[...])
        if fill is not None:
            pl.when(i == 0)(store.wait)

        if side is not None:
            @pl.when((i == nm - 1) & (s == ns - 1))
            def _():
                side.last(s_ins, s_outs, s_sem_refs)

    fills = [] if fill is None else [fill]
    out = pl.pallas_call(
        body, name=name,
        grid_spec=pltpu.PrefetchScalarGridSpec(
            num_scalar_prefetch=1, grid=(nm, ns),
            in_specs=[pl.BlockSpec((tm, D), lambda i, s, b: (i, 0)), pl.BlockSpec((D, tn), functools.partial(col, nb)),
                      HBM] + [HBM] * nf + s_in_specs,
            out_specs=[pl.BlockSpec((tm, tn), lambda i, s, b: (i, col(0, i, s, b)[1]))] + [HBM] * nf + s_out_specs,
            scratch_shapes=[pltpu.SemaphoreType.DMA((1,))] * nf + s_sems),
        out_shape=[jax.ShapeDtypeStruct((T, NIN), F32)] + [jax.ShapeDtypeStruct(f.shape, f.dtype) for f in fills]
                  + s_shapes,
        input_output_aliases={3: 0, **({4: 1} if fill is not None else {})},
        compiler_params=_params(("arbitrary", "arbitrary")),
    )(idx, h, w, z, *fills, *s_arrays)
    return (out[0], *out[1:1 + nf], out[1 + nf:])


def _lower_bound(lbl):
    mx = jnp.max(lbl, axis=0, keepdims=True)
    e = jnp.exp(lbl - mx)
    return e[0:1] / jnp.sum(e, axis=0, keepdims=True)


def _cumsum_rows(g, rows):
    b = g
    sh = 1
    while sh < CH:
        b = b + jnp.where(rows >= sh, pltpu.roll(b, sh, axis=0), 0.0)
        sh *= 2
    return b


def _rev_cumsum_rows(g, rows):
    b = g
    sh = 1
    while sh < CH:
        b = b + jnp.where(rows < CH - sh, pltpu.roll(b, CH - sh, axis=0), 0.0)
        sh *= 2
    return b


SUB = CH // 2


def _direct_block(qb, kb, vb, bb, rows8):
    ob = jnp.zeros_like(qb)
    for s in range(SUB):
        e_s = jnp.exp(jnp.where(rows8 >= s, bb - bb[s:s + 1], -jnp.inf))
        ob = ob + jnp.sum(qb * e_s * kb[s:s + 1], axis=1, keepdims=True) * vb[s:s + 1]
    return ob


def _direct_block_bwd(qb, kb, vb, bb, dob, rows8, rowc8):
    dq = dk = dv = db = jnp.zeros_like(qb)
    for s in range(SUB):
        one = (rowc8 == s).astype(F32)
        ks, vs = kb[s:s + 1], vb[s:s + 1]
        e_s = jnp.exp(jnp.where(rows8 >= s, bb - bb[s:s + 1], -jnp.inf))
        qes = qb * e_s
        w = qes * ks
        a = jnp.sum(w, axis=1, keepdims=True)
        da = jnp.sum(dob * vs, axis=1, keepdims=True)
        dv = dv + one * jnp.sum(a * dob, axis=0, keepdims=True)
        dq = dq + da * e_s * ks
        dk = dk + one * jnp.sum(da * qes, axis=0, keepdims=True)
        u = da * w
        db = db + u - one * jnp.sum(u, axis=0, keepdims=True)
    return dq, dk, dv, db


def _cross_factors(q, k, b):
    ref = b[SUB - 1:SUB]
    e_hi, e_lo = jnp.exp(b[SUB:] - ref), jnp.exp(ref - b[:SUB])
    return q[SUB:] * e_hi, k[:SUB] * e_lo, e_hi, e_lo


def _intra_fwd(q, k, v, b, rows8):
    lo = _direct_block(q[:SUB], k[:SUB], v[:SUB], b[:SUB], rows8)
    hi = _direct_block(q[SUB:], k[SUB:], v[SUB:], b[SUB:], rows8)
    qe_hi, ke_lo, _, _ = _cross_factors(q, k, b)
    for s in range(SUB):
        hi = hi + jnp.sum(qe_hi * ke_lo[s:s + 1], axis=1, keepdims=True) * v[s:s + 1]
    return jnp.concatenate([lo, hi], axis=0)


def _intra_bwd(q, k, v, b, do, rows8, rowc8):
    dq_lo, dk_lo, dv_lo, db_lo = _direct_block_bwd(q[:SUB], k[:SUB], v[:SUB], b[:SUB], do[:SUB], rows8, rowc8)
    dq_hi, dk_hi, dv_hi, db_hi = _direct_block_bwd(q[SUB:], k[SUB:], v[SUB:], b[SUB:], do[SUB:], rows8, rowc8)
    qe_hi, ke_lo, e_hi, e_lo = _cross_factors(q, k, b)
    do_hi, v_lo = do[SUB:], v[:SUB]
    dqe = dke = jnp.zeros_like(qe_hi)
    for s in range(SUB):
        one = (rowc8 == s).astype(F32)
        a = jnp.sum(qe_hi * ke_lo[s:s + 1], axis=1, keepdims=True)
        da = jnp.sum(do_hi * v_lo[s:s + 1], axis=1, keepdims=True)
        dv_lo = dv_lo + one * jnp.sum(a * do_hi, axis=0, keepdims=True)
        dqe = dqe + da * ke_lo[s:s + 1]
        dke = dke + one * jnp.sum(da * qe_hi, axis=0, keepdims=True)
    u_hi, u_lo = dqe * qe_hi, dke * ke_lo
    d_ref = jnp.sum(u_lo, axis=0, keepdims=True) - jnp.sum(u_hi, axis=0, keepdims=True)
    db_lo = db_lo - u_lo + (rowc8 == SUB - 1).astype(F32) * d_ref
    cat = lambda lo, hi: jnp.concatenate([lo, hi], axis=0)
    return (cat(dq_lo, dq_hi + dqe * e_hi), cat(dk_lo + dke * e_lo, dk_hi), cat(dv_lo, dv_hi),
            cat(db_lo, db_hi + u_hi))


def _hgrn_fwd(z, lbl, nw, side=None):
    s_arrays, s_in_specs, s_shapes, s_out_specs, s_sems = _side_io(side)
    na, no = len(s_arrays), len(s_shapes)
    nsteps = NCH // HSTEP

    def body(hq_ref, hf_ref, hi_ref, hg_ref, lbl_ref, nw_ref, *refs):
        s_ins, (oraw_ref, og_ref, sh_ref) = refs[:na], refs[na:na + 3]
        s_outs, st_ref, s_sem_refs = refs[na + 3:na + 3 + no], refs[na + 3 + no], refs[na + 4 + no:]

        @pl.when(pl.program_id(0) == 0)
        def _():
            st_ref[...] = jnp.zeros_like(st_ref)
            if side is not None:
                side.first(s_ins, s_outs, s_sem_refs)

        lb_all = _lower_bound(lbl_ref[...])
        rows = lax.broadcasted_iota(jnp.int32, (CH, HK), 0)
        rows8 = lax.broadcasted_iota(jnp.int32, (SUB, HK), 0)
        nwv = nw_ref[...]
        for cc, h in [(cc, h) for cc in range(HSTEP) for h in range(HEADS)]:
            rs = slice(CH * cc, CH * (cc + 1))
            sl = slice(HK * h, HK * (h + 1))
            lb = lb_all[:, sl]
            hq, hf, v, hg = hq_ref[rs, sl], hf_ref[rs, sl], hi_ref[rs, sl], hg_ref[rs, sl]
            q = hq * _sigmoid(hq)
            f = lb + (1.0 - lb) * _sigmoid(hf)
            k = 1.0 - f
            b = _cumsum_rows(jnp.log(f), rows)
            sh_ref[cc, h] = st_ref[h]
            o = _bdot(q * jnp.exp(b), st_ref[h], _NT) + _intra_fwd(q, k, v, b, rows8)
            bl = b[CH - 1:CH]
            st_ref[h] = st_ref[h] * jnp.exp(bl)
            st_ref[h] += _bdot(v, k * jnp.exp(bl - b), _TN)
            oraw_ref[rs, sl] = o
            nrm = o * lax.rsqrt(jnp.mean(o * o, axis=1, keepdims=True) + EPS)
            og_ref[rs, sl] = (nrm * nwv * (hg * _sigmoid(hg))).astype(BF)

        if side is not None:
            @pl.when(pl.program_id(0) == nsteps // 2)
            def _():
                side.mid(s_ins, s_outs, s_sem_refs)

            @pl.when(pl.program_id(0) == nsteps - 1)
            def _():
                side.last(s_ins, s_outs, s_sem_refs)

    zblk = lambda c: pl.BlockSpec((CH * HSTEP, D), lambda i, c=c: (i, c))
    out = pl.pallas_call(
        body, name="hgrn_fwd", grid=(nsteps,),
        in_specs=[zblk(0), zblk(1), zblk(2), zblk(3),
                  pl.BlockSpec((2, D), lambda i: (0, 0)), pl.BlockSpec((1, HK), lambda i: (0, 0))] + s_in_specs,
        out_specs=[zblk(0), zblk(0),
                   pl.BlockSpec((HSTEP, HEADS, HK, HK), lambda i: (i, 0, 0, 0))] + s_out_specs,
        out_shape=[jax.ShapeDtypeStruct((T, D), F32), jax.ShapeDtypeStruct((T, D), BF),
                   jax.ShapeDtypeStruct((NCH, HEADS, HK, HK), F32)] + s_shapes,
        scratch_shapes=[pltpu.VMEM((HEADS, HK, HK), F32)] + s_sems,
        compiler_params=_params(("arbitrary",)),
    )(z, z, z, z, lbl, nw, *s_arrays)
    return out[0], out[1], out[2], out[3:]


def _hgrn_bwd(z, lbl, nw, oraw, dog, shist, side=None):
    hstep = 1
    s_arrays, s_in_specs, s_shapes, s_out_specs, s_sems = _side_io(side)
    na, no = len(s_arrays), len(s_shapes)

    def body(*refs):
        hq_ref, hf_ref, hi_ref, hg_ref, lbl_ref, nw_ref, oraw_ref, dog_ref, sh_ref = refs[:9]
        s_ins = refs[9:9 + na]
        dz_ref, dlb_ref, dnw_ref = refs[9 + na:12 + na]
        s_outs = refs[12 + na:12 + na + no]
        dst_ref = refs[12 + na + no]
        s_sem_refs = refs[13 + na + no:]

        @pl.when(pl.program_id(0) == 0)
        def _():
            dst_ref[...] = jnp.zeros_like(dst_ref)
            dlb_ref[...] = jnp.zeros_like(dlb_ref)
            dnw_ref[...] = jnp.zeros_like(dnw_ref)
            if side is not None:
                side.first(s_ins, s_outs, s_sem_refs)

        lb_all = _lower_bound(lbl_ref[...])
        rows = lax.broadcasted_iota(jnp.int32, (CH, HK), 0)
        rowc = lax.broadcasted_iota(jnp.int32, (CH, 1), 0)
        rows8 = lax.broadcasted_iota(jnp.int32, (SUB, HK), 0)
        rowc8 = lax.broadcasted_iota(jnp.int32, (SUB, 1), 0)
        nwv = nw_ref[...]
        dnw = jnp.zeros((1, HK), F32)
        for cc, h in [(cc, h) for cc in reversed(range(hstep)) for h in range(HEADS)]:
            rs = slice(CH * cc, CH * (cc + 1))
            sl = slice(HK * h, HK * (h + 1))
            lb = lb_all[:, sl]
            hq, hf, v, hg = hq_ref[rs, sl], hf_ref[rs, sl], hi_ref[rs, sl], hg_ref[rs, sl]
            o, dg_out = oraw_ref[rs, sl], dog_ref[rs, sl]
            sg = _sigmoid(hg)
            sil = hg * sg
            r = lax.rsqrt(jnp.mean(o * o, axis=1, keepdims=True) + EPS)
            nrm = o * r
            d_hg = dg_out * (nrm * nwv) * (sg * (1.0 + hg * (1.0 - sg)))
            dn = dg_out * nwv * sil
            dnw = dnw + jnp.sum(dg_out * nrm * sil, axis=0, keepdims=True)
            do = r * (dn - nrm * jnp.mean(dn * nrm, axis=1, keepdims=True))
            sq = _sigmoid(hq)
            q = hq * sq
            sig = _sigmoid(hf)
            f = lb + (1.0 - lb) * sig
            k = 1.0 - f
            b = _cumsum_rows(jnp.log(f), rows)
            eb = jnp.exp(b)
            qe = q * eb
            bl = b[CH - 1:CH]
            ebl = jnp.exp(bl)
            kdec = jnp.exp(bl - b)
            ke = k * kdec
            dqe = _bdot(do, sh_ref[cc, h])
            dq = dqe * eb
            db = dqe * qe
            dke = _bdot(v, dst_ref[h])
            dv = _bdot(ke, dst_ref[h], _NT)
            dk = dke * kdec
            rr = dke * ke
            db = db - rr
            db_last = (jnp.sum(rr, axis=0, keepdims=True)
                       + ebl * jnp.sum(dst_ref[h] * sh_ref[cc, h], axis=0, keepdims=True))
            dst_ref[h] = dst_ref[h] * ebl
            dst_ref[h] += _bdot(do, qe, _TN)
            dq_i, dk_i, dv_i, db_i = _intra_bwd(q, k, v, b, do, rows8, rowc8)
            dq, dk, dv = dq + dq_i, dk + dk_i, dv + dv_i
            db = db + db_i + (rowc == CH - 1).astype(F32) * db_last
            dgl = _rev_cumsum_rows(db, rows)
            df = dgl / f - dk
            dlb_ref[:, sl] += jnp.sum(df * (1.0 - sig), axis=0, keepdims=True)
            dz_ref[rs, sl] = (dq * (sq * (1.0 + hq * (1.0 - sq)))).astype(BF)
            dz_ref[rs, D + HK * h:D + HK * (h + 1)] = (df * (1.0 - lb) * sig * (1.0 - sig)).astype(BF)
            dz_ref[rs, 2 * D + HK * h:2 * D + HK * (h + 1)] = dv.astype(BF)
            dz_ref[rs, 3 * D + HK * h:3 * D + HK * (h + 1)] = d_hg.astype(BF)
        dnw_ref[...] += dnw
        if side is not None:
            @pl.when(pl.program_id(0) == NCH // hstep - 1)
            def _():
                side.last(s_ins, s_outs, s_sem_refs)

    rev = lambda i: NCH // hstep - 1 - i
    zblk = lambda c: pl.BlockSpec((CH * hstep, D), lambda i, c=c: (rev(i), c))
    out = pl.pallas_call(
        body, name="hgrn_bwd", grid=(NCH // hstep,),
        in_specs=[zblk(0), zblk(1), zblk(2), zblk(3),
                  pl.BlockSpec((2, D), lambda i: (0, 0)), pl.BlockSpec((1, HK), lambda i: (0, 0)),
                  zblk(0), zblk(0),
                  pl.BlockSpec((hstep, HEADS, HK, HK), lambda i: (rev(i), 0, 0, 0))] + s_in_specs,
        out_specs=[pl.BlockSpec((CH * hstep, 4 * D), lambda i: (rev(i), 0)),
                   pl.BlockSpec((1, D), lambda i: (0, 0)), pl.BlockSpec((1, HK), lambda i: (0, 0))] + s_out_specs,
        out_shape=[jax.ShapeDtypeStruct((T, 4 * D), BF), jax.ShapeDtypeStruct((1, D), F32),
                   jax.ShapeDtypeStruct((1, HK), F32)] + s_shapes,
        scratch_shapes=[pltpu.VMEM((HEADS, HK, HK), F32)] + s_sems,
        compiler_params=_params(("arbitrary",)),
    )(z, z, z, z, lbl, nw, oraw, dog, shist, *s_arrays)
    return out[0], out[1], out[2], out[3:]


BLK = 128
NBLK = T // BLK
QK_SCALE = 0.125


def _head_masks():
    lane = lax.broadcasted_iota(jnp.int32, (1, BLK), 1)
    return [(lane < 64).astype(F32), (lane >= 64).astype(F32)]


def _pieces(dil):
    m = T // dil
    out = []
    for r in range(dil):
        for j in range(m // BLK):
            start = r + dil * BLK * j
            rows = pl.ds(start, BLK, stride=dil) if dil > 1 else pl.ds(start, BLK)
            out.append((rows, r * m + BLK * j))
    return out


def _rope(x, c, sa, sb):
    return x * c + pltpu.roll(x, 96, axis=1) * sa + pltpu.roll(x, 32, axis=1) * sb


def _rope_t(d, c, sa, sb):
    return d * c + pltpu.roll(d * sa, 32, axis=1) + pltpu.roll(d * sb, 96, axis=1)


def _rope_and_regroup(dil, q_ref, k_ref, v_ref, tables, stage_q, stage_k, qr_ref, kr_ref, vr_ref):
    cos_ref, sa_ref, sb_ref = tables
    to_q, to_k = (qr_ref, kr_ref) if dil == 1 else (stage_q, stage_k)
    for c in range(T // BLK):
        rows = pl.ds(BLK * c, BLK)
        cs, sa, sb = cos_ref[rows, :], sa_ref[rows, :], sb_ref[rows, :]
        to_q[rows, :] = (_rope(q_ref[rows, :], cs, sa, sb) * QK_SCALE).astype(to_q.dtype)
        to_k[rows, :] = _rope(k_ref[rows, :], cs, sa, sb).astype(to_k.dtype)
    for rows, dst in _pieces(dil):
        drows = pl.ds(dst, BLK)
        if dil > 1:
            qr_ref[drows, :] = stage_q[rows, :].astype(qr_ref.dtype)
            kr_ref[drows, :] = stage_k[rows, :].astype(kr_ref.dtype)
        vr_ref[drows, :] = v_ref[rows, :].astype(vr_ref.dtype)


def _window_bias(bias_ref):
    ii = lax.broadcasted_iota(jnp.int32, (2 * BLK, BLK), 0) % BLK
    jj = lax.broadcasted_iota(jnp.int32, (2 * BLK, BLK), 1)
    bias_ref[0] = jnp.where(jj <= ii, 0.0, -jnp.inf)
    bias_ref[1] = jnp.where(jj >= ii, 0.0, -jnp.inf)


def _blocks(bi):
    if isinstance(bi, int):
        return pl.ds(bi * BLK, BLK), pl.ds(max(bi - 1, 0) * BLK, BLK)
    return (pl.ds(pl.multiple_of(bi * BLK, BLK), BLK),
            pl.ds(pl.multiple_of(jnp.maximum(bi - 1, 0) * BLK, BLK), BLK))


def _stack_heads(x, masks):
    return jnp.concatenate([x * masks[0].astype(x.dtype), x * masks[1].astype(x.dtype)], axis=0).astype(BF)


def _attn_fwd(z, cos, sa, sb):
    def body(q_ref, k_ref, v_ref, ag_ref, cos_ref, sa_ref, sb_ref, ob_ref, opre_ref, lse_ref, qr_ref, kr_ref, vr_ref,
             bias_ref, og_ref, lg_ref, otok_ref, ltok_ref, sc_ref):
        g = pl.program_id(1)
        masks = _head_masks()

        @pl.when(g == 0)
        def _():
            _window_bias(bias_ref)

        def group(gi):
            dil = ATT_GROUPS[gi][1]
            nblk = (T // dil) // BLK
            _rope_and_regroup(dil, q_ref, k_ref, v_ref, (cos_ref, sa_ref, sb_ref), lg_ref.at[0], lg_ref.at[1],
                              qr_ref, kr_ref, vr_ref)

            def scores(bi, slot):
                cur, prev = _blocks(bi)
                q2 = _stack_heads(qr_ref[cur, :], masks)
                sc_ref[slot, 0] = _dot(q2, kr_ref[cur, :], _NT) + bias_ref[0]
                if nblk > 1:
                    sc_ref[slot, 1] = (_dot(q2, kr_ref[prev, :], _NT)
                                       + (bias_ref[1] + jnp.where((bi % nblk) != 0, 0.0, -jnp.inf)))

            def finish(bi, slot):
                cur, prev = _blocks(bi)
                s_c, vc = sc_ref[slot, 0], vr_ref[cur, :]
                if nblk > 1:
                    s_p, vp = sc_ref[slot, 1], vr_ref[prev, :]
                    mx = jnp.max(jnp.maximum(s_c, s_p), axis=1, keepdims=True)
                    p_c, p_p = jnp.exp(s_c - mx), jnp.exp(s_p - mx)
                    den = jnp.sum(p_c + p_p, axis=1, keepdims=True)
                    oh = _dot(p_c.astype(BF), vc) + _dot(p_p.astype(BF), vp)
                else:
                    mx = jnp.max(s_c, axis=1, keepdims=True)
                    p_c = jnp.exp(s_c - mx)
                    den = jnp.sum(p_c, axis=1, keepdims=True)
                    oh = _dot(p_c.astype(BF), vc)
                on = oh / den
                lsev = jnp.broadcast_to(mx + jnp.log(den), (2 * BLK, BLK))
                og_ref[cur, :] = on[:BLK] * masks[0] + on[BLK:] * masks[1]
                lg_ref[0, cur, :] = lsev[:BLK]
                lg_ref[1, cur, :] = lsev[BLK:]

            def pair(j, carry):
                finish(2 * j, 0)
                scores(2 * j + 1, 1)
                finish(2 * j + 1, 1)
                scores(jnp.minimum(2 * j + 2, NBLK - 1), 0)
                return carry

            scores(0, 0)
            lax.fori_loop(0, NBLK // 2, pair, 0)
            for rows, src in _pieces(dil):
                srows = pl.ds(src, BLK)
                otok_ref[gi, rows, :] = og_ref[srows, :]
                ltok_ref[gi, 0, rows, :] = lg_ref[0, srows, :]
                ltok_ref[gi, 1, rows, :] = lg_ref[1, srows, :]

        for gi in range(3):
            pl.when(g == gi)(functools.partial(group, gi))

        @pl.when(g == 2)
        def _():
            for c in range(T // BLK):
                rows = pl.ds(BLK * c, BLK)
                wts = []
                for hh in range(2):
                    l0, l1, l2 = ltok_ref[0, hh, rows, :], ltok_ref[1, hh, rows, :], ltok_ref[2, hh, rows, :]
                    mx = jnp.maximum(jnp.maximum(l0, l1), l2)
                    e0, e1, e2 = jnp.exp(l0 - mx), jnp.exp(l1 - mx), jnp.exp(l2 - mx)
                    tot = e0 + e1 + e2
                    lse_ref[rows, BLK * hh:BLK * (hh + 1)] = mx + jnp.log(tot)
                    inv = 1.0 / tot
                    wts.append([e0 * inv, e1 * inv, e2 * inv])
                o = sum((wts[0][gi] * masks[0] + wts[1][gi] * masks[1]) * otok_ref[gi, rows, :] for gi in range(3))
                ag = ag_ref[rows, :]
                opre_ref[rows, :] = o
                ob_ref[rows, :] = (o * (ag * _sigmoid(ag))).astype(BF)

    c0 = ATT_COL0 // BLK
    zspec = lambda part: pl.BlockSpec((T, BLK), lambda p, g, part=part: (0, c0 + 12 * part + 4 * g + p))
    outspec = pl.BlockSpec((T, BLK), lambda p, g: (0, p))
    table = pl.BlockSpec((T, BLK), lambda p, g: (0, 0))
    regrouped = pl.BlockSpec((None, T, BLK), lambda p, g: (g, 0, p))
    big = lambda: pltpu.VMEM((T, BLK), F32)
    return pl.pallas_call(
        body, name="attn_fwd", grid=(4, 3),
        in_specs=[zspec(0), zspec(1), zspec(2),
                  pl.BlockSpec((T, BLK), lambda p, g: (0, AG_COL0 // BLK + p)), table, table, table],
        out_specs=[outspec, outspec, pl.BlockSpec((T, 2 * BLK), lambda p, g: (0, p)), regrouped, regrouped, regrouped],
        out_shape=[jax.ShapeDtypeStruct((T, 512), BF), jax.ShapeDtypeStruct((T, 512), F32),
                   jax.ShapeDtypeStruct((T, 8 * BLK), F32)] + [jax.ShapeDtypeStruct((3, T, 512), BF)] * 3,
        scratch_shapes=[pltpu.VMEM((2, 2 * BLK, BLK), F32), big(),
                        pltpu.VMEM((2, T, BLK), F32), pltpu.VMEM((3, T, BLK), F32), pltpu.VMEM((3, 2, T, BLK), F32),
                        pltpu.VMEM((2, 2, 2 * BLK, BLK), F32)],
        compiler_params=_params(("parallel", "arbitrary")),
    )(z, z, z, z, cos, sa, sb)


def _attn_bwd(z, qs, ks, vs, cos, sa, sb, opre, lse, dob):
    def body(qs_ref, ks_ref, vs_ref, ag_ref, cos_ref, sa_ref, sb_ref, o_ref, lse0_ref, lse1_ref, dob_ref,
             dq_ref, dk_ref, dv_ref, dag_ref,
             bias_ref, dtok_ref, qr_ref, kr_ref, vr_ref, dor_ref, lr_ref, dr_ref,
             dqr_ref, dkr_ref, dvr_ref, pd_ref, dotok_ref):
        g = pl.program_id(1)
        masks = _head_masks()

        @pl.when(g == 0)
        def _():
            _window_bias(bias_ref)
            for c in range(T // BLK):
                rows = pl.ds(BLK * c, BLK)
                ag, dob_v, o = ag_ref[rows, :], dob_ref[rows, :], o_ref[rows, :]
                sg = _sigmoid(ag)
                dag_ref[rows, :] = (dob_v * o * (sg * (1.0 + ag * (1.0 - sg)))).astype(BF)
                do = dob_v * (ag * sg)
                dotok_ref[rows, :] = do
                prod = do * o
                for hh, mh in enumerate(masks):
                    dtok_ref[hh, rows, :] = jnp.broadcast_to(jnp.sum(prod * mh, axis=1, keepdims=True), (BLK, BLK))

        def group(gi):
            dil = ATT_GROUPS[gi][1]
            nblk = (T // dil) // BLK
            for rows, dst in _pieces(dil):
                drows = pl.ds(dst, BLK)
                dor_ref[drows, :] = dotok_ref[rows, :]
                for hh, lse_ref in enumerate((lse0_ref, lse1_ref)):
                    lr_ref[hh, drows, :] = lse_ref[rows, :]
                    dr_ref[hh, drows, :] = dtok_ref[hh, rows, :]
            dkr_ref[...] = jnp.zeros_like(dkr_ref)
            dvr_ref[...] = jnp.zeros_like(dvr_ref)

            def probs(bi, slot):
                cur, prev = _blocks(bi)
                q2, do2 = _stack_heads(qs_ref[cur, :], masks), _stack_heads(dor_ref[cur, :], masks)
                lh = jnp.concatenate([lr_ref[0, cur, :], lr_ref[1, cur, :]], axis=0)
                dh = jnp.concatenate([dr_ref[0, cur, :], dr_ref[1, cur, :]], axis=0)
                p_c = jnp.exp(_dot(q2, ks_ref[cur, :], _NT) + bias_ref[0] - lh)
                pd_ref[slot, 0] = p_c.astype(BF)
                pd_ref[slot, 1] = (p_c * (_dot(do2, vs_ref[cur, :], _NT) - dh)).astype(BF)
                if nblk > 1:
                    bias_p = bias_ref[1] + jnp.where((bi % nblk) != 0, 0.0, -jnp.inf)
                    p_p = jnp.exp(_dot(q2, ks_ref[prev, :], _NT) + bias_p - lh)
                    pd_ref[slot, 2] = p_p.astype(BF)
                    pd_ref[slot, 3] = (p_p * (_dot(do2, vs_ref[prev, :], _NT) - dh)).astype(BF)

            def grads(bi, slot):
                cur, prev = _blocks(bi)
                q2, do2 = _stack_heads(qs_ref[cur, :], masks), _stack_heads(dor_ref[cur, :], masks)
                p_c, ds_c = pd_ref[slot, 0], pd_ref[slot, 1]
                dq2 = _dot(ds_c, ks_ref[cur, :])
                dkr_ref[cur, :] += _dot(ds_c, q2, _TN)
                dvr_ref[cur, :] += _dot(p_c, do2, _TN)
                if nblk > 1:
                    p_p, ds_p = pd_ref[slot, 2], pd_ref[slot, 3]
                    dq2 = dq2 + _dot(ds_p, ks_ref[prev, :])
                    dkr_ref[prev, :] += _dot(ds_p, q2, _TN)
                    dvr_ref[prev, :] += _dot(p_p, do2, _TN)
                dqr_ref[cur, :] = dq2[:BLK] * masks[0] + dq2[BLK:] * masks[1]

            def pair(j, carry):
                grads(2 * j, 0)
                probs(2 * j + 1, 1)
                grads(2 * j + 1, 1)
                probs(jnp.minimum(2 * j + 2, NBLK - 1), 0)
                return carry

            probs(0, 0)
            lax.fori_loop(0, NBLK // 2, pair, 0)
            if dil > 1:
                for rows, src in _pieces(dil):
                    srows = pl.ds(src, BLK)
                    qr_ref[rows, :] = dqr_ref[srows, :]
                    kr_ref[rows, :] = dkr_ref[srows, :]
                    vr_ref[rows, :] = dvr_ref[srows, :]
            tq, tk, tv = (qr_ref, kr_ref, vr_ref) if dil > 1 else (dqr_ref, dkr_ref, dvr_ref)
            for c in range(T // BLK):
                rows = pl.ds(BLK * c, BLK)
                cs, sa, sb = cos_ref[rows, :], sa_ref[rows, :], sb_ref[rows, :]
                dq_ref[rows, :] = _rope_t(tq[rows, :] * QK_SCALE, cs, sa, sb).astype(BF)
                dk_ref[rows, :] = _rope_t(tk[rows, :], cs, sa, sb).astype(BF)
                dv_ref[rows, :] = tv[rows, :].astype(BF)

        for gi in range(3):
            pl.when(g == gi)(functools.partial(group, gi))

    regrouped = pl.BlockSpec((None, T, BLK), lambda p, g: (g, 0, p))
    pspec = pl.BlockSpec((T, BLK), lambda p, g: (0, p))
    gspec = pl.BlockSpec((T, BLK), lambda p, g: (0, 4 * g + p))
    table = pl.BlockSpec((T, BLK), lambda p, g: (0, 0))
    big = lambda: pltpu.VMEM((T, BLK), F32)
    two = lambda: pltpu.VMEM((2, T, BLK), F32)
    return pl.pallas_call(
        body, name="attn_bwd", grid=(4, 3),
        in_specs=[regrouped, regrouped, regrouped,
                  pl.BlockSpec((T, BLK), lambda p, g: (0, AG_COL0 // BLK + p)), table, table, table,
                  pspec, pl.BlockSpec((T, BLK), lambda p, g: (0, 2 * p)),
                  pl.BlockSpec((T, BLK), lambda p, g: (0, 2 * p + 1)), pspec],
        out_specs=[gspec, gspec, gspec, pspec],
        out_shape=[jax.ShapeDtypeStruct((T, 1536), BF), jax.ShapeDtypeStruct((T, 1536), BF),
                   jax.ShapeDtypeStruct((T, 1536), BF), jax.ShapeDtypeStruct((T, 512), BF)],
        scratch_shapes=[pltpu.VMEM((2, 2 * BLK, BLK), F32), two(), big(), big(), big(), big(),
                        two(), two(), big(), big(), big(), pltpu.VMEM((2, 4, 2 * BLK, BLK), BF), big()],
        compiler_params=_params(("parallel", "arbitrary")),
    )(qs, ks, vs, z, cos, sa, sb, opre, lse, lse, dob)


def _merge_out_loss(og, ob, z, w_a, w_b, w_out, x, tgt, wf):
    tm = 256

    def body(og_ref, ob_ref, ga_ref, gb_ref, wa_ref, wb_ref, wo_ref, x_ref, t_ref, wf_ref,
             ya_ref, yb_ref, m_ref, dout_ref, loss_ref, gwf_ref):
        @pl.when(pl.program_id(0) == 0)
        def _():
            loss_ref[...] = jnp.zeros_like(loss_ref)
            gwf_ref[...] = jnp.zeros_like(gwf_ref)

        ya, yb = _dot(og_ref[...], wa_ref[...]), _dot(ob_ref[...], wb_ref[...])
        ya_ref[...] = ya
        yb_ref[...] = yb
        m = (_sigmoid(ga_ref[...]) * ya + _sigmoid(gb_ref[...]) * yb).astype(BF)
        m_ref[...] = m
        out = x_ref[...] + _dot(m, wo_ref[...])
        r = lax.rsqrt(jnp.mean(out * out, axis=-1, keepdims=True) + EPS)
        yh = out * r
        wfv = wf_ref[...]
        err = yh * wfv - t_ref[...]
        loss_ref[...] += jnp.sum(err * err, axis=0, keepdims=True) * (0.5 / D)
        dy = err * (1.0 / D)
        gwf_ref[...] += jnp.sum(dy * yh, axis=0, keepdims=True)
        dyh = dy * wfv
        dout_ref[...] = r * (dyh - yh * jnp.mean(dyh * yh, axis=-1, keepdims=True))

    row = pl.BlockSpec((tm, D), lambda i: (i, 0))
    vec = pl.BlockSpec((1, D), lambda i: (0, 0))
    whole = lambda w: pl.BlockSpec(w.shape, lambda i: (0, 0))
    return pl.pallas_call(
        body, name="merge_out_loss", grid=(T // tm,),
        in_specs=[row, pl.BlockSpec((tm, ob.shape[1]), lambda i: (i, 0)),
                  pl.BlockSpec((tm, D), lambda i: (i, GATE_COL0 // D)),
                  pl.BlockSpec((tm, D), lambda i: (i, GATE_COL0 // D + 1)),
                  whole(w_a), whole(w_b), whole(w_out), row, row, vec],
        out_specs=[row, row, row, row, vec, vec],
        out_shape=[jax.ShapeDtypeStruct((T, D), F32), jax.ShapeDtypeStruct((T, D), F32),
                   jax.ShapeDtypeStruct((T, D), BF), jax.ShapeDtypeStruct((T, D), F32),
                   jax.ShapeDtypeStruct((1, D), F32), jax.ShapeDtypeStruct((1, D), F32)],
        compiler_params=_params(("arbitrary",)),
    )(og, ob, z, z, w_a, w_b, w_out, x, tgt, wf)


def _merge_proj_bwd(dout, ya, yb, z, w_a, w_b, w_out):
    tm = 256

    def body(dout_ref, ya_ref, yb_ref, ga_ref, gb_ref, wa_ref, wb_ref, wo_ref,
             dya_ref, dyb_ref, dg_ref, dog_ref, dob_ref):
        dmv = _dot(dout_ref[...].astype(BF), wo_ref[...], _NT)
        sa, sb = _sigmoid(ga_ref[...]), _sigmoid(gb_ref[...])
        dya, dyb = (sa * dmv).astype(BF), (sb * dmv).astype(BF)
        dya_ref[...] = dya
        dyb_ref[...] = dyb
        dg_ref[:, :D] = (dmv * ya_ref[...] * sa * (1.0 - sa)).astype(BF)
        dg_ref[:, D:] = (dmv * yb_ref[...] * sb * (1.0 - sb)).astype(BF)
        dog_ref[...] = _dot(dya, wa_ref[...], _NT)
        dob_ref[...] = _dot(dyb, wb_ref[...], _NT)

    row = pl.BlockSpec((tm, D), lambda i: (i, 0))
    whole = lambda w: pl.BlockSpec(w.shape, lambda i: (0, 0))
    nb = w_b.shape[0]
    return pl.pallas_call(
        body, name="merge_proj_bwd", grid=(T // tm,),
        in_specs=[row, row, row, pl.BlockSpec((tm, D), lambda i: (i, GATE_COL0 // D)),
                  pl.BlockSpec((tm, D), lambda i: (i, GATE_COL0 // D + 1)), whole(w_a), whole(w_b), whole(w_out)],
        out_specs=[row, row, pl.BlockSpec((tm, 2 * D), lambda i: (i, 0)), row,
                   pl.BlockSpec((tm, nb), lambda i: (i, 0))],
        out_shape=[jax.ShapeDtypeStruct((T, D), BF), jax.ShapeDtypeStruct((T, D), BF),
                   jax.ShapeDtypeStruct((T, 2 * D), BF), jax.ShapeDtypeStruct((T, D), F32),
                   jax.ShapeDtypeStruct((T, nb), F32)],
        compiler_params=_params(("parallel",)),
    )(dout, ya, yb, z, z, w_a, w_b, w_out)


def _rope_inv_freq():
    inv = ROPE_THETA ** (-jnp.arange(0, 64, 2, dtype=F32) / 64)
    return jnp.tile(inv, 4).reshape(1, BLK)


def _local_step(x, pos, norm_w, lbl, hnw, wf, tgt, w_in, w_a, w_b, w_out, shard_shapes=()):
    invf = _rope_inv_freq()
    if shard_shapes:
        blk = jnp.reshape(2 * lax.axis_index("x") + lax.axis_index("y"), (1,)).astype(jnp.int32)
        h, cos, sa, sb, z_own, (w_near,) = _norm_and_rope_tables(
            x, norm_w, pos, invf, side=_gather_near_side(w_in, WEIGHT_AXES[0]), own=(w_in, blk))
        near = jnp.concatenate([blk ^ 2, blk ^ 1])
        z, (w_diag,) = _z_blocks(h, w_near, z_own, jnp.concatenate([near, near]), 2, name="z_proj_near",
                                 side=_gather_diag_side(w_near, w_in.shape, WEIGHT_AXES[0]))
        z, w_in, _ = _z_blocks(h, w_diag, z, jnp.concatenate([blk ^ 3, jnp.zeros_like(blk)]), 1, name="z_proj_diag",
                               fill=w_near, side=None)
        oraw, og, shist, (w_a, w_b, w_out) = _hgrn_fwd(
            z, lbl, hnw, side=_gather_side([w_a, w_b, w_out], WEIGHT_AXES[1:]))
    else:
        h, cos, sa, sb, _, _ = _norm_and_rope_tables(x, norm_w, pos, invf)
        z = _matmul(h, w_in, tm=T, tn=512, name="z_proj")
        oraw, og, shist, _ = _hgrn_fwd(z, lbl, hnw)
    ob, opre, lse, qs, ks, vs = _attn_fwd(z, cos, sa, sb)
    ya, yb, merged, dout, loss_vec, g_wf = _merge_out_loss(og, ob, z, w_a, w_b, w_out, x, tgt, wf)

    dya, dyb, dgates, dog, dob = _merge_proj_bwd(dout, ya, yb, z, w_a, w_b, w_out)
    g_wout = _matmul(merged, dout, ta=True, out_dtype=BF, tm=512, tn=1024, name="g_wout")
    g_wa = _matmul(og, dya, ta=True, out_dtype=BF, tm=512, tn=1024, name="g_wa")
    g_wb = _matmul(ob, dyb, ta=True, out_dtype=BF, tm=512, tn=1024, name="g_wb")
    small = [g_wa, g_wb, g_wout]
    side_s = side_w = None
    if shard_shapes:
        p3_s = _rs_partials(small, shard_shapes[1:], WEIGHT_AXES[1:], "small")
        side_s = _chip_exchange_side(p3_s, shard_shapes[1:], WEIGHT_AXES[1:])
    dz_h, dlb, g_hnw, land_s = _hgrn_bwd(z, lbl, hnw, oraw, dog, shist, side=side_s)
    dq, dk, dv, dag = _attn_bwd(z, qs, ks, vs, cos, sa, sb, opre, lse, dob)
    dz_parts = [dz_h, dq, dk, dv, dag, dgates]
    if shard_shapes:
        c = lax.axis_index("c")
        half = lambda i: jnp.reshape(i, (1,)).astype(jnp.int32)
        g_send = _grad_w_in_half(h, dz_parts, half(1 - c))
        g_keep, (g_sib,) = _grad_w_in_half(h, dz_parts, half(c), side=_sibling_send_side(g_send))
        p3_w = [_add_bf16(g_keep, g_sib, "pair_sum_w_in").reshape(1, D // 2, NIN)]
        side_w = _chip_exchange_side(p3_w, shard_shapes[:1], WEIGHT_AXES[:1])
    else:
        g_big = [_grad_w_in(h, dz_parts)] + small
    gx, g_nw, land_w = _grad_x(dz_parts, w_in, x, dout, norm_w, side=side_w)
    if shard_shapes:
        g_big = _rs_finish(p3_w + p3_s, list(land_w) + list(land_s), shard_shapes, WEIGHT_AXES)
    return dict(loss_vec=loss_vec, gx=gx, g_nw=g_nw, dlb=dlb, g_hnw=g_hnw, g_wf=g_wf,
                g_win=g_big[0], g_wa=g_big[1], g_wb=g_big[2], g_wout=g_big[3])


MESH = pl.DeviceIdType.MESH
HBM = pl.BlockSpec(memory_space=pl.ANY)
WEIGHT_AXES = (1, 0, 1, 0)


def _place():
    x, y, c = lax.axis_index("x"), lax.axis_index("y"), lax.axis_index("c")
    chips = [(1 - x, y), (x, 1 - y), (1 - x, 1 - y)]
    return x, y, c, chips


def _block_half(ref, shard_shape, axis, j, half):
    r, c = shard_shape
    hr = r // 2
    if axis == 0:
        return ref.at[pl.ds(pl.multiple_of(j * r + half * hr, 16), hr), :]
    return ref.at[pl.ds(pl.multiple_of(half * hr, 16), hr), pl.ds(pl.multiple_of(j * c, 128), c)]


class _Side:
    def __init__(self, arrays, out_shapes, sems, first, last, mid=None):
        self.arrays, self.out_shapes, self.sems, self.first, self.last = arrays, out_shapes, sems, first, last
        self.mid = mid


def _gather_side(shards, axes):
    n = len(shards)
    shapes = [s.shape for s in shards]

    def copies(ins, outs, sems):
        send1, recv1, send2, recv2, send0, recv0 = sems
        x, y, c, chips = _place()
        me = 2 * x + y
        sib = (x, y, 1 - c)
        near = ((1 - c) * (1 - x) + c * x, (1 - c) * y + c * (1 - y))
        far = ((1 - c) * x + c * (1 - x), (1 - c) * (1 - y) + c * y)
        out = []
        for a in range(n):
            r, cc = shapes[a]
            mine = (outs[a].at[pl.ds(pl.multiple_of(me * r, 16), r), :] if axes[a] == 0
                    else outs[a].at[:, pl.ds(pl.multiple_of(me * cc, 128), cc)])
            own = pltpu.make_async_remote_copy(
                src_ref=ins[a], dst_ref=mine, send_sem=send0.at[a], recv_sem=recv0.at[a],
                device_id=sib, device_id_type=MESH)
            src = ins[a].at[pl.ds(pl.multiple_of(c * (r // 2), 16), r // 2), :]
            sends = [pltpu.make_async_remote_copy(
                src_ref=src, dst_ref=_block_half(outs[a], shapes[a], axes[a], me, c),
                send_sem=send1.at[a, k], recv_sem=recv1.at[a, k], device_id=(*chips[k], c), device_id_type=MESH)
                for k in range(2)]

            def region(chip, half):
                return _block_half(outs[a], shapes[a], axes[a], 2 * chip[0] + chip[1], half)

            def arrival(chip, k):
                reg = region(chip, c)
                return pltpu.make_async_remote_copy(
                    src_ref=reg, dst_ref=reg, send_sem=send1.at[a, k], recv_sem=recv1.at[a, k],
                    device_id=(*chip, c), device_id_type=MESH)

            def to_sibling(chip, k):
                reg = region(chip, c)
                return pltpu.make_async_remote_copy(
                    src_ref=reg, dst_ref=reg, send_sem=send2.at[a, k], recv_sem=recv2.at[a, k],
                    device_id=sib, device_id_type=MESH)

            def from_sibling(chip, k):
                reg = region(chip, 1 - c)
                return pltpu.make_async_remote_copy(
                    src_ref=reg, dst_ref=reg, send_sem=send2.at[a, k], recv_sem=recv2.at[a, k],
                    device_id=sib, device_id_type=MESH)

            relay = pltpu.make_async_remote_copy(
                src_ref=region(near, c), dst_ref=region(near, c), send_sem=send1.at[a, 2], recv_sem=recv1.at[a, 2],
                device_id=(*far, c), device_id_type=MESH)
            hops = [(arrival(near, c), to_sibling(near, c)), (arrival(far, 1 - c), to_sibling(far, 1 - c)),
                    (arrival(chips[2], 2), to_sibling(chips[2], 2))]
            back = [from_sibling(chips[k], k) for k in range(3)]
            out.append((own, sends, relay, hops, back))
        return out

    def first(ins, outs, sems):
        for own, sends, _, _, _ in copies(ins, outs, sems):
            own.start()
            for cp in sends:
                cp.start()

    def mid(ins, outs, sems):
        per_array = copies(ins, outs, sems)
        for step in range(2):
            for _, _, relay, hops, _ in per_array:
                arrived, onward = hops[step]
                arrived.wait_recv()
                if step == 0:
                    relay.start()
                onward.start()

    def last(ins, outs, sems):
        per_array = copies(ins, outs, sems)
        for _, _, _, hops, _ in per_array:
            arrived, onward = hops[2]
            arrived.wait_recv()
            onward.start()
        for own, sends, relay, hops, back in per_array:
            for cp in back:
                cp.wait_recv()
            for cp in sends + [relay] + [onward for _, onward in hops]:
                cp.wait_send()
            own.wait()

    full = [(4 * r, c) if ax == 0 else (r, 4 * c) for (r, c), ax in zip(shapes, axes)]
    sems = [pltpu.SemaphoreType.DMA((n, 3)), pltpu.SemaphoreType.DMA((n, 3)),
            pltpu.SemaphoreType.DMA((n, 3)), pltpu.SemaphoreType.DMA((n, 3)),
            pltpu.SemaphoreType.DMA((n,)), pltpu.SemaphoreType.DMA((n,))]
    return _Side(list(shards), [jax.ShapeDtypeStruct(f, BF) for f in full], sems, first, last, mid)


def _gather_near_side(shard, axis):
    shape = shard.shape
    r, cc = shape

    def copies(ins, outs, sems):
        send1, recv1, send2, recv2, send0, recv0 = sems
        x, y, c, chips = _place()
        me = 2 * x + y
        sib = (x, y, 1 - c)
        mine = (outs[0].at[pl.ds(pl.multiple_of(me * r, 16), r), :] if axis == 0
                else outs[0].at[:, pl.ds(pl.multiple_of(me * cc, 128), cc)])
        own = pltpu.make_async_remote_copy(
            src_ref=ins[0], dst_ref=mine, send_sem=send0.at[0], recv_sem=recv0.at[0],
            device_id=sib, device_id_type=MESH)
        src = ins[0].at[pl.ds(pl.multiple_of(c * (r // 2), 16), r // 2), :]

        def region(k, half):
            return _block_half(outs[0], shape, axis, 2 * chips[k][0] + chips[k][1], half)

        def moves(k):
            return [pltpu.make_async_remote_copy(
                        src_ref=s, dst_ref=d, send_sem=ss.at[k], recv_sem=rs.at[k], device_id=dev,
                        device_id_type=MESH)
                    for s, d, ss, rs, dev in (
                        (src, _block_half(outs[0], shape, axis, me, c), send1, recv1, (*chips[k], c)),
                        (region(k, c), region(k, c), send1, recv1, (*chips[k], c)),
                        (region(k, c), region(k, c), send2, recv2, sib),
                        (region(k, 1 - c), region(k, 1 - c), send2, recv2, sib))]

        return own, [moves(k) for k in range(2)]

    def first(ins, outs, sems):
        own, per_chip = copies(ins, outs, sems)
        own.start()
        for send, _, _, _ in per_chip:
            send.start()

    def last(ins, outs, sems):
        own, per_chip = copies(ins, outs, sems)
        for _, arrived, onward, _ in per_chip:
            arrived.wait_recv()
            onward.start()
        for send, _, onward, back in per_chip:
            back.wait_recv()
            send.wait_send()
            onward.wait_send()
        own.wait()

    full = (4 * r, cc) if axis == 0 else (r, 4 * cc)
    sems = [pltpu.SemaphoreType.DMA((2,))] * 4 + [pltpu.SemaphoreType.DMA((1,))] * 2
    return _Side([shard], [jax.ShapeDtypeStruct(full, BF)], sems, first, last)


def _gather_diag_side(gathered, shape, axis):
    r, cc = shape

    def copies(ins, outs, sems):
        send1, recv1, send2, recv2 = sems
        x, y, c, _ = _place()
        sib = (x, y, 1 - c)
        near = ((1 - c) * (1 - x) + c * x, (1 - c) * y + c * (1 - y))
        far = ((1 - c) * x + c * (1 - x), (1 - c) * (1 - y) + c * y)

        def half(i):
            return outs[0].at[pl.ds(pl.multiple_of(i * (r // 2), 16), r // 2), :]

        def move(s, d, ss, rs, dev):
            return pltpu.make_async_remote_copy(
                src_ref=s, dst_ref=d, send_sem=ss.at[0], recv_sem=rs.at[0], device_id=dev, device_id_type=MESH)

        relay = move(_block_half(ins[0], shape, axis, 2 * near[0] + near[1], c), half(c), send1, recv1, (*far, c))
        arrived = move(half(c), half(c), send1, recv1, (*far, c))
        onward = move(half(c), half(c), send2, recv2, sib)
        back = move(half(1 - c), half(1 - c), send2, recv2, sib)
        return relay, arrived, onward, back

    def first(ins, outs, sems):
        copies(ins, outs, sems)[0].start()

    def last(ins, outs, sems):
        relay, arrived, onward, back = copies(ins, outs, sems)
        arrived.wait_recv()
        onward.start()
        back.wait_recv()
        relay.wait_send()
        onward.wait_send()

    return _Side([gathered], [jax.ShapeDtypeStruct(shape, BF)], [pltpu.SemaphoreType.DMA((1,))] * 4, first, last)


def _as3d(g, shard_shape, axis):
    r, c = shard_shape
    return g.reshape(4, r, c) if axis == 0 else g.reshape(1, r, 4 * c)


def _half_rows(ref3, hr, half):
    return ref3.at[:, pl.ds(pl.multiple_of(half * hr, 16), hr), :]


def _rs_pair_exchange(g3s, name):
    n = len(g3s)

    def body(*refs):
        ins, outs = refs[:n], refs[n:2 * n]
        send, recv = refs[2 * n:]
        x, y, c, _ = _place()
        cps = []
        for a in range(n):
            hr = g3s[a].shape[1] // 2
            cp = pltpu.make_async_remote_copy(
                src_ref=_half_rows(ins[a], hr, 1 - c), dst_ref=outs[a],
                send_sem=send.at[a], recv_sem=recv.at[a], device_id=(x, y, 1 - c), device_id_type=MESH)
            cp.start()
            cps.append(cp)
        for cp in cps:
            cp.wait()

    return pl.pallas_call(
        body, name=name,
        in_specs=[HBM] * n, out_specs=[HBM] * n,
        out_shape=[jax.ShapeDtypeStruct((g.shape[0], g.shape[1] // 2, g.shape[2]), BF) for g in g3s],
        scratch_shapes=[pltpu.SemaphoreType.DMA((n,)), pltpu.SemaphoreType.DMA((n,))],
    )(*g3s)


def _pair_sum(g3, land, cidx, name):
    nb, r, w = g3.shape
    hr = r // 2
    tr = 64

    def body(c_ref, g_ref, l_ref, o_ref):
        o_ref[...] = (g_ref[...].astype(F32) + l_ref[...].astype(F32)).astype(BF)

    blk = (nb, tr, w)
    return pl.pallas_call(
        body, name=name,
        grid_spec=pltpu.PrefetchScalarGridSpec(
            num_scalar_prefetch=1, grid=(hr // tr,),
            in_specs=[pl.BlockSpec(blk, lambda i, c: (0, c[0] * (hr // tr) + i, 0)),
                      pl.BlockSpec(blk, lambda i, c: (0, i, 0))],
            out_specs=pl.BlockSpec(blk, lambda i, c: (0, i, 0))),
        out_shape=jax.ShapeDtypeStruct((nb, hr, w), BF),
        compiler_params=_params(("parallel",)),
    )(cidx, g3, land)


def _chip_exchange_side(p3s, shapes, axes):
    n = len(p3s)

    def copies(ins, outs, sems):
        send, recv = sems
        x, y, c, chips = _place()
        cps = []
        for a in range(n):
            r, cc = shapes[a]
            for k, (px, py) in enumerate(chips):
                j = 2 * px + py
                src = ins[a].at[j] if axes[a] == 0 else ins[a].at[0, :, pl.ds(pl.multiple_of(j * cc, 128), cc)]
                cps.append(pltpu.make_async_remote_copy(
                    src_ref=src, dst_ref=outs[a].at[k], send_sem=send.at[a, k], recv_sem=recv.at[a, k],
                    device_id=(px, py, c), device_id_type=MESH))
        return cps

    def first(ins, outs, sems):
        for cp in copies(ins, outs, sems):
            cp.start()

    def last(ins, outs, sems):
        for cp in copies(ins, outs, sems):
            cp.wait()

    return _Side(list(p3s), [jax.ShapeDtypeStruct((3, r // 2, c), BF) for r, c in shapes],
                 [pltpu.SemaphoreType.DMA((n, 3)), pltpu.SemaphoreType.DMA((n, 3))], first, last)


def _chip_sum(p3, land, shard_shape, axis, idx, name):
    r, c = shard_shape
    hr = r // 2
    tr = 64
    nt = hr // tr

    def body(idx_ref, p_ref, l_ref, o_ref):
        acc = p_ref[...].astype(F32)
        for k in range(3):
            acc = acc + l_ref[k].astype(F32)
        o_ref[...] = acc

    own = (pl.BlockSpec((None, tr, c), lambda i, idx: (idx[0], i, 0)) if axis == 0
           else pl.BlockSpec((None, tr, c), lambda i, idx: (0, i, idx[0])))
    return pl.pallas_call(
        body, name=name,
        grid_spec=pltpu.PrefetchScalarGridSpec(
            num_scalar_prefetch=1, grid=(nt,),
            in_specs=[own, pl.BlockSpec((3, tr, c), lambda i, idx: (0, i, 0))],
            out_specs=pl.BlockSpec((tr, c), lambda i, idx: (idx[1] * nt + i, 0))),
        out_shape=jax.ShapeDtypeStruct((r, c), F32),
        compiler_params=_params(("parallel",)),
    )(idx, p3, land)


def _rs_pair_gather(fulls):
    n = len(fulls)

    def body(*refs):
        ins, outs = refs[:n], refs[n:2 * n]
        send, recv = refs[2 * n:]
        x, y, c, _ = _place()
        cps = []
        for a in range(n):
            hr = fulls[a].shape[0] // 2
            rows = pl.ds(pl.multiple_of(c * hr, 8), hr)
            cp = pltpu.make_async_remote_copy(
                src_ref=ins[a].at[rows, :], dst_ref=outs[a].at[rows, :], send_sem=send.at[a], recv_sem=recv.at[a],
                device_id=(x, y, 1 - c), device_id_type=MESH)
            cp.start()
            cps.append(cp)
        for a, cp in enumerate(cps):
            cp.wait_send()
            hr = fulls[a].shape[0] // 2
            other = pl.ds(pl.multiple_of((1 - c) * hr, 8), hr)
            pltpu.make_async_remote_copy(
                src_ref=ins[a].at[other, :], dst_ref=outs[a].at[other, :], send_sem=send.at[a], recv_sem=recv.at[a],
                device_id=(x, y, 1 - c), device_id_type=MESH).wait_recv()

    return pl.pallas_call(
        body, name="grads_pair_gather",
        in_specs=[HBM] * n, out_specs=[HBM] * n,
        out_shape=[jax.ShapeDtypeStruct(f.shape, F32) for f in fulls],
        input_output_aliases={a: a for a in range(n)},
        scratch_shapes=[pltpu.SemaphoreType.DMA((n,)), pltpu.SemaphoreType.DMA((n,))],
    )(*fulls)


def _sibling_send_side(arr):
    def copy(ins, outs, sems):
        x, y, c, _ = _place()
        return pltpu.make_async_remote_copy(
            src_ref=ins[0], dst_ref=outs[0], send_sem=sems[0].at[0], recv_sem=sems[1].at[0],
            device_id=(x, y, 1 - c), device_id_type=MESH)

    return _Side([arr], [jax.ShapeDtypeStruct(arr.shape, arr.dtype)],
                 [pltpu.SemaphoreType.DMA((1,)), pltpu.SemaphoreType.DMA((1,))],
                 lambda ins, outs, sems: copy(ins, outs, sems).start(),
                 lambda ins, outs, sems: copy(ins, outs, sems).wait())


def _add_bf16(a, b, name):
    r, c = a.shape
    tr = 64

    def body(a_ref, b_ref, o_ref):
        o_ref[...] = (a_ref[...].astype(F32) + b_ref[...].astype(F32)).astype(BF)

    blk = pl.BlockSpec((tr, c), lambda i: (i, 0))
    return pl.pallas_call(
        body, name=name, grid=(r // tr,), in_specs=[blk, blk], out_specs=blk,
        out_shape=jax.ShapeDtypeStruct((r, c), BF), compiler_params=_params(("parallel",)),
    )(a, b)


def _rs_partials(grads, shapes, axes, tag):
    cidx = jnp.reshape(lax.axis_index("c"), (1,)).astype(jnp.int32)
    g3s = [_as3d(g, s, ax) for g, s, ax in zip(grads, shapes, axes)]
    lands = _rs_pair_exchange(g3s, f"grads_pair_exchange_{tag}")
    return [_pair_sum(g3, l, cidx, f"pair_sum_{tag}_{a}") for a, (g3, l) in enumerate(zip(g3s, lands))]


def _rs_finish(p3s, landed, shapes, axes):
    x, y, c = lax.axis_index("x"), lax.axis_index("y"), lax.axis_index("c")
    idx = jnp.stack([2 * x + y, c]).astype(jnp.int32)
    fulls = [_chip_sum(p3, l2, s, ax, idx, f"chip_sum_{a}")
             for a, (p3, l2, s, ax) in enumerate(zip(p3s, landed, shapes, axes))]
    return _rs_pair_gather(fulls)


NSMALL = 8


def _small_all_reduce(g_nw, dlb, g_hnw, g_wf, loss_vec):
    def body(nw_ref, lb_ref, hn_ref, wf_ref, ls_ref, out_ref, pack_ref, buf_ref, send, recv):
        x, y, c = lax.axis_index("x"), lax.axis_index("y"), lax.axis_index("c")
        me = 4 * x + 2 * y + c
        pack_ref[...] = jnp.zeros_like(pack_ref)
        pack_ref[0:1, :] = nw_ref[...]
        pack_ref[1:2, :] = lb_ref[...]
        pack_ref[2:3, 0:HK] = hn_ref[...]
        pack_ref[3:4, :] = wf_ref[...]
        pack_ref[4:5, :] = ls_ref[...]
        buf_ref[me] = pack_ref[...]
        cps = []
        for d in range(1, 8):
            dx, dy, dc = d >> 2, (d >> 1) & 1, d & 1
            peer = (1 - x if dx else x, 1 - y if dy else y, 1 - c if dc else c)
            cp = pltpu.make_async_remote_copy(
                src_ref=pack_ref, dst_ref=buf_ref.at[me], send_sem=send.at[d - 1], recv_sem=recv.at[d - 1],
                device_id=peer, device_id_type=MESH)
            cp.start()
            cps.append(cp)
        for d in range(1, 8):
            dx, dy, dc = d >> 2, (d >> 1) & 1, d & 1
            src = 4 * (1 - x if dx else x) + 2 * (1 - y if dy else y) + (1 - c if dc else c)
            pltpu.make_async_remote_copy(
                src_ref=pack_ref, dst_ref=buf_ref.at[src], send_sem=send.at[d - 1], recv_sem=recv.at[d - 1],
                device_id=(x, y, c), device_id_type=MESH).wait_recv()
        for cp in cps:
            cp.wait_send()
        acc = buf_ref[0]
        for i in range(1, 8):
            acc = acc + buf_ref[i]
        out_ref[...] = acc

    vm = pl.BlockSpec(memory_space=pltpu.VMEM)
    return pl.pallas_call(
        body, name="small_all_reduce",
        in_specs=[vm] * 5, out_specs=vm,
        out_shape=jax.ShapeDtypeStruct((NSMALL, D), F32),
        scratch_shapes=[pltpu.VMEM((NSMALL, D), F32), pltpu.VMEM((8, NSMALL, D), F32),
                        pltpu.SemaphoreType.DMA((7,)), pltpu.SemaphoreType.DMA((7,))],
    )(g_nw, dlb, g_hnw, g_wf, loss_vec)


def _adamw_math(w, g, m, v):
    m = B1 * m + (1.0 - B1) * g
    v = B2 * v + (1.0 - B2) * (g * g)
    m_hat = m / (1.0 - B1 ** STEP)
    v_hat = v / (1.0 - B2 ** STEP)
    return -LR * (m_hat / (jnp.sqrt(v_hat) + ADAM_EPS) + WD * w), m, v


def _adamw(w, g, m, v, name):
    r, c = w.shape
    tr = 64

    def body(w_ref, g_ref, m_ref, v_ref, d_ref, nm_ref, nv_ref, go_ref):
        g = g_ref[...]
        d_ref[...], nm_ref[...], nv_ref[...] = _adamw_math(w_ref[...], g, m_ref[...], v_ref[...])
        go_ref[...] = g

    blk = pl.BlockSpec((tr, c), lambda i: (i, 0))
    return pl.pallas_call(
        body, name=name, grid=(r // tr,), in_specs=[blk] * 4, out_specs=[blk] * 4,
        out_shape=[jax.ShapeDtypeStruct((r, c), F32)] * 4,
        compiler_params=_params(("parallel",)),
    )(w, g, m, v)


def _small_update(red, lbl, params):
    def body(red_ref, *refs):
        ins, outs = refs[:12], refs[12:]
        lb = _lower_bound(ins[3][...])
        dl0 = red_ref[1:2, :] * lb * (1.0 - lb)
        row = lax.broadcasted_iota(jnp.int32, (2, D), 0)
        grads = [red_ref[0:1, :], jnp.where(row == 0, dl0, -dl0), red_ref[2:3, 0:HK], red_ref[3:4, :]]
        for i, g in enumerate(grads):
            w, m, v = ins[3 * i][...], ins[3 * i + 1][...], ins[3 * i + 2][...]
            d, nm, nv = _adamw_math(w, g, m, v)
            outs[4 * i][...] = g
            outs[4 * i + 1][...] = d
            outs[4 * i + 2][...] = nm
            outs[4 * i + 3][...] = nv
        outs[16][...] = jnp.sum(red_ref[4:5, :], axis=1, keepdims=True)

    flat = [a for p in params for a in p]
    vm = pl.BlockSpec(memory_space=pltpu.VMEM)
    shapes = [jax.ShapeDtypeStruct(p[0].shape, F32) for p in params for _ in range(4)]
    return pl.pallas_call(
        body, name="small_update",
        in_specs=[vm] * 13, out_specs=[vm] * 17,
        out_shape=shapes + [jax.ShapeDtypeStruct((1, 1), F32)],
    )(red, *flat)


def kernel(x, positions, norm_w, w_in, lb_logits, hgrn_norm_w, w_branch_a, w_branch_b, w_out, final_norm_w, loss_target, m_norm_w, m_w_in, m_lb_logits, m_hgrn_norm_w, m_w_branch_a, m_w_branch_b, m_w_out, m_final_norm_w, v_norm_w, v_w_in, v_lb_logits, v_hgrn_norm_w, v_w_branch_a, v_w_branch_b, v_w_out, v_final_norm_w):
    big_w = [w_in[0], w_branch_a[0], w_branch_b[0], w_out[0]]
    big_m = [m_w_in[0], m_w_branch_a[0], m_w_branch_b[0], m_w_out[0]]
    big_v = [v_w_in[0], v_w_branch_a[0], v_w_branch_b[0], v_w_out[0]]
    shapes = [w.shape for w in big_w]
    wf = final_norm_w.reshape(1, D)

    shards = [w.astype(BF) for w in big_w]
    loc = _local_step(x[0], positions.reshape(T, 1), norm_w, lb_logits, hgrn_norm_w, wf, loss_target[0],
                      *shards, shard_shapes=shapes)
    g_big = [loc["g_win"], loc["g_wa"], loc["g_wb"], loc["g_wout"]]
    red = _small_all_reduce(loc["g_nw"], loc["dlb"], loc["g_hnw"], loc["g_wf"], loc["loss_vec"])

    small = _small_update(red, lb_logits, [
        (norm_w, m_norm_w, v_norm_w), (lb_logits, m_lb_logits, v_lb_logits),
        (hgrn_norm_w, m_hgrn_norm_w, v_hgrn_norm_w),
        (wf, m_final_norm_w.reshape(1, D), v_final_norm_w.reshape(1, D))])
    loss = small[16].reshape(())
    sg, sd, sm, sv = ([small[4 * i + j] for i in range(4)] for j in range(4))
    for lst in (sg, sd, sm, sv):
        lst[3] = lst[3].reshape(D)
    upd = [_adamw(w, g, m, v, f"adamw_{a}") for a, (w, g, m, v) in enumerate(zip(big_w, g_big, big_m, big_v))]
    bd, bm, bv, bg = ([u[j][None] for u in upd] for j in range(4))

    def order(s, b):
        return [s[0], b[0], s[1], s[2], b[1], b[2], b[3], s[3]]

    return (loss, loc["gx"][None], *order(sg, bg), *order(sd, bd), *order(sm, bm), *order(sv, bv))
```

```python
import functools

import jax
import jax.numpy as jnp
from jax import lax
from jax.experimental import pallas as pl
from jax.experimental.pallas import tpu as pltpu

T = 2048
D = 1024
NIN = 11264
HEADS = 8
HK = 128
CH = 16
NCH = T // CH
HSTEP = 2
ATT_GROUPS = ((128, 1), (512, 4), (2048, 16))
ATT_COL0 = 4096
AG_COL0 = 8704
GATE_COL0 = 9216
EPS = 1e-6
ROPE_THETA = 10000.0
LR, B1, B2, ADAM_EPS, WD, STEP = 0.001, 0.9, 0.999, 1e-08, 0.01, 10

F32 = jnp.float32
BF = jnp.bfloat16
VMEM_LIMIT = 56 * 1024 * 1024

_NN = (((1,), (0,)), ((), ()))
_NT = (((1,), (1,)), ((), ()))
_TN = (((0,), (0,)), ((), ()))


def _dot(a, b, dims=_NN):
    return lax.dot_general(a, b, dims, preferred_element_type=F32)


def _bdot(a, b, dims=_NN):
    return lax.dot_general(a.astype(BF), b.astype(BF), dims, preferred_element_type=F32)


def _sigmoid(x):
    return jax.nn.sigmoid(x)


def _params(sem=None):
    return pltpu.CompilerParams(dimension_semantics=sem, vmem_limit_bytes=VMEM_LIMIT)


def _matmul(a, b, *, ta=False, tb=False, out_dtype=F32, tm=512, tn=512, tk=None, name, side=None):
    m = a.shape[1] if ta else a.shape[0]
    kdim = a.shape[0] if ta else a.shape[1]
    n = b.shape[0] if tb else b.shape[1]
    tk = tk or kdim
    tm, tn = min(tm, m), min(tn, n)
    nm, nn, nk = m // tm, n // tn, kdim // tk
    dims = (((0 if ta else 1,), (1 if tb else 0,)), ((), ()))
    s_arrays, s_in_specs, s_shapes, s_out_specs, s_sems = _side_io(side)
    na, no = len(s_arrays), len(s_shapes)
    nacc = 1 if nk > 1 else 0

    def body(*refs):
        a_ref, b_ref = refs[:2]
        s_ins, o_ref, s_outs = refs[2:2 + na], refs[2 + na], refs[3 + na:3 + na + no]
        scratch = refs[3 + na + no:]
        s_sem_refs = scratch[nacc:]
        i, j, k = pl.program_id(0), pl.program_id(1), pl.program_id(2)
        if side is not None:
            @pl.when((i == 0) & (j == 0) & (k == 0))
            def _():
                side.first(s_ins, s_outs, s_sem_refs)

        prod = _bdot(a_ref[...], b_ref[...], dims)
        if nk == 1:
            o_ref[...] = prod.astype(out_dtype)
        else:
            acc = scratch[0]

            @pl.when(k == 0)
            def _():
                acc[...] = prod

            @pl.when(k > 0)
            def _():
                acc[...] += prod

            @pl.when(k == nk - 1)
            def _():
                o_ref[...] = acc[...].astype(out_dtype)

        if side is not None:
            @pl.when((i == nm - 1) & (j == nn - 1) & (k == nk - 1))
            def _():
                side.last(s_ins, s_outs, s_sem_refs)

    a_spec = pl.BlockSpec((tk, tm), lambda i, j, k: (k, i)) if ta else pl.BlockSpec((tm, tk), lambda i, j, k: (i, k))
    b_spec = pl.BlockSpec((tn, tk), lambda i, j, k: (j, k)) if tb else pl.BlockSpec((tk, tn), lambda i, j, k: (k, j))
    sem = ("parallel", "parallel", "arbitrary") if side is None else ("arbitrary",) * 3
    out = pl.pallas_call(
        body, name=name, grid=(nm, nn, nk),
        in_specs=[a_spec, b_spec] + s_in_specs,
        out_specs=[pl.BlockSpec((tm, tn), lambda i, j, k: (i, j))] + s_out_specs,
        out_shape=[jax.ShapeDtypeStruct((m, n), out_dtype)] + s_shapes,
        scratch_shapes=([pltpu.VMEM((tm, tn), F32)] if nk > 1 else []) + s_sems,
        compiler_params=_params(sem),
    )(a, b, *s_arrays)
    return out[0] if side is None else (out[0], out[1:])


DZ_TILE = 512


def _part_offsets(parts):
    counts = [p.shape[1] // DZ_TILE for p in parts]
    offs = [sum(counts[:i]) for i in range(len(parts))]
    return counts, offs


def _part_spec(rows, cnt, off, tile_axis):
    def index(*g):
        return (0 if rows is None else g[0], jnp.clip(g[tile_axis] - off, 0, cnt - 1))
    return index


def _grad_w_in(h, parts):
    counts, offs = _part_offsets(parts)
    n = len(parts)

    def body(h_ref, *refs):
        o_ref = refs[n]
        j = pl.program_id(0)
        for p_ref, cnt, off in zip(refs[:n], counts, offs):
            @pl.when((j >= off) & (j < off + cnt))
            def _(p_ref=p_ref):
                o_ref[...] = _bdot(h_ref[...], p_ref[...], _TN).astype(BF)

    return pl.pallas_call(
        body, name="g_win", grid=(sum(counts),),
        in_specs=[pl.BlockSpec((T, D), lambda j: (0, 0))] +
                 [pl.BlockSpec((T, DZ_TILE), _part_spec(None, c, o, 0)) for c, o in zip(counts, offs)],
        out_specs=pl.BlockSpec((D, DZ_TILE), lambda j: (0, j)),
        out_shape=jax.ShapeDtypeStruct((D, NIN), BF),
        compiler_params=_params(("parallel",)),
    )(h, *parts)


def _grad_w_in_half(h, parts, half_idx, side=None):
    counts, offs = _part_offsets(parts)
    n = len(parts)
    nj = sum(counts)
    s_arrays, s_in_specs, s_shapes, s_out_specs, s_sems = _side_io(side)
    na, no = len(s_arrays), len(s_shapes)

    def body(idx_ref, h_ref, *refs):
        s_ins, o_ref, s_outs, s_sem_refs = refs[n:n + na], refs[n + na], refs[n + na + 1:n + na + 1 + no], refs[n + na + 1 + no:]
        j = pl.program_id(0)
        if side is not None:
            @pl.when(j == 0)
            def _():
                side.first(s_ins, s_outs, s_sem_refs)

        for p_ref, cnt, off in zip(refs[:n], counts, offs):
            @pl.when((j >= off) & (j < off + cnt))
            def _(p_ref=p_ref):
                o_ref[...] = _bdot(h_ref[...], p_ref[...], _TN).astype(BF)

        if side is not None:
            @pl.when(j == nj - 1)
            def _():
                side.last(s_ins, s_outs, s_sem_refs)

    def part_spec(cnt, off):
        return pl.BlockSpec((T, DZ_TILE), lambda j, idx: (0, jnp.clip(j - off, 0, cnt - 1)))

    out = pl.pallas_call(
        body, name="g_win_half" if side is None else "g_win_half_carrying",
        grid_spec=pltpu.PrefetchScalarGridSpec(
            num_scalar_prefetch=1, grid=(nj,),
            in_specs=[pl.BlockSpec((T, D // 2), lambda j, idx: (0, idx[0]))] +
                     [part_spec(c, o) for c, o in zip(counts, offs)] + s_in_specs,
            out_specs=[pl.BlockSpec((D // 2, DZ_TILE), lambda j, idx: (0, j))] + s_out_specs,
            scratch_shapes=s_sems),
        out_shape=[jax.ShapeDtypeStruct((D // 2, NIN), BF)] + s_shapes,
        compiler_params=_params(("parallel",) if side is None else ("arbitrary",)),
    )(half_idx, h, *parts, *s_arrays)
    return out[0] if side is None else (out[0], out[1:])


def _side_io(side):
    if side is None:
        return [], [], [], [], []
    return (side.arrays, [HBM] * len(side.arrays), side.out_shapes, [HBM] * len(side.out_shapes), side.sems)


def _grad_x(parts, w_in, x, dout, norm_w, side=None):
    counts, offs = _part_offsets(parts)
    n = len(parts)
    tm = 1024
    nm, nk = T // tm, sum(counts)
    s_arrays, s_in_specs, s_shapes, s_out_specs, s_sems = _side_io(side)
    na, no = len(s_arrays), len(s_shapes)

    def body(*refs):
        w_ref, x_ref, dout_ref, nw_ref = refs[n:n + 4]
        s_ins = refs[n + 4:n + 4 + na]
        gx_ref, gw_ref = refs[n + 4 + na:n + 6 + na]
        s_outs = refs[n + 6 + na:n + 6 + na + no]
        acc = refs[n + 6 + na + no]
        s_sem_refs = refs[n + 7 + na + no:]
        i, k = pl.program_id(0), pl.program_id(1)

        @pl.when((i == 0) & (k == 0))
        def _():
            gw_ref[...] = jnp.zeros_like(gw_ref)
            if side is not None:
                side.first(s_ins, s_outs, s_sem_refs)

        @pl.when(k == 0)
        def _():
            acc[...] = jnp.zeros_like(acc)

        for p_ref, cnt, off in zip(refs[:n], counts, offs):
            @pl.when((k >= off) & (k < off + cnt))
            def _(p_ref=p_ref):
                acc[...] += _bdot(p_ref[...], w_ref[...], _NT)

        @pl.when(k == nk - 1)
        def _():
            gw = jnp.zeros((1, D), F32)
            for c in range(tm // BLK):
                rows = pl.ds(BLK * c, BLK)
                xv, dhv = x_ref[rows, :], acc[rows, :]
                r = lax.rsqrt(jnp.mean(xv * xv, axis=-1, keepdims=True) + EPS)
                nrm = xv * r
                dn = dhv * nw_ref[...]
                gw = gw + jnp.sum(dhv * nrm, axis=0, keepdims=True)
                gx_ref[rows, :] = dout_ref[rows, :] + r * (dn - nrm * jnp.mean(dn * nrm, axis=-1, keepdims=True))
            gw_ref[...] += gw

        if side is not None:
            @pl.when((i == nm - 1) & (k == nk - 1))
            def _():
                side.last(s_ins, s_outs, s_sem_refs)

    row = pl.BlockSpec((tm, D), lambda i, k: (i, 0))
    vec = pl.BlockSpec((1, D), lambda i, k: (0, 0))
    out = pl.pallas_call(
        body, name="grad_x", grid=(nm, nk),
        in_specs=[pl.BlockSpec((tm, DZ_TILE), _part_spec(0, c, o, 1)) for c, o in zip(counts, offs)] +
                 [pl.BlockSpec((D, DZ_TILE), lambda i, k: (0, k)), row, row, vec] + s_in_specs,
        out_specs=[row, vec] + s_out_specs,
        out_shape=[jax.ShapeDtypeStruct((T, D), F32), jax.ShapeDtypeStruct((1, D), F32)] + s_shapes,
        scratch_shapes=[pltpu.VMEM((tm, D), F32)] + s_sems,
        compiler_params=_params(("arbitrary", "arbitrary")),
    )(*parts, w_in, x, dout, norm_w, *s_arrays)
    return out[0], out[1], out[2:]


def _norm_and_rope_tables(x, w, pos, invf, side=None, own=None):
    tm = 256
    nm = T // tm
    s_arrays, s_in_specs, s_shapes, s_out_specs, s_sems = _side_io(side)
    na, no = len(s_arrays), len(s_shapes)
    nz = 0 if own is None else 1
    wsh, blk = own if own is not None else (None, jnp.zeros((1,), jnp.int32))

    def body(blk_ref, *refs):
        x_ref, w_ref, pos_ref, invf_ref = refs[:4]
        s_ins = refs[4 + nz:4 + nz + na]
        h_ref, cos_ref, sa_ref, sb_ref = refs[4 + nz + na:8 + nz + na]
        s_outs = refs[8 + 2 * nz + na:8 + 2 * nz + na + no]
        s_sem_refs = refs[8 + 2 * nz + na + no:]
        if side is not None:
            @pl.when(pl.program_id(0) == 0)
            def _():
                side.first(s_ins, s_outs, s_sem_refs)

        xv = x_ref[...]
        r = lax.rsqrt(jnp.mean(xv * xv, axis=-1, keepdims=True) + EPS)
        h = (xv * r * w_ref[...]).astype(BF)
        h_ref[...] = h
        if own is not None:
            refs[8 + nz + na][...] = _dot(h, refs[4][...])
        first = (lax.broadcasted_iota(jnp.int32, (tm, 128), 1) % 64) < 32
        ang = pos_ref[...].astype(F32) * invf_ref[...]
        s = jnp.sin(ang)
        cos_ref[...] = jnp.cos(ang)
        sa_ref[...] = jnp.where(first, -s, 0.0)
        sb_ref[...] = jnp.where(first, 0.0, s)
        if side is not None:
            @pl.when(pl.program_id(0) == nm - 1)
            def _():
                side.last(s_ins, s_outs, s_sem_refs)

    tab = pl.BlockSpec((tm, 128), lambda i, b: (i, 0))
    own_in = [] if own is None else [pl.BlockSpec(wsh.shape, lambda i, b: (0, 0))]
    own_out = [] if own is None else [pl.BlockSpec((tm, wsh.shape[1]), lambda i, b: (i, b[0]))]
    own_shape = [] if own is None else [jax.ShapeDtypeStruct((T, NIN), F32)]
    out = pl.pallas_call(
        body, name="norm_and_rope_tables",
        grid_spec=pltpu.PrefetchScalarGridSpec(
            num_scalar_prefetch=1, grid=(nm,),
            in_specs=[pl.BlockSpec((tm, D), lambda i, b: (i, 0)), pl.BlockSpec((1, D), lambda i, b: (0, 0)),
                      pl.BlockSpec((tm, 1), lambda i, b: (i, 0)), pl.BlockSpec((1, 128), lambda i, b: (0, 0))]
                     + own_in + s_in_specs,
            out_specs=[pl.BlockSpec((tm, D), lambda i, b: (i, 0)), tab, tab, tab] + own_out + s_out_specs,
            scratch_shapes=s_sems),
        out_shape=[jax.ShapeDtypeStruct((T, D), BF)] + [jax.ShapeDtypeStruct((T, 128), F32)] * 3 + own_shape + s_shapes,
        compiler_params=_params(("parallel",) if side is None else ("arbitrary",)),
    )(blk, x, w, pos, invf, *([] if own is None else [wsh]), *s_arrays)
    return out[0], out[1], out[2], out[3], (out[4] if own is not None else None), out[4 + nz:]


def _z_blocks(h, w, z, idx, nb, side, name, fill=None):
    tm, tn = 1024, NIN // 8
    s_arrays, s_in_specs, s_shapes, s_out_specs, s_sems = _side_io(side)
    na, no = len(s_arrays), len(s_shapes)
    nm, ns = T // tm, 2 * nb
    nf = 0 if fill is None else 1

    def col(first, i, s, b):
        return (0, b[first + s // 2] * 2 + s % 2)

    def body(idx_ref, h_ref, w_ref, zin_ref, *refs):
        s_ins = refs[nf:nf + na]
        o_ref = refs[nf + na]
        s_outs = refs[nf + na + 1 + nf:nf + na + 1 + nf + no]
        s_sem_refs = refs[nf + na + 1 + nf + no + nf:]
        i, s = pl.program_id(0), pl.program_id(1)

        if side is not None:
            @pl.when((i == 0) & (s == 0))
            def _():
                side.first(s_ins, s_outs, s_sem_refs)

        if fill is not None:
            tile = pl.ds(pl.multiple_of((idx_ref[0] * 2 + s) * tn, 128), tn)
            store = pltpu.make_async_copy(w_ref, refs[nf + na + 1].at[:, tile], refs[nf + na + 1 + nf + no].at[0])
            pl.when(i == 0)(store.start)
        o_ref[...] = _dot(h_ref[...], w_ref[...])
        if fill is not None:
            pl.when(i == 0)(store.wait)

        if side is not None:
            @pl.when((i == nm - 1) & (s == ns - 1))
            def _():
                side.last(s_ins, s_outs, s_sem_refs)

    fills = [] if fill is None else [fill]
    out = pl.pallas_call(
        body, name=name,
        grid_spec=pltpu.PrefetchScalarGridSpec(
            num_scalar_prefetch=1, grid=(nm, ns),
            in_specs=[pl.BlockSpec((tm, D), lambda i, s, b: (i, 0)), pl.BlockSpec((D, tn), functools.partial(col, nb)),
                      HBM] + [HBM] * nf + s_in_specs,
            out_specs=[pl.BlockSpec((tm, tn), lambda i, s, b: (i, col(0, i, s, b)[1]))] + [HBM] * nf + s_out_specs,
            scratch_shapes=[pltpu.SemaphoreType.DMA((1,))] * nf + s_sems),
        out_shape=[jax.ShapeDtypeStruct((T, NIN), F32)] + [jax.ShapeDtypeStruct(f.shape, f.dtype) for f in fills]
                  + s_shapes,
        input_output_aliases={3: 0, **({4: 1} if fill is not None else {})},
        compiler_params=_params(("arbitrary", "arbitrary")),
    )(idx, h, w, z, *fills, *s_arrays)
    return (out[0], *out[1:1 + nf], out[1 + nf:])


def _lower_bound(lbl):
    mx = jnp.max(lbl, axis=0, keepdims=True)
    e = jnp.exp(lbl - mx)
    return e[0:1] / jnp.sum(e, axis=0, keepdims=True)


def _cumsum_rows(g, rows):
    b = g
    sh = 1
    while sh < CH:
        b = b + jnp.where(rows >= sh, pltpu.roll(b, sh, axis=0), 0.0)
        sh *= 2
    return b


def _rev_cumsum_rows(g, rows):
    b = g
    sh = 1
    while sh < CH:
        b = b + jnp.where(rows < CH - sh, pltpu.roll(b, CH - sh, axis=0), 0.0)
        sh *= 2
    return b


SUB = CH // 2


def _direct_block(qb, kb, vb, bb, rows8):
    ob = jnp.zeros_like(qb)
    for s in range(SUB):
        e_s = jnp.exp(jnp.where(rows8 >= s, bb - bb[s:s + 1], -jnp.inf))
        ob = ob + jnp.sum(qb * e_s * kb[s:s + 1], axis=1, keepdims=True) * vb[s:s + 1]
    return ob


def _direct_block_bwd(qb, kb, vb, bb, dob, rows8, rowc8):
    dq = dk = dv = db = jnp.zeros_like(qb)
    for s in range(SUB):
        one = (rowc8 == s).astype(F32)
        ks, vs = kb[s:s + 1], vb[s:s + 1]
        e_s = jnp.exp(jnp.where(rows8 >= s, bb - bb[s:s + 1], -jnp.inf))
        qes = qb * e_s
        w = qes * ks
        a = jnp.sum(w, axis=1, keepdims=True)
        da = jnp.sum(dob * vs, axis=1, keepdims=True)
        dv = dv + one * jnp.sum(a * dob, axis=0, keepdims=True)
        dq = dq + da * e_s * ks
        dk = dk + one * jnp.sum(da * qes, axis=0, keepdims=True)
        u = da * w
        db = db + u - one * jnp.sum(u, axis=0, keepdims=True)
    return dq, dk, dv, db


def _cross_factors(q, k, b):
    ref = b[SUB - 1:SUB]
    e_hi, e_lo = jnp.exp(b[SUB:] - ref), jnp.exp(ref - b[:SUB])
    return q[SUB:] * e_hi, k[:SUB] * e_lo, e_hi, e_lo


def _intra_fwd(q, k, v, b, rows8):
    lo = _direct_block(q[:SUB], k[:SUB], v[:SUB], b[:SUB], rows8)
    hi = _direct_block(q[SUB:], k[SUB:], v[SUB:], b[SUB:], rows8)
    qe_hi, ke_lo, _, _ = _cross_factors(q, k, b)
    for s in range(SUB):
        hi = hi + jnp.sum(qe_hi * ke_lo[s:s + 1], axis=1, keepdims=True) * v[s:s + 1]
    return jnp.concatenate([lo, hi], axis=0)


def _intra_bwd(q, k, v, b, do, rows8, rowc8):
    dq_lo, dk_lo, dv_lo, db_lo = _direct_block_bwd(q[:SUB], k[:SUB], v[:SUB], b[:SUB], do[:SUB], rows8, rowc8)
    dq_hi, dk_hi, dv_hi, db_hi = _direct_block_bwd(q[SUB:], k[SUB:], v[SUB:], b[SUB:], do[SUB:], rows8, rowc8)
    qe_hi, ke_lo, e_hi, e_lo = _cross_factors(q, k, b)
    do_hi, v_lo = do[SUB:], v[:SUB]
    dqe = dke = jnp.zeros_like(qe_hi)
    for s in range(SUB):
        one = (rowc8 == s).astype(F32)
        a = jnp.sum(qe_hi * ke_lo[s:s + 1], axis=1, keepdims=True)
        da = jnp.sum(do_hi * v_lo[s:s + 1], axis=1, keepdims=True)
        dv_lo = dv_lo + one * jnp.sum(a * do_hi, axis=0, keepdims=True)
        dqe = dqe + da * ke_lo[s:s + 1]
        dke = dke + one * jnp.sum(da * qe_hi, axis=0, keepdims=True)
    u_hi, u_lo = dqe * qe_hi, dke * ke_lo
    d_ref = jnp.sum(u_lo, axis=0, keepdims=True) - jnp.sum(u_hi, axis=0, keepdims=True)
    db_lo = db_lo - u_lo + (rowc8 == SUB - 1).astype(F32) * d_ref
    cat = lambda lo, hi: jnp.concatenate([lo, hi], axis=0)
    return (cat(dq_lo, dq_hi + dqe * e_hi), cat(dk_lo + dke * e_lo, dk_hi), cat(dv_lo, dv_hi),
            cat(db_lo, db_hi + u_hi))


def _hgrn_fwd(z, lbl, nw, side=None):
    s_arrays, s_in_specs, s_shapes, s_out_specs, s_sems = _side_io(side)
    na, no = len(s_arrays), len(s_shapes)
    nsteps = NCH // HSTEP

    def body(hq_ref, hf_ref, hi_ref, hg_ref, lbl_ref, nw_ref, *refs):
        s_ins, (oraw_ref, og_ref, sh_ref) = refs[:na], refs[na:na + 3]
        s_outs, st_ref, s_sem_refs = refs[na + 3:na + 3 + no], refs[na + 3 + no], refs[na + 4 + no:]

        @pl.when(pl.program_id(0) == 0)
        def _():
            st_ref[...] = jnp.zeros_like(st_ref)
            if side is not None:
                side.first(s_ins, s_outs, s_sem_refs)

        lb_all = _lower_bound(lbl_ref[...])
        rows = lax.broadcasted_iota(jnp.int32, (CH, HK), 0)
        rows8 = lax.broadcasted_iota(jnp.int32, (SUB, HK), 0)
        nwv = nw_ref[...]
        for cc, h in [(cc, h) for cc in range(HSTEP) for h in range(HEADS)]:
            rs = slice(CH * cc, CH * (cc + 1))
            sl = slice(HK * h, HK * (h + 1))
            lb = lb_all[:, sl]
            hq, hf, v, hg = hq_ref[rs, sl], hf_ref[rs, sl], hi_ref[rs, sl], hg_ref[rs, sl]
            q = hq * _sigmoid(hq)
            f = lb + (1.0 - lb) * _sigmoid(hf)
            k = 1.0 - f
            b = _cumsum_rows(jnp.log(f), rows)
            sh_ref[cc, h] = st_ref[h]
            o = _bdot(q * jnp.exp(b), st_ref[h], _NT) + _intra_fwd(q, k, v, b, rows8)
            bl = b[CH - 1:CH]
            st_ref[h] = st_ref[h] * jnp.exp(bl)
            st_ref[h] += _bdot(v, k * jnp.exp(bl - b), _TN)
            oraw_ref[rs, sl] = o
            nrm = o * lax.rsqrt(jnp.mean(o * o, axis=1, keepdims=True) + EPS)
            og_ref[rs, sl] = (nrm * nwv * (hg * _sigmoid(hg))).astype(BF)

        if side is not None:
            @pl.when(pl.program_id(0) == nsteps // 2)
            def _():
                side.mid(s_ins, s_outs, s_sem_refs)

            @pl.when(pl.program_id(0) == nsteps - 1)
            def _():
                side.last(s_ins, s_outs, s_sem_refs)

    zblk = lambda c: pl.BlockSpec((CH * HSTEP, D), lambda i, c=c: (i, c))
    out = pl.pallas_call(
        body, name="hgrn_fwd", grid=(nsteps,),
        in_specs=[zblk(0), zblk(1), zblk(2), zblk(3),
                  pl.BlockSpec((2, D), lambda i: (0, 0)), pl.BlockSpec((1, HK), lambda i: (0, 0))] + s_in_specs,
        out_specs=[zblk(0), zblk(0),
                   pl.BlockSpec((HSTEP, HEADS, HK, HK), lambda i: (i, 0, 0, 0))] + s_out_specs,
        out_shape=[jax.ShapeDtypeStruct((T, D), F32), jax.ShapeDtypeStruct((T, D), BF),
                   jax.ShapeDtypeStruct((NCH, HEADS, HK, HK), F32)] + s_shapes,
        scratch_shapes=[pltpu.VMEM((HEADS, HK, HK), F32)] + s_sems,
        compiler_params=_params(("arbitrary",)),
    )(z, z, z, z, lbl, nw, *s_arrays)
    return out[0], out[1], out[2], out[3:]


def _hgrn_bwd(z, lbl, nw, oraw, dog, shist, side=None):
    hstep = 1
    s_arrays, s_in_specs, s_shapes, s_out_specs, s_sems = _side_io(side)
    na, no = len(s_arrays), len(s_shapes)

    def body(*refs):
        hq_ref, hf_ref, hi_ref, hg_ref, lbl_ref, nw_ref, oraw_ref, dog_ref, sh_ref = refs[:9]
        s_ins = refs[9:9 + na]
        dz_ref, dlb_ref, dnw_ref = refs[9 + na:12 + na]
        s_outs = refs[12 + na:12 + na + no]
        dst_ref = refs[12 + na + no]
        s_sem_refs = refs[13 + na + no:]

        @pl.when(pl.program_id(0) == 0)
        def _():
            dst_ref[...] = jnp.zeros_like(dst_ref)
            dlb_ref[...] = jnp.zeros_like(dlb_ref)
            dnw_ref[...] = jnp.zeros_like(dnw_ref)
            if side is not None:
                side.first(s_ins, s_outs, s_sem_refs)

        lb_all = _lower_bound(lbl_ref[...])
        rows = lax.broadcasted_iota(jnp.int32, (CH, HK), 0)
        rowc = lax.broadcasted_iota(jnp.int32, (CH, 1), 0)
        rows8 = lax.broadcasted_iota(jnp.int32, (SUB, HK), 0)
        rowc8 = lax.broadcasted_iota(jnp.int32, (SUB, 1), 0)
        nwv = nw_ref[...]
        dnw = jnp.zeros((1, HK), F32)
        for cc, h in [(cc, h) for cc in reversed(range(hstep)) for h in range(HEADS)]:
            rs = slice(CH * cc, CH * (cc + 1))
            sl = slice(HK * h, HK * (h + 1))
            lb = lb_all[:, sl]
            hq, hf, v, hg = hq_ref[rs, sl], hf_ref[rs, sl], hi_ref[rs, sl], hg_ref[rs, sl]
            o, dg_out = oraw_ref[rs, sl], dog_ref[rs, sl]
            sg = _sigmoid(hg)
            sil = hg * sg
            r = lax.rsqrt(jnp.mean(o * o, axis=1, keepdims=True) + EPS)
            nrm = o * r
            d_hg = dg_out * (nrm * nwv) * (sg * (1.0 + hg * (1.0 - sg)))
            dn = dg_out * nwv * sil
            dnw = dnw + jnp.sum(dg_out * nrm * sil, axis=0, keepdims=True)
            do = r * (dn - nrm * jnp.mean(dn * nrm, axis=1, keepdims=True))
            sq = _sigmoid(hq)
            q = hq * sq
            sig = _sigmoid(hf)
            f = lb + (1.0 - lb) * sig
            k = 1.0 - f
            b = _cumsum_rows(jnp.log(f), rows)
            eb = jnp.exp(b)
            qe = q * eb
            bl = b[CH - 1:CH]
            ebl = jnp.exp(bl)
            kdec = jnp.exp(bl - b)
            ke = k * kdec
            dqe = _bdot(do, sh_ref[cc, h])
            dq = dqe * eb
            db = dqe * qe
            dke = _bdot(v, dst_ref[h])
            dv = _bdot(ke, dst_ref[h], _NT)
            dk = dke * kdec
            rr = dke * ke
            db = db - rr
            db_last = (jnp.sum(rr, axis=0, keepdims=True)
                       + ebl * jnp.sum(dst_ref[h] * sh_ref[cc, h], axis=0, keepdims=True))
            dst_ref[h] = dst_ref[h] * ebl
            dst_ref[h] += _bdot(do, qe, _TN)
            dq_i, dk_i, dv_i, db_i = _intra_bwd(q, k, v, b, do, rows8, rowc8)
            dq, dk, dv = dq + dq_i, dk + dk_i, dv + dv_i
            db = db + db_i + (rowc == CH - 1).astype(F32) * db_last
            dgl = _rev_cumsum_rows(db, rows)
            df = dgl / f - dk
            dlb_ref[:, sl] += jnp.sum(df * (1.0 - sig), axis=0, keepdims=True)
            dz_ref[rs, sl] = (dq * (sq * (1.0 + hq * (1.0 - sq)))).astype(BF)
            dz_ref[rs, D + HK * h:D + HK * (h + 1)] = (df * (1.0 - lb) * sig * (1.0 - sig)).astype(BF)
            dz_ref[rs, 2 * D + HK * h:2 * D + HK * (h + 1)] = dv.astype(BF)
            dz_ref[rs, 3 * D + HK * h:3 * D + HK * (h + 1)] = d_hg.astype(BF)
        dnw_ref[...] += dnw
        if side is not None:
            @pl.when(pl.program_id(0) == NCH // hstep - 1)
            def _():
                side.last(s_ins, s_outs, s_sem_refs)

    rev = lambda i: NCH // hstep - 1 - i
    zblk = lambda c: pl.BlockSpec((CH * hstep, D), lambda i, c=c: (rev(i), c))
    out = pl.pallas_call(
        body, name="hgrn_bwd", grid=(NCH // hstep,),
        in_specs=[zblk(0), zblk(1), zblk(2), zblk(3),
                  pl.BlockSpec((2, D), lambda i: (0, 0)), pl.BlockSpec((1, HK), lambda i: (0, 0)),
                  zblk(0), zblk(0),
                  pl.BlockSpec((hstep, HEADS, HK, HK), lambda i: (rev(i), 0, 0, 0))] + s_in_specs,
        out_specs=[pl.BlockSpec((CH * hstep, 4 * D), lambda i: (rev(i), 0)),
                   pl.BlockSpec((1, D), lambda i: (0, 0)), pl.BlockSpec((1, HK), lambda i: (0, 0))] + s_out_specs,
        out_shape=[jax.ShapeDtypeStruct((T, 4 * D), BF), jax.ShapeDtypeStruct((1, D), F32),
                   jax.ShapeDtypeStruct((1, HK), F32)] + s_shapes,
        scratch_shapes=[pltpu.VMEM((HEADS, HK, HK), F32)] + s_sems,
        compiler_params=_params(("arbitrary",)),
    )(z, z, z, z, lbl, nw, oraw, dog, shist, *s_arrays)
    return out[0], out[1], out[2], out[3:]


BLK = 128
NBLK = T // BLK
QK_SCALE = 0.125


def _head_masks():
    lane = lax.broadcasted_iota(jnp.int32, (1, BLK), 1)
    return [(lane < 64).astype(F32), (lane >= 64).astype(F32)]


def _pieces(dil):
    m = T // dil
    out = []
    for r in range(dil):
        for j in range(m // BLK):
            start = r + dil * BLK * j
            rows = pl.ds(start, BLK, stride=dil) if dil > 1 else pl.ds(start, BLK)
            out.append((rows, r * m + BLK * j))
    return out


def _rope(x, c, sa, sb):
    return x * c + pltpu.roll(x, 96, axis=1) * sa + pltpu.roll(x, 32, axis=1) * sb


def _rope_t(d, c, sa, sb):
    return d * c + pltpu.roll(d * sa, 32, axis=1) + pltpu.roll(d * sb, 96, axis=1)


def _rope_and_regroup(dil, q_ref, k_ref, v_ref, tables, stage_q, stage_k, qr_ref, kr_ref, vr_ref):
    cos_ref, sa_ref, sb_ref = tables
    to_q, to_k = (qr_ref, kr_ref) if dil == 1 else (stage_q, stage_k)
    for c in range(T // BLK):
        rows = pl.ds(BLK * c, BLK)
        cs, sa, sb = cos_ref[rows, :], sa_ref[rows, :], sb_ref[rows, :]
        to_q[rows, :] = (_rope(q_ref[rows, :], cs, sa, sb) * QK_SCALE).astype(to_q.dtype)
        to_k[rows, :] = _rope(k_ref[rows, :], cs, sa, sb).astype(to_k.dtype)
    for rows, dst in _pieces(dil):
        drows = pl.ds(dst, BLK)
        if dil > 1:
            qr_ref[drows, :] = stage_q[rows, :].astype(qr_ref.dtype)
            kr_ref[drows, :] = stage_k[rows, :].astype(kr_ref.dtype)
        vr_ref[drows, :] = v_ref[rows, :].astype(vr_ref.dtype)


def _window_bias(bias_ref):
    ii = lax.broadcasted_iota(jnp.int32, (2 * BLK, BLK), 0) % BLK
    jj = lax.broadcasted_iota(jnp.int32, (2 * BLK, BLK), 1)
    bias_ref[0] = jnp.where(jj <= ii, 0.0, -jnp.inf)
    bias_ref[1] = jnp.where(jj >= ii, 0.0, -jnp.inf)


def _blocks(bi):
    if isinstance(bi, int):
        return pl.ds(bi * BLK, BLK), pl.ds(max(bi - 1, 0) * BLK, BLK)
    return (pl.ds(pl.multiple_of(bi * BLK, BLK), BLK),
            pl.ds(pl.multiple_of(jnp.maximum(bi - 1, 0) * BLK, BLK), BLK))


def _stack_heads(x, masks):
    return jnp.concatenate([x * masks[0].astype(x.dtype), x * masks[1].astype(x.dtype)], axis=0).astype(BF)


def _attn_fwd(z, cos, sa, sb):
    def body(q_ref, k_ref, v_ref, ag_ref, cos_ref, sa_ref, sb_ref, ob_ref, opre_ref, lse_ref, qr_ref, kr_ref, vr_ref,
             bias_ref, og_ref, lg_ref, otok_ref, ltok_ref, sc_ref):
        g = pl.program_id(1)
        masks = _head_masks()

        @pl.when(g == 0)
        def _():
            _window_bias(bias_ref)

        def group(gi):
            dil = ATT_GROUPS[gi][1]
            nblk = (T // dil) // BLK
            _rope_and_regroup(dil, q_ref, k_ref, v_ref, (cos_ref, sa_ref, sb_ref), lg_ref.at[0], lg_ref.at[1],
                              qr_ref, kr_ref, vr_ref)

            def scores(bi, slot):
                cur, prev = _blocks(bi)
                q2 = _stack_heads(qr_ref[cur, :], masks)
                sc_ref[slot, 0] = _dot(q2, kr_ref[cur, :], _NT) + bias_ref[0]
                if nblk > 1:
                    sc_ref[slot, 1] = (_dot(q2, kr_ref[prev, :], _NT)
                                       + (bias_ref[1] + jnp.where((bi % nblk) != 0, 0.0, -jnp.inf)))

            def finish(bi, slot):
                cur, prev = _blocks(bi)
                s_c, vc = sc_ref[slot, 0], vr_ref[cur, :]
                if nblk > 1:
                    s_p, vp = sc_ref[slot, 1], vr_ref[prev, :]
                    mx = jnp.max(jnp.maximum(s_c, s_p), axis=1, keepdims=True)
                    p_c, p_p = jnp.exp(s_c - mx), jnp.exp(s_p - mx)
                    den = jnp.sum(p_c + p_p, axis=1, keepdims=True)
                    oh = _dot(p_c.astype(BF), vc) + _dot(p_p.astype(BF), vp)
                else:
                    mx = jnp.max(s_c, axis=1, keepdims=True)
                    p_c = jnp.exp(s_c - mx)
                    den = jnp.sum(p_c, axis=1, keepdims=True)
                    oh = _dot(p_c.astype(BF), vc)
                on = oh / den
                lsev = jnp.broadcast_to(mx + jnp.log(den), (2 * BLK, BLK))
                og_ref[cur, :] = on[:BLK] * masks[0] + on[BLK:] * masks[1]
                lg_ref[0, cur, :] = lsev[:BLK]
                lg_ref[1, cur, :] = lsev[BLK:]

            def pair(j, carry):
                finish(2 * j, 0)
                scores(2 * j + 1, 1)
                finish(2 * j + 1, 1)
                scores(jnp.minimum(2 * j + 2, NBLK - 1), 0)
                return carry

            scores(0, 0)
            lax.fori_loop(0, NBLK // 2, pair, 0)
            for rows, src in _pieces(dil):
                srows = pl.ds(src, BLK)
                otok_ref[gi, rows, :] = og_ref[srows, :]
                ltok_ref[gi, 0, rows, :] = lg_ref[0, srows, :]
                ltok_ref[gi, 1, rows, :] = lg_ref[1, srows, :]

        for gi in range(3):
            pl.when(g == gi)(functools.partial(group, gi))

        @pl.when(g == 2)
        def _():
            for c in range(T // BLK):
                rows = pl.ds(BLK * c, BLK)
                wts = []
                for hh in range(2):
                    l0, l1, l2 = ltok_ref[0, hh, rows, :], ltok_ref[1, hh, rows, :], ltok_ref[2, hh, rows, :]
                    mx = jnp.maximum(jnp.maximum(l0, l1), l2)
                    e0, e1, e2 = jnp.exp(l0 - mx), jnp.exp(l1 - mx), jnp.exp(l2 - mx)
                    tot = e0 + e1 + e2
                    lse_ref[rows, BLK * hh:BLK * (hh + 1)] = mx + jnp.log(tot)
                    inv = 1.0 / tot
                    wts.append([e0 * inv, e1 * inv, e2 * inv])
                o = sum((wts[0][gi] * masks[0] + wts[1][gi] * masks[1]) * otok_ref[gi, rows, :] for gi in range(3))
                ag = ag_ref[rows, :]
                opre_ref[rows, :] = o
                ob_ref[rows, :] = (o * (ag * _sigmoid(ag))).astype(BF)

    c0 = ATT_COL0 // BLK
    zspec = lambda part: pl.BlockSpec((T, BLK), lambda p, g, part=part: (0, c0 + 12 * part + 4 * g + p))
    outspec = pl.BlockSpec((T, BLK), lambda p, g: (0, p))
    table = pl.BlockSpec((T, BLK), lambda p, g: (0, 0))
    regrouped = pl.BlockSpec((None, T, BLK), lambda p, g: (g, 0, p))
    big = lambda: pltpu.VMEM((T, BLK), F32)
    return pl.pallas_call(
        body, name="attn_fwd", grid=(4, 3),
        in_specs=[zspec(0), zspec(1), zspec(2),
                  pl.BlockSpec((T, BLK), lambda p, g: (0, AG_COL0 // BLK + p)), table, table, table],
        out_specs=[outspec, outspec, pl.BlockSpec((T, 2 * BLK), lambda p, g: (0, p)), regrouped, regrouped, regrouped],
        out_shape=[jax.ShapeDtypeStruct((T, 512), BF), jax.ShapeDtypeStruct((T, 512), F32),
                   jax.ShapeDtypeStruct((T, 8 * BLK), F32)] + [jax.ShapeDtypeStruct((3, T, 512), BF)] * 3,
        scratch_shapes=[pltpu.VMEM((2, 2 * BLK, BLK), F32), big(),
                        pltpu.VMEM((2, T, BLK), F32), pltpu.VMEM((3, T, BLK), F32), pltpu.VMEM((3, 2, T, BLK), F32),
                        pltpu.VMEM((2, 2, 2 * BLK, BLK), F32)],
        compiler_params=_params(("parallel", "arbitrary")),
    )(z, z, z, z, cos, sa, sb)


def _attn_bwd(z, qs, ks, vs, cos, sa, sb, opre, lse, dob):
    def body(qs_ref, ks_ref, vs_ref, ag_ref, cos_ref, sa_ref, sb_ref, o_ref, lse0_ref, lse1_ref, dob_ref,
             dq_ref, dk_ref, dv_ref, dag_ref,
             bias_ref, dtok_ref, qr_ref, kr_ref, vr_ref, dor_ref, lr_ref, dr_ref,
             dqr_ref, dkr_ref, dvr_ref, pd_ref, dotok_ref):
        g = pl.program_id(1)
        masks = _head_masks()

        @pl.when(g == 0)
        def _():
            _window_bias(bias_ref)
            for c in range(T // BLK):
                rows = pl.ds(BLK * c, BLK)
                ag, dob_v, o = ag_ref[rows, :], dob_ref[rows, :], o_ref[rows, :]
                sg = _sigmoid(ag)
                dag_ref[rows, :] = (dob_v * o * (sg * (1.0 + ag * (1.0 - sg)))).astype(BF)
                do = dob_v * (ag * sg)
                dotok_ref[rows, :] = do
                prod = do * o
                for hh, mh in enumerate(masks):
                    dtok_ref[hh, rows, :] = jnp.broadcast_to(jnp.sum(prod * mh, axis=1, keepdims=True), (BLK, BLK))

        def group(gi):
            dil = ATT_GROUPS[gi][1]
            nblk = (T // dil) // BLK
            for rows, dst in _pieces(dil):
                drows = pl.ds(dst, BLK)
                dor_ref[drows, :] = dotok_ref[rows, :]
                for hh, lse_ref in enumerate((lse0_ref, lse1_ref)):
                    lr_ref[hh, drows, :] = lse_ref[rows, :]
                    dr_ref[hh, drows, :] = dtok_ref[hh, rows, :]
            dkr_ref[...] = jnp.zeros_like(dkr_ref)
            dvr_ref[...] = jnp.zeros_like(dvr_ref)

            def probs(bi, slot):
                cur, prev = _blocks(bi)
                q2, do2 = _stack_heads(qs_ref[cur, :], masks), _stack_heads(dor_ref[cur, :], masks)
                lh = jnp.concatenate([lr_ref[0, cur, :], lr_ref[1, cur, :]], axis=0)
                dh = jnp.concatenate([dr_ref[0, cur, :], dr_ref[1, cur, :]], axis=0)
                p_c = jnp.exp(_dot(q2, ks_ref[cur, :], _NT) + bias_ref[0] - lh)
                pd_ref[slot, 0] = p_c.astype(BF)
                pd_ref[slot, 1] = (p_c * (_dot(do2, vs_ref[cur, :], _NT) - dh)).astype(BF)
                if nblk > 1:
                    bias_p = bias_ref[1] + jnp.where((bi % nblk) != 0, 0.0, -jnp.inf)
                    p_p = jnp.exp(_dot(q2, ks_ref[prev, :], _NT) + bias_p - lh)
                    pd_ref[slot, 2] = p_p.astype(BF)
                    pd_ref[slot, 3] = (p_p * (_dot(do2, vs_ref[prev, :], _NT) - dh)).astype(BF)

            def grads(bi, slot):
                cur, prev = _blocks(bi)
                q2, do2 = _stack_heads(qs_ref[cur, :], masks), _stack_heads(dor_ref[cur, :], masks)
                p_c, ds_c = pd_ref[slot, 0], pd_ref[slot, 1]
                dq2 = _dot(ds_c, ks_ref[cur, :])
                dkr_ref[cur, :] += _dot(ds_c, q2, _TN)
                dvr_ref[cur, :] += _dot(p_c, do2, _TN)
                if nblk > 1:
                    p_p, ds_p = pd_ref[slot, 2], pd_ref[slot, 3]
                    dq2 = dq2 + _dot(ds_p, ks_ref[prev, :])
                    dkr_ref[prev, :] += _dot(ds_p, q2, _TN)
                    dvr_ref[prev, :] += _dot(p_p, do2, _TN)
                dqr_ref[cur, :] = dq2[:BLK] * masks[0] + dq2[BLK:] * masks[1]

            def pair(j, carry):
                grads(2 * j, 0)
                probs(2 * j + 1, 1)
                grads(2 * j + 1, 1)
                probs(jnp.minimum(2 * j + 2, NBLK - 1), 0)
                return carry

            probs(0, 0)
            lax.fori_loop(0, NBLK // 2, pair, 0)
            if dil > 1:
                for rows, src in _pieces(dil):
                    srows = pl.ds(src, BLK)
                    qr_ref[rows, :] = dqr_ref[srows, :]
                    kr_ref[rows, :] = dkr_ref[srows, :]
                    vr_ref[rows, :] = dvr_ref[srows, :]
            tq, tk, tv = (qr_ref, kr_ref, vr_ref) if dil > 1 else (dqr_ref, dkr_ref, dvr_ref)
            for c in range(T // BLK):
                rows = pl.ds(BLK * c, BLK)
                cs, sa, sb = cos_ref[rows, :], sa_ref[rows, :], sb_ref[rows, :]
                dq_ref[rows, :] = _rope_t(tq[rows, :] * QK_SCALE, cs, sa, sb).astype(BF)
                dk_ref[rows, :] = _rope_t(tk[rows, :], cs, sa, sb).astype(BF)
                dv_ref[rows, :] = tv[rows, :].astype(BF)

        for gi in range(3):
            pl.when(g == gi)(functools.partial(group, gi))

    regrouped = pl.BlockSpec((None, T, BLK), lambda p, g: (g, 0, p))
    pspec = pl.BlockSpec((T, BLK), lambda p, g: (0, p))
    gspec = pl.BlockSpec((T, BLK), lambda p, g: (0, 4 * g + p))
    table = pl.BlockSpec((T, BLK), lambda p, g: (0, 0))
    big = lambda: pltpu.VMEM((T, BLK), F32)
    two = lambda: pltpu.VMEM((2, T, BLK), F32)
    return pl.pallas_call(
        body, name="attn_bwd", grid=(4, 3),
        in_specs=[regrouped, regrouped, regrouped,
                  pl.BlockSpec((T, BLK), lambda p, g: (0, AG_COL0 // BLK + p)), table, table, table,
                  pspec, pl.BlockSpec((T, BLK), lambda p, g: (0, 2 * p)),
                  pl.BlockSpec((T, BLK), lambda p, g: (0, 2 * p + 1)), pspec],
        out_specs=[gspec, gspec, gspec, pspec],
        out_shape=[jax.ShapeDtypeStruct((T, 1536), BF), jax.ShapeDtypeStruct((T, 1536), BF),
                   jax.ShapeDtypeStruct((T, 1536), BF), jax.ShapeDtypeStruct((T, 512), BF)],
        scratch_shapes=[pltpu.VMEM((2, 2 * BLK, BLK), F32), two(), big(), big(), big(), big(),
                        two(), two(), big(), big(), big(), pltpu.VMEM((2, 4, 2 * BLK, BLK), BF), big()],
        compiler_params=_params(("parallel", "arbitrary")),
    )(qs, ks, vs, z, cos, sa, sb, opre, lse, lse, dob)


def _merge_out_loss(og, ob, z, w_a, w_b, w_out, x, tgt, wf):
    tm = 512

    def body(og_ref, ob_ref, ga_ref, gb_ref, wa_ref, wb_ref, wo_ref, x_ref, t_ref, wf_ref,
             ya_ref, yb_ref, m_ref, dout_ref, loss_ref, gwf_ref):
        @pl.when(pl.program_id(0) == 0)
        def _():
            loss_ref[...] = jnp.zeros_like(loss_ref)
            gwf_ref[...] = jnp.zeros_like(gwf_ref)

        ya, yb = _dot(og_ref[...], wa_ref[...]), _dot(ob_ref[...], wb_ref[...])
        ya_ref[...] = ya
        yb_ref[...] = yb
        m = (_sigmoid(ga_ref[...]) * ya + _sigmoid(gb_ref[...]) * yb).astype(BF)
        m_ref[...] = m
        out = x_ref[...] + _dot(m, wo_ref[...])
        r = lax.rsqrt(jnp.mean(out * out, axis=-1, keepdims=True) + EPS)
        yh = out * r
        wfv = wf_ref[...]
        err = yh * wfv - t_ref[...]
        loss_ref[...] += jnp.sum(err * err, axis=0, keepdims=True) * (0.5 / D)
        dy = err * (1.0 / D)
        gwf_ref[...] += jnp.sum(dy * yh, axis=0, keepdims=True)
        dyh = dy * wfv
        dout_ref[...] = r * (dyh - yh * jnp.mean(dyh * yh, axis=-1, keepdims=True))

    row = pl.BlockSpec((tm, D), lambda i: (i, 0))
    vec = pl.BlockSpec((1, D), lambda i: (0, 0))
    whole = lambda w: pl.BlockSpec(w.shape, lambda i: (0, 0))
    return pl.pallas_call(
        body, name="merge_out_loss", grid=(T // tm,),
        in_specs=[row, pl.BlockSpec((tm, ob.shape[1]), lambda i: (i, 0)),
                  pl.BlockSpec((tm, D), lambda i: (i, GATE_COL0 // D)),
                  pl.BlockSpec((tm, D), lambda i: (i, GATE_COL0 // D + 1)),
                  whole(w_a), whole(w_b), whole(w_out), row, row, vec],
        out_specs=[row, row, row, row, vec, vec],
        out_shape=[jax.ShapeDtypeStruct((T, D), F32), jax.ShapeDtypeStruct((T, D), F32),
                   jax.ShapeDtypeStruct((T, D), BF), jax.ShapeDtypeStruct((T, D), F32),
                   jax.ShapeDtypeStruct((1, D), F32), jax.ShapeDtypeStruct((1, D), F32)],
        compiler_params=_params(("arbitrary",)),
    )(og, ob, z, z, w_a, w_b, w_out, x, tgt, wf)


def _merge_proj_bwd(dout, ya, yb, z, w_a, w_b, w_out):
    tm = 512

    def body(dout_ref, ya_ref, yb_ref, ga_ref, gb_ref, wa_ref, wb_ref, wo_ref,
             dya_ref, dyb_ref, dg_ref, dog_ref, dob_ref):
        dmv = _dot(dout_ref[...].astype(BF), wo_ref[...], _NT)
        sa, sb = _sigmoid(ga_ref[...]), _sigmoid(gb_ref[...])
        dya, dyb = (sa * dmv).astype(BF), (sb * dmv).astype(BF)
        dya_ref[...] = dya
        dyb_ref[...] = dyb
        dg_ref[:, :D] = (dmv * ya_ref[...] * sa * (1.0 - sa)).astype(BF)
        dg_ref[:, D:] = (dmv * yb_ref[...] * sb * (1.0 - sb)).astype(BF)
        dog_ref[...] = _dot(dya, wa_ref[...], _NT)
        dob_ref[...] = _dot(dyb, wb_ref[...], _NT)

    row = pl.BlockSpec((tm, D), lambda i: (i, 0))
    whole = lambda w: pl.BlockSpec(w.shape, lambda i: (0, 0))
    nb = w_b.shape[0]
    return pl.pallas_call(
        body, name="merge_proj_bwd", grid=(T // tm,),
        in_specs=[row, row, row, pl.BlockSpec((tm, D), lambda i: (i, GATE_COL0 // D)),
                  pl.BlockSpec((tm, D), lambda i: (i, GATE_COL0 // D + 1)), whole(w_a), whole(w_b), whole(w_out)],
        out_specs=[row, row, pl.BlockSpec((tm, 2 * D), lambda i: (i, 0)), row,
                   pl.BlockSpec((tm, nb), lambda i: (i, 0))],
        out_shape=[jax.ShapeDtypeStruct((T, D), BF), jax.ShapeDtypeStruct((T, D), BF),
                   jax.ShapeDtypeStruct((T, 2 * D), BF), jax.ShapeDtypeStruct((T, D), F32),
                   jax.ShapeDtypeStruct((T, nb), F32)],
        compiler_params=_params(("parallel",)),
    )(dout, ya, yb, z, z, w_a, w_b, w_out)


def _rope_inv_freq():
    inv = ROPE_THETA ** (-jnp.arange(0, 64, 2, dtype=F32) / 64)
    return jnp.tile(inv, 4).reshape(1, BLK)


def _local_step(x, pos, norm_w, lbl, hnw, wf, tgt, w_in, w_a, w_b, w_out, shard_shapes=()):
    invf = _rope_inv_freq()
    if shard_shapes:
        blk = jnp.reshape(2 * lax.axis_index("x") + lax.axis_index("y"), (1,)).astype(jnp.int32)
        h, cos, sa, sb, z_own, (w_near,) = _norm_and_rope_tables(
            x, norm_w, pos, invf, side=_gather_near_side(w_in, WEIGHT_AXES[0]), own=(w_in, blk))
        near = jnp.concatenate([blk ^ 2, blk ^ 1])
        z, (w_diag,) = _z_blocks(h, w_near, z_own, jnp.concatenate([near, near]), 2, name="z_proj_near",
                                 side=_gather_diag_side(w_near, w_in.shape, WEIGHT_AXES[0]))
        z, w_in, _ = _z_blocks(h, w_diag, z, jnp.concatenate([blk ^ 3, jnp.zeros_like(blk)]), 1, name="z_proj_diag",
                               fill=w_near, side=None)
        oraw, og, shist, (w_a, w_b, w_out) = _hgrn_fwd(
            z, lbl, hnw, side=_gather_side([w_a, w_b, w_out], WEIGHT_AXES[1:]))
    else:
        h, cos, sa, sb, _, _ = _norm_and_rope_tables(x, norm_w, pos, invf)
        z = _matmul(h, w_in, tm=T, tn=512, name="z_proj")
        oraw, og, shist, _ = _hgrn_fwd(z, lbl, hnw)
    ob, opre, lse, qs, ks, vs = _attn_fwd(z, cos, sa, sb)
    ya, yb, merged, dout, loss_vec, g_wf = _merge_out_loss(og, ob, z, w_a, w_b, w_out, x, tgt, wf)

    dya, dyb, dgates, dog, dob = _merge_proj_bwd(dout, ya, yb, z, w_a, w_b, w_out)
    g_wout = _matmul(merged, dout, ta=True, out_dtype=BF, tm=512, tn=1024, name="g_wout")
    g_wa = _matmul(og, dya, ta=True, out_dtype=BF, tm=512, tn=1024, name="g_wa")
    g_wb = _matmul(ob, dyb, ta=True, out_dtype=BF, tm=512, tn=1024, name="g_wb")
    small = [g_wa, g_wb, g_wout]
    side_s = side_w = None
    if shard_shapes:
        p3_s = _rs_partials(small, shard_shapes[1:], WEIGHT_AXES[1:], "small")
        side_s = _chip_exchange_side(p3_s, shard_shapes[1:], WEIGHT_AXES[1:])
    dz_h, dlb, g_hnw, land_s = _hgrn_bwd(z, lbl, hnw, oraw, dog, shist, side=side_s)
    dq, dk, dv, dag = _attn_bwd(z, qs, ks, vs, cos, sa, sb, opre, lse, dob)
    dz_parts = [dz_h, dq, dk, dv, dag, dgates]
    if shard_shapes:
        c = lax.axis_index("c")
        half = lambda i: jnp.reshape(i, (1,)).astype(jnp.int32)
        g_send = _grad_w_in_half(h, dz_parts, half(1 - c))
        g_keep, (g_sib,) = _grad_w_in_half(h, dz_parts, half(c), side=_sibling_send_side(g_send))
        p3_w = [_add_bf16(g_keep, g_sib, "pair_sum_w_in").reshape(1, D // 2, NIN)]
        side_w = _chip_exchange_side(p3_w, shard_shapes[:1], WEIGHT_AXES[:1])
    else:
        g_big = [_grad_w_in(h, dz_parts)] + small
    gx, g_nw, land_w = _grad_x(dz_parts, w_in, x, dout, norm_w, side=side_w)
    if shard_shapes:
        g_big = _rs_finish(p3_w + p3_s, list(land_w) + list(land_s), shard_shapes, WEIGHT_AXES)
    return dict(loss_vec=loss_vec, gx=gx, g_nw=g_nw, dlb=dlb, g_hnw=g_hnw, g_wf=g_wf,
                g_win=g_big[0], g_wa=g_big[1], g_wb=g_big[2], g_wout=g_big[3])


MESH = pl.DeviceIdType.MESH
HBM = pl.BlockSpec(memory_space=pl.ANY)
WEIGHT_AXES = (1, 0, 1, 0)


def _place():
    x, y, c = lax.axis_index("x"), lax.axis_index("y"), lax.axis_index("c")
    chips = [(1 - x, y), (x, 1 - y), (1 - x, 1 - y)]
    return x, y, c, chips


def _block_half(ref, shard_shape, axis, j, half):
    r, c = shard_shape
    hr = r // 2
    if axis == 0:
        return ref.at[pl.ds(pl.multiple_of(j * r + half * hr, 16), hr), :]
    return ref.at[pl.ds(pl.multiple_of(half * hr, 16), hr), pl.ds(pl.multiple_of(j * c, 128), c)]


class _Side:
    def __init__(self, arrays, out_shapes, sems, first, last, mid=None):
        self.arrays, self.out_shapes, self.sems, self.first, self.last = arrays, out_shapes, sems, first, last
        self.mid = mid


def _gather_side(shards, axes):
    n = len(shards)
    shapes = [s.shape for s in shards]

    def copies(ins, outs, sems):
        send1, recv1, send2, recv2, send0, recv0 = sems
        x, y, c, chips = _place()
        me = 2 * x + y
        sib = (x, y, 1 - c)
        near = ((1 - c) * (1 - x) + c * x, (1 - c) * y + c * (1 - y))
        far = ((1 - c) * x + c * (1 - x), (1 - c) * (1 - y) + c * y)
        out = []
        for a in range(n):
            r, cc = shapes[a]
            mine = (outs[a].at[pl.ds(pl.multiple_of(me * r, 16), r), :] if axes[a] == 0
                    else outs[a].at[:, pl.ds(pl.multiple_of(me * cc, 128), cc)])
            own = pltpu.make_async_remote_copy(
                src_ref=ins[a], dst_ref=mine, send_sem=send0.at[a], recv_sem=recv0.at[a],
                device_id=sib, device_id_type=MESH)
            src = ins[a].at[pl.ds(pl.multiple_of(c * (r // 2), 16), r // 2), :]
            sends = [pltpu.make_async_remote_copy(
                src_ref=src, dst_ref=_block_half(outs[a], shapes[a], axes[a], me, c),
                send_sem=send1.at[a, k], recv_sem=recv1.at[a, k], device_id=(*chips[k], c), device_id_type=MESH)
                for k in range(2)]

            def region(chip, half):
                return _block_half(outs[a], shapes[a], axes[a], 2 * chip[0] + chip[1], half)

            def arrival(chip, k):
                reg = region(chip, c)
                return pltpu.make_async_remote_copy(
                    src_ref=reg, dst_ref=reg, send_sem=send1.at[a, k], recv_sem=recv1.at[a, k],
                    device_id=(*chip, c), device_id_type=MESH)

            def to_sibling(chip, k):
                reg = region(chip, c)
                return pltpu.make_async_remote_copy(
                    src_ref=reg, dst_ref=reg, send_sem=send2.at[a, k], recv_sem=recv2.at[a, k],
                    device_id=sib, device_id_type=MESH)

            def from_sibling(chip, k):
                reg = region(chip, 1 - c)
                return pltpu.make_async_remote_copy(
                    src_ref=reg, dst_ref=reg, send_sem=send2.at[a, k], recv_sem=recv2.at[a, k],
                    device_id=sib, device_id_type=MESH)

            relay = pltpu.make_async_remote_copy(
                src_ref=region(near, c), dst_ref=region(near, c), send_sem=send1.at[a, 2], recv_sem=recv1.at[a, 2],
                device_id=(*far, c), device_id_type=MESH)
            hops = [(arrival(near, c), to_sibling(near, c)), (arrival(far, 1 - c), to_sibling(far, 1 - c)),
                    (arrival(chips[2], 2), to_sibling(chips[2], 2))]
            back = [from_sibling(chips[k], k) for k in range(3)]
            out.append((own, sends, relay, hops, back))
        return out

    def first(ins, outs, sems):
        for own, sends, _, _, _ in copies(ins, outs, sems):
            own.start()
            for cp in sends:
                cp.start()

    def mid(ins, outs, sems):
        per_array = copies(ins, outs, sems)
        for step in range(2):
            for _, _, relay, hops, _ in per_array:
                arrived, onward = hops[step]
                arrived.wait_recv()
                if step == 0:
                    relay.start()
                onward.start()

    def last(ins, outs, sems):
        per_array = copies(ins, outs, sems)
        for _, _, _, hops, _ in per_array:
            arrived, onward = hops[2]
            arrived.wait_recv()
            onward.start()
        for own, sends, relay, hops, back in per_array:
            for cp in back:
                cp.wait_recv()
            for cp in sends + [relay] + [onward for _, onward in hops]:
                cp.wait_send()
            own.wait()

    full = [(4 * r, c) if ax == 0 else (r, 4 * c) for (r, c), ax in zip(shapes, axes)]
    sems = [pltpu.SemaphoreType.DMA((n, 3)), pltpu.SemaphoreType.DMA((n, 3)),
            pltpu.SemaphoreType.DMA((n, 3)), pltpu.SemaphoreType.DMA((n, 3)),
            pltpu.SemaphoreType.DMA((n,)), pltpu.SemaphoreType.DMA((n,))]
    return _Side(list(shards), [jax.ShapeDtypeStruct(f, BF) for f in full], sems, first, last, mid)


def _gather_near_side(shard, axis):
    shape = shard.shape
    r, cc = shape

    def copies(ins, outs, sems):
        send1, recv1, send2, recv2, send0, recv0 = sems
        x, y, c, chips = _place()
        me = 2 * x + y
        sib = (x, y, 1 - c)
        mine = (outs[0].at[pl.ds(pl.multiple_of(me * r, 16), r), :] if axis == 0
                else outs[0].at[:, pl.ds(pl.multiple_of(me * cc, 128), cc)])
        own = pltpu.make_async_remote_copy(
            src_ref=ins[0], dst_ref=mine, send_sem=send0.at[0], recv_sem=recv0.at[0],
            device_id=sib, device_id_type=MESH)
        src = ins[0].at[pl.ds(pl.multiple_of(c * (r // 2), 16), r // 2), :]

        def region(k, half):
            return _block_half(outs[0], shape, axis, 2 * chips[k][0] + chips[k][1], half)

        def moves(k):
            return [pltpu.make_async_remote_copy(
                        src_ref=s, dst_ref=d, send_sem=ss.at[k], recv_sem=rs.at[k], device_id=dev,
                        device_id_type=MESH)
                    for s, d, ss, rs, dev in (
                        (src, _block_half(outs[0], shape, axis, me, c), send1, recv1, (*chips[k], c)),
                        (region(k, c), region(k, c), send1, recv1, (*chips[k], c)),
                        (region(k, c), region(k, c), send2, recv2, sib),
                        (region(k, 1 - c), region(k, 1 - c), send2, recv2, sib))]

        return own, [moves(k) for k in range(2)]

    def first(ins, outs, sems):
        own, per_chip = copies(ins, outs, sems)
        own.start()
        for send, _, _, _ in per_chip:
            send.start()

    def last(ins, outs, sems):
        own, per_chip = copies(ins, outs, sems)
        for _, arrived, onward, _ in per_chip:
            arrived.wait_recv()
            onward.start()
        for send, _, onward, back in per_chip:
            back.wait_recv()
            send.wait_send()
            onward.wait_send()
        own.wait()

    full = (4 * r, cc) if axis == 0 else (r, 4 * cc)
    sems = [pltpu.SemaphoreType.DMA((2,))] * 4 + [pltpu.SemaphoreType.DMA((1,))] * 2
    return _Side([shard], [jax.ShapeDtypeStruct(full, BF)], sems, first, last)


def _gather_diag_side(gathered, shape, axis):
    r, cc = shape

    def copies(ins, outs, sems):
        send1, recv1, send2, recv2 = sems
        x, y, c, _ = _place()
        sib = (x, y, 1 - c)
        near = ((1 - c) * (1 - x) + c * x, (1 - c) * y + c * (1 - y))
        far = ((1 - c) * x + c * (1 - x), (1 - c) * (1 - y) + c * y)

        def half(i):
            return outs[0].at[pl.ds(pl.multiple_of(i * (r // 2), 16), r // 2), :]

        def move(s, d, ss, rs, dev):
            return pltpu.make_async_remote_copy(
                src_ref=s, dst_ref=d, send_sem=ss.at[0], recv_sem=rs.at[0], device_id=dev, device_id_type=MESH)

        relay = move(_block_half(ins[0], shape, axis, 2 * near[0] + near[1], c), half(c), send1, recv1, (*far, c))
        arrived = move(half(c), half(c), send1, recv1, (*far, c))
        onward = move(half(c), half(c), send2, recv2, sib)
        back = move(half(1 - c), half(1 - c), send2, recv2, sib)
        return relay, arrived, onward, back

    def first(ins, outs, sems):
        copies(ins, outs, sems)[0].start()

    def last(ins, outs, sems):
        relay, arrived, onward, back = copies(ins, outs, sems)
        arrived.wait_recv()
        onward.start()
        back.wait_recv()
        relay.wait_send()
        onward.wait_send()

    return _Side([gathered], [jax.ShapeDtypeStruct(shape, BF)], [pltpu.SemaphoreType.DMA((1,))] * 4, first, last)


def _as3d(g, shard_shape, axis):
    r, c = shard_shape
    return g.reshape(4, r, c) if axis == 0 else g.reshape(1, r, 4 * c)


def _half_rows(ref3, hr, half):
    return ref3.at[:, pl.ds(pl.multiple_of(half * hr, 16), hr), :]


def _rs_pair_exchange(g3s, name):
    n = len(g3s)

    def body(*refs):
        ins, outs = refs[:n], refs[n:2 * n]
        send, recv = refs[2 * n:]
        x, y, c, _ = _place()
        cps = []
        for a in range(n):
            hr = g3s[a].shape[1] // 2
            cp = pltpu.make_async_remote_copy(
                src_ref=_half_rows(ins[a], hr, 1 - c), dst_ref=outs[a],
                send_sem=send.at[a], recv_sem=recv.at[a], device_id=(x, y, 1 - c), device_id_type=MESH)
            cp.start()
            cps.append(cp)
        for cp in cps:
            cp.wait()

    return pl.pallas_call(
        body, name=name,
        in_specs=[HBM] * n, out_specs=[HBM] * n,
        out_shape=[jax.ShapeDtypeStruct((g.shape[0], g.shape[1] // 2, g.shape[2]), BF) for g in g3s],
        scratch_shapes=[pltpu.SemaphoreType.DMA((n,)), pltpu.SemaphoreType.DMA((n,))],
    )(*g3s)


def _pair_sum(g3, land, cidx, name):
    nb, r, w = g3.shape
    hr = r // 2
    tr = 64

    def body(c_ref, g_ref, l_ref, o_ref):
        o_ref[...] = (g_ref[...].astype(F32) + l_ref[...].astype(F32)).astype(BF)

    blk = (nb, tr, w)
    return pl.pallas_call(
        body, name=name,
        grid_spec=pltpu.PrefetchScalarGridSpec(
            num_scalar_prefetch=1, grid=(hr // tr,),
            in_specs=[pl.BlockSpec(blk, lambda i, c: (0, c[0] * (hr // tr) + i, 0)),
                      pl.BlockSpec(blk, lambda i, c: (0, i, 0))],
            out_specs=pl.BlockSpec(blk, lambda i, c: (0, i, 0))),
        out_shape=jax.ShapeDtypeStruct((nb, hr, w), BF),
        compiler_params=_params(("parallel",)),
    )(cidx, g3, land)


def _chip_exchange_side(p3s, shapes, axes):
    n = len(p3s)

    def copies(ins, outs, sems):
        send, recv = sems
        x, y, c, chips = _place()
        cps = []
        for a in range(n):
            r, cc = shapes[a]
            for k, (px, py) in enumerate(chips):
                j = 2 * px + py
                src = ins[a].at[j] if axes[a] == 0 else ins[a].at[0, :, pl.ds(pl.multiple_of(j * cc, 128), cc)]
                cps.append(pltpu.make_async_remote_copy(
                    src_ref=src, dst_ref=outs[a].at[k], send_sem=send.at[a, k], recv_sem=recv.at[a, k],
                    device_id=(px, py, c), device_id_type=MESH))
        return cps

    def first(ins, outs, sems):
        for cp in copies(ins, outs, sems):
            cp.start()

    def last(ins, outs, sems):
        for cp in copies(ins, outs, sems):
            cp.wait()

    return _Side(list(p3s), [jax.ShapeDtypeStruct((3, r // 2, c), BF) for r, c in shapes],
                 [pltpu.SemaphoreType.DMA((n, 3)), pltpu.SemaphoreType.DMA((n, 3))], first, last)


def _chip_sum(p3, land, shard_shape, axis, idx, name):
    r, c = shard_shape
    hr = r // 2
    tr = 64
    nt = hr // tr

    def body(idx_ref, p_ref, l_ref, o_ref):
        acc = p_ref[...].astype(F32)
        for k in range(3):
            acc = acc + l_ref[k].astype(F32)
        o_ref[...] = acc

    own = (pl.BlockSpec((None, tr, c), lambda i, idx: (idx[0], i, 0)) if axis == 0
           else pl.BlockSpec((None, tr, c), lambda i, idx: (0, i, idx[0])))
    return pl.pallas_call(
        body, name=name,
        grid_spec=pltpu.PrefetchScalarGridSpec(
            num_scalar_prefetch=1, grid=(nt,),
            in_specs=[own, pl.BlockSpec((3, tr, c), lambda i, idx: (0, i, 0))],
            out_specs=pl.BlockSpec((tr, c), lambda i, idx: (idx[1] * nt + i, 0))),
        out_shape=jax.ShapeDtypeStruct((r, c), F32),
        compiler_params=_params(("parallel",)),
    )(idx, p3, land)


def _rs_pair_gather(fulls):
    n = len(fulls)

    def body(*refs):
        ins, outs = refs[:n], refs[n:2 * n]
        send, recv = refs[2 * n:]
        x, y, c, _ = _place()
        cps = []
        for a in range(n):
            hr = fulls[a].shape[0] // 2
            rows = pl.ds(pl.multiple_of(c * hr, 8), hr)
            cp = pltpu.make_async_remote_copy(
                src_ref=ins[a].at[rows, :], dst_ref=outs[a].at[rows, :], send_sem=send.at[a], recv_sem=recv.at[a],
                device_id=(x, y, 1 - c), device_id_type=MESH)
            cp.start()
            cps.append(cp)
        for a, cp in enumerate(cps):
            cp.wait_send()
            hr = fulls[a].shape[0] // 2
            other = pl.ds(pl.multiple_of((1 - c) * hr, 8), hr)
            pltpu.make_async_remote_copy(
                src_ref=ins[a].at[other, :], dst_ref=outs[a].at[other, :], send_sem=send.at[a], recv_sem=recv.at[a],
                device_id=(x, y, 1 - c), device_id_type=MESH).wait_recv()

    return pl.pallas_call(
        body, name="grads_pair_gather",
        in_specs=[HBM] * n, out_specs=[HBM] * n,
        out_shape=[jax.ShapeDtypeStruct(f.shape, F32) for f in fulls],
        input_output_aliases={a: a for a in range(n)},
        scratch_shapes=[pltpu.SemaphoreType.DMA((n,)), pltpu.SemaphoreType.DMA((n,))],
    )(*fulls)


def _sibling_send_side(arr):
    def copy(ins, outs, sems):
        x, y, c, _ = _place()
        return pltpu.make_async_remote_copy(
            src_ref=ins[0], dst_ref=outs[0], send_sem=sems[0].at[0], recv_sem=sems[1].at[0],
            device_id=(x, y, 1 - c), device_id_type=MESH)

    return _Side([arr], [jax.ShapeDtypeStruct(arr.shape, arr.dtype)],
                 [pltpu.SemaphoreType.DMA((1,)), pltpu.SemaphoreType.DMA((1,))],
                 lambda ins, outs, sems: copy(ins, outs, sems).start(),
                 lambda ins, outs, sems: copy(ins, outs, sems).wait())


def _add_bf16(a, b, name):
    r, c = a.shape
    tr = 64

    def body(a_ref, b_ref, o_ref):
        o_ref[...] = (a_ref[...].astype(F32) + b_ref[...].astype(F32)).astype(BF)

    blk = pl.BlockSpec((tr, c), lambda i: (i, 0))
    return pl.pallas_call(
        body, name=name, grid=(r // tr,), in_specs=[blk, blk], out_specs=blk,
        out_shape=jax.ShapeDtypeStruct((r, c), BF), compiler_params=_params(("parallel",)),
    )(a, b)


def _rs_partials(grads, shapes, axes, tag):
    cidx = jnp.reshape(lax.axis_index("c"), (1,)).astype(jnp.int32)
    g3s = [_as3d(g, s, ax) for g, s, ax in zip(grads, shapes, axes)]
    lands = _rs_pair_exchange(g3s, f"grads_pair_exchange_{tag}")
    return [_pair_sum(g3, l, cidx, f"pair_sum_{tag}_{a}") for a, (g3, l) in enumerate(zip(g3s, lands))]


def _rs_finish(p3s, landed, shapes, axes):
    x, y, c = lax.axis_index("x"), lax.axis_index("y"), lax.axis_index("c")
    idx = jnp.stack([2 * x + y, c]).astype(jnp.int32)
    fulls = [_chip_sum(p3, l2, s, ax, idx, f"chip_sum_{a}")
             for a, (p3, l2, s, ax) in enumerate(zip(p3s, landed, shapes, axes))]
    return _rs_pair_gather(fulls)


NSMALL = 8


def _small_all_reduce(g_nw, dlb, g_hnw, g_wf, loss_vec):
    def body(nw_ref, lb_ref, hn_ref, wf_ref, ls_ref, out_ref, pack_ref, buf_ref, send, recv):
        x, y, c = lax.axis_index("x"), lax.axis_index("y"), lax.axis_index("c")
        me = 4 * x + 2 * y + c
        pack_ref[...] = jnp.zeros_like(pack_ref)
        pack_ref[0:1, :] = nw_ref[...]
        pack_ref[1:2, :] = lb_ref[...]
        pack_ref[2:3, 0:HK] = hn_ref[...]
        pack_ref[3:4, :] = wf_ref[...]
        pack_ref[4:5, :] = ls_ref[...]
        buf_ref[me] = pack_ref[...]
        cps = []
        for d in range(1, 8):
            dx, dy, dc = d >> 2, (d >> 1) & 1, d & 1
            peer = (1 - x if dx else x, 1 - y if dy else y, 1 - c if dc else c)
            cp = pltpu.make_async_remote_copy(
                src_ref=pack_ref, dst_ref=buf_ref.at[me], send_sem=send.at[d - 1], recv_sem=recv.at[d - 1],
                device_id=peer, device_id_type=MESH)
            cp.start()
            cps.append(cp)
        for d in range(1, 8):
            dx, dy, dc = d >> 2, (d >> 1) & 1, d & 1
            src = 4 * (1 - x if dx else x) + 2 * (1 - y if dy else y) + (1 - c if dc else c)
            pltpu.make_async_remote_copy(
                src_ref=pack_ref, dst_ref=buf_ref.at[src], send_sem=send.at[d - 1], recv_sem=recv.at[d - 1],
                device_id=(x, y, c), device_id_type=MESH).wait_recv()
        for cp in cps:
            cp.wait_send()
        acc = buf_ref[0]
        for i in range(1, 8):
            acc = acc + buf_ref[i]
        out_ref[...] = acc

    vm = pl.BlockSpec(memory_space=pltpu.VMEM)
    return pl.pallas_call(
        body, name="small_all_reduce",
        in_specs=[vm] * 5, out_specs=vm,
        out_shape=jax.ShapeDtypeStruct((NSMALL, D), F32),
        scratch_shapes=[pltpu.VMEM((NSMALL, D), F32), pltpu.VMEM((8, NSMALL, D), F32),
                        pltpu.SemaphoreType.DMA((7,)), pltpu.SemaphoreType.DMA((7,))],
    )(g_nw, dlb, g_hnw, g_wf, loss_vec)


def _adamw_math(w, g, m, v):
    m = B1 * m + (1.0 - B1) * g
    v = B2 * v + (1.0 - B2) * (g * g)
    m_hat = m / (1.0 - B1 ** STEP)
    v_hat = v / (1.0 - B2 ** STEP)
    return -LR * (m_hat / (jnp.sqrt(v_hat) + ADAM_EPS) + WD * w), m, v


def _adamw(w, g, m, v, name):
    r, c = w.shape
    tr = 64

    def body(w_ref, g_ref, m_ref, v_ref, d_ref, nm_ref, nv_ref, go_ref):
        g = g_ref[...]
        d_ref[...], nm_ref[...], nv_ref[...] = _adamw_math(w_ref[...], g, m_ref[...], v_ref[...])
        go_ref[...] = g

    blk = pl.BlockSpec((tr, c), lambda i: (i, 0))
    return pl.pallas_call(
        body, name=name, grid=(r // tr,), in_specs=[blk] * 4, out_specs=[blk] * 4,
        out_shape=[jax.ShapeDtypeStruct((r, c), F32)] * 4,
        compiler_params=_params(("parallel",)),
    )(w, g, m, v)


def _adamw_whole(groups, name):
    n = len(groups)

    def body(*refs):
        ins, outs = refs[:4 * n], refs[4 * n:]
        for a in range(n):
            w_ref, g_ref, m_ref, v_ref = ins[4 * a:4 * a + 4]
            g = g_ref[...]
            outs[4 * a][...], outs[4 * a + 1][...], outs[4 * a + 2][...] = _adamw_math(
                w_ref[...], g, m_ref[...], v_ref[...])
            outs[4 * a + 3][...] = g

    vm = pl.BlockSpec(memory_space=pltpu.VMEM)
    out = pl.pallas_call(
        body, name=name, in_specs=[vm] * (4 * n), out_specs=[vm] * (4 * n),
        out_shape=[jax.ShapeDtypeStruct(grp[0].shape, F32) for grp in groups for _ in range(4)],
        compiler_params=_params(),
    )(*[a for grp in groups for a in grp])
    return [out[4 * a:4 * a + 4] for a in range(n)]


def _small_update(red, lbl, params):
    def body(red_ref, *refs):
        ins, outs = refs[:12], refs[12:]
        lb = _lower_bound(ins[3][...])
        dl0 = red_ref[1:2, :] * lb * (1.0 - lb)
        row = lax.broadcasted_iota(jnp.int32, (2, D), 0)
        grads = [red_ref[0:1, :], jnp.where(row == 0, dl0, -dl0), red_ref[2:3, 0:HK], red_ref[3:4, :]]
        for i, g in enumerate(grads):
            w, m, v = ins[3 * i][...], ins[3 * i + 1][...], ins[3 * i + 2][...]
            d, nm, nv = _adamw_math(w, g, m, v)
            outs[4 * i][...] = g
            outs[4 * i + 1][...] = d
            outs[4 * i + 2][...] = nm
            outs[4 * i + 3][...] = nv
        outs[16][...] = jnp.sum(red_ref[4:5, :], axis=1, keepdims=True)

    flat = [a for p in params for a in p]
    vm = pl.BlockSpec(memory_space=pltpu.VMEM)
    shapes = [jax.ShapeDtypeStruct(p[0].shape, F32) for p in params for _ in range(4)]
    return pl.pallas_call(
        body, name="small_update",
        in_specs=[vm] * 13, out_specs=[vm] * 17,
        out_shape=shapes + [jax.ShapeDtypeStruct((1, 1), F32)],
    )(red, *flat)


def kernel(x, positions, norm_w, w_in, lb_logits, hgrn_norm_w, w_branch_a, w_branch_b, w_out, final_norm_w, loss_target, m_norm_w, m_w_in, m_lb_logits, m_hgrn_norm_w, m_w_branch_a, m_w_branch_b, m_w_out, m_final_norm_w, v_norm_w, v_w_in, v_lb_logits, v_hgrn_norm_w, v_w_branch_a, v_w_branch_b, v_w_out, v_final_norm_w):
    big_w = [w_in[0], w_branch_a[0], w_branch_b[0], w_out[0]]
    big_m = [m_w_in[0], m_w_branch_a[0], m_w_branch_b[0], m_w_out[0]]
    big_v = [v_w_in[0], v_w_branch_a[0], v_w_branch_b[0], v_w_out[0]]
    shapes = [w.shape for w in big_w]
    wf = final_norm_w.reshape(1, D)

    shards = [w.astype(BF) for w in big_w]
    loc = _local_step(x[0], positions.reshape(T, 1), norm_w, lb_logits, hgrn_norm_w, wf, loss_target[0],
                      *shards, shard_shapes=shapes)
    g_big = [loc["g_win"], loc["g_wa"], loc["g_wb"], loc["g_wout"]]
    red = _small_all_reduce(loc["g_nw"], loc["dlb"], loc["g_hnw"], loc["g_wf"], loc["loss_vec"])

    small = _small_update(red, lb_logits, [
        (norm_w, m_norm_w, v_norm_w), (lb_logits, m_lb_logits, v_lb_logits),
        (hgrn_norm_w, m_hgrn_norm_w, v_hgrn_norm_w),
        (wf, m_final_norm_w.reshape(1, D), v_final_norm_w.reshape(1, D))])
    loss = small[16].reshape(())
    sg, sd, sm, sv = ([small[4 * i + j] for i in range(4)] for j in range(4))
    for lst in (sg, sd, sm, sv):
        lst[3] = lst[3].reshape(D)
    per_w = list(zip(big_w, g_big, big_m, big_v))
    upd = [_adamw(*per_w[0], "adamw_w_in")] + _adamw_whole(per_w[1:], "adamw_branches_out")
    bd, bm, bv, bg = ([u[j][None] for u in upd] for j in range(4))

    def order(s, b):
        return [s[0], b[0], s[1], s[2], b[1], b[2], b[3], s[3]]

    return (loss, loc["gx"][None], *order(sg, bg), *order(sd, bd), *order(sm, bm), *order(sv, bv))
```

```python
import functools

import jax
import jax.numpy as jnp
from jax import lax
from jax.experimental import pallas as pl
from jax.experimental.pallas import tpu as pltpu

T = 2048
D = 1024
NIN = 11264
HEADS = 8
HK = 128
CH = 16
NCH = T // CH
HSTEP = 2
ATT_GROUPS = ((128, 1), (512, 4), (2048, 16))
ATT_COL0 = 4096
AG_COL0 = 8704
GATE_COL0 = 9216
EPS = 1e-6
ROPE_THETA = 10000.0
LR, B1, B2, ADAM_EPS, WD, STEP = 0.001, 0.9, 0.999, 1e-08, 0.01, 10

F32 = jnp.float32
BF = jnp.bfloat16
VMEM_LIMIT = 56 * 1024 * 1024

_NN = (((1,), (0,)), ((), ()))
_NT = (((1,), (1,)), ((), ()))
_TN = (((0,), (0,)), ((), ()))


def _dot(a, b, dims=_NN):
    return lax.dot_general(a, b, dims, preferred_element_type=F32)


def _bdot(a, b, dims=_NN):
    return lax.dot_general(a.astype(BF), b.astype(BF), dims, preferred_element_type=F32)


def _sigmoid(x):
    return jax.nn.sigmoid(x)


def _params(sem=None):
    return pltpu.CompilerParams(dimension_semantics=sem, vmem_limit_bytes=VMEM_LIMIT)


def _matmul(a, b, *, ta=False, tb=False, out_dtype=F32, tm=512, tn=512, tk=None, name, side=None):
    m = a.shape[1] if ta else a.shape[0]
    kdim = a.shape[0] if ta else a.shape[1]
    n = b.shape[0] if tb else b.shape[1]
    tk = tk or kdim
    tm, tn = min(tm, m), min(tn, n)
    nm, nn, nk = m // tm, n // tn, kdim // tk
    dims = (((0 if ta else 1,), (1 if tb else 0,)), ((), ()))
    s_arrays, s_in_specs, s_shapes, s_out_specs, s_sems = _side_io(side)
    na, no = len(s_arrays), len(s_shapes)
    nacc = 1 if nk > 1 else 0

    def body(*refs):
        a_ref, b_ref = refs[:2]
        s_ins, o_ref, s_outs = refs[2:2 + na], refs[2 + na], refs[3 + na:3 + na + no]
        scratch = refs[3 + na + no:]
        s_sem_refs = scratch[nacc:]
        i, j, k = pl.program_id(0), pl.program_id(1), pl.program_id(2)
        if side is not None:
            @pl.when((i == 0) & (j == 0) & (k == 0))
            def _():
                side.first(s_ins, s_outs, s_sem_refs)

        prod = _bdot(a_ref[...], b_ref[...], dims)
        if nk == 1:
            o_ref[...] = prod.astype(out_dtype)
        else:
            acc = scratch[0]

            @pl.when(k == 0)
            def _():
                acc[...] = prod

            @pl.when(k > 0)
            def _():
                acc[...] += prod

            @pl.when(k == nk - 1)
            def _():
                o_ref[...] = acc[...].astype(out_dtype)

        if side is not None:
            @pl.when((i == nm - 1) & (j == nn - 1) & (k == nk - 1))
            def _():
                side.last(s_ins, s_outs, s_sem_refs)

    a_spec = pl.BlockSpec((tk, tm), lambda i, j, k: (k, i)) if ta else pl.BlockSpec((tm, tk), lambda i, j, k: (i, k))
    b_spec = pl.BlockSpec((tn, tk), lambda i, j, k: (j, k)) if tb else pl.BlockSpec((tk, tn), lambda i, j, k: (k, j))
    sem = ("parallel", "parallel", "arbitrary") if side is None else ("arbitrary",) * 3
    out = pl.pallas_call(
        body, name=name, grid=(nm, nn, nk),
        in_specs=[a_spec, b_spec] + s_in_specs,
        out_specs=[pl.BlockSpec((tm, tn), lambda i, j, k: (i, j))] + s_out_specs,
        out_shape=[jax.ShapeDtypeStruct((m, n), out_dtype)] + s_shapes,
        scratch_shapes=([pltpu.VMEM((tm, tn), F32)] if nk > 1 else []) + s_sems,
        compiler_params=_params(sem),
    )(a, b, *s_arrays)
    return out[0] if side is None else (out[0], out[1:])


DZ_TILE = 512


def _part_offsets(parts):
    counts = [p.shape[1] // DZ_TILE for p in parts]
    offs = [sum(counts[:i]) for i in range(len(parts))]
    return counts, offs


def _part_spec(rows, cnt, off, tile_axis):
    def index(*g):
        return (0 if rows is None else g[0], jnp.clip(g[tile_axis] - off, 0, cnt - 1))
    return index


def _grad_w_in(h, parts):
    counts, offs = _part_offsets(parts)
    n = len(parts)

    def body(h_ref, *refs):
        o_ref = refs[n]
        j = pl.program_id(0)
        for p_ref, cnt, off in zip(refs[:n], counts, offs):
            @pl.when((j >= off) & (j < off + cnt))
            def _(p_ref=p_ref):
                o_ref[...] = _bdot(h_ref[...], p_ref[...], _TN).astype(BF)

    return pl.pallas_call(
        body, name="g_win", grid=(sum(counts),),
        in_specs=[pl.BlockSpec((T, D), lambda j: (0, 0))] +
                 [pl.BlockSpec((T, DZ_TILE), _part_spec(None, c, o, 0)) for c, o in zip(counts, offs)],
        out_specs=pl.BlockSpec((D, DZ_TILE), lambda j: (0, j)),
        out_shape=jax.ShapeDtypeStruct((D, NIN), BF),
        compiler_params=_params(("parallel",)),
    )(h, *parts)


def _grad_w_in_half(h, parts, half_idx, side=None):
    counts, offs = _part_offsets(parts)
    n = len(parts)
    nj = sum(counts)
    s_arrays, s_in_specs, s_shapes, s_out_specs, s_sems = _side_io(side)
    na, no = len(s_arrays), len(s_shapes)

    def body(idx_ref, h_ref, *refs):
        s_ins, o_ref, s_outs, s_sem_refs = refs[n:n + na], refs[n + na], refs[n + na + 1:n + na + 1 + no], refs[n + na + 1 + no:]
        j = pl.program_id(0)
        if side is not None:
            @pl.when(j == 0)
            def _():
                side.first(s_ins, s_outs, s_sem_refs)

        for p_ref, cnt, off in zip(refs[:n], counts, offs):
            @pl.when((j >= off) & (j < off + cnt))
            def _(p_ref=p_ref):
                o_ref[...] = _bdot(h_ref[...], p_ref[...], _TN).astype(BF)

        if side is not None:
            @pl.when(j == nj - 1)
            def _():
                side.last(s_ins, s_outs, s_sem_refs)

    def part_spec(cnt, off):
        return pl.BlockSpec((T, DZ_TILE), lambda j, idx: (0, jnp.clip(j - off, 0, cnt - 1)))

    out = pl.pallas_call(
        body, name="g_win_half" if side is None else "g_win_half_carrying",
        grid_spec=pltpu.PrefetchScalarGridSpec(
            num_scalar_prefetch=1, grid=(nj,),
            in_specs=[pl.BlockSpec((T, D // 2), lambda j, idx: (0, idx[0]))] +
                     [part_spec(c, o) for c, o in zip(counts, offs)] + s_in_specs,
            out_specs=[pl.BlockSpec((D // 2, DZ_TILE), lambda j, idx: (0, j))] + s_out_specs,
            scratch_shapes=s_sems),
        out_shape=[jax.ShapeDtypeStruct((D // 2, NIN), BF)] + s_shapes,
        compiler_params=_params(("parallel",) if side is None else ("arbitrary",)),
    )(half_idx, h, *parts, *s_arrays)
    return out[0] if side is None else (out[0], out[1:])


def _side_io(side):
    if side is None:
        return [], [], [], [], []
    return (side.arrays, [HBM] * len(side.arrays), side.out_shapes, [HBM] * len(side.out_shapes), side.sems)


def _grad_x(parts, w_in, x, dout, norm_w, side=None):
    counts, offs = _part_offsets(parts)
    n = len(parts)
    tm = 1024
    nm, nk = T // tm, sum(counts)
    s_arrays, s_in_specs, s_shapes, s_out_specs, s_sems = _side_io(side)
    na, no = len(s_arrays), len(s_shapes)

    def body(*refs):
        w_ref, x_ref, dout_ref, nw_ref = refs[n:n + 4]
        s_ins = refs[n + 4:n + 4 + na]
        gx_ref, gw_ref = refs[n + 4 + na:n + 6 + na]
        s_outs = refs[n + 6 + na:n + 6 + na + no]
        acc = refs[n + 6 + na + no]
        s_sem_refs = refs[n + 7 + na + no:]
        i, k = pl.program_id(0), pl.program_id(1)

        @pl.when((i == 0) & (k == 0))
        def _():
            gw_ref[...] = jnp.zeros_like(gw_ref)
            if side is not None:
                side.first(s_ins, s_outs, s_sem_refs)

        @pl.when(k == 0)
        def _():
            acc[...] = jnp.zeros_like(acc)

        if side is not None and side.mid is not None:
            @pl.when((i == nm - 1) & (k == nk // 5))
            def _():
                side.mid(s_ins, s_outs, s_sem_refs)

        for p_ref, cnt, off in zip(refs[:n], counts, offs):
            @pl.when((k >= off) & (k < off + cnt))
            def _(p_ref=p_ref):
                acc[...] += _bdot(p_ref[...], w_ref[...], _NT)

        @pl.when(k == nk - 1)
        def _():
            gw = jnp.zeros((1, D), F32)
            for c in range(tm // BLK):
                rows = pl.ds(BLK * c, BLK)
                xv, dhv = x_ref[rows, :], acc[rows, :]
                r = lax.rsqrt(jnp.mean(xv * xv, axis=-1, keepdims=True) + EPS)
                nrm = xv * r
                dn = dhv * nw_ref[...]
                gw = gw + jnp.sum(dhv * nrm, axis=0, keepdims=True)
                gx_ref[rows, :] = dout_ref[rows, :] + r * (dn - nrm * jnp.mean(dn * nrm, axis=-1, keepdims=True))
            gw_ref[...] += gw

        if side is not None:
            @pl.when((i == nm - 1) & (k == nk - 1))
            def _():
                side.last(s_ins, s_outs, s_sem_refs)

    row = pl.BlockSpec((tm, D), lambda i, k: (i, 0))
    vec = pl.BlockSpec((1, D), lambda i, k: (0, 0))
    out = pl.pallas_call(
        body, name="grad_x", grid=(nm, nk),
        in_specs=[pl.BlockSpec((tm, DZ_TILE), _part_spec(0, c, o, 1)) for c, o in zip(counts, offs)] +
                 [pl.BlockSpec((D, DZ_TILE), lambda i, k: (0, k)), row, row, vec] + s_in_specs,
        out_specs=[row, vec] + s_out_specs,
        out_shape=[jax.ShapeDtypeStruct((T, D), F32), jax.ShapeDtypeStruct((1, D), F32)] + s_shapes,
        scratch_shapes=[pltpu.VMEM((tm, D), F32)] + s_sems,
        compiler_params=_params(("arbitrary", "arbitrary")),
    )(*parts, w_in, x, dout, norm_w, *s_arrays)
    return out[0], out[1], out[2:]


def _norm_and_rope_tables(x, w, pos, invf, side=None, own=None):
    tm = 256
    nm = T // tm
    s_arrays, s_in_specs, s_shapes, s_out_specs, s_sems = _side_io(side)
    na, no = len(s_arrays), len(s_shapes)
    nz = 0 if own is None else 1
    wsh, blk = own if own is not None else (None, jnp.zeros((1,), jnp.int32))

    def body(blk_ref, *refs):
        x_ref, w_ref, pos_ref, invf_ref = refs[:4]
        s_ins = refs[4 + nz:4 + nz + na]
        h_ref, cos_ref, sa_ref, sb_ref = refs[4 + nz + na:8 + nz + na]
        s_outs = refs[8 + 2 * nz + na:8 + 2 * nz + na + no]
        s_sem_refs = refs[8 + 2 * nz + na + no:]
        if side is not None:
            @pl.when(pl.program_id(0) == 0)
            def _():
                side.first(s_ins, s_outs, s_sem_refs)

        xv = x_ref[...]
        r = lax.rsqrt(jnp.mean(xv * xv, axis=-1, keepdims=True) + EPS)
        h = (xv * r * w_ref[...]).astype(BF)
        h_ref[...] = h
        if own is not None:
            refs[8 + nz + na][...] = _dot(h, refs[4][...])
        first = (lax.broadcasted_iota(jnp.int32, (tm, 128), 1) % 64) < 32
        ang = pos_ref[...].astype(F32) * invf_ref[...]
        s = jnp.sin(ang)
        cos_ref[...] = jnp.cos(ang)
        sa_ref[...] = jnp.where(first, -s, 0.0)
        sb_ref[...] = jnp.where(first, 0.0, s)
        if side is not None:
            @pl.when(pl.program_id(0) == nm - 1)
            def _():
                side.last(s_ins, s_outs, s_sem_refs)

    tab = pl.BlockSpec((tm, 128), lambda i, b: (i, 0))
    own_in = [] if own is None else [pl.BlockSpec(wsh.shape, lambda i, b: (0, 0))]
    own_out = [] if own is None else [pl.BlockSpec((tm, wsh.shape[1]), lambda i, b: (i, b[0]))]
    own_shape = [] if own is None else [jax.ShapeDtypeStruct((T, NIN), F32)]
    out = pl.pallas_call(
        body, name="norm_and_rope_tables",
        grid_spec=pltpu.PrefetchScalarGridSpec(
            num_scalar_prefetch=1, grid=(nm,),
            in_specs=[pl.BlockSpec((tm, D), lambda i, b: (i, 0)), pl.BlockSpec((1, D), lambda i, b: (0, 0)),
                      pl.BlockSpec((tm, 1), lambda i, b: (i, 0)), pl.BlockSpec((1, 128), lambda i, b: (0, 0))]
                     + own_in + s_in_specs,
            out_specs=[pl.BlockSpec((tm, D), lambda i, b: (i, 0)), tab, tab, tab] + own_out + s_out_specs,
            scratch_shapes=s_sems),
        out_shape=[jax.ShapeDtypeStruct((T, D), BF)] + [jax.ShapeDtypeStruct((T, 128), F32)] * 3 + own_shape + s_shapes,
        compiler_params=_params(("parallel",) if side is None else ("arbitrary",)),
    )(blk, x, w, pos, invf, *([] if own is None else [wsh]), *s_arrays)
    return out[0], out[1], out[2], out[3], (out[4] if own is not None else None), out[4 + nz:]


def _z_blocks(h, w, z, idx, nb, side, name, fill=None):
    tm, tn = 1024, NIN // 8
    s_arrays, s_in_specs, s_shapes, s_out_specs, s_sems = _side_io(side)
    na, no = len(s_arrays), len(s_shapes)
    nm, ns = T // tm, 2 * nb
    nf = 0 if fill is None else 1

    def col(first, i, s, b):
        return (0, b[first + s // 2] * 2 + s % 2)

    def body(idx_ref, h_ref, w_ref, zin_ref, *refs):
        s_ins = refs[nf:nf + na]
        o_ref = refs[nf + na]
        s_outs = refs[nf + na + 1 + nf:nf + na + 1 + nf + no]
        s_sem_refs = refs[nf + na + 1 + nf + no + nf:]
        i, s = pl.program_id(0), pl.program_id(1)

        if side is not None:
            @pl.when((i == 0) & (s == 0))
            def _():
                side.first(s_ins, s_outs, s_sem_refs)

        if fill is not None:
            tile = pl.ds(pl.multiple_of((idx_ref[0] * 2 + s) * tn, 128), tn)
            store = pltpu.make_async_copy(w_ref, refs[nf + na + 1].at[:, tile], refs[nf + na + 1 + nf + no].at[0])
            pl.when(i == 0)(store.start)
        o_ref[...] = _dot(h_ref[...], w_ref[...])
        if fill is not None:
            pl.when(i == 0)(store.wait)

        if side is not None:
            @pl.when((i == nm - 1) & (s == ns - 1))
            def _():
                side.last(s_ins, s_outs, s_sem_refs)

    fills = [] if fill is None else [fill]
    out = pl.pallas_call(
        body, name=name,
        grid_spec=pltpu.PrefetchScalarGridSpec(
            num_scalar_prefetch=1, grid=(nm, ns),
            in_specs=[pl.BlockSpec((tm, D), lambda i, s, b: (i, 0)), pl.BlockSpec((D, tn), functools.partial(col, nb)),
                      HBM] + [HBM] * nf + s_in_specs,
            out_specs=[pl.BlockSpec((tm, tn), lambda i, s, b: (i, col(0, i, s, b)[1]))] + [HBM] * nf + s_out_specs,
            scratch_shapes=[pltpu.SemaphoreType.DMA((1,))] * nf + s_sems),
        out_shape=[jax.ShapeDtypeStruct((T, NIN), F32)] + [jax.ShapeDtypeStruct(f.shape, f.dtype) for f in fills]
                  + s_shapes,
        input_output_aliases={3: 0, **({4: 1} if fill is not None else {})},
        compiler_params=_params(("arbitrary", "arbitrary")),
    )(idx, h, w, z, *fills, *s_arrays)
    return (out[0], *out[1:1 + nf], out[1 + nf:])


def _lower_bound(lbl):
    mx = jnp.max(lbl, axis=0, keepdims=True)
    e = jnp.exp(lbl - mx)
    return e[0:1] / jnp.sum(e, axis=0, keepdims=True)


def _cumsum_rows(g, rows):
    b = g
    sh = 1
    while sh < CH:
        b = b + jnp.where(rows >= sh, pltpu.roll(b, sh, axis=0), 0.0)
        sh *= 2
    return b


def _rev_cumsum_rows(g, rows):
    b = g
    sh = 1
    while sh < CH:
        b = b + jnp.where(rows < CH - sh, pltpu.roll(b, CH - sh, axis=0), 0.0)
        sh *= 2
    return b


SUB = CH // 2


def _direct_block(qb, kb, vb, bb, rows8):
    ob = jnp.zeros_like(qb)
    for s in range(SUB):
        e_s = jnp.exp(jnp.where(rows8 >= s, bb - bb[s:s + 1], -jnp.inf))
        ob = ob + jnp.sum(qb * e_s * kb[s:s + 1], axis=1, keepdims=True) * vb[s:s + 1]
    return ob


def _direct_block_bwd(qb, kb, vb, bb, dob, rows8, rowc8):
    dq = dk = dv = db = jnp.zeros_like(qb)
    for s in range(SUB):
        one = (rowc8 == s).astype(F32)
        ks, vs = kb[s:s + 1], vb[s:s + 1]
        e_s = jnp.exp(jnp.where(rows8 >= s, bb - bb[s:s + 1], -jnp.inf))
        qes = qb * e_s
        w = qes * ks
        a = jnp.sum(w, axis=1, keepdims=True)
        da = jnp.sum(dob * vs, axis=1, keepdims=True)
        dv = dv + one * jnp.sum(a * dob, axis=0, keepdims=True)
        dq = dq + da * e_s * ks
        dk = dk + one * jnp.sum(da * qes, axis=0, keepdims=True)
        u = da * w
        db = db + u - one * jnp.sum(u, axis=0, keepdims=True)
    return dq, dk, dv, db


def _cross_factors(q, k, b):
    ref = b[SUB - 1:SUB]
    e_hi, e_lo = jnp.exp(b[SUB:] - ref), jnp.exp(ref - b[:SUB])
    return q[SUB:] * e_hi, k[:SUB] * e_lo, e_hi, e_lo


def _intra_fwd(q, k, v, b, rows8):
    lo = _direct_block(q[:SUB], k[:SUB], v[:SUB], b[:SUB], rows8)
    hi = _direct_block(q[SUB:], k[SUB:], v[SUB:], b[SUB:], rows8)
    qe_hi, ke_lo, _, _ = _cross_factors(q, k, b)
    for s in range(SUB):
        hi = hi + jnp.sum(qe_hi * ke_lo[s:s + 1], axis=1, keepdims=True) * v[s:s + 1]
    return jnp.concatenate([lo, hi], axis=0)


def _intra_bwd(q, k, v, b, do, rows8, rowc8):
    dq_lo, dk_lo, dv_lo, db_lo = _direct_block_bwd(q[:SUB], k[:SUB], v[:SUB], b[:SUB], do[:SUB], rows8, rowc8)
    dq_hi, dk_hi, dv_hi, db_hi = _direct_block_bwd(q[SUB:], k[SUB:], v[SUB:], b[SUB:], do[SUB:], rows8, rowc8)
    qe_hi, ke_lo, e_hi, e_lo = _cross_factors(q, k, b)
    do_hi, v_lo = do[SUB:], v[:SUB]
    dqe = dke = jnp.zeros_like(qe_hi)
    for s in range(SUB):
        one = (rowc8 == s).astype(F32)
        a = jnp.sum(qe_hi * ke_lo[s:s + 1], axis=1, keepdims=True)
        da = jnp.sum(do_hi * v_lo[s:s + 1], axis=1, keepdims=True)
        dv_lo = dv_lo + one * jnp.sum(a * do_hi, axis=0, keepdims=True)
        dqe = dqe + da * ke_lo[s:s + 1]
        dke = dke + one * jnp.sum(da * qe_hi, axis=0, keepdims=True)
    u_hi, u_lo = dqe * qe_hi, dke * ke_lo
    d_ref = jnp.sum(u_lo, axis=0, keepdims=True) - jnp.sum(u_hi, axis=0, keepdims=True)
    db_lo = db_lo - u_lo + (rowc8 == SUB - 1).astype(F32) * d_ref
    cat = lambda lo, hi: jnp.concatenate([lo, hi], axis=0)
    return (cat(dq_lo, dq_hi + dqe * e_hi), cat(dk_lo + dke * e_lo, dk_hi), cat(dv_lo, dv_hi),
            cat(db_lo, db_hi + u_hi))


def _hgrn_fwd(z, lbl, nw, side=None):
    s_arrays, s_in_specs, s_shapes, s_out_specs, s_sems = _side_io(side)
    na, no = len(s_arrays), len(s_shapes)
    nsteps = NCH // HSTEP

    def body(hq_ref, hf_ref, hi_ref, hg_ref, lbl_ref, nw_ref, *refs):
        s_ins, (oraw_ref, og_ref, sh_ref) = refs[:na], refs[na:na + 3]
        s_outs, st_ref, s_sem_refs = refs[na + 3:na + 3 + no], refs[na + 3 + no], refs[na + 4 + no:]

        @pl.when(pl.program_id(0) == 0)
        def _():
            st_ref[...] = jnp.zeros_like(st_ref)
            if side is not None:
                side.first(s_ins, s_outs, s_sem_refs)

        lb_all = _lower_bound(lbl_ref[...])
        rows = lax.broadcasted_iota(jnp.int32, (CH, HK), 0)
        rows8 = lax.broadcasted_iota(jnp.int32, (SUB, HK), 0)
        nwv = nw_ref[...]
        for cc, h in [(cc, h) for cc in range(HSTEP) for h in range(HEADS)]:
            rs = slice(CH * cc, CH * (cc + 1))
            sl = slice(HK * h, HK * (h + 1))
            lb = lb_all[:, sl]
            hq, hf, v, hg = hq_ref[rs, sl], hf_ref[rs, sl], hi_ref[rs, sl], hg_ref[rs, sl]
            q = hq * _sigmoid(hq)
            f = lb + (1.0 - lb) * _sigmoid(hf)
            k = 1.0 - f
            b = _cumsum_rows(jnp.log(f), rows)
            sh_ref[cc, h] = st_ref[h]
            o = _bdot(q * jnp.exp(b), st_ref[h], _NT) + _intra_fwd(q, k, v, b, rows8)
            bl = b[CH - 1:CH]
            st_ref[h] = st_ref[h] * jnp.exp(bl)
            st_ref[h] += _bdot(v, k * jnp.exp(bl - b), _TN)
            oraw_ref[rs, sl] = o
            nrm = o * lax.rsqrt(jnp.mean(o * o, axis=1, keepdims=True) + EPS)
            og_ref[rs, sl] = (nrm * nwv * (hg * _sigmoid(hg))).astype(BF)

        if side is not None:
            @pl.when(pl.program_id(0) == nsteps // 2)
            def _():
                side.mid(s_ins, s_outs, s_sem_refs)

            @pl.when(pl.program_id(0) == nsteps - 1)
            def _():
                side.last(s_ins, s_outs, s_sem_refs)

    zblk = lambda c: pl.BlockSpec((CH * HSTEP, D), lambda i, c=c: (i, c))
    out = pl.pallas_call(
        body, name="hgrn_fwd", grid=(nsteps,),
        in_specs=[zblk(0), zblk(1), zblk(2), zblk(3),
                  pl.BlockSpec((2, D), lambda i: (0, 0)), pl.BlockSpec((1, HK), lambda i: (0, 0))] + s_in_specs,
        out_specs=[zblk(0), zblk(0),
                   pl.BlockSpec((HSTEP, HEADS, HK, HK), lambda i: (i, 0, 0, 0))] + s_out_specs,
        out_shape=[jax.ShapeDtypeStruct((T, D), F32), jax.ShapeDtypeStruct((T, D), BF),
                   jax.ShapeDtypeStruct((NCH, HEADS, HK, HK), F32)] + s_shapes,
        scratch_shapes=[pltpu.VMEM((HEADS, HK, HK), F32)] + s_sems,
        compiler_params=_params(("arbitrary",)),
    )(z, z, z, z, lbl, nw, *s_arrays)
    return out[0], out[1], out[2], out[3:]


def _hgrn_bwd(z, lbl, nw, oraw, dog, shist, side=None):
    hstep = 1
    s_arrays, s_in_specs, s_shapes, s_out_specs, s_sems = _side_io(side)
    na, no = len(s_arrays), len(s_shapes)

    def body(*refs):
        hq_ref, hf_ref, hi_ref, hg_ref, lbl_ref, nw_ref, oraw_ref, dog_ref, sh_ref = refs[:9]
        s_ins = refs[9:9 + na]
        dz_ref, dlb_ref, dnw_ref = refs[9 + na:12 + na]
        s_outs = refs[12 + na:12 + na + no]
        dst_ref = refs[12 + na + no]
        s_sem_refs = refs[13 + na + no:]

        @pl.when(pl.program_id(0) == 0)
        def _():
            dst_ref[...] = jnp.zeros_like(dst_ref)
            dlb_ref[...] = jnp.zeros_like(dlb_ref)
            dnw_ref[...] = jnp.zeros_like(dnw_ref)
            if side is not None:
                side.first(s_ins, s_outs, s_sem_refs)

        lb_all = _lower_bound(lbl_ref[...])
        rows = lax.broadcasted_iota(jnp.int32, (CH, HK), 0)
        rowc = lax.broadcasted_iota(jnp.int32, (CH, 1), 0)
        rows8 = lax.broadcasted_iota(jnp.int32, (SUB, HK), 0)
        rowc8 = lax.broadcasted_iota(jnp.int32, (SUB, 1), 0)
        nwv = nw_ref[...]
        dnw = jnp.zeros((1, HK), F32)
        for cc, h in [(cc, h) for cc in reversed(range(hstep)) for h in range(HEADS)]:
            rs = slice(CH * cc, CH * (cc + 1))
            sl = slice(HK * h, HK * (h + 1))
            lb = lb_all[:, sl]
            hq, hf, v, hg = hq_ref[rs, sl], hf_ref[rs, sl], hi_ref[rs, sl], hg_ref[rs, sl]
            o, dg_out = oraw_ref[rs, sl], dog_ref[rs, sl]
            sg = _sigmoid(hg)
            sil = hg * sg
            r = lax.rsqrt(jnp.mean(o * o, axis=1, keepdims=True) + EPS)
            nrm = o * r
            d_hg = dg_out * (nrm * nwv) * (sg * (1.0 + hg * (1.0 - sg)))
            dn = dg_out * nwv * sil
            dnw = dnw + jnp.sum(dg_out * nrm * sil, axis=0, keepdims=True)
            do = r * (dn - nrm * jnp.mean(dn * nrm, axis=1, keepdims=True))
            sq = _sigmoid(hq)
            q = hq * sq
            sig = _sigmoid(hf)
            f = lb + (1.0 - lb) * sig
            k = 1.0 - f
            b = _cumsum_rows(jnp.log(f), rows)
            eb = jnp.exp(b)
            qe = q * eb
            bl = b[CH - 1:CH]
            ebl = jnp.exp(bl)
            kdec = jnp.exp(bl - b)
            ke = k * kdec
            dqe = _bdot(do, sh_ref[cc, h])
            dq = dqe * eb
            db = dqe * qe
            dke = _bdot(v, dst_ref[h])
            dv = _bdot(ke, dst_ref[h], _NT)
            dk = dke * kdec
            rr = dke * ke
            db = db - rr
            db_last = (jnp.sum(rr, axis=0, keepdims=True)
                       + ebl * jnp.sum(dst_ref[h] * sh_ref[cc, h], axis=0, keepdims=True))
            dst_ref[h] = dst_ref[h] * ebl
            dst_ref[h] += _bdot(do, qe, _TN)
            dq_i, dk_i, dv_i, db_i = _intra_bwd(q, k, v, b, do, rows8, rowc8)
            dq, dk, dv = dq + dq_i, dk + dk_i, dv + dv_i
            db = db + db_i + (rowc == CH - 1).astype(F32) * db_last
            dgl = _rev_cumsum_rows(db, rows)
            df = dgl / f - dk
            dlb_ref[:, sl] += jnp.sum(df * (1.0 - sig), axis=0, keepdims=True)
            dz_ref[rs, sl] = (dq * (sq * (1.0 + hq * (1.0 - sq)))).astype(BF)
            dz_ref[rs, D + HK * h:D + HK * (h + 1)] = (df * (1.0 - lb) * sig * (1.0 - sig)).astype(BF)
            dz_ref[rs, 2 * D + HK * h:2 * D + HK * (h + 1)] = dv.astype(BF)
            dz_ref[rs, 3 * D + HK * h:3 * D + HK * (h + 1)] = d_hg.astype(BF)
        dnw_ref[...] += dnw
        if side is not None:
            @pl.when(pl.program_id(0) == NCH // hstep - 1)
            def _():
                side.last(s_ins, s_outs, s_sem_refs)

    rev = lambda i: NCH // hstep - 1 - i
    zblk = lambda c: pl.BlockSpec((CH * hstep, D), lambda i, c=c: (rev(i), c))
    out = pl.pallas_call(
        body, name="hgrn_bwd", grid=(NCH // hstep,),
        in_specs=[zblk(0), zblk(1), zblk(2), zblk(3),
                  pl.BlockSpec((2, D), lambda i: (0, 0)), pl.BlockSpec((1, HK), lambda i: (0, 0)),
                  zblk(0), zblk(0),
                  pl.BlockSpec((hstep, HEADS, HK, HK), lambda i: (rev(i), 0, 0, 0))] + s_in_specs,
        out_specs=[pl.BlockSpec((CH * hstep, 4 * D), lambda i: (rev(i), 0)),
                   pl.BlockSpec((1, D), lambda i: (0, 0)), pl.BlockSpec((1, HK), lambda i: (0, 0))] + s_out_specs,
        out_shape=[jax.ShapeDtypeStruct((T, 4 * D), BF), jax.ShapeDtypeStruct((1, D), F32),
                   jax.ShapeDtypeStruct((1, HK), F32)] + s_shapes,
        scratch_shapes=[pltpu.VMEM((HEADS, HK, HK), F32)] + s_sems,
        compiler_params=_params(("arbitrary",)),
    )(z, z, z, z, lbl, nw, oraw, dog, shist, *s_arrays)
    return out[0], out[1], out[2], out[3:]


BLK = 128
NBLK = T // BLK
QK_SCALE = 0.125


def _head_masks():
    lane = lax.broadcasted_iota(jnp.int32, (1, BLK), 1)
    return [(lane < 64).astype(F32), (lane >= 64).astype(F32)]


def _pieces(dil):
    m = T // dil
    out = []
    for r in range(dil):
        for j in range(m // BLK):
            start = r + dil * BLK * j
            rows = pl.ds(start, BLK, stride=dil) if dil > 1 else pl.ds(start, BLK)
            out.append((rows, r * m + BLK * j))
    return out


def _rope(x, c, sa, sb):
    return x * c + pltpu.roll(x, 96, axis=1) * sa + pltpu.roll(x, 32, axis=1) * sb


def _rope_t(d, c, sa, sb):
    return d * c + pltpu.roll(d * sa, 32, axis=1) + pltpu.roll(d * sb, 96, axis=1)


def _rope_and_regroup(dil, q_ref, k_ref, v_ref, tables, stage_q, stage_k, qr_ref, kr_ref, vr_ref):
    cos_ref, sa_ref, sb_ref = tables
    to_q, to_k = (qr_ref, kr_ref) if dil == 1 else (stage_q, stage_k)
    for c in range(T // BLK):
        rows = pl.ds(BLK * c, BLK)
        cs, sa, sb = cos_ref[rows, :], sa_ref[rows, :], sb_ref[rows, :]
        to_q[rows, :] = (_rope(q_ref[rows, :], cs, sa, sb) * QK_SCALE).astype(to_q.dtype)
        to_k[rows, :] = _rope(k_ref[rows, :], cs, sa, sb).astype(to_k.dtype)
    for rows, dst in _pieces(dil):
        drows = pl.ds(dst, BLK)
        if dil > 1:
            qr_ref[drows, :] = stage_q[rows, :].astype(qr_ref.dtype)
            kr_ref[drows, :] = stage_k[rows, :].astype(kr_ref.dtype)
        vr_ref[drows, :] = v_ref[rows, :].astype(vr_ref.dtype)


def _window_bias(bias_ref):
    ii = lax.broadcasted_iota(jnp.int32, (2 * BLK, BLK), 0) % BLK
    jj = lax.broadcasted_iota(jnp.int32, (2 * BLK, BLK), 1)
    bias_ref[0] = jnp.where(jj <= ii, 0.0, -jnp.inf)
    bias_ref[1] = jnp.where(jj >= ii, 0.0, -jnp.inf)


def _blocks(bi):
    if isinstance(bi, int):
        return pl.ds(bi * BLK, BLK), pl.ds(max(bi - 1, 0) * BLK, BLK)
    return (pl.ds(pl.multiple_of(bi * BLK, BLK), BLK),
            pl.ds(pl.multiple_of(jnp.maximum(bi - 1, 0) * BLK, BLK), BLK))


def _stack_heads(x, masks):
    return jnp.concatenate([x * masks[0].astype(x.dtype), x * masks[1].astype(x.dtype)], axis=0).astype(BF)


def _attn_fwd(z, cos, sa, sb):
    def body(q_ref, k_ref, v_ref, ag_ref, cos_ref, sa_ref, sb_ref, ob_ref, opre_ref, lse_ref, qr_ref, kr_ref, vr_ref,
             bias_ref, og_ref, lg_ref, otok_ref, ltok_ref, sc_ref):
        g = pl.program_id(1)
        masks = _head_masks()

        @pl.when(g == 0)
        def _():
            _window_bias(bias_ref)

        def group(gi):
            dil = ATT_GROUPS[gi][1]
            nblk = (T // dil) // BLK
            _rope_and_regroup(dil, q_ref, k_ref, v_ref, (cos_ref, sa_ref, sb_ref), lg_ref.at[0], lg_ref.at[1],
                              qr_ref, kr_ref, vr_ref)

            def scores(bi, slot):
                cur, prev = _blocks(bi)
                q2 = _stack_heads(qr_ref[cur, :], masks)
                sc_ref[slot, 0] = _dot(q2, kr_ref[cur, :], _NT) + bias_ref[0]
                if nblk > 1:
                    sc_ref[slot, 1] = (_dot(q2, kr_ref[prev, :], _NT)
                                       + (bias_ref[1] + jnp.where((bi % nblk) != 0, 0.0, -jnp.inf)))

            def finish(bi, slot):
                cur, prev = _blocks(bi)
                s_c, vc = sc_ref[slot, 0], vr_ref[cur, :]
                if nblk > 1:
                    s_p, vp = sc_ref[slot, 1], vr_ref[prev, :]
                    mx = jnp.max(jnp.maximum(s_c, s_p), axis=1, keepdims=True)
                    p_c, p_p = jnp.exp(s_c - mx), jnp.exp(s_p - mx)
                    den = jnp.sum(p_c + p_p, axis=1, keepdims=True)
                    oh = _dot(p_c.astype(BF), vc) + _dot(p_p.astype(BF), vp)
                else:
                    mx = jnp.max(s_c, axis=1, keepdims=True)
                    p_c = jnp.exp(s_c - mx)
                    den = jnp.sum(p_c, axis=1, keepdims=True)
                    oh = _dot(p_c.astype(BF), vc)
                on = oh / den
                lsev = jnp.broadcast_to(mx + jnp.log(den), (2 * BLK, BLK))
                og_ref[cur, :] = on[:BLK] * masks[0] + on[BLK:] * masks[1]
                lg_ref[0, cur, :] = lsev[:BLK]
                lg_ref[1, cur, :] = lsev[BLK:]

            def pair(j, carry):
                finish(2 * j, 0)
                scores(2 * j + 1, 1)
                finish(2 * j + 1, 1)
                scores(jnp.minimum(2 * j + 2, NBLK - 1), 0)
                return carry

            scores(0, 0)
            lax.fori_loop(0, NBLK // 2, pair, 0)
            for rows, src in _pieces(dil):
                srows = pl.ds(src, BLK)
                otok_ref[gi, rows, :] = og_ref[srows, :]
                ltok_ref[gi, 0, rows, :] = lg_ref[0, srows, :]
                ltok_ref[gi, 1, rows, :] = lg_ref[1, srows, :]

        for gi in range(3):
            pl.when(g == gi)(functools.partial(group, gi))

        @pl.when(g == 2)
        def _():
            for c in range(T // BLK):
                rows = pl.ds(BLK * c, BLK)
                wts = []
                for hh in range(2):
                    l0, l1, l2 = ltok_ref[0, hh, rows, :], ltok_ref[1, hh, rows, :], ltok_ref[2, hh, rows, :]
                    mx = jnp.maximum(jnp.maximum(l0, l1), l2)
                    e0, e1, e2 = jnp.exp(l0 - mx), jnp.exp(l1 - mx), jnp.exp(l2 - mx)
                    tot = e0 + e1 + e2
                    lse_ref[rows, BLK * hh:BLK * (hh + 1)] = mx + jnp.log(tot)
                    inv = 1.0 / tot
                    wts.append([e0 * inv, e1 * inv, e2 * inv])
                o = sum((wts[0][gi] * masks[0] + wts[1][gi] * masks[1]) * otok_ref[gi, rows, :] for gi in range(3))
                ag = ag_ref[rows, :]
                opre_ref[rows, :] = o
                ob_ref[rows, :] = (o * (ag * _sigmoid(ag))).astype(BF)

    c0 = ATT_COL0 // BLK
    zspec = lambda part: pl.BlockSpec((T, BLK), lambda p, g, part=part: (0, c0 + 12 * part + 4 * g + p))
    outspec = pl.BlockSpec((T, BLK), lambda p, g: (0, p))
    table = pl.BlockSpec((T, BLK), lambda p, g: (0, 0))
    regrouped = pl.BlockSpec((None, T, BLK), lambda p, g: (g, 0, p))
    big = lambda: pltpu.VMEM((T, BLK), F32)
    return pl.pallas_call(
        body, name="attn_fwd", grid=(4, 3),
        in_specs=[zspec(0), zspec(1), zspec(2),
                  pl.BlockSpec((T, BLK), lambda p, g: (0, AG_COL0 // BLK + p)), table, table, table],
        out_specs=[outspec, outspec, pl.BlockSpec((T, 2 * BLK), lambda p, g: (0, p)), regrouped, regrouped, regrouped],
        out_shape=[jax.ShapeDtypeStruct((T, 512), BF), jax.ShapeDtypeStruct((T, 512), F32),
                   jax.ShapeDtypeStruct((T, 8 * BLK), F32)] + [jax.ShapeDtypeStruct((3, T, 512), BF)] * 3,
        scratch_shapes=[pltpu.VMEM((2, 2 * BLK, BLK), F32), big(),
                        pltpu.VMEM((2, T, BLK), F32), pltpu.VMEM((3, T, BLK), F32), pltpu.VMEM((3, 2, T, BLK), F32),
                        pltpu.VMEM((2, 2, 2 * BLK, BLK), F32)],
        compiler_params=_params(("parallel", "arbitrary")),
    )(z, z, z, z, cos, sa, sb)


def _attn_bwd(z, qs, ks, vs, cos, sa, sb, opre, lse, dob):
    def body(qs_ref, ks_ref, vs_ref, ag_ref, cos_ref, sa_ref, sb_ref, o_ref, lse0_ref, lse1_ref, dob_ref,
             dq_ref, dk_ref, dv_ref, dag_ref,
             bias_ref, dtok_ref, qr_ref, kr_ref, vr_ref, dor_ref, lr_ref, dr_ref,
             dqr_ref, dkr_ref, dvr_ref, pd_ref, dotok_ref):
        g = pl.program_id(1)
        masks = _head_masks()

        @pl.when(g == 0)
        def _():
            _window_bias(bias_ref)
            for c in range(T // BLK):
                rows = pl.ds(BLK * c, BLK)
                ag, dob_v, o = ag_ref[rows, :], dob_ref[rows, :], o_ref[rows, :]
                sg = _sigmoid(ag)
                dag_ref[rows, :] = (dob_v * o * (sg * (1.0 + ag * (1.0 - sg)))).astype(BF)
                do = dob_v * (ag * sg)
                dotok_ref[rows, :] = do
                prod = do * o
                for hh, mh in enumerate(masks):
                    dtok_ref[hh, rows, :] = jnp.broadcast_to(jnp.sum(prod * mh, axis=1, keepdims=True), (BLK, BLK))

        def group(gi):
            dil = ATT_GROUPS[gi][1]
            nblk = (T // dil) // BLK
            for rows, dst in _pieces(dil):
                drows = pl.ds(dst, BLK)
                dor_ref[drows, :] = dotok_ref[rows, :]
                for hh, lse_ref in enumerate((lse0_ref, lse1_ref)):
                    lr_ref[hh, drows, :] = lse_ref[rows, :]
                    dr_ref[hh, drows, :] = dtok_ref[hh, rows, :]
            dkr_ref[...] = jnp.zeros_like(dkr_ref)
            dvr_ref[...] = jnp.zeros_like(dvr_ref)

            def probs(bi, slot):
                cur, prev = _blocks(bi)
                q2, do2 = _stack_heads(qs_ref[cur, :], masks), _stack_heads(dor_ref[cur, :], masks)
                lh = jnp.concatenate([lr_ref[0, cur, :], lr_ref[1, cur, :]], axis=0)
                dh = jnp.concatenate([dr_ref[0, cur, :], dr_ref[1, cur, :]], axis=0)
                p_c = jnp.exp(_dot(q2, ks_ref[cur, :], _NT) + bias_ref[0] - lh)
                pd_ref[slot, 0] = p_c.astype(BF)
                pd_ref[slot, 1] = (p_c * (_dot(do2, vs_ref[cur, :], _NT) - dh)).astype(BF)
                if nblk > 1:
                    bias_p = bias_ref[1] + jnp.where((bi % nblk) != 0, 0.0, -jnp.inf)
                    p_p = jnp.exp(_dot(q2, ks_ref[prev, :], _NT) + bias_p - lh)
                    pd_ref[slot, 2] = p_p.astype(BF)
                    pd_ref[slot, 3] = (p_p * (_dot(do2, vs_ref[prev, :], _NT) - dh)).astype(BF)

            def grads(bi, slot):
                cur, prev = _blocks(bi)
                q2, do2 = _stack_heads(qs_ref[cur, :], masks), _stack_heads(dor_ref[cur, :], masks)
                p_c, ds_c = pd_ref[slot, 0], pd_ref[slot, 1]
                dq2 = _dot(ds_c, ks_ref[cur, :])
                dkr_ref[cur, :] += _dot(ds_c, q2, _TN)
                dvr_ref[cur, :] += _dot(p_c, do2, _TN)
                if nblk > 1:
                    p_p, ds_p = pd_ref[slot, 2], pd_ref[slot, 3]
                    dq2 = dq2 + _dot(ds_p, ks_ref[prev, :])
                    dkr_ref[prev, :] += _dot(ds_p, q2, _TN)
                    dvr_ref[prev, :] += _dot(p_p, do2, _TN)
                dqr_ref[cur, :] = dq2[:BLK] * masks[0] + dq2[BLK:] * masks[1]

            def pair(j, carry):
                grads(2 * j, 0)
                probs(2 * j + 1, 1)
                grads(2 * j + 1, 1)
                probs(jnp.minimum(2 * j + 2, NBLK - 1), 0)
                return carry

            probs(0, 0)
            lax.fori_loop(0, NBLK // 2, pair, 0)
            if dil > 1:
                for rows, src in _pieces(dil):
                    srows = pl.ds(src, BLK)
                    qr_ref[rows, :] = dqr_ref[srows, :]
                    kr_ref[rows, :] = dkr_ref[srows, :]
                    vr_ref[rows, :] = dvr_ref[srows, :]
            tq, tk, tv = (qr_ref, kr_ref, vr_ref) if dil > 1 else (dqr_ref, dkr_ref, dvr_ref)
            for c in range(T // BLK):
                rows = pl.ds(BLK * c, BLK)
                cs, sa, sb = cos_ref[rows, :], sa_ref[rows, :], sb_ref[rows, :]
                dq_ref[rows, :] = _rope_t(tq[rows, :] * QK_SCALE, cs, sa, sb).astype(BF)
                dk_ref[rows, :] = _rope_t(tk[rows, :], cs, sa, sb).astype(BF)
                dv_ref[rows, :] = tv[rows, :].astype(BF)

        for gi in range(3):
            pl.when(g == gi)(functools.partial(group, gi))

    regrouped = pl.BlockSpec((None, T, BLK), lambda p, g: (g, 0, p))
    pspec = pl.BlockSpec((T, BLK), lambda p, g: (0, p))
    gspec = pl.BlockSpec((T, BLK), lambda p, g: (0, 4 * g + p))
    table = pl.BlockSpec((T, BLK), lambda p, g: (0, 0))
    big = lambda: pltpu.VMEM((T, BLK), F32)
    two = lambda: pltpu.VMEM((2, T, BLK), F32)
    return pl.pallas_call(
        body, name="attn_bwd", grid=(4, 3),
        in_specs=[regrouped, regrouped, regrouped,
                  pl.BlockSpec((T, BLK), lambda p, g: (0, AG_COL0 // BLK + p)), table, table, table,
                  pspec, pl.BlockSpec((T, BLK), lambda p, g: (0, 2 * p)),
                  pl.BlockSpec((T, BLK), lambda p, g: (0, 2 * p + 1)), pspec],
        out_specs=[gspec, gspec, gspec, pspec],
        out_shape=[jax.ShapeDtypeStruct((T, 1536), BF), jax.ShapeDtypeStruct((T, 1536), BF),
                   jax.ShapeDtypeStruct((T, 1536), BF), jax.ShapeDtypeStruct((T, 512), BF)],
        scratch_shapes=[pltpu.VMEM((2, 2 * BLK, BLK), F32), two(), big(), big(), big(), big(),
                        two(), two(), big(), big(), big(), pltpu.VMEM((2, 4, 2 * BLK, BLK), BF), big()],
        compiler_params=_params(("parallel", "arbitrary")),
    )(qs, ks, vs, z, cos, sa, sb, opre, lse, lse, dob)


def _merge_out_loss(og, ob, z, w_a, w_b, w_out, x, tgt, wf):
    tm = 512

    def body(og_ref, ob_ref, ga_ref, gb_ref, wa_ref, wb_ref, wo_ref, x_ref, t_ref, wf_ref,
             ya_ref, yb_ref, m_ref, dout_ref, loss_ref, gwf_ref):
        @pl.when(pl.program_id(0) == 0)
        def _():
            loss_ref[...] = jnp.zeros_like(loss_ref)
            gwf_ref[...] = jnp.zeros_like(gwf_ref)

        ya, yb = _dot(og_ref[...], wa_ref[...]), _dot(ob_ref[...], wb_ref[...])
        ya_ref[...] = ya
        yb_ref[...] = yb
        m = (_sigmoid(ga_ref[...]) * ya + _sigmoid(gb_ref[...]) * yb).astype(BF)
        m_ref[...] = m
        out = x_ref[...] + _dot(m, wo_ref[...])
        r = lax.rsqrt(jnp.mean(out * out, axis=-1, keepdims=True) + EPS)
        yh = out * r
        wfv = wf_ref[...]
        err = yh * wfv - t_ref[...]
        loss_ref[...] += jnp.sum(err * err, axis=0, keepdims=True) * (0.5 / D)
        dy = err * (1.0 / D)
        gwf_ref[...] += jnp.sum(dy * yh, axis=0, keepdims=True)
        dyh = dy * wfv
        dout_ref[...] = r * (dyh - yh * jnp.mean(dyh * yh, axis=-1, keepdims=True))

    row = pl.BlockSpec((tm, D), lambda i: (i, 0))
    vec = pl.BlockSpec((1, D), lambda i: (0, 0))
    whole = lambda w: pl.BlockSpec(w.shape, lambda i: (0, 0))
    return pl.pallas_call(
        body, name="merge_out_loss", grid=(T // tm,),
        in_specs=[row, pl.BlockSpec((tm, ob.shape[1]), lambda i: (i, 0)),
                  pl.BlockSpec((tm, D), lambda i: (i, GATE_COL0 // D)),
                  pl.BlockSpec((tm, D), lambda i: (i, GATE_COL0 // D + 1)),
                  whole(w_a), whole(w_b), whole(w_out), row, row, vec],
        out_specs=[row, row, row, row, vec, vec],
        out_shape=[jax.ShapeDtypeStruct((T, D), F32), jax.ShapeDtypeStruct((T, D), F32),
                   jax.ShapeDtypeStruct((T, D), BF), jax.ShapeDtypeStruct((T, D), F32),
                   jax.ShapeDtypeStruct((1, D), F32), jax.ShapeDtypeStruct((1, D), F32)],
        compiler_params=_params(("arbitrary",)),
    )(og, ob, z, z, w_a, w_b, w_out, x, tgt, wf)


def _merge_proj_bwd(dout, ya, yb, z, w_a, w_b, w_out):
    tm = 512

    def body(dout_ref, ya_ref, yb_ref, ga_ref, gb_ref, wa_ref, wb_ref, wo_ref,
             dya_ref, dyb_ref, dg_ref, dog_ref, dob_ref):
        dmv = _dot(dout_ref[...].astype(BF), wo_ref[...], _NT)
        sa, sb = _sigmoid(ga_ref[...]), _sigmoid(gb_ref[...])
        dya, dyb = (sa * dmv).astype(BF), (sb * dmv).astype(BF)
        dya_ref[...] = dya
        dyb_ref[...] = dyb
        dg_ref[:, :D] = (dmv * ya_ref[...] * sa * (1.0 - sa)).astype(BF)
        dg_ref[:, D:] = (dmv * yb_ref[...] * sb * (1.0 - sb)).astype(BF)
        dog_ref[...] = _dot(dya, wa_ref[...], _NT)
        dob_ref[...] = _dot(dyb, wb_ref[...], _NT)

    row = pl.BlockSpec((tm, D), lambda i: (i, 0))
    whole = lambda w: pl.BlockSpec(w.shape, lambda i: (0, 0))
    nb = w_b.shape[0]
    return pl.pallas_call(
        body, name="merge_proj_bwd", grid=(T // tm,),
        in_specs=[row, row, row, pl.BlockSpec((tm, D), lambda i: (i, GATE_COL0 // D)),
                  pl.BlockSpec((tm, D), lambda i: (i, GATE_COL0 // D + 1)), whole(w_a), whole(w_b), whole(w_out)],
        out_specs=[row, row, pl.BlockSpec((tm, 2 * D), lambda i: (i, 0)), row,
                   pl.BlockSpec((tm, nb), lambda i: (i, 0))],
        out_shape=[jax.ShapeDtypeStruct((T, D), BF), jax.ShapeDtypeStruct((T, D), BF),
                   jax.ShapeDtypeStruct((T, 2 * D), BF), jax.ShapeDtypeStruct((T, D), F32),
                   jax.ShapeDtypeStruct((T, nb), F32)],
        compiler_params=_params(("parallel",)),
    )(dout, ya, yb, z, z, w_a, w_b, w_out)


def _rope_inv_freq():
    inv = ROPE_THETA ** (-jnp.arange(0, 64, 2, dtype=F32) / 64)
    return jnp.tile(inv, 4).reshape(1, BLK)


def _local_step(x, pos, norm_w, lbl, hnw, wf, tgt, w_in, w_a, w_b, w_out, shard_shapes=()):
    invf = _rope_inv_freq()
    if shard_shapes:
        blk = jnp.reshape(2 * lax.axis_index("x") + lax.axis_index("y"), (1,)).astype(jnp.int32)
        h, cos, sa, sb, z_own, (w_near,) = _norm_and_rope_tables(
            x, norm_w, pos, invf, side=_gather_near_side(w_in, WEIGHT_AXES[0]), own=(w_in, blk))
        near = jnp.concatenate([blk ^ 2, blk ^ 1])
        z, (w_diag,) = _z_blocks(h, w_near, z_own, jnp.concatenate([near, near]), 2, name="z_proj_near",
                                 side=_gather_diag_side(w_near, w_in.shape, WEIGHT_AXES[0]))
        z, w_in, _ = _z_blocks(h, w_diag, z, jnp.concatenate([blk ^ 3, jnp.zeros_like(blk)]), 1, name="z_proj_diag",
                               fill=w_near, side=None)
        oraw, og, shist, (w_a, w_b, w_out) = _hgrn_fwd(
            z, lbl, hnw, side=_gather_side([w_a, w_b, w_out], WEIGHT_AXES[1:]))
    else:
        h, cos, sa, sb, _, _ = _norm_and_rope_tables(x, norm_w, pos, invf)
        z = _matmul(h, w_in, tm=T, tn=512, name="z_proj")
        oraw, og, shist, _ = _hgrn_fwd(z, lbl, hnw)
    ob, opre, lse, qs, ks, vs = _attn_fwd(z, cos, sa, sb)
    ya, yb, merged, dout, loss_vec, g_wf = _merge_out_loss(og, ob, z, w_a, w_b, w_out, x, tgt, wf)

    dya, dyb, dgates, dog, dob = _merge_proj_bwd(dout, ya, yb, z, w_a, w_b, w_out)
    g_wout = _matmul(merged, dout, ta=True, out_dtype=BF, tm=512, tn=1024, name="g_wout")
    g_wa = _matmul(og, dya, ta=True, out_dtype=BF, tm=512, tn=1024, name="g_wa")
    g_wb = _matmul(ob, dyb, ta=True, out_dtype=BF, tm=512, tn=1024, name="g_wb")
    small = [g_wa, g_wb, g_wout]
    side_s = side_w = None
    if shard_shapes:
        p3_s = _rs_partials(small, shard_shapes[1:], WEIGHT_AXES[1:], "small")
        side_s = _chip_exchange_side(p3_s, shard_shapes[1:], WEIGHT_AXES[1:])
    dz_h, dlb, g_hnw, land_s = _hgrn_bwd(z, lbl, hnw, oraw, dog, shist, side=side_s)
    dq, dk, dv, dag = _attn_bwd(z, qs, ks, vs, cos, sa, sb, opre, lse, dob)
    dz_parts = [dz_h, dq, dk, dv, dag, dgates]
    if shard_shapes:
        c = lax.axis_index("c")
        half = lambda i: jnp.reshape(i, (1,)).astype(jnp.int32)
        g_send = _grad_w_in_half(h, dz_parts, half(1 - c))
        g_keep, (g_sib,) = _grad_w_in_half(h, dz_parts, half(c), side=_sibling_send_side(g_send))
        p3_w = [_add_bf16(g_keep, g_sib, "pair_sum_w_in").reshape(1, D // 2, NIN)]
        side_w = _chip_exchange_relay_side(p3_w[0], shard_shapes[0])
    else:
        g_big = [_grad_w_in(h, dz_parts)] + small
    gx, g_nw, land_w = _grad_x(dz_parts, w_in, x, dout, norm_w, side=side_w)
    if shard_shapes:
        g_big = _rs_finish(p3_w + p3_s, [land_w[0]] + list(land_s), shard_shapes, WEIGHT_AXES)
    return dict(loss_vec=loss_vec, gx=gx, g_nw=g_nw, dlb=dlb, g_hnw=g_hnw, g_wf=g_wf,
                g_win=g_big[0], g_wa=g_big[1], g_wb=g_big[2], g_wout=g_big[3])


MESH = pl.DeviceIdType.MESH
HBM = pl.BlockSpec(memory_space=pl.ANY)
WEIGHT_AXES = (1, 0, 1, 0)


def _place():
    x, y, c = lax.axis_index("x"), lax.axis_index("y"), lax.axis_index("c")
    chips = [(1 - x, y), (x, 1 - y), (1 - x, 1 - y)]
    return x, y, c, chips


def _block_half(ref, shard_shape, axis, j, half):
    r, c = shard_shape
    hr = r // 2
    if axis == 0:
        return ref.at[pl.ds(pl.multiple_of(j * r + half * hr, 16), hr), :]
    return ref.at[pl.ds(pl.multiple_of(half * hr, 16), hr), pl.ds(pl.multiple_of(j * c, 128), c)]


class _Side:
    def __init__(self, arrays, out_shapes, sems, first, last, mid=None):
        self.arrays, self.out_shapes, self.sems, self.first, self.last = arrays, out_shapes, sems, first, last
        self.mid = mid


def _gather_side(shards, axes):
    n = len(shards)
    shapes = [s.shape for s in shards]

    def copies(ins, outs, sems):
        send1, recv1, send2, recv2, send0, recv0 = sems
        x, y, c, chips = _place()
        me = 2 * x + y
        sib = (x, y, 1 - c)
        near = ((1 - c) * (1 - x) + c * x, (1 - c) * y + c * (1 - y))
        far = ((1 - c) * x + c * (1 - x), (1 - c) * (1 - y) + c * y)
        out = []
        for a in range(n):
            r, cc = shapes[a]
            mine = (outs[a].at[pl.ds(pl.multiple_of(me * r, 16), r), :] if axes[a] == 0
                    else outs[a].at[:, pl.ds(pl.multiple_of(me * cc, 128), cc)])
            own = pltpu.make_async_remote_copy(
                src_ref=ins[a], dst_ref=mine, send_sem=send0.at[a], recv_sem=recv0.at[a],
                device_id=sib, device_id_type=MESH)
            src = ins[a].at[pl.ds(pl.multiple_of(c * (r // 2), 16), r // 2), :]
            sends = [pltpu.make_async_remote_copy(
                src_ref=src, dst_ref=_block_half(outs[a], shapes[a], axes[a], me, c),
                send_sem=send1.at[a, k], recv_sem=recv1.at[a, k], device_id=(*chips[k], c), device_id_type=MESH)
                for k in range(2)]

            def region(chip, half):
                return _block_half(outs[a], shapes[a], axes[a], 2 * chip[0] + chip[1], half)

            def arrival(chip, k):
                reg = region(chip, c)
                return pltpu.make_async_remote_copy(
                    src_ref=reg, dst_ref=reg, send_sem=send1.at[a, k], recv_sem=recv1.at[a, k],
                    device_id=(*chip, c), device_id_type=MESH)

            def to_sibling(chip, k):
                reg = region(chip, c)
                return pltpu.make_async_remote_copy(
                    src_ref=reg, dst_ref=reg, send_sem=send2.at[a, k], recv_sem=recv2.at[a, k],
                    device_id=sib, device_id_type=MESH)

            def from_sibling(chip, k):
                reg = region(chip, 1 - c)
                return pltpu.make_async_remote_copy(
                    src_ref=reg, dst_ref=reg, send_sem=send2.at[a, k], recv_sem=recv2.at[a, k],
                    device_id=sib, device_id_type=MESH)

            relay = pltpu.make_async_remote_copy(
                src_ref=region(near, c), dst_ref=region(near, c), send_sem=send1.at[a, 2], recv_sem=recv1.at[a, 2],
                device_id=(*far, c), device_id_type=MESH)
            hops = [(arrival(near, c), to_sibling(near, c)), (arrival(far, 1 - c), to_sibling(far, 1 - c)),
                    (arrival(chips[2], 2), to_sibling(chips[2], 2))]
            back = [from_sibling(chips[k], k) for k in range(3)]
            out.append((own, sends, relay, hops, back))
        return out

    def first(ins, outs, sems):
        for own, sends, _, _, _ in copies(ins, outs, sems):
            own.start()
            for cp in sends:
                cp.start()

    def mid(ins, outs, sems):
        per_array = copies(ins, outs, sems)
        for step in range(2):
            for _, _, relay, hops, _ in per_array:
                arrived, onward = hops[step]
                arrived.wait_recv()
                if step == 0:
                    relay.start()
                onward.start()

    def last(ins, outs, sems):
        per_array = copies(ins, outs, sems)
        for _, _, _, hops, _ in per_array:
            arrived, onward = hops[2]
            arrived.wait_recv()
            onward.start()
        for own, sends, relay, hops, back in per_array:
            for cp in back:
                cp.wait_recv()
            for cp in sends + [relay] + [onward for _, onward in hops]:
                cp.wait_send()
            own.wait()

    full = [(4 * r, c) if ax == 0 else (r, 4 * c) for (r, c), ax in zip(shapes, axes)]
    sems = [pltpu.SemaphoreType.DMA((n, 3)), pltpu.SemaphoreType.DMA((n, 3)),
            pltpu.SemaphoreType.DMA((n, 3)), pltpu.SemaphoreType.DMA((n, 3)),
            pltpu.SemaphoreType.DMA((n,)), pltpu.SemaphoreType.DMA((n,))]
    return _Side(list(shards), [jax.ShapeDtypeStruct(f, BF) for f in full], sems, first, last, mid)


def _gather_near_side(shard, axis):
    shape = shard.shape
    r, cc = shape

    def copies(ins, outs, sems):
        send1, recv1, send2, recv2, send0, recv0 = sems
        x, y, c, chips = _place()
        me = 2 * x + y
        sib = (x, y, 1 - c)
        mine = (outs[0].at[pl.ds(pl.multiple_of(me * r, 16), r), :] if axis == 0
                else outs[0].at[:, pl.ds(pl.multiple_of(me * cc, 128), cc)])
        own = pltpu.make_async_remote_copy(
            src_ref=ins[0], dst_ref=mine, send_sem=send0.at[0], recv_sem=recv0.at[0],
            device_id=sib, device_id_type=MESH)
        src = ins[0].at[pl.ds(pl.multiple_of(c * (r // 2), 16), r // 2), :]

        def region(k, half):
            return _block_half(outs[0], shape, axis, 2 * chips[k][0] + chips[k][1], half)

        def moves(k):
            return [pltpu.make_async_remote_copy(
                        src_ref=s, dst_ref=d, send_sem=ss.at[k], recv_sem=rs.at[k], device_id=dev,
                        device_id_type=MESH)
                    for s, d, ss, rs, dev in (
                        (src, _block_half(outs[0], shape, axis, me, c), send1, recv1, (*chips[k], c)),
                        (region(k, c), region(k, c), send1, recv1, (*chips[k], c)),
                        (region(k, c), region(k, c), send2, recv2, sib),
                        (region(k, 1 - c), region(k, 1 - c), send2, recv2, sib))]

        return own, [moves(k) for k in range(2)]

    def first(ins, outs, sems):
        own, per_chip = copies(ins, outs, sems)
        own.start()
        for send, _, _, _ in per_chip:
            send.start()

    def last(ins, outs, sems):
        own, per_chip = copies(ins, outs, sems)
        for _, arrived, onward, _ in per_chip:
            arrived.wait_recv()
            onward.start()
        for send, _, onward, back in per_chip:
            back.wait_recv()
            send.wait_send()
            onward.wait_send()
        own.wait()

    full = (4 * r, cc) if axis == 0 else (r, 4 * cc)
    sems = [pltpu.SemaphoreType.DMA((2,))] * 4 + [pltpu.SemaphoreType.DMA((1,))] * 2
    return _Side([shard], [jax.ShapeDtypeStruct(full, BF)], sems, first, last)


def _gather_diag_side(gathered, shape, axis):
    r, cc = shape

    def copies(ins, outs, sems):
        send1, recv1, send2, recv2 = sems
        x, y, c, _ = _place()
        sib = (x, y, 1 - c)
        near = ((1 - c) * (1 - x) + c * x, (1 - c) * y + c * (1 - y))
        far = ((1 - c) * x + c * (1 - x), (1 - c) * (1 - y) + c * y)

        def half(i):
            return outs[0].at[pl.ds(pl.multiple_of(i * (r // 2), 16), r // 2), :]

        def move(s, d, ss, rs, dev):
            return pltpu.make_async_remote_copy(
                src_ref=s, dst_ref=d, send_sem=ss.at[0], recv_sem=rs.at[0], device_id=dev, device_id_type=MESH)

        relay = move(_block_half(ins[0], shape, axis, 2 * near[0] + near[1], c), half(c), send1, recv1, (*far, c))
        arrived = move(half(c), half(c), send1, recv1, (*far, c))
        onward = move(half(c), half(c), send2, recv2, sib)
        back = move(half(1 - c), half(1 - c), send2, recv2, sib)
        return relay, arrived, onward, back

    def first(ins, outs, sems):
        copies(ins, outs, sems)[0].start()

    def last(ins, outs, sems):
        relay, arrived, onward, back = copies(ins, outs, sems)
        arrived.wait_recv()
        onward.start()
        back.wait_recv()
        relay.wait_send()
        onward.wait_send()

    return _Side([gathered], [jax.ShapeDtypeStruct(shape, BF)], [pltpu.SemaphoreType.DMA((1,))] * 4, first, last)


def _as3d(g, shard_shape, axis):
    r, c = shard_shape
    return g.reshape(4, r, c) if axis == 0 else g.reshape(1, r, 4 * c)


def _half_rows(ref3, hr, half):
    return ref3.at[:, pl.ds(pl.multiple_of(half * hr, 16), hr), :]


def _rs_pair_exchange(g3s, name):
    n = len(g3s)

    def body(*refs):
        ins, outs = refs[:n], refs[n:2 * n]
        send, recv = refs[2 * n:]
        x, y, c, _ = _place()
        cps = []
        for a in range(n):
            hr = g3s[a].shape[1] // 2
            cp = pltpu.make_async_remote_copy(
                src_ref=_half_rows(ins[a], hr, 1 - c), dst_ref=outs[a],
                send_sem=send.at[a], recv_sem=recv.at[a], device_id=(x, y, 1 - c), device_id_type=MESH)
            cp.start()
            cps.append(cp)
        for cp in cps:
            cp.wait()

    return pl.pallas_call(
        body, name=name,
        in_specs=[HBM] * n, out_specs=[HBM] * n,
        out_shape=[jax.ShapeDtypeStruct((g.shape[0], g.shape[1] // 2, g.shape[2]), BF) for g in g3s],
        scratch_shapes=[pltpu.SemaphoreType.DMA((n,)), pltpu.SemaphoreType.DMA((n,))],
    )(*g3s)


def _pair_sum(g3, land, cidx, name):
    nb, r, w = g3.shape
    hr = r // 2
    tr = 64

    def body(c_ref, g_ref, l_ref, o_ref):
        o_ref[...] = (g_ref[...].astype(F32) + l_ref[...].astype(F32)).astype(BF)

    blk = (nb, tr, w)
    return pl.pallas_call(
        body, name=name,
        grid_spec=pltpu.PrefetchScalarGridSpec(
            num_scalar_prefetch=1, grid=(hr // tr,),
            in_specs=[pl.BlockSpec(blk, lambda i, c: (0, c[0] * (hr // tr) + i, 0)),
                      pl.BlockSpec(blk, lambda i, c: (0, i, 0))],
            out_specs=pl.BlockSpec(blk, lambda i, c: (0, i, 0))),
        out_shape=jax.ShapeDtypeStruct((nb, hr, w), BF),
        compiler_params=_params(("parallel",)),
    )(cidx, g3, land)


def _chip_exchange_side(p3s, shapes, axes):
    n = len(p3s)

    def copies(ins, outs, sems):
        send, recv = sems
        x, y, c, chips = _place()
        cps = []
        for a in range(n):
            r, cc = shapes[a]
            for k, (px, py) in enumerate(chips):
                j = 2 * px + py
                src = ins[a].at[j] if axes[a] == 0 else ins[a].at[0, :, pl.ds(pl.multiple_of(j * cc, 128), cc)]
                cps.append(pltpu.make_async_remote_copy(
                    src_ref=src, dst_ref=outs[a].at[k], send_sem=send.at[a, k], recv_sem=recv.at[a, k],
                    device_id=(px, py, c), device_id_type=MESH))
        return cps

    def first(ins, outs, sems):
        for cp in copies(ins, outs, sems):
            cp.start()

    def last(ins, outs, sems):
        for cp in copies(ins, outs, sems):
            cp.wait()

    return _Side(list(p3s), [jax.ShapeDtypeStruct((3, r // 2, c), BF) for r, c in shapes],
                 [pltpu.SemaphoreType.DMA((n, 3)), pltpu.SemaphoreType.DMA((n, 3))], first, last)


def _chip_exchange_relay_side(p3, shape):
    r, cc = shape
    hr = r // 2
    rows = 64

    def copies(ins, outs, sems):
        send, recv, local, mine, theirs = sems
        x, y, c, chips = _place()
        near = ((1 - c) * (1 - x) + c * x, (1 - c) * y + c * (1 - y))
        far = ((1 - c) * x + c * (1 - x), (1 - c) * (1 - y) + c * y)
        land, staged = outs

        def block(chip):
            return ins[0].at[0, :, pl.ds(pl.multiple_of((2 * chip[0] + chip[1]) * cc, 128), cc)]

        def move(s, d, k, dev):
            return pltpu.make_async_remote_copy(
                src_ref=s, dst_ref=d, send_sem=send.at[k], recv_sem=recv.at[k], device_id=dev, device_id_type=MESH)

        return dict(
            direct=move(block(near), land.at[c], 0, (*near, c)),
            for_relay=move(block(chips[2]), staged, 1, (*near, c)),
            summed=move(mine, land.at[1 - c], 2, (*far, c)),
            direct_in=move(land.at[c], land.at[c], 0, (*near, c)),
            staged_in=move(staged, staged, 1, (*near, c)),
            summed_in=move(land.at[1 - c], land.at[1 - c], 2, (*far, c)),
            load_mine=pltpu.make_async_copy(block(far), mine, local.at[0]),
            load_theirs=pltpu.make_async_copy(staged, theirs, local.at[1]))

    def first(ins, outs, sems):
        cps = copies(ins, outs, sems)
        cps["for_relay"].start()
        cps["direct"].start()

    def mid(ins, outs, sems):
        cps = copies(ins, outs, sems)
        mine, theirs = sems[3], sems[4]
        cps["load_mine"].start()
        cps["staged_in"].wait_recv()
        cps["load_theirs"].start()
        cps["load_mine"].wait()
        cps["load_theirs"].wait()

        def add(i, carry):
            rs = pl.ds(pl.multiple_of(i * rows, 16), rows)
            mine[rs, :] = (mine[rs, :].astype(F32) + theirs[rs, :].astype(F32)).astype(BF)
            return carry

        lax.fori_loop(0, hr // rows, add, 0)
        cps["summed"].start()

    def last(ins, outs, sems):
        cps = copies(ins, outs, sems)
        cps["direct_in"].wait_recv()
        cps["summed_in"].wait_recv()
        for name in ("direct", "for_relay", "summed"):
            cps[name].wait_send()

    sems = [pltpu.SemaphoreType.DMA((3,)), pltpu.SemaphoreType.DMA((3,)), pltpu.SemaphoreType.DMA((2,)),
            pltpu.VMEM((hr, cc), BF), pltpu.VMEM((hr, cc), BF)]
    return _Side([p3], [jax.ShapeDtypeStruct((2, hr, cc), BF), jax.ShapeDtypeStruct((hr, cc), BF)], sems,
                 first, last, mid)


def _chip_sum(p3, land, shard_shape, axis, idx, name):
    r, c = shard_shape
    hr = r // 2
    tr = 64
    nt = hr // tr
    slots = land.shape[0]

    def body(idx_ref, p_ref, l_ref, o_ref):
        acc = p_ref[...].astype(F32)
        for k in range(slots):
            acc = acc + l_ref[k].astype(F32)
        o_ref[...] = acc

    own = (pl.BlockSpec((None, tr, c), lambda i, idx: (idx[0], i, 0)) if axis == 0
           else pl.BlockSpec((None, tr, c), lambda i, idx: (0, i, idx[0])))
    return pl.pallas_call(
        body, name=name,
        grid_spec=pltpu.PrefetchScalarGridSpec(
            num_scalar_prefetch=1, grid=(nt,),
            in_specs=[own, pl.BlockSpec((slots, tr, c), lambda i, idx: (0, i, 0))],
            out_specs=pl.BlockSpec((tr, c), lambda i, idx: (idx[1] * nt + i, 0))),
        out_shape=jax.ShapeDtypeStruct((r, c), F32),
        compiler_params=_params(("parallel",)),
    )(idx, p3, land)


def _rs_pair_gather(fulls):
    n = len(fulls)

    def body(*refs):
        ins, outs = refs[:n], refs[n:2 * n]
        send, recv = refs[2 * n:]
        x, y, c, _ = _place()
        cps = []
        for a in range(n):
            hr = fulls[a].shape[0] // 2
            rows = pl.ds(pl.multiple_of(c * hr, 8), hr)
            cp = pltpu.make_async_remote_copy(
                src_ref=ins[a].at[rows, :], dst_ref=outs[a].at[rows, :], send_sem=send.at[a], recv_sem=recv.at[a],
                device_id=(x, y, 1 - c), device_id_type=MESH)
            cp.start()
            cps.append(cp)
        for a, cp in enumerate(cps):
            cp.wait_send()
            hr = fulls[a].shape[0] // 2
            other = pl.ds(pl.multiple_of((1 - c) * hr, 8), hr)
            pltpu.make_async_remote_copy(
                src_ref=ins[a].at[other, :], dst_ref=outs[a].at[other, :], send_sem=send.at[a], recv_sem=recv.at[a],
                device_id=(x, y, 1 - c), device_id_type=MESH).wait_recv()

    return pl.pallas_call(
        body, name="grads_pair_gather",
        in_specs=[HBM] * n, out_specs=[HBM] * n,
        out_shape=[jax.ShapeDtypeStruct(f.shape, F32) for f in fulls],
        input_output_aliases={a: a for a in range(n)},
        scratch_shapes=[pltpu.SemaphoreType.DMA((n,)), pltpu.SemaphoreType.DMA((n,))],
    )(*fulls)


def _sibling_send_side(arr):
    def copy(ins, outs, sems):
        x, y, c, _ = _place()
        return pltpu.make_async_remote_copy(
            src_ref=ins[0], dst_ref=outs[0], send_sem=sems[0].at[0], recv_sem=sems[1].at[0],
            device_id=(x, y, 1 - c), device_id_type=MESH)

    return _Side([arr], [jax.ShapeDtypeStruct(arr.shape, arr.dtype)],
                 [pltpu.SemaphoreType.DMA((1,)), pltpu.SemaphoreType.DMA((1,))],
                 lambda ins, outs, sems: copy(ins, outs, sems).start(),
                 lambda ins, outs, sems: copy(ins, outs, sems).wait())


def _add_bf16(a, b, name):
    r, c = a.shape
    tr = 64

    def body(a_ref, b_ref, o_ref):
        o_ref[...] = (a_ref[...].astype(F32) + b_ref[...].astype(F32)).astype(BF)

    blk = pl.BlockSpec((tr, c), lambda i: (i, 0))
    return pl.pallas_call(
        body, name=name, grid=(r // tr,), in_specs=[blk, blk], out_specs=blk,
        out_shape=jax.ShapeDtypeStruct((r, c), BF), compiler_params=_params(("parallel",)),
    )(a, b)


def _rs_partials(grads, shapes, axes, tag):
    cidx = jnp.reshape(lax.axis_index("c"), (1,)).astype(jnp.int32)
    g3s = [_as3d(g, s, ax) for g, s, ax in zip(grads, shapes, axes)]
    lands = _rs_pair_exchange(g3s, f"grads_pair_exchange_{tag}")
    return [_pair_sum(g3, l, cidx, f"pair_sum_{tag}_{a}") for a, (g3, l) in enumerate(zip(g3s, lands))]


def _rs_finish(p3s, landed, shapes, axes):
    x, y, c = lax.axis_index("x"), lax.axis_index("y"), lax.axis_index("c")
    idx = jnp.stack([2 * x + y, c]).astype(jnp.int32)
    fulls = [_chip_sum(p3, l2, s, ax, idx, f"chip_sum_{a}")
             for a, (p3, l2, s, ax) in enumerate(zip(p3s, landed, shapes, axes))]
    return _rs_pair_gather(fulls)


NSMALL = 8


def _small_all_reduce(g_nw, dlb, g_hnw, g_wf, loss_vec):
    def body(nw_ref, lb_ref, hn_ref, wf_ref, ls_ref, out_ref, pack_ref, buf_ref, send, recv):
        x, y, c = lax.axis_index("x"), lax.axis_index("y"), lax.axis_index("c")
        me = 4 * x + 2 * y + c
        pack_ref[...] = jnp.zeros_like(pack_ref)
        pack_ref[0:1, :] = nw_ref[...]
        pack_ref[1:2, :] = lb_ref[...]
        pack_ref[2:3, 0:HK] = hn_ref[...]
        pack_ref[3:4, :] = wf_ref[...]
        pack_ref[4:5, :] = ls_ref[...]
        buf_ref[me] = pack_ref[...]
        cps = []
        for d in range(1, 8):
            dx, dy, dc = d >> 2, (d >> 1) & 1, d & 1
            peer = (1 - x if dx else x, 1 - y if dy else y, 1 - c if dc else c)
            cp = pltpu.make_async_remote_copy(
                src_ref=pack_ref, dst_ref=buf_ref.at[me], send_sem=send.at[d - 1], recv_sem=recv.at[d - 1],
                device_id=peer, device_id_type=MESH)
            cp.start()
            cps.append(cp)
        for d in range(1, 8):
            dx, dy, dc = d >> 2, (d >> 1) & 1, d & 1
            src = 4 * (1 - x if dx else x) + 2 * (1 - y if dy else y) + (1 - c if dc else c)
            pltpu.make_async_remote_copy(
                src_ref=pack_ref, dst_ref=buf_ref.at[src], send_sem=send.at[d - 1], recv_sem=recv.at[d - 1],
                device_id=(x, y, c), device_id_type=MESH).wait_recv()
        for cp in cps:
            cp.wait_send()
        acc = buf_ref[0]
        for i in range(1, 8):
            acc = acc + buf_ref[i]
        out_ref[...] = acc

    vm = pl.BlockSpec(memory_space=pltpu.VMEM)
    return pl.pallas_call(
        body, name="small_all_reduce",
        in_specs=[vm] * 5, out_specs=vm,
        out_shape=jax.ShapeDtypeStruct((NSMALL, D), F32),
        scratch_shapes=[pltpu.VMEM((NSMALL, D), F32), pltpu.VMEM((8, NSMALL, D), F32),
                        pltpu.SemaphoreType.DMA((7,)), pltpu.SemaphoreType.DMA((7,))],
    )(g_nw, dlb, g_hnw, g_wf, loss_vec)


def _adamw_math(w, g, m, v):
    m = B1 * m + (1.0 - B1) * g
    v = B2 * v + (1.0 - B2) * (g * g)
    m_hat = m / (1.0 - B1 ** STEP)
    v_hat = v / (1.0 - B2 ** STEP)
    return -LR * (m_hat / (jnp.sqrt(v_hat) + ADAM_EPS) + WD * w), m, v


def _adamw(w, g, m, v, name):
    r, c = w.shape
    tr = 64

    def body(w_ref, g_ref, m_ref, v_ref, d_ref, nm_ref, nv_ref, go_ref):
        g = g_ref[...]
        d_ref[...], nm_ref[...], nv_ref[...] = _adamw_math(w_ref[...], g, m_ref[...], v_ref[...])
        go_ref[...] = g

    blk = pl.BlockSpec((tr, c), lambda i: (i, 0))
    return pl.pallas_call(
        body, name=name, grid=(r // tr,), in_specs=[blk] * 4, out_specs=[blk] * 4,
        out_shape=[jax.ShapeDtypeStruct((r, c), F32)] * 4,
        compiler_params=_params(("parallel",)),
    )(w, g, m, v)


def _adamw_whole(groups, name):
    n = len(groups)

    def body(*refs):
        ins, outs = refs[:4 * n], refs[4 * n:]
        for a in range(n):
            w_ref, g_ref, m_ref, v_ref = ins[4 * a:4 * a + 4]
            g = g_ref[...]
            outs[4 * a][...], outs[4 * a + 1][...], outs[4 * a + 2][...] = _adamw_math(
                w_ref[...], g, m_ref[...], v_ref[...])
            outs[4 * a + 3][...] = g

    vm = pl.BlockSpec(memory_space=pltpu.VMEM)
    out = pl.pallas_call(
        body, name=name, in_specs=[vm] * (4 * n), out_specs=[vm] * (4 * n),
        out_shape=[jax.ShapeDtypeStruct(grp[0].shape, F32) for grp in groups for _ in range(4)],
        compiler_params=_params(),
    )(*[a for grp in groups for a in grp])
    return [out[4 * a:4 * a + 4] for a in range(n)]


def _small_update(red, lbl, params):
    def body(red_ref, *refs):
        ins, outs = refs[:12], refs[12:]
        lb = _lower_bound(ins[3][...])
        dl0 = red_ref[1:2, :] * lb * (1.0 - lb)
        row = lax.broadcasted_iota(jnp.int32, (2, D), 0)
        grads = [red_ref[0:1, :], jnp.where(row == 0, dl0, -dl0), red_ref[2:3, 0:HK], red_ref[3:4, :]]
        for i, g in enumerate(grads):
            w, m, v = ins[3 * i][...], ins[3 * i + 1][...], ins[3 * i + 2][...]
            d, nm, nv = _adamw_math(w, g, m, v)
            outs[4 * i][...] = g
            outs[4 * i + 1][...] = d
            outs[4 * i + 2][...] = nm
            outs[4 * i + 3][...] = nv
        outs[16][...] = jnp.sum(red_ref[4:5, :], axis=1, keepdims=True)

    flat = [a for p in params for a in p]
    vm = pl.BlockSpec(memory_space=pltpu.VMEM)
    shapes = [jax.ShapeDtypeStruct(p[0].shape, F32) for p in params for _ in range(4)]
    return pl.pallas_call(
        body, name="small_update",
        in_specs=[vm] * 13, out_specs=[vm] * 17,
        out_shape=shapes + [jax.ShapeDtypeStruct((1, 1), F32)],
    )(red, *flat)


def kernel(x, positions, norm_w, w_in, lb_logits, hgrn_norm_w, w_branch_a, w_branch_b, w_out, final_norm_w, loss_target, m_norm_w, m_w_in, m_lb_logits, m_hgrn_norm_w, m_w_branch_a, m_w_branch_b, m_w_out, m_final_norm_w, v_norm_w, v_w_in, v_lb_logits, v_hgrn_norm_w, v_w_branch_a, v_w_branch_b, v_w_out, v_final_norm_w):
    big_w = [w_in[0], w_branch_a[0], w_branch_b[0], w_out[0]]
    big_m = [m_w_in[0], m_w_branch_a[0], m_w_branch_b[0], m_w_out[0]]
    big_v = [v_w_in[0], v_w_branch_a[0], v_w_branch_b[0], v_w_out[0]]
    shapes = [w.shape for w in big_w]
    wf = final_norm_w.reshape(1, D)

    shards = [w.astype(BF) for w in big_w]
    loc = _local_step(x[0], positions.reshape(T, 1), norm_w, lb_logits, hgrn_norm_w, wf, loss_target[0],
                      *shards, shard_shapes=shapes)
    g_big = [loc["g_win"], loc["g_wa"], loc["g_wb"], loc["g_wout"]]
    red = _small_all_reduce(loc["g_nw"], loc["dlb"], loc["g_hnw"], loc["g_wf"], loc["loss_vec"])

    small = _small_update(red, lb_logits, [
        (norm_w, m_norm_w, v_norm_w), (lb_logits, m_lb_logits, v_lb_logits),
        (hgrn_norm_w, m_hgrn_norm_w, v_hgrn_norm_w),
        (wf, m_final_norm_w.reshape(1, D), v_final_norm_w.reshape(1, D))])
    loss = small[16].reshape(())
    sg, sd, sm, sv = ([small[4 * i + j] for i in range(4)] for j in range(4))
    for lst in (sg, sd, sm, sv):
        lst[3] = lst[3].reshape(D)
    per_w = list(zip(big_w, g_big, big_m, big_v))
    upd = [_adamw(*per_w[0], "adamw_w_in")] + _adamw_whole(per_w[1:], "adamw_branches_out")
    bd, bm, bv, bg = ([u[j][None] for u in upd] for j in range(4))

    def order(s, b):
        return [s[0], b[0], s[1], s[2], b[1], b[2], b[3], s[3]]

    return (loss, loc["gx"][None], *order(sg, bg), *order(sd, bd), *order(sm, bm), *order(sv, bv))
```

```python
import functools

import jax
import jax.numpy as jnp
from jax import lax
from jax.experimental import pallas as pl
from jax.experimental.pallas import tpu as pltpu

T = 2048
D = 1024
NIN = 11264
HEADS = 8
HK = 128
CH = 16
NCH = T // CH
HSTEP = 2
ATT_GROUPS = ((128, 1), (512, 4), (2048, 16))
ATT_COL0 = 4096
AG_COL0 = 8704
GATE_COL0 = 9216
EPS = 1e-6
ROPE_THETA = 10000.0
LR, B1, B2, ADAM_EPS, WD, STEP = 0.001, 0.9, 0.999, 1e-08, 0.01, 10

F32 = jnp.float32
BF = jnp.bfloat16
VMEM_LIMIT = 56 * 1024 * 1024

_NN = (((1,), (0,)), ((), ()))
_NT = (((1,), (1,)), ((), ()))
_TN = (((0,), (0,)), ((), ()))


def _dot(a, b, dims=_NN):
    return lax.dot_general(a, b, dims, preferred_element_type=F32)


def _bdot(a, b, dims=_NN):
    return lax.dot_general(a.astype(BF), b.astype(BF), dims, preferred_element_type=F32)


def _sigmoid(x):
    return jax.nn.sigmoid(x)


def _params(sem=None):
    return pltpu.CompilerParams(dimension_semantics=sem, vmem_limit_bytes=VMEM_LIMIT)


def _matmul(a, b, *, ta=False, tb=False, out_dtype=F32, tm=512, tn=512, tk=None, name, side=None):
    m = a.shape[1] if ta else a.shape[0]
    kdim = a.shape[0] if ta else a.shape[1]
    n = b.shape[0] if tb else b.shape[1]
    tk = tk or kdim
    tm, tn = min(tm, m), min(tn, n)
    nm, nn, nk = m // tm, n // tn, kdim // tk
    dims = (((0 if ta else 1,), (1 if tb else 0,)), ((), ()))
    s_arrays, s_in_specs, s_shapes, s_out_specs, s_sems = _side_io(side)
    na, no = len(s_arrays), len(s_shapes)
    nacc = 1 if nk > 1 else 0

    def body(*refs):
        a_ref, b_ref = refs[:2]
        s_ins, o_ref, s_outs = refs[2:2 + na], refs[2 + na], refs[3 + na:3 + na + no]
        scratch = refs[3 + na + no:]
        s_sem_refs = scratch[nacc:]
        i, j, k = pl.program_id(0), pl.program_id(1), pl.program_id(2)
        if side is not None:
            @pl.when((i == 0) & (j == 0) & (k == 0))
            def _():
                side.first(s_ins, s_outs, s_sem_refs)

        prod = _bdot(a_ref[...], b_ref[...], dims)
        if nk == 1:
            o_ref[...] = prod.astype(out_dtype)
        else:
            acc = scratch[0]

            @pl.when(k == 0)
            def _():
                acc[...] = prod

            @pl.when(k > 0)
            def _():
                acc[...] += prod

            @pl.when(k == nk - 1)
            def _():
                o_ref[...] = acc[...].astype(out_dtype)

        if side is not None:
            @pl.when((i == nm - 1) & (j == nn - 1) & (k == nk - 1))
            def _():
                side.last(s_ins, s_outs, s_sem_refs)

    a_spec = pl.BlockSpec((tk, tm), lambda i, j, k: (k, i)) if ta else pl.BlockSpec((tm, tk), lambda i, j, k: (i, k))
    b_spec = pl.BlockSpec((tn, tk), lambda i, j, k: (j, k)) if tb else pl.BlockSpec((tk, tn), lambda i, j, k: (k, j))
    sem = ("parallel", "parallel", "arbitrary") if side is None else ("arbitrary",) * 3
    out = pl.pallas_call(
        body, name=name, grid=(nm, nn, nk),
        in_specs=[a_spec, b_spec] + s_in_specs,
        out_specs=[pl.BlockSpec((tm, tn), lambda i, j, k: (i, j))] + s_out_specs,
        out_shape=[jax.ShapeDtypeStruct((m, n), out_dtype)] + s_shapes,
        scratch_shapes=([pltpu.VMEM((tm, tn), F32)] if nk > 1 else []) + s_sems,
        compiler_params=_params(sem),
    )(a, b, *s_arrays)
    return out[0] if side is None else (out[0], out[1:])


DZ_TILE = 512


def _part_offsets(parts):
    counts = [p.shape[1] // DZ_TILE for p in parts]
    offs = [sum(counts[:i]) for i in range(len(parts))]
    return counts, offs


def _part_spec(rows, cnt, off, tile_axis):
    def index(*g):
        return (0 if rows is None else g[0], jnp.clip(g[tile_axis] - off, 0, cnt - 1))
    return index


def _grad_w_in(h, parts):
    counts, offs = _part_offsets(parts)
    n = len(parts)

    def body(h_ref, *refs):
        o_ref = refs[n]
        j = pl.program_id(0)
        for p_ref, cnt, off in zip(refs[:n], counts, offs):
            @pl.when((j >= off) & (j < off + cnt))
            def _(p_ref=p_ref):
                o_ref[...] = _bdot(h_ref[...], p_ref[...], _TN).astype(BF)

    return pl.pallas_call(
        body, name="g_win", grid=(sum(counts),),
        in_specs=[pl.BlockSpec((T, D), lambda j: (0, 0))] +
                 [pl.BlockSpec((T, DZ_TILE), _part_spec(None, c, o, 0)) for c, o in zip(counts, offs)],
        out_specs=pl.BlockSpec((D, DZ_TILE), lambda j: (0, j)),
        out_shape=jax.ShapeDtypeStruct((D, NIN), BF),
        compiler_params=_params(("parallel",)),
    )(h, *parts)


def _grad_w_in_half(h, parts, half_idx, side=None):
    counts, offs = _part_offsets(parts)
    n = len(parts)
    nj = sum(counts)
    s_arrays, s_in_specs, s_shapes, s_out_specs, s_sems = _side_io(side)
    na, no = len(s_arrays), len(s_shapes)

    def body(idx_ref, h_ref, *refs):
        s_ins, o_ref, s_outs, s_sem_refs = refs[n:n + na], refs[n + na], refs[n + na + 1:n + na + 1 + no], refs[n + na + 1 + no:]
        j = pl.program_id(0)
        if side is not None:
            @pl.when(j == 0)
            def _():
                side.first(s_ins, s_outs, s_sem_refs)

        for p_ref, cnt, off in zip(refs[:n], counts, offs):
            @pl.when((j >= off) & (j < off + cnt))
            def _(p_ref=p_ref):
                o_ref[...] = _bdot(h_ref[...], p_ref[...], _TN).astype(BF)

        if side is not None:
            @pl.when(j == nj - 1)
            def _():
                side.last(s_ins, s_outs, s_sem_refs)

    def part_spec(cnt, off):
        return pl.BlockSpec((T, DZ_TILE), lambda j, idx: (0, jnp.clip(j - off, 0, cnt - 1)))

    out = pl.pallas_call(
        body, name="g_win_half" if side is None else "g_win_half_carrying",
        grid_spec=pltpu.PrefetchScalarGridSpec(
            num_scalar_prefetch=1, grid=(nj,),
            in_specs=[pl.BlockSpec((T, D // 2), lambda j, idx: (0, idx[0]))] +
                     [part_spec(c, o) for c, o in zip(counts, offs)] + s_in_specs,
            out_specs=[pl.BlockSpec((D // 2, DZ_TILE), lambda j, idx: (0, j))] + s_out_specs,
            scratch_shapes=s_sems),
        out_shape=[jax.ShapeDtypeStruct((D // 2, NIN), BF)] + s_shapes,
        compiler_params=_params(("parallel",) if side is None else ("arbitrary",)),
    )(half_idx, h, *parts, *s_arrays)
    return out[0] if side is None else (out[0], out[1:])


def _side_io(side):
    if side is None:
        return [], [], [], [], []
    return (side.arrays, [HBM] * len(side.arrays), side.out_shapes, [HBM] * len(side.out_shapes), side.sems)


def _grad_x(parts, w_in, x, dout, norm_w, side=None):
    counts, offs = _part_offsets(parts)
    n = len(parts)
    tm = 1024
    nm, nk = T // tm, sum(counts)
    s_arrays, s_in_specs, s_shapes, s_out_specs, s_sems = _side_io(side)
    na, no = len(s_arrays), len(s_shapes)

    def body(*refs):
        w_ref, x_ref, dout_ref, nw_ref = refs[n:n + 4]
        s_ins = refs[n + 4:n + 4 + na]
        gx_ref, gw_ref = refs[n + 4 + na:n + 6 + na]
        s_outs = refs[n + 6 + na:n + 6 + na + no]
        acc = refs[n + 6 + na + no]
        s_sem_refs = refs[n + 7 + na + no:]
        i, k = pl.program_id(0), pl.program_id(1)

        @pl.when((i == 0) & (k == 0))
        def _():
            gw_ref[...] = jnp.zeros_like(gw_ref)
            if side is not None:
                side.first(s_ins, s_outs, s_sem_refs)

        @pl.when(k == 0)
        def _():
            acc[...] = jnp.zeros_like(acc)

        if side is not None and side.mid is not None:
            @pl.when((i == nm - 1) & (k == 0))
            def _():
                side.mid(s_ins, s_outs, s_sem_refs)

        for p_ref, cnt, off in zip(refs[:n], counts, offs):
            @pl.when((k >= off) & (k < off + cnt))
            def _(p_ref=p_ref):
                acc[...] += _bdot(p_ref[...], w_ref[...], _NT)

        @pl.when(k == nk - 1)
        def _():
            gw = jnp.zeros((1, D), F32)
            for c in range(tm // BLK):
                rows = pl.ds(BLK * c, BLK)
                xv, dhv = x_ref[rows, :], acc[rows, :]
                r = lax.rsqrt(jnp.mean(xv * xv, axis=-1, keepdims=True) + EPS)
                nrm = xv * r
                dn = dhv * nw_ref[...]
                gw = gw + jnp.sum(dhv * nrm, axis=0, keepdims=True)
                gx_ref[rows, :] = dout_ref[rows, :] + r * (dn - nrm * jnp.mean(dn * nrm, axis=-1, keepdims=True))
            gw_ref[...] += gw

        if side is not None:
            @pl.when((i == nm - 1) & (k == nk - 1))
            def _():
                side.last(s_ins, s_outs, s_sem_refs)

    row = pl.BlockSpec((tm, D), lambda i, k: (i, 0))
    vec = pl.BlockSpec((1, D), lambda i, k: (0, 0))
    out = pl.pallas_call(
        body, name="grad_x", grid=(nm, nk),
        in_specs=[pl.BlockSpec((tm, DZ_TILE), _part_spec(0, c, o, 1)) for c, o in zip(counts, offs)] +
                 [pl.BlockSpec((D, DZ_TILE), lambda i, k: (0, k)), row, row, vec] + s_in_specs,
        out_specs=[row, vec] + s_out_specs,
        out_shape=[jax.ShapeDtypeStruct((T, D), F32), jax.ShapeDtypeStruct((1, D), F32)] + s_shapes,
        scratch_shapes=[pltpu.VMEM((tm, D), F32)] + s_sems,
        compiler_params=_params(("arbitrary", "arbitrary")),
    )(*parts, w_in, x, dout, norm_w, *s_arrays)
    return out[0], out[1], out[2:]


def _norm_and_rope_tables(x, w, pos, invf, side=None, own=None):
    tm = 256
    nm = T // tm
    s_arrays, s_in_specs, s_shapes, s_out_specs, s_sems = _side_io(side)
    na, no = len(s_arrays), len(s_shapes)
    nz = 0 if own is None else 1
    wsh, blk = own if own is not None else (None, jnp.zeros((1,), jnp.int32))

    def body(blk_ref, *refs):
        x_ref, w_ref, pos_ref, invf_ref = refs[:4]
        s_ins = refs[4 + nz:4 + nz + na]
        h_ref, cos_ref, sa_ref, sb_ref = refs[4 + nz + na:8 + nz + na]
        s_outs = refs[8 + 2 * nz + na:8 + 2 * nz + na + no]
        s_sem_refs = refs[8 + 2 * nz + na + no:]
        if side is not None:
            @pl.when(pl.program_id(0) == 0)
            def _():
                side.first(s_ins, s_outs, s_sem_refs)

        xv = x_ref[...]
        r = lax.rsqrt(jnp.mean(xv * xv, axis=-1, keepdims=True) + EPS)
        h = (xv * r * w_ref[...]).astype(BF)
        h_ref[...] = h
        if own is not None:
            refs[8 + nz + na][...] = _dot(h, refs[4][...])
        first = (lax.broadcasted_iota(jnp.int32, (tm, 128), 1) % 64) < 32
        ang = pos_ref[...].astype(F32) * invf_ref[...]
        s = jnp.sin(ang)
        cos_ref[...] = jnp.cos(ang)
        sa_ref[...] = jnp.where(first, -s, 0.0)
        sb_ref[...] = jnp.where(first, 0.0, s)
        if side is not None:
            @pl.when(pl.program_id(0) == nm - 1)
            def _():
                side.last(s_ins, s_outs, s_sem_refs)

    tab = pl.BlockSpec((tm, 128), lambda i, b: (i, 0))
    own_in = [] if own is None else [pl.BlockSpec(wsh.shape, lambda i, b: (0, 0))]
    own_out = [] if own is None else [pl.BlockSpec((tm, wsh.shape[1]), lambda i, b: (i, b[0]))]
    own_shape = [] if own is None else [jax.ShapeDtypeStruct((T, NIN), F32)]
    out = pl.pallas_call(
        body, name="norm_and_rope_tables",
        grid_spec=pltpu.PrefetchScalarGridSpec(
            num_scalar_prefetch=1, grid=(nm,),
            in_specs=[pl.BlockSpec((tm, D), lambda i, b: (i, 0)), pl.BlockSpec((1, D), lambda i, b: (0, 0)),
                      pl.BlockSpec((tm, 1), lambda i, b: (i, 0)), pl.BlockSpec((1, 128), lambda i, b: (0, 0))]
                     + own_in + s_in_specs,
            out_specs=[pl.BlockSpec((tm, D), lambda i, b: (i, 0)), tab, tab, tab] + own_out + s_out_specs,
            scratch_shapes=s_sems),
        out_shape=[jax.ShapeDtypeStruct((T, D), BF)] + [jax.ShapeDtypeStruct((T, 128), F32)] * 3 + own_shape + s_shapes,
        compiler_params=_params(("parallel",) if side is None else ("arbitrary",)),
    )(blk, x, w, pos, invf, *([] if own is None else [wsh]), *s_arrays)
    return out[0], out[1], out[2], out[3], (out[4] if own is not None else None), out[4 + nz:]


def _z_blocks(h, w, z, idx, nb, side, name, fill=None):
    tm, tn = 1024, NIN // 8
    s_arrays, s_in_specs, s_shapes, s_out_specs, s_sems = _side_io(side)
    na, no = len(s_arrays), len(s_shapes)
    nm, ns = T // tm, 2 * nb
    nf = 0 if fill is None else 1

    def col(first, i, s, b):
        return (0, b[first + s // 2] * 2 + s % 2)

    def body(idx_ref, h_ref, w_ref, zin_ref, *refs):
        s_ins = refs[nf:nf + na]
        o_ref = refs[nf + na]
        s_outs = refs[nf + na + 1 + nf:nf + na + 1 + nf + no]
        s_sem_refs = refs[nf + na + 1 + nf + no + nf:]
        i, s = pl.program_id(0), pl.program_id(1)

        if side is not None:
            @pl.when((i == 0) & (s == 0))
            def _():
                side.first(s_ins, s_outs, s_sem_refs)

        if fill is not None:
            tile = pl.ds(pl.multiple_of((idx_ref[0] * 2 + s) * tn, 128), tn)
            store = pltpu.make_async_copy(w_ref, refs[nf + na + 1].at[:, tile], refs[nf + na + 1 + nf + no].at[0])
            pl.when(i == 0)(store.start)
        o_ref[...] = _dot(h_ref[...], w_ref[...])
        if fill is not None:
            pl.when(i == 0)(store.wait)

        if side is not None:
            @pl.when((i == nm - 1) & (s == ns - 1))
            def _():
                side.last(s_ins, s_outs, s_sem_refs)

    fills = [] if fill is None else [fill]
    out = pl.pallas_call(
        body, name=name,
        grid_spec=pltpu.PrefetchScalarGridSpec(
            num_scalar_prefetch=1, grid=(nm, ns),
            in_specs=[pl.BlockSpec((tm, D), lambda i, s, b: (i, 0)), pl.BlockSpec((D, tn), functools.partial(col, nb)),
                      HBM] + [HBM] * nf + s_in_specs,
            out_specs=[pl.BlockSpec((tm, tn), lambda i, s, b: (i, col(0, i, s, b)[1]))] + [HBM] * nf + s_out_specs,
            scratch_shapes=[pltpu.SemaphoreType.DMA((1,))] * nf + s_sems),
        out_shape=[jax.ShapeDtypeStruct((T, NIN), F32)] + [jax.ShapeDtypeStruct(f.shape, f.dtype) for f in fills]
                  + s_shapes,
        input_output_aliases={3: 0, **({4: 1} if fill is not None else {})},
        compiler_params=_params(("arbitrary", "arbitrary")),
    )(idx, h, w, z, *fills, *s_arrays)
    return (out[0], *out[1:1 + nf], out[1 + nf:])


def _lower_bound(lbl):
    mx = jnp.max(lbl, axis=0, keepdims=True)
    e = jnp.exp(lbl - mx)
    return e[0:1] / jnp.sum(e, axis=0, keepdims=True)


def _cumsum_rows(g, rows):
    b = g
    sh = 1
    while sh < CH:
        b = b + jnp.where(rows >= sh, pltpu.roll(b, sh, axis=0), 0.0)
        sh *= 2
    return b


def _rev_cumsum_rows(g, rows):
    b = g
    sh = 1
    while sh < CH:
        b = b + jnp.where(rows < CH - sh, pltpu.roll(b, CH - sh, axis=0), 0.0)
        sh *= 2
    return b


SUB = CH // 2


def _direct_block(qb, kb, vb, bb, rows8):
    ob = jnp.zeros_like(qb)
    for s in range(SUB):
        e_s = jnp.exp(jnp.where(rows8 >= s, bb - bb[s:s + 1], -jnp.inf))
        ob = ob + jnp.sum(qb * e_s * kb[s:s + 1], axis=1, keepdims=True) * vb[s:s + 1]
    return ob


def _direct_block_bwd(qb, kb, vb, bb, dob, rows8, rowc8):
    dq = dk = dv = db = jnp.zeros_like(qb)
    for s in range(SUB):
        one = (rowc8 == s).astype(F32)
        ks, vs = kb[s:s + 1], vb[s:s + 1]
        e_s = jnp.exp(jnp.where(rows8 >= s, bb - bb[s:s + 1], -jnp.inf))
        qes = qb * e_s
        w = qes * ks
        a = jnp.sum(w, axis=1, keepdims=True)
        da = jnp.sum(dob * vs, axis=1, keepdims=True)
        dv = dv + one * jnp.sum(a * dob, axis=0, keepdims=True)
        dq = dq + da * e_s * ks
        dk = dk + one * jnp.sum(da * qes, axis=0, keepdims=True)
        u = da * w
        db = db + u - one * jnp.sum(u, axis=0, keepdims=True)
    return dq, dk, dv, db


def _cross_factors(q, k, b):
    ref = b[SUB - 1:SUB]
    e_hi, e_lo = jnp.exp(b[SUB:] - ref), jnp.exp(ref - b[:SUB])
    return q[SUB:] * e_hi, k[:SUB] * e_lo, e_hi, e_lo


def _intra_fwd(q, k, v, b, rows8):
    lo = _direct_block(q[:SUB], k[:SUB], v[:SUB], b[:SUB], rows8)
    hi = _direct_block(q[SUB:], k[SUB:], v[SUB:], b[SUB:], rows8)
    qe_hi, ke_lo, _, _ = _cross_factors(q, k, b)
    for s in range(SUB):
        hi = hi + jnp.sum(qe_hi * ke_lo[s:s + 1], axis=1, keepdims=True) * v[s:s + 1]
    return jnp.concatenate([lo, hi], axis=0)


def _intra_bwd(q, k, v, b, do, rows8, rowc8):
    dq_lo, dk_lo, dv_lo, db_lo = _direct_block_bwd(q[:SUB], k[:SUB], v[:SUB], b[:SUB], do[:SUB], rows8, rowc8)
    dq_hi, dk_hi, dv_hi, db_hi = _direct_block_bwd(q[SUB:], k[SUB:], v[SUB:], b[SUB:], do[SUB:], rows8, rowc8)
    qe_hi, ke_lo, e_hi, e_lo = _cross_factors(q, k, b)
    do_hi, v_lo = do[SUB:], v[:SUB]
    dqe = dke = jnp.zeros_like(qe_hi)
    for s in range(SUB):
        one = (rowc8 == s).astype(F32)
        a = jnp.sum(qe_hi * ke_lo[s:s + 1], axis=1, keepdims=True)
        da = jnp.sum(do_hi * v_lo[s:s + 1], axis=1, keepdims=True)
        dv_lo = dv_lo + one * jnp.sum(a * do_hi, axis=0, keepdims=True)
        dqe = dqe + da * ke_lo[s:s + 1]
        dke = dke + one * jnp.sum(da * qe_hi, axis=0, keepdims=True)
    u_hi, u_lo = dqe * qe_hi, dke * ke_lo
    d_ref = jnp.sum(u_lo, axis=0, keepdims=True) - jnp.sum(u_hi, axis=0, keepdims=True)
    db_lo = db_lo - u_lo + (rowc8 == SUB - 1).astype(F32) * d_ref
    cat = lambda lo, hi: jnp.concatenate([lo, hi], axis=0)
    return (cat(dq_lo, dq_hi + dqe * e_hi), cat(dk_lo + dke * e_lo, dk_hi), cat(dv_lo, dv_hi),
            cat(db_lo, db_hi + u_hi))


def _hgrn_fwd(z, lbl, nw, side=None):
    s_arrays, s_in_specs, s_shapes, s_out_specs, s_sems = _side_io(side)
    na, no = len(s_arrays), len(s_shapes)
    nsteps = NCH // HSTEP

    def body(hq_ref, hf_ref, hi_ref, hg_ref, lbl_ref, nw_ref, *refs):
        s_ins, (oraw_ref, og_ref, sh_ref) = refs[:na], refs[na:na + 3]
        s_outs, st_ref, s_sem_refs = refs[na + 3:na + 3 + no], refs[na + 3 + no], refs[na + 4 + no:]

        @pl.when(pl.program_id(0) == 0)
        def _():
            st_ref[...] = jnp.zeros_like(st_ref)
            if side is not None:
                side.first(s_ins, s_outs, s_sem_refs)

        lb_all = _lower_bound(lbl_ref[...])
        rows = lax.broadcasted_iota(jnp.int32, (CH, HK), 0)
        rows8 = lax.broadcasted_iota(jnp.int32, (SUB, HK), 0)
        nwv = nw_ref[...]
        for cc, h in [(cc, h) for cc in range(HSTEP) for h in range(HEADS)]:
            rs = slice(CH * cc, CH * (cc + 1))
            sl = slice(HK * h, HK * (h + 1))
            lb = lb_all[:, sl]
            hq, hf, v, hg = hq_ref[rs, sl], hf_ref[rs, sl], hi_ref[rs, sl], hg_ref[rs, sl]
            q = hq * _sigmoid(hq)
            f = lb + (1.0 - lb) * _sigmoid(hf)
            k = 1.0 - f
            b = _cumsum_rows(jnp.log(f), rows)
            sh_ref[cc, h] = st_ref[h]
            o = _bdot(q * jnp.exp(b), st_ref[h], _NT) + _intra_fwd(q, k, v, b, rows8)
            bl = b[CH - 1:CH]
            st_ref[h] = st_ref[h] * jnp.exp(bl)
            st_ref[h] += _bdot(v, k * jnp.exp(bl - b), _TN)
            oraw_ref[rs, sl] = o
            nrm = o * lax.rsqrt(jnp.mean(o * o, axis=1, keepdims=True) + EPS)
            og_ref[rs, sl] = (nrm * nwv * (hg * _sigmoid(hg))).astype(BF)

        if side is not None:
            @pl.when(pl.program_id(0) == nsteps // 2)
            def _():
                side.mid(s_ins, s_outs, s_sem_refs)

            @pl.when(pl.program_id(0) == nsteps - 1)
            def _():
                side.last(s_ins, s_outs, s_sem_refs)

    zblk = lambda c: pl.BlockSpec((CH * HSTEP, D), lambda i, c=c: (i, c))
    out = pl.pallas_call(
        body, name="hgrn_fwd", grid=(nsteps,),
        in_specs=[zblk(0), zblk(1), zblk(2), zblk(3),
                  pl.BlockSpec((2, D), lambda i: (0, 0)), pl.BlockSpec((1, HK), lambda i: (0, 0))] + s_in_specs,
        out_specs=[zblk(0), zblk(0),
                   pl.BlockSpec((HSTEP, HEADS, HK, HK), lambda i: (i, 0, 0, 0))] + s_out_specs,
        out_shape=[jax.ShapeDtypeStruct((T, D), F32), jax.ShapeDtypeStruct((T, D), BF),
                   jax.ShapeDtypeStruct((NCH, HEADS, HK, HK), F32)] + s_shapes,
        scratch_shapes=[pltpu.VMEM((HEADS, HK, HK), F32)] + s_sems,
        compiler_params=_params(("arbitrary",)),
    )(z, z, z, z, lbl, nw, *s_arrays)
    return out[0], out[1], out[2], out[3:]


def _hgrn_bwd(z, lbl, nw, oraw, dog, shist, side=None):
    hstep = 1
    s_arrays, s_in_specs, s_shapes, s_out_specs, s_sems = _side_io(side)
    na, no = len(s_arrays), len(s_shapes)

    def body(*refs):
        hq_ref, hf_ref, hi_ref, hg_ref, lbl_ref, nw_ref, oraw_ref, dog_ref, sh_ref = refs[:9]
        s_ins = refs[9:9 + na]
        dz_ref, dlb_ref, dnw_ref = refs[9 + na:12 + na]
        s_outs = refs[12 + na:12 + na + no]
        dst_ref = refs[12 + na + no]
        s_sem_refs = refs[13 + na + no:]

        @pl.when(pl.program_id(0) == 0)
        def _():
            dst_ref[...] = jnp.zeros_like(dst_ref)
            dlb_ref[...] = jnp.zeros_like(dlb_ref)
            dnw_ref[...] = jnp.zeros_like(dnw_ref)
            if side is not None:
                side.first(s_ins, s_outs, s_sem_refs)

        lb_all = _lower_bound(lbl_ref[...])
        rows = lax.broadcasted_iota(jnp.int32, (CH, HK), 0)
        rowc = lax.broadcasted_iota(jnp.int32, (CH, 1), 0)
        rows8 = lax.broadcasted_iota(jnp.int32, (SUB, HK), 0)
        rowc8 = lax.broadcasted_iota(jnp.int32, (SUB, 1), 0)
        nwv = nw_ref[...]
        dnw = jnp.zeros((1, HK), F32)
        for cc, h in [(cc, h) for cc in reversed(range(hstep)) for h in range(HEADS)]:
            rs = slice(CH * cc, CH * (cc + 1))
            sl = slice(HK * h, HK * (h + 1))
            lb = lb_all[:, sl]
            hq, hf, v, hg = hq_ref[rs, sl], hf_ref[rs, sl], hi_ref[rs, sl], hg_ref[rs, sl]
            o, dg_out = oraw_ref[rs, sl], dog_ref[rs, sl]
            sg = _sigmoid(hg)
            sil = hg * sg
            r = lax.rsqrt(jnp.mean(o * o, axis=1, keepdims=True) + EPS)
            nrm = o * r
            d_hg = dg_out * (nrm * nwv) * (sg * (1.0 + hg * (1.0 - sg)))
            dn = dg_out * nwv * sil
            dnw = dnw + jnp.sum(dg_out * nrm * sil, axis=0, keepdims=True)
            do = r * (dn - nrm * jnp.mean(dn * nrm, axis=1, keepdims=True))
            sq = _sigmoid(hq)
            q = hq * sq
            sig = _sigmoid(hf)
            f = lb + (1.0 - lb) * sig
            k = 1.0 - f
            b = _cumsum_rows(jnp.log(f), rows)
            eb = jnp.exp(b)
            qe = q * eb
            bl = b[CH - 1:CH]
            ebl = jnp.exp(bl)
            kdec = jnp.exp(bl - b)
            ke = k * kdec
            dqe = _bdot(do, sh_ref[cc, h])
            dq = dqe * eb
            db = dqe * qe
            dke = _bdot(v, dst_ref[h])
            dv = _bdot(ke, dst_ref[h], _NT)
            dk = dke * kdec
            rr = dke * ke
            db = db - rr
            db_last = (jnp.sum(rr, axis=0, keepdims=True)
                       + ebl * jnp.sum(dst_ref[h] * sh_ref[cc, h], axis=0, keepdims=True))
            dst_ref[h] = dst_ref[h] * ebl
            dst_ref[h] += _bdot(do, qe, _TN)
            dq_i, dk_i, dv_i, db_i = _intra_bwd(q, k, v, b, do, rows8, rowc8)
            dq, dk, dv = dq + dq_i, dk + dk_i, dv + dv_i
            db = db + db_i + (rowc == CH - 1).astype(F32) * db_last
            dgl = _rev_cumsum_rows(db, rows)
            df = dgl / f - dk
            dlb_ref[:, sl] += jnp.sum(df * (1.0 - sig), axis=0, keepdims=True)
            dz_ref[rs, sl] = (dq * (sq * (1.0 + hq * (1.0 - sq)))).astype(BF)
            dz_ref[rs, D + HK * h:D + HK * (h + 1)] = (df * (1.0 - lb) * sig * (1.0 - sig)).astype(BF)
            dz_ref[rs, 2 * D + HK * h:2 * D + HK * (h + 1)] = dv.astype(BF)
            dz_ref[rs, 3 * D + HK * h:3 * D + HK * (h + 1)] = d_hg.astype(BF)
        dnw_ref[...] += dnw
        if side is not None:
            @pl.when(pl.program_id(0) == NCH // hstep - 1)
            def _():
                side.last(s_ins, s_outs, s_sem_refs)

    rev = lambda i: NCH // hstep - 1 - i
    zblk = lambda c: pl.BlockSpec((CH * hstep, D), lambda i, c=c: (rev(i), c))
    out = pl.pallas_call(
        body, name="hgrn_bwd", grid=(NCH // hstep,),
        in_specs=[zblk(0), zblk(1), zblk(2), zblk(3),
                  pl.BlockSpec((2, D), lambda i: (0, 0)), pl.BlockSpec((1, HK), lambda i: (0, 0)),
                  zblk(0), zblk(0),
                  pl.BlockSpec((hstep, HEADS, HK, HK), lambda i: (rev(i), 0, 0, 0))] + s_in_specs,
        out_specs=[pl.BlockSpec((CH * hstep, 4 * D), lambda i: (rev(i), 0)),
                   pl.BlockSpec((1, D), lambda i: (0, 0)), pl.BlockSpec((1, HK), lambda i: (0, 0))] + s_out_specs,
        out_shape=[jax.ShapeDtypeStruct((T, 4 * D), BF), jax.ShapeDtypeStruct((1, D), F32),
                   jax.ShapeDtypeStruct((1, HK), F32)] + s_shapes,
        scratch_shapes=[pltpu.VMEM((HEADS, HK, HK), F32)] + s_sems,
        compiler_params=_params(("arbitrary",)),
    )(z, z, z, z, lbl, nw, oraw, dog, shist, *s_arrays)
    return out[0], out[1], out[2], out[3:]


BLK = 128
NBLK = T // BLK
QK_SCALE = 0.125


def _head_masks():
    lane = lax.broadcasted_iota(jnp.int32, (1, BLK), 1)
    return [(lane < 64).astype(F32), (lane >= 64).astype(F32)]


def _pieces(dil):
    m = T // dil
    out = []
    for r in range(dil):
        for j in range(m // BLK):
            start = r + dil * BLK * j
            rows = pl.ds(start, BLK, stride=dil) if dil > 1 else pl.ds(start, BLK)
            out.append((rows, r * m + BLK * j))
    return out


def _rope(x, c, sa, sb):
    return x * c + pltpu.roll(x, 96, axis=1) * sa + pltpu.roll(x, 32, axis=1) * sb


def _rope_t(d, c, sa, sb):
    return d * c + pltpu.roll(d * sa, 32, axis=1) + pltpu.roll(d * sb, 96, axis=1)


def _rope_and_regroup(dil, q_ref, k_ref, v_ref, tables, stage_q, stage_k, qr_ref, kr_ref, vr_ref):
    cos_ref, sa_ref, sb_ref = tables
    to_q, to_k = (qr_ref, kr_ref) if dil == 1 else (stage_q, stage_k)
    for c in range(T // BLK):
        rows = pl.ds(BLK * c, BLK)
        cs, sa, sb = cos_ref[rows, :], sa_ref[rows, :], sb_ref[rows, :]
        to_q[rows, :] = (_rope(q_ref[rows, :], cs, sa, sb) * QK_SCALE).astype(to_q.dtype)
        to_k[rows, :] = _rope(k_ref[rows, :], cs, sa, sb).astype(to_k.dtype)
    for rows, dst in _pieces(dil):
        drows = pl.ds(dst, BLK)
        if dil > 1:
            qr_ref[drows, :] = stage_q[rows, :].astype(qr_ref.dtype)
            kr_ref[drows, :] = stage_k[rows, :].astype(kr_ref.dtype)
        vr_ref[drows, :] = v_ref[rows, :].astype(vr_ref.dtype)


def _window_bias(bias_ref):
    ii = lax.broadcasted_iota(jnp.int32, (2 * BLK, BLK), 0) % BLK
    jj = lax.broadcasted_iota(jnp.int32, (2 * BLK, BLK), 1)
    bias_ref[0] = jnp.where(jj <= ii, 0.0, -jnp.inf)
    bias_ref[1] = jnp.where(jj >= ii, 0.0, -jnp.inf)


def _blocks(bi):
    if isinstance(bi, int):
        return pl.ds(bi * BLK, BLK), pl.ds(max(bi - 1, 0) * BLK, BLK)
    return (pl.ds(pl.multiple_of(bi * BLK, BLK), BLK),
            pl.ds(pl.multiple_of(jnp.maximum(bi - 1, 0) * BLK, BLK), BLK))


def _stack_heads(x, masks):
    return jnp.concatenate([x * masks[0].astype(x.dtype), x * masks[1].astype(x.dtype)], axis=0).astype(BF)


def _attn_fwd(z, cos, sa, sb):
    def body(q_ref, k_ref, v_ref, ag_ref, cos_ref, sa_ref, sb_ref, ob_ref, opre_ref, lse_ref, qr_ref, kr_ref, vr_ref,
             bias_ref, og_ref, lg_ref, otok_ref, ltok_ref, sc_ref):
        g = pl.program_id(1)
        masks = _head_masks()

        @pl.when(g == 0)
        def _():
            _window_bias(bias_ref)

        def group(gi):
            dil = ATT_GROUPS[gi][1]
            nblk = (T // dil) // BLK
            _rope_and_regroup(dil, q_ref, k_ref, v_ref, (cos_ref, sa_ref, sb_ref), lg_ref.at[0], lg_ref.at[1],
                              qr_ref, kr_ref, vr_ref)

            def scores(bi, slot):
                cur, prev = _blocks(bi)
                q2 = _stack_heads(qr_ref[cur, :], masks)
                sc_ref[slot, 0] = _dot(q2, kr_ref[cur, :], _NT) + bias_ref[0]
                if nblk > 1:
                    sc_ref[slot, 1] = (_dot(q2, kr_ref[prev, :], _NT)
                                       + (bias_ref[1] + jnp.where((bi % nblk) != 0, 0.0, -jnp.inf)))

            def finish(bi, slot):
                cur, prev = _blocks(bi)
                s_c, vc = sc_ref[slot, 0], vr_ref[cur, :]
                if nblk > 1:
                    s_p, vp = sc_ref[slot, 1], vr_ref[prev, :]
                    mx = jnp.max(jnp.maximum(s_c, s_p), axis=1, keepdims=True)
                    p_c, p_p = jnp.exp(s_c - mx), jnp.exp(s_p - mx)
                    den = jnp.sum(p_c + p_p, axis=1, keepdims=True)
                    oh = _dot(p_c.astype(BF), vc) + _dot(p_p.astype(BF), vp)
                else:
                    mx = jnp.max(s_c, axis=1, keepdims=True)
                    p_c = jnp.exp(s_c - mx)
                    den = jnp.sum(p_c, axis=1, keepdims=True)
                    oh = _dot(p_c.astype(BF), vc)
                on = oh / den
                lsev = jnp.broadcast_to(mx + jnp.log(den), (2 * BLK, BLK))
                og_ref[cur, :] = on[:BLK] * masks[0] + on[BLK:] * masks[1]
                lg_ref[0, cur, :] = lsev[:BLK]
                lg_ref[1, cur, :] = lsev[BLK:]

            def pair(j, carry):
                finish(2 * j, 0)
                scores(2 * j + 1, 1)
                finish(2 * j + 1, 1)
                scores(jnp.minimum(2 * j + 2, NBLK - 1), 0)
                return carry

            scores(0, 0)
            lax.fori_loop(0, NBLK // 2, pair, 0)
            for rows, src in _pieces(dil):
                srows = pl.ds(src, BLK)
                otok_ref[gi, rows, :] = og_ref[srows, :]
                ltok_ref[gi, 0, rows, :] = lg_ref[0, srows, :]
                ltok_ref[gi, 1, rows, :] = lg_ref[1, srows, :]

        for gi in range(3):
            pl.when(g == gi)(functools.partial(group, gi))

        @pl.when(g == 2)
        def _():
            for c in range(T // BLK):
                rows = pl.ds(BLK * c, BLK)
                wts = []
                for hh in range(2):
                    l0, l1, l2 = ltok_ref[0, hh, rows, :], ltok_ref[1, hh, rows, :], ltok_ref[2, hh, rows, :]
                    mx = jnp.maximum(jnp.maximum(l0, l1), l2)
                    e0, e1, e2 = jnp.exp(l0 - mx), jnp.exp(l1 - mx), jnp.exp(l2 - mx)
                    tot = e0 + e1 + e2
                    lse_ref[rows, BLK * hh:BLK * (hh + 1)] = mx + jnp.log(tot)
                    inv = 1.0 / tot
                    wts.append([e0 * inv, e1 * inv, e2 * inv])
                o = sum((wts[0][gi] * masks[0] + wts[1][gi] * masks[1]) * otok_ref[gi, rows, :] for gi in range(3))
                ag = ag_ref[rows, :]
                opre_ref[rows, :] = o
                ob_ref[rows, :] = (o * (ag * _sigmoid(ag))).astype(BF)

    c0 = ATT_COL0 // BLK
    zspec = lambda part: pl.BlockSpec((T, BLK), lambda p, g, part=part: (0, c0 + 12 * part + 4 * g + p))
    outspec = pl.BlockSpec((T, BLK), lambda p, g: (0, p))
    table = pl.BlockSpec((T, BLK), lambda p, g: (0, 0))
    regrouped = pl.BlockSpec((None, T, BLK), lambda p, g: (g, 0, p))
    big = lambda: pltpu.VMEM((T, BLK), F32)
    return pl.pallas_call(
        body, name="attn_fwd", grid=(4, 3),
        in_specs=[zspec(0), zspec(1), zspec(2),
                  pl.BlockSpec((T, BLK), lambda p, g: (0, AG_COL0 // BLK + p)), table, table, table],
        out_specs=[outspec, outspec, pl.BlockSpec((T, 2 * BLK), lambda p, g: (0, p)), regrouped, regrouped, regrouped],
        out_shape=[jax.ShapeDtypeStruct((T, 512), BF), jax.ShapeDtypeStruct((T, 512), F32),
                   jax.ShapeDtypeStruct((T, 8 * BLK), F32)] + [jax.ShapeDtypeStruct((3, T, 512), BF)] * 3,
        scratch_shapes=[pltpu.VMEM((2, 2 * BLK, BLK), F32), big(),
                        pltpu.VMEM((2, T, BLK), F32), pltpu.VMEM((3, T, BLK), F32), pltpu.VMEM((3, 2, T, BLK), F32),
                        pltpu.VMEM((2, 2, 2 * BLK, BLK), F32)],
        compiler_params=_params(("parallel", "arbitrary")),
    )(z, z, z, z, cos, sa, sb)


def _attn_bwd(z, qs, ks, vs, cos, sa, sb, opre, lse, dob):
    def body(qs_ref, ks_ref, vs_ref, ag_ref, cos_ref, sa_ref, sb_ref, o_ref, lse0_ref, lse1_ref, dob_ref,
             dq_ref, dk_ref, dv_ref, dag_ref,
             bias_ref, dtok_ref, qr_ref, kr_ref, vr_ref, dor_ref, lr_ref, dr_ref,
             dqr_ref, dkr_ref, dvr_ref, pd_ref, dotok_ref):
        g = pl.program_id(1)
        masks = _head_masks()

        @pl.when(g == 0)
        def _():
            _window_bias(bias_ref)
            for c in range(T // BLK):
                rows = pl.ds(BLK * c, BLK)
                ag, dob_v, o = ag_ref[rows, :], dob_ref[rows, :], o_ref[rows, :]
                sg = _sigmoid(ag)
                dag_ref[rows, :] = (dob_v * o * (sg * (1.0 + ag * (1.0 - sg)))).astype(BF)
                do = dob_v * (ag * sg)
                dotok_ref[rows, :] = do
                prod = do * o
                for hh, mh in enumerate(masks):
                    dtok_ref[hh, rows, :] = jnp.broadcast_to(jnp.sum(prod * mh, axis=1, keepdims=True), (BLK, BLK))

        def group(gi):
            dil = ATT_GROUPS[gi][1]
            nblk = (T // dil) // BLK
            for rows, dst in _pieces(dil):
                drows = pl.ds(dst, BLK)
                dor_ref[drows, :] = dotok_ref[rows, :]
                for hh, lse_ref in enumerate((lse0_ref, lse1_ref)):
                    lr_ref[hh, drows, :] = lse_ref[rows, :]
                    dr_ref[hh, drows, :] = dtok_ref[hh, rows, :]
            dkr_ref[...] = jnp.zeros_like(dkr_ref)
            dvr_ref[...] = jnp.zeros_like(dvr_ref)

            def probs(bi, slot):
                cur, prev = _blocks(bi)
                q2, do2 = _stack_heads(qs_ref[cur, :], masks), _stack_heads(dor_ref[cur, :], masks)
                lh = jnp.concatenate([lr_ref[0, cur, :], lr_ref[1, cur, :]], axis=0)
                dh = jnp.concatenate([dr_ref[0, cur, :], dr_ref[1, cur, :]], axis=0)
                p_c = jnp.exp(_dot(q2, ks_ref[cur, :], _NT) + bias_ref[0] - lh)
                pd_ref[slot, 0] = p_c.astype(BF)
                pd_ref[slot, 1] = (p_c * (_dot(do2, vs_ref[cur, :], _NT) - dh)).astype(BF)
                if nblk > 1:
                    bias_p = bias_ref[1] + jnp.where((bi % nblk) != 0, 0.0, -jnp.inf)
                    p_p = jnp.exp(_dot(q2, ks_ref[prev, :], _NT) + bias_p - lh)
                    pd_ref[slot, 2] = p_p.astype(BF)
                    pd_ref[slot, 3] = (p_p * (_dot(do2, vs_ref[prev, :], _NT) - dh)).astype(BF)

            def grads(bi, slot):
                cur, prev = _blocks(bi)
                q2, do2 = _stack_heads(qs_ref[cur, :], masks), _stack_heads(dor_ref[cur, :], masks)
                p_c, ds_c = pd_ref[slot, 0], pd_ref[slot, 1]
                dq2 = _dot(ds_c, ks_ref[cur, :])
                dkr_ref[cur, :] += _dot(ds_c, q2, _TN)
                dvr_ref[cur, :] += _dot(p_c, do2, _TN)
                if nblk > 1:
                    p_p, ds_p = pd_ref[slot, 2], pd_ref[slot, 3]
                    dq2 = dq2 + _dot(ds_p, ks_ref[prev, :])
                    dkr_ref[prev, :] += _dot(ds_p, q2, _TN)
                    dvr_ref[prev, :] += _dot(p_p, do2, _TN)
                dqr_ref[cur, :] = dq2[:BLK] * masks[0] + dq2[BLK:] * masks[1]

            def pair(j, carry):
                grads(2 * j, 0)
                probs(2 * j + 1, 1)
                grads(2 * j + 1, 1)
                probs(jnp.minimum(2 * j + 2, NBLK - 1), 0)
                return carry

            probs(0, 0)
            lax.fori_loop(0, NBLK // 2, pair, 0)
            if dil > 1:
                for rows, src in _pieces(dil):
                    srows = pl.ds(src, BLK)
                    qr_ref[rows, :] = dqr_ref[srows, :]
                    kr_ref[rows, :] = dkr_ref[srows, :]
                    vr_ref[rows, :] = dvr_ref[srows, :]
            tq, tk, tv = (qr_ref, kr_ref, vr_ref) if dil > 1 else (dqr_ref, dkr_ref, dvr_ref)
            for c in range(T // BLK):
                rows = pl.ds(BLK * c, BLK)
                cs, sa, sb = cos_ref[rows, :], sa_ref[rows, :], sb_ref[rows, :]
                dq_ref[rows, :] = _rope_t(tq[rows, :] * QK_SCALE, cs, sa, sb).astype(BF)
                dk_ref[rows, :] = _rope_t(tk[rows, :], cs, sa, sb).astype(BF)
                dv_ref[rows, :] = tv[rows, :].astype(BF)

        for gi in range(3):
            pl.when(g == gi)(functools.partial(group, gi))

    regrouped = pl.BlockSpec((None, T, BLK), lambda p, g: (g, 0, p))
    pspec = pl.BlockSpec((T, BLK), lambda p, g: (0, p))
    gspec = pl.BlockSpec((T, BLK), lambda p, g: (0, 4 * g + p))
    table = pl.BlockSpec((T, BLK), lambda p, g: (0, 0))
    big = lambda: pltpu.VMEM((T, BLK), F32)
    two = lambda: pltpu.VMEM((2, T, BLK), F32)
    return pl.pallas_call(
        body, name="attn_bwd", grid=(4, 3),
        in_specs=[regrouped, regrouped, regrouped,
                  pl.BlockSpec((T, BLK), lambda p, g: (0, AG_COL0 // BLK + p)), table, table, table,
                  pspec, pl.BlockSpec((T, BLK), lambda p, g: (0, 2 * p)),
                  pl.BlockSpec((T, BLK), lambda p, g: (0, 2 * p + 1)), pspec],
        out_specs=[gspec, gspec, gspec, pspec],
        out_shape=[jax.ShapeDtypeStruct((T, 1536), BF), jax.ShapeDtypeStruct((T, 1536), BF),
                   jax.ShapeDtypeStruct((T, 1536), BF), jax.ShapeDtypeStruct((T, 512), BF)],
        scratch_shapes=[pltpu.VMEM((2, 2 * BLK, BLK), F32), two(), big(), big(), big(), big(),
                        two(), two(), big(), big(), big(), pltpu.VMEM((2, 4, 2 * BLK, BLK), BF), big()],
        compiler_params=_params(("parallel", "arbitrary")),
    )(qs, ks, vs, z, cos, sa, sb, opre, lse, lse, dob)


def _merge_out_loss(og, ob, z, w_a, w_b, w_out, x, tgt, wf):
    tm = 512

    def body(og_ref, ob_ref, ga_ref, gb_ref, wa_ref, wb_ref, wo_ref, x_ref, t_ref, wf_ref,
             ya_ref, yb_ref, m_ref, dout_ref, loss_ref, gwf_ref):
        @pl.when(pl.program_id(0) == 0)
        def _():
            loss_ref[...] = jnp.zeros_like(loss_ref)
            gwf_ref[...] = jnp.zeros_like(gwf_ref)

        ya, yb = _dot(og_ref[...], wa_ref[...]), _dot(ob_ref[...], wb_ref[...])
        ya_ref[...] = ya
        yb_ref[...] = yb
        m = (_sigmoid(ga_ref[...]) * ya + _sigmoid(gb_ref[...]) * yb).astype(BF)
        m_ref[...] = m
        out = x_ref[...] + _dot(m, wo_ref[...])
        r = lax.rsqrt(jnp.mean(out * out, axis=-1, keepdims=True) + EPS)
        yh = out * r
        wfv = wf_ref[...]
        err = yh * wfv - t_ref[...]
        loss_ref[...] += jnp.sum(err * err, axis=0, keepdims=True) * (0.5 / D)
        dy = err * (1.0 / D)
        gwf_ref[...] += jnp.sum(dy * yh, axis=0, keepdims=True)
        dyh = dy * wfv
        dout_ref[...] = r * (dyh - yh * jnp.mean(dyh * yh, axis=-1, keepdims=True))

    row = pl.BlockSpec((tm, D), lambda i: (i, 0))
    vec = pl.BlockSpec((1, D), lambda i: (0, 0))
    whole = lambda w: pl.BlockSpec(w.shape, lambda i: (0, 0))
    return pl.pallas_call(
        body, name="merge_out_loss", grid=(T // tm,),
        in_specs=[row, pl.BlockSpec((tm, ob.shape[1]), lambda i: (i, 0)),
                  pl.BlockSpec((tm, D), lambda i: (i, GATE_COL0 // D)),
                  pl.BlockSpec((tm, D), lambda i: (i, GATE_COL0 // D + 1)),
                  whole(w_a), whole(w_b), whole(w_out), row, row, vec],
        out_specs=[row, row, row, row, vec, vec],
        out_shape=[jax.ShapeDtypeStruct((T, D), F32), jax.ShapeDtypeStruct((T, D), F32),
                   jax.ShapeDtypeStruct((T, D), BF), jax.ShapeDtypeStruct((T, D), F32),
                   jax.ShapeDtypeStruct((1, D), F32), jax.ShapeDtypeStruct((1, D), F32)],
        compiler_params=_params(("arbitrary",)),
    )(og, ob, z, z, w_a, w_b, w_out, x, tgt, wf)


def _merge_proj_bwd(dout, ya, yb, z, w_a, w_b, w_out):
    tm = 512

    def body(dout_ref, ya_ref, yb_ref, ga_ref, gb_ref, wa_ref, wb_ref, wo_ref,
             dya_ref, dyb_ref, dg_ref, dog_ref, dob_ref):
        dmv = _dot(dout_ref[...].astype(BF), wo_ref[...], _NT)
        sa, sb = _sigmoid(ga_ref[...]), _sigmoid(gb_ref[...])
        dya, dyb = (sa * dmv).astype(BF), (sb * dmv).astype(BF)
        dya_ref[...] = dya
        dyb_ref[...] = dyb
        dg_ref[:, :D] = (dmv * ya_ref[...] * sa * (1.0 - sa)).astype(BF)
        dg_ref[:, D:] = (dmv * yb_ref[...] * sb * (1.0 - sb)).astype(BF)
        dog_ref[...] = _dot(dya, wa_ref[...], _NT)
        dob_ref[...] = _dot(dyb, wb_ref[...], _NT)

    row = pl.BlockSpec((tm, D), lambda i: (i, 0))
    whole = lambda w: pl.BlockSpec(w.shape, lambda i: (0, 0))
    nb = w_b.shape[0]
    return pl.pallas_call(
        body, name="merge_proj_bwd", grid=(T // tm,),
        in_specs=[row, row, row, pl.BlockSpec((tm, D), lambda i: (i, GATE_COL0 // D)),
                  pl.BlockSpec((tm, D), lambda i: (i, GATE_COL0 // D + 1)), whole(w_a), whole(w_b), whole(w_out)],
        out_specs=[row, row, pl.BlockSpec((tm, 2 * D), lambda i: (i, 0)), row,
                   pl.BlockSpec((tm, nb), lambda i: (i, 0))],
        out_shape=[jax.ShapeDtypeStruct((T, D), BF), jax.ShapeDtypeStruct((T, D), BF),
                   jax.ShapeDtypeStruct((T, 2 * D), BF), jax.ShapeDtypeStruct((T, D), F32),
                   jax.ShapeDtypeStruct((T, nb), F32)],
        compiler_params=_params(("parallel",)),
    )(dout, ya, yb, z, z, w_a, w_b, w_out)


def _rope_inv_freq():
    inv = ROPE_THETA ** (-jnp.arange(0, 64, 2, dtype=F32) / 64)
    return jnp.tile(inv, 4).reshape(1, BLK)


def _local_step(x, pos, norm_w, lbl, hnw, wf, tgt, w_in, w_a, w_b, w_out, shard_shapes=()):
    invf = _rope_inv_freq()
    if shard_shapes:
        blk = jnp.reshape(2 * lax.axis_index("x") + lax.axis_index("y"), (1,)).astype(jnp.int32)
        h, cos, sa, sb, z_own, (w_near,) = _norm_and_rope_tables(
            x, norm_w, pos, invf, side=_gather_near_side(w_in, WEIGHT_AXES[0]), own=(w_in, blk))
        near = jnp.concatenate([blk ^ 2, blk ^ 1])
        z, (w_diag,) = _z_blocks(h, w_near, z_own, jnp.concatenate([near, near]), 2, name="z_proj_near",
                                 side=_gather_diag_side(w_near, w_in.shape, WEIGHT_AXES[0]))
        z, w_in, _ = _z_blocks(h, w_diag, z, jnp.concatenate([blk ^ 3, jnp.zeros_like(blk)]), 1, name="z_proj_diag",
                               fill=w_near, side=None)
        oraw, og, shist, (w_a, w_b, w_out) = _hgrn_fwd(
            z, lbl, hnw, side=_gather_side([w_a, w_b, w_out], WEIGHT_AXES[1:]))
    else:
        h, cos, sa, sb, _, _ = _norm_and_rope_tables(x, norm_w, pos, invf)
        z = _matmul(h, w_in, tm=T, tn=512, name="z_proj")
        oraw, og, shist, _ = _hgrn_fwd(z, lbl, hnw)
    ob, opre, lse, qs, ks, vs = _attn_fwd(z, cos, sa, sb)
    ya, yb, merged, dout, loss_vec, g_wf = _merge_out_loss(og, ob, z, w_a, w_b, w_out, x, tgt, wf)

    dya, dyb, dgates, dog, dob = _merge_proj_bwd(dout, ya, yb, z, w_a, w_b, w_out)
    g_wout = _matmul(merged, dout, ta=True, out_dtype=BF, tm=512, tn=1024, name="g_wout")
    g_wa = _matmul(og, dya, ta=True, out_dtype=BF, tm=512, tn=1024, name="g_wa")
    g_wb = _matmul(ob, dyb, ta=True, out_dtype=BF, tm=512, tn=1024, name="g_wb")
    small = [g_wa, g_wb, g_wout]
    side_s = side_w = None
    if shard_shapes:
        p3_s = _rs_partials(small, shard_shapes[1:], WEIGHT_AXES[1:], "small")
        side_s = _chip_exchange_side(p3_s, shard_shapes[1:], WEIGHT_AXES[1:])
    dz_h, dlb, g_hnw, land_s = _hgrn_bwd(z, lbl, hnw, oraw, dog, shist, side=side_s)
    dq, dk, dv, dag = _attn_bwd(z, qs, ks, vs, cos, sa, sb, opre, lse, dob)
    dz_parts = [dz_h, dq, dk, dv, dag, dgates]
    if shard_shapes:
        c = lax.axis_index("c")
        half = lambda i: jnp.reshape(i, (1,)).astype(jnp.int32)
        g_send = _grad_w_in_half(h, dz_parts, half(1 - c))
        g_keep, (g_sib,) = _grad_w_in_half(h, dz_parts, half(c), side=_sibling_send_side(g_send))
        p3_w = [_add_bf16(g_keep, g_sib, "pair_sum_w_in").reshape(1, D // 2, NIN)]
        side_w = _chip_exchange_relay_side(p3_w[0], shard_shapes[0])
    else:
        g_big = [_grad_w_in(h, dz_parts)] + small
    gx, g_nw, land_w = _grad_x(dz_parts, w_in, x, dout, norm_w, side=side_w)
    small_sums = None
    if shard_shapes:
        g_big, small_sums = _rs_finish(p3_w + p3_s, [land_w[0]] + list(land_s), shard_shapes, WEIGHT_AXES,
                                       (g_nw, dlb, g_hnw, g_wf, loss_vec))
    return dict(loss_vec=loss_vec, gx=gx, g_nw=g_nw, dlb=dlb, g_hnw=g_hnw, g_wf=g_wf, small_sums=small_sums,
                g_win=g_big[0], g_wa=g_big[1], g_wb=g_big[2], g_wout=g_big[3])


MESH = pl.DeviceIdType.MESH
HBM = pl.BlockSpec(memory_space=pl.ANY)
WEIGHT_AXES = (1, 0, 1, 0)


def _place():
    x, y, c = lax.axis_index("x"), lax.axis_index("y"), lax.axis_index("c")
    chips = [(1 - x, y), (x, 1 - y), (1 - x, 1 - y)]
    return x, y, c, chips


def _block_half(ref, shard_shape, axis, j, half):
    r, c = shard_shape
    hr = r // 2
    if axis == 0:
        return ref.at[pl.ds(pl.multiple_of(j * r + half * hr, 16), hr), :]
    return ref.at[pl.ds(pl.multiple_of(half * hr, 16), hr), pl.ds(pl.multiple_of(j * c, 128), c)]


class _Side:
    def __init__(self, arrays, out_shapes, sems, first, last, mid=None):
        self.arrays, self.out_shapes, self.sems, self.first, self.last = arrays, out_shapes, sems, first, last
        self.mid = mid


def _gather_side(shards, axes):
    n = len(shards)
    shapes = [s.shape for s in shards]

    def copies(ins, outs, sems):
        send1, recv1, send2, recv2, send0, recv0 = sems
        x, y, c, chips = _place()
        me = 2 * x + y
        sib = (x, y, 1 - c)
        near = ((1 - c) * (1 - x) + c * x, (1 - c) * y + c * (1 - y))
        far = ((1 - c) * x + c * (1 - x), (1 - c) * (1 - y) + c * y)
        out = []
        for a in range(n):
            r, cc = shapes[a]
            mine = (outs[a].at[pl.ds(pl.multiple_of(me * r, 16), r), :] if axes[a] == 0
                    else outs[a].at[:, pl.ds(pl.multiple_of(me * cc, 128), cc)])
            own = pltpu.make_async_remote_copy(
                src_ref=ins[a], dst_ref=mine, send_sem=send0.at[a], recv_sem=recv0.at[a],
                device_id=sib, device_id_type=MESH)
            src = ins[a].at[pl.ds(pl.multiple_of(c * (r // 2), 16), r // 2), :]
            sends = [pltpu.make_async_remote_copy(
                src_ref=src, dst_ref=_block_half(outs[a], shapes[a], axes[a], me, c),
                send_sem=send1.at[a, k], recv_sem=recv1.at[a, k], device_id=(*chips[k], c), device_id_type=MESH)
                for k in range(2)]

            def region(chip, half):
                return _block_half(outs[a], shapes[a], axes[a], 2 * chip[0] + chip[1], half)

            def arrival(chip, k):
                reg = region(chip, c)
                return pltpu.make_async_remote_copy(
                    src_ref=reg, dst_ref=reg, send_sem=send1.at[a, k], recv_sem=recv1.at[a, k],
                    device_id=(*chip, c), device_id_type=MESH)

            def to_sibling(chip, k):
                reg = region(chip, c)
                return pltpu.make_async_remote_copy(
                    src_ref=reg, dst_ref=reg, send_sem=send2.at[a, k], recv_sem=recv2.at[a, k],
                    device_id=sib, device_id_type=MESH)

            def from_sibling(chip, k):
                reg = region(chip, 1 - c)
                return pltpu.make_async_remote_copy(
                    src_ref=reg, dst_ref=reg, send_sem=send2.at[a, k], recv_sem=recv2.at[a, k],
                    device_id=sib, device_id_type=MESH)

            relay = pltpu.make_async_remote_copy(
                src_ref=region(near, c), dst_ref=region(near, c), send_sem=send1.at[a, 2], recv_sem=recv1.at[a, 2],
                device_id=(*far, c), device_id_type=MESH)
            hops = [(arrival(near, c), to_sibling(near, c)), (arrival(far, 1 - c), to_sibling(far, 1 - c)),
                    (arrival(chips[2], 2), to_sibling(chips[2], 2))]
            back = [from_sibling(chips[k], k) for k in range(3)]
            out.append((own, sends, relay, hops, back))
        return out

    def first(ins, outs, sems):
        for own, sends, _, _, _ in copies(ins, outs, sems):
            own.start()
            for cp in sends:
                cp.start()

    def mid(ins, outs, sems):
        per_array = copies(ins, outs, sems)
        for step in range(2):
            for _, _, relay, hops, _ in per_array:
                arrived, onward = hops[step]
                arrived.wait_recv()
                if step == 0:
                    relay.start()
                onward.start()

    def last(ins, outs, sems):
        per_array = copies(ins, outs, sems)
        for _, _, _, hops, _ in per_array:
            arrived, onward = hops[2]
            arrived.wait_recv()
            onward.start()
        for own, sends, relay, hops, back in per_array:
            for cp in back:
                cp.wait_recv()
            for cp in sends + [relay] + [onward for _, onward in hops]:
                cp.wait_send()
            own.wait()

    full = [(4 * r, c) if ax == 0 else (r, 4 * c) for (r, c), ax in zip(shapes, axes)]
    sems = [pltpu.SemaphoreType.DMA((n, 3)), pltpu.SemaphoreType.DMA((n, 3)),
            pltpu.SemaphoreType.DMA((n, 3)), pltpu.SemaphoreType.DMA((n, 3)),
            pltpu.SemaphoreType.DMA((n,)), pltpu.SemaphoreType.DMA((n,))]
    return _Side(list(shards), [jax.ShapeDtypeStruct(f, BF) for f in full], sems, first, last, mid)


def _gather_near_side(shard, axis):
    shape = shard.shape
    r, cc = shape

    def copies(ins, outs, sems):
        send1, recv1, send2, recv2, send0, recv0 = sems
        x, y, c, chips = _place()
        me = 2 * x + y
        sib = (x, y, 1 - c)
        mine = (outs[0].at[pl.ds(pl.multiple_of(me * r, 16), r), :] if axis == 0
                else outs[0].at[:, pl.ds(pl.multiple_of(me * cc, 128), cc)])
        own = pltpu.make_async_remote_copy(
            src_ref=ins[0], dst_ref=mine, send_sem=send0.at[0], recv_sem=recv0.at[0],
            device_id=sib, device_id_type=MESH)
        src = ins[0].at[pl.ds(pl.multiple_of(c * (r // 2), 16), r // 2), :]

        def region(k, half):
            return _block_half(outs[0], shape, axis, 2 * chips[k][0] + chips[k][1], half)

        def moves(k):
            return [pltpu.make_async_remote_copy(
                        src_ref=s, dst_ref=d, send_sem=ss.at[k], recv_sem=rs.at[k], device_id=dev,
                        device_id_type=MESH)
                    for s, d, ss, rs, dev in (
                        (src, _block_half(outs[0], shape, axis, me, c), send1, recv1, (*chips[k], c)),
                        (region(k, c), region(k, c), send1, recv1, (*chips[k], c)),
                        (region(k, c), region(k, c), send2, recv2, sib),
                        (region(k, 1 - c), region(k, 1 - c), send2, recv2, sib))]

        return own, [moves(k) for k in range(2)]

    def first(ins, outs, sems):
        own, per_chip = copies(ins, outs, sems)
        own.start()
        for send, _, _, _ in per_chip:
            send.start()

    def last(ins, outs, sems):
        own, per_chip = copies(ins, outs, sems)
        for _, arrived, onward, _ in per_chip:
            arrived.wait_recv()
            onward.start()
        for send, _, onward, back in per_chip:
            back.wait_recv()
            send.wait_send()
            onward.wait_send()
        own.wait()

    full = (4 * r, cc) if axis == 0 else (r, 4 * cc)
    sems = [pltpu.SemaphoreType.DMA((2,))] * 4 + [pltpu.SemaphoreType.DMA((1,))] * 2
    return _Side([shard], [jax.ShapeDtypeStruct(full, BF)], sems, first, last)


def _gather_diag_side(gathered, shape, axis):
    r, cc = shape

    def copies(ins, outs, sems):
        send1, recv1, send2, recv2 = sems
        x, y, c, _ = _place()
        sib = (x, y, 1 - c)
        near = ((1 - c) * (1 - x) + c * x, (1 - c) * y + c * (1 - y))
        far = ((1 - c) * x + c * (1 - x), (1 - c) * (1 - y) + c * y)

        def half(i):
            return outs[0].at[pl.ds(pl.multiple_of(i * (r // 2), 16), r // 2), :]

        def move(s, d, ss, rs, dev):
            return pltpu.make_async_remote_copy(
                src_ref=s, dst_ref=d, send_sem=ss.at[0], recv_sem=rs.at[0], device_id=dev, device_id_type=MESH)

        relay = move(_block_half(ins[0], shape, axis, 2 * near[0] + near[1], c), half(c), send1, recv1, (*far, c))
        arrived = move(half(c), half(c), send1, recv1, (*far, c))
        onward = move(half(c), half(c), send2, recv2, sib)
        back = move(half(1 - c), half(1 - c), send2, recv2, sib)
        return relay, arrived, onward, back

    def first(ins, outs, sems):
        copies(ins, outs, sems)[0].start()

    def last(ins, outs, sems):
        relay, arrived, onward, back = copies(ins, outs, sems)
        arrived.wait_recv()
        onward.start()
        back.wait_recv()
        relay.wait_send()
        onward.wait_send()

    return _Side([gathered], [jax.ShapeDtypeStruct(shape, BF)], [pltpu.SemaphoreType.DMA((1,))] * 4, first, last)


def _as3d(g, shard_shape, axis):
    r, c = shard_shape
    return g.reshape(4, r, c) if axis == 0 else g.reshape(1, r, 4 * c)


def _half_rows(ref3, hr, half):
    return ref3.at[:, pl.ds(pl.multiple_of(half * hr, 16), hr), :]


def _rs_pair_exchange(g3s, name):
    n = len(g3s)

    def body(*refs):
        ins, outs = refs[:n], refs[n:2 * n]
        send, recv = refs[2 * n:]
        x, y, c, _ = _place()
        cps = []
        for a in range(n):
            hr = g3s[a].shape[1] // 2
            cp = pltpu.make_async_remote_copy(
                src_ref=_half_rows(ins[a], hr, 1 - c), dst_ref=outs[a],
                send_sem=send.at[a], recv_sem=recv.at[a], device_id=(x, y, 1 - c), device_id_type=MESH)
            cp.start()
            cps.append(cp)
        for cp in cps:
            cp.wait()

    return pl.pallas_call(
        body, name=name,
        in_specs=[HBM] * n, out_specs=[HBM] * n,
        out_shape=[jax.ShapeDtypeStruct((g.shape[0], g.shape[1] // 2, g.shape[2]), BF) for g in g3s],
        scratch_shapes=[pltpu.SemaphoreType.DMA((n,)), pltpu.SemaphoreType.DMA((n,))],
    )(*g3s)


def _pair_sum(g3, land, cidx, name):
    nb, r, w = g3.shape
    hr = r // 2
    tr = 64

    def body(c_ref, g_ref, l_ref, o_ref):
        o_ref[...] = (g_ref[...].astype(F32) + l_ref[...].astype(F32)).astype(BF)

    blk = (nb, tr, w)
    return pl.pallas_call(
        body, name=name,
        grid_spec=pltpu.PrefetchScalarGridSpec(
            num_scalar_prefetch=1, grid=(hr // tr,),
            in_specs=[pl.BlockSpec(blk, lambda i, c: (0, c[0] * (hr // tr) + i, 0)),
                      pl.BlockSpec(blk, lambda i, c: (0, i, 0))],
            out_specs=pl.BlockSpec(blk, lambda i, c: (0, i, 0))),
        out_shape=jax.ShapeDtypeStruct((nb, hr, w), BF),
        compiler_params=_params(("parallel",)),
    )(cidx, g3, land)


def _chip_exchange_side(p3s, shapes, axes):
    n = len(p3s)

    def copies(ins, outs, sems):
        send, recv = sems
        x, y, c, chips = _place()
        cps = []
        for a in range(n):
            r, cc = shapes[a]
            for k, (px, py) in enumerate(chips):
                j = 2 * px + py
                src = ins[a].at[j] if axes[a] == 0 else ins[a].at[0, :, pl.ds(pl.multiple_of(j * cc, 128), cc)]
                cps.append(pltpu.make_async_remote_copy(
                    src_ref=src, dst_ref=outs[a].at[k], send_sem=send.at[a, k], recv_sem=recv.at[a, k],
                    device_id=(px, py, c), device_id_type=MESH))
        return cps

    def first(ins, outs, sems):
        for cp in copies(ins, outs, sems):
            cp.start()

    def last(ins, outs, sems):
        for cp in copies(ins, outs, sems):
            cp.wait()

    return _Side(list(p3s), [jax.ShapeDtypeStruct((3, r // 2, c), BF) for r, c in shapes],
                 [pltpu.SemaphoreType.DMA((n, 3)), pltpu.SemaphoreType.DMA((n, 3))], first, last)


def _chip_exchange_relay_side(p3, shape):
    r, cc = shape
    hr = r // 2
    rows = 64

    def copies(ins, outs, sems):
        send, recv, local, mine, theirs = sems
        x, y, c, chips = _place()
        near = ((1 - c) * (1 - x) + c * x, (1 - c) * y + c * (1 - y))
        far = ((1 - c) * x + c * (1 - x), (1 - c) * (1 - y) + c * y)
        land, staged = outs

        def block(chip):
            return ins[0].at[0, :, pl.ds(pl.multiple_of((2 * chip[0] + chip[1]) * cc, 128), cc)]

        def move(s, d, k, dev):
            return pltpu.make_async_remote_copy(
                src_ref=s, dst_ref=d, send_sem=send.at[k], recv_sem=recv.at[k], device_id=dev, device_id_type=MESH)

        return dict(
            direct=move(block(near), land.at[c], 0, (*near, c)),
            for_relay=move(block(chips[2]), staged, 1, (*near, c)),
            summed=move(mine, land.at[1 - c], 2, (*far, c)),
            direct_in=move(land.at[c], land.at[c], 0, (*near, c)),
            staged_in=move(staged, staged, 1, (*near, c)),
            summed_in=move(land.at[1 - c], land.at[1 - c], 2, (*far, c)),
            load_mine=pltpu.make_async_copy(block(far), mine, local.at[0]),
            load_theirs=pltpu.make_async_copy(staged, theirs, local.at[1]))

    def first(ins, outs, sems):
        cps = copies(ins, outs, sems)
        cps["for_relay"].start()
        cps["direct"].start()

    def mid(ins, outs, sems):
        cps = copies(ins, outs, sems)
        mine, theirs = sems[3], sems[4]
        cps["load_mine"].start()
        cps["staged_in"].wait_recv()
        cps["load_theirs"].start()
        cps["load_mine"].wait()
        cps["load_theirs"].wait()

        def add(i, carry):
            rs = pl.ds(pl.multiple_of(i * rows, 16), rows)
            mine[rs, :] = (mine[rs, :].astype(F32) + theirs[rs, :].astype(F32)).astype(BF)
            return carry

        lax.fori_loop(0, hr // rows, add, 0)
        cps["summed"].start()

    def last(ins, outs, sems):
        cps = copies(ins, outs, sems)
        cps["direct_in"].wait_recv()
        cps["summed_in"].wait_recv()
        for name in ("direct", "for_relay", "summed"):
            cps[name].wait_send()

    sems = [pltpu.SemaphoreType.DMA((3,)), pltpu.SemaphoreType.DMA((3,)), pltpu.SemaphoreType.DMA((2,)),
            pltpu.VMEM((hr, cc), BF), pltpu.VMEM((hr, cc), BF)]
    return _Side([p3], [jax.ShapeDtypeStruct((2, hr, cc), BF), jax.ShapeDtypeStruct((hr, cc), BF)], sems,
                 first, last, mid)


def _chip_sum(p3, land, shard_shape, axis, idx, name):
    r, c = shard_shape
    hr = r // 2
    tr = 64
    nt = hr // tr
    slots = land.shape[0]

    def body(idx_ref, p_ref, l_ref, o_ref):
        acc = p_ref[...].astype(F32)
        for k in range(slots):
            acc = acc + l_ref[k].astype(F32)
        o_ref[...] = acc

    own = (pl.BlockSpec((None, tr, c), lambda i, idx: (idx[0], i, 0)) if axis == 0
           else pl.BlockSpec((None, tr, c), lambda i, idx: (0, i, idx[0])))
    return pl.pallas_call(
        body, name=name,
        grid_spec=pltpu.PrefetchScalarGridSpec(
            num_scalar_prefetch=1, grid=(nt,),
            in_specs=[own, pl.BlockSpec((slots, tr, c), lambda i, idx: (0, i, 0))],
            out_specs=pl.BlockSpec((tr, c), lambda i, idx: (idx[1] * nt + i, 0))),
        out_shape=jax.ShapeDtypeStruct((r, c), F32),
        compiler_params=_params(("parallel",)),
    )(idx, p3, land)


def _rs_pair_gather(fulls, small):
    n = len(fulls)

    def body(*refs):
        ins, small_refs, outs, red_ref = refs[:n], refs[n:n + 5], refs[n + 5:2 * n + 5], refs[2 * n + 5]
        send, recv = refs[2 * n + 6:2 * n + 8]
        x, y, c, _ = _place()
        cps = []
        for a in range(n):
            hr = fulls[a].shape[0] // 2
            rows = pl.ds(pl.multiple_of(c * hr, 8), hr)
            cp = pltpu.make_async_remote_copy(
                src_ref=ins[a].at[rows, :], dst_ref=outs[a].at[rows, :], send_sem=send.at[a], recv_sem=recv.at[a],
                device_id=(x, y, 1 - c), device_id_type=MESH)
            cp.start()
            cps.append(cp)
        _small_all_reduce(small_refs, red_ref, *refs[2 * n + 8:])
        for a, cp in enumerate(cps):
            cp.wait_send()
            hr = fulls[a].shape[0] // 2
            other = pl.ds(pl.multiple_of((1 - c) * hr, 8), hr)
            pltpu.make_async_remote_copy(
                src_ref=ins[a].at[other, :], dst_ref=outs[a].at[other, :], send_sem=send.at[a], recv_sem=recv.at[a],
                device_id=(x, y, 1 - c), device_id_type=MESH).wait_recv()

    vm = pl.BlockSpec(memory_space=pltpu.VMEM)
    out = pl.pallas_call(
        body, name="grads_pair_gather",
        in_specs=[HBM] * n + [vm] * 5, out_specs=[HBM] * n + [vm],
        out_shape=[jax.ShapeDtypeStruct(f.shape, F32) for f in fulls] + [jax.ShapeDtypeStruct((NSMALL, D), F32)],
        input_output_aliases={a: a for a in range(n)},
        scratch_shapes=[pltpu.SemaphoreType.DMA((n,)), pltpu.SemaphoreType.DMA((n,)),
                        pltpu.VMEM((NSMALL, D), F32), pltpu.VMEM((8, NSMALL, D), F32),
                        pltpu.SemaphoreType.DMA((7,)), pltpu.SemaphoreType.DMA((7,))],
    )(*fulls, *small)
    return out[:n], out[n]


def _sibling_send_side(arr):
    def copy(ins, outs, sems):
        x, y, c, _ = _place()
        return pltpu.make_async_remote_copy(
            src_ref=ins[0], dst_ref=outs[0], send_sem=sems[0].at[0], recv_sem=sems[1].at[0],
            device_id=(x, y, 1 - c), device_id_type=MESH)

    return _Side([arr], [jax.ShapeDtypeStruct(arr.shape, arr.dtype)],
                 [pltpu.SemaphoreType.DMA((1,)), pltpu.SemaphoreType.DMA((1,))],
                 lambda ins, outs, sems: copy(ins, outs, sems).start(),
                 lambda ins, outs, sems: copy(ins, outs, sems).wait())


def _add_bf16(a, b, name):
    r, c = a.shape
    tr = 64

    def body(a_ref, b_ref, o_ref):
        o_ref[...] = (a_ref[...].astype(F32) + b_ref[...].astype(F32)).astype(BF)

    blk = pl.BlockSpec((tr, c), lambda i: (i, 0))
    return pl.pallas_call(
        body, name=name, grid=(r // tr,), in_specs=[blk, blk], out_specs=blk,
        out_shape=jax.ShapeDtypeStruct((r, c), BF), compiler_params=_params(("parallel",)),
    )(a, b)


def _rs_partials(grads, shapes, axes, tag):
    cidx = jnp.reshape(lax.axis_index("c"), (1,)).astype(jnp.int32)
    g3s = [_as3d(g, s, ax) for g, s, ax in zip(grads, shapes, axes)]
    lands = _rs_pair_exchange(g3s, f"grads_pair_exchange_{tag}")
    return [_pair_sum(g3, l, cidx, f"pair_sum_{tag}_{a}") for a, (g3, l) in enumerate(zip(g3s, lands))]


def _rs_finish(p3s, landed, shapes, axes, small):
    x, y, c = lax.axis_index("x"), lax.axis_index("y"), lax.axis_index("c")
    idx = jnp.stack([2 * x + y, c]).astype(jnp.int32)
    fulls = [_chip_sum(p3, l2, s, ax, idx, f"chip_sum_{a}")
             for a, (p3, l2, s, ax) in enumerate(zip(p3s, landed, shapes, axes))]
    return _rs_pair_gather(fulls, small)


NSMALL = 8


def _small_all_reduce(small_refs, out_ref, pack_ref, buf_ref, send, recv):
    nw_ref, lb_ref, hn_ref, wf_ref, ls_ref = small_refs
    x, y, c = lax.axis_index("x"), lax.axis_index("y"), lax.axis_index("c")
    me = 4 * x + 2 * y + c
    pack_ref[...] = jnp.zeros_like(pack_ref)
    pack_ref[0:1, :] = nw_ref[...]
    pack_ref[1:2, :] = lb_ref[...]
    pack_ref[2:3, 0:HK] = hn_ref[...]
    pack_ref[3:4, :] = wf_ref[...]
    pack_ref[4:5, :] = ls_ref[...]
    buf_ref[me] = pack_ref[...]
    cps = []
    for d in range(1, 8):
        dx, dy, dc = d >> 2, (d >> 1) & 1, d & 1
        peer = (1 - x if dx else x, 1 - y if dy else y, 1 - c if dc else c)
        cp = pltpu.make_async_remote_copy(
            src_ref=pack_ref, dst_ref=buf_ref.at[me], send_sem=send.at[d - 1], recv_sem=recv.at[d - 1],
            device_id=peer, device_id_type=MESH)
        cp.start()
        cps.append(cp)
    for d in range(1, 8):
        dx, dy, dc = d >> 2, (d >> 1) & 1, d & 1
        src = 4 * (1 - x if dx else x) + 2 * (1 - y if dy else y) + (1 - c if dc else c)
        pltpu.make_async_remote_copy(
            src_ref=pack_ref, dst_ref=buf_ref.at[src], send_sem=send.at[d - 1], recv_sem=recv.at[d - 1],
            device_id=(x, y, c), device_id_type=MESH).wait_recv()
    for cp in cps:
        cp.wait_send()
    acc = buf_ref[0]
    for i in range(1, 8):
        acc = acc + buf_ref[i]
    out_ref[...] = acc


def _adamw_math(w, g, m, v):
    m = B1 * m + (1.0 - B1) * g
    v = B2 * v + (1.0 - B2) * (g * g)
    m_hat = m / (1.0 - B1 ** STEP)
    v_hat = v / (1.0 - B2 ** STEP)
    return -LR * (m_hat / (jnp.sqrt(v_hat) + ADAM_EPS) + WD * w), m, v


def _adamw(w, g, m, v, name):
    r, c = w.shape
    tr = 128

    def body(w_ref, g_ref, m_ref, v_ref, d_ref, nm_ref, nv_ref, go_ref):
        g = g_ref[...]
        d_ref[...], nm_ref[...], nv_ref[...] = _adamw_math(w_ref[...], g, m_ref[...], v_ref[...])
        go_ref[...] = g

    blk = pl.BlockSpec((tr, c), lambda i: (i, 0))
    return pl.pallas_call(
        body, name=name, grid=(r // tr,), in_specs=[blk] * 4, out_specs=[blk] * 4,
        out_shape=[jax.ShapeDtypeStruct((r, c), F32)] * 4,
        compiler_params=_params(("parallel",)),
    )(w, g, m, v)


def _adamw_whole(groups, name):
    n = len(groups)

    def body(*refs):
        ins, outs = refs[:4 * n], refs[4 * n:]
        for a in range(n):
            w_ref, g_ref, m_ref, v_ref = ins[4 * a:4 * a + 4]
            g = g_ref[...]
            outs[4 * a][...], outs[4 * a + 1][...], outs[4 * a + 2][...] = _adamw_math(
                w_ref[...], g, m_ref[...], v_ref[...])
            outs[4 * a + 3][...] = g

    vm = pl.BlockSpec(memory_space=pltpu.VMEM)
    out = pl.pallas_call(
        body, name=name, in_specs=[vm] * (4 * n), out_specs=[vm] * (4 * n),
        out_shape=[jax.ShapeDtypeStruct(grp[0].shape, F32) for grp in groups for _ in range(4)],
        compiler_params=_params(),
    )(*[a for grp in groups for a in grp])
    return [out[4 * a:4 * a + 4] for a in range(n)]


def _small_update(red, lbl, params):
    def body(red_ref, *refs):
        ins, outs = refs[:12], refs[12:]
        lb = _lower_bound(ins[3][...])
        dl0 = red_ref[1:2, :] * lb * (1.0 - lb)
        row = lax.broadcasted_iota(jnp.int32, (2, D), 0)
        grads = [red_ref[0:1, :], jnp.where(row == 0, dl0, -dl0), red_ref[2:3, 0:HK], red_ref[3:4, :]]
        for i, g in enumerate(grads):
            w, m, v = ins[3 * i][...], ins[3 * i + 1][...], ins[3 * i + 2][...]
            d, nm, nv = _adamw_math(w, g, m, v)
            outs[4 * i][...] = g
            outs[4 * i + 1][...] = d
            outs[4 * i + 2][...] = nm
            outs[4 * i + 3][...] = nv
        outs[16][...] = jnp.sum(red_ref[4:5, :], axis=1, keepdims=True)

    flat = [a for p in params for a in p]
    vm = pl.BlockSpec(memory_space=pltpu.VMEM)
    shapes = [jax.ShapeDtypeStruct(p[0].shape, F32) for p in params for _ in range(4)]
    return pl.pallas_call(
        body, name="small_update",
        in_specs=[vm] * 13, out_specs=[vm] * 17,
        out_shape=shapes + [jax.ShapeDtypeStruct((1, 1), F32)],
    )(red, *flat)


def kernel(x, positions, norm_w, w_in, lb_logits, hgrn_norm_w, w_branch_a, w_branch_b, w_out, final_norm_w, loss_target, m_norm_w, m_w_in, m_lb_logits, m_hgrn_norm_w, m_w_branch_a, m_w_branch_b, m_w_out, m_final_norm_w, v_norm_w, v_w_in, v_lb_logits, v_hgrn_norm_w, v_w_branch_a, v_w_branch_b, v_w_out, v_final_norm_w):
    big_w = [w_in[0], w_branch_a[0], w_branch_b[0], w_out[0]]
    big_m = [m_w_in[0], m_w_branch_a[0], m_w_branch_b[0], m_w_out[0]]
    big_v = [v_w_in[0], v_w_branch_a[0], v_w_branch_b[0], v_w_out[0]]
    shapes = [w.shape for w in big_w]
    wf = final_norm_w.reshape(1, D)

    shards = [w.astype(BF) for w in big_w]
    loc = _local_step(x[0], positions.reshape(T, 1), norm_w, lb_logits, hgrn_norm_w, wf, loss_target[0],
                      *shards, shard_shapes=shapes)
    g_big = [loc["g_win"], loc["g_wa"], loc["g_wb"], loc["g_wout"]]
    red = loc["small_sums"]

    small = _small_update(red, lb_logits, [
        (norm_w, m_norm_w, v_norm_w), (lb_logits, m_lb_logits, v_lb_logits),
        (hgrn_norm_w, m_hgrn_norm_w, v_hgrn_norm_w),
        (wf, m_final_norm_w.reshape(1, D), v_final_norm_w.reshape(1, D))])
    loss = small[16].reshape(())
    sg, sd, sm, sv = ([small[4 * i + j] for i in range(4)] for j in range(4))
    for lst in (sg, sd, sm, sv):
        lst[3] = lst[3].reshape(D)
    per_w = list(zip(big_w, g_big, big_m, big_v))
    upd = [_adamw(*per_w[0], "adamw_w_in")] + _adamw_whole(per_w[1:], "adamw_branches_out")
    bd, bm, bv, bg = ([u[j][None] for u in upd] for j in range(4))

    def order(s, b):
        return [s[0], b[0], s[1], s[2], b[1], b[2], b[3], s[3]]

    return (loss, loc["gx"][None], *order(sg, bg), *order(sd, bd), *order(sm, bm), *order(sv, bv))
```

```python
import functools

import jax
import jax.numpy as jnp
from jax import lax
from jax.experimental import pallas as pl
from jax.experimental.pallas import tpu as pltpu

T = 2048
D = 1024
NIN = 11264
HEADS = 8
HK = 128
CH = 16
NCH = T // CH
HSTEP = 2
ATT_GROUPS = ((128, 1), (512, 4), (2048, 16))
ATT_COL0 = 4096
AG_COL0 = 8704
GATE_COL0 = 9216
EPS = 1e-6
ROPE_THETA = 10000.0
LR, B1, B2, ADAM_EPS, WD, STEP = 0.001, 0.9, 0.999, 1e-08, 0.01, 10

F32 = jnp.float32
BF = jnp.bfloat16
VMEM_LIMIT = 56 * 1024 * 1024

_NN = (((1,), (0,)), ((), ()))
_NT = (((1,), (1,)), ((), ()))
_TN = (((0,), (0,)), ((), ()))


def _dot(a, b, dims=_NN):
    return lax.dot_general(a, b, dims, preferred_element_type=F32)


def _bdot(a, b, dims=_NN):
    return lax.dot_general(a.astype(BF), b.astype(BF), dims, preferred_element_type=F32)


def _sigmoid(x):
    return jax.nn.sigmoid(x)


def _params(sem=None):
    return pltpu.CompilerParams(dimension_semantics=sem, vmem_limit_bytes=VMEM_LIMIT)


def _matmul(a, b, *, ta=False, tb=False, out_dtype=F32, tm=512, tn=512, tk=None, name, side=None):
    m = a.shape[1] if ta else a.shape[0]
    kdim = a.shape[0] if ta else a.shape[1]
    n = b.shape[0] if tb else b.shape[1]
    tk = tk or kdim
    tm, tn = min(tm, m), min(tn, n)
    nm, nn, nk = m // tm, n // tn, kdim // tk
    dims = (((0 if ta else 1,), (1 if tb else 0,)), ((), ()))
    s_arrays, s_in_specs, s_shapes, s_out_specs, s_sems = _side_io(side)
    na, no = len(s_arrays), len(s_shapes)
    nacc = 1 if nk > 1 else 0

    def body(*refs):
        a_ref, b_ref = refs[:2]
        s_ins, o_ref, s_outs = refs[2:2 + na], refs[2 + na], refs[3 + na:3 + na + no]
        scratch = refs[3 + na + no:]
        s_sem_refs = scratch[nacc:]
        i, j, k = pl.program_id(0), pl.program_id(1), pl.program_id(2)
        if side is not None:
            @pl.when((i == 0) & (j == 0) & (k == 0))
            def _():
                side.first(s_ins, s_outs, s_sem_refs)

        prod = _bdot(a_ref[...], b_ref[...], dims)
        if nk == 1:
            o_ref[...] = prod.astype(out_dtype)
        else:
            acc = scratch[0]

            @pl.when(k == 0)
            def _():
                acc[...] = prod

            @pl.when(k > 0)
            def _():
                acc[...] += prod

            @pl.when(k == nk - 1)
            def _():
                o_ref[...] = acc[...].astype(out_dtype)

        if side is not None:
            @pl.when((i == nm - 1) & (j == nn - 1) & (k == nk - 1))
            def _():
                side.last(s_ins, s_outs, s_sem_refs)

    a_spec = pl.BlockSpec((tk, tm), lambda i, j, k: (k, i)) if ta else pl.BlockSpec((tm, tk), lambda i, j, k: (i, k))
    b_spec = pl.BlockSpec((tn, tk), lambda i, j, k: (j, k)) if tb else pl.BlockSpec((tk, tn), lambda i, j, k: (k, j))
    sem = ("parallel", "parallel", "arbitrary") if side is None else ("arbitrary",) * 3
    out = pl.pallas_call(
        body, name=name, grid=(nm, nn, nk),
        in_specs=[a_spec, b_spec] + s_in_specs,
        out_specs=[pl.BlockSpec((tm, tn), lambda i, j, k: (i, j))] + s_out_specs,
        out_shape=[jax.ShapeDtypeStruct((m, n), out_dtype)] + s_shapes,
        scratch_shapes=([pltpu.VMEM((tm, tn), F32)] if nk > 1 else []) + s_sems,
        compiler_params=_params(sem),
    )(a, b, *s_arrays)
    return out[0] if side is None else (out[0], out[1:])


DZ_TILE = 512


def _part_offsets(parts):
    counts = [p.shape[1] // DZ_TILE for p in parts]
    offs = [sum(counts[:i]) for i in range(len(parts))]
    return counts, offs


def _part_spec(rows, cnt, off, tile_axis):
    def index(*g):
        return (0 if rows is None else g[0], jnp.clip(g[tile_axis] - off, 0, cnt - 1))
    return index


def _grad_w_in(h, parts):
    counts, offs = _part_offsets(parts)
    n = len(parts)

    def body(h_ref, *refs):
        o_ref = refs[n]
        j = pl.program_id(0)
        for p_ref, cnt, off in zip(refs[:n], counts, offs):
            @pl.when((j >= off) & (j < off + cnt))
            def _(p_ref=p_ref):
                o_ref[...] = _bdot(h_ref[...], p_ref[...], _TN).astype(BF)

    return pl.pallas_call(
        body, name="g_win", grid=(sum(counts),),
        in_specs=[pl.BlockSpec((T, D), lambda j: (0, 0))] +
                 [pl.BlockSpec((T, DZ_TILE), _part_spec(None, c, o, 0)) for c, o in zip(counts, offs)],
        out_specs=pl.BlockSpec((D, DZ_TILE), lambda j: (0, j)),
        out_shape=jax.ShapeDtypeStruct((D, NIN), BF),
        compiler_params=_params(("parallel",)),
    )(h, *parts)


def _grad_w_in_half(h, parts, half_idx, side=None):
    counts, offs = _part_offsets(parts)
    n = len(parts)
    nj = sum(counts)
    s_arrays, s_in_specs, s_shapes, s_out_specs, s_sems = _side_io(side)
    na, no = len(s_arrays), len(s_shapes)

    def body(idx_ref, h_ref, *refs):
        s_ins, o_ref, s_outs, s_sem_refs = refs[n:n + na], refs[n + na], refs[n + na + 1:n + na + 1 + no], refs[n + na + 1 + no:]
        j = pl.program_id(0)
        if side is not None:
            @pl.when(j == 0)
            def _():
                side.first(s_ins, s_outs, s_sem_refs)

        for p_ref, cnt, off in zip(refs[:n], counts, offs):
            @pl.when((j >= off) & (j < off + cnt))
            def _(p_ref=p_ref):
                o_ref[...] = _bdot(h_ref[...], p_ref[...], _TN).astype(BF)

        if side is not None:
            @pl.when(j == nj - 1)
            def _():
                side.last(s_ins, s_outs, s_sem_refs)

    def part_spec(cnt, off):
        return pl.BlockSpec((T, DZ_TILE), lambda j, idx: (0, jnp.clip(j - off, 0, cnt - 1)))

    out = pl.pallas_call(
        body, name="g_win_half" if side is None else "g_win_half_carrying",
        grid_spec=pltpu.PrefetchScalarGridSpec(
            num_scalar_prefetch=1, grid=(nj,),
            in_specs=[pl.BlockSpec((T, D // 2), lambda j, idx: (0, idx[0]))] +
                     [part_spec(c, o) for c, o in zip(counts, offs)] + s_in_specs,
            out_specs=[pl.BlockSpec((D // 2, DZ_TILE), lambda j, idx: (0, j))] + s_out_specs,
            scratch_shapes=s_sems),
        out_shape=[jax.ShapeDtypeStruct((D // 2, NIN), BF)] + s_shapes,
        compiler_params=_params(("parallel",) if side is None else ("arbitrary",)),
    )(half_idx, h, *parts, *s_arrays)
    return out[0] if side is None else (out[0], out[1:])


def _side_io(side):
    if side is None:
        return [], [], [], [], []
    return (side.arrays, [HBM] * len(side.arrays), side.out_shapes, [HBM] * len(side.out_shapes), side.sems)


def _grad_x(parts, w_in, x, dout, norm_w, side=None):
    counts, offs = _part_offsets(parts)
    n = len(parts)
    tm = 1024
    nm, nk = T // tm, sum(counts)
    s_arrays, s_in_specs, s_shapes, s_out_specs, s_sems = _side_io(side)
    na, no = len(s_arrays), len(s_shapes)

    def body(*refs):
        w_ref, x_ref, dout_ref, nw_ref = refs[n:n + 4]
        s_ins = refs[n + 4:n + 4 + na]
        gx_ref, gw_ref = refs[n + 4 + na:n + 6 + na]
        s_outs = refs[n + 6 + na:n + 6 + na + no]
        acc = refs[n + 6 + na + no]
        s_sem_refs = refs[n + 7 + na + no:]
        i, k = pl.program_id(0), pl.program_id(1)

        @pl.when((i == 0) & (k == 0))
        def _():
            gw_ref[...] = jnp.zeros_like(gw_ref)
            if side is not None:
                side.first(s_ins, s_outs, s_sem_refs)

        @pl.when(k == 0)
        def _():
            acc[...] = jnp.zeros_like(acc)

        if side is not None and side.mid is not None:
            @pl.when((i == nm - 1) & (k == 0))
            def _():
                side.mid(s_ins, s_outs, s_sem_refs)

        for p_ref, cnt, off in zip(refs[:n], counts, offs):
            @pl.when((k >= off) & (k < off + cnt))
            def _(p_ref=p_ref):
                acc[...] += _bdot(p_ref[...], w_ref[...], _NT)

        @pl.when(k == nk - 1)
        def _():
            gw = jnp.zeros((1, D), F32)
            for c in range(tm // BLK):
                rows = pl.ds(BLK * c, BLK)
                xv, dhv = x_ref[rows, :], acc[rows, :]
                r = lax.rsqrt(jnp.mean(xv * xv, axis=-1, keepdims=True) + EPS)
                nrm = xv * r
                dn = dhv * nw_ref[...]
                gw = gw + jnp.sum(dhv * nrm, axis=0, keepdims=True)
                gx_ref[rows, :] = dout_ref[rows, :] + r * (dn - nrm * jnp.mean(dn * nrm, axis=-1, keepdims=True))
            gw_ref[...] += gw

        if side is not None:
            @pl.when((i == nm - 1) & (k == nk - 1))
            def _():
                side.last(s_ins, s_outs, s_sem_refs)

    row = pl.BlockSpec((tm, D), lambda i, k: (i, 0))
    vec = pl.BlockSpec((1, D), lambda i, k: (0, 0))
    out = pl.pallas_call(
        body, name="grad_x", grid=(nm, nk),
        in_specs=[pl.BlockSpec((tm, DZ_TILE), _part_spec(0, c, o, 1)) for c, o in zip(counts, offs)] +
                 [pl.BlockSpec((D, DZ_TILE), lambda i, k: (0, k)), row, row, vec] + s_in_specs,
        out_specs=[row, vec] + s_out_specs,
        out_shape=[jax.ShapeDtypeStruct((T, D), F32), jax.ShapeDtypeStruct((1, D), F32)] + s_shapes,
        scratch_shapes=[pltpu.VMEM((tm, D), F32)] + s_sems,
        compiler_params=_params(("arbitrary", "arbitrary")),
    )(*parts, w_in, x, dout, norm_w, *s_arrays)
    return out[0], out[1], out[2:]


def _norm_and_rope_tables(x, w, pos, invf, side=None, own=None):
    tm = 256
    nm = T // tm
    s_arrays, s_in_specs, s_shapes, s_out_specs, s_sems = _side_io(side)
    na, no = len(s_arrays), len(s_shapes)
    nz = 0 if own is None else 1
    wsh, blk = own if own is not None else (None, jnp.zeros((1,), jnp.int32))

    def body(blk_ref, *refs):
        x_ref, w_ref, pos_ref, invf_ref = refs[:4]
        s_ins = refs[4 + nz:4 + nz + na]
        h_ref, cos_ref, sa_ref, sb_ref = refs[4 + nz + na:8 + nz + na]
        s_outs = refs[8 + 2 * nz + na:8 + 2 * nz + na + no]
        s_sem_refs = refs[8 + 2 * nz + na + no:]
        if side is not None:
            @pl.when(pl.program_id(0) == 0)
            def _():
                side.first(s_ins, s_outs, s_sem_refs)

        xv = x_ref[...]
        r = lax.rsqrt(jnp.mean(xv * xv, axis=-1, keepdims=True) + EPS)
        h = (xv * r * w_ref[...]).astype(BF)
        h_ref[...] = h
        if own is not None:
            refs[8 + nz + na][...] = _dot(h, refs[4][...])
        first = (lax.broadcasted_iota(jnp.int32, (tm, 128), 1) % 64) < 32
        ang = pos_ref[...].astype(F32) * invf_ref[...]
        s = jnp.sin(ang)
        cos_ref[...] = jnp.cos(ang)
        sa_ref[...] = jnp.where(first, -s, 0.0)
        sb_ref[...] = jnp.where(first, 0.0, s)
        if side is not None:
            @pl.when(pl.program_id(0) == nm - 1)
            def _():
                side.last(s_ins, s_outs, s_sem_refs)

    tab = pl.BlockSpec((tm, 128), lambda i, b: (i, 0))
    own_in = [] if own is None else [pl.BlockSpec(wsh.shape, lambda i, b: (0, 0))]
    own_out = [] if own is None else [pl.BlockSpec((tm, wsh.shape[1]), lambda i, b: (i, b[0]))]
    own_shape = [] if own is None else [jax.ShapeDtypeStruct((T, NIN), F32)]
    out = pl.pallas_call(
        body, name="norm_and_rope_tables",
        grid_spec=pltpu.PrefetchScalarGridSpec(
            num_scalar_prefetch=1, grid=(nm,),
            in_specs=[pl.BlockSpec((tm, D), lambda i, b: (i, 0)), pl.BlockSpec((1, D), lambda i, b: (0, 0)),
                      pl.BlockSpec((tm, 1), lambda i, b: (i, 0)), pl.BlockSpec((1, 128), lambda i, b: (0, 0))]
                     + own_in + s_in_specs,
            out_specs=[pl.BlockSpec((tm, D), lambda i, b: (i, 0)), tab, tab, tab] + own_out + s_out_specs,
            scratch_shapes=s_sems),
        out_shape=[jax.ShapeDtypeStruct((T, D), BF)] + [jax.ShapeDtypeStruct((T, 128), F32)] * 3 + own_shape + s_shapes,
        compiler_params=_params(("parallel",) if side is None else ("arbitrary",)),
    )(blk, x, w, pos, invf, *([] if own is None else [wsh]), *s_arrays)
    return out[0], out[1], out[2], out[3], (out[4] if own is not None else None), out[4 + nz:]


def _z_blocks(h, w, z, idx, nb, side, name, fill=None):
    tm, tn = 1024, NIN // 8
    s_arrays, s_in_specs, s_shapes, s_out_specs, s_sems = _side_io(side)
    na, no = len(s_arrays), len(s_shapes)
    nm, ns = T // tm, 2 * nb
    nf = 0 if fill is None else 1

    def col(first, i, s, b):
        return (0, b[first + s // 2] * 2 + s % 2)

    def body(idx_ref, h_ref, w_ref, zin_ref, *refs):
        s_ins = refs[nf:nf + na]
        o_ref = refs[nf + na]
        s_outs = refs[nf + na + 1 + nf:nf + na + 1 + nf + no]
        s_sem_refs = refs[nf + na + 1 + nf + no + nf:]
        i, s = pl.program_id(0), pl.program_id(1)

        if side is not None:
            @pl.when((i == 0) & (s == 0))
            def _():
                side.first(s_ins, s_outs, s_sem_refs)

        if fill is not None:
            tile = pl.ds(pl.multiple_of((idx_ref[0] * 2 + s) * tn, 128), tn)
            store = pltpu.make_async_copy(w_ref, refs[nf + na + 1].at[:, tile], refs[nf + na + 1 + nf + no].at[0])
            pl.when(i == 0)(store.start)
        o_ref[...] = _dot(h_ref[...], w_ref[...])
        if fill is not None:
            pl.when(i == 0)(store.wait)

        if side is not None:
            @pl.when((i == nm - 1) & (s == ns - 1))
            def _():
                side.last(s_ins, s_outs, s_sem_refs)

    fills = [] if fill is None else [fill]
    out = pl.pallas_call(
        body, name=name,
        grid_spec=pltpu.PrefetchScalarGridSpec(
            num_scalar_prefetch=1, grid=(nm, ns),
            in_specs=[pl.BlockSpec((tm, D), lambda i, s, b: (i, 0)), pl.BlockSpec((D, tn), functools.partial(col, nb)),
                      HBM] + [HBM] * nf + s_in_specs,
            out_specs=[pl.BlockSpec((tm, tn), lambda i, s, b: (i, col(0, i, s, b)[1]))] + [HBM] * nf + s_out_specs,
            scratch_shapes=[pltpu.SemaphoreType.DMA((1,))] * nf + s_sems),
        out_shape=[jax.ShapeDtypeStruct((T, NIN), F32)] + [jax.ShapeDtypeStruct(f.shape, f.dtype) for f in fills]
                  + s_shapes,
        input_output_aliases={3: 0, **({4: 1} if fill is not None else {})},
        compiler_params=_params(("arbitrary", "arbitrary")),
    )(idx, h, w, z, *fills, *s_arrays)
    return (out[0], *out[1:1 + nf], out[1 + nf:])


def _lower_bound(lbl):
    mx = jnp.max(lbl, axis=0, keepdims=True)
    e = jnp.exp(lbl - mx)
    return e[0:1] / jnp.sum(e, axis=0, keepdims=True)


def _cumsum_rows(g, rows):
    b = g
    sh = 1
    while sh < CH:
        b = b + jnp.where(rows >= sh, pltpu.roll(b, sh, axis=0), 0.0)
        sh *= 2
    return b


def _rev_cumsum_rows(g, rows):
    b = g
    sh = 1
    while sh < CH:
        b = b + jnp.where(rows < CH - sh, pltpu.roll(b, CH - sh, axis=0), 0.0)
        sh *= 2
    return b


SUB = CH // 2


def _direct_block(qb, kb, vb, bb, rows8):
    ob = jnp.zeros_like(qb)
    for s in range(SUB):
        e_s = jnp.exp(jnp.where(rows8 >= s, bb - bb[s:s + 1], -jnp.inf))
        ob = ob + jnp.sum(qb * e_s * kb[s:s + 1], axis=1, keepdims=True) * vb[s:s + 1]
    return ob


def _direct_block_bwd(qb, kb, vb, bb, dob, rows8, rowc8):
    dq = dk = dv = db = jnp.zeros_like(qb)
    for s in range(SUB):
        one = (rowc8 == s).astype(F32)
        ks, vs = kb[s:s + 1], vb[s:s + 1]
        e_s = jnp.exp(jnp.where(rows8 >= s, bb - bb[s:s + 1], -jnp.inf))
        qes = qb * e_s
        w = qes * ks
        a = jnp.sum(w, axis=1, keepdims=True)
        da = jnp.sum(dob * vs, axis=1, keepdims=True)
        dv = dv + one * jnp.sum(a * dob, axis=0, keepdims=True)
        dq = dq + da * e_s * ks
        dk = dk + one * jnp.sum(da * qes, axis=0, keepdims=True)
        u = da * w
        db = db + u - one * jnp.sum(u, axis=0, keepdims=True)
    return dq, dk, dv, db


def _cross_factors(q, k, b):
    ref = b[SUB - 1:SUB]
    e_hi, e_lo = jnp.exp(b[SUB:] - ref), jnp.exp(ref - b[:SUB])
    return q[SUB:] * e_hi, k[:SUB] * e_lo, e_hi, e_lo


def _intra_fwd(q, k, v, b, rows8):
    lo = _direct_block(q[:SUB], k[:SUB], v[:SUB], b[:SUB], rows8)
    hi = _direct_block(q[SUB:], k[SUB:], v[SUB:], b[SUB:], rows8)
    qe_hi, ke_lo, _, _ = _cross_factors(q, k, b)
    for s in range(SUB):
        hi = hi + jnp.sum(qe_hi * ke_lo[s:s + 1], axis=1, keepdims=True) * v[s:s + 1]
    return jnp.concatenate([lo, hi], axis=0)


def _intra_bwd(q, k, v, b, do, rows8, rowc8):
    dq_lo, dk_lo, dv_lo, db_lo = _direct_block_bwd(q[:SUB], k[:SUB], v[:SUB], b[:SUB], do[:SUB], rows8, rowc8)
    dq_hi, dk_hi, dv_hi, db_hi = _direct_block_bwd(q[SUB:], k[SUB:], v[SUB:], b[SUB:], do[SUB:], rows8, rowc8)
    qe_hi, ke_lo, e_hi, e_lo = _cross_factors(q, k, b)
    do_hi, v_lo = do[SUB:], v[:SUB]
    dqe = dke = jnp.zeros_like(qe_hi)
    for s in range(SUB):
        one = (rowc8 == s).astype(F32)
        a = jnp.sum(qe_hi * ke_lo[s:s + 1], axis=1, keepdims=True)
        da = jnp.sum(do_hi * v_lo[s:s + 1], axis=1, keepdims=True)
        dv_lo = dv_lo + one * jnp.sum(a * do_hi, axis=0, keepdims=True)
        dqe = dqe + da * ke_lo[s:s + 1]
        dke = dke + one * jnp.sum(da * qe_hi, axis=0, keepdims=True)
    u_hi, u_lo = dqe * qe_hi, dke * ke_lo
    d_ref = jnp.sum(u_lo, axis=0, keepdims=True) - jnp.sum(u_hi, axis=0, keepdims=True)
    db_lo = db_lo - u_lo + (rowc8 == SUB - 1).astype(F32) * d_ref
    cat = lambda lo, hi: jnp.concatenate([lo, hi], axis=0)
    return (cat(dq_lo, dq_hi + dqe * e_hi), cat(dk_lo + dke * e_lo, dk_hi), cat(dv_lo, dv_hi),
            cat(db_lo, db_hi + u_hi))


def _hgrn_fwd(z, lbl, nw, side=None):
    s_arrays, s_in_specs, s_shapes, s_out_specs, s_sems = _side_io(side)
    na, no = len(s_arrays), len(s_shapes)
    nsteps = NCH // HSTEP

    def body(hq_ref, hf_ref, hi_ref, hg_ref, lbl_ref, nw_ref, *refs):
        s_ins, (oraw_ref, og_ref, sh_ref) = refs[:na], refs[na:na + 3]
        s_outs, st_ref, s_sem_refs = refs[na + 3:na + 3 + no], refs[na + 3 + no], refs[na + 4 + no:]

        @pl.when(pl.program_id(0) == 0)
        def _():
            st_ref[...] = jnp.zeros_like(st_ref)
            if side is not None:
                side.first(s_ins, s_outs, s_sem_refs)

        lb_all = _lower_bound(lbl_ref[...])
        rows = lax.broadcasted_iota(jnp.int32, (CH, HK), 0)
        rows8 = lax.broadcasted_iota(jnp.int32, (SUB, HK), 0)
        nwv = nw_ref[...]
        for cc, h in [(cc, h) for cc in range(HSTEP) for h in range(HEADS)]:
            rs = slice(CH * cc, CH * (cc + 1))
            sl = slice(HK * h, HK * (h + 1))
            lb = lb_all[:, sl]
            hq, hf, v, hg = hq_ref[rs, sl], hf_ref[rs, sl], hi_ref[rs, sl], hg_ref[rs, sl]
            q = hq * _sigmoid(hq)
            f = lb + (1.0 - lb) * _sigmoid(hf)
            k = 1.0 - f
            b = _cumsum_rows(jnp.log(f), rows)
            sh_ref[cc, h] = st_ref[h]
            o = _bdot(q * jnp.exp(b), st_ref[h], _NT) + _intra_fwd(q, k, v, b, rows8)
            bl = b[CH - 1:CH]
            st_ref[h] = st_ref[h] * jnp.exp(bl)
            st_ref[h] += _bdot(v, k * jnp.exp(bl - b), _TN)
            oraw_ref[rs, sl] = o
            nrm = o * lax.rsqrt(jnp.mean(o * o, axis=1, keepdims=True) + EPS)
            og_ref[rs, sl] = (nrm * nwv * (hg * _sigmoid(hg))).astype(BF)

        if side is not None:
            @pl.when(pl.program_id(0) == nsteps // 2)
            def _():
                side.mid(s_ins, s_outs, s_sem_refs)

            @pl.when(pl.program_id(0) == nsteps - 1)
            def _():
                side.last(s_ins, s_outs, s_sem_refs)

    zblk = lambda c: pl.BlockSpec((CH * HSTEP, D), lambda i, c=c: (i, c))
    out = pl.pallas_call(
        body, name="hgrn_fwd", grid=(nsteps,),
        in_specs=[zblk(0), zblk(1), zblk(2), zblk(3),
                  pl.BlockSpec((2, D), lambda i: (0, 0)), pl.BlockSpec((1, HK), lambda i: (0, 0))] + s_in_specs,
        out_specs=[zblk(0), zblk(0),
                   pl.BlockSpec((HSTEP, HEADS, HK, HK), lambda i: (i, 0, 0, 0))] + s_out_specs,
        out_shape=[jax.ShapeDtypeStruct((T, D), F32), jax.ShapeDtypeStruct((T, D), BF),
                   jax.ShapeDtypeStruct((NCH, HEADS, HK, HK), F32)] + s_shapes,
        scratch_shapes=[pltpu.VMEM((HEADS, HK, HK), F32)] + s_sems,
        compiler_params=_params(("arbitrary",)),
    )(z, z, z, z, lbl, nw, *s_arrays)
    return out[0], out[1], out[2], out[3:]


def _hgrn_bwd(z, lbl, nw, oraw, dog, shist, side=None):
    hstep = 1
    s_arrays, s_in_specs, s_shapes, s_out_specs, s_sems = _side_io(side)
    na, no = len(s_arrays), len(s_shapes)

    def body(*refs):
        hq_ref, hf_ref, hi_ref, hg_ref, lbl_ref, nw_ref, oraw_ref, dog_ref, sh_ref = refs[:9]
        s_ins = refs[9:9 + na]
        dz_ref, dlb_ref, dnw_ref = refs[9 + na:12 + na]
        s_outs = refs[12 + na:12 + na + no]
        dst_ref = refs[12 + na + no]
        s_sem_refs = refs[13 + na + no:]

        @pl.when(pl.program_id(0) == 0)
        def _():
            dst_ref[...] = jnp.zeros_like(dst_ref)
            dlb_ref[...] = jnp.zeros_like(dlb_ref)
            dnw_ref[...] = jnp.zeros_like(dnw_ref)
            if side is not None:
                side.first(s_ins, s_outs, s_sem_refs)

        lb_all = _lower_bound(lbl_ref[...])
        rows = lax.broadcasted_iota(jnp.int32, (CH, HK), 0)
        rowc = lax.broadcasted_iota(jnp.int32, (CH, 1), 0)
        rows8 = lax.broadcasted_iota(jnp.int32, (SUB, HK), 0)
        rowc8 = lax.broadcasted_iota(jnp.int32, (SUB, 1), 0)
        nwv = nw_ref[...]
        dnw = jnp.zeros((1, HK), F32)
        for cc, h in [(cc, h) for cc in reversed(range(hstep)) for h in range(HEADS)]:
            rs = slice(CH * cc, CH * (cc + 1))
            sl = slice(HK * h, HK * (h + 1))
            lb = lb_all[:, sl]
            hq, hf, v, hg = hq_ref[rs, sl], hf_ref[rs, sl], hi_ref[rs, sl], hg_ref[rs, sl]
            o, dg_out = oraw_ref[rs, sl], dog_ref[rs, sl]
            sg = _sigmoid(hg)
            sil = hg * sg
            r = lax.rsqrt(jnp.mean(o * o, axis=1, keepdims=True) + EPS)
            nrm = o * r
            d_hg = dg_out * (nrm * nwv) * (sg * (1.0 + hg * (1.0 - sg)))
            dn = dg_out * nwv * sil
            dnw = dnw + jnp.sum(dg_out * nrm * sil, axis=0, keepdims=True)
            do = r * (dn - nrm * jnp.mean(dn * nrm, axis=1, keepdims=True))
            sq = _sigmoid(hq)
            q = hq * sq
            sig = _sigmoid(hf)
            f = lb + (1.0 - lb) * sig
            k = 1.0 - f
            b = _cumsum_rows(jnp.log(f), rows)
            eb = jnp.exp(b)
            qe = q * eb
            bl = b[CH - 1:CH]
            ebl = jnp.exp(bl)
            kdec = jnp.exp(bl - b)
            ke = k * kdec
            dqe = _bdot(do, sh_ref[cc, h])
            dq = dqe * eb
            db = dqe * qe
            dke = _bdot(v, dst_ref[h])
            dv = _bdot(ke, dst_ref[h], _NT)
            dk = dke * kdec
            rr = dke * ke
            db = db - rr
            db_last = (jnp.sum(rr, axis=0, keepdims=True)
                       + ebl * jnp.sum(dst_ref[h] * sh_ref[cc, h], axis=0, keepdims=True))
            dst_ref[h] = dst_ref[h] * ebl
            dst_ref[h] += _bdot(do, qe, _TN)
            dq_i, dk_i, dv_i, db_i = _intra_bwd(q, k, v, b, do, rows8, rowc8)
            dq, dk, dv = dq + dq_i, dk + dk_i, dv + dv_i
            db = db + db_i + (rowc == CH - 1).astype(F32) * db_last
            dgl = _rev_cumsum_rows(db, rows)
            df = dgl / f - dk
            dlb_ref[:, sl] += jnp.sum(df * (1.0 - sig), axis=0, keepdims=True)
            dz_ref[rs, sl] = (dq * (sq * (1.0 + hq * (1.0 - sq)))).astype(BF)
            dz_ref[rs, D + HK * h:D + HK * (h + 1)] = (df * (1.0 - lb) * sig * (1.0 - sig)).astype(BF)
            dz_ref[rs, 2 * D + HK * h:2 * D + HK * (h + 1)] = dv.astype(BF)
            dz_ref[rs, 3 * D + HK * h:3 * D + HK * (h + 1)] = d_hg.astype(BF)
        dnw_ref[...] += dnw
        if side is not None:
            @pl.when(pl.program_id(0) == NCH // hstep - 1)
            def _():
                side.last(s_ins, s_outs, s_sem_refs)

    rev = lambda i: NCH // hstep - 1 - i
    zblk = lambda c: pl.BlockSpec((CH * hstep, D), lambda i, c=c: (rev(i), c))
    out = pl.pallas_call(
        body, name="hgrn_bwd", grid=(NCH // hstep,),
        in_specs=[zblk(0), zblk(1), zblk(2), zblk(3),
                  pl.BlockSpec((2, D), lambda i: (0, 0)), pl.BlockSpec((1, HK), lambda i: (0, 0)),
                  zblk(0), zblk(0),
                  pl.BlockSpec((hstep, HEADS, HK, HK), lambda i: (rev(i), 0, 0, 0))] + s_in_specs,
        out_specs=[pl.BlockSpec((CH * hstep, 4 * D), lambda i: (rev(i), 0)),
                   pl.BlockSpec((1, D), lambda i: (0, 0)), pl.BlockSpec((1, HK), lambda i: (0, 0))] + s_out_specs,
        out_shape=[jax.ShapeDtypeStruct((T, 4 * D), BF), jax.ShapeDtypeStruct((1, D), F32),
                   jax.ShapeDtypeStruct((1, HK), F32)] + s_shapes,
        scratch_shapes=[pltpu.VMEM((HEADS, HK, HK), F32)] + s_sems,
        compiler_params=_params(("arbitrary",)),
    )(z, z, z, z, lbl, nw, oraw, dog, shist, *s_arrays)
    return out[0], out[1], out[2], out[3:]


BLK = 128
NBLK = T // BLK
QK_SCALE = 0.125


def _head_masks():
    lane = lax.broadcasted_iota(jnp.int32, (1, BLK), 1)
    return [(lane < 64).astype(F32), (lane >= 64).astype(F32)]


def _pieces(dil):
    m = T // dil
    out = []
    for r in range(dil):
        for j in range(m // BLK):
            start = r + dil * BLK * j
            rows = pl.ds(start, BLK, stride=dil) if dil > 1 else pl.ds(start, BLK)
            out.append((rows, r * m + BLK * j))
    return out


def _rope(x, c, sa, sb):
    return x * c + pltpu.roll(x, 96, axis=1) * sa + pltpu.roll(x, 32, axis=1) * sb


def _rope_t(d, c, sa, sb):
    return d * c + pltpu.roll(d * sa, 32, axis=1) + pltpu.roll(d * sb, 96, axis=1)


def _rope_and_regroup(dil, q_ref, k_ref, v_ref, tables, stage_q, stage_k, qr_ref, kr_ref, vr_ref):
    cos_ref, sa_ref, sb_ref = tables
    to_q, to_k = (qr_ref, kr_ref) if dil == 1 else (stage_q, stage_k)
    for c in range(T // BLK):
        rows = pl.ds(BLK * c, BLK)
        cs, sa, sb = cos_ref[rows, :], sa_ref[rows, :], sb_ref[rows, :]
        to_q[rows, :] = (_rope(q_ref[rows, :], cs, sa, sb) * QK_SCALE).astype(to_q.dtype)
        to_k[rows, :] = _rope(k_ref[rows, :], cs, sa, sb).astype(to_k.dtype)
    for rows, dst in _pieces(dil):
        drows = pl.ds(dst, BLK)
        if dil > 1:
            qr_ref[drows, :] = stage_q[rows, :].astype(qr_ref.dtype)
            kr_ref[drows, :] = stage_k[rows, :].astype(kr_ref.dtype)
        vr_ref[drows, :] = v_ref[rows, :].astype(vr_ref.dtype)


def _window_bias(bias_ref):
    ii = lax.broadcasted_iota(jnp.int32, (2 * BLK, BLK), 0) % BLK
    jj = lax.broadcasted_iota(jnp.int32, (2 * BLK, BLK), 1)
    bias_ref[0] = jnp.where(jj <= ii, 0.0, -jnp.inf)
    bias_ref[1] = jnp.where(jj >= ii, 0.0, -jnp.inf)


def _blocks(bi):
    if isinstance(bi, int):
        return pl.ds(bi * BLK, BLK), pl.ds(max(bi - 1, 0) * BLK, BLK)
    return (pl.ds(pl.multiple_of(bi * BLK, BLK), BLK),
            pl.ds(pl.multiple_of(jnp.maximum(bi - 1, 0) * BLK, BLK), BLK))


def _stack_heads(x, masks):
    return jnp.concatenate([x * masks[0].astype(x.dtype), x * masks[1].astype(x.dtype)], axis=0).astype(BF)


def _attn_fwd(z, cos, sa, sb):
    def body(q_ref, k_ref, v_ref, ag_ref, cos_ref, sa_ref, sb_ref, ob_ref, opre_ref, lse_ref, qr_ref, kr_ref, vr_ref,
             bias_ref, og_ref, lg_ref, otok_ref, ltok_ref, sc_ref):
        g = pl.program_id(1)
        masks = _head_masks()

        @pl.when(g == 0)
        def _():
            _window_bias(bias_ref)

        def group(gi):
            dil = ATT_GROUPS[gi][1]
            nblk = (T // dil) // BLK
            _rope_and_regroup(dil, q_ref, k_ref, v_ref, (cos_ref, sa_ref, sb_ref), lg_ref.at[0], lg_ref.at[1],
                              qr_ref, kr_ref, vr_ref)

            def scores(bi, slot):
                cur, prev = _blocks(bi)
                q2 = _stack_heads(qr_ref[cur, :], masks)
                sc_ref[slot, 0] = _dot(q2, kr_ref[cur, :], _NT) + bias_ref[0]
                if nblk > 1:
                    sc_ref[slot, 1] = (_dot(q2, kr_ref[prev, :], _NT)
                                       + (bias_ref[1] + jnp.where((bi % nblk) != 0, 0.0, -jnp.inf)))

            def finish(bi, slot):
                cur, prev = _blocks(bi)
                s_c, vc = sc_ref[slot, 0], vr_ref[cur, :]
                if nblk > 1:
                    s_p, vp = sc_ref[slot, 1], vr_ref[prev, :]
                    mx = jnp.max(jnp.maximum(s_c, s_p), axis=1, keepdims=True)
                    p_c, p_p = jnp.exp(s_c - mx), jnp.exp(s_p - mx)
                    den = jnp.sum(p_c + p_p, axis=1, keepdims=True)
                    oh = _dot(p_c.astype(BF), vc) + _dot(p_p.astype(BF), vp)
                else:
                    mx = jnp.max(s_c, axis=1, keepdims=True)
                    p_c = jnp.exp(s_c - mx)
                    den = jnp.sum(p_c, axis=1, keepdims=True)
                    oh = _dot(p_c.astype(BF), vc)
                on = oh / den
                lsev = jnp.broadcast_to(mx + jnp.log(den), (2 * BLK, BLK))
                og_ref[cur, :] = on[:BLK] * masks[0] + on[BLK:] * masks[1]
                lg_ref[0, cur, :] = lsev[:BLK]
                lg_ref[1, cur, :] = lsev[BLK:]

            def pair(j, carry):
                finish(2 * j, 0)
                scores(2 * j + 1, 1)
                finish(2 * j + 1, 1)
                scores(jnp.minimum(2 * j + 2, NBLK - 1), 0)
                return carry

            scores(0, 0)
            lax.fori_loop(0, NBLK // 2, pair, 0)
            for rows, src in _pieces(dil):
                srows = pl.ds(src, BLK)
                otok_ref[gi, rows, :] = og_ref[srows, :]
                ltok_ref[gi, 0, rows, :] = lg_ref[0, srows, :]
                ltok_ref[gi, 1, rows, :] = lg_ref[1, srows, :]

        for gi in range(3):
            pl.when(g == gi)(functools.partial(group, gi))

        @pl.when(g == 2)
        def _():
            for c in range(T // BLK):
                rows = pl.ds(BLK * c, BLK)
                wts = []
                for hh in range(2):
                    l0, l1, l2 = ltok_ref[0, hh, rows, :], ltok_ref[1, hh, rows, :], ltok_ref[2, hh, rows, :]
                    mx = jnp.maximum(jnp.maximum(l0, l1), l2)
                    e0, e1, e2 = jnp.exp(l0 - mx), jnp.exp(l1 - mx), jnp.exp(l2 - mx)
                    tot = e0 + e1 + e2
                    lse_ref[rows, BLK * hh:BLK * (hh + 1)] = mx + jnp.log(tot)
                    inv = 1.0 / tot
                    wts.append([e0 * inv, e1 * inv, e2 * inv])
                o = sum((wts[0][gi] * masks[0] + wts[1][gi] * masks[1]) * otok_ref[gi, rows, :] for gi in range(3))
                ag = ag_ref[rows, :]
                opre_ref[rows, :] = o
                ob_ref[rows, :] = (o * (ag * _sigmoid(ag))).astype(BF)

    c0 = ATT_COL0 // BLK
    zspec = lambda part: pl.BlockSpec((T, BLK), lambda p, g, part=part: (0, c0 + 12 * part + 4 * g + p))
    outspec = pl.BlockSpec((T, BLK), lambda p, g: (0, p))
    table = pl.BlockSpec((T, BLK), lambda p, g: (0, 0))
    regrouped = pl.BlockSpec((None, T, BLK), lambda p, g: (g, 0, p))
    big = lambda: pltpu.VMEM((T, BLK), F32)
    return pl.pallas_call(
        body, name="attn_fwd", grid=(4, 3),
        in_specs=[zspec(0), zspec(1), zspec(2),
                  pl.BlockSpec((T, BLK), lambda p, g: (0, AG_COL0 // BLK + p)), table, table, table],
        out_specs=[outspec, outspec, pl.BlockSpec((T, 2 * BLK), lambda p, g: (0, p)), regrouped, regrouped, regrouped],
        out_shape=[jax.ShapeDtypeStruct((T, 512), BF), jax.ShapeDtypeStruct((T, 512), F32),
                   jax.ShapeDtypeStruct((T, 8 * BLK), F32)] + [jax.ShapeDtypeStruct((3, T, 512), BF)] * 3,
        scratch_shapes=[pltpu.VMEM((2, 2 * BLK, BLK), F32), big(),
                        pltpu.VMEM((2, T, BLK), F32), pltpu.VMEM((3, T, BLK), F32), pltpu.VMEM((3, 2, T, BLK), F32),
                        pltpu.VMEM((2, 2, 2 * BLK, BLK), F32)],
        compiler_params=_params(("parallel", "arbitrary")),
    )(z, z, z, z, cos, sa, sb)


def _attn_bwd(z, qs, ks, vs, cos, sa, sb, opre, lse, dob):
    def body(qs_ref, ks_ref, vs_ref, ag_ref, cos_ref, sa_ref, sb_ref, o_ref, lse0_ref, lse1_ref, dob_ref,
             dq_ref, dk_ref, dv_ref, dag_ref,
             bias_ref, dtok_ref, qr_ref, kr_ref, vr_ref, dor_ref, lr_ref, dr_ref,
             dqr_ref, dkr_ref, dvr_ref, pd_ref, dotok_ref):
        g = pl.program_id(1)
        masks = _head_masks()

        @pl.when(g == 0)
        def _():
            _window_bias(bias_ref)
            for c in range(T // BLK):
                rows = pl.ds(BLK * c, BLK)
                ag, dob_v, o = ag_ref[rows, :], dob_ref[rows, :], o_ref[rows, :]
                sg = _sigmoid(ag)
                dag_ref[rows, :] = (dob_v * o * (sg * (1.0 + ag * (1.0 - sg)))).astype(BF)
                do = dob_v * (ag * sg)
                dotok_ref[rows, :] = do
                prod = do * o
                for hh, mh in enumerate(masks):
                    dtok_ref[hh, rows, :] = jnp.broadcast_to(jnp.sum(prod * mh, axis=1, keepdims=True), (BLK, BLK))

        def group(gi):
            dil = ATT_GROUPS[gi][1]
            nblk = (T // dil) // BLK
            for rows, dst in _pieces(dil):
                drows = pl.ds(dst, BLK)
                dor_ref[drows, :] = dotok_ref[rows, :]
                for hh, lse_ref in enumerate((lse0_ref, lse1_ref)):
                    lr_ref[hh, drows, :] = lse_ref[rows, :]
                    dr_ref[hh, drows, :] = dtok_ref[hh, rows, :]
            dkr_ref[...] = jnp.zeros_like(dkr_ref)
            dvr_ref[...] = jnp.zeros_like(dvr_ref)

            def probs(bi, slot):
                cur, prev = _blocks(bi)
                q2, do2 = _stack_heads(qs_ref[cur, :], masks), _stack_heads(dor_ref[cur, :], masks)
                lh = jnp.concatenate([lr_ref[0, cur, :], lr_ref[1, cur, :]], axis=0)
                dh = jnp.concatenate([dr_ref[0, cur, :], dr_ref[1, cur, :]], axis=0)
                p_c = jnp.exp(_dot(q2, ks_ref[cur, :], _NT) + bias_ref[0] - lh)
                pd_ref[slot, 0] = p_c.astype(BF)
                pd_ref[slot, 1] = (p_c * (_dot(do2, vs_ref[cur, :], _NT) - dh)).astype(BF)
                if nblk > 1:
                    bias_p = bias_ref[1] + jnp.where((bi % nblk) != 0, 0.0, -jnp.inf)
                    p_p = jnp.exp(_dot(q2, ks_ref[prev, :], _NT) + bias_p - lh)
                    pd_ref[slot, 2] = p_p.astype(BF)
                    pd_ref[slot, 3] = (p_p * (_dot(do2, vs_ref[prev, :], _NT) - dh)).astype(BF)

            def grads(bi, slot):
                cur, prev = _blocks(bi)
                q2, do2 = _stack_heads(qs_ref[cur, :], masks), _stack_heads(dor_ref[cur, :], masks)
                p_c, ds_c = pd_ref[slot, 0], pd_ref[slot, 1]
                dq2 = _dot(ds_c, ks_ref[cur, :])
                dkr_ref[cur, :] += _dot(ds_c, q2, _TN)
                dvr_ref[cur, :] += _dot(p_c, do2, _TN)
                if nblk > 1:
                    p_p, ds_p = pd_ref[slot, 2], pd_ref[slot, 3]
                    dq2 = dq2 + _dot(ds_p, ks_ref[prev, :])
                    dkr_ref[prev, :] += _dot(ds_p, q2, _TN)
                    dvr_ref[prev, :] += _dot(p_p, do2, _TN)
                dqr_ref[cur, :] = dq2[:BLK] * masks[0] + dq2[BLK:] * masks[1]

            def pair(j, carry):
                grads(2 * j, 0)
                probs(2 * j + 1, 1)
                grads(2 * j + 1, 1)
                probs(jnp.minimum(2 * j + 2, NBLK - 1), 0)
                return carry

            probs(0, 0)
            lax.fori_loop(0, NBLK // 2, pair, 0)
            if dil > 1:
                for rows, src in _pieces(dil):
                    srows = pl.ds(src, BLK)
                    qr_ref[rows, :] = dqr_ref[srows, :]
                    kr_ref[rows, :] = dkr_ref[srows, :]
                    vr_ref[rows, :] = dvr_ref[srows, :]
            tq, tk, tv = (qr_ref, kr_ref, vr_ref) if dil > 1 else (dqr_ref, dkr_ref, dvr_ref)
            for c in range(T // BLK):
                rows = pl.ds(BLK * c, BLK)
                cs, sa, sb = cos_ref[rows, :], sa_ref[rows, :], sb_ref[rows, :]
                dq_ref[rows, :] = _rope_t(tq[rows, :] * QK_SCALE, cs, sa, sb).astype(BF)
                dk_ref[rows, :] = _rope_t(tk[rows, :], cs, sa, sb).astype(BF)
                dv_ref[rows, :] = tv[rows, :].astype(BF)

        for gi in range(3):
            pl.when(g == gi)(functools.partial(group, gi))

    regrouped = pl.BlockSpec((None, T, BLK), lambda p, g: (g, 0, p))
    pspec = pl.BlockSpec((T, BLK), lambda p, g: (0, p))
    gspec = pl.BlockSpec((T, BLK), lambda p, g: (0, 4 * g + p))
    table = pl.BlockSpec((T, BLK), lambda p, g: (0, 0))
    big = lambda: pltpu.VMEM((T, BLK), F32)
    two = lambda: pltpu.VMEM((2, T, BLK), F32)
    return pl.pallas_call(
        body, name="attn_bwd", grid=(4, 3),
        in_specs=[regrouped, regrouped, regrouped,
                  pl.BlockSpec((T, BLK), lambda p, g: (0, AG_COL0 // BLK + p)), table, table, table,
                  pspec, pl.BlockSpec((T, BLK), lambda p, g: (0, 2 * p)),
                  pl.BlockSpec((T, BLK), lambda p, g: (0, 2 * p + 1)), pspec],
        out_specs=[gspec, gspec, gspec, pspec],
        out_shape=[jax.ShapeDtypeStruct((T, 1536), BF), jax.ShapeDtypeStruct((T, 1536), BF),
                   jax.ShapeDtypeStruct((T, 1536), BF), jax.ShapeDtypeStruct((T, 512), BF)],
        scratch_shapes=[pltpu.VMEM((2, 2 * BLK, BLK), F32), two(), big(), big(), big(), big(),
                        two(), two(), big(), big(), big(), pltpu.VMEM((2, 4, 2 * BLK, BLK), BF), big()],
        compiler_params=_params(("parallel", "arbitrary")),
    )(qs, ks, vs, z, cos, sa, sb, opre, lse, lse, dob)


def _merge_out_loss(og, ob, z, w_a, w_b, w_out, x, tgt, wf):
    tm = 512

    def body(og_ref, ob_ref, ga_ref, gb_ref, wa_ref, wb_ref, wo_ref, x_ref, t_ref, wf_ref,
             ya_ref, yb_ref, m_ref, dout_ref, loss_ref, gwf_ref):
        @pl.when(pl.program_id(0) == 0)
        def _():
            loss_ref[...] = jnp.zeros_like(loss_ref)
            gwf_ref[...] = jnp.zeros_like(gwf_ref)

        ya, yb = _dot(og_ref[...], wa_ref[...]), _dot(ob_ref[...], wb_ref[...])
        ya_ref[...] = ya
        yb_ref[...] = yb
        m = (_sigmoid(ga_ref[...]) * ya + _sigmoid(gb_ref[...]) * yb).astype(BF)
        m_ref[...] = m
        out = x_ref[...] + _dot(m, wo_ref[...])
        r = lax.rsqrt(jnp.mean(out * out, axis=-1, keepdims=True) + EPS)
        yh = out * r
        wfv = wf_ref[...]
        err = yh * wfv - t_ref[...]
        loss_ref[...] += jnp.sum(err * err, axis=0, keepdims=True) * (0.5 / D)
        dy = err * (1.0 / D)
        gwf_ref[...] += jnp.sum(dy * yh, axis=0, keepdims=True)
        dyh = dy * wfv
        dout_ref[...] = r * (dyh - yh * jnp.mean(dyh * yh, axis=-1, keepdims=True))

    row = pl.BlockSpec((tm, D), lambda i: (i, 0))
    vec = pl.BlockSpec((1, D), lambda i: (0, 0))
    whole = lambda w: pl.BlockSpec(w.shape, lambda i: (0, 0))
    return pl.pallas_call(
        body, name="merge_out_loss", grid=(T // tm,),
        in_specs=[row, pl.BlockSpec((tm, ob.shape[1]), lambda i: (i, 0)),
                  pl.BlockSpec((tm, D), lambda i: (i, GATE_COL0 // D)),
                  pl.BlockSpec((tm, D), lambda i: (i, GATE_COL0 // D + 1)),
                  whole(w_a), whole(w_b), whole(w_out), row, row, vec],
        out_specs=[row, row, row, row, vec, vec],
        out_shape=[jax.ShapeDtypeStruct((T, D), F32), jax.ShapeDtypeStruct((T, D), F32),
                   jax.ShapeDtypeStruct((T, D), BF), jax.ShapeDtypeStruct((T, D), F32),
                   jax.ShapeDtypeStruct((1, D), F32), jax.ShapeDtypeStruct((1, D), F32)],
        compiler_params=_params(("arbitrary",)),
    )(og, ob, z, z, w_a, w_b, w_out, x, tgt, wf)


def _merge_proj_bwd(dout, ya, yb, z, w_a, w_b, w_out):
    tm = 512

    def body(dout_ref, ya_ref, yb_ref, ga_ref, gb_ref, wa_ref, wb_ref, wo_ref,
             dya_ref, dyb_ref, dg_ref, dog_ref, dob_ref):
        dmv = _dot(dout_ref[...].astype(BF), wo_ref[...], _NT)
        sa, sb = _sigmoid(ga_ref[...]), _sigmoid(gb_ref[...])
        dya, dyb = (sa * dmv).astype(BF), (sb * dmv).astype(BF)
        dya_ref[...] = dya
        dyb_ref[...] = dyb
        dg_ref[:, :D] = (dmv * ya_ref[...] * sa * (1.0 - sa)).astype(BF)
        dg_ref[:, D:] = (dmv * yb_ref[...] * sb * (1.0 - sb)).astype(BF)
        dog_ref[...] = _dot(dya, wa_ref[...], _NT)
        dob_ref[...] = _dot(dyb, wb_ref[...], _NT)

    row = pl.BlockSpec((tm, D), lambda i: (i, 0))
    whole = lambda w: pl.BlockSpec(w.shape, lambda i: (0, 0))
    nb = w_b.shape[0]
    return pl.pallas_call(
        body, name="merge_proj_bwd", grid=(T // tm,),
        in_specs=[row, row, row, pl.BlockSpec((tm, D), lambda i: (i, GATE_COL0 // D)),
                  pl.BlockSpec((tm, D), lambda i: (i, GATE_COL0 // D + 1)), whole(w_a), whole(w_b), whole(w_out)],
        out_specs=[row, row, pl.BlockSpec((tm, 2 * D), lambda i: (i, 0)), row,
                   pl.BlockSpec((tm, nb), lambda i: (i, 0))],
        out_shape=[jax.ShapeDtypeStruct((T, D), BF), jax.ShapeDtypeStruct((T, D), BF),
                   jax.ShapeDtypeStruct((T, 2 * D), BF), jax.ShapeDtypeStruct((T, D), F32),
                   jax.ShapeDtypeStruct((T, nb), F32)],
        compiler_params=_params(("parallel",)),
    )(dout, ya, yb, z, z, w_a, w_b, w_out)


def _rope_inv_freq():
    inv = ROPE_THETA ** (-jnp.arange(0, 64, 2, dtype=F32) / 64)
    return jnp.tile(inv, 4).reshape(1, BLK)


def _local_step(x, pos, norm_w, lbl, hnw, wf, tgt, w_in, w_a, w_b, w_out, shard_shapes=()):
    invf = _rope_inv_freq()
    if shard_shapes:
        blk = jnp.reshape(2 * lax.axis_index("x") + lax.axis_index("y"), (1,)).astype(jnp.int32)
        h, cos, sa, sb, z_own, (w_near,) = _norm_and_rope_tables(
            x, norm_w, pos, invf, side=_gather_near_side(w_in, WEIGHT_AXES[0]), own=(w_in, blk))
        near = jnp.concatenate([blk ^ 2, blk ^ 1])
        z, (w_diag,) = _z_blocks(h, w_near, z_own, jnp.concatenate([near, near]), 2, name="z_proj_near",
                                 side=_gather_diag_side(w_near, w_in.shape, WEIGHT_AXES[0]))
        z, w_in, _ = _z_blocks(h, w_diag, z, jnp.concatenate([blk ^ 3, jnp.zeros_like(blk)]), 1, name="z_proj_diag",
                               fill=w_near, side=None)
        oraw, og, shist, (w_a, w_b, w_out) = _hgrn_fwd(
            z, lbl, hnw, side=_gather_side([w_a, w_b, w_out], WEIGHT_AXES[1:]))
    else:
        h, cos, sa, sb, _, _ = _norm_and_rope_tables(x, norm_w, pos, invf)
        z = _matmul(h, w_in, tm=T, tn=512, name="z_proj")
        oraw, og, shist, _ = _hgrn_fwd(z, lbl, hnw)
    ob, opre, lse, qs, ks, vs = _attn_fwd(z, cos, sa, sb)
    ya, yb, merged, dout, loss_vec, g_wf = _merge_out_loss(og, ob, z, w_a, w_b, w_out, x, tgt, wf)

    dya, dyb, dgates, dog, dob = _merge_proj_bwd(dout, ya, yb, z, w_a, w_b, w_out)
    g_wout = _matmul(merged, dout, ta=True, out_dtype=BF, tm=512, tn=1024, name="g_wout")
    g_wa = _matmul(og, dya, ta=True, out_dtype=BF, tm=512, tn=1024, name="g_wa")
    g_wb = _matmul(ob, dyb, ta=True, out_dtype=BF, tm=512, tn=1024, name="g_wb")
    small = [g_wa, g_wb, g_wout]
    side_s = side_w = None
    if shard_shapes:
        p3_s = _rs_partials(small, shard_shapes[1:], WEIGHT_AXES[1:], "small")
        side_s = _chip_exchange_side(p3_s, shard_shapes[1:], WEIGHT_AXES[1:])
    dz_h, dlb, g_hnw, land_s = _hgrn_bwd(z, lbl, hnw, oraw, dog, shist, side=side_s)
    dq, dk, dv, dag = _attn_bwd(z, qs, ks, vs, cos, sa, sb, opre, lse, dob)
    dz_parts = [dz_h, dq, dk, dv, dag, dgates]
    if shard_shapes:
        c = lax.axis_index("c")
        half = lambda i: jnp.reshape(i, (1,)).astype(jnp.int32)
        g_send = _grad_w_in_half(h, dz_parts, half(1 - c))
        g_keep, (g_sib,) = _grad_w_in_half(h, dz_parts, half(c), side=_sibling_send_side(g_send))
        p3_w = [_add_bf16(g_keep, g_sib, "pair_sum_w_in").reshape(1, D // 2, NIN)]
        side_w = _chip_exchange_relay_side(p3_w[0], shard_shapes[0])
    else:
        g_big = [_grad_w_in(h, dz_parts)] + small
    gx, g_nw, land_w = _grad_x(dz_parts, w_in, x, dout, norm_w, side=side_w)
    small_sums = None
    if shard_shapes:
        g_big, small_sums = _rs_finish(p3_w + p3_s, [land_w[0]] + list(land_s), shard_shapes, WEIGHT_AXES,
                                       (g_nw, dlb, g_hnw, g_wf, loss_vec))
    return dict(loss_vec=loss_vec, gx=gx, g_nw=g_nw, dlb=dlb, g_hnw=g_hnw, g_wf=g_wf, small_sums=small_sums,
                g_win=g_big[0], g_wa=g_big[1], g_wb=g_big[2], g_wout=g_big[3])


MESH = pl.DeviceIdType.MESH
HBM = pl.BlockSpec(memory_space=pl.ANY)
WEIGHT_AXES = (1, 0, 1, 0)


def _place():
    x, y, c = lax.axis_index("x"), lax.axis_index("y"), lax.axis_index("c")
    chips = [(1 - x, y), (x, 1 - y), (1 - x, 1 - y)]
    return x, y, c, chips


def _block_half(ref, shard_shape, axis, j, half):
    r, c = shard_shape
    hr = r // 2
    if axis == 0:
        return ref.at[pl.ds(pl.multiple_of(j * r + half * hr, 16), hr), :]
    return ref.at[pl.ds(pl.multiple_of(half * hr, 16), hr), pl.ds(pl.multiple_of(j * c, 128), c)]


class _Side:
    def __init__(self, arrays, out_shapes, sems, first, last, mid=None):
        self.arrays, self.out_shapes, self.sems, self.first, self.last = arrays, out_shapes, sems, first, last
        self.mid = mid


def _gather_side(shards, axes):
    n = len(shards)
    shapes = [s.shape for s in shards]

    def copies(ins, outs, sems):
        send1, recv1, send2, recv2, send0, recv0 = sems
        x, y, c, chips = _place()
        me = 2 * x + y
        sib = (x, y, 1 - c)
        near = ((1 - c) * (1 - x) + c * x, (1 - c) * y + c * (1 - y))
        far = ((1 - c) * x + c * (1 - x), (1 - c) * (1 - y) + c * y)
        out = []
        for a in range(n):
            r, cc = shapes[a]
            mine = (outs[a].at[pl.ds(pl.multiple_of(me * r, 16), r), :] if axes[a] == 0
                    else outs[a].at[:, pl.ds(pl.multiple_of(me * cc, 128), cc)])
            own = pltpu.make_async_remote_copy(
                src_ref=ins[a], dst_ref=mine, send_sem=send0.at[a], recv_sem=recv0.at[a],
                device_id=sib, device_id_type=MESH)
            src = ins[a].at[pl.ds(pl.multiple_of(c * (r // 2), 16), r // 2), :]
            sends = [pltpu.make_async_remote_copy(
                src_ref=src, dst_ref=_block_half(outs[a], shapes[a], axes[a], me, c),
                send_sem=send1.at[a, k], recv_sem=recv1.at[a, k], device_id=(*chips[k], c), device_id_type=MESH)
                for k in range(2)]

            def region(chip, half):
                return _block_half(outs[a], shapes[a], axes[a], 2 * chip[0] + chip[1], half)

            def arrival(chip, k):
                reg = region(chip, c)
                return pltpu.make_async_remote_copy(
                    src_ref=reg, dst_ref=reg, send_sem=send1.at[a, k], recv_sem=recv1.at[a, k],
                    device_id=(*chip, c), device_id_type=MESH)

            def to_sibling(chip, k):
                reg = region(chip, c)
                return pltpu.make_async_remote_copy(
                    src_ref=reg, dst_ref=reg, send_sem=send2.at[a, k], recv_sem=recv2.at[a, k],
                    device_id=sib, device_id_type=MESH)

            def from_sibling(chip, k):
                reg = region(chip, 1 - c)
                return pltpu.make_async_remote_copy(
                    src_ref=reg, dst_ref=reg, send_sem=send2.at[a, k], recv_sem=recv2.at[a, k],
                    device_id=sib, device_id_type=MESH)

            relay = pltpu.make_async_remote_copy(
                src_ref=region(near, c), dst_ref=region(near, c), send_sem=send1.at[a, 2], recv_sem=recv1.at[a, 2],
                device_id=(*far, c), device_id_type=MESH)
            hops = [(arrival(near, c), to_sibling(near, c)), (arrival(far, 1 - c), to_sibling(far, 1 - c)),
                    (arrival(chips[2], 2), to_sibling(chips[2], 2))]
            back = [from_sibling(chips[k], k) for k in range(3)]
            out.append((own, sends, relay, hops, back))
        return out

    def first(ins, outs, sems):
        for own, sends, _, _, _ in copies(ins, outs, sems):
            own.start()
            for cp in sends:
                cp.start()

    def mid(ins, outs, sems):
        per_array = copies(ins, outs, sems)
        for step in range(2):
            for _, _, relay, hops, _ in per_array:
                arrived, onward = hops[step]
                arrived.wait_recv()
                if step == 0:
                    relay.start()
                onward.start()

    def last(ins, outs, sems):
        per_array = copies(ins, outs, sems)
        for _, _, _, hops, _ in per_array:
            arrived, onward = hops[2]
            arrived.wait_recv()
            onward.start()
        for own, sends, relay, hops, back in per_array:
            for cp in back:
                cp.wait_recv()
            for cp in sends + [relay] + [onward for _, onward in hops]:
                cp.wait_send()
            own.wait()

    full = [(4 * r, c) if ax == 0 else (r, 4 * c) for (r, c), ax in zip(shapes, axes)]
    sems = [pltpu.SemaphoreType.DMA((n, 3)), pltpu.SemaphoreType.DMA((n, 3)),
            pltpu.SemaphoreType.DMA((n, 3)), pltpu.SemaphoreType.DMA((n, 3)),
            pltpu.SemaphoreType.DMA((n,)), pltpu.SemaphoreType.DMA((n,))]
    return _Side(list(shards), [jax.ShapeDtypeStruct(f, BF) for f in full], sems, first, last, mid)


def _gather_near_side(shard, axis):
    shape = shard.shape
    r, cc = shape

    def copies(ins, outs, sems):
        send1, recv1, send2, recv2, send0, recv0 = sems
        x, y, c, chips = _place()
        me = 2 * x + y
        sib = (x, y, 1 - c)
        mine = (outs[0].at[pl.ds(pl.multiple_of(me * r, 16), r), :] if axis == 0
                else outs[0].at[:, pl.ds(pl.multiple_of(me * cc, 128), cc)])
        own = pltpu.make_async_remote_copy(
            src_ref=ins[0], dst_ref=mine, send_sem=send0.at[0], recv_sem=recv0.at[0],
            device_id=sib, device_id_type=MESH)
        src = ins[0].at[pl.ds(pl.multiple_of(c * (r // 2), 16), r // 2), :]

        def region(k, half):
            return _block_half(outs[0], shape, axis, 2 * chips[k][0] + chips[k][1], half)

        def moves(k):
            return [pltpu.make_async_remote_copy(
                        src_ref=s, dst_ref=d, send_sem=ss.at[k], recv_sem=rs.at[k], device_id=dev,
                        device_id_type=MESH)
                    for s, d, ss, rs, dev in (
                        (src, _block_half(outs[0], shape, axis, me, c), send1, recv1, (*chips[k], c)),
                        (region(k, c), region(k, c), send1, recv1, (*chips[k], c)),
                        (region(k, c), region(k, c), send2, recv2, sib),
                        (region(k, 1 - c), region(k, 1 - c), send2, recv2, sib))]

        return own, [moves(k) for k in range(2)]

    def first(ins, outs, sems):
        own, per_chip = copies(ins, outs, sems)
        own.start()
        for send, _, _, _ in per_chip:
            send.start()

    def last(ins, outs, sems):
        own, per_chip = copies(ins, outs, sems)
        for _, arrived, onward, _ in per_chip:
            arrived.wait_recv()
            onward.start()
        for send, _, onward, back in per_chip:
            back.wait_recv()
            send.wait_send()
            onward.wait_send()
        own.wait()

    full = (4 * r, cc) if axis == 0 else (r, 4 * cc)
    sems = [pltpu.SemaphoreType.DMA((2,))] * 4 + [pltpu.SemaphoreType.DMA((1,))] * 2
    return _Side([shard], [jax.ShapeDtypeStruct(full, BF)], sems, first, last)


def _gather_diag_side(gathered, shape, axis):
    r, cc = shape

    def copies(ins, outs, sems):
        send1, recv1, send2, recv2 = sems
        x, y, c, _ = _place()
        sib = (x, y, 1 - c)
        near = ((1 - c) * (1 - x) + c * x, (1 - c) * y + c * (1 - y))
        far = ((1 - c) * x + c * (1 - x), (1 - c) * (1 - y) + c * y)

        def half(i):
            return outs[0].at[pl.ds(pl.multiple_of(i * (r // 2), 16), r // 2), :]

        def move(s, d, ss, rs, dev):
            return pltpu.make_async_remote_copy(
                src_ref=s, dst_ref=d, send_sem=ss.at[0], recv_sem=rs.at[0], device_id=dev, device_id_type=MESH)

        relay = move(_block_half(ins[0], shape, axis, 2 * near[0] + near[1], c), half(c), send1, recv1, (*far, c))
        arrived = move(half(c), half(c), send1, recv1, (*far, c))
        onward = move(half(c), half(c), send2, recv2, sib)
        back = move(half(1 - c), half(1 - c), send2, recv2, sib)
        return relay, arrived, onward, back

    def first(ins, outs, sems):
        copies(ins, outs, sems)[0].start()

    def last(ins, outs, sems):
        relay, arrived, onward, back = copies(ins, outs, sems)
        arrived.wait_recv()
        onward.start()
        back.wait_recv()
        relay.wait_send()
        onward.wait_send()

    return _Side([gathered], [jax.ShapeDtypeStruct(shape, BF)], [pltpu.SemaphoreType.DMA((1,))] * 4, first, last)


def _as3d(g, shard_shape, axis):
    r, c = shard_shape
    return g.reshape(4, r, c) if axis == 0 else g.reshape(1, r, 4 * c)


def _half_rows(ref3, hr, half):
    return ref3.at[:, pl.ds(pl.multiple_of(half * hr, 16), hr), :]


def _rs_pair_exchange(g3s, name):
    n = len(g3s)

    def body(*refs):
        ins, outs = refs[:n], refs[n:2 * n]
        send, recv = refs[2 * n:]
        x, y, c, _ = _place()
        cps = []
        for a in range(n):
            hr = g3s[a].shape[1] // 2
            cp = pltpu.make_async_remote_copy(
                src_ref=_half_rows(ins[a], hr, 1 - c), dst_ref=outs[a],
                send_sem=send.at[a], recv_sem=recv.at[a], device_id=(x, y, 1 - c), device_id_type=MESH)
            cp.start()
            cps.append(cp)
        for cp in cps:
            cp.wait()

    return pl.pallas_call(
        body, name=name,
        in_specs=[HBM] * n, out_specs=[HBM] * n,
        out_shape=[jax.ShapeDtypeStruct((g.shape[0], g.shape[1] // 2, g.shape[2]), BF) for g in g3s],
        scratch_shapes=[pltpu.SemaphoreType.DMA((n,)), pltpu.SemaphoreType.DMA((n,))],
    )(*g3s)


def _pair_sums(g3s, lands, cidx, name):
    n = len(g3s)

    def body(c_ref, *refs):
        for g_ref, l_ref, o_ref in zip(refs[:n], refs[n:2 * n], refs[2 * n:]):
            o_ref[...] = (g_ref[...].astype(F32) + l_ref[...].astype(F32)).astype(BF)

    halves = [(g.shape[0], g.shape[1] // 2, g.shape[2]) for g in g3s]
    return pl.pallas_call(
        body, name=name,
        grid_spec=pltpu.PrefetchScalarGridSpec(
            num_scalar_prefetch=1, grid=(1,),
            in_specs=[pl.BlockSpec(h, lambda i, c: (0, c[0], 0)) for h in halves]
                     + [pl.BlockSpec(h, lambda i, c: (0, 0, 0)) for h in halves],
            out_specs=[pl.BlockSpec(h, lambda i, c: (0, 0, 0)) for h in halves]),
        out_shape=[jax.ShapeDtypeStruct(h, BF) for h in halves],
        compiler_params=_params(("arbitrary",)),
    )(cidx, *g3s, *lands)


def _chip_exchange_side(p3s, shapes, axes):
    n = len(p3s)

    def copies(ins, outs, sems):
        send, recv = sems
        x, y, c, chips = _place()
        cps = []
        for a in range(n):
            r, cc = shapes[a]
            for k, (px, py) in enumerate(chips):
                j = 2 * px + py
                src = ins[a].at[j] if axes[a] == 0 else ins[a].at[0, :, pl.ds(pl.multiple_of(j * cc, 128), cc)]
                cps.append(pltpu.make_async_remote_copy(
                    src_ref=src, dst_ref=outs[a].at[k], send_sem=send.at[a, k], recv_sem=recv.at[a, k],
                    device_id=(px, py, c), device_id_type=MESH))
        return cps

    def first(ins, outs, sems):
        for cp in copies(ins, outs, sems):
            cp.start()

    def last(ins, outs, sems):
        for cp in copies(ins, outs, sems):
            cp.wait()

    return _Side(list(p3s), [jax.ShapeDtypeStruct((3, r // 2, c), BF) for r, c in shapes],
                 [pltpu.SemaphoreType.DMA((n, 3)), pltpu.SemaphoreType.DMA((n, 3))], first, last)


def _chip_exchange_relay_side(p3, shape):
    r, cc = shape
    hr = r // 2
    rows = 64

    def copies(ins, outs, sems):
        send, recv, local, mine, theirs = sems
        x, y, c, chips = _place()
        near = ((1 - c) * (1 - x) + c * x, (1 - c) * y + c * (1 - y))
        far = ((1 - c) * x + c * (1 - x), (1 - c) * (1 - y) + c * y)
        land, staged = outs

        def block(chip):
            return ins[0].at[0, :, pl.ds(pl.multiple_of((2 * chip[0] + chip[1]) * cc, 128), cc)]

        def move(s, d, k, dev):
            return pltpu.make_async_remote_copy(
                src_ref=s, dst_ref=d, send_sem=send.at[k], recv_sem=recv.at[k], device_id=dev, device_id_type=MESH)

        return dict(
            direct=move(block(near), land.at[c], 0, (*near, c)),
            for_relay=move(block(chips[2]), staged, 1, (*near, c)),
            summed=move(mine, land.at[1 - c], 2, (*far, c)),
            direct_in=move(land.at[c], land.at[c], 0, (*near, c)),
            staged_in=move(staged, staged, 1, (*near, c)),
            summed_in=move(land.at[1 - c], land.at[1 - c], 2, (*far, c)),
            load_mine=pltpu.make_async_copy(block(far), mine, local.at[0]),
            load_theirs=pltpu.make_async_copy(staged, theirs, local.at[1]))

    def first(ins, outs, sems):
        cps = copies(ins, outs, sems)
        cps["for_relay"].start()
        cps["direct"].start()

    def mid(ins, outs, sems):
        cps = copies(ins, outs, sems)
        mine, theirs = sems[3], sems[4]
        cps["load_mine"].start()
        cps["staged_in"].wait_recv()
        cps["load_theirs"].start()
        cps["load_mine"].wait()
        cps["load_theirs"].wait()

        def add(i, carry):
            rs = pl.ds(pl.multiple_of(i * rows, 16), rows)
            mine[rs, :] = (mine[rs, :].astype(F32) + theirs[rs, :].astype(F32)).astype(BF)
            return carry

        lax.fori_loop(0, hr // rows, add, 0)
        cps["summed"].start()

    def last(ins, outs, sems):
        cps = copies(ins, outs, sems)
        cps["direct_in"].wait_recv()
        cps["summed_in"].wait_recv()
        for name in ("direct", "for_relay", "summed"):
            cps[name].wait_send()

    sems = [pltpu.SemaphoreType.DMA((3,)), pltpu.SemaphoreType.DMA((3,)), pltpu.SemaphoreType.DMA((2,)),
            pltpu.VMEM((hr, cc), BF), pltpu.VMEM((hr, cc), BF)]
    return _Side([p3], [jax.ShapeDtypeStruct((2, hr, cc), BF), jax.ShapeDtypeStruct((hr, cc), BF)], sems,
                 first, last, mid)


def _chip_sum(p3, land, shard_shape, axis, idx, name):
    r, c = shard_shape
    hr = r // 2
    tr = 64
    nt = hr // tr
    slots = land.shape[0]

    def body(idx_ref, p_ref, l_ref, o_ref):
        acc = p_ref[...].astype(F32)
        for k in range(slots):
            acc = acc + l_ref[k].astype(F32)
        o_ref[...] = acc

    own = (pl.BlockSpec((None, tr, c), lambda i, idx: (idx[0], i, 0)) if axis == 0
           else pl.BlockSpec((None, tr, c), lambda i, idx: (0, i, idx[0])))
    return pl.pallas_call(
        body, name=name,
        grid_spec=pltpu.PrefetchScalarGridSpec(
            num_scalar_prefetch=1, grid=(nt,),
            in_specs=[own, pl.BlockSpec((slots, tr, c), lambda i, idx: (0, i, 0))],
            out_specs=pl.BlockSpec((tr, c), lambda i, idx: (idx[1] * nt + i, 0))),
        out_shape=jax.ShapeDtypeStruct((r, c), F32),
        compiler_params=_params(("parallel",)),
    )(idx, p3, land)


def _chip_sums(p3s, lands, shapes, axes, idx, name):
    n = len(p3s)

    def body(idx_ref, *refs):
        for p_ref, l_ref, o_ref in zip(refs[:n], refs[n:2 * n], refs[2 * n:]):
            acc = p_ref[...].astype(F32)
            for k in range(l_ref.shape[0]):
                acc = acc + l_ref[k].astype(F32)
            o_ref[...] = acc

    own = [pl.BlockSpec((None, r // 2, c), (lambda i, idx: (idx[0], 0, 0)) if ax == 0 else (lambda i, idx: (0, 0, idx[0])))
           for (r, c), ax in zip(shapes, axes)]
    return pl.pallas_call(
        body, name=name,
        grid_spec=pltpu.PrefetchScalarGridSpec(
            num_scalar_prefetch=1, grid=(1,),
            in_specs=own + [pl.BlockSpec(l.shape, lambda i, idx: (0, 0, 0)) for l in lands],
            out_specs=[pl.BlockSpec((r // 2, c), lambda i, idx: (idx[1], 0)) for r, c in shapes]),
        out_shape=[jax.ShapeDtypeStruct((r, c), F32) for r, c in shapes],
        compiler_params=_params(("arbitrary",)),
    )(idx, *p3s, *lands)


def _rs_pair_gather(fulls, small):
    n = len(fulls)

    def body(*refs):
        ins, small_refs, outs, red_ref = refs[:n], refs[n:n + 5], refs[n + 5:2 * n + 5], refs[2 * n + 5]
        send, recv = refs[2 * n + 6:2 * n + 8]
        x, y, c, _ = _place()
        cps = []
        for a in range(n):
            hr = fulls[a].shape[0] // 2
            rows = pl.ds(pl.multiple_of(c * hr, 8), hr)
            cp = pltpu.make_async_remote_copy(
                src_ref=ins[a].at[rows, :], dst_ref=outs[a].at[rows, :], send_sem=send.at[a], recv_sem=recv.at[a],
                device_id=(x, y, 1 - c), device_id_type=MESH)
            cp.start()
            cps.append(cp)
        _small_all_reduce(small_refs, red_ref, *refs[2 * n + 8:])
        for a, cp in enumerate(cps):
            cp.wait_send()
            hr = fulls[a].shape[0] // 2
            other = pl.ds(pl.multiple_of((1 - c) * hr, 8), hr)
            pltpu.make_async_remote_copy(
                src_ref=ins[a].at[other, :], dst_ref=outs[a].at[other, :], send_sem=send.at[a], recv_sem=recv.at[a],
                device_id=(x, y, 1 - c), device_id_type=MESH).wait_recv()

    vm = pl.BlockSpec(memory_space=pltpu.VMEM)
    out = pl.pallas_call(
        body, name="grads_pair_gather",
        in_specs=[HBM] * n + [vm] * 5, out_specs=[HBM] * n + [vm],
        out_shape=[jax.ShapeDtypeStruct(f.shape, F32) for f in fulls] + [jax.ShapeDtypeStruct((NSMALL, D), F32)],
        input_output_aliases={a: a for a in range(n)},
        scratch_shapes=[pltpu.SemaphoreType.DMA((n,)), pltpu.SemaphoreType.DMA((n,)),
                        pltpu.VMEM((NSMALL, D), F32), pltpu.VMEM((8, NSMALL, D), F32),
                        pltpu.SemaphoreType.DMA((7,)), pltpu.SemaphoreType.DMA((7,))],
    )(*fulls, *small)
    return out[:n], out[n]


def _sibling_send_side(arr):
    def copy(ins, outs, sems):
        x, y, c, _ = _place()
        return pltpu.make_async_remote_copy(
            src_ref=ins[0], dst_ref=outs[0], send_sem=sems[0].at[0], recv_sem=sems[1].at[0],
            device_id=(x, y, 1 - c), device_id_type=MESH)

    return _Side([arr], [jax.ShapeDtypeStruct(arr.shape, arr.dtype)],
                 [pltpu.SemaphoreType.DMA((1,)), pltpu.SemaphoreType.DMA((1,))],
                 lambda ins, outs, sems: copy(ins, outs, sems).start(),
                 lambda ins, outs, sems: copy(ins, outs, sems).wait())


def _add_bf16(a, b, name):
    r, c = a.shape
    tr = 64

    def body(a_ref, b_ref, o_ref):
        o_ref[...] = (a_ref[...].astype(F32) + b_ref[...].astype(F32)).astype(BF)

    blk = pl.BlockSpec((tr, c), lambda i: (i, 0))
    return pl.pallas_call(
        body, name=name, grid=(r // tr,), in_specs=[blk, blk], out_specs=blk,
        out_shape=jax.ShapeDtypeStruct((r, c), BF), compiler_params=_params(("parallel",)),
    )(a, b)


def _rs_partials(grads, shapes, axes, tag):
    cidx = jnp.reshape(lax.axis_index("c"), (1,)).astype(jnp.int32)
    g3s = [_as3d(g, s, ax) for g, s, ax in zip(grads, shapes, axes)]
    lands = _rs_pair_exchange(g3s, f"grads_pair_exchange_{tag}")
    return list(_pair_sums(g3s, lands, cidx, f"pair_sums_{tag}"))


def _rs_finish(p3s, landed, shapes, axes, small):
    x, y, c = lax.axis_index("x"), lax.axis_index("y"), lax.axis_index("c")
    idx = jnp.stack([2 * x + y, c]).astype(jnp.int32)
    fulls = [_chip_sum(p3s[0], landed[0], shapes[0], axes[0], idx, "chip_sum_w_in")]
    fulls += _chip_sums(p3s[1:], landed[1:], shapes[1:], axes[1:], idx, "chip_sums_branches_out")
    return _rs_pair_gather(fulls, small)


NSMALL = 8


def _small_all_reduce(small_refs, out_ref, pack_ref, buf_ref, send, recv):
    nw_ref, lb_ref, hn_ref, wf_ref, ls_ref = small_refs
    x, y, c = lax.axis_index("x"), lax.axis_index("y"), lax.axis_index("c")
    me = 4 * x + 2 * y + c
    pack_ref[...] = jnp.zeros_like(pack_ref)
    pack_ref[0:1, :] = nw_ref[...]
    pack_ref[1:2, :] = lb_ref[...]
    pack_ref[2:3, 0:HK] = hn_ref[...]
    pack_ref[3:4, :] = wf_ref[...]
    pack_ref[4:5, :] = ls_ref[...]
    buf_ref[me] = pack_ref[...]
    cps = []
    for d in range(1, 8):
        dx, dy, dc = d >> 2, (d >> 1) & 1, d & 1
        peer = (1 - x if dx else x, 1 - y if dy else y, 1 - c if dc else c)
        cp = pltpu.make_async_remote_copy(
            src_ref=pack_ref, dst_ref=buf_ref.at[me], send_sem=send.at[d - 1], recv_sem=recv.at[d - 1],
            device_id=peer, device_id_type=MESH)
        cp.start()
        cps.append(cp)
    for d in range(1, 8):
        dx, dy, dc = d >> 2, (d >> 1) & 1, d & 1
        src = 4 * (1 - x if dx else x) + 2 * (1 - y if dy else y) + (1 - c if dc else c)
        pltpu.make_async_remote_copy(
            src_ref=pack_ref, dst_ref=buf_ref.at[src], send_sem=send.at[d - 1], recv_sem=recv.at[d - 1],
            device_id=(x, y, c), device_id_type=MESH).wait_recv()
    for cp in cps:
        cp.wait_send()
    acc = buf_ref[0]
    for i in range(1, 8):
        acc = acc + buf_ref[i]
    out_ref[...] = acc


def _adamw_math(w, g, m, v):
    m = B1 * m + (1.0 - B1) * g
    v = B2 * v + (1.0 - B2) * (g * g)
    m_hat = m / (1.0 - B1 ** STEP)
    v_hat = v / (1.0 - B2 ** STEP)
    return -LR * (m_hat / (jnp.sqrt(v_hat) + ADAM_EPS) + WD * w), m, v


def _adamw(w, g, m, v, name):
    r, c = w.shape
    tr = 128

    def body(w_ref, g_ref, m_ref, v_ref, d_ref, nm_ref, nv_ref, go_ref):
        g = g_ref[...]
        d_ref[...], nm_ref[...], nv_ref[...] = _adamw_math(w_ref[...], g, m_ref[...], v_ref[...])
        go_ref[...] = g

    blk = pl.BlockSpec((tr, c), lambda i: (i, 0))
    return pl.pallas_call(
        body, name=name, grid=(r // tr,), in_specs=[blk] * 4, out_specs=[blk] * 4,
        out_shape=[jax.ShapeDtypeStruct((r, c), F32)] * 4,
        compiler_params=_params(("parallel",)),
    )(w, g, m, v)


def _adamw_whole(groups, name):
    n = len(groups)

    def body(*refs):
        ins, outs = refs[:4 * n], refs[4 * n:]
        for a in range(n):
            w_ref, g_ref, m_ref, v_ref = ins[4 * a:4 * a + 4]
            g = g_ref[...]
            outs[4 * a][...], outs[4 * a + 1][...], outs[4 * a + 2][...] = _adamw_math(
                w_ref[...], g, m_ref[...], v_ref[...])
            outs[4 * a + 3][...] = g

    vm = pl.BlockSpec(memory_space=pltpu.VMEM)
    out = pl.pallas_call(
        body, name=name, in_specs=[vm] * (4 * n), out_specs=[vm] * (4 * n),
        out_shape=[jax.ShapeDtypeStruct(grp[0].shape, F32) for grp in groups for _ in range(4)],
        compiler_params=_params(),
    )(*[a for grp in groups for a in grp])
    return [out[4 * a:4 * a + 4] for a in range(n)]


def _small_update(red, lbl, params):
    def body(red_ref, *refs):
        ins, outs = refs[:12], refs[12:]
        lb = _lower_bound(ins[3][...])
        dl0 = red_ref[1:2, :] * lb * (1.0 - lb)
        row = lax.broadcasted_iota(jnp.int32, (2, D), 0)
        grads = [red_ref[0:1, :], jnp.where(row == 0, dl0, -dl0), red_ref[2:3, 0:HK], red_ref[3:4, :]]
        for i, g in enumerate(grads):
            w, m, v = ins[3 * i][...], ins[3 * i + 1][...], ins[3 * i + 2][...]
            d, nm, nv = _adamw_math(w, g, m, v)
            outs[4 * i][...] = g
            outs[4 * i + 1][...] = d
            outs[4 * i + 2][...] = nm
            outs[4 * i + 3][...] = nv
        outs[16][...] = jnp.sum(red_ref[4:5, :], axis=1, keepdims=True)

    flat = [a for p in params for a in p]
    vm = pl.BlockSpec(memory_space=pltpu.VMEM)
    shapes = [jax.ShapeDtypeStruct(p[0].shape, F32) for p in params for _ in range(4)]
    return pl.pallas_call(
        body, name="small_update",
        in_specs=[vm] * 13, out_specs=[vm] * 17,
        out_shape=shapes + [jax.ShapeDtypeStruct((1, 1), F32)],
    )(red, *flat)


def kernel(x, positions, norm_w, w_in, lb_logits, hgrn_norm_w, w_branch_a, w_branch_b, w_out, final_norm_w, loss_target, m_norm_w, m_w_in, m_lb_logits, m_hgrn_norm_w, m_w_branch_a, m_w_branch_b, m_w_out, m_final_norm_w, v_norm_w, v_w_in, v_lb_logits, v_hgrn_norm_w, v_w_branch_a, v_w_branch_b, v_w_out, v_final_norm_w):
    big_w = [w_in[0], w_branch_a[0], w_branch_b[0], w_out[0]]
    big_m = [m_w_in[0], m_w_branch_a[0], m_w_branch_b[0], m_w_out[0]]
    big_v = [v_w_in[0], v_w_branch_a[0], v_w_branch_b[0], v_w_out[0]]
    shapes = [w.shape for w in big_w]
    wf = final_norm_w.reshape(1, D)

    shards = [w.astype(BF) for w in big_w]
    loc = _local_step(x[0], positions.reshape(T, 1), norm_w, lb_logits, hgrn_norm_w, wf, loss_target[0],
                      *shards, shard_shapes=shapes)
    g_big = [loc["g_win"], loc["g_wa"], loc["g_wb"], loc["g_wout"]]
    red = loc["small_sums"]

    small = _small_update(red, lb_logits, [
        (norm_w, m_norm_w, v_norm_w), (lb_logits, m_lb_logits, v_lb_logits),
        (hgrn_norm_w, m_hgrn_norm_w, v_hgrn_norm_w),
        (wf, m_final_norm_w.reshape(1, D), v_final_norm_w.reshape(1, D))])
    loss = small[16].reshape(())
    sg, sd, sm, sv = ([small[4 * i + j] for i in range(4)] for j in range(4))
    for lst in (sg, sd, sm, sv):
        lst[3] = lst[3].reshape(D)
    per_w = list(zip(big_w, g_big, big_m, big_v))
    upd = [_adamw(*per_w[0], "adamw_w_in")] + _adamw_whole(per_w[1:], "adamw_branches_out")
    bd, bm, bv, bg = ([u[j][None] for u in upd] for j in range(4))

    def order(s, b):
        return [s[0], b[0], s[1], s[2], b[1], b[2], b[3], s[3]]

    return (loss, loc["gx"][None], *order(sg, bg), *order(sd, bd), *order(sm, bm), *order(sv, bv))
```

```python
import functools

import jax
import jax.numpy as jnp
from jax import lax
from jax.experimental import pallas as pl
from jax.experimental.pallas import tpu as pltpu

T = 2048
D = 1024
NIN = 11264
HEADS = 8
HK = 128
CH = 16
NCH = T // CH
HSTEP = 2
ATT_GROUPS = ((128, 1), (512, 4), (2048, 16))
ATT_COL0 = 4096
AG_COL0 = 8704
GATE_COL0 = 9216
EPS = 1e-6
ROPE_THETA = 10000.0
LR, B1, B2, ADAM_EPS, WD, STEP = 0.001, 0.9, 0.999, 1e-08, 0.01, 10

F32 = jnp.float32
BF = jnp.bfloat16
VMEM_LIMIT = 56 * 1024 * 1024

_NN = (((1,), (0,)), ((), ()))
_NT = (((1,), (1,)), ((), ()))
_TN = (((0,), (0,)), ((), ()))


def _dot(a, b, dims=_NN):
    return lax.dot_general(a, b, dims, preferred_element_type=F32)


def _bdot(a, b, dims=_NN):
    return lax.dot_general(a.astype(BF), b.astype(BF), dims, preferred_element_type=F32)


def _sigmoid(x):
    return jax.nn.sigmoid(x)


def _params(sem=None):
    return pltpu.CompilerParams(dimension_semantics=sem, vmem_limit_bytes=VMEM_LIMIT)


def _matmul(a, b, *, ta=False, tb=False, out_dtype=F32, tm=512, tn=512, tk=None, name, side=None):
    m = a.shape[1] if ta else a.shape[0]
    kdim = a.shape[0] if ta else a.shape[1]
    n = b.shape[0] if tb else b.shape[1]
    tk = tk or kdim
    tm, tn = min(tm, m), min(tn, n)
    nm, nn, nk = m // tm, n // tn, kdim // tk
    dims = (((0 if ta else 1,), (1 if tb else 0,)), ((), ()))
    s_arrays, s_in_specs, s_shapes, s_out_specs, s_sems = _side_io(side)
    na, no = len(s_arrays), len(s_shapes)
    nacc = 1 if nk > 1 else 0

    def body(*refs):
        a_ref, b_ref = refs[:2]
        s_ins, o_ref, s_outs = refs[2:2 + na], refs[2 + na], refs[3 + na:3 + na + no]
        scratch = refs[3 + na + no:]
        s_sem_refs = scratch[nacc:]
        i, j, k = pl.program_id(0), pl.program_id(1), pl.program_id(2)
        if side is not None:
            @pl.when((i == 0) & (j == 0) & (k == 0))
            def _():
                side.first(s_ins, s_outs, s_sem_refs)

        prod = _bdot(a_ref[...], b_ref[...], dims)
        if nk == 1:
            o_ref[...] = prod.astype(out_dtype)
        else:
            acc = scratch[0]

            @pl.when(k == 0)
            def _():
                acc[...] = prod

            @pl.when(k > 0)
            def _():
                acc[...] += prod

            @pl.when(k == nk - 1)
            def _():
                o_ref[...] = acc[...].astype(out_dtype)

        if side is not None:
            @pl.when((i == nm - 1) & (j == nn - 1) & (k == nk - 1))
            def _():
                side.last(s_ins, s_outs, s_sem_refs)

    a_spec = pl.BlockSpec((tk, tm), lambda i, j, k: (k, i)) if ta else pl.BlockSpec((tm, tk), lambda i, j, k: (i, k))
    b_spec = pl.BlockSpec((tn, tk), lambda i, j, k: (j, k)) if tb else pl.BlockSpec((tk, tn), lambda i, j, k: (k, j))
    sem = ("parallel", "parallel", "arbitrary") if side is None else ("arbitrary",) * 3
    out = pl.pallas_call(
        body, name=name, grid=(nm, nn, nk),
        in_specs=[a_spec, b_spec] + s_in_specs,
        out_specs=[pl.BlockSpec((tm, tn), lambda i, j, k: (i, j))] + s_out_specs,
        out_shape=[jax.ShapeDtypeStruct((m, n), out_dtype)] + s_shapes,
        scratch_shapes=([pltpu.VMEM((tm, tn), F32)] if nk > 1 else []) + s_sems,
        compiler_params=_params(sem),
    )(a, b, *s_arrays)
    return out[0] if side is None else (out[0], out[1:])


DZ_TILE = 512


def _part_offsets(parts):
    counts = [p.shape[1] // DZ_TILE for p in parts]
    offs = [sum(counts[:i]) for i in range(len(parts))]
    return counts, offs


def _part_spec(rows, cnt, off, tile_axis):
    def index(*g):
        return (0 if rows is None else g[0], jnp.clip(g[tile_axis] - off, 0, cnt - 1))
    return index


def _grad_w_in(h, parts):
    counts, offs = _part_offsets(parts)
    n = len(parts)

    def body(h_ref, *refs):
        o_ref = refs[n]
        j = pl.program_id(0)
        for p_ref, cnt, off in zip(refs[:n], counts, offs):
            @pl.when((j >= off) & (j < off + cnt))
            def _(p_ref=p_ref):
                o_ref[...] = _bdot(h_ref[...], p_ref[...], _TN).astype(BF)

    return pl.pallas_call(
        body, name="g_win", grid=(sum(counts),),
        in_specs=[pl.BlockSpec((T, D), lambda j: (0, 0))] +
                 [pl.BlockSpec((T, DZ_TILE), _part_spec(None, c, o, 0)) for c, o in zip(counts, offs)],
        out_specs=pl.BlockSpec((D, DZ_TILE), lambda j: (0, j)),
        out_shape=jax.ShapeDtypeStruct((D, NIN), BF),
        compiler_params=_params(("parallel",)),
    )(h, *parts)


def _grad_w_in_half(h, parts, half_idx, side=None):
    counts, offs = _part_offsets(parts)
    n = len(parts)
    nj = sum(counts)
    s_arrays, s_in_specs, s_shapes, s_out_specs, s_sems = _side_io(side)
    na, no = len(s_arrays), len(s_shapes)

    def body(idx_ref, h_ref, *refs):
        s_ins, o_ref, s_outs, s_sem_refs = refs[n:n + na], refs[n + na], refs[n + na + 1:n + na + 1 + no], refs[n + na + 1 + no:]
        j = pl.program_id(0)
        if side is not None:
            @pl.when(j == 0)
            def _():
                side.first(s_ins, s_outs, s_sem_refs)

        for p_ref, cnt, off in zip(refs[:n], counts, offs):
            @pl.when((j >= off) & (j < off + cnt))
            def _(p_ref=p_ref):
                o_ref[...] = _bdot(h_ref[...], p_ref[...], _TN).astype(BF)

        if side is not None:
            @pl.when(j == nj - 1)
            def _():
                side.last(s_ins, s_outs, s_sem_refs)

    def part_spec(cnt, off):
        return pl.BlockSpec((T, DZ_TILE), lambda j, idx: (0, jnp.clip(j - off, 0, cnt - 1)))

    out = pl.pallas_call(
        body, name="g_win_half" if side is None else "g_win_half_carrying",
        grid_spec=pltpu.PrefetchScalarGridSpec(
            num_scalar_prefetch=1, grid=(nj,),
            in_specs=[pl.BlockSpec((T, D // 2), lambda j, idx: (0, idx[0]))] +
                     [part_spec(c, o) for c, o in zip(counts, offs)] + s_in_specs,
            out_specs=[pl.BlockSpec((D // 2, DZ_TILE), lambda j, idx: (0, j))] + s_out_specs,
            scratch_shapes=s_sems),
        out_shape=[jax.ShapeDtypeStruct((D // 2, NIN), BF)] + s_shapes,
        compiler_params=_params(("parallel",) if side is None else ("arbitrary",)),
    )(half_idx, h, *parts, *s_arrays)
    return out[0] if side is None else (out[0], out[1:])


def _side_io(side):
    if side is None:
        return [], [], [], [], []
    return (side.arrays, [HBM] * len(side.arrays), side.out_shapes, [HBM] * len(side.out_shapes), side.sems)


def _grad_x(parts, w_in, x, dout, norm_w, side=None):
    counts, offs = _part_offsets(parts)
    n = len(parts)
    tm = 1024
    nm, nk = T // tm, sum(counts)
    s_arrays, s_in_specs, s_shapes, s_out_specs, s_sems = _side_io(side)
    na, no = len(s_arrays), len(s_shapes)

    def body(*refs):
        w_ref, x_ref, dout_ref, nw_ref = refs[n:n + 4]
        s_ins = refs[n + 4:n + 4 + na]
        gx_ref, gw_ref = refs[n + 4 + na:n + 6 + na]
        s_outs = refs[n + 6 + na:n + 6 + na + no]
        acc = refs[n + 6 + na + no]
        s_sem_refs = refs[n + 7 + na + no:]
        i, k = pl.program_id(0), pl.program_id(1)

        @pl.when((i == 0) & (k == 0))
        def _():
            gw_ref[...] = jnp.zeros_like(gw_ref)
            if side is not None:
                side.first(s_ins, s_outs, s_sem_refs)

        @pl.when(k == 0)
        def _():
            acc[...] = jnp.zeros_like(acc)

        if side is not None and side.mid is not None:
            @pl.when((i == nm - 1) & (k == 0))
            def _():
                side.mid(s_ins, s_outs, s_sem_refs)

        for p_ref, cnt, off in zip(refs[:n], counts, offs):
            @pl.when((k >= off) & (k < off + cnt))
            def _(p_ref=p_ref):
                acc[...] += _bdot(p_ref[...], w_ref[...], _NT)

        @pl.when(k == nk - 1)
        def _():
            gw = jnp.zeros((1, D), F32)
            for c in range(tm // BLK):
                rows = pl.ds(BLK * c, BLK)
                xv, dhv = x_ref[rows, :], acc[rows, :]
                r = lax.rsqrt(jnp.mean(xv * xv, axis=-1, keepdims=True) + EPS)
                nrm = xv * r
                dn = dhv * nw_ref[...]
                gw = gw + jnp.sum(dhv * nrm, axis=0, keepdims=True)
                gx_ref[rows, :] = dout_ref[rows, :] + r * (dn - nrm * jnp.mean(dn * nrm, axis=-1, keepdims=True))
            gw_ref[...] += gw

        if side is not None:
            @pl.when((i == nm - 1) & (k == nk - 1))
            def _():
                side.last(s_ins, s_outs, s_sem_refs)

    row = pl.BlockSpec((tm, D), lambda i, k: (i, 0))
    vec = pl.BlockSpec((1, D), lambda i, k: (0, 0))
    out = pl.pallas_call(
        body, name="grad_x", grid=(nm, nk),
        in_specs=[pl.BlockSpec((tm, DZ_TILE), _part_spec(0, c, o, 1)) for c, o in zip(counts, offs)] +
                 [pl.BlockSpec((D, DZ_TILE), lambda i, k: (0, k)), row, row, vec] + s_in_specs,
        out_specs=[row, vec] + s_out_specs,
        out_shape=[jax.ShapeDtypeStruct((T, D), F32), jax.ShapeDtypeStruct((1, D), F32)] + s_shapes,
        scratch_shapes=[pltpu.VMEM((tm, D), F32)] + s_sems,
        compiler_params=_params(("arbitrary", "arbitrary")),
    )(*parts, w_in, x, dout, norm_w, *s_arrays)
    return out[0], out[1], out[2:]


def _norm_and_rope_tables(x, w, pos, invf, side=None, own=None):
    tm = 256
    nm = T // tm
    s_arrays, s_in_specs, s_shapes, s_out_specs, s_sems = _side_io(side)
    na, no = len(s_arrays), len(s_shapes)
    nz = 0 if own is None else 1
    wsh, blk = own if own is not None else (None, jnp.zeros((1,), jnp.int32))

    def body(blk_ref, *refs):
        x_ref, w_ref, pos_ref, invf_ref = refs[:4]
        s_ins = refs[4 + nz:4 + nz + na]
        h_ref, cos_ref, sa_ref, sb_ref = refs[4 + nz + na:8 + nz + na]
        s_outs = refs[8 + 2 * nz + na:8 + 2 * nz + na + no]
        s_sem_refs = refs[8 + 2 * nz + na + no:]
        if side is not None:
            @pl.when(pl.program_id(0) == 0)
            def _():
                side.first(s_ins, s_outs, s_sem_refs)

        xv = x_ref[...]
        r = lax.rsqrt(jnp.mean(xv * xv, axis=-1, keepdims=True) + EPS)
        h = (xv * r * w_ref[...]).astype(BF)
        h_ref[...] = h
        if own is not None:
            refs[8 + nz + na][...] = _dot(h, refs[4][...])
        first = (lax.broadcasted_iota(jnp.int32, (tm, 128), 1) % 64) < 32
        ang = pos_ref[...].astype(F32) * invf_ref[...]
        s = jnp.sin(ang)
        cos_ref[...] = jnp.cos(ang)
        sa_ref[...] = jnp.where(first, -s, 0.0)
        sb_ref[...] = jnp.where(first, 0.0, s)
        if side is not None:
            @pl.when(pl.program_id(0) == nm - 1)
            def _():
                side.last(s_ins, s_outs, s_sem_refs)

    tab = pl.BlockSpec((tm, 128), lambda i, b: (i, 0))
    own_in = [] if own is None else [pl.BlockSpec(wsh.shape, lambda i, b: (0, 0))]
    own_out = [] if own is None else [pl.BlockSpec((tm, wsh.shape[1]), lambda i, b: (i, b[0]))]
    own_shape = [] if own is None else [jax.ShapeDtypeStruct((T, NIN), F32)]
    out = pl.pallas_call(
        body, name="norm_and_rope_tables",
        grid_spec=pltpu.PrefetchScalarGridSpec(
            num_scalar_prefetch=1, grid=(nm,),
            in_specs=[pl.BlockSpec((tm, D), lambda i, b: (i, 0)), pl.BlockSpec((1, D), lambda i, b: (0, 0)),
                      pl.BlockSpec((tm, 1), lambda i, b: (i, 0)), pl.BlockSpec((1, 128), lambda i, b: (0, 0))]
                     + own_in + s_in_specs,
            out_specs=[pl.BlockSpec((tm, D), lambda i, b: (i, 0)), tab, tab, tab] + own_out + s_out_specs,
            scratch_shapes=s_sems),
        out_shape=[jax.ShapeDtypeStruct((T, D), BF)] + [jax.ShapeDtypeStruct((T, 128), F32)] * 3 + own_shape + s_shapes,
        compiler_params=_params(("parallel",) if side is None else ("arbitrary",)),
    )(blk, x, w, pos, invf, *([] if own is None else [wsh]), *s_arrays)
    return out[0], out[1], out[2], out[3], (out[4] if own is not None else None), out[4 + nz:]


def _z_blocks(h, w, z, idx, nb, side, name, fill=None):
    tm, tn = 1024, NIN // 8
    s_arrays, s_in_specs, s_shapes, s_out_specs, s_sems = _side_io(side)
    na, no = len(s_arrays), len(s_shapes)
    nm, ns = T // tm, 2 * nb
    nf = 0 if fill is None else 1

    def col(first, i, s, b):
        return (0, b[first + s // 2] * 2 + s % 2)

    def body(idx_ref, h_ref, w_ref, zin_ref, *refs):
        s_ins = refs[nf:nf + na]
        o_ref = refs[nf + na]
        s_outs = refs[nf + na + 1 + nf:nf + na + 1 + nf + no]
        s_sem_refs = refs[nf + na + 1 + nf + no + nf:]
        i, s = pl.program_id(0), pl.program_id(1)

        if side is not None:
            @pl.when((i == 0) & (s == 0))
            def _():
                side.first(s_ins, s_outs, s_sem_refs)

        if fill is not None:
            tile = pl.ds(pl.multiple_of((idx_ref[0] * 2 + s) * tn, 128), tn)
            store = pltpu.make_async_copy(w_ref, refs[nf + na + 1].at[:, tile], refs[nf + na + 1 + nf + no].at[0])
            pl.when(i == 0)(store.start)
        o_ref[...] = _dot(h_ref[...], w_ref[...])
        if fill is not None:
            pl.when(i == 0)(store.wait)

        if side is not None:
            @pl.when((i == nm - 1) & (s == ns - 1))
            def _():
                side.last(s_ins, s_outs, s_sem_refs)

    fills = [] if fill is None else [fill]
    out = pl.pallas_call(
        body, name=name,
        grid_spec=pltpu.PrefetchScalarGridSpec(
            num_scalar_prefetch=1, grid=(nm, ns),
            in_specs=[pl.BlockSpec((tm, D), lambda i, s, b: (i, 0)), pl.BlockSpec((D, tn), functools.partial(col, nb)),
                      HBM] + [HBM] * nf + s_in_specs,
            out_specs=[pl.BlockSpec((tm, tn), lambda i, s, b: (i, col(0, i, s, b)[1]))] + [HBM] * nf + s_out_specs,
            scratch_shapes=[pltpu.SemaphoreType.DMA((1,))] * nf + s_sems),
        out_shape=[jax.ShapeDtypeStruct((T, NIN), F32)] + [jax.ShapeDtypeStruct(f.shape, f.dtype) for f in fills]
                  + s_shapes,
        input_output_aliases={3: 0, **({4: 1} if fill is not None else {})},
        compiler_params=_params(("arbitrary", "arbitrary")),
    )(idx, h, w, z, *fills, *s_arrays)
    return (out[0], *out[1:1 + nf], out[1 + nf:])


def _lower_bound(lbl):
    mx = jnp.max(lbl, axis=0, keepdims=True)
    e = jnp.exp(lbl - mx)
    return e[0:1] / jnp.sum(e, axis=0, keepdims=True)


def _cumsum_rows(g, rows):
    b = g
    sh = 1
    while sh < CH:
        b = b + jnp.where(rows >= sh, pltpu.roll(b, sh, axis=0), 0.0)
        sh *= 2
    return b


def _rev_cumsum_rows(g, rows):
    b = g
    sh = 1
    while sh < CH:
        b = b + jnp.where(rows < CH - sh, pltpu.roll(b, CH - sh, axis=0), 0.0)
        sh *= 2
    return b


SUB = CH // 2


def _direct_block(qb, kb, vb, bb, rows8):
    ob = jnp.zeros_like(qb)
    for s in range(SUB):
        e_s = jnp.exp(jnp.where(rows8 >= s, bb - bb[s:s + 1], -jnp.inf))
        ob = ob + jnp.sum(qb * e_s * kb[s:s + 1], axis=1, keepdims=True) * vb[s:s + 1]
    return ob


def _direct_block_bwd(qb, kb, vb, bb, dob, rows8, rowc8):
    dq = dk = dv = db = jnp.zeros_like(qb)
    for s in range(SUB):
        one = (rowc8 == s).astype(F32)
        ks, vs = kb[s:s + 1], vb[s:s + 1]
        e_s = jnp.exp(jnp.where(rows8 >= s, bb - bb[s:s + 1], -jnp.inf))
        qes = qb * e_s
        w = qes * ks
        a = jnp.sum(w, axis=1, keepdims=True)
        da = jnp.sum(dob * vs, axis=1, keepdims=True)
        dv = dv + one * jnp.sum(a * dob, axis=0, keepdims=True)
        dq = dq + da * e_s * ks
        dk = dk + one * jnp.sum(da * qes, axis=0, keepdims=True)
        u = da * w
        db = db + u - one * jnp.sum(u, axis=0, keepdims=True)
    return dq, dk, dv, db


def _cross_factors(q, k, b):
    ref = b[SUB - 1:SUB]
    e_hi, e_lo = jnp.exp(b[SUB:] - ref), jnp.exp(ref - b[:SUB])
    return q[SUB:] * e_hi, k[:SUB] * e_lo, e_hi, e_lo


def _intra_fwd(q, k, v, b, rows8):
    lo = _direct_block(q[:SUB], k[:SUB], v[:SUB], b[:SUB], rows8)
    hi = _direct_block(q[SUB:], k[SUB:], v[SUB:], b[SUB:], rows8)
    qe_hi, ke_lo, _, _ = _cross_factors(q, k, b)
    for s in range(SUB):
        hi = hi + jnp.sum(qe_hi * ke_lo[s:s + 1], axis=1, keepdims=True) * v[s:s + 1]
    return jnp.concatenate([lo, hi], axis=0)


def _intra_bwd(q, k, v, b, do, rows8, rowc8):
    dq_lo, dk_lo, dv_lo, db_lo = _direct_block_bwd(q[:SUB], k[:SUB], v[:SUB], b[:SUB], do[:SUB], rows8, rowc8)
    dq_hi, dk_hi, dv_hi, db_hi = _direct_block_bwd(q[SUB:], k[SUB:], v[SUB:], b[SUB:], do[SUB:], rows8, rowc8)
    qe_hi, ke_lo, e_hi, e_lo = _cross_factors(q, k, b)
    do_hi, v_lo = do[SUB:], v[:SUB]
    dqe = dke = jnp.zeros_like(qe_hi)
    for s in range(SUB):
        one = (rowc8 == s).astype(F32)
        a = jnp.sum(qe_hi * ke_lo[s:s + 1], axis=1, keepdims=True)
        da = jnp.sum(do_hi * v_lo[s:s + 1], axis=1, keepdims=True)
        dv_lo = dv_lo + one * jnp.sum(a * do_hi, axis=0, keepdims=True)
        dqe = dqe + da * ke_lo[s:s + 1]
        dke = dke + one * jnp.sum(da * qe_hi, axis=0, keepdims=True)
    u_hi, u_lo = dqe * qe_hi, dke * ke_lo
    d_ref = jnp.sum(u_lo, axis=0, keepdims=True) - jnp.sum(u_hi, axis=0, keepdims=True)
    db_lo = db_lo - u_lo + (rowc8 == SUB - 1).astype(F32) * d_ref
    cat = lambda lo, hi: jnp.concatenate([lo, hi], axis=0)
    return (cat(dq_lo, dq_hi + dqe * e_hi), cat(dk_lo + dke * e_lo, dk_hi), cat(dv_lo, dv_hi),
            cat(db_lo, db_hi + u_hi))


def _hgrn_fwd(z, lbl, nw, side=None):
    s_arrays, s_in_specs, s_shapes, s_out_specs, s_sems = _side_io(side)
    na, no = len(s_arrays), len(s_shapes)
    nsteps = NCH // HSTEP

    def body(hq_ref, hf_ref, hi_ref, hg_ref, lbl_ref, nw_ref, *refs):
        s_ins, (oraw_ref, og_ref, sh_ref) = refs[:na], refs[na:na + 3]
        s_outs, st_ref, s_sem_refs = refs[na + 3:na + 3 + no], refs[na + 3 + no], refs[na + 4 + no:]

        @pl.when(pl.program_id(0) == 0)
        def _():
            st_ref[...] = jnp.zeros_like(st_ref)
            if side is not None:
                side.first(s_ins, s_outs, s_sem_refs)

        lb_all = _lower_bound(lbl_ref[...])
        rows = lax.broadcasted_iota(jnp.int32, (CH, HK), 0)
        rows8 = lax.broadcasted_iota(jnp.int32, (SUB, HK), 0)
        nwv = nw_ref[...]
        for cc, h in [(cc, h) for cc in range(HSTEP) for h in range(HEADS)]:
            rs = slice(CH * cc, CH * (cc + 1))
            sl = slice(HK * h, HK * (h + 1))
            lb = lb_all[:, sl]
            hq, hf, v, hg = hq_ref[rs, sl], hf_ref[rs, sl], hi_ref[rs, sl], hg_ref[rs, sl]
            q = hq * _sigmoid(hq)
            f = lb + (1.0 - lb) * _sigmoid(hf)
            k = 1.0 - f
            b = _cumsum_rows(jnp.log(f), rows)
            sh_ref[cc, h] = st_ref[h]
            o = _bdot(q * jnp.exp(b), st_ref[h], _NT) + _intra_fwd(q, k, v, b, rows8)
            bl = b[CH - 1:CH]
            st_ref[h] = st_ref[h] * jnp.exp(bl)
            st_ref[h] += _bdot(v, k * jnp.exp(bl - b), _TN)
            oraw_ref[rs, sl] = o
            nrm = o * lax.rsqrt(jnp.mean(o * o, axis=1, keepdims=True) + EPS)
            og_ref[rs, sl] = (nrm * nwv * (hg * _sigmoid(hg))).astype(BF)

        if side is not None:
            @pl.when(pl.program_id(0) == nsteps // 2)
            def _():
                side.mid(s_ins, s_outs, s_sem_refs)

            @pl.when(pl.program_id(0) == nsteps - 1)
            def _():
                side.last(s_ins, s_outs, s_sem_refs)

    zblk = lambda c: pl.BlockSpec((CH * HSTEP, D), lambda i, c=c: (i, c))
    out = pl.pallas_call(
        body, name="hgrn_fwd", grid=(nsteps,),
        in_specs=[zblk(0), zblk(1), zblk(2), zblk(3),
                  pl.BlockSpec((2, D), lambda i: (0, 0)), pl.BlockSpec((1, HK), lambda i: (0, 0))] + s_in_specs,
        out_specs=[zblk(0), zblk(0),
                   pl.BlockSpec((HSTEP, HEADS, HK, HK), lambda i: (i, 0, 0, 0))] + s_out_specs,
        out_shape=[jax.ShapeDtypeStruct((T, D), F32), jax.ShapeDtypeStruct((T, D), BF),
                   jax.ShapeDtypeStruct((NCH, HEADS, HK, HK), F32)] + s_shapes,
        scratch_shapes=[pltpu.VMEM((HEADS, HK, HK), F32)] + s_sems,
        compiler_params=_params(("arbitrary",)),
    )(z, z, z, z, lbl, nw, *s_arrays)
    return out[0], out[1], out[2], out[3:]


def _hgrn_bwd(z, lbl, nw, oraw, dog, shist, side=None):
    hstep = 1
    s_arrays, s_in_specs, s_shapes, s_out_specs, s_sems = _side_io(side)
    na, no = len(s_arrays), len(s_shapes)

    def body(*refs):
        hq_ref, hf_ref, hi_ref, hg_ref, lbl_ref, nw_ref, oraw_ref, dog_ref, sh_ref = refs[:9]
        s_ins = refs[9:9 + na]
        dz_ref, dlb_ref, dnw_ref = refs[9 + na:12 + na]
        s_outs = refs[12 + na:12 + na + no]
        dst_ref = refs[12 + na + no]
        s_sem_refs = refs[13 + na + no:]

        @pl.when(pl.program_id(0) == 0)
        def _():
            dst_ref[...] = jnp.zeros_like(dst_ref)
            dlb_ref[...] = jnp.zeros_like(dlb_ref)
            dnw_ref[...] = jnp.zeros_like(dnw_ref)
            if side is not None:
                side.first(s_ins, s_outs, s_sem_refs)

        lb_all = _lower_bound(lbl_ref[...])
        rows = lax.broadcasted_iota(jnp.int32, (CH, HK), 0)
        rowc = lax.broadcasted_iota(jnp.int32, (CH, 1), 0)
        rows8 = lax.broadcasted_iota(jnp.int32, (SUB, HK), 0)
        rowc8 = lax.broadcasted_iota(jnp.int32, (SUB, 1), 0)
        nwv = nw_ref[...]
        dnw = jnp.zeros((1, HK), F32)
        for cc, h in [(cc, h) for cc in reversed(range(hstep)) for h in range(HEADS)]:
            rs = slice(CH * cc, CH * (cc + 1))
            sl = slice(HK * h, HK * (h + 1))
            lb = lb_all[:, sl]
            hq, hf, v, hg = hq_ref[rs, sl], hf_ref[rs, sl], hi_ref[rs, sl], hg_ref[rs, sl]
            o, dg_out = oraw_ref[rs, sl], dog_ref[rs, sl]
            sg = _sigmoid(hg)
            sil = hg * sg
            r = lax.rsqrt(jnp.mean(o * o, axis=1, keepdims=True) + EPS)
            nrm = o * r
            d_hg = dg_out * (nrm * nwv) * (sg * (1.0 + hg * (1.0 - sg)))
            dn = dg_out * nwv * sil
            dnw = dnw + jnp.sum(dg_out * nrm * sil, axis=0, keepdims=True)
            do = r * (dn - nrm * jnp.mean(dn * nrm, axis=1, keepdims=True))
            sq = _sigmoid(hq)
            q = hq * sq
            sig = _sigmoid(hf)
            f = lb + (1.0 - lb) * sig
            k = 1.0 - f
            b = _cumsum_rows(jnp.log(f), rows)
            eb = jnp.exp(b)
            qe = q * eb
            bl = b[CH - 1:CH]
            ebl = jnp.exp(bl)
            kdec = jnp.exp(bl - b)
            ke = k * kdec
            dqe = _bdot(do, sh_ref[cc, h])
            dq = dqe * eb
            db = dqe * qe
            dke = _bdot(v, dst_ref[h])
            dv = _bdot(ke, dst_ref[h], _NT)
            dk = dke * kdec
            rr = dke * ke
            db = db - rr
            db_last = (jnp.sum(rr, axis=0, keepdims=True)
                       + ebl * jnp.sum(dst_ref[h] * sh_ref[cc, h], axis=0, keepdims=True))
            dst_ref[h] = dst_ref[h] * ebl
            dst_ref[h] += _bdot(do, qe, _TN)
            dq_i, dk_i, dv_i, db_i = _intra_bwd(q, k, v, b, do, rows8, rowc8)
            dq, dk, dv = dq + dq_i, dk + dk_i, dv + dv_i
            db = db + db_i + (rowc == CH - 1).astype(F32) * db_last
            dgl = _rev_cumsum_rows(db, rows)
            df = dgl / f - dk
            dlb_ref[:, sl] += jnp.sum(df * (1.0 - sig), axis=0, keepdims=True)
            dz_ref[rs, sl] = (dq * (sq * (1.0 + hq * (1.0 - sq)))).astype(BF)
            dz_ref[rs, D + HK * h:D + HK * (h + 1)] = (df * (1.0 - lb) * sig * (1.0 - sig)).astype(BF)
            dz_ref[rs, 2 * D + HK * h:2 * D + HK * (h + 1)] = dv.astype(BF)
            dz_ref[rs, 3 * D + HK * h:3 * D + HK * (h + 1)] = d_hg.astype(BF)
        dnw_ref[...] += dnw
        if side is not None:
            @pl.when(pl.program_id(0) == NCH // hstep - 1)
            def _():
                side.last(s_ins, s_outs, s_sem_refs)

    rev = lambda i: NCH // hstep - 1 - i
    zblk = lambda c: pl.BlockSpec((CH * hstep, D), lambda i, c=c: (rev(i), c))
    out = pl.pallas_call(
        body, name="hgrn_bwd", grid=(NCH // hstep,),
        in_specs=[zblk(0), zblk(1), zblk(2), zblk(3),
                  pl.BlockSpec((2, D), lambda i: (0, 0)), pl.BlockSpec((1, HK), lambda i: (0, 0)),
                  zblk(0), zblk(0),
                  pl.BlockSpec((hstep, HEADS, HK, HK), lambda i: (rev(i), 0, 0, 0))] + s_in_specs,
        out_specs=[pl.BlockSpec((CH * hstep, 4 * D), lambda i: (rev(i), 0)),
                   pl.BlockSpec((1, D), lambda i: (0, 0)), pl.BlockSpec((1, HK), lambda i: (0, 0))] + s_out_specs,
        out_shape=[jax.ShapeDtypeStruct((T, 4 * D), BF), jax.ShapeDtypeStruct((1, D), F32),
                   jax.ShapeDtypeStruct((1, HK), F32)] + s_shapes,
        scratch_shapes=[pltpu.VMEM((HEADS, HK, HK), F32)] + s_sems,
        compiler_params=_params(("arbitrary",)),
    )(z, z, z, z, lbl, nw, oraw, dog, shist, *s_arrays)
    return out[0], out[1], out[2], out[3:]


BLK = 128
NBLK = T // BLK
QK_SCALE = 0.125


def _head_masks():
    lane = lax.broadcasted_iota(jnp.int32, (1, BLK), 1)
    return [(lane < 64).astype(F32), (lane >= 64).astype(F32)]


def _pieces(dil):
    m = T // dil
    out = []
    for r in range(dil):
        for j in range(m // BLK):
            start = r + dil * BLK * j
            rows = pl.ds(start, BLK, stride=dil) if dil > 1 else pl.ds(start, BLK)
            out.append((rows, r * m + BLK * j))
    return out


def _rope(x, c, sa, sb):
    return x * c + pltpu.roll(x, 96, axis=1) * sa + pltpu.roll(x, 32, axis=1) * sb


def _rope_t(d, c, sa, sb):
    return d * c + pltpu.roll(d * sa, 32, axis=1) + pltpu.roll(d * sb, 96, axis=1)


def _rope_and_regroup(dil, q_ref, k_ref, v_ref, tables, stage_q, stage_k, qr_ref, kr_ref, vr_ref):
    cos_ref, sa_ref, sb_ref = tables
    to_q, to_k = (qr_ref, kr_ref) if dil == 1 else (stage_q, stage_k)
    for c in range(T // BLK):
        rows = pl.ds(BLK * c, BLK)
        cs, sa, sb = cos_ref[rows, :], sa_ref[rows, :], sb_ref[rows, :]
        to_q[rows, :] = (_rope(q_ref[rows, :], cs, sa, sb) * QK_SCALE).astype(to_q.dtype)
        to_k[rows, :] = _rope(k_ref[rows, :], cs, sa, sb).astype(to_k.dtype)
    for rows, dst in _pieces(dil):
        drows = pl.ds(dst, BLK)
        if dil > 1:
            qr_ref[drows, :] = stage_q[rows, :].astype(qr_ref.dtype)
            kr_ref[drows, :] = stage_k[rows, :].astype(kr_ref.dtype)
        vr_ref[drows, :] = v_ref[rows, :].astype(vr_ref.dtype)


def _window_bias(bias_ref):
    ii = lax.broadcasted_iota(jnp.int32, (2 * BLK, BLK), 0) % BLK
    jj = lax.broadcasted_iota(jnp.int32, (2 * BLK, BLK), 1)
    bias_ref[0] = jnp.where(jj <= ii, 0.0, -jnp.inf)
    bias_ref[1] = jnp.where(jj >= ii, 0.0, -jnp.inf)


def _blocks(bi):
    if isinstance(bi, int):
        return pl.ds(bi * BLK, BLK), pl.ds(max(bi - 1, 0) * BLK, BLK)
    return (pl.ds(pl.multiple_of(bi * BLK, BLK), BLK),
            pl.ds(pl.multiple_of(jnp.maximum(bi - 1, 0) * BLK, BLK), BLK))


def _stack_heads(x, masks):
    return jnp.concatenate([x * masks[0].astype(x.dtype), x * masks[1].astype(x.dtype)], axis=0).astype(BF)


def _attn_fwd(z, cos, sa, sb):
    def body(q_ref, k_ref, v_ref, ag_ref, cos_ref, sa_ref, sb_ref, ob_ref, opre_ref, lse_ref, qr_ref, kr_ref, vr_ref,
             bias_ref, og_ref, lg_ref, otok_ref, ltok_ref, sc_ref):
        g = pl.program_id(1)
        masks = _head_masks()

        @pl.when(g == 0)
        def _():
            _window_bias(bias_ref)

        def group(gi):
            dil = ATT_GROUPS[gi][1]
            nblk = (T // dil) // BLK
            _rope_and_regroup(dil, q_ref, k_ref, v_ref, (cos_ref, sa_ref, sb_ref), lg_ref.at[0], lg_ref.at[1],
                              qr_ref, kr_ref, vr_ref)

            def scores(bi, slot):
                cur, prev = _blocks(bi)
                q2 = _stack_heads(qr_ref[cur, :], masks)
                sc_ref[slot, 0] = _dot(q2, kr_ref[cur, :], _NT) + bias_ref[0]
                if nblk > 1:
                    sc_ref[slot, 1] = (_dot(q2, kr_ref[prev, :], _NT)
                                       + (bias_ref[1] + jnp.where((bi % nblk) != 0, 0.0, -jnp.inf)))

            def finish(bi, slot):
                cur, prev = _blocks(bi)
                s_c, vc = sc_ref[slot, 0], vr_ref[cur, :]
                if nblk > 1:
                    s_p, vp = sc_ref[slot, 1], vr_ref[prev, :]
                    mx = jnp.max(jnp.maximum(s_c, s_p), axis=1, keepdims=True)
                    p_c, p_p = jnp.exp(s_c - mx), jnp.exp(s_p - mx)
                    den = jnp.sum(p_c + p_p, axis=1, keepdims=True)
                    oh = _dot(p_c.astype(BF), vc) + _dot(p_p.astype(BF), vp)
                else:
                    mx = jnp.max(s_c, axis=1, keepdims=True)
                    p_c = jnp.exp(s_c - mx)
                    den = jnp.sum(p_c, axis=1, keepdims=True)
                    oh = _dot(p_c.astype(BF), vc)
                on = oh / den
                lsev = jnp.broadcast_to(mx + jnp.log(den), (2 * BLK, BLK))
                og_ref[cur, :] = on[:BLK] * masks[0] + on[BLK:] * masks[1]
                lg_ref[0, cur, :] = lsev[:BLK]
                lg_ref[1, cur, :] = lsev[BLK:]

            def pair(j, carry):
                finish(2 * j, 0)
                scores(2 * j + 1, 1)
                finish(2 * j + 1, 1)
                scores(jnp.minimum(2 * j + 2, NBLK - 1), 0)
                return carry

            scores(0, 0)
            lax.fori_loop(0, NBLK // 2, pair, 0)
            for rows, src in _pieces(dil):
                srows = pl.ds(src, BLK)
                otok_ref[gi, rows, :] = og_ref[srows, :]
                ltok_ref[gi, 0, rows, :] = lg_ref[0, srows, :]
                ltok_ref[gi, 1, rows, :] = lg_ref[1, srows, :]

        for gi in range(3):
            pl.when(g == gi)(functools.partial(group, gi))

        @pl.when(g == 2)
        def _():
            for c in range(T // BLK):
                rows = pl.ds(BLK * c, BLK)
                wts = []
                for hh in range(2):
                    l0, l1, l2 = ltok_ref[0, hh, rows, :], ltok_ref[1, hh, rows, :], ltok_ref[2, hh, rows, :]
                    mx = jnp.maximum(jnp.maximum(l0, l1), l2)
                    e0, e1, e2 = jnp.exp(l0 - mx), jnp.exp(l1 - mx), jnp.exp(l2 - mx)
                    tot = e0 + e1 + e2
                    lse_ref[rows, BLK * hh:BLK * (hh + 1)] = mx + jnp.log(tot)
                    inv = 1.0 / tot
                    wts.append([e0 * inv, e1 * inv, e2 * inv])
                o = sum((wts[0][gi] * masks[0] + wts[1][gi] * masks[1]) * otok_ref[gi, rows, :] for gi in range(3))
                ag = ag_ref[rows, :]
                opre_ref[rows, :] = o
                ob_ref[rows, :] = (o * (ag * _sigmoid(ag))).astype(BF)

    c0 = ATT_COL0 // BLK
    zspec = lambda part: pl.BlockSpec((T, BLK), lambda p, g, part=part: (0, c0 + 12 * part + 4 * g + p))
    outspec = pl.BlockSpec((T, BLK), lambda p, g: (0, p))
    table = pl.BlockSpec((T, BLK), lambda p, g: (0, 0))
    regrouped = pl.BlockSpec((None, T, BLK), lambda p, g: (g, 0, p))
    big = lambda: pltpu.VMEM((T, BLK), F32)
    return pl.pallas_call(
        body, name="attn_fwd", grid=(4, 3),
        in_specs=[zspec(0), zspec(1), zspec(2),
                  pl.BlockSpec((T, BLK), lambda p, g: (0, AG_COL0 // BLK + p)), table, table, table],
        out_specs=[outspec, outspec, pl.BlockSpec((T, 2 * BLK), lambda p, g: (0, p)), regrouped, regrouped, regrouped],
        out_shape=[jax.ShapeDtypeStruct((T, 512), BF), jax.ShapeDtypeStruct((T, 512), F32),
                   jax.ShapeDtypeStruct((T, 8 * BLK), F32)] + [jax.ShapeDtypeStruct((3, T, 512), BF)] * 3,
        scratch_shapes=[pltpu.VMEM((2, 2 * BLK, BLK), F32), big(),
                        pltpu.VMEM((2, T, BLK), F32), pltpu.VMEM((3, T, BLK), F32), pltpu.VMEM((3, 2, T, BLK), F32),
                        pltpu.VMEM((2, 2, 2 * BLK, BLK), F32)],
        compiler_params=_params(("parallel", "arbitrary")),
    )(z, z, z, z, cos, sa, sb)


def _attn_bwd(z, qs, ks, vs, cos, sa, sb, opre, lse, dob):
    def body(qs_ref, ks_ref, vs_ref, ag_ref, cos_ref, sa_ref, sb_ref, o_ref, lse0_ref, lse1_ref, dob_ref,
             dq_ref, dk_ref, dv_ref, dag_ref,
             bias_ref, dtok_ref, qr_ref, kr_ref, vr_ref, dor_ref, lr_ref, dr_ref,
             dqr_ref, dkr_ref, dvr_ref, pd_ref, dotok_ref):
        g = pl.program_id(1)
        masks = _head_masks()

        @pl.when(g == 0)
        def _():
            _window_bias(bias_ref)
            for c in range(T // BLK):
                rows = pl.ds(BLK * c, BLK)
                ag, dob_v, o = ag_ref[rows, :], dob_ref[rows, :], o_ref[rows, :]
                sg = _sigmoid(ag)
                dag_ref[rows, :] = (dob_v * o * (sg * (1.0 + ag * (1.0 - sg)))).astype(BF)
                do = dob_v * (ag * sg)
                dotok_ref[rows, :] = do
                prod = do * o
                for hh, mh in enumerate(masks):
                    dtok_ref[hh, rows, :] = jnp.broadcast_to(jnp.sum(prod * mh, axis=1, keepdims=True), (BLK, BLK))

        def group(gi):
            dil = ATT_GROUPS[gi][1]
            nblk = (T // dil) // BLK
            for rows, dst in _pieces(dil):
                drows = pl.ds(dst, BLK)
                dor_ref[drows, :] = dotok_ref[rows, :]
                for hh, lse_ref in enumerate((lse0_ref, lse1_ref)):
                    lr_ref[hh, drows, :] = lse_ref[rows, :]
                    dr_ref[hh, drows, :] = dtok_ref[hh, rows, :]
            dkr_ref[...] = jnp.zeros_like(dkr_ref)
            dvr_ref[...] = jnp.zeros_like(dvr_ref)

            def probs(bi, slot):
                cur, prev = _blocks(bi)
                q2, do2 = _stack_heads(qs_ref[cur, :], masks), _stack_heads(dor_ref[cur, :], masks)
                lh = jnp.concatenate([lr_ref[0, cur, :], lr_ref[1, cur, :]], axis=0)
                dh = jnp.concatenate([dr_ref[0, cur, :], dr_ref[1, cur, :]], axis=0)
                p_c = jnp.exp(_dot(q2, ks_ref[cur, :], _NT) + bias_ref[0] - lh)
                pd_ref[slot, 0] = p_c.astype(BF)
                pd_ref[slot, 1] = (p_c * (_dot(do2, vs_ref[cur, :], _NT) - dh)).astype(BF)
                if nblk > 1:
                    bias_p = bias_ref[1] + jnp.where((bi % nblk) != 0, 0.0, -jnp.inf)
                    p_p = jnp.exp(_dot(q2, ks_ref[prev, :], _NT) + bias_p - lh)
                    pd_ref[slot, 2] = p_p.astype(BF)
                    pd_ref[slot, 3] = (p_p * (_dot(do2, vs_ref[prev, :], _NT) - dh)).astype(BF)

            def grads(bi, slot):
                cur, prev = _blocks(bi)
                q2, do2 = _stack_heads(qs_ref[cur, :], masks), _stack_heads(dor_ref[cur, :], masks)
                p_c, ds_c = pd_ref[slot, 0], pd_ref[slot, 1]
                dq2 = _dot(ds_c, ks_ref[cur, :])
                dkr_ref[cur, :] += _dot(ds_c, q2, _TN)
                dvr_ref[cur, :] += _dot(p_c, do2, _TN)
                if nblk > 1:
                    p_p, ds_p = pd_ref[slot, 2], pd_ref[slot, 3]
                    dq2 = dq2 + _dot(ds_p, ks_ref[prev, :])
                    dkr_ref[prev, :] += _dot(ds_p, q2, _TN)
                    dvr_ref[prev, :] += _dot(p_p, do2, _TN)
                dqr_ref[cur, :] = dq2[:BLK] * masks[0] + dq2[BLK:] * masks[1]

            def pair(j, carry):
                grads(2 * j, 0)
                probs(2 * j + 1, 1)
                grads(2 * j + 1, 1)
                probs(jnp.minimum(2 * j + 2, NBLK - 1), 0)
                return carry

            probs(0, 0)
            lax.fori_loop(0, NBLK // 2, pair, 0)
            if dil > 1:
                for rows, src in _pieces(dil):
                    srows = pl.ds(src, BLK)
                    qr_ref[rows, :] = dqr_ref[srows, :]
                    kr_ref[rows, :] = dkr_ref[srows, :]
                    vr_ref[rows, :] = dvr_ref[srows, :]
            tq, tk, tv = (qr_ref, kr_ref, vr_ref) if dil > 1 else (dqr_ref, dkr_ref, dvr_ref)
            for c in range(T // BLK):
                rows = pl.ds(BLK * c, BLK)
                cs, sa, sb = cos_ref[rows, :], sa_ref[rows, :], sb_ref[rows, :]
                dq_ref[rows, :] = _rope_t(tq[rows, :] * QK_SCALE, cs, sa, sb).astype(BF)
                dk_ref[rows, :] = _rope_t(tk[rows, :], cs, sa, sb).astype(BF)
                dv_ref[rows, :] = tv[rows, :].astype(BF)

        for gi in range(3):
            pl.when(g == gi)(functools.partial(group, gi))

    regrouped = pl.BlockSpec((None, T, BLK), lambda p, g: (g, 0, p))
    pspec = pl.BlockSpec((T, BLK), lambda p, g: (0, p))
    gspec = pl.BlockSpec((T, BLK), lambda p, g: (0, 4 * g + p))
    table = pl.BlockSpec((T, BLK), lambda p, g: (0, 0))
    big = lambda: pltpu.VMEM((T, BLK), F32)
    two = lambda: pltpu.VMEM((2, T, BLK), F32)
    return pl.pallas_call(
        body, name="attn_bwd", grid=(4, 3),
        in_specs=[regrouped, regrouped, regrouped,
                  pl.BlockSpec((T, BLK), lambda p, g: (0, AG_COL0 // BLK + p)), table, table, table,
                  pspec, pl.BlockSpec((T, BLK), lambda p, g: (0, 2 * p)),
                  pl.BlockSpec((T, BLK), lambda p, g: (0, 2 * p + 1)), pspec],
        out_specs=[gspec, gspec, gspec, pspec],
        out_shape=[jax.ShapeDtypeStruct((T, 1536), BF), jax.ShapeDtypeStruct((T, 1536), BF),
                   jax.ShapeDtypeStruct((T, 1536), BF), jax.ShapeDtypeStruct((T, 512), BF)],
        scratch_shapes=[pltpu.VMEM((2, 2 * BLK, BLK), F32), two(), big(), big(), big(), big(),
                        two(), two(), big(), big(), big(), pltpu.VMEM((2, 4, 2 * BLK, BLK), BF), big()],
        compiler_params=_params(("parallel", "arbitrary")),
    )(qs, ks, vs, z, cos, sa, sb, opre, lse, lse, dob)


def _merge_out_loss(og, ob, z, w_a, w_b, w_out, x, tgt, wf):
    tm = 512

    def body(og_ref, ob_ref, ga_ref, gb_ref, wa_ref, wb_ref, wo_ref, x_ref, t_ref, wf_ref,
             m_ref, dout_ref, loss_ref, gwf_ref):
        @pl.when(pl.program_id(0) == 0)
        def _():
            loss_ref[...] = jnp.zeros_like(loss_ref)
            gwf_ref[...] = jnp.zeros_like(gwf_ref)

        ya, yb = _dot(og_ref[...], wa_ref[...]), _dot(ob_ref[...], wb_ref[...])
        m =(_sigmoid(ga_ref[...]) * ya + _sigmoid(gb_ref[...]) * yb).astype(BF)
        m_ref[...] = m
        out = x_ref[...] + _dot(m, wo_ref[...])
        r = lax.rsqrt(jnp.mean(out * out, axis=-1, keepdims=True) + EPS)
        yh = out * r
        wfv = wf_ref[...]
        err = yh * wfv - t_ref[...]
        loss_ref[...] += jnp.sum(err * err, axis=0, keepdims=True) * (0.5 / D)
        dy = err * (1.0 / D)
        gwf_ref[...] += jnp.sum(dy * yh, axis=0, keepdims=True)
        dyh = dy * wfv
        dout_ref[...] = r * (dyh - yh * jnp.mean(dyh * yh, axis=-1, keepdims=True))

    row = pl.BlockSpec((tm, D), lambda i: (i, 0))
    vec = pl.BlockSpec((1, D), lambda i: (0, 0))
    whole = lambda w: pl.BlockSpec(w.shape, lambda i: (0, 0))
    return pl.pallas_call(
        body, name="merge_out_loss", grid=(T // tm,),
        in_specs=[row, pl.BlockSpec((tm, ob.shape[1]), lambda i: (i, 0)),
                  pl.BlockSpec((tm, D), lambda i: (i, GATE_COL0 // D)),
                  pl.BlockSpec((tm, D), lambda i: (i, GATE_COL0 // D + 1)),
                  whole(w_a), whole(w_b), whole(w_out), row, row, vec],
        out_specs=[row, row, vec, vec],
        out_shape=[jax.ShapeDtypeStruct((T, D), BF), jax.ShapeDtypeStruct((T, D), F32),
                   jax.ShapeDtypeStruct((1, D), F32), jax.ShapeDtypeStruct((1, D), F32)],
        compiler_params=_params(("arbitrary",)),
    )(og, ob, z, z, w_a, w_b, w_out, x, tgt, wf)


def _merge_proj_bwd(dout, og, ob, z, w_a, w_b, w_out):
    tm = 512

    def body(dout_ref, og_ref, ob_ref, ga_ref, gb_ref, wa_ref, wb_ref, wo_ref,
             dya_ref, dyb_ref, dg_ref, dog_ref, dob_ref):
        dmv = _dot(dout_ref[...].astype(BF), wo_ref[...], _NT)
        sa, sb = _sigmoid(ga_ref[...]), _sigmoid(gb_ref[...])
        dya, dyb = (sa * dmv).astype(BF), (sb * dmv).astype(BF)
        dya_ref[...] = dya
        dyb_ref[...] = dyb
        dg_ref[:, :D] = (dmv * _dot(og_ref[...], wa_ref[...]) * sa * (1.0 - sa)).astype(BF)
        dg_ref[:, D:] = (dmv * _dot(ob_ref[...], wb_ref[...]) * sb * (1.0 - sb)).astype(BF)
        dog_ref[...] = _dot(dya, wa_ref[...], _NT)
        dob_ref[...] = _dot(dyb, wb_ref[...], _NT)

    row = pl.BlockSpec((tm, D), lambda i: (i, 0))
    whole = lambda w: pl.BlockSpec(w.shape, lambda i: (0, 0))
    nb = w_b.shape[0]
    return pl.pallas_call(
        body, name="merge_proj_bwd", grid=(T // tm,),
        in_specs=[row, row, pl.BlockSpec((tm, nb), lambda i: (i, 0)),
                  pl.BlockSpec((tm, D), lambda i: (i, GATE_COL0 // D)),
                  pl.BlockSpec((tm, D), lambda i: (i, GATE_COL0 // D + 1)), whole(w_a), whole(w_b), whole(w_out)],
        out_specs=[row, row, pl.BlockSpec((tm, 2 * D), lambda i: (i, 0)), row,
                   pl.BlockSpec((tm, nb), lambda i: (i, 0))],
        out_shape=[jax.ShapeDtypeStruct((T, D), BF), jax.ShapeDtypeStruct((T, D), BF),
                   jax.ShapeDtypeStruct((T, 2 * D), BF), jax.ShapeDtypeStruct((T, D), F32),
                   jax.ShapeDtypeStruct((T, nb), F32)],
        compiler_params=_params(("parallel",)),
    )(dout, og, ob, z, z, w_a, w_b, w_out)


def _rope_inv_freq():
    inv = ROPE_THETA ** (-jnp.arange(0, 64, 2, dtype=F32) / 64)
    return jnp.tile(inv, 4).reshape(1, BLK)


def _local_step(x, pos, norm_w, lbl, hnw, wf, tgt, w_in, w_a, w_b, w_out, shard_shapes=()):
    invf = _rope_inv_freq()
    if shard_shapes:
        blk = jnp.reshape(2 * lax.axis_index("x") + lax.axis_index("y"), (1,)).astype(jnp.int32)
        h, cos, sa, sb, z_own, (w_near,) = _norm_and_rope_tables(
            x, norm_w, pos, invf, side=_gather_near_side(w_in, WEIGHT_AXES[0]), own=(w_in, blk))
        near = jnp.concatenate([blk ^ 2, blk ^ 1])
        z, (w_diag,) = _z_blocks(h, w_near, z_own, jnp.concatenate([near, near]), 2, name="z_proj_near",
                                 side=_gather_diag_side(w_near, w_in.shape, WEIGHT_AXES[0]))
        z, w_in, _ = _z_blocks(h, w_diag, z, jnp.concatenate([blk ^ 3, jnp.zeros_like(blk)]), 1, name="z_proj_diag",
                               fill=w_near, side=None)
        oraw, og, shist, (w_a, w_b, w_out) = _hgrn_fwd(
            z, lbl, hnw, side=_gather_side([w_a, w_b, w_out], WEIGHT_AXES[1:]))
    else:
        h, cos, sa, sb, _, _ = _norm_and_rope_tables(x, norm_w, pos, invf)
        z = _matmul(h, w_in, tm=T, tn=512, name="z_proj")
        oraw, og, shist, _ = _hgrn_fwd(z, lbl, hnw)
    ob, opre, lse, qs, ks, vs = _attn_fwd(z, cos, sa, sb)
    merged, dout, loss_vec, g_wf = _merge_out_loss(og, ob, z, w_a, w_b, w_out, x, tgt, wf)

    dya, dyb, dgates, dog, dob = _merge_proj_bwd(dout, og, ob, z, w_a, w_b, w_out)
    g_wout = _matmul(merged, dout, ta=True, out_dtype=BF, tm=512, tn=1024, name="g_wout")
    g_wa = _matmul(og, dya, ta=True, out_dtype=BF, tm=512, tn=1024, name="g_wa")
    g_wb = _matmul(ob, dyb, ta=True, out_dtype=BF, tm=512, tn=1024, name="g_wb")
    small = [g_wa, g_wb, g_wout]
    side_s = side_w = None
    if shard_shapes:
        p3_s = _rs_partials(small, shard_shapes[1:], WEIGHT_AXES[1:], "small")
        side_s = _chip_exchange_side(p3_s, shard_shapes[1:], WEIGHT_AXES[1:])
    dz_h, dlb, g_hnw, land_s = _hgrn_bwd(z, lbl, hnw, oraw, dog, shist, side=side_s)
    dq, dk, dv, dag = _attn_bwd(z, qs, ks, vs, cos, sa, sb, opre, lse, dob)
    dz_parts = [dz_h, dq, dk, dv, dag, dgates]
    if shard_shapes:
        c = lax.axis_index("c")
        half = lambda i: jnp.reshape(i, (1,)).astype(jnp.int32)
        g_send = _grad_w_in_half(h, dz_parts, half(1 - c))
        g_keep, (g_sib,) = _grad_w_in_half(h, dz_parts, half(c), side=_sibling_send_side(g_send))
        p3_w = [_add_bf16(g_keep, g_sib, "pair_sum_w_in").reshape(1, D // 2, NIN)]
        side_w = _chip_exchange_relay_side(p3_w[0], shard_shapes[0])
    else:
        g_big = [_grad_w_in(h, dz_parts)] + small
    gx, g_nw, land_w = _grad_x(dz_parts, w_in, x, dout, norm_w, side=side_w)
    small_sums = None
    if shard_shapes:
        g_big, small_sums = _rs_finish(p3_w + p3_s, [land_w[0]] + list(land_s), shard_shapes, WEIGHT_AXES,
                                       (g_nw, dlb, g_hnw, g_wf, loss_vec))
    return dict(loss_vec=loss_vec, gx=gx, g_nw=g_nw, dlb=dlb, g_hnw=g_hnw, g_wf=g_wf, small_sums=small_sums,
                g_win=g_big[0], g_wa=g_big[1], g_wb=g_big[2], g_wout=g_big[3])


MESH = pl.DeviceIdType.MESH
HBM = pl.BlockSpec(memory_space=pl.ANY)
WEIGHT_AXES = (1, 0, 1, 0)


def _place():
    x, y, c = lax.axis_index("x"), lax.axis_index("y"), lax.axis_index("c")
    chips = [(1 - x, y), (x, 1 - y), (1 - x, 1 - y)]
    return x, y, c, chips


def _block_half(ref, shard_shape, axis, j, half):
    r, c = shard_shape
    hr = r // 2
    if axis == 0:
        return ref.at[pl.ds(pl.multiple_of(j * r + half * hr, 16), hr), :]
    return ref.at[pl.ds(pl.multiple_of(half * hr, 16), hr), pl.ds(pl.multiple_of(j * c, 128), c)]


class _Side:
    def __init__(self, arrays, out_shapes, sems, first, last, mid=None):
        self.arrays, self.out_shapes, self.sems, self.first, self.last = arrays, out_shapes, sems, first, last
        self.mid = mid


def _gather_side(shards, axes):
    n = len(shards)
    shapes = [s.shape for s in shards]

    def copies(ins, outs, sems):
        send1, recv1, send2, recv2, send0, recv0 = sems
        x, y, c, chips = _place()
        me = 2 * x + y
        sib = (x, y, 1 - c)
        near = ((1 - c) * (1 - x) + c * x, (1 - c) * y + c * (1 - y))
        far = ((1 - c) * x + c * (1 - x), (1 - c) * (1 - y) + c * y)
        out = []
        for a in range(n):
            r, cc = shapes[a]
            mine = (outs[a].at[pl.ds(pl.multiple_of(me * r, 16), r), :] if axes[a] == 0
                    else outs[a].at[:, pl.ds(pl.multiple_of(me * cc, 128), cc)])
            own = pltpu.make_async_remote_copy(
                src_ref=ins[a], dst_ref=mine, send_sem=send0.at[a], recv_sem=recv0.at[a],
                device_id=sib, device_id_type=MESH)
            src = ins[a].at[pl.ds(pl.multiple_of(c * (r // 2), 16), r // 2), :]
            sends = [pltpu.make_async_remote_copy(
                src_ref=src, dst_ref=_block_half(outs[a], shapes[a], axes[a], me, c),
                send_sem=send1.at[a, k], recv_sem=recv1.at[a, k], device_id=(*chips[k], c), device_id_type=MESH)
                for k in range(2)]

            def region(chip, half):
                return _block_half(outs[a], shapes[a], axes[a], 2 * chip[0] + chip[1], half)

            def arrival(chip, k):
                reg = region(chip, c)
                return pltpu.make_async_remote_copy(
                    src_ref=reg, dst_ref=reg, send_sem=send1.at[a, k], recv_sem=recv1.at[a, k],
                    device_id=(*chip, c), device_id_type=MESH)

            def to_sibling(chip, k):
                reg = region(chip, c)
                return pltpu.make_async_remote_copy(
                    src_ref=reg, dst_ref=reg, send_sem=send2.at[a, k], recv_sem=recv2.at[a, k],
                    device_id=sib, device_id_type=MESH)

            def from_sibling(chip, k):
                reg = region(chip, 1 - c)
                return pltpu.make_async_remote_copy(
                    src_ref=reg, dst_ref=reg, send_sem=send2.at[a, k], recv_sem=recv2.at[a, k],
                    device_id=sib, device_id_type=MESH)

            relay = pltpu.make_async_remote_copy(
                src_ref=region(near, c), dst_ref=region(near, c), send_sem=send1.at[a, 2], recv_sem=recv1.at[a, 2],
                device_id=(*far, c), device_id_type=MESH)
            hops = [(arrival(near, c), to_sibling(near, c)), (arrival(far, 1 - c), to_sibling(far, 1 - c)),
                    (arrival(chips[2], 2), to_sibling(chips[2], 2))]
            back = [from_sibling(chips[k], k) for k in range(3)]
            out.append((own, sends, relay, hops, back))
        return out

    def first(ins, outs, sems):
        for own, sends, _, _, _ in copies(ins, outs, sems):
            own.start()
            for cp in sends:
                cp.start()

    def mid(ins, outs, sems):
        per_array = copies(ins, outs, sems)
        for step in range(2):
            for _, _, relay, hops, _ in per_array:
                arrived, onward = hops[step]
                arrived.wait_recv()
                if step == 0:
                    relay.start()
                onward.start()

    def last(ins, outs, sems):
        per_array = copies(ins, outs, sems)
        for _, _, _, hops, _ in per_array:
            arrived, onward = hops[2]
            arrived.wait_recv()
            onward.start()
        for own, sends, relay, hops, back in per_array:
            for cp in back:
                cp.wait_recv()
            for cp in sends + [relay] + [onward for _, onward in hops]:
                cp.wait_send()
            own.wait()

    full = [(4 * r, c) if ax == 0 else (r, 4 * c) for (r, c), ax in zip(shapes, axes)]
    sems = [pltpu.SemaphoreType.DMA((n, 3)), pltpu.SemaphoreType.DMA((n, 3)),
            pltpu.SemaphoreType.DMA((n, 3)), pltpu.SemaphoreType.DMA((n, 3)),
            pltpu.SemaphoreType.DMA((n,)), pltpu.SemaphoreType.DMA((n,))]
    return _Side(list(shards), [jax.ShapeDtypeStruct(f, BF) for f in full], sems, first, last, mid)


def _gather_near_side(shard, axis):
    shape = shard.shape
    r, cc = shape

    def copies(ins, outs, sems):
        send1, recv1, send2, recv2, send0, recv0 = sems
        x, y, c, chips = _place()
        me = 2 * x + y
        sib = (x, y, 1 - c)
        mine = (outs[0].at[pl.ds(pl.multiple_of(me * r, 16), r), :] if axis == 0
                else outs[0].at[:, pl.ds(pl.multiple_of(me * cc, 128), cc)])
        own = pltpu.make_async_remote_copy(
            src_ref=ins[0], dst_ref=mine, send_sem=send0.at[0], recv_sem=recv0.at[0],
            device_id=sib, device_id_type=MESH)
        src = ins[0].at[pl.ds(pl.multiple_of(c * (r // 2), 16), r // 2), :]

        def region(k, half):
            return _block_half(outs[0], shape, axis, 2 * chips[k][0] + chips[k][1], half)

        def moves(k):
            return [pltpu.make_async_remote_copy(
                        src_ref=s, dst_ref=d, send_sem=ss.at[k], recv_sem=rs.at[k], device_id=dev,
                        device_id_type=MESH)
                    for s, d, ss, rs, dev in (
                        (src, _block_half(outs[0], shape, axis, me, c), send1, recv1, (*chips[k], c)),
                        (region(k, c), region(k, c), send1, recv1, (*chips[k], c)),
                        (region(k, c), region(k, c), send2, recv2, sib),
                        (region(k, 1 - c), region(k, 1 - c), send2, recv2, sib))]

        return own, [moves(k) for k in range(2)]

    def first(ins, outs, sems):
        own, per_chip = copies(ins, outs, sems)
        own.start()
        for send, _, _, _ in per_chip:
            send.start()

    def last(ins, outs, sems):
        own, per_chip = copies(ins, outs, sems)
        for _, arrived, onward, _ in per_chip:
            arrived.wait_recv()
            onward.start()
        for send, _, onward, back in per_chip:
            back.wait_recv()
            send.wait_send()
            onward.wait_send()
        own.wait()

    full = (4 * r, cc) if axis == 0 else (r, 4 * cc)
    sems = [pltpu.SemaphoreType.DMA((2,))] * 4 + [pltpu.SemaphoreType.DMA((1,))] * 2
    return _Side([shard], [jax.ShapeDtypeStruct(full, BF)], sems, first, last)


def _gather_diag_side(gathered, shape, axis):
    r, cc = shape

    def copies(ins, outs, sems):
        send1, recv1, send2, recv2 = sems
        x, y, c, _ = _place()
        sib = (x, y, 1 - c)
        near = ((1 - c) * (1 - x) + c * x, (1 - c) * y + c * (1 - y))
        far = ((1 - c) * x + c * (1 - x), (1 - c) * (1 - y) + c * y)

        def half(i):
            return outs[0].at[pl.ds(pl.multiple_of(i * (r // 2), 16), r // 2), :]

        def move(s, d, ss, rs, dev):
            return pltpu.make_async_remote_copy(
                src_ref=s, dst_ref=d, send_sem=ss.at[0], recv_sem=rs.at[0], device_id=dev, device_id_type=MESH)

        relay = move(_block_half(ins[0], shape, axis, 2 * near[0] + near[1], c), half(c), send1, recv1, (*far, c))
        arrived = move(half(c), half(c), send1, recv1, (*far, c))
        onward = move(half(c), half(c), send2, recv2, sib)
        back = move(half(1 - c), half(1 - c), send2, recv2, sib)
        return relay, arrived, onward, back

    def first(ins, outs, sems):
        copies(ins, outs, sems)[0].start()

    def last(ins, outs, sems):
        relay, arrived, onward, back = copies(ins, outs, sems)
        arrived.wait_recv()
        onward.start()
        back.wait_recv()
        relay.wait_send()
        onward.wait_send()

    return _Side([gathered], [jax.ShapeDtypeStruct(shape, BF)], [pltpu.SemaphoreType.DMA((1,))] * 4, first, last)


def _as3d(g, shard_shape, axis):
    r, c = shard_shape
    return g.reshape(4, r, c) if axis == 0 else g.reshape(1, r, 4 * c)


def _half_rows(ref3, hr, half):
    return ref3.at[:, pl.ds(pl.multiple_of(half * hr, 16), hr), :]


def _rs_pair_exchange(g3s, name):
    n = len(g3s)

    def body(*refs):
        ins, outs = refs[:n], refs[n:2 * n]
        send, recv = refs[2 * n:]
        x, y, c, _ = _place()
        cps = []
        for a in range(n):
            hr = g3s[a].shape[1] // 2
            cp = pltpu.make_async_remote_copy(
                src_ref=_half_rows(ins[a], hr, 1 - c), dst_ref=outs[a],
                send_sem=send.at[a], recv_sem=recv.at[a], device_id=(x, y, 1 - c), device_id_type=MESH)
            cp.start()
            cps.append(cp)
        for cp in cps:
            cp.wait()

    return pl.pallas_call(
        body, name=name,
        in_specs=[HBM] * n, out_specs=[HBM] * n,
        out_shape=[jax.ShapeDtypeStruct((g.shape[0], g.shape[1] // 2, g.shape[2]), BF) for g in g3s],
        scratch_shapes=[pltpu.SemaphoreType.DMA((n,)), pltpu.SemaphoreType.DMA((n,))],
    )(*g3s)


def _pair_sums(g3s, lands, cidx, name):
    n = len(g3s)

    def body(c_ref, *refs):
        for g_ref, l_ref, o_ref in zip(refs[:n], refs[n:2 * n], refs[2 * n:]):
            o_ref[...] = (g_ref[...].astype(F32) + l_ref[...].astype(F32)).astype(BF)

    halves = [(g.shape[0], g.shape[1] // 2, g.shape[2]) for g in g3s]
    return pl.pallas_call(
        body, name=name,
        grid_spec=pltpu.PrefetchScalarGridSpec(
            num_scalar_prefetch=1, grid=(1,),
            in_specs=[pl.BlockSpec(h, lambda i, c: (0, c[0], 0)) for h in halves]
                     + [pl.BlockSpec(h, lambda i, c: (0, 0, 0)) for h in halves],
            out_specs=[pl.BlockSpec(h, lambda i, c: (0, 0, 0)) for h in halves]),
        out_shape=[jax.ShapeDtypeStruct(h, BF) for h in halves],
        compiler_params=_params(("arbitrary",)),
    )(cidx, *g3s, *lands)


def _chip_exchange_side(p3s, shapes, axes):
    n = len(p3s)

    def copies(ins, outs, sems):
        send, recv = sems
        x, y, c, chips = _place()
        cps = []
        for a in range(n):
            r, cc = shapes[a]
            for k, (px, py) in enumerate(chips):
                j = 2 * px + py
                src = ins[a].at[j] if axes[a] == 0 else ins[a].at[0, :, pl.ds(pl.multiple_of(j * cc, 128), cc)]
                cps.append(pltpu.make_async_remote_copy(
                    src_ref=src, dst_ref=outs[a].at[k], send_sem=send.at[a, k], recv_sem=recv.at[a, k],
                    device_id=(px, py, c), device_id_type=MESH))
        return cps

    def first(ins, outs, sems):
        for cp in copies(ins, outs, sems):
            cp.start()

    def last(ins, outs, sems):
        for cp in copies(ins, outs, sems):
            cp.wait()

    return _Side(list(p3s), [jax.ShapeDtypeStruct((3, r // 2, c), BF) for r, c in shapes],
                 [pltpu.SemaphoreType.DMA((n, 3)), pltpu.SemaphoreType.DMA((n, 3))], first, last)


def _chip_exchange_relay_side(p3, shape):
    r, cc = shape
    hr = r // 2
    rows = 64

    def copies(ins, outs, sems):
        send, recv, local, mine, theirs = sems
        x, y, c, chips = _place()
        near = ((1 - c) * (1 - x) + c * x, (1 - c) * y + c * (1 - y))
        far = ((1 - c) * x + c * (1 - x), (1 - c) * (1 - y) + c * y)
        land, staged = outs

        def block(chip):
            return ins[0].at[0, :, pl.ds(pl.multiple_of((2 * chip[0] + chip[1]) * cc, 128), cc)]

        def move(s, d, k, dev):
            return pltpu.make_async_remote_copy(
                src_ref=s, dst_ref=d, send_sem=send.at[k], recv_sem=recv.at[k], device_id=dev, device_id_type=MESH)

        return dict(
            direct=move(block(near), land.at[c], 0, (*near, c)),
            for_relay=move(block(chips[2]), staged, 1, (*near, c)),
            summed=move(mine, land.at[1 - c], 2, (*far, c)),
            direct_in=move(land.at[c], land.at[c], 0, (*near, c)),
            staged_in=move(staged, staged, 1, (*near, c)),
            summed_in=move(land.at[1 - c], land.at[1 - c], 2, (*far, c)),
            load_mine=pltpu.make_async_copy(block(far), mine, local.at[0]),
            load_theirs=pltpu.make_async_copy(staged, theirs, local.at[1]))

    def first(ins, outs, sems):
        cps = copies(ins, outs, sems)
        cps["for_relay"].start()
        cps["direct"].start()

    def mid(ins, outs, sems):
        cps = copies(ins, outs, sems)
        mine, theirs = sems[3], sems[4]
        cps["load_mine"].start()
        cps["staged_in"].wait_recv()
        cps["load_theirs"].start()
        cps["load_mine"].wait()
        cps["load_theirs"].wait()

        def add(i, carry):
            rs = pl.ds(pl.multiple_of(i * rows, 16), rows)
            mine[rs, :] = (mine[rs, :].astype(F32) + theirs[rs, :].astype(F32)).astype(BF)
            return carry

        lax.fori_loop(0, hr // rows, add, 0)
        cps["summed"].start()

    def last(ins, outs, sems):
        cps = copies(ins, outs, sems)
        cps["direct_in"].wait_recv()
        cps["summed_in"].wait_recv()
        for name in ("direct", "for_relay", "summed"):
            cps[name].wait_send()

    sems = [pltpu.SemaphoreType.DMA((3,)), pltpu.SemaphoreType.DMA((3,)), pltpu.SemaphoreType.DMA((2,)),
            pltpu.VMEM((hr, cc), BF), pltpu.VMEM((hr, cc), BF)]
    return _Side([p3], [jax.ShapeDtypeStruct((2, hr, cc), BF), jax.ShapeDtypeStruct((hr, cc), BF)], sems,
                 first, last, mid)


def _chip_sum(p3, land, shard_shape, axis, idx, name):
    r, c = shard_shape
    hr = r // 2
    tr = 64
    nt = hr // tr
    slots = land.shape[0]

    def body(idx_ref, p_ref, l_ref, o_ref):
        acc = p_ref[...].astype(F32)
        for k in range(slots):
            acc = acc + l_ref[k].astype(F32)
        o_ref[...] = acc

    own = (pl.BlockSpec((None, tr, c), lambda i, idx: (idx[0], i, 0)) if axis == 0
           else pl.BlockSpec((None, tr, c), lambda i, idx: (0, i, idx[0])))
    return pl.pallas_call(
        body, name=name,
        grid_spec=pltpu.PrefetchScalarGridSpec(
            num_scalar_prefetch=1, grid=(nt,),
            in_specs=[own, pl.BlockSpec((slots, tr, c), lambda i, idx: (0, i, 0))],
            out_specs=pl.BlockSpec((tr, c), lambda i, idx: (idx[1] * nt + i, 0))),
        out_shape=jax.ShapeDtypeStruct((r, c), F32),
        compiler_params=_params(("parallel",)),
    )(idx, p3, land)


def _chip_sums(p3s, lands, shapes, axes, idx, name):
    n = len(p3s)

    def body(idx_ref, *refs):
        for p_ref, l_ref, o_ref in zip(refs[:n], refs[n:2 * n], refs[2 * n:]):
            acc = p_ref[...].astype(F32)
            for k in range(l_ref.shape[0]):
                acc = acc + l_ref[k].astype(F32)
            o_ref[...] = acc

    own = [pl.BlockSpec((None, r // 2, c), (lambda i, idx: (idx[0], 0, 0)) if ax == 0 else (lambda i, idx: (0, 0, idx[0])))
           for (r, c), ax in zip(shapes, axes)]
    return pl.pallas_call(
        body, name=name,
        grid_spec=pltpu.PrefetchScalarGridSpec(
            num_scalar_prefetch=1, grid=(1,),
            in_specs=own + [pl.BlockSpec(l.shape, lambda i, idx: (0, 0, 0)) for l in lands],
            out_specs=[pl.BlockSpec((r // 2, c), lambda i, idx: (idx[1], 0)) for r, c in shapes]),
        out_shape=[jax.ShapeDtypeStruct((r, c), F32) for r, c in shapes],
        compiler_params=_params(("arbitrary",)),
    )(idx, *p3s, *lands)


def _rs_pair_gather(fulls, small):
    n = len(fulls)

    def body(*refs):
        ins, small_refs, outs, red_ref = refs[:n], refs[n:n + 5], refs[n + 5:2 * n + 5], refs[2 * n + 5]
        send, recv = refs[2 * n + 6:2 * n + 8]
        x, y, c, _ = _place()
        cps = []
        for a in range(n):
            hr = fulls[a].shape[0] // 2
            rows = pl.ds(pl.multiple_of(c * hr, 8), hr)
            cp = pltpu.make_async_remote_copy(
                src_ref=ins[a].at[rows, :], dst_ref=outs[a].at[rows, :], send_sem=send.at[a], recv_sem=recv.at[a],
                device_id=(x, y, 1 - c), device_id_type=MESH)
            cp.start()
            cps.append(cp)
        _small_all_reduce(small_refs, red_ref, *refs[2 * n + 8:])
        for a, cp in enumerate(cps):
            cp.wait_send()
            hr = fulls[a].shape[0] // 2
            other = pl.ds(pl.multiple_of((1 - c) * hr, 8), hr)
            pltpu.make_async_remote_copy(
                src_ref=ins[a].at[other, :], dst_ref=outs[a].at[other, :], send_sem=send.at[a], recv_sem=recv.at[a],
                device_id=(x, y, 1 - c), device_id_type=MESH).wait_recv()

    vm = pl.BlockSpec(memory_space=pltpu.VMEM)
    out = pl.pallas_call(
        body, name="grads_pair_gather",
        in_specs=[HBM] * n + [vm] * 5, out_specs=[HBM] * n + [vm],
        out_shape=[jax.ShapeDtypeStruct(f.shape, F32) for f in fulls] + [jax.ShapeDtypeStruct((NSMALL, D), F32)],
        input_output_aliases={a: a for a in range(n)},
        scratch_shapes=[pltpu.SemaphoreType.DMA((n,)), pltpu.SemaphoreType.DMA((n,)),
                        pltpu.VMEM((NSMALL, D), F32), pltpu.VMEM((8, NSMALL, D), F32),
                        pltpu.SemaphoreType.DMA((7,)), pltpu.SemaphoreType.DMA((7,))],
    )(*fulls, *small)
    return out[:n], out[n]


def _sibling_send_side(arr):
    def copy(ins, outs, sems):
        x, y, c, _ = _place()
        return pltpu.make_async_remote_copy(
            src_ref=ins[0], dst_ref=outs[0], send_sem=sems[0].at[0], recv_sem=sems[1].at[0],
            device_id=(x, y, 1 - c), device_id_type=MESH)

    return _Side([arr], [jax.ShapeDtypeStruct(arr.shape, arr.dtype)],
                 [pltpu.SemaphoreType.DMA((1,)), pltpu.SemaphoreType.DMA((1,))],
                 lambda ins, outs, sems: copy(ins, outs, sems).start(),
                 lambda ins, outs, sems: copy(ins, outs, sems).wait())


def _add_bf16(a, b, name):
    r, c = a.shape
    tr = 64

    def body(a_ref, b_ref, o_ref):
        o_ref[...] = (a_ref[...].astype(F32) + b_ref[...].astype(F32)).astype(BF)

    blk = pl.BlockSpec((tr, c), lambda i: (i, 0))
    return pl.pallas_call(
        body, name=name, grid=(r // tr,), in_specs=[blk, blk], out_specs=blk,
        out_shape=jax.ShapeDtypeStruct((r, c), BF), compiler_params=_params(("parallel",)),
    )(a, b)


def _rs_partials(grads, shapes, axes, tag):
    cidx = jnp.reshape(lax.axis_index("c"), (1,)).astype(jnp.int32)
    g3s = [_as3d(g, s, ax) for g, s, ax in zip(grads, shapes, axes)]
    lands = _rs_pair_exchange(g3s, f"grads_pair_exchange_{tag}")
    return list(_pair_sums(g3s, lands, cidx, f"pair_sums_{tag}"))


def _rs_finish(p3s, landed, shapes, axes, small):
    x, y, c = lax.axis_index("x"), lax.axis_index("y"), lax.axis_index("c")
    idx = jnp.stack([2 * x + y, c]).astype(jnp.int32)
    fulls = [_chip_sum(p3s[0], landed[0], shapes[0], axes[0], idx, "chip_sum_w_in")]
    fulls += _chip_sums(p3s[1:], landed[1:], shapes[1:], axes[1:], idx, "chip_sums_branches_out")
    return _rs_pair_gather(fulls, small)


NSMALL = 8


def _small_all_reduce(small_refs, out_ref, pack_ref, buf_ref, send, recv):
    nw_ref, lb_ref, hn_ref, wf_ref, ls_ref = small_refs
    x, y, c = lax.axis_index("x"), lax.axis_index("y"), lax.axis_index("c")
    me = 4 * x + 2 * y + c
    pack_ref[...] = jnp.zeros_like(pack_ref)
    pack_ref[0:1, :] = nw_ref[...]
    pack_ref[1:2, :] = lb_ref[...]
    pack_ref[2:3, 0:HK] = hn_ref[...]
    pack_ref[3:4, :] = wf_ref[...]
    pack_ref[4:5, :] = ls_ref[...]
    buf_ref[me] = pack_ref[...]
    cps = []
    for d in range(1, 8):
        dx, dy, dc = d >> 2, (d >> 1) & 1, d & 1
        peer = (1 - x if dx else x, 1 - y if dy else y, 1 - c if dc else c)
        cp = pltpu.make_async_remote_copy(
            src_ref=pack_ref, dst_ref=buf_ref.at[me], send_sem=send.at[d - 1], recv_sem=recv.at[d - 1],
            device_id=peer, device_id_type=MESH)
        cp.start()
        cps.append(cp)
    for d in range(1, 8):
        dx, dy, dc = d >> 2, (d >> 1) & 1, d & 1
        src = 4 * (1 - x if dx else x) + 2 * (1 - y if dy else y) + (1 - c if dc else c)
        pltpu.make_async_remote_copy(
            src_ref=pack_ref, dst_ref=buf_ref.at[src], send_sem=send.at[d - 1], recv_sem=recv.at[d - 1],
            device_id=(x, y, c), device_id_type=MESH).wait_recv()
    for cp in cps:
        cp.wait_send()
    acc = buf_ref[0]
    for i in range(1, 8):
        acc = acc + buf_ref[i]
    out_ref[...] = acc


def _adamw_math(w, g, m, v):
    m = B1 * m + (1.0 - B1) * g
    v = B2 * v + (1.0 - B2) * (g * g)
    m_hat = m / (1.0 - B1 ** STEP)
    v_hat = v / (1.0 - B2 ** STEP)
    return -LR * (m_hat / (jnp.sqrt(v_hat) + ADAM_EPS) + WD * w), m, v


def _adamw(w, g, m, v, name):
    r, c = w.shape
    tr = 128

    def body(w_ref, g_ref, m_ref, v_ref, d_ref, nm_ref, nv_ref, go_ref):
        g = g_ref[...]
        d_ref[...], nm_ref[...], nv_ref[...] = _adamw_math(w_ref[...], g, m_ref[...], v_ref[...])
        go_ref[...] = g

    blk = pl.BlockSpec((tr, c), lambda i: (i, 0))
    return pl.pallas_call(
        body, name=name, grid=(r // tr,), in_specs=[blk] * 4, out_specs=[blk] * 4,
        out_shape=[jax.ShapeDtypeStruct((r, c), F32)] * 4,
        compiler_params=_params(("parallel",)),
    )(w, g, m, v)


def _adamw_whole(groups, name):
    n = len(groups)

    def body(*refs):
        ins, outs = refs[:4 * n], refs[4 * n:]
        for a in range(n):
            w_ref, g_ref, m_ref, v_ref = ins[4 * a:4 * a + 4]
            g = g_ref[...]
            outs[4 * a][...], outs[4 * a + 1][...], outs[4 * a + 2][...] = _adamw_math(
                w_ref[...], g, m_ref[...], v_ref[...])
            outs[4 * a + 3][...] = g

    vm = pl.BlockSpec(memory_space=pltpu.VMEM)
    out = pl.pallas_call(
        body, name=name, in_specs=[vm] * (4 * n), out_specs=[vm] * (4 * n),
        out_shape=[jax.ShapeDtypeStruct(grp[0].shape, F32) for grp in groups for _ in range(4)],
        compiler_params=_params(),
    )(*[a for grp in groups for a in grp])
    return [out[4 * a:4 * a + 4] for a in range(n)]


def _small_update(red, lbl, params):
    def body(red_ref, *refs):
        ins, outs = refs[:12], refs[12:]
        lb = _lower_bound(ins[3][...])
        dl0 = red_ref[1:2, :] * lb * (1.0 - lb)
        row = lax.broadcasted_iota(jnp.int32, (2, D), 0)
        grads = [red_ref[0:1, :], jnp.where(row == 0, dl0, -dl0), red_ref[2:3, 0:HK], red_ref[3:4, :]]
        for i, g in enumerate(grads):
            w, m, v = ins[3 * i][...], ins[3 * i + 1][...], ins[3 * i + 2][...]
            d, nm, nv = _adamw_math(w, g, m, v)
            outs[4 * i][...] = g
            outs[4 * i + 1][...] = d
            outs[4 * i + 2][...] = nm
            outs[4 * i + 3][...] = nv
        outs[16][...] = jnp.sum(red_ref[4:5, :], axis=1, keepdims=True)

    flat = [a for p in params for a in p]
    vm = pl.BlockSpec(memory_space=pltpu.VMEM)
    shapes = [jax.ShapeDtypeStruct(p[0].shape, F32) for p in params for _ in range(4)]
    return pl.pallas_call(
        body, name="small_update",
        in_specs=[vm] * 13, out_specs=[vm] * 17,
        out_shape=shapes + [jax.ShapeDtypeStruct((1, 1), F32)],
    )(red, *flat)


def kernel(x, positions, norm_w, w_in, lb_logits, hgrn_norm_w, w_branch_a, w_branch_b, w_out, final_norm_w, loss_target, m_norm_w, m_w_in, m_lb_logits, m_hgrn_norm_w, m_w_branch_a, m_w_branch_b, m_w_out, m_final_norm_w, v_norm_w, v_w_in, v_lb_logits, v_hgrn_norm_w, v_w_branch_a, v_w_branch_b, v_w_out, v_final_norm_w):
    big_w = [w_in[0], w_branch_a[0], w_branch_b[0], w_out[0]]
    big_m = [m_w_in[0], m_w_branch_a[0], m_w_branch_b[0], m_w_out[0]]
    big_v = [v_w_in[0], v_w_branch_a[0], v_w_branch_b[0], v_w_out[0]]
    shapes = [w.shape for w in big_w]
    wf = final_norm_w.reshape(1, D)

    shards = [w.astype(BF) for w in big_w]
    loc = _local_step(x[0], positions.reshape(T, 1), norm_w, lb_logits, hgrn_norm_w, wf, loss_target[0],
                      *shards, shard_shapes=shapes)
    g_big = [loc["g_win"], loc["g_wa"], loc["g_wb"], loc["g_wout"]]
    red = loc["small_sums"]

    small = _small_update(red, lb_logits, [
        (norm_w, m_norm_w, v_norm_w), (lb_logits, m_lb_logits, v_lb_logits),
        (hgrn_norm_w, m_hgrn_norm_w, v_hgrn_norm_w),
        (wf, m_final_norm_w.reshape(1, D), v_final_norm_w.reshape(1, D))])
    loss = small[16].reshape(())
    sg, sd, sm, sv = ([small[4 * i + j] for i in range(4)] for j in range(4))
    for lst in (sg, sd, sm, sv):
        lst[3] = lst[3].reshape(D)
    per_w = list(zip(big_w, g_big, big_m, big_v))
    upd = [_adamw(*per_w[0], "adamw_w_in")] + _adamw_whole(per_w[1:], "adamw_branches_out")
    bd, bm, bv, bg = ([u[j][None] for u in upd] for j in range(4))

    def order(s, b):
        return [s[0], b[0], s[1], s[2], b[1], b[2], b[3], s[3]]

    return (loss, loc["gx"][None], *order(sg, bg), *order(sd, bd), *order(sm, bm), *order(sv, bv))
```

```python
import functools

import jax
import jax.numpy as jnp
from jax import lax
from jax.experimental import pallas as pl
from jax.experimental.pallas import tpu as pltpu

T = 2048
D = 1024
NIN = 11264
HEADS = 8
HK = 128
CH = 16
NCH = T // CH
HSTEP = 2
ATT_GROUPS = ((128, 1), (512, 4), (2048, 16))
ATT_COL0 = 4096
AG_COL0 = 8704
GATE_COL0 = 9216
EPS = 1e-6
ROPE_THETA = 10000.0
LR, B1, B2, ADAM_EPS, WD, STEP = 0.001, 0.9, 0.999, 1e-08, 0.01, 10

F32 = jnp.float32
BF = jnp.bfloat16
VMEM_LIMIT = 56 * 1024 * 1024

_NN = (((1,), (0,)), ((), ()))
_NT = (((1,), (1,)), ((), ()))
_TN = (((0,), (0,)), ((), ()))


def _dot(a, b, dims=_NN):
    return lax.dot_general(a, b, dims, preferred_element_type=F32)


def _bdot(a, b, dims=_NN):
    return lax.dot_general(a.astype(BF), b.astype(BF), dims, preferred_element_type=F32)


def _sigmoid(x):
    return jax.nn.sigmoid(x)


def _params(sem=None):
    return pltpu.CompilerParams(dimension_semantics=sem, vmem_limit_bytes=VMEM_LIMIT)


def _matmul(a, b, *, ta=False, tb=False, out_dtype=F32, tm=512, tn=512, tk=None, name, side=None):
    m = a.shape[1] if ta else a.shape[0]
    kdim = a.shape[0] if ta else a.shape[1]
    n = b.shape[0] if tb else b.shape[1]
    tk = tk or kdim
    tm, tn = min(tm, m), min(tn, n)
    nm, nn, nk = m // tm, n // tn, kdim // tk
    dims = (((0 if ta else 1,), (1 if tb else 0,)), ((), ()))
    s_arrays, s_in_specs, s_shapes, s_out_specs, s_sems = _side_io(side)
    na, no = len(s_arrays), len(s_shapes)
    nacc = 1 if nk > 1 else 0

    def body(*refs):
        a_ref, b_ref = refs[:2]
        s_ins, o_ref, s_outs = refs[2:2 + na], refs[2 + na], refs[3 + na:3 + na + no]
        scratch = refs[3 + na + no:]
        s_sem_refs = scratch[nacc:]
        i, j, k = pl.program_id(0), pl.program_id(1), pl.program_id(2)
        if side is not None:
            @pl.when((i == 0) & (j == 0) & (k == 0))
            def _():
                side.first(s_ins, s_outs, s_sem_refs)

        prod = _bdot(a_ref[...], b_ref[...], dims)
        if nk == 1:
            o_ref[...] = prod.astype(out_dtype)
        else:
            acc = scratch[0]

            @pl.when(k == 0)
            def _():
                acc[...] = prod

            @pl.when(k > 0)
            def _():
                acc[...] += prod

            @pl.when(k == nk - 1)
            def _():
                o_ref[...] = acc[...].astype(out_dtype)

        if side is not None:
            @pl.when((i == nm - 1) & (j == nn - 1) & (k == nk - 1))
            def _():
                side.last(s_ins, s_outs, s_sem_refs)

    a_spec = pl.BlockSpec((tk, tm), lambda i, j, k: (k, i)) if ta else pl.BlockSpec((tm, tk), lambda i, j, k: (i, k))
    b_spec = pl.BlockSpec((tn, tk), lambda i, j, k: (j, k)) if tb else pl.BlockSpec((tk, tn), lambda i, j, k: (k, j))
    sem = ("parallel", "parallel", "arbitrary") if side is None else ("arbitrary",) * 3
    out = pl.pallas_call(
        body, name=name, grid=(nm, nn, nk),
        in_specs=[a_spec, b_spec] + s_in_specs,
        out_specs=[pl.BlockSpec((tm, tn), lambda i, j, k: (i, j))] + s_out_specs,
        out_shape=[jax.ShapeDtypeStruct((m, n), out_dtype)] + s_shapes,
        scratch_shapes=([pltpu.VMEM((tm, tn), F32)] if nk > 1 else []) + s_sems,
        compiler_params=_params(sem),
    )(a, b, *s_arrays)
    return out[0] if side is None else (out[0], out[1:])


DZ_TILE = 512


def _part_offsets(parts):
    counts = [p.shape[1] // DZ_TILE for p in parts]
    offs = [sum(counts[:i]) for i in range(len(parts))]
    return counts, offs


def _part_spec(rows, cnt, off, tile_axis):
    def index(*g):
        return (0 if rows is None else g[0], jnp.clip(g[tile_axis] - off, 0, cnt - 1))
    return index


def _grad_w_in(h, parts):
    counts, offs = _part_offsets(parts)
    n = len(parts)

    def body(h_ref, *refs):
        o_ref = refs[n]
        j = pl.program_id(0)
        for p_ref, cnt, off in zip(refs[:n], counts, offs):
            @pl.when((j >= off) & (j < off + cnt))
            def _(p_ref=p_ref):
                o_ref[...] = _bdot(h_ref[...], p_ref[...], _TN).astype(BF)

    return pl.pallas_call(
        body, name="g_win", grid=(sum(counts),),
        in_specs=[pl.BlockSpec((T, D), lambda j: (0, 0))] +
                 [pl.BlockSpec((T, DZ_TILE), _part_spec(None, c, o, 0)) for c, o in zip(counts, offs)],
        out_specs=pl.BlockSpec((D, DZ_TILE), lambda j: (0, j)),
        out_shape=jax.ShapeDtypeStruct((D, NIN), BF),
        compiler_params=_params(("parallel",)),
    )(h, *parts)


def _grad_w_in_half(h, parts, half_idx, side=None):
    counts, offs = _part_offsets(parts)
    n = len(parts)
    nj = sum(counts)
    s_arrays, s_in_specs, s_shapes, s_out_specs, s_sems = _side_io(side)
    na, no = len(s_arrays), len(s_shapes)

    def body(idx_ref, h_ref, *refs):
        s_ins, o_ref, s_outs, s_sem_refs = refs[n:n + na], refs[n + na], refs[n + na + 1:n + na + 1 + no], refs[n + na + 1 + no:]
        j = pl.program_id(0)
        if side is not None:
            @pl.when(j == 0)
            def _():
                side.first(s_ins, s_outs, s_sem_refs)

        for p_ref, cnt, off in zip(refs[:n], counts, offs):
            @pl.when((j >= off) & (j < off + cnt))
            def _(p_ref=p_ref):
                o_ref[...] = _bdot(h_ref[...], p_ref[...], _TN).astype(BF)

        if side is not None:
            @pl.when(j == nj - 1)
            def _():
                side.last(s_ins, s_outs, s_sem_refs)

    def part_spec(cnt, off):
        return pl.BlockSpec((T, DZ_TILE), lambda j, idx: (0, jnp.clip(j - off, 0, cnt - 1)))

    out = pl.pallas_call(
        body, name="g_win_half" if side is None else "g_win_half_carrying",
        grid_spec=pltpu.PrefetchScalarGridSpec(
            num_scalar_prefetch=1, grid=(nj,),
            in_specs=[pl.BlockSpec((T, D // 2), lambda j, idx: (0, idx[0]))] +
                     [part_spec(c, o) for c, o in zip(counts, offs)] + s_in_specs,
            out_specs=[pl.BlockSpec((D // 2, DZ_TILE), lambda j, idx: (0, j))] + s_out_specs,
            scratch_shapes=s_sems),
        out_shape=[jax.ShapeDtypeStruct((D // 2, NIN), BF)] + s_shapes,
        compiler_params=_params(("parallel",) if side is None else ("arbitrary",)),
    )(half_idx, h, *parts, *s_arrays)
    return out[0] if side is None else (out[0], out[1:])


def _side_io(side):
    if side is None:
        return [], [], [], [], []
    return (side.arrays, [HBM] * len(side.arrays), side.out_shapes, [HBM] * len(side.out_shapes), side.sems)


def _grad_x(parts, w_in, x, dout, norm_w, side=None):
    counts, offs = _part_offsets(parts)
    n = len(parts)
    tm = 1024
    nm, nk = T // tm, sum(counts)
    s_arrays, s_in_specs, s_shapes, s_out_specs, s_sems = _side_io(side)
    na, no = len(s_arrays), len(s_shapes)

    def body(*refs):
        w_ref, x_ref, dout_ref, nw_ref = refs[n:n + 4]
        s_ins = refs[n + 4:n + 4 + na]
        gx_ref, gw_ref = refs[n + 4 + na:n + 6 + na]
        s_outs = refs[n + 6 + na:n + 6 + na + no]
        acc = refs[n + 6 + na + no]
        s_sem_refs = refs[n + 7 + na + no:]
        i, k = pl.program_id(0), pl.program_id(1)

        @pl.when((i == 0) & (k == 0))
        def _():
            gw_ref[...] = jnp.zeros_like(gw_ref)
            if side is not None:
                side.first(s_ins, s_outs, s_sem_refs)

        @pl.when(k == 0)
        def _():
            acc[...] = jnp.zeros_like(acc)

        if side is not None and side.mid is not None:
            @pl.when((i == nm - 1) & (k == 0))
            def _():
                side.mid(s_ins, s_outs, s_sem_refs)

        for p_ref, cnt, off in zip(refs[:n], counts, offs):
            @pl.when((k >= off) & (k < off + cnt))
            def _(p_ref=p_ref):
                acc[...] += _bdot(p_ref[...], w_ref[...], _NT)

        @pl.when(k == nk - 1)
        def _():
            gw = jnp.zeros((1, D), F32)
            for c in range(tm // BLK):
                rows = pl.ds(BLK * c, BLK)
                xv, dhv = x_ref[rows, :], acc[rows, :]
                r = lax.rsqrt(jnp.mean(xv * xv, axis=-1, keepdims=True) + EPS)
                nrm = xv * r
                dn = dhv * nw_ref[...]
                gw = gw + jnp.sum(dhv * nrm, axis=0, keepdims=True)
                gx_ref[rows, :] = dout_ref[rows, :] + r * (dn - nrm * jnp.mean(dn * nrm, axis=-1, keepdims=True))
            gw_ref[...] += gw

        if side is not None:
            @pl.when((i == nm - 1) & (k == nk - 1))
            def _():
                side.last(s_ins, s_outs, s_sem_refs)

    row = pl.BlockSpec((tm, D), lambda i, k: (i, 0))
    vec = pl.BlockSpec((1, D), lambda i, k: (0, 0))
    out = pl.pallas_call(
        body, name="grad_x", grid=(nm, nk),
        in_specs=[pl.BlockSpec((tm, DZ_TILE), _part_spec(0, c, o, 1)) for c, o in zip(counts, offs)] +
                 [pl.BlockSpec((D, DZ_TILE), lambda i, k: (0, k)), row, row, vec] + s_in_specs,
        out_specs=[row, vec] + s_out_specs,
        out_shape=[jax.ShapeDtypeStruct((T, D), F32), jax.ShapeDtypeStruct((1, D), F32)] + s_shapes,
        scratch_shapes=[pltpu.VMEM((tm, D), F32)] + s_sems,
        compiler_params=_params(("arbitrary", "arbitrary")),
    )(*parts, w_in, x, dout, norm_w, *s_arrays)
    return out[0], out[1], out[2:]


def _norm_and_rope_tables(x, w, pos, invf, side=None, own=None):
    tm = 256
    nm = T // tm
    s_arrays, s_in_specs, s_shapes, s_out_specs, s_sems = _side_io(side)
    na, no = len(s_arrays), len(s_shapes)
    nz = 0 if own is None else 1
    wsh, blk = own if own is not None else (None, jnp.zeros((1,), jnp.int32))

    def body(blk_ref, *refs):
        x_ref, w_ref, pos_ref, invf_ref = refs[:4]
        s_ins = refs[4 + nz:4 + nz + na]
        h_ref, cos_ref, sa_ref, sb_ref = refs[4 + nz + na:8 + nz + na]
        s_outs = refs[8 + 2 * nz + na:8 + 2 * nz + na + no]
        s_sem_refs = refs[8 + 2 * nz + na + no:]
        if side is not None:
            @pl.when(pl.program_id(0) == 0)
            def _():
                side.first(s_ins, s_outs, s_sem_refs)

        xv = x_ref[...]
        r = lax.rsqrt(jnp.mean(xv * xv, axis=-1, keepdims=True) + EPS)
        h = (xv * r * w_ref[...]).astype(BF)
        h_ref[...] = h
        if own is not None:
            refs[8 + nz + na][...] = _dot(h, refs[4][...])
        first = (lax.broadcasted_iota(jnp.int32, (tm, 128), 1) % 64) < 32
        ang = pos_ref[...].astype(F32) * invf_ref[...]
        s = jnp.sin(ang)
        cos_ref[...] = jnp.cos(ang)
        sa_ref[...] = jnp.where(first, -s, 0.0)
        sb_ref[...] = jnp.where(first, 0.0, s)
        if side is not None:
            @pl.when(pl.program_id(0) == nm - 1)
            def _():
                side.last(s_ins, s_outs, s_sem_refs)

    tab = pl.BlockSpec((tm, 128), lambda i, b: (i, 0))
    own_in = [] if own is None else [pl.BlockSpec(wsh.shape, lambda i, b: (0, 0))]
    own_out = [] if own is None else [pl.BlockSpec((tm, wsh.shape[1]), lambda i, b: (i, b[0]))]
    own_shape = [] if own is None else [jax.ShapeDtypeStruct((T, NIN), F32)]
    out = pl.pallas_call(
        body, name="norm_and_rope_tables",
        grid_spec=pltpu.PrefetchScalarGridSpec(
            num_scalar_prefetch=1, grid=(nm,),
            in_specs=[pl.BlockSpec((tm, D), lambda i, b: (i, 0)), pl.BlockSpec((1, D), lambda i, b: (0, 0)),
                      pl.BlockSpec((tm, 1), lambda i, b: (i, 0)), pl.BlockSpec((1, 128), lambda i, b: (0, 0))]
                     + own_in + s_in_specs,
            out_specs=[pl.BlockSpec((tm, D), lambda i, b: (i, 0)), tab, tab, tab] + own_out + s_out_specs,
            scratch_shapes=s_sems),
        out_shape=[jax.ShapeDtypeStruct((T, D), BF)] + [jax.ShapeDtypeStruct((T, 128), F32)] * 3 + own_shape + s_shapes,
        compiler_params=_params(("parallel",) if side is None else ("arbitrary",)),
    )(blk, x, w, pos, invf, *([] if own is None else [wsh]), *s_arrays)
    return out[0], out[1], out[2], out[3], (out[4] if own is not None else None), out[4 + nz:]


def _z_blocks(h, w, z, idx, nb, side, name, fill=None):
    tm, tn = 1024, NIN // 8
    s_arrays, s_in_specs, s_shapes, s_out_specs, s_sems = _side_io(side)
    na, no = len(s_arrays), len(s_shapes)
    nm, ns = T // tm, 2 * nb
    nf = 0 if fill is None else 1

    def col(first, i, s, b):
        return (0, b[first + s // 2] * 2 + s % 2)

    def body(idx_ref, h_ref, w_ref, zin_ref, *refs):
        s_ins = refs[nf:nf + na]
        o_ref = refs[nf + na]
        s_outs = refs[nf + na + 1 + nf:nf + na + 1 + nf + no]
        s_sem_refs = refs[nf + na + 1 + nf + no + nf:]
        i, s = pl.program_id(0), pl.program_id(1)

        if side is not None:
            @pl.when((i == 0) & (s == 0))
            def _():
                side.first(s_ins, s_outs, s_sem_refs)

        if fill is not None:
            tile = pl.ds(pl.multiple_of((idx_ref[0] * 2 + s) * tn, 128), tn)
            store = pltpu.make_async_copy(w_ref, refs[nf + na + 1].at[:, tile], refs[nf + na + 1 + nf + no].at[0])
            pl.when(i == 0)(store.start)
        o_ref[...] = _dot(h_ref[...], w_ref[...])
        if fill is not None:
            pl.when(i == 0)(store.wait)

        if side is not None:
            @pl.when((i == nm - 1) & (s == ns - 1))
            def _():
                side.last(s_ins, s_outs, s_sem_refs)

    fills = [] if fill is None else [fill]
    out = pl.pallas_call(
        body, name=name,
        grid_spec=pltpu.PrefetchScalarGridSpec(
            num_scalar_prefetch=1, grid=(nm, ns),
            in_specs=[pl.BlockSpec((tm, D), lambda i, s, b: (i, 0)), pl.BlockSpec((D, tn), functools.partial(col, nb)),
                      HBM] + [HBM] * nf + s_in_specs,
            out_specs=[pl.BlockSpec((tm, tn), lambda i, s, b: (i, col(0, i, s, b)[1]))] + [HBM] * nf + s_out_specs,
            scratch_shapes=[pltpu.SemaphoreType.DMA((1,))] * nf + s_sems),
        out_shape=[jax.ShapeDtypeStruct((T, NIN), F32)] + [jax.ShapeDtypeStruct(f.shape, f.dtype) for f in fills]
                  + s_shapes,
        input_output_aliases={3: 0, **({4: 1} if fill is not None else {})},
        compiler_params=_params(("arbitrary", "arbitrary")),
    )(idx, h, w, z, *fills, *s_arrays)
    return (out[0], *out[1:1 + nf], out[1 + nf:])


def _lower_bound(lbl):
    mx = jnp.max(lbl, axis=0, keepdims=True)
    e = jnp.exp(lbl - mx)
    return e[0:1] / jnp.sum(e, axis=0, keepdims=True)


def _cumsum_rows(g, rows):
    b = g
    sh = 1
    while sh < CH:
        b = b + jnp.where(rows >= sh, pltpu.roll(b, sh, axis=0), 0.0)
        sh *= 2
    return b


def _rev_cumsum_rows(g, rows):
    b = g
    sh = 1
    while sh < CH:
        b = b + jnp.where(rows < CH - sh, pltpu.roll(b, CH - sh, axis=0), 0.0)
        sh *= 2
    return b


SUB = CH // 2


def _direct_block(qb, kb, vb, bb, rows8):
    ob = jnp.zeros_like(qb)
    for s in range(SUB):
        e_s = jnp.exp(jnp.where(rows8 >= s, bb - bb[s:s + 1], -jnp.inf))
        ob = ob + jnp.sum(qb * e_s * kb[s:s + 1], axis=1, keepdims=True) * vb[s:s + 1]
    return ob


def _direct_block_bwd(qb, kb, vb, bb, dob, rows8, rowc8):
    dq = dk = dv = db = jnp.zeros_like(qb)
    for s in range(SUB):
        one = (rowc8 == s).astype(F32)
        ks, vs = kb[s:s + 1], vb[s:s + 1]
        e_s = jnp.exp(jnp.where(rows8 >= s, bb - bb[s:s + 1], -jnp.inf))
        qes = qb * e_s
        w = qes * ks
        a = jnp.sum(w, axis=1, keepdims=True)
        da = jnp.sum(dob * vs, axis=1, keepdims=True)
        dv = dv + one * jnp.sum(a * dob, axis=0, keepdims=True)
        dq = dq + da * e_s * ks
        dk = dk + one * jnp.sum(da * qes, axis=0, keepdims=True)
        u = da * w
        db = db + u - one * jnp.sum(u, axis=0, keepdims=True)
    return dq, dk, dv, db


def _cross_factors(q, k, b):
    ref = b[SUB - 1:SUB]
    e_hi, e_lo = jnp.exp(b[SUB:] - ref), jnp.exp(ref - b[:SUB])
    return q[SUB:] * e_hi, k[:SUB] * e_lo, e_hi, e_lo


def _intra_fwd(q, k, v, b, rows8):
    lo = _direct_block(q[:SUB], k[:SUB], v[:SUB], b[:SUB], rows8)
    hi = _direct_block(q[SUB:], k[SUB:], v[SUB:], b[SUB:], rows8)
    qe_hi, ke_lo, _, _ = _cross_factors(q, k, b)
    for s in range(SUB):
        hi = hi + jnp.sum(qe_hi * ke_lo[s:s + 1], axis=1, keepdims=True) * v[s:s + 1]
    return jnp.concatenate([lo, hi], axis=0)


def _intra_bwd(q, k, v, b, do, rows8, rowc8):
    dq_lo, dk_lo, dv_lo, db_lo = _direct_block_bwd(q[:SUB], k[:SUB], v[:SUB], b[:SUB], do[:SUB], rows8, rowc8)
    dq_hi, dk_hi, dv_hi, db_hi = _direct_block_bwd(q[SUB:], k[SUB:], v[SUB:], b[SUB:], do[SUB:], rows8, rowc8)
    qe_hi, ke_lo, e_hi, e_lo = _cross_factors(q, k, b)
    do_hi, v_lo = do[SUB:], v[:SUB]
    dqe = dke = jnp.zeros_like(qe_hi)
    for s in range(SUB):
        one = (rowc8 == s).astype(F32)
        a = jnp.sum(qe_hi * ke_lo[s:s + 1], axis=1, keepdims=True)
        da = jnp.sum(do_hi * v_lo[s:s + 1], axis=1, keepdims=True)
        dv_lo = dv_lo + one * jnp.sum(a * do_hi, axis=0, keepdims=True)
        dqe = dqe + da * ke_lo[s:s + 1]
        dke = dke + one * jnp.sum(da * qe_hi, axis=0, keepdims=True)
    u_hi, u_lo = dqe * qe_hi, dke * ke_lo
    d_ref = jnp.sum(u_lo, axis=0, keepdims=True) - jnp.sum(u_hi, axis=0, keepdims=True)
    db_lo = db_lo - u_lo + (rowc8 == SUB - 1).astype(F32) * d_ref
    cat = lambda lo, hi: jnp.concatenate([lo, hi], axis=0)
    return (cat(dq_lo, dq_hi + dqe * e_hi), cat(dk_lo + dke * e_lo, dk_hi), cat(dv_lo, dv_hi),
            cat(db_lo, db_hi + u_hi))


def _hgrn_fwd(z, lbl, nw, side=None):
    s_arrays, s_in_specs, s_shapes, s_out_specs, s_sems = _side_io(side)
    na, no = len(s_arrays), len(s_shapes)
    nsteps = NCH // HSTEP

    def body(hq_ref, hf_ref, hi_ref, hg_ref, lbl_ref, nw_ref, *refs):
        s_ins, (oraw_ref, og_ref, sh_ref) = refs[:na], refs[na:na + 3]
        s_outs, st_ref, s_sem_refs = refs[na + 3:na + 3 + no], refs[na + 3 + no], refs[na + 4 + no:]

        @pl.when(pl.program_id(0) == 0)
        def _():
            st_ref[...] = jnp.zeros_like(st_ref)
            if side is not None:
                side.first(s_ins, s_outs, s_sem_refs)

        lb_all = _lower_bound(lbl_ref[...])
        rows = lax.broadcasted_iota(jnp.int32, (CH, HK), 0)
        rows8 = lax.broadcasted_iota(jnp.int32, (SUB, HK), 0)
        nwv = nw_ref[...]
        for cc, h in [(cc, h) for cc in range(HSTEP) for h in range(HEADS)]:
            rs = slice(CH * cc, CH * (cc + 1))
            sl = slice(HK * h, HK * (h + 1))
            lb = lb_all[:, sl]
            hq, hf, v, hg = hq_ref[rs, sl], hf_ref[rs, sl], hi_ref[rs, sl], hg_ref[rs, sl]
            q = hq * _sigmoid(hq)
            f = lb + (1.0 - lb) * _sigmoid(hf)
            k = 1.0 - f
            b = _cumsum_rows(jnp.log(f), rows)
            sh_ref[cc, h] = st_ref[h]
            o = _bdot(q * jnp.exp(b), st_ref[h], _NT) + _intra_fwd(q, k, v, b, rows8)
            bl = b[CH - 1:CH]
            st_ref[h] = st_ref[h] * jnp.exp(bl)
            st_ref[h] += _bdot(v, k * jnp.exp(bl - b), _TN)
            oraw_ref[rs, sl] = o
            nrm = o * lax.rsqrt(jnp.mean(o * o, axis=1, keepdims=True) + EPS)
            og_ref[rs, sl] = (nrm * nwv * (hg * _sigmoid(hg))).astype(BF)

        if side is not None:
            @pl.when(pl.program_id(0) == nsteps // 2)
            def _():
                side.mid(s_ins, s_outs, s_sem_refs)

            @pl.when(pl.program_id(0) == nsteps - 1)
            def _():
                side.last(s_ins, s_outs, s_sem_refs)

    zblk = lambda c: pl.BlockSpec((CH * HSTEP, D), lambda i, c=c: (i, c))
    out = pl.pallas_call(
        body, name="hgrn_fwd", grid=(nsteps,),
        in_specs=[zblk(0), zblk(1), zblk(2), zblk(3),
                  pl.BlockSpec((2, D), lambda i: (0, 0)), pl.BlockSpec((1, HK), lambda i: (0, 0))] + s_in_specs,
        out_specs=[zblk(0), zblk(0),
                   pl.BlockSpec((HSTEP, HEADS, HK, HK), lambda i: (i, 0, 0, 0))] + s_out_specs,
        out_shape=[jax.ShapeDtypeStruct((T, D), F32), jax.ShapeDtypeStruct((T, D), BF),
                   jax.ShapeDtypeStruct((NCH, HEADS, HK, HK), F32)] + s_shapes,
        scratch_shapes=[pltpu.VMEM((HEADS, HK, HK), F32)] + s_sems,
        compiler_params=_params(("arbitrary",)),
    )(z, z, z, z, lbl, nw, *s_arrays)
    return out[0], out[1], out[2], out[3:]


def _hgrn_bwd(z, lbl, nw, oraw, dog, shist, side=None):
    hstep = 1
    s_arrays, s_in_specs, s_shapes, s_out_specs, s_sems = _side_io(side)
    na, no = len(s_arrays), len(s_shapes)

    def body(*refs):
        hq_ref, hf_ref, hi_ref, hg_ref, lbl_ref, nw_ref, oraw_ref, dog_ref, sh_ref = refs[:9]
        s_ins = refs[9:9 + na]
        dz_ref, dlb_ref, dnw_ref = refs[9 + na:12 + na]
        s_outs = refs[12 + na:12 + na + no]
        dst_ref = refs[12 + na + no]
        s_sem_refs = refs[13 + na + no:]

        @pl.when(pl.program_id(0) == 0)
        def _():
            dst_ref[...] = jnp.zeros_like(dst_ref)
            dlb_ref[...] = jnp.zeros_like(dlb_ref)
            dnw_ref[...] = jnp.zeros_like(dnw_ref)
            if side is not None:
                side.first(s_ins, s_outs, s_sem_refs)

        lb_all = _lower_bound(lbl_ref[...])
        rows = lax.broadcasted_iota(jnp.int32, (CH, HK), 0)
        rowc = lax.broadcasted_iota(jnp.int32, (CH, 1), 0)
        rows8 = lax.broadcasted_iota(jnp.int32, (SUB, HK), 0)
        rowc8 = lax.broadcasted_iota(jnp.int32, (SUB, 1), 0)
        nwv = nw_ref[...]
        dnw = jnp.zeros((1, HK), F32)
        for cc, h in [(cc, h) for cc in reversed(range(hstep)) for h in range(HEADS)]:
            rs = slice(CH * cc, CH * (cc + 1))
            sl = slice(HK * h, HK * (h + 1))
            lb = lb_all[:, sl]
            hq, hf, v, hg = hq_ref[rs, sl], hf_ref[rs, sl], hi_ref[rs, sl], hg_ref[rs, sl]
            o, dg_out = oraw_ref[rs, sl], dog_ref[rs, sl]
            sg = _sigmoid(hg)
            sil = hg * sg
            r = lax.rsqrt(jnp.mean(o * o, axis=1, keepdims=True) + EPS)
            nrm = o * r
            d_hg = dg_out * (nrm * nwv) * (sg * (1.0 + hg * (1.0 - sg)))
            dn = dg_out * nwv * sil
            dnw = dnw + jnp.sum(dg_out * nrm * sil, axis=0, keepdims=True)
            do = r * (dn - nrm * jnp.mean(dn * nrm, axis=1, keepdims=True))
            sq = _sigmoid(hq)
            q = hq * sq
            sig = _sigmoid(hf)
            f = lb + (1.0 - lb) * sig
            k = 1.0 - f
            b = _cumsum_rows(jnp.log(f), rows)
            eb = jnp.exp(b)
            qe = q * eb
            bl = b[CH - 1:CH]
            ebl = jnp.exp(bl)
            kdec = jnp.exp(bl - b)
            ke = k * kdec
            dqe = _bdot(do, sh_ref[cc, h])
            dq = dqe * eb
            db = dqe * qe
            dke = _bdot(v, dst_ref[h])
            dv = _bdot(ke, dst_ref[h], _NT)
            dk = dke * kdec
            rr = dke * ke
            db = db - rr
            db_last = (jnp.sum(rr, axis=0, keepdims=True)
                       + ebl * jnp.sum(dst_ref[h] * sh_ref[cc, h], axis=0, keepdims=True))
            dst_ref[h] = dst_ref[h] * ebl
            dst_ref[h] += _bdot(do, qe, _TN)
            dq_i, dk_i, dv_i, db_i = _intra_bwd(q, k, v, b, do, rows8, rowc8)
            dq, dk, dv = dq + dq_i, dk + dk_i, dv + dv_i
            db = db + db_i + (rowc == CH - 1).astype(F32) * db_last
            dgl = _rev_cumsum_rows(db, rows)
            df = dgl / f - dk
            dlb_ref[:, sl] += jnp.sum(df * (1.0 - sig), axis=0, keepdims=True)
            dz_ref[rs, sl] = (dq * (sq * (1.0 + hq * (1.0 - sq)))).astype(BF)
            dz_ref[rs, D + HK * h:D + HK * (h + 1)] = (df * (1.0 - lb) * sig * (1.0 - sig)).astype(BF)
            dz_ref[rs, 2 * D + HK * h:2 * D + HK * (h + 1)] = dv.astype(BF)
            dz_ref[rs, 3 * D + HK * h:3 * D + HK * (h + 1)] = d_hg.astype(BF)
        dnw_ref[...] += dnw
        if side is not None:
            @pl.when(pl.program_id(0) == NCH // hstep - 1)
            def _():
                side.last(s_ins, s_outs, s_sem_refs)

    rev = lambda i: NCH // hstep - 1 - i
    zblk = lambda c: pl.BlockSpec((CH * hstep, D), lambda i, c=c: (rev(i), c))
    out = pl.pallas_call(
        body, name="hgrn_bwd", grid=(NCH // hstep,),
        in_specs=[zblk(0), zblk(1), zblk(2), zblk(3),
                  pl.BlockSpec((2, D), lambda i: (0, 0)), pl.BlockSpec((1, HK), lambda i: (0, 0)),
                  zblk(0), zblk(0),
                  pl.BlockSpec((hstep, HEADS, HK, HK), lambda i: (rev(i), 0, 0, 0))] + s_in_specs,
        out_specs=[pl.BlockSpec((CH * hstep, 4 * D), lambda i: (rev(i), 0)),
                   pl.BlockSpec((1, D), lambda i: (0, 0)), pl.BlockSpec((1, HK), lambda i: (0, 0))] + s_out_specs,
        out_shape=[jax.ShapeDtypeStruct((T, 4 * D), BF), jax.ShapeDtypeStruct((1, D), F32),
                   jax.ShapeDtypeStruct((1, HK), F32)] + s_shapes,
        scratch_shapes=[pltpu.VMEM((HEADS, HK, HK), F32)] + s_sems,
        compiler_params=_params(("arbitrary",)),
    )(z, z, z, z, lbl, nw, oraw, dog, shist, *s_arrays)
    return out[0], out[1], out[2], out[3:]


BLK = 128
NBLK = T // BLK
QK_SCALE = 0.125


def _head_masks():
    lane = lax.broadcasted_iota(jnp.int32, (1, BLK), 1)
    return [(lane < 64).astype(F32), (lane >= 64).astype(F32)]


def _pieces(dil):
    m = T // dil
    out = []
    for r in range(dil):
        for j in range(m // BLK):
            start = r + dil * BLK * j
            rows = pl.ds(start, BLK, stride=dil) if dil > 1 else pl.ds(start, BLK)
            out.append((rows, r * m + BLK * j))
    return out


def _rope(x, c, sa, sb):
    return x * c + pltpu.roll(x, 96, axis=1) * sa + pltpu.roll(x, 32, axis=1) * sb


def _rope_t(d, c, sa, sb):
    return d * c + pltpu.roll(d * sa, 32, axis=1) + pltpu.roll(d * sb, 96, axis=1)


def _rope_and_regroup(dil, q_ref, k_ref, v_ref, tables, stage_q, stage_k, qr_ref, kr_ref, vr_ref):
    cos_ref, sa_ref, sb_ref = tables
    to_q, to_k = (qr_ref, kr_ref) if dil == 1 else (stage_q, stage_k)
    for c in range(T // BLK):
        rows = pl.ds(BLK * c, BLK)
        cs, sa, sb = cos_ref[rows, :], sa_ref[rows, :], sb_ref[rows, :]
        to_q[rows, :] = (_rope(q_ref[rows, :], cs, sa, sb) * QK_SCALE).astype(to_q.dtype)
        to_k[rows, :] = _rope(k_ref[rows, :], cs, sa, sb).astype(to_k.dtype)
    for rows, dst in _pieces(dil):
        drows = pl.ds(dst, BLK)
        if dil > 1:
            qr_ref[drows, :] = stage_q[rows, :].astype(qr_ref.dtype)
            kr_ref[drows, :] = stage_k[rows, :].astype(kr_ref.dtype)
        vr_ref[drows, :] = v_ref[rows, :].astype(vr_ref.dtype)


def _window_bias(bias_ref):
    ii = lax.broadcasted_iota(jnp.int32, (2 * BLK, BLK), 0) % BLK
    jj = lax.broadcasted_iota(jnp.int32, (2 * BLK, BLK), 1)
    bias_ref[0] = jnp.where(jj <= ii, 0.0, -jnp.inf)
    bias_ref[1] = jnp.where(jj >= ii, 0.0, -jnp.inf)


def _blocks(bi):
    if isinstance(bi, int):
        return pl.ds(bi * BLK, BLK), pl.ds(max(bi - 1, 0) * BLK, BLK)
    return (pl.ds(pl.multiple_of(bi * BLK, BLK), BLK),
            pl.ds(pl.multiple_of(jnp.maximum(bi - 1, 0) * BLK, BLK), BLK))


def _stack_heads(x, masks):
    return jnp.concatenate([x * masks[0].astype(x.dtype), x * masks[1].astype(x.dtype)], axis=0).astype(BF)


def _side_steps(side, refs, **when):
    if side is None:
        return
    for stage, cond in when.items():
        if getattr(side, stage) is not None:
            pl.when(cond)(functools.partial(getattr(side, stage), *refs))


def _attn_fwd(z, cos, sa, sb, side=None):
    s_arrays, s_in_specs, s_shapes, s_out_specs, s_sems = _side_io(side)
    na, no = len(s_arrays), len(s_shapes)

    def body(q_ref, k_ref, v_ref, ag_ref, cos_ref, sa_ref, sb_ref, *refs):
        ob_ref, opre_ref, lse_ref, qr_ref, kr_ref, vr_ref = refs[na:na + 6]
        bias_ref, og_ref, lg_ref, otok_ref, ltok_ref, sc_ref = refs[na + 6 + no:na + 12 + no]
        s_refs = (refs[:na], refs[na + 6:na + 6 + no], refs[na + 12 + no:])
        p, g = pl.program_id(0), pl.program_id(1)
        _side_steps(side, s_refs, first=(p == 0) & (g == 0), mid=(p == 2) & (g == 0))
        masks = _head_masks()

        @pl.when(g == 0)
        def _():
            _window_bias(bias_ref)

        def group(gi):
            dil = ATT_GROUPS[gi][1]
            nblk = (T // dil) // BLK
            _rope_and_regroup(dil, q_ref, k_ref, v_ref, (cos_ref, sa_ref, sb_ref), lg_ref.at[0], lg_ref.at[1],
                              qr_ref, kr_ref, vr_ref)

            def scores(bi, slot):
                cur, prev = _blocks(bi)
                q2 = _stack_heads(qr_ref[cur, :], masks)
                sc_ref[slot, 0] = _dot(q2, kr_ref[cur, :], _NT) + bias_ref[0]
                if nblk > 1:
                    sc_ref[slot, 1] = (_dot(q2, kr_ref[prev, :], _NT)
                                       + (bias_ref[1] + jnp.where((bi % nblk) != 0, 0.0, -jnp.inf)))

            def finish(bi, slot):
                cur, prev = _blocks(bi)
                s_c, vc = sc_ref[slot, 0], vr_ref[cur, :]
                if nblk > 1:
                    s_p, vp = sc_ref[slot, 1], vr_ref[prev, :]
                    mx = jnp.max(jnp.maximum(s_c, s_p), axis=1, keepdims=True)
                    p_c, p_p = jnp.exp(s_c - mx), jnp.exp(s_p - mx)
                    den = jnp.sum(p_c + p_p, axis=1, keepdims=True)
                    oh = _dot(p_c.astype(BF), vc) + _dot(p_p.astype(BF), vp)
                else:
                    mx = jnp.max(s_c, axis=1, keepdims=True)
                    p_c = jnp.exp(s_c - mx)
                    den = jnp.sum(p_c, axis=1, keepdims=True)
                    oh = _dot(p_c.astype(BF), vc)
                on = oh / den
                lsev = jnp.broadcast_to(mx + jnp.log(den), (2 * BLK, BLK))
                og_ref[cur, :] = on[:BLK] * masks[0] + on[BLK:] * masks[1]
                lg_ref[0, cur, :] = lsev[:BLK]
                lg_ref[1, cur, :] = lsev[BLK:]

            def pair(j, carry):
                finish(2 * j, 0)
                scores(2 * j + 1, 1)
                finish(2 * j + 1, 1)
                scores(jnp.minimum(2 * j + 2, NBLK - 1), 0)
                return carry

            scores(0, 0)
            lax.fori_loop(0, NBLK // 2, pair, 0)
            for rows, src in _pieces(dil):
                srows = pl.ds(src, BLK)
                otok_ref[gi, rows, :] = og_ref[srows, :]
                ltok_ref[gi, 0, rows, :] = lg_ref[0, srows, :]
                ltok_ref[gi, 1, rows, :] = lg_ref[1, srows, :]

        for gi in range(3):
            pl.when(g == gi)(functools.partial(group, gi))

        @pl.when(g == 2)
        def _():
            for c in range(T // BLK):
                rows = pl.ds(BLK * c, BLK)
                wts = []
                for hh in range(2):
                    l0, l1, l2 = ltok_ref[0, hh, rows, :], ltok_ref[1, hh, rows, :], ltok_ref[2, hh, rows, :]
                    mx = jnp.maximum(jnp.maximum(l0, l1), l2)
                    e0, e1, e2 = jnp.exp(l0 - mx), jnp.exp(l1 - mx), jnp.exp(l2 - mx)
                    tot = e0 + e1 + e2
                    lse_ref[rows, BLK * hh:BLK * (hh + 1)] = mx + jnp.log(tot)
                    inv = 1.0 / tot
                    wts.append([e0 * inv, e1 * inv, e2 * inv])
                o = sum((wts[0][gi] * masks[0] + wts[1][gi] * masks[1]) * otok_ref[gi, rows, :] for gi in range(3))
                ag = ag_ref[rows, :]
                opre_ref[rows, :] = o
                ob_ref[rows, :] = (o * (ag * _sigmoid(ag))).astype(BF)

        _side_steps(side, s_refs, last=(p == 3) & (g == 2))

    c0 = ATT_COL0 // BLK
    zspec = lambda part: pl.BlockSpec((T, BLK), lambda p, g, part=part: (0, c0 + 12 * part + 4 * g + p))
    outspec = pl.BlockSpec((T, BLK), lambda p, g: (0, p))
    table = pl.BlockSpec((T, BLK), lambda p, g: (0, 0))
    regrouped = pl.BlockSpec((None, T, BLK), lambda p, g: (g, 0, p))
    big = lambda: pltpu.VMEM((T, BLK), F32)
    out = pl.pallas_call(
        body, name="attn_fwd", grid=(4, 3),
        in_specs=[zspec(0), zspec(1), zspec(2),
                  pl.BlockSpec((T, BLK), lambda p, g: (0, AG_COL0 // BLK + p)), table, table, table] + s_in_specs,
        out_specs=[outspec, outspec, pl.BlockSpec((T, 2 * BLK), lambda p, g: (0, p)), regrouped, regrouped, regrouped]
                  + s_out_specs,
        out_shape=[jax.ShapeDtypeStruct((T, 512), BF), jax.ShapeDtypeStruct((T, 512), F32),
                   jax.ShapeDtypeStruct((T, 8 * BLK), F32)] + [jax.ShapeDtypeStruct((3, T, 512), BF)] * 3 + s_shapes,
        scratch_shapes=[pltpu.VMEM((2, 2 * BLK, BLK), F32), big(),
                        pltpu.VMEM((2, T, BLK), F32), pltpu.VMEM((3, T, BLK), F32), pltpu.VMEM((3, 2, T, BLK), F32),
                        pltpu.VMEM((2, 2, 2 * BLK, BLK), F32)] + s_sems,
        compiler_params=_params(("parallel" if side is None else "arbitrary", "arbitrary")),
    )(z, z, z, z, cos, sa, sb, *s_arrays)
    return (*out[:6], out[6:])


def _attn_bwd(z, qs, ks, vs, cos, sa, sb, opre, lse, dob, side=None):
    s_arrays, s_in_specs, s_shapes, s_out_specs, s_sems = _side_io(side)
    na, no = len(s_arrays), len(s_shapes)

    def body(qs_ref, ks_ref, vs_ref, ag_ref, cos_ref, sa_ref, sb_ref, o_ref, lse0_ref, lse1_ref, dob_ref, *refs):
        dq_ref, dk_ref, dv_ref, dag_ref = refs[na:na + 4]
        (bias_ref, dtok_ref, qr_ref, kr_ref, vr_ref, dor_ref, lr_ref, dr_ref,
         dqr_ref, dkr_ref, dvr_ref, pd_ref, dotok_ref) = refs[na + 4 + no:na + 17 + no]
        s_refs = (refs[:na], refs[na + 4:na + 4 + no], refs[na + 17 + no:])
        p, g = pl.program_id(0), pl.program_id(1)
        _side_steps(side, s_refs, first=(p == 0) & (g == 0), mid=(p == 2) & (g == 0))
        masks = _head_masks()

        @pl.when(g == 0)
        def _():
            _window_bias(bias_ref)
            for c in range(T // BLK):
                rows = pl.ds(BLK * c, BLK)
                ag, dob_v, o = ag_ref[rows, :], dob_ref[rows, :], o_ref[rows, :]
                sg = _sigmoid(ag)
                dag_ref[rows, :] = (dob_v * o * (sg * (1.0 + ag * (1.0 - sg)))).astype(BF)
                do = dob_v * (ag * sg)
                dotok_ref[rows, :] = do
                prod = do * o
                for hh, mh in enumerate(masks):
                    dtok_ref[hh, rows, :] = jnp.broadcast_to(jnp.sum(prod * mh, axis=1, keepdims=True), (BLK, BLK))

        def group(gi):
            dil = ATT_GROUPS[gi][1]
            nblk = (T // dil) // BLK
            for rows, dst in _pieces(dil):
                drows = pl.ds(dst, BLK)
                dor_ref[drows, :] = dotok_ref[rows, :]
                for hh, lse_ref in enumerate((lse0_ref, lse1_ref)):
                    lr_ref[hh, drows, :] = lse_ref[rows, :]
                    dr_ref[hh, drows, :] = dtok_ref[hh, rows, :]
            dkr_ref[...] = jnp.zeros_like(dkr_ref)
            dvr_ref[...] = jnp.zeros_like(dvr_ref)

            def probs(bi, slot):
                cur, prev = _blocks(bi)
                q2, do2 = _stack_heads(qs_ref[cur, :], masks), _stack_heads(dor_ref[cur, :], masks)
                lh = jnp.concatenate([lr_ref[0, cur, :], lr_ref[1, cur, :]], axis=0)
                dh = jnp.concatenate([dr_ref[0, cur, :], dr_ref[1, cur, :]], axis=0)
                p_c = jnp.exp(_dot(q2, ks_ref[cur, :], _NT) + bias_ref[0] - lh)
                pd_ref[slot, 0] = p_c.astype(BF)
                pd_ref[slot, 1] = (p_c * (_dot(do2, vs_ref[cur, :], _NT) - dh)).astype(BF)
                if nblk > 1:
                    bias_p = bias_ref[1] + jnp.where((bi % nblk) != 0, 0.0, -jnp.inf)
                    p_p = jnp.exp(_dot(q2, ks_ref[prev, :], _NT) + bias_p - lh)
                    pd_ref[slot, 2] = p_p.astype(BF)
                    pd_ref[slot, 3] = (p_p * (_dot(do2, vs_ref[prev, :], _NT) - dh)).astype(BF)

            def grads(bi, slot):
                cur, prev = _blocks(bi)
                q2, do2 = _stack_heads(qs_ref[cur, :], masks), _stack_heads(dor_ref[cur, :], masks)
                p_c, ds_c = pd_ref[slot, 0], pd_ref[slot, 1]
                dq2 = _dot(ds_c, ks_ref[cur, :])
                dkr_ref[cur, :] += _dot(ds_c, q2, _TN)
                dvr_ref[cur, :] += _dot(p_c, do2, _TN)
                if nblk > 1:
                    p_p, ds_p = pd_ref[slot, 2], pd_ref[slot, 3]
                    dq2 = dq2 + _dot(ds_p, ks_ref[prev, :])
                    dkr_ref[prev, :] += _dot(ds_p, q2, _TN)
                    dvr_ref[prev, :] += _dot(p_p, do2, _TN)
                dqr_ref[cur, :] = dq2[:BLK] * masks[0] + dq2[BLK:] * masks[1]

            def pair(j, carry):
                grads(2 * j, 0)
                probs(2 * j + 1, 1)
                grads(2 * j + 1, 1)
                probs(jnp.minimum(2 * j + 2, NBLK - 1), 0)
                return carry

            probs(0, 0)
            lax.fori_loop(0, NBLK // 2, pair, 0)
            if dil > 1:
                for rows, src in _pieces(dil):
                    srows = pl.ds(src, BLK)
                    qr_ref[rows, :] = dqr_ref[srows, :]
                    kr_ref[rows, :] = dkr_ref[srows, :]
                    vr_ref[rows, :] = dvr_ref[srows, :]
            tq, tk, tv = (qr_ref, kr_ref, vr_ref) if dil > 1 else (dqr_ref, dkr_ref, dvr_ref)
            for c in range(T // BLK):
                rows = pl.ds(BLK * c, BLK)
                cs, sa, sb = cos_ref[rows, :], sa_ref[rows, :], sb_ref[rows, :]
                dq_ref[rows, :] = _rope_t(tq[rows, :] * QK_SCALE, cs, sa, sb).astype(BF)
                dk_ref[rows, :] = _rope_t(tk[rows, :], cs, sa, sb).astype(BF)
                dv_ref[rows, :] = tv[rows, :].astype(BF)

        for gi in range(3):
            pl.when(g == gi)(functools.partial(group, gi))
        _side_steps(side, s_refs, last=(p == 3) & (g == 2))

    regrouped = pl.BlockSpec((None, T, BLK), lambda p, g: (g, 0, p))
    pspec = pl.BlockSpec((T, BLK), lambda p, g: (0, p))
    gspec = pl.BlockSpec((T, BLK), lambda p, g: (0, 4 * g + p))
    table = pl.BlockSpec((T, BLK), lambda p, g: (0, 0))
    big = lambda: pltpu.VMEM((T, BLK), F32)
    two = lambda: pltpu.VMEM((2, T, BLK), F32)
    out = pl.pallas_call(
        body, name="attn_bwd", grid=(4, 3),
        in_specs=[regrouped, regrouped, regrouped,
                  pl.BlockSpec((T, BLK), lambda p, g: (0, AG_COL0 // BLK + p)), table, table, table,
                  pspec, pl.BlockSpec((T, BLK), lambda p, g: (0, 2 * p)),
                  pl.BlockSpec((T, BLK), lambda p, g: (0, 2 * p + 1)), pspec] + s_in_specs,
        out_specs=[gspec, gspec, gspec, pspec] + s_out_specs,
        out_shape=[jax.ShapeDtypeStruct((T, 1536), BF), jax.ShapeDtypeStruct((T, 1536), BF),
                   jax.ShapeDtypeStruct((T, 1536), BF), jax.ShapeDtypeStruct((T, 512), BF)] + s_shapes,
        scratch_shapes=[pltpu.VMEM((2, 2 * BLK, BLK), F32), two(), big(), big(), big(), big(),
                        two(), two(), big(), big(), big(), pltpu.VMEM((2, 4, 2 * BLK, BLK), BF), big()] + s_sems,
        compiler_params=_params(("parallel" if side is None else "arbitrary", "arbitrary")),
    )(qs, ks, vs, z, cos, sa, sb, opre, lse, lse, dob, *s_arrays)
    return (*out[:4], out[4:])


def _merge_out_loss(og, ob, z, w_a, w_b, w_out, x, tgt, wf):
    tm = 512

    def body(og_ref, ob_ref, ga_ref, gb_ref, wa_ref, wb_ref, wo_ref, x_ref, t_ref, wf_ref,
             m_ref, dout_ref, loss_ref, gwf_ref):
        @pl.when(pl.program_id(0) == 0)
        def _():
            loss_ref[...] = jnp.zeros_like(loss_ref)
            gwf_ref[...] = jnp.zeros_like(gwf_ref)

        ya, yb = _dot(og_ref[...], wa_ref[...]), _dot(ob_ref[...], wb_ref[...])
        m =(_sigmoid(ga_ref[...]) * ya + _sigmoid(gb_ref[...]) * yb).astype(BF)
        m_ref[...] = m
        out = x_ref[...] + _dot(m, wo_ref[...])
        r = lax.rsqrt(jnp.mean(out * out, axis=-1, keepdims=True) + EPS)
        yh = out * r
        wfv = wf_ref[...]
        err = yh * wfv - t_ref[...]
        loss_ref[...] += jnp.sum(err * err, axis=0, keepdims=True) * (0.5 / D)
        dy = err * (1.0 / D)
        gwf_ref[...] += jnp.sum(dy * yh, axis=0, keepdims=True)
        dyh = dy * wfv
        dout_ref[...] = r * (dyh - yh * jnp.mean(dyh * yh, axis=-1, keepdims=True))

    row = pl.BlockSpec((tm, D), lambda i: (i, 0))
    vec = pl.BlockSpec((1, D), lambda i: (0, 0))
    whole = lambda w: pl.BlockSpec(w.shape, lambda i: (0, 0))
    return pl.pallas_call(
        body, name="merge_out_loss", grid=(T // tm,),
        in_specs=[row, pl.BlockSpec((tm, ob.shape[1]), lambda i: (i, 0)),
                  pl.BlockSpec((tm, D), lambda i: (i, GATE_COL0 // D)),
                  pl.BlockSpec((tm, D), lambda i: (i, GATE_COL0 // D + 1)),
                  whole(w_a), whole(w_b), whole(w_out), row, row, vec],
        out_specs=[row, row, vec, vec],
        out_shape=[jax.ShapeDtypeStruct((T, D), BF), jax.ShapeDtypeStruct((T, D), F32),
                   jax.ShapeDtypeStruct((1, D), F32), jax.ShapeDtypeStruct((1, D), F32)],
        compiler_params=_params(("arbitrary",)),
    )(og, ob, z, z, w_a, w_b, w_out, x, tgt, wf)


def _merge_proj_bwd(dout, og, ob, z, w_a, w_b, w_out):
    tm = 512

    def body(dout_ref, og_ref, ob_ref, ga_ref, gb_ref, wa_ref, wb_ref, wo_ref,
             dya_ref, dyb_ref, dg_ref, dog_ref, dob_ref):
        dmv = _dot(dout_ref[...].astype(BF), wo_ref[...], _NT)
        sa, sb = _sigmoid(ga_ref[...]), _sigmoid(gb_ref[...])
        dya, dyb = (sa * dmv).astype(BF), (sb * dmv).astype(BF)
        dya_ref[...] = dya
        dyb_ref[...] = dyb
        dg_ref[:, :D] = (dmv * _dot(og_ref[...], wa_ref[...]) * sa * (1.0 - sa)).astype(BF)
        dg_ref[:, D:] = (dmv * _dot(ob_ref[...], wb_ref[...]) * sb * (1.0 - sb)).astype(BF)
        dog_ref[...] = _dot(dya, wa_ref[...], _NT)
        dob_ref[...] = _dot(dyb, wb_ref[...], _NT)

    row = pl.BlockSpec((tm, D), lambda i: (i, 0))
    whole = lambda w: pl.BlockSpec(w.shape, lambda i: (0, 0))
    nb = w_b.shape[0]
    return pl.pallas_call(
        body, name="merge_proj_bwd", grid=(T // tm,),
        in_specs=[row, row, pl.BlockSpec((tm, nb), lambda i: (i, 0)),
                  pl.BlockSpec((tm, D), lambda i: (i, GATE_COL0 // D)),
                  pl.BlockSpec((tm, D), lambda i: (i, GATE_COL0 // D + 1)), whole(w_a), whole(w_b), whole(w_out)],
        out_specs=[row, row, pl.BlockSpec((tm, 2 * D), lambda i: (i, 0)), row,
                   pl.BlockSpec((tm, nb), lambda i: (i, 0))],
        out_shape=[jax.ShapeDtypeStruct((T, D), BF), jax.ShapeDtypeStruct((T, D), BF),
                   jax.ShapeDtypeStruct((T, 2 * D), BF), jax.ShapeDtypeStruct((T, D), F32),
                   jax.ShapeDtypeStruct((T, nb), F32)],
        compiler_params=_params(("parallel",)),
    )(dout, og, ob, z, z, w_a, w_b, w_out)


def _rope_inv_freq():
    inv = ROPE_THETA ** (-jnp.arange(0, 64, 2, dtype=F32) / 64)
    return jnp.tile(inv, 4).reshape(1, BLK)


def _local_step(x, pos, norm_w, lbl, hnw, wf, tgt, w_in, w_a, w_b, w_out, shard_shapes=()):
    invf = _rope_inv_freq()
    if shard_shapes:
        blk = jnp.reshape(2 * lax.axis_index("x") + lax.axis_index("y"), (1,)).astype(jnp.int32)
        h, cos, sa, sb, z_own, (w_near,) = _norm_and_rope_tables(
            x, norm_w, pos, invf, side=_gather_near_side(w_in, WEIGHT_AXES[0]), own=(w_in, blk))
        near = jnp.concatenate([blk ^ 2, blk ^ 1])
        z, (w_diag,) = _z_blocks(h, w_near, z_own, jnp.concatenate([near, near]), 2, name="z_proj_near",
                                 side=_gather_diag_side(w_near, w_in.shape, WEIGHT_AXES[0]))
        z, w_in, _ = _z_blocks(h, w_diag, z, jnp.concatenate([blk ^ 3, jnp.zeros_like(blk)]), 1, name="z_proj_diag",
                               fill=w_near, side=None)
    else:
        h, cos, sa, sb, _, _ = _norm_and_rope_tables(x, norm_w, pos, invf)
        z = _matmul(h, w_in, tm=T, tn=512, name="z_proj")
    oraw, og, shist, _ = _hgrn_fwd(z, lbl, hnw)
    side_a = _gather_side([w_a, w_b, w_out], WEIGHT_AXES[1:]) if shard_shapes else None
    ob, opre, lse, qs, ks, vs, gathered = _attn_fwd(z, cos, sa, sb, side=side_a)
    if shard_shapes:
        w_a, w_b, w_out = gathered
    merged, dout, loss_vec, g_wf = _merge_out_loss(og, ob, z, w_a, w_b, w_out, x, tgt, wf)

    dya, dyb, dgates, dog, dob = _merge_proj_bwd(dout, og, ob, z, w_a, w_b, w_out)
    g_wout = _matmul(merged, dout, ta=True, out_dtype=BF, tm=512, tn=1024, name="g_wout")
    g_wa = _matmul(og, dya, ta=True, out_dtype=BF, tm=512, tn=1024, name="g_wa")
    g_wb = _matmul(ob, dyb, ta=True, out_dtype=BF, tm=512, tn=1024, name="g_wb")
    small = [g_wa, g_wb, g_wout]
    side_s = side_w = None
    if shard_shapes:
        p3_s = _rs_partials(small, shard_shapes[1:], WEIGHT_AXES[1:], "small")
        side_s = _chip_exchange_side(p3_s, shard_shapes[1:], WEIGHT_AXES[1:])
    dz_h, dlb, g_hnw, _ = _hgrn_bwd(z, lbl, hnw, oraw, dog, shist)
    dq, dk, dv, dag, land_s = _attn_bwd(z, qs, ks, vs, cos, sa, sb, opre, lse, dob, side=side_s)
    dz_parts = [dz_h, dq, dk, dv, dag, dgates]
    if shard_shapes:
        c = lax.axis_index("c")
        half = lambda i: jnp.reshape(i, (1,)).astype(jnp.int32)
        g_send = _grad_w_in_half(h, dz_parts, half(1 - c))
        g_keep, (g_sib,) = _grad_w_in_half(h, dz_parts, half(c), side=_sibling_send_side(g_send))
        p3_w = [_add_bf16(g_keep, g_sib, "pair_sum_w_in").reshape(1, D // 2, NIN)]
        side_w = _chip_exchange_relay_side(p3_w[0], shard_shapes[0])
    else:
        g_big = [_grad_w_in(h, dz_parts)] + small
    gx, g_nw, land_w = _grad_x(dz_parts, w_in, x, dout, norm_w, side=side_w)
    small_sums = None
    if shard_shapes:
        g_big, small_sums = _rs_finish(p3_w + p3_s, [land_w[0]] + list(land_s), shard_shapes, WEIGHT_AXES,
                                       (g_nw, dlb, g_hnw, g_wf, loss_vec))
    return dict(loss_vec=loss_vec, gx=gx, g_nw=g_nw, dlb=dlb, g_hnw=g_hnw, g_wf=g_wf, small_sums=small_sums,
                g_win=g_big[0], g_wa=g_big[1], g_wb=g_big[2], g_wout=g_big[3])


MESH = pl.DeviceIdType.MESH
HBM = pl.BlockSpec(memory_space=pl.ANY)
WEIGHT_AXES = (1, 0, 1, 0)


def _place():
    x, y, c = lax.axis_index("x"), lax.axis_index("y"), lax.axis_index("c")
    chips = [(1 - x, y), (x, 1 - y), (1 - x, 1 - y)]
    return x, y, c, chips


def _block_half(ref, shard_shape, axis, j, half):
    r, c = shard_shape
    hr = r // 2
    if axis == 0:
        return ref.at[pl.ds(pl.multiple_of(j * r + half * hr, 16), hr), :]
    return ref.at[pl.ds(pl.multiple_of(half * hr, 16), hr), pl.ds(pl.multiple_of(j * c, 128), c)]


class _Side:
    def __init__(self, arrays, out_shapes, sems, first, last, mid=None):
        self.arrays, self.out_shapes, self.sems, self.first, self.last = arrays, out_shapes, sems, first, last
        self.mid = mid


def _gather_side(shards, axes):
    n = len(shards)
    shapes = [s.shape for s in shards]

    def copies(ins, outs, sems):
        send1, recv1, send2, recv2, send0, recv0 = sems
        x, y, c, chips = _place()
        me = 2 * x + y
        sib = (x, y, 1 - c)
        near = ((1 - c) * (1 - x) + c * x, (1 - c) * y + c * (1 - y))
        far = ((1 - c) * x + c * (1 - x), (1 - c) * (1 - y) + c * y)
        out = []
        for a in range(n):
            r, cc = shapes[a]
            mine = (outs[a].at[pl.ds(pl.multiple_of(me * r, 16), r), :] if axes[a] == 0
                    else outs[a].at[:, pl.ds(pl.multiple_of(me * cc, 128), cc)])
            own = pltpu.make_async_remote_copy(
                src_ref=ins[a], dst_ref=mine, send_sem=send0.at[a], recv_sem=recv0.at[a],
                device_id=sib, device_id_type=MESH)
            src = ins[a].at[pl.ds(pl.multiple_of(c * (r // 2), 16), r // 2), :]
            sends = [pltpu.make_async_remote_copy(
                src_ref=src, dst_ref=_block_half(outs[a], shapes[a], axes[a], me, c),
                send_sem=send1.at[a, k], recv_sem=recv1.at[a, k], device_id=(*chips[k], c), device_id_type=MESH)
                for k in range(2)]

            def region(chip, half):
                return _block_half(outs[a], shapes[a], axes[a], 2 * chip[0] + chip[1], half)

            def arrival(chip, k):
                reg = region(chip, c)
                return pltpu.make_async_remote_copy(
                    src_ref=reg, dst_ref=reg, send_sem=send1.at[a, k], recv_sem=recv1.at[a, k],
                    device_id=(*chip, c), device_id_type=MESH)

            def to_sibling(chip, k):
                reg = region(chip, c)
                return pltpu.make_async_remote_copy(
                    src_ref=reg, dst_ref=reg, send_sem=send2.at[a, k], recv_sem=recv2.at[a, k],
                    device_id=sib, device_id_type=MESH)

            def from_sibling(chip, k):
                reg = region(chip, 1 - c)
                return pltpu.make_async_remote_copy(
                    src_ref=reg, dst_ref=reg, send_sem=send2.at[a, k], recv_sem=recv2.at[a, k],
                    device_id=sib, device_id_type=MESH)

            relay = pltpu.make_async_remote_copy(
                src_ref=region(near, c), dst_ref=region(near, c), send_sem=send1.at[a, 2], recv_sem=recv1.at[a, 2],
                device_id=(*far, c), device_id_type=MESH)
            hops = [(arrival(near, c), to_sibling(near, c)), (arrival(far, 1 - c), to_sibling(far, 1 - c)),
                    (arrival(chips[2], 2), to_sibling(chips[2], 2))]
            back = [from_sibling(chips[k], k) for k in range(3)]
            out.append((own, sends, relay, hops, back))
        return out

    def first(ins, outs, sems):
        for own, sends, _, _, _ in copies(ins, outs, sems):
            own.start()
            for cp in sends:
                cp.start()

    def mid(ins, outs, sems):
        per_array = copies(ins, outs, sems)
        for step in range(2):
            for _, _, relay, hops, _ in per_array:
                arrived, onward = hops[step]
                arrived.wait_recv()
                if step == 0:
                    relay.start()
                onward.start()

    def last(ins, outs, sems):
        per_array = copies(ins, outs, sems)
        for _, _, _, hops, _ in per_array:
            arrived, onward = hops[2]
            arrived.wait_recv()
            onward.start()
        for own, sends, relay, hops, back in per_array:
            for cp in back:
                cp.wait_recv()
            for cp in sends + [relay] + [onward for _, onward in hops]:
                cp.wait_send()
            own.wait()

    full = [(4 * r, c) if ax == 0 else (r, 4 * c) for (r, c), ax in zip(shapes, axes)]
    sems = [pltpu.SemaphoreType.DMA((n, 3)), pltpu.SemaphoreType.DMA((n, 3)),
            pltpu.SemaphoreType.DMA((n, 3)), pltpu.SemaphoreType.DMA((n, 3)),
            pltpu.SemaphoreType.DMA((n,)), pltpu.SemaphoreType.DMA((n,))]
    return _Side(list(shards), [jax.ShapeDtypeStruct(f, BF) for f in full], sems, first, last, mid)


def _gather_near_side(shard, axis):
    shape = shard.shape
    r, cc = shape

    def copies(ins, outs, sems):
        send1, recv1, send2, recv2, send0, recv0 = sems
        x, y, c, chips = _place()
        me = 2 * x + y
        sib = (x, y, 1 - c)
        mine = (outs[0].at[pl.ds(pl.multiple_of(me * r, 16), r), :] if axis == 0
                else outs[0].at[:, pl.ds(pl.multiple_of(me * cc, 128), cc)])
        own = pltpu.make_async_remote_copy(
            src_ref=ins[0], dst_ref=mine, send_sem=send0.at[0], recv_sem=recv0.at[0],
            device_id=sib, device_id_type=MESH)
        src = ins[0].at[pl.ds(pl.multiple_of(c * (r // 2), 16), r // 2), :]

        def region(k, half):
            return _block_half(outs[0], shape, axis, 2 * chips[k][0] + chips[k][1], half)

        def moves(k):
            return [pltpu.make_async_remote_copy(
                        src_ref=s, dst_ref=d, send_sem=ss.at[k], recv_sem=rs.at[k], device_id=dev,
                        device_id_type=MESH)
                    for s, d, ss, rs, dev in (
                        (src, _block_half(outs[0], shape, axis, me, c), send1, recv1, (*chips[k], c)),
                        (region(k, c), region(k, c), send1, recv1, (*chips[k], c)),
                        (region(k, c), region(k, c), send2, recv2, sib),
                        (region(k, 1 - c), region(k, 1 - c), send2, recv2, sib))]

        return own, [moves(k) for k in range(2)]

    def first(ins, outs, sems):
        own, per_chip = copies(ins, outs, sems)
        own.start()
        for send, _, _, _ in per_chip:
            send.start()

    def last(ins, outs, sems):
        own, per_chip = copies(ins, outs, sems)
        for _, arrived, onward, _ in per_chip:
            arrived.wait_recv()
            onward.start()
        for send, _, onward, back in per_chip:
            back.wait_recv()
            send.wait_send()
            onward.wait_send()
        own.wait()

    full = (4 * r, cc) if axis == 0 else (r, 4 * cc)
    sems = [pltpu.SemaphoreType.DMA((2,))] * 4 + [pltpu.SemaphoreType.DMA((1,))] * 2
    return _Side([shard], [jax.ShapeDtypeStruct(full, BF)], sems, first, last)


def _gather_diag_side(gathered, shape, axis):
    r, cc = shape

    def copies(ins, outs, sems):
        send1, recv1, send2, recv2 = sems
        x, y, c, _ = _place()
        sib = (x, y, 1 - c)
        near = ((1 - c) * (1 - x) + c * x, (1 - c) * y + c * (1 - y))
        far = ((1 - c) * x + c * (1 - x), (1 - c) * (1 - y) + c * y)

        def half(i):
            return outs[0].at[pl.ds(pl.multiple_of(i * (r // 2), 16), r // 2), :]

        def move(s, d, ss, rs, dev):
            return pltpu.make_async_remote_copy(
                src_ref=s, dst_ref=d, send_sem=ss.at[0], recv_sem=rs.at[0], device_id=dev, device_id_type=MESH)

        relay = move(_block_half(ins[0], shape, axis, 2 * near[0] + near[1], c), half(c), send1, recv1, (*far, c))
        arrived = move(half(c), half(c), send1, recv1, (*far, c))
        onward = move(half(c), half(c), send2, recv2, sib)
        back = move(half(1 - c), half(1 - c), send2, recv2, sib)
        return relay, arrived, onward, back

    def first(ins, outs, sems):
        copies(ins, outs, sems)[0].start()

    def last(ins, outs, sems):
        relay, arrived, onward, back = copies(ins, outs, sems)
        arrived.wait_recv()
        onward.start()
        back.wait_recv()
        relay.wait_send()
        onward.wait_send()

    return _Side([gathered], [jax.ShapeDtypeStruct(shape, BF)], [pltpu.SemaphoreType.DMA((1,))] * 4, first, last)


def _as3d(g, shard_shape, axis):
    r, c = shard_shape
    return g.reshape(4, r, c) if axis == 0 else g.reshape(1, r, 4 * c)


def _half_rows(ref3, hr, half):
    return ref3.at[:, pl.ds(pl.multiple_of(half * hr, 16), hr), :]


def _rs_pair_exchange(g3s, name):
    n = len(g3s)

    def body(*refs):
        ins, outs = refs[:n], refs[n:2 * n]
        send, recv = refs[2 * n:]
        x, y, c, _ = _place()
        cps = []
        for a in range(n):
            hr = g3s[a].shape[1] // 2
            cp = pltpu.make_async_remote_copy(
                src_ref=_half_rows(ins[a], hr, 1 - c), dst_ref=outs[a],
                send_sem=send.at[a], recv_sem=recv.at[a], device_id=(x, y, 1 - c), device_id_type=MESH)
            cp.start()
            cps.append(cp)
        for cp in cps:
            cp.wait()

    return pl.pallas_call(
        body, name=name,
        in_specs=[HBM] * n, out_specs=[HBM] * n,
        out_shape=[jax.ShapeDtypeStruct((g.shape[0], g.shape[1] // 2, g.shape[2]), BF) for g in g3s],
        scratch_shapes=[pltpu.SemaphoreType.DMA((n,)), pltpu.SemaphoreType.DMA((n,))],
    )(*g3s)


def _pair_sums(g3s, lands, cidx, name):
    n = len(g3s)

    def body(c_ref, *refs):
        for g_ref, l_ref, o_ref in zip(refs[:n], refs[n:2 * n], refs[2 * n:]):
            o_ref[...] = (g_ref[...].astype(F32) + l_ref[...].astype(F32)).astype(BF)

    halves = [(g.shape[0], g.shape[1] // 2, g.shape[2]) for g in g3s]
    return pl.pallas_call(
        body, name=name,
        grid_spec=pltpu.PrefetchScalarGridSpec(
            num_scalar_prefetch=1, grid=(1,),
            in_specs=[pl.BlockSpec(h, lambda i, c: (0, c[0], 0)) for h in halves]
                     + [pl.BlockSpec(h, lambda i, c: (0, 0, 0)) for h in halves],
            out_specs=[pl.BlockSpec(h, lambda i, c: (0, 0, 0)) for h in halves]),
        out_shape=[jax.ShapeDtypeStruct(h, BF) for h in halves],
        compiler_params=_params(("arbitrary",)),
    )(cidx, *g3s, *lands)


def _chip_exchange_side(p3s, shapes, axes):
    n = len(p3s)

    def copies(ins, outs, sems):
        send, recv = sems
        x, y, c, chips = _place()
        cps = []
        for a in range(n):
            r, cc = shapes[a]
            for k, (px, py) in enumerate(chips):
                j = 2 * px + py
                src = ins[a].at[j] if axes[a] == 0 else ins[a].at[0, :, pl.ds(pl.multiple_of(j * cc, 128), cc)]
                cps.append(pltpu.make_async_remote_copy(
                    src_ref=src, dst_ref=outs[a].at[k], send_sem=send.at[a, k], recv_sem=recv.at[a, k],
                    device_id=(px, py, c), device_id_type=MESH))
        return cps

    def first(ins, outs, sems):
        for cp in copies(ins, outs, sems):
            cp.start()

    def last(ins, outs, sems):
        for cp in copies(ins, outs, sems):
            cp.wait()

    return _Side(list(p3s), [jax.ShapeDtypeStruct((3, r // 2, c), BF) for r, c in shapes],
                 [pltpu.SemaphoreType.DMA((n, 3)), pltpu.SemaphoreType.DMA((n, 3))], first, last)


def _chip_exchange_relay_side(p3, shape):
    r, cc = shape
    hr = r // 2
    rows = 64

    def copies(ins, outs, sems):
        send, recv, local, mine, theirs = sems
        x, y, c, chips = _place()
        near = ((1 - c) * (1 - x) + c * x, (1 - c) * y + c * (1 - y))
        far = ((1 - c) * x + c * (1 - x), (1 - c) * (1 - y) + c * y)
        land, staged = outs

        def block(chip):
            return ins[0].at[0, :, pl.ds(pl.multiple_of((2 * chip[0] + chip[1]) * cc, 128), cc)]

        def move(s, d, k, dev):
            return pltpu.make_async_remote_copy(
                src_ref=s, dst_ref=d, send_sem=send.at[k], recv_sem=recv.at[k], device_id=dev, device_id_type=MESH)

        return dict(
            direct=move(block(near), land.at[c], 0, (*near, c)),
            for_relay=move(block(chips[2]), staged, 1, (*near, c)),
            summed=move(mine, land.at[1 - c], 2, (*far, c)),
            direct_in=move(land.at[c], land.at[c], 0, (*near, c)),
            staged_in=move(staged, staged, 1, (*near, c)),
            summed_in=move(land.at[1 - c], land.at[1 - c], 2, (*far, c)),
            load_mine=pltpu.make_async_copy(block(far), mine, local.at[0]),
            load_theirs=pltpu.make_async_copy(staged, theirs, local.at[1]))

    def first(ins, outs, sems):
        cps = copies(ins, outs, sems)
        cps["for_relay"].start()
        cps["direct"].start()

    def mid(ins, outs, sems):
        cps = copies(ins, outs, sems)
        mine, theirs = sems[3], sems[4]
        cps["load_mine"].start()
        cps["staged_in"].wait_recv()
        cps["load_theirs"].start()
        cps["load_mine"].wait()
        cps["load_theirs"].wait()

        def add(i, carry):
            rs = pl.ds(pl.multiple_of(i * rows, 16), rows)
            mine[rs, :] = (mine[rs, :].astype(F32) + theirs[rs, :].astype(F32)).astype(BF)
            return carry

        lax.fori_loop(0, hr // rows, add, 0)
        cps["summed"].start()

    def last(ins, outs, sems):
        cps = copies(ins, outs, sems)
        cps["direct_in"].wait_recv()
        cps["summed_in"].wait_recv()
        for name in ("direct", "for_relay", "summed"):
            cps[name].wait_send()

    sems = [pltpu.SemaphoreType.DMA((3,)), pltpu.SemaphoreType.DMA((3,)), pltpu.SemaphoreType.DMA((2,)),
            pltpu.VMEM((hr, cc), BF), pltpu.VMEM((hr, cc), BF)]
    return _Side([p3], [jax.ShapeDtypeStruct((2, hr, cc), BF), jax.ShapeDtypeStruct((hr, cc), BF)], sems,
                 first, last, mid)


def _chip_sum(p3, land, shard_shape, axis, idx, name):
    r, c = shard_shape
    hr = r // 2
    tr = 64
    nt = hr // tr
    slots = land.shape[0]

    def body(idx_ref, p_ref, l_ref, o_ref):
        acc = p_ref[...].astype(F32)
        for k in range(slots):
            acc = acc + l_ref[k].astype(F32)
        o_ref[...] = acc

    own = (pl.BlockSpec((None, tr, c), lambda i, idx: (idx[0], i, 0)) if axis == 0
           else pl.BlockSpec((None, tr, c), lambda i, idx: (0, i, idx[0])))
    return pl.pallas_call(
        body, name=name,
        grid_spec=pltpu.PrefetchScalarGridSpec(
            num_scalar_prefetch=1, grid=(nt,),
            in_specs=[own, pl.BlockSpec((slots, tr, c), lambda i, idx: (0, i, 0))],
            out_specs=pl.BlockSpec((tr, c), lambda i, idx: (idx[1] * nt + i, 0))),
        out_shape=jax.ShapeDtypeStruct((r, c), F32),
        compiler_params=_params(("parallel",)),
    )(idx, p3, land)


def _chip_sums(p3s, lands, shapes, axes, idx, name):
    n = len(p3s)

    def body(idx_ref, *refs):
        for p_ref, l_ref, o_ref in zip(refs[:n], refs[n:2 * n], refs[2 * n:]):
            acc = p_ref[...].astype(F32)
            for k in range(l_ref.shape[0]):
                acc = acc + l_ref[k].astype(F32)
            o_ref[...] = acc

    own = [pl.BlockSpec((None, r // 2, c), (lambda i, idx: (idx[0], 0, 0)) if ax == 0 else (lambda i, idx: (0, 0, idx[0])))
           for (r, c), ax in zip(shapes, axes)]
    return pl.pallas_call(
        body, name=name,
        grid_spec=pltpu.PrefetchScalarGridSpec(
            num_scalar_prefetch=1, grid=(1,),
            in_specs=own + [pl.BlockSpec(l.shape, lambda i, idx: (0, 0, 0)) for l in lands],
            out_specs=[pl.BlockSpec((r // 2, c), lambda i, idx: (idx[1], 0)) for r, c in shapes]),
        out_shape=[jax.ShapeDtypeStruct((r, c), F32) for r, c in shapes],
        compiler_params=_params(("arbitrary",)),
    )(idx, *p3s, *lands)


def _rs_pair_gather(fulls, small):
    n = len(fulls)

    def body(*refs):
        ins, small_refs, outs, red_ref = refs[:n], refs[n:n + 5], refs[n + 5:2 * n + 5], refs[2 * n + 5]
        send, recv = refs[2 * n + 6:2 * n + 8]
        x, y, c, _ = _place()
        cps = []
        for a in range(n):
            hr = fulls[a].shape[0] // 2
            rows = pl.ds(pl.multiple_of(c * hr, 8), hr)
            cp = pltpu.make_async_remote_copy(
                src_ref=ins[a].at[rows, :], dst_ref=outs[a].at[rows, :], send_sem=send.at[a], recv_sem=recv.at[a],
                device_id=(x, y, 1 - c), device_id_type=MESH)
            cp.start()
            cps.append(cp)
        _small_all_reduce(small_refs, red_ref, *refs[2 * n + 8:])
        for a, cp in enumerate(cps):
            cp.wait_send()
            hr = fulls[a].shape[0] // 2
            other = pl.ds(pl.multiple_of((1 - c) * hr, 8), hr)
            pltpu.make_async_remote_copy(
                src_ref=ins[a].at[other, :], dst_ref=outs[a].at[other, :], send_sem=send.at[a], recv_sem=recv.at[a],
                device_id=(x, y, 1 - c), device_id_type=MESH).wait_recv()

    vm = pl.BlockSpec(memory_space=pltpu.VMEM)
    out = pl.pallas_call(
        body, name="grads_pair_gather",
        in_specs=[HBM] * n + [vm] * 5, out_specs=[HBM] * n + [vm],
        out_shape=[jax.ShapeDtypeStruct(f.shape, F32) for f in fulls] + [jax.ShapeDtypeStruct((NSMALL, D), F32)],
        input_output_aliases={a: a for a in range(n)},
        scratch_shapes=[pltpu.SemaphoreType.DMA((n,)), pltpu.SemaphoreType.DMA((n,)),
                        pltpu.VMEM((NSMALL, D), F32), pltpu.VMEM((8, NSMALL, D), F32),
                        pltpu.SemaphoreType.DMA((7,)), pltpu.SemaphoreType.DMA((7,))],
    )(*fulls, *small)
    return out[:n], out[n]


def _sibling_send_side(arr):
    def copy(ins, outs, sems):
        x, y, c, _ = _place()
        return pltpu.make_async_remote_copy(
            src_ref=ins[0], dst_ref=outs[0], send_sem=sems[0].at[0], recv_sem=sems[1].at[0],
            device_id=(x, y, 1 - c), device_id_type=MESH)

    return _Side([arr], [jax.ShapeDtypeStruct(arr.shape, arr.dtype)],
                 [pltpu.SemaphoreType.DMA((1,)), pltpu.SemaphoreType.DMA((1,))],
                 lambda ins, outs, sems: copy(ins, outs, sems).start(),
                 lambda ins, outs, sems: copy(ins, outs, sems).wait())


def _add_bf16(a, b, name):
    r, c = a.shape
    tr = 64

    def body(a_ref, b_ref, o_ref):
        o_ref[...] = (a_ref[...].astype(F32) + b_ref[...].astype(F32)).astype(BF)

    blk = pl.BlockSpec((tr, c), lambda i: (i, 0))
    return pl.pallas_call(
        body, name=name, grid=(r // tr,), in_specs=[blk, blk], out_specs=blk,
        out_shape=jax.ShapeDtypeStruct((r, c), BF), compiler_params=_params(("parallel",)),
    )(a, b)


def _rs_partials(grads, shapes, axes, tag):
    cidx = jnp.reshape(lax.axis_index("c"), (1,)).astype(jnp.int32)
    g3s = [_as3d(g, s, ax) for g, s, ax in zip(grads, shapes, axes)]
    lands = _rs_pair_exchange(g3s, f"grads_pair_exchange_{tag}")
    return list(_pair_sums(g3s, lands, cidx, f"pair_sums_{tag}"))


def _rs_finish(p3s, landed, shapes, axes, small):
    x, y, c = lax.axis_index("x"), lax.axis_index("y"), lax.axis_index("c")
    idx = jnp.stack([2 * x + y, c]).astype(jnp.int32)
    fulls = [_chip_sum(p3s[0], landed[0], shapes[0], axes[0], idx, "chip_sum_w_in")]
    fulls += _chip_sums(p3s[1:], landed[1:], shapes[1:], axes[1:], idx, "chip_sums_branches_out")
    return _rs_pair_gather(fulls, small)


NSMALL = 8


def _small_all_reduce(small_refs, out_ref, pack_ref, buf_ref, send, recv):
    nw_ref, lb_ref, hn_ref, wf_ref, ls_ref = small_refs
    x, y, c = lax.axis_index("x"), lax.axis_index("y"), lax.axis_index("c")
    me = 4 * x + 2 * y + c
    pack_ref[...] = jnp.zeros_like(pack_ref)
    pack_ref[0:1, :] = nw_ref[...]
    pack_ref[1:2, :] = lb_ref[...]
    pack_ref[2:3, 0:HK] = hn_ref[...]
    pack_ref[3:4, :] = wf_ref[...]
    pack_ref[4:5, :] = ls_ref[...]
    buf_ref[me] = pack_ref[...]
    cps = []
    for d in range(1, 8):
        dx, dy, dc = d >> 2, (d >> 1) & 1, d & 1
        peer = (1 - x if dx else x, 1 - y if dy else y, 1 - c if dc else c)
        cp = pltpu.make_async_remote_copy(
            src_ref=pack_ref, dst_ref=buf_ref.at[me], send_sem=send.at[d - 1], recv_sem=recv.at[d - 1],
            device_id=peer, device_id_type=MESH)
        cp.start()
        cps.append(cp)
    for d in range(1, 8):
        dx, dy, dc = d >> 2, (d >> 1) & 1, d & 1
        src = 4 * (1 - x if dx else x) + 2 * (1 - y if dy else y) + (1 - c if dc else c)
        pltpu.make_async_remote_copy(
            src_ref=pack_ref, dst_ref=buf_ref.at[src], send_sem=send.at[d - 1], recv_sem=recv.at[d - 1],
            device_id=(x, y, c), device_id_type=MESH).wait_recv()
    for cp in cps:
        cp.wait_send()
    acc = buf_ref[0]
    for i in range(1, 8):
        acc = acc + buf_ref[i]
    out_ref[...] = acc


def _adamw_math(w, g, m, v):
    m = B1 * m + (1.0 - B1) * g
    v = B2 * v + (1.0 - B2) * (g * g)
    m_hat = m / (1.0 - B1 ** STEP)
    v_hat = v / (1.0 - B2 ** STEP)
    return -LR * (m_hat / (jnp.sqrt(v_hat) + ADAM_EPS) + WD * w), m, v


def _adamw(w, g, m, v, name):
    r, c = w.shape
    tr = 128

    def body(w_ref, g_ref, m_ref, v_ref, d_ref, nm_ref, nv_ref, go_ref):
        g = g_ref[...]
        d_ref[...], nm_ref[...], nv_ref[...] = _adamw_math(w_ref[...], g, m_ref[...], v_ref[...])
        go_ref[...] = g

    blk = pl.BlockSpec((tr, c), lambda i: (i, 0))
    return pl.pallas_call(
        body, name=name, grid=(r // tr,), in_specs=[blk] * 4, out_specs=[blk] * 4,
        out_shape=[jax.ShapeDtypeStruct((r, c), F32)] * 4,
        compiler_params=_params(("parallel",)),
    )(w, g, m, v)


def _adamw_whole(groups, name):
    n = len(groups)

    def body(*refs):
        ins, outs = refs[:4 * n], refs[4 * n:]
        for a in range(n):
            w_ref, g_ref, m_ref, v_ref = ins[4 * a:4 * a + 4]
            g = g_ref[...]
            outs[4 * a][...], outs[4 * a + 1][...], outs[4 * a + 2][...] = _adamw_math(
                w_ref[...], g, m_ref[...], v_ref[...])
            outs[4 * a + 3][...] = g

    vm = pl.BlockSpec(memory_space=pltpu.VMEM)
    out = pl.pallas_call(
        body, name=name, in_specs=[vm] * (4 * n), out_specs=[vm] * (4 * n),
        out_shape=[jax.ShapeDtypeStruct(grp[0].shape, F32) for grp in groups for _ in range(4)],
        compiler_params=_params(),
    )(*[a for grp in groups for a in grp])
    return [out[4 * a:4 * a + 4] for a in range(n)]


def _small_update(red, lbl, params):
    def body(red_ref, *refs):
        ins, outs = refs[:12], refs[12:]
        lb = _lower_bound(ins[3][...])
        dl0 = red_ref[1:2, :] * lb * (1.0 - lb)
        row = lax.broadcasted_iota(jnp.int32, (2, D), 0)
        grads = [red_ref[0:1, :], jnp.where(row == 0, dl0, -dl0), red_ref[2:3, 0:HK], red_ref[3:4, :]]
        for i, g in enumerate(grads):
            w, m, v = ins[3 * i][...], ins[3 * i + 1][...], ins[3 * i + 2][...]
            d, nm, nv = _adamw_math(w, g, m, v)
            outs[4 * i][...] = g
            outs[4 * i + 1][...] = d
            outs[4 * i + 2][...] = nm
            outs[4 * i + 3][...] = nv
        outs[16][...] = jnp.sum(red_ref[4:5, :], axis=1, keepdims=True)

    flat = [a for p in params for a in p]
    vm = pl.BlockSpec(memory_space=pltpu.VMEM)
    shapes = [jax.ShapeDtypeStruct(p[0].shape, F32) for p in params for _ in range(4)]
    return pl.pallas_call(
        body, name="small_update",
        in_specs=[vm] * 13, out_specs=[vm] * 17,
        out_shape=shapes + [jax.ShapeDtypeStruct((1, 1), F32)],
    )(red, *flat)


def kernel(x, positions, norm_w, w_in, lb_logits, hgrn_norm_w, w_branch_a, w_branch_b, w_out, final_norm_w, loss_target, m_norm_w, m_w_in, m_lb_logits, m_hgrn_norm_w, m_w_branch_a, m_w_branch_b, m_w_out, m_final_norm_w, v_norm_w, v_w_in, v_lb_logits, v_hgrn_norm_w, v_w_branch_a, v_w_branch_b, v_w_out, v_final_norm_w):
    big_w = [w_in[0], w_branch_a[0], w_branch_b[0], w_out[0]]
    big_m = [m_w_in[0], m_w_branch_a[0], m_w_branch_b[0], m_w_out[0]]
    big_v = [v_w_in[0], v_w_branch_a[0], v_w_branch_b[0], v_w_out[0]]
    shapes = [w.shape for w in big_w]
    wf = final_norm_w.reshape(1, D)

    shards = [w.astype(BF) for w in big_w]
    loc = _local_step(x[0], positions.reshape(T, 1), norm_w, lb_logits, hgrn_norm_w, wf, loss_target[0],
                      *shards, shard_shapes=shapes)
    g_big = [loc["g_win"], loc["g_wa"], loc["g_wb"], loc["g_wout"]]
    red = loc["small_sums"]

    small = _small_update(red, lb_logits, [
        (norm_w, m_norm_w, v_norm_w), (lb_logits, m_lb_logits, v_lb_logits),
        (hgrn_norm_w, m_hgrn_norm_w, v_hgrn_norm_w),
        (wf, m_final_norm_w.reshape(1, D), v_final_norm_w.reshape(1, D))])
    loss = small[16].reshape(())
    sg, sd, sm, sv = ([small[4 * i + j] for i in range(4)] for j in range(4))
    for lst in (sg, sd, sm, sv):
        lst[3] = lst[3].reshape(D)
    per_w = list(zip(big_w, g_big, big_m, big_v))
    upd = [_adamw(*per_w[0], "adamw_w_in")] + _adamw_whole(per_w[1:], "adamw_branches_out")
    bd, bm, bv, bg = ([u[j][None] for u in upd] for j in range(4))

    def order(s, b):
        return [s[0], b[0], s[1], s[2], b[1], b[2], b[3], s[3]]

    return (loss, loc["gx"][None], *order(sg, bg), *order(sd, bd), *order(sm, bm), *order(sv, bv))
```

```python
import functools

import jax
import jax.numpy as jnp
from jax import lax
from jax.experimental import pallas as pl
from jax.experimental.pallas import tpu as pltpu

T = 2048
D = 1024
NIN = 11264
HEADS = 8
HK = 128
CH = 16
NCH = T // CH
HSTEP = 2
ATT_GROUPS = ((128, 1), (512, 4), (2048, 16))
ATT_COL0 = 4096
AG_COL0 = 8704
GATE_COL0 = 9216
EPS = 1e-6
ROPE_THETA = 10000.0
LR, B1, B2, ADAM_EPS, WD, STEP = 0.001, 0.9, 0.999, 1e-08, 0.01, 10

F32 = jnp.float32
BF = jnp.bfloat16
VMEM_LIMIT = 56 * 1024 * 1024

_NN = (((1,), (0,)), ((), ()))
_NT = (((1,), (1,)), ((), ()))
_TN = (((0,), (0,)), ((), ()))


def _dot(a, b, dims=_NN):
    return lax.dot_general(a, b, dims, preferred_element_type=F32)


def _bdot(a, b, dims=_NN):
    return lax.dot_general(a.astype(BF), b.astype(BF), dims, preferred_element_type=F32)


def _sigmoid(x):
    return jax.nn.sigmoid(x)


def _params(sem=None):
    return pltpu.CompilerParams(dimension_semantics=sem, vmem_limit_bytes=VMEM_LIMIT)


def _matmul(a, b, *, ta=False, tb=False, out_dtype=F32, tm=512, tn=512, tk=None, name, side=None):
    m = a.shape[1] if ta else a.shape[0]
    kdim = a.shape[0] if ta else a.shape[1]
    n = b.shape[0] if tb else b.shape[1]
    tk = tk or kdim
    tm, tn = min(tm, m), min(tn, n)
    nm, nn, nk = m // tm, n // tn, kdim // tk
    dims = (((0 if ta else 1,), (1 if tb else 0,)), ((), ()))
    s_arrays, s_in_specs, s_shapes, s_out_specs, s_sems = _side_io(side)
    na, no = len(s_arrays), len(s_shapes)
    nacc = 1 if nk > 1 else 0

    def body(*refs):
        a_ref, b_ref = refs[:2]
        s_ins, o_ref, s_outs = refs[2:2 + na], refs[2 + na], refs[3 + na:3 + na + no]
        scratch = refs[3 + na + no:]
        s_sem_refs = scratch[nacc:]
        i, j, k = pl.program_id(0), pl.program_id(1), pl.program_id(2)
        if side is not None:
            @pl.when((i == 0) & (j == 0) & (k == 0))
            def _():
                side.first(s_ins, s_outs, s_sem_refs)

        prod = _bdot(a_ref[...], b_ref[...], dims)
        if nk == 1:
            o_ref[...] = prod.astype(out_dtype)
        else:
            acc = scratch[0]

            @pl.when(k == 0)
            def _():
                acc[...] = prod

            @pl.when(k > 0)
            def _():
                acc[...] += prod

            @pl.when(k == nk - 1)
            def _():
                o_ref[...] = acc[...].astype(out_dtype)

        if side is not None:
            @pl.when((i == nm - 1) & (j == nn - 1) & (k == nk - 1))
            def _():
                side.last(s_ins, s_outs, s_sem_refs)

    a_spec = pl.BlockSpec((tk, tm), lambda i, j, k: (k, i)) if ta else pl.BlockSpec((tm, tk), lambda i, j, k: (i, k))
    b_spec = pl.BlockSpec((tn, tk), lambda i, j, k: (j, k)) if tb else pl.BlockSpec((tk, tn), lambda i, j, k: (k, j))
    sem = ("parallel", "parallel", "arbitrary") if side is None else ("arbitrary",) * 3
    out = pl.pallas_call(
        body, name=name, grid=(nm, nn, nk),
        in_specs=[a_spec, b_spec] + s_in_specs,
        out_specs=[pl.BlockSpec((tm, tn), lambda i, j, k: (i, j))] + s_out_specs,
        out_shape=[jax.ShapeDtypeStruct((m, n), out_dtype)] + s_shapes,
        scratch_shapes=([pltpu.VMEM((tm, tn), F32)] if nk > 1 else []) + s_sems,
        compiler_params=_params(sem),
    )(a, b, *s_arrays)
    return out[0] if side is None else (out[0], out[1:])


DZ_TILE = 512


def _part_offsets(parts):
    counts = [p.shape[1] // DZ_TILE for p in parts]
    offs = [sum(counts[:i]) for i in range(len(parts))]
    return counts, offs


def _part_spec(rows, cnt, off, tile_axis):
    def index(*g):
        return (0 if rows is None else g[0], jnp.clip(g[tile_axis] - off, 0, cnt - 1))
    return index


def _grad_w_in(h, parts):
    counts, offs = _part_offsets(parts)
    n = len(parts)

    def body(h_ref, *refs):
        o_ref = refs[n]
        j = pl.program_id(0)
        for p_ref, cnt, off in zip(refs[:n], counts, offs):
            @pl.when((j >= off) & (j < off + cnt))
            def _(p_ref=p_ref):
                o_ref[...] = _bdot(h_ref[...], p_ref[...], _TN).astype(BF)

    return pl.pallas_call(
        body, name="g_win", grid=(sum(counts),),
        in_specs=[pl.BlockSpec((T, D), lambda j: (0, 0))] +
                 [pl.BlockSpec((T, DZ_TILE), _part_spec(None, c, o, 0)) for c, o in zip(counts, offs)],
        out_specs=pl.BlockSpec((D, DZ_TILE), lambda j: (0, j)),
        out_shape=jax.ShapeDtypeStruct((D, NIN), BF),
        compiler_params=_params(("parallel",)),
    )(h, *parts)


def _grad_w_in_half(h, parts, half_idx, side=None):
    counts, offs = _part_offsets(parts)
    n = len(parts)
    nj = sum(counts)
    s_arrays, s_in_specs, s_shapes, s_out_specs, s_sems = _side_io(side)
    na, no = len(s_arrays), len(s_shapes)

    def body(idx_ref, h_ref, *refs):
        s_ins, o_ref, s_outs, s_sem_refs = refs[n:n + na], refs[n + na], refs[n + na + 1:n + na + 1 + no], refs[n + na + 1 + no:]
        j = pl.program_id(0)
        if side is not None:
            @pl.when(j == 0)
            def _():
                side.first(s_ins, s_outs, s_sem_refs)

        for p_ref, cnt, off in zip(refs[:n], counts, offs):
            @pl.when((j >= off) & (j < off + cnt))
            def _(p_ref=p_ref):
                o_ref[...] = _bdot(h_ref[...], p_ref[...], _TN).astype(BF)

        if side is not None:
            @pl.when(j == nj - 1)
            def _():
                side.last(s_ins, s_outs, s_sem_refs)

    def part_spec(cnt, off):
        return pl.BlockSpec((T, DZ_TILE), lambda j, idx: (0, jnp.clip(j - off, 0, cnt - 1)))

    out = pl.pallas_call(
        body, name="g_win_half" if side is None else "g_win_half_carrying",
        grid_spec=pltpu.PrefetchScalarGridSpec(
            num_scalar_prefetch=1, grid=(nj,),
            in_specs=[pl.BlockSpec((T, D // 2), lambda j, idx: (0, idx[0]))] +
                     [part_spec(c, o) for c, o in zip(counts, offs)] + s_in_specs,
            out_specs=[pl.BlockSpec((D // 2, DZ_TILE), lambda j, idx: (0, j))] + s_out_specs,
            scratch_shapes=s_sems),
        out_shape=[jax.ShapeDtypeStruct((D // 2, NIN), BF)] + s_shapes,
        compiler_params=_params(("parallel",) if side is None else ("arbitrary",)),
    )(half_idx, h, *parts, *s_arrays)
    return out[0] if side is None else (out[0], out[1:])


def _side_io(side):
    if side is None:
        return [], [], [], [], []
    return (side.arrays, [HBM] * len(side.arrays), side.out_shapes, [HBM] * len(side.out_shapes), side.sems)


def _grad_x(parts, w_in, x, dout, norm_w, side=None):
    counts, offs = _part_offsets(parts)
    n = len(parts)
    tm = 1024
    nm, nk = T // tm, sum(counts)
    s_arrays, s_in_specs, s_shapes, s_out_specs, s_sems = _side_io(side)
    na, no = len(s_arrays), len(s_shapes)

    def body(*refs):
        w_ref, x_ref, dout_ref, nw_ref = refs[n:n + 4]
        s_ins = refs[n + 4:n + 4 + na]
        gx_ref, gw_ref = refs[n + 4 + na:n + 6 + na]
        s_outs = refs[n + 6 + na:n + 6 + na + no]
        acc = refs[n + 6 + na + no]
        s_sem_refs = refs[n + 7 + na + no:]
        i, k = pl.program_id(0), pl.program_id(1)

        @pl.when((i == 0) & (k == 0))
        def _():
            gw_ref[...] = jnp.zeros_like(gw_ref)
            if side is not None:
                side.first(s_ins, s_outs, s_sem_refs)

        @pl.when(k == 0)
        def _():
            acc[...] = jnp.zeros_like(acc)

        if side is not None and side.mid is not None:
            @pl.when((i == nm - 1) & (k == 0))
            def _():
                side.mid(s_ins, s_outs, s_sem_refs)

        for p_ref, cnt, off in zip(refs[:n], counts, offs):
            @pl.when((k >= off) & (k < off + cnt))
            def _(p_ref=p_ref):
                acc[...] += _bdot(p_ref[...], w_ref[...], _NT)

        @pl.when(k == nk - 1)
        def _():
            gw = jnp.zeros((1, D), F32)
            for c in range(tm // BLK):
                rows = pl.ds(BLK * c, BLK)
                xv, dhv = x_ref[rows, :], acc[rows, :]
                r = lax.rsqrt(jnp.mean(xv * xv, axis=-1, keepdims=True) + EPS)
                nrm = xv * r
                dn = dhv * nw_ref[...]
                gw = gw + jnp.sum(dhv * nrm, axis=0, keepdims=True)
                gx_ref[rows, :] = dout_ref[rows, :] + r * (dn - nrm * jnp.mean(dn * nrm, axis=-1, keepdims=True))
            gw_ref[...] += gw

        if side is not None:
            @pl.when((i == nm - 1) & (k == nk - 1))
            def _():
                side.last(s_ins, s_outs, s_sem_refs)

    row = pl.BlockSpec((tm, D), lambda i, k: (i, 0))
    vec = pl.BlockSpec((1, D), lambda i, k: (0, 0))
    out = pl.pallas_call(
        body, name="grad_x", grid=(nm, nk),
        in_specs=[pl.BlockSpec((tm, DZ_TILE), _part_spec(0, c, o, 1)) for c, o in zip(counts, offs)] +
                 [pl.BlockSpec((D, DZ_TILE), lambda i, k: (0, k)), row, row, vec] + s_in_specs,
        out_specs=[row, vec] + s_out_specs,
        out_shape=[jax.ShapeDtypeStruct((T, D), F32), jax.ShapeDtypeStruct((1, D), F32)] + s_shapes,
        scratch_shapes=[pltpu.VMEM((tm, D), F32)] + s_sems,
        compiler_params=_params(("arbitrary", "arbitrary")),
    )(*parts, w_in, x, dout, norm_w, *s_arrays)
    return out[0], out[1], out[2:]


def _norm_and_rope_tables(x, w, pos, invf, side=None, own=None):
    tm = 256
    nm = T // tm
    s_arrays, s_in_specs, s_shapes, s_out_specs, s_sems = _side_io(side)
    na, no = len(s_arrays), len(s_shapes)
    nz = 0 if own is None else 1
    wsh, blk = own if own is not None else (None, jnp.zeros((1,), jnp.int32))

    def body(blk_ref, *refs):
        x_ref, w_ref, pos_ref, invf_ref = refs[:4]
        s_ins = refs[4 + nz:4 + nz + na]
        h_ref, cos_ref, sa_ref, sb_ref = refs[4 + nz + na:8 + nz + na]
        s_outs = refs[8 + 2 * nz + na:8 + 2 * nz + na + no]
        s_sem_refs = refs[8 + 2 * nz + na + no:]
        if side is not None:
            @pl.when(pl.program_id(0) == 0)
            def _():
                side.first(s_ins, s_outs, s_sem_refs)

        xv = x_ref[...]
        r = lax.rsqrt(jnp.mean(xv * xv, axis=-1, keepdims=True) + EPS)
        h = (xv * r * w_ref[...]).astype(BF)
        h_ref[...] = h
        if own is not None:
            refs[8 + nz + na][...] = _dot(h, refs[4][...])
        first = (lax.broadcasted_iota(jnp.int32, (tm, 128), 1) % 64) < 32
        ang = pos_ref[...].astype(F32) * invf_ref[...]
        s = jnp.sin(ang)
        cos_ref[...] = jnp.cos(ang)
        sa_ref[...] = jnp.where(first, -s, 0.0)
        sb_ref[...] = jnp.where(first, 0.0, s)
        if side is not None:
            @pl.when(pl.program_id(0) == nm - 1)
            def _():
                side.last(s_ins, s_outs, s_sem_refs)

    tab = pl.BlockSpec((tm, 128), lambda i, b: (i, 0))
    own_in = [] if own is None else [pl.BlockSpec(wsh.shape, lambda i, b: (0, 0))]
    own_out = [] if own is None else [pl.BlockSpec((tm, wsh.shape[1]), lambda i, b: (i, b[0]))]
    own_shape = [] if own is None else [jax.ShapeDtypeStruct((T, NIN), F32)]
    out = pl.pallas_call(
        body, name="norm_and_rope_tables",
        grid_spec=pltpu.PrefetchScalarGridSpec(
            num_scalar_prefetch=1, grid=(nm,),
            in_specs=[pl.BlockSpec((tm, D), lambda i, b: (i, 0)), pl.BlockSpec((1, D), lambda i, b: (0, 0)),
                      pl.BlockSpec((tm, 1), lambda i, b: (i, 0)), pl.BlockSpec((1, 128), lambda i, b: (0, 0))]
                     + own_in + s_in_specs,
            out_specs=[pl.BlockSpec((tm, D), lambda i, b: (i, 0)), tab, tab, tab] + own_out + s_out_specs,
            scratch_shapes=s_sems),
        out_shape=[jax.ShapeDtypeStruct((T, D), BF)] + [jax.ShapeDtypeStruct((T, 128), F32)] * 3 + own_shape + s_shapes,
        compiler_params=_params(("parallel",) if side is None else ("arbitrary",)),
    )(blk, x, w, pos, invf, *([] if own is None else [wsh]), *s_arrays)
    return out[0], out[1], out[2], out[3], (out[4] if own is not None else None), out[4 + nz:]


def _z_blocks(h, w, z, idx, nb, side, name, fill=None):
    tm, tn = 1024, NIN // 8
    s_arrays, s_in_specs, s_shapes, s_out_specs, s_sems = _side_io(side)
    na, no = len(s_arrays), len(s_shapes)
    nm, ns = T // tm, 2 * nb
    nf = 0 if fill is None else 1

    def col(first, i, s, b):
        return (0, b[first + s // 2] * 2 + s % 2)

    def body(idx_ref, h_ref, w_ref, zin_ref, *refs):
        s_ins = refs[nf:nf + na]
        o_ref = refs[nf + na]
        s_outs = refs[nf + na + 1 + nf:nf + na + 1 + nf + no]
        s_sem_refs = refs[nf + na + 1 + nf + no + nf:]
        i, s = pl.program_id(0), pl.program_id(1)

        if side is not None:
            @pl.when((i == 0) & (s == 0))
            def _():
                side.first(s_ins, s_outs, s_sem_refs)

        if fill is not None:
            tile = pl.ds(pl.multiple_of((idx_ref[0] * 2 + s) * tn, 128), tn)
            store = pltpu.make_async_copy(w_ref, refs[nf + na + 1].at[:, tile], refs[nf + na + 1 + nf + no].at[0])
            pl.when(i == 0)(store.start)
        o_ref[...] = _dot(h_ref[...], w_ref[...])
        if fill is not None:
            pl.when(i == 0)(store.wait)

        if side is not None:
            @pl.when((i == nm - 1) & (s == ns - 1))
            def _():
                side.last(s_ins, s_outs, s_sem_refs)

    fills = [] if fill is None else [fill]
    out = pl.pallas_call(
        body, name=name,
        grid_spec=pltpu.PrefetchScalarGridSpec(
            num_scalar_prefetch=1, grid=(nm, ns),
            in_specs=[pl.BlockSpec((tm, D), lambda i, s, b: (i, 0)), pl.BlockSpec((D, tn), functools.partial(col, nb)),
                      HBM] + [HBM] * nf + s_in_specs,
            out_specs=[pl.BlockSpec((tm, tn), lambda i, s, b: (i, col(0, i, s, b)[1]))] + [HBM] * nf + s_out_specs,
            scratch_shapes=[pltpu.SemaphoreType.DMA((1,))] * nf + s_sems),
        out_shape=[jax.ShapeDtypeStruct((T, NIN), F32)] + [jax.ShapeDtypeStruct(f.shape, f.dtype) for f in fills]
                  + s_shapes,
        input_output_aliases={3: 0, **({4: 1} if fill is not None else {})},
        compiler_params=_params(("arbitrary", "arbitrary")),
    )(idx, h, w, z, *fills, *s_arrays)
    return (out[0], *out[1:1 + nf], out[1 + nf:])


def _lower_bound(lbl):
    mx = jnp.max(lbl, axis=0, keepdims=True)
    e = jnp.exp(lbl - mx)
    return e[0:1] / jnp.sum(e, axis=0, keepdims=True)


def _cumsum_rows(g, rows):
    b = g
    sh = 1
    while sh < CH:
        b = b + jnp.where(rows >= sh, pltpu.roll(b, sh, axis=0), 0.0)
        sh *= 2
    return b


def _rev_cumsum_rows(g, rows):
    b = g
    sh = 1
    while sh < CH:
        b = b + jnp.where(rows < CH - sh, pltpu.roll(b, CH - sh, axis=0), 0.0)
        sh *= 2
    return b


SUB = CH // 2


def _direct_block(qb, kb, vb, bb, rows8):
    ob = jnp.zeros_like(qb)
    for s in range(SUB):
        e_s = jnp.exp(jnp.where(rows8 >= s, bb - bb[s:s + 1], -jnp.inf))
        ob = ob + jnp.sum(qb * e_s * kb[s:s + 1], axis=1, keepdims=True) * vb[s:s + 1]
    return ob


def _direct_block_bwd(qb, kb, vb, bb, dob, rows8, rowc8):
    dq = dk = dv = db = jnp.zeros_like(qb)
    for s in range(SUB):
        one = (rowc8 == s).astype(F32)
        ks, vs = kb[s:s + 1], vb[s:s + 1]
        e_s = jnp.exp(jnp.where(rows8 >= s, bb - bb[s:s + 1], -jnp.inf))
        qes = qb * e_s
        w = qes * ks
        a = jnp.sum(w, axis=1, keepdims=True)
        da = jnp.sum(dob * vs, axis=1, keepdims=True)
        dv = dv + one * jnp.sum(a * dob, axis=0, keepdims=True)
        dq = dq + da * e_s * ks
        dk = dk + one * jnp.sum(da * qes, axis=0, keepdims=True)
        u = da * w
        db = db + u - one * jnp.sum(u, axis=0, keepdims=True)
    return dq, dk, dv, db


def _cross_factors(q, k, b):
    ref = b[SUB - 1:SUB]
    e_hi, e_lo = jnp.exp(b[SUB:] - ref), jnp.exp(ref - b[:SUB])
    return q[SUB:] * e_hi, k[:SUB] * e_lo, e_hi, e_lo


def _intra_fwd(q, k, v, b, rows8):
    lo = _direct_block(q[:SUB], k[:SUB], v[:SUB], b[:SUB], rows8)
    hi = _direct_block(q[SUB:], k[SUB:], v[SUB:], b[SUB:], rows8)
    qe_hi, ke_lo, _, _ = _cross_factors(q, k, b)
    for s in range(SUB):
        hi = hi + jnp.sum(qe_hi * ke_lo[s:s + 1], axis=1, keepdims=True) * v[s:s + 1]
    return jnp.concatenate([lo, hi], axis=0)


def _intra_bwd(q, k, v, b, do, rows8, rowc8):
    dq_lo, dk_lo, dv_lo, db_lo = _direct_block_bwd(q[:SUB], k[:SUB], v[:SUB], b[:SUB], do[:SUB], rows8, rowc8)
    dq_hi, dk_hi, dv_hi, db_hi = _direct_block_bwd(q[SUB:], k[SUB:], v[SUB:], b[SUB:], do[SUB:], rows8, rowc8)
    qe_hi, ke_lo, e_hi, e_lo = _cross_factors(q, k, b)
    do_hi, v_lo = do[SUB:], v[:SUB]
    dqe = dke = jnp.zeros_like(qe_hi)
    for s in range(SUB):
        one = (rowc8 == s).astype(F32)
        a = jnp.sum(qe_hi * ke_lo[s:s + 1], axis=1, keepdims=True)
        da = jnp.sum(do_hi * v_lo[s:s + 1], axis=1, keepdims=True)
        dv_lo = dv_lo + one * jnp.sum(a * do_hi, axis=0, keepdims=True)
        dqe = dqe + da * ke_lo[s:s + 1]
        dke = dke + one * jnp.sum(da * qe_hi, axis=0, keepdims=True)
    u_hi, u_lo = dqe * qe_hi, dke * ke_lo
    d_ref = jnp.sum(u_lo, axis=0, keepdims=True) - jnp.sum(u_hi, axis=0, keepdims=True)
    db_lo = db_lo - u_lo + (rowc8 == SUB - 1).astype(F32) * d_ref
    cat = lambda lo, hi: jnp.concatenate([lo, hi], axis=0)
    return (cat(dq_lo, dq_hi + dqe * e_hi), cat(dk_lo + dke * e_lo, dk_hi), cat(dv_lo, dv_hi),
            cat(db_lo, db_hi + u_hi))


def _hgrn_fwd(z, lbl, nw):
    def body(hq_ref, hf_ref, hi_ref, hg_ref, lbl_ref, nw_ref, oraw_ref, og_ref, sh_ref, st_ref):
        @pl.when(pl.program_id(0) == 0)
        def _():
            st_ref[...] = jnp.zeros_like(st_ref)

        lb_all = _lower_bound(lbl_ref[...])
        rows = lax.broadcasted_iota(jnp.int32, (CH, HK), 0)
        rows8 = lax.broadcasted_iota(jnp.int32, (SUB, HK), 0)
        nwv = nw_ref[...]
        for cc, h in [(cc, h) for cc in range(HSTEP) for h in range(HEADS)]:
            rs = slice(CH * cc, CH * (cc + 1))
            sl = slice(HK * h, HK * (h + 1))
            lb = lb_all[:, sl]
            hq, hf, v, hg = hq_ref[rs, sl], hf_ref[rs, sl], hi_ref[rs, sl], hg_ref[rs, sl]
            q = hq * _sigmoid(hq)
            f = lb + (1.0 - lb) * _sigmoid(hf)
            k = 1.0 - f
            b = _cumsum_rows(jnp.log(f), rows)
            sh_ref[cc, h] = st_ref[h]
            o = _bdot(q * jnp.exp(b), st_ref[h], _NT) + _intra_fwd(q, k, v, b, rows8)
            bl = b[CH - 1:CH]
            st_ref[h] = st_ref[h] * jnp.exp(bl)
            st_ref[h] += _bdot(v, k * jnp.exp(bl - b), _TN)
            oraw_ref[rs, sl] = o
            nrm = o * lax.rsqrt(jnp.mean(o * o, axis=1, keepdims=True) + EPS)
            og_ref[rs, sl] = (nrm * nwv * (hg * _sigmoid(hg))).astype(BF)

    zblk = lambda c: pl.BlockSpec((CH * HSTEP, D), lambda i, c=c: (i, c))
    return pl.pallas_call(
        body, name="hgrn_fwd", grid=(NCH // HSTEP,),
        in_specs=[zblk(0), zblk(1), zblk(2), zblk(3),
                  pl.BlockSpec((2, D), lambda i: (0, 0)), pl.BlockSpec((1, HK), lambda i: (0, 0))],
        out_specs=[zblk(0), zblk(0),
                   pl.BlockSpec((HSTEP, HEADS, HK, HK), lambda i: (i, 0, 0, 0))],
        out_shape=[jax.ShapeDtypeStruct((T, D), F32), jax.ShapeDtypeStruct((T, D), BF),
                   jax.ShapeDtypeStruct((NCH, HEADS, HK, HK), F32)],
        scratch_shapes=[pltpu.VMEM((HEADS, HK, HK), F32)],
        compiler_params=_params(("arbitrary",)),
    )(z, z, z, z, lbl, nw)


def _hgrn_bwd(z, lbl, nw, oraw, dog, shist):
    hstep = 1

    def body(hq_ref, hf_ref, hi_ref, hg_ref, lbl_ref, nw_ref, oraw_ref, dog_ref, sh_ref,
             dz_ref, dlb_ref, dnw_ref, dst_ref):
        @pl.when(pl.program_id(0) == 0)
        def _():
            dst_ref[...] = jnp.zeros_like(dst_ref)
            dlb_ref[...] = jnp.zeros_like(dlb_ref)
            dnw_ref[...] = jnp.zeros_like(dnw_ref)

        lb_all = _lower_bound(lbl_ref[...])
        rows = lax.broadcasted_iota(jnp.int32, (CH, HK), 0)
        rowc = lax.broadcasted_iota(jnp.int32, (CH, 1), 0)
        rows8 = lax.broadcasted_iota(jnp.int32, (SUB, HK), 0)
        rowc8 = lax.broadcasted_iota(jnp.int32, (SUB, 1), 0)
        nwv = nw_ref[...]
        dnw = jnp.zeros((1, HK), F32)
        for cc, h in [(cc, h) for cc in reversed(range(hstep)) for h in range(HEADS)]:
            rs = slice(CH * cc, CH * (cc + 1))
            sl = slice(HK * h, HK * (h + 1))
            lb = lb_all[:, sl]
            hq, hf, v, hg = hq_ref[rs, sl], hf_ref[rs, sl], hi_ref[rs, sl], hg_ref[rs, sl]
            o, dg_out = oraw_ref[rs, sl], dog_ref[rs, sl]
            sg = _sigmoid(hg)
            sil = hg * sg
            r = lax.rsqrt(jnp.mean(o * o, axis=1, keepdims=True) + EPS)
            nrm = o * r
            d_hg = dg_out * (nrm * nwv) * (sg * (1.0 + hg * (1.0 - sg)))
            dn = dg_out * nwv * sil
            dnw = dnw + jnp.sum(dg_out * nrm * sil, axis=0, keepdims=True)
            do = r * (dn - nrm * jnp.mean(dn * nrm, axis=1, keepdims=True))
            sq = _sigmoid(hq)
            q = hq * sq
            sig = _sigmoid(hf)
            f = lb + (1.0 - lb) * sig
            k = 1.0 - f
            b = _cumsum_rows(jnp.log(f), rows)
            eb = jnp.exp(b)
            qe = q * eb
            bl = b[CH - 1:CH]
            ebl = jnp.exp(bl)
            kdec = jnp.exp(bl - b)
            ke = k * kdec
            dqe = _bdot(do, sh_ref[cc, h])
            dq = dqe * eb
            db = dqe * qe
            dke = _bdot(v, dst_ref[h])
            dv = _bdot(ke, dst_ref[h], _NT)
            dk = dke * kdec
            rr = dke * ke
            db = db - rr
            db_last = (jnp.sum(rr, axis=0, keepdims=True)
                       + ebl * jnp.sum(dst_ref[h] * sh_ref[cc, h], axis=0, keepdims=True))
            dst_ref[h] = dst_ref[h] * ebl
            dst_ref[h] += _bdot(do, qe, _TN)
            dq_i, dk_i, dv_i, db_i = _intra_bwd(q, k, v, b, do, rows8, rowc8)
            dq, dk, dv = dq + dq_i, dk + dk_i, dv + dv_i
            db = db + db_i + (rowc == CH - 1).astype(F32) * db_last
            dgl = _rev_cumsum_rows(db, rows)
            df = dgl / f - dk
            dlb_ref[:, sl] += jnp.sum(df * (1.0 - sig), axis=0, keepdims=True)
            dz_ref[rs, sl] = (dq * (sq * (1.0 + hq * (1.0 - sq)))).astype(BF)
            dz_ref[rs, D + HK * h:D + HK * (h + 1)] = (df * (1.0 - lb) * sig * (1.0 - sig)).astype(BF)
            dz_ref[rs, 2 * D + HK * h:2 * D + HK * (h + 1)] = dv.astype(BF)
            dz_ref[rs, 3 * D + HK * h:3 * D + HK * (h + 1)] = d_hg.astype(BF)
        dnw_ref[...] += dnw

    rev = lambda i: NCH // hstep - 1 - i
    zblk = lambda c: pl.BlockSpec((CH * hstep, D), lambda i, c=c: (rev(i), c))
    return pl.pallas_call(
        body, name="hgrn_bwd", grid=(NCH // hstep,),
        in_specs=[zblk(0), zblk(1), zblk(2), zblk(3),
                  pl.BlockSpec((2, D), lambda i: (0, 0)), pl.BlockSpec((1, HK), lambda i: (0, 0)),
                  zblk(0), zblk(0),
                  pl.BlockSpec((hstep, HEADS, HK, HK), lambda i: (rev(i), 0, 0, 0))],
        out_specs=[pl.BlockSpec((CH * hstep, 4 * D), lambda i: (rev(i), 0)),
                   pl.BlockSpec((1, D), lambda i: (0, 0)), pl.BlockSpec((1, HK), lambda i: (0, 0))],
        out_shape=[jax.ShapeDtypeStruct((T, 4 * D), BF), jax.ShapeDtypeStruct((1, D), F32),
                   jax.ShapeDtypeStruct((1, HK), F32)],
        scratch_shapes=[pltpu.VMEM((HEADS, HK, HK), F32)],
        compiler_params=_params(("arbitrary",)),
    )(z, z, z, z, lbl, nw, oraw, dog, shist)


BLK = 128
NBLK = T // BLK
QK_SCALE = 0.125


def _head_masks():
    lane = lax.broadcasted_iota(jnp.int32, (1, BLK), 1)
    return [(lane < 64).astype(F32), (lane >= 64).astype(F32)]


def _pieces(dil):
    m = T // dil
    out = []
    for r in range(dil):
        for j in range(m // BLK):
            start = r + dil * BLK * j
            rows = pl.ds(start, BLK, stride=dil) if dil > 1 else pl.ds(start, BLK)
            out.append((rows, r * m + BLK * j))
    return out


def _rope(x, c, sa, sb):
    return x * c + pltpu.roll(x, 96, axis=1) * sa + pltpu.roll(x, 32, axis=1) * sb


def _rope_t(d, c, sa, sb):
    return d * c + pltpu.roll(d * sa, 32, axis=1) + pltpu.roll(d * sb, 96, axis=1)


def _rope_and_regroup(dil, q_ref, k_ref, v_ref, tables, stage_q, stage_k, qr_ref, kr_ref, vr_ref):
    cos_ref, sa_ref, sb_ref = tables
    to_q, to_k = (qr_ref, kr_ref) if dil == 1 else (stage_q, stage_k)
    for c in range(T // BLK):
        rows = pl.ds(BLK * c, BLK)
        cs, sa, sb = cos_ref[rows, :], sa_ref[rows, :], sb_ref[rows, :]
        to_q[rows, :] = (_rope(q_ref[rows, :], cs, sa, sb) * QK_SCALE).astype(to_q.dtype)
        to_k[rows, :] = _rope(k_ref[rows, :], cs, sa, sb).astype(to_k.dtype)
    for rows, dst in _pieces(dil):
        drows = pl.ds(dst, BLK)
        if dil > 1:
            qr_ref[drows, :] = stage_q[rows, :].astype(qr_ref.dtype)
            kr_ref[drows, :] = stage_k[rows, :].astype(kr_ref.dtype)
        vr_ref[drows, :] = v_ref[rows, :].astype(vr_ref.dtype)


def _window_bias(bias_ref):
    ii = lax.broadcasted_iota(jnp.int32, (2 * BLK, BLK), 0) % BLK
    jj = lax.broadcasted_iota(jnp.int32, (2 * BLK, BLK), 1)
    bias_ref[0] = jnp.where(jj <= ii, 0.0, -jnp.inf)
    bias_ref[1] = jnp.where(jj >= ii, 0.0, -jnp.inf)


def _blocks(bi):
    if isinstance(bi, int):
        return pl.ds(bi * BLK, BLK), pl.ds(max(bi - 1, 0) * BLK, BLK)
    return (pl.ds(pl.multiple_of(bi * BLK, BLK), BLK),
            pl.ds(pl.multiple_of(jnp.maximum(bi - 1, 0) * BLK, BLK), BLK))


def _stack_heads(x, masks):
    return jnp.concatenate([x * masks[0].astype(x.dtype), x * masks[1].astype(x.dtype)], axis=0).astype(BF)


def _side_steps(side, refs, **when):
    if side is None:
        return
    for stage, cond in when.items():
        if getattr(side, stage) is not None:
            pl.when(cond)(functools.partial(getattr(side, stage), *refs))


def _attn_fwd(z, cos, sa, sb, side=None):
    s_arrays, s_in_specs, s_shapes, s_out_specs, s_sems = _side_io(side)
    na, no = len(s_arrays), len(s_shapes)

    def body(q_ref, k_ref, v_ref, ag_ref, cos_ref, sa_ref, sb_ref, *refs):
        ob_ref, opre_ref, lse_ref, qr_ref, kr_ref, vr_ref = refs[na:na + 6]
        bias_ref, og_ref, lg_ref, otok_ref, ltok_ref, sc_ref = refs[na + 6 + no:na + 12 + no]
        s_refs = (refs[:na], refs[na + 6:na + 6 + no], refs[na + 12 + no:])
        p, g = pl.program_id(0), pl.program_id(1)
        _side_steps(side, s_refs, first=(p == 0) & (g == 0), mid=(p == 1) & (g == 0))
        masks = _head_masks()

        @pl.when(g == 0)
        def _():
            _window_bias(bias_ref)

        def group(gi):
            dil = ATT_GROUPS[gi][1]
            nblk = (T // dil) // BLK
            _rope_and_regroup(dil, q_ref, k_ref, v_ref, (cos_ref, sa_ref, sb_ref), lg_ref.at[0], lg_ref.at[1],
                              qr_ref, kr_ref, vr_ref)

            def scores(bi, slot):
                cur, prev = _blocks(bi)
                q2 = _stack_heads(qr_ref[cur, :], masks)
                sc_ref[slot, 0] = _dot(q2, kr_ref[cur, :], _NT) + bias_ref[0]
                if nblk > 1:
                    sc_ref[slot, 1] = (_dot(q2, kr_ref[prev, :], _NT)
                                       + (bias_ref[1] + jnp.where((bi % nblk) != 0, 0.0, -jnp.inf)))

            def finish(bi, slot):
                cur, prev = _blocks(bi)
                s_c, vc = sc_ref[slot, 0], vr_ref[cur, :]
                if nblk > 1:
                    s_p, vp = sc_ref[slot, 1], vr_ref[prev, :]
                    mx = jnp.max(jnp.maximum(s_c, s_p), axis=1, keepdims=True)
                    p_c, p_p = jnp.exp(s_c - mx), jnp.exp(s_p - mx)
                    den = jnp.sum(p_c + p_p, axis=1, keepdims=True)
                    oh = _dot(p_c.astype(BF), vc) + _dot(p_p.astype(BF), vp)
                else:
                    mx = jnp.max(s_c, axis=1, keepdims=True)
                    p_c = jnp.exp(s_c - mx)
                    den = jnp.sum(p_c, axis=1, keepdims=True)
                    oh = _dot(p_c.astype(BF), vc)
                on = oh / den
                lsev = jnp.broadcast_to(mx + jnp.log(den), (2 * BLK, BLK))
                og_ref[cur, :] = on[:BLK] * masks[0] + on[BLK:] * masks[1]
                lg_ref[0, cur, :] = lsev[:BLK]
                lg_ref[1, cur, :] = lsev[BLK:]

            def pair(j, carry):
                finish(2 * j, 0)
                scores(2 * j + 1, 1)
                finish(2 * j + 1, 1)
                scores(jnp.minimum(2 * j + 2, NBLK - 1), 0)
                return carry

            scores(0, 0)
            lax.fori_loop(0, NBLK // 2, pair, 0)
            for rows, src in _pieces(dil):
                srows = pl.ds(src, BLK)
                otok_ref[gi, rows, :] = og_ref[srows, :]
                ltok_ref[gi, 0, rows, :] = lg_ref[0, srows, :]
                ltok_ref[gi, 1, rows, :] = lg_ref[1, srows, :]

        for gi in range(3):
            pl.when(g == gi)(functools.partial(group, gi))

        @pl.when(g == 2)
        def _():
            for c in range(T // BLK):
                rows = pl.ds(BLK * c, BLK)
                wts = []
                for hh in range(2):
                    l0, l1, l2 = ltok_ref[0, hh, rows, :], ltok_ref[1, hh, rows, :], ltok_ref[2, hh, rows, :]
                    mx = jnp.maximum(jnp.maximum(l0, l1), l2)
                    e0, e1, e2 = jnp.exp(l0 - mx), jnp.exp(l1 - mx), jnp.exp(l2 - mx)
                    tot = e0 + e1 + e2
                    lse_ref[rows, BLK * hh:BLK * (hh + 1)] = mx + jnp.log(tot)
                    inv = 1.0 / tot
                    wts.append([e0 * inv, e1 * inv, e2 * inv])
                o = sum((wts[0][gi] * masks[0] + wts[1][gi] * masks[1]) * otok_ref[gi, rows, :] for gi in range(3))
                ag = ag_ref[rows, :]
                opre_ref[rows, :] = o
                ob_ref[rows, :] = (o * (ag * _sigmoid(ag))).astype(BF)

        _side_steps(side, s_refs, last=(p == 3) & (g == 2))

    c0 = ATT_COL0 // BLK
    zspec = lambda part: pl.BlockSpec((T, BLK), lambda p, g, part=part: (0, c0 + 12 * part + 4 * g + p))
    outspec = pl.BlockSpec((T, BLK), lambda p, g: (0, p))
    table = pl.BlockSpec((T, BLK), lambda p, g: (0, 0))
    regrouped = pl.BlockSpec((None, T, BLK), lambda p, g: (g, 0, p))
    big = lambda: pltpu.VMEM((T, BLK), F32)
    out = pl.pallas_call(
        body, name="attn_fwd", grid=(4, 3),
        in_specs=[zspec(0), zspec(1), zspec(2),
                  pl.BlockSpec((T, BLK), lambda p, g: (0, AG_COL0 // BLK + p)), table, table, table] + s_in_specs,
        out_specs=[outspec, outspec, pl.BlockSpec((T, 2 * BLK), lambda p, g: (0, p)), regrouped, regrouped, regrouped]
                  + s_out_specs,
        out_shape=[jax.ShapeDtypeStruct((T, 512), BF), jax.ShapeDtypeStruct((T, 512), F32),
                   jax.ShapeDtypeStruct((T, 8 * BLK), F32)] + [jax.ShapeDtypeStruct((3, T, 512), BF)] * 3 + s_shapes,
        scratch_shapes=[pltpu.VMEM((2, 2 * BLK, BLK), F32), big(),
                        pltpu.VMEM((2, T, BLK), F32), pltpu.VMEM((3, T, BLK), F32), pltpu.VMEM((3, 2, T, BLK), F32),
                        pltpu.VMEM((2, 2, 2 * BLK, BLK), F32)] + s_sems,
        compiler_params=_params(("parallel" if side is None else "arbitrary", "arbitrary")),
    )(z, z, z, z, cos, sa, sb, *s_arrays)
    return (*out[:6], out[6:])


def _attn_bwd(z, qs, ks, vs, cos, sa, sb, opre, lse, dob, side=None):
    s_arrays, s_in_specs, s_shapes, s_out_specs, s_sems = _side_io(side)
    na, no = len(s_arrays), len(s_shapes)

    def body(qs_ref, ks_ref, vs_ref, ag_ref, cos_ref, sa_ref, sb_ref, o_ref, lse0_ref, lse1_ref, dob_ref, *refs):
        dq_ref, dk_ref, dv_ref, dag_ref = refs[na:na + 4]
        (bias_ref, dtok_ref, qr_ref, kr_ref, vr_ref, dor_ref, lr_ref, dr_ref,
         dqr_ref, dkr_ref, dvr_ref, pd_ref, dotok_ref) = refs[na + 4 + no:na + 17 + no]
        s_refs = (refs[:na], refs[na + 4:na + 4 + no], refs[na + 17 + no:])
        p, g = pl.program_id(0), pl.program_id(1)
        _side_steps(side, s_refs, first=(p == 0) & (g == 0), mid=(p == 1) & (g == 0))
        masks = _head_masks()

        @pl.when(g == 0)
        def _():
            _window_bias(bias_ref)
            for c in range(T // BLK):
                rows = pl.ds(BLK * c, BLK)
                ag, dob_v, o = ag_ref[rows, :], dob_ref[rows, :], o_ref[rows, :]
                sg = _sigmoid(ag)
                dag_ref[rows, :] = (dob_v * o * (sg * (1.0 + ag * (1.0 - sg)))).astype(BF)
                do = dob_v * (ag * sg)
                dotok_ref[rows, :] = do
                prod = do * o
                for hh, mh in enumerate(masks):
                    dtok_ref[hh, rows, :] = jnp.broadcast_to(jnp.sum(prod * mh, axis=1, keepdims=True), (BLK, BLK))

        def group(gi):
            dil = ATT_GROUPS[gi][1]
            nblk = (T // dil) // BLK
            for rows, dst in _pieces(dil):
                drows = pl.ds(dst, BLK)
                dor_ref[drows, :] = dotok_ref[rows, :]
                for hh, lse_ref in enumerate((lse0_ref, lse1_ref)):
                    lr_ref[hh, drows, :] = lse_ref[rows, :]
                    dr_ref[hh, drows, :] = dtok_ref[hh, rows, :]
            dkr_ref[...] = jnp.zeros_like(dkr_ref)
            dvr_ref[...] = jnp.zeros_like(dvr_ref)

            def probs(bi, slot):
                cur, prev = _blocks(bi)
                q2, do2 = _stack_heads(qs_ref[cur, :], masks), _stack_heads(dor_ref[cur, :], masks)
                lh = jnp.concatenate([lr_ref[0, cur, :], lr_ref[1, cur, :]], axis=0)
                dh = jnp.concatenate([dr_ref[0, cur, :], dr_ref[1, cur, :]], axis=0)
                p_c = jnp.exp(_dot(q2, ks_ref[cur, :], _NT) + bias_ref[0] - lh)
                pd_ref[slot, 0] = p_c.astype(BF)
                pd_ref[slot, 1] = (p_c * (_dot(do2, vs_ref[cur, :], _NT) - dh)).astype(BF)
                if nblk > 1:
                    bias_p = bias_ref[1] + jnp.where((bi % nblk) != 0, 0.0, -jnp.inf)
                    p_p = jnp.exp(_dot(q2, ks_ref[prev, :], _NT) + bias_p - lh)
                    pd_ref[slot, 2] = p_p.astype(BF)
                    pd_ref[slot, 3] = (p_p * (_dot(do2, vs_ref[prev, :], _NT) - dh)).astype(BF)

            def grads(bi, slot):
                cur, prev = _blocks(bi)
                q2, do2 = _stack_heads(qs_ref[cur, :], masks), _stack_heads(dor_ref[cur, :], masks)
                p_c, ds_c = pd_ref[slot, 0], pd_ref[slot, 1]
                dq2 = _dot(ds_c, ks_ref[cur, :])
                dkr_ref[cur, :] += _dot(ds_c, q2, _TN)
                dvr_ref[cur, :] += _dot(p_c, do2, _TN)
                if nblk > 1:
                    p_p, ds_p = pd_ref[slot, 2], pd_ref[slot, 3]
                    dq2 = dq2 + _dot(ds_p, ks_ref[prev, :])
                    dkr_ref[prev, :] += _dot(ds_p, q2, _TN)
                    dvr_ref[prev, :] += _dot(p_p, do2, _TN)
                dqr_ref[cur, :] = dq2[:BLK] * masks[0] + dq2[BLK:] * masks[1]

            def pair(j, carry):
                grads(2 * j, 0)
                probs(2 * j + 1, 1)
                grads(2 * j + 1, 1)
                probs(jnp.minimum(2 * j + 2, NBLK - 1), 0)
                return carry

            probs(0, 0)
            lax.fori_loop(0, NBLK // 2, pair, 0)
            if dil > 1:
                for rows, src in _pieces(dil):
                    srows = pl.ds(src, BLK)
                    qr_ref[rows, :] = dqr_ref[srows, :]
                    kr_ref[rows, :] = dkr_ref[srows, :]
                    vr_ref[rows, :] = dvr_ref[srows, :]
            tq, tk, tv = (qr_ref, kr_ref, vr_ref) if dil > 1 else (dqr_ref, dkr_ref, dvr_ref)
            for c in range(T // BLK):
                rows = pl.ds(BLK * c, BLK)
                cs, sa, sb = cos_ref[rows, :], sa_ref[rows, :], sb_ref[rows, :]
                dq_ref[rows, :] = _rope_t(tq[rows, :] * QK_SCALE, cs, sa, sb).astype(BF)
                dk_ref[rows, :] = _rope_t(tk[rows, :], cs, sa, sb).astype(BF)
                dv_ref[rows, :] = tv[rows, :].astype(BF)

        for gi in range(3):
            pl.when(g == gi)(functools.partial(group, gi))
        _side_steps(side, s_refs, last=(p == 3) & (g == 2))

    regrouped = pl.BlockSpec((None, T, BLK), lambda p, g: (g, 0, p))
    pspec = pl.BlockSpec((T, BLK), lambda p, g: (0, p))
    gspec = pl.BlockSpec((T, BLK), lambda p, g: (0, 4 * g + p))
    table = pl.BlockSpec((T, BLK), lambda p, g: (0, 0))
    big = lambda: pltpu.VMEM((T, BLK), F32)
    two = lambda: pltpu.VMEM((2, T, BLK), F32)
    out = pl.pallas_call(
        body, name="attn_bwd", grid=(4, 3),
        in_specs=[regrouped, regrouped, regrouped,
                  pl.BlockSpec((T, BLK), lambda p, g: (0, AG_COL0 // BLK + p)), table, table, table,
                  pspec, pl.BlockSpec((T, BLK), lambda p, g: (0, 2 * p)),
                  pl.BlockSpec((T, BLK), lambda p, g: (0, 2 * p + 1)), pspec] + s_in_specs,
        out_specs=[gspec, gspec, gspec, pspec] + s_out_specs,
        out_shape=[jax.ShapeDtypeStruct((T, 1536), BF), jax.ShapeDtypeStruct((T, 1536), BF),
                   jax.ShapeDtypeStruct((T, 1536), BF), jax.ShapeDtypeStruct((T, 512), BF)] + s_shapes,
        scratch_shapes=[pltpu.VMEM((2, 2 * BLK, BLK), F32), two(), big(), big(), big(), big(),
                        two(), two(), big(), big(), big(), pltpu.VMEM((2, 4, 2 * BLK, BLK), BF), big()] + s_sems,
        compiler_params=_params(("parallel" if side is None else "arbitrary", "arbitrary")),
    )(qs, ks, vs, z, cos, sa, sb, opre, lse, lse, dob, *s_arrays)
    return (*out[:4], out[4:])


def _merge_out_loss(og, ob, z, w_a, w_b, w_out, x, tgt, wf):
    tm = 512

    def body(og_ref, ob_ref, ga_ref, gb_ref, wa_ref, wb_ref, wo_ref, x_ref, t_ref, wf_ref,
             m_ref, dout_ref, loss_ref, gwf_ref):
        @pl.when(pl.program_id(0) == 0)
        def _():
            loss_ref[...] = jnp.zeros_like(loss_ref)
            gwf_ref[...] = jnp.zeros_like(gwf_ref)

        ya, yb = _dot(og_ref[...], wa_ref[...]), _dot(ob_ref[...], wb_ref[...])
        m =(_sigmoid(ga_ref[...]) * ya + _sigmoid(gb_ref[...]) * yb).astype(BF)
        m_ref[...] = m
        out = x_ref[...] + _dot(m, wo_ref[...])
        r = lax.rsqrt(jnp.mean(out * out, axis=-1, keepdims=True) + EPS)
        yh = out * r
        wfv = wf_ref[...]
        err = yh * wfv - t_ref[...]
        loss_ref[...] += jnp.sum(err * err, axis=0, keepdims=True) * (0.5 / D)
        dy = err * (1.0 / D)
        gwf_ref[...] += jnp.sum(dy * yh, axis=0, keepdims=True)
        dyh = dy * wfv
        dout_ref[...] = r * (dyh - yh * jnp.mean(dyh * yh, axis=-1, keepdims=True))

    row = pl.BlockSpec((tm, D), lambda i: (i, 0))
    vec = pl.BlockSpec((1, D), lambda i: (0, 0))
    whole = lambda w: pl.BlockSpec(w.shape, lambda i: (0, 0))
    return pl.pallas_call(
        body, name="merge_out_loss", grid=(T // tm,),
        in_specs=[row, pl.BlockSpec((tm, ob.shape[1]), lambda i: (i, 0)),
                  pl.BlockSpec((tm, D), lambda i: (i, GATE_COL0 // D)),
                  pl.BlockSpec((tm, D), lambda i: (i, GATE_COL0 // D + 1)),
                  whole(w_a), whole(w_b), whole(w_out), row, row, vec],
        out_specs=[row, row, vec, vec],
        out_shape=[jax.ShapeDtypeStruct((T, D), BF), jax.ShapeDtypeStruct((T, D), F32),
                   jax.ShapeDtypeStruct((1, D), F32), jax.ShapeDtypeStruct((1, D), F32)],
        compiler_params=_params(("arbitrary",)),
    )(og, ob, z, z, w_a, w_b, w_out, x, tgt, wf)


def _merge_proj_bwd(dout, og, ob, z, w_a, w_b, w_out):
    tm = 512

    def body(dout_ref, og_ref, ob_ref, ga_ref, gb_ref, wa_ref, wb_ref, wo_ref,
             dya_ref, dyb_ref, dg_ref, dog_ref, dob_ref):
        dmv = _dot(dout_ref[...].astype(BF), wo_ref[...], _NT)
        sa, sb = _sigmoid(ga_ref[...]), _sigmoid(gb_ref[...])
        dya, dyb = (sa * dmv).astype(BF), (sb * dmv).astype(BF)
        dya_ref[...] = dya
        dyb_ref[...] = dyb
        dg_ref[:, :D] = (dmv * _dot(og_ref[...], wa_ref[...]) * sa * (1.0 - sa)).astype(BF)
        dg_ref[:, D:] = (dmv * _dot(ob_ref[...], wb_ref[...]) * sb * (1.0 - sb)).astype(BF)
        dog_ref[...] = _dot(dya, wa_ref[...], _NT)
        dob_ref[...] = _dot(dyb, wb_ref[...], _NT)

    row = pl.BlockSpec((tm, D), lambda i: (i, 0))
    whole = lambda w: pl.BlockSpec(w.shape, lambda i: (0, 0))
    nb = w_b.shape[0]
    return pl.pallas_call(
        body, name="merge_proj_bwd", grid=(T // tm,),
        in_specs=[row, row, pl.BlockSpec((tm, nb), lambda i: (i, 0)),
                  pl.BlockSpec((tm, D), lambda i: (i, GATE_COL0 // D)),
                  pl.BlockSpec((tm, D), lambda i: (i, GATE_COL0 // D + 1)), whole(w_a), whole(w_b), whole(w_out)],
        out_specs=[row, row, pl.BlockSpec((tm, 2 * D), lambda i: (i, 0)), row,
                   pl.BlockSpec((tm, nb), lambda i: (i, 0))],
        out_shape=[jax.ShapeDtypeStruct((T, D), BF), jax.ShapeDtypeStruct((T, D), BF),
                   jax.ShapeDtypeStruct((T, 2 * D), BF), jax.ShapeDtypeStruct((T, D), F32),
                   jax.ShapeDtypeStruct((T, nb), F32)],
        compiler_params=_params(("parallel",)),
    )(dout, og, ob, z, z, w_a, w_b, w_out)


def _rope_inv_freq():
    inv = ROPE_THETA ** (-jnp.arange(0, 64, 2, dtype=F32) / 64)
    return jnp.tile(inv, 4).reshape(1, BLK)


def _local_step(x, pos, norm_w, lbl, hnw, wf, tgt, w_in, w_a, w_b, w_out, shard_shapes=()):
    invf = _rope_inv_freq()
    if shard_shapes:
        blk = jnp.reshape(2 * lax.axis_index("x") + lax.axis_index("y"), (1,)).astype(jnp.int32)
        h, cos, sa, sb, z_own, (w_near,) = _norm_and_rope_tables(
            x, norm_w, pos, invf, side=_gather_near_side(w_in, WEIGHT_AXES[0]), own=(w_in, blk))
        near = jnp.concatenate([blk ^ 2, blk ^ 1])
        z, (w_diag,) = _z_blocks(h, w_near, z_own, jnp.concatenate([near, near]), 2, name="z_proj_near",
                                 side=_gather_diag_side(w_near, w_in.shape, WEIGHT_AXES[0]))
        z, w_in, _ = _z_blocks(h, w_diag, z, jnp.concatenate([blk ^ 3, jnp.zeros_like(blk)]), 1, name="z_proj_diag",
                               fill=w_near, side=None)
    else:
        h, cos, sa, sb, _, _ = _norm_and_rope_tables(x, norm_w, pos, invf)
        z = _matmul(h, w_in, tm=T, tn=512, name="z_proj")
    oraw, og, shist = _hgrn_fwd(z, lbl, hnw)
    side_a = _gather_side([w_a, w_b, w_out], WEIGHT_AXES[1:]) if shard_shapes else None
    ob, opre, lse, qs, ks, vs, gathered = _attn_fwd(z, cos, sa, sb, side=side_a)
    if shard_shapes:
        w_a, w_b, w_out = gathered
    merged, dout, loss_vec, g_wf = _merge_out_loss(og, ob, z, w_a, w_b, w_out, x, tgt, wf)

    dya, dyb, dgates, dog, dob = _merge_proj_bwd(dout, og, ob, z, w_a, w_b, w_out)
    g_wout = _matmul(merged, dout, ta=True, out_dtype=BF, tm=512, tn=1024, name="g_wout")
    g_wa = _matmul(og, dya, ta=True, out_dtype=BF, tm=512, tn=1024, name="g_wa")
    g_wb = _matmul(ob, dyb, ta=True, out_dtype=BF, tm=512, tn=1024, name="g_wb")
    small = [g_wa, g_wb, g_wout]
    side_s = side_w = None
    if shard_shapes:
        p3_s = _rs_partials(small, shard_shapes[1:], WEIGHT_AXES[1:], "small")
        side_s = _chip_exchange_side(p3_s, shard_shapes[1:], WEIGHT_AXES[1:])
    dz_h, dlb, g_hnw = _hgrn_bwd(z, lbl, hnw, oraw, dog, shist)
    dq, dk, dv, dag, land_s = _attn_bwd(z, qs, ks, vs, cos, sa, sb, opre, lse, dob, side=side_s)
    dz_parts = [dz_h, dq, dk, dv, dag, dgates]
    if shard_shapes:
        c = lax.axis_index("c")
        half = lambda i: jnp.reshape(i, (1,)).astype(jnp.int32)
        g_send = _grad_w_in_half(h, dz_parts, half(1 - c))
        g_keep, (g_sib,) = _grad_w_in_half(h, dz_parts, half(c), side=_sibling_send_side(g_send))
        p3_w = [_add_bf16(g_keep, g_sib, "pair_sum_w_in").reshape(1, D // 2, NIN)]
        side_w = _chip_exchange_relay_side(p3_w[0], shard_shapes[0])
    else:
        g_big = [_grad_w_in(h, dz_parts)] + small
    gx, g_nw, land_w = _grad_x(dz_parts, w_in, x, dout, norm_w, side=side_w)
    small_sums = None
    if shard_shapes:
        g_big, small_sums = _rs_finish(p3_w + p3_s, [land_w[0]] + list(land_s), shard_shapes, WEIGHT_AXES,
                                       (g_nw, dlb, g_hnw, g_wf, loss_vec))
    return dict(loss_vec=loss_vec, gx=gx, g_nw=g_nw, dlb=dlb, g_hnw=g_hnw, g_wf=g_wf, small_sums=small_sums,
                g_win=g_big[0], g_wa=g_big[1], g_wb=g_big[2], g_wout=g_big[3])


MESH = pl.DeviceIdType.MESH
HBM = pl.BlockSpec(memory_space=pl.ANY)
WEIGHT_AXES = (1, 0, 1, 0)


def _place():
    x, y, c = lax.axis_index("x"), lax.axis_index("y"), lax.axis_index("c")
    chips = [(1 - x, y), (x, 1 - y), (1 - x, 1 - y)]
    return x, y, c, chips


def _block_half(ref, shard_shape, axis, j, half):
    r, c = shard_shape
    hr = r // 2
    if axis == 0:
        return ref.at[pl.ds(pl.multiple_of(j * r + half * hr, 16), hr), :]
    return ref.at[pl.ds(pl.multiple_of(half * hr, 16), hr), pl.ds(pl.multiple_of(j * c, 128), c)]


class _Side:
    def __init__(self, arrays, out_shapes, sems, first, last, mid=None):
        self.arrays, self.out_shapes, self.sems, self.first, self.last = arrays, out_shapes, sems, first, last
        self.mid = mid


def _gather_side(shards, axes):
    n = len(shards)
    shapes = [s.shape for s in shards]

    def copies(ins, outs, sems):
        send1, recv1, send2, recv2, send0, recv0 = sems
        x, y, c, chips = _place()
        me = 2 * x + y
        sib = (x, y, 1 - c)
        near = ((1 - c) * (1 - x) + c * x, (1 - c) * y + c * (1 - y))
        far = ((1 - c) * x + c * (1 - x), (1 - c) * (1 - y) + c * y)
        out = []
        for a in range(n):
            r, cc = shapes[a]
            mine = (outs[a].at[pl.ds(pl.multiple_of(me * r, 16), r), :] if axes[a] == 0
                    else outs[a].at[:, pl.ds(pl.multiple_of(me * cc, 128), cc)])
            own = pltpu.make_async_remote_copy(
                src_ref=ins[a], dst_ref=mine, send_sem=send0.at[a], recv_sem=recv0.at[a],
                device_id=sib, device_id_type=MESH)
            src = ins[a].at[pl.ds(pl.multiple_of(c * (r // 2), 16), r // 2), :]
            sends = [pltpu.make_async_remote_copy(
                src_ref=src, dst_ref=_block_half(outs[a], shapes[a], axes[a], me, c),
                send_sem=send1.at[a, k], recv_sem=recv1.at[a, k], device_id=(*chips[k], c), device_id_type=MESH)
                for k in range(2)]

            def region(chip, half):
                return _block_half(outs[a], shapes[a], axes[a], 2 * chip[0] + chip[1], half)

            def arrival(chip, k):
                reg = region(chip, c)
                return pltpu.make_async_remote_copy(
                    src_ref=reg, dst_ref=reg, send_sem=send1.at[a, k], recv_sem=recv1.at[a, k],
                    device_id=(*chip, c), device_id_type=MESH)

            def to_sibling(chip, k):
                reg = region(chip, c)
                return pltpu.make_async_remote_copy(
                    src_ref=reg, dst_ref=reg, send_sem=send2.at[a, k], recv_sem=recv2.at[a, k],
                    device_id=sib, device_id_type=MESH)

            def from_sibling(chip, k):
                reg = region(chip, 1 - c)
                return pltpu.make_async_remote_copy(
                    src_ref=reg, dst_ref=reg, send_sem=send2.at[a, k], recv_sem=recv2.at[a, k],
                    device_id=sib, device_id_type=MESH)

            relay = pltpu.make_async_remote_copy(
                src_ref=region(near, c), dst_ref=region(near, c), send_sem=send1.at[a, 2], recv_sem=recv1.at[a, 2],
                device_id=(*far, c), device_id_type=MESH)
            hops = [(arrival(near, c), to_sibling(near, c)), (arrival(far, 1 - c), to_sibling(far, 1 - c)),
                    (arrival(chips[2], 2), to_sibling(chips[2], 2))]
            back = [from_sibling(chips[k], k) for k in range(3)]
            out.append((own, sends, relay, hops, back))
        return out

    def first(ins, outs, sems):
        for own, sends, _, _, _ in copies(ins, outs, sems):
            own.start()
            for cp in sends:
                cp.start()

    def mid(ins, outs, sems):
        per_array = copies(ins, outs, sems)
        for step in range(2):
            for _, _, relay, hops, _ in per_array:
                arrived, onward = hops[step]
                arrived.wait_recv()
                if step == 0:
                    relay.start()
                onward.start()

    def last(ins, outs, sems):
        per_array = copies(ins, outs, sems)
        for _, _, _, hops, _ in per_array:
            arrived, onward = hops[2]
            arrived.wait_recv()
            onward.start()
        for own, sends, relay, hops, back in per_array:
            for cp in back:
                cp.wait_recv()
            for cp in sends + [relay] + [onward for _, onward in hops]:
                cp.wait_send()
            own.wait()

    full = [(4 * r, c) if ax == 0 else (r, 4 * c) for (r, c), ax in zip(shapes, axes)]
    sems = [pltpu.SemaphoreType.DMA((n, 3)), pltpu.SemaphoreType.DMA((n, 3)),
            pltpu.SemaphoreType.DMA((n, 3)), pltpu.SemaphoreType.DMA((n, 3)),
            pltpu.SemaphoreType.DMA((n,)), pltpu.SemaphoreType.DMA((n,))]
    return _Side(list(shards), [jax.ShapeDtypeStruct(f, BF) for f in full], sems, first, last, mid)


def _gather_near_side(shard, axis):
    shape = shard.shape
    r, cc = shape

    def copies(ins, outs, sems):
        send1, recv1, send2, recv2, send0, recv0 = sems
        x, y, c, chips = _place()
        me = 2 * x + y
        sib = (x, y, 1 - c)
        mine = (outs[0].at[pl.ds(pl.multiple_of(me * r, 16), r), :] if axis == 0
                else outs[0].at[:, pl.ds(pl.multiple_of(me * cc, 128), cc)])
        own = pltpu.make_async_remote_copy(
            src_ref=ins[0], dst_ref=mine, send_sem=send0.at[0], recv_sem=recv0.at[0],
            device_id=sib, device_id_type=MESH)
        src = ins[0].at[pl.ds(pl.multiple_of(c * (r // 2), 16), r // 2), :]

        def region(k, half):
            return _block_half(outs[0], shape, axis, 2 * chips[k][0] + chips[k][1], half)

        def moves(k):
            return [pltpu.make_async_remote_copy(
                        src_ref=s, dst_ref=d, send_sem=ss.at[k], recv_sem=rs.at[k], device_id=dev,
                        device_id_type=MESH)
                    for s, d, ss, rs, dev in (
                        (src, _block_half(outs[0], shape, axis, me, c), send1, recv1, (*chips[k], c)),
                        (region(k, c), region(k, c), send1, recv1, (*chips[k], c)),
                        (region(k, c), region(k, c), send2, recv2, sib),
                        (region(k, 1 - c), region(k, 1 - c), send2, recv2, sib))]

        return own, [moves(k) for k in range(2)]

    def first(ins, outs, sems):
        own, per_chip = copies(ins, outs, sems)
        own.start()
        for send, _, _, _ in per_chip:
            send.start()

    def last(ins, outs, sems):
        own, per_chip = copies(ins, outs, sems)
        for _, arrived, onward, _ in per_chip:
            arrived.wait_recv()
            onward.start()
        for send, _, onward, back in per_chip:
            back.wait_recv()
            send.wait_send()
            onward.wait_send()
        own.wait()

    full = (4 * r, cc) if axis == 0 else (r, 4 * cc)
    sems = [pltpu.SemaphoreType.DMA((2,))] * 4 + [pltpu.SemaphoreType.DMA((1,))] * 2
    return _Side([shard], [jax.ShapeDtypeStruct(full, BF)], sems, first, last)


def _gather_diag_side(gathered, shape, axis):
    r, cc = shape

    def copies(ins, outs, sems):
        send1, recv1, send2, recv2 = sems
        x, y, c, _ = _place()
        sib = (x, y, 1 - c)
        near = ((1 - c) * (1 - x) + c * x, (1 - c) * y + c * (1 - y))
        far = ((1 - c) * x + c * (1 - x), (1 - c) * (1 - y) + c * y)

        def half(i):
            return outs[0].at[pl.ds(pl.multiple_of(i * (r // 2), 16), r // 2), :]

        def move(s, d, ss, rs, dev):
            return pltpu.make_async_remote_copy(
                src_ref=s, dst_ref=d, send_sem=ss.at[0], recv_sem=rs.at[0], device_id=dev, device_id_type=MESH)

        relay = move(_block_half(ins[0], shape, axis, 2 * near[0] + near[1], c), half(c), send1, recv1, (*far, c))
        arrived = move(half(c), half(c), send1, recv1, (*far, c))
        onward = move(half(c), half(c), send2, recv2, sib)
        back = move(half(1 - c), half(1 - c), send2, recv2, sib)
        return relay, arrived, onward, back

    def first(ins, outs, sems):
        copies(ins, outs, sems)[0].start()

    def last(ins, outs, sems):
        relay, arrived, onward, back = copies(ins, outs, sems)
        arrived.wait_recv()
        onward.start()
        back.wait_recv()
        relay.wait_send()
        onward.wait_send()

    return _Side([gathered], [jax.ShapeDtypeStruct(shape, BF)], [pltpu.SemaphoreType.DMA((1,))] * 4, first, last)


def _as3d(g, shard_shape, axis):
    r, c = shard_shape
    return g.reshape(4, r, c) if axis == 0 else g.reshape(1, r, 4 * c)


def _half_rows(ref3, hr, half):
    return ref3.at[:, pl.ds(pl.multiple_of(half * hr, 16), hr), :]


def _rs_pair_exchange(g3s, name):
    n = len(g3s)

    def body(*refs):
        ins, outs = refs[:n], refs[n:2 * n]
        send, recv = refs[2 * n:]
        x, y, c, _ = _place()
        cps = []
        for a in range(n):
            hr = g3s[a].shape[1] // 2
            cp = pltpu.make_async_remote_copy(
                src_ref=_half_rows(ins[a], hr, 1 - c), dst_ref=outs[a],
                send_sem=send.at[a], recv_sem=recv.at[a], device_id=(x, y, 1 - c), device_id_type=MESH)
            cp.start()
            cps.append(cp)
        for cp in cps:
            cp.wait()

    return pl.pallas_call(
        body, name=name,
        in_specs=[HBM] * n, out_specs=[HBM] * n,
        out_shape=[jax.ShapeDtypeStruct((g.shape[0], g.shape[1] // 2, g.shape[2]), BF) for g in g3s],
        scratch_shapes=[pltpu.SemaphoreType.DMA((n,)), pltpu.SemaphoreType.DMA((n,))],
    )(*g3s)


def _pair_sums(g3s, lands, cidx, name):
    n = len(g3s)

    def body(c_ref, *refs):
        for g_ref, l_ref, o_ref in zip(refs[:n], refs[n:2 * n], refs[2 * n:]):
            o_ref[...] = (g_ref[...].astype(F32) + l_ref[...].astype(F32)).astype(BF)

    halves = [(g.shape[0], g.shape[1] // 2, g.shape[2]) for g in g3s]
    return pl.pallas_call(
        body, name=name,
        grid_spec=pltpu.PrefetchScalarGridSpec(
            num_scalar_prefetch=1, grid=(1,),
            in_specs=[pl.BlockSpec(h, lambda i, c: (0, c[0], 0)) for h in halves]
                     + [pl.BlockSpec(h, lambda i, c: (0, 0, 0)) for h in halves],
            out_specs=[pl.BlockSpec(h, lambda i, c: (0, 0, 0)) for h in halves]),
        out_shape=[jax.ShapeDtypeStruct(h, BF) for h in halves],
        compiler_params=_params(("arbitrary",)),
    )(cidx, *g3s, *lands)


def _chip_exchange_side(p3s, shapes, axes):
    n = len(p3s)

    def copies(ins, outs, sems):
        send, recv = sems
        x, y, c, chips = _place()
        cps = []
        for a in range(n):
            r, cc = shapes[a]
            for k, (px, py) in enumerate(chips):
                j = 2 * px + py
                src = ins[a].at[j] if axes[a] == 0 else ins[a].at[0, :, pl.ds(pl.multiple_of(j * cc, 128), cc)]
                cps.append(pltpu.make_async_remote_copy(
                    src_ref=src, dst_ref=outs[a].at[k], send_sem=send.at[a, k], recv_sem=recv.at[a, k],
                    device_id=(px, py, c), device_id_type=MESH))
        return cps

    def first(ins, outs, sems):
        for cp in copies(ins, outs, sems):
            cp.start()

    def last(ins, outs, sems):
        for cp in copies(ins, outs, sems):
            cp.wait()

    return _Side(list(p3s), [jax.ShapeDtypeStruct((3, r // 2, c), BF) for r, c in shapes],
                 [pltpu.SemaphoreType.DMA((n, 3)), pltpu.SemaphoreType.DMA((n, 3))], first, last)


def _chip_exchange_relay_side(p3, shape):
    r, cc = shape
    hr = r // 2
    rows = 64

    def copies(ins, outs, sems):
        send, recv, local, mine, theirs = sems
        x, y, c, chips = _place()
        near = ((1 - c) * (1 - x) + c * x, (1 - c) * y + c * (1 - y))
        far = ((1 - c) * x + c * (1 - x), (1 - c) * (1 - y) + c * y)
        land, staged = outs

        def block(chip):
            return ins[0].at[0, :, pl.ds(pl.multiple_of((2 * chip[0] + chip[1]) * cc, 128), cc)]

        def move(s, d, k, dev):
            return pltpu.make_async_remote_copy(
                src_ref=s, dst_ref=d, send_sem=send.at[k], recv_sem=recv.at[k], device_id=dev, device_id_type=MESH)

        return dict(
            direct=move(block(near), land.at[c], 0, (*near, c)),
            for_relay=move(block(chips[2]), staged, 1, (*near, c)),
            summed=move(mine, land.at[1 - c], 2, (*far, c)),
            direct_in=move(land.at[c], land.at[c], 0, (*near, c)),
            staged_in=move(staged, staged, 1, (*near, c)),
            summed_in=move(land.at[1 - c], land.at[1 - c], 2, (*far, c)),
            load_mine=pltpu.make_async_copy(block(far), mine, local.at[0]),
            load_theirs=pltpu.make_async_copy(staged, theirs, local.at[1]))

    def first(ins, outs, sems):
        cps = copies(ins, outs, sems)
        cps["for_relay"].start()
        cps["direct"].start()

    def mid(ins, outs, sems):
        cps = copies(ins, outs, sems)
        mine, theirs = sems[3], sems[4]
        cps["load_mine"].start()
        cps["staged_in"].wait_recv()
        cps["load_theirs"].start()
        cps["load_mine"].wait()
        cps["load_theirs"].wait()

        def add(i, carry):
            rs = pl.ds(pl.multiple_of(i * rows, 16), rows)
            mine[rs, :] = (mine[rs, :].astype(F32) + theirs[rs, :].astype(F32)).astype(BF)
            return carry

        lax.fori_loop(0, hr // rows, add, 0)
        cps["summed"].start()

    def last(ins, outs, sems):
        cps = copies(ins, outs, sems)
        cps["direct_in"].wait_recv()
        cps["summed_in"].wait_recv()
        for name in ("direct", "for_relay", "summed"):
            cps[name].wait_send()

    sems = [pltpu.SemaphoreType.DMA((3,)), pltpu.SemaphoreType.DMA((3,)), pltpu.SemaphoreType.DMA((2,)),
            pltpu.VMEM((hr, cc), BF), pltpu.VMEM((hr, cc), BF)]
    return _Side([p3], [jax.ShapeDtypeStruct((2, hr, cc), BF), jax.ShapeDtypeStruct((hr, cc), BF)], sems,
                 first, last, mid)


def _chip_sum(p3, land, shard_shape, axis, idx, name):
    r, c = shard_shape
    hr = r // 2
    tr = 128
    nt = hr // tr
    slots = land.shape[0]

    def body(idx_ref, p_ref, l_ref, o_ref):
        acc = p_ref[...].astype(F32)
        for k in range(slots):
            acc = acc + l_ref[k].astype(F32)
        o_ref[...] = acc

    own = (pl.BlockSpec((None, tr, c), lambda i, idx: (idx[0], i, 0)) if axis == 0
           else pl.BlockSpec((None, tr, c), lambda i, idx: (0, i, idx[0])))
    return pl.pallas_call(
        body, name=name,
        grid_spec=pltpu.PrefetchScalarGridSpec(
            num_scalar_prefetch=1, grid=(nt,),
            in_specs=[own, pl.BlockSpec((slots, tr, c), lambda i, idx: (0, i, 0))],
            out_specs=pl.BlockSpec((tr, c), lambda i, idx: (idx[1] * nt + i, 0))),
        out_shape=jax.ShapeDtypeStruct((r, c), F32),
        compiler_params=_params(("parallel",)),
    )(idx, p3, land)


def _chip_sums(p3s, lands, shapes, axes, idx, name):
    n = len(p3s)

    def body(idx_ref, *refs):
        for p_ref, l_ref, o_ref in zip(refs[:n], refs[n:2 * n], refs[2 * n:]):
            acc = p_ref[...].astype(F32)
            for k in range(l_ref.shape[0]):
                acc = acc + l_ref[k].astype(F32)
            o_ref[...] = acc

    own = [pl.BlockSpec((None, r // 2, c), (lambda i, idx: (idx[0], 0, 0)) if ax == 0 else (lambda i, idx: (0, 0, idx[0])))
           for (r, c), ax in zip(shapes, axes)]
    return pl.pallas_call(
        body, name=name,
        grid_spec=pltpu.PrefetchScalarGridSpec(
            num_scalar_prefetch=1, grid=(1,),
            in_specs=own + [pl.BlockSpec(l.shape, lambda i, idx: (0, 0, 0)) for l in lands],
            out_specs=[pl.BlockSpec((r // 2, c), lambda i, idx: (idx[1], 0)) for r, c in shapes]),
        out_shape=[jax.ShapeDtypeStruct((r, c), F32) for r, c in shapes],
        compiler_params=_params(("arbitrary",)),
    )(idx, *p3s, *lands)


def _rs_pair_gather(fulls, small):
    n = len(fulls)

    def body(*refs):
        ins, small_refs, outs, red_ref = refs[:n], refs[n:n + 5], refs[n + 5:2 * n + 5], refs[2 * n + 5]
        send, recv = refs[2 * n + 6:2 * n + 8]
        x, y, c, _ = _place()
        cps = []
        for a in range(n):
            hr = fulls[a].shape[0] // 2
            rows = pl.ds(pl.multiple_of(c * hr, 8), hr)
            cp = pltpu.make_async_remote_copy(
                src_ref=ins[a].at[rows, :], dst_ref=outs[a].at[rows, :], send_sem=send.at[a], recv_sem=recv.at[a],
                device_id=(x, y, 1 - c), device_id_type=MESH)
            cp.start()
            cps.append(cp)
        _small_all_reduce(small_refs, red_ref, *refs[2 * n + 8:])
        for a, cp in enumerate(cps):
            cp.wait_send()
            hr = fulls[a].shape[0] // 2
            other = pl.ds(pl.multiple_of((1 - c) * hr, 8), hr)
            pltpu.make_async_remote_copy(
                src_ref=ins[a].at[other, :], dst_ref=outs[a].at[other, :], send_sem=send.at[a], recv_sem=recv.at[a],
                device_id=(x, y, 1 - c), device_id_type=MESH).wait_recv()

    vm = pl.BlockSpec(memory_space=pltpu.VMEM)
    out = pl.pallas_call(
        body, name="grads_pair_gather",
        in_specs=[HBM] * n + [vm] * 5, out_specs=[HBM] * n + [vm],
        out_shape=[jax.ShapeDtypeStruct(f.shape, F32) for f in fulls] + [jax.ShapeDtypeStruct((NSMALL, D), F32)],
        input_output_aliases={a: a for a in range(n)},
        scratch_shapes=[pltpu.SemaphoreType.DMA((n,)), pltpu.SemaphoreType.DMA((n,)),
                        pltpu.VMEM((NSMALL, D), F32), pltpu.VMEM((8, NSMALL, D), F32),
                        pltpu.SemaphoreType.DMA((7,)), pltpu.SemaphoreType.DMA((7,))],
    )(*fulls, *small)
    return out[:n], out[n]


def _sibling_send_side(arr):
    def copy(ins, outs, sems):
        x, y, c, _ = _place()
        return pltpu.make_async_remote_copy(
            src_ref=ins[0], dst_ref=outs[0], send_sem=sems[0].at[0], recv_sem=sems[1].at[0],
            device_id=(x, y, 1 - c), device_id_type=MESH)

    return _Side([arr], [jax.ShapeDtypeStruct(arr.shape, arr.dtype)],
                 [pltpu.SemaphoreType.DMA((1,)), pltpu.SemaphoreType.DMA((1,))],
                 lambda ins, outs, sems: copy(ins, outs, sems).start(),
                 lambda ins, outs, sems: copy(ins, outs, sems).wait())


def _add_bf16(a, b, name):
    r, c = a.shape
    tr = 128

    def body(a_ref, b_ref, o_ref):
        o_ref[...] = (a_ref[...].astype(F32) + b_ref[...].astype(F32)).astype(BF)

    blk = pl.BlockSpec((tr, c), lambda i: (i, 0))
    return pl.pallas_call(
        body, name=name, grid=(r // tr,), in_specs=[blk, blk], out_specs=blk,
        out_shape=jax.ShapeDtypeStruct((r, c), BF), compiler_params=_params(("parallel",)),
    )(a, b)


def _rs_partials(grads, shapes, axes, tag):
    cidx = jnp.reshape(lax.axis_index("c"), (1,)).astype(jnp.int32)
    g3s = [_as3d(g, s, ax) for g, s, ax in zip(grads, shapes, axes)]
    lands = _rs_pair_exchange(g3s, f"grads_pair_exchange_{tag}")
    return list(_pair_sums(g3s, lands, cidx, f"pair_sums_{tag}"))


def _rs_finish(p3s, landed, shapes, axes, small):
    x, y, c = lax.axis_index("x"), lax.axis_index("y"), lax.axis_index("c")
    idx = jnp.stack([2 * x + y, c]).astype(jnp.int32)
    fulls = [_chip_sum(p3s[0], landed[0], shapes[0], axes[0], idx, "chip_sum_w_in")]
    fulls += _chip_sums(p3s[1:], landed[1:], shapes[1:], axes[1:], idx, "chip_sums_branches_out")
    return _rs_pair_gather(fulls, small)


NSMALL = 8


def _small_all_reduce(small_refs, out_ref, pack_ref, buf_ref, send, recv):
    nw_ref, lb_ref, hn_ref, wf_ref, ls_ref = small_refs
    x, y, c = lax.axis_index("x"), lax.axis_index("y"), lax.axis_index("c")
    me = 4 * x + 2 * y + c
    pack_ref[...] = jnp.zeros_like(pack_ref)
    pack_ref[0:1, :] = nw_ref[...]
    pack_ref[1:2, :] = lb_ref[...]
    pack_ref[2:3, 0:HK] = hn_ref[...]
    pack_ref[3:4, :] = wf_ref[...]
    pack_ref[4:5, :] = ls_ref[...]
    buf_ref[me] = pack_ref[...]
    cps = []
    for d in range(1, 8):
        dx, dy, dc = d >> 2, (d >> 1) & 1, d & 1
        peer = (1 - x if dx else x, 1 - y if dy else y, 1 - c if dc else c)
        cp = pltpu.make_async_remote_copy(
            src_ref=pack_ref, dst_ref=buf_ref.at[me], send_sem=send.at[d - 1], recv_sem=recv.at[d - 1],
            device_id=peer, device_id_type=MESH)
        cp.start()
        cps.append(cp)
    for d in range(1, 8):
        dx, dy, dc = d >> 2, (d >> 1) & 1, d & 1
        src = 4 * (1 - x if dx else x) + 2 * (1 - y if dy else y) + (1 - c if dc else c)
        pltpu.make_async_remote_copy(
            src_ref=pack_ref, dst_ref=buf_ref.at[src], send_sem=send.at[d - 1], recv_sem=recv.at[d - 1],
            device_id=(x, y, c), device_id_type=MESH).wait_recv()
    for cp in cps:
        cp.wait_send()
    acc = buf_ref[0]
    for i in range(1, 8):
        acc = acc + buf_ref[i]
    out_ref[...] = acc


def _adamw_math(w, g, m, v):
    m = B1 * m + (1.0 - B1) * g
    v = B2 * v + (1.0 - B2) * (g * g)
    m_hat = m / (1.0 - B1 ** STEP)
    v_hat = v / (1.0 - B2 ** STEP)
    return -LR * (m_hat / (jnp.sqrt(v_hat) + ADAM_EPS) + WD * w), m, v


def _adamw(w, g, m, v, name):
    r, c = w.shape
    tr = 128

    def body(w_ref, g_ref, m_ref, v_ref, d_ref, nm_ref, nv_ref, go_ref):
        g = g_ref[...]
        d_ref[...], nm_ref[...], nv_ref[...] = _adamw_math(w_ref[...], g, m_ref[...], v_ref[...])
        go_ref[...] = g

    blk = pl.BlockSpec((tr, c), lambda i: (i, 0))
    return pl.pallas_call(
        body, name=name, grid=(r // tr,), in_specs=[blk] * 4, out_specs=[blk] * 4,
        out_shape=[jax.ShapeDtypeStruct((r, c), F32)] * 4,
        compiler_params=_params(("parallel",)),
    )(w, g, m, v)


def _adamw_whole(groups, name):
    n = len(groups)

    def body(*refs):
        ins, outs = refs[:4 * n], refs[4 * n:]
        for a in range(n):
            w_ref, g_ref, m_ref, v_ref = ins[4 * a:4 * a + 4]
            g = g_ref[...]
            outs[4 * a][...], outs[4 * a + 1][...], outs[4 * a + 2][...] = _adamw_math(
                w_ref[...], g, m_ref[...], v_ref[...])
            outs[4 * a + 3][...] = g

    vm = pl.BlockSpec(memory_space=pltpu.VMEM)
    out = pl.pallas_call(
        body, name=name, in_specs=[vm] * (4 * n), out_specs=[vm] * (4 * n),
        out_shape=[jax.ShapeDtypeStruct(grp[0].shape, F32) for grp in groups for _ in range(4)],
        compiler_params=_params(),
    )(*[a for grp in groups for a in grp])
    return [out[4 * a:4 * a + 4] for a in range(n)]


def _small_update(red, lbl, params):
    def body(red_ref, *refs):
        ins, outs = refs[:12], refs[12:]
        lb = _lower_bound(ins[3][...])
        dl0 = red_ref[1:2, :] * lb * (1.0 - lb)
        row = lax.broadcasted_iota(jnp.int32, (2, D), 0)
        grads = [red_ref[0:1, :], jnp.where(row == 0, dl0, -dl0), red_ref[2:3, 0:HK], red_ref[3:4, :]]
        for i, g in enumerate(grads):
            w, m, v = ins[3 * i][...], ins[3 * i + 1][...], ins[3 * i + 2][...]
            d, nm, nv = _adamw_math(w, g, m, v)
            outs[4 * i][...] = g
            outs[4 * i + 1][...] = d
            outs[4 * i + 2][...] = nm
            outs[4 * i + 3][...] = nv
        outs[16][...] = jnp.sum(red_ref[4:5, :], axis=1, keepdims=True)

    flat = [a for p in params for a in p]
    vm = pl.BlockSpec(memory_space=pltpu.VMEM)
    shapes = [jax.ShapeDtypeStruct(p[0].shape, F32) for p in params for _ in range(4)]
    return pl.pallas_call(
        body, name="small_update",
        in_specs=[vm] * 13, out_specs=[vm] * 17,
        out_shape=shapes + [jax.ShapeDtypeStruct((1, 1), F32)],
    )(red, *flat)


def kernel(x, positions, norm_w, w_in, lb_logits, hgrn_norm_w, w_branch_a, w_branch_b, w_out, final_norm_w, loss_target, m_norm_w, m_w_in, m_lb_logits, m_hgrn_norm_w, m_w_branch_a, m_w_branch_b, m_w_out, m_final_norm_w, v_norm_w, v_w_in, v_lb_logits, v_hgrn_norm_w, v_w_branch_a, v_w_branch_b, v_w_out, v_final_norm_w):
    big_w = [w_in[0], w_branch_a[0], w_branch_b[0], w_out[0]]
    big_m = [m_w_in[0], m_w_branch_a[0], m_w_branch_b[0], m_w_out[0]]
    big_v = [v_w_in[0], v_w_branch_a[0], v_w_branch_b[0], v_w_out[0]]
    shapes = [w.shape for w in big_w]
    wf = final_norm_w.reshape(1, D)

    shards = [w.astype(BF) for w in big_w]
    loc = _local_step(x[0], positions.reshape(T, 1), norm_w, lb_logits, hgrn_norm_w, wf, loss_target[0],
                      *shards, shard_shapes=shapes)
    g_big = [loc["g_win"], loc["g_wa"], loc["g_wb"], loc["g_wout"]]
    red = loc["small_sums"]

    small = _small_update(red, lb_logits, [
        (norm_w, m_norm_w, v_norm_w), (lb_logits, m_lb_logits, v_lb_logits),
        (hgrn_norm_w, m_hgrn_norm_w, v_hgrn_norm_w),
        (wf, m_final_norm_w.reshape(1, D), v_final_norm_w.reshape(1, D))])
    loss = small[16].reshape(())
    sg, sd, sm, sv = ([small[4 * i + j] for i in range(4)] for j in range(4))
    for lst in (sg, sd, sm, sv):
        lst[3] = lst[3].reshape(D)
    per_w = list(zip(big_w, g_big, big_m, big_v))
    upd = [_adamw(*per_w[0], "adamw_w_in")] + _adamw_whole(per_w[1:], "adamw_branches_out")
    bd, bm, bv, bg = ([u[j][None] for u in upd] for j in range(4))

    def order(s, b):
        return [s[0], b[0], s[1], s[2], b[1], b[2], b[3], s[3]]

    return (loss, loc["gx"][None], *order(sg, bg), *order(sd, bd), *order(sm, bm), *order(sv, bv))
```

```python
import functools

import jax
import jax.numpy as jnp
from jax import lax
from jax.experimental import pallas as pl
from jax.experimental.pallas import tpu as pltpu

T = 2048
D = 1024
NIN = 11264
HEADS = 8
HK = 128
CH = 16
NCH = T // CH
HSTEP = 2
ATT_GROUPS = ((128, 1), (512, 4), (2048, 16))
ATT_COL0 = 4096
AG_COL0 = 8704
GATE_COL0 = 9216
EPS = 1e-6
ROPE_THETA = 10000.0
LR, B1, B2, ADAM_EPS, WD, STEP = 0.001, 0.9, 0.999, 1e-08, 0.01, 10

F32 = jnp.float32
BF = jnp.bfloat16
VMEM_LIMIT = 56 * 1024 * 1024

_NN = (((1,), (0,)), ((), ()))
_NT = (((1,), (1,)), ((), ()))
_TN = (((0,), (0,)), ((), ()))


def _dot(a, b, dims=_NN):
    return lax.dot_general(a, b, dims, preferred_element_type=F32)


def _bdot(a, b, dims=_NN):
    return lax.dot_general(a.astype(BF), b.astype(BF), dims, preferred_element_type=F32)


def _sigmoid(x):
    return jax.nn.sigmoid(x)


def _params(sem=None):
    return pltpu.CompilerParams(dimension_semantics=sem, vmem_limit_bytes=VMEM_LIMIT)


def _matmul(a, b, *, ta=False, tb=False, out_dtype=F32, tm=512, tn=512, tk=None, name, side=None):
    m = a.shape[1] if ta else a.shape[0]
    kdim = a.shape[0] if ta else a.shape[1]
    n = b.shape[0] if tb else b.shape[1]
    tk = tk or kdim
    tm, tn = min(tm, m), min(tn, n)
    nm, nn, nk = m // tm, n // tn, kdim // tk
    dims = (((0 if ta else 1,), (1 if tb else 0,)), ((), ()))
    s_arrays, s_in_specs, s_shapes, s_out_specs, s_sems = _side_io(side)
    na, no = len(s_arrays), len(s_shapes)
    nacc = 1 if nk > 1 else 0

    def body(*refs):
        a_ref, b_ref = refs[:2]
        s_ins, o_ref, s_outs = refs[2:2 + na], refs[2 + na], refs[3 + na:3 + na + no]
        scratch = refs[3 + na + no:]
        s_sem_refs = scratch[nacc:]
        i, j, k = pl.program_id(0), pl.program_id(1), pl.program_id(2)
        if side is not None:
            @pl.when((i == 0) & (j == 0) & (k == 0))
            def _():
                side.first(s_ins, s_outs, s_sem_refs)

        prod = _bdot(a_ref[...], b_ref[...], dims)
        if nk == 1:
            o_ref[...] = prod.astype(out_dtype)
        else:
            acc = scratch[0]

            @pl.when(k == 0)
            def _():
                acc[...] = prod

            @pl.when(k > 0)
            def _():
                acc[...] += prod

            @pl.when(k == nk - 1)
            def _():
                o_ref[...] = acc[...].astype(out_dtype)

        if side is not None:
            @pl.when((i == nm - 1) & (j == nn - 1) & (k == nk - 1))
            def _():
                side.last(s_ins, s_outs, s_sem_refs)

    a_spec = pl.BlockSpec((tk, tm), lambda i, j, k: (k, i)) if ta else pl.BlockSpec((tm, tk), lambda i, j, k: (i, k))
    b_spec = pl.BlockSpec((tn, tk), lambda i, j, k: (j, k)) if tb else pl.BlockSpec((tk, tn), lambda i, j, k: (k, j))
    sem = ("parallel", "parallel", "arbitrary") if side is None else ("arbitrary",) * 3
    out = pl.pallas_call(
        body, name=name, grid=(nm, nn, nk),
        in_specs=[a_spec, b_spec] + s_in_specs,
        out_specs=[pl.BlockSpec((tm, tn), lambda i, j, k: (i, j))] + s_out_specs,
        out_shape=[jax.ShapeDtypeStruct((m, n), out_dtype)] + s_shapes,
        scratch_shapes=([pltpu.VMEM((tm, tn), F32)] if nk > 1 else []) + s_sems,
        compiler_params=_params(sem),
    )(a, b, *s_arrays)
    return out[0] if side is None else (out[0], out[1:])


DZ_TILE = 512


def _part_offsets(parts):
    counts = [p.shape[1] // DZ_TILE for p in parts]
    offs = [sum(counts[:i]) for i in range(len(parts))]
    return counts, offs


def _part_spec(rows, cnt, off, tile_axis):
    def index(*g):
        return (0 if rows is None else g[0], jnp.clip(g[tile_axis] - off, 0, cnt - 1))
    return index


def _grad_w_in(h, parts):
    counts, offs = _part_offsets(parts)
    n = len(parts)

    def body(h_ref, *refs):
        o_ref = refs[n]
        j = pl.program_id(0)
        for p_ref, cnt, off in zip(refs[:n], counts, offs):
            @pl.when((j >= off) & (j < off + cnt))
            def _(p_ref=p_ref):
                o_ref[...] = _bdot(h_ref[...], p_ref[...], _TN).astype(BF)

    return pl.pallas_call(
        body, name="g_win", grid=(sum(counts),),
        in_specs=[pl.BlockSpec((T, D), lambda j: (0, 0))] +
                 [pl.BlockSpec((T, DZ_TILE), _part_spec(None, c, o, 0)) for c, o in zip(counts, offs)],
        out_specs=pl.BlockSpec((D, DZ_TILE), lambda j: (0, j)),
        out_shape=jax.ShapeDtypeStruct((D, NIN), BF),
        compiler_params=_params(("parallel",)),
    )(h, *parts)


def _grad_w_in_half(h, parts, half_idx, side=None):
    counts, offs = _part_offsets(parts)
    n = len(parts)
    nj = sum(counts)
    s_arrays, s_in_specs, s_shapes, s_out_specs, s_sems = _side_io(side)
    na, no = len(s_arrays), len(s_shapes)

    def body(idx_ref, h_ref, *refs):
        s_ins, o_ref, s_outs, s_sem_refs = refs[n:n + na], refs[n + na], refs[n + na + 1:n + na + 1 + no], refs[n + na + 1 + no:]
        j = pl.program_id(0)
        if side is not None:
            @pl.when(j == 0)
            def _():
                side.first(s_ins, s_outs, s_sem_refs)

        for p_ref, cnt, off in zip(refs[:n], counts, offs):
            @pl.when((j >= off) & (j < off + cnt))
            def _(p_ref=p_ref):
                o_ref[...] = _bdot(h_ref[...], p_ref[...], _TN).astype(BF)

        if side is not None:
            @pl.when(j == nj - 1)
            def _():
                side.last(s_ins, s_outs, s_sem_refs)

    def part_spec(cnt, off):
        return pl.BlockSpec((T, DZ_TILE), lambda j, idx: (0, jnp.clip(j - off, 0, cnt - 1)))

    out = pl.pallas_call(
        body, name="g_win_half" if side is None else "g_win_half_carrying",
        grid_spec=pltpu.PrefetchScalarGridSpec(
            num_scalar_prefetch=1, grid=(nj,),
            in_specs=[pl.BlockSpec((T, D // 2), lambda j, idx: (0, idx[0]))] +
                     [part_spec(c, o) for c, o in zip(counts, offs)] + s_in_specs,
            out_specs=[pl.BlockSpec((D // 2, DZ_TILE), lambda j, idx: (0, j))] + s_out_specs,
            scratch_shapes=s_sems),
        out_shape=[jax.ShapeDtypeStruct((D // 2, NIN), BF)] + s_shapes,
        compiler_params=_params(("parallel",) if side is None else ("arbitrary",)),
    )(half_idx, h, *parts, *s_arrays)
    return out[0] if side is None else (out[0], out[1:])


def _side_io(side):
    if side is None:
        return [], [], [], [], []
    return (side.arrays, [HBM] * len(side.arrays), side.out_shapes, [HBM] * len(side.out_shapes), side.sems)


def _grad_x(parts, w_in, x, dout, norm_w, side=None):
    counts, offs = _part_offsets(parts)
    n = len(parts)
    tm = 1024
    nm, nk = T // tm, sum(counts)
    s_arrays, s_in_specs, s_shapes, s_out_specs, s_sems = _side_io(side)
    na, no = len(s_arrays), len(s_shapes)

    def body(*refs):
        w_ref, x_ref, dout_ref, nw_ref = refs[n:n + 4]
        s_ins = refs[n + 4:n + 4 + na]
        gx_ref, gw_ref = refs[n + 4 + na:n + 6 + na]
        s_outs = refs[n + 6 + na:n + 6 + na + no]
        acc = refs[n + 6 + na + no]
        s_sem_refs = refs[n + 7 + na + no:]
        i, k = pl.program_id(0), pl.program_id(1)

        @pl.when((i == 0) & (k == 0))
        def _():
            gw_ref[...] = jnp.zeros_like(gw_ref)
            if side is not None:
                side.first(s_ins, s_outs, s_sem_refs)

        @pl.when(k == 0)
        def _():
            acc[...] = jnp.zeros_like(acc)

        if side is not None and side.mid is not None:
            @pl.when((i == nm - 1) & (k == 0))
            def _():
                side.mid(s_ins, s_outs, s_sem_refs)

        for p_ref, cnt, off in zip(refs[:n], counts, offs):
            @pl.when((k >= off) & (k < off + cnt))
            def _(p_ref=p_ref):
                acc[...] += _bdot(p_ref[...], w_ref[...], _NT)

        @pl.when(k == nk - 1)
        def _():
            gw = jnp.zeros((1, D), F32)
            for c in range(tm // BLK):
                rows = pl.ds(BLK * c, BLK)
                xv, dhv = x_ref[rows, :], acc[rows, :]
                r = lax.rsqrt(jnp.mean(xv * xv, axis=-1, keepdims=True) + EPS)
                nrm = xv * r
                dn = dhv * nw_ref[...]
                gw = gw + jnp.sum(dhv * nrm, axis=0, keepdims=True)
                gx_ref[rows, :] = dout_ref[rows, :] + r * (dn - nrm * jnp.mean(dn * nrm, axis=-1, keepdims=True))
            gw_ref[...] += gw

        if side is not None:
            @pl.when((i == nm - 1) & (k == nk - 1))
            def _():
                side.last(s_ins, s_outs, s_sem_refs)

    row = pl.BlockSpec((tm, D), lambda i, k: (i, 0))
    vec = pl.BlockSpec((1, D), lambda i, k: (0, 0))
    out = pl.pallas_call(
        body, name="grad_x", grid=(nm, nk),
        in_specs=[pl.BlockSpec((tm, DZ_TILE), _part_spec(0, c, o, 1)) for c, o in zip(counts, offs)] +
                 [pl.BlockSpec((D, DZ_TILE), lambda i, k: (0, k)), row, row, vec] + s_in_specs,
        out_specs=[row, vec] + s_out_specs,
        out_shape=[jax.ShapeDtypeStruct((T, D), F32), jax.ShapeDtypeStruct((1, D), F32)] + s_shapes,
        scratch_shapes=[pltpu.VMEM((tm, D), F32)] + s_sems,
        compiler_params=_params(("arbitrary", "arbitrary")),
    )(*parts, w_in, x, dout, norm_w, *s_arrays)
    return out[0], out[1], out[2:]


def _norm_and_rope_tables(x, w, pos, invf, side=None, own=None, cast=()):
    tm = 256
    nm = T // tm
    s_arrays, s_in_specs, s_shapes, s_out_specs, s_sems = _side_io(side)
    na, no, nc = len(s_arrays), len(s_shapes), len(cast)
    nz = 0 if own is None else 1
    wsh, blk = own if own is not None else (None, jnp.zeros((1,), jnp.int32))
    first_out = 4 + nz + nc + na

    def body(blk_ref, *refs):
        x_ref, w_ref, pos_ref, invf_ref = refs[:4]
        s_ins = refs[4 + nz + nc:first_out]
        h_ref, cos_ref, sa_ref, sb_ref = refs[first_out:first_out + 4]
        s_outs = refs[first_out + 4 + nz + nc:first_out + 4 + nz + nc + no]
        s_sem_refs = refs[first_out + 4 + nz + nc + no:]

        @pl.when(pl.program_id(0) == 0)
        def _():
            if side is not None:
                side.first(s_ins, s_outs, s_sem_refs)
            for src, dst in zip(refs[4 + nz:4 + nz + nc], refs[first_out + 4 + nz:first_out + 4 + nz + nc]):
                dst[...] = src[...].astype(BF)

        xv = x_ref[...]
        r = lax.rsqrt(jnp.mean(xv * xv, axis=-1, keepdims=True) + EPS)
        h = (xv * r * w_ref[...]).astype(BF)
        h_ref[...] = h
        if own is not None:
            refs[first_out + 4][...] = _dot(h, refs[4][...])
        first = (lax.broadcasted_iota(jnp.int32, (tm, 128), 1) % 64) < 32
        ang = pos_ref[...].astype(F32) * invf_ref[...]
        s = jnp.sin(ang)
        cos_ref[...] = jnp.cos(ang)
        sa_ref[...] = jnp.where(first, -s, 0.0)
        sb_ref[...] = jnp.where(first, 0.0, s)
        if side is not None:
            @pl.when(pl.program_id(0) == nm - 1)
            def _():
                side.last(s_ins, s_outs, s_sem_refs)

    tab = pl.BlockSpec((tm, 128), lambda i, b: (i, 0))
    own_in = [] if own is None else [pl.BlockSpec(wsh.shape, lambda i, b: (0, 0))]
    own_out = [] if own is None else [pl.BlockSpec((tm, wsh.shape[1]), lambda i, b: (i, b[0]))]
    own_shape = [] if own is None else [jax.ShapeDtypeStruct((T, NIN), F32)]
    whole = [pl.BlockSpec(a.shape, lambda i, b: (0, 0)) for a in cast]
    out = pl.pallas_call(
        body, name="norm_and_rope_tables",
        grid_spec=pltpu.PrefetchScalarGridSpec(
            num_scalar_prefetch=1, grid=(nm,),
            in_specs=[pl.BlockSpec((tm, D), lambda i, b: (i, 0)), pl.BlockSpec((1, D), lambda i, b: (0, 0)),
                      pl.BlockSpec((tm, 1), lambda i, b: (i, 0)), pl.BlockSpec((1, 128), lambda i, b: (0, 0))]
                     + own_in + whole + s_in_specs,
            out_specs=[pl.BlockSpec((tm, D), lambda i, b: (i, 0)), tab, tab, tab] + own_out + whole + s_out_specs,
            scratch_shapes=s_sems),
        out_shape=[jax.ShapeDtypeStruct((T, D), BF)] + [jax.ShapeDtypeStruct((T, 128), F32)] * 3 + own_shape
                  + [jax.ShapeDtypeStruct(a.shape, BF) for a in cast] + s_shapes,
        compiler_params=_params(("parallel",) if side is None and not cast else ("arbitrary",)),
    )(blk, x, w, pos, invf, *([] if own is None else [wsh]), *cast, *s_arrays)
    return (out[0], out[1], out[2], out[3], (out[4] if own is not None else None), out[4 + nz:4 + nz + nc],
            out[4 + nz + nc:])


def _z_blocks(h, w, z, idx, nb, side, name, fill=None):
    tm, tn = 1024, NIN // 8
    s_arrays, s_in_specs, s_shapes, s_out_specs, s_sems = _side_io(side)
    na, no = len(s_arrays), len(s_shapes)
    nm, ns = T // tm, 2 * nb
    nf = 0 if fill is None else 1

    def col(first, i, s, b):
        return (0, b[first + s // 2] * 2 + s % 2)

    def body(idx_ref, h_ref, w_ref, zin_ref, *refs):
        s_ins = refs[nf:nf + na]
        o_ref = refs[nf + na]
        s_outs = refs[nf + na + 1 + nf:nf + na + 1 + nf + no]
        s_sem_refs = refs[nf + na + 1 + nf + no + nf:]
        i, s = pl.program_id(0), pl.program_id(1)

        if side is not None:
            @pl.when((i == 0) & (s == 0))
            def _():
                side.first(s_ins, s_outs, s_sem_refs)

        if fill is not None:
            tile = pl.ds(pl.multiple_of((idx_ref[0] * 2 + s) * tn, 128), tn)
            store = pltpu.make_async_copy(w_ref, refs[nf + na + 1].at[:, tile], refs[nf + na + 1 + nf + no].at[0])
            pl.when(i == 0)(store.start)
        o_ref[...] = _dot(h_ref[...], w_ref[...])
        if fill is not None:
            pl.when(i == 0)(store.wait)

        if side is not None:
            @pl.when((i == nm - 1) & (s == ns - 1))
            def _():
                side.last(s_ins, s_outs, s_sem_refs)

    fills = [] if fill is None else [fill]
    out = pl.pallas_call(
        body, name=name,
        grid_spec=pltpu.PrefetchScalarGridSpec(
            num_scalar_prefetch=1, grid=(nm, ns),
            in_specs=[pl.BlockSpec((tm, D), lambda i, s, b: (i, 0)), pl.BlockSpec((D, tn), functools.partial(col, nb)),
                      HBM] + [HBM] * nf + s_in_specs,
            out_specs=[pl.BlockSpec((tm, tn), lambda i, s, b: (i, col(0, i, s, b)[1]))] + [HBM] * nf + s_out_specs,
            scratch_shapes=[pltpu.SemaphoreType.DMA((1,))] * nf + s_sems),
        out_shape=[jax.ShapeDtypeStruct((T, NIN), F32)] + [jax.ShapeDtypeStruct(f.shape, f.dtype) for f in fills]
                  + s_shapes,
        input_output_aliases={3: 0, **({4: 1} if fill is not None else {})},
        compiler_params=_params(("arbitrary", "arbitrary")),
    )(idx, h, w, z, *fills, *s_arrays)
    return (out[0], *out[1:1 + nf], out[1 + nf:])


def _lower_bound(lbl):
    mx = jnp.max(lbl, axis=0, keepdims=True)
    e = jnp.exp(lbl - mx)
    return e[0:1] / jnp.sum(e, axis=0, keepdims=True)


def _cumsum_rows(g, rows):
    b = g
    sh = 1
    while sh < CH:
        b = b + jnp.where(rows >= sh, pltpu.roll(b, sh, axis=0), 0.0)
        sh *= 2
    return b


def _rev_cumsum_rows(g, rows):
    b = g
    sh = 1
    while sh < CH:
        b = b + jnp.where(rows < CH - sh, pltpu.roll(b, CH - sh, axis=0), 0.0)
        sh *= 2
    return b


SUB = CH // 2


def _direct_block(qb, kb, vb, bb, rows8):
    ob = jnp.zeros_like(qb)
    for s in range(SUB):
        e_s = jnp.exp(jnp.where(rows8 >= s, bb - bb[s:s + 1], -jnp.inf))
        ob = ob + jnp.sum(qb * e_s * kb[s:s + 1], axis=1, keepdims=True) * vb[s:s + 1]
    return ob


def _direct_block_bwd(qb, kb, vb, bb, dob, rows8, rowc8):
    dq = dk = dv = db = jnp.zeros_like(qb)
    for s in range(SUB):
        one = (rowc8 == s).astype(F32)
        ks, vs = kb[s:s + 1], vb[s:s + 1]
        e_s = jnp.exp(jnp.where(rows8 >= s, bb - bb[s:s + 1], -jnp.inf))
        qes = qb * e_s
        w = qes * ks
        a = jnp.sum(w, axis=1, keepdims=True)
        da = jnp.sum(dob * vs, axis=1, keepdims=True)
        dv = dv + one * jnp.sum(a * dob, axis=0, keepdims=True)
        dq = dq + da * e_s * ks
        dk = dk + one * jnp.sum(da * qes, axis=0, keepdims=True)
        u = da * w
        db = db + u - one * jnp.sum(u, axis=0, keepdims=True)
    return dq, dk, dv, db


def _cross_factors(q, k, b):
    ref = b[SUB - 1:SUB]
    e_hi, e_lo = jnp.exp(b[SUB:] - ref), jnp.exp(ref - b[:SUB])
    return q[SUB:] * e_hi, k[:SUB] * e_lo, e_hi, e_lo


def _intra_fwd(q, k, v, b, rows8):
    lo = _direct_block(q[:SUB], k[:SUB], v[:SUB], b[:SUB], rows8)
    hi = _direct_block(q[SUB:], k[SUB:], v[SUB:], b[SUB:], rows8)
    qe_hi, ke_lo, _, _ = _cross_factors(q, k, b)
    for s in range(SUB):
        hi = hi + jnp.sum(qe_hi * ke_lo[s:s + 1], axis=1, keepdims=True) * v[s:s + 1]
    return jnp.concatenate([lo, hi], axis=0)


def _intra_bwd(q, k, v, b, do, rows8, rowc8):
    dq_lo, dk_lo, dv_lo, db_lo = _direct_block_bwd(q[:SUB], k[:SUB], v[:SUB], b[:SUB], do[:SUB], rows8, rowc8)
    dq_hi, dk_hi, dv_hi, db_hi = _direct_block_bwd(q[SUB:], k[SUB:], v[SUB:], b[SUB:], do[SUB:], rows8, rowc8)
    qe_hi, ke_lo, e_hi, e_lo = _cross_factors(q, k, b)
    do_hi, v_lo = do[SUB:], v[:SUB]
    dqe = dke = jnp.zeros_like(qe_hi)
    for s in range(SUB):
        one = (rowc8 == s).astype(F32)
        a = jnp.sum(qe_hi * ke_lo[s:s + 1], axis=1, keepdims=True)
        da = jnp.sum(do_hi * v_lo[s:s + 1], axis=1, keepdims=True)
        dv_lo = dv_lo + one * jnp.sum(a * do_hi, axis=0, keepdims=True)
        dqe = dqe + da * ke_lo[s:s + 1]
        dke = dke + one * jnp.sum(da * qe_hi, axis=0, keepdims=True)
    u_hi, u_lo = dqe * qe_hi, dke * ke_lo
    d_ref = jnp.sum(u_lo, axis=0, keepdims=True) - jnp.sum(u_hi, axis=0, keepdims=True)
    db_lo = db_lo - u_lo + (rowc8 == SUB - 1).astype(F32) * d_ref
    cat = lambda lo, hi: jnp.concatenate([lo, hi], axis=0)
    return (cat(dq_lo, dq_hi + dqe * e_hi), cat(dk_lo + dke * e_lo, dk_hi), cat(dv_lo, dv_hi),
            cat(db_lo, db_hi + u_hi))


def _hgrn_fwd(z, lbl, nw):
    def body(hq_ref, hf_ref, hi_ref, hg_ref, lbl_ref, nw_ref, oraw_ref, og_ref, sh_ref, st_ref):
        @pl.when(pl.program_id(0) == 0)
        def _():
            st_ref[...] = jnp.zeros_like(st_ref)

        lb_all = _lower_bound(lbl_ref[...])
        rows = lax.broadcasted_iota(jnp.int32, (CH, HK), 0)
        rows8 = lax.broadcasted_iota(jnp.int32, (SUB, HK), 0)
        nwv = nw_ref[...]
        for cc, h in [(cc, h) for cc in range(HSTEP) for h in range(HEADS)]:
            rs = slice(CH * cc, CH * (cc + 1))
            sl = slice(HK * h, HK * (h + 1))
            lb = lb_all[:, sl]
            hq, hf, v, hg = hq_ref[rs, sl], hf_ref[rs, sl], hi_ref[rs, sl], hg_ref[rs, sl]
            q = hq * _sigmoid(hq)
            f = lb + (1.0 - lb) * _sigmoid(hf)
            k = 1.0 - f
            b = _cumsum_rows(jnp.log(f), rows)
            sh_ref[cc, h] = st_ref[h]
            o = _bdot(q * jnp.exp(b), st_ref[h], _NT) + _intra_fwd(q, k, v, b, rows8)
            bl = b[CH - 1:CH]
            st_ref[h] = st_ref[h] * jnp.exp(bl)
            st_ref[h] += _bdot(v, k * jnp.exp(bl - b), _TN)
            oraw_ref[rs, sl] = o
            nrm = o * lax.rsqrt(jnp.mean(o * o, axis=1, keepdims=True) + EPS)
            og_ref[rs, sl] = (nrm * nwv * (hg * _sigmoid(hg))).astype(BF)

    zblk = lambda c: pl.BlockSpec((CH * HSTEP, D), lambda i, c=c: (i, c))
    return pl.pallas_call(
        body, name="hgrn_fwd", grid=(NCH // HSTEP,),
        in_specs=[zblk(0), zblk(1), zblk(2), zblk(3),
                  pl.BlockSpec((2, D), lambda i: (0, 0)), pl.BlockSpec((1, HK), lambda i: (0, 0))],
        out_specs=[zblk(0), zblk(0),
                   pl.BlockSpec((HSTEP, HEADS, HK, HK), lambda i: (i, 0, 0, 0))],
        out_shape=[jax.ShapeDtypeStruct((T, D), F32), jax.ShapeDtypeStruct((T, D), BF),
                   jax.ShapeDtypeStruct((NCH, HEADS, HK, HK), F32)],
        scratch_shapes=[pltpu.VMEM((HEADS, HK, HK), F32)],
        compiler_params=_params(("arbitrary",)),
    )(z, z, z, z, lbl, nw)


def _hgrn_bwd(z, lbl, nw, oraw, dog, shist):
    hstep = 1

    def body(hq_ref, hf_ref, hi_ref, hg_ref, lbl_ref, nw_ref, oraw_ref, dog_ref, sh_ref,
             dz_ref, dlb_ref, dnw_ref, dst_ref):
        @pl.when(pl.program_id(0) == 0)
        def _():
            dst_ref[...] = jnp.zeros_like(dst_ref)
            dlb_ref[...] = jnp.zeros_like(dlb_ref)
            dnw_ref[...] = jnp.zeros_like(dnw_ref)

        lb_all = _lower_bound(lbl_ref[...])
        rows = lax.broadcasted_iota(jnp.int32, (CH, HK), 0)
        rowc = lax.broadcasted_iota(jnp.int32, (CH, 1), 0)
        rows8 = lax.broadcasted_iota(jnp.int32, (SUB, HK), 0)
        rowc8 = lax.broadcasted_iota(jnp.int32, (SUB, 1), 0)
        nwv = nw_ref[...]
        dnw = jnp.zeros((1, HK), F32)
        for cc, h in [(cc, h) for cc in reversed(range(hstep)) for h in range(HEADS)]:
            rs = slice(CH * cc, CH * (cc + 1))
            sl = slice(HK * h, HK * (h + 1))
            lb = lb_all[:, sl]
            hq, hf, v, hg = hq_ref[rs, sl], hf_ref[rs, sl], hi_ref[rs, sl], hg_ref[rs, sl]
            o, dg_out = oraw_ref[rs, sl], dog_ref[rs, sl]
            sg = _sigmoid(hg)
            sil = hg * sg
            r = lax.rsqrt(jnp.mean(o * o, axis=1, keepdims=True) + EPS)
            nrm = o * r
            d_hg = dg_out * (nrm * nwv) * (sg * (1.0 + hg * (1.0 - sg)))
            dn = dg_out * nwv * sil
            dnw = dnw + jnp.sum(dg_out * nrm * sil, axis=0, keepdims=True)
            do = r * (dn - nrm * jnp.mean(dn * nrm, axis=1, keepdims=True))
            sq = _sigmoid(hq)
            q = hq * sq
            sig = _sigmoid(hf)
            f = lb + (1.0 - lb) * sig
            k = 1.0 - f
            b = _cumsum_rows(jnp.log(f), rows)
            eb = jnp.exp(b)
            qe = q * eb
            bl = b[CH - 1:CH]
            ebl = jnp.exp(bl)
            kdec = jnp.exp(bl - b)
            ke = k * kdec
            dqe = _bdot(do, sh_ref[cc, h])
            dq = dqe * eb
            db = dqe * qe
            dke = _bdot(v, dst_ref[h])
            dv = _bdot(ke, dst_ref[h], _NT)
            dk = dke * kdec
            rr = dke * ke
            db = db - rr
            db_last = (jnp.sum(rr, axis=0, keepdims=True)
                       + ebl * jnp.sum(dst_ref[h] * sh_ref[cc, h], axis=0, keepdims=True))
            dst_ref[h] = dst_ref[h] * ebl
            dst_ref[h] += _bdot(do, qe, _TN)
            dq_i, dk_i, dv_i, db_i = _intra_bwd(q, k, v, b, do, rows8, rowc8)
            dq, dk, dv = dq + dq_i, dk + dk_i, dv + dv_i
            db = db + db_i + (rowc == CH - 1).astype(F32) * db_last
            dgl = _rev_cumsum_rows(db, rows)
            df = dgl / f - dk
            dlb_ref[:, sl] += jnp.sum(df * (1.0 - sig), axis=0, keepdims=True)
            dz_ref[rs, sl] = (dq * (sq * (1.0 + hq * (1.0 - sq)))).astype(BF)
            dz_ref[rs, D + HK * h:D + HK * (h + 1)] = (df * (1.0 - lb) * sig * (1.0 - sig)).astype(BF)
            dz_ref[rs, 2 * D + HK * h:2 * D + HK * (h + 1)] = dv.astype(BF)
            dz_ref[rs, 3 * D + HK * h:3 * D + HK * (h + 1)] = d_hg.astype(BF)
        dnw_ref[...] += dnw

    rev = lambda i: NCH // hstep - 1 - i
    zblk = lambda c: pl.BlockSpec((CH * hstep, D), lambda i, c=c: (rev(i), c))
    return pl.pallas_call(
        body, name="hgrn_bwd", grid=(NCH // hstep,),
        in_specs=[zblk(0), zblk(1), zblk(2), zblk(3),
                  pl.BlockSpec((2, D), lambda i: (0, 0)), pl.BlockSpec((1, HK), lambda i: (0, 0)),
                  zblk(0), zblk(0),
                  pl.BlockSpec((hstep, HEADS, HK, HK), lambda i: (rev(i), 0, 0, 0))],
        out_specs=[pl.BlockSpec((CH * hstep, 4 * D), lambda i: (rev(i), 0)),
                   pl.BlockSpec((1, D), lambda i: (0, 0)), pl.BlockSpec((1, HK), lambda i: (0, 0))],
        out_shape=[jax.ShapeDtypeStruct((T, 4 * D), BF), jax.ShapeDtypeStruct((1, D), F32),
                   jax.ShapeDtypeStruct((1, HK), F32)],
        scratch_shapes=[pltpu.VMEM((HEADS, HK, HK), F32)],
        compiler_params=_params(("arbitrary",)),
    )(z, z, z, z, lbl, nw, oraw, dog, shist)


BLK = 128
NBLK = T // BLK
QK_SCALE = 0.125


def _head_masks():
    lane = lax.broadcasted_iota(jnp.int32, (1, BLK), 1)
    return [(lane < 64).astype(F32), (lane >= 64).astype(F32)]


def _pieces(dil):
    m = T // dil
    out = []
    for r in range(dil):
        for j in range(m // BLK):
            start = r + dil * BLK * j
            rows = pl.ds(start, BLK, stride=dil) if dil > 1 else pl.ds(start, BLK)
            out.append((rows, r * m + BLK * j))
    return out


def _rope(x, c, sa, sb):
    return x * c + pltpu.roll(x, 96, axis=1) * sa + pltpu.roll(x, 32, axis=1) * sb


def _rope_t(d, c, sa, sb):
    return d * c + pltpu.roll(d * sa, 32, axis=1) + pltpu.roll(d * sb, 96, axis=1)


def _rope_and_regroup(dil, q_ref, k_ref, v_ref, tables, stage_q, stage_k, qr_ref, kr_ref, vr_ref):
    cos_ref, sa_ref, sb_ref = tables
    to_q, to_k = (qr_ref, kr_ref) if dil == 1 else (stage_q, stage_k)
    for c in range(T // BLK):
        rows = pl.ds(BLK * c, BLK)
        cs, sa, sb = cos_ref[rows, :], sa_ref[rows, :], sb_ref[rows, :]
        to_q[rows, :] = (_rope(q_ref[rows, :], cs, sa, sb) * QK_SCALE).astype(to_q.dtype)
        to_k[rows, :] = _rope(k_ref[rows, :], cs, sa, sb).astype(to_k.dtype)
    for rows, dst in _pieces(dil):
        drows = pl.ds(dst, BLK)
        if dil > 1:
            qr_ref[drows, :] = stage_q[rows, :].astype(qr_ref.dtype)
            kr_ref[drows, :] = stage_k[rows, :].astype(kr_ref.dtype)
        vr_ref[drows, :] = v_ref[rows, :].astype(vr_ref.dtype)


def _window_bias(bias_ref):
    ii = lax.broadcasted_iota(jnp.int32, (2 * BLK, BLK), 0) % BLK
    jj = lax.broadcasted_iota(jnp.int32, (2 * BLK, BLK), 1)
    bias_ref[0] = jnp.where(jj <= ii, 0.0, -jnp.inf)
    bias_ref[1] = jnp.where(jj >= ii, 0.0, -jnp.inf)


def _blocks(bi):
    if isinstance(bi, int):
        return pl.ds(bi * BLK, BLK), pl.ds(max(bi - 1, 0) * BLK, BLK)
    return (pl.ds(pl.multiple_of(bi * BLK, BLK), BLK),
            pl.ds(pl.multiple_of(jnp.maximum(bi - 1, 0) * BLK, BLK), BLK))


def _stack_heads(x, masks):
    return jnp.concatenate([x * masks[0].astype(x.dtype), x * masks[1].astype(x.dtype)], axis=0).astype(BF)


def _side_steps(side, refs, **when):
    if side is None:
        return
    for stage, cond in when.items():
        if getattr(side, stage) is not None:
            pl.when(cond)(functools.partial(getattr(side, stage), *refs))


def _attn_fwd(z, cos, sa, sb, side=None):
    s_arrays, s_in_specs, s_shapes, s_out_specs, s_sems = _side_io(side)
    na, no = len(s_arrays), len(s_shapes)

    def body(q_ref, k_ref, v_ref, ag_ref, cos_ref, sa_ref, sb_ref, *refs):
        ob_ref, opre_ref, lse_ref, qr_ref, kr_ref, vr_ref = refs[na:na + 6]
        bias_ref, og_ref, lg_ref, otok_ref, ltok_ref, sc_ref = refs[na + 6 + no:na + 12 + no]
        s_refs = (refs[:na], refs[na + 6:na + 6 + no], refs[na + 12 + no:])
        p, g = pl.program_id(0), pl.program_id(1)
        _side_steps(side, s_refs, first=(p == 0) & (g == 0), mid=(p == 1) & (g == 0))
        masks = _head_masks()

        @pl.when(g == 0)
        def _():
            _window_bias(bias_ref)

        def group(gi):
            dil = ATT_GROUPS[gi][1]
            nblk = (T // dil) // BLK
            _rope_and_regroup(dil, q_ref, k_ref, v_ref, (cos_ref, sa_ref, sb_ref), lg_ref.at[0], lg_ref.at[1],
                              qr_ref, kr_ref, vr_ref)

            def scores(bi, slot):
                cur, prev = _blocks(bi)
                q2 = _stack_heads(qr_ref[cur, :], masks)
                sc_ref[slot, 0] = _dot(q2, kr_ref[cur, :], _NT) + bias_ref[0]
                if nblk > 1:
                    sc_ref[slot, 1] = (_dot(q2, kr_ref[prev, :], _NT)
                                       + (bias_ref[1] + jnp.where((bi % nblk) != 0, 0.0, -jnp.inf)))

            def finish(bi, slot):
                cur, prev = _blocks(bi)
                s_c, vc = sc_ref[slot, 0], vr_ref[cur, :]
                if nblk > 1:
                    s_p, vp = sc_ref[slot, 1], vr_ref[prev, :]
                    mx = jnp.max(jnp.maximum(s_c, s_p), axis=1, keepdims=True)
                    p_c, p_p = jnp.exp(s_c - mx), jnp.exp(s_p - mx)
                    den = jnp.sum(p_c + p_p, axis=1, keepdims=True)
                    oh = _dot(p_c.astype(BF), vc) + _dot(p_p.astype(BF), vp)
                else:
                    mx = jnp.max(s_c, axis=1, keepdims=True)
                    p_c = jnp.exp(s_c - mx)
                    den = jnp.sum(p_c, axis=1, keepdims=True)
                    oh = _dot(p_c.astype(BF), vc)
                on = oh / den
                lsev = jnp.broadcast_to(mx + jnp.log(den), (2 * BLK, BLK))
                og_ref[cur, :] = on[:BLK] * masks[0] + on[BLK:] * masks[1]
                lg_ref[0, cur, :] = lsev[:BLK]
                lg_ref[1, cur, :] = lsev[BLK:]

            def pair(j, carry):
                finish(2 * j, 0)
                scores(2 * j + 1, 1)
                finish(2 * j + 1, 1)
                scores(jnp.minimum(2 * j + 2, NBLK - 1), 0)
                return carry

            scores(0, 0)
            lax.fori_loop(0, NBLK // 2, pair, 0)
            for rows, src in _pieces(dil):
                srows = pl.ds(src, BLK)
                otok_ref[gi, rows, :] = og_ref[srows, :]
                ltok_ref[gi, 0, rows, :] = lg_ref[0, srows, :]
                ltok_ref[gi, 1, rows, :] = lg_ref[1, srows, :]

        for gi in range(3):
            pl.when(g == gi)(functools.partial(group, gi))

        @pl.when(g == 2)
        def _():
            for c in range(T // BLK):
                rows = pl.ds(BLK * c, BLK)
                wts = []
                for hh in range(2):
                    l0, l1, l2 = ltok_ref[0, hh, rows, :], ltok_ref[1, hh, rows, :], ltok_ref[2, hh, rows, :]
                    mx = jnp.maximum(jnp.maximum(l0, l1), l2)
                    e0, e1, e2 = jnp.exp(l0 - mx), jnp.exp(l1 - mx), jnp.exp(l2 - mx)
                    tot = e0 + e1 + e2
                    lse_ref[rows, BLK * hh:BLK * (hh + 1)] = mx + jnp.log(tot)
                    inv = 1.0 / tot
                    wts.append([e0 * inv, e1 * inv, e2 * inv])
                o = sum((wts[0][gi] * masks[0] + wts[1][gi] * masks[1]) * otok_ref[gi, rows, :] for gi in range(3))
                ag = ag_ref[rows, :]
                opre_ref[rows, :] = o
                ob_ref[rows, :] = (o * (ag * _sigmoid(ag))).astype(BF)

        _side_steps(side, s_refs, last=(p == 3) & (g == 2))

    c0 = ATT_COL0 // BLK
    zspec = lambda part: pl.BlockSpec((T, BLK), lambda p, g, part=part: (0, c0 + 12 * part + 4 * g + p))
    outspec = pl.BlockSpec((T, BLK), lambda p, g: (0, p))
    table = pl.BlockSpec((T, BLK), lambda p, g: (0, 0))
    regrouped = pl.BlockSpec((None, T, BLK), lambda p, g: (g, 0, p))
    big = lambda: pltpu.VMEM((T, BLK), F32)
    out = pl.pallas_call(
        body, name="attn_fwd", grid=(4, 3),
        in_specs=[zspec(0), zspec(1), zspec(2),
                  pl.BlockSpec((T, BLK), lambda p, g: (0, AG_COL0 // BLK + p)), table, table, table] + s_in_specs,
        out_specs=[outspec, outspec, pl.BlockSpec((T, 2 * BLK), lambda p, g: (0, p)), regrouped, regrouped, regrouped]
                  + s_out_specs,
        out_shape=[jax.ShapeDtypeStruct((T, 512), BF), jax.ShapeDtypeStruct((T, 512), F32),
                   jax.ShapeDtypeStruct((T, 8 * BLK), F32)] + [jax.ShapeDtypeStruct((3, T, 512), BF)] * 3 + s_shapes,
        scratch_shapes=[pltpu.VMEM((2, 2 * BLK, BLK), F32), big(),
                        pltpu.VMEM((2, T, BLK), F32), pltpu.VMEM((3, T, BLK), F32), pltpu.VMEM((3, 2, T, BLK), F32),
                        pltpu.VMEM((2, 2, 2 * BLK, BLK), F32)] + s_sems,
        compiler_params=_params(("parallel" if side is None else "arbitrary", "arbitrary")),
    )(z, z, z, z, cos, sa, sb, *s_arrays)
    return (*out[:6], out[6:])


def _attn_bwd(z, qs, ks, vs, cos, sa, sb, opre, lse, dob, side=None):
    s_arrays, s_in_specs, s_shapes, s_out_specs, s_sems = _side_io(side)
    na, no = len(s_arrays), len(s_shapes)

    def body(qs_ref, ks_ref, vs_ref, ag_ref, cos_ref, sa_ref, sb_ref, o_ref, lse0_ref, lse1_ref, dob_ref, *refs):
        dq_ref, dk_ref, dv_ref, dag_ref = refs[na:na + 4]
        (bias_ref, dtok_ref, qr_ref, kr_ref, vr_ref, dor_ref, lr_ref, dr_ref,
         dqr_ref, dkr_ref, dvr_ref, pd_ref, dotok_ref) = refs[na + 4 + no:na + 17 + no]
        s_refs = (refs[:na], refs[na + 4:na + 4 + no], refs[na + 17 + no:])
        p, g = pl.program_id(0), pl.program_id(1)
        _side_steps(side, s_refs, first=(p == 0) & (g == 0), mid=(p == 1) & (g == 0))
        masks = _head_masks()

        @pl.when(g == 0)
        def _():
            _window_bias(bias_ref)
            for c in range(T // BLK):
                rows = pl.ds(BLK * c, BLK)
                ag, dob_v, o = ag_ref[rows, :], dob_ref[rows, :], o_ref[rows, :]
                sg = _sigmoid(ag)
                dag_ref[rows, :] = (dob_v * o * (sg * (1.0 + ag * (1.0 - sg)))).astype(BF)
                do = dob_v * (ag * sg)
                dotok_ref[rows, :] = do
                prod = do * o
                for hh, mh in enumerate(masks):
                    dtok_ref[hh, rows, :] = jnp.broadcast_to(jnp.sum(prod * mh, axis=1, keepdims=True), (BLK, BLK))

        def group(gi):
            dil = ATT_GROUPS[gi][1]
            nblk = (T // dil) // BLK
            for rows, dst in _pieces(dil):
                drows = pl.ds(dst, BLK)
                dor_ref[drows, :] = dotok_ref[rows, :]
                for hh, lse_ref in enumerate((lse0_ref, lse1_ref)):
                    lr_ref[hh, drows, :] = lse_ref[rows, :]
                    dr_ref[hh, drows, :] = dtok_ref[hh, rows, :]
            dkr_ref[...] = jnp.zeros_like(dkr_ref)
            dvr_ref[...] = jnp.zeros_like(dvr_ref)

            def probs(bi, slot):
                cur, prev = _blocks(bi)
                q2, do2 = _stack_heads(qs_ref[cur, :], masks), _stack_heads(dor_ref[cur, :], masks)
                lh = jnp.concatenate([lr_ref[0, cur, :], lr_ref[1, cur, :]], axis=0)
                dh = jnp.concatenate([dr_ref[0, cur, :], dr_ref[1, cur, :]], axis=0)
                p_c = jnp.exp(_dot(q2, ks_ref[cur, :], _NT) + bias_ref[0] - lh)
                pd_ref[slot, 0] = p_c.astype(BF)
                pd_ref[slot, 1] = (p_c * (_dot(do2, vs_ref[cur, :], _NT) - dh)).astype(BF)
                if nblk > 1:
                    bias_p = bias_ref[1] + jnp.where((bi % nblk) != 0, 0.0, -jnp.inf)
                    p_p = jnp.exp(_dot(q2, ks_ref[prev, :], _NT) + bias_p - lh)
                    pd_ref[slot, 2] = p_p.astype(BF)
                    pd_ref[slot, 3] = (p_p * (_dot(do2, vs_ref[prev, :], _NT) - dh)).astype(BF)

            def grads(bi, slot):
                cur, prev = _blocks(bi)
                q2, do2 = _stack_heads(qs_ref[cur, :], masks), _stack_heads(dor_ref[cur, :], masks)
                p_c, ds_c = pd_ref[slot, 0], pd_ref[slot, 1]
                dq2 = _dot(ds_c, ks_ref[cur, :])
                dkr_ref[cur, :] += _dot(ds_c, q2, _TN)
                dvr_ref[cur, :] += _dot(p_c, do2, _TN)
                if nblk > 1:
                    p_p, ds_p = pd_ref[slot, 2], pd_ref[slot, 3]
                    dq2 = dq2 + _dot(ds_p, ks_ref[prev, :])
                    dkr_ref[prev, :] += _dot(ds_p, q2, _TN)
                    dvr_ref[prev, :] += _dot(p_p, do2, _TN)
                dqr_ref[cur, :] = dq2[:BLK] * masks[0] + dq2[BLK:] * masks[1]

            def pair(j, carry):
                grads(2 * j, 0)
                probs(2 * j + 1, 1)
                grads(2 * j + 1, 1)
                probs(jnp.minimum(2 * j + 2, NBLK - 1), 0)
                return carry

            probs(0, 0)
            lax.fori_loop(0, NBLK // 2, pair, 0)
            if dil > 1:
                for rows, src in _pieces(dil):
                    srows = pl.ds(src, BLK)
                    qr_ref[rows, :] = dqr_ref[srows, :]
                    kr_ref[rows, :] = dkr_ref[srows, :]
                    vr_ref[rows, :] = dvr_ref[srows, :]
            tq, tk, tv = (qr_ref, kr_ref, vr_ref) if dil > 1 else (dqr_ref, dkr_ref, dvr_ref)
            for c in range(T // BLK):
                rows = pl.ds(BLK * c, BLK)
                cs, sa, sb = cos_ref[rows, :], sa_ref[rows, :], sb_ref[rows, :]
                dq_ref[rows, :] = _rope_t(tq[rows, :] * QK_SCALE, cs, sa, sb).astype(BF)
                dk_ref[rows, :] = _rope_t(tk[rows, :], cs, sa, sb).astype(BF)
                dv_ref[rows, :] = tv[rows, :].astype(BF)

        for gi in range(3):
            pl.when(g == gi)(functools.partial(group, gi))
        _side_steps(side, s_refs, last=(p == 3) & (g == 2))

    regrouped = pl.BlockSpec((None, T, BLK), lambda p, g: (g, 0, p))
    pspec = pl.BlockSpec((T, BLK), lambda p, g: (0, p))
    gspec = pl.BlockSpec((T, BLK), lambda p, g: (0, 4 * g + p))
    table = pl.BlockSpec((T, BLK), lambda p, g: (0, 0))
    big = lambda: pltpu.VMEM((T, BLK), F32)
    two = lambda: pltpu.VMEM((2, T, BLK), F32)
    out = pl.pallas_call(
        body, name="attn_bwd", grid=(4, 3),
        in_specs=[regrouped, regrouped, regrouped,
                  pl.BlockSpec((T, BLK), lambda p, g: (0, AG_COL0 // BLK + p)), table, table, table,
                  pspec, pl.BlockSpec((T, BLK), lambda p, g: (0, 2 * p)),
                  pl.BlockSpec((T, BLK), lambda p, g: (0, 2 * p + 1)), pspec] + s_in_specs,
        out_specs=[gspec, gspec, gspec, pspec] + s_out_specs,
        out_shape=[jax.ShapeDtypeStruct((T, 1536), BF), jax.ShapeDtypeStruct((T, 1536), BF),
                   jax.ShapeDtypeStruct((T, 1536), BF), jax.ShapeDtypeStruct((T, 512), BF)] + s_shapes,
        scratch_shapes=[pltpu.VMEM((2, 2 * BLK, BLK), F32), two(), big(), big(), big(), big(),
                        two(), two(), big(), big(), big(), pltpu.VMEM((2, 4, 2 * BLK, BLK), BF), big()] + s_sems,
        compiler_params=_params(("parallel" if side is None else "arbitrary", "arbitrary")),
    )(qs, ks, vs, z, cos, sa, sb, opre, lse, lse, dob, *s_arrays)
    return (*out[:4], out[4:])


def _merge_out_loss(og, ob, z, w_a, w_b, w_out, x, tgt, wf):
    tm = 512

    def body(og_ref, ob_ref, ga_ref, gb_ref, wa_ref, wb_ref, wo_ref, x_ref, t_ref, wf_ref,
             m_ref, dout_ref, loss_ref, gwf_ref):
        @pl.when(pl.program_id(0) == 0)
        def _():
            loss_ref[...] = jnp.zeros_like(loss_ref)
            gwf_ref[...] = jnp.zeros_like(gwf_ref)

        ya, yb = _dot(og_ref[...], wa_ref[...]), _dot(ob_ref[...], wb_ref[...])
        m =(_sigmoid(ga_ref[...]) * ya + _sigmoid(gb_ref[...]) * yb).astype(BF)
        m_ref[...] = m
        out = x_ref[...] + _dot(m, wo_ref[...])
        r = lax.rsqrt(jnp.mean(out * out, axis=-1, keepdims=True) + EPS)
        yh = out * r
        wfv = wf_ref[...]
        err = yh * wfv - t_ref[...]
        loss_ref[...] += jnp.sum(err * err, axis=0, keepdims=True) * (0.5 / D)
        dy = err * (1.0 / D)
        gwf_ref[...] += jnp.sum(dy * yh, axis=0, keepdims=True)
        dyh = dy * wfv
        dout_ref[...] = r * (dyh - yh * jnp.mean(dyh * yh, axis=-1, keepdims=True))

    row = pl.BlockSpec((tm, D), lambda i: (i, 0))
    vec = pl.BlockSpec((1, D), lambda i: (0, 0))
    whole = lambda w: pl.BlockSpec(w.shape, lambda i: (0, 0))
    return pl.pallas_call(
        body, name="merge_out_loss", grid=(T // tm,),
        in_specs=[row, pl.BlockSpec((tm, ob.shape[1]), lambda i: (i, 0)),
                  pl.BlockSpec((tm, D), lambda i: (i, GATE_COL0 // D)),
                  pl.BlockSpec((tm, D), lambda i: (i, GATE_COL0 // D + 1)),
                  whole(w_a), whole(w_b), whole(w_out), row, row, vec],
        out_specs=[row, row, vec, vec],
        out_shape=[jax.ShapeDtypeStruct((T, D), BF), jax.ShapeDtypeStruct((T, D), F32),
                   jax.ShapeDtypeStruct((1, D), F32), jax.ShapeDtypeStruct((1, D), F32)],
        compiler_params=_params(("arbitrary",)),
    )(og, ob, z, z, w_a, w_b, w_out, x, tgt, wf)


def _merge_proj_bwd(dout, og, ob, z, w_a, w_b, w_out):
    tm = 512

    def body(dout_ref, og_ref, ob_ref, ga_ref, gb_ref, wa_ref, wb_ref, wo_ref,
             dya_ref, dyb_ref, dg_ref, dog_ref, dob_ref):
        dmv = _dot(dout_ref[...].astype(BF), wo_ref[...], _NT)
        sa, sb = _sigmoid(ga_ref[...]), _sigmoid(gb_ref[...])
        dya, dyb = (sa * dmv).astype(BF), (sb * dmv).astype(BF)
        dya_ref[...] = dya
        dyb_ref[...] = dyb
        dg_ref[:, :D] = (dmv * _dot(og_ref[...], wa_ref[...]) * sa * (1.0 - sa)).astype(BF)
        dg_ref[:, D:] = (dmv * _dot(ob_ref[...], wb_ref[...]) * sb * (1.0 - sb)).astype(BF)
        dog_ref[...] = _dot(dya, wa_ref[...], _NT)
        dob_ref[...] = _dot(dyb, wb_ref[...], _NT)

    row = pl.BlockSpec((tm, D), lambda i: (i, 0))
    whole = lambda w: pl.BlockSpec(w.shape, lambda i: (0, 0))
    nb = w_b.shape[0]
    return pl.pallas_call(
        body, name="merge_proj_bwd", grid=(T // tm,),
        in_specs=[row, row, pl.BlockSpec((tm, nb), lambda i: (i, 0)),
                  pl.BlockSpec((tm, D), lambda i: (i, GATE_COL0 // D)),
                  pl.BlockSpec((tm, D), lambda i: (i, GATE_COL0 // D + 1)), whole(w_a), whole(w_b), whole(w_out)],
        out_specs=[row, row, pl.BlockSpec((tm, 2 * D), lambda i: (i, 0)), row,
                   pl.BlockSpec((tm, nb), lambda i: (i, 0))],
        out_shape=[jax.ShapeDtypeStruct((T, D), BF), jax.ShapeDtypeStruct((T, D), BF),
                   jax.ShapeDtypeStruct((T, 2 * D), BF), jax.ShapeDtypeStruct((T, D), F32),
                   jax.ShapeDtypeStruct((T, nb), F32)],
        compiler_params=_params(("parallel",)),
    )(dout, og, ob, z, z, w_a, w_b, w_out)


def _rope_inv_freq():
    inv = ROPE_THETA ** (-jnp.arange(0, 64, 2, dtype=F32) / 64)
    return jnp.tile(inv, 4).reshape(1, BLK)


def _local_step(x, pos, norm_w, lbl, hnw, wf, tgt, w_in, w_a, w_b, w_out, shard_shapes=()):
    invf = _rope_inv_freq()
    if shard_shapes:
        blk = jnp.reshape(2 * lax.axis_index("x") + lax.axis_index("y"), (1,)).astype(jnp.int32)
        h, cos, sa, sb, z_own, (w_a, w_b, w_out), (w_near,) = _norm_and_rope_tables(
            x, norm_w, pos, invf, side=_gather_near_side(w_in, WEIGHT_AXES[0]), own=(w_in, blk),
            cast=(w_a, w_b, w_out))
        near = jnp.concatenate([blk ^ 2, blk ^ 1])
        z, (w_diag,) = _z_blocks(h, w_near, z_own, jnp.concatenate([near, near]), 2, name="z_proj_near",
                                 side=_gather_diag_side(w_near, w_in.shape, WEIGHT_AXES[0]))
        z, w_in, _ = _z_blocks(h, w_diag, z, jnp.concatenate([blk ^ 3, jnp.zeros_like(blk)]), 1, name="z_proj_diag",
                               fill=w_near, side=None)
    else:
        h, cos, sa, sb, _, _, _ = _norm_and_rope_tables(x, norm_w, pos, invf)
        z = _matmul(h, w_in, tm=T, tn=512, name="z_proj")
    oraw, og, shist = _hgrn_fwd(z, lbl, hnw)
    side_a = _gather_side([w_a, w_b, w_out], WEIGHT_AXES[1:]) if shard_shapes else None
    ob, opre, lse, qs, ks, vs, gathered = _attn_fwd(z, cos, sa, sb, side=side_a)
    if shard_shapes:
        w_a, w_b, w_out = gathered
    merged, dout, loss_vec, g_wf = _merge_out_loss(og, ob, z, w_a, w_b, w_out, x, tgt, wf)

    dya, dyb, dgates, dog, dob = _merge_proj_bwd(dout, og, ob, z, w_a, w_b, w_out)
    g_wout = _matmul(merged, dout, ta=True, out_dtype=BF, tm=512, tn=1024, name="g_wout")
    g_wa = _matmul(og, dya, ta=True, out_dtype=BF, tm=512, tn=1024, name="g_wa")
    g_wb = _matmul(ob, dyb, ta=True, out_dtype=BF, tm=512, tn=1024, name="g_wb")
    small = [g_wa, g_wb, g_wout]
    side_s = side_w = None
    if shard_shapes:
        p3_s = _rs_partials(small, shard_shapes[1:], WEIGHT_AXES[1:], "small")
        side_s = _chip_exchange_side(p3_s, shard_shapes[1:], WEIGHT_AXES[1:])
    dz_h, dlb, g_hnw = _hgrn_bwd(z, lbl, hnw, oraw, dog, shist)
    dq, dk, dv, dag, land_s = _attn_bwd(z, qs, ks, vs, cos, sa, sb, opre, lse, dob, side=side_s)
    dz_parts = [dz_h, dq, dk, dv, dag, dgates]
    if shard_shapes:
        c = lax.axis_index("c")
        half = lambda i: jnp.reshape(i, (1,)).astype(jnp.int32)
        g_send = _grad_w_in_half(h, dz_parts, half(1 - c))
        g_keep, (g_sib,) = _grad_w_in_half(h, dz_parts, half(c), side=_sibling_send_side(g_send))
        p3_w = [_add_bf16(g_keep, g_sib, "pair_sum_w_in").reshape(1, D // 2, NIN)]
        side_w = _chip_exchange_relay_side(p3_w[0], shard_shapes[0])
    else:
        g_big = [_grad_w_in(h, dz_parts)] + small
    gx, g_nw, land_w = _grad_x(dz_parts, w_in, x, dout, norm_w, side=side_w)
    small_sums = None
    if shard_shapes:
        g_big, small_sums = _rs_finish(p3_w + p3_s, [land_w[0]] + list(land_s), shard_shapes, WEIGHT_AXES,
                                       (g_nw, dlb, g_hnw, g_wf, loss_vec))
    return dict(loss_vec=loss_vec, gx=gx, g_nw=g_nw, dlb=dlb, g_hnw=g_hnw, g_wf=g_wf, small_sums=small_sums,
                g_win=g_big[0], g_wa=g_big[1], g_wb=g_big[2], g_wout=g_big[3])


MESH = pl.DeviceIdType.MESH
HBM = pl.BlockSpec(memory_space=pl.ANY)
WEIGHT_AXES = (1, 0, 1, 0)


def _place():
    x, y, c = lax.axis_index("x"), lax.axis_index("y"), lax.axis_index("c")
    chips = [(1 - x, y), (x, 1 - y), (1 - x, 1 - y)]
    return x, y, c, chips


def _block_half(ref, shard_shape, axis, j, half):
    r, c = shard_shape
    hr = r // 2
    if axis == 0:
        return ref.at[pl.ds(pl.multiple_of(j * r + half * hr, 16), hr), :]
    return ref.at[pl.ds(pl.multiple_of(half * hr, 16), hr), pl.ds(pl.multiple_of(j * c, 128), c)]


class _Side:
    def __init__(self, arrays, out_shapes, sems, first, last, mid=None):
        self.arrays, self.out_shapes, self.sems, self.first, self.last = arrays, out_shapes, sems, first, last
        self.mid = mid


def _gather_side(shards, axes):
    n = len(shards)
    shapes = [s.shape for s in shards]

    def copies(ins, outs, sems):
        send1, recv1, send2, recv2, send0, recv0 = sems
        x, y, c, chips = _place()
        me = 2 * x + y
        sib = (x, y, 1 - c)
        near = ((1 - c) * (1 - x) + c * x, (1 - c) * y + c * (1 - y))
        far = ((1 - c) * x + c * (1 - x), (1 - c) * (1 - y) + c * y)
        out = []
        for a in range(n):
            r, cc = shapes[a]
            mine = (outs[a].at[pl.ds(pl.multiple_of(me * r, 16), r), :] if axes[a] == 0
                    else outs[a].at[:, pl.ds(pl.multiple_of(me * cc, 128), cc)])
            own = pltpu.make_async_remote_copy(
                src_ref=ins[a], dst_ref=mine, send_sem=send0.at[a], recv_sem=recv0.at[a],
                device_id=sib, device_id_type=MESH)
            src = ins[a].at[pl.ds(pl.multiple_of(c * (r // 2), 16), r // 2), :]
            sends = [pltpu.make_async_remote_copy(
                src_ref=src, dst_ref=_block_half(outs[a], shapes[a], axes[a], me, c),
                send_sem=send1.at[a, k], recv_sem=recv1.at[a, k], device_id=(*chips[k], c), device_id_type=MESH)
                for k in range(2)]

            def region(chip, half):
                return _block_half(outs[a], shapes[a], axes[a], 2 * chip[0] + chip[1], half)

            def arrival(chip, k):
                reg = region(chip, c)
                return pltpu.make_async_remote_copy(
                    src_ref=reg, dst_ref=reg, send_sem=send1.at[a, k], recv_sem=recv1.at[a, k],
                    device_id=(*chip, c), device_id_type=MESH)

            def to_sibling(chip, k):
                reg = region(chip, c)
                return pltpu.make_async_remote_copy(
                    src_ref=reg, dst_ref=reg, send_sem=send2.at[a, k], recv_sem=recv2.at[a, k],
                    device_id=sib, device_id_type=MESH)

            def from_sibling(chip, k):
                reg = region(chip, 1 - c)
                return pltpu.make_async_remote_copy(
                    src_ref=reg, dst_ref=reg, send_sem=send2.at[a, k], recv_sem=recv2.at[a, k],
                    device_id=sib, device_id_type=MESH)

            relay = pltpu.make_async_remote_copy(
                src_ref=region(near, c), dst_ref=region(near, c), send_sem=send1.at[a, 2], recv_sem=recv1.at[a, 2],
                device_id=(*far, c), device_id_type=MESH)
            hops = [(arrival(near, c), to_sibling(near, c)), (arrival(far, 1 - c), to_sibling(far, 1 - c)),
                    (arrival(chips[2], 2), to_sibling(chips[2], 2))]
            back = [from_sibling(chips[k], k) for k in range(3)]
            out.append((own, sends, relay, hops, back))
        return out

    def first(ins, outs, sems):
        for own, sends, _, _, _ in copies(ins, outs, sems):
            own.start()
            for cp in sends:
                cp.start()

    def mid(ins, outs, sems):
        per_array = copies(ins, outs, sems)
        for step in range(2):
            for _, _, relay, hops, _ in per_array:
                arrived, onward = hops[step]
                arrived.wait_recv()
                if step == 0:
                    relay.start()
                onward.start()

    def last(ins, outs, sems):
        per_array = copies(ins, outs, sems)
        for _, _, _, hops, _ in per_array:
            arrived, onward = hops[2]
            arrived.wait_recv()
            onward.start()
        for own, sends, relay, hops, back in per_array:
            for cp in back:
                cp.wait_recv()
            for cp in sends + [relay] + [onward for _, onward in hops]:
                cp.wait_send()
            own.wait()

    full = [(4 * r, c) if ax == 0 else (r, 4 * c) for (r, c), ax in zip(shapes, axes)]
    sems = [pltpu.SemaphoreType.DMA((n, 3)), pltpu.SemaphoreType.DMA((n, 3)),
            pltpu.SemaphoreType.DMA((n, 3)), pltpu.SemaphoreType.DMA((n, 3)),
            pltpu.SemaphoreType.DMA((n,)), pltpu.SemaphoreType.DMA((n,))]
    return _Side(list(shards), [jax.ShapeDtypeStruct(f, BF) for f in full], sems, first, last, mid)


def _gather_near_side(shard, axis):
    shape = shard.shape
    r, cc = shape

    def copies(ins, outs, sems):
        send1, recv1, send2, recv2, send0, recv0 = sems
        x, y, c, chips = _place()
        me = 2 * x + y
        sib = (x, y, 1 - c)
        mine = (outs[0].at[pl.ds(pl.multiple_of(me * r, 16), r), :] if axis == 0
                else outs[0].at[:, pl.ds(pl.multiple_of(me * cc, 128), cc)])
        own = pltpu.make_async_remote_copy(
            src_ref=ins[0], dst_ref=mine, send_sem=send0.at[0], recv_sem=recv0.at[0],
            device_id=sib, device_id_type=MESH)
        src = ins[0].at[pl.ds(pl.multiple_of(c * (r // 2), 16), r // 2), :]

        def region(k, half):
            return _block_half(outs[0], shape, axis, 2 * chips[k][0] + chips[k][1], half)

        def moves(k):
            return [pltpu.make_async_remote_copy(
                        src_ref=s, dst_ref=d, send_sem=ss.at[k], recv_sem=rs.at[k], device_id=dev,
                        device_id_type=MESH)
                    for s, d, ss, rs, dev in (
                        (src, _block_half(outs[0], shape, axis, me, c), send1, recv1, (*chips[k], c)),
                        (region(k, c), region(k, c), send1, recv1, (*chips[k], c)),
                        (region(k, c), region(k, c), send2, recv2, sib),
                        (region(k, 1 - c), region(k, 1 - c), send2, recv2, sib))]

        return own, [moves(k) for k in range(2)]

    def first(ins, outs, sems):
        own, per_chip = copies(ins, outs, sems)
        own.start()
        for send, _, _, _ in per_chip:
            send.start()

    def last(ins, outs, sems):
        own, per_chip = copies(ins, outs, sems)
        for _, arrived, onward, _ in per_chip:
            arrived.wait_recv()
            onward.start()
        for send, _, onward, back in per_chip:
            back.wait_recv()
            send.wait_send()
            onward.wait_send()
        own.wait()

    full = (4 * r, cc) if axis == 0 else (r, 4 * cc)
    sems = [pltpu.SemaphoreType.DMA((2,))] * 4 + [pltpu.SemaphoreType.DMA((1,))] * 2
    return _Side([shard], [jax.ShapeDtypeStruct(full, BF)], sems, first, last)


def _gather_diag_side(gathered, shape, axis):
    r, cc = shape

    def copies(ins, outs, sems):
        send1, recv1, send2, recv2 = sems
        x, y, c, _ = _place()
        sib = (x, y, 1 - c)
        near = ((1 - c) * (1 - x) + c * x, (1 - c) * y + c * (1 - y))
        far = ((1 - c) * x + c * (1 - x), (1 - c) * (1 - y) + c * y)

        def half(i):
            return outs[0].at[pl.ds(pl.multiple_of(i * (r // 2), 16), r // 2), :]

        def move(s, d, ss, rs, dev):
            return pltpu.make_async_remote_copy(
                src_ref=s, dst_ref=d, send_sem=ss.at[0], recv_sem=rs.at[0], device_id=dev, device_id_type=MESH)

        relay = move(_block_half(ins[0], shape, axis, 2 * near[0] + near[1], c), half(c), send1, recv1, (*far, c))
        arrived = move(half(c), half(c), send1, recv1, (*far, c))
        onward = move(half(c), half(c), send2, recv2, sib)
        back = move(half(1 - c), half(1 - c), send2, recv2, sib)
        return relay, arrived, onward, back

    def first(ins, outs, sems):
        copies(ins, outs, sems)[0].start()

    def last(ins, outs, sems):
        relay, arrived, onward, back = copies(ins, outs, sems)
        arrived.wait_recv()
        onward.start()
        back.wait_recv()
        relay.wait_send()
        onward.wait_send()

    return _Side([gathered], [jax.ShapeDtypeStruct(shape, BF)], [pltpu.SemaphoreType.DMA((1,))] * 4, first, last)


def _as3d(g, shard_shape, axis):
    r, c = shard_shape
    return g.reshape(4, r, c) if axis == 0 else g.reshape(1, r, 4 * c)


def _half_rows(ref3, hr, half):
    return ref3.at[:, pl.ds(pl.multiple_of(half * hr, 16), hr), :]


def _rs_pair_exchange(g3s, name):
    n = len(g3s)

    def body(*refs):
        ins, outs = refs[:n], refs[n:2 * n]
        send, recv = refs[2 * n:]
        x, y, c, _ = _place()
        cps = []
        for a in range(n):
            hr = g3s[a].shape[1] // 2
            cp = pltpu.make_async_remote_copy(
                src_ref=_half_rows(ins[a], hr, 1 - c), dst_ref=outs[a],
                send_sem=send.at[a], recv_sem=recv.at[a], device_id=(x, y, 1 - c), device_id_type=MESH)
            cp.start()
            cps.append(cp)
        for cp in cps:
            cp.wait()

    return pl.pallas_call(
        body, name=name,
        in_specs=[HBM] * n, out_specs=[HBM] * n,
        out_shape=[jax.ShapeDtypeStruct((g.shape[0], g.shape[1] // 2, g.shape[2]), BF) for g in g3s],
        scratch_shapes=[pltpu.SemaphoreType.DMA((n,)), pltpu.SemaphoreType.DMA((n,))],
    )(*g3s)


def _pair_sums(g3s, lands, cidx, name):
    n = len(g3s)

    def body(c_ref, *refs):
        for g_ref, l_ref, o_ref in zip(refs[:n], refs[n:2 * n], refs[2 * n:]):
            o_ref[...] = (g_ref[...].astype(F32) + l_ref[...].astype(F32)).astype(BF)

    halves = [(g.shape[0], g.shape[1] // 2, g.shape[2]) for g in g3s]
    return pl.pallas_call(
        body, name=name,
        grid_spec=pltpu.PrefetchScalarGridSpec(
            num_scalar_prefetch=1, grid=(1,),
            in_specs=[pl.BlockSpec(h, lambda i, c: (0, c[0], 0)) for h in halves]
                     + [pl.BlockSpec(h, lambda i, c: (0, 0, 0)) for h in halves],
            out_specs=[pl.BlockSpec(h, lambda i, c: (0, 0, 0)) for h in halves]),
        out_shape=[jax.ShapeDtypeStruct(h, BF) for h in halves],
        compiler_params=_params(("arbitrary",)),
    )(cidx, *g3s, *lands)


def _chip_exchange_side(p3s, shapes, axes):
    n = len(p3s)

    def copies(ins, outs, sems):
        send, recv = sems
        x, y, c, chips = _place()
        cps = []
        for a in range(n):
            r, cc = shapes[a]
            for k, (px, py) in enumerate(chips):
                j = 2 * px + py
                src = ins[a].at[j] if axes[a] == 0 else ins[a].at[0, :, pl.ds(pl.multiple_of(j * cc, 128), cc)]
                cps.append(pltpu.make_async_remote_copy(
                    src_ref=src, dst_ref=outs[a].at[k], send_sem=send.at[a, k], recv_sem=recv.at[a, k],
                    device_id=(px, py, c), device_id_type=MESH))
        return cps

    def first(ins, outs, sems):
        for cp in copies(ins, outs, sems):
            cp.start()

    def last(ins, outs, sems):
        for cp in copies(ins, outs, sems):
            cp.wait()

    return _Side(list(p3s), [jax.ShapeDtypeStruct((3, r // 2, c), BF) for r, c in shapes],
                 [pltpu.SemaphoreType.DMA((n, 3)), pltpu.SemaphoreType.DMA((n, 3))], first, last)


def _chip_exchange_relay_side(p3, shape):
    r, cc = shape
    hr = r // 2
    rows = 64

    def copies(ins, outs, sems):
        send, recv, local, mine, theirs = sems
        x, y, c, chips = _place()
        near = ((1 - c) * (1 - x) + c * x, (1 - c) * y + c * (1 - y))
        far = ((1 - c) * x + c * (1 - x), (1 - c) * (1 - y) + c * y)
        land, staged = outs

        def block(chip):
            return ins[0].at[0, :, pl.ds(pl.multiple_of((2 * chip[0] + chip[1]) * cc, 128), cc)]

        def move(s, d, k, dev):
            return pltpu.make_async_remote_copy(
                src_ref=s, dst_ref=d, send_sem=send.at[k], recv_sem=recv.at[k], device_id=dev, device_id_type=MESH)

        return dict(
            direct=move(block(near), land.at[c], 0, (*near, c)),
            for_relay=move(block(chips[2]), staged, 1, (*near, c)),
            summed=move(mine, land.at[1 - c], 2, (*far, c)),
            direct_in=move(land.at[c], land.at[c], 0, (*near, c)),
            staged_in=move(staged, staged, 1, (*near, c)),
            summed_in=move(land.at[1 - c], land.at[1 - c], 2, (*far, c)),
            load_mine=pltpu.make_async_copy(block(far), mine, local.at[0]),
            load_theirs=pltpu.make_async_copy(staged, theirs, local.at[1]))

    def first(ins, outs, sems):
        cps = copies(ins, outs, sems)
        cps["for_relay"].start()
        cps["direct"].start()

    def mid(ins, outs, sems):
        cps = copies(ins, outs, sems)
        mine, theirs = sems[3], sems[4]
        cps["load_mine"].start()
        cps["staged_in"].wait_recv()
        cps["load_theirs"].start()
        cps["load_mine"].wait()
        cps["load_theirs"].wait()

        def add(i, carry):
            rs = pl.ds(pl.multiple_of(i * rows, 16), rows)
            mine[rs, :] = (mine[rs, :].astype(F32) + theirs[rs, :].astype(F32)).astype(BF)
            return carry

        lax.fori_loop(0, hr // rows, add, 0)
        cps["summed"].start()

    def last(ins, outs, sems):
        cps = copies(ins, outs, sems)
        cps["direct_in"].wait_recv()
        cps["summed_in"].wait_recv()
        for name in ("direct", "for_relay", "summed"):
            cps[name].wait_send()

    sems = [pltpu.SemaphoreType.DMA((3,)), pltpu.SemaphoreType.DMA((3,)), pltpu.SemaphoreType.DMA((2,)),
            pltpu.VMEM((hr, cc), BF), pltpu.VMEM((hr, cc), BF)]
    return _Side([p3], [jax.ShapeDtypeStruct((2, hr, cc), BF), jax.ShapeDtypeStruct((hr, cc), BF)], sems,
                 first, last, mid)


def _chip_sum(p3, land, shard_shape, axis, idx, name):
    r, c = shard_shape
    hr = r // 2
    tr = 128
    nt = hr // tr
    slots = land.shape[0]

    def body(idx_ref, p_ref, l_ref, o_ref):
        acc = p_ref[...].astype(F32)
        for k in range(slots):
            acc = acc + l_ref[k].astype(F32)
        o_ref[...] = acc

    own = (pl.BlockSpec((None, tr, c), lambda i, idx: (idx[0], i, 0)) if axis == 0
           else pl.BlockSpec((None, tr, c), lambda i, idx: (0, i, idx[0])))
    return pl.pallas_call(
        body, name=name,
        grid_spec=pltpu.PrefetchScalarGridSpec(
            num_scalar_prefetch=1, grid=(nt,),
            in_specs=[own, pl.BlockSpec((slots, tr, c), lambda i, idx: (0, i, 0))],
            out_specs=pl.BlockSpec((tr, c), lambda i, idx: (idx[1] * nt + i, 0))),
        out_shape=jax.ShapeDtypeStruct((r, c), F32),
        compiler_params=_params(("parallel",)),
    )(idx, p3, land)


def _chip_sums(p3s, lands, shapes, axes, idx, name):
    n = len(p3s)

    def body(idx_ref, *refs):
        for p_ref, l_ref, o_ref in zip(refs[:n], refs[n:2 * n], refs[2 * n:]):
            acc = p_ref[...].astype(F32)
            for k in range(l_ref.shape[0]):
                acc = acc + l_ref[k].astype(F32)
            o_ref[...] = acc

    own = [pl.BlockSpec((None, r // 2, c), (lambda i, idx: (idx[0], 0, 0)) if ax == 0 else (lambda i, idx: (0, 0, idx[0])))
           for (r, c), ax in zip(shapes, axes)]
    return pl.pallas_call(
        body, name=name,
        grid_spec=pltpu.PrefetchScalarGridSpec(
            num_scalar_prefetch=1, grid=(1,),
            in_specs=own + [pl.BlockSpec(l.shape, lambda i, idx: (0, 0, 0)) for l in lands],
            out_specs=[pl.BlockSpec((r // 2, c), lambda i, idx: (idx[1], 0)) for r, c in shapes]),
        out_shape=[jax.ShapeDtypeStruct((r, c), F32) for r, c in shapes],
        compiler_params=_params(("arbitrary",)),
    )(idx, *p3s, *lands)


def _rs_pair_gather(fulls, small):
    n = len(fulls)

    def body(*refs):
        ins, small_refs, outs, red_ref = refs[:n], refs[n:n + 5], refs[n + 5:2 * n + 5], refs[2 * n + 5]
        send, recv = refs[2 * n + 6:2 * n + 8]
        x, y, c, _ = _place()
        cps = []
        for a in range(n):
            hr = fulls[a].shape[0] // 2
            rows = pl.ds(pl.multiple_of(c * hr, 8), hr)
            cp = pltpu.make_async_remote_copy(
                src_ref=ins[a].at[rows, :], dst_ref=outs[a].at[rows, :], send_sem=send.at[a], recv_sem=recv.at[a],
                device_id=(x, y, 1 - c), device_id_type=MESH)
            cp.start()
            cps.append(cp)
        _small_all_reduce(small_refs, red_ref, *refs[2 * n + 8:])
        for a, cp in enumerate(cps):
            cp.wait_send()
            hr = fulls[a].shape[0] // 2
            other = pl.ds(pl.multiple_of((1 - c) * hr, 8), hr)
            pltpu.make_async_remote_copy(
                src_ref=ins[a].at[other, :], dst_ref=outs[a].at[other, :], send_sem=send.at[a], recv_sem=recv.at[a],
                device_id=(x, y, 1 - c), device_id_type=MESH).wait_recv()

    vm = pl.BlockSpec(memory_space=pltpu.VMEM)
    out = pl.pallas_call(
        body, name="grads_pair_gather",
        in_specs=[HBM] * n + [vm] * 5, out_specs=[HBM] * n + [vm],
        out_shape=[jax.ShapeDtypeStruct(f.shape, F32) for f in fulls] + [jax.ShapeDtypeStruct((NSMALL, D), F32)],
        input_output_aliases={a: a for a in range(n)},
        scratch_shapes=[pltpu.SemaphoreType.DMA((n,)), pltpu.SemaphoreType.DMA((n,)),
                        pltpu.VMEM((NSMALL, D), F32), pltpu.VMEM((8, NSMALL, D), F32),
                        pltpu.SemaphoreType.DMA((7,)), pltpu.SemaphoreType.DMA((7,))],
    )(*fulls, *small)
    return out[:n], out[n]


def _sibling_send_side(arr):
    def copy(ins, outs, sems):
        x, y, c, _ = _place()
        return pltpu.make_async_remote_copy(
            src_ref=ins[0], dst_ref=outs[0], send_sem=sems[0].at[0], recv_sem=sems[1].at[0],
            device_id=(x, y, 1 - c), device_id_type=MESH)

    return _Side([arr], [jax.ShapeDtypeStruct(arr.shape, arr.dtype)],
                 [pltpu.SemaphoreType.DMA((1,)), pltpu.SemaphoreType.DMA((1,))],
                 lambda ins, outs, sems: copy(ins, outs, sems).start(),
                 lambda ins, outs, sems: copy(ins, outs, sems).wait())


def _add_bf16(a, b, name):
    r, c = a.shape
    tr = 128

    def body(a_ref, b_ref, o_ref):
        o_ref[...] = (a_ref[...].astype(F32) + b_ref[...].astype(F32)).astype(BF)

    blk = pl.BlockSpec((tr, c), lambda i: (i, 0))
    return pl.pallas_call(
        body, name=name, grid=(r // tr,), in_specs=[blk, blk], out_specs=blk,
        out_shape=jax.ShapeDtypeStruct((r, c), BF), compiler_params=_params(("parallel",)),
    )(a, b)


def _rs_partials(grads, shapes, axes, tag):
    cidx = jnp.reshape(lax.axis_index("c"), (1,)).astype(jnp.int32)
    g3s = [_as3d(g, s, ax) for g, s, ax in zip(grads, shapes, axes)]
    lands = _rs_pair_exchange(g3s, f"grads_pair_exchange_{tag}")
    return list(_pair_sums(g3s, lands, cidx, f"pair_sums_{tag}"))


def _rs_finish(p3s, landed, shapes, axes, small):
    x, y, c = lax.axis_index("x"), lax.axis_index("y"), lax.axis_index("c")
    idx = jnp.stack([2 * x + y, c]).astype(jnp.int32)
    fulls = [_chip_sum(p3s[0], landed[0], shapes[0], axes[0], idx, "chip_sum_w_in")]
    fulls += _chip_sums(p3s[1:], landed[1:], shapes[1:], axes[1:], idx, "chip_sums_branches_out")
    return _rs_pair_gather(fulls, small)


NSMALL = 8


def _small_all_reduce(small_refs, out_ref, pack_ref, buf_ref, send, recv):
    nw_ref, lb_ref, hn_ref, wf_ref, ls_ref = small_refs
    x, y, c = lax.axis_index("x"), lax.axis_index("y"), lax.axis_index("c")
    me = 4 * x + 2 * y + c
    pack_ref[...] = jnp.zeros_like(pack_ref)
    pack_ref[0:1, :] = nw_ref[...]
    pack_ref[1:2, :] = lb_ref[...]
    pack_ref[2:3, 0:HK] = hn_ref[...]
    pack_ref[3:4, :] = wf_ref[...]
    pack_ref[4:5, :] = ls_ref[...]
    buf_ref[me] = pack_ref[...]
    cps = []
    for d in range(1, 8):
        dx, dy, dc = d >> 2, (d >> 1) & 1, d & 1
        peer = (1 - x if dx else x, 1 - y if dy else y, 1 - c if dc else c)
        cp = pltpu.make_async_remote_copy(
            src_ref=pack_ref, dst_ref=buf_ref.at[me], send_sem=send.at[d - 1], recv_sem=recv.at[d - 1],
            device_id=peer, device_id_type=MESH)
        cp.start()
        cps.append(cp)
    for d in range(1, 8):
        dx, dy, dc = d >> 2, (d >> 1) & 1, d & 1
        src = 4 * (1 - x if dx else x) + 2 * (1 - y if dy else y) + (1 - c if dc else c)
        pltpu.make_async_remote_copy(
            src_ref=pack_ref, dst_ref=buf_ref.at[src], send_sem=send.at[d - 1], recv_sem=recv.at[d - 1],
            device_id=(x, y, c), device_id_type=MESH).wait_recv()
    for cp in cps:
        cp.wait_send()
    acc = buf_ref[0]
    for i in range(1, 8):
        acc = acc + buf_ref[i]
    out_ref[...] = acc


def _adamw_math(w, g, m, v):
    m = B1 * m + (1.0 - B1) * g
    v = B2 * v + (1.0 - B2) * (g * g)
    m_hat = m / (1.0 - B1 ** STEP)
    v_hat = v / (1.0 - B2 ** STEP)
    return -LR * (m_hat / (jnp.sqrt(v_hat) + ADAM_EPS) + WD * w), m, v


def _adamw(w, g, m, v, name):
    r, c = w.shape
    tr = 128

    def body(w_ref, g_ref, m_ref, v_ref, d_ref, nm_ref, nv_ref, go_ref):
        g = g_ref[...]
        d_ref[...], nm_ref[...], nv_ref[...] = _adamw_math(w_ref[...], g, m_ref[...], v_ref[...])
        go_ref[...] = g

    blk = pl.BlockSpec((tr, c), lambda i: (i, 0))
    return pl.pallas_call(
        body, name=name, grid=(r // tr,), in_specs=[blk] * 4, out_specs=[blk] * 4,
        out_shape=[jax.ShapeDtypeStruct((r, c), F32)] * 4,
        compiler_params=_params(("parallel",)),
    )(w, g, m, v)


def _adamw_whole(groups, name):
    n = len(groups)

    def body(*refs):
        ins, outs = refs[:4 * n], refs[4 * n:]
        for a in range(n):
            w_ref, g_ref, m_ref, v_ref = ins[4 * a:4 * a + 4]
            g = g_ref[...]
            outs[4 * a][...], outs[4 * a + 1][...], outs[4 * a + 2][...] = _adamw_math(
                w_ref[...], g, m_ref[...], v_ref[...])
            outs[4 * a + 3][...] = g

    vm = pl.BlockSpec(memory_space=pltpu.VMEM)
    out = pl.pallas_call(
        body, name=name, in_specs=[vm] * (4 * n), out_specs=[vm] * (4 * n),
        out_shape=[jax.ShapeDtypeStruct(grp[0].shape, F32) for grp in groups for _ in range(4)],
        compiler_params=_params(),
    )(*[a for grp in groups for a in grp])
    return [out[4 * a:4 * a + 4] for a in range(n)]


def _small_update(red, lbl, params):
    def body(red_ref, *refs):
        ins, outs = refs[:12], refs[12:]
        lb = _lower_bound(ins[3][...])
        dl0 = red_ref[1:2, :] * lb * (1.0 - lb)
        row = lax.broadcasted_iota(jnp.int32, (2, D), 0)
        grads = [red_ref[0:1, :], jnp.where(row == 0, dl0, -dl0), red_ref[2:3, 0:HK], red_ref[3:4, :]]
        for i, g in enumerate(grads):
            w, m, v = ins[3 * i][...], ins[3 * i + 1][...], ins[3 * i + 2][...]
            d, nm, nv = _adamw_math(w, g, m, v)
            outs[4 * i][...] = g
            outs[4 * i + 1][...] = d
            outs[4 * i + 2][...] = nm
            outs[4 * i + 3][...] = nv
        outs[16][...] = jnp.sum(red_ref[4:5, :], axis=1, keepdims=True)

    flat = [a for p in params for a in p]
    vm = pl.BlockSpec(memory_space=pltpu.VMEM)
    shapes = [jax.ShapeDtypeStruct(p[0].shape, F32) for p in params for _ in range(4)]
    return pl.pallas_call(
        body, name="small_update",
        in_specs=[vm] * 13, out_specs=[vm] * 17,
        out_shape=shapes + [jax.ShapeDtypeStruct((1, 1), F32)],
    )(red, *flat)


def kernel(x, positions, norm_w, w_in, lb_logits, hgrn_norm_w, w_branch_a, w_branch_b, w_out, final_norm_w, loss_target, m_norm_w, m_w_in, m_lb_logits, m_hgrn_norm_w, m_w_branch_a, m_w_branch_b, m_w_out, m_final_norm_w, v_norm_w, v_w_in, v_lb_logits, v_hgrn_norm_w, v_w_branch_a, v_w_branch_b, v_w_out, v_final_norm_w):
    big_w = [w_in[0], w_branch_a[0], w_branch_b[0], w_out[0]]
    big_m = [m_w_in[0], m_w_branch_a[0], m_w_branch_b[0], m_w_out[0]]
    big_v = [v_w_in[0], v_w_branch_a[0], v_w_branch_b[0], v_w_out[0]]
    shapes = [w.shape for w in big_w]
    wf = final_norm_w.reshape(1, D)

    shards = [big_w[0].astype(BF)] + big_w[1:]
    loc = _local_step(x[0], positions.reshape(T, 1), norm_w, lb_logits, hgrn_norm_w, wf, loss_target[0],
                      *shards, shard_shapes=shapes)
    g_big = [loc["g_win"], loc["g_wa"], loc["g_wb"], loc["g_wout"]]
    red = loc["small_sums"]

    small = _small_update(red, lb_logits, [
        (norm_w, m_norm_w, v_norm_w), (lb_logits, m_lb_logits, v_lb_logits),
        (hgrn_norm_w, m_hgrn_norm_w, v_hgrn_norm_w),
        (wf, m_final_norm_w.reshape(1, D), v_final_norm_w.reshape(1, D))])
    loss = small[16].reshape(())
    sg, sd, sm, sv = ([small[4 * i + j] for i in range(4)] for j in range(4))
    for lst in (sg, sd, sm, sv):
        lst[3] = lst[3].reshape(D)
    per_w = list(zip(big_w, g_big, big_m, big_v))
    upd = [_adamw(*per_w[0], "adamw_w_in")] + _adamw_whole(per_w[1:], "adamw_branches_out")
    bd, bm, bv, bg = ([u[j][None] for u in upd] for j in range(4))

    def order(s, b):
        return [s[0], b[0], s[1], s[2], b[1], b[2], b[3], s[3]]

    return (loss, loc["gx"][None], *order(sg, bg), *order(sd, bd), *order(sm, bm), *order(sv, bv))
```

```python
import functools

import jax
import jax.numpy as jnp
from jax import lax
from jax.experimental import pallas as pl
from jax.experimental.pallas import tpu as pltpu

T = 2048
D = 1024
NIN = 11264
HEADS = 8
HK = 128
CH = 16
NCH = T // CH
HSTEP = 2
ATT_GROUPS = ((128, 1), (512, 4), (2048, 16))
ATT_COL0 = 4096
AG_COL0 = 8704
GATE_COL0 = 9216
EPS = 1e-6
ROPE_THETA = 10000.0
LR, B1, B2, ADAM_EPS, WD, STEP = 0.001, 0.9, 0.999, 1e-08, 0.01, 10

F32 = jnp.float32
BF = jnp.bfloat16
VMEM_LIMIT = 56 * 1024 * 1024

_NN = (((1,), (0,)), ((), ()))
_NT = (((1,), (1,)), ((), ()))
_TN = (((0,), (0,)), ((), ()))


def _dot(a, b, dims=_NN):
    return lax.dot_general(a, b, dims, preferred_element_type=F32)


def _bdot(a, b, dims=_NN):
    return lax.dot_general(a.astype(BF), b.astype(BF), dims, preferred_element_type=F32)


def _sigmoid(x):
    return jax.nn.sigmoid(x)


def _params(sem=None):
    return pltpu.CompilerParams(dimension_semantics=sem, vmem_limit_bytes=VMEM_LIMIT)


def _matmul(a, b, *, ta=False, tb=False, out_dtype=F32, tm=512, tn=512, tk=None, name, side=None):
    m = a.shape[1] if ta else a.shape[0]
    kdim = a.shape[0] if ta else a.shape[1]
    n = b.shape[0] if tb else b.shape[1]
    tk = tk or kdim
    tm, tn = min(tm, m), min(tn, n)
    nm, nn, nk = m // tm, n // tn, kdim // tk
    dims = (((0 if ta else 1,), (1 if tb else 0,)), ((), ()))
    s_arrays, s_in_specs, s_shapes, s_out_specs, s_sems = _side_io(side)
    na, no = len(s_arrays), len(s_shapes)
    nacc = 1 if nk > 1 else 0

    def body(*refs):
        a_ref, b_ref = refs[:2]
        s_ins, o_ref, s_outs = refs[2:2 + na], refs[2 + na], refs[3 + na:3 + na + no]
        scratch = refs[3 + na + no:]
        s_sem_refs = scratch[nacc:]
        i, j, k = pl.program_id(0), pl.program_id(1), pl.program_id(2)
        if side is not None:
            @pl.when((i == 0) & (j == 0) & (k == 0))
            def _():
                side.first(s_ins, s_outs, s_sem_refs)

        prod = _bdot(a_ref[...], b_ref[...], dims)
        if nk == 1:
            o_ref[...] = prod.astype(out_dtype)
        else:
            acc = scratch[0]

            @pl.when(k == 0)
            def _():
                acc[...] = prod

            @pl.when(k > 0)
            def _():
                acc[...] += prod

            @pl.when(k == nk - 1)
            def _():
                o_ref[...] = acc[...].astype(out_dtype)

        if side is not None:
            @pl.when((i == nm - 1) & (j == nn - 1) & (k == nk - 1))
            def _():
                side.last(s_ins, s_outs, s_sem_refs)

    a_spec = pl.BlockSpec((tk, tm), lambda i, j, k: (k, i)) if ta else pl.BlockSpec((tm, tk), lambda i, j, k: (i, k))
    b_spec = pl.BlockSpec((tn, tk), lambda i, j, k: (j, k)) if tb else pl.BlockSpec((tk, tn), lambda i, j, k: (k, j))
    sem = ("parallel", "parallel", "arbitrary") if side is None else ("arbitrary",) * 3
    out = pl.pallas_call(
        body, name=name, grid=(nm, nn, nk),
        in_specs=[a_spec, b_spec] + s_in_specs,
        out_specs=[pl.BlockSpec((tm, tn), lambda i, j, k: (i, j))] + s_out_specs,
        out_shape=[jax.ShapeDtypeStruct((m, n), out_dtype)] + s_shapes,
        scratch_shapes=([pltpu.VMEM((tm, tn), F32)] if nk > 1 else []) + s_sems,
        compiler_params=_params(sem),
    )(a, b, *s_arrays)
    return out[0] if side is None else (out[0], out[1:])


DZ_TILE = 512


def _part_offsets(parts):
    counts = [p.shape[1] // DZ_TILE for p in parts]
    offs = [sum(counts[:i]) for i in range(len(parts))]
    return counts, offs


def _part_spec(rows, cnt, off, tile_axis):
    def index(*g):
        return (0 if rows is None else g[0], jnp.clip(g[tile_axis] - off, 0, cnt - 1))
    return index


def _grad_w_in(h, parts):
    counts, offs = _part_offsets(parts)
    n = len(parts)

    def body(h_ref, *refs):
        o_ref = refs[n]
        j = pl.program_id(0)
        for p_ref, cnt, off in zip(refs[:n], counts, offs):
            @pl.when((j >= off) & (j < off + cnt))
            def _(p_ref=p_ref):
                o_ref[...] = _bdot(h_ref[...], p_ref[...], _TN).astype(BF)

    return pl.pallas_call(
        body, name="g_win", grid=(sum(counts),),
        in_specs=[pl.BlockSpec((T, D), lambda j: (0, 0))] +
                 [pl.BlockSpec((T, DZ_TILE), _part_spec(None, c, o, 0)) for c, o in zip(counts, offs)],
        out_specs=pl.BlockSpec((D, DZ_TILE), lambda j: (0, j)),
        out_shape=jax.ShapeDtypeStruct((D, NIN), BF),
        compiler_params=_params(("parallel",)),
    )(h, *parts)


def _grad_w_in_half(h, parts, half_idx, side=None):
    counts, offs = _part_offsets(parts)
    n = len(parts)
    nj = sum(counts)
    s_arrays, s_in_specs, s_shapes, s_out_specs, s_sems = _side_io(side)
    na, no = len(s_arrays), len(s_shapes)

    def body(idx_ref, h_ref, *refs):
        s_ins, o_ref, s_outs, s_sem_refs = refs[n:n + na], refs[n + na], refs[n + na + 1:n + na + 1 + no], refs[n + na + 1 + no:]
        j = pl.program_id(0)
        if side is not None:
            @pl.when(j == 0)
            def _():
                side.first(s_ins, s_outs, s_sem_refs)

        for p_ref, cnt, off in zip(refs[:n], counts, offs):
            @pl.when((j >= off) & (j < off + cnt))
            def _(p_ref=p_ref):
                o_ref[...] = _bdot(h_ref[...], p_ref[...], _TN).astype(BF)

        if side is not None:
            @pl.when(j == nj - 1)
            def _():
                side.last(s_ins, s_outs, s_sem_refs)

    def part_spec(cnt, off):
        return pl.BlockSpec((T, DZ_TILE), lambda j, idx: (0, jnp.clip(j - off, 0, cnt - 1)))

    out = pl.pallas_call(
        body, name="g_win_half" if side is None else "g_win_half_carrying",
        grid_spec=pltpu.PrefetchScalarGridSpec(
            num_scalar_prefetch=1, grid=(nj,),
            in_specs=[pl.BlockSpec((T, D // 2), lambda j, idx: (0, idx[0]))] +
                     [part_spec(c, o) for c, o in zip(counts, offs)] + s_in_specs,
            out_specs=[pl.BlockSpec((D // 2, DZ_TILE), lambda j, idx: (0, j))] + s_out_specs,
            scratch_shapes=s_sems),
        out_shape=[jax.ShapeDtypeStruct((D // 2, NIN), BF)] + s_shapes,
        compiler_params=_params(("parallel",) if side is None else ("arbitrary",)),
    )(half_idx, h, *parts, *s_arrays)
    return out[0] if side is None else (out[0], out[1:])


def _side_io(side):
    if side is None:
        return [], [], [], [], []
    return (side.arrays, [HBM] * len(side.arrays), side.out_shapes, [HBM] * len(side.out_shapes), side.sems)


def _grad_x(parts, w_in, x, dout, norm_w, side=None):
    counts, offs = _part_offsets(parts)
    n = len(parts)
    tm = 1024
    nm, nk = T // tm, sum(counts)
    s_arrays, s_in_specs, s_shapes, s_out_specs, s_sems = _side_io(side)
    na, no = len(s_arrays), len(s_shapes)

    def body(*refs):
        w_ref, x_ref, dout_ref, nw_ref = refs[n:n + 4]
        s_ins = refs[n + 4:n + 4 + na]
        gx_ref, gw_ref = refs[n + 4 + na:n + 6 + na]
        s_outs = refs[n + 6 + na:n + 6 + na + no]
        acc = refs[n + 6 + na + no]
        s_sem_refs = refs[n + 7 + na + no:]
        i, k = pl.program_id(0), pl.program_id(1)

        @pl.when((i == 0) & (k == 0))
        def _():
            gw_ref[...] = jnp.zeros_like(gw_ref)
            if side is not None:
                side.first(s_ins, s_outs, s_sem_refs)

        @pl.when(k == 0)
        def _():
            acc[...] = jnp.zeros_like(acc)

        if side is not None and side.mid is not None:
            @pl.when((i == nm - 1) & (k == 0))
            def _():
                side.mid(s_ins, s_outs, s_sem_refs)

        for p_ref, cnt, off in zip(refs[:n], counts, offs):
            @pl.when((k >= off) & (k < off + cnt))
            def _(p_ref=p_ref):
                acc[...] += _bdot(p_ref[...], w_ref[...], _NT)

        @pl.when(k == nk - 1)
        def _():
            gw = jnp.zeros((1, D), F32)
            for c in range(tm // BLK):
                rows = pl.ds(BLK * c, BLK)
                xv, dhv = x_ref[rows, :], acc[rows, :]
                r = lax.rsqrt(jnp.mean(xv * xv, axis=-1, keepdims=True) + EPS)
                nrm = xv * r
                dn = dhv * nw_ref[...]
                gw = gw + jnp.sum(dhv * nrm, axis=0, keepdims=True)
                gx_ref[rows, :] = dout_ref[rows, :] + r * (dn - nrm * jnp.mean(dn * nrm, axis=-1, keepdims=True))
            gw_ref[...] += gw

        if side is not None:
            @pl.when((i == nm - 1) & (k == nk - 1))
            def _():
                side.last(s_ins, s_outs, s_sem_refs)

    row = pl.BlockSpec((tm, D), lambda i, k: (i, 0))
    vec = pl.BlockSpec((1, D), lambda i, k: (0, 0))
    out = pl.pallas_call(
        body, name="grad_x", grid=(nm, nk),
        in_specs=[pl.BlockSpec((tm, DZ_TILE), _part_spec(0, c, o, 1)) for c, o in zip(counts, offs)] +
                 [pl.BlockSpec((D, DZ_TILE), lambda i, k: (0, k)), row, row, vec] + s_in_specs,
        out_specs=[row, vec] + s_out_specs,
        out_shape=[jax.ShapeDtypeStruct((T, D), F32), jax.ShapeDtypeStruct((1, D), F32)] + s_shapes,
        scratch_shapes=[pltpu.VMEM((tm, D), F32)] + s_sems,
        compiler_params=_params(("arbitrary", "arbitrary")),
    )(*parts, w_in, x, dout, norm_w, *s_arrays)
    return out[0], out[1], out[2:]


def _norm_and_rope_tables(x, w, pos, invf, side=None, own=None, cast=()):
    tm = 256
    nm = T // tm
    s_arrays, s_in_specs, s_shapes, s_out_specs, s_sems = _side_io(side)
    na, no, nc = len(s_arrays), len(s_shapes), len(cast)
    nz = 0 if own is None else 1
    wsh, blk = own if own is not None else (None, jnp.zeros((1,), jnp.int32))
    first_out = 4 + nz + nc + na

    def body(blk_ref, *refs):
        x_ref, w_ref, pos_ref, invf_ref = refs[:4]
        s_ins = refs[4 + nz + nc:first_out]
        h_ref, cos_ref, sa_ref, sb_ref = refs[first_out:first_out + 4]
        s_outs = refs[first_out + 4 + nz + nc:first_out + 4 + nz + nc + no]
        s_sem_refs = refs[first_out + 4 + nz + nc + no:]

        @pl.when(pl.program_id(0) == 0)
        def _():
            if side is not None:
                side.first(s_ins, s_outs, s_sem_refs)
            for src, dst in zip(refs[4 + nz:4 + nz + nc], refs[first_out + 4 + nz:first_out + 4 + nz + nc]):
                dst[...] = src[...].astype(BF)

        xv = x_ref[...]
        r = lax.rsqrt(jnp.mean(xv * xv, axis=-1, keepdims=True) + EPS)
        h = (xv * r * w_ref[...]).astype(BF)
        h_ref[...] = h
        if own is not None:
            refs[first_out + 4][...] = _dot(h, refs[4][...])
        first = (lax.broadcasted_iota(jnp.int32, (tm, 128), 1) % 64) < 32
        ang = pos_ref[...].astype(F32) * invf_ref[...]
        s = jnp.sin(ang)
        cos_ref[...] = jnp.cos(ang)
        sa_ref[...] = jnp.where(first, -s, 0.0)
        sb_ref[...] = jnp.where(first, 0.0, s)
        if side is not None:
            @pl.when(pl.program_id(0) == nm - 1)
            def _():
                side.last(s_ins, s_outs, s_sem_refs)

    tab = pl.BlockSpec((tm, 128), lambda i, b: (i, 0))
    own_in = [] if own is None else [pl.BlockSpec(wsh.shape, lambda i, b: (0, 0))]
    own_out = [] if own is None else [pl.BlockSpec((tm, wsh.shape[1]), lambda i, b: (i, b[0]))]
    own_shape = [] if own is None else [jax.ShapeDtypeStruct((T, NIN), F32)]
    whole = [pl.BlockSpec(a.shape, lambda i, b: (0, 0)) for a in cast]
    out = pl.pallas_call(
        body, name="norm_and_rope_tables",
        grid_spec=pltpu.PrefetchScalarGridSpec(
            num_scalar_prefetch=1, grid=(nm,),
            in_specs=[pl.BlockSpec((tm, D), lambda i, b: (i, 0)), pl.BlockSpec((1, D), lambda i, b: (0, 0)),
                      pl.BlockSpec((tm, 1), lambda i, b: (i, 0)), pl.BlockSpec((1, 128), lambda i, b: (0, 0))]
                     + own_in + whole + s_in_specs,
            out_specs=[pl.BlockSpec((tm, D), lambda i, b: (i, 0)), tab, tab, tab] + own_out + whole + s_out_specs,
            scratch_shapes=s_sems),
        out_shape=[jax.ShapeDtypeStruct((T, D), BF)] + [jax.ShapeDtypeStruct((T, 128), F32)] * 3 + own_shape
                  + [jax.ShapeDtypeStruct(a.shape, BF) for a in cast] + s_shapes,
        compiler_params=_params(("parallel",) if side is None and not cast else ("arbitrary",)),
    )(blk, x, w, pos, invf, *([] if own is None else [wsh]), *cast, *s_arrays)
    return (out[0], out[1], out[2], out[3], (out[4] if own is not None else None), out[4 + nz:4 + nz + nc],
            out[4 + nz + nc:])


def _z_blocks(h, w, z, idx, nb, side, name, fill=None):
    tm, tn = 1024, NIN // 8
    s_arrays, s_in_specs, s_shapes, s_out_specs, s_sems = _side_io(side)
    na, no = len(s_arrays), len(s_shapes)
    nm, ns = T // tm, 2 * nb
    nf = 0 if fill is None else 1

    def col(first, i, s, b):
        return (0, b[first + s // 2] * 2 + s % 2)

    def body(idx_ref, h_ref, w_ref, zin_ref, *refs):
        s_ins = refs[nf:nf + na]
        o_ref = refs[nf + na]
        s_outs = refs[nf + na + 1 + nf:nf + na + 1 + nf + no]
        s_sem_refs = refs[nf + na + 1 + nf + no + nf:]
        i, s = pl.program_id(0), pl.program_id(1)

        if side is not None:
            @pl.when((i == 0) & (s == 0))
            def _():
                side.first(s_ins, s_outs, s_sem_refs)

        if fill is not None:
            tile = pl.ds(pl.multiple_of((idx_ref[0] * 2 + s) * tn, 128), tn)
            store = pltpu.make_async_copy(w_ref, refs[nf + na + 1].at[:, tile], refs[nf + na + 1 + nf + no].at[0])
            pl.when(i == 0)(store.start)
        o_ref[...] = _dot(h_ref[...], w_ref[...])
        if fill is not None:
            pl.when(i == 0)(store.wait)

        if side is not None:
            @pl.when((i == nm - 1) & (s == ns - 1))
            def _():
                side.last(s_ins, s_outs, s_sem_refs)

    fills = [] if fill is None else [fill]
    out = pl.pallas_call(
        body, name=name,
        grid_spec=pltpu.PrefetchScalarGridSpec(
            num_scalar_prefetch=1, grid=(nm, ns),
            in_specs=[pl.BlockSpec((tm, D), lambda i, s, b: (i, 0)), pl.BlockSpec((D, tn), functools.partial(col, nb)),
                      HBM] + [HBM] * nf + s_in_specs,
            out_specs=[pl.BlockSpec((tm, tn), lambda i, s, b: (i, col(0, i, s, b)[1]))] + [HBM] * nf + s_out_specs,
            scratch_shapes=[pltpu.SemaphoreType.DMA((1,))] * nf + s_sems),
        out_shape=[jax.ShapeDtypeStruct((T, NIN), F32)] + [jax.ShapeDtypeStruct(f.shape, f.dtype) for f in fills]
                  + s_shapes,
        input_output_aliases={3: 0, **({4: 1} if fill is not None else {})},
        compiler_params=_params(("arbitrary", "arbitrary")),
    )(idx, h, w, z, *fills, *s_arrays)
    return (out[0], *out[1:1 + nf], out[1 + nf:])


def _lower_bound(lbl):
    mx = jnp.max(lbl, axis=0, keepdims=True)
    e = jnp.exp(lbl - mx)
    return e[0:1] / jnp.sum(e, axis=0, keepdims=True)


def _cumsum_rows(g, rows):
    b = g
    sh = 1
    while sh < CH:
        b = b + jnp.where(rows >= sh, pltpu.roll(b, sh, axis=0), 0.0)
        sh *= 2
    return b


def _rev_cumsum_rows(g, rows):
    b = g
    sh = 1
    while sh < CH:
        b = b + jnp.where(rows < CH - sh, pltpu.roll(b, CH - sh, axis=0), 0.0)
        sh *= 2
    return b


SUB = CH // 2


def _direct_block(qb, kb, vb, bb, rows8):
    ob = jnp.zeros_like(qb)
    for s in range(SUB):
        e_s = jnp.exp(jnp.where(rows8 >= s, bb - bb[s:s + 1], -jnp.inf))
        ob = ob + jnp.sum(qb * e_s * kb[s:s + 1], axis=1, keepdims=True) * vb[s:s + 1]
    return ob


def _direct_block_bwd(qb, kb, vb, bb, dob, rows8, rowc8):
    dq = dk = dv = db = jnp.zeros_like(qb)
    for s in range(SUB):
        one = (rowc8 == s).astype(F32)
        ks, vs = kb[s:s + 1], vb[s:s + 1]
        e_s = jnp.exp(jnp.where(rows8 >= s, bb - bb[s:s + 1], -jnp.inf))
        qes = qb * e_s
        w = qes * ks
        a = jnp.sum(w, axis=1, keepdims=True)
        da = jnp.sum(dob * vs, axis=1, keepdims=True)
        dv = dv + one * jnp.sum(a * dob, axis=0, keepdims=True)
        dq = dq + da * e_s * ks
        dk = dk + one * jnp.sum(da * qes, axis=0, keepdims=True)
        u = da * w
        db = db + u - one * jnp.sum(u, axis=0, keepdims=True)
    return dq, dk, dv, db


def _cross_factors(q, k, b):
    ref = b[SUB - 1:SUB]
    e_hi, e_lo = jnp.exp(b[SUB:] - ref), jnp.exp(ref - b[:SUB])
    return q[SUB:] * e_hi, k[:SUB] * e_lo, e_hi, e_lo


def _intra_fwd(q, k, v, b, rows8):
    lo = _direct_block(q[:SUB], k[:SUB], v[:SUB], b[:SUB], rows8)
    hi = _direct_block(q[SUB:], k[SUB:], v[SUB:], b[SUB:], rows8)
    qe_hi, ke_lo, _, _ = _cross_factors(q, k, b)
    for s in range(SUB):
        hi = hi + jnp.sum(qe_hi * ke_lo[s:s + 1], axis=1, keepdims=True) * v[s:s + 1]
    return jnp.concatenate([lo, hi], axis=0)


def _intra_bwd(q, k, v, b, do, rows8, rowc8):
    dq_lo, dk_lo, dv_lo, db_lo = _direct_block_bwd(q[:SUB], k[:SUB], v[:SUB], b[:SUB], do[:SUB], rows8, rowc8)
    dq_hi, dk_hi, dv_hi, db_hi = _direct_block_bwd(q[SUB:], k[SUB:], v[SUB:], b[SUB:], do[SUB:], rows8, rowc8)
    qe_hi, ke_lo, e_hi, e_lo = _cross_factors(q, k, b)
    do_hi, v_lo = do[SUB:], v[:SUB]
    dqe = dke = jnp.zeros_like(qe_hi)
    for s in range(SUB):
        one = (rowc8 == s).astype(F32)
        a = jnp.sum(qe_hi * ke_lo[s:s + 1], axis=1, keepdims=True)
        da = jnp.sum(do_hi * v_lo[s:s + 1], axis=1, keepdims=True)
        dv_lo = dv_lo + one * jnp.sum(a * do_hi, axis=0, keepdims=True)
        dqe = dqe + da * ke_lo[s:s + 1]
        dke = dke + one * jnp.sum(da * qe_hi, axis=0, keepdims=True)
    u_hi, u_lo = dqe * qe_hi, dke * ke_lo
    d_ref = jnp.sum(u_lo, axis=0, keepdims=True) - jnp.sum(u_hi, axis=0, keepdims=True)
    db_lo = db_lo - u_lo + (rowc8 == SUB - 1).astype(F32) * d_ref
    cat = lambda lo, hi: jnp.concatenate([lo, hi], axis=0)
    return (cat(dq_lo, dq_hi + dqe * e_hi), cat(dk_lo + dke * e_lo, dk_hi), cat(dv_lo, dv_hi),
            cat(db_lo, db_hi + u_hi))


def _hgrn_fwd(z, lbl, nw):
    def body(hq_ref, hf_ref, hi_ref, hg_ref, lbl_ref, nw_ref, oraw_ref, og_ref, sh_ref, st_ref):
        @pl.when(pl.program_id(0) == 0)
        def _():
            st_ref[...] = jnp.zeros_like(st_ref)

        lb_all = _lower_bound(lbl_ref[...])
        rows = lax.broadcasted_iota(jnp.int32, (CH, HK), 0)
        rows8 = lax.broadcasted_iota(jnp.int32, (SUB, HK), 0)
        nwv = nw_ref[...]
        for cc, h in [(cc, h) for cc in range(HSTEP) for h in range(HEADS)]:
            rs = slice(CH * cc, CH * (cc + 1))
            sl = slice(HK * h, HK * (h + 1))
            lb = lb_all[:, sl]
            hq, hf, v, hg = hq_ref[rs, sl], hf_ref[rs, sl], hi_ref[rs, sl], hg_ref[rs, sl]
            q = hq * _sigmoid(hq)
            f = lb + (1.0 - lb) * _sigmoid(hf)
            k = 1.0 - f
            b = _cumsum_rows(jnp.log(f), rows)
            sh_ref[cc, h] = st_ref[h]
            o = _bdot(q * jnp.exp(b), st_ref[h], _NT) + _intra_fwd(q, k, v, b, rows8)
            bl = b[CH - 1:CH]
            st_ref[h] = st_ref[h] * jnp.exp(bl)
            st_ref[h] += _bdot(v, k * jnp.exp(bl - b), _TN)
            oraw_ref[rs, sl] = o
            nrm = o * lax.rsqrt(jnp.mean(o * o, axis=1, keepdims=True) + EPS)
            og_ref[rs, sl] = (nrm * nwv * (hg * _sigmoid(hg))).astype(BF)

    zblk = lambda c: pl.BlockSpec((CH * HSTEP, D), lambda i, c=c: (i, c))
    return pl.pallas_call(
        body, name="hgrn_fwd", grid=(NCH // HSTEP,),
        in_specs=[zblk(0), zblk(1), zblk(2), zblk(3),
                  pl.BlockSpec((2, D), lambda i: (0, 0)), pl.BlockSpec((1, HK), lambda i: (0, 0))],
        out_specs=[zblk(0), zblk(0),
                   pl.BlockSpec((HSTEP, HEADS, HK, HK), lambda i: (i, 0, 0, 0))],
        out_shape=[jax.ShapeDtypeStruct((T, D), F32), jax.ShapeDtypeStruct((T, D), BF),
                   jax.ShapeDtypeStruct((NCH, HEADS, HK, HK), F32)],
        scratch_shapes=[pltpu.VMEM((HEADS, HK, HK), F32)],
        compiler_params=_params(("arbitrary",)),
    )(z, z, z, z, lbl, nw)


def _hgrn_bwd(z, lbl, nw, oraw, dog, shist):
    hstep = 1

    def body(hq_ref, hf_ref, hi_ref, hg_ref, lbl_ref, nw_ref, oraw_ref, dog_ref, sh_ref,
             dz_ref, dlb_ref, dnw_ref, dst_ref):
        @pl.when(pl.program_id(0) == 0)
        def _():
            dst_ref[...] = jnp.zeros_like(dst_ref)
            dlb_ref[...] = jnp.zeros_like(dlb_ref)
            dnw_ref[...] = jnp.zeros_like(dnw_ref)

        lb_all = _lower_bound(lbl_ref[...])
        rows = lax.broadcasted_iota(jnp.int32, (CH, HK), 0)
        rowc = lax.broadcasted_iota(jnp.int32, (CH, 1), 0)
        rows8 = lax.broadcasted_iota(jnp.int32, (SUB, HK), 0)
        rowc8 = lax.broadcasted_iota(jnp.int32, (SUB, 1), 0)
        nwv = nw_ref[...]
        dnw = jnp.zeros((1, HK), F32)
        for cc, h in [(cc, h) for cc in reversed(range(hstep)) for h in range(HEADS)]:
            rs = slice(CH * cc, CH * (cc + 1))
            sl = slice(HK * h, HK * (h + 1))
            lb = lb_all[:, sl]
            hq, hf, v, hg = hq_ref[rs, sl], hf_ref[rs, sl], hi_ref[rs, sl], hg_ref[rs, sl]
            o, dg_out = oraw_ref[rs, sl], dog_ref[rs, sl]
            sg = _sigmoid(hg)
            sil = hg * sg
            r = lax.rsqrt(jnp.mean(o * o, axis=1, keepdims=True) + EPS)
            nrm = o * r
            d_hg = dg_out * (nrm * nwv) * (sg * (1.0 + hg * (1.0 - sg)))
            dn = dg_out * nwv * sil
            dnw = dnw + jnp.sum(dg_out * nrm * sil, axis=0, keepdims=True)
            do = r * (dn - nrm * jnp.mean(dn * nrm, axis=1, keepdims=True))
            sq = _sigmoid(hq)
            q = hq * sq
            sig = _sigmoid(hf)
            f = lb + (1.0 - lb) * sig
            k = 1.0 - f
            b = _cumsum_rows(jnp.log(f), rows)
            eb = jnp.exp(b)
            qe = q * eb
            bl = b[CH - 1:CH]
            ebl = jnp.exp(bl)
            kdec = jnp.exp(bl - b)
            ke = k * kdec
            dqe = _bdot(do, sh_ref[cc, h])
            dq = dqe * eb
            db = dqe * qe
            dke = _bdot(v, dst_ref[h])
            dv = _bdot(ke, dst_ref[h], _NT)
            dk = dke * kdec
            rr = dke * ke
            db = db - rr
            db_last = (jnp.sum(rr, axis=0, keepdims=True)
                       + ebl * jnp.sum(dst_ref[h] * sh_ref[cc, h], axis=0, keepdims=True))
            dst_ref[h] = dst_ref[h] * ebl
            dst_ref[h] += _bdot(do, qe, _TN)
            dq_i, dk_i, dv_i, db_i = _intra_bwd(q, k, v, b, do, rows8, rowc8)
            dq, dk, dv = dq + dq_i, dk + dk_i, dv + dv_i
            db = db + db_i + (rowc == CH - 1).astype(F32) * db_last
            dgl = _rev_cumsum_rows(db, rows)
            df = dgl / f - dk
            dlb_ref[:, sl] += jnp.sum(df * (1.0 - sig), axis=0, keepdims=True)
            dz_ref[rs, sl] = (dq * (sq * (1.0 + hq * (1.0 - sq)))).astype(BF)
            dz_ref[rs, D + HK * h:D + HK * (h + 1)] = (df * (1.0 - lb) * sig * (1.0 - sig)).astype(BF)
            dz_ref[rs, 2 * D + HK * h:2 * D + HK * (h + 1)] = dv.astype(BF)
            dz_ref[rs, 3 * D + HK * h:3 * D + HK * (h + 1)] = d_hg.astype(BF)
        dnw_ref[...] += dnw

    rev = lambda i: NCH // hstep - 1 - i
    zblk = lambda c: pl.BlockSpec((CH * hstep, D), lambda i, c=c: (rev(i), c))
    return pl.pallas_call(
        body, name="hgrn_bwd", grid=(NCH // hstep,),
        in_specs=[zblk(0), zblk(1), zblk(2), zblk(3),
                  pl.BlockSpec((2, D), lambda i: (0, 0)), pl.BlockSpec((1, HK), lambda i: (0, 0)),
                  zblk(0), zblk(0),
                  pl.BlockSpec((hstep, HEADS, HK, HK), lambda i: (rev(i), 0, 0, 0))],
        out_specs=[pl.BlockSpec((CH * hstep, 4 * D), lambda i: (rev(i), 0)),
                   pl.BlockSpec((1, D), lambda i: (0, 0)), pl.BlockSpec((1, HK), lambda i: (0, 0))],
        out_shape=[jax.ShapeDtypeStruct((T, 4 * D), BF), jax.ShapeDtypeStruct((1, D), F32),
                   jax.ShapeDtypeStruct((1, HK), F32)],
        scratch_shapes=[pltpu.VMEM((HEADS, HK, HK), F32)],
        compiler_params=_params(("arbitrary",)),
    )(z, z, z, z, lbl, nw, oraw, dog, shist)


BLK = 128
NBLK = T // BLK
QK_SCALE = 0.125


def _head_masks():
    lane = lax.broadcasted_iota(jnp.int32, (1, BLK), 1)
    return [(lane < 64).astype(F32), (lane >= 64).astype(F32)]


def _pieces(dil):
    m = T // dil
    out = []
    for r in range(dil):
        for j in range(m // BLK):
            start = r + dil * BLK * j
            rows = pl.ds(start, BLK, stride=dil) if dil > 1 else pl.ds(start, BLK)
            out.append((rows, r * m + BLK * j))
    return out


def _rope(x, c, sa, sb):
    return x * c + pltpu.roll(x, 96, axis=1) * sa + pltpu.roll(x, 32, axis=1) * sb


def _rope_t(d, c, sa, sb):
    return d * c + pltpu.roll(d * sa, 32, axis=1) + pltpu.roll(d * sb, 96, axis=1)


def _rope_and_regroup(dil, q_ref, k_ref, v_ref, tables, stage_q, stage_k, qr_ref, kr_ref, vr_ref):
    cos_ref, sa_ref, sb_ref = tables
    to_q, to_k = (qr_ref, kr_ref) if dil == 1 else (stage_q, stage_k)
    for c in range(T // BLK):
        rows = pl.ds(BLK * c, BLK)
        cs, sa, sb = cos_ref[rows, :], sa_ref[rows, :], sb_ref[rows, :]
        to_q[rows, :] = (_rope(q_ref[rows, :], cs, sa, sb) * QK_SCALE).astype(to_q.dtype)
        to_k[rows, :] = _rope(k_ref[rows, :], cs, sa, sb).astype(to_k.dtype)
    for rows, dst in _pieces(dil):
        drows = pl.ds(dst, BLK)
        if dil > 1:
            qr_ref[drows, :] = stage_q[rows, :].astype(qr_ref.dtype)
            kr_ref[drows, :] = stage_k[rows, :].astype(kr_ref.dtype)
        vr_ref[drows, :] = v_ref[rows, :].astype(vr_ref.dtype)


def _window_bias(bias_ref):
    ii = lax.broadcasted_iota(jnp.int32, (2 * BLK, BLK), 0) % BLK
    jj = lax.broadcasted_iota(jnp.int32, (2 * BLK, BLK), 1)
    bias_ref[0] = jnp.where(jj <= ii, 0.0, -jnp.inf)
    bias_ref[1] = jnp.where(jj >= ii, 0.0, -jnp.inf)


def _blocks(bi):
    if isinstance(bi, int):
        return pl.ds(bi * BLK, BLK), pl.ds(max(bi - 1, 0) * BLK, BLK)
    return (pl.ds(pl.multiple_of(bi * BLK, BLK), BLK),
            pl.ds(pl.multiple_of(jnp.maximum(bi - 1, 0) * BLK, BLK), BLK))


def _stack_heads(x, masks):
    return jnp.concatenate([x * masks[0].astype(x.dtype), x * masks[1].astype(x.dtype)], axis=0).astype(BF)


def _side_steps(side, refs, **when):
    if side is None:
        return
    for stage, cond in when.items():
        if getattr(side, stage) is not None:
            pl.when(cond)(functools.partial(getattr(side, stage), *refs))


def _attn_fwd(z, cos, sa, sb, side=None):
    s_arrays, s_in_specs, s_shapes, s_out_specs, s_sems = _side_io(side)
    na, no = len(s_arrays), len(s_shapes)

    def body(q_ref, k_ref, v_ref, ag_ref, cos_ref, sa_ref, sb_ref, *refs):
        ob_ref, opre_ref, lse_ref, qr_ref, kr_ref, vr_ref = refs[na:na + 6]
        bias_ref, og_ref, lg_ref, otok_ref, ltok_ref, sc_ref = refs[na + 6 + no:na + 12 + no]
        s_refs = (refs[:na], refs[na + 6:na + 6 + no], refs[na + 12 + no:])
        p, g = pl.program_id(0), pl.program_id(1)
        _side_steps(side, s_refs, first=(p == 0) & (g == 0), mid=(p == 1) & (g == 0))
        masks = _head_masks()

        @pl.when(g == 0)
        def _():
            _window_bias(bias_ref)

        def group(gi):
            dil = ATT_GROUPS[gi][1]
            nblk = (T // dil) // BLK
            _rope_and_regroup(dil, q_ref, k_ref, v_ref, (cos_ref, sa_ref, sb_ref), lg_ref.at[0], lg_ref.at[1],
                              qr_ref, kr_ref, vr_ref)

            def scores(bi, slot):
                cur, prev = _blocks(bi)
                q2 = _stack_heads(qr_ref[cur, :], masks)
                sc_ref[slot, 0] = _dot(q2, kr_ref[cur, :], _NT) + bias_ref[0]
                if nblk > 1:
                    sc_ref[slot, 1] = (_dot(q2, kr_ref[prev, :], _NT)
                                       + (bias_ref[1] + jnp.where((bi % nblk) != 0, 0.0, -jnp.inf)))

            def finish(bi, slot):
                cur, prev = _blocks(bi)
                s_c, vc = sc_ref[slot, 0], vr_ref[cur, :]
                if nblk > 1:
                    s_p, vp = sc_ref[slot, 1], vr_ref[prev, :]
                    mx = jnp.max(jnp.maximum(s_c, s_p), axis=1, keepdims=True)
                    p_c, p_p = jnp.exp(s_c - mx), jnp.exp(s_p - mx)
                    den = jnp.sum(p_c + p_p, axis=1, keepdims=True)
                    oh = _dot(p_c.astype(BF), vc) + _dot(p_p.astype(BF), vp)
                else:
                    mx = jnp.max(s_c, axis=1, keepdims=True)
                    p_c = jnp.exp(s_c - mx)
                    den = jnp.sum(p_c, axis=1, keepdims=True)
                    oh = _dot(p_c.astype(BF), vc)
                on = oh / den
                lsev = jnp.broadcast_to(mx + jnp.log(den), (2 * BLK, BLK))
                og_ref[cur, :] = on[:BLK] * masks[0] + on[BLK:] * masks[1]
                lg_ref[0, cur, :] = lsev[:BLK]
                lg_ref[1, cur, :] = lsev[BLK:]

            def pair(j, carry):
                finish(2 * j, 0)
                scores(2 * j + 1, 1)
                finish(2 * j + 1, 1)
                scores(jnp.minimum(2 * j + 2, NBLK - 1), 0)
                return carry

            scores(0, 0)
            lax.fori_loop(0, NBLK // 2, pair, 0)
            for rows, src in _pieces(dil):
                srows = pl.ds(src, BLK)
                otok_ref[gi, rows, :] = og_ref[srows, :]
                ltok_ref[gi, 0, rows, :] = lg_ref[0, srows, :]
                ltok_ref[gi, 1, rows, :] = lg_ref[1, srows, :]

        for gi in range(3):
            pl.when(g == gi)(functools.partial(group, gi))

        @pl.when(g == 2)
        def _():
            for c in range(T // BLK):
                rows = pl.ds(BLK * c, BLK)
                wts = []
                for hh in range(2):
                    l0, l1, l2 = ltok_ref[0, hh, rows, :], ltok_ref[1, hh, rows, :], ltok_ref[2, hh, rows, :]
                    mx = jnp.maximum(jnp.maximum(l0, l1), l2)
                    e0, e1, e2 = jnp.exp(l0 - mx), jnp.exp(l1 - mx), jnp.exp(l2 - mx)
                    tot = e0 + e1 + e2
                    lse_ref[rows, BLK * hh:BLK * (hh + 1)] = mx + jnp.log(tot)
                    inv = 1.0 / tot
                    wts.append([e0 * inv, e1 * inv, e2 * inv])
                o = sum((wts[0][gi] * masks[0] + wts[1][gi] * masks[1]) * otok_ref[gi, rows, :] for gi in range(3))
                ag = ag_ref[rows, :]
                opre_ref[rows, :] = o
                ob_ref[rows, :] = (o * (ag * _sigmoid(ag))).astype(BF)

        _side_steps(side, s_refs, last=(p == 3) & (g == 2))

    c0 = ATT_COL0 // BLK
    zspec = lambda part: pl.BlockSpec((T, BLK), lambda p, g, part=part: (0, c0 + 12 * part + 4 * g + p))
    outspec = pl.BlockSpec((T, BLK), lambda p, g: (0, p))
    table = pl.BlockSpec((T, BLK), lambda p, g: (0, 0))
    regrouped = pl.BlockSpec((None, T, BLK), lambda p, g: (g, 0, p))
    big = lambda: pltpu.VMEM((T, BLK), F32)
    out = pl.pallas_call(
        body, name="attn_fwd", grid=(4, 3),
        in_specs=[zspec(0), zspec(1), zspec(2),
                  pl.BlockSpec((T, BLK), lambda p, g: (0, AG_COL0 // BLK + p)), table, table, table] + s_in_specs,
        out_specs=[outspec, outspec, pl.BlockSpec((T, 2 * BLK), lambda p, g: (0, p)), regrouped, regrouped, regrouped]
                  + s_out_specs,
        out_shape=[jax.ShapeDtypeStruct((T, 512), BF), jax.ShapeDtypeStruct((T, 512), F32),
                   jax.ShapeDtypeStruct((T, 8 * BLK), F32)] + [jax.ShapeDtypeStruct((3, T, 512), BF)] * 3 + s_shapes,
        scratch_shapes=[pltpu.VMEM((2, 2 * BLK, BLK), F32), big(),
                        pltpu.VMEM((2, T, BLK), F32), pltpu.VMEM((3, T, BLK), F32), pltpu.VMEM((3, 2, T, BLK), F32),
                        pltpu.VMEM((2, 2, 2 * BLK, BLK), F32)] + s_sems,
        compiler_params=_params(("parallel" if side is None else "arbitrary", "arbitrary")),
    )(z, z, z, z, cos, sa, sb, *s_arrays)
    return (*out[:6], out[6:])


def _attn_bwd(z, qs, ks, vs, cos, sa, sb, opre, lse, dob, side=None):
    s_arrays, s_in_specs, s_shapes, s_out_specs, s_sems = _side_io(side)
    na, no = len(s_arrays), len(s_shapes)

    def body(qs_ref, ks_ref, vs_ref, ag_ref, cos_ref, sa_ref, sb_ref, o_ref, lse0_ref, lse1_ref, dob_ref, *refs):
        dq_ref, dk_ref, dv_ref, dag_ref = refs[na:na + 4]
        (bias_ref, dtok_ref, qr_ref, kr_ref, vr_ref, dor_ref, lr_ref, dr_ref,
         dqr_ref, dkr_ref, dvr_ref, pd_ref, dotok_ref) = refs[na + 4 + no:na + 17 + no]
        s_refs = (refs[:na], refs[na + 4:na + 4 + no], refs[na + 17 + no:])
        p, g = pl.program_id(0), pl.program_id(1)
        _side_steps(side, s_refs, first=(p == 0) & (g == 0), mid=(p == 1) & (g == 0))
        masks = _head_masks()

        @pl.when(g == 0)
        def _():
            _window_bias(bias_ref)
            for c in range(T // BLK):
                rows = pl.ds(BLK * c, BLK)
                ag, dob_v, o = ag_ref[rows, :], dob_ref[rows, :], o_ref[rows, :]
                sg = _sigmoid(ag)
                dag_ref[rows, :] = (dob_v * o * (sg * (1.0 + ag * (1.0 - sg)))).astype(BF)
                do = dob_v * (ag * sg)
                dotok_ref[rows, :] = do
                prod = do * o
                for hh, mh in enumerate(masks):
                    dtok_ref[hh, rows, :] = jnp.broadcast_to(jnp.sum(prod * mh, axis=1, keepdims=True), (BLK, BLK))

        def group(gi):
            dil = ATT_GROUPS[gi][1]
            nblk = (T // dil) // BLK
            for rows, dst in _pieces(dil):
                drows = pl.ds(dst, BLK)
                dor_ref[drows, :] = dotok_ref[rows, :]
                for hh, lse_ref in enumerate((lse0_ref, lse1_ref)):
                    lr_ref[hh, drows, :] = lse_ref[rows, :]
                    dr_ref[hh, drows, :] = dtok_ref[hh, rows, :]
            dkr_ref[...] = jnp.zeros_like(dkr_ref)
            dvr_ref[...] = jnp.zeros_like(dvr_ref)

            def probs(bi, slot):
                cur, prev = _blocks(bi)
                q2, do2 = _stack_heads(qs_ref[cur, :], masks), _stack_heads(dor_ref[cur, :], masks)
                lh = jnp.concatenate([lr_ref[0, cur, :], lr_ref[1, cur, :]], axis=0)
                dh = jnp.concatenate([dr_ref[0, cur, :], dr_ref[1, cur, :]], axis=0)
                p_c = jnp.exp(_dot(q2, ks_ref[cur, :], _NT) + bias_ref[0] - lh)
                pd_ref[slot, 0] = p_c.astype(BF)
                pd_ref[slot, 1] = (p_c * (_dot(do2, vs_ref[cur, :], _NT) - dh)).astype(BF)
                if nblk > 1:
                    bias_p = bias_ref[1] + jnp.where((bi % nblk) != 0, 0.0, -jnp.inf)
                    p_p = jnp.exp(_dot(q2, ks_ref[prev, :], _NT) + bias_p - lh)
                    pd_ref[slot, 2] = p_p.astype(BF)
                    pd_ref[slot, 3] = (p_p * (_dot(do2, vs_ref[prev, :], _NT) - dh)).astype(BF)

            def grads(bi, slot):
                cur, prev = _blocks(bi)
                q2, do2 = _stack_heads(qs_ref[cur, :], masks), _stack_heads(dor_ref[cur, :], masks)
                p_c, ds_c = pd_ref[slot, 0], pd_ref[slot, 1]
                dq2 = _dot(ds_c, ks_ref[cur, :])
                dkr_ref[cur, :] += _dot(ds_c, q2, _TN)
                dvr_ref[cur, :] += _dot(p_c, do2, _TN)
                if nblk > 1:
                    p_p, ds_p = pd_ref[slot, 2], pd_ref[slot, 3]
                    dq2 = dq2 + _dot(ds_p, ks_ref[prev, :])
                    dkr_ref[prev, :] += _dot(ds_p, q2, _TN)
                    dvr_ref[prev, :] += _dot(p_p, do2, _TN)
                dqr_ref[cur, :] = dq2[:BLK] * masks[0] + dq2[BLK:] * masks[1]

            def pair(j, carry):
                grads(2 * j, 0)
                probs(2 * j + 1, 1)
                grads(2 * j + 1, 1)
                probs(jnp.minimum(2 * j + 2, NBLK - 1), 0)
                return carry

            probs(0, 0)
            lax.fori_loop(0, NBLK // 2, pair, 0)
            if dil > 1:
                for rows, src in _pieces(dil):
                    srows = pl.ds(src, BLK)
                    qr_ref[rows, :] = dqr_ref[srows, :]
                    kr_ref[rows, :] = dkr_ref[srows, :]
                    vr_ref[rows, :] = dvr_ref[srows, :]
            tq, tk, tv = (qr_ref, kr_ref, vr_ref) if dil > 1 else (dqr_ref, dkr_ref, dvr_ref)
            for c in range(T // BLK):
                rows = pl.ds(BLK * c, BLK)
                cs, sa, sb = cos_ref[rows, :], sa_ref[rows, :], sb_ref[rows, :]
                dq_ref[rows, :] = _rope_t(tq[rows, :] * QK_SCALE, cs, sa, sb).astype(BF)
                dk_ref[rows, :] = _rope_t(tk[rows, :], cs, sa, sb).astype(BF)
                dv_ref[rows, :] = tv[rows, :].astype(BF)

        for gi in range(3):
            pl.when(g == gi)(functools.partial(group, gi))
        _side_steps(side, s_refs, last=(p == 3) & (g == 2))

    regrouped = pl.BlockSpec((None, T, BLK), lambda p, g: (g, 0, p))
    pspec = pl.BlockSpec((T, BLK), lambda p, g: (0, p))
    gspec = pl.BlockSpec((T, BLK), lambda p, g: (0, 4 * g + p))
    table = pl.BlockSpec((T, BLK), lambda p, g: (0, 0))
    big = lambda: pltpu.VMEM((T, BLK), F32)
    two = lambda: pltpu.VMEM((2, T, BLK), F32)
    out = pl.pallas_call(
        body, name="attn_bwd", grid=(4, 3),
        in_specs=[regrouped, regrouped, regrouped,
                  pl.BlockSpec((T, BLK), lambda p, g: (0, AG_COL0 // BLK + p)), table, table, table,
                  pspec, pl.BlockSpec((T, BLK), lambda p, g: (0, 2 * p)),
                  pl.BlockSpec((T, BLK), lambda p, g: (0, 2 * p + 1)), pspec] + s_in_specs,
        out_specs=[gspec, gspec, gspec, pspec] + s_out_specs,
        out_shape=[jax.ShapeDtypeStruct((T, 1536), BF), jax.ShapeDtypeStruct((T, 1536), BF),
                   jax.ShapeDtypeStruct((T, 1536), BF), jax.ShapeDtypeStruct((T, 512), BF)] + s_shapes,
        scratch_shapes=[pltpu.VMEM((2, 2 * BLK, BLK), F32), two(), big(), big(), big(), big(),
                        two(), two(), big(), big(), big(), pltpu.VMEM((2, 4, 2 * BLK, BLK), BF), big()] + s_sems,
        compiler_params=_params(("parallel" if side is None else "arbitrary", "arbitrary")),
    )(qs, ks, vs, z, cos, sa, sb, opre, lse, lse, dob, *s_arrays)
    return (*out[:4], out[4:])


def _merge_out_loss(og, ob, z, w_a, w_b, w_out, x, tgt, wf):
    tm = 512

    def body(og_ref, ob_ref, ga_ref, gb_ref, wa_ref, wb_ref, wo_ref, x_ref, t_ref, wf_ref,
             m_ref, dout_ref, loss_ref, gwf_ref):
        @pl.when(pl.program_id(0) == 0)
        def _():
            loss_ref[...] = jnp.zeros_like(loss_ref)
            gwf_ref[...] = jnp.zeros_like(gwf_ref)

        ya, yb = _dot(og_ref[...], wa_ref[...]), _dot(ob_ref[...], wb_ref[...])
        m =(_sigmoid(ga_ref[...]) * ya + _sigmoid(gb_ref[...]) * yb).astype(BF)
        m_ref[...] = m
        out = x_ref[...] + _dot(m, wo_ref[...])
        r = lax.rsqrt(jnp.mean(out * out, axis=-1, keepdims=True) + EPS)
        yh = out * r
        wfv = wf_ref[...]
        err = yh * wfv - t_ref[...]
        loss_ref[...] += jnp.sum(err * err, axis=0, keepdims=True) * (0.5 / D)
        dy = err * (1.0 / D)
        gwf_ref[...] += jnp.sum(dy * yh, axis=0, keepdims=True)
        dyh = dy * wfv
        dout_ref[...] = r * (dyh - yh * jnp.mean(dyh * yh, axis=-1, keepdims=True))

    row = pl.BlockSpec((tm, D), lambda i: (i, 0))
    vec = pl.BlockSpec((1, D), lambda i: (0, 0))
    whole = lambda w: pl.BlockSpec(w.shape, lambda i: (0, 0))
    return pl.pallas_call(
        body, name="merge_out_loss", grid=(T // tm,),
        in_specs=[row, pl.BlockSpec((tm, ob.shape[1]), lambda i: (i, 0)),
                  pl.BlockSpec((tm, D), lambda i: (i, GATE_COL0 // D)),
                  pl.BlockSpec((tm, D), lambda i: (i, GATE_COL0 // D + 1)),
                  whole(w_a), whole(w_b), whole(w_out), row, row, vec],
        out_specs=[row, row, vec, vec],
        out_shape=[jax.ShapeDtypeStruct((T, D), BF), jax.ShapeDtypeStruct((T, D), F32),
                   jax.ShapeDtypeStruct((1, D), F32), jax.ShapeDtypeStruct((1, D), F32)],
        compiler_params=_params(("arbitrary",)),
    )(og, ob, z, z, w_a, w_b, w_out, x, tgt, wf)


def _merge_proj_bwd(dout, og, ob, z, w_a, w_b, w_out):
    tm = 512

    def body(dout_ref, og_ref, ob_ref, ga_ref, gb_ref, wa_ref, wb_ref, wo_ref,
             dya_ref, dyb_ref, dg_ref, dog_ref, dob_ref):
        dmv = _dot(dout_ref[...].astype(BF), wo_ref[...], _NT)
        sa, sb = _sigmoid(ga_ref[...]), _sigmoid(gb_ref[...])
        dya, dyb = (sa * dmv).astype(BF), (sb * dmv).astype(BF)
        dya_ref[...] = dya
        dyb_ref[...] = dyb
        dg_ref[:, :D] = (dmv * _dot(og_ref[...], wa_ref[...]) * sa * (1.0 - sa)).astype(BF)
        dg_ref[:, D:] = (dmv * _dot(ob_ref[...], wb_ref[...]) * sb * (1.0 - sb)).astype(BF)
        dog_ref[...] = _dot(dya, wa_ref[...], _NT)
        dob_ref[...] = _dot(dyb, wb_ref[...], _NT)

    row = pl.BlockSpec((tm, D), lambda i: (i, 0))
    whole = lambda w: pl.BlockSpec(w.shape, lambda i: (0, 0))
    nb = w_b.shape[0]
    return pl.pallas_call(
        body, name="merge_proj_bwd", grid=(T // tm,),
        in_specs=[row, row, pl.BlockSpec((tm, nb), lambda i: (i, 0)),
                  pl.BlockSpec((tm, D), lambda i: (i, GATE_COL0 // D)),
                  pl.BlockSpec((tm, D), lambda i: (i, GATE_COL0 // D + 1)), whole(w_a), whole(w_b), whole(w_out)],
        out_specs=[row, row, pl.BlockSpec((tm, 2 * D), lambda i: (i, 0)), row,
                   pl.BlockSpec((tm, nb), lambda i: (i, 0))],
        out_shape=[jax.ShapeDtypeStruct((T, D), BF), jax.ShapeDtypeStruct((T, D), BF),
                   jax.ShapeDtypeStruct((T, 2 * D), BF), jax.ShapeDtypeStruct((T, D), F32),
                   jax.ShapeDtypeStruct((T, nb), F32)],
        compiler_params=_params(("parallel",)),
    )(dout, og, ob, z, z, w_a, w_b, w_out)


def _rope_inv_freq():
    inv = ROPE_THETA ** (-jnp.arange(0, 64, 2, dtype=F32) / 64)
    return jnp.tile(inv, 4).reshape(1, BLK)


def _local_step(x, pos, norm_w, lbl, hnw, wf, tgt, w_in, w_a, w_b, w_out, shard_shapes=()):
    invf = _rope_inv_freq()
    if shard_shapes:
        blk = jnp.reshape(2 * lax.axis_index("x") + lax.axis_index("y"), (1,)).astype(jnp.int32)
        h, cos, sa, sb, z_own, (w_a, w_b, w_out), (w_near,) = _norm_and_rope_tables(
            x, norm_w, pos, invf, side=_gather_near_side(w_in, WEIGHT_AXES[0]), own=(w_in, blk),
            cast=(w_a, w_b, w_out))
        near = jnp.concatenate([blk ^ 2, blk ^ 1])
        z, (w_diag,) = _z_blocks(h, w_near, z_own, jnp.concatenate([near, near]), 2, name="z_proj_near",
                                 side=_gather_diag_side(w_near, w_in.shape, WEIGHT_AXES[0]))
        z, w_in, _ = _z_blocks(h, w_diag, z, jnp.concatenate([blk ^ 3, jnp.zeros_like(blk)]), 1, name="z_proj_diag",
                               fill=w_near, side=None)
    else:
        h, cos, sa, sb, _, _, _ = _norm_and_rope_tables(x, norm_w, pos, invf)
        z = _matmul(h, w_in, tm=T, tn=512, name="z_proj")
    oraw, og, shist = _hgrn_fwd(z, lbl, hnw)
    side_a = _gather_side([w_a, w_b, w_out], WEIGHT_AXES[1:]) if shard_shapes else None
    ob, opre, lse, qs, ks, vs, gathered = _attn_fwd(z, cos, sa, sb, side=side_a)
    if shard_shapes:
        w_a, w_b, w_out = gathered
    merged, dout, loss_vec, g_wf = _merge_out_loss(og, ob, z, w_a, w_b, w_out, x, tgt, wf)

    dya, dyb, dgates, dog, dob = _merge_proj_bwd(dout, og, ob, z, w_a, w_b, w_out)
    g_wout = _matmul(merged, dout, ta=True, out_dtype=BF, tm=512, tn=1024, name="g_wout")
    g_wa = _matmul(og, dya, ta=True, out_dtype=BF, tm=512, tn=1024, name="g_wa")
    g_wb = _matmul(ob, dyb, ta=True, out_dtype=BF, tm=512, tn=1024, name="g_wb")
    small = [g_wa, g_wb, g_wout]
    side_s = side_w = None
    if shard_shapes:
        p3_s = [_as3d(g, s, ax) for g, s, ax in zip(small, shard_shapes[1:], WEIGHT_AXES[1:])]
        side_s = _chip_exchange_direct_side(p3_s, shard_shapes[1:], WEIGHT_AXES[1:])
    dz_h, dlb, g_hnw = _hgrn_bwd(z, lbl, hnw, oraw, dog, shist)
    dq, dk, dv, dag, land_s = _attn_bwd(z, qs, ks, vs, cos, sa, sb, opre, lse, dob, side=side_s)
    dz_parts = [dz_h, dq, dk, dv, dag, dgates]
    if shard_shapes:
        c = lax.axis_index("c")
        half = lambda i: jnp.reshape(i, (1,)).astype(jnp.int32)
        g_send = _grad_w_in_half(h, dz_parts, half(1 - c))
        g_keep, (g_sib,) = _grad_w_in_half(h, dz_parts, half(c), side=_sibling_send_side(g_send))
        p3_w = [_add_bf16(g_keep, g_sib, "pair_sum_w_in").reshape(1, D // 2, NIN)]
        side_w = _chip_exchange_relay_side(p3_w[0], shard_shapes[0])
    else:
        g_big = [_grad_w_in(h, dz_parts)] + small
    gx, g_nw, land_w = _grad_x(dz_parts, w_in, x, dout, norm_w, side=side_w)
    small_sums = None
    if shard_shapes:
        g_big, small_sums = _rs_finish(p3_w + p3_s, [land_w[0]] + list(land_s), shard_shapes, WEIGHT_AXES,
                                       (g_nw, dlb, g_hnw, g_wf, loss_vec))
    return dict(loss_vec=loss_vec, gx=gx, g_nw=g_nw, dlb=dlb, g_hnw=g_hnw, g_wf=g_wf, small_sums=small_sums,
                g_win=g_big[0], g_wa=g_big[1], g_wb=g_big[2], g_wout=g_big[3])


MESH = pl.DeviceIdType.MESH
HBM = pl.BlockSpec(memory_space=pl.ANY)
WEIGHT_AXES = (1, 0, 1, 0)


def _place():
    x, y, c = lax.axis_index("x"), lax.axis_index("y"), lax.axis_index("c")
    chips = [(1 - x, y), (x, 1 - y), (1 - x, 1 - y)]
    return x, y, c, chips


def _block_half(ref, shard_shape, axis, j, half):
    r, c = shard_shape
    hr = r // 2
    if axis == 0:
        return ref.at[pl.ds(pl.multiple_of(j * r + half * hr, 16), hr), :]
    return ref.at[pl.ds(pl.multiple_of(half * hr, 16), hr), pl.ds(pl.multiple_of(j * c, 128), c)]


class _Side:
    def __init__(self, arrays, out_shapes, sems, first, last, mid=None):
        self.arrays, self.out_shapes, self.sems, self.first, self.last = arrays, out_shapes, sems, first, last
        self.mid = mid


def _gather_side(shards, axes):
    n = len(shards)
    shapes = [s.shape for s in shards]

    def copies(ins, outs, sems):
        send1, recv1, send2, recv2, send0, recv0 = sems
        x, y, c, chips = _place()
        me = 2 * x + y
        sib = (x, y, 1 - c)
        near = ((1 - c) * (1 - x) + c * x, (1 - c) * y + c * (1 - y))
        far = ((1 - c) * x + c * (1 - x), (1 - c) * (1 - y) + c * y)
        out = []
        for a in range(n):
            r, cc = shapes[a]
            mine = (outs[a].at[pl.ds(pl.multiple_of(me * r, 16), r), :] if axes[a] == 0
                    else outs[a].at[:, pl.ds(pl.multiple_of(me * cc, 128), cc)])
            own = pltpu.make_async_remote_copy(
                src_ref=ins[a], dst_ref=mine, send_sem=send0.at[a], recv_sem=recv0.at[a],
                device_id=sib, device_id_type=MESH)
            src = ins[a].at[pl.ds(pl.multiple_of(c * (r // 2), 16), r // 2), :]
            sends = [pltpu.make_async_remote_copy(
                src_ref=src, dst_ref=_block_half(outs[a], shapes[a], axes[a], me, c),
                send_sem=send1.at[a, k], recv_sem=recv1.at[a, k], device_id=(*chips[k], c), device_id_type=MESH)
                for k in range(2)]

            def region(chip, half):
                return _block_half(outs[a], shapes[a], axes[a], 2 * chip[0] + chip[1], half)

            def arrival(chip, k):
                reg = region(chip, c)
                return pltpu.make_async_remote_copy(
                    src_ref=reg, dst_ref=reg, send_sem=send1.at[a, k], recv_sem=recv1.at[a, k],
                    device_id=(*chip, c), device_id_type=MESH)

            def to_sibling(chip, k):
                reg = region(chip, c)
                return pltpu.make_async_remote_copy(
                    src_ref=reg, dst_ref=reg, send_sem=send2.at[a, k], recv_sem=recv2.at[a, k],
                    device_id=sib, device_id_type=MESH)

            def from_sibling(chip, k):
                reg = region(chip, 1 - c)
                return pltpu.make_async_remote_copy(
                    src_ref=reg, dst_ref=reg, send_sem=send2.at[a, k], recv_sem=recv2.at[a, k],
                    device_id=sib, device_id_type=MESH)

            relay = pltpu.make_async_remote_copy(
                src_ref=region(near, c), dst_ref=region(near, c), send_sem=send1.at[a, 2], recv_sem=recv1.at[a, 2],
                device_id=(*far, c), device_id_type=MESH)
            hops = [(arrival(near, c), to_sibling(near, c)), (arrival(far, 1 - c), to_sibling(far, 1 - c)),
                    (arrival(chips[2], 2), to_sibling(chips[2], 2))]
            back = [from_sibling(chips[k], k) for k in range(3)]
            out.append((own, sends, relay, hops, back))
        return out

    def first(ins, outs, sems):
        for own, sends, _, _, _ in copies(ins, outs, sems):
            own.start()
            for cp in sends:
                cp.start()

    def mid(ins, outs, sems):
        per_array = copies(ins, outs, sems)
        for step in range(2):
            for _, _, relay, hops, _ in per_array:
                arrived, onward = hops[step]
                arrived.wait_recv()
                if step == 0:
                    relay.start()
                onward.start()

    def last(ins, outs, sems):
        per_array = copies(ins, outs, sems)
        for _, _, _, hops, _ in per_array:
            arrived, onward = hops[2]
            arrived.wait_recv()
            onward.start()
        for own, sends, relay, hops, back in per_array:
            for cp in back:
                cp.wait_recv()
            for cp in sends + [relay] + [onward for _, onward in hops]:
                cp.wait_send()
            own.wait()

    full = [(4 * r, c) if ax == 0 else (r, 4 * c) for (r, c), ax in zip(shapes, axes)]
    sems = [pltpu.SemaphoreType.DMA((n, 3)), pltpu.SemaphoreType.DMA((n, 3)),
            pltpu.SemaphoreType.DMA((n, 3)), pltpu.SemaphoreType.DMA((n, 3)),
            pltpu.SemaphoreType.DMA((n,)), pltpu.SemaphoreType.DMA((n,))]
    return _Side(list(shards), [jax.ShapeDtypeStruct(f, BF) for f in full], sems, first, last, mid)


def _gather_near_side(shard, axis):
    shape = shard.shape
    r, cc = shape

    def copies(ins, outs, sems):
        send1, recv1, send2, recv2, send0, recv0 = sems
        x, y, c, chips = _place()
        me = 2 * x + y
        sib = (x, y, 1 - c)
        mine = (outs[0].at[pl.ds(pl.multiple_of(me * r, 16), r), :] if axis == 0
                else outs[0].at[:, pl.ds(pl.multiple_of(me * cc, 128), cc)])
        own = pltpu.make_async_remote_copy(
            src_ref=ins[0], dst_ref=mine, send_sem=send0.at[0], recv_sem=recv0.at[0],
            device_id=sib, device_id_type=MESH)
        src = ins[0].at[pl.ds(pl.multiple_of(c * (r // 2), 16), r // 2), :]

        def region(k, half):
            return _block_half(outs[0], shape, axis, 2 * chips[k][0] + chips[k][1], half)

        def moves(k):
            return [pltpu.make_async_remote_copy(
                        src_ref=s, dst_ref=d, send_sem=ss.at[k], recv_sem=rs.at[k], device_id=dev,
                        device_id_type=MESH)
                    for s, d, ss, rs, dev in (
                        (src, _block_half(outs[0], shape, axis, me, c), send1, recv1, (*chips[k], c)),
                        (region(k, c), region(k, c), send1, recv1, (*chips[k], c)),
                        (region(k, c), region(k, c), send2, recv2, sib),
                        (region(k, 1 - c), region(k, 1 - c), send2, recv2, sib))]

        return own, [moves(k) for k in range(2)]

    def first(ins, outs, sems):
        own, per_chip = copies(ins, outs, sems)
        own.start()
        for send, _, _, _ in per_chip:
            send.start()

    def last(ins, outs, sems):
        own, per_chip = copies(ins, outs, sems)
        for _, arrived, onward, _ in per_chip:
            arrived.wait_recv()
            onward.start()
        for send, _, onward, back in per_chip:
            back.wait_recv()
            send.wait_send()
            onward.wait_send()
        own.wait()

    full = (4 * r, cc) if axis == 0 else (r, 4 * cc)
    sems = [pltpu.SemaphoreType.DMA((2,))] * 4 + [pltpu.SemaphoreType.DMA((1,))] * 2
    return _Side([shard], [jax.ShapeDtypeStruct(full, BF)], sems, first, last)


def _gather_diag_side(gathered, shape, axis):
    r, cc = shape

    def copies(ins, outs, sems):
        send1, recv1, send2, recv2 = sems
        x, y, c, _ = _place()
        sib = (x, y, 1 - c)
        near = ((1 - c) * (1 - x) + c * x, (1 - c) * y + c * (1 - y))
        far = ((1 - c) * x + c * (1 - x), (1 - c) * (1 - y) + c * y)

        def half(i):
            return outs[0].at[pl.ds(pl.multiple_of(i * (r // 2), 16), r // 2), :]

        def move(s, d, ss, rs, dev):
            return pltpu.make_async_remote_copy(
                src_ref=s, dst_ref=d, send_sem=ss.at[0], recv_sem=rs.at[0], device_id=dev, device_id_type=MESH)

        relay = move(_block_half(ins[0], shape, axis, 2 * near[0] + near[1], c), half(c), send1, recv1, (*far, c))
        arrived = move(half(c), half(c), send1, recv1, (*far, c))
        onward = move(half(c), half(c), send2, recv2, sib)
        back = move(half(1 - c), half(1 - c), send2, recv2, sib)
        return relay, arrived, onward, back

    def first(ins, outs, sems):
        copies(ins, outs, sems)[0].start()

    def last(ins, outs, sems):
        relay, arrived, onward, back = copies(ins, outs, sems)
        arrived.wait_recv()
        onward.start()
        back.wait_recv()
        relay.wait_send()
        onward.wait_send()

    return _Side([gathered], [jax.ShapeDtypeStruct(shape, BF)], [pltpu.SemaphoreType.DMA((1,))] * 4, first, last)


def _as3d(g, shard_shape, axis):
    r, c = shard_shape
    return g.reshape(4, r, c) if axis == 0 else g.reshape(1, r, 4 * c)


def _half_rows(ref3, hr, half):
    return ref3.at[:, pl.ds(pl.multiple_of(half * hr, 16), hr), :]


def _rs_pair_exchange(g3s, name):
    n = len(g3s)

    def body(*refs):
        ins, outs = refs[:n], refs[n:2 * n]
        send, recv = refs[2 * n:]
        x, y, c, _ = _place()
        cps = []
        for a in range(n):
            hr = g3s[a].shape[1] // 2
            cp = pltpu.make_async_remote_copy(
                src_ref=_half_rows(ins[a], hr, 1 - c), dst_ref=outs[a],
                send_sem=send.at[a], recv_sem=recv.at[a], device_id=(x, y, 1 - c), device_id_type=MESH)
            cp.start()
            cps.append(cp)
        for cp in cps:
            cp.wait()

    return pl.pallas_call(
        body, name=name,
        in_specs=[HBM] * n, out_specs=[HBM] * n,
        out_shape=[jax.ShapeDtypeStruct((g.shape[0], g.shape[1] // 2, g.shape[2]), BF) for g in g3s],
        scratch_shapes=[pltpu.SemaphoreType.DMA((n,)), pltpu.SemaphoreType.DMA((n,))],
    )(*g3s)


def _pair_sums(g3s, lands, cidx, name):
    n = len(g3s)

    def body(c_ref, *refs):
        for g_ref, l_ref, o_ref in zip(refs[:n], refs[n:2 * n], refs[2 * n:]):
            o_ref[...] = (g_ref[...].astype(F32) + l_ref[...].astype(F32)).astype(BF)

    halves = [(g.shape[0], g.shape[1] // 2, g.shape[2]) for g in g3s]
    return pl.pallas_call(
        body, name=name,
        grid_spec=pltpu.PrefetchScalarGridSpec(
            num_scalar_prefetch=1, grid=(1,),
            in_specs=[pl.BlockSpec(h, lambda i, c: (0, c[0], 0)) for h in halves]
                     + [pl.BlockSpec(h, lambda i, c: (0, 0, 0)) for h in halves],
            out_specs=[pl.BlockSpec(h, lambda i, c: (0, 0, 0)) for h in halves]),
        out_shape=[jax.ShapeDtypeStruct(h, BF) for h in halves],
        compiler_params=_params(("arbitrary",)),
    )(cidx, *g3s, *lands)


def _chip_exchange_side(p3s, shapes, axes):
    n = len(p3s)

    def copies(ins, outs, sems):
        send, recv = sems
        x, y, c, chips = _place()
        cps = []
        for a in range(n):
            r, cc = shapes[a]
            for k, (px, py) in enumerate(chips):
                j = 2 * px + py
                src = ins[a].at[j] if axes[a] == 0 else ins[a].at[0, :, pl.ds(pl.multiple_of(j * cc, 128), cc)]
                cps.append(pltpu.make_async_remote_copy(
                    src_ref=src, dst_ref=outs[a].at[k], send_sem=send.at[a, k], recv_sem=recv.at[a, k],
                    device_id=(px, py, c), device_id_type=MESH))
        return cps

    def first(ins, outs, sems):
        for cp in copies(ins, outs, sems):
            cp.start()

    def last(ins, outs, sems):
        for cp in copies(ins, outs, sems):
            cp.wait()

    return _Side(list(p3s), [jax.ShapeDtypeStruct((3, r // 2, c), BF) for r, c in shapes],
                 [pltpu.SemaphoreType.DMA((n, 3)), pltpu.SemaphoreType.DMA((n, 3))], first, last)


def _chip_exchange_direct_side(g3s, shapes, axes):
    n = len(g3s)

    def copies(ins, outs, sems):
        send, recv = sems
        x, y, c, chips = _place()
        sends, arrivals = [], []
        for a in range(n):
            r, cc = shapes[a]
            hr = r // 2

            def part(j, h, a=a, hr=hr, cc=cc):
                rows = pl.ds(pl.multiple_of(h * hr, 16), hr)
                return (ins[a].at[j, rows, :] if axes[a] == 0
                        else ins[a].at[0, rows, pl.ds(pl.multiple_of(j * cc, 128), cc)])

            def move(s, d, i_send, i_recv, dev, a=a):
                return pltpu.make_async_remote_copy(
                    src_ref=s, dst_ref=d, send_sem=send.at[a, i_send], recv_sem=recv.at[a, i_recv], device_id=dev,
                    device_id_type=MESH)

            for k, (px, py) in enumerate(chips):
                for h in range(2):
                    sends.append(move(part(2 * px + py, h), outs[a].at[2 * k + c], 2 * k + h, 2 * k + c, (px, py, h)))
                    slot = outs[a].at[2 * k + h]
                    arrivals.append(move(slot, slot, 2 * k + h, 2 * k + h, (px, py, h)))
            sends.append(move(part(2 * x + y, 1 - c), outs[a].at[6], 6, 6, (x, y, 1 - c)))
            arrivals.append(move(outs[a].at[6], outs[a].at[6], 6, 6, (x, y, 1 - c)))
        return sends, arrivals

    def first(ins, outs, sems):
        for cp in copies(ins, outs, sems)[0]:
            cp.start()

    def last(ins, outs, sems):
        sends, arrivals = copies(ins, outs, sems)
        for cp in arrivals:
            cp.wait_recv()
        for cp in sends:
            cp.wait_send()

    return _Side(list(g3s), [jax.ShapeDtypeStruct((7, r // 2, c), BF) for r, c in shapes],
                 [pltpu.SemaphoreType.DMA((n, 7)), pltpu.SemaphoreType.DMA((n, 7))], first, last)


def _chip_exchange_relay_side(p3, shape):
    r, cc = shape
    hr = r // 2
    rows = 64

    def copies(ins, outs, sems):
        send, recv, local, mine, theirs = sems
        x, y, c, chips = _place()
        near = ((1 - c) * (1 - x) + c * x, (1 - c) * y + c * (1 - y))
        far = ((1 - c) * x + c * (1 - x), (1 - c) * (1 - y) + c * y)
        land, staged = outs

        def block(chip):
            return ins[0].at[0, :, pl.ds(pl.multiple_of((2 * chip[0] + chip[1]) * cc, 128), cc)]

        def move(s, d, k, dev):
            return pltpu.make_async_remote_copy(
                src_ref=s, dst_ref=d, send_sem=send.at[k], recv_sem=recv.at[k], device_id=dev, device_id_type=MESH)

        return dict(
            direct=move(block(near), land.at[c], 0, (*near, c)),
            for_relay=move(block(chips[2]), staged, 1, (*near, c)),
            summed=move(mine, land.at[1 - c], 2, (*far, c)),
            direct_in=move(land.at[c], land.at[c], 0, (*near, c)),
            staged_in=move(staged, staged, 1, (*near, c)),
            summed_in=move(land.at[1 - c], land.at[1 - c], 2, (*far, c)),
            load_mine=pltpu.make_async_copy(block(far), mine, local.at[0]),
            load_theirs=pltpu.make_async_copy(staged, theirs, local.at[1]))

    def first(ins, outs, sems):
        cps = copies(ins, outs, sems)
        cps["for_relay"].start()
        cps["direct"].start()

    def mid(ins, outs, sems):
        cps = copies(ins, outs, sems)
        mine, theirs = sems[3], sems[4]
        cps["load_mine"].start()
        cps["staged_in"].wait_recv()
        cps["load_theirs"].start()
        cps["load_mine"].wait()
        cps["load_theirs"].wait()

        def add(i, carry):
            rs = pl.ds(pl.multiple_of(i * rows, 16), rows)
            mine[rs, :] = (mine[rs, :].astype(F32) + theirs[rs, :].astype(F32)).astype(BF)
            return carry

        lax.fori_loop(0, hr // rows, add, 0)
        cps["summed"].start()

    def last(ins, outs, sems):
        cps = copies(ins, outs, sems)
        cps["direct_in"].wait_recv()
        cps["summed_in"].wait_recv()
        for name in ("direct", "for_relay", "summed"):
            cps[name].wait_send()

    sems = [pltpu.SemaphoreType.DMA((3,)), pltpu.SemaphoreType.DMA((3,)), pltpu.SemaphoreType.DMA((2,)),
            pltpu.VMEM((hr, cc), BF), pltpu.VMEM((hr, cc), BF)]
    return _Side([p3], [jax.ShapeDtypeStruct((2, hr, cc), BF), jax.ShapeDtypeStruct((hr, cc), BF)], sems,
                 first, last, mid)


def _chip_sum(p3, land, shard_shape, axis, idx, name):
    r, c = shard_shape
    hr = r // 2
    tr = 128
    nt = hr // tr
    slots = land.shape[0]

    def body(idx_ref, p_ref, l_ref, o_ref):
        acc = p_ref[...].astype(F32)
        for k in range(slots):
            acc = acc + l_ref[k].astype(F32)
        o_ref[...] = acc

    own = (pl.BlockSpec((None, tr, c), lambda i, idx: (idx[0], i, 0)) if axis == 0
           else pl.BlockSpec((None, tr, c), lambda i, idx: (0, i, idx[0])))
    return pl.pallas_call(
        body, name=name,
        grid_spec=pltpu.PrefetchScalarGridSpec(
            num_scalar_prefetch=1, grid=(nt,),
            in_specs=[own, pl.BlockSpec((slots, tr, c), lambda i, idx: (0, i, 0))],
            out_specs=pl.BlockSpec((tr, c), lambda i, idx: (idx[1] * nt + i, 0))),
        out_shape=jax.ShapeDtypeStruct((r, c), F32),
        compiler_params=_params(("parallel",)),
    )(idx, p3, land)


def _chip_sums(p3s, lands, shapes, axes, idx, name):
    n = len(p3s)

    def body(idx_ref, *refs):
        for p_ref, l_ref, o_ref in zip(refs[:n], refs[n:2 * n], refs[2 * n:]):
            acc = p_ref[...].astype(F32)
            for k in range(l_ref.shape[0]):
                acc = acc + l_ref[k].astype(F32)
            o_ref[...] = acc

    own = [pl.BlockSpec((None, r // 2, c),
                        (lambda i, idx: (idx[0], idx[1], 0)) if ax == 0 else (lambda i, idx: (0, idx[1], idx[0])))
           for (r, c), ax in zip(shapes, axes)]
    return pl.pallas_call(
        body, name=name,
        grid_spec=pltpu.PrefetchScalarGridSpec(
            num_scalar_prefetch=1, grid=(1,),
            in_specs=own + [pl.BlockSpec(l.shape, lambda i, idx: (0, 0, 0)) for l in lands],
            out_specs=[pl.BlockSpec((r // 2, c), lambda i, idx: (idx[1], 0)) for r, c in shapes]),
        out_shape=[jax.ShapeDtypeStruct((r, c), F32) for r, c in shapes],
        compiler_params=_params(("arbitrary",)),
    )(idx, *p3s, *lands)


def _rs_pair_gather(fulls, small):
    n = len(fulls)

    def body(*refs):
        ins, small_refs, outs, red_ref = refs[:n], refs[n:n + 5], refs[n + 5:2 * n + 5], refs[2 * n + 5]
        send, recv = refs[2 * n + 6:2 * n + 8]
        x, y, c, _ = _place()
        cps = []
        for a in range(n):
            hr = fulls[a].shape[0] // 2
            rows = pl.ds(pl.multiple_of(c * hr, 8), hr)
            cp = pltpu.make_async_remote_copy(
                src_ref=ins[a].at[rows, :], dst_ref=outs[a].at[rows, :], send_sem=send.at[a], recv_sem=recv.at[a],
                device_id=(x, y, 1 - c), device_id_type=MESH)
            cp.start()
            cps.append(cp)
        _small_all_reduce(small_refs, red_ref, *refs[2 * n + 8:])
        for a, cp in enumerate(cps):
            cp.wait_send()
            hr = fulls[a].shape[0] // 2
            other = pl.ds(pl.multiple_of((1 - c) * hr, 8), hr)
            pltpu.make_async_remote_copy(
                src_ref=ins[a].at[other, :], dst_ref=outs[a].at[other, :], send_sem=send.at[a], recv_sem=recv.at[a],
                device_id=(x, y, 1 - c), device_id_type=MESH).wait_recv()

    vm = pl.BlockSpec(memory_space=pltpu.VMEM)
    out = pl.pallas_call(
        body, name="grads_pair_gather",
        in_specs=[HBM] * n + [vm] * 5, out_specs=[HBM] * n + [vm],
        out_shape=[jax.ShapeDtypeStruct(f.shape, F32) for f in fulls] + [jax.ShapeDtypeStruct((NSMALL, D), F32)],
        input_output_aliases={a: a for a in range(n)},
        scratch_shapes=[pltpu.SemaphoreType.DMA((n,)), pltpu.SemaphoreType.DMA((n,)),
                        pltpu.VMEM((NSMALL, D), F32), pltpu.VMEM((8, NSMALL, D), F32),
                        pltpu.SemaphoreType.DMA((7,)), pltpu.SemaphoreType.DMA((7,))],
    )(*fulls, *small)
    return out[:n], out[n]


def _sibling_send_side(arr):
    def copy(ins, outs, sems):
        x, y, c, _ = _place()
        return pltpu.make_async_remote_copy(
            src_ref=ins[0], dst_ref=outs[0], send_sem=sems[0].at[0], recv_sem=sems[1].at[0],
            device_id=(x, y, 1 - c), device_id_type=MESH)

    return _Side([arr], [jax.ShapeDtypeStruct(arr.shape, arr.dtype)],
                 [pltpu.SemaphoreType.DMA((1,)), pltpu.SemaphoreType.DMA((1,))],
                 lambda ins, outs, sems: copy(ins, outs, sems).start(),
                 lambda ins, outs, sems: copy(ins, outs, sems).wait())


def _add_bf16(a, b, name):
    r, c = a.shape
    tr = 128

    def body(a_ref, b_ref, o_ref):
        o_ref[...] = (a_ref[...].astype(F32) + b_ref[...].astype(F32)).astype(BF)

    blk = pl.BlockSpec((tr, c), lambda i: (i, 0))
    return pl.pallas_call(
        body, name=name, grid=(r // tr,), in_specs=[blk, blk], out_specs=blk,
        out_shape=jax.ShapeDtypeStruct((r, c), BF), compiler_params=_params(("parallel",)),
    )(a, b)


def _rs_partials(grads, shapes, axes, tag):
    cidx = jnp.reshape(lax.axis_index("c"), (1,)).astype(jnp.int32)
    g3s = [_as3d(g, s, ax) for g, s, ax in zip(grads, shapes, axes)]
    lands = _rs_pair_exchange(g3s, f"grads_pair_exchange_{tag}")
    return list(_pair_sums(g3s, lands, cidx, f"pair_sums_{tag}"))


def _rs_finish(p3s, landed, shapes, axes, small):
    x, y, c = lax.axis_index("x"), lax.axis_index("y"), lax.axis_index("c")
    idx = jnp.stack([2 * x + y, c]).astype(jnp.int32)
    fulls = [_chip_sum(p3s[0], landed[0], shapes[0], axes[0], idx, "chip_sum_w_in")]
    fulls += _chip_sums(p3s[1:], landed[1:], shapes[1:], axes[1:], idx, "chip_sums_branches_out")
    return _rs_pair_gather(fulls, small)


NSMALL = 8


def _small_all_reduce(small_refs, out_ref, pack_ref, buf_ref, send, recv):
    nw_ref, lb_ref, hn_ref, wf_ref, ls_ref = small_refs
    x, y, c = lax.axis_index("x"), lax.axis_index("y"), lax.axis_index("c")
    me = 4 * x + 2 * y + c
    pack_ref[...] = jnp.zeros_like(pack_ref)
    pack_ref[0:1, :] = nw_ref[...]
    pack_ref[1:2, :] = lb_ref[...]
    pack_ref[2:3, 0:HK] = hn_ref[...]
    pack_ref[3:4, :] = wf_ref[...]
    pack_ref[4:5, :] = ls_ref[...]
    buf_ref[me] = pack_ref[...]
    cps = []
    for d in range(1, 8):
        dx, dy, dc = d >> 2, (d >> 1) & 1, d & 1
        peer = (1 - x if dx else x, 1 - y if dy else y, 1 - c if dc else c)
        cp = pltpu.make_async_remote_copy(
            src_ref=pack_ref, dst_ref=buf_ref.at[me], send_sem=send.at[d - 1], recv_sem=recv.at[d - 1],
            device_id=peer, device_id_type=MESH)
        cp.start()
        cps.append(cp)
    for d in range(1, 8):
        dx, dy, dc = d >> 2, (d >> 1) & 1, d & 1
        src = 4 * (1 - x if dx else x) + 2 * (1 - y if dy else y) + (1 - c if dc else c)
        pltpu.make_async_remote_copy(
            src_ref=pack_ref, dst_ref=buf_ref.at[src], send_sem=send.at[d - 1], recv_sem=recv.at[d - 1],
            device_id=(x, y, c), device_id_type=MESH).wait_recv()
    for cp in cps:
        cp.wait_send()
    acc = buf_ref[0]
    for i in range(1, 8):
        acc = acc + buf_ref[i]
    out_ref[...] = acc


def _adamw_math(w, g, m, v):
    m = B1 * m + (1.0 - B1) * g
    v = B2 * v + (1.0 - B2) * (g * g)
    m_hat = m / (1.0 - B1 ** STEP)
    v_hat = v / (1.0 - B2 ** STEP)
    return -LR * (m_hat / (jnp.sqrt(v_hat) + ADAM_EPS) + WD * w), m, v


def _adamw(w, g, m, v, name):
    r, c = w.shape
    tr = 128

    def body(w_ref, g_ref, m_ref, v_ref, d_ref, nm_ref, nv_ref, go_ref):
        g = g_ref[...]
        d_ref[...], nm_ref[...], nv_ref[...] = _adamw_math(w_ref[...], g, m_ref[...], v_ref[...])
        go_ref[...] = g

    blk = pl.BlockSpec((tr, c), lambda i: (i, 0))
    return pl.pallas_call(
        body, name=name, grid=(r // tr,), in_specs=[blk] * 4, out_specs=[blk] * 4,
        out_shape=[jax.ShapeDtypeStruct((r, c), F32)] * 4,
        compiler_params=_params(("parallel",)),
    )(w, g, m, v)


def _adamw_whole(groups, name):
    n = len(groups)

    def body(*refs):
        ins, outs = refs[:4 * n], refs[4 * n:]
        for a in range(n):
            w_ref, g_ref, m_ref, v_ref = ins[4 * a:4 * a + 4]
            g = g_ref[...]
            outs[4 * a][...], outs[4 * a + 1][...], outs[4 * a + 2][...] = _adamw_math(
                w_ref[...], g, m_ref[...], v_ref[...])
            outs[4 * a + 3][...] = g

    vm = pl.BlockSpec(memory_space=pltpu.VMEM)
    out = pl.pallas_call(
        body, name=name, in_specs=[vm] * (4 * n), out_specs=[vm] * (4 * n),
        out_shape=[jax.ShapeDtypeStruct(grp[0].shape, F32) for grp in groups for _ in range(4)],
        compiler_params=_params(),
    )(*[a for grp in groups for a in grp])
    return [out[4 * a:4 * a + 4] for a in range(n)]


def _small_update(red, lbl, params):
    def body(red_ref, *refs):
        ins, outs = refs[:12], refs[12:]
        lb = _lower_bound(ins[3][...])
        dl0 = red_ref[1:2, :] * lb * (1.0 - lb)
        row = lax.broadcasted_iota(jnp.int32, (2, D), 0)
        grads = [red_ref[0:1, :], jnp.where(row == 0, dl0, -dl0), red_ref[2:3, 0:HK], red_ref[3:4, :]]
        for i, g in enumerate(grads):
            w, m, v = ins[3 * i][...], ins[3 * i + 1][...], ins[3 * i + 2][...]
            d, nm, nv = _adamw_math(w, g, m, v)
            outs[4 * i][...] = g
            outs[4 * i + 1][...] = d
            outs[4 * i + 2][...] = nm
            outs[4 * i + 3][...] = nv
        outs[16][...] = jnp.sum(red_ref[4:5, :], axis=1, keepdims=True)

    flat = [a for p in params for a in p]
    vm = pl.BlockSpec(memory_space=pltpu.VMEM)
    shapes = [jax.ShapeDtypeStruct(p[0].shape, F32) for p in params for _ in range(4)]
    return pl.pallas_call(
        body, name="small_update",
        in_specs=[vm] * 13, out_specs=[vm] * 17,
        out_shape=shapes + [jax.ShapeDtypeStruct((1, 1), F32)],
    )(red, *flat)


def kernel(x, positions, norm_w, w_in, lb_logits, hgrn_norm_w, w_branch_a, w_branch_b, w_out, final_norm_w, loss_target, m_norm_w, m_w_in, m_lb_logits, m_hgrn_norm_w, m_w_branch_a, m_w_branch_b, m_w_out, m_final_norm_w, v_norm_w, v_w_in, v_lb_logits, v_hgrn_norm_w, v_w_branch_a, v_w_branch_b, v_w_out, v_final_norm_w):
    big_w = [w_in[0], w_branch_a[0], w_branch_b[0], w_out[0]]
    big_m = [m_w_in[0], m_w_branch_a[0], m_w_branch_b[0], m_w_out[0]]
    big_v = [v_w_in[0], v_w_branch_a[0], v_w_branch_b[0], v_w_out[0]]
    shapes = [w.shape for w in big_w]
    wf = final_norm_w.reshape(1, D)

    shards = [big_w[0].astype(BF)] + big_w[1:]
    loc = _local_step(x[0], positions.reshape(T, 1), norm_w, lb_logits, hgrn_norm_w, wf, loss_target[0],
                      *shards, shard_shapes=shapes)
    g_big = [loc["g_win"], loc["g_wa"], loc["g_wb"], loc["g_wout"]]
    red = loc["small_sums"]

    small = _small_update(red, lb_logits, [
        (norm_w, m_norm_w, v_norm_w), (lb_logits, m_lb_logits, v_lb_logits),
        (hgrn_norm_w, m_hgrn_norm_w, v_hgrn_norm_w),
        (wf, m_final_norm_w.reshape(1, D), v_final_norm_w.reshape(1, D))])
    loss = small[16].reshape(())
    sg, sd, sm, sv = ([small[4 * i + j] for i in range(4)] for j in range(4))
    for lst in (sg, sd, sm, sv):
        lst[3] = lst[3].reshape(D)
    per_w = list(zip(big_w, g_big, big_m, big_v))
    upd = [_adamw(*per_w[0], "adamw_w_in")] + _adamw_whole(per_w[1:], "adamw_branches_out")
    bd, bm, bv, bg = ([u[j][None] for u in upd] for j in range(4))

    def order(s, b):
        return [s[0], b[0], s[1], s[2], b[1], b[2], b[3], s[3]]

    return (loss, loc["gx"][None], *order(sg, bg), *order(sd, bd), *order(sm, bm), *order(sv, bv))
```

```python
import functools

import jax
import jax.numpy as jnp
from jax import lax
from jax.experimental import pallas as pl
from jax.experimental.pallas import tpu as pltpu

T = 2048
D = 1024
NIN = 11264
HEADS = 8
HK = 128
CH = 16
NCH = T // CH
HSTEP = 2
ATT_GROUPS = ((128, 1), (512, 4), (2048, 16))
ATT_COL0 = 4096
AG_COL0 = 8704
GATE_COL0 = 9216
EPS = 1e-6
ROPE_THETA = 10000.0
LR, B1, B2, ADAM_EPS, WD, STEP = 0.001, 0.9, 0.999, 1e-08, 0.01, 10

F32 = jnp.float32
BF = jnp.bfloat16
VMEM_LIMIT = 56 * 1024 * 1024

_NN = (((1,), (0,)), ((), ()))
_NT = (((1,), (1,)), ((), ()))
_TN = (((0,), (0,)), ((), ()))


def _dot(a, b, dims=_NN):
    return lax.dot_general(a, b, dims, preferred_element_type=F32)


def _bdot(a, b, dims=_NN):
    return lax.dot_general(a.astype(BF), b.astype(BF), dims, preferred_element_type=F32)


def _sigmoid(x):
    return jax.nn.sigmoid(x)


def _params(sem=None):
    return pltpu.CompilerParams(dimension_semantics=sem, vmem_limit_bytes=VMEM_LIMIT)


def _matmul(a, b, *, ta=False, tb=False, out_dtype=F32, tm=512, tn=512, tk=None, name, side=None):
    m = a.shape[1] if ta else a.shape[0]
    kdim = a.shape[0] if ta else a.shape[1]
    n = b.shape[0] if tb else b.shape[1]
    tk = tk or kdim
    tm, tn = min(tm, m), min(tn, n)
    nm, nn, nk = m // tm, n // tn, kdim // tk
    dims = (((0 if ta else 1,), (1 if tb else 0,)), ((), ()))
    s_arrays, s_in_specs, s_shapes, s_out_specs, s_sems = _side_io(side)
    na, no = len(s_arrays), len(s_shapes)
    nacc = 1 if nk > 1 else 0

    def body(*refs):
        a_ref, b_ref = refs[:2]
        s_ins, o_ref, s_outs = refs[2:2 + na], refs[2 + na], refs[3 + na:3 + na + no]
        scratch = refs[3 + na + no:]
        s_sem_refs = scratch[nacc:]
        i, j, k = pl.program_id(0), pl.program_id(1), pl.program_id(2)
        if side is not None:
            @pl.when((i == 0) & (j == 0) & (k == 0))
            def _():
                side.first(s_ins, s_outs, s_sem_refs)

        prod = _bdot(a_ref[...], b_ref[...], dims)
        if nk == 1:
            o_ref[...] = prod.astype(out_dtype)
        else:
            acc = scratch[0]

            @pl.when(k == 0)
            def _():
                acc[...] = prod

            @pl.when(k > 0)
            def _():
                acc[...] += prod

            @pl.when(k == nk - 1)
            def _():
                o_ref[...] = acc[...].astype(out_dtype)

        if side is not None:
            @pl.when((i == nm - 1) & (j == nn - 1) & (k == nk - 1))
            def _():
                side.last(s_ins, s_outs, s_sem_refs)

    a_spec = pl.BlockSpec((tk, tm), lambda i, j, k: (k, i)) if ta else pl.BlockSpec((tm, tk), lambda i, j, k: (i, k))
    b_spec = pl.BlockSpec((tn, tk), lambda i, j, k: (j, k)) if tb else pl.BlockSpec((tk, tn), lambda i, j, k: (k, j))
    sem = ("parallel", "parallel", "arbitrary") if side is None else ("arbitrary",) * 3
    out = pl.pallas_call(
        body, name=name, grid=(nm, nn, nk),
        in_specs=[a_spec, b_spec] + s_in_specs,
        out_specs=[pl.BlockSpec((tm, tn), lambda i, j, k: (i, j))] + s_out_specs,
        out_shape=[jax.ShapeDtypeStruct((m, n), out_dtype)] + s_shapes,
        scratch_shapes=([pltpu.VMEM((tm, tn), F32)] if nk > 1 else []) + s_sems,
        compiler_params=_params(sem),
    )(a, b, *s_arrays)
    return out[0] if side is None else (out[0], out[1:])


DZ_TILE = 512


def _part_offsets(parts):
    counts = [p.shape[1] // DZ_TILE for p in parts]
    offs = [sum(counts[:i]) for i in range(len(parts))]
    return counts, offs


def _part_spec(rows, cnt, off, tile_axis):
    def index(*g):
        return (0 if rows is None else g[0], jnp.clip(g[tile_axis] - off, 0, cnt - 1))
    return index


def _grad_w_in(h, parts):
    counts, offs = _part_offsets(parts)
    n = len(parts)

    def body(h_ref, *refs):
        o_ref = refs[n]
        j = pl.program_id(0)
        for p_ref, cnt, off in zip(refs[:n], counts, offs):
            @pl.when((j >= off) & (j < off + cnt))
            def _(p_ref=p_ref):
                o_ref[...] = _bdot(h_ref[...], p_ref[...], _TN).astype(BF)

    return pl.pallas_call(
        body, name="g_win", grid=(sum(counts),),
        in_specs=[pl.BlockSpec((T, D), lambda j: (0, 0))] +
                 [pl.BlockSpec((T, DZ_TILE), _part_spec(None, c, o, 0)) for c, o in zip(counts, offs)],
        out_specs=pl.BlockSpec((D, DZ_TILE), lambda j: (0, j)),
        out_shape=jax.ShapeDtypeStruct((D, NIN), BF),
        compiler_params=_params(("parallel",)),
    )(h, *parts)


def _grad_w_in_half(h, parts, half_idx, side=None):
    counts, offs = _part_offsets(parts)
    n = len(parts)
    nj = sum(counts)
    s_arrays, s_in_specs, s_shapes, s_out_specs, s_sems = _side_io(side)
    na, no = len(s_arrays), len(s_shapes)

    def body(idx_ref, h_ref, *refs):
        s_ins, o_ref, s_outs, s_sem_refs = refs[n:n + na], refs[n + na], refs[n + na + 1:n + na + 1 + no], refs[n + na + 1 + no:]
        j = pl.program_id(0)
        if side is not None:
            @pl.when(j == 0)
            def _():
                side.first(s_ins, s_outs, s_sem_refs)

        for p_ref, cnt, off in zip(refs[:n], counts, offs):
            @pl.when((j >= off) & (j < off + cnt))
            def _(p_ref=p_ref):
                o_ref[...] = _bdot(h_ref[...], p_ref[...], _TN).astype(BF)

        if side is not None:
            @pl.when(j == nj - 1)
            def _():
                side.last(s_ins, s_outs, s_sem_refs)

    def part_spec(cnt, off):
        return pl.BlockSpec((T, DZ_TILE), lambda j, idx: (0, jnp.clip(j - off, 0, cnt - 1)))

    out = pl.pallas_call(
        body, name="g_win_half" if side is None else "g_win_half_carrying",
        grid_spec=pltpu.PrefetchScalarGridSpec(
            num_scalar_prefetch=1, grid=(nj,),
            in_specs=[pl.BlockSpec((T, D // 2), lambda j, idx: (0, idx[0]))] +
                     [part_spec(c, o) for c, o in zip(counts, offs)] + s_in_specs,
            out_specs=[pl.BlockSpec((D // 2, DZ_TILE), lambda j, idx: (0, j))] + s_out_specs,
            scratch_shapes=s_sems),
        out_shape=[jax.ShapeDtypeStruct((D // 2, NIN), BF)] + s_shapes,
        compiler_params=_params(("parallel",) if side is None else ("arbitrary",)),
    )(half_idx, h, *parts, *s_arrays)
    return out[0] if side is None else (out[0], out[1:])


def _side_io(side):
    if side is None:
        return [], [], [], [], []
    return (side.arrays, [HBM] * len(side.arrays), side.out_shapes, [HBM] * len(side.out_shapes), side.sems)


def _grad_x(parts, w_in, x, dout, norm_w, side=None):
    counts, offs = _part_offsets(parts)
    n = len(parts)
    tm = 1024
    nm, nk = T // tm, sum(counts)
    s_arrays, s_in_specs, s_shapes, s_out_specs, s_sems = _side_io(side)
    na, no = len(s_arrays), len(s_shapes)

    def body(*refs):
        w_ref, x_ref, dout_ref, nw_ref = refs[n:n + 4]
        s_ins = refs[n + 4:n + 4 + na]
        gx_ref, gw_ref = refs[n + 4 + na:n + 6 + na]
        s_outs = refs[n + 6 + na:n + 6 + na + no]
        acc = refs[n + 6 + na + no]
        s_sem_refs = refs[n + 7 + na + no:]
        i, k = pl.program_id(0), pl.program_id(1)

        @pl.when((i == 0) & (k == 0))
        def _():
            gw_ref[...] = jnp.zeros_like(gw_ref)
            if side is not None:
                side.first(s_ins, s_outs, s_sem_refs)

        @pl.when(k == 0)
        def _():
            acc[...] = jnp.zeros_like(acc)

        if side is not None and side.mid is not None:
            @pl.when((i == nm - 1) & (k == 0))
            def _():
                side.mid(s_ins, s_outs, s_sem_refs)

        for p_ref, cnt, off in zip(refs[:n], counts, offs):
            @pl.when((k >= off) & (k < off + cnt))
            def _(p_ref=p_ref):
                acc[...] += _bdot(p_ref[...], w_ref[...], _NT)

        @pl.when(k == nk - 1)
        def _():
            gw = jnp.zeros((1, D), F32)
            for c in range(tm // BLK):
                rows = pl.ds(BLK * c, BLK)
                xv, dhv = x_ref[rows, :], acc[rows, :]
                r = lax.rsqrt(jnp.mean(xv * xv, axis=-1, keepdims=True) + EPS)
                nrm = xv * r
                dn = dhv * nw_ref[...]
                gw = gw + jnp.sum(dhv * nrm, axis=0, keepdims=True)
                gx_ref[rows, :] = dout_ref[rows, :] + r * (dn - nrm * jnp.mean(dn * nrm, axis=-1, keepdims=True))
            gw_ref[...] += gw

        if side is not None:
            @pl.when((i == nm - 1) & (k == nk - 1))
            def _():
                side.last(s_ins, s_outs, s_sem_refs)

    row = pl.BlockSpec((tm, D), lambda i, k: (i, 0))
    vec = pl.BlockSpec((1, D), lambda i, k: (0, 0))
    out = pl.pallas_call(
        body, name="grad_x", grid=(nm, nk),
        in_specs=[pl.BlockSpec((tm, DZ_TILE), _part_spec(0, c, o, 1)) for c, o in zip(counts, offs)] +
                 [pl.BlockSpec((D, DZ_TILE), lambda i, k: (0, k)), row, row, vec] + s_in_specs,
        out_specs=[row, vec] + s_out_specs,
        out_shape=[jax.ShapeDtypeStruct((T, D), F32), jax.ShapeDtypeStruct((1, D), F32)] + s_shapes,
        scratch_shapes=[pltpu.VMEM((tm, D), F32)] + s_sems,
        compiler_params=_params(("arbitrary", "arbitrary")),
    )(*parts, w_in, x, dout, norm_w, *s_arrays)
    return out[0], out[1], out[2:]


def _norm_and_rope_tables(x, w, pos, invf, side=None, own=None, cast=()):
    tm = 256
    nm = T // tm
    s_arrays, s_in_specs, s_shapes, s_out_specs, s_sems = _side_io(side)
    na, no, nc = len(s_arrays), len(s_shapes), len(cast)
    nz = 0 if own is None else 1
    wsh, blk = own if own is not None else (None, jnp.zeros((1,), jnp.int32))
    first_out = 4 + nz + nc + na

    def body(blk_ref, *refs):
        x_ref, w_ref, pos_ref, invf_ref = refs[:4]
        s_ins = refs[4 + nz + nc:first_out]
        h_ref, cos_ref, sa_ref, sb_ref = refs[first_out:first_out + 4]
        s_outs = refs[first_out + 4 + nz + nc:first_out + 4 + nz + nc + no]
        s_sem_refs = refs[first_out + 4 + nz + nc + no:]

        @pl.when(pl.program_id(0) == 0)
        def _():
            if side is not None:
                side.first(s_ins, s_outs, s_sem_refs)
            for src, dst in zip(refs[4 + nz:4 + nz + nc], refs[first_out + 4 + nz:first_out + 4 + nz + nc]):
                dst[...] = src[...].astype(BF)

        xv = x_ref[...]
        r = lax.rsqrt(jnp.mean(xv * xv, axis=-1, keepdims=True) + EPS)
        h = (xv * r * w_ref[...]).astype(BF)
        h_ref[...] = h
        if own is not None:
            refs[first_out + 4][...] = _dot(h, refs[4][...])
        first = (lax.broadcasted_iota(jnp.int32, (tm, 128), 1) % 64) < 32
        ang = pos_ref[...].astype(F32) * invf_ref[...]
        s = jnp.sin(ang)
        cos_ref[...] = jnp.cos(ang)
        sa_ref[...] = jnp.where(first, -s, 0.0)
        sb_ref[...] = jnp.where(first, 0.0, s)
        if side is not None:
            @pl.when(pl.program_id(0) == nm - 1)
            def _():
                side.last(s_ins, s_outs, s_sem_refs)

    tab = pl.BlockSpec((tm, 128), lambda i, b: (i, 0))
    own_in = [] if own is None else [pl.BlockSpec(wsh.shape, lambda i, b: (0, 0))]
    own_out = [] if own is None else [pl.BlockSpec((tm, wsh.shape[1]), lambda i, b: (i, b[0]))]
    own_shape = [] if own is None else [jax.ShapeDtypeStruct((T, NIN), F32)]
    whole = [pl.BlockSpec(a.shape, lambda i, b: (0, 0)) for a in cast]
    out = pl.pallas_call(
        body, name="norm_and_rope_tables",
        grid_spec=pltpu.PrefetchScalarGridSpec(
            num_scalar_prefetch=1, grid=(nm,),
            in_specs=[pl.BlockSpec((tm, D), lambda i, b: (i, 0)), pl.BlockSpec((1, D), lambda i, b: (0, 0)),
                      pl.BlockSpec((tm, 1), lambda i, b: (i, 0)), pl.BlockSpec((1, 128), lambda i, b: (0, 0))]
                     + own_in + whole + s_in_specs,
            out_specs=[pl.BlockSpec((tm, D), lambda i, b: (i, 0)), tab, tab, tab] + own_out + whole + s_out_specs,
            scratch_shapes=s_sems),
        out_shape=[jax.ShapeDtypeStruct((T, D), BF)] + [jax.ShapeDtypeStruct((T, 128), F32)] * 3 + own_shape
                  + [jax.ShapeDtypeStruct(a.shape, BF) for a in cast] + s_shapes,
        compiler_params=_params(("parallel",) if side is None and not cast else ("arbitrary",)),
    )(blk, x, w, pos, invf, *([] if own is None else [wsh]), *cast, *s_arrays)
    return (out[0], out[1], out[2], out[3], (out[4] if own is not None else None), out[4 + nz:4 + nz + nc],
            out[4 + nz + nc:])


def _z_blocks(h, w, z, idx, nb, side, name, fill=None):
    tm, tn = 1024, NIN // 8
    s_arrays, s_in_specs, s_shapes, s_out_specs, s_sems = _side_io(side)
    na, no = len(s_arrays), len(s_shapes)
    nm, ns = T // tm, 2 * nb
    nf = 0 if fill is None else 1

    def col(first, i, s, b):
        return (0, b[first + s // 2] * 2 + s % 2)

    def body(idx_ref, h_ref, w_ref, zin_ref, *refs):
        s_ins = refs[nf:nf + na]
        o_ref = refs[nf + na]
        s_outs = refs[nf + na + 1 + nf:nf + na + 1 + nf + no]
        s_sem_refs = refs[nf + na + 1 + nf + no + nf:]
        i, s = pl.program_id(0), pl.program_id(1)

        if side is not None:
            @pl.when((i == 0) & (s == 0))
            def _():
                side.first(s_ins, s_outs, s_sem_refs)

        if fill is not None:
            tile = pl.ds(pl.multiple_of((idx_ref[0] * 2 + s) * tn, 128), tn)
            store = pltpu.make_async_copy(w_ref, refs[nf + na + 1].at[:, tile], refs[nf + na + 1 + nf + no].at[0])
            pl.when(i == 0)(store.start)
        o_ref[...] = _dot(h_ref[...], w_ref[...])
        if fill is not None:
            pl.when(i == 0)(store.wait)

        if side is not None:
            @pl.when((i == nm - 1) & (s == ns - 1))
            def _():
                side.last(s_ins, s_outs, s_sem_refs)

    fills = [] if fill is None else [fill]
    out = pl.pallas_call(
        body, name=name,
        grid_spec=pltpu.PrefetchScalarGridSpec(
            num_scalar_prefetch=1, grid=(nm, ns),
            in_specs=[pl.BlockSpec((tm, D), lambda i, s, b: (i, 0)), pl.BlockSpec((D, tn), functools.partial(col, nb)),
                      HBM] + [HBM] * nf + s_in_specs,
            out_specs=[pl.BlockSpec((tm, tn), lambda i, s, b: (i, col(0, i, s, b)[1]))] + [HBM] * nf + s_out_specs,
            scratch_shapes=[pltpu.SemaphoreType.DMA((1,))] * nf + s_sems),
        out_shape=[jax.ShapeDtypeStruct((T, NIN), F32)] + [jax.ShapeDtypeStruct(f.shape, f.dtype) for f in fills]
                  + s_shapes,
        input_output_aliases={3: 0, **({4: 1} if fill is not None else {})},
        compiler_params=_params(("arbitrary", "arbitrary")),
    )(idx, h, w, z, *fills, *s_arrays)
    return (out[0], *out[1:1 + nf], out[1 + nf:])


def _lower_bound(lbl):
    mx = jnp.max(lbl, axis=0, keepdims=True)
    e = jnp.exp(lbl - mx)
    return e[0:1] / jnp.sum(e, axis=0, keepdims=True)


def _cumsum_rows(g, rows):
    b = g
    sh = 1
    while sh < CH:
        b = b + jnp.where(rows >= sh, pltpu.roll(b, sh, axis=0), 0.0)
        sh *= 2
    return b


def _rev_cumsum_rows(g, rows):
    b = g
    sh = 1
    while sh < CH:
        b = b + jnp.where(rows < CH - sh, pltpu.roll(b, CH - sh, axis=0), 0.0)
        sh *= 2
    return b


SUB = CH // 2


def _direct_block(qb, kb, vb, bb, rows8):
    ob = jnp.zeros_like(qb)
    for s in range(SUB):
        e_s = jnp.exp(jnp.where(rows8 >= s, bb - bb[s:s + 1], -jnp.inf))
        ob = ob + jnp.sum(qb * e_s * kb[s:s + 1], axis=1, keepdims=True) * vb[s:s + 1]
    return ob


def _direct_block_bwd(qb, kb, vb, bb, dob, rows8, rowc8):
    dq = dk = dv = db = jnp.zeros_like(qb)
    for s in range(SUB):
        one = (rowc8 == s).astype(F32)
        ks, vs = kb[s:s + 1], vb[s:s + 1]
        e_s = jnp.exp(jnp.where(rows8 >= s, bb - bb[s:s + 1], -jnp.inf))
        qes = qb * e_s
        w = qes * ks
        a = jnp.sum(w, axis=1, keepdims=True)
        da = jnp.sum(dob * vs, axis=1, keepdims=True)
        dv = dv + one * jnp.sum(a * dob, axis=0, keepdims=True)
        dq = dq + da * e_s * ks
        dk = dk + one * jnp.sum(da * qes, axis=0, keepdims=True)
        u = da * w
        db = db + u - one * jnp.sum(u, axis=0, keepdims=True)
    return dq, dk, dv, db


def _cross_factors(q, k, b):
    ref = b[SUB - 1:SUB]
    e_hi, e_lo = jnp.exp(b[SUB:] - ref), jnp.exp(ref - b[:SUB])
    return q[SUB:] * e_hi, k[:SUB] * e_lo, e_hi, e_lo


def _intra_fwd(q, k, v, b, rows8):
    lo = _direct_block(q[:SUB], k[:SUB], v[:SUB], b[:SUB], rows8)
    hi = _direct_block(q[SUB:], k[SUB:], v[SUB:], b[SUB:], rows8)
    qe_hi, ke_lo, _, _ = _cross_factors(q, k, b)
    for s in range(SUB):
        hi = hi + jnp.sum(qe_hi * ke_lo[s:s + 1], axis=1, keepdims=True) * v[s:s + 1]
    return jnp.concatenate([lo, hi], axis=0)


def _intra_bwd(q, k, v, b, do, rows8, rowc8):
    dq_lo, dk_lo, dv_lo, db_lo = _direct_block_bwd(q[:SUB], k[:SUB], v[:SUB], b[:SUB], do[:SUB], rows8, rowc8)
    dq_hi, dk_hi, dv_hi, db_hi = _direct_block_bwd(q[SUB:], k[SUB:], v[SUB:], b[SUB:], do[SUB:], rows8, rowc8)
    qe_hi, ke_lo, e_hi, e_lo = _cross_factors(q, k, b)
    do_hi, v_lo = do[SUB:], v[:SUB]
    dqe = dke = jnp.zeros_like(qe_hi)
    for s in range(SUB):
        one = (rowc8 == s).astype(F32)
        a = jnp.sum(qe_hi * ke_lo[s:s + 1], axis=1, keepdims=True)
        da = jnp.sum(do_hi * v_lo[s:s + 1], axis=1, keepdims=True)
        dv_lo = dv_lo + one * jnp.sum(a * do_hi, axis=0, keepdims=True)
        dqe = dqe + da * ke_lo[s:s + 1]
        dke = dke + one * jnp.sum(da * qe_hi, axis=0, keepdims=True)
    u_hi, u_lo = dqe * qe_hi, dke * ke_lo
    d_ref = jnp.sum(u_lo, axis=0, keepdims=True) - jnp.sum(u_hi, axis=0, keepdims=True)
    db_lo = db_lo - u_lo + (rowc8 == SUB - 1).astype(F32) * d_ref
    cat = lambda lo, hi: jnp.concatenate([lo, hi], axis=0)
    return (cat(dq_lo, dq_hi + dqe * e_hi), cat(dk_lo + dke * e_lo, dk_hi), cat(dv_lo, dv_hi),
            cat(db_lo, db_hi + u_hi))


def _hgrn_fwd(z, lbl, nw):
    def body(hq_ref, hf_ref, hi_ref, hg_ref, lbl_ref, nw_ref, oraw_ref, og_ref, sh_ref, st_ref):
        @pl.when(pl.program_id(0) == 0)
        def _():
            st_ref[...] = jnp.zeros_like(st_ref)

        lb_all = _lower_bound(lbl_ref[...])
        rows = lax.broadcasted_iota(jnp.int32, (CH, HK), 0)
        rows8 = lax.broadcasted_iota(jnp.int32, (SUB, HK), 0)
        nwv = nw_ref[...]
        for cc, h in [(cc, h) for cc in range(HSTEP) for h in range(HEADS)]:
            rs = slice(CH * cc, CH * (cc + 1))
            sl = slice(HK * h, HK * (h + 1))
            lb = lb_all[:, sl]
            hq, hf, v, hg = hq_ref[rs, sl], hf_ref[rs, sl], hi_ref[rs, sl], hg_ref[rs, sl]
            q = hq * _sigmoid(hq)
            f = lb + (1.0 - lb) * _sigmoid(hf)
            k = 1.0 - f
            b = _cumsum_rows(jnp.log(f), rows)
            sh_ref[cc, h] = st_ref[h]
            o = _bdot(q * jnp.exp(b), st_ref[h], _NT) + _intra_fwd(q, k, v, b, rows8)
            bl = b[CH - 1:CH]
            st_ref[h] = st_ref[h] * jnp.exp(bl)
            st_ref[h] += _bdot(v, k * jnp.exp(bl - b), _TN)
            oraw_ref[rs, sl] = o
            nrm = o * lax.rsqrt(jnp.mean(o * o, axis=1, keepdims=True) + EPS)
            og_ref[rs, sl] = (nrm * nwv * (hg * _sigmoid(hg))).astype(BF)

    zblk = lambda c: pl.BlockSpec((CH * HSTEP, D), lambda i, c=c: (i, c))
    return pl.pallas_call(
        body, name="hgrn_fwd", grid=(NCH // HSTEP,),
        in_specs=[zblk(0), zblk(1), zblk(2), zblk(3),
                  pl.BlockSpec((2, D), lambda i: (0, 0)), pl.BlockSpec((1, HK), lambda i: (0, 0))],
        out_specs=[zblk(0), zblk(0),
                   pl.BlockSpec((HSTEP, HEADS, HK, HK), lambda i: (i, 0, 0, 0))],
        out_shape=[jax.ShapeDtypeStruct((T, D), F32), jax.ShapeDtypeStruct((T, D), BF),
                   jax.ShapeDtypeStruct((NCH, HEADS, HK, HK), F32)],
        scratch_shapes=[pltpu.VMEM((HEADS, HK, HK), F32)],
        compiler_params=_params(("arbitrary",)),
    )(z, z, z, z, lbl, nw)


def _hgrn_bwd(z, lbl, nw, oraw, dog, shist):
    hstep = 1

    def body(hq_ref, hf_ref, hi_ref, hg_ref, lbl_ref, nw_ref, oraw_ref, dog_ref, sh_ref,
             dz_ref, dlb_ref, dnw_ref, dst_ref):
        @pl.when(pl.program_id(0) == 0)
        def _():
            dst_ref[...] = jnp.zeros_like(dst_ref)
            dlb_ref[...] = jnp.zeros_like(dlb_ref)
            dnw_ref[...] = jnp.zeros_like(dnw_ref)

        lb_all = _lower_bound(lbl_ref[...])
        rows = lax.broadcasted_iota(jnp.int32, (CH, HK), 0)
        rowc = lax.broadcasted_iota(jnp.int32, (CH, 1), 0)
        rows8 = lax.broadcasted_iota(jnp.int32, (SUB, HK), 0)
        rowc8 = lax.broadcasted_iota(jnp.int32, (SUB, 1), 0)
        nwv = nw_ref[...]
        dnw = jnp.zeros((1, HK), F32)
        for cc, h in [(cc, h) for cc in reversed(range(hstep)) for h in range(HEADS)]:
            rs = slice(CH * cc, CH * (cc + 1))
            sl = slice(HK * h, HK * (h + 1))
            lb = lb_all[:, sl]
            hq, hf, v, hg = hq_ref[rs, sl], hf_ref[rs, sl], hi_ref[rs, sl], hg_ref[rs, sl]
            o, dg_out = oraw_ref[rs, sl], dog_ref[rs, sl]
            sg = _sigmoid(hg)
            sil = hg * sg
            r = lax.rsqrt(jnp.mean(o * o, axis=1, keepdims=True) + EPS)
            nrm = o * r
            d_hg = dg_out * (nrm * nwv) * (sg * (1.0 + hg * (1.0 - sg)))
            dn = dg_out * nwv * sil
            dnw = dnw + jnp.sum(dg_out * nrm * sil, axis=0, keepdims=True)
            do = r * (dn - nrm * jnp.mean(dn * nrm, axis=1, keepdims=True))
            sq = _sigmoid(hq)
            q = hq * sq
            sig = _sigmoid(hf)
            f = lb + (1.0 - lb) * sig
            k = 1.0 - f
            b = _cumsum_rows(jnp.log(f), rows)
            eb = jnp.exp(b)
            qe = q * eb
            bl = b[CH - 1:CH]
            ebl = jnp.exp(bl)
            kdec = jnp.exp(bl - b)
            ke = k * kdec
            dqe = _bdot(do, sh_ref[cc, h])
            dq = dqe * eb
            db = dqe * qe
            dke = _bdot(v, dst_ref[h])
            dv = _bdot(ke, dst_ref[h], _NT)
            dk = dke * kdec
            rr = dke * ke
            db = db - rr
            db_last = (jnp.sum(rr, axis=0, keepdims=True)
                       + ebl * jnp.sum(dst_ref[h] * sh_ref[cc, h], axis=0, keepdims=True))
            dst_ref[h] = dst_ref[h] * ebl
            dst_ref[h] += _bdot(do, qe, _TN)
            dq_i, dk_i, dv_i, db_i = _intra_bwd(q, k, v, b, do, rows8, rowc8)
            dq, dk, dv = dq + dq_i, dk + dk_i, dv + dv_i
            db = db + db_i + (rowc == CH - 1).astype(F32) * db_last
            dgl = _rev_cumsum_rows(db, rows)
            df = dgl / f - dk
            dlb_ref[:, sl] += jnp.sum(df * (1.0 - sig), axis=0, keepdims=True)
            dz_ref[rs, sl] = (dq * (sq * (1.0 + hq * (1.0 - sq)))).astype(BF)
            dz_ref[rs, D + HK * h:D + HK * (h + 1)] = (df * (1.0 - lb) * sig * (1.0 - sig)).astype(BF)
            dz_ref[rs, 2 * D + HK * h:2 * D + HK * (h + 1)] = dv.astype(BF)
            dz_ref[rs, 3 * D + HK * h:3 * D + HK * (h + 1)] = d_hg.astype(BF)
        dnw_ref[...] += dnw

    rev = lambda i: NCH // hstep - 1 - i
    zblk = lambda c: pl.BlockSpec((CH * hstep, D), lambda i, c=c: (rev(i), c))
    return pl.pallas_call(
        body, name="hgrn_bwd", grid=(NCH // hstep,),
        in_specs=[zblk(0), zblk(1), zblk(2), zblk(3),
                  pl.BlockSpec((2, D), lambda i: (0, 0)), pl.BlockSpec((1, HK), lambda i: (0, 0)),
                  zblk(0), zblk(0),
                  pl.BlockSpec((hstep, HEADS, HK, HK), lambda i: (rev(i), 0, 0, 0))],
        out_specs=[pl.BlockSpec((CH * hstep, 4 * D), lambda i: (rev(i), 0)),
                   pl.BlockSpec((1, D), lambda i: (0, 0)), pl.BlockSpec((1, HK), lambda i: (0, 0))],
        out_shape=[jax.ShapeDtypeStruct((T, 4 * D), BF), jax.ShapeDtypeStruct((1, D), F32),
                   jax.ShapeDtypeStruct((1, HK), F32)],
        scratch_shapes=[pltpu.VMEM((HEADS, HK, HK), F32)],
        compiler_params=_params(("arbitrary",)),
    )(z, z, z, z, lbl, nw, oraw, dog, shist)


BLK = 128
NBLK = T // BLK
QK_SCALE = 0.125


def _head_masks():
    lane = lax.broadcasted_iota(jnp.int32, (1, BLK), 1)
    return [(lane < 64).astype(F32), (lane >= 64).astype(F32)]


def _pieces(dil):
    m = T // dil
    out = []
    for r in range(dil):
        for j in range(m // BLK):
            start = r + dil * BLK * j
            rows = pl.ds(start, BLK, stride=dil) if dil > 1 else pl.ds(start, BLK)
            out.append((rows, r * m + BLK * j))
    return out


def _rope(x, c, sa, sb):
    return x * c + pltpu.roll(x, 96, axis=1) * sa + pltpu.roll(x, 32, axis=1) * sb


def _rope_t(d, c, sa, sb):
    return d * c + pltpu.roll(d * sa, 32, axis=1) + pltpu.roll(d * sb, 96, axis=1)


def _rope_and_regroup(dil, q_ref, k_ref, v_ref, tables, stage_q, stage_k, qr_ref, kr_ref, vr_ref):
    cos_ref, sa_ref, sb_ref = tables
    to_q, to_k = (qr_ref, kr_ref) if dil == 1 else (stage_q, stage_k)
    for c in range(T // BLK):
        rows = pl.ds(BLK * c, BLK)
        cs, sa, sb = cos_ref[rows, :], sa_ref[rows, :], sb_ref[rows, :]
        to_q[rows, :] = (_rope(q_ref[rows, :], cs, sa, sb) * QK_SCALE).astype(to_q.dtype)
        to_k[rows, :] = _rope(k_ref[rows, :], cs, sa, sb).astype(to_k.dtype)
    for rows, dst in _pieces(dil):
        drows = pl.ds(dst, BLK)
        if dil > 1:
            qr_ref[drows, :] = stage_q[rows, :].astype(qr_ref.dtype)
            kr_ref[drows, :] = stage_k[rows, :].astype(kr_ref.dtype)
        vr_ref[drows, :] = v_ref[rows, :].astype(vr_ref.dtype)


def _window_bias(bias_ref):
    ii = lax.broadcasted_iota(jnp.int32, (2 * BLK, BLK), 0) % BLK
    jj = lax.broadcasted_iota(jnp.int32, (2 * BLK, BLK), 1)
    bias_ref[0] = jnp.where(jj <= ii, 0.0, -jnp.inf)
    bias_ref[1] = jnp.where(jj >= ii, 0.0, -jnp.inf)


def _blocks(bi):
    if isinstance(bi, int):
        return pl.ds(bi * BLK, BLK), pl.ds(max(bi - 1, 0) * BLK, BLK)
    return (pl.ds(pl.multiple_of(bi * BLK, BLK), BLK),
            pl.ds(pl.multiple_of(jnp.maximum(bi - 1, 0) * BLK, BLK), BLK))


def _stack_heads(x, masks):
    return jnp.concatenate([x * masks[0].astype(x.dtype), x * masks[1].astype(x.dtype)], axis=0).astype(BF)


def _side_steps(side, refs, **when):
    if side is None:
        return
    for stage, cond in when.items():
        if getattr(side, stage) is not None:
            pl.when(cond)(functools.partial(getattr(side, stage), *refs))


def _attn_fwd(z, cos, sa, sb, side=None):
    s_arrays, s_in_specs, s_shapes, s_out_specs, s_sems = _side_io(side)
    na, no = len(s_arrays), len(s_shapes)

    def body(q_ref, k_ref, v_ref, ag_ref, cos_ref, sa_ref, sb_ref, *refs):
        ob_ref, opre_ref, lse_ref, qr_ref, kr_ref, vr_ref = refs[na:na + 6]
        bias_ref, og_ref, lg_ref, otok_ref, ltok_ref, sc_ref = refs[na + 6 + no:na + 12 + no]
        s_refs = (refs[:na], refs[na + 6:na + 6 + no], refs[na + 12 + no:])
        p, g = pl.program_id(0), pl.program_id(1)
        _side_steps(side, s_refs, first=(p == 0) & (g == 0), mid=(p == 1) & (g == 0))
        masks = _head_masks()

        @pl.when(g == 0)
        def _():
            _window_bias(bias_ref)

        def group(gi):
            dil = ATT_GROUPS[gi][1]
            nblk = (T // dil) // BLK
            _rope_and_regroup(dil, q_ref, k_ref, v_ref, (cos_ref, sa_ref, sb_ref), lg_ref.at[0], lg_ref.at[1],
                              qr_ref, kr_ref, vr_ref)

            def scores(bi, slot):
                cur, prev = _blocks(bi)
                q2 = _stack_heads(qr_ref[cur, :], masks)
                sc_ref[slot, 0] = _dot(q2, kr_ref[cur, :], _NT) + bias_ref[0]
                if nblk > 1:
                    sc_ref[slot, 1] = (_dot(q2, kr_ref[prev, :], _NT)
                                       + (bias_ref[1] + jnp.where((bi % nblk) != 0, 0.0, -jnp.inf)))

            def finish(bi, slot):
                cur, prev = _blocks(bi)
                s_c, vc = sc_ref[slot, 0], vr_ref[cur, :]
                if nblk > 1:
                    s_p, vp = sc_ref[slot, 1], vr_ref[prev, :]
                    mx = jnp.max(jnp.maximum(s_c, s_p), axis=1, keepdims=True)
                    p_c, p_p = jnp.exp(s_c - mx), jnp.exp(s_p - mx)
                    den = jnp.sum(p_c + p_p, axis=1, keepdims=True)
                    oh = _dot(p_c.astype(BF), vc) + _dot(p_p.astype(BF), vp)
                else:
                    mx = jnp.max(s_c, axis=1, keepdims=True)
                    p_c = jnp.exp(s_c - mx)
                    den = jnp.sum(p_c, axis=1, keepdims=True)
                    oh = _dot(p_c.astype(BF), vc)
                on = oh / den
                lsev = jnp.broadcast_to(mx + jnp.log(den), (2 * BLK, BLK))
                og_ref[cur, :] = on[:BLK] * masks[0] + on[BLK:] * masks[1]
                lg_ref[0, cur, :] = lsev[:BLK]
                lg_ref[1, cur, :] = lsev[BLK:]

            def pair(j, carry):
                finish(2 * j, 0)
                scores(2 * j + 1, 1)
                finish(2 * j + 1, 1)
                scores(jnp.minimum(2 * j + 2, NBLK - 1), 0)
                return carry

            scores(0, 0)
            lax.fori_loop(0, NBLK // 2, pair, 0)
            for rows, src in _pieces(dil):
                srows = pl.ds(src, BLK)
                otok_ref[gi, rows, :] = og_ref[srows, :]
                ltok_ref[gi, 0, rows, :] = lg_ref[0, srows, :]
                ltok_ref[gi, 1, rows, :] = lg_ref[1, srows, :]

        for gi in range(3):
            pl.when(g == gi)(functools.partial(group, gi))

        @pl.when(g == 2)
        def _():
            for c in range(T // BLK):
                rows = pl.ds(BLK * c, BLK)
                wts = []
                for hh in range(2):
                    l0, l1, l2 = ltok_ref[0, hh, rows, :], ltok_ref[1, hh, rows, :], ltok_ref[2, hh, rows, :]
                    mx = jnp.maximum(jnp.maximum(l0, l1), l2)
                    e0, e1, e2 = jnp.exp(l0 - mx), jnp.exp(l1 - mx), jnp.exp(l2 - mx)
                    tot = e0 + e1 + e2
                    lse_ref[rows, BLK * hh:BLK * (hh + 1)] = mx + jnp.log(tot)
                    inv = 1.0 / tot
                    wts.append([e0 * inv, e1 * inv, e2 * inv])
                o = sum((wts[0][gi] * masks[0] + wts[1][gi] * masks[1]) * otok_ref[gi, rows, :] for gi in range(3))
                ag = ag_ref[rows, :]
                opre_ref[rows, :] = o
                ob_ref[rows, :] = (o * (ag * _sigmoid(ag))).astype(BF)

        _side_steps(side, s_refs, last=(p == 3) & (g == 2))

    c0 = ATT_COL0 // BLK
    zspec = lambda part: pl.BlockSpec((T, BLK), lambda p, g, part=part: (0, c0 + 12 * part + 4 * g + p))
    outspec = pl.BlockSpec((T, BLK), lambda p, g: (0, p))
    table = pl.BlockSpec((T, BLK), lambda p, g: (0, 0))
    regrouped = pl.BlockSpec((None, T, BLK), lambda p, g: (g, 0, p))
    big = lambda: pltpu.VMEM((T, BLK), F32)
    out = pl.pallas_call(
        body, name="attn_fwd", grid=(4, 3),
        in_specs=[zspec(0), zspec(1), zspec(2),
                  pl.BlockSpec((T, BLK), lambda p, g: (0, AG_COL0 // BLK + p)), table, table, table] + s_in_specs,
        out_specs=[outspec, outspec, pl.BlockSpec((T, 2 * BLK), lambda p, g: (0, p)), regrouped, regrouped, regrouped]
                  + s_out_specs,
        out_shape=[jax.ShapeDtypeStruct((T, 512), BF), jax.ShapeDtypeStruct((T, 512), F32),
                   jax.ShapeDtypeStruct((T, 8 * BLK), F32)] + [jax.ShapeDtypeStruct((3, T, 512), BF)] * 3 + s_shapes,
        scratch_shapes=[pltpu.VMEM((2, 2 * BLK, BLK), F32), big(),
                        pltpu.VMEM((2, T, BLK), F32), pltpu.VMEM((3, T, BLK), F32), pltpu.VMEM((3, 2, T, BLK), F32),
                        pltpu.VMEM((2, 2, 2 * BLK, BLK), F32)] + s_sems,
        compiler_params=_params(("parallel" if side is None else "arbitrary", "arbitrary")),
    )(z, z, z, z, cos, sa, sb, *s_arrays)
    return (*out[:6], out[6:])


def _attn_bwd(z, qs, ks, vs, cos, sa, sb, opre, lse, dob, side=None):
    s_arrays, s_in_specs, s_shapes, s_out_specs, s_sems = _side_io(side)
    na, no = len(s_arrays), len(s_shapes)

    def body(qs_ref, ks_ref, vs_ref, ag_ref, cos_ref, sa_ref, sb_ref, o_ref, lse0_ref, lse1_ref, dob_ref, *refs):
        dq_ref, dk_ref, dv_ref, dag_ref = refs[na:na + 4]
        (bias_ref, dtok_ref, qr_ref, kr_ref, vr_ref, dor_ref, lr_ref, dr_ref,
         dqr_ref, dkr_ref, dvr_ref, pd_ref, dotok_ref) = refs[na + 4 + no:na + 17 + no]
        s_refs = (refs[:na], refs[na + 4:na + 4 + no], refs[na + 17 + no:])
        p, g = pl.program_id(0), pl.program_id(1)
        _side_steps(side, s_refs, first=(p == 0) & (g == 0), mid=(p == 1) & (g == 0))
        masks = _head_masks()

        @pl.when(g == 0)
        def _():
            _window_bias(bias_ref)
            for c in range(T // BLK):
                rows = pl.ds(BLK * c, BLK)
                ag, dob_v, o = ag_ref[rows, :], dob_ref[rows, :], o_ref[rows, :]
                sg = _sigmoid(ag)
                dag_ref[rows, :] = (dob_v * o * (sg * (1.0 + ag * (1.0 - sg)))).astype(BF)
                do = dob_v * (ag * sg)
                dotok_ref[rows, :] = do
                prod = do * o
                for hh, mh in enumerate(masks):
                    dtok_ref[hh, rows, :] = jnp.broadcast_to(jnp.sum(prod * mh, axis=1, keepdims=True), (BLK, BLK))

        def group(gi):
            dil = ATT_GROUPS[gi][1]
            nblk = (T // dil) // BLK
            for rows, dst in _pieces(dil):
                drows = pl.ds(dst, BLK)
                dor_ref[drows, :] = dotok_ref[rows, :]
                for hh, lse_ref in enumerate((lse0_ref, lse1_ref)):
                    lr_ref[hh, drows, :] = lse_ref[rows, :]
                    dr_ref[hh, drows, :] = dtok_ref[hh, rows, :]
            dkr_ref[...] = jnp.zeros_like(dkr_ref)
            dvr_ref[...] = jnp.zeros_like(dvr_ref)

            def probs(bi, slot):
                cur, prev = _blocks(bi)
                q2, do2 = _stack_heads(qs_ref[cur, :], masks), _stack_heads(dor_ref[cur, :], masks)
                lh = jnp.concatenate([lr_ref[0, cur, :], lr_ref[1, cur, :]], axis=0)
                dh = jnp.concatenate([dr_ref[0, cur, :], dr_ref[1, cur, :]], axis=0)
                p_c = jnp.exp(_dot(q2, ks_ref[cur, :], _NT) + bias_ref[0] - lh)
                pd_ref[slot, 0] = p_c.astype(BF)
                pd_ref[slot, 1] = (p_c * (_dot(do2, vs_ref[cur, :], _NT) - dh)).astype(BF)
                if nblk > 1:
                    bias_p = bias_ref[1] + jnp.where((bi % nblk) != 0, 0.0, -jnp.inf)
                    p_p = jnp.exp(_dot(q2, ks_ref[prev, :], _NT) + bias_p - lh)
                    pd_ref[slot, 2] = p_p.astype(BF)
                    pd_ref[slot, 3] = (p_p * (_dot(do2, vs_ref[prev, :], _NT) - dh)).astype(BF)

            def grads(bi, slot):
                cur, prev = _blocks(bi)
                q2, do2 = _stack_heads(qs_ref[cur, :], masks), _stack_heads(dor_ref[cur, :], masks)
                p_c, ds_c = pd_ref[slot, 0], pd_ref[slot, 1]
                dq2 = _dot(ds_c, ks_ref[cur, :])
                dkr_ref[cur, :] += _dot(ds_c, q2, _TN)
                dvr_ref[cur, :] += _dot(p_c, do2, _TN)
                if nblk > 1:
                    p_p, ds_p = pd_ref[slot, 2], pd_ref[slot, 3]
                    dq2 = dq2 + _dot(ds_p, ks_ref[prev, :])
                    dkr_ref[prev, :] += _dot(ds_p, q2, _TN)
                    dvr_ref[prev, :] += _dot(p_p, do2, _TN)
                dqr_ref[cur, :] = dq2[:BLK] * masks[0] + dq2[BLK:] * masks[1]

            def pair(j, carry):
                grads(2 * j, 0)
                probs(2 * j + 1, 1)
                grads(2 * j + 1, 1)
                probs(jnp.minimum(2 * j + 2, NBLK - 1), 0)
                return carry

            probs(0, 0)
            lax.fori_loop(0, NBLK // 2, pair, 0)
            if dil > 1:
                for rows, src in _pieces(dil):
                    srows = pl.ds(src, BLK)
                    qr_ref[rows, :] = dqr_ref[srows, :]
                    kr_ref[rows, :] = dkr_ref[srows, :]
                    vr_ref[rows, :] = dvr_ref[srows, :]
            tq, tk, tv = (qr_ref, kr_ref, vr_ref) if dil > 1 else (dqr_ref, dkr_ref, dvr_ref)
            for c in range(T // BLK):
                rows = pl.ds(BLK * c, BLK)
                cs, sa, sb = cos_ref[rows, :], sa_ref[rows, :], sb_ref[rows, :]
                dq_ref[rows, :] = _rope_t(tq[rows, :] * QK_SCALE, cs, sa, sb).astype(BF)
                dk_ref[rows, :] = _rope_t(tk[rows, :], cs, sa, sb).astype(BF)
                dv_ref[rows, :] = tv[rows, :].astype(BF)

        for gi in range(3):
            pl.when(g == gi)(functools.partial(group, gi))
        _side_steps(side, s_refs, last=(p == 3) & (g == 2))

    regrouped = pl.BlockSpec((None, T, BLK), lambda p, g: (g, 0, p))
    pspec = pl.BlockSpec((T, BLK), lambda p, g: (0, p))
    gspec = pl.BlockSpec((T, BLK), lambda p, g: (0, 4 * g + p))
    table = pl.BlockSpec((T, BLK), lambda p, g: (0, 0))
    big = lambda: pltpu.VMEM((T, BLK), F32)
    two = lambda: pltpu.VMEM((2, T, BLK), F32)
    out = pl.pallas_call(
        body, name="attn_bwd", grid=(4, 3),
        in_specs=[regrouped, regrouped, regrouped,
                  pl.BlockSpec((T, BLK), lambda p, g: (0, AG_COL0 // BLK + p)), table, table, table,
                  pspec, pl.BlockSpec((T, BLK), lambda p, g: (0, 2 * p)),
                  pl.BlockSpec((T, BLK), lambda p, g: (0, 2 * p + 1)), pspec] + s_in_specs,
        out_specs=[gspec, gspec, gspec, pspec] + s_out_specs,
        out_shape=[jax.ShapeDtypeStruct((T, 1536), BF), jax.ShapeDtypeStruct((T, 1536), BF),
                   jax.ShapeDtypeStruct((T, 1536), BF), jax.ShapeDtypeStruct((T, 512), BF)] + s_shapes,
        scratch_shapes=[pltpu.VMEM((2, 2 * BLK, BLK), F32), two(), big(), big(), big(), big(),
                        two(), two(), big(), big(), big(), pltpu.VMEM((2, 4, 2 * BLK, BLK), BF), big()] + s_sems,
        compiler_params=_params(("parallel" if side is None else "arbitrary", "arbitrary")),
    )(qs, ks, vs, z, cos, sa, sb, opre, lse, lse, dob, *s_arrays)
    return (*out[:4], out[4:])


def _merge_out_loss(og, ob, z, w_a, w_b, w_out, x, tgt, wf):
    tm = 512

    def body(og_ref, ob_ref, ga_ref, gb_ref, wa_ref, wb_ref, wo_ref, x_ref, t_ref, wf_ref,
             m_ref, dout_ref, loss_ref, gwf_ref):
        @pl.when(pl.program_id(0) == 0)
        def _():
            loss_ref[...] = jnp.zeros_like(loss_ref)
            gwf_ref[...] = jnp.zeros_like(gwf_ref)

        ya, yb = _dot(og_ref[...], wa_ref[...]), _dot(ob_ref[...], wb_ref[...])
        m =(_sigmoid(ga_ref[...]) * ya + _sigmoid(gb_ref[...]) * yb).astype(BF)
        m_ref[...] = m
        out = x_ref[...] + _dot(m, wo_ref[...])
        r = lax.rsqrt(jnp.mean(out * out, axis=-1, keepdims=True) + EPS)
        yh = out * r
        wfv = wf_ref[...]
        err = yh * wfv - t_ref[...]
        loss_ref[...] += jnp.sum(err * err, axis=0, keepdims=True) * (0.5 / D)
        dy = err * (1.0 / D)
        gwf_ref[...] += jnp.sum(dy * yh, axis=0, keepdims=True)
        dyh = dy * wfv
        dout_ref[...] = r * (dyh - yh * jnp.mean(dyh * yh, axis=-1, keepdims=True))

    row = pl.BlockSpec((tm, D), lambda i: (i, 0))
    vec = pl.BlockSpec((1, D), lambda i: (0, 0))
    whole = lambda w: pl.BlockSpec(w.shape, lambda i: (0, 0))
    return pl.pallas_call(
        body, name="merge_out_loss", grid=(T // tm,),
        in_specs=[row, pl.BlockSpec((tm, ob.shape[1]), lambda i: (i, 0)),
                  pl.BlockSpec((tm, D), lambda i: (i, GATE_COL0 // D)),
                  pl.BlockSpec((tm, D), lambda i: (i, GATE_COL0 // D + 1)),
                  whole(w_a), whole(w_b), whole(w_out), row, row, vec],
        out_specs=[row, row, vec, vec],
        out_shape=[jax.ShapeDtypeStruct((T, D), BF), jax.ShapeDtypeStruct((T, D), F32),
                   jax.ShapeDtypeStruct((1, D), F32), jax.ShapeDtypeStruct((1, D), F32)],
        compiler_params=_params(("arbitrary",)),
    )(og, ob, z, z, w_a, w_b, w_out, x, tgt, wf)


def _merge_proj_bwd(dout, og, ob, z, w_a, w_b, w_out):
    tm = 512

    def body(dout_ref, og_ref, ob_ref, ga_ref, gb_ref, wa_ref, wb_ref, wo_ref,
             dya_ref, dyb_ref, dg_ref, dog_ref, dob_ref):
        dmv = _dot(dout_ref[...].astype(BF), wo_ref[...], _NT)
        sa, sb = _sigmoid(ga_ref[...]), _sigmoid(gb_ref[...])
        dya, dyb = (sa * dmv).astype(BF), (sb * dmv).astype(BF)
        dya_ref[...] = dya
        dyb_ref[...] = dyb
        dg_ref[:, :D] = (dmv * _dot(og_ref[...], wa_ref[...]) * sa * (1.0 - sa)).astype(BF)
        dg_ref[:, D:] = (dmv * _dot(ob_ref[...], wb_ref[...]) * sb * (1.0 - sb)).astype(BF)
        dog_ref[...] = _dot(dya, wa_ref[...], _NT)
        dob_ref[...] = _dot(dyb, wb_ref[...], _NT)

    row = pl.BlockSpec((tm, D), lambda i: (i, 0))
    whole = lambda w: pl.BlockSpec(w.shape, lambda i: (0, 0))
    nb = w_b.shape[0]
    return pl.pallas_call(
        body, name="merge_proj_bwd", grid=(T // tm,),
        in_specs=[row, row, pl.BlockSpec((tm, nb), lambda i: (i, 0)),
                  pl.BlockSpec((tm, D), lambda i: (i, GATE_COL0 // D)),
                  pl.BlockSpec((tm, D), lambda i: (i, GATE_COL0 // D + 1)), whole(w_a), whole(w_b), whole(w_out)],
        out_specs=[row, row, pl.BlockSpec((tm, 2 * D), lambda i: (i, 0)), row,
                   pl.BlockSpec((tm, nb), lambda i: (i, 0))],
        out_shape=[jax.ShapeDtypeStruct((T, D), BF), jax.ShapeDtypeStruct((T, D), BF),
                   jax.ShapeDtypeStruct((T, 2 * D), BF), jax.ShapeDtypeStruct((T, D), F32),
                   jax.ShapeDtypeStruct((T, nb), F32)],
        compiler_params=_params(("parallel",)),
    )(dout, og, ob, z, z, w_a, w_b, w_out)


def _rope_inv_freq():
    inv = ROPE_THETA ** (-jnp.arange(0, 64, 2, dtype=F32) / 64)
    return jnp.tile(inv, 4).reshape(1, BLK)


def _local_step(x, pos, norm_w, lbl, hnw, wf, tgt, w_in, w_a, w_b, w_out, shard_shapes=()):
    invf = _rope_inv_freq()
    if shard_shapes:
        blk = jnp.reshape(2 * lax.axis_index("x") + lax.axis_index("y"), (1,)).astype(jnp.int32)
        h, cos, sa, sb, z_own, (w_a, w_b, w_out), (w_near,) = _norm_and_rope_tables(
            x, norm_w, pos, invf, side=_gather_near_side(w_in, WEIGHT_AXES[0]), own=(w_in, blk),
            cast=(w_a, w_b, w_out))
        near = jnp.concatenate([blk ^ 2, blk ^ 1])
        z, (w_diag,) = _z_blocks(h, w_near, z_own, jnp.concatenate([near, near]), 2, name="z_proj_near",
                                 side=_gather_diag_side(w_near, w_in.shape, WEIGHT_AXES[0]))
        z, w_in, _ = _z_blocks(h, w_diag, z, jnp.concatenate([blk ^ 3, jnp.zeros_like(blk)]), 1, name="z_proj_diag",
                               fill=w_near, side=None)
    else:
        h, cos, sa, sb, _, _, _ = _norm_and_rope_tables(x, norm_w, pos, invf)
        z = _matmul(h, w_in, tm=T, tn=512, name="z_proj")
    oraw, og, shist = _hgrn_fwd(z, lbl, hnw)
    side_a = _gather_side([w_a, w_b, w_out], WEIGHT_AXES[1:]) if shard_shapes else None
    ob, opre, lse, qs, ks, vs, gathered = _attn_fwd(z, cos, sa, sb, side=side_a)
    if shard_shapes:
        w_a, w_b, w_out = gathered
    merged, dout, loss_vec, g_wf = _merge_out_loss(og, ob, z, w_a, w_b, w_out, x, tgt, wf)

    dya, dyb, dgates, dog, dob = _merge_proj_bwd(dout, og, ob, z, w_a, w_b, w_out)
    g_wout = _matmul(merged, dout, ta=True, out_dtype=BF, tm=512, tn=1024, name="g_wout")
    g_wa = _matmul(og, dya, ta=True, out_dtype=BF, tm=512, tn=1024, name="g_wa")
    g_wb = _matmul(ob, dyb, ta=True, out_dtype=BF, tm=512, tn=1024, name="g_wb")
    small = [g_wa, g_wb, g_wout]
    side_s = side_w = None
    if shard_shapes:
        p3_s = [_as3d(g, s, ax) for g, s, ax in zip(small, shard_shapes[1:], WEIGHT_AXES[1:])]
        side_s = _chip_exchange_direct_side(p3_s, shard_shapes[1:], WEIGHT_AXES[1:])
    dz_h, dlb, g_hnw = _hgrn_bwd(z, lbl, hnw, oraw, dog, shist)
    dq, dk, dv, dag, land_s = _attn_bwd(z, qs, ks, vs, cos, sa, sb, opre, lse, dob, side=side_s)
    dz_parts = [dz_h, dq, dk, dv, dag, dgates]
    if shard_shapes:
        c = lax.axis_index("c")
        half = lambda i: jnp.reshape(i, (1,)).astype(jnp.int32)
        g_send = _grad_w_in_half(h, dz_parts, half(1 - c))
        g_keep, (g_sib,) = _grad_w_in_half(h, dz_parts, half(c), side=_sibling_send_side(g_send))
        p3_w = [_add_bf16(g_keep, g_sib, "pair_sum_w_in").reshape(1, D // 2, NIN)]
        side_w = _chip_exchange_relay_side(p3_w[0], shard_shapes[0])
    else:
        g_big = [_grad_w_in(h, dz_parts)] + small
    gx, g_nw, land_w = _grad_x(dz_parts, w_in, x, dout, norm_w, side=side_w)
    small_sums = None
    if shard_shapes:
        g_big, small_sums = _rs_finish(p3_w + p3_s, [land_w[0]] + list(land_s), shard_shapes, WEIGHT_AXES,
                                       (g_nw, dlb, g_hnw, g_wf, loss_vec))
    return dict(loss_vec=loss_vec, gx=gx, g_nw=g_nw, dlb=dlb, g_hnw=g_hnw, g_wf=g_wf, small_sums=small_sums,
                g_win=g_big[0], g_wa=g_big[1], g_wb=g_big[2], g_wout=g_big[3])


MESH = pl.DeviceIdType.MESH
HBM = pl.BlockSpec(memory_space=pl.ANY)
WEIGHT_AXES = (1, 0, 1, 0)


def _place():
    x, y, c = lax.axis_index("x"), lax.axis_index("y"), lax.axis_index("c")
    chips = [(1 - x, y), (x, 1 - y), (1 - x, 1 - y)]
    return x, y, c, chips


def _block_half(ref, shard_shape, axis, j, half):
    r, c = shard_shape
    hr = r // 2
    if axis == 0:
        return ref.at[pl.ds(pl.multiple_of(j * r + half * hr, 16), hr), :]
    return ref.at[pl.ds(pl.multiple_of(half * hr, 16), hr), pl.ds(pl.multiple_of(j * c, 128), c)]


PIECES = 2


def _block_piece(ref, shard_shape, axis, j, half, q):
    r, c = shard_shape
    pr = r // 2 // PIECES
    if axis == 0:
        return ref.at[pl.ds(pl.multiple_of(j * r + half * (r // 2) + q * pr, 16), pr), :]
    return ref.at[pl.ds(pl.multiple_of(half * (r // 2) + q * pr, 16), pr), pl.ds(pl.multiple_of(j * c, 128), c)]


class _Side:
    def __init__(self, arrays, out_shapes, sems, first, last, mid=None):
        self.arrays, self.out_shapes, self.sems, self.first, self.last = arrays, out_shapes, sems, first, last
        self.mid = mid


def _gather_side(shards, axes):
    n = len(shards)
    shapes = [s.shape for s in shards]

    def copies(ins, outs, sems):
        send1, recv1, send2, recv2, send0, recv0 = sems
        x, y, c, chips = _place()
        me = 2 * x + y
        sib = (x, y, 1 - c)
        near = ((1 - c) * (1 - x) + c * x, (1 - c) * y + c * (1 - y))
        far = ((1 - c) * x + c * (1 - x), (1 - c) * (1 - y) + c * y)
        out = []
        for a in range(n):
            r, cc = shapes[a]
            mine = (outs[a].at[pl.ds(pl.multiple_of(me * r, 16), r), :] if axes[a] == 0
                    else outs[a].at[:, pl.ds(pl.multiple_of(me * cc, 128), cc)])
            own = pltpu.make_async_remote_copy(
                src_ref=ins[a], dst_ref=mine, send_sem=send0.at[a], recv_sem=recv0.at[a],
                device_id=sib, device_id_type=MESH)
            src = ins[a].at[pl.ds(pl.multiple_of(c * (r // 2), 16), r // 2), :]
            sends = [pltpu.make_async_remote_copy(
                src_ref=src, dst_ref=_block_half(outs[a], shapes[a], axes[a], me, c),
                send_sem=send1.at[a, k], recv_sem=recv1.at[a, k], device_id=(*chips[k], c), device_id_type=MESH)
                for k in range(2)]

            def region(chip, half):
                return _block_half(outs[a], shapes[a], axes[a], 2 * chip[0] + chip[1], half)

            def arrival(chip, k):
                reg = region(chip, c)
                return pltpu.make_async_remote_copy(
                    src_ref=reg, dst_ref=reg, send_sem=send1.at[a, k], recv_sem=recv1.at[a, k],
                    device_id=(*chip, c), device_id_type=MESH)

            def to_sibling(chip, k):
                reg = region(chip, c)
                return pltpu.make_async_remote_copy(
                    src_ref=reg, dst_ref=reg, send_sem=send2.at[a, k], recv_sem=recv2.at[a, k],
                    device_id=sib, device_id_type=MESH)

            def from_sibling(chip, k):
                reg = region(chip, 1 - c)
                return pltpu.make_async_remote_copy(
                    src_ref=reg, dst_ref=reg, send_sem=send2.at[a, k], recv_sem=recv2.at[a, k],
                    device_id=sib, device_id_type=MESH)

            relay = pltpu.make_async_remote_copy(
                src_ref=region(near, c), dst_ref=region(near, c), send_sem=send1.at[a, 2], recv_sem=recv1.at[a, 2],
                device_id=(*far, c), device_id_type=MESH)
            hops = [(arrival(near, c), to_sibling(near, c)), (arrival(far, 1 - c), to_sibling(far, 1 - c)),
                    (arrival(chips[2], 2), to_sibling(chips[2], 2))]
            back = [from_sibling(chips[k], k) for k in range(3)]
            out.append((own, sends, relay, hops, back))
        return out

    def first(ins, outs, sems):
        for own, sends, _, _, _ in copies(ins, outs, sems):
            own.start()
            for cp in sends:
                cp.start()

    def mid(ins, outs, sems):
        per_array = copies(ins, outs, sems)
        for step in range(2):
            for _, _, relay, hops, _ in per_array:
                arrived, onward = hops[step]
                arrived.wait_recv()
                if step == 0:
                    relay.start()
                onward.start()

    def last(ins, outs, sems):
        per_array = copies(ins, outs, sems)
        for _, _, _, hops, _ in per_array:
            arrived, onward = hops[2]
            arrived.wait_recv()
            onward.start()
        for own, sends, relay, hops, back in per_array:
            for cp in back:
                cp.wait_recv()
            for cp in sends + [relay] + [onward for _, onward in hops]:
                cp.wait_send()
            own.wait()

    full = [(4 * r, c) if ax == 0 else (r, 4 * c) for (r, c), ax in zip(shapes, axes)]
    sems = [pltpu.SemaphoreType.DMA((n, 3)), pltpu.SemaphoreType.DMA((n, 3)),
            pltpu.SemaphoreType.DMA((n, 3)), pltpu.SemaphoreType.DMA((n, 3)),
            pltpu.SemaphoreType.DMA((n,)), pltpu.SemaphoreType.DMA((n,))]
    return _Side(list(shards), [jax.ShapeDtypeStruct(f, BF) for f in full], sems, first, last, mid)


def _gather_near_side(shard, axis):
    shape = shard.shape
    r, cc = shape

    def copies(ins, outs, sems):
        send1, recv1, send2, recv2, send0, recv0 = sems
        x, y, c, chips = _place()
        me = 2 * x + y
        sib = (x, y, 1 - c)
        mine = (outs[0].at[pl.ds(pl.multiple_of(me * r, 16), r), :] if axis == 0
                else outs[0].at[:, pl.ds(pl.multiple_of(me * cc, 128), cc)])
        own = pltpu.make_async_remote_copy(
            src_ref=ins[0], dst_ref=mine, send_sem=send0.at[0], recv_sem=recv0.at[0],
            device_id=sib, device_id_type=MESH)
        def region(k, half, q):
            return _block_piece(outs[0], shape, axis, 2 * chips[k][0] + chips[k][1], half, q)

        def moves(k, q):
            src = ins[0].at[pl.ds(pl.multiple_of(c * (r // 2) + q * (r // 2 // PIECES), 16), r // 2 // PIECES), :]
            return [pltpu.make_async_remote_copy(
                        src_ref=s, dst_ref=d, send_sem=ss.at[k, q], recv_sem=rs.at[k, q], device_id=dev,
                        device_id_type=MESH)
                    for s, d, ss, rs, dev in (
                        (src, _block_piece(outs[0], shape, axis, me, c, q), send1, recv1, (*chips[k], c)),
                        (region(k, c, q), region(k, c, q), send1, recv1, (*chips[k], c)),
                        (region(k, c, q), region(k, c, q), send2, recv2, sib),
                        (region(k, 1 - c, q), region(k, 1 - c, q), send2, recv2, sib))]

        return own, [moves(k, q) for q in range(PIECES) for k in range(2)]

    def first(ins, outs, sems):
        own, per_piece = copies(ins, outs, sems)
        own.start()
        for send, _, _, _ in per_piece:
            send.start()

    def last(ins, outs, sems):
        own, per_piece = copies(ins, outs, sems)
        for _, arrived, onward, _ in per_piece:
            arrived.wait_recv()
            onward.start()
        for send, _, onward, back in per_piece:
            back.wait_recv()
            send.wait_send()
            onward.wait_send()
        own.wait()

    full = (4 * r, cc) if axis == 0 else (r, 4 * cc)
    sems = [pltpu.SemaphoreType.DMA((2, PIECES))] * 4 + [pltpu.SemaphoreType.DMA((1,))] * 2
    return _Side([shard], [jax.ShapeDtypeStruct(full, BF)], sems, first, last)


def _gather_diag_side(gathered, shape, axis):
    r, cc = shape

    def copies(ins, outs, sems):
        send1, recv1, send2, recv2 = sems
        x, y, c, _ = _place()
        sib = (x, y, 1 - c)
        near = ((1 - c) * (1 - x) + c * x, (1 - c) * y + c * (1 - y))
        far = ((1 - c) * x + c * (1 - x), (1 - c) * (1 - y) + c * y)

        def piece(i, q):
            return outs[0].at[pl.ds(pl.multiple_of(i * (r // 2) + q * (r // 2 // PIECES), 16), r // 2 // PIECES), :]

        def moves(q):
            def move(s, d, ss, rs, dev):
                return pltpu.make_async_remote_copy(
                    src_ref=s, dst_ref=d, send_sem=ss.at[q], recv_sem=rs.at[q], device_id=dev, device_id_type=MESH)

            return (move(_block_piece(ins[0], shape, axis, 2 * near[0] + near[1], c, q), piece(c, q), send1, recv1,
                         (*far, c)),
                    move(piece(c, q), piece(c, q), send1, recv1, (*far, c)),
                    move(piece(c, q), piece(c, q), send2, recv2, sib),
                    move(piece(1 - c, q), piece(1 - c, q), send2, recv2, sib))

        return [moves(q) for q in range(PIECES)]

    def first(ins, outs, sems):
        for relay, _, _, _ in copies(ins, outs, sems):
            relay.start()

    def last(ins, outs, sems):
        per_piece = copies(ins, outs, sems)
        for _, arrived, onward, _ in per_piece:
            arrived.wait_recv()
            onward.start()
        for relay, _, onward, back in per_piece:
            back.wait_recv()
            relay.wait_send()
            onward.wait_send()

    return _Side([gathered], [jax.ShapeDtypeStruct(shape, BF)], [pltpu.SemaphoreType.DMA((PIECES,))] * 4,
                 first, last)


def _as3d(g, shard_shape, axis):
    r, c = shard_shape
    return g.reshape(4, r, c) if axis == 0 else g.reshape(1, r, 4 * c)


def _chip_exchange_direct_side(g3s, shapes, axes):
    n = len(g3s)

    def copies(ins, outs, sems):
        send, recv = sems
        x, y, c, chips = _place()
        sends, arrivals = [], []
        for a in range(n):
            r, cc = shapes[a]
            hr = r // 2

            def part(j, h, a=a, hr=hr, cc=cc):
                rows = pl.ds(pl.multiple_of(h * hr, 16), hr)
                return (ins[a].at[j, rows, :] if axes[a] == 0
                        else ins[a].at[0, rows, pl.ds(pl.multiple_of(j * cc, 128), cc)])

            def move(s, d, i_send, i_recv, dev, a=a):
                return pltpu.make_async_remote_copy(
                    src_ref=s, dst_ref=d, send_sem=send.at[a, i_send], recv_sem=recv.at[a, i_recv], device_id=dev,
                    device_id_type=MESH)

            for k, (px, py) in enumerate(chips):
                for h in range(2):
                    sends.append(move(part(2 * px + py, h), outs[a].at[2 * k + c], 2 * k + h, 2 * k + c, (px, py, h)))
                    slot = outs[a].at[2 * k + h]
                    arrivals.append(move(slot, slot, 2 * k + h, 2 * k + h, (px, py, h)))
            sends.append(move(part(2 * x + y, 1 - c), outs[a].at[6], 6, 6, (x, y, 1 - c)))
            arrivals.append(move(outs[a].at[6], outs[a].at[6], 6, 6, (x, y, 1 - c)))
        return sends, arrivals

    def first(ins, outs, sems):
        for cp in copies(ins, outs, sems)[0]:
            cp.start()

    def last(ins, outs, sems):
        sends, arrivals = copies(ins, outs, sems)
        for cp in arrivals:
            cp.wait_recv()
        for cp in sends:
            cp.wait_send()

    return _Side(list(g3s), [jax.ShapeDtypeStruct((7, r // 2, c), BF) for r, c in shapes],
                 [pltpu.SemaphoreType.DMA((n, 7)), pltpu.SemaphoreType.DMA((n, 7))], first, last)


def _chip_exchange_relay_side(p3, shape):
    r, cc = shape
    hr = r // 2
    rows = 64

    def copies(ins, outs, sems):
        send, recv, local, mine, theirs = sems
        x, y, c, chips = _place()
        near = ((1 - c) * (1 - x) + c * x, (1 - c) * y + c * (1 - y))
        far = ((1 - c) * x + c * (1 - x), (1 - c) * (1 - y) + c * y)
        land, staged = outs

        def block(chip):
            return ins[0].at[0, :, pl.ds(pl.multiple_of((2 * chip[0] + chip[1]) * cc, 128), cc)]

        def move(s, d, k, dev):
            return pltpu.make_async_remote_copy(
                src_ref=s, dst_ref=d, send_sem=send.at[k], recv_sem=recv.at[k], device_id=dev, device_id_type=MESH)

        return dict(
            direct=move(block(near), land.at[c], 0, (*near, c)),
            for_relay=move(block(chips[2]), staged, 1, (*near, c)),
            summed=move(mine, land.at[1 - c], 2, (*far, c)),
            direct_in=move(land.at[c], land.at[c], 0, (*near, c)),
            staged_in=move(staged, staged, 1, (*near, c)),
            summed_in=move(land.at[1 - c], land.at[1 - c], 2, (*far, c)),
            load_mine=pltpu.make_async_copy(block(far), mine, local.at[0]),
            load_theirs=pltpu.make_async_copy(staged, theirs, local.at[1]))

    def first(ins, outs, sems):
        cps = copies(ins, outs, sems)
        cps["for_relay"].start()
        cps["direct"].start()

    def mid(ins, outs, sems):
        cps = copies(ins, outs, sems)
        mine, theirs = sems[3], sems[4]
        cps["load_mine"].start()
        cps["staged_in"].wait_recv()
        cps["load_theirs"].start()
        cps["load_mine"].wait()
        cps["load_theirs"].wait()

        def add(i, carry):
            rs = pl.ds(pl.multiple_of(i * rows, 16), rows)
            mine[rs, :] = (mine[rs, :].astype(F32) + theirs[rs, :].astype(F32)).astype(BF)
            return carry

        lax.fori_loop(0, hr // rows, add, 0)
        cps["summed"].start()

    def last(ins, outs, sems):
        cps = copies(ins, outs, sems)
        cps["direct_in"].wait_recv()
        cps["summed_in"].wait_recv()
        for name in ("direct", "for_relay", "summed"):
            cps[name].wait_send()

    sems = [pltpu.SemaphoreType.DMA((3,)), pltpu.SemaphoreType.DMA((3,)), pltpu.SemaphoreType.DMA((2,)),
            pltpu.VMEM((hr, cc), BF), pltpu.VMEM((hr, cc), BF)]
    return _Side([p3], [jax.ShapeDtypeStruct((2, hr, cc), BF), jax.ShapeDtypeStruct((hr, cc), BF)], sems,
                 first, last, mid)


def _chip_sum(p3, land, shard_shape, axis, idx, name):
    r, c = shard_shape
    hr = r // 2
    tr = 128
    nt = hr // tr
    slots = land.shape[0]

    def body(idx_ref, p_ref, l_ref, o_ref):
        acc = p_ref[...].astype(F32)
        for k in range(slots):
            acc = acc + l_ref[k].astype(F32)
        o_ref[...] = acc

    own = (pl.BlockSpec((None, tr, c), lambda i, idx: (idx[0], i, 0)) if axis == 0
           else pl.BlockSpec((None, tr, c), lambda i, idx: (0, i, idx[0])))
    return pl.pallas_call(
        body, name=name,
        grid_spec=pltpu.PrefetchScalarGridSpec(
            num_scalar_prefetch=1, grid=(nt,),
            in_specs=[own, pl.BlockSpec((slots, tr, c), lambda i, idx: (0, i, 0))],
            out_specs=pl.BlockSpec((tr, c), lambda i, idx: (idx[1] * nt + i, 0))),
        out_shape=jax.ShapeDtypeStruct((r, c), F32),
        compiler_params=_params(("parallel",)),
    )(idx, p3, land)


def _chip_sums(p3s, lands, shapes, axes, idx, name):
    n = len(p3s)

    def body(idx_ref, *refs):
        for p_ref, l_ref, o_ref in zip(refs[:n], refs[n:2 * n], refs[2 * n:]):
            acc = p_ref[...].astype(F32)
            for k in range(l_ref.shape[0]):
                acc = acc + l_ref[k].astype(F32)
            o_ref[...] = acc

    own = [pl.BlockSpec((None, r // 2, c),
                        (lambda i, idx: (idx[0], idx[1], 0)) if ax == 0 else (lambda i, idx: (0, idx[1], idx[0])))
           for (r, c), ax in zip(shapes, axes)]
    return pl.pallas_call(
        body, name=name,
        grid_spec=pltpu.PrefetchScalarGridSpec(
            num_scalar_prefetch=1, grid=(1,),
            in_specs=own + [pl.BlockSpec(l.shape, lambda i, idx: (0, 0, 0)) for l in lands],
            out_specs=[pl.BlockSpec((r // 2, c), lambda i, idx: (idx[1], 0)) for r, c in shapes]),
        out_shape=[jax.ShapeDtypeStruct((r, c), F32) for r, c in shapes],
        compiler_params=_params(("arbitrary",)),
    )(idx, *p3s, *lands)


def _rs_pair_gather(fulls, small):
    n = len(fulls)

    def body(*refs):
        ins, small_refs, outs, red_ref = refs[:n], refs[n:n + 5], refs[n + 5:2 * n + 5], refs[2 * n + 5]
        send, recv = refs[2 * n + 6:2 * n + 8]
        x, y, c, _ = _place()
        cps = []
        for a in range(n):
            hr = fulls[a].shape[0] // 2
            rows = pl.ds(pl.multiple_of(c * hr, 8), hr)
            cp = pltpu.make_async_remote_copy(
                src_ref=ins[a].at[rows, :], dst_ref=outs[a].at[rows, :], send_sem=send.at[a], recv_sem=recv.at[a],
                device_id=(x, y, 1 - c), device_id_type=MESH)
            cp.start()
            cps.append(cp)
        _small_all_reduce(small_refs, red_ref, *refs[2 * n + 8:])
        for a, cp in enumerate(cps):
            cp.wait_send()
            hr = fulls[a].shape[0] // 2
            other = pl.ds(pl.multiple_of((1 - c) * hr, 8), hr)
            pltpu.make_async_remote_copy(
                src_ref=ins[a].at[other, :], dst_ref=outs[a].at[other, :], send_sem=send.at[a], recv_sem=recv.at[a],
                device_id=(x, y, 1 - c), device_id_type=MESH).wait_recv()

    vm = pl.BlockSpec(memory_space=pltpu.VMEM)
    out = pl.pallas_call(
        body, name="grads_pair_gather",
        in_specs=[HBM] * n + [vm] * 5, out_specs=[HBM] * n + [vm],
        out_shape=[jax.ShapeDtypeStruct(f.shape, F32) for f in fulls] + [jax.ShapeDtypeStruct((NSMALL, D), F32)],
        input_output_aliases={a: a for a in range(n)},
        scratch_shapes=[pltpu.SemaphoreType.DMA((n,)), pltpu.SemaphoreType.DMA((n,)),
                        pltpu.VMEM((NSMALL, D), F32), pltpu.VMEM((8, NSMALL, D), F32),
                        pltpu.SemaphoreType.DMA((7,)), pltpu.SemaphoreType.DMA((7,))],
    )(*fulls, *small)
    return out[:n], out[n]


def _sibling_send_side(arr):
    def copy(ins, outs, sems):
        x, y, c, _ = _place()
        return pltpu.make_async_remote_copy(
            src_ref=ins[0], dst_ref=outs[0], send_sem=sems[0].at[0], recv_sem=sems[1].at[0],
            device_id=(x, y, 1 - c), device_id_type=MESH)

    return _Side([arr], [jax.ShapeDtypeStruct(arr.shape, arr.dtype)],
                 [pltpu.SemaphoreType.DMA((1,)), pltpu.SemaphoreType.DMA((1,))],
                 lambda ins, outs, sems: copy(ins, outs, sems).start(),
                 lambda ins, outs, sems: copy(ins, outs, sems).wait())


def _add_bf16(a, b, name):
    r, c = a.shape
    tr = 128

    def body(a_ref, b_ref, o_ref):
        o_ref[...] = (a_ref[...].astype(F32) + b_ref[...].astype(F32)).astype(BF)

    blk = pl.BlockSpec((tr, c), lambda i: (i, 0))
    return pl.pallas_call(
        body, name=name, grid=(r // tr,), in_specs=[blk, blk], out_specs=blk,
        out_shape=jax.ShapeDtypeStruct((r, c), BF), compiler_params=_params(("parallel",)),
    )(a, b)


def _rs_finish(p3s, landed, shapes, axes, small):
    x, y, c = lax.axis_index("x"), lax.axis_index("y"), lax.axis_index("c")
    idx = jnp.stack([2 * x + y, c]).astype(jnp.int32)
    fulls = [_chip_sum(p3s[0], landed[0], shapes[0], axes[0], idx, "chip_sum_w_in")]
    fulls += _chip_sums(p3s[1:], landed[1:], shapes[1:], axes[1:], idx, "chip_sums_branches_out")
    return _rs_pair_gather(fulls, small)


NSMALL = 8


def _small_all_reduce(small_refs, out_ref, pack_ref, buf_ref, send, recv):
    nw_ref, lb_ref, hn_ref, wf_ref, ls_ref = small_refs
    x, y, c = lax.axis_index("x"), lax.axis_index("y"), lax.axis_index("c")
    me = 4 * x + 2 * y + c
    pack_ref[...] = jnp.zeros_like(pack_ref)
    pack_ref[0:1, :] = nw_ref[...]
    pack_ref[1:2, :] = lb_ref[...]
    pack_ref[2:3, 0:HK] = hn_ref[...]
    pack_ref[3:4, :] = wf_ref[...]
    pack_ref[4:5, :] = ls_ref[...]
    buf_ref[me] = pack_ref[...]
    cps = []
    for d in range(1, 8):
        dx, dy, dc = d >> 2, (d >> 1) & 1, d & 1
        peer = (1 - x if dx else x, 1 - y if dy else y, 1 - c if dc else c)
        cp = pltpu.make_async_remote_copy(
            src_ref=pack_ref, dst_ref=buf_ref.at[me], send_sem=send.at[d - 1], recv_sem=recv.at[d - 1],
            device_id=peer, device_id_type=MESH)
        cp.start()
        cps.append(cp)
    for d in range(1, 8):
        dx, dy, dc = d >> 2, (d >> 1) & 1, d & 1
        src = 4 * (1 - x if dx else x) + 2 * (1 - y if dy else y) + (1 - c if dc else c)
        pltpu.make_async_remote_copy(
            src_ref=pack_ref, dst_ref=buf_ref.at[src], send_sem=send.at[d - 1], recv_sem=recv.at[d - 1],
            device_id=(x, y, c), device_id_type=MESH).wait_recv()
    for cp in cps:
        cp.wait_send()
    acc = buf_ref[0]
    for i in range(1, 8):
        acc = acc + buf_ref[i]
    out_ref[...] = acc


def _adamw_math(w, g, m, v):
    m = B1 * m + (1.0 - B1) * g
    v = B2 * v + (1.0 - B2) * (g * g)
    m_hat = m / (1.0 - B1 ** STEP)
    v_hat = v / (1.0 - B2 ** STEP)
    return -LR * (m_hat / (jnp.sqrt(v_hat) + ADAM_EPS) + WD * w), m, v


def _adamw(w, g, m, v, name):
    r, c = w.shape
    tr = 128

    def body(w_ref, g_ref, m_ref, v_ref, d_ref, nm_ref, nv_ref, go_ref):
        g = g_ref[...]
        d_ref[...], nm_ref[...], nv_ref[...] = _adamw_math(w_ref[...], g, m_ref[...], v_ref[...])
        go_ref[...] = g

    blk = pl.BlockSpec((tr, c), lambda i: (i, 0))
    return pl.pallas_call(
        body, name=name, grid=(r // tr,), in_specs=[blk] * 4, out_specs=[blk] * 4,
        out_shape=[jax.ShapeDtypeStruct((r, c), F32)] * 4,
        compiler_params=_params(("parallel",)),
    )(w, g, m, v)


def _adamw_whole(groups, name):
    n = len(groups)

    def body(*refs):
        ins, outs = refs[:4 * n], refs[4 * n:]
        for a in range(n):
            w_ref, g_ref, m_ref, v_ref = ins[4 * a:4 * a + 4]
            g = g_ref[...]
            outs[4 * a][...], outs[4 * a + 1][...], outs[4 * a + 2][...] = _adamw_math(
                w_ref[...], g, m_ref[...], v_ref[...])
            outs[4 * a + 3][...] = g

    vm = pl.BlockSpec(memory_space=pltpu.VMEM)
    out = pl.pallas_call(
        body, name=name, in_specs=[vm] * (4 * n), out_specs=[vm] * (4 * n),
        out_shape=[jax.ShapeDtypeStruct(grp[0].shape, F32) for grp in groups for _ in range(4)],
        compiler_params=_params(),
    )(*[a for grp in groups for a in grp])
    return [out[4 * a:4 * a + 4] for a in range(n)]


def _small_update(red, lbl, params):
    def body(red_ref, *refs):
        ins, outs = refs[:12], refs[12:]
        lb = _lower_bound(ins[3][...])
        dl0 = red_ref[1:2, :] * lb * (1.0 - lb)
        row = lax.broadcasted_iota(jnp.int32, (2, D), 0)
        grads = [red_ref[0:1, :], jnp.where(row == 0, dl0, -dl0), red_ref[2:3, 0:HK], red_ref[3:4, :]]
        for i, g in enumerate(grads):
            w, m, v = ins[3 * i][...], ins[3 * i + 1][...], ins[3 * i + 2][...]
            d, nm, nv = _adamw_math(w, g, m, v)
            outs[4 * i][...] = g
            outs[4 * i + 1][...] = d
            outs[4 * i + 2][...] = nm
            outs[4 * i + 3][...] = nv
        outs[16][...] = jnp.sum(red_ref[4:5, :], axis=1, keepdims=True)

    flat = [a for p in params for a in p]
    vm = pl.BlockSpec(memory_space=pltpu.VMEM)
    shapes = [jax.ShapeDtypeStruct(p[0].shape, F32) for p in params for _ in range(4)]
    return pl.pallas_call(
        body, name="small_update",
        in_specs=[vm] * 13, out_specs=[vm] * 17,
        out_shape=shapes + [jax.ShapeDtypeStruct((1, 1), F32)],
    )(red, *flat)


def kernel(x, positions, norm_w, w_in, lb_logits, hgrn_norm_w, w_branch_a, w_branch_b, w_out, final_norm_w, loss_target, m_norm_w, m_w_in, m_lb_logits, m_hgrn_norm_w, m_w_branch_a, m_w_branch_b, m_w_out, m_final_norm_w, v_norm_w, v_w_in, v_lb_logits, v_hgrn_norm_w, v_w_branch_a, v_w_branch_b, v_w_out, v_final_norm_w):
    big_w = [w_in[0], w_branch_a[0], w_branch_b[0], w_out[0]]
    big_m = [m_w_in[0], m_w_branch_a[0], m_w_branch_b[0], m_w_out[0]]
    big_v = [v_w_in[0], v_w_branch_a[0], v_w_branch_b[0], v_w_out[0]]
    shapes = [w.shape for w in big_w]
    wf = final_norm_w.reshape(1, D)

    shards = [big_w[0].astype(BF)] + big_w[1:]
    loc = _local_step(x[0], positions.reshape(T, 1), norm_w, lb_logits, hgrn_norm_w, wf, loss_target[0],
                      *shards, shard_shapes=shapes)
    g_big = [loc["g_win"], loc["g_wa"], loc["g_wb"], loc["g_wout"]]
    red = loc["small_sums"]

    small = _small_update(red, lb_logits, [
        (norm_w, m_norm_w, v_norm_w), (lb_logits, m_lb_logits, v_lb_logits),
        (hgrn_norm_w, m_hgrn_norm_w, v_hgrn_norm_w),
        (wf, m_final_norm_w.reshape(1, D), v_final_norm_w.reshape(1, D))])
    loss = small[16].reshape(())
    sg, sd, sm, sv = ([small[4 * i + j] for i in range(4)] for j in range(4))
    for lst in (sg, sd, sm, sv):
        lst[3] = lst[3].reshape(D)
    per_w = list(zip(big_w, g_big, big_m, big_v))
    upd = [_adamw(*per_w[0], "adamw_w_in")] + _adamw_whole(per_w[1:], "adamw_branches_out")
    bd, bm, bv, bg = ([u[j][None] for u in upd] for j in range(4))

    def order(s, b):
        return [s[0], b[0], s[1], s[2], b[1], b[2], b[3], s[3]]

    return (loss, loc["gx"][None], *order(sg, bg), *order(sd, bd), *order(sm, bm), *order(sv, bv))
```

```python
import functools

import jax
import jax.numpy as jnp
from jax import lax
from jax.experimental import pallas as pl
from jax.experimental.pallas import tpu as pltpu

T = 2048
D = 1024
NIN = 11264
HEADS = 8
HK = 128
CH = 16
NCH = T // CH
HSTEP = 2
ATT_GROUPS = ((128, 1), (512, 4), (2048, 16))
ATT_COL0 = 4096
AG_COL0 = 8704
GATE_COL0 = 9216
EPS = 1e-6
ROPE_THETA = 10000.0
LR, B1, B2, ADAM_EPS, WD, STEP = 0.001, 0.9, 0.999, 1e-08, 0.01, 10

F32 = jnp.float32
BF = jnp.bfloat16
VMEM_LIMIT = 56 * 1024 * 1024

_NN = (((1,), (0,)), ((), ()))
_NT = (((1,), (1,)), ((), ()))
_TN = (((0,), (0,)), ((), ()))


def _dot(a, b, dims=_NN):
    return lax.dot_general(a, b, dims, preferred_element_type=F32)


def _bdot(a, b, dims=_NN):
    return lax.dot_general(a.astype(BF), b.astype(BF), dims, preferred_element_type=F32)


def _sigmoid(x):
    return jax.nn.sigmoid(x)


def _params(sem=None):
    return pltpu.CompilerParams(dimension_semantics=sem, vmem_limit_bytes=VMEM_LIMIT)


def _matmul(a, b, *, ta=False, tb=False, out_dtype=F32, tm=512, tn=512, tk=None, name, side=None):
    m = a.shape[1] if ta else a.shape[0]
    kdim = a.shape[0] if ta else a.shape[1]
    n = b.shape[0] if tb else b.shape[1]
    tk = tk or kdim
    tm, tn = min(tm, m), min(tn, n)
    nm, nn, nk = m // tm, n // tn, kdim // tk
    dims = (((0 if ta else 1,), (1 if tb else 0,)), ((), ()))
    s_arrays, s_in_specs, s_shapes, s_out_specs, s_sems = _side_io(side)
    na, no = len(s_arrays), len(s_shapes)
    nacc = 1 if nk > 1 else 0

    def body(*refs):
        a_ref, b_ref = refs[:2]
        s_ins, o_ref, s_outs = refs[2:2 + na], refs[2 + na], refs[3 + na:3 + na + no]
        scratch = refs[3 + na + no:]
        s_sem_refs = scratch[nacc:]
        i, j, k = pl.program_id(0), pl.program_id(1), pl.program_id(2)
        if side is not None:
            @pl.when((i == 0) & (j == 0) & (k == 0))
            def _():
                side.first(s_ins, s_outs, s_sem_refs)

        prod = _bdot(a_ref[...], b_ref[...], dims)
        if nk == 1:
            o_ref[...] = prod.astype(out_dtype)
        else:
            acc = scratch[0]

            @pl.when(k == 0)
            def _():
                acc[...] = prod

            @pl.when(k > 0)
            def _():
                acc[...] += prod

            @pl.when(k == nk - 1)
            def _():
                o_ref[...] = acc[...].astype(out_dtype)

        if side is not None:
            @pl.when((i == nm - 1) & (j == nn - 1) & (k == nk - 1))
            def _():
                side.last(s_ins, s_outs, s_sem_refs)

    a_spec = pl.BlockSpec((tk, tm), lambda i, j, k: (k, i)) if ta else pl.BlockSpec((tm, tk), lambda i, j, k: (i, k))
    b_spec = pl.BlockSpec((tn, tk), lambda i, j, k: (j, k)) if tb else pl.BlockSpec((tk, tn), lambda i, j, k: (k, j))
    sem = ("parallel", "parallel", "arbitrary") if side is None else ("arbitrary",) * 3
    out = pl.pallas_call(
        body, name=name, grid=(nm, nn, nk),
        in_specs=[a_spec, b_spec] + s_in_specs,
        out_specs=[pl.BlockSpec((tm, tn), lambda i, j, k: (i, j))] + s_out_specs,
        out_shape=[jax.ShapeDtypeStruct((m, n), out_dtype)] + s_shapes,
        scratch_shapes=([pltpu.VMEM((tm, tn), F32)] if nk > 1 else []) + s_sems,
        compiler_params=_params(sem),
    )(a, b, *s_arrays)
    return out[0] if side is None else (out[0], out[1:])


DZ_TILE = 512


def _part_offsets(parts):
    counts = [p.shape[1] // DZ_TILE for p in parts]
    offs = [sum(counts[:i]) for i in range(len(parts))]
    return counts, offs


def _part_spec(rows, cnt, off, tile_axis):
    def index(*g):
        return (0 if rows is None else g[0], jnp.clip(g[tile_axis] - off, 0, cnt - 1))
    return index


def _grad_w_in(h, parts):
    counts, offs = _part_offsets(parts)
    n = len(parts)

    def body(h_ref, *refs):
        o_ref = refs[n]
        j = pl.program_id(0)
        for p_ref, cnt, off in zip(refs[:n], counts, offs):
            @pl.when((j >= off) & (j < off + cnt))
            def _(p_ref=p_ref):
                o_ref[...] = _bdot(h_ref[...], p_ref[...], _TN).astype(BF)

    return pl.pallas_call(
        body, name="g_win", grid=(sum(counts),),
        in_specs=[pl.BlockSpec((T, D), lambda j: (0, 0))] +
                 [pl.BlockSpec((T, DZ_TILE), _part_spec(None, c, o, 0)) for c, o in zip(counts, offs)],
        out_specs=pl.BlockSpec((D, DZ_TILE), lambda j: (0, j)),
        out_shape=jax.ShapeDtypeStruct((D, NIN), BF),
        compiler_params=_params(("parallel",)),
    )(h, *parts)


def _grad_w_in_half(h, parts, half_idx, side=None):
    counts, offs = _part_offsets(parts)
    n = len(parts)
    nj = sum(counts)
    s_arrays, s_in_specs, s_shapes, s_out_specs, s_sems = _side_io(side)
    na, no = len(s_arrays), len(s_shapes)

    def body(idx_ref, h_ref, *refs):
        s_ins, o_ref, s_outs, s_sem_refs = refs[n:n + na], refs[n + na], refs[n + na + 1:n + na + 1 + no], refs[n + na + 1 + no:]
        j = pl.program_id(0)
        if side is not None:
            @pl.when(j == 0)
            def _():
                side.first(s_ins, s_outs, s_sem_refs)

        for p_ref, cnt, off in zip(refs[:n], counts, offs):
            @pl.when((j >= off) & (j < off + cnt))
            def _(p_ref=p_ref):
                o_ref[...] = _bdot(h_ref[...], p_ref[...], _TN).astype(BF)

        if side is not None:
            @pl.when(j == nj - 1)
            def _():
                side.last(s_ins, s_outs, s_sem_refs)

    def part_spec(cnt, off):
        return pl.BlockSpec((T, DZ_TILE), lambda j, idx: (0, jnp.clip(j - off, 0, cnt - 1)))

    out = pl.pallas_call(
        body, name="g_win_half" if side is None else "g_win_half_carrying",
        grid_spec=pltpu.PrefetchScalarGridSpec(
            num_scalar_prefetch=1, grid=(nj,),
            in_specs=[pl.BlockSpec((T, D // 2), lambda j, idx: (0, idx[0]))] +
                     [part_spec(c, o) for c, o in zip(counts, offs)] + s_in_specs,
            out_specs=[pl.BlockSpec((D // 2, DZ_TILE), lambda j, idx: (0, j))] + s_out_specs,
            scratch_shapes=s_sems),
        out_shape=[jax.ShapeDtypeStruct((D // 2, NIN), BF)] + s_shapes,
        compiler_params=_params(("parallel",) if side is None else ("arbitrary",)),
    )(half_idx, h, *parts, *s_arrays)
    return out[0] if side is None else (out[0], out[1:])


def _side_io(side):
    if side is None:
        return [], [], [], [], []
    return (side.arrays, [HBM] * len(side.arrays), side.out_shapes, [HBM] * len(side.out_shapes), side.sems)


def _grad_x(parts, w_in, x, dout, norm_w, side=None):
    counts, offs = _part_offsets(parts)
    n = len(parts)
    tm = 1024
    nm, nk = T // tm, sum(counts)
    s_arrays, s_in_specs, s_shapes, s_out_specs, s_sems = _side_io(side)
    na, no = len(s_arrays), len(s_shapes)

    def body(*refs):
        w_ref, x_ref, dout_ref, nw_ref = refs[n:n + 4]
        s_ins = refs[n + 4:n + 4 + na]
        gx_ref, gw_ref = refs[n + 4 + na:n + 6 + na]
        s_outs = refs[n + 6 + na:n + 6 + na + no]
        acc = refs[n + 6 + na + no]
        s_sem_refs = refs[n + 7 + na + no:]
        i, k = pl.program_id(0), pl.program_id(1)

        @pl.when((i == 0) & (k == 0))
        def _():
            gw_ref[...] = jnp.zeros_like(gw_ref)
            if side is not None:
                side.first(s_ins, s_outs, s_sem_refs)

        @pl.when(k == 0)
        def _():
            acc[...] = jnp.zeros_like(acc)

        if side is not None and side.mid is not None:
            @pl.when((i == nm - 1) & (k == 0))
            def _():
                side.mid(s_ins, s_outs, s_sem_refs)

        for p_ref, cnt, off in zip(refs[:n], counts, offs):
            @pl.when((k >= off) & (k < off + cnt))
            def _(p_ref=p_ref):
                acc[...] += _bdot(p_ref[...], w_ref[...], _NT)

        @pl.when(k == nk - 1)
        def _():
            gw = jnp.zeros((1, D), F32)
            for c in range(tm // BLK):
                rows = pl.ds(BLK * c, BLK)
                xv, dhv = x_ref[rows, :], acc[rows, :]
                r = lax.rsqrt(jnp.mean(xv * xv, axis=-1, keepdims=True) + EPS)
                nrm = xv * r
                dn = dhv * nw_ref[...]
                gw = gw + jnp.sum(dhv * nrm, axis=0, keepdims=True)
                gx_ref[rows, :] = dout_ref[rows, :] + r * (dn - nrm * jnp.mean(dn * nrm, axis=-1, keepdims=True))
            gw_ref[...] += gw

        if side is not None:
            @pl.when((i == nm - 1) & (k == nk - 1))
            def _():
                side.last(s_ins, s_outs, s_sem_refs)

    row = pl.BlockSpec((tm, D), lambda i, k: (i, 0))
    vec = pl.BlockSpec((1, D), lambda i, k: (0, 0))
    out = pl.pallas_call(
        body, name="grad_x", grid=(nm, nk),
        in_specs=[pl.BlockSpec((tm, DZ_TILE), _part_spec(0, c, o, 1)) for c, o in zip(counts, offs)] +
                 [pl.BlockSpec((D, DZ_TILE), lambda i, k: (0, k)), row, row, vec] + s_in_specs,
        out_specs=[row, vec] + s_out_specs,
        out_shape=[jax.ShapeDtypeStruct((T, D), F32), jax.ShapeDtypeStruct((1, D), F32)] + s_shapes,
        scratch_shapes=[pltpu.VMEM((tm, D), F32)] + s_sems,
        compiler_params=_params(("arbitrary", "arbitrary")),
    )(*parts, w_in, x, dout, norm_w, *s_arrays)
    return out[0], out[1], out[2:]


def _norm_and_rope_tables(x, w, pos, invf, side=None, own=None, cast=()):
    tm = 256
    nm = T // tm
    s_arrays, s_in_specs, s_shapes, s_out_specs, s_sems = _side_io(side)
    na, no, nc = len(s_arrays), len(s_shapes), len(cast)
    nz = 0 if own is None else 1
    wsh, blk = own if own is not None else (None, jnp.zeros((1,), jnp.int32))
    first_out = 4 + nz + nc + na

    def body(blk_ref, *refs):
        x_ref, w_ref, pos_ref, invf_ref = refs[:4]
        s_ins = refs[4 + nz + nc:first_out]
        h_ref, cos_ref, sa_ref, sb_ref = refs[first_out:first_out + 4]
        s_outs = refs[first_out + 4 + nz + nc:first_out + 4 + nz + nc + no]
        s_sem_refs = refs[first_out + 4 + nz + nc + no:]

        @pl.when(pl.program_id(0) == 0)
        def _():
            if side is not None:
                side.first(s_ins, s_outs, s_sem_refs)
            for src, dst in zip(refs[4 + nz:4 + nz + nc], refs[first_out + 4 + nz:first_out + 4 + nz + nc]):
                dst[...] = src[...].astype(BF)

        xv = x_ref[...]
        r = lax.rsqrt(jnp.mean(xv * xv, axis=-1, keepdims=True) + EPS)
        h = (xv * r * w_ref[...]).astype(BF)
        h_ref[...] = h
        if own is not None:
            refs[first_out + 4][...] = _dot(h, refs[4][...])
        first = (lax.broadcasted_iota(jnp.int32, (tm, 128), 1) % 64) < 32
        ang = pos_ref[...].astype(F32) * invf_ref[...]
        s = jnp.sin(ang)
        cos_ref[...] = jnp.cos(ang)
        sa_ref[...] = jnp.where(first, -s, 0.0)
        sb_ref[...] = jnp.where(first, 0.0, s)
        if side is not None:
            @pl.when(pl.program_id(0) == nm - 1)
            def _():
                side.last(s_ins, s_outs, s_sem_refs)

    tab = pl.BlockSpec((tm, 128), lambda i, b: (i, 0))
    own_in = [] if own is None else [pl.BlockSpec(wsh.shape, lambda i, b: (0, 0))]
    own_out = [] if own is None else [pl.BlockSpec((tm, wsh.shape[1]), lambda i, b: (i, b[0]))]
    own_shape = [] if own is None else [jax.ShapeDtypeStruct((T, NIN), F32)]
    whole = [pl.BlockSpec(a.shape, lambda i, b: (0, 0)) for a in cast]
    out = pl.pallas_call(
        body, name="norm_and_rope_tables",
        grid_spec=pltpu.PrefetchScalarGridSpec(
            num_scalar_prefetch=1, grid=(nm,),
            in_specs=[pl.BlockSpec((tm, D), lambda i, b: (i, 0)), pl.BlockSpec((1, D), lambda i, b: (0, 0)),
                      pl.BlockSpec((tm, 1), lambda i, b: (i, 0)), pl.BlockSpec((1, 128), lambda i, b: (0, 0))]
                     + own_in + whole + s_in_specs,
            out_specs=[pl.BlockSpec((tm, D), lambda i, b: (i, 0)), tab, tab, tab] + own_out + whole + s_out_specs,
            scratch_shapes=s_sems),
        out_shape=[jax.ShapeDtypeStruct((T, D), BF)] + [jax.ShapeDtypeStruct((T, 128), F32)] * 3 + own_shape
                  + [jax.ShapeDtypeStruct(a.shape, BF) for a in cast] + s_shapes,
        compiler_params=_params(("parallel",) if side is None and not cast else ("arbitrary",)),
    )(blk, x, w, pos, invf, *([] if own is None else [wsh]), *cast, *s_arrays)
    return (out[0], out[1], out[2], out[3], (out[4] if own is not None else None), out[4 + nz:4 + nz + nc],
            out[4 + nz + nc:])


def _z_blocks(h, w, z, idx, nb, side, name, fill=None):
    tm, tn = 1024, NIN // 8
    s_arrays, s_in_specs, s_shapes, s_out_specs, s_sems = _side_io(side)
    na, no = len(s_arrays), len(s_shapes)
    nm, ns = T // tm, 2 * nb
    nf = 0 if fill is None else 1

    def col(first, i, s, b):
        return (0, b[first + s // 2] * 2 + s % 2)

    def body(idx_ref, h_ref, w_ref, zin_ref, *refs):
        s_ins = refs[nf:nf + na]
        o_ref = refs[nf + na]
        s_outs = refs[nf + na + 1 + nf:nf + na + 1 + nf + no]
        s_sem_refs = refs[nf + na + 1 + nf + no + nf:]
        i, s = pl.program_id(0), pl.program_id(1)

        if side is not None:
            @pl.when((i == 0) & (s == 0))
            def _():
                side.first(s_ins, s_outs, s_sem_refs)

        if fill is not None:
            tile = pl.ds(pl.multiple_of((idx_ref[0] * 2 + s) * tn, 128), tn)
            store = pltpu.make_async_copy(w_ref, refs[nf + na + 1].at[:, tile], refs[nf + na + 1 + nf + no].at[0])
            pl.when(i == 0)(store.start)
        o_ref[...] = _dot(h_ref[...], w_ref[...])
        if fill is not None:
            pl.when(i == 0)(store.wait)

        if side is not None:
            @pl.when((i == nm - 1) & (s == ns - 1))
            def _():
                side.last(s_ins, s_outs, s_sem_refs)

    fills = [] if fill is None else [fill]
    out = pl.pallas_call(
        body, name=name,
        grid_spec=pltpu.PrefetchScalarGridSpec(
            num_scalar_prefetch=1, grid=(nm, ns),
            in_specs=[pl.BlockSpec((tm, D), lambda i, s, b: (i, 0)), pl.BlockSpec((D, tn), functools.partial(col, nb)),
                      HBM] + [HBM] * nf + s_in_specs,
            out_specs=[pl.BlockSpec((tm, tn), lambda i, s, b: (i, col(0, i, s, b)[1]))] + [HBM] * nf + s_out_specs,
            scratch_shapes=[pltpu.SemaphoreType.DMA((1,))] * nf + s_sems),
        out_shape=[jax.ShapeDtypeStruct((T, NIN), F32)] + [jax.ShapeDtypeStruct(f.shape, f.dtype) for f in fills]
                  + s_shapes,
        input_output_aliases={3: 0, **({4: 1} if fill is not None else {})},
        compiler_params=_params(("arbitrary", "arbitrary")),
    )(idx, h, w, z, *fills, *s_arrays)
    return (out[0], *out[1:1 + nf], out[1 + nf:])


def _lower_bound(lbl):
    mx = jnp.max(lbl, axis=0, keepdims=True)
    e = jnp.exp(lbl - mx)
    return e[0:1] / jnp.sum(e, axis=0, keepdims=True)


def _cumsum_rows(g, rows):
    b = g
    sh = 1
    while sh < CH:
        b = b + jnp.where(rows >= sh, pltpu.roll(b, sh, axis=0), 0.0)
        sh *= 2
    return b


def _rev_cumsum_rows(g, rows):
    b = g
    sh = 1
    while sh < CH:
        b = b + jnp.where(rows < CH - sh, pltpu.roll(b, CH - sh, axis=0), 0.0)
        sh *= 2
    return b


SUB = CH // 2


def _direct_block(qb, kb, vb, bb, rows8):
    ob = jnp.zeros_like(qb)
    for s in range(SUB):
        e_s = jnp.exp(jnp.where(rows8 >= s, bb - bb[s:s + 1], -jnp.inf))
        ob = ob + jnp.sum(qb * e_s * kb[s:s + 1], axis=1, keepdims=True) * vb[s:s + 1]
    return ob


def _direct_block_bwd(qb, kb, vb, bb, dob, rows8, rowc8):
    dq = dk = dv = db = jnp.zeros_like(qb)
    for s in range(SUB):
        one = (rowc8 == s).astype(F32)
        ks, vs = kb[s:s + 1], vb[s:s + 1]
        e_s = jnp.exp(jnp.where(rows8 >= s, bb - bb[s:s + 1], -jnp.inf))
        qes = qb * e_s
        w = qes * ks
        a = jnp.sum(w, axis=1, keepdims=True)
        da = jnp.sum(dob * vs, axis=1, keepdims=True)
        dv = dv + one * jnp.sum(a * dob, axis=0, keepdims=True)
        dq = dq + da * e_s * ks
        dk = dk + one * jnp.sum(da * qes, axis=0, keepdims=True)
        u = da * w
        db = db + u - one * jnp.sum(u, axis=0, keepdims=True)
    return dq, dk, dv, db


def _cross_factors(q, k, b):
    ref = b[SUB - 1:SUB]
    e_hi, e_lo = jnp.exp(b[SUB:] - ref), jnp.exp(ref - b[:SUB])
    return q[SUB:] * e_hi, k[:SUB] * e_lo, e_hi, e_lo


def _intra_fwd(q, k, v, b, rows8):
    lo = _direct_block(q[:SUB], k[:SUB], v[:SUB], b[:SUB], rows8)
    hi = _direct_block(q[SUB:], k[SUB:], v[SUB:], b[SUB:], rows8)
    qe_hi, ke_lo, _, _ = _cross_factors(q, k, b)
    for s in range(SUB):
        hi = hi + jnp.sum(qe_hi * ke_lo[s:s + 1], axis=1, keepdims=True) * v[s:s + 1]
    return jnp.concatenate([lo, hi], axis=0)


def _intra_bwd(q, k, v, b, do, rows8, rowc8):
    dq_lo, dk_lo, dv_lo, db_lo = _direct_block_bwd(q[:SUB], k[:SUB], v[:SUB], b[:SUB], do[:SUB], rows8, rowc8)
    dq_hi, dk_hi, dv_hi, db_hi = _direct_block_bwd(q[SUB:], k[SUB:], v[SUB:], b[SUB:], do[SUB:], rows8, rowc8)
    qe_hi, ke_lo, e_hi, e_lo = _cross_factors(q, k, b)
    do_hi, v_lo = do[SUB:], v[:SUB]
    dqe = dke = jnp.zeros_like(qe_hi)
    for s in range(SUB):
        one = (rowc8 == s).astype(F32)
        a = jnp.sum(qe_hi * ke_lo[s:s + 1], axis=1, keepdims=True)
        da = jnp.sum(do_hi * v_lo[s:s + 1], axis=1, keepdims=True)
        dv_lo = dv_lo + one * jnp.sum(a * do_hi, axis=0, keepdims=True)
        dqe = dqe + da * ke_lo[s:s + 1]
        dke = dke + one * jnp.sum(da * qe_hi, axis=0, keepdims=True)
    u_hi, u_lo = dqe * qe_hi, dke * ke_lo
    d_ref = jnp.sum(u_lo, axis=0, keepdims=True) - jnp.sum(u_hi, axis=0, keepdims=True)
    db_lo = db_lo - u_lo + (rowc8 == SUB - 1).astype(F32) * d_ref
    cat = lambda lo, hi: jnp.concatenate([lo, hi], axis=0)
    return (cat(dq_lo, dq_hi + dqe * e_hi), cat(dk_lo + dke * e_lo, dk_hi), cat(dv_lo, dv_hi),
            cat(db_lo, db_hi + u_hi))


def _hgrn_fwd(z, lbl, nw):
    def body(hq_ref, hf_ref, hi_ref, hg_ref, lbl_ref, nw_ref, oraw_ref, og_ref, sh_ref, st_ref):
        @pl.when(pl.program_id(0) == 0)
        def _():
            st_ref[...] = jnp.zeros_like(st_ref)

        lb_all = _lower_bound(lbl_ref[...])
        rows = lax.broadcasted_iota(jnp.int32, (CH, HK), 0)
        rows8 = lax.broadcasted_iota(jnp.int32, (SUB, HK), 0)
        nwv = nw_ref[...]
        for cc, h in [(cc, h) for cc in range(HSTEP) for h in range(HEADS)]:
            rs = slice(CH * cc, CH * (cc + 1))
            sl = slice(HK * h, HK * (h + 1))
            lb = lb_all[:, sl]
            hq, hf, v, hg = hq_ref[rs, sl], hf_ref[rs, sl], hi_ref[rs, sl], hg_ref[rs, sl]
            q = hq * _sigmoid(hq)
            f = lb + (1.0 - lb) * _sigmoid(hf)
            k = 1.0 - f
            b = _cumsum_rows(jnp.log(f), rows)
            sh_ref[cc, h] = st_ref[h]
            o = _bdot(q * jnp.exp(b), st_ref[h], _NT) + _intra_fwd(q, k, v, b, rows8)
            bl = b[CH - 1:CH]
            st_ref[h] = st_ref[h] * jnp.exp(bl)
            st_ref[h] += _bdot(v, k * jnp.exp(bl - b), _TN)
            oraw_ref[rs, sl] = o
            nrm = o * lax.rsqrt(jnp.mean(o * o, axis=1, keepdims=True) + EPS)
            og_ref[rs, sl] = (nrm * nwv * (hg * _sigmoid(hg))).astype(BF)

    zblk = lambda c: pl.BlockSpec((CH * HSTEP, D), lambda i, c=c: (i, c))
    return pl.pallas_call(
        body, name="hgrn_fwd", grid=(NCH // HSTEP,),
        in_specs=[zblk(0), zblk(1), zblk(2), zblk(3),
                  pl.BlockSpec((2, D), lambda i: (0, 0)), pl.BlockSpec((1, HK), lambda i: (0, 0))],
        out_specs=[zblk(0), zblk(0),
                   pl.BlockSpec((HSTEP, HEADS, HK, HK), lambda i: (i, 0, 0, 0))],
        out_shape=[jax.ShapeDtypeStruct((T, D), F32), jax.ShapeDtypeStruct((T, D), BF),
                   jax.ShapeDtypeStruct((NCH, HEADS, HK, HK), F32)],
        scratch_shapes=[pltpu.VMEM((HEADS, HK, HK), F32)],
        compiler_params=_params(("arbitrary",)),
    )(z, z, z, z, lbl, nw)


def _hgrn_bwd(z, lbl, nw, oraw, dog, shist):
    hstep = 1

    def body(hq_ref, hf_ref, hi_ref, hg_ref, lbl_ref, nw_ref, oraw_ref, dog_ref, sh_ref,
             dz_ref, dlb_ref, dnw_ref, dst_ref):
        @pl.when(pl.program_id(0) == 0)
        def _():
            dst_ref[...] = jnp.zeros_like(dst_ref)
            dlb_ref[...] = jnp.zeros_like(dlb_ref)
            dnw_ref[...] = jnp.zeros_like(dnw_ref)

        lb_all = _lower_bound(lbl_ref[...])
        rows = lax.broadcasted_iota(jnp.int32, (CH, HK), 0)
        rowc = lax.broadcasted_iota(jnp.int32, (CH, 1), 0)
        rows8 = lax.broadcasted_iota(jnp.int32, (SUB, HK), 0)
        rowc8 = lax.broadcasted_iota(jnp.int32, (SUB, 1), 0)
        nwv = nw_ref[...]
        dnw = jnp.zeros((1, HK), F32)
        for cc, h in [(cc, h) for cc in reversed(range(hstep)) for h in range(HEADS)]:
            rs = slice(CH * cc, CH * (cc + 1))
            sl = slice(HK * h, HK * (h + 1))
            lb = lb_all[:, sl]
            hq, hf, v, hg = hq_ref[rs, sl], hf_ref[rs, sl], hi_ref[rs, sl], hg_ref[rs, sl]
            o, dg_out = oraw_ref[rs, sl], dog_ref[rs, sl]
            sg = _sigmoid(hg)
            sil = hg * sg
            r = lax.rsqrt(jnp.mean(o * o, axis=1, keepdims=True) + EPS)
            nrm = o * r
            d_hg = dg_out * (nrm * nwv) * (sg * (1.0 + hg * (1.0 - sg)))
            dn = dg_out * nwv * sil
            dnw = dnw + jnp.sum(dg_out * nrm * sil, axis=0, keepdims=True)
            do = r * (dn - nrm * jnp.mean(dn * nrm, axis=1, keepdims=True))
            sq = _sigmoid(hq)
            q = hq * sq
            sig = _sigmoid(hf)
            f = lb + (1.0 - lb) * sig
            k = 1.0 - f
            b = _cumsum_rows(jnp.log(f), rows)
            eb = jnp.exp(b)
            qe = q * eb
            bl = b[CH - 1:CH]
            ebl = jnp.exp(bl)
            kdec = jnp.exp(bl - b)
            ke = k * kdec
            dqe = _bdot(do, sh_ref[cc, h])
            dq = dqe * eb
            db = dqe * qe
            dke = _bdot(v, dst_ref[h])
            dv = _bdot(ke, dst_ref[h], _NT)
            dk = dke * kdec
            rr = dke * ke
            db = db - rr
            db_last = (jnp.sum(rr, axis=0, keepdims=True)
                       + ebl * jnp.sum(dst_ref[h] * sh_ref[cc, h], axis=0, keepdims=True))
            dst_ref[h] = dst_ref[h] * ebl
            dst_ref[h] += _bdot(do, qe, _TN)
            dq_i, dk_i, dv_i, db_i = _intra_bwd(q, k, v, b, do, rows8, rowc8)
            dq, dk, dv = dq + dq_i, dk + dk_i, dv + dv_i
            db = db + db_i + (rowc == CH - 1).astype(F32) * db_last
            dgl = _rev_cumsum_rows(db, rows)
            df = dgl / f - dk
            dlb_ref[:, sl] += jnp.sum(df * (1.0 - sig), axis=0, keepdims=True)
            dz_ref[rs, sl] = (dq * (sq * (1.0 + hq * (1.0 - sq)))).astype(BF)
            dz_ref[rs, D + HK * h:D + HK * (h + 1)] = (df * (1.0 - lb) * sig * (1.0 - sig)).astype(BF)
            dz_ref[rs, 2 * D + HK * h:2 * D + HK * (h + 1)] = dv.astype(BF)
            dz_ref[rs, 3 * D + HK * h:3 * D + HK * (h + 1)] = d_hg.astype(BF)
        dnw_ref[...] += dnw

    rev = lambda i: NCH // hstep - 1 - i
    zblk = lambda c: pl.BlockSpec((CH * hstep, D), lambda i, c=c: (rev(i), c))
    return pl.pallas_call(
        body, name="hgrn_bwd", grid=(NCH // hstep,),
        in_specs=[zblk(0), zblk(1), zblk(2), zblk(3),
                  pl.BlockSpec((2, D), lambda i: (0, 0)), pl.BlockSpec((1, HK), lambda i: (0, 0)),
                  zblk(0), zblk(0),
                  pl.BlockSpec((hstep, HEADS, HK, HK), lambda i: (rev(i), 0, 0, 0))],
        out_specs=[pl.BlockSpec((CH * hstep, 4 * D), lambda i: (rev(i), 0)),
                   pl.BlockSpec((1, D), lambda i: (0, 0)), pl.BlockSpec((1, HK), lambda i: (0, 0))],
        out_shape=[jax.ShapeDtypeStruct((T, 4 * D), BF), jax.ShapeDtypeStruct((1, D), F32),
                   jax.ShapeDtypeStruct((1, HK), F32)],
        scratch_shapes=[pltpu.VMEM((HEADS, HK, HK), F32)],
        compiler_params=_params(("arbitrary",)),
    )(z, z, z, z, lbl, nw, oraw, dog, shist)


BLK = 128
NBLK = T // BLK
QK_SCALE = 0.125


def _head_masks():
    lane = lax.broadcasted_iota(jnp.int32, (1, BLK), 1)
    return [(lane < 64).astype(F32), (lane >= 64).astype(F32)]


def _pieces(dil):
    m = T // dil
    out = []
    for r in range(dil):
        for j in range(m // BLK):
            start = r + dil * BLK * j
            rows = pl.ds(start, BLK, stride=dil) if dil > 1 else pl.ds(start, BLK)
            out.append((rows, r * m + BLK * j))
    return out


def _rope(x, c, sa, sb):
    return x * c + pltpu.roll(x, 96, axis=1) * sa + pltpu.roll(x, 32, axis=1) * sb


def _rope_t(d, c, sa, sb):
    return d * c + pltpu.roll(d * sa, 32, axis=1) + pltpu.roll(d * sb, 96, axis=1)


def _rope_and_regroup(dil, q_ref, k_ref, v_ref, tables, stage_q, stage_k, qr_ref, kr_ref, vr_ref):
    cos_ref, sa_ref, sb_ref = tables
    to_q, to_k = (qr_ref, kr_ref) if dil == 1 else (stage_q, stage_k)
    for c in range(T // BLK):
        rows = pl.ds(BLK * c, BLK)
        cs, sa, sb = cos_ref[rows, :], sa_ref[rows, :], sb_ref[rows, :]
        to_q[rows, :] = (_rope(q_ref[rows, :], cs, sa, sb) * QK_SCALE).astype(to_q.dtype)
        to_k[rows, :] = _rope(k_ref[rows, :], cs, sa, sb).astype(to_k.dtype)
    for rows, dst in _pieces(dil):
        drows = pl.ds(dst, BLK)
        if dil > 1:
            qr_ref[drows, :] = stage_q[rows, :].astype(qr_ref.dtype)
            kr_ref[drows, :] = stage_k[rows, :].astype(kr_ref.dtype)
        vr_ref[drows, :] = v_ref[rows, :].astype(vr_ref.dtype)


def _window_bias(bias_ref):
    ii = lax.broadcasted_iota(jnp.int32, (2 * BLK, BLK), 0) % BLK
    jj = lax.broadcasted_iota(jnp.int32, (2 * BLK, BLK), 1)
    bias_ref[0] = jnp.where(jj <= ii, 0.0, -jnp.inf)
    bias_ref[1] = jnp.where(jj >= ii, 0.0, -jnp.inf)


def _blocks(bi):
    if isinstance(bi, int):
        return pl.ds(bi * BLK, BLK), pl.ds(max(bi - 1, 0) * BLK, BLK)
    return (pl.ds(pl.multiple_of(bi * BLK, BLK), BLK),
            pl.ds(pl.multiple_of(jnp.maximum(bi - 1, 0) * BLK, BLK), BLK))


def _stack_heads(x, masks):
    return jnp.concatenate([x * masks[0].astype(x.dtype), x * masks[1].astype(x.dtype)], axis=0).astype(BF)


def _side_steps(side, refs, **when):
    if side is None:
        return
    for stage, cond in when.items():
        if getattr(side, stage) is not None:
            pl.when(cond)(functools.partial(getattr(side, stage), *refs))


def _attn_fwd(z, cos, sa, sb, side=None):
    s_arrays, s_in_specs, s_shapes, s_out_specs, s_sems = _side_io(side)
    na, no = len(s_arrays), len(s_shapes)

    def body(q_ref, k_ref, v_ref, ag_ref, cos_ref, sa_ref, sb_ref, *refs):
        ob_ref, opre_ref, lse_ref, qr_ref, kr_ref, vr_ref = refs[na:na + 6]
        bias_ref, og_ref, lg_ref, otok_ref, ltok_ref, sc_ref = refs[na + 6 + no:na + 12 + no]
        s_refs = (refs[:na], refs[na + 6:na + 6 + no], refs[na + 12 + no:])
        p, g = pl.program_id(0), pl.program_id(1)
        _side_steps(side, s_refs, first=(p == 0) & (g == 0), mid=(p == 1) & (g == 0))
        masks = _head_masks()

        @pl.when(g == 0)
        def _():
            _window_bias(bias_ref)

        def group(gi):
            dil = ATT_GROUPS[gi][1]
            nblk = (T // dil) // BLK
            _rope_and_regroup(dil, q_ref, k_ref, v_ref, (cos_ref, sa_ref, sb_ref), lg_ref.at[0], lg_ref.at[1],
                              qr_ref, kr_ref, vr_ref)

            def scores(bi, slot):
                cur, prev = _blocks(bi)
                q2 = _stack_heads(qr_ref[cur, :], masks)
                sc_ref[slot, 0] = _dot(q2, kr_ref[cur, :], _NT) + bias_ref[0]
                if nblk > 1:
                    sc_ref[slot, 1] = (_dot(q2, kr_ref[prev, :], _NT)
                                       + (bias_ref[1] + jnp.where((bi % nblk) != 0, 0.0, -jnp.inf)))

            def finish(bi, slot):
                cur, prev = _blocks(bi)
                s_c, vc = sc_ref[slot, 0], vr_ref[cur, :]
                if nblk > 1:
                    s_p, vp = sc_ref[slot, 1], vr_ref[prev, :]
                    mx = jnp.max(jnp.maximum(s_c, s_p), axis=1, keepdims=True)
                    p_c, p_p = jnp.exp(s_c - mx), jnp.exp(s_p - mx)
                    den = jnp.sum(p_c + p_p, axis=1, keepdims=True)
                    oh = _dot(p_c.astype(BF), vc) + _dot(p_p.astype(BF), vp)
                else:
                    mx = jnp.max(s_c, axis=1, keepdims=True)
                    p_c = jnp.exp(s_c - mx)
                    den = jnp.sum(p_c, axis=1, keepdims=True)
                    oh = _dot(p_c.astype(BF), vc)
                on = oh / den
                lsev = jnp.broadcast_to(mx + jnp.log(den), (2 * BLK, BLK))
                og_ref[cur, :] = on[:BLK] * masks[0] + on[BLK:] * masks[1]
                lg_ref[0, cur, :] = lsev[:BLK]
                lg_ref[1, cur, :] = lsev[BLK:]

            def pair(j, carry):
                finish(2 * j, 0)
                scores(2 * j + 1, 1)
                finish(2 * j + 1, 1)
                scores(jnp.minimum(2 * j + 2, NBLK - 1), 0)
                return carry

            scores(0, 0)
            lax.fori_loop(0, NBLK // 2, pair, 0)
            for rows, src in _pieces(dil):
                srows = pl.ds(src, BLK)
                otok_ref[gi, rows, :] = og_ref[srows, :]
                ltok_ref[gi, 0, rows, :] = lg_ref[0, srows, :]
                ltok_ref[gi, 1, rows, :] = lg_ref[1, srows, :]

        for gi in range(3):
            pl.when(g == gi)(functools.partial(group, gi))

        @pl.when(g == 2)
        def _():
            for c in range(T // BLK):
                rows = pl.ds(BLK * c, BLK)
                wts = []
                for hh in range(2):
                    l0, l1, l2 = ltok_ref[0, hh, rows, :], ltok_ref[1, hh, rows, :], ltok_ref[2, hh, rows, :]
                    mx = jnp.maximum(jnp.maximum(l0, l1), l2)
                    e0, e1, e2 = jnp.exp(l0 - mx), jnp.exp(l1 - mx), jnp.exp(l2 - mx)
                    tot = e0 + e1 + e2
                    lse_ref[rows, BLK * hh:BLK * (hh + 1)] = mx + jnp.log(tot)
                    inv = 1.0 / tot
                    wts.append([e0 * inv, e1 * inv, e2 * inv])
                o = sum((wts[0][gi] * masks[0] + wts[1][gi] * masks[1]) * otok_ref[gi, rows, :] for gi in range(3))
                ag = ag_ref[rows, :]
                opre_ref[rows, :] = o
                ob_ref[rows, :] = (o * (ag * _sigmoid(ag))).astype(BF)

        _side_steps(side, s_refs, last=(p == 3) & (g == 2))

    c0 = ATT_COL0 // BLK
    zspec = lambda part: pl.BlockSpec((T, BLK), lambda p, g, part=part: (0, c0 + 12 * part + 4 * g + p))
    outspec = pl.BlockSpec((T, BLK), lambda p, g: (0, p))
    table = pl.BlockSpec((T, BLK), lambda p, g: (0, 0))
    regrouped = pl.BlockSpec((None, T, BLK), lambda p, g: (g, 0, p))
    big = lambda: pltpu.VMEM((T, BLK), F32)
    out = pl.pallas_call(
        body, name="attn_fwd", grid=(4, 3),
        in_specs=[zspec(0), zspec(1), zspec(2),
                  pl.BlockSpec((T, BLK), lambda p, g: (0, AG_COL0 // BLK + p)), table, table, table] + s_in_specs,
        out_specs=[outspec, outspec, pl.BlockSpec((T, 2 * BLK), lambda p, g: (0, p)), regrouped, regrouped, regrouped]
                  + s_out_specs,
        out_shape=[jax.ShapeDtypeStruct((T, 512), BF), jax.ShapeDtypeStruct((T, 512), F32),
                   jax.ShapeDtypeStruct((T, 8 * BLK), F32)] + [jax.ShapeDtypeStruct((3, T, 512), BF)] * 3 + s_shapes,
        scratch_shapes=[pltpu.VMEM((2, 2 * BLK, BLK), F32), big(),
                        pltpu.VMEM((2, T, BLK), F32), pltpu.VMEM((3, T, BLK), F32), pltpu.VMEM((3, 2, T, BLK), F32),
                        pltpu.VMEM((2, 2, 2 * BLK, BLK), F32)] + s_sems,
        compiler_params=_params(("parallel" if side is None else "arbitrary", "arbitrary")),
    )(z, z, z, z, cos, sa, sb, *s_arrays)
    return (*out[:6], out[6:])


def _attn_bwd(z, qs, ks, vs, cos, sa, sb, opre, lse, dob, side=None):
    s_arrays, s_in_specs, s_shapes, s_out_specs, s_sems = _side_io(side)
    na, no = len(s_arrays), len(s_shapes)

    def body(qs_ref, ks_ref, vs_ref, ag_ref, cos_ref, sa_ref, sb_ref, o_ref, lse0_ref, lse1_ref, dob_ref, *refs):
        dq_ref, dk_ref, dv_ref, dag_ref = refs[na:na + 4]
        (bias_ref, dtok_ref, qr_ref, kr_ref, vr_ref, dor_ref, lr_ref, dr_ref,
         dqr_ref, dkr_ref, dvr_ref, pd_ref, dotok_ref) = refs[na + 4 + no:na + 17 + no]
        s_refs = (refs[:na], refs[na + 4:na + 4 + no], refs[na + 17 + no:])
        p, g = pl.program_id(0), pl.program_id(1)
        _side_steps(side, s_refs, first=(p == 0) & (g == 0), mid=(p == 1) & (g == 0))
        masks = _head_masks()

        @pl.when(g == 0)
        def _():
            _window_bias(bias_ref)
            for c in range(T // BLK):
                rows = pl.ds(BLK * c, BLK)
                ag, dob_v, o = ag_ref[rows, :], dob_ref[rows, :], o_ref[rows, :]
                sg = _sigmoid(ag)
                dag_ref[rows, :] = (dob_v * o * (sg * (1.0 + ag * (1.0 - sg)))).astype(BF)
                do = dob_v * (ag * sg)
                dotok_ref[rows, :] = do
                prod = do * o
                for hh, mh in enumerate(masks):
                    dtok_ref[hh, rows, :] = jnp.broadcast_to(jnp.sum(prod * mh, axis=1, keepdims=True), (BLK, BLK))

        def group(gi):
            dil = ATT_GROUPS[gi][1]
            nblk = (T // dil) // BLK
            for rows, dst in _pieces(dil):
                drows = pl.ds(dst, BLK)
                dor_ref[drows, :] = dotok_ref[rows, :]
                for hh, lse_ref in enumerate((lse0_ref, lse1_ref)):
                    lr_ref[hh, drows, :] = lse_ref[rows, :]
                    dr_ref[hh, drows, :] = dtok_ref[hh, rows, :]
            dkr_ref[...] = jnp.zeros_like(dkr_ref)
            dvr_ref[...] = jnp.zeros_like(dvr_ref)

            def probs(bi, slot):
                cur, prev = _blocks(bi)
                q2, do2 = _stack_heads(qs_ref[cur, :], masks), _stack_heads(dor_ref[cur, :], masks)
                lh = jnp.concatenate([lr_ref[0, cur, :], lr_ref[1, cur, :]], axis=0)
                dh = jnp.concatenate([dr_ref[0, cur, :], dr_ref[1, cur, :]], axis=0)
                p_c = jnp.exp(_dot(q2, ks_ref[cur, :], _NT) + bias_ref[0] - lh)
                pd_ref[slot, 0] = p_c.astype(BF)
                pd_ref[slot, 1] = (p_c * (_dot(do2, vs_ref[cur, :], _NT) - dh)).astype(BF)
                if nblk > 1:
                    bias_p = bias_ref[1] + jnp.where((bi % nblk) != 0, 0.0, -jnp.inf)
                    p_p = jnp.exp(_dot(q2, ks_ref[prev, :], _NT) + bias_p - lh)
                    pd_ref[slot, 2] = p_p.astype(BF)
                    pd_ref[slot, 3] = (p_p * (_dot(do2, vs_ref[prev, :], _NT) - dh)).astype(BF)

            def grads(bi, slot):
                cur, prev = _blocks(bi)
                q2, do2 = _stack_heads(qs_ref[cur, :], masks), _stack_heads(dor_ref[cur, :], masks)
                p_c, ds_c = pd_ref[slot, 0], pd_ref[slot, 1]
                dq2 = _dot(ds_c, ks_ref[cur, :])
                dkr_ref[cur, :] += _dot(ds_c, q2, _TN)
                dvr_ref[cur, :] += _dot(p_c, do2, _TN)
                if nblk > 1:
                    p_p, ds_p = pd_ref[slot, 2], pd_ref[slot, 3]
                    dq2 = dq2 + _dot(ds_p, ks_ref[prev, :])
                    dkr_ref[prev, :] += _dot(ds_p, q2, _TN)
                    dvr_ref[prev, :] += _dot(p_p, do2, _TN)
                dqr_ref[cur, :] = dq2[:BLK] * masks[0] + dq2[BLK:] * masks[1]

            def pair(j, carry):
                grads(2 * j, 0)
                probs(2 * j + 1, 1)
                grads(2 * j + 1, 1)
                probs(jnp.minimum(2 * j + 2, NBLK - 1), 0)
                return carry

            probs(0, 0)
            lax.fori_loop(0, NBLK // 2, pair, 0)
            if dil > 1:
                for rows, src in _pieces(dil):
                    srows = pl.ds(src, BLK)
                    qr_ref[rows, :] = dqr_ref[srows, :]
                    kr_ref[rows, :] = dkr_ref[srows, :]
                    vr_ref[rows, :] = dvr_ref[srows, :]
            tq, tk, tv = (qr_ref, kr_ref, vr_ref) if dil > 1 else (dqr_ref, dkr_ref, dvr_ref)
            for c in range(T // BLK):
                rows = pl.ds(BLK * c, BLK)
                cs, sa, sb = cos_ref[rows, :], sa_ref[rows, :], sb_ref[rows, :]
                dq_ref[rows, :] = _rope_t(tq[rows, :] * QK_SCALE, cs, sa, sb).astype(BF)
                dk_ref[rows, :] = _rope_t(tk[rows, :], cs, sa, sb).astype(BF)
                dv_ref[rows, :] = tv[rows, :].astype(BF)

        for gi in range(3):
            pl.when(g == gi)(functools.partial(group, gi))
        _side_steps(side, s_refs, last=(p == 3) & (g == 2))

    regrouped = pl.BlockSpec((None, T, BLK), lambda p, g: (g, 0, p))
    pspec = pl.BlockSpec((T, BLK), lambda p, g: (0, p))
    gspec = pl.BlockSpec((T, BLK), lambda p, g: (0, 4 * g + p))
    table = pl.BlockSpec((T, BLK), lambda p, g: (0, 0))
    big = lambda: pltpu.VMEM((T, BLK), F32)
    two = lambda: pltpu.VMEM((2, T, BLK), F32)
    out = pl.pallas_call(
        body, name="attn_bwd", grid=(4, 3),
        in_specs=[regrouped, regrouped, regrouped,
                  pl.BlockSpec((T, BLK), lambda p, g: (0, AG_COL0 // BLK + p)), table, table, table,
                  pspec, pl.BlockSpec((T, BLK), lambda p, g: (0, 2 * p)),
                  pl.BlockSpec((T, BLK), lambda p, g: (0, 2 * p + 1)), pspec] + s_in_specs,
        out_specs=[gspec, gspec, gspec, pspec] + s_out_specs,
        out_shape=[jax.ShapeDtypeStruct((T, 1536), BF), jax.ShapeDtypeStruct((T, 1536), BF),
                   jax.ShapeDtypeStruct((T, 1536), BF), jax.ShapeDtypeStruct((T, 512), BF)] + s_shapes,
        scratch_shapes=[pltpu.VMEM((2, 2 * BLK, BLK), F32), two(), big(), big(), big(), big(),
                        two(), two(), big(), big(), big(), pltpu.VMEM((2, 4, 2 * BLK, BLK), BF), big()] + s_sems,
        compiler_params=_params(("parallel" if side is None else "arbitrary", "arbitrary")),
    )(qs, ks, vs, z, cos, sa, sb, opre, lse, lse, dob, *s_arrays)
    return (*out[:4], out[4:])


def _merge_out_loss(og, ob, z, w_a, w_b, w_out, x, tgt, wf):
    tm = 512

    def body(og_ref, ob_ref, ga_ref, gb_ref, wa_ref, wb_ref, wo_ref, x_ref, t_ref, wf_ref,
             m_ref, dout_ref, loss_ref, gwf_ref):
        @pl.when(pl.program_id(0) == 0)
        def _():
            loss_ref[...] = jnp.zeros_like(loss_ref)
            gwf_ref[...] = jnp.zeros_like(gwf_ref)

        ya, yb = _dot(og_ref[...], wa_ref[...]), _dot(ob_ref[...], wb_ref[...])
        m =(_sigmoid(ga_ref[...]) * ya + _sigmoid(gb_ref[...]) * yb).astype(BF)
        m_ref[...] = m
        out = x_ref[...] + _dot(m, wo_ref[...])
        r = lax.rsqrt(jnp.mean(out * out, axis=-1, keepdims=True) + EPS)
        yh = out * r
        wfv = wf_ref[...]
        err = yh * wfv - t_ref[...]
        loss_ref[...] += jnp.sum(err * err, axis=0, keepdims=True) * (0.5 / D)
        dy = err * (1.0 / D)
        gwf_ref[...] += jnp.sum(dy * yh, axis=0, keepdims=True)
        dyh = dy * wfv
        dout_ref[...] = r * (dyh - yh * jnp.mean(dyh * yh, axis=-1, keepdims=True))

    row = pl.BlockSpec((tm, D), lambda i: (i, 0))
    vec = pl.BlockSpec((1, D), lambda i: (0, 0))
    whole = lambda w: pl.BlockSpec(w.shape, lambda i: (0, 0))
    return pl.pallas_call(
        body, name="merge_out_loss", grid=(T // tm,),
        in_specs=[row, pl.BlockSpec((tm, ob.shape[1]), lambda i: (i, 0)),
                  pl.BlockSpec((tm, D), lambda i: (i, GATE_COL0 // D)),
                  pl.BlockSpec((tm, D), lambda i: (i, GATE_COL0 // D + 1)),
                  whole(w_a), whole(w_b), whole(w_out), row, row, vec],
        out_specs=[row, row, vec, vec],
        out_shape=[jax.ShapeDtypeStruct((T, D), BF), jax.ShapeDtypeStruct((T, D), F32),
                   jax.ShapeDtypeStruct((1, D), F32), jax.ShapeDtypeStruct((1, D), F32)],
        compiler_params=_params(("arbitrary",)),
    )(og, ob, z, z, w_a, w_b, w_out, x, tgt, wf)


def _merge_proj_bwd(dout, og, ob, z, w_a, w_b, w_out):
    tm = 512

    def body(dout_ref, og_ref, ob_ref, ga_ref, gb_ref, wa_ref, wb_ref, wo_ref,
             dya_ref, dyb_ref, dg_ref, dog_ref, dob_ref):
        dmv = _dot(dout_ref[...].astype(BF), wo_ref[...], _NT)
        sa, sb = _sigmoid(ga_ref[...]), _sigmoid(gb_ref[...])
        dya, dyb = (sa * dmv).astype(BF), (sb * dmv).astype(BF)
        dya_ref[...] = dya
        dyb_ref[...] = dyb
        dg_ref[:, :D] = (dmv * _dot(og_ref[...], wa_ref[...]) * sa * (1.0 - sa)).astype(BF)
        dg_ref[:, D:] = (dmv * _dot(ob_ref[...], wb_ref[...]) * sb * (1.0 - sb)).astype(BF)
        dog_ref[...] = _dot(dya, wa_ref[...], _NT)
        dob_ref[...] = _dot(dyb, wb_ref[...], _NT)

    row = pl.BlockSpec((tm, D), lambda i: (i, 0))
    whole = lambda w: pl.BlockSpec(w.shape, lambda i: (0, 0))
    nb = w_b.shape[0]
    return pl.pallas_call(
        body, name="merge_proj_bwd", grid=(T // tm,),
        in_specs=[row, row, pl.BlockSpec((tm, nb), lambda i: (i, 0)),
                  pl.BlockSpec((tm, D), lambda i: (i, GATE_COL0 // D)),
                  pl.BlockSpec((tm, D), lambda i: (i, GATE_COL0 // D + 1)), whole(w_a), whole(w_b), whole(w_out)],
        out_specs=[row, row, pl.BlockSpec((tm, 2 * D), lambda i: (i, 0)), row,
                   pl.BlockSpec((tm, nb), lambda i: (i, 0))],
        out_shape=[jax.ShapeDtypeStruct((T, D), BF), jax.ShapeDtypeStruct((T, D), BF),
                   jax.ShapeDtypeStruct((T, 2 * D), BF), jax.ShapeDtypeStruct((T, D), F32),
                   jax.ShapeDtypeStruct((T, nb), F32)],
        compiler_params=_params(("parallel",)),
    )(dout, og, ob, z, z, w_a, w_b, w_out)


def _rope_inv_freq():
    inv = ROPE_THETA ** (-jnp.arange(0, 64, 2, dtype=F32) / 64)
    return jnp.tile(inv, 4).reshape(1, BLK)


def _local_step(x, pos, norm_w, lbl, hnw, wf, tgt, w_in, w_a, w_b, w_out, shard_shapes=()):
    invf = _rope_inv_freq()
    if shard_shapes:
        blk = jnp.reshape(2 * lax.axis_index("x") + lax.axis_index("y"), (1,)).astype(jnp.int32)
        h, cos, sa, sb, z_own, (w_a, w_b, w_out), (w_near,) = _norm_and_rope_tables(
            x, norm_w, pos, invf, side=_gather_near_side(w_in, WEIGHT_AXES[0]), own=(w_in, blk),
            cast=(w_a, w_b, w_out))
        near = jnp.concatenate([blk ^ 2, blk ^ 1])
        z, (w_diag,) = _z_blocks(h, w_near, z_own, jnp.concatenate([near, near]), 2, name="z_proj_near",
                                 side=_gather_diag_side(w_near, w_in.shape, WEIGHT_AXES[0]))
        z, w_in, _ = _z_blocks(h, w_diag, z, jnp.concatenate([blk ^ 3, jnp.zeros_like(blk)]), 1, name="z_proj_diag",
                               fill=w_near, side=None)
    else:
        h, cos, sa, sb, _, _, _ = _norm_and_rope_tables(x, norm_w, pos, invf)
        z = _matmul(h, w_in, tm=T, tn=512, name="z_proj")
    oraw, og, shist = _hgrn_fwd(z, lbl, hnw)
    side_a = _gather_side([w_a, w_b, w_out], WEIGHT_AXES[1:]) if shard_shapes else None
    ob, opre, lse, qs, ks, vs, gathered = _attn_fwd(z, cos, sa, sb, side=side_a)
    if shard_shapes:
        w_a, w_b, w_out = gathered
    merged, dout, loss_vec, g_wf = _merge_out_loss(og, ob, z, w_a, w_b, w_out, x, tgt, wf)

    dya, dyb, dgates, dog, dob = _merge_proj_bwd(dout, og, ob, z, w_a, w_b, w_out)
    g_wout = _matmul(merged, dout, ta=True, out_dtype=BF, tm=512, tn=1024, name="g_wout")
    g_wa = _matmul(og, dya, ta=True, out_dtype=BF, tm=512, tn=1024, name="g_wa")
    g_wb = _matmul(ob, dyb, ta=True, out_dtype=BF, tm=512, tn=1024, name="g_wb")
    small = [g_wa, g_wb, g_wout]
    side_s = side_w = None
    if shard_shapes:
        p3_s = [_as3d(g, s, ax) for g, s, ax in zip(small, shard_shapes[1:], WEIGHT_AXES[1:])]
        side_s = _chip_exchange_direct_side(p3_s, shard_shapes[1:], WEIGHT_AXES[1:])
    dz_h, dlb, g_hnw = _hgrn_bwd(z, lbl, hnw, oraw, dog, shist)
    dq, dk, dv, dag, land_s = _attn_bwd(z, qs, ks, vs, cos, sa, sb, opre, lse, dob, side=side_s)
    dz_parts = [dz_h, dq, dk, dv, dag, dgates]
    if shard_shapes:
        c = lax.axis_index("c")
        half = lambda i: jnp.reshape(i, (1,)).astype(jnp.int32)
        g_send = _grad_w_in_half(h, dz_parts, half(1 - c))
        g_keep, (g_sib,) = _grad_w_in_half(h, dz_parts, half(c), side=_sibling_send_side(g_send))
        p3_w = [_add_bf16(g_keep, g_sib, "pair_sum_w_in").reshape(1, D // 2, NIN)]
        side_w = _chip_exchange_relay_side(p3_w[0], shard_shapes[0])
    else:
        g_big = [_grad_w_in(h, dz_parts)] + small
    gx, g_nw, land_w = _grad_x(dz_parts, w_in, x, dout, norm_w, side=side_w)
    small_sums = None
    if shard_shapes:
        g_big, small_sums = _rs_finish(p3_w + p3_s, [land_w[0]] + list(land_s), shard_shapes, WEIGHT_AXES,
                                       (g_nw, dlb, g_hnw, g_wf, loss_vec))
    return dict(loss_vec=loss_vec, gx=gx, g_nw=g_nw, dlb=dlb, g_hnw=g_hnw, g_wf=g_wf, small_sums=small_sums,
                g_win=g_big[0], g_wa=g_big[1], g_wb=g_big[2], g_wout=g_big[3])


MESH = pl.DeviceIdType.MESH
HBM = pl.BlockSpec(memory_space=pl.ANY)
WEIGHT_AXES = (1, 0, 1, 0)


def _place():
    x, y, c = lax.axis_index("x"), lax.axis_index("y"), lax.axis_index("c")
    chips = [(1 - x, y), (x, 1 - y), (1 - x, 1 - y)]
    return x, y, c, chips


def _block_half(ref, shard_shape, axis, j, half):
    r, c = shard_shape
    hr = r // 2
    if axis == 0:
        return ref.at[pl.ds(pl.multiple_of(j * r + half * hr, 16), hr), :]
    return ref.at[pl.ds(pl.multiple_of(half * hr, 16), hr), pl.ds(pl.multiple_of(j * c, 128), c)]


PIECES = 4


def _block_piece(ref, shard_shape, axis, j, half, q):
    r, c = shard_shape
    pr = r // 2 // PIECES
    if axis == 0:
        return ref.at[pl.ds(pl.multiple_of(j * r + half * (r // 2) + q * pr, 16), pr), :]
    return ref.at[pl.ds(pl.multiple_of(half * (r // 2) + q * pr, 16), pr), pl.ds(pl.multiple_of(j * c, 128), c)]


class _Side:
    def __init__(self, arrays, out_shapes, sems, first, last, mid=None):
        self.arrays, self.out_shapes, self.sems, self.first, self.last = arrays, out_shapes, sems, first, last
        self.mid = mid


def _gather_side(shards, axes):
    n = len(shards)
    shapes = [s.shape for s in shards]

    def copies(ins, outs, sems):
        send1, recv1, send2, recv2, send0, recv0 = sems
        x, y, c, chips = _place()
        me = 2 * x + y
        sib = (x, y, 1 - c)
        near = ((1 - c) * (1 - x) + c * x, (1 - c) * y + c * (1 - y))
        far = ((1 - c) * x + c * (1 - x), (1 - c) * (1 - y) + c * y)
        out = []
        for a in range(n):
            r, cc = shapes[a]
            mine = (outs[a].at[pl.ds(pl.multiple_of(me * r, 16), r), :] if axes[a] == 0
                    else outs[a].at[:, pl.ds(pl.multiple_of(me * cc, 128), cc)])
            own = pltpu.make_async_remote_copy(
                src_ref=ins[a], dst_ref=mine, send_sem=send0.at[a], recv_sem=recv0.at[a],
                device_id=sib, device_id_type=MESH)
            src = ins[a].at[pl.ds(pl.multiple_of(c * (r // 2), 16), r // 2), :]
            sends = [pltpu.make_async_remote_copy(
                src_ref=src, dst_ref=_block_half(outs[a], shapes[a], axes[a], me, c),
                send_sem=send1.at[a, k], recv_sem=recv1.at[a, k], device_id=(*chips[k], c), device_id_type=MESH)
                for k in range(2)]

            def region(chip, half):
                return _block_half(outs[a], shapes[a], axes[a], 2 * chip[0] + chip[1], half)

            def arrival(chip, k):
                reg = region(chip, c)
                return pltpu.make_async_remote_copy(
                    src_ref=reg, dst_ref=reg, send_sem=send1.at[a, k], recv_sem=recv1.at[a, k],
                    device_id=(*chip, c), device_id_type=MESH)

            def to_sibling(chip, k):
                reg = region(chip, c)
                return pltpu.make_async_remote_copy(
                    src_ref=reg, dst_ref=reg, send_sem=send2.at[a, k], recv_sem=recv2.at[a, k],
                    device_id=sib, device_id_type=MESH)

            def from_sibling(chip, k):
                reg = region(chip, 1 - c)
                return pltpu.make_async_remote_copy(
                    src_ref=reg, dst_ref=reg, send_sem=send2.at[a, k], recv_sem=recv2.at[a, k],
                    device_id=sib, device_id_type=MESH)

            relay = pltpu.make_async_remote_copy(
                src_ref=region(near, c), dst_ref=region(near, c), send_sem=send1.at[a, 2], recv_sem=recv1.at[a, 2],
                device_id=(*far, c), device_id_type=MESH)
            hops = [(arrival(near, c), to_sibling(near, c)), (arrival(far, 1 - c), to_sibling(far, 1 - c)),
                    (arrival(chips[2], 2), to_sibling(chips[2], 2))]
            back = [from_sibling(chips[k], k) for k in range(3)]
            out.append((own, sends, relay, hops, back))
        return out

    def first(ins, outs, sems):
        for own, sends, _, _, _ in copies(ins, outs, sems):
            own.start()
            for cp in sends:
                cp.start()

    def mid(ins, outs, sems):
        per_array = copies(ins, outs, sems)
        for step in range(2):
            for _, _, relay, hops, _ in per_array:
                arrived, onward = hops[step]
                arrived.wait_recv()
                if step == 0:
                    relay.start()
                onward.start()

    def last(ins, outs, sems):
        per_array = copies(ins, outs, sems)
        for _, _, _, hops, _ in per_array:
            arrived, onward = hops[2]
            arrived.wait_recv()
            onward.start()
        for own, sends, relay, hops, back in per_array:
            for cp in back:
                cp.wait_recv()
            for cp in sends + [relay] + [onward for _, onward in hops]:
                cp.wait_send()
            own.wait()

    full = [(4 * r, c) if ax == 0 else (r, 4 * c) for (r, c), ax in zip(shapes, axes)]
    sems = [pltpu.SemaphoreType.DMA((n, 3)), pltpu.SemaphoreType.DMA((n, 3)),
            pltpu.SemaphoreType.DMA((n, 3)), pltpu.SemaphoreType.DMA((n, 3)),
            pltpu.SemaphoreType.DMA((n,)), pltpu.SemaphoreType.DMA((n,))]
    return _Side(list(shards), [jax.ShapeDtypeStruct(f, BF) for f in full], sems, first, last, mid)


def _gather_near_side(shard, axis):
    shape = shard.shape
    r, cc = shape

    def copies(ins, outs, sems):
        send1, recv1, send2, recv2, send0, recv0 = sems
        x, y, c, chips = _place()
        me = 2 * x + y
        sib = (x, y, 1 - c)
        mine = (outs[0].at[pl.ds(pl.multiple_of(me * r, 16), r), :] if axis == 0
                else outs[0].at[:, pl.ds(pl.multiple_of(me * cc, 128), cc)])
        own = pltpu.make_async_remote_copy(
            src_ref=ins[0], dst_ref=mine, send_sem=send0.at[0], recv_sem=recv0.at[0],
            device_id=sib, device_id_type=MESH)
        def region(k, half, q):
            return _block_piece(outs[0], shape, axis, 2 * chips[k][0] + chips[k][1], half, q)

        def moves(k, q):
            src = ins[0].at[pl.ds(pl.multiple_of(c * (r // 2) + q * (r // 2 // PIECES), 16), r // 2 // PIECES), :]
            return [pltpu.make_async_remote_copy(
                        src_ref=s, dst_ref=d, send_sem=ss.at[k, q], recv_sem=rs.at[k, q], device_id=dev,
                        device_id_type=MESH)
                    for s, d, ss, rs, dev in (
                        (src, _block_piece(outs[0], shape, axis, me, c, q), send1, recv1, (*chips[k], c)),
                        (region(k, c, q), region(k, c, q), send1, recv1, (*chips[k], c)),
                        (region(k, c, q), region(k, c, q), send2, recv2, sib),
                        (region(k, 1 - c, q), region(k, 1 - c, q), send2, recv2, sib))]

        return own, [moves(k, q) for q in range(PIECES) for k in range(2)]

    def first(ins, outs, sems):
        own, per_piece = copies(ins, outs, sems)
        own.start()
        for send, _, _, _ in per_piece:
            send.start()

    def last(ins, outs, sems):
        own, per_piece = copies(ins, outs, sems)
        for _, arrived, onward, _ in per_piece:
            arrived.wait_recv()
            onward.start()
        for send, _, onward, back in per_piece:
            back.wait_recv()
            send.wait_send()
            onward.wait_send()
        own.wait()

    full = (4 * r, cc) if axis == 0 else (r, 4 * cc)
    sems = [pltpu.SemaphoreType.DMA((2, PIECES))] * 4 + [pltpu.SemaphoreType.DMA((1,))] * 2
    return _Side([shard], [jax.ShapeDtypeStruct(full, BF)], sems, first, last)


def _gather_diag_side(gathered, shape, axis):
    r, cc = shape

    def copies(ins, outs, sems):
        send1, recv1, send2, recv2 = sems
        x, y, c, _ = _place()
        sib = (x, y, 1 - c)
        near = ((1 - c) * (1 - x) + c * x, (1 - c) * y + c * (1 - y))
        far = ((1 - c) * x + c * (1 - x), (1 - c) * (1 - y) + c * y)

        def piece(i, q):
            return outs[0].at[pl.ds(pl.multiple_of(i * (r // 2) + q * (r // 2 // PIECES), 16), r // 2 // PIECES), :]

        def moves(q):
            def move(s, d, ss, rs, dev):
                return pltpu.make_async_remote_copy(
                    src_ref=s, dst_ref=d, send_sem=ss.at[q], recv_sem=rs.at[q], device_id=dev, device_id_type=MESH)

            return (move(_block_piece(ins[0], shape, axis, 2 * near[0] + near[1], c, q), piece(c, q), send1, recv1,
                         (*far, c)),
                    move(piece(c, q), piece(c, q), send1, recv1, (*far, c)),
                    move(piece(c, q), piece(c, q), send2, recv2, sib),
                    move(piece(1 - c, q), piece(1 - c, q), send2, recv2, sib))

        return [moves(q) for q in range(PIECES)]

    def first(ins, outs, sems):
        for relay, _, _, _ in copies(ins, outs, sems):
            relay.start()

    def last(ins, outs, sems):
        per_piece = copies(ins, outs, sems)
        for _, arrived, onward, _ in per_piece:
            arrived.wait_recv()
            onward.start()
        for relay, _, onward, back in per_piece:
            back.wait_recv()
            relay.wait_send()
            onward.wait_send()

    return _Side([gathered], [jax.ShapeDtypeStruct(shape, BF)], [pltpu.SemaphoreType.DMA((PIECES,))] * 4,
                 first, last)


def _as3d(g, shard_shape, axis):
    r, c = shard_shape
    return g.reshape(4, r, c) if axis == 0 else g.reshape(1, r, 4 * c)


def _chip_exchange_direct_side(g3s, shapes, axes):
    n = len(g3s)

    def copies(ins, outs, sems):
        send, recv = sems
        x, y, c, chips = _place()
        sends, arrivals = [], []
        for a in range(n):
            r, cc = shapes[a]
            hr = r // 2

            def part(j, h, a=a, hr=hr, cc=cc):
                rows = pl.ds(pl.multiple_of(h * hr, 16), hr)
                return (ins[a].at[j, rows, :] if axes[a] == 0
                        else ins[a].at[0, rows, pl.ds(pl.multiple_of(j * cc, 128), cc)])

            def move(s, d, i_send, i_recv, dev, a=a):
                return pltpu.make_async_remote_copy(
                    src_ref=s, dst_ref=d, send_sem=send.at[a, i_send], recv_sem=recv.at[a, i_recv], device_id=dev,
                    device_id_type=MESH)

            for k, (px, py) in enumerate(chips):
                for h in range(2):
                    sends.append(move(part(2 * px + py, h), outs[a].at[2 * k + c], 2 * k + h, 2 * k + c, (px, py, h)))
                    slot = outs[a].at[2 * k + h]
                    arrivals.append(move(slot, slot, 2 * k + h, 2 * k + h, (px, py, h)))
            sends.append(move(part(2 * x + y, 1 - c), outs[a].at[6], 6, 6, (x, y, 1 - c)))
            arrivals.append(move(outs[a].at[6], outs[a].at[6], 6, 6, (x, y, 1 - c)))
        return sends, arrivals

    def first(ins, outs, sems):
        for cp in copies(ins, outs, sems)[0]:
            cp.start()

    def last(ins, outs, sems):
        sends, arrivals = copies(ins, outs, sems)
        for cp in arrivals:
            cp.wait_recv()
        for cp in sends:
            cp.wait_send()

    return _Side(list(g3s), [jax.ShapeDtypeStruct((7, r // 2, c), BF) for r, c in shapes],
                 [pltpu.SemaphoreType.DMA((n, 7)), pltpu.SemaphoreType.DMA((n, 7))], first, last)


def _chip_exchange_relay_side(p3, shape):
    r, cc = shape
    hr = r // 2
    rows = 64

    def copies(ins, outs, sems):
        send, recv, local, mine, theirs = sems
        x, y, c, chips = _place()
        near = ((1 - c) * (1 - x) + c * x, (1 - c) * y + c * (1 - y))
        far = ((1 - c) * x + c * (1 - x), (1 - c) * (1 - y) + c * y)
        land, staged = outs

        def block(chip):
            return ins[0].at[0, :, pl.ds(pl.multiple_of((2 * chip[0] + chip[1]) * cc, 128), cc)]

        def move(s, d, k, dev):
            return pltpu.make_async_remote_copy(
                src_ref=s, dst_ref=d, send_sem=send.at[k], recv_sem=recv.at[k], device_id=dev, device_id_type=MESH)

        return dict(
            direct=move(block(near), land.at[c], 0, (*near, c)),
            for_relay=move(block(chips[2]), staged, 1, (*near, c)),
            summed=move(mine, land.at[1 - c], 2, (*far, c)),
            direct_in=move(land.at[c], land.at[c], 0, (*near, c)),
            staged_in=move(staged, staged, 1, (*near, c)),
            summed_in=move(land.at[1 - c], land.at[1 - c], 2, (*far, c)),
            load_mine=pltpu.make_async_copy(block(far), mine, local.at[0]),
            load_theirs=pltpu.make_async_copy(staged, theirs, local.at[1]))

    def first(ins, outs, sems):
        cps = copies(ins, outs, sems)
        cps["for_relay"].start()
        cps["direct"].start()

    def mid(ins, outs, sems):
        cps = copies(ins, outs, sems)
        mine, theirs = sems[3], sems[4]
        cps["load_mine"].start()
        cps["staged_in"].wait_recv()
        cps["load_theirs"].start()
        cps["load_mine"].wait()
        cps["load_theirs"].wait()

        def add(i, carry):
            rs = pl.ds(pl.multiple_of(i * rows, 16), rows)
            mine[rs, :] = (mine[rs, :].astype(F32) + theirs[rs, :].astype(F32)).astype(BF)
            return carry

        lax.fori_loop(0, hr // rows, add, 0)
        cps["summed"].start()

    def last(ins, outs, sems):
        cps = copies(ins, outs, sems)
        cps["direct_in"].wait_recv()
        cps["summed_in"].wait_recv()
        for name in ("direct", "for_relay", "summed"):
            cps[name].wait_send()

    sems = [pltpu.SemaphoreType.DMA((3,)), pltpu.SemaphoreType.DMA((3,)), pltpu.SemaphoreType.DMA((2,)),
            pltpu.VMEM((hr, cc), BF), pltpu.VMEM((hr, cc), BF)]
    return _Side([p3], [jax.ShapeDtypeStruct((2, hr, cc), BF), jax.ShapeDtypeStruct((hr, cc), BF)], sems,
                 first, last, mid)


def _chip_sum(p3, land, shard_shape, axis, idx, name):
    r, c = shard_shape
    hr = r // 2
    tr = 128
    nt = hr // tr
    slots = land.shape[0]

    def body(idx_ref, p_ref, l_ref, o_ref):
        acc = p_ref[...].astype(F32)
        for k in range(slots):
            acc = acc + l_ref[k].astype(F32)
        o_ref[...] = acc

    own = (pl.BlockSpec((None, tr, c), lambda i, idx: (idx[0], i, 0)) if axis == 0
           else pl.BlockSpec((None, tr, c), lambda i, idx: (0, i, idx[0])))
    return pl.pallas_call(
        body, name=name,
        grid_spec=pltpu.PrefetchScalarGridSpec(
            num_scalar_prefetch=1, grid=(nt,),
            in_specs=[own, pl.BlockSpec((slots, tr, c), lambda i, idx: (0, i, 0))],
            out_specs=pl.BlockSpec((tr, c), lambda i, idx: (idx[1] * nt + i, 0))),
        out_shape=jax.ShapeDtypeStruct((r, c), F32),
        compiler_params=_params(("parallel",)),
    )(idx, p3, land)


def _chip_sums(p3s, lands, shapes, axes, idx, name):
    n = len(p3s)

    def body(idx_ref, *refs):
        for p_ref, l_ref, o_ref in zip(refs[:n], refs[n:2 * n], refs[2 * n:]):
            acc = p_ref[...].astype(F32)
            for k in range(l_ref.shape[0]):
                acc = acc + l_ref[k].astype(F32)
            o_ref[...] = acc

    own = [pl.BlockSpec((None, r // 2, c),
                        (lambda i, idx: (idx[0], idx[1], 0)) if ax == 0 else (lambda i, idx: (0, idx[1], idx[0])))
           for (r, c), ax in zip(shapes, axes)]
    return pl.pallas_call(
        body, name=name,
        grid_spec=pltpu.PrefetchScalarGridSpec(
            num_scalar_prefetch=1, grid=(1,),
            in_specs=own + [pl.BlockSpec(l.shape, lambda i, idx: (0, 0, 0)) for l in lands],
            out_specs=[pl.BlockSpec((r // 2, c), lambda i, idx: (idx[1], 0)) for r, c in shapes]),
        out_shape=[jax.ShapeDtypeStruct((r, c), F32) for r, c in shapes],
        compiler_params=_params(("arbitrary",)),
    )(idx, *p3s, *lands)


def _rs_pair_gather(fulls, small):
    n = len(fulls)

    def body(*refs):
        ins, small_refs, outs, red_ref = refs[:n], refs[n:n + 5], refs[n + 5:2 * n + 5], refs[2 * n + 5]
        send, recv = refs[2 * n + 6:2 * n + 8]
        x, y, c, _ = _place()
        cps = []
        for a in range(n):
            hr = fulls[a].shape[0] // 2
            rows = pl.ds(pl.multiple_of(c * hr, 8), hr)
            cp = pltpu.make_async_remote_copy(
                src_ref=ins[a].at[rows, :], dst_ref=outs[a].at[rows, :], send_sem=send.at[a], recv_sem=recv.at[a],
                device_id=(x, y, 1 - c), device_id_type=MESH)
            cp.start()
            cps.append(cp)
        _small_all_reduce(small_refs, red_ref, *refs[2 * n + 8:])
        for a, cp in enumerate(cps):
            cp.wait_send()
            hr = fulls[a].shape[0] // 2
            other = pl.ds(pl.multiple_of((1 - c) * hr, 8), hr)
            pltpu.make_async_remote_copy(
                src_ref=ins[a].at[other, :], dst_ref=outs[a].at[other, :], send_sem=send.at[a], recv_sem=recv.at[a],
                device_id=(x, y, 1 - c), device_id_type=MESH).wait_recv()

    vm = pl.BlockSpec(memory_space=pltpu.VMEM)
    out = pl.pallas_call(
        body, name="grads_pair_gather",
        in_specs=[HBM] * n + [vm] * 5, out_specs=[HBM] * n + [vm],
        out_shape=[jax.ShapeDtypeStruct(f.shape, F32) for f in fulls] + [jax.ShapeDtypeStruct((NSMALL, D), F32)],
        input_output_aliases={a: a for a in range(n)},
        scratch_shapes=[pltpu.SemaphoreType.DMA((n,)), pltpu.SemaphoreType.DMA((n,)),
                        pltpu.VMEM((NSMALL, D), F32), pltpu.VMEM((8, NSMALL, D), F32),
                        pltpu.SemaphoreType.DMA((7,)), pltpu.SemaphoreType.DMA((7,))],
    )(*fulls, *small)
    return out[:n], out[n]


def _sibling_send_side(arr):
    def copy(ins, outs, sems):
        x, y, c, _ = _place()
        return pltpu.make_async_remote_copy(
            src_ref=ins[0], dst_ref=outs[0], send_sem=sems[0].at[0], recv_sem=sems[1].at[0],
            device_id=(x, y, 1 - c), device_id_type=MESH)

    return _Side([arr], [jax.ShapeDtypeStruct(arr.shape, arr.dtype)],
                 [pltpu.SemaphoreType.DMA((1,)), pltpu.SemaphoreType.DMA((1,))],
                 lambda ins, outs, sems: copy(ins, outs, sems).start(),
                 lambda ins, outs, sems: copy(ins, outs, sems).wait())


def _add_bf16(a, b, name):
    r, c = a.shape
    tr = 128

    def body(a_ref, b_ref, o_ref):
        o_ref[...] = (a_ref[...].astype(F32) + b_ref[...].astype(F32)).astype(BF)

    blk = pl.BlockSpec((tr, c), lambda i: (i, 0))
    return pl.pallas_call(
        body, name=name, grid=(r // tr,), in_specs=[blk, blk], out_specs=blk,
        out_shape=jax.ShapeDtypeStruct((r, c), BF), compiler_params=_params(("parallel",)),
    )(a, b)


def _rs_finish(p3s, landed, shapes, axes, small):
    x, y, c = lax.axis_index("x"), lax.axis_index("y"), lax.axis_index("c")
    idx = jnp.stack([2 * x + y, c]).astype(jnp.int32)
    fulls = [_chip_sum(p3s[0], landed[0], shapes[0], axes[0], idx, "chip_sum_w_in")]
    fulls += _chip_sums(p3s[1:], landed[1:], shapes[1:], axes[1:], idx, "chip_sums_branches_out")
    return _rs_pair_gather(fulls, small)


NSMALL = 8


def _small_all_reduce(small_refs, out_ref, pack_ref, buf_ref, send, recv):
    nw_ref, lb_ref, hn_ref, wf_ref, ls_ref = small_refs
    x, y, c = lax.axis_index("x"), lax.axis_index("y"), lax.axis_index("c")
    me = 4 * x + 2 * y + c
    pack_ref[...] = jnp.zeros_like(pack_ref)
    pack_ref[0:1, :] = nw_ref[...]
    pack_ref[1:2, :] = lb_ref[...]
    pack_ref[2:3, 0:HK] = hn_ref[...]
    pack_ref[3:4, :] = wf_ref[...]
    pack_ref[4:5, :] = ls_ref[...]
    buf_ref[me] = pack_ref[...]
    cps = []
    for d in range(1, 8):
        dx, dy, dc = d >> 2, (d >> 1) & 1, d & 1
        peer = (1 - x if dx else x, 1 - y if dy else y, 1 - c if dc else c)
        cp = pltpu.make_async_remote_copy(
            src_ref=pack_ref, dst_ref=buf_ref.at[me], send_sem=send.at[d - 1], recv_sem=recv.at[d - 1],
            device_id=peer, device_id_type=MESH)
        cp.start()
        cps.append(cp)
    for d in range(1, 8):
        dx, dy, dc = d >> 2, (d >> 1) & 1, d & 1
        src = 4 * (1 - x if dx else x) + 2 * (1 - y if dy else y) + (1 - c if dc else c)
        pltpu.make_async_remote_copy(
            src_ref=pack_ref, dst_ref=buf_ref.at[src], send_sem=send.at[d - 1], recv_sem=recv.at[d - 1],
            device_id=(x, y, c), device_id_type=MESH).wait_recv()
    for cp in cps:
        cp.wait_send()
    acc = buf_ref[0]
    for i in range(1, 8):
        acc = acc + buf_ref[i]
    out_ref[...] = acc


def _adamw_math(w, g, m, v):
    m = B1 * m + (1.0 - B1) * g
    v = B2 * v + (1.0 - B2) * (g * g)
    m_hat = m / (1.0 - B1 ** STEP)
    v_hat = v / (1.0 - B2 ** STEP)
    return -LR * (m_hat / (jnp.sqrt(v_hat) + ADAM_EPS) + WD * w), m, v


def _adamw(w, g, m, v, name):
    r, c = w.shape
    tr = 128

    def body(w_ref, g_ref, m_ref, v_ref, d_ref, nm_ref, nv_ref, go_ref):
        g = g_ref[...]
        d_ref[...], nm_ref[...], nv_ref[...] = _adamw_math(w_ref[...], g, m_ref[...], v_ref[...])
        go_ref[...] = g

    blk = pl.BlockSpec((tr, c), lambda i: (i, 0))
    return pl.pallas_call(
        body, name=name, grid=(r // tr,), in_specs=[blk] * 4, out_specs=[blk] * 4,
        out_shape=[jax.ShapeDtypeStruct((r, c), F32)] * 4,
        compiler_params=_params(("parallel",)),
    )(w, g, m, v)


def _adamw_whole(groups, name):
    n = len(groups)

    def body(*refs):
        ins, outs = refs[:4 * n], refs[4 * n:]
        for a in range(n):
            w_ref, g_ref, m_ref, v_ref = ins[4 * a:4 * a + 4]
            g = g_ref[...]
            outs[4 * a][...], outs[4 * a + 1][...], outs[4 * a + 2][...] = _adamw_math(
                w_ref[...], g, m_ref[...], v_ref[...])
            outs[4 * a + 3][...] = g

    vm = pl.BlockSpec(memory_space=pltpu.VMEM)
    out = pl.pallas_call(
        body, name=name, in_specs=[vm] * (4 * n), out_specs=[vm] * (4 * n),
        out_shape=[jax.ShapeDtypeStruct(grp[0].shape, F32) for grp in groups for _ in range(4)],
        compiler_params=_params(),
    )(*[a for grp in groups for a in grp])
    return [out[4 * a:4 * a + 4] for a in range(n)]


def _small_update(red, lbl, params):
    def body(red_ref, *refs):
        ins, outs = refs[:12], refs[12:]
        lb = _lower_bound(ins[3][...])
        dl0 = red_ref[1:2, :] * lb * (1.0 - lb)
        row = lax.broadcasted_iota(jnp.int32, (2, D), 0)
        grads = [red_ref[0:1, :], jnp.where(row == 0, dl0, -dl0), red_ref[2:3, 0:HK], red_ref[3:4, :]]
        for i, g in enumerate(grads):
            w, m, v = ins[3 * i][...], ins[3 * i + 1][...], ins[3 * i + 2][...]
            d, nm, nv = _adamw_math(w, g, m, v)
            outs[4 * i][...] = g
            outs[4 * i + 1][...] = d
            outs[4 * i + 2][...] = nm
            outs[4 * i + 3][...] = nv
        outs[16][...] = jnp.sum(red_ref[4:5, :], axis=1, keepdims=True)

    flat = [a for p in params for a in p]
    vm = pl.BlockSpec(memory_space=pltpu.VMEM)
    shapes = [jax.ShapeDtypeStruct(p[0].shape, F32) for p in params for _ in range(4)]
    return pl.pallas_call(
        body, name="small_update",
        in_specs=[vm] * 13, out_specs=[vm] * 17,
        out_shape=shapes + [jax.ShapeDtypeStruct((1, 1), F32)],
    )(red, *flat)


def kernel(x, positions, norm_w, w_in, lb_logits, hgrn_norm_w, w_branch_a, w_branch_b, w_out, final_norm_w, loss_target, m_norm_w, m_w_in, m_lb_logits, m_hgrn_norm_w, m_w_branch_a, m_w_branch_b, m_w_out, m_final_norm_w, v_norm_w, v_w_in, v_lb_logits, v_hgrn_norm_w, v_w_branch_a, v_w_branch_b, v_w_out, v_final_norm_w):
    big_w = [w_in[0], w_branch_a[0], w_branch_b[0], w_out[0]]
    big_m = [m_w_in[0], m_w_branch_a[0], m_w_branch_b[0], m_w_out[0]]
    big_v = [v_w_in[0], v_w_branch_a[0], v_w_branch_b[0], v_w_out[0]]
    shapes = [w.shape for w in big_w]
    wf = final_norm_w.reshape(1, D)

    shards = [big_w[0].astype(BF)] + big_w[1:]
    loc = _local_step(x[0], positions.reshape(T, 1), norm_w, lb_logits, hgrn_norm_w, wf, loss_target[0],
                      *shards, shard_shapes=shapes)
    g_big = [loc["g_win"], loc["g_wa"], loc["g_wb"], loc["g_wout"]]
    red = loc["small_sums"]

    small = _small_update(red, lb_logits, [
        (norm_w, m_norm_w, v_norm_w), (lb_logits, m_lb_logits, v_lb_logits),
        (hgrn_norm_w, m_hgrn_norm_w, v_hgrn_norm_w),
        (wf, m_final_norm_w.reshape(1, D), v_final_norm_w.reshape(1, D))])
    loss = small[16].reshape(())
    sg, sd, sm, sv = ([small[4 * i + j] for i in range(4)] for j in range(4))
    for lst in (sg, sd, sm, sv):
        lst[3] = lst[3].reshape(D)
    per_w = list(zip(big_w, g_big, big_m, big_v))
    upd = [_adamw(*per_w[0], "adamw_w_in")] + _adamw_whole(per_w[1:], "adamw_branches_out")
    bd, bm, bv, bg = ([u[j][None] for u in upd] for j in range(4))

    def order(s, b):
        return [s[0], b[0], s[1], s[2], b[1], b[2], b[3], s[3]]

    return (loss, loc["gx"][None], *order(sg, bg), *order(sd, bd), *order(sm, bm), *order(sv, bv))
```

```python
import functools

import jax
import jax.numpy as jnp
from jax import lax
from jax.experimental import pallas as pl
from jax.experimental.pallas import tpu as pltpu

T = 2048
D = 1024
NIN = 11264
HEADS = 8
HK = 128
CH = 16
NCH = T // CH
HSTEP = 2
ATT_GROUPS = ((128, 1), (512, 4), (2048, 16))
ATT_COL0 = 4096
AG_COL0 = 8704
GATE_COL0 = 9216
EPS = 1e-6
ROPE_THETA = 10000.0
LR, B1, B2, ADAM_EPS, WD, STEP = 0.001, 0.9, 0.999, 1e-08, 0.01, 10

F32 = jnp.float32
BF = jnp.bfloat16
VMEM_LIMIT = 56 * 1024 * 1024

_NN = (((1,), (0,)), ((), ()))
_NT = (((1,), (1,)), ((), ()))
_TN = (((0,), (0,)), ((), ()))


def _dot(a, b, dims=_NN):
    return lax.dot_general(a, b, dims, preferred_element_type=F32)


def _bdot(a, b, dims=_NN):
    return lax.dot_general(a.astype(BF), b.astype(BF), dims, preferred_element_type=F32)


def _sigmoid(x):
    return jax.nn.sigmoid(x)


def _params(sem=None):
    return pltpu.CompilerParams(dimension_semantics=sem, vmem_limit_bytes=VMEM_LIMIT)


def _matmul(a, b, *, ta=False, tb=False, out_dtype=F32, tm=512, tn=512, tk=None, name, side=None):
    m = a.shape[1] if ta else a.shape[0]
    kdim = a.shape[0] if ta else a.shape[1]
    n = b.shape[0] if tb else b.shape[1]
    tk = tk or kdim
    tm, tn = min(tm, m), min(tn, n)
    nm, nn, nk = m // tm, n // tn, kdim // tk
    dims = (((0 if ta else 1,), (1 if tb else 0,)), ((), ()))
    s_arrays, s_in_specs, s_shapes, s_out_specs, s_sems = _side_io(side)
    na, no = len(s_arrays), len(s_shapes)
    nacc = 1 if nk > 1 else 0

    def body(*refs):
        a_ref, b_ref = refs[:2]
        s_ins, o_ref, s_outs = refs[2:2 + na], refs[2 + na], refs[3 + na:3 + na + no]
        scratch = refs[3 + na + no:]
        s_sem_refs = scratch[nacc:]
        i, j, k = pl.program_id(0), pl.program_id(1), pl.program_id(2)
        if side is not None:
            @pl.when((i == 0) & (j == 0) & (k == 0))
            def _():
                side.first(s_ins, s_outs, s_sem_refs)

        prod = _bdot(a_ref[...], b_ref[...], dims)
        if nk == 1:
            o_ref[...] = prod.astype(out_dtype)
        else:
            acc = scratch[0]

            @pl.when(k == 0)
            def _():
                acc[...] = prod

            @pl.when(k > 0)
            def _():
                acc[...] += prod

            @pl.when(k == nk - 1)
            def _():
                o_ref[...] = acc[...].astype(out_dtype)

        if side is not None:
            @pl.when((i == nm - 1) & (j == nn - 1) & (k == nk - 1))
            def _():
                side.last(s_ins, s_outs, s_sem_refs)

    a_spec = pl.BlockSpec((tk, tm), lambda i, j, k: (k, i)) if ta else pl.BlockSpec((tm, tk), lambda i, j, k: (i, k))
    b_spec = pl.BlockSpec((tn, tk), lambda i, j, k: (j, k)) if tb else pl.BlockSpec((tk, tn), lambda i, j, k: (k, j))
    sem = ("parallel", "parallel", "arbitrary") if side is None else ("arbitrary",) * 3
    out = pl.pallas_call(
        body, name=name, grid=(nm, nn, nk),
        in_specs=[a_spec, b_spec] + s_in_specs,
        out_specs=[pl.BlockSpec((tm, tn), lambda i, j, k: (i, j))] + s_out_specs,
        out_shape=[jax.ShapeDtypeStruct((m, n), out_dtype)] + s_shapes,
        scratch_shapes=([pltpu.VMEM((tm, tn), F32)] if nk > 1 else []) + s_sems,
        compiler_params=_params(sem),
    )(a, b, *s_arrays)
    return out[0] if side is None else (out[0], out[1:])


DZ_TILE = 512


def _part_offsets(parts):
    counts = [p.shape[1] // DZ_TILE for p in parts]
    offs = [sum(counts[:i]) for i in range(len(parts))]
    return counts, offs


def _part_spec(rows, cnt, off, tile_axis):
    def index(*g):
        return (0 if rows is None else g[0], jnp.clip(g[tile_axis] - off, 0, cnt - 1))
    return index


def _grad_w_in(h, parts):
    counts, offs = _part_offsets(parts)
    n = len(parts)

    def body(h_ref, *refs):
        o_ref = refs[n]
        j = pl.program_id(0)
        for p_ref, cnt, off in zip(refs[:n], counts, offs):
            @pl.when((j >= off) & (j < off + cnt))
            def _(p_ref=p_ref):
                o_ref[...] = _bdot(h_ref[...], p_ref[...], _TN).astype(BF)

    return pl.pallas_call(
        body, name="g_win", grid=(sum(counts),),
        in_specs=[pl.BlockSpec((T, D), lambda j: (0, 0))] +
                 [pl.BlockSpec((T, DZ_TILE), _part_spec(None, c, o, 0)) for c, o in zip(counts, offs)],
        out_specs=pl.BlockSpec((D, DZ_TILE), lambda j: (0, j)),
        out_shape=jax.ShapeDtypeStruct((D, NIN), BF),
        compiler_params=_params(("parallel",)),
    )(h, *parts)


def _grad_w_in_half(h, parts, half_idx, side=None):
    counts, offs = _part_offsets(parts)
    n = len(parts)
    nj = sum(counts)
    s_arrays, s_in_specs, s_shapes, s_out_specs, s_sems = _side_io(side)
    na, no = len(s_arrays), len(s_shapes)

    def body(idx_ref, h_ref, *refs):
        s_ins, o_ref, s_outs, s_sem_refs = refs[n:n + na], refs[n + na], refs[n + na + 1:n + na + 1 + no], refs[n + na + 1 + no:]
        j = pl.program_id(0)
        if side is not None:
            @pl.when(j == 0)
            def _():
                side.first(s_ins, s_outs, s_sem_refs)

        for p_ref, cnt, off in zip(refs[:n], counts, offs):
            @pl.when((j >= off) & (j < off + cnt))
            def _(p_ref=p_ref):
                o_ref[...] = _bdot(h_ref[...], p_ref[...], _TN).astype(BF)

        if side is not None:
            @pl.when(j == nj - 1)
            def _():
                side.last(s_ins, s_outs, s_sem_refs)

    def part_spec(cnt, off):
        return pl.BlockSpec((T, DZ_TILE), lambda j, idx: (0, jnp.clip(j - off, 0, cnt - 1)))

    out = pl.pallas_call(
        body, name="g_win_half" if side is None else "g_win_half_carrying",
        grid_spec=pltpu.PrefetchScalarGridSpec(
            num_scalar_prefetch=1, grid=(nj,),
            in_specs=[pl.BlockSpec((T, D // 2), lambda j, idx: (0, idx[0]))] +
                     [part_spec(c, o) for c, o in zip(counts, offs)] + s_in_specs,
            out_specs=[pl.BlockSpec((D // 2, DZ_TILE), lambda j, idx: (0, j))] + s_out_specs,
            scratch_shapes=s_sems),
        out_shape=[jax.ShapeDtypeStruct((D // 2, NIN), BF)] + s_shapes,
        compiler_params=_params(("parallel",) if side is None else ("arbitrary",)),
    )(half_idx, h, *parts, *s_arrays)
    return out[0] if side is None else (out[0], out[1:])


def _side_io(side):
    if side is None:
        return [], [], [], [], []
    return (side.arrays, [HBM] * len(side.arrays), side.out_shapes, [HBM] * len(side.out_shapes), side.sems)


def _grad_x(parts, w_in, x, dout, norm_w, side=None):
    counts, offs = _part_offsets(parts)
    n = len(parts)
    tm = 1024
    nm, nk = T // tm, sum(counts)
    s_arrays, s_in_specs, s_shapes, s_out_specs, s_sems = _side_io(side)
    na, no = len(s_arrays), len(s_shapes)

    def body(*refs):
        w_ref, x_ref, dout_ref, nw_ref = refs[n:n + 4]
        s_ins = refs[n + 4:n + 4 + na]
        gx_ref, gw_ref = refs[n + 4 + na:n + 6 + na]
        s_outs = refs[n + 6 + na:n + 6 + na + no]
        acc = refs[n + 6 + na + no]
        s_sem_refs = refs[n + 7 + na + no:]
        i, k = pl.program_id(0), pl.program_id(1)

        @pl.when((i == 0) & (k == 0))
        def _():
            gw_ref[...] = jnp.zeros_like(gw_ref)
            if side is not None:
                side.first(s_ins, s_outs, s_sem_refs)

        @pl.when(k == 0)
        def _():
            acc[...] = jnp.zeros_like(acc)

        if side is not None and side.mid is not None:
            @pl.when((i == nm - 1) & (k == 0))
            def _():
                side.mid(s_ins, s_outs, s_sem_refs)

        for p_ref, cnt, off in zip(refs[:n], counts, offs):
            @pl.when((k >= off) & (k < off + cnt))
            def _(p_ref=p_ref):
                acc[...] += _bdot(p_ref[...], w_ref[...], _NT)

        @pl.when(k == nk - 1)
        def _():
            gw = jnp.zeros((1, D), F32)
            for c in range(tm // BLK):
                rows = pl.ds(BLK * c, BLK)
                xv, dhv = x_ref[rows, :], acc[rows, :]
                r = lax.rsqrt(jnp.mean(xv * xv, axis=-1, keepdims=True) + EPS)
                nrm = xv * r
                dn = dhv * nw_ref[...]
                gw = gw + jnp.sum(dhv * nrm, axis=0, keepdims=True)
                gx_ref[rows, :] = dout_ref[rows, :] + r * (dn - nrm * jnp.mean(dn * nrm, axis=-1, keepdims=True))
            gw_ref[...] += gw

        if side is not None:
            @pl.when((i == nm - 1) & (k == nk - 1))
            def _():
                side.last(s_ins, s_outs, s_sem_refs)

    row = pl.BlockSpec((tm, D), lambda i, k: (i, 0))
    vec = pl.BlockSpec((1, D), lambda i, k: (0, 0))
    out = pl.pallas_call(
        body, name="grad_x", grid=(nm, nk),
        in_specs=[pl.BlockSpec((tm, DZ_TILE), _part_spec(0, c, o, 1)) for c, o in zip(counts, offs)] +
                 [pl.BlockSpec((D, DZ_TILE), lambda i, k: (0, k)), row, row, vec] + s_in_specs,
        out_specs=[row, vec] + s_out_specs,
        out_shape=[jax.ShapeDtypeStruct((T, D), F32), jax.ShapeDtypeStruct((1, D), F32)] + s_shapes,
        scratch_shapes=[pltpu.VMEM((tm, D), F32)] + s_sems,
        compiler_params=_params(("arbitrary", "arbitrary")),
    )(*parts, w_in, x, dout, norm_w, *s_arrays)
    return out[0], out[1], out[2:]


def _norm_and_rope_tables(x, w, pos, invf, side=None, own=None, cast=()):
    tm = 256
    nm = T // tm
    s_arrays, s_in_specs, s_shapes, s_out_specs, s_sems = _side_io(side)
    na, no, nc = len(s_arrays), len(s_shapes), len(cast)
    nz = 0 if own is None else 1
    wsh, blk = own if own is not None else (None, jnp.zeros((1,), jnp.int32))
    first_out = 4 + nz + nc + na

    def body(blk_ref, *refs):
        x_ref, w_ref, pos_ref, invf_ref = refs[:4]
        s_ins = refs[4 + nz + nc:first_out]
        h_ref, cos_ref, sa_ref, sb_ref = refs[first_out:first_out + 4]
        s_outs = refs[first_out + 4 + nz + nc:first_out + 4 + nz + nc + no]
        s_sem_refs = refs[first_out + 4 + nz + nc + no:]

        @pl.when(pl.program_id(0) == 0)
        def _():
            if side is not None:
                side.first(s_ins, s_outs, s_sem_refs)
            for src, dst in zip(refs[4 + nz:4 + nz + nc], refs[first_out + 4 + nz:first_out + 4 + nz + nc]):
                dst[...] = src[...].astype(BF)

        xv = x_ref[...]
        r = lax.rsqrt(jnp.mean(xv * xv, axis=-1, keepdims=True) + EPS)
        h = (xv * r * w_ref[...]).astype(BF)
        h_ref[...] = h
        if own is not None:
            refs[first_out + 4][...] = _dot(h, refs[4][...])
        first = (lax.broadcasted_iota(jnp.int32, (tm, 128), 1) % 64) < 32
        ang = pos_ref[...].astype(F32) * invf_ref[...]
        s = jnp.sin(ang)
        cos_ref[...] = jnp.cos(ang)
        sa_ref[...] = jnp.where(first, -s, 0.0)
        sb_ref[...] = jnp.where(first, 0.0, s)
        if side is not None:
            @pl.when(pl.program_id(0) == nm - 1)
            def _():
                side.last(s_ins, s_outs, s_sem_refs)

    tab = pl.BlockSpec((tm, 128), lambda i, b: (i, 0))
    own_in = [] if own is None else [pl.BlockSpec(wsh.shape, lambda i, b: (0, 0))]
    own_out = [] if own is None else [pl.BlockSpec((tm, wsh.shape[1]), lambda i, b: (i, b[0]))]
    own_shape = [] if own is None else [jax.ShapeDtypeStruct((T, NIN), F32)]
    whole = [pl.BlockSpec(a.shape, lambda i, b: (0, 0)) for a in cast]
    out = pl.pallas_call(
        body, name="norm_and_rope_tables",
        grid_spec=pltpu.PrefetchScalarGridSpec(
            num_scalar_prefetch=1, grid=(nm,),
            in_specs=[pl.BlockSpec((tm, D), lambda i, b: (i, 0)), pl.BlockSpec((1, D), lambda i, b: (0, 0)),
                      pl.BlockSpec((tm, 1), lambda i, b: (i, 0)), pl.BlockSpec((1, 128), lambda i, b: (0, 0))]
                     + own_in + whole + s_in_specs,
            out_specs=[pl.BlockSpec((tm, D), lambda i, b: (i, 0)), tab, tab, tab] + own_out + whole + s_out_specs,
            scratch_shapes=s_sems),
        out_shape=[jax.ShapeDtypeStruct((T, D), BF)] + [jax.ShapeDtypeStruct((T, 128), F32)] * 3 + own_shape
                  + [jax.ShapeDtypeStruct(a.shape, BF) for a in cast] + s_shapes,
        compiler_params=_params(("parallel",) if side is None and not cast else ("arbitrary",)),
    )(blk, x, w, pos, invf, *([] if own is None else [wsh]), *cast, *s_arrays)
    return (out[0], out[1], out[2], out[3], (out[4] if own is not None else None), out[4 + nz:4 + nz + nc],
            out[4 + nz + nc:])


def _z_blocks(h, w, z, idx, nb, side, name, fill=None):
    tm, tn = 1024, NIN // 8
    s_arrays, s_in_specs, s_shapes, s_out_specs, s_sems = _side_io(side)
    na, no = len(s_arrays), len(s_shapes)
    nm, ns = T // tm, 2 * nb
    nf = 0 if fill is None else 1

    def col(first, i, s, b):
        return (0, b[first + s // 2] * 2 + s % 2)

    def body(idx_ref, h_ref, w_ref, zin_ref, *refs):
        s_ins = refs[nf:nf + na]
        o_ref = refs[nf + na]
        s_outs = refs[nf + na + 1 + nf:nf + na + 1 + nf + no]
        s_sem_refs = refs[nf + na + 1 + nf + no + nf:]
        i, s = pl.program_id(0), pl.program_id(1)

        if side is not None:
            @pl.when((i == 0) & (s == 0))
            def _():
                side.first(s_ins, s_outs, s_sem_refs)

        if fill is not None:
            tile = pl.ds(pl.multiple_of((idx_ref[0] * 2 + s) * tn, 128), tn)
            store = pltpu.make_async_copy(w_ref, refs[nf + na + 1].at[:, tile], refs[nf + na + 1 + nf + no].at[0])
            pl.when(i == 0)(store.start)
        o_ref[...] = _dot(h_ref[...], w_ref[...])
        if fill is not None:
            pl.when(i == 0)(store.wait)

        if side is not None:
            @pl.when((i == nm - 1) & (s == ns - 1))
            def _():
                side.last(s_ins, s_outs, s_sem_refs)

    fills = [] if fill is None else [fill]
    out = pl.pallas_call(
        body, name=name,
        grid_spec=pltpu.PrefetchScalarGridSpec(
            num_scalar_prefetch=1, grid=(nm, ns),
            in_specs=[pl.BlockSpec((tm, D), lambda i, s, b: (i, 0)), pl.BlockSpec((D, tn), functools.partial(col, nb)),
                      HBM] + [HBM] * nf + s_in_specs,
            out_specs=[pl.BlockSpec((tm, tn), lambda i, s, b: (i, col(0, i, s, b)[1]))] + [HBM] * nf + s_out_specs,
            scratch_shapes=[pltpu.SemaphoreType.DMA((1,))] * nf + s_sems),
        out_shape=[jax.ShapeDtypeStruct((T, NIN), F32)] + [jax.ShapeDtypeStruct(f.shape, f.dtype) for f in fills]
                  + s_shapes,
        input_output_aliases={3: 0, **({4: 1} if fill is not None else {})},
        compiler_params=_params(("arbitrary", "arbitrary")),
    )(idx, h, w, z, *fills, *s_arrays)
    return (out[0], *out[1:1 + nf], out[1 + nf:])


def _lower_bound(lbl):
    mx = jnp.max(lbl, axis=0, keepdims=True)
    e = jnp.exp(lbl - mx)
    return e[0:1] / jnp.sum(e, axis=0, keepdims=True)


def _cumsum_rows(g, rows):
    b = g
    sh = 1
    while sh < CH:
        b = b + jnp.where(rows >= sh, pltpu.roll(b, sh, axis=0), 0.0)
        sh *= 2
    return b


def _rev_cumsum_rows(g, rows):
    b = g
    sh = 1
    while sh < CH:
        b = b + jnp.where(rows < CH - sh, pltpu.roll(b, CH - sh, axis=0), 0.0)
        sh *= 2
    return b


SUB = CH // 2


def _direct_block(qb, kb, vb, bb, rows8):
    ob = jnp.zeros_like(qb)
    for s in range(SUB):
        e_s = jnp.exp(jnp.where(rows8 >= s, bb - bb[s:s + 1], -jnp.inf))
        ob = ob + jnp.sum(qb * e_s * kb[s:s + 1], axis=1, keepdims=True) * vb[s:s + 1]
    return ob


def _direct_block_bwd(qb, kb, vb, bb, dob, rows8, rowc8):
    dq = dk = dv = db = jnp.zeros_like(qb)
    for s in range(SUB):
        one = (rowc8 == s).astype(F32)
        ks, vs = kb[s:s + 1], vb[s:s + 1]
        e_s = jnp.exp(jnp.where(rows8 >= s, bb - bb[s:s + 1], -jnp.inf))
        qes = qb * e_s
        w = qes * ks
        a = jnp.sum(w, axis=1, keepdims=True)
        da = jnp.sum(dob * vs, axis=1, keepdims=True)
        dv = dv + one * jnp.sum(a * dob, axis=0, keepdims=True)
        dq = dq + da * e_s * ks
        dk = dk + one * jnp.sum(da * qes, axis=0, keepdims=True)
        u = da * w
        db = db + u - one * jnp.sum(u, axis=0, keepdims=True)
    return dq, dk, dv, db


def _cross_factors(q, k, b):
    ref = b[SUB - 1:SUB]
    e_hi, e_lo = jnp.exp(b[SUB:] - ref), jnp.exp(ref - b[:SUB])
    return q[SUB:] * e_hi, k[:SUB] * e_lo, e_hi, e_lo


def _intra_fwd(q, k, v, b, rows8):
    lo = _direct_block(q[:SUB], k[:SUB], v[:SUB], b[:SUB], rows8)
    hi = _direct_block(q[SUB:], k[SUB:], v[SUB:], b[SUB:], rows8)
    qe_hi, ke_lo, _, _ = _cross_factors(q, k, b)
    for s in range(SUB):
        hi = hi + jnp.sum(qe_hi * ke_lo[s:s + 1], axis=1, keepdims=True) * v[s:s + 1]
    return jnp.concatenate([lo, hi], axis=0)


def _intra_bwd(q, k, v, b, do, rows8, rowc8):
    dq_lo, dk_lo, dv_lo, db_lo = _direct_block_bwd(q[:SUB], k[:SUB], v[:SUB], b[:SUB], do[:SUB], rows8, rowc8)
    dq_hi, dk_hi, dv_hi, db_hi = _direct_block_bwd(q[SUB:], k[SUB:], v[SUB:], b[SUB:], do[SUB:], rows8, rowc8)
    qe_hi, ke_lo, e_hi, e_lo = _cross_factors(q, k, b)
    do_hi, v_lo = do[SUB:], v[:SUB]
    dqe = dke = jnp.zeros_like(qe_hi)
    for s in range(SUB):
        one = (rowc8 == s).astype(F32)
        a = jnp.sum(qe_hi * ke_lo[s:s + 1], axis=1, keepdims=True)
        da = jnp.sum(do_hi * v_lo[s:s + 1], axis=1, keepdims=True)
        dv_lo = dv_lo + one * jnp.sum(a * do_hi, axis=0, keepdims=True)
        dqe = dqe + da * ke_lo[s:s + 1]
        dke = dke + one * jnp.sum(da * qe_hi, axis=0, keepdims=True)
    u_hi, u_lo = dqe * qe_hi, dke * ke_lo
    d_ref = jnp.sum(u_lo, axis=0, keepdims=True) - jnp.sum(u_hi, axis=0, keepdims=True)
    db_lo = db_lo - u_lo + (rowc8 == SUB - 1).astype(F32) * d_ref
    cat = lambda lo, hi: jnp.concatenate([lo, hi], axis=0)
    return (cat(dq_lo, dq_hi + dqe * e_hi), cat(dk_lo + dke * e_lo, dk_hi), cat(dv_lo, dv_hi),
            cat(db_lo, db_hi + u_hi))


def _hgrn_fwd(z, lbl, nw):
    def body(hq_ref, hf_ref, hi_ref, hg_ref, lbl_ref, nw_ref, oraw_ref, og_ref, sh_ref, st_ref):
        @pl.when(pl.program_id(0) == 0)
        def _():
            st_ref[...] = jnp.zeros_like(st_ref)

        lb_all = _lower_bound(lbl_ref[...])
        rows = lax.broadcasted_iota(jnp.int32, (CH, HK), 0)
        rows8 = lax.broadcasted_iota(jnp.int32, (SUB, HK), 0)
        nwv = nw_ref[...]
        for cc, h in [(cc, h) for cc in range(HSTEP) for h in range(HEADS)]:
            rs = slice(CH * cc, CH * (cc + 1))
            sl = slice(HK * h, HK * (h + 1))
            lb = lb_all[:, sl]
            hq, hf, v, hg = hq_ref[rs, sl], hf_ref[rs, sl], hi_ref[rs, sl], hg_ref[rs, sl]
            q = hq * _sigmoid(hq)
            f = lb + (1.0 - lb) * _sigmoid(hf)
            k = 1.0 - f
            b = _cumsum_rows(jnp.log(f), rows)
            sh_ref[cc, h] = st_ref[h]
            o = _bdot(q * jnp.exp(b), st_ref[h], _NT) + _intra_fwd(q, k, v, b, rows8)
            bl = b[CH - 1:CH]
            st_ref[h] = st_ref[h] * jnp.exp(bl)
            st_ref[h] += _bdot(v, k * jnp.exp(bl - b), _TN)
            oraw_ref[rs, sl] = o
            nrm = o * lax.rsqrt(jnp.mean(o * o, axis=1, keepdims=True) + EPS)
            og_ref[rs, sl] = (nrm * nwv * (hg * _sigmoid(hg))).astype(BF)

    zblk = lambda c: pl.BlockSpec((CH * HSTEP, D), lambda i, c=c: (i, c))
    return pl.pallas_call(
        body, name="hgrn_fwd", grid=(NCH // HSTEP,),
        in_specs=[zblk(0), zblk(1), zblk(2), zblk(3),
                  pl.BlockSpec((2, D), lambda i: (0, 0)), pl.BlockSpec((1, HK), lambda i: (0, 0))],
        out_specs=[zblk(0), zblk(0),
                   pl.BlockSpec((HSTEP, HEADS, HK, HK), lambda i: (i, 0, 0, 0))],
        out_shape=[jax.ShapeDtypeStruct((T, D), F32), jax.ShapeDtypeStruct((T, D), BF),
                   jax.ShapeDtypeStruct((NCH, HEADS, HK, HK), F32)],
        scratch_shapes=[pltpu.VMEM((HEADS, HK, HK), F32)],
        compiler_params=_params(("arbitrary",)),
    )(z, z, z, z, lbl, nw)


def _hgrn_bwd(z, lbl, nw, oraw, dog, shist):
    hstep = 1

    def body(hq_ref, hf_ref, hi_ref, hg_ref, lbl_ref, nw_ref, oraw_ref, dog_ref, sh_ref,
             dz_ref, dlb_ref, dnw_ref, dst_ref):
        @pl.when(pl.program_id(0) == 0)
        def _():
            dst_ref[...] = jnp.zeros_like(dst_ref)
            dlb_ref[...] = jnp.zeros_like(dlb_ref)
            dnw_ref[...] = jnp.zeros_like(dnw_ref)

        lb_all = _lower_bound(lbl_ref[...])
        rows = lax.broadcasted_iota(jnp.int32, (CH, HK), 0)
        rowc = lax.broadcasted_iota(jnp.int32, (CH, 1), 0)
        rows8 = lax.broadcasted_iota(jnp.int32, (SUB, HK), 0)
        rowc8 = lax.broadcasted_iota(jnp.int32, (SUB, 1), 0)
        nwv = nw_ref[...]
        dnw = jnp.zeros((1, HK), F32)
        for cc, h in [(cc, h) for cc in reversed(range(hstep)) for h in range(HEADS)]:
            rs = slice(CH * cc, CH * (cc + 1))
            sl = slice(HK * h, HK * (h + 1))
            lb = lb_all[:, sl]
            hq, hf, v, hg = hq_ref[rs, sl], hf_ref[rs, sl], hi_ref[rs, sl], hg_ref[rs, sl]
            o, dg_out = oraw_ref[rs, sl], dog_ref[rs, sl]
            sg = _sigmoid(hg)
            sil = hg * sg
            r = lax.rsqrt(jnp.mean(o * o, axis=1, keepdims=True) + EPS)
            nrm = o * r
            d_hg = dg_out * (nrm * nwv) * (sg * (1.0 + hg * (1.0 - sg)))
            dn = dg_out * nwv * sil
            dnw = dnw + jnp.sum(dg_out * nrm * sil, axis=0, keepdims=True)
            do = r * (dn - nrm * jnp.mean(dn * nrm, axis=1, keepdims=True))
            sq = _sigmoid(hq)
            q = hq * sq
            sig = _sigmoid(hf)
            f = lb + (1.0 - lb) * sig
            k = 1.0 - f
            b = _cumsum_rows(jnp.log(f), rows)
            eb = jnp.exp(b)
            qe = q * eb
            bl = b[CH - 1:CH]
            ebl = jnp.exp(bl)
            kdec = jnp.exp(bl - b)
            ke = k * kdec
            dqe = _bdot(do, sh_ref[cc, h])
            dq = dqe * eb
            db = dqe * qe
            dke = _bdot(v, dst_ref[h])
            dv = _bdot(ke, dst_ref[h], _NT)
            dk = dke * kdec
            rr = dke * ke
            db = db - rr
            db_last = (jnp.sum(rr, axis=0, keepdims=True)
                       + ebl * jnp.sum(dst_ref[h] * sh_ref[cc, h], axis=0, keepdims=True))
            dst_ref[h] = dst_ref[h] * ebl
            dst_ref[h] += _bdot(do, qe, _TN)
            dq_i, dk_i, dv_i, db_i = _intra_bwd(q, k, v, b, do, rows8, rowc8)
            dq, dk, dv = dq + dq_i, dk + dk_i, dv + dv_i
            db = db + db_i + (rowc == CH - 1).astype(F32) * db_last
            dgl = _rev_cumsum_rows(db, rows)
            df = dgl / f - dk
            dlb_ref[:, sl] += jnp.sum(df * (1.0 - sig), axis=0, keepdims=True)
            dz_ref[rs, sl] = (dq * (sq * (1.0 + hq * (1.0 - sq)))).astype(BF)
            dz_ref[rs, D + HK * h:D + HK * (h + 1)] = (df * (1.0 - lb) * sig * (1.0 - sig)).astype(BF)
            dz_ref[rs, 2 * D + HK * h:2 * D + HK * (h + 1)] = dv.astype(BF)
            dz_ref[rs, 3 * D + HK * h:3 * D + HK * (h + 1)] = d_hg.astype(BF)
        dnw_ref[...] += dnw

    rev = lambda i: NCH // hstep - 1 - i
    zblk = lambda c: pl.BlockSpec((CH * hstep, D), lambda i, c=c: (rev(i), c))
    return pl.pallas_call(
        body, name="hgrn_bwd", grid=(NCH // hstep,),
        in_specs=[zblk(0), zblk(1), zblk(2), zblk(3),
                  pl.BlockSpec((2, D), lambda i: (0, 0)), pl.BlockSpec((1, HK), lambda i: (0, 0)),
                  zblk(0), zblk(0),
                  pl.BlockSpec((hstep, HEADS, HK, HK), lambda i: (rev(i), 0, 0, 0))],
        out_specs=[pl.BlockSpec((CH * hstep, 4 * D), lambda i: (rev(i), 0)),
                   pl.BlockSpec((1, D), lambda i: (0, 0)), pl.BlockSpec((1, HK), lambda i: (0, 0))],
        out_shape=[jax.ShapeDtypeStruct((T, 4 * D), BF), jax.ShapeDtypeStruct((1, D), F32),
                   jax.ShapeDtypeStruct((1, HK), F32)],
        scratch_shapes=[pltpu.VMEM((HEADS, HK, HK), F32)],
        compiler_params=_params(("arbitrary",)),
    )(z, z, z, z, lbl, nw, oraw, dog, shist)


BLK = 128
NBLK = T // BLK
QK_SCALE = 0.125


def _head_masks():
    lane = lax.broadcasted_iota(jnp.int32, (1, BLK), 1)
    return [(lane < 64).astype(F32), (lane >= 64).astype(F32)]


def _pieces(dil):
    m = T // dil
    out = []
    for r in range(dil):
        for j in range(m // BLK):
            start = r + dil * BLK * j
            rows = pl.ds(start, BLK, stride=dil) if dil > 1 else pl.ds(start, BLK)
            out.append((rows, r * m + BLK * j))
    return out


def _rope(x, c, sa, sb):
    return x * c + pltpu.roll(x, 96, axis=1) * sa + pltpu.roll(x, 32, axis=1) * sb


def _rope_t(d, c, sa, sb):
    return d * c + pltpu.roll(d * sa, 32, axis=1) + pltpu.roll(d * sb, 96, axis=1)


def _rope_and_regroup(dil, q_ref, k_ref, v_ref, tables, stage_q, stage_k, qr_ref, kr_ref, vr_ref):
    cos_ref, sa_ref, sb_ref = tables
    to_q, to_k = (qr_ref, kr_ref) if dil == 1 else (stage_q, stage_k)
    for c in range(T // BLK):
        rows = pl.ds(BLK * c, BLK)
        cs, sa, sb = cos_ref[rows, :], sa_ref[rows, :], sb_ref[rows, :]
        to_q[rows, :] = (_rope(q_ref[rows, :], cs, sa, sb) * QK_SCALE).astype(to_q.dtype)
        to_k[rows, :] = _rope(k_ref[rows, :], cs, sa, sb).astype(to_k.dtype)
    for rows, dst in _pieces(dil):
        drows = pl.ds(dst, BLK)
        if dil > 1:
            qr_ref[drows, :] = stage_q[rows, :].astype(qr_ref.dtype)
            kr_ref[drows, :] = stage_k[rows, :].astype(kr_ref.dtype)
        vr_ref[drows, :] = v_ref[rows, :].astype(vr_ref.dtype)


def _window_bias(bias_ref):
    ii = lax.broadcasted_iota(jnp.int32, (2 * BLK, BLK), 0) % BLK
    jj = lax.broadcasted_iota(jnp.int32, (2 * BLK, BLK), 1)
    bias_ref[0] = jnp.where(jj <= ii, 0.0, -jnp.inf)
    bias_ref[1] = jnp.where(jj >= ii, 0.0, -jnp.inf)


def _blocks(bi):
    if isinstance(bi, int):
        return pl.ds(bi * BLK, BLK), pl.ds(max(bi - 1, 0) * BLK, BLK)
    return (pl.ds(pl.multiple_of(bi * BLK, BLK), BLK),
            pl.ds(pl.multiple_of(jnp.maximum(bi - 1, 0) * BLK, BLK), BLK))


def _stack_heads(x, masks):
    return jnp.concatenate([x * masks[0].astype(x.dtype), x * masks[1].astype(x.dtype)], axis=0).astype(BF)


def _side_steps(side, refs, **when):
    if side is None:
        return
    for stage, cond in when.items():
        if getattr(side, stage) is not None:
            pl.when(cond)(functools.partial(getattr(side, stage), *refs))


def _attn_fwd(z, cos, sa, sb, side=None):
    s_arrays, s_in_specs, s_shapes, s_out_specs, s_sems = _side_io(side)
    na, no = len(s_arrays), len(s_shapes)

    def body(q_ref, k_ref, v_ref, ag_ref, cos_ref, sa_ref, sb_ref, *refs):
        ob_ref, opre_ref, lse_ref, qr_ref, kr_ref, vr_ref = refs[na:na + 6]
        bias_ref, og_ref, lg_ref, otok_ref, ltok_ref, sc_ref = refs[na + 6 + no:na + 12 + no]
        s_refs = (refs[:na], refs[na + 6:na + 6 + no], refs[na + 12 + no:])
        p, g = pl.program_id(0), pl.program_id(1)
        _side_steps(side, s_refs, first=(p == 0) & (g == 0), mid=(p == 1) & (g == 0))
        masks = _head_masks()

        @pl.when(g == 0)
        def _():
            _window_bias(bias_ref)

        def group(gi):
            dil = ATT_GROUPS[gi][1]
            nblk = (T // dil) // BLK
            _rope_and_regroup(dil, q_ref, k_ref, v_ref, (cos_ref, sa_ref, sb_ref), lg_ref.at[0], lg_ref.at[1],
                              qr_ref, kr_ref, vr_ref)

            def scores(bi, slot):
                cur, prev = _blocks(bi)
                q2 = _stack_heads(qr_ref[cur, :], masks)
                sc_ref[slot, 0] = _dot(q2, kr_ref[cur, :], _NT) + bias_ref[0]
                if nblk > 1:
                    sc_ref[slot, 1] = (_dot(q2, kr_ref[prev, :], _NT)
                                       + (bias_ref[1] + jnp.where((bi % nblk) != 0, 0.0, -jnp.inf)))

            def finish(bi, slot):
                cur, prev = _blocks(bi)
                s_c, vc = sc_ref[slot, 0], vr_ref[cur, :]
                if nblk > 1:
                    s_p, vp = sc_ref[slot, 1], vr_ref[prev, :]
                    mx = jnp.max(jnp.maximum(s_c, s_p), axis=1, keepdims=True)
                    p_c, p_p = jnp.exp(s_c - mx), jnp.exp(s_p - mx)
                    den = jnp.sum(p_c + p_p, axis=1, keepdims=True)
                    oh = _dot(p_c.astype(BF), vc) + _dot(p_p.astype(BF), vp)
                else:
                    mx = jnp.max(s_c, axis=1, keepdims=True)
                    p_c = jnp.exp(s_c - mx)
                    den = jnp.sum(p_c, axis=1, keepdims=True)
                    oh = _dot(p_c.astype(BF), vc)
                on = oh / den
                lsev = jnp.broadcast_to(mx + jnp.log(den), (2 * BLK, BLK))
                og_ref[cur, :] = on[:BLK] * masks[0] + on[BLK:] * masks[1]
                lg_ref[0, cur, :] = lsev[:BLK]
                lg_ref[1, cur, :] = lsev[BLK:]

            def pair(j, carry):
                finish(2 * j, 0)
                scores(2 * j + 1, 1)
                finish(2 * j + 1, 1)
                scores(jnp.minimum(2 * j + 2, NBLK - 1), 0)
                return carry

            scores(0, 0)
            lax.fori_loop(0, NBLK // 2, pair, 0)
            for rows, src in _pieces(dil):
                srows = pl.ds(src, BLK)
                otok_ref[gi, rows, :] = og_ref[srows, :]
                ltok_ref[gi, 0, rows, :] = lg_ref[0, srows, :]
                ltok_ref[gi, 1, rows, :] = lg_ref[1, srows, :]

        for gi in range(3):
            pl.when(g == gi)(functools.partial(group, gi))

        @pl.when(g == 2)
        def _():
            for c in range(T // BLK):
                rows = pl.ds(BLK * c, BLK)
                wts = []
                for hh in range(2):
                    l0, l1, l2 = ltok_ref[0, hh, rows, :], ltok_ref[1, hh, rows, :], ltok_ref[2, hh, rows, :]
                    mx = jnp.maximum(jnp.maximum(l0, l1), l2)
                    e0, e1, e2 = jnp.exp(l0 - mx), jnp.exp(l1 - mx), jnp.exp(l2 - mx)
                    tot = e0 + e1 + e2
                    lse_ref[rows, BLK * hh:BLK * (hh + 1)] = mx + jnp.log(tot)
                    inv = 1.0 / tot
                    wts.append([e0 * inv, e1 * inv, e2 * inv])
                o = sum((wts[0][gi] * masks[0] + wts[1][gi] * masks[1]) * otok_ref[gi, rows, :] for gi in range(3))
                ag = ag_ref[rows, :]
                opre_ref[rows, :] = o
                ob_ref[rows, :] = (o * (ag * _sigmoid(ag))).astype(BF)

        _side_steps(side, s_refs, last=(p == 3) & (g == 2))

    c0 = ATT_COL0 // BLK
    zspec = lambda part: pl.BlockSpec((T, BLK), lambda p, g, part=part: (0, c0 + 12 * part + 4 * g + p))
    outspec = pl.BlockSpec((T, BLK), lambda p, g: (0, p))
    table = pl.BlockSpec((T, BLK), lambda p, g: (0, 0))
    regrouped = pl.BlockSpec((None, T, BLK), lambda p, g: (g, 0, p))
    big = lambda: pltpu.VMEM((T, BLK), F32)
    out = pl.pallas_call(
        body, name="attn_fwd", grid=(4, 3),
        in_specs=[zspec(0), zspec(1), zspec(2),
                  pl.BlockSpec((T, BLK), lambda p, g: (0, AG_COL0 // BLK + p)), table, table, table] + s_in_specs,
        out_specs=[outspec, outspec, pl.BlockSpec((T, 2 * BLK), lambda p, g: (0, p)), regrouped, regrouped, regrouped]
                  + s_out_specs,
        out_shape=[jax.ShapeDtypeStruct((T, 512), BF), jax.ShapeDtypeStruct((T, 512), F32),
                   jax.ShapeDtypeStruct((T, 8 * BLK), F32)] + [jax.ShapeDtypeStruct((3, T, 512), BF)] * 3 + s_shapes,
        scratch_shapes=[pltpu.VMEM((2, 2 * BLK, BLK), F32), big(),
                        pltpu.VMEM((2, T, BLK), F32), pltpu.VMEM((3, T, BLK), F32), pltpu.VMEM((3, 2, T, BLK), F32),
                        pltpu.VMEM((2, 2, 2 * BLK, BLK), F32)] + s_sems,
        compiler_params=_params(("parallel" if side is None else "arbitrary", "arbitrary")),
    )(z, z, z, z, cos, sa, sb, *s_arrays)
    return (*out[:6], out[6:])


def _attn_bwd(z, qs, ks, vs, cos, sa, sb, opre, lse, dob, side=None):
    s_arrays, s_in_specs, s_shapes, s_out_specs, s_sems = _side_io(side)
    na, no = len(s_arrays), len(s_shapes)

    def body(qs_ref, ks_ref, vs_ref, ag_ref, cos_ref, sa_ref, sb_ref, o_ref, lse0_ref, lse1_ref, dob_ref, *refs):
        dq_ref, dk_ref, dv_ref, dag_ref = refs[na:na + 4]
        (bias_ref, dtok_ref, qr_ref, kr_ref, vr_ref, dor_ref, lr_ref, dr_ref,
         dqr_ref, dkr_ref, dvr_ref, pd_ref, dotok_ref) = refs[na + 4 + no:na + 17 + no]
        s_refs = (refs[:na], refs[na + 4:na + 4 + no], refs[na + 17 + no:])
        p, g = pl.program_id(0), pl.program_id(1)
        _side_steps(side, s_refs, first=(p == 0) & (g == 0), mid=(p == 1) & (g == 0))
        masks = _head_masks()

        @pl.when(g == 0)
        def _():
            _window_bias(bias_ref)
            for c in range(T // BLK):
                rows = pl.ds(BLK * c, BLK)
                ag, dob_v, o = ag_ref[rows, :], dob_ref[rows, :], o_ref[rows, :]
                sg = _sigmoid(ag)
                dag_ref[rows, :] = (dob_v * o * (sg * (1.0 + ag * (1.0 - sg)))).astype(BF)
                do = dob_v * (ag * sg)
                dotok_ref[rows, :] = do
                prod = do * o
                for hh, mh in enumerate(masks):
                    dtok_ref[hh, rows, :] = jnp.broadcast_to(jnp.sum(prod * mh, axis=1, keepdims=True), (BLK, BLK))

        def group(gi):
            dil = ATT_GROUPS[gi][1]
            nblk = (T // dil) // BLK
            for rows, dst in _pieces(dil):
                drows = pl.ds(dst, BLK)
                dor_ref[drows, :] = dotok_ref[rows, :]
                for hh, lse_ref in enumerate((lse0_ref, lse1_ref)):
                    lr_ref[hh, drows, :] = lse_ref[rows, :]
                    dr_ref[hh, drows, :] = dtok_ref[hh, rows, :]
            dkr_ref[...] = jnp.zeros_like(dkr_ref)
            dvr_ref[...] = jnp.zeros_like(dvr_ref)

            def probs(bi, slot):
                cur, prev = _blocks(bi)
                q2, do2 = _stack_heads(qs_ref[cur, :], masks), _stack_heads(dor_ref[cur, :], masks)
                lh = jnp.concatenate([lr_ref[0, cur, :], lr_ref[1, cur, :]], axis=0)
                dh = jnp.concatenate([dr_ref[0, cur, :], dr_ref[1, cur, :]], axis=0)
                p_c = jnp.exp(_dot(q2, ks_ref[cur, :], _NT) + bias_ref[0] - lh)
                pd_ref[slot, 0] = p_c.astype(BF)
                pd_ref[slot, 1] = (p_c * (_dot(do2, vs_ref[cur, :], _NT) - dh)).astype(BF)
                if nblk > 1:
                    bias_p = bias_ref[1] + jnp.where((bi % nblk) != 0, 0.0, -jnp.inf)
                    p_p = jnp.exp(_dot(q2, ks_ref[prev, :], _NT) + bias_p - lh)
                    pd_ref[slot, 2] = p_p.astype(BF)
                    pd_ref[slot, 3] = (p_p * (_dot(do2, vs_ref[prev, :], _NT) - dh)).astype(BF)

            def grads(bi, slot):
                cur, prev = _blocks(bi)
                q2, do2 = _stack_heads(qs_ref[cur, :], masks), _stack_heads(dor_ref[cur, :], masks)
                p_c, ds_c = pd_ref[slot, 0], pd_ref[slot, 1]
                dq2 = _dot(ds_c, ks_ref[cur, :])
                dkr_ref[cur, :] += _dot(ds_c, q2, _TN)
                dvr_ref[cur, :] += _dot(p_c, do2, _TN)
                if nblk > 1:
                    p_p, ds_p = pd_ref[slot, 2], pd_ref[slot, 3]
                    dq2 = dq2 + _dot(ds_p, ks_ref[prev, :])
                    dkr_ref[prev, :] += _dot(ds_p, q2, _TN)
                    dvr_ref[prev, :] += _dot(p_p, do2, _TN)
                dqr_ref[cur, :] = dq2[:BLK] * masks[0] + dq2[BLK:] * masks[1]

            def pair(j, carry):
                grads(2 * j, 0)
                probs(2 * j + 1, 1)
                grads(2 * j + 1, 1)
                probs(jnp.minimum(2 * j + 2, NBLK - 1), 0)
                return carry

            probs(0, 0)
            lax.fori_loop(0, NBLK // 2, pair, 0)
            if dil > 1:
                for rows, src in _pieces(dil):
                    srows = pl.ds(src, BLK)
                    qr_ref[rows, :] = dqr_ref[srows, :]
                    kr_ref[rows, :] = dkr_ref[srows, :]
                    vr_ref[rows, :] = dvr_ref[srows, :]
            tq, tk, tv = (qr_ref, kr_ref, vr_ref) if dil > 1 else (dqr_ref, dkr_ref, dvr_ref)
            for c in range(T // BLK):
                rows = pl.ds(BLK * c, BLK)
                cs, sa, sb = cos_ref[rows, :], sa_ref[rows, :], sb_ref[rows, :]
                dq_ref[rows, :] = _rope_t(tq[rows, :] * QK_SCALE, cs, sa, sb).astype(BF)
                dk_ref[rows, :] = _rope_t(tk[rows, :], cs, sa, sb).astype(BF)
                dv_ref[rows, :] = tv[rows, :].astype(BF)

        for gi in range(3):
            pl.when(g == gi)(functools.partial(group, gi))
        _side_steps(side, s_refs, last=(p == 3) & (g == 2))

    regrouped = pl.BlockSpec((None, T, BLK), lambda p, g: (g, 0, p))
    pspec = pl.BlockSpec((T, BLK), lambda p, g: (0, p))
    gspec = pl.BlockSpec((T, BLK), lambda p, g: (0, 4 * g + p))
    table = pl.BlockSpec((T, BLK), lambda p, g: (0, 0))
    big = lambda: pltpu.VMEM((T, BLK), F32)
    two = lambda: pltpu.VMEM((2, T, BLK), F32)
    out = pl.pallas_call(
        body, name="attn_bwd", grid=(4, 3),
        in_specs=[regrouped, regrouped, regrouped,
                  pl.BlockSpec((T, BLK), lambda p, g: (0, AG_COL0 // BLK + p)), table, table, table,
                  pspec, pl.BlockSpec((T, BLK), lambda p, g: (0, 2 * p)),
                  pl.BlockSpec((T, BLK), lambda p, g: (0, 2 * p + 1)), pspec] + s_in_specs,
        out_specs=[gspec, gspec, gspec, pspec] + s_out_specs,
        out_shape=[jax.ShapeDtypeStruct((T, 1536), BF), jax.ShapeDtypeStruct((T, 1536), BF),
                   jax.ShapeDtypeStruct((T, 1536), BF), jax.ShapeDtypeStruct((T, 512), BF)] + s_shapes,
        scratch_shapes=[pltpu.VMEM((2, 2 * BLK, BLK), F32), two(), big(), big(), big(), big(),
                        two(), two(), big(), big(), big(), pltpu.VMEM((2, 4, 2 * BLK, BLK), BF), big()] + s_sems,
        compiler_params=_params(("parallel" if side is None else "arbitrary", "arbitrary")),
    )(qs, ks, vs, z, cos, sa, sb, opre, lse, lse, dob, *s_arrays)
    return (*out[:4], out[4:])


def _merge_out_loss(og, ob, z, w_a, w_b, w_out, x, tgt, wf):
    tm = 512

    def body(og_ref, ob_ref, ga_ref, gb_ref, wa_ref, wb_ref, wo_ref, x_ref, t_ref, wf_ref,
             m_ref, dout_ref, loss_ref, gwf_ref):
        @pl.when(pl.program_id(0) == 0)
        def _():
            loss_ref[...] = jnp.zeros_like(loss_ref)
            gwf_ref[...] = jnp.zeros_like(gwf_ref)

        ya, yb = _dot(og_ref[...], wa_ref[...]), _dot(ob_ref[...], wb_ref[...])
        m =(_sigmoid(ga_ref[...]) * ya + _sigmoid(gb_ref[...]) * yb).astype(BF)
        m_ref[...] = m
        out = x_ref[...] + _dot(m, wo_ref[...])
        r = lax.rsqrt(jnp.mean(out * out, axis=-1, keepdims=True) + EPS)
        yh = out * r
        wfv = wf_ref[...]
        err = yh * wfv - t_ref[...]
        loss_ref[...] += jnp.sum(err * err, axis=0, keepdims=True) * (0.5 / D)
        dy = err * (1.0 / D)
        gwf_ref[...] += jnp.sum(dy * yh, axis=0, keepdims=True)
        dyh = dy * wfv
        dout_ref[...] = r * (dyh - yh * jnp.mean(dyh * yh, axis=-1, keepdims=True))

    row = pl.BlockSpec((tm, D), lambda i: (i, 0))
    vec = pl.BlockSpec((1, D), lambda i: (0, 0))
    whole = lambda w: pl.BlockSpec(w.shape, lambda i: (0, 0))
    return pl.pallas_call(
        body, name="merge_out_loss", grid=(T // tm,),
        in_specs=[row, pl.BlockSpec((tm, ob.shape[1]), lambda i: (i, 0)),
                  pl.BlockSpec((tm, D), lambda i: (i, GATE_COL0 // D)),
                  pl.BlockSpec((tm, D), lambda i: (i, GATE_COL0 // D + 1)),
                  whole(w_a), whole(w_b), whole(w_out), row, row, vec],
        out_specs=[row, row, vec, vec],
        out_shape=[jax.ShapeDtypeStruct((T, D), BF), jax.ShapeDtypeStruct((T, D), F32),
                   jax.ShapeDtypeStruct((1, D), F32), jax.ShapeDtypeStruct((1, D), F32)],
        compiler_params=_params(("arbitrary",)),
    )(og, ob, z, z, w_a, w_b, w_out, x, tgt, wf)


def _merge_proj_bwd(dout, og, ob, z, w_a, w_b, w_out):
    tm = 512

    def body(dout_ref, og_ref, ob_ref, ga_ref, gb_ref, wa_ref, wb_ref, wo_ref,
             dya_ref, dyb_ref, dg_ref, dog_ref, dob_ref):
        dmv = _dot(dout_ref[...].astype(BF), wo_ref[...], _NT)
        sa, sb = _sigmoid(ga_ref[...]), _sigmoid(gb_ref[...])
        dya, dyb = (sa * dmv).astype(BF), (sb * dmv).astype(BF)
        dya_ref[...] = dya
        dyb_ref[...] = dyb
        dg_ref[:, :D] = (dmv * _dot(og_ref[...], wa_ref[...]) * sa * (1.0 - sa)).astype(BF)
        dg_ref[:, D:] = (dmv * _dot(ob_ref[...], wb_ref[...]) * sb * (1.0 - sb)).astype(BF)
        dog_ref[...] = _dot(dya, wa_ref[...], _NT)
        dob_ref[...] = _dot(dyb, wb_ref[...], _NT)

    row = pl.BlockSpec((tm, D), lambda i: (i, 0))
    whole = lambda w: pl.BlockSpec(w.shape, lambda i: (0, 0))
    nb = w_b.shape[0]
    return pl.pallas_call(
        body, name="merge_proj_bwd", grid=(T // tm,),
        in_specs=[row, row, pl.BlockSpec((tm, nb), lambda i: (i, 0)),
                  pl.BlockSpec((tm, D), lambda i: (i, GATE_COL0 // D)),
                  pl.BlockSpec((tm, D), lambda i: (i, GATE_COL0 // D + 1)), whole(w_a), whole(w_b), whole(w_out)],
        out_specs=[row, row, pl.BlockSpec((tm, 2 * D), lambda i: (i, 0)), row,
                   pl.BlockSpec((tm, nb), lambda i: (i, 0))],
        out_shape=[jax.ShapeDtypeStruct((T, D), BF), jax.ShapeDtypeStruct((T, D), BF),
                   jax.ShapeDtypeStruct((T, 2 * D), BF), jax.ShapeDtypeStruct((T, D), F32),
                   jax.ShapeDtypeStruct((T, nb), F32)],
        compiler_params=_params(("parallel",)),
    )(dout, og, ob, z, z, w_a, w_b, w_out)


def _rope_inv_freq():
    inv = ROPE_THETA ** (-jnp.arange(0, 64, 2, dtype=F32) / 64)
    return jnp.tile(inv, 4).reshape(1, BLK)


def _local_step(x, pos, norm_w, lbl, hnw, wf, tgt, w_in, w_a, w_b, w_out, shard_shapes=()):
    invf = _rope_inv_freq()
    if shard_shapes:
        blk = jnp.reshape(2 * lax.axis_index("x") + lax.axis_index("y"), (1,)).astype(jnp.int32)
        h, cos, sa, sb, z_own, (w_a, w_b, w_out), (w_near,) = _norm_and_rope_tables(
            x, norm_w, pos, invf, side=_gather_near_side(w_in, WEIGHT_AXES[0]), own=(w_in, blk),
            cast=(w_a, w_b, w_out))
        near = jnp.concatenate([blk ^ 2, blk ^ 1])
        z, (w_diag,) = _z_blocks(h, w_near, z_own, jnp.concatenate([near, near]), 2, name="z_proj_near",
                                 side=_gather_diag_side(w_near, w_in.shape, WEIGHT_AXES[0]))
        z, w_in, _ = _z_blocks(h, w_diag, z, jnp.concatenate([blk ^ 3, jnp.zeros_like(blk)]), 1, name="z_proj_diag",
                               fill=w_near, side=None)
    else:
        h, cos, sa, sb, _, _, _ = _norm_and_rope_tables(x, norm_w, pos, invf)
        z = _matmul(h, w_in, tm=T, tn=512, name="z_proj")
    oraw, og, shist = _hgrn_fwd(z, lbl, hnw)
    side_a = _gather_side([w_a, w_b, w_out], WEIGHT_AXES[1:]) if shard_shapes else None
    ob, opre, lse, qs, ks, vs, gathered = _attn_fwd(z, cos, sa, sb, side=side_a)
    if shard_shapes:
        w_a, w_b, w_out = gathered
    merged, dout, loss_vec, g_wf = _merge_out_loss(og, ob, z, w_a, w_b, w_out, x, tgt, wf)

    dya, dyb, dgates, dog, dob = _merge_proj_bwd(dout, og, ob, z, w_a, w_b, w_out)
    g_wout = _matmul(merged, dout, ta=True, out_dtype=BF, tm=512, tn=1024, name="g_wout")
    g_wa = _matmul(og, dya, ta=True, out_dtype=BF, tm=512, tn=1024, name="g_wa")
    g_wb = _matmul(ob, dyb, ta=True, out_dtype=BF, tm=512, tn=1024, name="g_wb")
    small = [g_wa, g_wb, g_wout]
    side_s = side_w = None
    if shard_shapes:
        p3_s = [_as3d(g, s, ax) for g, s, ax in zip(small, shard_shapes[1:], WEIGHT_AXES[1:])]
        side_s = _chip_exchange_direct_side(p3_s, shard_shapes[1:], WEIGHT_AXES[1:])
    dz_h, dlb, g_hnw = _hgrn_bwd(z, lbl, hnw, oraw, dog, shist)
    dq, dk, dv, dag, land_s = _attn_bwd(z, qs, ks, vs, cos, sa, sb, opre, lse, dob, side=side_s)
    dz_parts = [dz_h, dq, dk, dv, dag, dgates]
    if shard_shapes:
        c = lax.axis_index("c")
        half = lambda i: jnp.reshape(i, (1,)).astype(jnp.int32)
        g_send = _grad_w_in_half(h, dz_parts, half(1 - c))
        g_keep, (g_sib,) = _grad_w_in_half(h, dz_parts, half(c), side=_sibling_send_side(g_send))
        p3_w = [_add_bf16(g_keep, g_sib, "pair_sum_w_in").reshape(1, D // 2, NIN)]
        side_w = _chip_exchange_relay_side(p3_w[0], shard_shapes[0])
    else:
        g_big = [_grad_w_in(h, dz_parts)] + small
    gx, g_nw, land_w = _grad_x(dz_parts, w_in, x, dout, norm_w, side=side_w)
    small_sums = None
    if shard_shapes:
        g_big, small_sums = _rs_finish(p3_w + p3_s, [land_w[0]] + list(land_s), shard_shapes, WEIGHT_AXES,
                                       (g_nw, dlb, g_hnw, g_wf, loss_vec))
    return dict(loss_vec=loss_vec, gx=gx, g_nw=g_nw, dlb=dlb, g_hnw=g_hnw, g_wf=g_wf, small_sums=small_sums,
                g_win=g_big[0], g_wa=g_big[1], g_wb=g_big[2], g_wout=g_big[3])


MESH = pl.DeviceIdType.MESH
HBM = pl.BlockSpec(memory_space=pl.ANY)
WEIGHT_AXES = (1, 0, 1, 0)


def _place():
    x, y, c = lax.axis_index("x"), lax.axis_index("y"), lax.axis_index("c")
    chips = [(1 - x, y), (x, 1 - y), (1 - x, 1 - y)]
    return x, y, c, chips


def _block_half(ref, shard_shape, axis, j, half):
    r, c = shard_shape
    hr = r // 2
    if axis == 0:
        return ref.at[pl.ds(pl.multiple_of(j * r + half * hr, 16), hr), :]
    return ref.at[pl.ds(pl.multiple_of(half * hr, 16), hr), pl.ds(pl.multiple_of(j * c, 128), c)]


PIECES = 4


def _block_piece(ref, shard_shape, axis, j, half, q):
    r, c = shard_shape
    pr = r // 2 // PIECES
    if axis == 0:
        return ref.at[pl.ds(pl.multiple_of(j * r + half * (r // 2) + q * pr, 16), pr), :]
    return ref.at[pl.ds(pl.multiple_of(half * (r // 2) + q * pr, 16), pr), pl.ds(pl.multiple_of(j * c, 128), c)]


class _Side:
    def __init__(self, arrays, out_shapes, sems, first, last, mid=None):
        self.arrays, self.out_shapes, self.sems, self.first, self.last = arrays, out_shapes, sems, first, last
        self.mid = mid


def _gather_side(shards, axes):
    n = len(shards)
    shapes = [s.shape for s in shards]

    def copies(ins, outs, sems):
        send1, recv1, send2, recv2, send0, recv0 = sems
        x, y, c, chips = _place()
        me = 2 * x + y
        sib = (x, y, 1 - c)
        near = ((1 - c) * (1 - x) + c * x, (1 - c) * y + c * (1 - y))
        far = ((1 - c) * x + c * (1 - x), (1 - c) * (1 - y) + c * y)
        out = []
        for a in range(n):
            r, cc = shapes[a]
            mine = (outs[a].at[pl.ds(pl.multiple_of(me * r, 16), r), :] if axes[a] == 0
                    else outs[a].at[:, pl.ds(pl.multiple_of(me * cc, 128), cc)])
            own = pltpu.make_async_remote_copy(
                src_ref=ins[a], dst_ref=mine, send_sem=send0.at[a], recv_sem=recv0.at[a],
                device_id=sib, device_id_type=MESH)
            src = ins[a].at[pl.ds(pl.multiple_of(c * (r // 2), 16), r // 2), :]
            sends = [pltpu.make_async_remote_copy(
                src_ref=src, dst_ref=_block_half(outs[a], shapes[a], axes[a], me, c),
                send_sem=send1.at[a, k], recv_sem=recv1.at[a, k], device_id=(*chips[k], c), device_id_type=MESH)
                for k in range(2)]

            def region(chip, half):
                return _block_half(outs[a], shapes[a], axes[a], 2 * chip[0] + chip[1], half)

            def arrival(chip, k):
                reg = region(chip, c)
                return pltpu.make_async_remote_copy(
                    src_ref=reg, dst_ref=reg, send_sem=send1.at[a, k], recv_sem=recv1.at[a, k],
                    device_id=(*chip, c), device_id_type=MESH)

            def to_sibling(chip, k):
                reg = region(chip, c)
                return pltpu.make_async_remote_copy(
                    src_ref=reg, dst_ref=reg, send_sem=send2.at[a, k], recv_sem=recv2.at[a, k],
                    device_id=sib, device_id_type=MESH)

            def from_sibling(chip, k):
                reg = region(chip, 1 - c)
                return pltpu.make_async_remote_copy(
                    src_ref=reg, dst_ref=reg, send_sem=send2.at[a, k], recv_sem=recv2.at[a, k],
                    device_id=sib, device_id_type=MESH)

            relay = pltpu.make_async_remote_copy(
                src_ref=region(near, c), dst_ref=region(near, c), send_sem=send1.at[a, 2], recv_sem=recv1.at[a, 2],
                device_id=(*far, c), device_id_type=MESH)
            hops = [(arrival(near, c), to_sibling(near, c)), (arrival(far, 1 - c), to_sibling(far, 1 - c)),
                    (arrival(chips[2], 2), to_sibling(chips[2], 2))]
            back = [from_sibling(chips[k], k) for k in range(3)]
            out.append((own, sends, relay, hops, back))
        return out

    def first(ins, outs, sems):
        for own, sends, _, _, _ in copies(ins, outs, sems):
            own.start()
            for cp in sends:
                cp.start()

    def mid(ins, outs, sems):
        per_array = copies(ins, outs, sems)
        for step in range(2):
            for _, _, relay, hops, _ in per_array:
                arrived, onward = hops[step]
                arrived.wait_recv()
                if step == 0:
                    relay.start()
                onward.start()

    def last(ins, outs, sems):
        per_array = copies(ins, outs, sems)
        for _, _, _, hops, _ in per_array:
            arrived, onward = hops[2]
            arrived.wait_recv()
            onward.start()
        for own, sends, relay, hops, back in per_array:
            for cp in back:
                cp.wait_recv()
            for cp in sends + [relay] + [onward for _, onward in hops]:
                cp.wait_send()
            own.wait()

    full = [(4 * r, c) if ax == 0 else (r, 4 * c) for (r, c), ax in zip(shapes, axes)]
    sems = [pltpu.SemaphoreType.DMA((n, 3)), pltpu.SemaphoreType.DMA((n, 3)),
            pltpu.SemaphoreType.DMA((n, 3)), pltpu.SemaphoreType.DMA((n, 3)),
            pltpu.SemaphoreType.DMA((n,)), pltpu.SemaphoreType.DMA((n,))]
    return _Side(list(shards), [jax.ShapeDtypeStruct(f, BF) for f in full], sems, first, last, mid)


def _gather_near_side(shard, axis):
    shape = shard.shape
    r, cc = shape

    def copies(ins, outs, sems):
        send1, recv1, send2, recv2, send0, recv0 = sems
        x, y, c, chips = _place()
        me = 2 * x + y
        sib = (x, y, 1 - c)
        mine = (outs[0].at[pl.ds(pl.multiple_of(me * r, 16), r), :] if axis == 0
                else outs[0].at[:, pl.ds(pl.multiple_of(me * cc, 128), cc)])
        own = pltpu.make_async_remote_copy(
            src_ref=ins[0], dst_ref=mine, send_sem=send0.at[0], recv_sem=recv0.at[0],
            device_id=sib, device_id_type=MESH)
        def region(k, half, q):
            return _block_piece(outs[0], shape, axis, 2 * chips[k][0] + chips[k][1], half, q)

        def moves(k, q):
            src = ins[0].at[pl.ds(pl.multiple_of(c * (r // 2) + q * (r // 2 // PIECES), 16), r // 2 // PIECES), :]
            return [pltpu.make_async_remote_copy(
                        src_ref=s, dst_ref=d, send_sem=ss.at[k, q], recv_sem=rs.at[k, q], device_id=dev,
                        device_id_type=MESH)
                    for s, d, ss, rs, dev in (
                        (src, _block_piece(outs[0], shape, axis, me, c, q), send1, recv1, (*chips[k], c)),
                        (region(k, c, q), region(k, c, q), send1, recv1, (*chips[k], c)),
                        (region(k, c, q), region(k, c, q), send2, recv2, sib),
                        (region(k, 1 - c, q), region(k, 1 - c, q), send2, recv2, sib))]

        return own, [moves(k, q) for q in range(PIECES) for k in range(2)]

    def first(ins, outs, sems):
        own, per_piece = copies(ins, outs, sems)
        for send, _, _, _ in per_piece:
            send.start()
        own.start()

    def last(ins, outs, sems):
        own, per_piece = copies(ins, outs, sems)
        for _, arrived, onward, _ in per_piece:
            arrived.wait_recv()
            onward.start()
        for send, _, onward, back in per_piece:
            back.wait_recv()
            send.wait_send()
            onward.wait_send()
        own.wait()

    full = (4 * r, cc) if axis == 0 else (r, 4 * cc)
    sems = [pltpu.SemaphoreType.DMA((2, PIECES))] * 4 + [pltpu.SemaphoreType.DMA((1,))] * 2
    return _Side([shard], [jax.ShapeDtypeStruct(full, BF)], sems, first, last)


def _gather_diag_side(gathered, shape, axis):
    r, cc = shape

    def copies(ins, outs, sems):
        send1, recv1, send2, recv2 = sems
        x, y, c, _ = _place()
        sib = (x, y, 1 - c)
        near = ((1 - c) * (1 - x) + c * x, (1 - c) * y + c * (1 - y))
        far = ((1 - c) * x + c * (1 - x), (1 - c) * (1 - y) + c * y)

        def piece(i, q):
            return outs[0].at[pl.ds(pl.multiple_of(i * (r // 2) + q * (r // 2 // PIECES), 16), r // 2 // PIECES), :]

        def moves(q):
            def move(s, d, ss, rs, dev):
                return pltpu.make_async_remote_copy(
                    src_ref=s, dst_ref=d, send_sem=ss.at[q], recv_sem=rs.at[q], device_id=dev, device_id_type=MESH)

            return (move(_block_piece(ins[0], shape, axis, 2 * near[0] + near[1], c, q), piece(c, q), send1, recv1,
                         (*far, c)),
                    move(piece(c, q), piece(c, q), send1, recv1, (*far, c)),
                    move(piece(c, q), piece(c, q), send2, recv2, sib),
                    move(piece(1 - c, q), piece(1 - c, q), send2, recv2, sib))

        return [moves(q) for q in range(PIECES)]

    def first(ins, outs, sems):
        for relay, _, _, _ in copies(ins, outs, sems):
            relay.start()

    def last(ins, outs, sems):
        per_piece = copies(ins, outs, sems)
        for _, arrived, onward, _ in per_piece:
            arrived.wait_recv()
            onward.start()
        for relay, _, onward, back in per_piece:
            back.wait_recv()
            relay.wait_send()
            onward.wait_send()

    return _Side([gathered], [jax.ShapeDtypeStruct(shape, BF)], [pltpu.SemaphoreType.DMA((PIECES,))] * 4,
                 first, last)


def _as3d(g, shard_shape, axis):
    r, c = shard_shape
    return g.reshape(4, r, c) if axis == 0 else g.reshape(1, r, 4 * c)


def _chip_exchange_direct_side(g3s, shapes, axes):
    n = len(g3s)

    def copies(ins, outs, sems):
        send, recv = sems
        x, y, c, chips = _place()
        sends, arrivals = [], []
        for a in range(n):
            r, cc = shapes[a]
            hr = r // 2

            def part(j, h, a=a, hr=hr, cc=cc):
                rows = pl.ds(pl.multiple_of(h * hr, 16), hr)
                return (ins[a].at[j, rows, :] if axes[a] == 0
                        else ins[a].at[0, rows, pl.ds(pl.multiple_of(j * cc, 128), cc)])

            def move(s, d, i_send, i_recv, dev, a=a):
                return pltpu.make_async_remote_copy(
                    src_ref=s, dst_ref=d, send_sem=send.at[a, i_send], recv_sem=recv.at[a, i_recv], device_id=dev,
                    device_id_type=MESH)

            for k, (px, py) in enumerate(chips):
                for h in range(2):
                    sends.append(move(part(2 * px + py, h), outs[a].at[2 * k + c], 2 * k + h, 2 * k + c, (px, py, h)))
                    slot = outs[a].at[2 * k + h]
                    arrivals.append(move(slot, slot, 2 * k + h, 2 * k + h, (px, py, h)))
            sends.append(move(part(2 * x + y, 1 - c), outs[a].at[6], 6, 6, (x, y, 1 - c)))
            arrivals.append(move(outs[a].at[6], outs[a].at[6], 6, 6, (x, y, 1 - c)))
        return sends, arrivals

    def first(ins, outs, sems):
        for cp in copies(ins, outs, sems)[0]:
            cp.start()

    def last(ins, outs, sems):
        sends, arrivals = copies(ins, outs, sems)
        for cp in arrivals:
            cp.wait_recv()
        for cp in sends:
            cp.wait_send()

    return _Side(list(g3s), [jax.ShapeDtypeStruct((7, r // 2, c), BF) for r, c in shapes],
                 [pltpu.SemaphoreType.DMA((n, 7)), pltpu.SemaphoreType.DMA((n, 7))], first, last)


def _chip_exchange_relay_side(p3, shape):
    r, cc = shape
    hr = r // 2
    rows = 64

    def copies(ins, outs, sems):
        send, recv, local, mine, theirs = sems
        x, y, c, chips = _place()
        near = ((1 - c) * (1 - x) + c * x, (1 - c) * y + c * (1 - y))
        far = ((1 - c) * x + c * (1 - x), (1 - c) * (1 - y) + c * y)
        land, staged = outs

        def block(chip):
            return ins[0].at[0, :, pl.ds(pl.multiple_of((2 * chip[0] + chip[1]) * cc, 128), cc)]

        def move(s, d, k, dev):
            return pltpu.make_async_remote_copy(
                src_ref=s, dst_ref=d, send_sem=send.at[k], recv_sem=recv.at[k], device_id=dev, device_id_type=MESH)

        return dict(
            direct=move(block(near), land.at[c], 0, (*near, c)),
            for_relay=move(block(chips[2]), staged, 1, (*near, c)),
            summed=move(mine, land.at[1 - c], 2, (*far, c)),
            direct_in=move(land.at[c], land.at[c], 0, (*near, c)),
            staged_in=move(staged, staged, 1, (*near, c)),
            summed_in=move(land.at[1 - c], land.at[1 - c], 2, (*far, c)),
            load_mine=pltpu.make_async_copy(block(far), mine, local.at[0]),
            load_theirs=pltpu.make_async_copy(staged, theirs, local.at[1]))

    def first(ins, outs, sems):
        cps = copies(ins, outs, sems)
        cps["for_relay"].start()
        cps["direct"].start()

    def mid(ins, outs, sems):
        cps = copies(ins, outs, sems)
        mine, theirs = sems[3], sems[4]
        cps["load_mine"].start()
        cps["staged_in"].wait_recv()
        cps["load_theirs"].start()
        cps["load_mine"].wait()
        cps["load_theirs"].wait()

        def add(i, carry):
            rs = pl.ds(pl.multiple_of(i * rows, 16), rows)
            mine[rs, :] = (mine[rs, :].astype(F32) + theirs[rs, :].astype(F32)).astype(BF)
            return carry

        lax.fori_loop(0, hr // rows, add, 0)
        cps["summed"].start()

    def last(ins, outs, sems):
        cps = copies(ins, outs, sems)
        cps["direct_in"].wait_recv()
        cps["summed_in"].wait_recv()
        for name in ("direct", "for_relay", "summed"):
            cps[name].wait_send()

    sems = [pltpu.SemaphoreType.DMA((3,)), pltpu.SemaphoreType.DMA((3,)), pltpu.SemaphoreType.DMA((2,)),
            pltpu.VMEM((hr, cc), BF), pltpu.VMEM((hr, cc), BF)]
    return _Side([p3], [jax.ShapeDtypeStruct((2, hr, cc), BF), jax.ShapeDtypeStruct((hr, cc), BF)], sems,
                 first, last, mid)


def _chip_sum(p3, land, shard_shape, axis, idx, name):
    r, c = shard_shape
    hr = r // 2
    tr = 128
    nt = hr // tr
    slots = land.shape[0]

    def body(idx_ref, p_ref, l_ref, o_ref):
        acc = p_ref[...].astype(F32)
        for k in range(slots):
            acc = acc + l_ref[k].astype(F32)
        o_ref[...] = acc

    own = (pl.BlockSpec((None, tr, c), lambda i, idx: (idx[0], i, 0)) if axis == 0
           else pl.BlockSpec((None, tr, c), lambda i, idx: (0, i, idx[0])))
    return pl.pallas_call(
        body, name=name,
        grid_spec=pltpu.PrefetchScalarGridSpec(
            num_scalar_prefetch=1, grid=(nt,),
            in_specs=[own, pl.BlockSpec((slots, tr, c), lambda i, idx: (0, i, 0))],
            out_specs=pl.BlockSpec((tr, c), lambda i, idx: (idx[1] * nt + i, 0))),
        out_shape=jax.ShapeDtypeStruct((r, c), F32),
        compiler_params=_params(("parallel",)),
    )(idx, p3, land)


def _chip_sums(p3s, lands, shapes, axes, idx, name):
    n = len(p3s)

    def body(idx_ref, *refs):
        for p_ref, l_ref, o_ref in zip(refs[:n], refs[n:2 * n], refs[2 * n:]):
            acc = p_ref[...].astype(F32)
            for k in range(l_ref.shape[0]):
                acc = acc + l_ref[k].astype(F32)
            o_ref[...] = acc

    own = [pl.BlockSpec((None, r // 2, c),
                        (lambda i, idx: (idx[0], idx[1], 0)) if ax == 0 else (lambda i, idx: (0, idx[1], idx[0])))
           for (r, c), ax in zip(shapes, axes)]
    return pl.pallas_call(
        body, name=name,
        grid_spec=pltpu.PrefetchScalarGridSpec(
            num_scalar_prefetch=1, grid=(1,),
            in_specs=own + [pl.BlockSpec(l.shape, lambda i, idx: (0, 0, 0)) for l in lands],
            out_specs=[pl.BlockSpec((r // 2, c), lambda i, idx: (idx[1], 0)) for r, c in shapes]),
        out_shape=[jax.ShapeDtypeStruct((r, c), F32) for r, c in shapes],
        compiler_params=_params(("arbitrary",)),
    )(idx, *p3s, *lands)


def _rs_pair_gather(fulls, small):
    n = len(fulls)

    def body(*refs):
        ins, small_refs, outs, red_ref = refs[:n], refs[n:n + 5], refs[n + 5:2 * n + 5], refs[2 * n + 5]
        send, recv = refs[2 * n + 6:2 * n + 8]
        x, y, c, _ = _place()
        cps = []
        for a in range(n):
            hr = fulls[a].shape[0] // 2
            rows = pl.ds(pl.multiple_of(c * hr, 8), hr)
            cp = pltpu.make_async_remote_copy(
                src_ref=ins[a].at[rows, :], dst_ref=outs[a].at[rows, :], send_sem=send.at[a], recv_sem=recv.at[a],
                device_id=(x, y, 1 - c), device_id_type=MESH)
            cp.start()
            cps.append(cp)
        _small_all_reduce(small_refs, red_ref, *refs[2 * n + 8:])
        for a, cp in enumerate(cps):
            cp.wait_send()
            hr = fulls[a].shape[0] // 2
            other = pl.ds(pl.multiple_of((1 - c) * hr, 8), hr)
            pltpu.make_async_remote_copy(
                src_ref=ins[a].at[other, :], dst_ref=outs[a].at[other, :], send_sem=send.at[a], recv_sem=recv.at[a],
                device_id=(x, y, 1 - c), device_id_type=MESH).wait_recv()

    vm = pl.BlockSpec(memory_space=pltpu.VMEM)
    out = pl.pallas_call(
        body, name="grads_pair_gather",
        in_specs=[HBM] * n + [vm] * 5, out_specs=[HBM] * n + [vm],
        out_shape=[jax.ShapeDtypeStruct(f.shape, F32) for f in fulls] + [jax.ShapeDtypeStruct((NSMALL, D), F32)],
        input_output_aliases={a: a for a in range(n)},
        scratch_shapes=[pltpu.SemaphoreType.DMA((n,)), pltpu.SemaphoreType.DMA((n,)),
                        pltpu.VMEM((NSMALL, D), F32), pltpu.VMEM((8, NSMALL, D), F32),
                        pltpu.SemaphoreType.DMA((7,)), pltpu.SemaphoreType.DMA((7,))],
    )(*fulls, *small)
    return out[:n], out[n]


def _sibling_send_side(arr):
    def copy(ins, outs, sems):
        x, y, c, _ = _place()
        return pltpu.make_async_remote_copy(
            src_ref=ins[0], dst_ref=outs[0], send_sem=sems[0].at[0], recv_sem=sems[1].at[0],
            device_id=(x, y, 1 - c), device_id_type=MESH)

    return _Side([arr], [jax.ShapeDtypeStruct(arr.shape, arr.dtype)],
                 [pltpu.SemaphoreType.DMA((1,)), pltpu.SemaphoreType.DMA((1,))],
                 lambda ins, outs, sems: copy(ins, outs, sems).start(),
                 lambda ins, outs, sems: copy(ins, outs, sems).wait())


def _add_bf16(a, b, name):
    r, c = a.shape
    tr = 128

    def body(a_ref, b_ref, o_ref):
        o_ref[...] = (a_ref[...].astype(F32) + b_ref[...].astype(F32)).astype(BF)

    blk = pl.BlockSpec((tr, c), lambda i: (i, 0))
    return pl.pallas_call(
        body, name=name, grid=(r // tr,), in_specs=[blk, blk], out_specs=blk,
        out_shape=jax.ShapeDtypeStruct((r, c), BF), compiler_params=_params(("parallel",)),
    )(a, b)


def _rs_finish(p3s, landed, shapes, axes, small):
    x, y, c = lax.axis_index("x"), lax.axis_index("y"), lax.axis_index("c")
    idx = jnp.stack([2 * x + y, c]).astype(jnp.int32)
    fulls = [_chip_sum(p3s[0], landed[0], shapes[0], axes[0], idx, "chip_sum_w_in")]
    fulls += _chip_sums(p3s[1:], landed[1:], shapes[1:], axes[1:], idx, "chip_sums_branches_out")
    return _rs_pair_gather(fulls, small)


NSMALL = 8


def _small_all_reduce(small_refs, out_ref, pack_ref, buf_ref, send, recv):
    nw_ref, lb_ref, hn_ref, wf_ref, ls_ref = small_refs
    x, y, c = lax.axis_index("x"), lax.axis_index("y"), lax.axis_index("c")
    me = 4 * x + 2 * y + c
    pack_ref[...] = jnp.zeros_like(pack_ref)
    pack_ref[0:1, :] = nw_ref[...]
    pack_ref[1:2, :] = lb_ref[...]
    pack_ref[2:3, 0:HK] = hn_ref[...]
    pack_ref[3:4, :] = wf_ref[...]
    pack_ref[4:5, :] = ls_ref[...]
    buf_ref[me] = pack_ref[...]
    cps = []
    for d in range(1, 8):
        dx, dy, dc = d >> 2, (d >> 1) & 1, d & 1
        peer = (1 - x if dx else x, 1 - y if dy else y, 1 - c if dc else c)
        cp = pltpu.make_async_remote_copy(
            src_ref=pack_ref, dst_ref=buf_ref.at[me], send_sem=send.at[d - 1], recv_sem=recv.at[d - 1],
            device_id=peer, device_id_type=MESH)
        cp.start()
        cps.append(cp)
    for d in range(1, 8):
        dx, dy, dc = d >> 2, (d >> 1) & 1, d & 1
        src = 4 * (1 - x if dx else x) + 2 * (1 - y if dy else y) + (1 - c if dc else c)
        pltpu.make_async_remote_copy(
            src_ref=pack_ref, dst_ref=buf_ref.at[src], send_sem=send.at[d - 1], recv_sem=recv.at[d - 1],
            device_id=(x, y, c), device_id_type=MESH).wait_recv()
    for cp in cps:
        cp.wait_send()
    acc = buf_ref[0]
    for i in range(1, 8):
        acc = acc + buf_ref[i]
    out_ref[...] = acc


def _adamw_math(w, g, m, v):
    m = B1 * m + (1.0 - B1) * g
    v = B2 * v + (1.0 - B2) * (g * g)
    m_hat = m / (1.0 - B1 ** STEP)
    v_hat = v / (1.0 - B2 ** STEP)
    return -LR * (m_hat / (jnp.sqrt(v_hat) + ADAM_EPS) + WD * w), m, v


def _adamw(w, g, m, v, name):
    r, c = w.shape
    tr = 128

    def body(w_ref, g_ref, m_ref, v_ref, d_ref, nm_ref, nv_ref, go_ref):
        g = g_ref[...]
        d_ref[...], nm_ref[...], nv_ref[...] = _adamw_math(w_ref[...], g, m_ref[...], v_ref[...])
        go_ref[...] = g

    blk = pl.BlockSpec((tr, c), lambda i: (i, 0))
    return pl.pallas_call(
        body, name=name, grid=(r // tr,), in_specs=[blk] * 4, out_specs=[blk] * 4,
        out_shape=[jax.ShapeDtypeStruct((r, c), F32)] * 4,
        compiler_params=_params(("parallel",)),
    )(w, g, m, v)


def _adamw_whole(groups, name):
    n = len(groups)

    def body(*refs):
        ins, outs = refs[:4 * n], refs[4 * n:]
        for a in range(n):
            w_ref, g_ref, m_ref, v_ref = ins[4 * a:4 * a + 4]
            g = g_ref[...]
            outs[4 * a][...], outs[4 * a + 1][...], outs[4 * a + 2][...] = _adamw_math(
                w_ref[...], g, m_ref[...], v_ref[...])
            outs[4 * a + 3][...] = g

    vm = pl.BlockSpec(memory_space=pltpu.VMEM)
    out = pl.pallas_call(
        body, name=name, in_specs=[vm] * (4 * n), out_specs=[vm] * (4 * n),
        out_shape=[jax.ShapeDtypeStruct(grp[0].shape, F32) for grp in groups for _ in range(4)],
        compiler_params=_params(),
    )(*[a for grp in groups for a in grp])
    return [out[4 * a:4 * a + 4] for a in range(n)]


def _small_update(red, lbl, params):
    def body(red_ref, *refs):
        ins, outs = refs[:12], refs[12:]
        lb = _lower_bound(ins[3][...])
        dl0 = red_ref[1:2, :] * lb * (1.0 - lb)
        row = lax.broadcasted_iota(jnp.int32, (2, D), 0)
        grads = [red_ref[0:1, :], jnp.where(row == 0, dl0, -dl0), red_ref[2:3, 0:HK], red_ref[3:4, :]]
        for i, g in enumerate(grads):
            w, m, v = ins[3 * i][...], ins[3 * i + 1][...], ins[3 * i + 2][...]
            d, nm, nv = _adamw_math(w, g, m, v)
            outs[4 * i][...] = g
            outs[4 * i + 1][...] = d
            outs[4 * i + 2][...] = nm
            outs[4 * i + 3][...] = nv
        outs[16][...] = jnp.sum(red_ref[4:5, :], axis=1, keepdims=True)

    flat = [a for p in params for a in p]
    vm = pl.BlockSpec(memory_space=pltpu.VMEM)
    shapes = [jax.ShapeDtypeStruct(p[0].shape, F32) for p in params for _ in range(4)]
    return pl.pallas_call(
        body, name="small_update",
        in_specs=[vm] * 13, out_specs=[vm] * 17,
        out_shape=shapes + [jax.ShapeDtypeStruct((1, 1), F32)],
    )(red, *flat)


def kernel(x, positions, norm_w, w_in, lb_logits, hgrn_norm_w, w_branch_a, w_branch_b, w_out, final_norm_w, loss_target, m_norm_w, m_w_in, m_lb_logits, m_hgrn_norm_w, m_w_branch_a, m_w_branch_b, m_w_out, m_final_norm_w, v_norm_w, v_w_in, v_lb_logits, v_hgrn_norm_w, v_w_branch_a, v_w_branch_b, v_w_out, v_final_norm_w):
    big_w = [w_in[0], w_branch_a[0], w_branch_b[0], w_out[0]]
    big_m = [m_w_in[0], m_w_branch_a[0], m_w_branch_b[0], m_w_out[0]]
    big_v = [v_w_in[0], v_w_branch_a[0], v_w_branch_b[0], v_w_out[0]]
    shapes = [w.shape for w in big_w]
    wf = final_norm_w.reshape(1, D)

    shards = [big_w[0].astype(BF)] + big_w[1:]
    loc = _local_step(x[0], positions.reshape(T, 1), norm_w, lb_logits, hgrn_norm_w, wf, loss_target[0],
                      *shards, shard_shapes=shapes)
    g_big = [loc["g_win"], loc["g_wa"], loc["g_wb"], loc["g_wout"]]
    red = loc["small_sums"]

    small = _small_update(red, lb_logits, [
        (norm_w, m_norm_w, v_norm_w), (lb_logits, m_lb_logits, v_lb_logits),
        (hgrn_norm_w, m_hgrn_norm_w, v_hgrn_norm_w),
        (wf, m_final_norm_w.reshape(1, D), v_final_norm_w.reshape(1, D))])
    loss = small[16].reshape(())
    sg, sd, sm, sv = ([small[4 * i + j] for i in range(4)] for j in range(4))
    for lst in (sg, sd, sm, sv):
        lst[3] = lst[3].reshape(D)
    per_w = list(zip(big_w, g_big, big_m, big_v))
    upd = [_adamw(*per_w[0], "adamw_w_in")] + _adamw_whole(per_w[1:], "adamw_branches_out")
    bd, bm, bv, bg = ([u[j][None] for u in upd] for j in range(4))

    def order(s, b):
        return [s[0], b[0], s[1], s[2], b[1], b[2], b[3], s[3]]

    return (loss, loc["gx"][None], *order(sg, bg), *order(sd, bd), *order(sm, bm), *order(sv, bv))
```

```python
import functools

import jax
import jax.numpy as jnp
from jax import lax
from jax.experimental import pallas as pl
from jax.experimental.pallas import tpu as pltpu

T = 2048
D = 1024
NIN = 11264
HEADS = 8
HK = 128
CH = 16
NCH = T // CH
HSTEP = 2
ATT_GROUPS = ((128, 1), (512, 4), (2048, 16))
ATT_COL0 = 4096
AG_COL0 = 8704
GATE_COL0 = 9216
EPS = 1e-6
ROPE_THETA = 10000.0
LR, B1, B2, ADAM_EPS, WD, STEP = 0.001, 0.9, 0.999, 1e-08, 0.01, 10

F32 = jnp.float32
BF = jnp.bfloat16
VMEM_LIMIT = 56 * 1024 * 1024

_NN = (((1,), (0,)), ((), ()))
_NT = (((1,), (1,)), ((), ()))
_TN = (((0,), (0,)), ((), ()))


def _dot(a, b, dims=_NN):
    return lax.dot_general(a, b, dims, preferred_element_type=F32)


def _bdot(a, b, dims=_NN):
    return lax.dot_general(a.astype(BF), b.astype(BF), dims, preferred_element_type=F32)


def _sigmoid(x):
    return jax.nn.sigmoid(x)


def _params(sem=None):
    return pltpu.CompilerParams(dimension_semantics=sem, vmem_limit_bytes=VMEM_LIMIT)


def _matmul(a, b, *, ta=False, tb=False, out_dtype=F32, tm=512, tn=512, tk=None, name, side=None):
    m = a.shape[1] if ta else a.shape[0]
    kdim = a.shape[0] if ta else a.shape[1]
    n = b.shape[0] if tb else b.shape[1]
    tk = tk or kdim
    tm, tn = min(tm, m), min(tn, n)
    nm, nn, nk = m // tm, n // tn, kdim // tk
    dims = (((0 if ta else 1,), (1 if tb else 0,)), ((), ()))
    s_arrays, s_in_specs, s_shapes, s_out_specs, s_sems = _side_io(side)
    na, no = len(s_arrays), len(s_shapes)
    nacc = 1 if nk > 1 else 0

    def body(*refs):
        a_ref, b_ref = refs[:2]
        s_ins, o_ref, s_outs = refs[2:2 + na], refs[2 + na], refs[3 + na:3 + na + no]
        scratch = refs[3 + na + no:]
        s_sem_refs = scratch[nacc:]
        i, j, k = pl.program_id(0), pl.program_id(1), pl.program_id(2)
        if side is not None:
            @pl.when((i == 0) & (j == 0) & (k == 0))
            def _():
                side.first(s_ins, s_outs, s_sem_refs)

        prod = _bdot(a_ref[...], b_ref[...], dims)
        if nk == 1:
            o_ref[...] = prod.astype(out_dtype)
        else:
            acc = scratch[0]

            @pl.when(k == 0)
            def _():
                acc[...] = prod

            @pl.when(k > 0)
            def _():
                acc[...] += prod

            @pl.when(k == nk - 1)
            def _():
                o_ref[...] = acc[...].astype(out_dtype)

        if side is not None:
            @pl.when((i == nm - 1) & (j == nn - 1) & (k == nk - 1))
            def _():
                side.last(s_ins, s_outs, s_sem_refs)

    a_spec = pl.BlockSpec((tk, tm), lambda i, j, k: (k, i)) if ta else pl.BlockSpec((tm, tk), lambda i, j, k: (i, k))
    b_spec = pl.BlockSpec((tn, tk), lambda i, j, k: (j, k)) if tb else pl.BlockSpec((tk, tn), lambda i, j, k: (k, j))
    sem = ("parallel", "parallel", "arbitrary") if side is None else ("arbitrary",) * 3
    out = pl.pallas_call(
        body, name=name, grid=(nm, nn, nk),
        in_specs=[a_spec, b_spec] + s_in_specs,
        out_specs=[pl.BlockSpec((tm, tn), lambda i, j, k: (i, j))] + s_out_specs,
        out_shape=[jax.ShapeDtypeStruct((m, n), out_dtype)] + s_shapes,
        scratch_shapes=([pltpu.VMEM((tm, tn), F32)] if nk > 1 else []) + s_sems,
        compiler_params=_params(sem),
    )(a, b, *s_arrays)
    return out[0] if side is None else (out[0], out[1:])


DZ_TILE = 512


def _part_offsets(parts):
    counts = [p.shape[1] // DZ_TILE for p in parts]
    offs = [sum(counts[:i]) for i in range(len(parts))]
    return counts, offs


def _part_spec(rows, cnt, off, tile_axis):
    def index(*g):
        return (0 if rows is None else g[0], jnp.clip(g[tile_axis] - off, 0, cnt - 1))
    return index


def _grad_w_in(h, parts):
    counts, offs = _part_offsets(parts)
    n = len(parts)

    def body(h_ref, *refs):
        o_ref = refs[n]
        j = pl.program_id(0)
        for p_ref, cnt, off in zip(refs[:n], counts, offs):
            @pl.when((j >= off) & (j < off + cnt))
            def _(p_ref=p_ref):
                o_ref[...] = _bdot(h_ref[...], p_ref[...], _TN).astype(BF)

    return pl.pallas_call(
        body, name="g_win", grid=(sum(counts),),
        in_specs=[pl.BlockSpec((T, D), lambda j: (0, 0))] +
                 [pl.BlockSpec((T, DZ_TILE), _part_spec(None, c, o, 0)) for c, o in zip(counts, offs)],
        out_specs=pl.BlockSpec((D, DZ_TILE), lambda j: (0, j)),
        out_shape=jax.ShapeDtypeStruct((D, NIN), BF),
        compiler_params=_params(("parallel",)),
    )(h, *parts)


def _grad_w_in_half(h, parts, half_idx, side=None):
    counts, offs = _part_offsets(parts)
    n = len(parts)
    nj = sum(counts)
    s_arrays, s_in_specs, s_shapes, s_out_specs, s_sems = _side_io(side)
    na, no = len(s_arrays), len(s_shapes)

    def body(idx_ref, h_ref, *refs):
        s_ins, o_ref, s_outs, s_sem_refs = refs[n:n + na], refs[n + na], refs[n + na + 1:n + na + 1 + no], refs[n + na + 1 + no:]
        j = pl.program_id(0)
        if side is not None:
            @pl.when(j == 0)
            def _():
                side.first(s_ins, s_outs, s_sem_refs)

        for p_ref, cnt, off in zip(refs[:n], counts, offs):
            @pl.when((j >= off) & (j < off + cnt))
            def _(p_ref=p_ref):
                o_ref[...] = _bdot(h_ref[...], p_ref[...], _TN).astype(BF)

        if side is not None:
            @pl.when(j == nj - 1)
            def _():
                side.last(s_ins, s_outs, s_sem_refs)

    def part_spec(cnt, off):
        return pl.BlockSpec((T, DZ_TILE), lambda j, idx: (0, jnp.clip(j - off, 0, cnt - 1)))

    out = pl.pallas_call(
        body, name="g_win_half" if side is None else "g_win_half_carrying",
        grid_spec=pltpu.PrefetchScalarGridSpec(
            num_scalar_prefetch=1, grid=(nj,),
            in_specs=[pl.BlockSpec((T, D // 2), lambda j, idx: (0, idx[0]))] +
                     [part_spec(c, o) for c, o in zip(counts, offs)] + s_in_specs,
            out_specs=[pl.BlockSpec((D // 2, DZ_TILE), lambda j, idx: (0, j))] + s_out_specs,
            scratch_shapes=s_sems),
        out_shape=[jax.ShapeDtypeStruct((D // 2, NIN), BF)] + s_shapes,
        compiler_params=_params(("parallel",) if side is None else ("arbitrary",)),
    )(half_idx, h, *parts, *s_arrays)
    return out[0] if side is None else (out[0], out[1:])


def _side_io(side):
    if side is None:
        return [], [], [], [], []
    return (side.arrays, [HBM] * len(side.arrays), side.out_shapes, [HBM] * len(side.out_shapes), side.sems)


def _grad_x(parts, w_in, x, dout, norm_w, side=None):
    counts, offs = _part_offsets(parts)
    n = len(parts)
    tm = 1024
    nm, nk = T // tm, sum(counts)
    s_arrays, s_in_specs, s_shapes, s_out_specs, s_sems = _side_io(side)
    na, no = len(s_arrays), len(s_shapes)

    def body(*refs):
        w_ref, x_ref, dout_ref, nw_ref = refs[n:n + 4]
        s_ins = refs[n + 4:n + 4 + na]
        gx_ref, gw_ref = refs[n + 4 + na:n + 6 + na]
        s_outs = refs[n + 6 + na:n + 6 + na + no]
        acc = refs[n + 6 + na + no]
        s_sem_refs = refs[n + 7 + na + no:]
        i, k = pl.program_id(0), pl.program_id(1)

        @pl.when((i == 0) & (k == 0))
        def _():
            gw_ref[...] = jnp.zeros_like(gw_ref)
            if side is not None:
                side.first(s_ins, s_outs, s_sem_refs)

        @pl.when(k == 0)
        def _():
            acc[...] = jnp.zeros_like(acc)

        if side is not None and side.mid is not None:
            @pl.when((i == nm - 1) & (k == 0))
            def _():
                side.mid(s_ins, s_outs, s_sem_refs)

        for p_ref, cnt, off in zip(refs[:n], counts, offs):
            @pl.when((k >= off) & (k < off + cnt))
            def _(p_ref=p_ref):
                acc[...] += _bdot(p_ref[...], w_ref[...], _NT)

        @pl.when(k == nk - 1)
        def _():
            gw = jnp.zeros((1, D), F32)
            for c in range(tm // BLK):
                rows = pl.ds(BLK * c, BLK)
                xv, dhv = x_ref[rows, :], acc[rows, :]
                r = lax.rsqrt(jnp.mean(xv * xv, axis=-1, keepdims=True) + EPS)
                nrm = xv * r
                dn = dhv * nw_ref[...]
                gw = gw + jnp.sum(dhv * nrm, axis=0, keepdims=True)
                gx_ref[rows, :] = dout_ref[rows, :] + r * (dn - nrm * jnp.mean(dn * nrm, axis=-1, keepdims=True))
            gw_ref[...] += gw

        if side is not None:
            @pl.when((i == nm - 1) & (k == nk - 1))
            def _():
                side.last(s_ins, s_outs, s_sem_refs)

    row = pl.BlockSpec((tm, D), lambda i, k: (i, 0))
    vec = pl.BlockSpec((1, D), lambda i, k: (0, 0))
    out = pl.pallas_call(
        body, name="grad_x", grid=(nm, nk),
        in_specs=[pl.BlockSpec((tm, DZ_TILE), _part_spec(0, c, o, 1)) for c, o in zip(counts, offs)] +
                 [pl.BlockSpec((D, DZ_TILE), lambda i, k: (0, k)), row, row, vec] + s_in_specs,
        out_specs=[row, vec] + s_out_specs,
        out_shape=[jax.ShapeDtypeStruct((T, D), F32), jax.ShapeDtypeStruct((1, D), F32)] + s_shapes,
        scratch_shapes=[pltpu.VMEM((tm, D), F32)] + s_sems,
        compiler_params=_params(("arbitrary", "arbitrary")),
    )(*parts, w_in, x, dout, norm_w, *s_arrays)
    return out[0], out[1], out[2:]


def _norm_and_rope_tables(x, w, pos, invf, side=None, own=None, cast=()):
    tm = 256
    nm = T // tm
    s_arrays, s_in_specs, s_shapes, s_out_specs, s_sems = _side_io(side)
    na, no, nc = len(s_arrays), len(s_shapes), len(cast)
    nz = 0 if own is None else 1
    wsh, blk = own if own is not None else (None, jnp.zeros((1,), jnp.int32))
    first_out = 4 + nz + nc + na

    def body(blk_ref, *refs):
        x_ref, w_ref, pos_ref, invf_ref = refs[:4]
        s_ins = refs[4 + nz + nc:first_out]
        h_ref, cos_ref, sa_ref, sb_ref = refs[first_out:first_out + 4]
        s_outs = refs[first_out + 4 + nz + nc:first_out + 4 + nz + nc + no]
        s_sem_refs = refs[first_out + 4 + nz + nc + no:]

        @pl.when(pl.program_id(0) == 0)
        def _():
            if side is not None:
                side.first(s_ins, s_outs, s_sem_refs)
            for src, dst in zip(refs[4 + nz:4 + nz + nc], refs[first_out + 4 + nz:first_out + 4 + nz + nc]):
                dst[...] = src[...].astype(BF)

        xv = x_ref[...]
        r = lax.rsqrt(jnp.mean(xv * xv, axis=-1, keepdims=True) + EPS)
        h = (xv * r * w_ref[...]).astype(BF)
        h_ref[...] = h
        if own is not None:
            refs[first_out + 4][...] = _dot(h, refs[4][...])
        first = (lax.broadcasted_iota(jnp.int32, (tm, 128), 1) % 64) < 32
        ang = pos_ref[...].astype(F32) * invf_ref[...]
        s = jnp.sin(ang)
        cos_ref[...] = jnp.cos(ang)
        sa_ref[...] = jnp.where(first, -s, 0.0)
        sb_ref[...] = jnp.where(first, 0.0, s)
        if side is not None:
            @pl.when(pl.program_id(0) == nm - 1)
            def _():
                side.last(s_ins, s_outs, s_sem_refs)

    tab = pl.BlockSpec((tm, 128), lambda i, b: (i, 0))
    own_in = [] if own is None else [pl.BlockSpec(wsh.shape, lambda i, b: (0, 0))]
    own_out = [] if own is None else [pl.BlockSpec((tm, wsh.shape[1]), lambda i, b: (i, b[0]))]
    own_shape = [] if own is None else [jax.ShapeDtypeStruct((T, NIN), F32)]
    whole = [pl.BlockSpec(a.shape, lambda i, b: (0, 0)) for a in cast]
    out = pl.pallas_call(
        body, name="norm_and_rope_tables",
        grid_spec=pltpu.PrefetchScalarGridSpec(
            num_scalar_prefetch=1, grid=(nm,),
            in_specs=[pl.BlockSpec((tm, D), lambda i, b: (i, 0)), pl.BlockSpec((1, D), lambda i, b: (0, 0)),
                      pl.BlockSpec((tm, 1), lambda i, b: (i, 0)), pl.BlockSpec((1, 128), lambda i, b: (0, 0))]
                     + own_in + whole + s_in_specs,
            out_specs=[pl.BlockSpec((tm, D), lambda i, b: (i, 0)), tab, tab, tab] + own_out + whole + s_out_specs,
            scratch_shapes=s_sems),
        out_shape=[jax.ShapeDtypeStruct((T, D), BF)] + [jax.ShapeDtypeStruct((T, 128), F32)] * 3 + own_shape
                  + [jax.ShapeDtypeStruct(a.shape, BF) for a in cast] + s_shapes,
        compiler_params=_params(("parallel",) if side is None and not cast else ("arbitrary",)),
    )(blk, x, w, pos, invf, *([] if own is None else [wsh]), *cast, *s_arrays)
    return (out[0], out[1], out[2], out[3], (out[4] if own is not None else None), out[4 + nz:4 + nz + nc],
            out[4 + nz + nc:])


def _z_blocks(h, w, z, idx, nb, side, name, fill=None):
    tm, tn = 1024, NIN // 8
    s_arrays, s_in_specs, s_shapes, s_out_specs, s_sems = _side_io(side)
    na, no = len(s_arrays), len(s_shapes)
    nm, ns = T // tm, 2 * nb
    nf = 0 if fill is None else 1

    def col(first, i, s, b):
        return (0, b[first + s // 2] * 2 + s % 2)

    def body(idx_ref, h_ref, w_ref, zin_ref, *refs):
        s_ins = refs[nf:nf + na]
        o_ref = refs[nf + na]
        s_outs = refs[nf + na + 1 + nf:nf + na + 1 + nf + no]
        s_sem_refs = refs[nf + na + 1 + nf + no + nf:]
        i, s = pl.program_id(0), pl.program_id(1)

        if side is not None:
            @pl.when((i == 0) & (s == 0))
            def _():
                side.first(s_ins, s_outs, s_sem_refs)

        if fill is not None:
            tile = pl.ds(pl.multiple_of((idx_ref[0] * 2 + s) * tn, 128), tn)
            store = pltpu.make_async_copy(w_ref, refs[nf + na + 1].at[:, tile], refs[nf + na + 1 + nf + no].at[0])
            pl.when(i == 0)(store.start)
        o_ref[...] = _dot(h_ref[...], w_ref[...])
        if fill is not None:
            pl.when(i == 0)(store.wait)

        if side is not None:
            @pl.when((i == nm - 1) & (s == ns - 1))
            def _():
                side.last(s_ins, s_outs, s_sem_refs)

    fills = [] if fill is None else [fill]
    out = pl.pallas_call(
        body, name=name,
        grid_spec=pltpu.PrefetchScalarGridSpec(
            num_scalar_prefetch=1, grid=(nm, ns),
            in_specs=[pl.BlockSpec((tm, D), lambda i, s, b: (i, 0)), pl.BlockSpec((D, tn), functools.partial(col, nb)),
                      HBM] + [HBM] * nf + s_in_specs,
            out_specs=[pl.BlockSpec((tm, tn), lambda i, s, b: (i, col(0, i, s, b)[1]))] + [HBM] * nf + s_out_specs,
            scratch_shapes=[pltpu.SemaphoreType.DMA((1,))] * nf + s_sems),
        out_shape=[jax.ShapeDtypeStruct((T, NIN), F32)] + [jax.ShapeDtypeStruct(f.shape, f.dtype) for f in fills]
                  + s_shapes,
        input_output_aliases={3: 0, **({4: 1} if fill is not None else {})},
        compiler_params=_params(("arbitrary", "arbitrary")),
    )(idx, h, w, z, *fills, *s_arrays)
    return (out[0], *out[1:1 + nf], out[1 + nf:])


def _lower_bound(lbl):
    mx = jnp.max(lbl, axis=0, keepdims=True)
    e = jnp.exp(lbl - mx)
    return e[0:1] / jnp.sum(e, axis=0, keepdims=True)


def _cumsum_rows(g, rows):
    b = g
    sh = 1
    while sh < CH:
        b = b + jnp.where(rows >= sh, pltpu.roll(b, sh, axis=0), 0.0)
        sh *= 2
    return b


def _rev_cumsum_rows(g, rows):
    b = g
    sh = 1
    while sh < CH:
        b = b + jnp.where(rows < CH - sh, pltpu.roll(b, CH - sh, axis=0), 0.0)
        sh *= 2
    return b


SUB = CH // 2


def _direct_block(qb, kb, vb, bb, rows8):
    ob = jnp.zeros_like(qb)
    for s in range(SUB):
        e_s = jnp.exp(jnp.where(rows8 >= s, bb - bb[s:s + 1], -jnp.inf))
        ob = ob + jnp.sum(qb * e_s * kb[s:s + 1], axis=1, keepdims=True) * vb[s:s + 1]
    return ob


def _direct_block_bwd(qb, kb, vb, bb, dob, rows8, rowc8):
    dq = dk = dv = db = jnp.zeros_like(qb)
    for s in range(SUB):
        one = (rowc8 == s).astype(F32)
        ks, vs = kb[s:s + 1], vb[s:s + 1]
        e_s = jnp.exp(jnp.where(rows8 >= s, bb - bb[s:s + 1], -jnp.inf))
        qes = qb * e_s
        w = qes * ks
        a = jnp.sum(w, axis=1, keepdims=True)
        da = jnp.sum(dob * vs, axis=1, keepdims=True)
        dv = dv + one * jnp.sum(a * dob, axis=0, keepdims=True)
        dq = dq + da * e_s * ks
        dk = dk + one * jnp.sum(da * qes, axis=0, keepdims=True)
        u = da * w
        db = db + u - one * jnp.sum(u, axis=0, keepdims=True)
    return dq, dk, dv, db


def _cross_factors(q, k, b):
    ref = b[SUB - 1:SUB]
    e_hi, e_lo = jnp.exp(b[SUB:] - ref), jnp.exp(ref - b[:SUB])
    return q[SUB:] * e_hi, k[:SUB] * e_lo, e_hi, e_lo


def _intra_fwd(q, k, v, b, rows8):
    lo = _direct_block(q[:SUB], k[:SUB], v[:SUB], b[:SUB], rows8)
    hi = _direct_block(q[SUB:], k[SUB:], v[SUB:], b[SUB:], rows8)
    qe_hi, ke_lo, _, _ = _cross_factors(q, k, b)
    for s in range(SUB):
        hi = hi + jnp.sum(qe_hi * ke_lo[s:s + 1], axis=1, keepdims=True) * v[s:s + 1]
    return jnp.concatenate([lo, hi], axis=0)


def _intra_bwd(q, k, v, b, do, rows8, rowc8):
    dq_lo, dk_lo, dv_lo, db_lo = _direct_block_bwd(q[:SUB], k[:SUB], v[:SUB], b[:SUB], do[:SUB], rows8, rowc8)
    dq_hi, dk_hi, dv_hi, db_hi = _direct_block_bwd(q[SUB:], k[SUB:], v[SUB:], b[SUB:], do[SUB:], rows8, rowc8)
    qe_hi, ke_lo, e_hi, e_lo = _cross_factors(q, k, b)
    do_hi, v_lo = do[SUB:], v[:SUB]
    dqe = dke = jnp.zeros_like(qe_hi)
    for s in range(SUB):
        one = (rowc8 == s).astype(F32)
        a = jnp.sum(qe_hi * ke_lo[s:s + 1], axis=1, keepdims=True)
        da = jnp.sum(do_hi * v_lo[s:s + 1], axis=1, keepdims=True)
        dv_lo = dv_lo + one * jnp.sum(a * do_hi, axis=0, keepdims=True)
        dqe = dqe + da * ke_lo[s:s + 1]
        dke = dke + one * jnp.sum(da * qe_hi, axis=0, keepdims=True)
    u_hi, u_lo = dqe * qe_hi, dke * ke_lo
    d_ref = jnp.sum(u_lo, axis=0, keepdims=True) - jnp.sum(u_hi, axis=0, keepdims=True)
    db_lo = db_lo - u_lo + (rowc8 == SUB - 1).astype(F32) * d_ref
    cat = lambda lo, hi: jnp.concatenate([lo, hi], axis=0)
    return (cat(dq_lo, dq_hi + dqe * e_hi), cat(dk_lo + dke * e_lo, dk_hi), cat(dv_lo, dv_hi),
            cat(db_lo, db_hi + u_hi))


def _hgrn_fwd(z, lbl, nw):
    def body(hq_ref, hf_ref, hi_ref, hg_ref, lbl_ref, nw_ref, oraw_ref, og_ref, sh_ref, st_ref):
        @pl.when(pl.program_id(0) == 0)
        def _():
            st_ref[...] = jnp.zeros_like(st_ref)

        lb_all = _lower_bound(lbl_ref[...])
        rows = lax.broadcasted_iota(jnp.int32, (CH, HK), 0)
        rows8 = lax.broadcasted_iota(jnp.int32, (SUB, HK), 0)
        nwv = nw_ref[...]
        for cc, h in [(cc, h) for cc in range(HSTEP) for h in range(HEADS)]:
            rs = slice(CH * cc, CH * (cc + 1))
            sl = slice(HK * h, HK * (h + 1))
            lb = lb_all[:, sl]
            hq, hf, v, hg = hq_ref[rs, sl], hf_ref[rs, sl], hi_ref[rs, sl], hg_ref[rs, sl]
            q = hq * _sigmoid(hq)
            f = lb + (1.0 - lb) * _sigmoid(hf)
            k = 1.0 - f
            b = _cumsum_rows(jnp.log(f), rows)
            sh_ref[cc, h] = st_ref[h]
            o = _bdot(q * jnp.exp(b), st_ref[h], _NT) + _intra_fwd(q, k, v, b, rows8)
            bl = b[CH - 1:CH]
            st_ref[h] = st_ref[h] * jnp.exp(bl)
            st_ref[h] += _bdot(v, k * jnp.exp(bl - b), _TN)
            oraw_ref[rs, sl] = o
            nrm = o * lax.rsqrt(jnp.mean(o * o, axis=1, keepdims=True) + EPS)
            og_ref[rs, sl] = (nrm * nwv * (hg * _sigmoid(hg))).astype(BF)

    zblk = lambda c: pl.BlockSpec((CH * HSTEP, D), lambda i, c=c: (i, c))
    return pl.pallas_call(
        body, name="hgrn_fwd", grid=(NCH // HSTEP,),
        in_specs=[zblk(0), zblk(1), zblk(2), zblk(3),
                  pl.BlockSpec((2, D), lambda i: (0, 0)), pl.BlockSpec((1, HK), lambda i: (0, 0))],
        out_specs=[zblk(0), zblk(0),
                   pl.BlockSpec((HSTEP, HEADS, HK, HK), lambda i: (i, 0, 0, 0))],
        out_shape=[jax.ShapeDtypeStruct((T, D), F32), jax.ShapeDtypeStruct((T, D), BF),
                   jax.ShapeDtypeStruct((NCH, HEADS, HK, HK), F32)],
        scratch_shapes=[pltpu.VMEM((HEADS, HK, HK), F32)],
        compiler_params=_params(("arbitrary",)),
    )(z, z, z, z, lbl, nw)


def _hgrn_bwd(z, lbl, nw, oraw, dog, shist):
    hstep = 1

    def body(hq_ref, hf_ref, hi_ref, hg_ref, lbl_ref, nw_ref, oraw_ref, dog_ref, sh_ref,
             dz_ref, dlb_ref, dnw_ref, dst_ref):
        @pl.when(pl.program_id(0) == 0)
        def _():
            dst_ref[...] = jnp.zeros_like(dst_ref)
            dlb_ref[...] = jnp.zeros_like(dlb_ref)
            dnw_ref[...] = jnp.zeros_like(dnw_ref)

        lb_all = _lower_bound(lbl_ref[...])
        rows = lax.broadcasted_iota(jnp.int32, (CH, HK), 0)
        rowc = lax.broadcasted_iota(jnp.int32, (CH, 1), 0)
        rows8 = lax.broadcasted_iota(jnp.int32, (SUB, HK), 0)
        rowc8 = lax.broadcasted_iota(jnp.int32, (SUB, 1), 0)
        nwv = nw_ref[...]
        dnw = jnp.zeros((1, HK), F32)
        for cc, h in [(cc, h) for cc in reversed(range(hstep)) for h in range(HEADS)]:
            rs = slice(CH * cc, CH * (cc + 1))
            sl = slice(HK * h, HK * (h + 1))
            lb = lb_all[:, sl]
            hq, hf, v, hg = hq_ref[rs, sl], hf_ref[rs, sl], hi_ref[rs, sl], hg_ref[rs, sl]
            o, dg_out = oraw_ref[rs, sl], dog_ref[rs, sl]
            sg = _sigmoid(hg)
            sil = hg * sg
            r = lax.rsqrt(jnp.mean(o * o, axis=1, keepdims=True) + EPS)
            nrm = o * r
            d_hg = dg_out * (nrm * nwv) * (sg * (1.0 + hg * (1.0 - sg)))
            dn = dg_out * nwv * sil
            dnw = dnw + jnp.sum(dg_out * nrm * sil, axis=0, keepdims=True)
            do = r * (dn - nrm * jnp.mean(dn * nrm, axis=1, keepdims=True))
            sq = _sigmoid(hq)
            q = hq * sq
            sig = _sigmoid(hf)
            f = lb + (1.0 - lb) * sig
            k = 1.0 - f
            b = _cumsum_rows(jnp.log(f), rows)
            eb = jnp.exp(b)
            qe = q * eb
            bl = b[CH - 1:CH]
            ebl = jnp.exp(bl)
            kdec = jnp.exp(bl - b)
            ke = k * kdec
            dqe = _bdot(do, sh_ref[cc, h])
            dq = dqe * eb
            db = dqe * qe
            dke = _bdot(v, dst_ref[h])
            dv = _bdot(ke, dst_ref[h], _NT)
            dk = dke * kdec
            rr = dke * ke
            db = db - rr
            db_last = (jnp.sum(rr, axis=0, keepdims=True)
                       + ebl * jnp.sum(dst_ref[h] * sh_ref[cc, h], axis=0, keepdims=True))
            dst_ref[h] = dst_ref[h] * ebl
            dst_ref[h] += _bdot(do, qe, _TN)
            dq_i, dk_i, dv_i, db_i = _intra_bwd(q, k, v, b, do, rows8, rowc8)
            dq, dk, dv = dq + dq_i, dk + dk_i, dv + dv_i
            db = db + db_i + (rowc == CH - 1).astype(F32) * db_last
            dgl = _rev_cumsum_rows(db, rows)
            df = dgl / f - dk
            dlb_ref[:, sl] += jnp.sum(df * (1.0 - sig), axis=0, keepdims=True)
            dz_ref[rs, sl] = (dq * (sq * (1.0 + hq * (1.0 - sq)))).astype(BF)
            dz_ref[rs, D + HK * h:D + HK * (h + 1)] = (df * (1.0 - lb) * sig * (1.0 - sig)).astype(BF)
            dz_ref[rs, 2 * D + HK * h:2 * D + HK * (h + 1)] = dv.astype(BF)
            dz_ref[rs, 3 * D + HK * h:3 * D + HK * (h + 1)] = d_hg.astype(BF)
        dnw_ref[...] += dnw

    rev = lambda i: NCH // hstep - 1 - i
    zblk = lambda c: pl.BlockSpec((CH * hstep, D), lambda i, c=c: (rev(i), c))
    return pl.pallas_call(
        body, name="hgrn_bwd", grid=(NCH // hstep,),
        in_specs=[zblk(0), zblk(1), zblk(2), zblk(3),
                  pl.BlockSpec((2, D), lambda i: (0, 0)), pl.BlockSpec((1, HK), lambda i: (0, 0)),
                  zblk(0), zblk(0),
                  pl.BlockSpec((hstep, HEADS, HK, HK), lambda i: (rev(i), 0, 0, 0))],
        out_specs=[pl.BlockSpec((CH * hstep, 4 * D), lambda i: (rev(i), 0)),
                   pl.BlockSpec((1, D), lambda i: (0, 0)), pl.BlockSpec((1, HK), lambda i: (0, 0))],
        out_shape=[jax.ShapeDtypeStruct((T, 4 * D), BF), jax.ShapeDtypeStruct((1, D), F32),
                   jax.ShapeDtypeStruct((1, HK), F32)],
        scratch_shapes=[pltpu.VMEM((HEADS, HK, HK), F32)],
        compiler_params=_params(("arbitrary",)),
    )(z, z, z, z, lbl, nw, oraw, dog, shist)


BLK = 128
NBLK = T // BLK
QK_SCALE = 0.125


def _head_masks():
    lane = lax.broadcasted_iota(jnp.int32, (1, BLK), 1)
    return [(lane < 64).astype(F32), (lane >= 64).astype(F32)]


def _pieces(dil):
    m = T // dil
    out = []
    for r in range(dil):
        for j in range(m // BLK):
            start = r + dil * BLK * j
            rows = pl.ds(start, BLK, stride=dil) if dil > 1 else pl.ds(start, BLK)
            out.append((rows, r * m + BLK * j))
    return out


def _rope(x, c, sa, sb):
    return x * c + pltpu.roll(x, 96, axis=1) * sa + pltpu.roll(x, 32, axis=1) * sb


def _rope_t(d, c, sa, sb):
    return d * c + pltpu.roll(d * sa, 32, axis=1) + pltpu.roll(d * sb, 96, axis=1)


def _rope_and_regroup(dil, q_ref, k_ref, v_ref, tables, stage_q, stage_k, qr_ref, kr_ref, vr_ref):
    cos_ref, sa_ref, sb_ref = tables
    to_q, to_k = (qr_ref, kr_ref) if dil == 1 else (stage_q, stage_k)
    for c in range(T // BLK):
        rows = pl.ds(BLK * c, BLK)
        cs, sa, sb = cos_ref[rows, :], sa_ref[rows, :], sb_ref[rows, :]
        to_q[rows, :] = (_rope(q_ref[rows, :], cs, sa, sb) * QK_SCALE).astype(to_q.dtype)
        to_k[rows, :] = _rope(k_ref[rows, :], cs, sa, sb).astype(to_k.dtype)
    for rows, dst in _pieces(dil):
        drows = pl.ds(dst, BLK)
        if dil > 1:
            qr_ref[drows, :] = stage_q[rows, :].astype(qr_ref.dtype)
            kr_ref[drows, :] = stage_k[rows, :].astype(kr_ref.dtype)
        vr_ref[drows, :] = v_ref[rows, :].astype(vr_ref.dtype)


def _window_bias(bias_ref):
    ii = lax.broadcasted_iota(jnp.int32, (2 * BLK, BLK), 0) % BLK
    jj = lax.broadcasted_iota(jnp.int32, (2 * BLK, BLK), 1)
    bias_ref[0] = jnp.where(jj <= ii, 0.0, -jnp.inf)
    bias_ref[1] = jnp.where(jj >= ii, 0.0, -jnp.inf)


def _blocks(bi):
    if isinstance(bi, int):
        return pl.ds(bi * BLK, BLK), pl.ds(max(bi - 1, 0) * BLK, BLK)
    return (pl.ds(pl.multiple_of(bi * BLK, BLK), BLK),
            pl.ds(pl.multiple_of(jnp.maximum(bi - 1, 0) * BLK, BLK), BLK))


def _stack_heads(x, masks):
    return jnp.concatenate([x * masks[0].astype(x.dtype), x * masks[1].astype(x.dtype)], axis=0).astype(BF)


def _side_steps(side, refs, **when):
    if side is None:
        return
    for stage, cond in when.items():
        if getattr(side, stage) is not None:
            pl.when(cond)(functools.partial(getattr(side, stage), *refs))


def _attn_fwd(z, cos, sa, sb, side=None):
    s_arrays, s_in_specs, s_shapes, s_out_specs, s_sems = _side_io(side)
    na, no = len(s_arrays), len(s_shapes)

    def body(q_ref, k_ref, v_ref, ag_ref, cos_ref, sa_ref, sb_ref, *refs):
        ob_ref, opre_ref, lse_ref, qr_ref, kr_ref, vr_ref = refs[na:na + 6]
        bias_ref, og_ref, lg_ref, otok_ref, ltok_ref, sc_ref = refs[na + 6 + no:na + 12 + no]
        s_refs = (refs[:na], refs[na + 6:na + 6 + no], refs[na + 12 + no:])
        p, g = pl.program_id(0), pl.program_id(1)
        _side_steps(side, s_refs, first=(p == 0) & (g == 0), mid=(p == 1) & (g == 0))
        masks = _head_masks()

        @pl.when(g == 0)
        def _():
            _window_bias(bias_ref)

        def group(gi):
            dil = ATT_GROUPS[gi][1]
            nblk = (T // dil) // BLK
            _rope_and_regroup(dil, q_ref, k_ref, v_ref, (cos_ref, sa_ref, sb_ref), lg_ref.at[0], lg_ref.at[1],
                              qr_ref, kr_ref, vr_ref)

            def scores(bi, slot):
                cur, prev = _blocks(bi)
                q2 = _stack_heads(qr_ref[cur, :], masks)
                sc_ref[slot, 0] = _dot(q2, kr_ref[cur, :], _NT) + bias_ref[0]
                if nblk > 1:
                    sc_ref[slot, 1] = (_dot(q2, kr_ref[prev, :], _NT)
                                       + (bias_ref[1] + jnp.where((bi % nblk) != 0, 0.0, -jnp.inf)))

            def finish(bi, slot):
                cur, prev = _blocks(bi)
                s_c, vc = sc_ref[slot, 0], vr_ref[cur, :]
                if nblk > 1:
                    s_p, vp = sc_ref[slot, 1], vr_ref[prev, :]
                    mx = jnp.max(jnp.maximum(s_c, s_p), axis=1, keepdims=True)
                    p_c, p_p = jnp.exp(s_c - mx), jnp.exp(s_p - mx)
                    den = jnp.sum(p_c + p_p, axis=1, keepdims=True)
                    oh = _dot(p_c.astype(BF), vc) + _dot(p_p.astype(BF), vp)
                else:
                    mx = jnp.max(s_c, axis=1, keepdims=True)
                    p_c = jnp.exp(s_c - mx)
                    den = jnp.sum(p_c, axis=1, keepdims=True)
                    oh = _dot(p_c.astype(BF), vc)
                on = oh / den
                lsev = jnp.broadcast_to(mx + jnp.log(den), (2 * BLK, BLK))
                og_ref[cur, :] = on[:BLK] * masks[0] + on[BLK:] * masks[1]
                lg_ref[0, cur, :] = lsev[:BLK]
                lg_ref[1, cur, :] = lsev[BLK:]

            def pair(j, carry):
                finish(2 * j, 0)
                scores(2 * j + 1, 1)
                finish(2 * j + 1, 1)
                scores(jnp.minimum(2 * j + 2, NBLK - 1), 0)
                return carry

            scores(0, 0)
            lax.fori_loop(0, NBLK // 2, pair, 0)
            for rows, src in _pieces(dil):
                srows = pl.ds(src, BLK)
                otok_ref[gi, rows, :] = og_ref[srows, :]
                ltok_ref[gi, 0, rows, :] = lg_ref[0, srows, :]
                ltok_ref[gi, 1, rows, :] = lg_ref[1, srows, :]

        for gi in range(3):
            pl.when(g == gi)(functools.partial(group, gi))

        @pl.when(g == 2)
        def _():
            for c in range(T // BLK):
                rows = pl.ds(BLK * c, BLK)
                wts = []
                for hh in range(2):
                    l0, l1, l2 = ltok_ref[0, hh, rows, :], ltok_ref[1, hh, rows, :], ltok_ref[2, hh, rows, :]
                    mx = jnp.maximum(jnp.maximum(l0, l1), l2)
                    e0, e1, e2 = jnp.exp(l0 - mx), jnp.exp(l1 - mx), jnp.exp(l2 - mx)
                    tot = e0 + e1 + e2
                    lse_ref[rows, BLK * hh:BLK * (hh + 1)] = mx + jnp.log(tot)
                    inv = 1.0 / tot
                    wts.append([e0 * inv, e1 * inv, e2 * inv])
                o = sum((wts[0][gi] * masks[0] + wts[1][gi] * masks[1]) * otok_ref[gi, rows, :] for gi in range(3))
                ag = ag_ref[rows, :]
                opre_ref[rows, :] = o
                ob_ref[rows, :] = (o * (ag * _sigmoid(ag))).astype(BF)

        _side_steps(side, s_refs, last=(p == 3) & (g == 2))

    c0 = ATT_COL0 // BLK
    zspec = lambda part: pl.BlockSpec((T, BLK), lambda p, g, part=part: (0, c0 + 12 * part + 4 * g + p))
    outspec = pl.BlockSpec((T, BLK), lambda p, g: (0, p))
    table = pl.BlockSpec((T, BLK), lambda p, g: (0, 0))
    regrouped = pl.BlockSpec((None, T, BLK), lambda p, g: (g, 0, p))
    big = lambda: pltpu.VMEM((T, BLK), F32)
    out = pl.pallas_call(
        body, name="attn_fwd", grid=(4, 3),
        in_specs=[zspec(0), zspec(1), zspec(2),
                  pl.BlockSpec((T, BLK), lambda p, g: (0, AG_COL0 // BLK + p)), table, table, table] + s_in_specs,
        out_specs=[outspec, outspec, pl.BlockSpec((T, 2 * BLK), lambda p, g: (0, p)), regrouped, regrouped, regrouped]
                  + s_out_specs,
        out_shape=[jax.ShapeDtypeStruct((T, 512), BF), jax.ShapeDtypeStruct((T, 512), F32),
                   jax.ShapeDtypeStruct((T, 8 * BLK), F32)] + [jax.ShapeDtypeStruct((3, T, 512), BF)] * 3 + s_shapes,
        scratch_shapes=[pltpu.VMEM((2, 2 * BLK, BLK), F32), big(),
                        pltpu.VMEM((2, T, BLK), F32), pltpu.VMEM((3, T, BLK), F32), pltpu.VMEM((3, 2, T, BLK), F32),
                        pltpu.VMEM((2, 2, 2 * BLK, BLK), F32)] + s_sems,
        compiler_params=_params(("parallel" if side is None else "arbitrary", "arbitrary")),
    )(z, z, z, z, cos, sa, sb, *s_arrays)
    return (*out[:6], out[6:])


def _attn_bwd(z, qs, ks, vs, cos, sa, sb, opre, lse, dob, side=None):
    s_arrays, s_in_specs, s_shapes, s_out_specs, s_sems = _side_io(side)
    na, no = len(s_arrays), len(s_shapes)

    def body(qs_ref, ks_ref, vs_ref, ag_ref, cos_ref, sa_ref, sb_ref, o_ref, lse0_ref, lse1_ref, dob_ref, *refs):
        dq_ref, dk_ref, dv_ref, dag_ref = refs[na:na + 4]
        (bias_ref, dtok_ref, qr_ref, kr_ref, vr_ref, dor_ref, lr_ref, dr_ref,
         dqr_ref, dkr_ref, dvr_ref, pd_ref, dotok_ref) = refs[na + 4 + no:na + 17 + no]
        s_refs = (refs[:na], refs[na + 4:na + 4 + no], refs[na + 17 + no:])
        p, g = pl.program_id(0), pl.program_id(1)
        _side_steps(side, s_refs, first=(p == 0) & (g == 0), mid=(p == 1) & (g == 0))
        masks = _head_masks()

        @pl.when(g == 0)
        def _():
            _window_bias(bias_ref)
            for c in range(T // BLK):
                rows = pl.ds(BLK * c, BLK)
                ag, dob_v, o = ag_ref[rows, :], dob_ref[rows, :], o_ref[rows, :]
                sg = _sigmoid(ag)
                dag_ref[rows, :] = (dob_v * o * (sg * (1.0 + ag * (1.0 - sg)))).astype(BF)
                do = dob_v * (ag * sg)
                dotok_ref[rows, :] = do
                prod = do * o
                for hh, mh in enumerate(masks):
                    dtok_ref[hh, rows, :] = jnp.broadcast_to(jnp.sum(prod * mh, axis=1, keepdims=True), (BLK, BLK))

        def group(gi):
            dil = ATT_GROUPS[gi][1]
            nblk = (T // dil) // BLK
            for rows, dst in _pieces(dil):
                drows = pl.ds(dst, BLK)
                dor_ref[drows, :] = dotok_ref[rows, :]
                for hh, lse_ref in enumerate((lse0_ref, lse1_ref)):
                    lr_ref[hh, drows, :] = lse_ref[rows, :]
                    dr_ref[hh, drows, :] = dtok_ref[hh, rows, :]
            dkr_ref[...] = jnp.zeros_like(dkr_ref)
            dvr_ref[...] = jnp.zeros_like(dvr_ref)

            def probs(bi, slot):
                cur, prev = _blocks(bi)
                q2, do2 = _stack_heads(qs_ref[cur, :], masks), _stack_heads(dor_ref[cur, :], masks)
                lh = jnp.concatenate([lr_ref[0, cur, :], lr_ref[1, cur, :]], axis=0)
                dh = jnp.concatenate([dr_ref[0, cur, :], dr_ref[1, cur, :]], axis=0)
                p_c = jnp.exp(_dot(q2, ks_ref[cur, :], _NT) + bias_ref[0] - lh)
                pd_ref[slot, 0] = p_c.astype(BF)
                pd_ref[slot, 1] = (p_c * (_dot(do2, vs_ref[cur, :], _NT) - dh)).astype(BF)
                if nblk > 1:
                    bias_p = bias_ref[1] + jnp.where((bi % nblk) != 0, 0.0, -jnp.inf)
                    p_p = jnp.exp(_dot(q2, ks_ref[prev, :], _NT) + bias_p - lh)
                    pd_ref[slot, 2] = p_p.astype(BF)
                    pd_ref[slot, 3] = (p_p * (_dot(do2, vs_ref[prev, :], _NT) - dh)).astype(BF)

            def grads(bi, slot):
                cur, prev = _blocks(bi)
                q2, do2 = _stack_heads(qs_ref[cur, :], masks), _stack_heads(dor_ref[cur, :], masks)
                p_c, ds_c = pd_ref[slot, 0], pd_ref[slot, 1]
                dq2 = _dot(ds_c, ks_ref[cur, :])
                dkr_ref[cur, :] += _dot(ds_c, q2, _TN)
                dvr_ref[cur, :] += _dot(p_c, do2, _TN)
                if nblk > 1:
                    p_p, ds_p = pd_ref[slot, 2], pd_ref[slot, 3]
                    dq2 = dq2 + _dot(ds_p, ks_ref[prev, :])
                    dkr_ref[prev, :] += _dot(ds_p, q2, _TN)
                    dvr_ref[prev, :] += _dot(p_p, do2, _TN)
                dqr_ref[cur, :] = dq2[:BLK] * masks[0] + dq2[BLK:] * masks[1]

            def pair(j, carry):
                grads(2 * j, 0)
                probs(2 * j + 1, 1)
                grads(2 * j + 1, 1)
                probs(jnp.minimum(2 * j + 2, NBLK - 1), 0)
                return carry

            probs(0, 0)
            lax.fori_loop(0, NBLK // 2, pair, 0)
            if dil > 1:
                for rows, src in _pieces(dil):
                    srows = pl.ds(src, BLK)
                    qr_ref[rows, :] = dqr_ref[srows, :]
                    kr_ref[rows, :] = dkr_ref[srows, :]
                    vr_ref[rows, :] = dvr_ref[srows, :]
            tq, tk, tv = (qr_ref, kr_ref, vr_ref) if dil > 1 else (dqr_ref, dkr_ref, dvr_ref)
            for c in range(T // BLK):
                rows = pl.ds(BLK * c, BLK)
                cs, sa, sb = cos_ref[rows, :], sa_ref[rows, :], sb_ref[rows, :]
                dq_ref[rows, :] = _rope_t(tq[rows, :] * QK_SCALE, cs, sa, sb).astype(BF)
                dk_ref[rows, :] = _rope_t(tk[rows, :], cs, sa, sb).astype(BF)
                dv_ref[rows, :] = tv[rows, :].astype(BF)

        for gi in range(3):
            pl.when(g == gi)(functools.partial(group, gi))
        _side_steps(side, s_refs, last=(p == 3) & (g == 2))

    regrouped = pl.BlockSpec((None, T, BLK), lambda p, g: (g, 0, p))
    pspec = pl.BlockSpec((T, BLK), lambda p, g: (0, p))
    gspec = pl.BlockSpec((T, BLK), lambda p, g: (0, 4 * g + p))
    table = pl.BlockSpec((T, BLK), lambda p, g: (0, 0))
    big = lambda: pltpu.VMEM((T, BLK), F32)
    two = lambda: pltpu.VMEM((2, T, BLK), F32)
    out = pl.pallas_call(
        body, name="attn_bwd", grid=(4, 3),
        in_specs=[regrouped, regrouped, regrouped,
                  pl.BlockSpec((T, BLK), lambda p, g: (0, AG_COL0 // BLK + p)), table, table, table,
                  pspec, pl.BlockSpec((T, BLK), lambda p, g: (0, 2 * p)),
                  pl.BlockSpec((T, BLK), lambda p, g: (0, 2 * p + 1)), pspec] + s_in_specs,
        out_specs=[gspec, gspec, gspec, pspec] + s_out_specs,
        out_shape=[jax.ShapeDtypeStruct((T, 1536), BF), jax.ShapeDtypeStruct((T, 1536), BF),
                   jax.ShapeDtypeStruct((T, 1536), BF), jax.ShapeDtypeStruct((T, 512), BF)] + s_shapes,
        scratch_shapes=[pltpu.VMEM((2, 2 * BLK, BLK), F32), two(), big(), big(), big(), big(),
                        two(), two(), big(), big(), big(), pltpu.VMEM((2, 4, 2 * BLK, BLK), BF), big()] + s_sems,
        compiler_params=_params(("parallel" if side is None else "arbitrary", "arbitrary")),
    )(qs, ks, vs, z, cos, sa, sb, opre, lse, lse, dob, *s_arrays)
    return (*out[:4], out[4:])


def _merge_out_loss(og, ob, z, w_a, w_b, w_out, x, tgt, wf):
    tm = 512

    def body(og_ref, ob_ref, ga_ref, gb_ref, wa_ref, wb_ref, wo_ref, x_ref, t_ref, wf_ref,
             m_ref, dout_ref, loss_ref, gwf_ref):
        @pl.when(pl.program_id(0) == 0)
        def _():
            loss_ref[...] = jnp.zeros_like(loss_ref)
            gwf_ref[...] = jnp.zeros_like(gwf_ref)

        ya, yb = _dot(og_ref[...], wa_ref[...]), _dot(ob_ref[...], wb_ref[...])
        m =(_sigmoid(ga_ref[...]) * ya + _sigmoid(gb_ref[...]) * yb).astype(BF)
        m_ref[...] = m
        out = x_ref[...] + _dot(m, wo_ref[...])
        r = lax.rsqrt(jnp.mean(out * out, axis=-1, keepdims=True) + EPS)
        yh = out * r
        wfv = wf_ref[...]
        err = yh * wfv - t_ref[...]
        loss_ref[...] += jnp.sum(err * err, axis=0, keepdims=True) * (0.5 / D)
        dy = err * (1.0 / D)
        gwf_ref[...] += jnp.sum(dy * yh, axis=0, keepdims=True)
        dyh = dy * wfv
        dout_ref[...] = r * (dyh - yh * jnp.mean(dyh * yh, axis=-1, keepdims=True))

    row = pl.BlockSpec((tm, D), lambda i: (i, 0))
    vec = pl.BlockSpec((1, D), lambda i: (0, 0))
    whole = lambda w: pl.BlockSpec(w.shape, lambda i: (0, 0))
    return pl.pallas_call(
        body, name="merge_out_loss", grid=(T // tm,),
        in_specs=[row, pl.BlockSpec((tm, ob.shape[1]), lambda i: (i, 0)),
                  pl.BlockSpec((tm, D), lambda i: (i, GATE_COL0 // D)),
                  pl.BlockSpec((tm, D), lambda i: (i, GATE_COL0 // D + 1)),
                  whole(w_a), whole(w_b), whole(w_out), row, row, vec],
        out_specs=[row, row, vec, vec],
        out_shape=[jax.ShapeDtypeStruct((T, D), BF), jax.ShapeDtypeStruct((T, D), F32),
                   jax.ShapeDtypeStruct((1, D), F32), jax.ShapeDtypeStruct((1, D), F32)],
        compiler_params=_params(("arbitrary",)),
    )(og, ob, z, z, w_a, w_b, w_out, x, tgt, wf)


def _merge_proj_bwd(dout, og, ob, z, w_a, w_b, w_out):
    tm = 512

    def body(dout_ref, og_ref, ob_ref, ga_ref, gb_ref, wa_ref, wb_ref, wo_ref,
             dya_ref, dyb_ref, dg_ref, dog_ref, dob_ref):
        dmv = _dot(dout_ref[...].astype(BF), wo_ref[...], _NT)
        sa, sb = _sigmoid(ga_ref[...]), _sigmoid(gb_ref[...])
        dya, dyb = (sa * dmv).astype(BF), (sb * dmv).astype(BF)
        dya_ref[...] = dya
        dyb_ref[...] = dyb
        dg_ref[:, :D] = (dmv * _dot(og_ref[...], wa_ref[...]) * sa * (1.0 - sa)).astype(BF)
        dg_ref[:, D:] = (dmv * _dot(ob_ref[...], wb_ref[...]) * sb * (1.0 - sb)).astype(BF)
        dog_ref[...] = _dot(dya, wa_ref[...], _NT)
        dob_ref[...] = _dot(dyb, wb_ref[...], _NT)

    row = pl.BlockSpec((tm, D), lambda i: (i, 0))
    whole = lambda w: pl.BlockSpec(w.shape, lambda i: (0, 0))
    nb = w_b.shape[0]
    return pl.pallas_call(
        body, name="merge_proj_bwd", grid=(T // tm,),
        in_specs=[row, row, pl.BlockSpec((tm, nb), lambda i: (i, 0)),
                  pl.BlockSpec((tm, D), lambda i: (i, GATE_COL0 // D)),
                  pl.BlockSpec((tm, D), lambda i: (i, GATE_COL0 // D + 1)), whole(w_a), whole(w_b), whole(w_out)],
        out_specs=[row, row, pl.BlockSpec((tm, 2 * D), lambda i: (i, 0)), row,
                   pl.BlockSpec((tm, nb), lambda i: (i, 0))],
        out_shape=[jax.ShapeDtypeStruct((T, D), BF), jax.ShapeDtypeStruct((T, D), BF),
                   jax.ShapeDtypeStruct((T, 2 * D), BF), jax.ShapeDtypeStruct((T, D), F32),
                   jax.ShapeDtypeStruct((T, nb), F32)],
        compiler_params=_params(("parallel",)),
    )(dout, og, ob, z, z, w_a, w_b, w_out)


def _rope_inv_freq():
    inv = ROPE_THETA ** (-jnp.arange(0, 64, 2, dtype=F32) / 64)
    return jnp.tile(inv, 4).reshape(1, BLK)


def _local_step(x, pos, norm_w, lbl, hnw, wf, tgt, w_in, w_a, w_b, w_out, shard_shapes=()):
    invf = _rope_inv_freq()
    if shard_shapes:
        blk = jnp.reshape(2 * lax.axis_index("x") + lax.axis_index("y"), (1,)).astype(jnp.int32)
        h, cos, sa, sb, z_own, (w_a, w_b, w_out), (w_near,) = _norm_and_rope_tables(
            x, norm_w, pos, invf, side=_gather_near_side(w_in, WEIGHT_AXES[0]), own=(w_in, blk),
            cast=(w_a, w_b, w_out))
        near = jnp.concatenate([blk ^ 2, blk ^ 1])
        z, (w_diag,) = _z_blocks(h, w_near, z_own, jnp.concatenate([near, near]), 2, name="z_proj_near",
                                 side=_gather_diag_side(w_near, w_in.shape, WEIGHT_AXES[0]))
        z, w_in, _ = _z_blocks(h, w_diag, z, jnp.concatenate([blk ^ 3, jnp.zeros_like(blk)]), 1, name="z_proj_diag",
                               fill=w_near, side=None)
    else:
        h, cos, sa, sb, _, _, _ = _norm_and_rope_tables(x, norm_w, pos, invf)
        z = _matmul(h, w_in, tm=T, tn=512, name="z_proj")
    oraw, og, shist = _hgrn_fwd(z, lbl, hnw)
    side_a = _gather_side([w_a, w_b, w_out], WEIGHT_AXES[1:]) if shard_shapes else None
    ob, opre, lse, qs, ks, vs, gathered = _attn_fwd(z, cos, sa, sb, side=side_a)
    if shard_shapes:
        w_a, w_b, w_out = gathered
    merged, dout, loss_vec, g_wf = _merge_out_loss(og, ob, z, w_a, w_b, w_out, x, tgt, wf)

    dya, dyb, dgates, dog, dob = _merge_proj_bwd(dout, og, ob, z, w_a, w_b, w_out)
    g_wout = _matmul(merged, dout, ta=True, out_dtype=BF, tm=512, tn=1024, name="g_wout")
    g_wa = _matmul(og, dya, ta=True, out_dtype=BF, tm=512, tn=1024, name="g_wa")
    g_wb = _matmul(ob, dyb, ta=True, out_dtype=BF, tm=512, tn=1024, name="g_wb")
    small = [g_wa, g_wb, g_wout]
    side_s = side_w = None
    if shard_shapes:
        p3_s = [_as3d(g, s, ax) for g, s, ax in zip(small, shard_shapes[1:], WEIGHT_AXES[1:])]
        side_s = _chip_exchange_direct_side(p3_s, shard_shapes[1:], WEIGHT_AXES[1:])
    dz_h, dlb, g_hnw = _hgrn_bwd(z, lbl, hnw, oraw, dog, shist)
    dq, dk, dv, dag, land_s = _attn_bwd(z, qs, ks, vs, cos, sa, sb, opre, lse, dob, side=side_s)
    dz_parts = [dz_h, dq, dk, dv, dag, dgates]
    if shard_shapes:
        c = lax.axis_index("c")
        half = lambda i: jnp.reshape(i, (1,)).astype(jnp.int32)
        g_send = _grad_w_in_half(h, dz_parts, half(1 - c))
        g_keep, (g_sib,) = _grad_w_in_half(h, dz_parts, half(c), side=_sibling_send_side(g_send))
        p3_w = [_add_bf16(g_keep, g_sib, "pair_sum_w_in").reshape(1, D // 2, NIN)]
        side_w = _chip_exchange_relay_side(p3_w[0], shard_shapes[0])
    else:
        g_big = [_grad_w_in(h, dz_parts)] + small
    gx, g_nw, land_w = _grad_x(dz_parts, w_in, x, dout, norm_w, side=side_w)
    small_sums = None
    if shard_shapes:
        g_big, small_sums = _rs_finish(p3_w + p3_s, [land_w[0]] + list(land_s), shard_shapes, WEIGHT_AXES,
                                       (g_nw, dlb, g_hnw, g_wf, loss_vec))
    return dict(loss_vec=loss_vec, gx=gx, g_nw=g_nw, dlb=dlb, g_hnw=g_hnw, g_wf=g_wf, small_sums=small_sums,
                g_win=g_big[0], g_wa=g_big[1], g_wb=g_big[2], g_wout=g_big[3])


MESH = pl.DeviceIdType.MESH
HBM = pl.BlockSpec(memory_space=pl.ANY)
WEIGHT_AXES = (1, 0, 1, 0)


def _place():
    x, y, c = lax.axis_index("x"), lax.axis_index("y"), lax.axis_index("c")
    chips = [(1 - x, y), (x, 1 - y), (1 - x, 1 - y)]
    return x, y, c, chips


def _block_half(ref, shard_shape, axis, j, half):
    r, c = shard_shape
    hr = r // 2
    if axis == 0:
        return ref.at[pl.ds(pl.multiple_of(j * r + half * hr, 16), hr), :]
    return ref.at[pl.ds(pl.multiple_of(half * hr, 16), hr), pl.ds(pl.multiple_of(j * c, 128), c)]


PIECES = 8


def _block_piece(ref, shard_shape, axis, j, half, q):
    r, c = shard_shape
    pr = r // 2 // PIECES
    if axis == 0:
        return ref.at[pl.ds(pl.multiple_of(j * r + half * (r // 2) + q * pr, 16), pr), :]
    return ref.at[pl.ds(pl.multiple_of(half * (r // 2) + q * pr, 16), pr), pl.ds(pl.multiple_of(j * c, 128), c)]


class _Side:
    def __init__(self, arrays, out_shapes, sems, first, last, mid=None):
        self.arrays, self.out_shapes, self.sems, self.first, self.last = arrays, out_shapes, sems, first, last
        self.mid = mid


def _gather_side(shards, axes):
    n = len(shards)
    shapes = [s.shape for s in shards]

    def copies(ins, outs, sems):
        send1, recv1, send2, recv2, send0, recv0 = sems
        x, y, c, chips = _place()
        me = 2 * x + y
        sib = (x, y, 1 - c)
        near = ((1 - c) * (1 - x) + c * x, (1 - c) * y + c * (1 - y))
        far = ((1 - c) * x + c * (1 - x), (1 - c) * (1 - y) + c * y)
        out = []
        for a in range(n):
            r, cc = shapes[a]
            mine = (outs[a].at[pl.ds(pl.multiple_of(me * r, 16), r), :] if axes[a] == 0
                    else outs[a].at[:, pl.ds(pl.multiple_of(me * cc, 128), cc)])
            own = pltpu.make_async_remote_copy(
                src_ref=ins[a], dst_ref=mine, send_sem=send0.at[a], recv_sem=recv0.at[a],
                device_id=sib, device_id_type=MESH)
            src = ins[a].at[pl.ds(pl.multiple_of(c * (r // 2), 16), r // 2), :]
            sends = [pltpu.make_async_remote_copy(
                src_ref=src, dst_ref=_block_half(outs[a], shapes[a], axes[a], me, c),
                send_sem=send1.at[a, k], recv_sem=recv1.at[a, k], device_id=(*chips[k], c), device_id_type=MESH)
                for k in range(2)]

            def region(chip, half):
                return _block_half(outs[a], shapes[a], axes[a], 2 * chip[0] + chip[1], half)

            def arrival(chip, k):
                reg = region(chip, c)
                return pltpu.make_async_remote_copy(
                    src_ref=reg, dst_ref=reg, send_sem=send1.at[a, k], recv_sem=recv1.at[a, k],
                    device_id=(*chip, c), device_id_type=MESH)

            def to_sibling(chip, k):
                reg = region(chip, c)
                return pltpu.make_async_remote_copy(
                    src_ref=reg, dst_ref=reg, send_sem=send2.at[a, k], recv_sem=recv2.at[a, k],
                    device_id=sib, device_id_type=MESH)

            def from_sibling(chip, k):
                reg = region(chip, 1 - c)
                return pltpu.make_async_remote_copy(
                    src_ref=reg, dst_ref=reg, send_sem=send2.at[a, k], recv_sem=recv2.at[a, k],
                    device_id=sib, device_id_type=MESH)

            relay = pltpu.make_async_remote_copy(
                src_ref=region(near, c), dst_ref=region(near, c), send_sem=send1.at[a, 2], recv_sem=recv1.at[a, 2],
                device_id=(*far, c), device_id_type=MESH)
            hops = [(arrival(near, c), to_sibling(near, c)), (arrival(far, 1 - c), to_sibling(far, 1 - c)),
                    (arrival(chips[2], 2), to_sibling(chips[2], 2))]
            back = [from_sibling(chips[k], k) for k in range(3)]
            out.append((own, sends, relay, hops, back))
        return out

    def first(ins, outs, sems):
        for own, sends, _, _, _ in copies(ins, outs, sems):
            own.start()
            for cp in sends:
                cp.start()

    def mid(ins, outs, sems):
        per_array = copies(ins, outs, sems)
        for step in range(2):
            for _, _, relay, hops, _ in per_array:
                arrived, onward = hops[step]
                arrived.wait_recv()
                if step == 0:
                    relay.start()
                onward.start()

    def last(ins, outs, sems):
        per_array = copies(ins, outs, sems)
        for _, _, _, hops, _ in per_array:
            arrived, onward = hops[2]
            arrived.wait_recv()
            onward.start()
        for own, sends, relay, hops, back in per_array:
            for cp in back:
                cp.wait_recv()
            for cp in sends + [relay] + [onward for _, onward in hops]:
                cp.wait_send()
            own.wait()

    full = [(4 * r, c) if ax == 0 else (r, 4 * c) for (r, c), ax in zip(shapes, axes)]
    sems = [pltpu.SemaphoreType.DMA((n, 3)), pltpu.SemaphoreType.DMA((n, 3)),
            pltpu.SemaphoreType.DMA((n, 3)), pltpu.SemaphoreType.DMA((n, 3)),
            pltpu.SemaphoreType.DMA((n,)), pltpu.SemaphoreType.DMA((n,))]
    return _Side(list(shards), [jax.ShapeDtypeStruct(f, BF) for f in full], sems, first, last, mid)


def _gather_near_side(shard, axis):
    shape = shard.shape
    r, cc = shape

    def copies(ins, outs, sems):
        send1, recv1, send2, recv2, send0, recv0 = sems
        x, y, c, chips = _place()
        me = 2 * x + y
        sib = (x, y, 1 - c)
        mine = (outs[0].at[pl.ds(pl.multiple_of(me * r, 16), r), :] if axis == 0
                else outs[0].at[:, pl.ds(pl.multiple_of(me * cc, 128), cc)])
        own = pltpu.make_async_remote_copy(
            src_ref=ins[0], dst_ref=mine, send_sem=send0.at[0], recv_sem=recv0.at[0],
            device_id=sib, device_id_type=MESH)
        def region(k, half, q):
            return _block_piece(outs[0], shape, axis, 2 * chips[k][0] + chips[k][1], half, q)

        def moves(k, q):
            src = ins[0].at[pl.ds(pl.multiple_of(c * (r // 2) + q * (r // 2 // PIECES), 16), r // 2 // PIECES), :]
            return [pltpu.make_async_remote_copy(
                        src_ref=s, dst_ref=d, send_sem=ss.at[k, q], recv_sem=rs.at[k, q], device_id=dev,
                        device_id_type=MESH)
                    for s, d, ss, rs, dev in (
                        (src, _block_piece(outs[0], shape, axis, me, c, q), send1, recv1, (*chips[k], c)),
                        (region(k, c, q), region(k, c, q), send1, recv1, (*chips[k], c)),
                        (region(k, c, q), region(k, c, q), send2, recv2, sib),
                        (region(k, 1 - c, q), region(k, 1 - c, q), send2, recv2, sib))]

        return own, [moves(k, q) for q in range(PIECES) for k in range(2)]

    def first(ins, outs, sems):
        own, per_piece = copies(ins, outs, sems)
        own.start()
        for send, _, _, _ in per_piece:
            send.start()

    def last(ins, outs, sems):
        own, per_piece = copies(ins, outs, sems)
        for _, arrived, onward, _ in per_piece:
            arrived.wait_recv()
            onward.start()
        for send, _, onward, back in per_piece:
            back.wait_recv()
            send.wait_send()
            onward.wait_send()
        own.wait()

    full = (4 * r, cc) if axis == 0 else (r, 4 * cc)
    sems = [pltpu.SemaphoreType.DMA((2, PIECES))] * 4 + [pltpu.SemaphoreType.DMA((1,))] * 2
    return _Side([shard], [jax.ShapeDtypeStruct(full, BF)], sems, first, last)


def _gather_diag_side(gathered, shape, axis):
    r, cc = shape

    def copies(ins, outs, sems):
        send1, recv1, send2, recv2 = sems
        x, y, c, _ = _place()
        sib = (x, y, 1 - c)
        near = ((1 - c) * (1 - x) + c * x, (1 - c) * y + c * (1 - y))
        far = ((1 - c) * x + c * (1 - x), (1 - c) * (1 - y) + c * y)

        def piece(i, q):
            return outs[0].at[pl.ds(pl.multiple_of(i * (r // 2) + q * (r // 2 // PIECES), 16), r // 2 // PIECES), :]

        def moves(q):
            def move(s, d, ss, rs, dev):
                return pltpu.make_async_remote_copy(
                    src_ref=s, dst_ref=d, send_sem=ss.at[q], recv_sem=rs.at[q], device_id=dev, device_id_type=MESH)

            return (move(_block_piece(ins[0], shape, axis, 2 * near[0] + near[1], c, q), piece(c, q), send1, recv1,
                         (*far, c)),
                    move(piece(c, q), piece(c, q), send1, recv1, (*far, c)),
                    move(piece(c, q), piece(c, q), send2, recv2, sib),
                    move(piece(1 - c, q), piece(1 - c, q), send2, recv2, sib))

        return [moves(q) for q in range(PIECES)]

    def first(ins, outs, sems):
        for relay, _, _, _ in copies(ins, outs, sems):
            relay.start()

    def last(ins, outs, sems):
        per_piece = copies(ins, outs, sems)
        for _, arrived, onward, _ in per_piece:
            arrived.wait_recv()
            onward.start()
        for relay, _, onward, back in per_piece:
            back.wait_recv()
            relay.wait_send()
            onward.wait_send()

    return _Side([gathered], [jax.ShapeDtypeStruct(shape, BF)], [pltpu.SemaphoreType.DMA((PIECES,))] * 4,
                 first, last)


def _as3d(g, shard_shape, axis):
    r, c = shard_shape
    return g.reshape(4, r, c) if axis == 0 else g.reshape(1, r, 4 * c)


def _chip_exchange_direct_side(g3s, shapes, axes):
    n = len(g3s)

    def copies(ins, outs, sems):
        send, recv = sems
        x, y, c, chips = _place()
        sends, arrivals = [], []
        for a in range(n):
            r, cc = shapes[a]
            hr = r // 2

            def part(j, h, a=a, hr=hr, cc=cc):
                rows = pl.ds(pl.multiple_of(h * hr, 16), hr)
                return (ins[a].at[j, rows, :] if axes[a] == 0
                        else ins[a].at[0, rows, pl.ds(pl.multiple_of(j * cc, 128), cc)])

            def move(s, d, i_send, i_recv, dev, a=a):
                return pltpu.make_async_remote_copy(
                    src_ref=s, dst_ref=d, send_sem=send.at[a, i_send], recv_sem=recv.at[a, i_recv], device_id=dev,
                    device_id_type=MESH)

            for k, (px, py) in enumerate(chips):
                for h in range(2):
                    sends.append(move(part(2 * px + py, h), outs[a].at[2 * k + c], 2 * k + h, 2 * k + c, (px, py, h)))
                    slot = outs[a].at[2 * k + h]
                    arrivals.append(move(slot, slot, 2 * k + h, 2 * k + h, (px, py, h)))
            sends.append(move(part(2 * x + y, 1 - c), outs[a].at[6], 6, 6, (x, y, 1 - c)))
            arrivals.append(move(outs[a].at[6], outs[a].at[6], 6, 6, (x, y, 1 - c)))
        return sends, arrivals

    def first(ins, outs, sems):
        for cp in copies(ins, outs, sems)[0]:
            cp.start()

    def last(ins, outs, sems):
        sends, arrivals = copies(ins, outs, sems)
        for cp in arrivals:
            cp.wait_recv()
        for cp in sends:
            cp.wait_send()

    return _Side(list(g3s), [jax.ShapeDtypeStruct((7, r // 2, c), BF) for r, c in shapes],
                 [pltpu.SemaphoreType.DMA((n, 7)), pltpu.SemaphoreType.DMA((n, 7))], first, last)


def _chip_exchange_relay_side(p3, shape):
    r, cc = shape
    hr = r // 2
    rows = 64

    def copies(ins, outs, sems):
        send, recv, local, mine, theirs = sems
        x, y, c, chips = _place()
        near = ((1 - c) * (1 - x) + c * x, (1 - c) * y + c * (1 - y))
        far = ((1 - c) * x + c * (1 - x), (1 - c) * (1 - y) + c * y)
        land, staged = outs

        def block(chip):
            return ins[0].at[0, :, pl.ds(pl.multiple_of((2 * chip[0] + chip[1]) * cc, 128), cc)]

        def move(s, d, k, dev):
            return pltpu.make_async_remote_copy(
                src_ref=s, dst_ref=d, send_sem=send.at[k], recv_sem=recv.at[k], device_id=dev, device_id_type=MESH)

        return dict(
            direct=move(block(near), land.at[c], 0, (*near, c)),
            for_relay=move(block(chips[2]), staged, 1, (*near, c)),
            summed=move(mine, land.at[1 - c], 2, (*far, c)),
            direct_in=move(land.at[c], land.at[c], 0, (*near, c)),
            staged_in=move(staged, staged, 1, (*near, c)),
            summed_in=move(land.at[1 - c], land.at[1 - c], 2, (*far, c)),
            load_mine=pltpu.make_async_copy(block(far), mine, local.at[0]),
            load_theirs=pltpu.make_async_copy(staged, theirs, local.at[1]))

    def first(ins, outs, sems):
        cps = copies(ins, outs, sems)
        cps["for_relay"].start()
        cps["direct"].start()

    def mid(ins, outs, sems):
        cps = copies(ins, outs, sems)
        mine, theirs = sems[3], sems[4]
        cps["load_mine"].start()
        cps["staged_in"].wait_recv()
        cps["load_theirs"].start()
        cps["load_mine"].wait()
        cps["load_theirs"].wait()

        def add(i, carry):
            rs = pl.ds(pl.multiple_of(i * rows, 16), rows)
            mine[rs, :] = (mine[rs, :].astype(F32) + theirs[rs, :].astype(F32)).astype(BF)
            return carry

        lax.fori_loop(0, hr // rows, add, 0)
        cps["summed"].start()

    def last(ins, outs, sems):
        cps = copies(ins, outs, sems)
        cps["direct_in"].wait_recv()
        cps["summed_in"].wait_recv()
        for name in ("direct", "for_relay", "summed"):
            cps[name].wait_send()

    sems = [pltpu.SemaphoreType.DMA((3,)), pltpu.SemaphoreType.DMA((3,)), pltpu.SemaphoreType.DMA((2,)),
            pltpu.VMEM((hr, cc), BF), pltpu.VMEM((hr, cc), BF)]
    return _Side([p3], [jax.ShapeDtypeStruct((2, hr, cc), BF), jax.ShapeDtypeStruct((hr, cc), BF)], sems,
                 first, last, mid)


def _chip_sum(p3, land, shard_shape, axis, idx, name):
    r, c = shard_shape
    hr = r // 2
    tr = 128
    nt = hr // tr
    slots = land.shape[0]

    def body(idx_ref, p_ref, l_ref, o_ref):
        acc = p_ref[...].astype(F32)
        for k in range(slots):
            acc = acc + l_ref[k].astype(F32)
        o_ref[...] = acc

    own = (pl.BlockSpec((None, tr, c), lambda i, idx: (idx[0], i, 0)) if axis == 0
           else pl.BlockSpec((None, tr, c), lambda i, idx: (0, i, idx[0])))
    return pl.pallas_call(
        body, name=name,
        grid_spec=pltpu.PrefetchScalarGridSpec(
            num_scalar_prefetch=1, grid=(nt,),
            in_specs=[own, pl.BlockSpec((slots, tr, c), lambda i, idx: (0, i, 0))],
            out_specs=pl.BlockSpec((tr, c), lambda i, idx: (idx[1] * nt + i, 0))),
        out_shape=jax.ShapeDtypeStruct((r, c), F32),
        compiler_params=_params(("parallel",)),
    )(idx, p3, land)


def _chip_sums(p3s, lands, shapes, axes, idx, name):
    n = len(p3s)

    def body(idx_ref, *refs):
        for p_ref, l_ref, o_ref in zip(refs[:n], refs[n:2 * n], refs[2 * n:]):
            acc = p_ref[...].astype(F32)
            for k in range(l_ref.shape[0]):
                acc = acc + l_ref[k].astype(F32)
            o_ref[...] = acc

    own = [pl.BlockSpec((None, r // 2, c),
                        (lambda i, idx: (idx[0], idx[1], 0)) if ax == 0 else (lambda i, idx: (0, idx[1], idx[0])))
           for (r, c), ax in zip(shapes, axes)]
    return pl.pallas_call(
        body, name=name,
        grid_spec=pltpu.PrefetchScalarGridSpec(
            num_scalar_prefetch=1, grid=(1,),
            in_specs=own + [pl.BlockSpec(l.shape, lambda i, idx: (0, 0, 0)) for l in lands],
            out_specs=[pl.BlockSpec((r // 2, c), lambda i, idx: (idx[1], 0)) for r, c in shapes]),
        out_shape=[jax.ShapeDtypeStruct((r, c), F32) for r, c in shapes],
        compiler_params=_params(("arbitrary",)),
    )(idx, *p3s, *lands)


def _rs_pair_gather(fulls, small):
    n = len(fulls)

    def body(*refs):
        ins, small_refs, outs, red_ref = refs[:n], refs[n:n + 5], refs[n + 5:2 * n + 5], refs[2 * n + 5]
        send, recv = refs[2 * n + 6:2 * n + 8]
        x, y, c, _ = _place()
        cps = []
        for a in range(n):
            hr = fulls[a].shape[0] // 2
            rows = pl.ds(pl.multiple_of(c * hr, 8), hr)
            cp = pltpu.make_async_remote_copy(
                src_ref=ins[a].at[rows, :], dst_ref=outs[a].at[rows, :], send_sem=send.at[a], recv_sem=recv.at[a],
                device_id=(x, y, 1 - c), device_id_type=MESH)
            cp.start()
            cps.append(cp)
        _small_all_reduce(small_refs, red_ref, *refs[2 * n + 8:])
        for a, cp in enumerate(cps):
            cp.wait_send()
            hr = fulls[a].shape[0] // 2
            other = pl.ds(pl.multiple_of((1 - c) * hr, 8), hr)
            pltpu.make_async_remote_copy(
                src_ref=ins[a].at[other, :], dst_ref=outs[a].at[other, :], send_sem=send.at[a], recv_sem=recv.at[a],
                device_id=(x, y, 1 - c), device_id_type=MESH).wait_recv()

    vm = pl.BlockSpec(memory_space=pltpu.VMEM)
    out = pl.pallas_call(
        body, name="grads_pair_gather",
        in_specs=[HBM] * n + [vm] * 5, out_specs=[HBM] * n + [vm],
        out_shape=[jax.ShapeDtypeStruct(f.shape, F32) for f in fulls] + [jax.ShapeDtypeStruct((NSMALL, D), F32)],
        input_output_aliases={a: a for a in range(n)},
        scratch_shapes=[pltpu.SemaphoreType.DMA((n,)), pltpu.SemaphoreType.DMA((n,)),
                        pltpu.VMEM((NSMALL, D), F32), pltpu.VMEM((8, NSMALL, D), F32),
                        pltpu.SemaphoreType.DMA((7,)), pltpu.SemaphoreType.DMA((7,))],
    )(*fulls, *small)
    return out[:n], out[n]


def _sibling_send_side(arr):
    def copy(ins, outs, sems):
        x, y, c, _ = _place()
        return pltpu.make_async_remote_copy(
            src_ref=ins[0], dst_ref=outs[0], send_sem=sems[0].at[0], recv_sem=sems[1].at[0],
            device_id=(x, y, 1 - c), device_id_type=MESH)

    return _Side([arr], [jax.ShapeDtypeStruct(arr.shape, arr.dtype)],
                 [pltpu.SemaphoreType.DMA((1,)), pltpu.SemaphoreType.DMA((1,))],
                 lambda ins, outs, sems: copy(ins, outs, sems).start(),
                 lambda ins, outs, sems: copy(ins, outs, sems).wait())


def _add_bf16(a, b, name):
    r, c = a.shape
    tr = 128

    def body(a_ref, b_ref, o_ref):
        o_ref[...] = (a_ref[...].astype(F32) + b_ref[...].astype(F32)).astype(BF)

    blk = pl.BlockSpec((tr, c), lambda i: (i, 0))
    return pl.pallas_call(
        body, name=name, grid=(r // tr,), in_specs=[blk, blk], out_specs=blk,
        out_shape=jax.ShapeDtypeStruct((r, c), BF), compiler_params=_params(("parallel",)),
    )(a, b)


def _rs_finish(p3s, landed, shapes, axes, small):
    x, y, c = lax.axis_index("x"), lax.axis_index("y"), lax.axis_index("c")
    idx = jnp.stack([2 * x + y, c]).astype(jnp.int32)
    fulls = [_chip_sum(p3s[0], landed[0], shapes[0], axes[0], idx, "chip_sum_w_in")]
    fulls += _chip_sums(p3s[1:], landed[1:], shapes[1:], axes[1:], idx, "chip_sums_branches_out")
    return _rs_pair_gather(fulls, small)


NSMALL = 8


def _small_all_reduce(small_refs, out_ref, pack_ref, buf_ref, send, recv):
    nw_ref, lb_ref, hn_ref, wf_ref, ls_ref = small_refs
    x, y, c = lax.axis_index("x"), lax.axis_index("y"), lax.axis_index("c")
    me = 4 * x + 2 * y + c
    pack_ref[...] = jnp.zeros_like(pack_ref)
    pack_ref[0:1, :] = nw_ref[...]
    pack_ref[1:2, :] = lb_ref[...]
    pack_ref[2:3, 0:HK] = hn_ref[...]
    pack_ref[3:4, :] = wf_ref[...]
    pack_ref[4:5, :] = ls_ref[...]
    buf_ref[me] = pack_ref[...]
    cps = []
    for d in range(1, 8):
        dx, dy, dc = d >> 2, (d >> 1) & 1, d & 1
        peer = (1 - x if dx else x, 1 - y if dy else y, 1 - c if dc else c)
        cp = pltpu.make_async_remote_copy(
            src_ref=pack_ref, dst_ref=buf_ref.at[me], send_sem=send.at[d - 1], recv_sem=recv.at[d - 1],
            device_id=peer, device_id_type=MESH)
        cp.start()
        cps.append(cp)
    for d in range(1, 8):
        dx, dy, dc = d >> 2, (d >> 1) & 1, d & 1
        src = 4 * (1 - x if dx else x) + 2 * (1 - y if dy else y) + (1 - c if dc else c)
        pltpu.make_async_remote_copy(
            src_ref=pack_ref, dst_ref=buf_ref.at[src], send_sem=send.at[d - 1], recv_sem=recv.at[d - 1],
            device_id=(x, y, c), device_id_type=MESH).wait_recv()
    for cp in cps:
        cp.wait_send()
    acc = buf_ref[0]
    for i in range(1, 8):
        acc = acc + buf_ref[i]
    out_ref[...] = acc


def _adamw_math(w, g, m, v):
    m = B1 * m + (1.0 - B1) * g
    v = B2 * v + (1.0 - B2) * (g * g)
    m_hat = m / (1.0 - B1 ** STEP)
    v_hat = v / (1.0 - B2 ** STEP)
    return -LR * (m_hat / (jnp.sqrt(v_hat) + ADAM_EPS) + WD * w), m, v


def _adamw(w, g, m, v, name):
    r, c = w.shape
    tr = 128

    def body(w_ref, g_ref, m_ref, v_ref, d_ref, nm_ref, nv_ref, go_ref):
        g = g_ref[...]
        d_ref[...], nm_ref[...], nv_ref[...] = _adamw_math(w_ref[...], g, m_ref[...], v_ref[...])
        go_ref[...] = g

    blk = pl.BlockSpec((tr, c), lambda i: (i, 0))
    return pl.pallas_call(
        body, name=name, grid=(r // tr,), in_specs=[blk] * 4, out_specs=[blk] * 4,
        out_shape=[jax.ShapeDtypeStruct((r, c), F32)] * 4,
        compiler_params=_params(("parallel",)),
    )(w, g, m, v)


def _adamw_whole(groups, name):
    n = len(groups)

    def body(*refs):
        ins, outs = refs[:4 * n], refs[4 * n:]
        for a in range(n):
            w_ref, g_ref, m_ref, v_ref = ins[4 * a:4 * a + 4]
            g = g_ref[...]
            outs[4 * a][...], outs[4 * a + 1][...], outs[4 * a + 2][...] = _adamw_math(
                w_ref[...], g, m_ref[...], v_ref[...])
            outs[4 * a + 3][...] = g

    vm = pl.BlockSpec(memory_space=pltpu.VMEM)
    out = pl.pallas_call(
        body, name=name, in_specs=[vm] * (4 * n), out_specs=[vm] * (4 * n),
        out_shape=[jax.ShapeDtypeStruct(grp[0].shape, F32) for grp in groups for _ in range(4)],
        compiler_params=_params(),
    )(*[a for grp in groups for a in grp])
    return [out[4 * a:4 * a + 4] for a in range(n)]


def _small_update(red, lbl, params):
    def body(red_ref, *refs):
        ins, outs = refs[:12], refs[12:]
        lb = _lower_bound(ins[3][...])
        dl0 = red_ref[1:2, :] * lb * (1.0 - lb)
        row = lax.broadcasted_iota(jnp.int32, (2, D), 0)
        grads = [red_ref[0:1, :], jnp.where(row == 0, dl0, -dl0), red_ref[2:3, 0:HK], red_ref[3:4, :]]
        for i, g in enumerate(grads):
            w, m, v = ins[3 * i][...], ins[3 * i + 1][...], ins[3 * i + 2][...]
            d, nm, nv = _adamw_math(w, g, m, v)
            outs[4 * i][...] = g
            outs[4 * i + 1][...] = d
            outs[4 * i + 2][...] = nm
            outs[4 * i + 3][...] = nv
        outs[16][...] = jnp.sum(red_ref[4:5, :], axis=1, keepdims=True)

    flat = [a for p in params for a in p]
    vm = pl.BlockSpec(memory_space=pltpu.VMEM)
    shapes = [jax.ShapeDtypeStruct(p[0].shape, F32) for p in params for _ in range(4)]
    return pl.pallas_call(
        body, name="small_update",
        in_specs=[vm] * 13, out_specs=[vm] * 17,
        out_shape=shapes + [jax.ShapeDtypeStruct((1, 1), F32)],
    )(red, *flat)


def kernel(x, positions, norm_w, w_in, lb_logits, hgrn_norm_w, w_branch_a, w_branch_b, w_out, final_norm_w, loss_target, m_norm_w, m_w_in, m_lb_logits, m_hgrn_norm_w, m_w_branch_a, m_w_branch_b, m_w_out, m_final_norm_w, v_norm_w, v_w_in, v_lb_logits, v_hgrn_norm_w, v_w_branch_a, v_w_branch_b, v_w_out, v_final_norm_w):
    big_w = [w_in[0], w_branch_a[0], w_branch_b[0], w_out[0]]
    big_m = [m_w_in[0], m_w_branch_a[0], m_w_branch_b[0], m_w_out[0]]
    big_v = [v_w_in[0], v_w_branch_a[0], v_w_branch_b[0], v_w_out[0]]
    shapes = [w.shape for w in big_w]
    wf = final_norm_w.reshape(1, D)

    shards = [big_w[0].astype(BF)] + big_w[1:]
    loc = _local_step(x[0], positions.reshape(T, 1), norm_w, lb_logits, hgrn_norm_w, wf, loss_target[0],
                      *shards, shard_shapes=shapes)
    g_big = [loc["g_win"], loc["g_wa"], loc["g_wb"], loc["g_wout"]]
    red = loc["small_sums"]

    small = _small_update(red, lb_logits, [
        (norm_w, m_norm_w, v_norm_w), (lb_logits, m_lb_logits, v_lb_logits),
        (hgrn_norm_w, m_hgrn_norm_w, v_hgrn_norm_w),
        (wf, m_final_norm_w.reshape(1, D), v_final_norm_w.reshape(1, D))])
    loss = small[16].reshape(())
    sg, sd, sm, sv = ([small[4 * i + j] for i in range(4)] for j in range(4))
    for lst in (sg, sd, sm, sv):
        lst[3] = lst[3].reshape(D)
    per_w = list(zip(big_w, g_big, big_m, big_v))
    upd = [_adamw(*per_w[0], "adamw_w_in")] + _adamw_whole(per_w[1:], "adamw_branches_out")
    bd, bm, bv, bg = ([u[j][None] for u in upd] for j in range(4))

    def order(s, b):
        return [s[0], b[0], s[1], s[2], b[1], b[2], b[3], s[3]]

    return (loss, loc["gx"][None], *order(sg, bg), *order(sd, bd), *order(sm, bm), *order(sv, bv))
```
